```python
import jax
import jax.numpy as jnp
from jax import lax
import numpy as np

D_MODEL = 1024
BATCH = 8
SEQ = 4096
DEPTH = 1

N_META = 16
HG_HEADS = 8
HG_DK = 128
HG_DV = D_MODEL // HG_HEADS
HG_KWIDTH = HG_HEADS * HG_DK
HG_VWIDTH = HG_HEADS * HG_DV
CHUNK = 64
SUB = 16
POOL_WINDOWS = (2, 4, 8, 16)
POOL_GROUPS = len(POOL_WINDOWS)
POOL_WIDTH = D_MODEL
POOL_GDIM = POOL_WIDTH // POOL_GROUPS
EPS = 1e-6
IN_SIZES = (HG_KWIDTH, HG_KWIDTH, HG_VWIDTH, HG_VWIDTH, POOL_WIDTH, POOL_WIDTH, D_MODEL, D_MODEL)
IN_COLS = sum(IN_SIZES)

kernel_name = 'hybrid_hgrn2_pool_block'


def rms_norm(x, w):
    xf = x.astype(jnp.float32)
    y = xf * lax.rsqrt(jnp.mean(xf * xf, axis=-1, keepdims=True) + EPS)
    return (y * w.astype(jnp.float32)).astype(x.dtype)


def to_chunks(t, dh):
    B = t.shape[0]
    t = jnp.pad(t, ((0, 0), (CHUNK - N_META, 0), (0, 0)))
    n = t.shape[1] // CHUNK
    return t.reshape(B, n, CHUNK, HG_HEADS, dh).transpose(1, 0, 3, 2, 4)


def hgrn2_chunkwise(q, k, v, log_f):
    _, B, H, C, DK = q.shape
    DV = v.shape[-1]
    ns = C // SUB
    causal = jnp.tril(jnp.ones((SUB, SUB), dtype=bool))
    earlier = jnp.tril(jnp.ones((ns, ns), dtype=bool), -1)
    diag = jnp.eye(ns, dtype=jnp.float32)

    def step(S, inp):
        qc, kc, vc, gc = inp
        b = jnp.cumsum(gc, axis=2)
        o_inter = jnp.einsum('bhtk,bhkv->bhtv', qc * jnp.exp(b), S)
        qs = qc.reshape(B, H, ns, SUB, DK)
        ks = kc.reshape(B, H, ns, SUB, DK)
        bs = b.reshape(B, H, ns, SUB, DK)
        ref = bs[:, :, :, 0]
        q_ref = qs * jnp.exp(bs - ref[:, :, :, None])
        expo = jnp.where(earlier[None, None, :, :, None, None],
                         ref[:, :, :, None, None, :] - bs[:, :, None, :, :, :], -jnp.inf)
        k_ref = ks[:, :, None] * jnp.exp(expo)
        a_off = jnp.einsum('bhitk,bhijsk->bhitjs', q_ref, k_ref)
        dexp = jnp.where(causal[:, :, None],
                         bs[:, :, :, :, None, :] - bs[:, :, :, None, :, :], -jnp.inf)
        a_diag = jnp.einsum('bhitk,bhitsk,bhisk->bhits', qs, jnp.exp(dexp), ks)
        a = (a_off + a_diag[:, :, :, :, None, :] * diag[:, None, :, None]).reshape(B, H, C, C)
        o = o_inter + jnp.einsum('bhts,bhsv->bhtv', a, vc)
        b_last = b[:, :, -1]
        S_new = jnp.exp(b_last)[..., None] * S + jnp.einsum(
            'bhsk,bhsv->bhkv', kc * jnp.exp(b_last[:, :, None] - b), vc)
        return S_new, o

    S0 = jnp.zeros((B, H, DK, DV), jnp.float32)
    _, o = lax.scan(step, S0, (q, k, v, log_f))
    return o


def causal_multiscale_pool(u):
    B, L, _ = u.shape
    ug = u.astype(jnp.float32).reshape(B, L, POOL_GROUPS, POOL_GDIM)
    cs = jnp.cumsum(ug, axis=1)
    pos = jnp.arange(L)
    outs = []
    for g, w in enumerate(POOL_WINDOWS):
        c = cs[:, :, g]
        lagged = jnp.pad(c, ((0, 0), (w, 0), (0, 0)))[:, :L]
        cnt = jnp.minimum(pos + 1, w).astype(jnp.float32)[None, :, None]
        outs.append((c - lagged) / cnt - ug[:, :, g])
    return jnp.stack(outs, axis=2)


def _fwd_setup_inputs(seed: int = 0) -> dict:
    key = jax.random.key(seed)
    ks = jax.random.split(key, 13)
    nrm = jax.random.normal
    d = D_MODEL
    f32 = jnp.float32
    return {
        'x': nrm(ks[0], (BATCH, SEQ, d), f32),
        'meta_tokens': nrm(ks[1], (N_META, d), f32),
        'norm_w': 1.0 + 0.02 * nrm(ks[2], (DEPTH, d), f32),
        'w_in': nrm(ks[3], (DEPTH, d, IN_COLS), f32) * d ** -0.5,
        'b_in': 0.01 * nrm(ks[4], (DEPTH, IN_COLS), f32),
        'lb_logits': 0.1 * nrm(ks[5], (DEPTH + 1, HG_KWIDTH), f32),
        'hg_norm_w': 1.0 + 0.02 * nrm(ks[6], (DEPTH, HG_VWIDTH), f32),
        'pool_w': nrm(ks[7], (DEPTH, POOL_GROUPS, POOL_GDIM, POOL_GDIM), f32) * POOL_GDIM ** -0.5,
        'pool_scale': 1.0 + 0.02 * nrm(ks[8], (DEPTH, POOL_WIDTH), f32),
        'w_down_hg': nrm(ks[9], (DEPTH, HG_VWIDTH, d), f32) * HG_VWIDTH ** -0.5,
        'w_down_pool': nrm(ks[10], (DEPTH, POOL_WIDTH, d), f32) * POOL_WIDTH ** -0.5,
        'w_out': nrm(ks[11], (DEPTH, d, d), f32) * d ** -0.5,
        'final_norm_w': 1.0 + 0.02 * nrm(ks[12], (d,), f32),
    }


def _fwd_reference(x, meta_tokens, norm_w, w_in, b_in, lb_logits, hg_norm_w, pool_w, pool_scale,
              w_down_hg, w_down_pool, w_out, final_norm_w):
    f32 = jnp.float32
    B = x.shape[0]
    meta = jnp.broadcast_to(meta_tokens.astype(x.dtype)[None], (B, N_META, D_MODEL))
    z = jnp.concatenate([meta, x], axis=1)
    L = z.shape[1]
    lower_bounds = jnp.cumsum(jax.nn.softmax(lb_logits.astype(f32), axis=0), axis=0)
    split_at = np.cumsum(IN_SIZES)[:-1].tolist()
    for l in range(DEPTH):
        h = rms_norm(z, norm_w[l])
        p = (h @ w_in[l] + b_in[l]).astype(f32)
        q, fz, iv, g_hg, u, g_pool, m_hg, m_pool = jnp.split(p, split_at, axis=-1)
        lb = lower_bounds[l]
        log_f = jnp.log(lb + (1.0 - lb) * jax.nn.sigmoid(fz))
        k = (1.0 - lb) * jax.nn.sigmoid(-fz)
        o = hgrn2_chunkwise(to_chunks(q, HG_DK), to_chunks(k, HG_DK),
                            to_chunks(iv, HG_DV), to_chunks(log_f, HG_DK))
        o = o.transpose(1, 0, 3, 2, 4).reshape(B, -1, HG_HEADS, HG_DV)[:, CHUNK - N_META:]
        o = rms_norm(o, hg_norm_w[l].reshape(HG_HEADS, HG_DV)).reshape(B, L, HG_VWIDTH)
        y_hg = (o * jax.nn.silu(g_hg)) @ w_down_hg[l].astype(f32)
        pooled = causal_multiscale_pool(u)
        mixed = jnp.einsum('blgc,gcd->blgd', pooled, pool_w[l].astype(f32)).reshape(B, L, POOL_WIDTH)
        y_pool = (mixed * pool_scale[l].astype(f32) * jax.nn.silu(g_pool)) @ w_down_pool[l].astype(f32)
        merged = jax.nn.sigmoid(m_hg) * y_hg + jax.nn.sigmoid(m_pool) * y_pool
        z = z + (merged @ w_out[l].astype(f32)).astype(z.dtype)
    return rms_norm(z, final_norm_w)[:, N_META:]


import jax as _jax
import jax.numpy as _jnp

TWIN_FORMAT = 'train_step'
FWD_PARAMS = ['x', 'meta_tokens', 'norm_w', 'w_in', 'b_in', 'lb_logits', 'hg_norm_w', 'pool_w', 'pool_scale', 'w_down_hg', 'w_down_pool', 'w_out', 'final_norm_w']
TWIN_WEIGHTS = ['meta_tokens', 'norm_w', 'w_in', 'b_in', 'lb_logits', 'hg_norm_w', 'pool_w', 'pool_scale', 'w_down_hg', 'w_down_pool', 'w_out', 'final_norm_w']
TWIN_DIFF_INPUT = 'x'
TWIN_INPUTS = ['x', 'meta_tokens', 'norm_w', 'w_in', 'b_in', 'lb_logits', 'hg_norm_w', 'pool_w', 'pool_scale', 'w_down_hg', 'w_down_pool', 'w_out', 'final_norm_w', 'loss_target', 'm_meta_tokens', 'm_norm_w', 'm_w_in', 'm_b_in', 'm_lb_logits', 'm_hg_norm_w', 'm_pool_w', 'm_pool_scale', 'm_w_down_hg', 'm_w_down_pool', 'm_w_out', 'm_final_norm_w', 'v_meta_tokens', 'v_norm_w', 'v_w_in', 'v_b_in', 'v_lb_logits', 'v_hg_norm_w', 'v_pool_w', 'v_pool_scale', 'v_w_down_hg', 'v_w_down_pool', 'v_w_out', 'v_final_norm_w']
TWIN_OUTPUTS = ['loss', 'grad_x', 'grad_meta_tokens', 'grad_norm_w', 'grad_w_in', 'grad_b_in', 'grad_lb_logits', 'grad_hg_norm_w', 'grad_pool_w', 'grad_pool_scale', 'grad_w_down_hg', 'grad_w_down_pool', 'grad_w_out', 'grad_final_norm_w', 'delta_meta_tokens', 'delta_norm_w', 'delta_w_in', 'delta_b_in', 'delta_lb_logits', 'delta_hg_norm_w', 'delta_pool_w', 'delta_pool_scale', 'delta_w_down_hg', 'delta_w_down_pool', 'delta_w_out', 'delta_final_norm_w', 'new_m_meta_tokens', 'new_m_norm_w', 'new_m_w_in', 'new_m_b_in', 'new_m_lb_logits', 'new_m_hg_norm_w', 'new_m_pool_w', 'new_m_pool_scale', 'new_m_w_down_hg', 'new_m_w_down_pool', 'new_m_w_out', 'new_m_final_norm_w', 'new_v_meta_tokens', 'new_v_norm_w', 'new_v_w_in', 'new_v_b_in', 'new_v_lb_logits', 'new_v_hg_norm_w', 'new_v_pool_w', 'new_v_pool_scale', 'new_v_w_down_hg', 'new_v_w_down_pool', 'new_v_w_out', 'new_v_final_norm_w']
TWIN_LEAF_KINDS = {'loss': 'loss', 'grad_x': 'grad_x', 'grad_meta_tokens': 'grad_w', 'grad_norm_w': 'grad_w', 'grad_w_in': 'grad_w', 'grad_b_in': 'grad_w', 'grad_lb_logits': 'grad_w', 'grad_hg_norm_w': 'grad_w', 'grad_pool_w': 'grad_w', 'grad_pool_scale': 'grad_w', 'grad_w_down_hg': 'grad_w', 'grad_w_down_pool': 'grad_w', 'grad_w_out': 'grad_w', 'grad_final_norm_w': 'grad_w', 'delta_meta_tokens': 'delta_w', 'delta_norm_w': 'delta_w', 'delta_w_in': 'delta_w', 'delta_b_in': 'delta_w', 'delta_lb_logits': 'delta_w', 'delta_hg_norm_w': 'delta_w', 'delta_pool_w': 'delta_w', 'delta_pool_scale': 'delta_w', 'delta_w_down_hg': 'delta_w', 'delta_w_down_pool': 'delta_w', 'delta_w_out': 'delta_w', 'delta_final_norm_w': 'delta_w', 'new_m_meta_tokens': 'new_m', 'new_m_norm_w': 'new_m', 'new_m_w_in': 'new_m', 'new_m_b_in': 'new_m', 'new_m_lb_logits': 'new_m', 'new_m_hg_norm_w': 'new_m', 'new_m_pool_w': 'new_m', 'new_m_pool_scale': 'new_m', 'new_m_w_down_hg': 'new_m', 'new_m_w_down_pool': 'new_m', 'new_m_w_out': 'new_m', 'new_m_final_norm_w': 'new_m', 'new_v_meta_tokens': 'new_v', 'new_v_norm_w': 'new_v', 'new_v_w_in': 'new_v', 'new_v_b_in': 'new_v', 'new_v_lb_logits': 'new_v', 'new_v_hg_norm_w': 'new_v', 'new_v_pool_w': 'new_v', 'new_v_pool_scale': 'new_v', 'new_v_w_down_hg': 'new_v', 'new_v_w_down_pool': 'new_v', 'new_v_w_out': 'new_v', 'new_v_final_norm_w': 'new_v'}


def _forward(args):
    return _fwd_reference(*[args[k] for k in FWD_PARAMS])


def _output_shape():
    def fwd():
        inp = _fwd_setup_inputs(0)
        return _fwd_reference(*[inp[k] for k in FWD_PARAMS])
    out = _jax.eval_shape(fwd)
    return out.shape, out.dtype

N_MICROBATCH = 1
ADAM_LR = 0.001
ADAM_B1 = 0.9
ADAM_B2 = 0.999
ADAM_EPS = 1e-08
ADAM_WD = 0.01
ADAM_STEP = 10
PER_EXAMPLE_BATCH_AXIS = {'x': 0, 'loss_target': 0}
SHARED_INPUTS = []
_WEIGHT_DTYPES = {'meta_tokens': _jnp.float32, 'norm_w': _jnp.float32, 'w_in': _jnp.float32, 'b_in': _jnp.float32, 'lb_logits': _jnp.float32, 'hg_norm_w': _jnp.float32, 'pool_w': _jnp.float32, 'pool_scale': _jnp.float32, 'w_down_hg': _jnp.float32, 'w_down_pool': _jnp.float32, 'w_out': _jnp.float32, 'final_norm_w': _jnp.float32}
MOMENT_SCALE = {'meta_tokens': 1.317583e-03, 'norm_w': 1.386660e-01, 'w_in': 4.789629e-02, 'b_in': 5.580437e-02, 'lb_logits': 3.505396e-02, 'hg_norm_w': 5.258034e-02, 'pool_w': 4.617099e-02, 'pool_scale': 4.695145e-02, 'w_down_hg': 5.250222e-02, 'w_down_pool': 4.647482e-02, 'w_out': 7.024454e-02, 'final_norm_w': 3.204761e+01}


def _to_microbatches(a, axis):
    t = _jnp.moveaxis(a, axis, 0)
    t = t.reshape((N_MICROBATCH, t.shape[0] // N_MICROBATCH) + t.shape[1:])
    return _jnp.moveaxis(t, 1, axis + 1)


def setup_inputs(seed: int = 0) -> dict:
    inp = _fwd_setup_inputs(seed)
    key = _jax.random.fold_in(_jax.random.key(seed), 7919)
    shape, _ = _output_shape()
    out = dict(inp)
    out["loss_target"] = _jax.random.normal(_jax.random.fold_in(key, 0), shape, _jnp.float32)
    for i, name in enumerate(TWIN_WEIGHTS):
        w = inp[name].astype(_jnp.float32)
        if MOMENT_SCALE is None:
            s = _jnp.sqrt(_jnp.mean(_jnp.square(w)) + 1e-30)
        else:
            s = MOMENT_SCALE[name]
        km, kv = _jax.random.split(_jax.random.fold_in(key, i + 1))
        out[name] = w
        out["m_" + name] = s * _jax.random.normal(km, w.shape, _jnp.float32)
        out["v_" + name] = (s * s) * _jax.random.uniform(kv, w.shape, _jnp.float32, 0.5, 1.5)
    if N_MICROBATCH > 1:
        for name, axis in PER_EXAMPLE_BATCH_AXIS.items():
            out[name] = _to_microbatches(out[name], axis)
    return {'x': out['x'], 'meta_tokens': out['meta_tokens'], 'norm_w': out['norm_w'], 'w_in': out['w_in'], 'b_in': out['b_in'], 'lb_logits': out['lb_logits'], 'hg_norm_w': out['hg_norm_w'], 'pool_w': out['pool_w'], 'pool_scale': out['pool_scale'], 'w_down_hg': out['w_down_hg'], 'w_down_pool': out['w_down_pool'], 'w_out': out['w_out'], 'final_norm_w': out['final_norm_w'], 'loss_target': out['loss_target'], 'm_meta_tokens': out['m_meta_tokens'], 'm_norm_w': out['m_norm_w'], 'm_w_in': out['m_w_in'], 'm_b_in': out['m_b_in'], 'm_lb_logits': out['m_lb_logits'], 'm_hg_norm_w': out['m_hg_norm_w'], 'm_pool_w': out['m_pool_w'], 'm_pool_scale': out['m_pool_scale'], 'm_w_down_hg': out['m_w_down_hg'], 'm_w_down_pool': out['m_w_down_pool'], 'm_w_out': out['m_w_out'], 'm_final_norm_w': out['m_final_norm_w'], 'v_meta_tokens': out['v_meta_tokens'], 'v_norm_w': out['v_norm_w'], 'v_w_in': out['v_w_in'], 'v_b_in': out['v_b_in'], 'v_lb_logits': out['v_lb_logits'], 'v_hg_norm_w': out['v_hg_norm_w'], 'v_pool_w': out['v_pool_w'], 'v_pool_scale': out['v_pool_scale'], 'v_w_down_hg': out['v_w_down_hg'], 'v_w_down_pool': out['v_w_down_pool'], 'v_w_out': out['v_w_out'], 'v_final_norm_w': out['v_final_norm_w']}


def _loss(weights, diff, rest, loss_target):
    with _jax.named_scope("forward"):
        args = {**rest, TWIN_DIFF_INPUT: diff, **{k: w.astype(_WEIGHT_DTYPES[k]) for k, w in weights.items()}}
        y = _forward(args)
    with _jax.named_scope("loss_head"):
        err = _jnp.square(y.astype(_jnp.float32) - loss_target)
        return 0.5 * _jnp.sum(_jnp.mean(err, axis=-1)) if err.ndim else 0.5 * err


def _adamw(w, g, m, v):
    m = ADAM_B1 * m + (1.0 - ADAM_B1) * g
    v = ADAM_B2 * v + (1.0 - ADAM_B2) * _jnp.square(g)
    m_hat = m / (1.0 - ADAM_B1 ** ADAM_STEP)
    v_hat = v / (1.0 - ADAM_B2 ** ADAM_STEP)
    delta = -ADAM_LR * (m_hat / (_jnp.sqrt(v_hat) + ADAM_EPS) + ADAM_WD * w)
    return delta, m, v


def reference(x, meta_tokens, norm_w, w_in, b_in, lb_logits, hg_norm_w, pool_w, pool_scale, w_down_hg, w_down_pool, w_out, final_norm_w, loss_target, m_meta_tokens, m_norm_w, m_w_in, m_b_in, m_lb_logits, m_hg_norm_w, m_pool_w, m_pool_scale, m_w_down_hg, m_w_down_pool, m_w_out, m_final_norm_w, v_meta_tokens, v_norm_w, v_w_in, v_b_in, v_lb_logits, v_hg_norm_w, v_pool_w, v_pool_scale, v_w_down_hg, v_w_down_pool, v_w_out, v_final_norm_w):
    given = dict(x=x, meta_tokens=meta_tokens, norm_w=norm_w, w_in=w_in, b_in=b_in, lb_logits=lb_logits, hg_norm_w=hg_norm_w, pool_w=pool_w, pool_scale=pool_scale, w_down_hg=w_down_hg, w_down_pool=w_down_pool, w_out=w_out, final_norm_w=final_norm_w, loss_target=loss_target, m_meta_tokens=m_meta_tokens, m_norm_w=m_norm_w, m_w_in=m_w_in, m_b_in=m_b_in, m_lb_logits=m_lb_logits, m_hg_norm_w=m_hg_norm_w, m_pool_w=m_pool_w, m_pool_scale=m_pool_scale, m_w_down_hg=m_w_down_hg, m_w_down_pool=m_w_down_pool, m_w_out=m_w_out, m_final_norm_w=m_final_norm_w, v_meta_tokens=v_meta_tokens, v_norm_w=v_norm_w, v_w_in=v_w_in, v_b_in=v_b_in, v_lb_logits=v_lb_logits, v_hg_norm_w=v_hg_norm_w, v_pool_w=v_pool_w, v_pool_scale=v_pool_scale, v_w_down_hg=v_w_down_hg, v_w_down_pool=v_w_down_pool, v_w_out=v_w_out, v_final_norm_w=v_final_norm_w)
    weights = {n: given[n] for n in TWIN_WEIGHTS}
    shared = {n: given[n] for n in SHARED_INPUTS}
    per_example = {n: given[n] for n in ['x']}
    grad_fn = _jax.value_and_grad(_loss, argnums=(0, 1))

    def one_microbatch(ex, loss_target):
        ex = dict(ex)
        diff = ex.pop(TWIN_DIFF_INPUT)
        return grad_fn(weights, diff, {**shared, **ex}, loss_target)

    if N_MICROBATCH == 1:
        loss, (grad_w, grad_x) = one_microbatch(per_example, given["loss_target"])
    else:
        def body(carry, xs):
            loss_sum, grad_sum = carry
            l_k, (gw_k, gx_k) = one_microbatch(xs[0], xs[1])
            with _jax.named_scope("update"):
                return (loss_sum + l_k, _jax.tree.map(_jnp.add, grad_sum, gw_k)), gx_k

        init = (_jnp.zeros((), _jnp.float32), _jax.tree.map(_jnp.zeros_like, weights))
        (loss, grad_w), grad_x = _jax.lax.scan(body, init, (per_example, given["loss_target"]))
    with _jax.named_scope("update"):
        delta_w, new_m, new_v = {}, {}, {}
        for n in TWIN_WEIGHTS:
            delta_w[n], new_m[n], new_v[n] = _adamw(weights[n], grad_w[n], given["m_" + n], given["v_" + n])
    return (loss, grad_x, *[grad_w[n] for n in TWIN_WEIGHTS], *[delta_w[n] for n in TWIN_WEIGHTS],
            *[new_m[n] for n in TWIN_WEIGHTS], *[new_v[n] for n in TWIN_WEIGHTS])
```

```python
import functools

import numpy as np
import jax
import jax.numpy as jnp
from jax import lax
from jax.experimental import pallas as pl
from jax.experimental.pallas import tpu as pltpu

F32 = jnp.float32
BF16 = jnp.bfloat16

D_MODEL = 1024
N_SEG = 8
N_HEADS = 8
HEAD_DIM = 128
CHUNK = 64
N_META = 16
PAD_ROWS = CHUNK - N_META
FIRST_TOKEN_ROW = CHUNK
LEVELS = (32, 16, 8, 4, 2, 1)
N_EXP = 2 + len(LEVELS)
POOL_WINDOWS = (2, 4, 8, 16)
POOL_GDIM = D_MODEL // len(POOL_WINDOWS)
HALO = 16
EPS = 1e-6
N_CHIPS = 4
N_DEV = 8

ADAM_LR = 0.001
ADAM_B1 = 0.9
ADAM_B2 = 0.999
ADAM_EPS = 1e-08
ADAM_WD = 0.01
ADAM_STEP = 10

VMEM_LIMIT_BYTES = 56 * 1024 * 1024

ROW_LOSS = 0
ROW_META = 1
ROW_NORM_W = ROW_META + N_META
ROW_B_IN = ROW_NORM_W + 1
ROW_LB = ROW_B_IN + N_SEG
ROW_HG_W = ROW_LB + 2
ROW_POOL_SCALE = ROW_HG_W + 1
ROW_FINAL_W = ROW_POOL_SCALE + 1
SMALL_ROWS = 32


def _tile(total, cap, mult=16):
    best = None
    for t in range(mult, min(total, cap) + 1, mult):
        if total % t == 0:
            best = t
    assert best is not None, (total, cap, mult)
    return best


def _params(sem=None):
    return pltpu.CompilerParams(dimension_semantics=sem, vmem_limit_bytes=VMEM_LIMIT_BYTES)


def _dot(a, b):
    return jnp.dot(a, b, preferred_element_type=F32)


def _dot_nt(a, b):
    return lax.dot_general(a, b, (((1,), (1,)), ((), ())), preferred_element_type=F32)


def _dot_tn(a, b):
    return lax.dot_general(a, b, (((0,), (0,)), ((), ())), preferred_element_type=F32)


def _sigmoid_pair(x):
    t = jnp.exp(-jnp.abs(x))
    r = 1.0 / (1.0 + t)
    pos = x >= 0
    return jnp.where(pos, r, t * r), jnp.where(pos, t * r, r)


def _exponent_matrix():
    t = np.arange(CHUNK)[:, None]
    j = np.arange(CHUNK)[None, :]
    blocks = [j <= t, j > t]
    for m in LEVELS:
        rho = (t // (2 * m)) * (2 * m) + m
        upper = (t >= rho) & (j > rho) & (j <= t)
        lower = (t < rho) & (j > t) & (j <= rho)
        blocks.append(upper | lower)
    return np.concatenate(blocks, axis=0).astype(np.float32)


def _pair_masks():
    t = np.arange(CHUNK)[:, None]
    s = np.arange(CHUNK)[None, :]
    masks = [t == s]
    for m in LEVELS:
        same = (t // (2 * m)) == (s // (2 * m))
        masks.append(same & ((t % (2 * m)) >= m) & ((s % (2 * m)) < m))
    return np.stack(masks).astype(np.float32)


def _split3(x):
    hi = x.astype(BF16)
    r = x - hi.astype(F32)
    mid = r.astype(BF16)
    lo = (r - mid.astype(F32)).astype(BF16)
    return hi, mid, lo


def _chunk_forward(q, fz, lb, valid, wexp, masks):
    sg, sn = _sigmoid_pair(fz)
    f = lb + (1.0 - lb) * sg
    g = jnp.where(valid, jnp.log(f), 0.0)
    kk = jnp.where(valid, (1.0 - lb) * sn, 0.0)
    q = jnp.where(valid, q, 0.0)
    hi, mid, lo = _split3(g)
    ex = _dot(wexp, jnp.concatenate([hi, mid, lo], axis=1))
    ex = ex[:, :HEAD_DIM] + ex[:, HEAD_DIM:2 * HEAD_DIM] + ex[:, 2 * HEAD_DIM:]
    e = jnp.exp(ex)
    e_b = e[0:CHUNK]
    e_c = e[CHUNK:2 * CHUNK]
    a = masks[0] * _dot_nt(q.astype(BF16), kk.astype(BF16))
    qm, km = [], []
    for l in range(len(LEVELS)):
        e_m = e[(2 + l) * CHUNK:(3 + l) * CHUNK]
        qm.append(q * e_m)
        km.append(kk * e_m)
        a = a + masks[1 + l] * _dot_nt(qm[l].astype(BF16), km[l].astype(BF16))
    return dict(sg=sg, sn=sn, f=f, kk=kk, q=q, e=e, e_b=e_b, e_c=e_c, a=a, qm=qm, km=km)


def _lower_bound(lbl):
    return 1.0 / (1.0 + jnp.exp(lbl[1:2, :] - lbl[0:1, :]))


def _in_proj(z, norm_w, w4, b_in, rows):
    tm = _tile(rows, 1040)

    def body(z_ref, nw_ref, w_ref, b_ref, h_ref, p_ref):
        @pl.when(pl.program_id(1) == 0)
        def _():
            zt = z_ref[...]
            rstd = lax.rsqrt(jnp.mean(zt * zt, axis=-1, keepdims=True) + EPS)
            h_ref[...] = (zt * rstd * nw_ref[...]).astype(BF16)

        p_ref[0] = _dot(h_ref[...], w_ref[0]) + b_ref[...]

    return pl.pallas_call(
        body, name="in_proj",
        grid=(rows // tm, N_SEG),
        in_specs=[
            pl.BlockSpec((tm, D_MODEL), lambda i, k: (i, 0)),
            pl.BlockSpec((1, D_MODEL), lambda i, k: (0, 0)),
            pl.BlockSpec((1, D_MODEL, D_MODEL), lambda i, k: (k // 2, 0, k % 2)),
            pl.BlockSpec((1, D_MODEL), lambda i, k: (0, k)),
        ],
        out_specs=[
            pl.BlockSpec((tm, D_MODEL), lambda i, k: (i, 0)),
            pl.BlockSpec((1, tm, D_MODEL), lambda i, k: (k, i, 0)),
        ],
        out_shape=[
            jax.ShapeDtypeStruct((rows, D_MODEL), BF16),
            jax.ShapeDtypeStruct((N_SEG, rows, D_MODEL), F32),
        ],
        compiler_params=_params(("arbitrary", "arbitrary")),
    )(z, norm_w, w4, b_in)


def _hgrn_forward(p3, lb_logits, wexp, masks, rows):
    n_chunks = rows // CHUNK
    cpb = _tile(n_chunks, 13, mult=1)
    rb_rows = cpb * CHUNK

    def body(q_ref, fz_ref, v_ref, lbl_ref, wexp_ref, mask_ref, o_ref, s_ref, st_ref):
        rb = pl.program_id(1)

        @pl.when(rb == 0)
        def _():
            st_ref[...] = jnp.zeros_like(st_ref)

        lb = _lower_bound(lbl_ref[...])
        wexp = wexp_ref[...]
        masks = mask_ref[...]

        def chunk(c, carry):
            r0 = pl.multiple_of(c * CHUNK, CHUNK)
            row = rb * rb_rows + r0 + lax.broadcasted_iota(jnp.int32, (CHUNK, 1), 0)
            valid = row >= PAD_ROWS
            cf = _chunk_forward(q_ref[0, pl.ds(r0, CHUNK), :], fz_ref[0, pl.ds(r0, CHUNK), :],
                                lb, valid, wexp, masks)
            v16 = jnp.where(valid, v_ref[0, pl.ds(r0, CHUNK), :], 0.0).astype(BF16)
            st = st_ref[...]
            s_ref[0, c] = st
            o = _dot_nt((cf["q"] * cf["e_b"]).astype(BF16), st.astype(BF16))
            o_ref[pl.ds(r0, CHUNK), :] = o + _dot(cf["a"].astype(BF16), v16)
            kc16 = (cf["kk"] * cf["e_c"]).astype(BF16)
            st_ref[...] = st * cf["e_b"][CHUNK - 1:CHUNK, :] + _dot_tn(v16, kc16)
            return carry

        lax.fori_loop(0, cpb, chunk, 0)

    head_block = lambda seg: pl.BlockSpec((1, rb_rows, HEAD_DIM), lambda h, r: (seg, r, h))
    return pl.pallas_call(
        body, name="hgrn_forward",
        grid=(N_HEADS, n_chunks // cpb),
        in_specs=[
            head_block(0), head_block(1), head_block(2),
            pl.BlockSpec((2, HEAD_DIM), lambda h, r: (0, h)),
            pl.BlockSpec((N_EXP * CHUNK, CHUNK), lambda h, r: (0, 0)),
            pl.BlockSpec((1 + len(LEVELS), CHUNK, CHUNK), lambda h, r: (0, 0, 0)),
        ],
        out_specs=[
            pl.BlockSpec((rb_rows, HEAD_DIM), lambda h, r: (r, h)),
            pl.BlockSpec((1, cpb, HEAD_DIM, HEAD_DIM), lambda h, r: (h, r, 0, 0)),
        ],
        out_shape=[
            jax.ShapeDtypeStruct((rows, D_MODEL), F32),
            jax.ShapeDtypeStruct((N_HEADS, n_chunks, HEAD_DIM, HEAD_DIM), F32),
        ],
        scratch_shapes=[pltpu.VMEM((HEAD_DIM, HEAD_DIM), F32)],
        compiler_params=_params(("arbitrary", "arbitrary")),
    )(p3, p3, p3, lb_logits, wexp, masks)


def _hgrn_backward(p3, d_o, states, lb_logits, wexp, wexp_t, masks, rows):
    n_chunks = rows // CHUNK
    cpb = _tile(n_chunks, 13, mult=1)
    rb_rows = cpb * CHUNK
    n_rb = n_chunks // cpb

    def body(q_ref, fz_ref, v_ref, do_ref, s_ref, lbl_ref, wexp_ref, wexpt_ref, mask_ref,
             dp_ref, dlb_ref, dst_ref):
        step = pl.program_id(1)
        rb = n_rb - 1 - step

        @pl.when(step == 0)
        def _():
            dst_ref[...] = jnp.zeros_like(dst_ref)
            dlb_ref[...] = jnp.zeros_like(dlb_ref)

        lb = _lower_bound(lbl_ref[...])
        wexp = wexp_ref[...]
        wexp_t = wexpt_ref[...]
        masks = mask_ref[...]
        last_row = lax.broadcasted_iota(jnp.int32, (CHUNK, 1), 0) == CHUNK - 1

        def chunk(i, carry):
            c = cpb - 1 - i
            r0 = pl.multiple_of(c * CHUNK, CHUNK)
            row = rb * rb_rows + r0 + lax.broadcasted_iota(jnp.int32, (CHUNK, 1), 0)
            valid = row >= PAD_ROWS
            cf = _chunk_forward(q_ref[0, pl.ds(r0, CHUNK), :], fz_ref[0, pl.ds(r0, CHUNK), :],
                                lb, valid, wexp, masks)
            q, kk, e_b, e_c = cf["q"], cf["kk"], cf["e_b"], cf["e_c"]
            v16 = jnp.where(valid, v_ref[0, pl.ds(r0, CHUNK), :], 0.0).astype(BF16)
            do16 = do_ref[pl.ds(r0, CHUNK), :].astype(BF16)
            st = s_ref[0, c]
            dst = dst_ref[...]
            dst16 = dst.astype(BF16)
            qb = q * e_b
            kc = kk * e_c
            q16, kk16 = q.astype(BF16), kk.astype(BF16)

            dv = _dot_tn(cf["a"].astype(BF16), do16) + _dot_nt(kc.astype(BF16), dst16)
            da = _dot_nt(do16, v16)
            dqb = _dot(do16, st.astype(BF16))
            dkc = _dot(v16, dst16)
            e_last = e_b[CHUNK - 1:CHUNK, :]
            de = jnp.sum(dst * st, axis=0, keepdims=True)
            dst_ref[...] = dst * e_last + _dot_tn(do16, qb.astype(BF16))

            dq = e_b * dqb
            dkk = e_c * dkc
            dx = [qb * dqb + jnp.where(last_row, de * e_last, 0.0), kc * dkc]
            dm0 = (masks[0] * da).astype(BF16)
            dq = dq + _dot(dm0, kk16)
            dkk = dkk + _dot(dm0, q16)
            for l in range(len(LEVELS)):
                e_m = cf["e"][(2 + l) * CHUNK:(3 + l) * CHUNK]
                dm = (masks[1 + l] * da).astype(BF16)
                dqm = _dot(dm, cf["km"][l].astype(BF16))
                dkm = _dot_tn(dm, cf["qm"][l].astype(BF16))
                dq = dq + e_m * dqm
                dkk = dkk + e_m * dkm
                dx.append(cf["qm"][l] * dqm + cf["km"][l] * dkm)
            dxa = jnp.concatenate(dx, axis=0)
            hi = dxa.astype(BF16)
            mid = (dxa - hi.astype(F32)).astype(BF16)
            dg = _dot(wexp_t, jnp.concatenate([hi, mid], axis=1))
            dg = dg[:, :HEAD_DIM] + dg[:, HEAD_DIM:]

            t = jnp.where(valid, dg / cf["f"] - dkk, 0.0)
            dfz = (1.0 - lb) * cf["sg"] * cf["sn"] * t
            dlb_ref[...] += jnp.sum(cf["sn"] * t, axis=0, keepdims=True)
            dp_ref[0, pl.ds(r0, CHUNK), :] = jnp.where(valid, dq, 0.0).astype(BF16)
            dp_ref[1, pl.ds(r0, CHUNK), :] = dfz.astype(BF16)
            dp_ref[2, pl.ds(r0, CHUNK), :] = jnp.where(valid, dv, 0.0).astype(BF16)
            return carry

        lax.fori_loop(0, cpb, chunk, 0)

    head_block = lambda seg: pl.BlockSpec((1, rb_rows, HEAD_DIM), lambda h, s: (seg, n_rb - 1 - s, h))
    return pl.pallas_call(
        body, name="hgrn_backward",
        grid=(N_HEADS, n_rb),
        in_specs=[
            head_block(0), head_block(1), head_block(2),
            pl.BlockSpec((rb_rows, HEAD_DIM), lambda h, s: (n_rb - 1 - s, h)),
            pl.BlockSpec((1, cpb, HEAD_DIM, HEAD_DIM), lambda h, s: (h, n_rb - 1 - s, 0, 0)),
            pl.BlockSpec((2, HEAD_DIM), lambda h, s: (0, h)),
            pl.BlockSpec((N_EXP * CHUNK, CHUNK), lambda h, s: (0, 0)),
            pl.BlockSpec((CHUNK, N_EXP * CHUNK), lambda h, s: (0, 0)),
            pl.BlockSpec((1 + len(LEVELS), CHUNK, CHUNK), lambda h, s: (0, 0, 0)),
        ],
        out_specs=[
            pl.BlockSpec((3, rb_rows, HEAD_DIM), lambda h, s: (0, n_rb - 1 - s, h)),
            pl.BlockSpec((1, HEAD_DIM), lambda h, s: (0, h)),
        ],
        out_shape=[
            jax.ShapeDtypeStruct((3, rows, D_MODEL), BF16),
            jax.ShapeDtypeStruct((1, D_MODEL), F32),
        ],
        scratch_shapes=[pltpu.VMEM((HEAD_DIM, HEAD_DIM), F32)],
        compiler_params=_params(("arbitrary", "arbitrary")),
    )(p3, p3, p3, d_o, states, lb_logits, wexp, wexp_t, masks)


def _silu_and_grad(x):
    s, _ = _sigmoid_pair(x)
    return x * s, s * (1.0 + x * (1.0 - s))


def _window_sum(ext, width, forward_looking):
    n = ext.shape[0]
    s = ext
    step = 1
    while step < width:
        s = s + pltpu.roll(s, (n - step) if forward_looking else step, 0)
        step *= 2
    return s


def _mixers(o, p3, z, tgt, wdh, wdp, wout, poolw, hg_w, pool_scale, final_w, rows):
    tm = _tile(rows, 160)
    nt = rows // tm
    halo_blocks = tm // HALO
    n_grp = len(POOL_WINDOWS)

    def body(o_ref, ghg_ref, u_ref, gpl_ref, mhg_ref, mpl_ref, uh_ref, z_ref, t_ref,
             wdh_ref, wdp_ref, wout_ref, pw_ref, hgw_ref, ps_ref, fw_ref,
             do_ref, dz2_ref, dp_ref, dwdh_ref, dwdp_ref, dwout_ref, dpw_ref, small_ref, carry_ref):
        step = pl.program_id(0)
        tile = nt - 1 - step

        @pl.when(step == 0)
        def _():
            dwdh_ref[...] = jnp.zeros_like(dwdh_ref)
            dwdp_ref[...] = jnp.zeros_like(dwdp_ref)
            dwout_ref[...] = jnp.zeros_like(dwout_ref)
            dpw_ref[...] = jnp.zeros_like(dpw_ref)
            small_ref[...] = jnp.zeros_like(small_ref)
            carry_ref[...] = jnp.zeros_like(carry_ref)

        row = tile * tm + lax.broadcasted_iota(jnp.int32, (tm, 1), 0)
        real = row >= PAD_ROWS
        pos1 = jnp.maximum(row - PAD_ROWS + 1, 1).astype(F32)

        u = jnp.where(real, u_ref[0], 0.0)
        halo_row = tile * tm - HALO + lax.broadcasted_iota(jnp.int32, (HALO, 1), 0)
        uh = jnp.where(halo_row >= PAD_ROWS, uh_ref[0], 0.0)
        ext = jnp.concatenate([uh, u], axis=0)
        pooled, inv_cnt, mixed = [], [], []
        for g, w in enumerate(POOL_WINDOWS):
            cols = slice(g * POOL_GDIM, (g + 1) * POOL_GDIM)
            inv = 1.0 / jnp.minimum(pos1, float(w))
            ws = _window_sum(ext[:, cols], w, False)[HALO:]
            pg = (ws * inv - u[:, cols]).astype(BF16)
            pooled.append(pg)
            inv_cnt.append(inv)
            mixed.append(_dot(pg, pw_ref[g]))
        mixed = jnp.concatenate(mixed, axis=1)
        gpl = gpl_ref[0]
        sp, dsp = _silu_and_grad(gpl)
        ps = ps_ref[...]
        a_pool = (mixed * ps * sp).astype(BF16)
        y_pool = _dot(a_pool, wdp_ref[...])

        o = o_ref[...]
        o_hat, rstd_h = [], []
        for h in range(N_HEADS):
            oh = o[:, h * HEAD_DIM:(h + 1) * HEAD_DIM]
            r = lax.rsqrt(jnp.mean(oh * oh, axis=-1, keepdims=True) + EPS)
            rstd_h.append(r)
            o_hat.append(oh * r)
        o_hat = jnp.concatenate(o_hat, axis=1)
        hgw = hgw_ref[...]
        o_n = o_hat * hgw
        ghg = ghg_ref[0]
        sh, dsh = _silu_and_grad(ghg)
        a_hg = (o_n * sh).astype(BF16)
        y_hg = _dot(a_hg, wdh_ref[...])

        s_mh, _ = _sigmoid_pair(mhg_ref[0])
        s_mp, _ = _sigmoid_pair(mpl_ref[0])
        merged = (s_mh * y_hg + s_mp * y_pool).astype(BF16)
        z2 = z_ref[...] + _dot(merged, wout_ref[...])
        rstd2 = lax.rsqrt(jnp.mean(z2 * z2, axis=-1, keepdims=True) + EPS)
        zh = z2 * rstd2
        fw = fw_ref[...]
        err = jnp.where(row >= FIRST_TOKEN_ROW, zh * fw - t_ref[...], 0.0)
        small_ref[ROW_LOSS:ROW_LOSS + 1, :] += jnp.sum(err * err, axis=0, keepdims=True) * (0.5 / D_MODEL)
        dy = err * (1.0 / D_MODEL)

        small_ref[ROW_FINAL_W:ROW_FINAL_W + 1, :] += jnp.sum(dy * zh, axis=0, keepdims=True)
        uu = dy * fw
        dz2 = rstd2 * (uu - zh * jnp.mean(uu * zh, axis=-1, keepdims=True))
        dz2_ref[...] = dz2
        dz2_16 = dz2.astype(BF16)
        dmerged = _dot_nt(dz2_16, wout_ref[...])
        dwout_ref[...] += _dot_tn(merged, dz2_16)
        dy_hg = (s_mh * dmerged).astype(BF16)
        dy_pool = (s_mp * dmerged).astype(BF16)
        dp_ref[3] = (dmerged * y_hg * s_mh * (1.0 - s_mh)).astype(BF16)
        dp_ref[4] = (dmerged * y_pool * s_mp * (1.0 - s_mp)).astype(BF16)

        da_hg = _dot_nt(dy_hg, wdh_ref[...])
        dwdh_ref[...] += _dot_tn(a_hg, dy_hg)
        dp_ref[0] = (da_hg * o_n * dsh).astype(BF16)
        do_n = da_hg * sh
        small_ref[ROW_HG_W:ROW_HG_W + 1, :] += jnp.sum(do_n * o_hat, axis=0, keepdims=True)
        d_hat = do_n * hgw
        for h in range(N_HEADS):
            cols = slice(h * HEAD_DIM, (h + 1) * HEAD_DIM)
            dh_, oh_ = d_hat[:, cols], o_hat[:, cols]
            do_ref[:, cols] = rstd_h[h] * (dh_ - oh_ * jnp.mean(dh_ * oh_, axis=-1, keepdims=True))

        da_pool = _dot_nt(dy_pool, wdp_ref[...])
        dwdp_ref[...] += _dot_tn(a_pool, dy_pool)
        small_ref[ROW_POOL_SCALE:ROW_POOL_SCALE + 1, :] += jnp.sum(da_pool * mixed * sp, axis=0, keepdims=True)
        dp_ref[2] = (da_pool * mixed * ps * dsp).astype(BF16)
        dmixed = (da_pool * ps * sp).astype(BF16)
        carry = carry_ref[...]
        du, new_carry = [], []
        for g, w in enumerate(POOL_WINDOWS):
            cols = slice(g * POOL_GDIM, (g + 1) * POOL_GDIM)
            dmg = dmixed[:, cols]
            dpooled = _dot_nt(dmg, pw_ref[g])
            dpw_ref[g] += _dot_tn(pooled[g], dmg)
            dps = dpooled * inv_cnt[g]
            ext_b = jnp.concatenate([dps, carry[:, cols]], axis=0)
            du.append(_window_sum(ext_b, w, True)[:tm] - dpooled)
            new_carry.append(dps[:HALO])
        dp_ref[1] = jnp.where(real, jnp.concatenate(du, axis=1), 0.0).astype(BF16)
        carry_ref[...] = jnp.concatenate(new_carry, axis=1)

    row_block = pl.BlockSpec((tm, D_MODEL), lambda s: (nt - 1 - s, 0))
    seg_block = lambda seg: pl.BlockSpec((1, tm, D_MODEL), lambda s: (seg, nt - 1 - s, 0))
    whole = pl.BlockSpec(memory_space=pltpu.VMEM)
    return pl.pallas_call(
        body, name="mixers",
        grid=(nt,),
        in_specs=[
            row_block, seg_block(3), seg_block(4), seg_block(5), seg_block(6), seg_block(7),
            pl.BlockSpec((1, HALO, D_MODEL),
                         lambda s: (4, jnp.maximum((nt - 1 - s) * halo_blocks - 1, 0), 0)),
            row_block, row_block,
            whole, whole, whole, whole, whole, whole, whole,
        ],
        out_specs=[
            row_block, row_block,
            pl.BlockSpec((5, tm, D_MODEL), lambda s: (0, nt - 1 - s, 0)),
            whole, whole, whole, whole, whole,
        ],
        out_shape=[
            jax.ShapeDtypeStruct((rows, D_MODEL), F32),
            jax.ShapeDtypeStruct((rows, D_MODEL), F32),
            jax.ShapeDtypeStruct((5, rows, D_MODEL), BF16),
            jax.ShapeDtypeStruct((D_MODEL, D_MODEL), F32),
            jax.ShapeDtypeStruct((D_MODEL, D_MODEL), F32),
            jax.ShapeDtypeStruct((D_MODEL, D_MODEL), F32),
            jax.ShapeDtypeStruct((n_grp, POOL_GDIM, POOL_GDIM), F32),
            jax.ShapeDtypeStruct((SMALL_ROWS, D_MODEL), F32),
        ],
        scratch_shapes=[pltpu.VMEM((HALO, D_MODEL), F32)],
        compiler_params=_params(("arbitrary",)),
    )(o, p3, p3, p3, p3, p3, p3, z, tgt, wdh, wdp, wout, poolw, hg_w, pool_scale, final_w)


def _seg_specs(tm, row_of, seg_of):
    def spec_a(*g):
        k = seg_of(*g)
        return (jnp.minimum(k, 2), jnp.where(k < 3, row_of(*g), 0), 0)

    def spec_b(*g):
        k = seg_of(*g)
        return (jnp.maximum(k - 3, 0), jnp.where(k >= 3, row_of(*g), 0), 0)

    return pl.BlockSpec((1, tm, D_MODEL), spec_a), pl.BlockSpec((1, tm, D_MODEL), spec_b)


def _in_proj_weight_grad(h, dpa, dpb, rows):
    tm = _tile(rows, 1040)
    nt = rows // tm

    def body(h_ref, dpa_ref, dpb_ref, dw_ref, dw16_ref, db_ref, acc_ref, bacc_ref):
        k, i = pl.program_id(0), pl.program_id(1)

        @pl.when(i == 0)
        def _():
            acc_ref[...] = jnp.zeros_like(acc_ref)
            bacc_ref[...] = jnp.zeros_like(bacc_ref)

        def accumulate(dp_ref):
            dp = dp_ref[0]
            acc_ref[...] += _dot_tn(h_ref[...], dp)
            bacc_ref[...] += jnp.sum(dp.astype(F32), axis=0, keepdims=True)

        @pl.when(k < 3)
        def _():
            accumulate(dpa_ref)

        @pl.when(k >= 3)
        def _():
            accumulate(dpb_ref)

        @pl.when(i == nt - 1)
        def _():
            dw_ref[0] = acc_ref[...]
            dw16_ref[0] = acc_ref[...].astype(BF16)
            db_ref[...] = bacc_ref[...]

    spec_a, spec_b = _seg_specs(tm, lambda k, i: i, lambda k, i: k)
    w_block = pl.BlockSpec((1, D_MODEL, D_MODEL), lambda k, i: (k // 2, 0, k % 2))
    return pl.pallas_call(
        body, name="in_proj_weight_grad",
        grid=(N_SEG, nt),
        in_specs=[pl.BlockSpec((tm, D_MODEL), lambda k, i: (i, 0)), spec_a, spec_b],
        out_specs=[w_block, w_block, pl.BlockSpec((1, D_MODEL), lambda k, i: (0, k))],
        out_shape=[
            jax.ShapeDtypeStruct((N_CHIPS, D_MODEL, 2 * D_MODEL), F32),
            jax.ShapeDtypeStruct((N_CHIPS, D_MODEL, 2 * D_MODEL), BF16),
            jax.ShapeDtypeStruct((1, N_SEG * D_MODEL), F32),
        ],
        scratch_shapes=[pltpu.VMEM((D_MODEL, D_MODEL), F32), pltpu.VMEM((1, D_MODEL), F32)],
        compiler_params=_params(("arbitrary", "arbitrary")),
    )(h, dpa, dpb)


def _input_grad(dpa, dpb, w4, z, dz2, norm_w, rows):
    tm = _tile(rows, 1040)
    nt = rows // tm

    def body(dpa_ref, dpb_ref, w_ref, z_ref, dz2_ref, nw_ref, dz_ref, dnw_ref, acc_ref):
        i, k = pl.program_id(0), pl.program_id(1)

        @pl.when((i == 0) & (k == 0))
        def _():
            dnw_ref[...] = jnp.zeros_like(dnw_ref)

        @pl.when(k == 0)
        def _():
            acc_ref[...] = jnp.zeros_like(acc_ref)

        @pl.when(k < 3)
        def _():
            acc_ref[...] += _dot_nt(dpa_ref[0], w_ref[0])

        @pl.when(k >= 3)
        def _():
            acc_ref[...] += _dot_nt(dpb_ref[0], w_ref[0])

        @pl.when(k == N_SEG - 1)
        def _():
            zt = z_ref[...]
            rstd = lax.rsqrt(jnp.mean(zt * zt, axis=-1, keepdims=True) + EPS)
            zh = zt * rstd
            dh = acc_ref[...]
            dnw_ref[...] += jnp.sum(dh * zh, axis=0, keepdims=True)
            uu = dh * nw_ref[...]
            dz_ref[...] = dz2_ref[...] + rstd * (uu - zh * jnp.mean(uu * zh, axis=-1, keepdims=True))

    spec_a, spec_b = _seg_specs(tm, lambda i, k: i, lambda i, k: k)
    last_only = pl.BlockSpec((tm, D_MODEL), lambda i, k: (jnp.where(k == N_SEG - 1, i, 0), 0))
    return pl.pallas_call(
        body, name="input_grad",
        grid=(nt, N_SEG),
        in_specs=[
            spec_a, spec_b,
            pl.BlockSpec((1, D_MODEL, D_MODEL), lambda i, k: (k // 2, 0, k % 2)),
            last_only, last_only,
            pl.BlockSpec((1, D_MODEL), lambda i, k: (0, 0)),
        ],
        out_specs=[
            pl.BlockSpec((tm, D_MODEL), lambda i, k: (i, 0)),
            pl.BlockSpec((1, D_MODEL), lambda i, k: (0, 0)),
        ],
        out_shape=[
            jax.ShapeDtypeStruct((rows, D_MODEL), F32),
            jax.ShapeDtypeStruct((1, D_MODEL), F32),
        ],
        scratch_shapes=[pltpu.VMEM((tm, D_MODEL), F32)],
        compiler_params=_params(("arbitrary", "arbitrary")),
    )(dpa, dpb, w4, z, dz2, norm_w)


def _local_step(z, tgt, w4, blob4, norm_w, b_in, lb_logits, hg_w, pool_scale, final_w):
    rows = z.shape[0]
    q = D_MODEL // N_CHIPS
    wdh = blob4[:, 0:q].reshape(D_MODEL, D_MODEL)
    wdp = blob4[:, q:2 * q].reshape(D_MODEL, D_MODEL)
    wout = blob4[:, 2 * q:3 * q].reshape(D_MODEL, D_MODEL)
    n_grp = len(POOL_WINDOWS)
    pg = POOL_GDIM // N_CHIPS
    poolw = blob4[:, 3 * q:].reshape(N_CHIPS, n_grp, pg, POOL_GDIM).transpose(1, 0, 2, 3)
    poolw = poolw.reshape(n_grp, POOL_GDIM, POOL_GDIM)

    wexp = jnp.asarray(_exponent_matrix(), BF16)
    wexp_t = jnp.asarray(_exponent_matrix().T, BF16)
    masks = jnp.asarray(_pair_masks(), F32)

    h, p3 = _in_proj(z, norm_w, w4, b_in, rows)
    o, states = _hgrn_forward(p3, lb_logits, wexp, masks, rows)
    d_o, dz2, dpb, dwdh, dwdp, dwout, dpw, small = _mixers(
        o, p3, z, tgt, wdh, wdp, wout, poolw, hg_w, pool_scale, final_w, rows)
    dpa, dlb = _hgrn_backward(p3, d_o, states, lb_logits, wexp, wexp_t, masks, rows)
    dw4, dw4_16, db_in = _in_proj_weight_grad(h, dpa, dpb, rows)
    dz, dnw = _input_grad(dpa, dpb, w4, z, dz2, norm_w, rows)

    dpw4 = dpw.reshape(n_grp, N_CHIPS, pg, POOL_GDIM).transpose(1, 0, 2, 3)
    dpw4 = dpw4.reshape(N_CHIPS, n_grp * pg * POOL_GDIM // D_MODEL, D_MODEL)
    dblob4 = jnp.concatenate([dwdh.reshape(N_CHIPS, q, D_MODEL), dwdp.reshape(N_CHIPS, q, D_MODEL),
                              dwout.reshape(N_CHIPS, q, D_MODEL), dpw4], axis=1)
    small = jnp.concatenate([
        small[ROW_LOSS:ROW_LOSS + 1],
        dz[PAD_ROWS:PAD_ROWS + N_META],
        dnw,
        db_in.reshape(N_SEG, D_MODEL),
        dlb, jnp.zeros_like(dlb),
        small[ROW_HG_W:ROW_HG_W + 1], small[ROW_POOL_SCALE:ROW_POOL_SCALE + 1],
        small[ROW_FINAL_W:ROW_FINAL_W + 1],
        jnp.zeros((SMALL_ROWS - ROW_FINAL_W - 1, D_MODEL), F32),
    ], axis=0)
    return dz, dw4, dw4_16, dblob4, small


ANY = pl.BlockSpec(memory_space=pl.ANY)
MESH = pl.DeviceIdType.MESH


def _place():
    x, y, c = lax.axis_index("x"), lax.axis_index("y"), lax.axis_index("c")
    chips = [(1 - x, y), (x, 1 - y), (1 - x, 1 - y)]
    return x, y, c, chips


def _gather_weights(win16, blob16, meta):
    hw, hb = win16.shape[0] // 2, blob16.shape[0] // 2

    def body(win_ref, blob_ref, meta_ref, w4_ref, b4_ref, m4_ref, send_sems, recv_sems, local_sems):
        x, y, c, chips = _place()
        me = 2 * x + y
        sibling = (x, y, 1 - c)

        local = [pltpu.make_async_copy(win_ref, w4_ref.at[me], local_sems.at[0]),
                 pltpu.make_async_copy(blob_ref, b4_ref.at[me], local_sems.at[1]),
                 pltpu.make_async_copy(meta_ref, m4_ref.at[me], local_sems.at[2])]
        for cp in local:
            cp.start()

        def half(ref4, chip, rows, which):
            return ref4.at[chip, pl.ds(which * rows, rows), :]

        def copy(k, src, dst, to):
            return pltpu.make_async_remote_copy(src_ref=src, dst_ref=dst, send_sem=send_sems.at[k],
                                                recv_sem=recv_sems.at[k], device_id=to, device_id_type=MESH)

        tensors = [(win_ref, w4_ref, hw), (blob_ref, b4_ref, hb)]
        sends = []
        for t, (src, dst4, rows) in enumerate(tensors):
            for j, (cx, cy) in enumerate(chips):
                sends.append(copy(6 * t + j, src.at[pl.ds(c * rows, rows), :], half(dst4, me, rows, c), (cx, cy, c)))
        for j, (cx, cy) in enumerate(chips):
            sends.append(copy(12 + j, meta_ref, m4_ref.at[me], (cx, cy, c)))
        for cp in sends:
            cp.start()

        passed = []
        for t, (src, dst4, rows) in enumerate(tensors):
            for j, (cx, cy) in enumerate(chips):
                landed = half(dst4, 2 * cx + cy, rows, c)
                copy(6 * t + j, landed, landed, (cx, cy, c)).wait_recv()
                fwd = copy(6 * t + 3 + j, landed, landed, sibling)
                fwd.start()
                passed.append(fwd)
        for t, (src, dst4, rows) in enumerate(tensors):
            for j, (cx, cy) in enumerate(chips):
                landed = half(dst4, 2 * cx + cy, rows, 1 - c)
                copy(6 * t + 3 + j, landed, landed, sibling).wait_recv()
        for j, (cx, cy) in enumerate(chips):
            landed = m4_ref.at[2 * cx + cy]
            copy(12 + j, landed, landed, (cx, cy, c)).wait_recv()
        for cp in sends + passed:
            cp.wait_send()
        for cp in local:
            cp.wait()

    return pl.pallas_call(
        body, name="gather_weights",
        in_specs=[ANY, ANY, ANY], out_specs=[ANY, ANY, ANY],
        out_shape=[jax.ShapeDtypeStruct((N_CHIPS,) + win16.shape, win16.dtype),
                   jax.ShapeDtypeStruct((N_CHIPS,) + blob16.shape, blob16.dtype),
                   jax.ShapeDtypeStruct((N_CHIPS,) + meta.shape, meta.dtype)],
        scratch_shapes=[pltpu.SemaphoreType.DMA((15,)), pltpu.SemaphoreType.DMA((15,)),
                        pltpu.SemaphoreType.DMA((3,))],
    )(win16, blob16, meta)


def _sibling_swap_halves(g4w, g4b):
    def body(w_ref, b_ref, rw_ref, rb_ref, send_sems, recv_sems):
        x, y, c, _ = _place()
        copies = []
        for k, (src, dst) in enumerate([(w_ref, rw_ref), (b_ref, rb_ref)]):
            rows = dst.shape[1]
            cp = pltpu.make_async_remote_copy(
                src_ref=src.at[pl.ds(0, N_CHIPS), pl.ds((1 - c) * rows, rows), :], dst_ref=dst,
                send_sem=send_sems.at[k], recv_sem=recv_sems.at[k],
                device_id=(x, y, 1 - c), device_id_type=MESH)
            cp.start()
            copies.append(cp)
        for cp in copies:
            cp.wait()

    half = lambda a: jax.ShapeDtypeStruct((a.shape[0], a.shape[1] // 2, a.shape[2]), a.dtype)
    return pl.pallas_call(
        body, name="sibling_swap_halves",
        in_specs=[ANY, ANY], out_specs=[ANY, ANY], out_shape=[half(g4w), half(g4b)],
        scratch_shapes=[pltpu.SemaphoreType.DMA((2,)), pltpu.SemaphoreType.DMA((2,))],
    )(g4w, g4b)


def _add_sibling(g4, rx, c_arr):
    n, rows, cols = rx.shape
    tm = _tile(rows, 256)

    def body(c_ref, g_ref, rx_ref, pf_ref, pb_ref):
        s = g_ref[0, 0] + rx_ref[0].astype(F32)
        pf_ref[0] = s
        pb_ref[0] = s.astype(BF16)

    blk = pl.BlockSpec((1, tm, cols), lambda k, i, c_ref: (k, i, 0))
    return pl.pallas_call(
        body, name="add_sibling",
        grid_spec=pltpu.PrefetchScalarGridSpec(
            num_scalar_prefetch=1, grid=(n, rows // tm),
            in_specs=[pl.BlockSpec((1, 1, tm, cols), lambda k, i, c_ref: (k, c_ref[0], i, 0)), blk],
            out_specs=[blk, blk]),
        out_shape=[jax.ShapeDtypeStruct(rx.shape, F32), jax.ShapeDtypeStruct(rx.shape, BF16)],
        compiler_params=_params(("arbitrary", "arbitrary")),
    )(c_arr, g4.reshape(n, 2, rows, cols), rx)


def _chip_exchange(pw16, pb16, small):
    def body(w_ref, b_ref, s_ref, rw_ref, rb_ref, rs_ref, send_sems, recv_sems, local_sem):
        x, y, c, chips = _place()
        own = pltpu.make_async_copy(s_ref, rs_ref.at[0], local_sem)
        own.start()
        copies = []
        for t, (src, dst) in enumerate([(w_ref, rw_ref), (b_ref, rb_ref)]):
            for j, (cx, cy) in enumerate(chips):
                copies.append(pltpu.make_async_remote_copy(
                    src_ref=src.at[2 * cx + cy], dst_ref=dst.at[j],
                    send_sem=send_sems.at[3 * t + j], recv_sem=recv_sems.at[3 * t + j],
                    device_id=(cx, cy, c), device_id_type=MESH))
        for r in range(1, N_DEV):
            fx, fy, fc = (r >> 2) & 1, (r >> 1) & 1, r & 1
            peer = (x ^ fx, y ^ fy, c ^ fc)
            copies.append(pltpu.make_async_remote_copy(
                src_ref=s_ref, dst_ref=rs_ref.at[r],
                send_sem=send_sems.at[5 + r], recv_sem=recv_sems.at[5 + r],
                device_id=peer, device_id_type=MESH))
        for cp in copies:
            cp.start()
        for cp in copies:
            cp.wait()
        own.wait()

    n_sem = 6 + N_DEV - 1
    return pl.pallas_call(
        body, name="chip_exchange",
        in_specs=[ANY, ANY, ANY], out_specs=[ANY, ANY, ANY],
        out_shape=[jax.ShapeDtypeStruct((3,) + pw16.shape[1:], BF16),
                   jax.ShapeDtypeStruct((3,) + pb16.shape[1:], BF16),
                   jax.ShapeDtypeStruct((N_DEV,) + small.shape, F32)],
        scratch_shapes=[pltpu.SemaphoreType.DMA((n_sem,)), pltpu.SemaphoreType.DMA((n_sem,)),
                        pltpu.SemaphoreType.DMA],
    )(pw16, pb16, small)


def _finish_half(pf, rx, chip_arr):
    _, rows, cols = rx.shape
    tm = _tile(rows, 256)

    def body(chip_ref, p_ref, rx_ref, out_ref):
        out_ref[...] = ((p_ref[0] + rx_ref[0].astype(F32)) + rx_ref[1].astype(F32)) + rx_ref[2].astype(F32)

    return pl.pallas_call(
        body, name="finish_half",
        grid_spec=pltpu.PrefetchScalarGridSpec(
            num_scalar_prefetch=1, grid=(rows // tm,),
            in_specs=[pl.BlockSpec((1, tm, cols), lambda i, chip_ref: (chip_ref[0], i, 0)),
                      pl.BlockSpec((3, tm, cols), lambda i, chip_ref: (0, i, 0))],
            out_specs=pl.BlockSpec((tm, cols), lambda i, chip_ref: (i, 0))),
        out_shape=jax.ShapeDtypeStruct((rows, cols), F32),
        compiler_params=_params(("arbitrary",)),
    )(chip_arr, pf, rx)


def _sum_small(slots, lb_logits, me_arr):
    def body(me_ref, slots_ref, lbl_ref, out_ref):
        me = me_ref[0]
        total = slots_ref[me]
        for d in range(1, N_DEV):
            total = total + slots_ref[d ^ me]
        out_ref[...] = total
        out_ref[ROW_LOSS:ROW_LOSS + 1, :] = jnp.broadcast_to(
            jnp.sum(total[ROW_LOSS:ROW_LOSS + 1, :], axis=-1, keepdims=True), (1, D_MODEL))
        lb = _lower_bound(lbl_ref[...])
        g0 = total[ROW_LB:ROW_LB + 1, :] * lb * (1.0 - lb)
        out_ref[ROW_LB:ROW_LB + 1, :] = g0
        out_ref[ROW_LB + 1:ROW_LB + 2, :] = -g0

    return pl.pallas_call(
        body, name="sum_small",
        grid_spec=pltpu.PrefetchScalarGridSpec(
            num_scalar_prefetch=1, grid=(1,),
            in_specs=[pl.BlockSpec((N_DEV, SMALL_ROWS, D_MODEL), lambda i, me_ref: (0, 0, 0)),
                      pl.BlockSpec((2, D_MODEL), lambda i, me_ref: (0, 0))],
            out_specs=pl.BlockSpec((SMALL_ROWS, D_MODEL), lambda i, me_ref: (0, 0))),
        out_shape=jax.ShapeDtypeStruct((SMALL_ROWS, D_MODEL), F32),
        compiler_params=_params(("arbitrary",)),
    )(me_arr, slots, lb_logits)


def _share_finished(fw, fb):
    def body(w_ref, b_ref, ow_ref, ob_ref, send_sems, recv_sems, local_sems):
        x, y, c, _ = _place()
        work = []
        for k, (src, dst) in enumerate([(w_ref, ow_ref), (b_ref, ob_ref)]):
            loc = pltpu.make_async_copy(src, dst.at[c], local_sems.at[k])
            rem = pltpu.make_async_remote_copy(
                src_ref=src, dst_ref=dst.at[c], send_sem=send_sems.at[k], recv_sem=recv_sems.at[k],
                device_id=(x, y, 1 - c), device_id_type=MESH)
            loc.start()
            rem.start()
            work.append((loc, rem, dst))
        for k, (loc, rem, dst) in enumerate(work):
            rem.wait_send()
            pltpu.make_async_remote_copy(
                src_ref=dst.at[1 - c], dst_ref=dst.at[1 - c], send_sem=send_sems.at[k],
                recv_sem=recv_sems.at[k], device_id=(x, y, 1 - c), device_id_type=MESH).wait_recv()
            loc.wait()

    both = lambda a: jax.ShapeDtypeStruct((2,) + a.shape, a.dtype)
    return pl.pallas_call(
        body, name="share_finished",
        in_specs=[ANY, ANY], out_specs=[ANY, ANY], out_shape=[both(fw), both(fb)],
        scratch_shapes=[pltpu.SemaphoreType.DMA((2,)), pltpu.SemaphoreType.DMA((2,)),
                        pltpu.SemaphoreType.DMA((2,))],
    )(fw, fb)


def _adamw(w, g, m, v):
    rows, cols = w.shape
    tm = _tile(rows, 256, mult=8) if rows % 8 == 0 else rows
    c1 = 1.0 / (1.0 - ADAM_B1 ** ADAM_STEP)
    c2 = 1.0 / (1.0 - ADAM_B2 ** ADAM_STEP)

    def body(w_ref, g_ref, m_ref, v_ref, d_ref, nm_ref, nv_ref):
        gt = g_ref[...]
        nm = ADAM_B1 * m_ref[...] + (1.0 - ADAM_B1) * gt
        nv = ADAM_B2 * v_ref[...] + (1.0 - ADAM_B2) * (gt * gt)
        nm_ref[...] = nm
        nv_ref[...] = nv
        d_ref[...] = -ADAM_LR * ((nm * c1) / (jnp.sqrt(nv * c2) + ADAM_EPS) + ADAM_WD * w_ref[...])

    blk = pl.BlockSpec((tm, cols), lambda i: (i, 0))
    sds = jax.ShapeDtypeStruct((rows, cols), F32)
    return pl.pallas_call(
        body, name="adamw",
        grid=(rows // tm,), in_specs=[blk] * 4, out_specs=[blk] * 3, out_shape=[sds] * 3,
        compiler_params=_params(("arbitrary",)),
    )(w, g, m, v)


def kernel(x, meta_tokens, norm_w, w_in, b_in, lb_logits, hg_norm_w, pool_w, pool_scale, w_down_hg, w_down_pool, w_out, final_norm_w, loss_target, m_meta_tokens, m_norm_w, m_w_in, m_b_in, m_lb_logits, m_hg_norm_w, m_pool_w, m_pool_scale, m_w_down_hg, m_w_down_pool, m_w_out, m_final_norm_w, v_meta_tokens, v_norm_w, v_w_in, v_b_in, v_lb_logits, v_hg_norm_w, v_pool_w, v_pool_scale, v_w_down_hg, v_w_down_pool, v_w_out, v_final_norm_w):
    seq = x.shape[1]
    xi, yi, ci = lax.axis_index("x"), lax.axis_index("y"), lax.axis_index("c")
    chip = 2 * xi + yi
    c_arr = jnp.reshape(ci, (1,)).astype(jnp.int32)
    chip_arr = jnp.reshape(chip, (1,)).astype(jnp.int32)
    me_arr = jnp.reshape(4 * xi + 2 * yi + ci, (1,)).astype(jnp.int32)
    q = D_MODEL // N_CHIPS

    def blob_of(wdh, wdp, wo, pw):
        return jnp.concatenate([wdh[0], wdp[0], wo[0], pw[0].reshape(-1, D_MODEL)], axis=0)

    w4, blob4, meta4 = _gather_weights(
        w_in[0].astype(BF16), blob_of(w_down_hg, w_down_pool, w_out, pool_w).astype(BF16), meta_tokens)
    meta_full = meta4.transpose(1, 0, 2).reshape(N_META, D_MODEL)

    z = jnp.concatenate([jnp.zeros((PAD_ROWS, D_MODEL), F32), meta_full, x[0]], axis=0)
    tgt = jnp.concatenate([jnp.zeros((FIRST_TOKEN_ROW, D_MODEL), F32), loss_target[0]], axis=0)
    fw2 = final_norm_w.reshape(1, D_MODEL)
    dz, dw4, dw4_16, dblob4, small = _local_step(
        z, tgt, w4, blob4, norm_w, b_in, lb_logits, hg_norm_w, pool_scale, fw2)
    grad_x = dz[FIRST_TOKEN_ROW:][None]

    rx_w, rx_b = _sibling_swap_halves(dw4_16, dblob4.astype(BF16))
    pw_f, pw_16 = _add_sibling(dw4, rx_w, c_arr)
    pb_f, pb_16 = _add_sibling(dblob4, rx_b, c_arr)
    land_w, land_b, slots = _chip_exchange(pw_16, pb_16, small)
    fin_w = _finish_half(pw_f, land_w, chip_arr)
    fin_b = _finish_half(pb_f, land_b, chip_arr)
    tot = _sum_small(slots, lb_logits, me_arr)
    gw2, gb2 = _share_finished(fin_w, fin_b)
    g_w_in = gw2.reshape(D_MODEL, 2 * D_MODEL)
    g_blob = gb2.reshape(-1, D_MODEL)

    d_win, nm_win, nv_win = _adamw(w_in[0], g_w_in, m_w_in[0], v_w_in[0])
    d_blob, nm_blob, nv_blob = _adamw(
        blob_of(w_down_hg, w_down_pool, w_out, pool_w), g_blob,
        blob_of(m_w_down_hg, m_w_down_pool, m_w_out, m_pool_w),
        blob_of(v_w_down_hg, v_w_down_pool, v_w_out, v_pool_w))
    g_meta = lax.dynamic_slice_in_dim(tot[ROW_META:ROW_META + N_META], chip * q, q, axis=1)
    d_meta, nm_meta, nv_meta = _adamw(meta_tokens, g_meta, m_meta_tokens, v_meta_tokens)

    def rows_of(nw, bi, lbl, hg, ps, fw):
        return jnp.concatenate([nw, bi.reshape(N_SEG, D_MODEL), lbl, hg, ps, fw.reshape(1, D_MODEL),
                                jnp.zeros((2, D_MODEL), F32)], axis=0)

    g_rows = jnp.concatenate([tot[ROW_NORM_W:ROW_FINAL_W + 1], jnp.zeros((2, D_MODEL), F32)], axis=0)
    d_rows, nm_rows, nv_rows = _adamw(
        rows_of(norm_w, b_in, lb_logits, hg_norm_w, pool_scale, final_norm_w), g_rows,
        rows_of(m_norm_w, m_b_in, m_lb_logits, m_hg_norm_w, m_pool_scale, m_final_norm_w),
        rows_of(v_norm_w, v_b_in, v_lb_logits, v_hg_norm_w, v_pool_scale, v_final_norm_w))

    def unblob(b):
        return (b[0:q][None], b[q:2 * q][None], b[2 * q:3 * q][None], b[3 * q:].reshape(pool_w.shape))

    def unrows(r):
        o = ROW_NORM_W
        return dict(norm_w=r[ROW_NORM_W - o:ROW_B_IN - o], b_in=r[ROW_B_IN - o:ROW_LB - o].reshape(1, -1),
                    lb_logits=r[ROW_LB - o:ROW_HG_W - o], hg_norm_w=r[ROW_HG_W - o:ROW_POOL_SCALE - o],
                    pool_scale=r[ROW_POOL_SCALE - o:ROW_FINAL_W - o], final_norm_w=r[ROW_FINAL_W - o])

    def leaves(meta_part, rows_part, win_part, blob_part):
        r = unrows(rows_part)
        wdh, wdp, wo, pw = unblob(blob_part)
        return [meta_part, r["norm_w"], win_part[None], r["b_in"], r["lb_logits"], r["hg_norm_w"], pw,
                r["pool_scale"], wdh, wdp, wo, r["final_norm_w"]]

    loss = tot[ROW_LOSS, 0]
    return (loss, grad_x,
            *leaves(g_meta, g_rows, g_w_in, g_blob),
            *leaves(d_meta, d_rows, d_win, d_blob),
            *leaves(nm_meta, nm_rows, nm_win, nm_blob),
            *leaves(nv_meta, nv_rows, nv_win, nv_blob))
```

```python
import functools

import numpy as np
import jax
import jax.numpy as jnp
from jax import lax
from jax.experimental import pallas as pl
from jax.experimental.pallas import tpu as pltpu

F32 = jnp.float32
BF16 = jnp.bfloat16

D_MODEL = 1024
N_SEG = 8
N_HEADS = 8
HEAD_DIM = 128
CHUNK = 64
N_META = 16
PAD_ROWS = CHUNK - N_META
FIRST_TOKEN_ROW = CHUNK
LEVELS = (32, 16, 8, 4, 2, 1)
N_EXP = 2 + len(LEVELS)
POOL_WINDOWS = (2, 4, 8, 16)
POOL_GDIM = D_MODEL // len(POOL_WINDOWS)
HALO = 16
HEADS_PER_STEP = 4
EPS = 1e-6
N_CHIPS = 4
N_DEV = 8
SEGS_REC = (0, 1, 2)
SEGS_MIX = (3, 4, 5, 6, 7)

ADAM_LR = 0.001
ADAM_B1 = 0.9
ADAM_B2 = 0.999
ADAM_EPS = 1e-08
ADAM_WD = 0.01
ADAM_STEP = 10

VMEM_LIMIT_BYTES = 56 * 1024 * 1024

ROW_LOSS = 0
ROW_META = 1
ROW_NORM_W = ROW_META + N_META
ROW_B_IN = ROW_NORM_W + 1
ROW_LB = ROW_B_IN + N_SEG
ROW_HG_W = ROW_LB + 2
ROW_POOL_SCALE = ROW_HG_W + 1
ROW_FINAL_W = ROW_POOL_SCALE + 1
SMALL_ROWS = 32


def _tile(total, cap, mult=16):
    best = None
    for t in range(mult, min(total, cap) + 1, mult):
        if total % t == 0:
            best = t
    assert best is not None, (total, cap, mult)
    return best


def _params(sem=None):
    return pltpu.CompilerParams(dimension_semantics=sem, vmem_limit_bytes=VMEM_LIMIT_BYTES)


def _dot(a, b):
    return jnp.dot(a, b, preferred_element_type=F32)


def _dot_nt(a, b):
    return lax.dot_general(a, b, (((1,), (1,)), ((), ())), preferred_element_type=F32)


def _dot_tn(a, b):
    return lax.dot_general(a, b, (((0,), (0,)), ((), ())), preferred_element_type=F32)


def _sigmoid_pair(x):
    t = jnp.exp(-jnp.abs(x))
    r = 1.0 / (1.0 + t)
    pos = x >= 0
    return jnp.where(pos, r, t * r), jnp.where(pos, t * r, r)


def _exponent_matrix():
    t = np.arange(CHUNK)[:, None]
    j = np.arange(CHUNK)[None, :]
    blocks = [j <= t, j > t]
    for m in LEVELS:
        rho = (t // (2 * m)) * (2 * m) + m
        upper = (t >= rho) & (j > rho) & (j <= t)
        lower = (t < rho) & (j > t) & (j <= rho)
        blocks.append(upper | lower)
    return np.concatenate(blocks, axis=0).astype(np.float32)


def _pair_masks():
    t = np.arange(CHUNK)[:, None]
    s = np.arange(CHUNK)[None, :]
    masks = [t == s]
    for m in LEVELS:
        same = (t // (2 * m)) == (s // (2 * m))
        masks.append(same & ((t % (2 * m)) >= m) & ((s % (2 * m)) < m))
    return np.stack(masks).astype(np.float32)


def _split3(x):
    hi = x.astype(BF16)
    r = x - hi.astype(F32)
    mid = r.astype(BF16)
    lo = (r - mid.astype(F32)).astype(BF16)
    return hi, mid, lo


def _chunk_forward(q, fz, lb, valid, wexp, masks):
    sg, sn = _sigmoid_pair(fz)
    f = lb + (1.0 - lb) * sg
    g = jnp.where(valid, jnp.log(f), 0.0)
    kk = jnp.where(valid, (1.0 - lb) * sn, 0.0)
    q = jnp.where(valid, q, 0.0)
    e = jnp.exp(_dot(wexp, jnp.concatenate(_split3(g), axis=0)))
    e_b = e[0:CHUNK]
    e_c = e[CHUNK:2 * CHUNK]
    a = masks[0] * _dot_nt(q.astype(BF16), kk.astype(BF16))
    qm, km = [], []
    for l in range(len(LEVELS)):
        e_m = e[(2 + l) * CHUNK:(3 + l) * CHUNK]
        qm.append(q * e_m)
        km.append(kk * e_m)
        a = a + masks[1 + l] * _dot_nt(qm[l].astype(BF16), km[l].astype(BF16))
    return dict(sg=sg, sn=sn, f=f, kk=kk, q=q, e=e, e_b=e_b, e_c=e_c, a=a, qm=qm, km=km)


def _lower_bound(lbl):
    return 1.0 / (1.0 + jnp.exp(lbl[1:2, :] - lbl[0:1, :]))


def _in_proj(z, norm_w, w4, b_in, rows):
    tm = _tile(rows, 1040)

    def body(z_ref, nw_ref, w_ref, b_ref, h_ref, p_ref):
        @pl.when(pl.program_id(1) == 0)
        def _():
            zt = z_ref[...]
            rstd = lax.rsqrt(jnp.mean(zt * zt, axis=-1, keepdims=True) + EPS)
            h_ref[...] = (zt * rstd * nw_ref[...]).astype(BF16)

        p_ref[0] = _dot(h_ref[...], w_ref[0]) + b_ref[...]

    return pl.pallas_call(
        body, name="in_proj",
        grid=(rows // tm, N_SEG),
        in_specs=[
            pl.BlockSpec((tm, D_MODEL), lambda i, k: (i, 0)),
            pl.BlockSpec((1, D_MODEL), lambda i, k: (0, 0)),
            pl.BlockSpec((1, D_MODEL, D_MODEL), lambda i, k: (k // 2, 0, k % 2)),
            pl.BlockSpec((1, D_MODEL), lambda i, k: (0, k)),
        ],
        out_specs=[
            pl.BlockSpec((tm, D_MODEL), lambda i, k: (i, 0)),
            pl.BlockSpec((1, tm, D_MODEL), lambda i, k: (k, i, 0)),
        ],
        out_shape=[
            jax.ShapeDtypeStruct((rows, D_MODEL), BF16),
            jax.ShapeDtypeStruct((N_SEG, rows, D_MODEL), F32),
        ],
        compiler_params=_params(("arbitrary", "arbitrary")),
    )(z, norm_w, w4, b_in)


def _hgrn_forward(p3, lb_logits, wexp3, masks, rows):
    n_chunks = rows // CHUNK
    cpb = _tile(n_chunks, 13, mult=1)
    rb_rows = cpb * CHUNK
    hps = HEADS_PER_STEP
    width = hps * HEAD_DIM

    def body(q_ref, fz_ref, v_ref, lbl_ref, wexp_ref, mask_ref, o_ref, s_ref, st_ref):
        rb = pl.program_id(1)

        @pl.when(rb == 0)
        def _():
            st_ref[...] = jnp.zeros_like(st_ref)

        lb_all = _lower_bound(lbl_ref[...])
        wexp = wexp_ref[...]
        masks = mask_ref[...]

        def chunk(c, carry):
            r0 = pl.multiple_of(c * CHUNK, CHUNK)
            row = rb * rb_rows + r0 + lax.broadcasted_iota(jnp.int32, (CHUNK, 1), 0)
            valid = row >= PAD_ROWS
            q_all = q_ref[0, pl.ds(r0, CHUNK), :]
            fz_all = fz_ref[0, pl.ds(r0, CHUNK), :]
            v_all = jnp.where(valid, v_ref[0, pl.ds(r0, CHUNK), :], 0.0).astype(BF16)
            st_all = [st_ref[j] for j in range(hps)]
            o_all, st_new = [], []
            for j in range(hps):
                cols = slice(j * HEAD_DIM, (j + 1) * HEAD_DIM)
                cf = _chunk_forward(q_all[:, cols], fz_all[:, cols], lb_all[:, cols], valid, wexp, masks)
                v16 = v_all[:, cols]
                o = _dot_nt((cf["q"] * cf["e_b"]).astype(BF16), st_all[j].astype(BF16))
                o_all.append(o + _dot(cf["a"].astype(BF16), v16))
                kc16 = (cf["kk"] * cf["e_c"]).astype(BF16)
                st_new.append(st_all[j] * cf["e_b"][CHUNK - 1:CHUNK, :] + _dot_tn(v16, kc16))
            o_ref[pl.ds(r0, CHUNK), :] = jnp.concatenate(o_all, axis=1)
            for j in range(hps):
                s_ref[j, c] = st_all[j]
                st_ref[j] = st_new[j]
            return carry

        lax.fori_loop(0, cpb, chunk, 0)

    head_block = lambda seg: pl.BlockSpec((1, rb_rows, width), lambda h, r: (seg, r, h))
    return pl.pallas_call(
        body, name="hgrn_forward",
        grid=(N_HEADS // hps, n_chunks // cpb),
        in_specs=[
            head_block(0), head_block(1), head_block(2),
            pl.BlockSpec((2, width), lambda h, r: (0, h)),
            pl.BlockSpec((N_EXP * CHUNK, 3 * CHUNK), lambda h, r: (0, 0)),
            pl.BlockSpec((1 + len(LEVELS), CHUNK, CHUNK), lambda h, r: (0, 0, 0)),
        ],
        out_specs=[
            pl.BlockSpec((rb_rows, width), lambda h, r: (r, h)),
            pl.BlockSpec((hps, cpb, HEAD_DIM, HEAD_DIM), lambda h, r: (h, r, 0, 0)),
        ],
        out_shape=[
            jax.ShapeDtypeStruct((rows, D_MODEL), F32),
            jax.ShapeDtypeStruct((N_HEADS, n_chunks, HEAD_DIM, HEAD_DIM), F32),
        ],
        scratch_shapes=[pltpu.VMEM((hps, HEAD_DIM, HEAD_DIM), F32)],
        compiler_params=_params(("arbitrary", "arbitrary")),
    )(p3, p3, p3, lb_logits, wexp3, masks)


def _hgrn_backward(p3, d_o, states, lb_logits, wexp3, wexp_t2, masks, dw16, blob16, rows):
    n_chunks = rows // CHUNK
    cpb = _tile(n_chunks, 13, mult=1)
    rb_rows = cpb * CHUNK
    n_rb = n_chunks // cpb
    hps = HEADS_PER_STEP
    width = hps * HEAD_DIM
    n_hb = N_HEADS // hps
    exchange = _GradExchange(SEGS_MIX, with_blob=True)

    def body(q_ref, fz_ref, v_ref, do_ref, s_ref, lbl_ref, wexp_ref, wexpt_ref, mask_ref, dw_ref, blob_ref,
             dp_ref, dlb_ref, rxw_ref, rxb_ref, dst_ref, send_sems, recv_sems):
        step = pl.program_id(1)
        rb = n_rb - 1 - step

        @pl.when((pl.program_id(0) == 0) & (step == 0))
        def _():
            exchange.start(dw_ref, rxw_ref, blob_ref, rxb_ref, send_sems, recv_sems)

        @pl.when(step == 0)
        def _():
            dst_ref[...] = jnp.zeros_like(dst_ref)
            dlb_ref[...] = jnp.zeros_like(dlb_ref)

        lb_all = _lower_bound(lbl_ref[...])
        wexp = wexp_ref[...]
        wexp_t = wexpt_ref[...]
        masks = mask_ref[...]
        last_row = lax.broadcasted_iota(jnp.int32, (CHUNK, 1), 0) == CHUNK - 1

        def one_head(j, c, r0, valid):
            cols = slice(j * HEAD_DIM, (j + 1) * HEAD_DIM)
            lb = lb_all[:, cols]
            cf = _chunk_forward(q_ref[0, pl.ds(r0, CHUNK), cols], fz_ref[0, pl.ds(r0, CHUNK), cols],
                                lb, valid, wexp, masks)
            q, kk, e_b, e_c = cf["q"], cf["kk"], cf["e_b"], cf["e_c"]
            v16 = jnp.where(valid, v_ref[0, pl.ds(r0, CHUNK), cols], 0.0).astype(BF16)
            do16 = do_ref[pl.ds(r0, CHUNK), cols].astype(BF16)
            st = s_ref[j, c]
            dst = dst_ref[j]
            dst16 = dst.astype(BF16)
            qb = q * e_b
            kc = kk * e_c
            q16, kk16 = q.astype(BF16), kk.astype(BF16)

            dv = _dot_tn(cf["a"].astype(BF16), do16) + _dot_nt(kc.astype(BF16), dst16)
            da = _dot_nt(do16, v16)
            dqb = _dot(do16, st.astype(BF16))
            dkc = _dot(v16, dst16)
            e_last = e_b[CHUNK - 1:CHUNK, :]
            de = jnp.sum(dst * st, axis=0, keepdims=True)
            dst_ref[j] = dst * e_last + _dot_tn(do16, qb.astype(BF16))

            dq = e_b * dqb
            dkk = e_c * dkc
            dx = [qb * dqb + jnp.where(last_row, de * e_last, 0.0), kc * dkc]
            dm0 = (masks[0] * da).astype(BF16)
            dq = dq + _dot(dm0, kk16)
            dkk = dkk + _dot(dm0, q16)
            for l in range(len(LEVELS)):
                e_m = cf["e"][(2 + l) * CHUNK:(3 + l) * CHUNK]
                dm = (masks[1 + l] * da).astype(BF16)
                dqm = _dot(dm, cf["km"][l].astype(BF16))
                dkm = _dot_tn(dm, cf["qm"][l].astype(BF16))
                dq = dq + e_m * dqm
                dkk = dkk + e_m * dkm
                dx.append(cf["qm"][l] * dqm + cf["km"][l] * dkm)
            dxa = jnp.concatenate(dx, axis=0)
            hi = dxa.astype(BF16)
            mid = (dxa - hi.astype(F32)).astype(BF16)
            dg = _dot(wexp_t, jnp.concatenate([hi, mid], axis=0))

            t = jnp.where(valid, dg / cf["f"] - dkk, 0.0)
            dfz = (1.0 - lb) * cf["sg"] * cf["sn"] * t
            dlb_ref[:, cols] += jnp.sum(cf["sn"] * t, axis=0, keepdims=True)
            dp_ref[0, pl.ds(r0, CHUNK), cols] = jnp.where(valid, dq, 0.0).astype(BF16)
            dp_ref[1, pl.ds(r0, CHUNK), cols] = dfz.astype(BF16)
            dp_ref[2, pl.ds(r0, CHUNK), cols] = jnp.where(valid, dv, 0.0).astype(BF16)

        def chunk(i, carry):
            c = cpb - 1 - i
            r0 = pl.multiple_of(c * CHUNK, CHUNK)
            row = rb * rb_rows + r0 + lax.broadcasted_iota(jnp.int32, (CHUNK, 1), 0)
            for j in range(hps):
                one_head(j, c, r0, row >= PAD_ROWS)
            return carry

        lax.fori_loop(0, cpb, chunk, 0)

        @pl.when((pl.program_id(0) == n_hb - 1) & (step == n_rb - 1))
        def _():
            exchange.wait(dw_ref, rxw_ref, blob_ref, rxb_ref, send_sems, recv_sems)

    head_block = lambda seg: pl.BlockSpec((1, rb_rows, width), lambda h, s: (seg, n_rb - 1 - s, h))
    return pl.pallas_call(
        body, name="hgrn_backward",
        grid=(n_hb, n_rb),
        in_specs=[
            head_block(0), head_block(1), head_block(2),
            pl.BlockSpec((rb_rows, width), lambda h, s: (n_rb - 1 - s, h)),
            pl.BlockSpec((hps, cpb, HEAD_DIM, HEAD_DIM), lambda h, s: (h, n_rb - 1 - s, 0, 0)),
            pl.BlockSpec((2, width), lambda h, s: (0, h)),
            pl.BlockSpec((N_EXP * CHUNK, 3 * CHUNK), lambda h, s: (0, 0)),
            pl.BlockSpec((CHUNK, 2 * N_EXP * CHUNK), lambda h, s: (0, 0)),
            pl.BlockSpec((1 + len(LEVELS), CHUNK, CHUNK), lambda h, s: (0, 0, 0)),
            ANY, ANY,
        ],
        out_specs=[
            pl.BlockSpec((3, rb_rows, width), lambda h, s: (0, n_rb - 1 - s, h)),
            pl.BlockSpec((1, width), lambda h, s: (0, h)),
            ANY, ANY,
        ],
        out_shape=[
            jax.ShapeDtypeStruct((3, rows, D_MODEL), BF16),
            jax.ShapeDtypeStruct((1, D_MODEL), F32),
            exchange.landing_w(), exchange.landing_blob(blob16),
        ],
        scratch_shapes=[pltpu.VMEM((hps, HEAD_DIM, HEAD_DIM), F32)] + exchange.semaphores(),
        compiler_params=_params(("arbitrary", "arbitrary")),
    )(p3, p3, p3, d_o, states, lb_logits, wexp3, wexp_t2, masks, dw16, blob16)


def _silu_and_grad(x):
    s, _ = _sigmoid_pair(x)
    return x * s, s * (1.0 + x * (1.0 - s))


def _window_sum(ext, width, forward_looking):
    n = ext.shape[0]
    s = ext
    step = 1
    while step < width:
        s = s + pltpu.roll(s, (n - step) if forward_looking else step, 0)
        step *= 2
    return s


def _mixers(o, p3, z, tgt, wdh, wdp, wout, poolw, hg_w, pool_scale, final_w, rows):
    tm = _tile(rows, 160)
    nt = rows // tm
    halo_blocks = tm // HALO
    n_grp = len(POOL_WINDOWS)

    def body(o_ref, ghg_ref, u_ref, gpl_ref, mhg_ref, mpl_ref, uh_ref, z_ref, t_ref,
             wdh_ref, wdp_ref, wout_ref, pw_ref, hgw_ref, ps_ref, fw_ref,
             do_ref, dz2_ref, dp_ref, dwdh_ref, dwdp_ref, dwout_ref, dpw_ref, small_ref, carry_ref):
        step = pl.program_id(0)
        tile = nt - 1 - step

        @pl.when(step == 0)
        def _():
            dwdh_ref[...] = jnp.zeros_like(dwdh_ref)
            dwdp_ref[...] = jnp.zeros_like(dwdp_ref)
            dwout_ref[...] = jnp.zeros_like(dwout_ref)
            dpw_ref[...] = jnp.zeros_like(dpw_ref)
            small_ref[...] = jnp.zeros_like(small_ref)
            carry_ref[...] = jnp.zeros_like(carry_ref)

        row = tile * tm + lax.broadcasted_iota(jnp.int32, (tm, 1), 0)
        real = row >= PAD_ROWS
        pos1 = jnp.maximum(row - PAD_ROWS + 1, 1).astype(F32)

        u = jnp.where(real, u_ref[0], 0.0)
        halo_row = tile * tm - HALO + lax.broadcasted_iota(jnp.int32, (HALO, 1), 0)
        uh = jnp.where(halo_row >= PAD_ROWS, uh_ref[0], 0.0)
        ext = jnp.concatenate([uh, u], axis=0)
        pooled, inv_cnt, mixed = [], [], []
        for g, w in enumerate(POOL_WINDOWS):
            cols = slice(g * POOL_GDIM, (g + 1) * POOL_GDIM)
            inv = 1.0 / jnp.minimum(pos1, float(w))
            ws = _window_sum(ext[:, cols], w, False)[HALO:]
            pg = (ws * inv - u[:, cols]).astype(BF16)
            pooled.append(pg)
            inv_cnt.append(inv)
            mixed.append(_dot(pg, pw_ref[g]))
        mixed = jnp.concatenate(mixed, axis=1)
        gpl = gpl_ref[0]
        sp, dsp = _silu_and_grad(gpl)
        ps = ps_ref[...]
        a_pool = (mixed * ps * sp).astype(BF16)
        y_pool = _dot(a_pool, wdp_ref[...])

        o = o_ref[...]
        o_hat, rstd_h = [], []
        for h in range(N_HEADS):
            oh = o[:, h * HEAD_DIM:(h + 1) * HEAD_DIM]
            r = lax.rsqrt(jnp.mean(oh * oh, axis=-1, keepdims=True) + EPS)
            rstd_h.append(r)
            o_hat.append(oh * r)
        o_hat = jnp.concatenate(o_hat, axis=1)
        hgw = hgw_ref[...]
        o_n = o_hat * hgw
        ghg = ghg_ref[0]
        sh, dsh = _silu_and_grad(ghg)
        a_hg = (o_n * sh).astype(BF16)
        y_hg = _dot(a_hg, wdh_ref[...])

        s_mh, _ = _sigmoid_pair(mhg_ref[0])
        s_mp, _ = _sigmoid_pair(mpl_ref[0])
        merged = (s_mh * y_hg + s_mp * y_pool).astype(BF16)
        z2 = z_ref[...] + _dot(merged, wout_ref[...])
        rstd2 = lax.rsqrt(jnp.mean(z2 * z2, axis=-1, keepdims=True) + EPS)
        zh = z2 * rstd2
        fw = fw_ref[...]
        err = jnp.where(row >= FIRST_TOKEN_ROW, zh * fw - t_ref[...], 0.0)
        small_ref[ROW_LOSS:ROW_LOSS + 1, :] += jnp.sum(err * err, axis=0, keepdims=True) * (0.5 / D_MODEL)
        dy = err * (1.0 / D_MODEL)

        small_ref[ROW_FINAL_W:ROW_FINAL_W + 1, :] += jnp.sum(dy * zh, axis=0, keepdims=True)
        uu = dy * fw
        dz2 = rstd2 * (uu - zh * jnp.mean(uu * zh, axis=-1, keepdims=True))
        dz2_ref[...] = dz2
        dz2_16 = dz2.astype(BF16)
        dmerged = _dot_nt(dz2_16, wout_ref[...])
        dwout_ref[...] += _dot_tn(merged, dz2_16)
        dy_hg = (s_mh * dmerged).astype(BF16)
        dy_pool = (s_mp * dmerged).astype(BF16)
        dp_ref[3] = (dmerged * y_hg * s_mh * (1.0 - s_mh)).astype(BF16)
        dp_ref[4] = (dmerged * y_pool * s_mp * (1.0 - s_mp)).astype(BF16)

        da_hg = _dot_nt(dy_hg, wdh_ref[...])
        dwdh_ref[...] += _dot_tn(a_hg, dy_hg)
        dp_ref[0] = (da_hg * o_n * dsh).astype(BF16)
        do_n = da_hg * sh
        small_ref[ROW_HG_W:ROW_HG_W + 1, :] += jnp.sum(do_n * o_hat, axis=0, keepdims=True)
        d_hat = do_n * hgw
        for h in range(N_HEADS):
            cols = slice(h * HEAD_DIM, (h + 1) * HEAD_DIM)
            dh_, oh_ = d_hat[:, cols], o_hat[:, cols]
            do_ref[:, cols] = rstd_h[h] * (dh_ - oh_ * jnp.mean(dh_ * oh_, axis=-1, keepdims=True))

        da_pool = _dot_nt(dy_pool, wdp_ref[...])
        dwdp_ref[...] += _dot_tn(a_pool, dy_pool)
        small_ref[ROW_POOL_SCALE:ROW_POOL_SCALE + 1, :] += jnp.sum(da_pool * mixed * sp, axis=0, keepdims=True)
        dp_ref[2] = (da_pool * mixed * ps * dsp).astype(BF16)
        dmixed = (da_pool * ps * sp).astype(BF16)
        carry = carry_ref[...]
        du, new_carry = [], []
        for g, w in enumerate(POOL_WINDOWS):
            cols = slice(g * POOL_GDIM, (g + 1) * POOL_GDIM)
            dmg = dmixed[:, cols]
            dpooled = _dot_nt(dmg, pw_ref[g])
            dpw_ref[g] += _dot_tn(pooled[g], dmg)
            dps = dpooled * inv_cnt[g]
            ext_b = jnp.concatenate([dps, carry[:, cols]], axis=0)
            du.append(_window_sum(ext_b, w, True)[:tm] - dpooled)
            new_carry.append(dps[:HALO])
        dp_ref[1] = jnp.where(real, jnp.concatenate(du, axis=1), 0.0).astype(BF16)
        carry_ref[...] = jnp.concatenate(new_carry, axis=1)

    row_block = pl.BlockSpec((tm, D_MODEL), lambda s: (nt - 1 - s, 0))
    seg_block = lambda seg: pl.BlockSpec((1, tm, D_MODEL), lambda s: (seg, nt - 1 - s, 0))
    whole = pl.BlockSpec(memory_space=pltpu.VMEM)
    return pl.pallas_call(
        body, name="mixers",
        grid=(nt,),
        in_specs=[
            row_block, seg_block(3), seg_block(4), seg_block(5), seg_block(6), seg_block(7),
            pl.BlockSpec((1, HALO, D_MODEL),
                         lambda s: (4, jnp.maximum((nt - 1 - s) * halo_blocks - 1, 0), 0)),
            row_block, row_block,
            whole, whole, whole, whole, whole, whole, whole,
        ],
        out_specs=[
            row_block, row_block,
            pl.BlockSpec((5, tm, D_MODEL), lambda s: (0, nt - 1 - s, 0)),
            whole, whole, whole, whole, whole,
        ],
        out_shape=[
            jax.ShapeDtypeStruct((rows, D_MODEL), F32),
            jax.ShapeDtypeStruct((rows, D_MODEL), F32),
            jax.ShapeDtypeStruct((5, rows, D_MODEL), BF16),
            jax.ShapeDtypeStruct((D_MODEL, D_MODEL), F32),
            jax.ShapeDtypeStruct((D_MODEL, D_MODEL), F32),
            jax.ShapeDtypeStruct((D_MODEL, D_MODEL), F32),
            jax.ShapeDtypeStruct((n_grp, POOL_GDIM, POOL_GDIM), F32),
            jax.ShapeDtypeStruct((SMALL_ROWS, D_MODEL), F32),
        ],
        scratch_shapes=[pltpu.VMEM((HALO, D_MODEL), F32)],
        compiler_params=_params(("arbitrary",)),
    )(o, p3, p3, p3, p3, p3, p3, z, tgt, wdh, wdp, wout, poolw, hg_w, pool_scale, final_w)


def _seg_specs(tm, row_of, seg_of):
    def spec_a(*g):
        k = seg_of(*g)
        return (jnp.minimum(k, 2), jnp.where(k < 3, row_of(*g), 0), 0)

    def spec_b(*g):
        k = seg_of(*g)
        return (jnp.maximum(k - 3, 0), jnp.where(k >= 3, row_of(*g), 0), 0)

    return pl.BlockSpec((1, tm, D_MODEL), spec_a), pl.BlockSpec((1, tm, D_MODEL), spec_b)


def _in_proj_weight_grad(h, dp, rows, name):
    n_seg = dp.shape[0]
    tm = _tile(rows, 1040)
    nt = rows // tm

    def body(h_ref, dp_ref, dw_ref, dw16_ref, db_ref, acc_ref, bacc_ref):
        i = pl.program_id(1)

        @pl.when(i == 0)
        def _():
            acc_ref[...] = jnp.zeros_like(acc_ref)
            bacc_ref[...] = jnp.zeros_like(bacc_ref)

        dpt = dp_ref[0]
        acc_ref[...] += _dot_tn(h_ref[...], dpt)
        bacc_ref[...] += jnp.sum(dpt.astype(F32), axis=0, keepdims=True)

        @pl.when(i == nt - 1)
        def _():
            dw_ref[0] = acc_ref[...]
            dw16_ref[0] = acc_ref[...].astype(BF16)
            db_ref[0] = bacc_ref[...]

    w_block = pl.BlockSpec((1, D_MODEL, D_MODEL), lambda k, i: (k, 0, 0))
    return pl.pallas_call(
        body, name=name,
        grid=(n_seg, nt),
        in_specs=[pl.BlockSpec((tm, D_MODEL), lambda k, i: (i, 0)),
                  pl.BlockSpec((1, tm, D_MODEL), lambda k, i: (k, i, 0))],
        out_specs=[w_block, w_block, pl.BlockSpec((1, 1, D_MODEL), lambda k, i: (k, 0, 0))],
        out_shape=[
            jax.ShapeDtypeStruct((n_seg, D_MODEL, D_MODEL), F32),
            jax.ShapeDtypeStruct((n_seg, D_MODEL, D_MODEL), BF16),
            jax.ShapeDtypeStruct((n_seg, 1, D_MODEL), F32),
        ],
        scratch_shapes=[pltpu.VMEM((D_MODEL, D_MODEL), F32), pltpu.VMEM((1, D_MODEL), F32)],
        compiler_params=_params(("arbitrary", "arbitrary")),
    )(h, dp)


def _input_grad(dpa, dpb, w4, z, dz2, norm_w, dw16, rows):
    tm = _tile(rows, 1040)
    nt = rows // tm
    exchange = _GradExchange(SEGS_REC, with_blob=False)

    def body(dpa_ref, dpb_ref, w_ref, z_ref, dz2_ref, nw_ref, dw_ref, dz_ref, dnw_ref, rxw_ref,
             acc_ref, send_sems, recv_sems):
        i, k = pl.program_id(0), pl.program_id(1)

        @pl.when((i == 0) & (k == 0))
        def _():
            exchange.start(dw_ref, rxw_ref, None, None, send_sems, recv_sems)
            dnw_ref[...] = jnp.zeros_like(dnw_ref)

        @pl.when((i == nt - 1) & (k == N_SEG - 1))
        def _():
            exchange.wait(dw_ref, rxw_ref, None, None, send_sems, recv_sems)

        @pl.when(k == 0)
        def _():
            acc_ref[...] = jnp.zeros_like(acc_ref)

        @pl.when(k < 3)
        def _():
            acc_ref[...] += _dot_nt(dpa_ref[0], w_ref[0])

        @pl.when(k >= 3)
        def _():
            acc_ref[...] += _dot_nt(dpb_ref[0], w_ref[0])

        @pl.when(k == N_SEG - 1)
        def _():
            zt = z_ref[...]
            rstd = lax.rsqrt(jnp.mean(zt * zt, axis=-1, keepdims=True) + EPS)
            zh = zt * rstd
            dh = acc_ref[...]
            dnw_ref[...] += jnp.sum(dh * zh, axis=0, keepdims=True)
            uu = dh * nw_ref[...]
            dz_ref[...] = dz2_ref[...] + rstd * (uu - zh * jnp.mean(uu * zh, axis=-1, keepdims=True))

    spec_a, spec_b = _seg_specs(tm, lambda i, k: i, lambda i, k: k)
    last_only = pl.BlockSpec((tm, D_MODEL), lambda i, k: (jnp.where(k == N_SEG - 1, i, 0), 0))
    return pl.pallas_call(
        body, name="input_grad",
        grid=(nt, N_SEG),
        in_specs=[
            spec_a, spec_b,
            pl.BlockSpec((1, D_MODEL, D_MODEL), lambda i, k: (k // 2, 0, k % 2)),
            last_only, last_only,
            pl.BlockSpec((1, D_MODEL), lambda i, k: (0, 0)),
            ANY,
        ],
        out_specs=[
            pl.BlockSpec((tm, D_MODEL), lambda i, k: (i, 0)),
            pl.BlockSpec((1, D_MODEL), lambda i, k: (0, 0)),
            ANY,
        ],
        out_shape=[
            jax.ShapeDtypeStruct((rows, D_MODEL), F32),
            jax.ShapeDtypeStruct((1, D_MODEL), F32),
            exchange.landing_w(),
        ],
        scratch_shapes=[pltpu.VMEM((tm, D_MODEL), F32)] + exchange.semaphores(),
        compiler_params=_params(("arbitrary", "arbitrary")),
    )(dpa, dpb, w4, z, dz2, norm_w, dw16)


def _local_step(z, tgt, w4, blob4, norm_w, b_in, lb_logits, hg_w, pool_scale, final_w):
    rows = z.shape[0]
    q = D_MODEL // N_CHIPS
    wdh = blob4[:, 0:q].reshape(D_MODEL, D_MODEL)
    wdp = blob4[:, q:2 * q].reshape(D_MODEL, D_MODEL)
    wout = blob4[:, 2 * q:3 * q].reshape(D_MODEL, D_MODEL)
    n_grp = len(POOL_WINDOWS)
    pg = POOL_GDIM // N_CHIPS
    poolw = blob4[:, 3 * q:].reshape(N_CHIPS, n_grp, pg, POOL_GDIM).transpose(1, 0, 2, 3)
    poolw = poolw.reshape(n_grp, POOL_GDIM, POOL_GDIM)

    wexp = jnp.asarray(np.tile(_exponent_matrix(), (1, 3)), BF16)
    wexp_t = jnp.asarray(np.tile(_exponent_matrix().T, (1, 2)), BF16)
    masks = jnp.asarray(_pair_masks(), F32)

    h, p3 = _in_proj(z, norm_w, w4, b_in, rows)
    o, states = _hgrn_forward(p3, lb_logits, wexp, masks, rows)
    d_o, dz2, dpb, dwdh, dwdp, dwout, dpw, small = _mixers(
        o, p3, z, tgt, wdh, wdp, wout, poolw, hg_w, pool_scale, final_w, rows)
    dpw4 = dpw.reshape(n_grp, N_CHIPS, pg, POOL_GDIM).transpose(1, 0, 2, 3)
    dpw4 = dpw4.reshape(N_CHIPS, n_grp * pg * POOL_GDIM // D_MODEL, D_MODEL)
    dblob4 = jnp.concatenate([dwdh.reshape(N_CHIPS, q, D_MODEL), dwdp.reshape(N_CHIPS, q, D_MODEL),
                              dwout.reshape(N_CHIPS, q, D_MODEL), dpw4], axis=1)

    dw_mix, dw_mix16, db_mix = _in_proj_weight_grad(h, dpb, rows, "in_proj_weight_grad_mix")
    dpa, dlb, rxw_mix, rx_blob = _hgrn_backward(
        p3, d_o, states, lb_logits, wexp, wexp_t, masks, dw_mix16, dblob4.astype(BF16), rows)
    dw_rec, dw_rec16, db_rec = _in_proj_weight_grad(h, dpa, rows, "in_proj_weight_grad_rec")
    dz, dnw, rxw_rec = _input_grad(dpa, dpb, w4, z, dz2, norm_w, dw_rec16, rows)

    small = jnp.concatenate([
        small[ROW_LOSS:ROW_LOSS + 1],
        dz[PAD_ROWS:PAD_ROWS + N_META],
        dnw,
        db_rec.reshape(len(SEGS_REC), D_MODEL), db_mix.reshape(len(SEGS_MIX), D_MODEL),
        dlb, jnp.zeros_like(dlb),
        small[ROW_HG_W:ROW_HG_W + 1], small[ROW_POOL_SCALE:ROW_POOL_SCALE + 1],
        small[ROW_FINAL_W:ROW_FINAL_W + 1],
        jnp.zeros((SMALL_ROWS - ROW_FINAL_W - 1, D_MODEL), F32),
    ], axis=0)
    return dz, (dw_rec, dw_mix, rxw_rec, rxw_mix), (dblob4, rx_blob), small


ANY = pl.BlockSpec(memory_space=pl.ANY)
MESH = pl.DeviceIdType.MESH


def _place():
    x, y, c = lax.axis_index("x"), lax.axis_index("y"), lax.axis_index("c")
    chips = [(1 - x, y), (x, 1 - y), (1 - x, 1 - y)]
    return x, y, c, chips


def _gather_weights(win16, blob16, meta):
    hw, hb = win16.shape[0] // 2, blob16.shape[0] // 2

    def body(win_ref, blob_ref, meta_ref, w4_ref, b4_ref, m4_ref, send_sems, recv_sems, local_sems):
        x, y, c, chips = _place()
        me = 2 * x + y
        sibling = (x, y, 1 - c)

        local = [pltpu.make_async_copy(win_ref, w4_ref.at[me], local_sems.at[0]),
                 pltpu.make_async_copy(blob_ref, b4_ref.at[me], local_sems.at[1]),
                 pltpu.make_async_copy(meta_ref, m4_ref.at[me], local_sems.at[2])]
        for cp in local:
            cp.start()

        def half(ref4, chip, rows, which):
            return ref4.at[chip, pl.ds(which * rows, rows), :]

        def copy(k, src, dst, to):
            return pltpu.make_async_remote_copy(src_ref=src, dst_ref=dst, send_sem=send_sems.at[k],
                                                recv_sem=recv_sems.at[k], device_id=to, device_id_type=MESH)

        tensors = [(win_ref, w4_ref, hw), (blob_ref, b4_ref, hb)]
        sends = []
        for t, (src, dst4, rows) in enumerate(tensors):
            for j, (cx, cy) in enumerate(chips):
                sends.append(copy(6 * t + j, src.at[pl.ds(c * rows, rows), :], half(dst4, me, rows, c), (cx, cy, c)))
        for j, (cx, cy) in enumerate(chips):
            sends.append(copy(12 + j, meta_ref, m4_ref.at[me], (cx, cy, c)))
        for cp in sends:
            cp.start()

        passed = []
        for t, (src, dst4, rows) in enumerate(tensors):
            for j, (cx, cy) in enumerate(chips):
                landed = half(dst4, 2 * cx + cy, rows, c)
                copy(6 * t + j, landed, landed, (cx, cy, c)).wait_recv()
                fwd = copy(6 * t + 3 + j, landed, landed, sibling)
                fwd.start()
                passed.append(fwd)
        for t, (src, dst4, rows) in enumerate(tensors):
            for j, (cx, cy) in enumerate(chips):
                landed = half(dst4, 2 * cx + cy, rows, 1 - c)
                copy(6 * t + 3 + j, landed, landed, sibling).wait_recv()
        for j, (cx, cy) in enumerate(chips):
            landed = m4_ref.at[2 * cx + cy]
            copy(12 + j, landed, landed, (cx, cy, c)).wait_recv()
        for cp in sends + passed:
            cp.wait_send()
        for cp in local:
            cp.wait()

    return pl.pallas_call(
        body, name="gather_weights",
        in_specs=[ANY, ANY, ANY], out_specs=[ANY, ANY, ANY],
        out_shape=[jax.ShapeDtypeStruct((N_CHIPS,) + win16.shape, win16.dtype),
                   jax.ShapeDtypeStruct((N_CHIPS,) + blob16.shape, blob16.dtype),
                   jax.ShapeDtypeStruct((N_CHIPS,) + meta.shape, meta.dtype)],
        scratch_shapes=[pltpu.SemaphoreType.DMA((15,)), pltpu.SemaphoreType.DMA((15,)),
                        pltpu.SemaphoreType.DMA((3,))],
    )(win16, blob16, meta)


class _GradExchange:
    def __init__(self, segs, with_blob):
        self.segs = tuple(segs)
        self.with_blob = with_blob

    def landing_w(self):
        return jax.ShapeDtypeStruct((N_DEV, 2, D_MODEL // 2, D_MODEL), BF16)

    def landing_blob(self, blob16):
        return jax.ShapeDtypeStruct((N_DEV, blob16.shape[1] // 2, D_MODEL), BF16)

    def semaphores(self):
        n_send = 2 * len(self.segs) + (2 * N_CHIPS if self.with_blob else 0)
        n_recv = 2 * N_DEV + (N_DEV if self.with_blob else 0)
        return [pltpu.SemaphoreType.DMA((n_send,)), pltpu.SemaphoreType.DMA((n_recv,))]

    def _copies(self, dw_ref, rxw_ref, blob_ref, rxb_ref, send_sems, recv_sems):
        x, y, c = lax.axis_index("x"), lax.axis_index("y"), lax.axis_index("c")
        chip = 2 * x + y
        half = D_MODEL // 2

        def relation(kx, ky, h):
            return (x ^ kx) * 4 + (y ^ ky) * 2 + (c ^ h)

        def copy(src, dst, send_k, recv_k, to):
            return functools.partial(pltpu.make_async_remote_copy, src_ref=src, dst_ref=dst,
                                     send_sem=send_sems.at[send_k], recv_sem=recv_sems.at[recv_k],
                                     device_id=to, device_id_type=MESH)

        sends, recvs = [], []
        for i, s in enumerate(self.segs):
            kx, ky = (s // 2) >> 1, (s // 2) & 1
            for h in range(2):
                r = relation(kx, ky, h)
                sends.append((r != 0, copy(dw_ref.at[i, pl.ds(h * half, half), :], rxw_ref.at[r, s % 2],
                                           2 * i + h, 2 * r + s % 2, (kx, ky, h))))
        for j in range(2):
            mine = [s // 2 for s in self.segs if s % 2 == j]
            if mine:
                cond = functools.reduce(lambda a, b: a | b, [chip == k for k in mine])
                for r in range(1, N_DEV):
                    slot = rxw_ref.at[r, j]
                    recvs.append((cond, copy(slot, slot, 0, 2 * r + j, (x, y, c))))
        if self.with_blob:
            hb = blob_ref.shape[1] // 2
            for k in range(N_CHIPS):
                for h in range(2):
                    r = relation(k >> 1, k & 1, h)
                    sends.append((r != 0, copy(blob_ref.at[k, pl.ds(h * hb, hb), :], rxb_ref.at[r],
                                               2 * len(self.segs) + 2 * k + h, 2 * N_DEV + r, (k >> 1, k & 1, h))))
            for r in range(1, N_DEV):
                slot = rxb_ref.at[r]
                recvs.append((None, copy(slot, slot, 0, 2 * N_DEV + r, (x, y, c))))
        return sends, recvs

    def start(self, *refs):
        sends, _ = self._copies(*refs)
        for cond, make in sends:
            pl.when(cond)(lambda make=make: make().start())

    def wait(self, *refs):
        sends, recvs = self._copies(*refs)
        for cond, make in sends:
            pl.when(cond)(lambda make=make: make().wait_send())
        for cond, make in recvs:
            if cond is None:
                make().wait_recv()
            else:
                pl.when(cond)(lambda make=make: make().wait_recv())


def _sum_landed(own, rx_ref):
    total = own
    for r in range(1, N_DEV):
        total = total + rx_ref[r, 0].astype(F32)
    return total


def _finish_w(dw_rec, dw_mix, rx_rec, rx_mix, place_arr):
    half = D_MODEL // 2
    tm = _tile(half, 256)
    n_rec = len(SEGS_REC)

    def body(place_ref, own_rec_ref, own_mix_ref, rx_rec_ref, rx_mix_ref, out_ref):
        seg = 2 * place_ref[0] + pl.program_id(0)

        @pl.when(seg < n_rec)
        def _():
            out_ref[...] = _sum_landed(own_rec_ref[0], rx_rec_ref)

        @pl.when(seg >= n_rec)
        def _():
            out_ref[...] = _sum_landed(own_mix_ref[0], rx_mix_ref)

    def own_spec(first, count):
        def index(j, i, place_ref):
            seg = 2 * place_ref[0] + j
            return (jnp.clip(seg - first, 0, count - 1), place_ref[1] * (half // tm) + i, 0)
        return pl.BlockSpec((1, tm, D_MODEL), index)

    rx_spec = pl.BlockSpec((N_DEV, 1, tm, D_MODEL), lambda j, i, place_ref: (0, j, i, 0))
    return pl.pallas_call(
        body, name="finish_w",
        grid_spec=pltpu.PrefetchScalarGridSpec(
            num_scalar_prefetch=1, grid=(2, half // tm),
            in_specs=[own_spec(0, n_rec), own_spec(n_rec, len(SEGS_MIX)), rx_spec, rx_spec],
            out_specs=pl.BlockSpec((tm, D_MODEL), lambda j, i, place_ref: (i, j))),
        out_shape=jax.ShapeDtypeStruct((half, 2 * D_MODEL), F32),
        compiler_params=_params(("arbitrary", "arbitrary")),
    )(place_arr, dw_rec, dw_mix, rx_rec, rx_mix)


def _finish_blob(dblob4, rx_blob, place_arr):
    n, rows, cols = rx_blob.shape
    tm = _tile(rows, 256)

    def body(place_ref, own_ref, rx_ref, out_ref):
        out_ref[...] = _sum_landed(own_ref[0, 0], rx_ref)

    return pl.pallas_call(
        body, name="finish_blob",
        grid_spec=pltpu.PrefetchScalarGridSpec(
            num_scalar_prefetch=1, grid=(rows // tm,),
            in_specs=[pl.BlockSpec((1, 1, tm, cols), lambda i, place_ref: (place_ref[0], place_ref[1], i, 0)),
                      pl.BlockSpec((n, 1, tm, cols), lambda i, place_ref: (0, 0, i, 0))],
            out_specs=pl.BlockSpec((tm, cols), lambda i, place_ref: (i, 0))),
        out_shape=jax.ShapeDtypeStruct((rows, cols), F32),
        compiler_params=_params(("arbitrary",)),
    )(place_arr, dblob4.reshape(N_CHIPS, 2, rows, cols), rx_blob.reshape(n, 1, rows, cols))


def _share_finished(fw, fb, small):
    def body(w_ref, b_ref, s_ref, ow_ref, ob_ref, rs_ref, send_sems, recv_sems, local_sems):
        x, y, c, _ = _place()
        sibling = (x, y, 1 - c)
        local = [pltpu.make_async_copy(w_ref, ow_ref.at[c], local_sems.at[0]),
                 pltpu.make_async_copy(b_ref, ob_ref.at[c], local_sems.at[1]),
                 pltpu.make_async_copy(s_ref, rs_ref.at[0], local_sems.at[2])]
        for cp in local:
            cp.start()

        def copy(k, src, dst, to):
            return pltpu.make_async_remote_copy(src_ref=src, dst_ref=dst, send_sem=send_sems.at[k],
                                                recv_sem=recv_sems.at[k], device_id=to, device_id_type=MESH)

        sends = [copy(0, w_ref, ow_ref.at[c], sibling), copy(1, b_ref, ob_ref.at[c], sibling)]
        for r in range(1, N_DEV):
            peer = (x ^ ((r >> 2) & 1), y ^ ((r >> 1) & 1), c ^ (r & 1))
            sends.append(copy(1 + r, s_ref, rs_ref.at[r], peer))
        for cp in sends:
            cp.start()
        landed = [ow_ref.at[1 - c], ob_ref.at[1 - c]] + [rs_ref.at[r] for r in range(1, N_DEV)]
        for k, slot in enumerate(landed):
            copy(k, slot, slot, (x, y, c)).wait_recv()
        for cp in sends:
            cp.wait_send()
        for cp in local:
            cp.wait()

    both = lambda a: jax.ShapeDtypeStruct((2,) + a.shape, a.dtype)
    n_sem = 2 + N_DEV - 1
    return pl.pallas_call(
        body, name="share_finished",
        in_specs=[ANY, ANY, ANY], out_specs=[ANY, ANY, ANY],
        out_shape=[both(fw), both(fb), jax.ShapeDtypeStruct((N_DEV,) + small.shape, F32)],
        scratch_shapes=[pltpu.SemaphoreType.DMA((n_sem,)), pltpu.SemaphoreType.DMA((n_sem,)),
                        pltpu.SemaphoreType.DMA((3,))],
    )(fw, fb, small)


def _sum_small(slots, lb_logits, me_arr):
    def body(me_ref, slots_ref, lbl_ref, out_ref):
        me = me_ref[0]
        total = slots_ref[me]
        for d in range(1, N_DEV):
            total = total + slots_ref[d ^ me]
        out_ref[...] = total
        out_ref[ROW_LOSS:ROW_LOSS + 1, :] = jnp.broadcast_to(
            jnp.sum(total[ROW_LOSS:ROW_LOSS + 1, :], axis=-1, keepdims=True), (1, D_MODEL))
        lb = _lower_bound(lbl_ref[...])
        g0 = total[ROW_LB:ROW_LB + 1, :] * lb * (1.0 - lb)
        out_ref[ROW_LB:ROW_LB + 1, :] = g0
        out_ref[ROW_LB + 1:ROW_LB + 2, :] = -g0

    return pl.pallas_call(
        body, name="sum_small",
        grid_spec=pltpu.PrefetchScalarGridSpec(
            num_scalar_prefetch=1, grid=(1,),
            in_specs=[pl.BlockSpec((N_DEV, SMALL_ROWS, D_MODEL), lambda i, me_ref: (0, 0, 0)),
                      pl.BlockSpec((2, D_MODEL), lambda i, me_ref: (0, 0))],
            out_specs=pl.BlockSpec((SMALL_ROWS, D_MODEL), lambda i, me_ref: (0, 0))),
        out_shape=jax.ShapeDtypeStruct((SMALL_ROWS, D_MODEL), F32),
        compiler_params=_params(("arbitrary",)),
    )(me_arr, slots, lb_logits)


def _adamw(w, g, m, v):
    rows, cols = w.shape
    tm = _tile(rows, 256, mult=8) if rows % 8 == 0 else rows
    c1 = 1.0 / (1.0 - ADAM_B1 ** ADAM_STEP)
    c2 = 1.0 / (1.0 - ADAM_B2 ** ADAM_STEP)

    def body(w_ref, g_ref, m_ref, v_ref, d_ref, nm_ref, nv_ref):
        gt = g_ref[...]
        nm = ADAM_B1 * m_ref[...] + (1.0 - ADAM_B1) * gt
        nv = ADAM_B2 * v_ref[...] + (1.0 - ADAM_B2) * (gt * gt)
        nm_ref[...] = nm
        nv_ref[...] = nv
        d_ref[...] = -ADAM_LR * ((nm * c1) / (jnp.sqrt(nv * c2) + ADAM_EPS) + ADAM_WD * w_ref[...])

    blk = pl.BlockSpec((tm, cols), lambda i: (i, 0))
    sds = jax.ShapeDtypeStruct((rows, cols), F32)
    return pl.pallas_call(
        body, name="adamw",
        grid=(rows // tm,), in_specs=[blk] * 4, out_specs=[blk] * 3, out_shape=[sds] * 3,
        compiler_params=_params(("arbitrary",)),
    )(w, g, m, v)


def kernel(x, meta_tokens, norm_w, w_in, b_in, lb_logits, hg_norm_w, pool_w, pool_scale, w_down_hg, w_down_pool, w_out, final_norm_w, loss_target, m_meta_tokens, m_norm_w, m_w_in, m_b_in, m_lb_logits, m_hg_norm_w, m_pool_w, m_pool_scale, m_w_down_hg, m_w_down_pool, m_w_out, m_final_norm_w, v_meta_tokens, v_norm_w, v_w_in, v_b_in, v_lb_logits, v_hg_norm_w, v_pool_w, v_pool_scale, v_w_down_hg, v_w_down_pool, v_w_out, v_final_norm_w):
    seq = x.shape[1]
    xi, yi, ci = lax.axis_index("x"), lax.axis_index("y"), lax.axis_index("c")
    chip = 2 * xi + yi
    place_arr = jnp.stack([chip, ci]).astype(jnp.int32)
    me_arr = jnp.reshape(4 * xi + 2 * yi + ci, (1,)).astype(jnp.int32)
    q = D_MODEL // N_CHIPS

    def blob_of(wdh, wdp, wo, pw):
        return jnp.concatenate([wdh[0], wdp[0], wo[0], pw[0].reshape(-1, D_MODEL)], axis=0)

    w4, blob4, meta4 = _gather_weights(
        w_in[0].astype(BF16), blob_of(w_down_hg, w_down_pool, w_out, pool_w).astype(BF16), meta_tokens)
    meta_full = meta4.transpose(1, 0, 2).reshape(N_META, D_MODEL)

    z = jnp.concatenate([jnp.zeros((PAD_ROWS, D_MODEL), F32), meta_full, x[0]], axis=0)
    tgt = jnp.concatenate([jnp.zeros((FIRST_TOKEN_ROW, D_MODEL), F32), loss_target[0]], axis=0)
    fw2 = final_norm_w.reshape(1, D_MODEL)
    dz, w_parts, blob_parts, small = _local_step(
        z, tgt, w4, blob4, norm_w, b_in, lb_logits, hg_norm_w, pool_scale, fw2)
    grad_x = dz[FIRST_TOKEN_ROW:][None]

    fin_w = _finish_w(*w_parts, place_arr)
    fin_b = _finish_blob(*blob_parts, place_arr)
    gw2, gb2, slots = _share_finished(fin_w, fin_b, small)
    tot = _sum_small(slots, lb_logits, me_arr)
    g_w_in = gw2.reshape(D_MODEL, 2 * D_MODEL)
    g_blob = gb2.reshape(-1, D_MODEL)

    d_win, nm_win, nv_win = _adamw(w_in[0], g_w_in, m_w_in[0], v_w_in[0])
    d_blob, nm_blob, nv_blob = _adamw(
        blob_of(w_down_hg, w_down_pool, w_out, pool_w), g_blob,
        blob_of(m_w_down_hg, m_w_down_pool, m_w_out, m_pool_w),
        blob_of(v_w_down_hg, v_w_down_pool, v_w_out, v_pool_w))
    g_meta = lax.dynamic_slice_in_dim(tot[ROW_META:ROW_META + N_META], chip * q, q, axis=1)
    d_meta, nm_meta, nv_meta = _adamw(meta_tokens, g_meta, m_meta_tokens, v_meta_tokens)

    def rows_of(nw, bi, lbl, hg, ps, fw):
        return jnp.concatenate([nw, bi.reshape(N_SEG, D_MODEL), lbl, hg, ps, fw.reshape(1, D_MODEL),
                                jnp.zeros((2, D_MODEL), F32)], axis=0)

    g_rows = jnp.concatenate([tot[ROW_NORM_W:ROW_FINAL_W + 1], jnp.zeros((2, D_MODEL), F32)], axis=0)
    d_rows, nm_rows, nv_rows = _adamw(
        rows_of(norm_w, b_in, lb_logits, hg_norm_w, pool_scale, final_norm_w), g_rows,
        rows_of(m_norm_w, m_b_in, m_lb_logits, m_hg_norm_w, m_pool_scale, m_final_norm_w),
        rows_of(v_norm_w, v_b_in, v_lb_logits, v_hg_norm_w, v_pool_scale, v_final_norm_w))

    def unblob(b):
        return (b[0:q][None], b[q:2 * q][None], b[2 * q:3 * q][None], b[3 * q:].reshape(pool_w.shape))

    def unrows(r):
        o = ROW_NORM_W
        return dict(norm_w=r[ROW_NORM_W - o:ROW_B_IN - o], b_in=r[ROW_B_IN - o:ROW_LB - o].reshape(1, -1),
                    lb_logits=r[ROW_LB - o:ROW_HG_W - o], hg_norm_w=r[ROW_HG_W - o:ROW_POOL_SCALE - o],
                    pool_scale=r[ROW_POOL_SCALE - o:ROW_FINAL_W - o], final_norm_w=r[ROW_FINAL_W - o])

    def leaves(meta_part, rows_part, win_part, blob_part):
        r = unrows(rows_part)
        wdh, wdp, wo, pw = unblob(blob_part)
        return [meta_part, r["norm_w"], win_part[None], r["b_in"], r["lb_logits"], r["hg_norm_w"], pw,
                r["pool_scale"], wdh, wdp, wo, r["final_norm_w"]]

    loss = tot[ROW_LOSS, 0]
    return (loss, grad_x,
            *leaves(g_meta, g_rows, g_w_in, g_blob),
            *leaves(d_meta, d_rows, d_win, d_blob),
            *leaves(nm_meta, nm_rows, nm_win, nm_blob),
            *leaves(nv_meta, nv_rows, nv_win, nv_blob))
```

```python
import functools

import numpy as np
import jax
import jax.numpy as jnp
from jax import lax
from jax.experimental import pallas as pl
from jax.experimental.pallas import tpu as pltpu

F32 = jnp.float32
BF16 = jnp.bfloat16

D_MODEL = 1024
N_SEG = 8
N_HEADS = 8
HEAD_DIM = 128
CHUNK = 64
N_META = 16
PAD_ROWS = CHUNK - N_META
FIRST_TOKEN_ROW = CHUNK
LEVELS = (32, 16, 8, 4, 2, 1)
N_EXP = 2 + len(LEVELS)
POOL_WINDOWS = (2, 4, 8, 16)
POOL_GDIM = D_MODEL // len(POOL_WINDOWS)
HALO = 16
HEADS_PER_STEP = 4
EPS = 1e-6
N_CHIPS = 4
N_DEV = 8
SEGS_REC = (0, 1, 2)
SEGS_MIX = (3, 4, 5, 6, 7)

ADAM_LR = 0.001
ADAM_B1 = 0.9
ADAM_B2 = 0.999
ADAM_EPS = 1e-08
ADAM_WD = 0.01
ADAM_STEP = 10

VMEM_LIMIT_BYTES = 56 * 1024 * 1024

ROW_LOSS = 0
ROW_META = 1
ROW_NORM_W = ROW_META + N_META
ROW_B_IN = ROW_NORM_W + 1
ROW_LB = ROW_B_IN + N_SEG
ROW_HG_W = ROW_LB + 2
ROW_POOL_SCALE = ROW_HG_W + 1
ROW_FINAL_W = ROW_POOL_SCALE + 1
SMALL_ROWS = 32


def _tile(total, cap, mult=16):
    best = None
    for t in range(mult, min(total, cap) + 1, mult):
        if total % t == 0:
            best = t
    assert best is not None, (total, cap, mult)
    return best


def _params(sem=None):
    return pltpu.CompilerParams(dimension_semantics=sem, vmem_limit_bytes=VMEM_LIMIT_BYTES)


def _dot(a, b):
    return jnp.dot(a, b, preferred_element_type=F32)


def _dot_nt(a, b):
    return lax.dot_general(a, b, (((1,), (1,)), ((), ())), preferred_element_type=F32)


def _dot_tn(a, b):
    return lax.dot_general(a, b, (((0,), (0,)), ((), ())), preferred_element_type=F32)


def _sigmoid_pair(x):
    t = jnp.exp(-jnp.abs(x))
    r = 1.0 / (1.0 + t)
    pos = x >= 0
    return jnp.where(pos, r, t * r), jnp.where(pos, t * r, r)


def _exponent_matrix():
    t = np.arange(CHUNK)[:, None]
    j = np.arange(CHUNK)[None, :]
    blocks = [j <= t, j > t]
    for m in LEVELS:
        rho = (t // (2 * m)) * (2 * m) + m
        upper = (t >= rho) & (j > rho) & (j <= t)
        lower = (t < rho) & (j > t) & (j <= rho)
        blocks.append(upper | lower)
    return np.concatenate(blocks, axis=0).astype(np.float32)


def _pair_masks():
    t = np.arange(CHUNK)[:, None]
    s = np.arange(CHUNK)[None, :]
    masks = [t == s]
    for m in LEVELS:
        same = (t // (2 * m)) == (s // (2 * m))
        masks.append(same & ((t % (2 * m)) >= m) & ((s % (2 * m)) < m))
    return np.stack(masks).astype(np.float32)


def _split3(x):
    hi = x.astype(BF16)
    r = x - hi.astype(F32)
    mid = r.astype(BF16)
    lo = (r - mid.astype(F32)).astype(BF16)
    return hi, mid, lo


def _chunk_forward(q, fz, lb, valid, wexp, masks):
    sg, sn = _sigmoid_pair(fz)
    f = lb + (1.0 - lb) * sg
    g = jnp.where(valid, jnp.log(f), 0.0)
    kk = jnp.where(valid, (1.0 - lb) * sn, 0.0)
    q = jnp.where(valid, q, 0.0)
    e = jnp.exp(_dot(wexp, jnp.concatenate(_split3(g), axis=0)))
    e_b = e[0:CHUNK]
    e_c = e[CHUNK:2 * CHUNK]
    a = masks[0] * _dot_nt(q.astype(BF16), kk.astype(BF16))
    qm, km = [], []
    for l in range(len(LEVELS)):
        e_m = e[(2 + l) * CHUNK:(3 + l) * CHUNK]
        qm.append(q * e_m)
        km.append(kk * e_m)
        a = a + masks[1 + l] * _dot_nt(qm[l].astype(BF16), km[l].astype(BF16))
    return dict(sg=sg, sn=sn, f=f, kk=kk, q=q, e=e, e_b=e_b, e_c=e_c, a=a, qm=qm, km=km)


def _lower_bound(lbl):
    return 1.0 / (1.0 + jnp.exp(lbl[1:2, :] - lbl[0:1, :]))


def _in_proj(z, norm_w, w4, b_in, rows):
    tm = _tile(rows, 1040)

    def body(z_ref, nw_ref, w_ref, b_ref, h_ref, p_ref):
        @pl.when(pl.program_id(1) == 0)
        def _():
            zt = z_ref[...]
            rstd = lax.rsqrt(jnp.mean(zt * zt, axis=-1, keepdims=True) + EPS)
            h_ref[...] = (zt * rstd * nw_ref[...]).astype(BF16)

        p_ref[0] = _dot(h_ref[...], w_ref[0]) + b_ref[...]

    return pl.pallas_call(
        body, name="in_proj",
        grid=(rows // tm, N_SEG),
        in_specs=[
            pl.BlockSpec((tm, D_MODEL), lambda i, k: (i, 0)),
            pl.BlockSpec((1, D_MODEL), lambda i, k: (0, 0)),
            pl.BlockSpec((1, D_MODEL, D_MODEL), lambda i, k: (k // 2, 0, k % 2)),
            pl.BlockSpec((1, D_MODEL), lambda i, k: (0, k)),
        ],
        out_specs=[
            pl.BlockSpec((tm, D_MODEL), lambda i, k: (i, 0)),
            pl.BlockSpec((1, tm, D_MODEL), lambda i, k: (k, i, 0)),
        ],
        out_shape=[
            jax.ShapeDtypeStruct((rows, D_MODEL), BF16),
            jax.ShapeDtypeStruct((N_SEG, rows, D_MODEL), F32),
        ],
        compiler_params=_params(("arbitrary", "arbitrary")),
    )(z, norm_w, w4, b_in)


def _hgrn_forward(p3, lb_logits, wexp3, masks, rows):
    n_chunks = rows // CHUNK
    cpb = _tile(n_chunks, 13, mult=1)
    rb_rows = cpb * CHUNK
    hps = HEADS_PER_STEP
    width = hps * HEAD_DIM

    def body(q_ref, fz_ref, v_ref, lbl_ref, wexp_ref, mask_ref, o_ref, s_ref, st_ref):
        rb = pl.program_id(1)

        @pl.when(rb == 0)
        def _():
            st_ref[...] = jnp.zeros_like(st_ref)

        lb_all = _lower_bound(lbl_ref[...])
        wexp = wexp_ref[...]
        masks = mask_ref[...]

        def chunk(c, carry):
            r0 = pl.multiple_of(c * CHUNK, CHUNK)
            row = rb * rb_rows + r0 + lax.broadcasted_iota(jnp.int32, (CHUNK, 1), 0)
            valid = row >= PAD_ROWS
            q_all = q_ref[0, pl.ds(r0, CHUNK), :]
            fz_all = fz_ref[0, pl.ds(r0, CHUNK), :]
            v_all = jnp.where(valid, v_ref[0, pl.ds(r0, CHUNK), :], 0.0).astype(BF16)
            st_all = [st_ref[j] for j in range(hps)]
            o_all, st_new = [], []
            for j in range(hps):
                cols = slice(j * HEAD_DIM, (j + 1) * HEAD_DIM)
                cf = _chunk_forward(q_all[:, cols], fz_all[:, cols], lb_all[:, cols], valid, wexp, masks)
                v16 = v_all[:, cols]
                o = _dot_nt((cf["q"] * cf["e_b"]).astype(BF16), st_all[j].astype(BF16))
                o_all.append(o + _dot(cf["a"].astype(BF16), v16))
                kc16 = (cf["kk"] * cf["e_c"]).astype(BF16)
                st_new.append(st_all[j] * cf["e_b"][CHUNK - 1:CHUNK, :] + _dot_tn(v16, kc16))
            o_ref[pl.ds(r0, CHUNK), :] = jnp.concatenate(o_all, axis=1)
            for j in range(hps):
                s_ref[j, c] = st_all[j]
                st_ref[j] = st_new[j]
            return carry

        lax.fori_loop(0, cpb, chunk, 0)

    head_block = lambda seg: pl.BlockSpec((1, rb_rows, width), lambda h, r: (seg, r, h))
    return pl.pallas_call(
        body, name="hgrn_forward",
        grid=(N_HEADS // hps, n_chunks // cpb),
        in_specs=[
            head_block(0), head_block(1), head_block(2),
            pl.BlockSpec((2, width), lambda h, r: (0, h)),
            pl.BlockSpec((N_EXP * CHUNK, 3 * CHUNK), lambda h, r: (0, 0)),
            pl.BlockSpec((1 + len(LEVELS), CHUNK, CHUNK), lambda h, r: (0, 0, 0)),
        ],
        out_specs=[
            pl.BlockSpec((rb_rows, width), lambda h, r: (r, h)),
            pl.BlockSpec((hps, cpb, HEAD_DIM, HEAD_DIM), lambda h, r: (h, r, 0, 0)),
        ],
        out_shape=[
            jax.ShapeDtypeStruct((rows, D_MODEL), F32),
            jax.ShapeDtypeStruct((N_HEADS, n_chunks, HEAD_DIM, HEAD_DIM), F32),
        ],
        scratch_shapes=[pltpu.VMEM((hps, HEAD_DIM, HEAD_DIM), F32)],
        compiler_params=_params(("arbitrary", "arbitrary")),
    )(p3, p3, p3, lb_logits, wexp3, masks)


def _hgrn_backward(p3, d_o, states, lb_logits, wexp3, wexp_t2, masks, dw16, blob16, rows):
    n_chunks = rows // CHUNK
    cpb = _tile(n_chunks, 13, mult=1)
    rb_rows = cpb * CHUNK
    n_rb = n_chunks // cpb
    hps = HEADS_PER_STEP
    width = hps * HEAD_DIM
    n_hb = N_HEADS // hps
    exchange = _GradExchange(SEGS_MIX, with_blob=True)

    def body(q_ref, fz_ref, v_ref, do_ref, s_ref, lbl_ref, wexp_ref, wexpt_ref, mask_ref, dw_ref, blob_ref,
             dp_ref, dlb_ref, rxw_ref, rxb_ref, dst_ref, send_sems, recv_sems):
        step = pl.program_id(1)
        rb = n_rb - 1 - step

        @pl.when((pl.program_id(0) == 0) & (step == 0))
        def _():
            exchange.start(dw_ref, rxw_ref, blob_ref, rxb_ref, send_sems, recv_sems)

        @pl.when(step == 0)
        def _():
            dst_ref[...] = jnp.zeros_like(dst_ref)
            dlb_ref[...] = jnp.zeros_like(dlb_ref)

        lb_all = _lower_bound(lbl_ref[...])
        wexp = wexp_ref[...]
        wexp_t = wexpt_ref[...]
        masks = mask_ref[...]
        last_row = lax.broadcasted_iota(jnp.int32, (CHUNK, 1), 0) == CHUNK - 1

        def one_head(j, c, r0, valid):
            cols = slice(j * HEAD_DIM, (j + 1) * HEAD_DIM)
            lb = lb_all[:, cols]
            cf = _chunk_forward(q_ref[0, pl.ds(r0, CHUNK), cols], fz_ref[0, pl.ds(r0, CHUNK), cols],
                                lb, valid, wexp, masks)
            q, kk, e_b, e_c = cf["q"], cf["kk"], cf["e_b"], cf["e_c"]
            v16 = jnp.where(valid, v_ref[0, pl.ds(r0, CHUNK), cols], 0.0).astype(BF16)
            do16 = do_ref[pl.ds(r0, CHUNK), cols].astype(BF16)
            st = s_ref[j, c]
            dst = dst_ref[j]
            dst16 = dst.astype(BF16)
            qb = q * e_b
            kc = kk * e_c
            q16, kk16 = q.astype(BF16), kk.astype(BF16)

            dv = _dot_tn(cf["a"].astype(BF16), do16) + _dot_nt(kc.astype(BF16), dst16)
            da = _dot_nt(do16, v16)
            dqb = _dot(do16, st.astype(BF16))
            dkc = _dot(v16, dst16)
            e_last = e_b[CHUNK - 1:CHUNK, :]
            de = jnp.sum(dst * st, axis=0, keepdims=True)
            dst_ref[j] = dst * e_last + _dot_tn(do16, qb.astype(BF16))

            dq = e_b * dqb
            dkk = e_c * dkc
            dx = [qb * dqb + jnp.where(last_row, de * e_last, 0.0), kc * dkc]
            dm0 = (masks[0] * da).astype(BF16)
            dq = dq + _dot(dm0, kk16)
            dkk = dkk + _dot(dm0, q16)
            for l in range(len(LEVELS)):
                e_m = cf["e"][(2 + l) * CHUNK:(3 + l) * CHUNK]
                dm = (masks[1 + l] * da).astype(BF16)
                dqm = _dot(dm, cf["km"][l].astype(BF16))
                dkm = _dot_tn(dm, cf["qm"][l].astype(BF16))
                dq = dq + e_m * dqm
                dkk = dkk + e_m * dkm
                dx.append(cf["qm"][l] * dqm + cf["km"][l] * dkm)
            dxa = jnp.concatenate(dx, axis=0)
            hi = dxa.astype(BF16)
            mid = (dxa - hi.astype(F32)).astype(BF16)
            dg = _dot(wexp_t, jnp.concatenate([hi, mid], axis=0))

            t = jnp.where(valid, dg / cf["f"] - dkk, 0.0)
            dfz = (1.0 - lb) * cf["sg"] * cf["sn"] * t
            dlb_ref[:, cols] += jnp.sum(cf["sn"] * t, axis=0, keepdims=True)
            dp_ref[0, pl.ds(r0, CHUNK), cols] = jnp.where(valid, dq, 0.0).astype(BF16)
            dp_ref[1, pl.ds(r0, CHUNK), cols] = dfz.astype(BF16)
            dp_ref[2, pl.ds(r0, CHUNK), cols] = jnp.where(valid, dv, 0.0).astype(BF16)

        def chunk(i, carry):
            c = cpb - 1 - i
            r0 = pl.multiple_of(c * CHUNK, CHUNK)
            row = rb * rb_rows + r0 + lax.broadcasted_iota(jnp.int32, (CHUNK, 1), 0)
            for j in range(hps):
                one_head(j, c, r0, row >= PAD_ROWS)
            return carry

        lax.fori_loop(0, cpb, chunk, 0)

        @pl.when((pl.program_id(0) == n_hb - 1) & (step == n_rb - 1))
        def _():
            exchange.wait(dw_ref, rxw_ref, blob_ref, rxb_ref, send_sems, recv_sems)

    head_block = lambda seg: pl.BlockSpec((1, rb_rows, width), lambda h, s: (seg, n_rb - 1 - s, h))
    return pl.pallas_call(
        body, name="hgrn_backward",
        grid=(n_hb, n_rb),
        in_specs=[
            head_block(0), head_block(1), head_block(2),
            pl.BlockSpec((rb_rows, width), lambda h, s: (n_rb - 1 - s, h)),
            pl.BlockSpec((hps, cpb, HEAD_DIM, HEAD_DIM), lambda h, s: (h, n_rb - 1 - s, 0, 0)),
            pl.BlockSpec((2, width), lambda h, s: (0, h)),
            pl.BlockSpec((N_EXP * CHUNK, 3 * CHUNK), lambda h, s: (0, 0)),
            pl.BlockSpec((CHUNK, 2 * N_EXP * CHUNK), lambda h, s: (0, 0)),
            pl.BlockSpec((1 + len(LEVELS), CHUNK, CHUNK), lambda h, s: (0, 0, 0)),
            ANY, ANY,
        ],
        out_specs=[
            pl.BlockSpec((3, rb_rows, width), lambda h, s: (0, n_rb - 1 - s, h)),
            pl.BlockSpec((1, width), lambda h, s: (0, h)),
            ANY, ANY,
        ],
        out_shape=[
            jax.ShapeDtypeStruct((3, rows, D_MODEL), BF16),
            jax.ShapeDtypeStruct((1, D_MODEL), F32),
            exchange.landing_w(), exchange.landing_blob(blob16),
        ],
        scratch_shapes=[pltpu.VMEM((hps, HEAD_DIM, HEAD_DIM), F32)] + exchange.semaphores(),
        compiler_params=_params(("arbitrary", "arbitrary")),
    )(p3, p3, p3, d_o, states, lb_logits, wexp3, wexp_t2, masks, dw16, blob16)


def _silu_and_grad(x):
    s, _ = _sigmoid_pair(x)
    return x * s, s * (1.0 + x * (1.0 - s))


def _window_sum(ext, width, forward_looking):
    n = ext.shape[0]
    s = ext
    step = 1
    while step < width:
        s = s + pltpu.roll(s, (n - step) if forward_looking else step, 0)
        step *= 2
    return s


def _mixers(o, p3, z, tgt, wdh, wdp, wout, poolw, hg_w, pool_scale, final_w, rows):
    tm = _tile(rows, 160)
    nt = rows // tm
    halo_blocks = tm // HALO
    n_grp = len(POOL_WINDOWS)

    def body(o_ref, ghg_ref, u_ref, gpl_ref, mhg_ref, mpl_ref, uh_ref, z_ref, t_ref,
             wdh_ref, wdp_ref, wout_ref, pw_ref, hgw_ref, ps_ref, fw_ref,
             do_ref, dz2_ref, dp_ref, dwdh_ref, dwdp_ref, dwout_ref, dpw_ref, small_ref, carry_ref):
        step = pl.program_id(0)
        tile = nt - 1 - step

        @pl.when(step == 0)
        def _():
            dwdh_ref[...] = jnp.zeros_like(dwdh_ref)
            dwdp_ref[...] = jnp.zeros_like(dwdp_ref)
            dwout_ref[...] = jnp.zeros_like(dwout_ref)
            dpw_ref[...] = jnp.zeros_like(dpw_ref)
            small_ref[...] = jnp.zeros_like(small_ref)
            carry_ref[...] = jnp.zeros_like(carry_ref)

        row = tile * tm + lax.broadcasted_iota(jnp.int32, (tm, 1), 0)
        real = row >= PAD_ROWS
        pos1 = jnp.maximum(row - PAD_ROWS + 1, 1).astype(F32)

        u = jnp.where(real, u_ref[0], 0.0)
        halo_row = tile * tm - HALO + lax.broadcasted_iota(jnp.int32, (HALO, 1), 0)
        uh = jnp.where(halo_row >= PAD_ROWS, uh_ref[0], 0.0)
        ext = jnp.concatenate([uh, u], axis=0)
        pooled, inv_cnt, mixed = [], [], []
        for g, w in enumerate(POOL_WINDOWS):
            cols = slice(g * POOL_GDIM, (g + 1) * POOL_GDIM)
            inv = 1.0 / jnp.minimum(pos1, float(w))
            ws = _window_sum(ext[:, cols], w, False)[HALO:]
            pg = (ws * inv - u[:, cols]).astype(BF16)
            pooled.append(pg)
            inv_cnt.append(inv)
            mixed.append(_dot(pg, pw_ref[g]))
        mixed = jnp.concatenate(mixed, axis=1)
        gpl = gpl_ref[0]
        sp, dsp = _silu_and_grad(gpl)
        ps = ps_ref[...]
        a_pool = (mixed * ps * sp).astype(BF16)
        y_pool = _dot(a_pool, wdp_ref[...])

        o = o_ref[...]
        o_hat, rstd_h = [], []
        for h in range(N_HEADS):
            oh = o[:, h * HEAD_DIM:(h + 1) * HEAD_DIM]
            r = lax.rsqrt(jnp.mean(oh * oh, axis=-1, keepdims=True) + EPS)
            rstd_h.append(r)
            o_hat.append(oh * r)
        o_hat = jnp.concatenate(o_hat, axis=1)
        hgw = hgw_ref[...]
        o_n = o_hat * hgw
        ghg = ghg_ref[0]
        sh, dsh = _silu_and_grad(ghg)
        a_hg = (o_n * sh).astype(BF16)
        y_hg = _dot(a_hg, wdh_ref[...])

        s_mh, _ = _sigmoid_pair(mhg_ref[0])
        s_mp, _ = _sigmoid_pair(mpl_ref[0])
        merged = (s_mh * y_hg + s_mp * y_pool).astype(BF16)
        z2 = z_ref[...] + _dot(merged, wout_ref[...])
        rstd2 = lax.rsqrt(jnp.mean(z2 * z2, axis=-1, keepdims=True) + EPS)
        zh = z2 * rstd2
        fw = fw_ref[...]
        err = jnp.where(row >= FIRST_TOKEN_ROW, zh * fw - t_ref[...], 0.0)
        small_ref[ROW_LOSS:ROW_LOSS + 1, :] += jnp.sum(err * err, axis=0, keepdims=True) * (0.5 / D_MODEL)
        dy = err * (1.0 / D_MODEL)

        small_ref[ROW_FINAL_W:ROW_FINAL_W + 1, :] += jnp.sum(dy * zh, axis=0, keepdims=True)
        uu = dy * fw
        dz2 = rstd2 * (uu - zh * jnp.mean(uu * zh, axis=-1, keepdims=True))
        dz2_ref[...] = dz2
        dz2_16 = dz2.astype(BF16)
        dmerged = _dot_nt(dz2_16, wout_ref[...])
        dwout_ref[...] += _dot_tn(merged, dz2_16)
        dy_hg = (s_mh * dmerged).astype(BF16)
        dy_pool = (s_mp * dmerged).astype(BF16)
        dp_ref[3] = (dmerged * y_hg * s_mh * (1.0 - s_mh)).astype(BF16)
        dp_ref[4] = (dmerged * y_pool * s_mp * (1.0 - s_mp)).astype(BF16)

        da_hg = _dot_nt(dy_hg, wdh_ref[...])
        dwdh_ref[...] += _dot_tn(a_hg, dy_hg)
        dp_ref[0] = (da_hg * o_n * dsh).astype(BF16)
        do_n = da_hg * sh
        small_ref[ROW_HG_W:ROW_HG_W + 1, :] += jnp.sum(do_n * o_hat, axis=0, keepdims=True)
        d_hat = do_n * hgw
        for h in range(N_HEADS):
            cols = slice(h * HEAD_DIM, (h + 1) * HEAD_DIM)
            dh_, oh_ = d_hat[:, cols], o_hat[:, cols]
            do_ref[:, cols] = rstd_h[h] * (dh_ - oh_ * jnp.mean(dh_ * oh_, axis=-1, keepdims=True))

        da_pool = _dot_nt(dy_pool, wdp_ref[...])
        dwdp_ref[...] += _dot_tn(a_pool, dy_pool)
        small_ref[ROW_POOL_SCALE:ROW_POOL_SCALE + 1, :] += jnp.sum(da_pool * mixed * sp, axis=0, keepdims=True)
        dp_ref[2] = (da_pool * mixed * ps * dsp).astype(BF16)
        dmixed = (da_pool * ps * sp).astype(BF16)
        carry = carry_ref[...]
        du, new_carry = [], []
        for g, w in enumerate(POOL_WINDOWS):
            cols = slice(g * POOL_GDIM, (g + 1) * POOL_GDIM)
            dmg = dmixed[:, cols]
            dpooled = _dot_nt(dmg, pw_ref[g])
            dpw_ref[g] += _dot_tn(pooled[g], dmg)
            dps = dpooled * inv_cnt[g]
            ext_b = jnp.concatenate([dps, carry[:, cols]], axis=0)
            du.append(_window_sum(ext_b, w, True)[:tm] - dpooled)
            new_carry.append(dps[:HALO])
        dp_ref[1] = jnp.where(real, jnp.concatenate(du, axis=1), 0.0).astype(BF16)
        carry_ref[...] = jnp.concatenate(new_carry, axis=1)

    row_block = pl.BlockSpec((tm, D_MODEL), lambda s: (nt - 1 - s, 0))
    seg_block = lambda seg: pl.BlockSpec((1, tm, D_MODEL), lambda s: (seg, nt - 1 - s, 0))
    whole = pl.BlockSpec(memory_space=pltpu.VMEM)
    return pl.pallas_call(
        body, name="mixers",
        grid=(nt,),
        in_specs=[
            row_block, seg_block(3), seg_block(4), seg_block(5), seg_block(6), seg_block(7),
            pl.BlockSpec((1, HALO, D_MODEL),
                         lambda s: (4, jnp.maximum((nt - 1 - s) * halo_blocks - 1, 0), 0)),
            row_block, row_block,
            whole, whole, whole, whole, whole, whole, whole,
        ],
        out_specs=[
            row_block, row_block,
            pl.BlockSpec((5, tm, D_MODEL), lambda s: (0, nt - 1 - s, 0)),
            whole, whole, whole, whole, whole,
        ],
        out_shape=[
            jax.ShapeDtypeStruct((rows, D_MODEL), F32),
            jax.ShapeDtypeStruct((rows, D_MODEL), F32),
            jax.ShapeDtypeStruct((5, rows, D_MODEL), BF16),
            jax.ShapeDtypeStruct((D_MODEL, D_MODEL), F32),
            jax.ShapeDtypeStruct((D_MODEL, D_MODEL), F32),
            jax.ShapeDtypeStruct((D_MODEL, D_MODEL), F32),
            jax.ShapeDtypeStruct((n_grp, POOL_GDIM, POOL_GDIM), F32),
            jax.ShapeDtypeStruct((SMALL_ROWS, D_MODEL), F32),
        ],
        scratch_shapes=[pltpu.VMEM((HALO, D_MODEL), F32)],
        compiler_params=_params(("arbitrary",)),
    )(o, p3, p3, p3, p3, p3, p3, z, tgt, wdh, wdp, wout, poolw, hg_w, pool_scale, final_w)


def _seg_specs(tm, row_of, seg_of):
    def spec_a(*g):
        k = seg_of(*g)
        return (jnp.minimum(k, 2), jnp.where(k < 3, row_of(*g), 0), 0)

    def spec_b(*g):
        k = seg_of(*g)
        return (jnp.maximum(k - 3, 0), jnp.where(k >= 3, row_of(*g), 0), 0)

    return pl.BlockSpec((1, tm, D_MODEL), spec_a), pl.BlockSpec((1, tm, D_MODEL), spec_b)


def _in_proj_weight_grad(h, dp, rows, name):
    n_seg = dp.shape[0]
    tm = _tile(rows, 1040)
    nt = rows // tm

    def body(h_ref, dp_ref, dw_ref, dw16_ref, db_ref, acc_ref, bacc_ref):
        i = pl.program_id(1)

        @pl.when(i == 0)
        def _():
            acc_ref[...] = jnp.zeros_like(acc_ref)
            bacc_ref[...] = jnp.zeros_like(bacc_ref)

        dpt = dp_ref[0]
        acc_ref[...] += _dot_tn(h_ref[...], dpt)
        bacc_ref[...] += jnp.sum(dpt.astype(F32), axis=0, keepdims=True)

        @pl.when(i == nt - 1)
        def _():
            dw_ref[0] = acc_ref[...]
            dw16_ref[0] = acc_ref[...].astype(BF16)
            db_ref[0] = bacc_ref[...]

    w_block = pl.BlockSpec((1, D_MODEL, D_MODEL), lambda k, i: (k, 0, 0))
    return pl.pallas_call(
        body, name=name,
        grid=(n_seg, nt),
        in_specs=[pl.BlockSpec((tm, D_MODEL), lambda k, i: (i, 0)),
                  pl.BlockSpec((1, tm, D_MODEL), lambda k, i: (k, i, 0))],
        out_specs=[w_block, w_block, pl.BlockSpec((1, 1, D_MODEL), lambda k, i: (k, 0, 0))],
        out_shape=[
            jax.ShapeDtypeStruct((n_seg, D_MODEL, D_MODEL), F32),
            jax.ShapeDtypeStruct((n_seg, D_MODEL, D_MODEL), BF16),
            jax.ShapeDtypeStruct((n_seg, 1, D_MODEL), F32),
        ],
        scratch_shapes=[pltpu.VMEM((D_MODEL, D_MODEL), F32), pltpu.VMEM((1, D_MODEL), F32)],
        compiler_params=_params(("arbitrary", "arbitrary")),
    )(h, dp)


def _input_grad(dpa, dpb, w4, z, dz2, norm_w, dw16, rows):
    tm = _tile(rows, 1040)
    nt = rows // tm
    exchange = _GradExchange(SEGS_REC, with_blob=False)

    def body(dpa_ref, dpb_ref, w_ref, z_ref, dz2_ref, nw_ref, dw_ref, dz_ref, dnw_ref, rxw_ref,
             acc_ref, send_sems, recv_sems):
        i, k = pl.program_id(0), pl.program_id(1)

        @pl.when((i == 0) & (k == 0))
        def _():
            exchange.start(dw_ref, rxw_ref, None, None, send_sems, recv_sems)
            dnw_ref[...] = jnp.zeros_like(dnw_ref)

        @pl.when((i == nt - 1) & (k == N_SEG - 1))
        def _():
            exchange.wait(dw_ref, rxw_ref, None, None, send_sems, recv_sems)

        @pl.when(k == 0)
        def _():
            acc_ref[...] = jnp.zeros_like(acc_ref)

        @pl.when(k < 3)
        def _():
            acc_ref[...] += _dot_nt(dpa_ref[0], w_ref[0])

        @pl.when(k >= 3)
        def _():
            acc_ref[...] += _dot_nt(dpb_ref[0], w_ref[0])

        @pl.when(k == N_SEG - 1)
        def _():
            zt = z_ref[...]
            rstd = lax.rsqrt(jnp.mean(zt * zt, axis=-1, keepdims=True) + EPS)
            zh = zt * rstd
            dh = acc_ref[...]
            dnw_ref[...] += jnp.sum(dh * zh, axis=0, keepdims=True)
            uu = dh * nw_ref[...]
            dz_ref[...] = dz2_ref[...] + rstd * (uu - zh * jnp.mean(uu * zh, axis=-1, keepdims=True))

    spec_a, spec_b = _seg_specs(tm, lambda i, k: i, lambda i, k: k)
    last_only = pl.BlockSpec((tm, D_MODEL), lambda i, k: (jnp.where(k == N_SEG - 1, i, 0), 0))
    return pl.pallas_call(
        body, name="input_grad",
        grid=(nt, N_SEG),
        in_specs=[
            spec_a, spec_b,
            pl.BlockSpec((1, D_MODEL, D_MODEL), lambda i, k: (k // 2, 0, k % 2)),
            last_only, last_only,
            pl.BlockSpec((1, D_MODEL), lambda i, k: (0, 0)),
            ANY,
        ],
        out_specs=[
            pl.BlockSpec((tm, D_MODEL), lambda i, k: (i, 0)),
            pl.BlockSpec((1, D_MODEL), lambda i, k: (0, 0)),
            ANY,
        ],
        out_shape=[
            jax.ShapeDtypeStruct((rows, D_MODEL), F32),
            jax.ShapeDtypeStruct((1, D_MODEL), F32),
            exchange.landing_w(),
        ],
        scratch_shapes=[pltpu.VMEM((tm, D_MODEL), F32)] + exchange.semaphores(),
        compiler_params=_params(("arbitrary", "arbitrary")),
    )(dpa, dpb, w4, z, dz2, norm_w, dw16)


def _local_step(z, tgt, w4, blob4, norm_w, b_in, lb_logits, hg_w, pool_scale, final_w):
    rows = z.shape[0]
    q = D_MODEL // N_CHIPS
    wdh = blob4[:, 0:q].reshape(D_MODEL, D_MODEL)
    wdp = blob4[:, q:2 * q].reshape(D_MODEL, D_MODEL)
    wout = blob4[:, 2 * q:3 * q].reshape(D_MODEL, D_MODEL)
    n_grp = len(POOL_WINDOWS)
    pg = POOL_GDIM // N_CHIPS
    poolw = blob4[:, 3 * q:].reshape(N_CHIPS, n_grp, pg, POOL_GDIM).transpose(1, 0, 2, 3)
    poolw = poolw.reshape(n_grp, POOL_GDIM, POOL_GDIM)

    wexp = jnp.asarray(np.tile(_exponent_matrix(), (1, 3)), BF16)
    wexp_t = jnp.asarray(np.tile(_exponent_matrix().T, (1, 2)), BF16)
    masks = jnp.asarray(_pair_masks(), F32)

    h, p3 = _in_proj(z, norm_w, w4, b_in, rows)
    o, states = _hgrn_forward(p3, lb_logits, wexp, masks, rows)
    d_o, dz2, dpb, dwdh, dwdp, dwout, dpw, small = _mixers(
        o, p3, z, tgt, wdh, wdp, wout, poolw, hg_w, pool_scale, final_w, rows)
    dpw4 = dpw.reshape(n_grp, N_CHIPS, pg, POOL_GDIM).transpose(1, 0, 2, 3)
    dpw4 = dpw4.reshape(N_CHIPS, n_grp * pg * POOL_GDIM // D_MODEL, D_MODEL)
    dblob4 = jnp.concatenate([dwdh.reshape(N_CHIPS, q, D_MODEL), dwdp.reshape(N_CHIPS, q, D_MODEL),
                              dwout.reshape(N_CHIPS, q, D_MODEL), dpw4], axis=1)

    dw_mix, dw_mix16, db_mix = _in_proj_weight_grad(h, dpb, rows, "in_proj_weight_grad_mix")
    dpa, dlb, rxw_mix, rx_blob = _hgrn_backward(
        p3, d_o, states, lb_logits, wexp, wexp_t, masks, dw_mix16, dblob4.astype(BF16), rows)
    dw_rec, dw_rec16, db_rec = _in_proj_weight_grad(h, dpa, rows, "in_proj_weight_grad_rec")
    dz, dnw, rxw_rec = _input_grad(dpa, dpb, w4, z, dz2, norm_w, dw_rec16, rows)

    small = jnp.concatenate([
        small[ROW_LOSS:ROW_LOSS + 1],
        dz[PAD_ROWS:PAD_ROWS + N_META],
        dnw,
        db_rec.reshape(len(SEGS_REC), D_MODEL), db_mix.reshape(len(SEGS_MIX), D_MODEL),
        dlb, jnp.zeros_like(dlb),
        small[ROW_HG_W:ROW_HG_W + 1], small[ROW_POOL_SCALE:ROW_POOL_SCALE + 1],
        small[ROW_FINAL_W:ROW_FINAL_W + 1],
        jnp.zeros((SMALL_ROWS - ROW_FINAL_W - 1, D_MODEL), F32),
    ], axis=0)
    return dz, (dw_rec, dw_mix, rxw_rec, rxw_mix), (dblob4, rx_blob), small


ANY = pl.BlockSpec(memory_space=pl.ANY)
MESH = pl.DeviceIdType.MESH


def _place():
    x, y, c = lax.axis_index("x"), lax.axis_index("y"), lax.axis_index("c")
    chips = [(1 - x, y), (x, 1 - y), (1 - x, 1 - y)]
    return x, y, c, chips


def _gather_weights(w4, b4, m4):
    hw, hb = w4.shape[1] // 2, b4.shape[1] // 2

    def body(w_in_ref, b_in_ref, m_in_ref, w4_ref, b4_ref, m4_ref, send_sems, recv_sems):
        x, y, c, chips = _place()
        me = 2 * x + y
        sibling = (x, y, 1 - c)

        def half(ref4, chip, rows, which):
            return ref4.at[chip, pl.ds(which * rows, rows), :]

        def copy(k, src, dst, to):
            return pltpu.make_async_remote_copy(src_ref=src, dst_ref=dst, send_sem=send_sems.at[k],
                                                recv_sem=recv_sems.at[k], device_id=to, device_id_type=MESH)

        tensors = [(w4_ref, hw), (b4_ref, hb)]
        sends = []
        for t, (dst4, rows) in enumerate(tensors):
            for j, (cx, cy) in enumerate(chips):
                own = half(dst4, me, rows, c)
                sends.append(copy(6 * t + j, own, own, (cx, cy, c)))
        for j, (cx, cy) in enumerate(chips):
            sends.append(copy(12 + j, m4_ref.at[me], m4_ref.at[me], (cx, cy, c)))
        for cp in sends:
            cp.start()

        passed = []
        for t, (dst4, rows) in enumerate(tensors):
            for j, (cx, cy) in enumerate(chips):
                landed = half(dst4, 2 * cx + cy, rows, c)
                copy(6 * t + j, landed, landed, (cx, cy, c)).wait_recv()
                fwd = copy(6 * t + 3 + j, landed, landed, sibling)
                fwd.start()
                passed.append(fwd)
        for t, (dst4, rows) in enumerate(tensors):
            for j, (cx, cy) in enumerate(chips):
                landed = half(dst4, 2 * cx + cy, rows, 1 - c)
                copy(6 * t + 3 + j, landed, landed, sibling).wait_recv()
        for j, (cx, cy) in enumerate(chips):
            landed = m4_ref.at[2 * cx + cy]
            copy(12 + j, landed, landed, (cx, cy, c)).wait_recv()
        for cp in sends + passed:
            cp.wait_send()

    same = lambda a: jax.ShapeDtypeStruct(a.shape, a.dtype)
    return pl.pallas_call(
        body, name="gather_weights",
        in_specs=[ANY, ANY, ANY], out_specs=[ANY, ANY, ANY],
        out_shape=[same(w4), same(b4), same(m4)],
        input_output_aliases={0: 0, 1: 1, 2: 2},
        scratch_shapes=[pltpu.SemaphoreType.DMA((15,)), pltpu.SemaphoreType.DMA((15,))],
    )(w4, b4, m4)


class _GradExchange:
    def __init__(self, segs, with_blob):
        self.segs = tuple(segs)
        self.with_blob = with_blob

    def landing_w(self):
        return jax.ShapeDtypeStruct((N_DEV, 2, D_MODEL // 2, D_MODEL), BF16)

    def landing_blob(self, blob16):
        return jax.ShapeDtypeStruct((N_DEV, blob16.shape[1] // 2, D_MODEL), BF16)

    def semaphores(self):
        n_send = 2 * len(self.segs) + (2 * N_CHIPS if self.with_blob else 0)
        n_recv = 2 * N_DEV + (N_DEV if self.with_blob else 0)
        return [pltpu.SemaphoreType.DMA((n_send,)), pltpu.SemaphoreType.DMA((n_recv,))]

    def _copies(self, dw_ref, rxw_ref, blob_ref, rxb_ref, send_sems, recv_sems):
        x, y, c = lax.axis_index("x"), lax.axis_index("y"), lax.axis_index("c")
        chip = 2 * x + y
        half = D_MODEL // 2

        def relation(kx, ky, h):
            return (x ^ kx) * 4 + (y ^ ky) * 2 + (c ^ h)

        def copy(src, dst, send_k, recv_k, to):
            return functools.partial(pltpu.make_async_remote_copy, src_ref=src, dst_ref=dst,
                                     send_sem=send_sems.at[send_k], recv_sem=recv_sems.at[recv_k],
                                     device_id=to, device_id_type=MESH)

        sends, recvs = [], []
        for i, s in enumerate(self.segs):
            kx, ky = (s // 2) >> 1, (s // 2) & 1
            for h in range(2):
                r = relation(kx, ky, h)
                sends.append((r != 0, copy(dw_ref.at[i, pl.ds(h * half, half), :], rxw_ref.at[r, s % 2],
                                           2 * i + h, 2 * r + s % 2, (kx, ky, h))))
        for j in range(2):
            mine = [s // 2 for s in self.segs if s % 2 == j]
            if mine:
                cond = functools.reduce(lambda a, b: a | b, [chip == k for k in mine])
                for r in range(1, N_DEV):
                    slot = rxw_ref.at[r, j]
                    recvs.append((cond, copy(slot, slot, 0, 2 * r + j, (x, y, c))))
        if self.with_blob:
            hb = blob_ref.shape[1] // 2
            for k in range(N_CHIPS):
                for h in range(2):
                    r = relation(k >> 1, k & 1, h)
                    sends.append((r != 0, copy(blob_ref.at[k, pl.ds(h * hb, hb), :], rxb_ref.at[r],
                                               2 * len(self.segs) + 2 * k + h, 2 * N_DEV + r, (k >> 1, k & 1, h))))
            for r in range(1, N_DEV):
                slot = rxb_ref.at[r]
                recvs.append((None, copy(slot, slot, 0, 2 * N_DEV + r, (x, y, c))))
        return sends, recvs

    def start(self, *refs):
        sends, _ = self._copies(*refs)
        for cond, make in sends:
            pl.when(cond)(lambda make=make: make().start())

    def wait(self, *refs):
        sends, recvs = self._copies(*refs)
        for cond, make in sends:
            pl.when(cond)(lambda make=make: make().wait_send())
        for cond, make in recvs:
            if cond is None:
                make().wait_recv()
            else:
                pl.when(cond)(lambda make=make: make().wait_recv())


def _sum_landed(own, rx_ref):
    total = own
    for r in range(1, N_DEV):
        total = total + rx_ref[r, 0].astype(F32)
    return total


def _finish_w(dw_rec, dw_mix, rx_rec, rx_mix, place_arr):
    half = D_MODEL // 2
    tm = _tile(half, 256)
    n_rec = len(SEGS_REC)

    def body(place_ref, own_rec_ref, own_mix_ref, rx_rec_ref, rx_mix_ref, out_ref):
        seg = 2 * place_ref[0] + pl.program_id(0)

        @pl.when(seg < n_rec)
        def _():
            out_ref[0] = _sum_landed(own_rec_ref[0], rx_rec_ref)

        @pl.when(seg >= n_rec)
        def _():
            out_ref[0] = _sum_landed(own_mix_ref[0], rx_mix_ref)

    def own_spec(first, count):
        def index(j, i, place_ref):
            seg = 2 * place_ref[0] + j
            return (jnp.clip(seg - first, 0, count - 1), place_ref[1] * (half // tm) + i, 0)
        return pl.BlockSpec((1, tm, D_MODEL), index)

    rx_spec = pl.BlockSpec((N_DEV, 1, tm, D_MODEL), lambda j, i, place_ref: (0, j, i, 0))
    return pl.pallas_call(
        body, name="finish_w",
        grid_spec=pltpu.PrefetchScalarGridSpec(
            num_scalar_prefetch=1, grid=(2, half // tm),
            in_specs=[own_spec(0, n_rec), own_spec(n_rec, len(SEGS_MIX)), rx_spec, rx_spec],
            out_specs=pl.BlockSpec((1, tm, D_MODEL), lambda j, i, place_ref: (place_ref[1], i, j))),
        out_shape=jax.ShapeDtypeStruct((2, half, 2 * D_MODEL), F32),
        compiler_params=_params(("arbitrary", "arbitrary")),
    )(place_arr, dw_rec, dw_mix, rx_rec, rx_mix)


def _finish_blob(dblob4, rx_blob, place_arr):
    n, rows, cols = rx_blob.shape
    tm = _tile(rows, 256)

    def body(place_ref, own_ref, rx_ref, out_ref):
        out_ref[0] = _sum_landed(own_ref[0, 0], rx_ref)

    return pl.pallas_call(
        body, name="finish_blob",
        grid_spec=pltpu.PrefetchScalarGridSpec(
            num_scalar_prefetch=1, grid=(rows // tm,),
            in_specs=[pl.BlockSpec((1, 1, tm, cols), lambda i, place_ref: (place_ref[0], place_ref[1], i, 0)),
                      pl.BlockSpec((n, 1, tm, cols), lambda i, place_ref: (0, 0, i, 0))],
            out_specs=pl.BlockSpec((1, tm, cols), lambda i, place_ref: (place_ref[1], i, 0))),
        out_shape=jax.ShapeDtypeStruct((2, rows, cols), F32),
        compiler_params=_params(("arbitrary",)),
    )(place_arr, dblob4.reshape(N_CHIPS, 2, rows, cols), rx_blob.reshape(n, 1, rows, cols))


def _share_finished(fw2, fb2, slots):
    def body(w_in_ref, b_in_ref, s_in_ref, w_ref, b_ref, s_ref, send_sems, recv_sems):
        x, y, c, _ = _place()
        sibling = (x, y, 1 - c)

        def copy(k, src, dst, to):
            return pltpu.make_async_remote_copy(src_ref=src, dst_ref=dst, send_sem=send_sems.at[k],
                                                recv_sem=recv_sems.at[k], device_id=to, device_id_type=MESH)

        sends = [copy(0, w_ref.at[c], w_ref.at[c], sibling), copy(1, b_ref.at[c], b_ref.at[c], sibling)]
        for r in range(1, N_DEV):
            peer = (x ^ ((r >> 2) & 1), y ^ ((r >> 1) & 1), c ^ (r & 1))
            sends.append(copy(1 + r, s_ref.at[0], s_ref.at[r], peer))
        for cp in sends:
            cp.start()
        landed = [w_ref.at[1 - c], b_ref.at[1 - c]] + [s_ref.at[r] for r in range(1, N_DEV)]
        for k, slot in enumerate(landed):
            copy(k, slot, slot, (x, y, c)).wait_recv()
        for cp in sends:
            cp.wait_send()

    same = lambda a: jax.ShapeDtypeStruct(a.shape, a.dtype)
    n_sem = 2 + N_DEV - 1
    return pl.pallas_call(
        body, name="share_finished",
        in_specs=[ANY, ANY, ANY], out_specs=[ANY, ANY, ANY],
        out_shape=[same(fw2), same(fb2), same(slots)],
        input_output_aliases={0: 0, 1: 1, 2: 2},
        scratch_shapes=[pltpu.SemaphoreType.DMA((n_sem,)), pltpu.SemaphoreType.DMA((n_sem,))],
    )(fw2, fb2, slots)


def _sum_small(slots, lb_logits, me_arr):
    def body(me_ref, slots_ref, lbl_ref, out_ref):
        me = me_ref[0]
        total = slots_ref[me]
        for d in range(1, N_DEV):
            total = total + slots_ref[d ^ me]
        out_ref[...] = total
        out_ref[ROW_LOSS:ROW_LOSS + 1, :] = jnp.broadcast_to(
            jnp.sum(total[ROW_LOSS:ROW_LOSS + 1, :], axis=-1, keepdims=True), (1, D_MODEL))
        lb = _lower_bound(lbl_ref[...])
        g0 = total[ROW_LB:ROW_LB + 1, :] * lb * (1.0 - lb)
        out_ref[ROW_LB:ROW_LB + 1, :] = g0
        out_ref[ROW_LB + 1:ROW_LB + 2, :] = -g0

    return pl.pallas_call(
        body, name="sum_small",
        grid_spec=pltpu.PrefetchScalarGridSpec(
            num_scalar_prefetch=1, grid=(1,),
            in_specs=[pl.BlockSpec((N_DEV, SMALL_ROWS, D_MODEL), lambda i, me_ref: (0, 0, 0)),
                      pl.BlockSpec((2, D_MODEL), lambda i, me_ref: (0, 0))],
            out_specs=pl.BlockSpec((SMALL_ROWS, D_MODEL), lambda i, me_ref: (0, 0))),
        out_shape=jax.ShapeDtypeStruct((SMALL_ROWS, D_MODEL), F32),
        compiler_params=_params(("arbitrary",)),
    )(me_arr, slots, lb_logits)


def _adamw(w, g, m, v):
    rows, cols = w.shape
    tm = _tile(rows, 256, mult=8) if rows % 8 == 0 else rows
    c1 = 1.0 / (1.0 - ADAM_B1 ** ADAM_STEP)
    c2 = 1.0 / (1.0 - ADAM_B2 ** ADAM_STEP)

    def body(w_ref, g_ref, m_ref, v_ref, d_ref, nm_ref, nv_ref):
        gt = g_ref[...]
        nm = ADAM_B1 * m_ref[...] + (1.0 - ADAM_B1) * gt
        nv = ADAM_B2 * v_ref[...] + (1.0 - ADAM_B2) * (gt * gt)
        nm_ref[...] = nm
        nv_ref[...] = nv
        d_ref[...] = -ADAM_LR * ((nm * c1) / (jnp.sqrt(nv * c2) + ADAM_EPS) + ADAM_WD * w_ref[...])

    blk = pl.BlockSpec((tm, cols), lambda i: (i, 0))
    sds = jax.ShapeDtypeStruct((rows, cols), F32)
    return pl.pallas_call(
        body, name="adamw",
        grid=(rows // tm,), in_specs=[blk] * 4, out_specs=[blk] * 3, out_shape=[sds] * 3,
        compiler_params=_params(("arbitrary",)),
    )(w, g, m, v)


def kernel(x, meta_tokens, norm_w, w_in, b_in, lb_logits, hg_norm_w, pool_w, pool_scale, w_down_hg, w_down_pool, w_out, final_norm_w, loss_target, m_meta_tokens, m_norm_w, m_w_in, m_b_in, m_lb_logits, m_hg_norm_w, m_pool_w, m_pool_scale, m_w_down_hg, m_w_down_pool, m_w_out, m_final_norm_w, v_meta_tokens, v_norm_w, v_w_in, v_b_in, v_lb_logits, v_hg_norm_w, v_pool_w, v_pool_scale, v_w_down_hg, v_w_down_pool, v_w_out, v_final_norm_w):
    seq = x.shape[1]
    xi, yi, ci = lax.axis_index("x"), lax.axis_index("y"), lax.axis_index("c")
    chip = 2 * xi + yi
    place_arr = jnp.stack([chip, ci]).astype(jnp.int32)
    me_arr = jnp.reshape(4 * xi + 2 * yi + ci, (1,)).astype(jnp.int32)
    q = D_MODEL // N_CHIPS

    def blob_of(wdh, wdp, wo, pw):
        return jnp.concatenate([wdh[0], wdp[0], wo[0], pw[0].reshape(-1, D_MODEL)], axis=0)

    def in_every_slot(a):
        return jnp.broadcast_to(a[None], (N_CHIPS,) + a.shape)

    w4, blob4, meta4 = _gather_weights(
        in_every_slot(w_in[0].astype(BF16)),
        in_every_slot(blob_of(w_down_hg, w_down_pool, w_out, pool_w).astype(BF16)),
        in_every_slot(meta_tokens))
    meta_full = meta4.transpose(1, 0, 2).reshape(N_META, D_MODEL)

    z = jnp.concatenate([jnp.zeros((PAD_ROWS, D_MODEL), F32), meta_full, x[0]], axis=0)
    tgt = jnp.concatenate([jnp.zeros((FIRST_TOKEN_ROW, D_MODEL), F32), loss_target[0]], axis=0)
    fw2 = final_norm_w.reshape(1, D_MODEL)
    dz, w_parts, blob_parts, small = _local_step(
        z, tgt, w4, blob4, norm_w, b_in, lb_logits, hg_norm_w, pool_scale, fw2)
    grad_x = dz[FIRST_TOKEN_ROW:][None]

    fin_w = _finish_w(*w_parts, place_arr)
    fin_b = _finish_blob(*blob_parts, place_arr)
    gw2, gb2, slots = _share_finished(fin_w, fin_b, jnp.broadcast_to(small[None], (N_DEV,) + small.shape))
    tot = _sum_small(slots, lb_logits, me_arr)
    g_w_in = gw2.reshape(D_MODEL, 2 * D_MODEL)
    g_blob = gb2.reshape(-1, D_MODEL)

    d_win, nm_win, nv_win = _adamw(w_in[0], g_w_in, m_w_in[0], v_w_in[0])
    d_blob, nm_blob, nv_blob = _adamw(
        blob_of(w_down_hg, w_down_pool, w_out, pool_w), g_blob,
        blob_of(m_w_down_hg, m_w_down_pool, m_w_out, m_pool_w),
        blob_of(v_w_down_hg, v_w_down_pool, v_w_out, v_pool_w))
    g_meta = lax.dynamic_slice_in_dim(tot[ROW_META:ROW_META + N_META], chip * q, q, axis=1)
    d_meta, nm_meta, nv_meta = _adamw(meta_tokens, g_meta, m_meta_tokens, v_meta_tokens)

    def rows_of(nw, bi, lbl, hg, ps, fw):
        return jnp.concatenate([nw, bi.reshape(N_SEG, D_MODEL), lbl, hg, ps, fw.reshape(1, D_MODEL),
                                jnp.zeros((2, D_MODEL), F32)], axis=0)

    g_rows = jnp.concatenate([tot[ROW_NORM_W:ROW_FINAL_W + 1], jnp.zeros((2, D_MODEL), F32)], axis=0)
    d_rows, nm_rows, nv_rows = _adamw(
        rows_of(norm_w, b_in, lb_logits, hg_norm_w, pool_scale, final_norm_w), g_rows,
        rows_of(m_norm_w, m_b_in, m_lb_logits, m_hg_norm_w, m_pool_scale, m_final_norm_w),
        rows_of(v_norm_w, v_b_in, v_lb_logits, v_hg_norm_w, v_pool_scale, v_final_norm_w))

    def unblob(b):
        return (b[0:q][None], b[q:2 * q][None], b[2 * q:3 * q][None], b[3 * q:].reshape(pool_w.shape))

    def unrows(r):
        o = ROW_NORM_W
        return dict(norm_w=r[ROW_NORM_W - o:ROW_B_IN - o], b_in=r[ROW_B_IN - o:ROW_LB - o].reshape(1, -1),
                    lb_logits=r[ROW_LB - o:ROW_HG_W - o], hg_norm_w=r[ROW_HG_W - o:ROW_POOL_SCALE - o],
                    pool_scale=r[ROW_POOL_SCALE - o:ROW_FINAL_W - o], final_norm_w=r[ROW_FINAL_W - o])

    def leaves(meta_part, rows_part, win_part, blob_part):
        r = unrows(rows_part)
        wdh, wdp, wo, pw = unblob(blob_part)
        return [meta_part, r["norm_w"], win_part[None], r["b_in"], r["lb_logits"], r["hg_norm_w"], pw,
                r["pool_scale"], wdh, wdp, wo, r["final_norm_w"]]

    loss = tot[ROW_LOSS, 0]
    return (loss, grad_x,
            *leaves(g_meta, g_rows, g_w_in, g_blob),
            *leaves(d_meta, d_rows, d_win, d_blob),
            *leaves(nm_meta, nm_rows, nm_win, nm_blob),
            *leaves(nv_meta, nv_rows, nv_win, nv_blob))
```

```python
import functools

import numpy as np
import jax
import jax.numpy as jnp
from jax import lax
from jax.experimental import pallas as pl
from jax.experimental.pallas import tpu as pltpu

F32 = jnp.float32
BF16 = jnp.bfloat16

D_MODEL = 1024
N_SEG = 8
N_HEADS = 8
HEAD_DIM = 128
CHUNK = 64
N_META = 16
PAD_ROWS = CHUNK - N_META
FIRST_TOKEN_ROW = CHUNK
LEVELS = (32, 16, 8, 4, 2, 1)
N_EXP = 2 + len(LEVELS)
POOL_WINDOWS = (2, 4, 8, 16)
POOL_GDIM = D_MODEL // len(POOL_WINDOWS)
HALO = 16
LOCAL_UNROLL = 13
HEADS_PER_STEP = 4
EPS = 1e-6
N_CHIPS = 4
N_DEV = 8
SEGS_REC = (0, 1, 2)
SEGS_MIX = (3, 4, 5, 6, 7)

ADAM_LR = 0.001
ADAM_B1 = 0.9
ADAM_B2 = 0.999
ADAM_EPS = 1e-08
ADAM_WD = 0.01
ADAM_STEP = 10

VMEM_LIMIT_BYTES = 56 * 1024 * 1024

ROW_LOSS = 0
ROW_META = 1
ROW_NORM_W = ROW_META + N_META
ROW_B_IN = ROW_NORM_W + 1
ROW_LB = ROW_B_IN + N_SEG
ROW_HG_W = ROW_LB + 2
ROW_POOL_SCALE = ROW_HG_W + 1
ROW_FINAL_W = ROW_POOL_SCALE + 1
SMALL_ROWS = 32


def _tile(total, cap, mult=16):
    best = None
    for t in range(mult, min(total, cap) + 1, mult):
        if total % t == 0:
            best = t
    assert best is not None, (total, cap, mult)
    return best


def _params(sem=None):
    return pltpu.CompilerParams(dimension_semantics=sem, vmem_limit_bytes=VMEM_LIMIT_BYTES)


def _dot(a, b):
    return jnp.dot(a, b, preferred_element_type=F32)


def _dot_nt(a, b):
    return lax.dot_general(a, b, (((1,), (1,)), ((), ())), preferred_element_type=F32)


def _dot_tn(a, b):
    return lax.dot_general(a, b, (((0,), (0,)), ((), ())), preferred_element_type=F32)


def _sigmoid_pair(x):
    t = jnp.exp(-jnp.abs(x))
    r = 1.0 / (1.0 + t)
    pos = x >= 0
    return jnp.where(pos, r, t * r), jnp.where(pos, t * r, r)


def _exponent_matrix():
    t = np.arange(CHUNK)[:, None]
    j = np.arange(CHUNK)[None, :]
    blocks = [j <= t, j > t]
    for m in LEVELS:
        rho = (t // (2 * m)) * (2 * m) + m
        upper = (t >= rho) & (j > rho) & (j <= t)
        lower = (t < rho) & (j > t) & (j <= rho)
        blocks.append(upper | lower)
    return np.concatenate(blocks, axis=0).astype(np.float32)


def _pair_masks():
    t = np.arange(CHUNK)[:, None]
    s = np.arange(CHUNK)[None, :]
    masks = [t == s]
    for m in LEVELS:
        same = (t // (2 * m)) == (s // (2 * m))
        masks.append(same & ((t % (2 * m)) >= m) & ((s % (2 * m)) < m))
    return np.stack(masks).astype(np.float32)


LEVEL_PAIRS = ((0, 1), (2, 3), (4, 5), (6, None))


def _paired_masks():
    m = _pair_masks()
    zero = np.zeros_like(m[0])
    return np.stack([np.concatenate([m[a], zero if b is None else m[b]], axis=1) for a, b in LEVEL_PAIRS])


def _split3(x):
    hi = x.astype(BF16)
    r = x - hi.astype(F32)
    mid = r.astype(BF16)
    lo = (r - mid.astype(F32)).astype(BF16)
    return hi, mid, lo


def _chunk_forward(q, fz, lb, valid, wexp, masks):
    sg, sn = _sigmoid_pair(fz)
    f = lb + (1.0 - lb) * sg
    g = jnp.where(valid, jnp.log(f), 0.0)
    kk = jnp.where(valid, (1.0 - lb) * sn, 0.0)
    q = jnp.where(valid, q, 0.0)
    e = jnp.exp(_dot(wexp, jnp.concatenate(_split3(g), axis=0)))
    e_b = e[0:CHUNK]
    e_c = e[CHUNK:2 * CHUNK]
    a = masks[0] * _dot_nt(q.astype(BF16), kk.astype(BF16))
    qm, km = [], []
    for l in range(len(LEVELS)):
        e_m = e[(2 + l) * CHUNK:(3 + l) * CHUNK]
        qm.append(q * e_m)
        km.append(kk * e_m)
        a = a + masks[1 + l] * _dot_nt(qm[l].astype(BF16), km[l].astype(BF16))
    return dict(sg=sg, sn=sn, f=f, kk=kk, q=q, e=e, e_b=e_b, e_c=e_c, a=a, qm=qm, km=km)


def _lower_bound(lbl):
    return 1.0 / (1.0 + jnp.exp(lbl[1:2, :] - lbl[0:1, :]))


def _in_proj(z, norm_w, w4, b_in, rows):
    tm = _tile(rows, 1040)

    def body(z_ref, nw_ref, w_ref, b_ref, h_ref, p_ref):
        @pl.when(pl.program_id(1) == 0)
        def _():
            zt = z_ref[...]
            rstd = lax.rsqrt(jnp.mean(zt * zt, axis=-1, keepdims=True) + EPS)
            h_ref[...] = (zt * rstd * nw_ref[...]).astype(BF16)

        p_ref[0] = _dot(h_ref[...], w_ref[0]) + b_ref[...]

    return pl.pallas_call(
        body, name="in_proj",
        grid=(rows // tm, N_SEG),
        in_specs=[
            pl.BlockSpec((tm, D_MODEL), lambda i, k: (i, 0)),
            pl.BlockSpec((1, D_MODEL), lambda i, k: (0, 0)),
            pl.BlockSpec((1, D_MODEL, D_MODEL), lambda i, k: (k // 2, 0, k % 2)),
            pl.BlockSpec((1, D_MODEL), lambda i, k: (0, k)),
        ],
        out_specs=[
            pl.BlockSpec((tm, D_MODEL), lambda i, k: (i, 0)),
            pl.BlockSpec((1, tm, D_MODEL), lambda i, k: (k, i, 0)),
        ],
        out_shape=[
            jax.ShapeDtypeStruct((rows, D_MODEL), BF16),
            jax.ShapeDtypeStruct((N_SEG, rows, D_MODEL), F32),
        ],
        compiler_params=_params(("arbitrary", "arbitrary")),
    )(z, norm_w, w4, b_in)


def _hgrn_forward(p3, lb_logits, wexp2, masks2, rows):
    n_chunks = rows // CHUNK
    cpb = _tile(n_chunks, 13, mult=1)
    rb_rows = cpb * CHUNK
    lanes = cpb * HEAD_DIM

    def body(q_ref, fz_ref, v_ref, lbl_ref, wexp_ref, mask_ref, o_ref, s_ref, e16_ref, a2_ref,
             st_ref, e_ref, u_ref, q_s, kk_s, v_s, qb_s, oi_s):
        rb = pl.program_id(1)

        @pl.when(rb == 0)
        def _():
            st_ref[...] = jnp.zeros_like(st_ref)

        lb = _lower_bound(lbl_ref[...])
        row = rb * rb_rows + lax.broadcasted_iota(jnp.int32, (rb_rows, 1), 0)
        valid = row >= PAD_ROWS
        sg, sn = _sigmoid_pair(fz_ref[0])
        g = jnp.where(valid, jnp.log(lb + (1.0 - lb) * sg), 0.0)
        kk_s[...] = jnp.where(valid, (1.0 - lb) * sn, 0.0)
        q_s[...] = jnp.where(valid, q_ref[0], 0.0)
        v_s[...] = jnp.where(valid, v_ref[0], 0.0).astype(BF16)
        hi = g.astype(BF16)
        mid = (g - hi.astype(F32)).astype(BF16)
        g2 = jnp.concatenate(
            [jnp.concatenate([hi[b * CHUNK:(b + 1) * CHUNK], mid[b * CHUNK:(b + 1) * CHUNK]], axis=0)
             for b in range(cpb)], axis=1)
        e_ref[...] = jnp.exp(_dot(wexp_ref[...], g2))
        e16_ref[0, 0] = e_ref[...].astype(BF16)

        zeros16 = jnp.zeros((CHUNK, HEAD_DIM), BF16)

        def local(b, carry):
            r0 = pl.multiple_of(b * CHUNK, CHUNK)
            l0 = pl.multiple_of(b * HEAD_DIM, HEAD_DIM)
            q = q_s[pl.ds(r0, CHUNK), :]
            kk = kk_s[pl.ds(r0, CHUNK), :]
            v16 = v_s[pl.ds(r0, CHUNK), :]

            def scaled(entry):
                if entry == 0:
                    return q.astype(BF16), kk.astype(BF16)
                e_m = e_ref[(1 + entry) * CHUNK:(2 + entry) * CHUNK, pl.ds(l0, HEAD_DIM)]
                return (q * e_m).astype(BF16), (kk * e_m).astype(BF16)

            a2 = jnp.zeros((CHUNK, 2 * CHUNK), F32)
            for p, (ea, eb) in enumerate(LEVEL_PAIRS):
                qa, ka = scaled(ea)
                if eb is None:
                    prod = _dot_nt(qa, jnp.concatenate([ka, zeros16], axis=0))
                else:
                    qb_, kb_ = scaled(eb)
                    rhs = jnp.concatenate([jnp.concatenate([ka, zeros16], axis=1),
                                           jnp.concatenate([zeros16, kb_], axis=1)], axis=0)
                    prod = _dot_nt(jnp.concatenate([qa, qb_], axis=1), rhs)
                a2 = a2 + mask_ref[p] * prod
            a2_16 = a2.astype(BF16)
            a2_ref[pl.ds(r0, CHUNK), :] = a2_16
            oi_s[pl.ds(r0, CHUNK), :] = _dot(a2_16, jnp.concatenate([v16, v16], axis=0))
            e_b = e_ref[0:CHUNK, pl.ds(l0, HEAD_DIM)]
            e_c = e_ref[CHUNK:2 * CHUNK, pl.ds(l0, HEAD_DIM)]
            qb_s[pl.ds(r0, CHUNK), :] = (q * e_b).astype(BF16)
            u_ref[b] = _dot_tn(v16, (kk * e_c).astype(BF16))
            return carry

        lax.fori_loop(0, cpb, local, 0, unroll=LOCAL_UNROLL)

        def recur(b, st):
            l0 = pl.multiple_of(b * HEAD_DIM, HEAD_DIM)
            s_ref[0, b] = st
            return st * e_ref[CHUNK - 1:CHUNK, pl.ds(l0, HEAD_DIM)] + u_ref[b]

        st_ref[...] = lax.fori_loop(0, cpb, recur, st_ref[...])

        def inter(b, carry):
            r0 = pl.multiple_of(b * CHUNK, CHUNK)
            o_ref[pl.ds(r0, CHUNK), :] = oi_s[pl.ds(r0, CHUNK), :] + _dot_nt(
                qb_s[pl.ds(r0, CHUNK), :], s_ref[0, b].astype(BF16))
            return carry

        lax.fori_loop(0, cpb, inter, 0, unroll=LOCAL_UNROLL)

    head_block = lambda seg: pl.BlockSpec((1, rb_rows, HEAD_DIM), lambda h, r: (seg, r, h))
    return pl.pallas_call(
        body, name="hgrn_forward",
        grid=(N_HEADS, n_chunks // cpb),
        in_specs=[
            head_block(0), head_block(1), head_block(2),
            pl.BlockSpec((2, HEAD_DIM), lambda h, r: (0, h)),
            pl.BlockSpec((N_EXP * CHUNK, 2 * CHUNK), lambda h, r: (0, 0)),
            pl.BlockSpec((len(LEVEL_PAIRS), CHUNK, 2 * CHUNK), lambda h, r: (0, 0, 0)),
        ],
        out_specs=[
            pl.BlockSpec((rb_rows, HEAD_DIM), lambda h, r: (r, h)),
            pl.BlockSpec((1, cpb, HEAD_DIM, HEAD_DIM), lambda h, r: (h, r, 0, 0)),
            pl.BlockSpec((1, 1, N_EXP * CHUNK, lanes), lambda h, r: (h, r, 0, 0)),
            pl.BlockSpec((rb_rows, HEAD_DIM), lambda h, r: (r, h)),
        ],
        out_shape=[
            jax.ShapeDtypeStruct((rows, D_MODEL), F32),
            jax.ShapeDtypeStruct((N_HEADS, n_chunks, HEAD_DIM, HEAD_DIM), F32),
            jax.ShapeDtypeStruct((N_HEADS, n_chunks // cpb, N_EXP * CHUNK, lanes), BF16),
            jax.ShapeDtypeStruct((rows, D_MODEL), BF16),
        ],
        scratch_shapes=[
            pltpu.VMEM((HEAD_DIM, HEAD_DIM), F32),
            pltpu.VMEM((N_EXP * CHUNK, lanes), F32),
            pltpu.VMEM((cpb, HEAD_DIM, HEAD_DIM), F32),
            pltpu.VMEM((rb_rows, HEAD_DIM), F32),
            pltpu.VMEM((rb_rows, HEAD_DIM), F32),
            pltpu.VMEM((rb_rows, HEAD_DIM), BF16),
            pltpu.VMEM((rb_rows, HEAD_DIM), BF16),
            pltpu.VMEM((rb_rows, HEAD_DIM), F32),
        ],
        compiler_params=_params(("arbitrary", "arbitrary")),
    )(p3, p3, p3, lb_logits, wexp2, masks2)


def _hgrn_forward_old(p3, lb_logits, wexp3, masks, rows):
    n_chunks = rows // CHUNK
    cpb = _tile(n_chunks, 13, mult=1)
    rb_rows = cpb * CHUNK
    hps = HEADS_PER_STEP
    width = hps * HEAD_DIM

    def body(q_ref, fz_ref, v_ref, lbl_ref, wexp_ref, mask_ref, o_ref, s_ref, st_ref):
        rb = pl.program_id(1)

        @pl.when(rb == 0)
        def _():
            st_ref[...] = jnp.zeros_like(st_ref)

        lb_all = _lower_bound(lbl_ref[...])
        wexp = wexp_ref[...]
        masks = mask_ref[...]

        def chunk(c, carry):
            r0 = pl.multiple_of(c * CHUNK, CHUNK)
            row = rb * rb_rows + r0 + lax.broadcasted_iota(jnp.int32, (CHUNK, 1), 0)
            valid = row >= PAD_ROWS
            q_all = q_ref[0, pl.ds(r0, CHUNK), :]
            fz_all = fz_ref[0, pl.ds(r0, CHUNK), :]
            v_all = jnp.where(valid, v_ref[0, pl.ds(r0, CHUNK), :], 0.0).astype(BF16)
            st_all = [st_ref[j] for j in range(hps)]
            o_all, st_new = [], []
            for j in range(hps):
                cols = slice(j * HEAD_DIM, (j + 1) * HEAD_DIM)
                cf = _chunk_forward(q_all[:, cols], fz_all[:, cols], lb_all[:, cols], valid, wexp, masks)
                v16 = v_all[:, cols]
                o = _dot_nt((cf["q"] * cf["e_b"]).astype(BF16), st_all[j].astype(BF16))
                o_all.append(o + _dot(cf["a"].astype(BF16), v16))
                kc16 = (cf["kk"] * cf["e_c"]).astype(BF16)
                st_new.append(st_all[j] * cf["e_b"][CHUNK - 1:CHUNK, :] + _dot_tn(v16, kc16))
            o_ref[pl.ds(r0, CHUNK), :] = jnp.concatenate(o_all, axis=1)
            for j in range(hps):
                s_ref[j, c] = st_all[j]
                st_ref[j] = st_new[j]
            return carry

        lax.fori_loop(0, cpb, chunk, 0)

    head_block = lambda seg: pl.BlockSpec((1, rb_rows, width), lambda h, r: (seg, r, h))
    return pl.pallas_call(
        body, name="hgrn_forward",
        grid=(N_HEADS // hps, n_chunks // cpb),
        in_specs=[
            head_block(0), head_block(1), head_block(2),
            pl.BlockSpec((2, width), lambda h, r: (0, h)),
            pl.BlockSpec((N_EXP * CHUNK, 3 * CHUNK), lambda h, r: (0, 0)),
            pl.BlockSpec((1 + len(LEVELS), CHUNK, CHUNK), lambda h, r: (0, 0, 0)),
        ],
        out_specs=[
            pl.BlockSpec((rb_rows, width), lambda h, r: (r, h)),
            pl.BlockSpec((hps, cpb, HEAD_DIM, HEAD_DIM), lambda h, r: (h, r, 0, 0)),
        ],
        out_shape=[
            jax.ShapeDtypeStruct((rows, D_MODEL), F32),
            jax.ShapeDtypeStruct((N_HEADS, n_chunks, HEAD_DIM, HEAD_DIM), F32),
        ],
        scratch_shapes=[pltpu.VMEM((hps, HEAD_DIM, HEAD_DIM), F32)],
        compiler_params=_params(("arbitrary", "arbitrary")),
    )(p3, p3, p3, lb_logits, wexp3, masks)


def _hgrn_backward(p3, d_o, states, e16, a2, lb_logits, wexp_t, masks2, dw16, blob16, rows):
    n_chunks = rows // CHUNK
    cpb = _tile(n_chunks, 13, mult=1)
    rb_rows = cpb * CHUNK
    n_rb = n_chunks // cpb
    lanes = cpb * HEAD_DIM
    exchange = _GradExchange(SEGS_MIX, with_blob=True)

    def body(q_ref, fz_ref, v_ref, do_ref, s_ref, e_ref, a2_ref, lbl_ref, wexpt_ref, mask_ref, dw_ref, blob_ref,
             dp_ref, dlb_ref, rxw_ref, rxb_ref,
             dst_ref, g_ref, dsn_ref, q_s, kk_s, v_s, do_s, dq_s, dkk_s, dg_s, dx_s, send_sems, recv_sems):
        step = pl.program_id(1)
        rb = n_rb - 1 - step

        @pl.when((pl.program_id(0) == 0) & (step == 0))
        def _():
            exchange.start(dw_ref, rxw_ref, blob_ref, rxb_ref, send_sems, recv_sems)

        @pl.when(step == 0)
        def _():
            dst_ref[...] = jnp.zeros_like(dst_ref)
            dlb_ref[...] = jnp.zeros_like(dlb_ref)

        lb = _lower_bound(lbl_ref[...])
        row = rb * rb_rows + lax.broadcasted_iota(jnp.int32, (rb_rows, 1), 0)
        valid = row >= PAD_ROWS
        sg, sn = _sigmoid_pair(fz_ref[0])
        f = lb + (1.0 - lb) * sg
        g = jnp.where(valid, jnp.log(f), 0.0)
        kk_s[...] = jnp.where(valid, (1.0 - lb) * sn, 0.0)
        q_s[...] = jnp.where(valid, q_ref[0], 0.0)
        v_s[...] = jnp.where(valid, v_ref[0], 0.0).astype(BF16)
        do_s[...] = do_ref[...].astype(BF16)
        e_last_all = jnp.exp(jnp.concatenate(
            [jnp.sum(g[b * CHUNK:(b + 1) * CHUNK], axis=0, keepdims=True) for b in range(cpb)], axis=0))
        last_row = lax.broadcasted_iota(jnp.int32, (CHUNK, 1), 0) == CHUNK - 1
        zeros16 = jnp.zeros((CHUNK, HEAD_DIM), BF16)

        def factor(block, l0):
            return e_ref[0, 0, block * CHUNK:(block + 1) * CHUNK, pl.ds(l0, HEAD_DIM)].astype(F32)

        def contribution(b, carry):
            r0 = pl.multiple_of(b * CHUNK, CHUNK)
            l0 = pl.multiple_of(b * HEAD_DIM, HEAD_DIM)
            qb16 = (q_s[pl.ds(r0, CHUNK), :] * factor(0, l0)).astype(BF16)
            g_ref[b] = _dot_tn(do_s[pl.ds(r0, CHUNK), :], qb16)
            return carry

        lax.fori_loop(0, cpb, contribution, 0, unroll=LOCAL_UNROLL)

        cur = dst_ref[...]
        for b in reversed(range(cpb)):
            dsn_ref[b] = cur
            cur = cur * e_last_all[b:b + 1, :] + g_ref[b]
        dst_ref[...] = cur

        def local(b, carry):
            r0 = pl.multiple_of(b * CHUNK, CHUNK)
            l0 = pl.multiple_of(b * HEAD_DIM, HEAD_DIM)
            q = q_s[pl.ds(r0, CHUNK), :]
            kk = kk_s[pl.ds(r0, CHUNK), :]
            v16 = v_s[pl.ds(r0, CHUNK), :]
            do16 = do_s[pl.ds(r0, CHUNK), :]
            st = s_ref[0, b]
            dsn = dsn_ref[b]
            dsn16 = dsn.astype(BF16)
            e_b, e_c = factor(0, l0), factor(1, l0)
            qb, kc = q * e_b, kk * e_c

            t = _dot_tn(a2_ref[pl.ds(r0, CHUNK), :], do16)
            dv = t[0:CHUNK] + t[CHUNK:2 * CHUNK] + _dot_nt(kc.astype(BF16), dsn16)
            dp_ref[2, pl.ds(r0, CHUNK), :] = dv.astype(BF16)
            da2 = _dot_nt(do16, jnp.concatenate([v16, v16], axis=0))
            dqb = _dot(do16, st.astype(BF16))
            dkc = _dot(v16, dsn16)
            de = jnp.sum(dsn * st, axis=0, keepdims=True) * e_b[CHUNK - 1:CHUNK, :]
            dq = e_b * dqb
            dkk = e_c * dkc
            dx_s[0:CHUNK, pl.ds(l0, HEAD_DIM)] = (qb * dqb + jnp.where(last_row, de, 0.0)).astype(BF16)
            dx_s[CHUNK:2 * CHUNK, pl.ds(l0, HEAD_DIM)] = (kc * dkc).astype(BF16)

            def scaled(entry):
                if entry == 0:
                    return q, kk, None
                e_m = factor(1 + entry, l0)
                return q * e_m, kk * e_m, e_m

            for p, (ea, eb) in enumerate(LEVEL_PAIRS):
                dm = (mask_ref[p] * da2).astype(BF16)
                qa, ka, e_a = scaled(ea)
                if eb is None:
                    lhs_q = jnp.concatenate([qa.astype(BF16), zeros16], axis=1)
                    rhs_k = jnp.concatenate([jnp.concatenate([ka.astype(BF16), zeros16], axis=1),
                                             jnp.concatenate([zeros16, zeros16], axis=1)], axis=0)
                else:
                    qb_, kb_, e_bb = scaled(eb)
                    lhs_q = jnp.concatenate([qa.astype(BF16), qb_.astype(BF16)], axis=1)
                    rhs_k = jnp.concatenate([jnp.concatenate([ka.astype(BF16), zeros16], axis=1),
                                             jnp.concatenate([zeros16, kb_.astype(BF16)], axis=1)], axis=0)
                dq2 = _dot(dm, rhs_k)
                dk2 = _dot_tn(dm, lhs_q)
                parts = [(ea, qa, ka, e_a, dq2[:, :HEAD_DIM], dk2[0:CHUNK, :HEAD_DIM])]
                if eb is not None:
                    parts.append((eb, qb_, kb_, e_bb, dq2[:, HEAD_DIM:], dk2[CHUNK:2 * CHUNK, HEAD_DIM:]))
                for entry, q_m, k_m, e_m, dq_m, dk_m in parts:
                    if entry == 0:
                        dq = dq + dq_m
                        dkk = dkk + dk_m
                    else:
                        dq = dq + e_m * dq_m
                        dkk = dkk + e_m * dk_m
                        dx_s[(1 + entry) * CHUNK:(2 + entry) * CHUNK, pl.ds(l0, HEAD_DIM)] = (
                            q_m * dq_m + k_m * dk_m).astype(BF16)
            dq_s[pl.ds(r0, CHUNK), :] = dq
            dkk_s[pl.ds(r0, CHUNK), :] = dkk
            return carry

        lax.fori_loop(0, cpb, local, 0, unroll=LOCAL_UNROLL)

        dg_all = _dot(wexpt_ref[...], dx_s[...])
        for b in range(cpb):
            dg_s[b * CHUNK:(b + 1) * CHUNK, :] = dg_all[:, b * HEAD_DIM:(b + 1) * HEAD_DIM]
        t = jnp.where(valid, dg_s[...] / f - dkk_s[...], 0.0)
        dlb_ref[...] += jnp.sum(sn * t, axis=0, keepdims=True)
        dp_ref[0] = jnp.where(valid, dq_s[...], 0.0).astype(BF16)
        dp_ref[1] = ((1.0 - lb) * sg * sn * t).astype(BF16)

        @pl.when((pl.program_id(0) == N_HEADS - 1) & (step == n_rb - 1))
        def _():
            exchange.wait(dw_ref, rxw_ref, blob_ref, rxb_ref, send_sems, recv_sems)

    head_block = lambda seg: pl.BlockSpec((1, rb_rows, HEAD_DIM), lambda h, s: (seg, n_rb - 1 - s, h))
    row_block = pl.BlockSpec((rb_rows, HEAD_DIM), lambda h, s: (n_rb - 1 - s, h))
    return pl.pallas_call(
        body, name="hgrn_backward",
        grid=(N_HEADS, n_rb),
        in_specs=[
            head_block(0), head_block(1), head_block(2),
            row_block,
            pl.BlockSpec((1, cpb, HEAD_DIM, HEAD_DIM), lambda h, s: (h, n_rb - 1 - s, 0, 0)),
            pl.BlockSpec((1, 1, N_EXP * CHUNK, lanes), lambda h, s: (h, n_rb - 1 - s, 0, 0)),
            row_block,
            pl.BlockSpec((2, HEAD_DIM), lambda h, s: (0, h)),
            pl.BlockSpec((CHUNK, N_EXP * CHUNK), lambda h, s: (0, 0)),
            pl.BlockSpec((len(LEVEL_PAIRS), CHUNK, 2 * CHUNK), lambda h, s: (0, 0, 0)),
            ANY, ANY,
        ],
        out_specs=[
            pl.BlockSpec((3, rb_rows, HEAD_DIM), lambda h, s: (0, n_rb - 1 - s, h)),
            pl.BlockSpec((1, HEAD_DIM), lambda h, s: (0, h)),
            ANY, ANY,
        ],
        out_shape=[
            jax.ShapeDtypeStruct((3, rows, D_MODEL), BF16),
            jax.ShapeDtypeStruct((1, D_MODEL), F32),
            exchange.landing_w(), exchange.landing_blob(blob16),
        ],
        scratch_shapes=[
            pltpu.VMEM((HEAD_DIM, HEAD_DIM), F32),
            pltpu.VMEM((cpb, HEAD_DIM, HEAD_DIM), F32),
            pltpu.VMEM((cpb, HEAD_DIM, HEAD_DIM), F32),
            pltpu.VMEM((rb_rows, HEAD_DIM), F32),
            pltpu.VMEM((rb_rows, HEAD_DIM), F32),
            pltpu.VMEM((rb_rows, HEAD_DIM), BF16),
            pltpu.VMEM((rb_rows, HEAD_DIM), BF16),
            pltpu.VMEM((rb_rows, HEAD_DIM), F32),
            pltpu.VMEM((rb_rows, HEAD_DIM), F32),
            pltpu.VMEM((rb_rows, HEAD_DIM), F32),
            pltpu.VMEM((N_EXP * CHUNK, lanes), BF16),
        ] + exchange.semaphores(),
        compiler_params=_params(("arbitrary", "arbitrary")),
    )(p3, p3, p3, d_o, states, e16, a2, lb_logits, wexp_t, masks2, dw16, blob16)


def _hgrn_backward_old(p3, d_o, states, lb_logits, wexp3, wexp_t2, masks, dw16, blob16, rows):
    n_chunks = rows // CHUNK
    cpb = _tile(n_chunks, 13, mult=1)
    rb_rows = cpb * CHUNK
    n_rb = n_chunks // cpb
    hps = HEADS_PER_STEP
    width = hps * HEAD_DIM
    n_hb = N_HEADS // hps
    exchange = _GradExchange(SEGS_MIX, with_blob=True)

    def body(q_ref, fz_ref, v_ref, do_ref, s_ref, lbl_ref, wexp_ref, wexpt_ref, mask_ref, dw_ref, blob_ref,
             dp_ref, dlb_ref, rxw_ref, rxb_ref, dst_ref, send_sems, recv_sems):
        step = pl.program_id(1)
        rb = n_rb - 1 - step

        @pl.when((pl.program_id(0) == 0) & (step == 0))
        def _():
            exchange.start(dw_ref, rxw_ref, blob_ref, rxb_ref, send_sems, recv_sems)

        @pl.when(step == 0)
        def _():
            dst_ref[...] = jnp.zeros_like(dst_ref)
            dlb_ref[...] = jnp.zeros_like(dlb_ref)

        lb_all = _lower_bound(lbl_ref[...])
        wexp = wexp_ref[...]
        wexp_t = wexpt_ref[...]
        masks = mask_ref[...]
        last_row = lax.broadcasted_iota(jnp.int32, (CHUNK, 1), 0) == CHUNK - 1

        def one_head(j, c, r0, valid):
            cols = slice(j * HEAD_DIM, (j + 1) * HEAD_DIM)
            lb = lb_all[:, cols]
            cf = _chunk_forward(q_ref[0, pl.ds(r0, CHUNK), cols], fz_ref[0, pl.ds(r0, CHUNK), cols],
                                lb, valid, wexp, masks)
            q, kk, e_b, e_c = cf["q"], cf["kk"], cf["e_b"], cf["e_c"]
            v16 = jnp.where(valid, v_ref[0, pl.ds(r0, CHUNK), cols], 0.0).astype(BF16)
            do16 = do_ref[pl.ds(r0, CHUNK), cols].astype(BF16)
            st = s_ref[j, c]
            dst = dst_ref[j]
            dst16 = dst.astype(BF16)
            qb = q * e_b
            kc = kk * e_c
            q16, kk16 = q.astype(BF16), kk.astype(BF16)

            dv = _dot_tn(cf["a"].astype(BF16), do16) + _dot_nt(kc.astype(BF16), dst16)
            da = _dot_nt(do16, v16)
            dqb = _dot(do16, st.astype(BF16))
            dkc = _dot(v16, dst16)
            e_last = e_b[CHUNK - 1:CHUNK, :]
            de = jnp.sum(dst * st, axis=0, keepdims=True)
            dst_ref[j] = dst * e_last + _dot_tn(do16, qb.astype(BF16))

            dq = e_b * dqb
            dkk = e_c * dkc
            dx = [qb * dqb + jnp.where(last_row, de * e_last, 0.0), kc * dkc]
            dm0 = (masks[0] * da).astype(BF16)
            dq = dq + _dot(dm0, kk16)
            dkk = dkk + _dot(dm0, q16)
            for l in range(len(LEVELS)):
                e_m = cf["e"][(2 + l) * CHUNK:(3 + l) * CHUNK]
                dm = (masks[1 + l] * da).astype(BF16)
                dqm = _dot(dm, cf["km"][l].astype(BF16))
                dkm = _dot_tn(dm, cf["qm"][l].astype(BF16))
                dq = dq + e_m * dqm
                dkk = dkk + e_m * dkm
                dx.append(cf["qm"][l] * dqm + cf["km"][l] * dkm)
            dxa = jnp.concatenate(dx, axis=0)
            hi = dxa.astype(BF16)
            mid = (dxa - hi.astype(F32)).astype(BF16)
            dg = _dot(wexp_t, jnp.concatenate([hi, mid], axis=0))

            t = jnp.where(valid, dg / cf["f"] - dkk, 0.0)
            dfz = (1.0 - lb) * cf["sg"] * cf["sn"] * t
            dlb_ref[:, cols] += jnp.sum(cf["sn"] * t, axis=0, keepdims=True)
            dp_ref[0, pl.ds(r0, CHUNK), cols] = jnp.where(valid, dq, 0.0).astype(BF16)
            dp_ref[1, pl.ds(r0, CHUNK), cols] = dfz.astype(BF16)
            dp_ref[2, pl.ds(r0, CHUNK), cols] = jnp.where(valid, dv, 0.0).astype(BF16)

        def chunk(i, carry):
            c = cpb - 1 - i
            r0 = pl.multiple_of(c * CHUNK, CHUNK)
            row = rb * rb_rows + r0 + lax.broadcasted_iota(jnp.int32, (CHUNK, 1), 0)
            for j in range(hps):
                one_head(j, c, r0, row >= PAD_ROWS)
            return carry

        lax.fori_loop(0, cpb, chunk, 0)

        @pl.when((pl.program_id(0) == n_hb - 1) & (step == n_rb - 1))
        def _():
            exchange.wait(dw_ref, rxw_ref, blob_ref, rxb_ref, send_sems, recv_sems)

    head_block = lambda seg: pl.BlockSpec((1, rb_rows, width), lambda h, s: (seg, n_rb - 1 - s, h))
    return pl.pallas_call(
        body, name="hgrn_backward",
        grid=(n_hb, n_rb),
        in_specs=[
            head_block(0), head_block(1), head_block(2),
            pl.BlockSpec((rb_rows, width), lambda h, s: (n_rb - 1 - s, h)),
            pl.BlockSpec((hps, cpb, HEAD_DIM, HEAD_DIM), lambda h, s: (h, n_rb - 1 - s, 0, 0)),
            pl.BlockSpec((2, width), lambda h, s: (0, h)),
            pl.BlockSpec((N_EXP * CHUNK, 3 * CHUNK), lambda h, s: (0, 0)),
            pl.BlockSpec((CHUNK, 2 * N_EXP * CHUNK), lambda h, s: (0, 0)),
            pl.BlockSpec((1 + len(LEVELS), CHUNK, CHUNK), lambda h, s: (0, 0, 0)),
            ANY, ANY,
        ],
        out_specs=[
            pl.BlockSpec((3, rb_rows, width), lambda h, s: (0, n_rb - 1 - s, h)),
            pl.BlockSpec((1, width), lambda h, s: (0, h)),
            ANY, ANY,
        ],
        out_shape=[
            jax.ShapeDtypeStruct((3, rows, D_MODEL), BF16),
            jax.ShapeDtypeStruct((1, D_MODEL), F32),
            exchange.landing_w(), exchange.landing_blob(blob16),
        ],
        scratch_shapes=[pltpu.VMEM((hps, HEAD_DIM, HEAD_DIM), F32)] + exchange.semaphores(),
        compiler_params=_params(("arbitrary", "arbitrary")),
    )(p3, p3, p3, d_o, states, lb_logits, wexp3, wexp_t2, masks, dw16, blob16)


def _silu_and_grad(x):
    s, _ = _sigmoid_pair(x)
    return x * s, s * (1.0 + x * (1.0 - s))


def _window_sum(ext, width, forward_looking):
    n = ext.shape[0]
    s = ext
    step = 1
    while step < width:
        s = s + pltpu.roll(s, (n - step) if forward_looking else step, 0)
        step *= 2
    return s


def _mixers(o, p3, z, tgt, wdh, wdp, wout, poolw, hg_w, pool_scale, final_w, rows):
    tm = _tile(rows, 160)
    nt = rows // tm
    halo_blocks = tm // HALO
    n_grp = len(POOL_WINDOWS)

    def body(o_ref, ghg_ref, u_ref, gpl_ref, mhg_ref, mpl_ref, uh_ref, z_ref, t_ref,
             wdh_ref, wdp_ref, wout_ref, pw_ref, hgw_ref, ps_ref, fw_ref,
             do_ref, dz2_ref, dp_ref, dwdh_ref, dwdp_ref, dwout_ref, dpw_ref, small_ref, carry_ref):
        step = pl.program_id(0)
        tile = nt - 1 - step

        @pl.when(step == 0)
        def _():
            dwdh_ref[...] = jnp.zeros_like(dwdh_ref)
            dwdp_ref[...] = jnp.zeros_like(dwdp_ref)
            dwout_ref[...] = jnp.zeros_like(dwout_ref)
            dpw_ref[...] = jnp.zeros_like(dpw_ref)
            small_ref[...] = jnp.zeros_like(small_ref)
            carry_ref[...] = jnp.zeros_like(carry_ref)

        row = tile * tm + lax.broadcasted_iota(jnp.int32, (tm, 1), 0)
        real = row >= PAD_ROWS
        pos1 = jnp.maximum(row - PAD_ROWS + 1, 1).astype(F32)

        u = jnp.where(real, u_ref[0], 0.0)
        halo_row = tile * tm - HALO + lax.broadcasted_iota(jnp.int32, (HALO, 1), 0)
        uh = jnp.where(halo_row >= PAD_ROWS, uh_ref[0], 0.0)
        ext = jnp.concatenate([uh, u], axis=0)
        pooled, inv_cnt, mixed = [], [], []
        for g, w in enumerate(POOL_WINDOWS):
            cols = slice(g * POOL_GDIM, (g + 1) * POOL_GDIM)
            inv = 1.0 / jnp.minimum(pos1, float(w))
            ws = _window_sum(ext[:, cols], w, False)[HALO:]
            pg = (ws * inv - u[:, cols]).astype(BF16)
            pooled.append(pg)
            inv_cnt.append(inv)
            mixed.append(_dot(pg, pw_ref[g]))
        mixed = jnp.concatenate(mixed, axis=1)
        gpl = gpl_ref[0]
        sp, dsp = _silu_and_grad(gpl)
        ps = ps_ref[...]
        a_pool = (mixed * ps * sp).astype(BF16)
        y_pool = _dot(a_pool, wdp_ref[...])

        o = o_ref[...]
        o_hat, rstd_h = [], []
        for h in range(N_HEADS):
            oh = o[:, h * HEAD_DIM:(h + 1) * HEAD_DIM]
            r = lax.rsqrt(jnp.mean(oh * oh, axis=-1, keepdims=True) + EPS)
            rstd_h.append(r)
            o_hat.append(oh * r)
        o_hat = jnp.concatenate(o_hat, axis=1)
        hgw = hgw_ref[...]
        o_n = o_hat * hgw
        ghg = ghg_ref[0]
        sh, dsh = _silu_and_grad(ghg)
        a_hg = (o_n * sh).astype(BF16)
        y_hg = _dot(a_hg, wdh_ref[...])

        s_mh, _ = _sigmoid_pair(mhg_ref[0])
        s_mp, _ = _sigmoid_pair(mpl_ref[0])
        merged = (s_mh * y_hg + s_mp * y_pool).astype(BF16)
        z2 = z_ref[...] + _dot(merged, wout_ref[...])
        rstd2 = lax.rsqrt(jnp.mean(z2 * z2, axis=-1, keepdims=True) + EPS)
        zh = z2 * rstd2
        fw = fw_ref[...]
        err = jnp.where(row >= FIRST_TOKEN_ROW, zh * fw - t_ref[...], 0.0)
        small_ref[ROW_LOSS:ROW_LOSS + 1, :] += jnp.sum(err * err, axis=0, keepdims=True) * (0.5 / D_MODEL)
        dy = err * (1.0 / D_MODEL)

        small_ref[ROW_FINAL_W:ROW_FINAL_W + 1, :] += jnp.sum(dy * zh, axis=0, keepdims=True)
        uu = dy * fw
        dz2 = rstd2 * (uu - zh * jnp.mean(uu * zh, axis=-1, keepdims=True))
        dz2_ref[...] = dz2
        dz2_16 = dz2.astype(BF16)
        dmerged = _dot_nt(dz2_16, wout_ref[...])
        dwout_ref[...] += _dot_tn(merged, dz2_16)
        dy_hg = (s_mh * dmerged).astype(BF16)
        dy_pool = (s_mp * dmerged).astype(BF16)
        dp_ref[3] = (dmerged * y_hg * s_mh * (1.0 - s_mh)).astype(BF16)
        dp_ref[4] = (dmerged * y_pool * s_mp * (1.0 - s_mp)).astype(BF16)

        da_hg = _dot_nt(dy_hg, wdh_ref[...])
        dwdh_ref[...] += _dot_tn(a_hg, dy_hg)
        dp_ref[0] = (da_hg * o_n * dsh).astype(BF16)
        do_n = da_hg * sh
        small_ref[ROW_HG_W:ROW_HG_W + 1, :] += jnp.sum(do_n * o_hat, axis=0, keepdims=True)
        d_hat = do_n * hgw
        for h in range(N_HEADS):
            cols = slice(h * HEAD_DIM, (h + 1) * HEAD_DIM)
            dh_, oh_ = d_hat[:, cols], o_hat[:, cols]
            do_ref[:, cols] = rstd_h[h] * (dh_ - oh_ * jnp.mean(dh_ * oh_, axis=-1, keepdims=True))

        da_pool = _dot_nt(dy_pool, wdp_ref[...])
        dwdp_ref[...] += _dot_tn(a_pool, dy_pool)
        small_ref[ROW_POOL_SCALE:ROW_POOL_SCALE + 1, :] += jnp.sum(da_pool * mixed * sp, axis=0, keepdims=True)
        dp_ref[2] = (da_pool * mixed * ps * dsp).astype(BF16)
        dmixed = (da_pool * ps * sp).astype(BF16)
        carry = carry_ref[...]
        du, new_carry = [], []
        for g, w in enumerate(POOL_WINDOWS):
            cols = slice(g * POOL_GDIM, (g + 1) * POOL_GDIM)
            dmg = dmixed[:, cols]
            dpooled = _dot_nt(dmg, pw_ref[g])
            dpw_ref[g] += _dot_tn(pooled[g], dmg)
            dps = dpooled * inv_cnt[g]
            ext_b = jnp.concatenate([dps, carry[:, cols]], axis=0)
            du.append(_window_sum(ext_b, w, True)[:tm] - dpooled)
            new_carry.append(dps[:HALO])
        dp_ref[1] = jnp.where(real, jnp.concatenate(du, axis=1), 0.0).astype(BF16)
        carry_ref[...] = jnp.concatenate(new_carry, axis=1)

    row_block = pl.BlockSpec((tm, D_MODEL), lambda s: (nt - 1 - s, 0))
    seg_block = lambda seg: pl.BlockSpec((1, tm, D_MODEL), lambda s: (seg, nt - 1 - s, 0))
    whole = pl.BlockSpec(memory_space=pltpu.VMEM)
    return pl.pallas_call(
        body, name="mixers",
        grid=(nt,),
        in_specs=[
            row_block, seg_block(3), seg_block(4), seg_block(5), seg_block(6), seg_block(7),
            pl.BlockSpec((1, HALO, D_MODEL),
                         lambda s: (4, jnp.maximum((nt - 1 - s) * halo_blocks - 1, 0), 0)),
            row_block, row_block,
            whole, whole, whole, whole, whole, whole, whole,
        ],
        out_specs=[
            row_block, row_block,
            pl.BlockSpec((5, tm, D_MODEL), lambda s: (0, nt - 1 - s, 0)),
            whole, whole, whole, whole, whole,
        ],
        out_shape=[
            jax.ShapeDtypeStruct((rows, D_MODEL), F32),
            jax.ShapeDtypeStruct((rows, D_MODEL), F32),
            jax.ShapeDtypeStruct((5, rows, D_MODEL), BF16),
            jax.ShapeDtypeStruct((D_MODEL, D_MODEL), F32),
            jax.ShapeDtypeStruct((D_MODEL, D_MODEL), F32),
            jax.ShapeDtypeStruct((D_MODEL, D_MODEL), F32),
            jax.ShapeDtypeStruct((n_grp, POOL_GDIM, POOL_GDIM), F32),
            jax.ShapeDtypeStruct((SMALL_ROWS, D_MODEL), F32),
        ],
        scratch_shapes=[pltpu.VMEM((HALO, D_MODEL), F32)],
        compiler_params=_params(("arbitrary",)),
    )(o, p3, p3, p3, p3, p3, p3, z, tgt, wdh, wdp, wout, poolw, hg_w, pool_scale, final_w)


def _seg_specs(tm, row_of, seg_of):
    def spec_a(*g):
        k = seg_of(*g)
        return (jnp.minimum(k, 2), jnp.where(k < 3, row_of(*g), 0), 0)

    def spec_b(*g):
        k = seg_of(*g)
        return (jnp.maximum(k - 3, 0), jnp.where(k >= 3, row_of(*g), 0), 0)

    return pl.BlockSpec((1, tm, D_MODEL), spec_a), pl.BlockSpec((1, tm, D_MODEL), spec_b)


def _in_proj_weight_grad(h, dp, rows, name):
    n_seg = dp.shape[0]
    tm = _tile(rows, 1040)
    nt = rows // tm

    def body(h_ref, dp_ref, dw_ref, dw16_ref, db_ref, acc_ref, bacc_ref):
        i = pl.program_id(1)

        @pl.when(i == 0)
        def _():
            acc_ref[...] = jnp.zeros_like(acc_ref)
            bacc_ref[...] = jnp.zeros_like(bacc_ref)

        dpt = dp_ref[0]
        acc_ref[...] += _dot_tn(h_ref[...], dpt)
        bacc_ref[...] += jnp.sum(dpt.astype(F32), axis=0, keepdims=True)

        @pl.when(i == nt - 1)
        def _():
            dw_ref[0] = acc_ref[...]
            dw16_ref[0] = acc_ref[...].astype(BF16)
            db_ref[0] = bacc_ref[...]

    w_block = pl.BlockSpec((1, D_MODEL, D_MODEL), lambda k, i: (k, 0, 0))
    return pl.pallas_call(
        body, name=name,
        grid=(n_seg, nt),
        in_specs=[pl.BlockSpec((tm, D_MODEL), lambda k, i: (i, 0)),
                  pl.BlockSpec((1, tm, D_MODEL), lambda k, i: (k, i, 0))],
        out_specs=[w_block, w_block, pl.BlockSpec((1, 1, D_MODEL), lambda k, i: (k, 0, 0))],
        out_shape=[
            jax.ShapeDtypeStruct((n_seg, D_MODEL, D_MODEL), F32),
            jax.ShapeDtypeStruct((n_seg, D_MODEL, D_MODEL), BF16),
            jax.ShapeDtypeStruct((n_seg, 1, D_MODEL), F32),
        ],
        scratch_shapes=[pltpu.VMEM((D_MODEL, D_MODEL), F32), pltpu.VMEM((1, D_MODEL), F32)],
        compiler_params=_params(("arbitrary", "arbitrary")),
    )(h, dp)


def _input_grad(dpa, dpb, w4, z, dz2, norm_w, dw16, rows):
    tm = _tile(rows, 1040)
    nt = rows // tm
    exchange = _GradExchange(SEGS_REC, with_blob=False)

    def body(dpa_ref, dpb_ref, w_ref, z_ref, dz2_ref, nw_ref, dw_ref, dz_ref, dnw_ref, rxw_ref,
             acc_ref, send_sems, recv_sems):
        i, k = pl.program_id(0), pl.program_id(1)

        @pl.when((i == 0) & (k == 0))
        def _():
            exchange.start(dw_ref, rxw_ref, None, None, send_sems, recv_sems)
            dnw_ref[...] = jnp.zeros_like(dnw_ref)

        @pl.when((i == nt - 1) & (k == N_SEG - 1))
        def _():
            exchange.wait(dw_ref, rxw_ref, None, None, send_sems, recv_sems)

        @pl.when(k == 0)
        def _():
            acc_ref[...] = jnp.zeros_like(acc_ref)

        @pl.when(k < 3)
        def _():
            acc_ref[...] += _dot_nt(dpa_ref[0], w_ref[0])

        @pl.when(k >= 3)
        def _():
            acc_ref[...] += _dot_nt(dpb_ref[0], w_ref[0])

        @pl.when(k == N_SEG - 1)
        def _():
            zt = z_ref[...]
            rstd = lax.rsqrt(jnp.mean(zt * zt, axis=-1, keepdims=True) + EPS)
            zh = zt * rstd
            dh = acc_ref[...]
            dnw_ref[...] += jnp.sum(dh * zh, axis=0, keepdims=True)
            uu = dh * nw_ref[...]
            dz_ref[...] = dz2_ref[...] + rstd * (uu - zh * jnp.mean(uu * zh, axis=-1, keepdims=True))

    spec_a, spec_b = _seg_specs(tm, lambda i, k: i, lambda i, k: k)
    last_only = pl.BlockSpec((tm, D_MODEL), lambda i, k: (jnp.where(k == N_SEG - 1, i, 0), 0))
    return pl.pallas_call(
        body, name="input_grad",
        grid=(nt, N_SEG),
        in_specs=[
            spec_a, spec_b,
            pl.BlockSpec((1, D_MODEL, D_MODEL), lambda i, k: (k // 2, 0, k % 2)),
            last_only, last_only,
            pl.BlockSpec((1, D_MODEL), lambda i, k: (0, 0)),
            ANY,
        ],
        out_specs=[
            pl.BlockSpec((tm, D_MODEL), lambda i, k: (i, 0)),
            pl.BlockSpec((1, D_MODEL), lambda i, k: (0, 0)),
            ANY,
        ],
        out_shape=[
            jax.ShapeDtypeStruct((rows, D_MODEL), F32),
            jax.ShapeDtypeStruct((1, D_MODEL), F32),
            exchange.landing_w(),
        ],
        scratch_shapes=[pltpu.VMEM((tm, D_MODEL), F32)] + exchange.semaphores(),
        compiler_params=_params(("arbitrary", "arbitrary")),
    )(dpa, dpb, w4, z, dz2, norm_w, dw16)


def _local_step(z, tgt, w4, blob4, norm_w, b_in, lb_logits, hg_w, pool_scale, final_w):
    rows = z.shape[0]
    q = D_MODEL // N_CHIPS
    wdh = blob4[:, 0:q].reshape(D_MODEL, D_MODEL)
    wdp = blob4[:, q:2 * q].reshape(D_MODEL, D_MODEL)
    wout = blob4[:, 2 * q:3 * q].reshape(D_MODEL, D_MODEL)
    n_grp = len(POOL_WINDOWS)
    pg = POOL_GDIM // N_CHIPS
    poolw = blob4[:, 3 * q:].reshape(N_CHIPS, n_grp, pg, POOL_GDIM).transpose(1, 0, 2, 3)
    poolw = poolw.reshape(n_grp, POOL_GDIM, POOL_GDIM)

    wexp2 = jnp.asarray(np.tile(_exponent_matrix(), (1, 2)), BF16)
    wexp_t = jnp.asarray(_exponent_matrix().T, BF16)
    masks2 = jnp.asarray(_paired_masks(), F32)

    h, p3 = _in_proj(z, norm_w, w4, b_in, rows)
    o, states, e16, a2 = _hgrn_forward(p3, lb_logits, wexp2, masks2, rows)
    d_o, dz2, dpb, dwdh, dwdp, dwout, dpw, small = _mixers(
        o, p3, z, tgt, wdh, wdp, wout, poolw, hg_w, pool_scale, final_w, rows)
    dpw4 = dpw.reshape(n_grp, N_CHIPS, pg, POOL_GDIM).transpose(1, 0, 2, 3)
    dpw4 = dpw4.reshape(N_CHIPS, n_grp * pg * POOL_GDIM // D_MODEL, D_MODEL)
    dblob4 = jnp.concatenate([dwdh.reshape(N_CHIPS, q, D_MODEL), dwdp.reshape(N_CHIPS, q, D_MODEL),
                              dwout.reshape(N_CHIPS, q, D_MODEL), dpw4], axis=1)

    dw_mix, dw_mix16, db_mix = _in_proj_weight_grad(h, dpb, rows, "in_proj_weight_grad_mix")
    dpa, dlb, rxw_mix, rx_blob = _hgrn_backward(
        p3, d_o, states, e16, a2, lb_logits, wexp_t, masks2, dw_mix16, dblob4.astype(BF16), rows)
    dw_rec, dw_rec16, db_rec = _in_proj_weight_grad(h, dpa, rows, "in_proj_weight_grad_rec")
    dz, dnw, rxw_rec = _input_grad(dpa, dpb, w4, z, dz2, norm_w, dw_rec16, rows)

    small = jnp.concatenate([
        small[ROW_LOSS:ROW_LOSS + 1],
        dz[PAD_ROWS:PAD_ROWS + N_META],
        dnw,
        db_rec.reshape(len(SEGS_REC), D_MODEL), db_mix.reshape(len(SEGS_MIX), D_MODEL),
        dlb, jnp.zeros_like(dlb),
        small[ROW_HG_W:ROW_HG_W + 1], small[ROW_POOL_SCALE:ROW_POOL_SCALE + 1],
        small[ROW_FINAL_W:ROW_FINAL_W + 1],
        jnp.zeros((SMALL_ROWS - ROW_FINAL_W - 1, D_MODEL), F32),
    ], axis=0)
    return dz, (dw_rec, dw_mix, rxw_rec, rxw_mix), (dblob4, rx_blob), small


ANY = pl.BlockSpec(memory_space=pl.ANY)
MESH = pl.DeviceIdType.MESH


def _place():
    x, y, c = lax.axis_index("x"), lax.axis_index("y"), lax.axis_index("c")
    chips = [(1 - x, y), (x, 1 - y), (1 - x, 1 - y)]
    return x, y, c, chips


def _gather_weights(w4, b4, m4):
    hw, hb = w4.shape[1] // 2, b4.shape[1] // 2

    def body(w_in_ref, b_in_ref, m_in_ref, w4_ref, b4_ref, m4_ref, send_sems, recv_sems):
        x, y, c, chips = _place()
        me = 2 * x + y
        sibling = (x, y, 1 - c)

        def half(ref4, chip, rows, which):
            return ref4.at[chip, pl.ds(which * rows, rows), :]

        def copy(k, src, dst, to):
            return pltpu.make_async_remote_copy(src_ref=src, dst_ref=dst, send_sem=send_sems.at[k],
                                                recv_sem=recv_sems.at[k], device_id=to, device_id_type=MESH)

        tensors = [(w4_ref, hw), (b4_ref, hb)]
        sends = []
        for t, (dst4, rows) in enumerate(tensors):
            for j, (cx, cy) in enumerate(chips):
                own = half(dst4, me, rows, c)
                sends.append(copy(6 * t + j, own, own, (cx, cy, c)))
        for j, (cx, cy) in enumerate(chips):
            sends.append(copy(12 + j, m4_ref.at[me], m4_ref.at[me], (cx, cy, c)))
        for cp in sends:
            cp.start()

        passed = []
        for t, (dst4, rows) in enumerate(tensors):
            for j, (cx, cy) in enumerate(chips):
                landed = half(dst4, 2 * cx + cy, rows, c)
                copy(6 * t + j, landed, landed, (cx, cy, c)).wait_recv()
                fwd = copy(6 * t + 3 + j, landed, landed, sibling)
                fwd.start()
                passed.append(fwd)
        for t, (dst4, rows) in enumerate(tensors):
            for j, (cx, cy) in enumerate(chips):
                landed = half(dst4, 2 * cx + cy, rows, 1 - c)
                copy(6 * t + 3 + j, landed, landed, sibling).wait_recv()
        for j, (cx, cy) in enumerate(chips):
            landed = m4_ref.at[2 * cx + cy]
            copy(12 + j, landed, landed, (cx, cy, c)).wait_recv()
        for cp in sends + passed:
            cp.wait_send()

    same = lambda a: jax.ShapeDtypeStruct(a.shape, a.dtype)
    return pl.pallas_call(
        body, name="gather_weights",
        in_specs=[ANY, ANY, ANY], out_specs=[ANY, ANY, ANY],
        out_shape=[same(w4), same(b4), same(m4)],
        input_output_aliases={0: 0, 1: 1, 2: 2},
        scratch_shapes=[pltpu.SemaphoreType.DMA((15,)), pltpu.SemaphoreType.DMA((15,))],
    )(w4, b4, m4)


class _GradExchange:
    def __init__(self, segs, with_blob):
        self.segs = tuple(segs)
        self.with_blob = with_blob

    def landing_w(self):
        return jax.ShapeDtypeStruct((N_DEV, 2, D_MODEL // 2, D_MODEL), BF16)

    def landing_blob(self, blob16):
        return jax.ShapeDtypeStruct((N_DEV, blob16.shape[1] // 2, D_MODEL), BF16)

    def semaphores(self):
        n_send = 2 * len(self.segs) + (2 * N_CHIPS if self.with_blob else 0)
        n_recv = 2 * N_DEV + (N_DEV if self.with_blob else 0)
        return [pltpu.SemaphoreType.DMA((n_send,)), pltpu.SemaphoreType.DMA((n_recv,))]

    def _copies(self, dw_ref, rxw_ref, blob_ref, rxb_ref, send_sems, recv_sems):
        x, y, c = lax.axis_index("x"), lax.axis_index("y"), lax.axis_index("c")
        chip = 2 * x + y
        half = D_MODEL // 2

        def relation(kx, ky, h):
            return (x ^ kx) * 4 + (y ^ ky) * 2 + (c ^ h)

        def copy(src, dst, send_k, recv_k, to):
            return functools.partial(pltpu.make_async_remote_copy, src_ref=src, dst_ref=dst,
                                     send_sem=send_sems.at[send_k], recv_sem=recv_sems.at[recv_k],
                                     device_id=to, device_id_type=MESH)

        sends, recvs = [], []
        for i, s in enumerate(self.segs):
            kx, ky = (s // 2) >> 1, (s // 2) & 1
            for h in range(2):
                r = relation(kx, ky, h)
                sends.append((r != 0, copy(dw_ref.at[i, pl.ds(h * half, half), :], rxw_ref.at[r, s % 2],
                                           2 * i + h, 2 * r + s % 2, (kx, ky, h))))
        for j in range(2):
            mine = [s // 2 for s in self.segs if s % 2 == j]
            if mine:
                cond = functools.reduce(lambda a, b: a | b, [chip == k for k in mine])
                for r in range(1, N_DEV):
                    slot = rxw_ref.at[r, j]
                    recvs.append((cond, copy(slot, slot, 0, 2 * r + j, (x, y, c))))
        if self.with_blob:
            hb = blob_ref.shape[1] // 2
            for k in range(N_CHIPS):
                for h in range(2):
                    r = relation(k >> 1, k & 1, h)
                    sends.append((r != 0, copy(blob_ref.at[k, pl.ds(h * hb, hb), :], rxb_ref.at[r],
                                               2 * len(self.segs) + 2 * k + h, 2 * N_DEV + r, (k >> 1, k & 1, h))))
            for r in range(1, N_DEV):
                slot = rxb_ref.at[r]
                recvs.append((None, copy(slot, slot, 0, 2 * N_DEV + r, (x, y, c))))
        return sends, recvs

    def start(self, *refs):
        sends, _ = self._copies(*refs)
        for cond, make in sends:
            pl.when(cond)(lambda make=make: make().start())

    def wait(self, *refs):
        sends, recvs = self._copies(*refs)
        for cond, make in sends:
            pl.when(cond)(lambda make=make: make().wait_send())
        for cond, make in recvs:
            if cond is None:
                make().wait_recv()
            else:
                pl.when(cond)(lambda make=make: make().wait_recv())


def _sum_landed(own, rx_ref):
    total = own
    for r in range(1, N_DEV):
        total = total + rx_ref[r, 0].astype(F32)
    return total


def _finish_w(dw_rec, dw_mix, rx_rec, rx_mix, place_arr):
    half = D_MODEL // 2
    tm = _tile(half, 256)
    n_rec = len(SEGS_REC)

    def body(place_ref, own_rec_ref, own_mix_ref, rx_rec_ref, rx_mix_ref, out_ref):
        seg = 2 * place_ref[0] + pl.program_id(0)

        @pl.when(seg < n_rec)
        def _():
            out_ref[0] = _sum_landed(own_rec_ref[0], rx_rec_ref)

        @pl.when(seg >= n_rec)
        def _():
            out_ref[0] = _sum_landed(own_mix_ref[0], rx_mix_ref)

    def own_spec(first, count):
        def index(j, i, place_ref):
            seg = 2 * place_ref[0] + j
            return (jnp.clip(seg - first, 0, count - 1), place_ref[1] * (half // tm) + i, 0)
        return pl.BlockSpec((1, tm, D_MODEL), index)

    rx_spec = pl.BlockSpec((N_DEV, 1, tm, D_MODEL), lambda j, i, place_ref: (0, j, i, 0))
    return pl.pallas_call(
        body, name="finish_w",
        grid_spec=pltpu.PrefetchScalarGridSpec(
            num_scalar_prefetch=1, grid=(2, half // tm),
            in_specs=[own_spec(0, n_rec), own_spec(n_rec, len(SEGS_MIX)), rx_spec, rx_spec],
            out_specs=pl.BlockSpec((1, tm, D_MODEL), lambda j, i, place_ref: (place_ref[1], i, j))),
        out_shape=jax.ShapeDtypeStruct((2, half, 2 * D_MODEL), F32),
        compiler_params=_params(("arbitrary", "arbitrary")),
    )(place_arr, dw_rec, dw_mix, rx_rec, rx_mix)


def _finish_blob(dblob4, rx_blob, place_arr):
    n, rows, cols = rx_blob.shape
    tm = _tile(rows, 256)

    def body(place_ref, own_ref, rx_ref, out_ref):
        out_ref[0] = _sum_landed(own_ref[0, 0], rx_ref)

    return pl.pallas_call(
        body, name="finish_blob",
        grid_spec=pltpu.PrefetchScalarGridSpec(
            num_scalar_prefetch=1, grid=(rows // tm,),
            in_specs=[pl.BlockSpec((1, 1, tm, cols), lambda i, place_ref: (place_ref[0], place_ref[1], i, 0)),
                      pl.BlockSpec((n, 1, tm, cols), lambda i, place_ref: (0, 0, i, 0))],
            out_specs=pl.BlockSpec((1, tm, cols), lambda i, place_ref: (place_ref[1], i, 0))),
        out_shape=jax.ShapeDtypeStruct((2, rows, cols), F32),
        compiler_params=_params(("arbitrary",)),
    )(place_arr, dblob4.reshape(N_CHIPS, 2, rows, cols), rx_blob.reshape(n, 1, rows, cols))


def _share_finished(fw2, fb2, slots):
    def body(w_in_ref, b_in_ref, s_in_ref, w_ref, b_ref, s_ref, send_sems, recv_sems):
        x, y, c, _ = _place()
        sibling = (x, y, 1 - c)

        def copy(k, src, dst, to):
            return pltpu.make_async_remote_copy(src_ref=src, dst_ref=dst, send_sem=send_sems.at[k],
                                                recv_sem=recv_sems.at[k], device_id=to, device_id_type=MESH)

        sends = [copy(0, w_ref.at[c], w_ref.at[c], sibling), copy(1, b_ref.at[c], b_ref.at[c], sibling)]
        for r in range(1, N_DEV):
            peer = (x ^ ((r >> 2) & 1), y ^ ((r >> 1) & 1), c ^ (r & 1))
            sends.append(copy(1 + r, s_ref.at[0], s_ref.at[r], peer))
        for cp in sends:
            cp.start()
        landed = [w_ref.at[1 - c], b_ref.at[1 - c]] + [s_ref.at[r] for r in range(1, N_DEV)]
        for k, slot in enumerate(landed):
            copy(k, slot, slot, (x, y, c)).wait_recv()
        for cp in sends:
            cp.wait_send()

    same = lambda a: jax.ShapeDtypeStruct(a.shape, a.dtype)
    n_sem = 2 + N_DEV - 1
    return pl.pallas_call(
        body, name="share_finished",
        in_specs=[ANY, ANY, ANY], out_specs=[ANY, ANY, ANY],
        out_shape=[same(fw2), same(fb2), same(slots)],
        input_output_aliases={0: 0, 1: 1, 2: 2},
        scratch_shapes=[pltpu.SemaphoreType.DMA((n_sem,)), pltpu.SemaphoreType.DMA((n_sem,))],
    )(fw2, fb2, slots)


def _sum_small(slots, lb_logits, me_arr):
    def body(me_ref, slots_ref, lbl_ref, out_ref):
        me = me_ref[0]
        total = slots_ref[me]
        for d in range(1, N_DEV):
            total = total + slots_ref[d ^ me]
        out_ref[...] = total
        out_ref[ROW_LOSS:ROW_LOSS + 1, :] = jnp.broadcast_to(
            jnp.sum(total[ROW_LOSS:ROW_LOSS + 1, :], axis=-1, keepdims=True), (1, D_MODEL))
        lb = _lower_bound(lbl_ref[...])
        g0 = total[ROW_LB:ROW_LB + 1, :] * lb * (1.0 - lb)
        out_ref[ROW_LB:ROW_LB + 1, :] = g0
        out_ref[ROW_LB + 1:ROW_LB + 2, :] = -g0

    return pl.pallas_call(
        body, name="sum_small",
        grid_spec=pltpu.PrefetchScalarGridSpec(
            num_scalar_prefetch=1, grid=(1,),
            in_specs=[pl.BlockSpec((N_DEV, SMALL_ROWS, D_MODEL), lambda i, me_ref: (0, 0, 0)),
                      pl.BlockSpec((2, D_MODEL), lambda i, me_ref: (0, 0))],
            out_specs=pl.BlockSpec((SMALL_ROWS, D_MODEL), lambda i, me_ref: (0, 0))),
        out_shape=jax.ShapeDtypeStruct((SMALL_ROWS, D_MODEL), F32),
        compiler_params=_params(("arbitrary",)),
    )(me_arr, slots, lb_logits)


def _adamw(w, g, m, v):
    rows, cols = w.shape
    tm = _tile(rows, 256, mult=8) if rows % 8 == 0 else rows
    c1 = 1.0 / (1.0 - ADAM_B1 ** ADAM_STEP)
    c2 = 1.0 / (1.0 - ADAM_B2 ** ADAM_STEP)

    def body(w_ref, g_ref, m_ref, v_ref, d_ref, nm_ref, nv_ref):
        gt = g_ref[...]
        nm = ADAM_B1 * m_ref[...] + (1.0 - ADAM_B1) * gt
        nv = ADAM_B2 * v_ref[...] + (1.0 - ADAM_B2) * (gt * gt)
        nm_ref[...] = nm
        nv_ref[...] = nv
        d_ref[...] = -ADAM_LR * ((nm * c1) / (jnp.sqrt(nv * c2) + ADAM_EPS) + ADAM_WD * w_ref[...])

    blk = pl.BlockSpec((tm, cols), lambda i: (i, 0))
    sds = jax.ShapeDtypeStruct((rows, cols), F32)
    return pl.pallas_call(
        body, name="adamw",
        grid=(rows // tm,), in_specs=[blk] * 4, out_specs=[blk] * 3, out_shape=[sds] * 3,
        compiler_params=_params(("arbitrary",)),
    )(w, g, m, v)


def kernel(x, meta_tokens, norm_w, w_in, b_in, lb_logits, hg_norm_w, pool_w, pool_scale, w_down_hg, w_down_pool, w_out, final_norm_w, loss_target, m_meta_tokens, m_norm_w, m_w_in, m_b_in, m_lb_logits, m_hg_norm_w, m_pool_w, m_pool_scale, m_w_down_hg, m_w_down_pool, m_w_out, m_final_norm_w, v_meta_tokens, v_norm_w, v_w_in, v_b_in, v_lb_logits, v_hg_norm_w, v_pool_w, v_pool_scale, v_w_down_hg, v_w_down_pool, v_w_out, v_final_norm_w):
    seq = x.shape[1]
    xi, yi, ci = lax.axis_index("x"), lax.axis_index("y"), lax.axis_index("c")
    chip = 2 * xi + yi
    place_arr = jnp.stack([chip, ci]).astype(jnp.int32)
    me_arr = jnp.reshape(4 * xi + 2 * yi + ci, (1,)).astype(jnp.int32)
    q = D_MODEL // N_CHIPS

    def blob_of(wdh, wdp, wo, pw):
        return jnp.concatenate([wdh[0], wdp[0], wo[0], pw[0].reshape(-1, D_MODEL)], axis=0)

    def in_every_slot(a):
        return jnp.broadcast_to(a[None], (N_CHIPS,) + a.shape)

    w4, blob4, meta4 = _gather_weights(
        in_every_slot(w_in[0].astype(BF16)),
        in_every_slot(blob_of(w_down_hg, w_down_pool, w_out, pool_w).astype(BF16)),
        in_every_slot(meta_tokens))
    meta_full = meta4.transpose(1, 0, 2).reshape(N_META, D_MODEL)

    z = jnp.concatenate([jnp.zeros((PAD_ROWS, D_MODEL), F32), meta_full, x[0]], axis=0)
    tgt = jnp.concatenate([jnp.zeros((FIRST_TOKEN_ROW, D_MODEL), F32), loss_target[0]], axis=0)
    fw2 = final_norm_w.reshape(1, D_MODEL)
    dz, w_parts, blob_parts, small = _local_step(
        z, tgt, w4, blob4, norm_w, b_in, lb_logits, hg_norm_w, pool_scale, fw2)
    grad_x = dz[FIRST_TOKEN_ROW:][None]

    fin_w = _finish_w(*w_parts, place_arr)
    fin_b = _finish_blob(*blob_parts, place_arr)
    gw2, gb2, slots = _share_finished(fin_w, fin_b, jnp.broadcast_to(small[None], (N_DEV,) + small.shape))
    tot = _sum_small(slots, lb_logits, me_arr)
    g_w_in = gw2.reshape(D_MODEL, 2 * D_MODEL)
    g_blob = gb2.reshape(-1, D_MODEL)

    d_win, nm_win, nv_win = _adamw(w_in[0], g_w_in, m_w_in[0], v_w_in[0])
    d_blob, nm_blob, nv_blob = _adamw(
        blob_of(w_down_hg, w_down_pool, w_out, pool_w), g_blob,
        blob_of(m_w_down_hg, m_w_down_pool, m_w_out, m_pool_w),
        blob_of(v_w_down_hg, v_w_down_pool, v_w_out, v_pool_w))
    g_meta = lax.dynamic_slice_in_dim(tot[ROW_META:ROW_META + N_META], chip * q, q, axis=1)
    d_meta, nm_meta, nv_meta = _adamw(meta_tokens, g_meta, m_meta_tokens, v_meta_tokens)

    def rows_of(nw, bi, lbl, hg, ps, fw):
        return jnp.concatenate([nw, bi.reshape(N_SEG, D_MODEL), lbl, hg, ps, fw.reshape(1, D_MODEL),
                                jnp.zeros((2, D_MODEL), F32)], axis=0)

    g_rows = jnp.concatenate([tot[ROW_NORM_W:ROW_FINAL_W + 1], jnp.zeros((2, D_MODEL), F32)], axis=0)
    d_rows, nm_rows, nv_rows = _adamw(
        rows_of(norm_w, b_in, lb_logits, hg_norm_w, pool_scale, final_norm_w), g_rows,
        rows_of(m_norm_w, m_b_in, m_lb_logits, m_hg_norm_w, m_pool_scale, m_final_norm_w),
        rows_of(v_norm_w, v_b_in, v_lb_logits, v_hg_norm_w, v_pool_scale, v_final_norm_w))

    def unblob(b):
        return (b[0:q][None], b[q:2 * q][None], b[2 * q:3 * q][None], b[3 * q:].reshape(pool_w.shape))

    def unrows(r):
        o = ROW_NORM_W
        return dict(norm_w=r[ROW_NORM_W - o:ROW_B_IN - o], b_in=r[ROW_B_IN - o:ROW_LB - o].reshape(1, -1),
                    lb_logits=r[ROW_LB - o:ROW_HG_W - o], hg_norm_w=r[ROW_HG_W - o:ROW_POOL_SCALE - o],
                    pool_scale=r[ROW_POOL_SCALE - o:ROW_FINAL_W - o], final_norm_w=r[ROW_FINAL_W - o])

    def leaves(meta_part, rows_part, win_part, blob_part):
        r = unrows(rows_part)
        wdh, wdp, wo, pw = unblob(blob_part)
        return [meta_part, r["norm_w"], win_part[None], r["b_in"], r["lb_logits"], r["hg_norm_w"], pw,
                r["pool_scale"], wdh, wdp, wo, r["final_norm_w"]]

    loss = tot[ROW_LOSS, 0]
    return (loss, grad_x,
            *leaves(g_meta, g_rows, g_w_in, g_blob),
            *leaves(d_meta, d_rows, d_win, d_blob),
            *leaves(nm_meta, nm_rows, nm_win, nm_blob),
            *leaves(nv_meta, nv_rows, nv_win, nv_blob))
```

```python
import functools

import numpy as np
import jax
import jax.numpy as jnp
from jax import lax
from jax.experimental import pallas as pl
from jax.experimental.pallas import tpu as pltpu

F32 = jnp.float32
BF16 = jnp.bfloat16

D_MODEL = 1024
N_SEG = 8
N_HEADS = 8
HEAD_DIM = 128
CHUNK = 64
N_META = 16
PAD_ROWS = CHUNK - N_META
FIRST_TOKEN_ROW = CHUNK
LEVELS = (32, 16, 8, 4, 2, 1)
N_EXP = 2 + len(LEVELS)
POOL_WINDOWS = (2, 4, 8, 16)
POOL_GDIM = D_MODEL // len(POOL_WINDOWS)
HALO = 16
LOCAL_UNROLL = 13
HEADS_PER_STEP = 4
EPS = 1e-6
N_CHIPS = 4
N_DEV = 8
SEGS_REC = (0, 1, 2)
SEGS_MIX = (3, 4, 5, 6, 7)

ADAM_LR = 0.001
ADAM_B1 = 0.9
ADAM_B2 = 0.999
ADAM_EPS = 1e-08
ADAM_WD = 0.01
ADAM_STEP = 10

VMEM_LIMIT_BYTES = 56 * 1024 * 1024

ROW_LOSS = 0
ROW_META = 1
ROW_NORM_W = ROW_META + N_META
ROW_B_IN = ROW_NORM_W + 1
ROW_LB = ROW_B_IN + N_SEG
ROW_HG_W = ROW_LB + 2
ROW_POOL_SCALE = ROW_HG_W + 1
ROW_FINAL_W = ROW_POOL_SCALE + 1
SMALL_ROWS = 32


def _tile(total, cap, mult=16):
    best = None
    for t in range(mult, min(total, cap) + 1, mult):
        if total % t == 0:
            best = t
    assert best is not None, (total, cap, mult)
    return best


def _params(sem=None):
    return pltpu.CompilerParams(dimension_semantics=sem, vmem_limit_bytes=VMEM_LIMIT_BYTES)


def _dot(a, b):
    return jnp.dot(a, b, preferred_element_type=F32)


def _dot_nt(a, b):
    return lax.dot_general(a, b, (((1,), (1,)), ((), ())), preferred_element_type=F32)


def _dot_tn(a, b):
    return lax.dot_general(a, b, (((0,), (0,)), ((), ())), preferred_element_type=F32)


def _sigmoid_pair(x):
    t = jnp.exp(-jnp.abs(x))
    r = 1.0 / (1.0 + t)
    pos = x >= 0
    return jnp.where(pos, r, t * r), jnp.where(pos, t * r, r)


def _exponent_matrix():
    t = np.arange(CHUNK)[:, None]
    j = np.arange(CHUNK)[None, :]
    blocks = [j <= t, j > t]
    for m in LEVELS:
        rho = (t // (2 * m)) * (2 * m) + m
        upper = (t >= rho) & (j > rho) & (j <= t)
        lower = (t < rho) & (j > t) & (j <= rho)
        blocks.append(upper | lower)
    return np.concatenate(blocks, axis=0).astype(np.float32)


def _pair_masks():
    t = np.arange(CHUNK)[:, None]
    s = np.arange(CHUNK)[None, :]
    masks = [t == s]
    for m in LEVELS:
        same = (t // (2 * m)) == (s // (2 * m))
        masks.append(same & ((t % (2 * m)) >= m) & ((s % (2 * m)) < m))
    return np.stack(masks).astype(np.float32)


LEVEL_PAIRS = ((0, 1), (2, 3), (4, 5), (6, None))


def _paired_masks():
    m = _pair_masks()
    zero = np.zeros_like(m[0])
    return np.stack([np.concatenate([m[a], zero if b is None else m[b]], axis=1) for a, b in LEVEL_PAIRS])


def _split3(x):
    hi = x.astype(BF16)
    r = x - hi.astype(F32)
    mid = r.astype(BF16)
    lo = (r - mid.astype(F32)).astype(BF16)
    return hi, mid, lo


def _chunk_forward(q, fz, lb, valid, wexp, masks):
    sg, sn = _sigmoid_pair(fz)
    f = lb + (1.0 - lb) * sg
    g = jnp.where(valid, jnp.log(f), 0.0)
    kk = jnp.where(valid, (1.0 - lb) * sn, 0.0)
    q = jnp.where(valid, q, 0.0)
    e = jnp.exp(_dot(wexp, jnp.concatenate(_split3(g), axis=0)))
    e_b = e[0:CHUNK]
    e_c = e[CHUNK:2 * CHUNK]
    a = masks[0] * _dot_nt(q.astype(BF16), kk.astype(BF16))
    qm, km = [], []
    for l in range(len(LEVELS)):
        e_m = e[(2 + l) * CHUNK:(3 + l) * CHUNK]
        qm.append(q * e_m)
        km.append(kk * e_m)
        a = a + masks[1 + l] * _dot_nt(qm[l].astype(BF16), km[l].astype(BF16))
    return dict(sg=sg, sn=sn, f=f, kk=kk, q=q, e=e, e_b=e_b, e_c=e_c, a=a, qm=qm, km=km)


def _lower_bound(lbl):
    return 1.0 / (1.0 + jnp.exp(lbl[1:2, :] - lbl[0:1, :]))


def _in_proj(z, norm_w, w4, b_in, seg_order, rows):
    tm = _tile(rows, 1040)
    nt = rows // tm
    gather = _ShardGather(w4.shape[1])

    def body(order_ref, z_ref, nw_ref, b_ref, w_in_ref, h_ref, p_ref, w4_ref,
             h_all, w_buf, w_sem, send_sems, recv_sems):
        kk, i = pl.program_id(0), pl.program_id(1)

        @pl.when((kk == 0) & (i == 0))
        def _():
            gather.start(w4_ref, send_sems, recv_sems)

        @pl.when(kk == 0)
        def _():
            zt = z_ref[...]
            rstd = lax.rsqrt(jnp.mean(zt * zt, axis=-1, keepdims=True) + EPS)
            h = (zt * rstd * nw_ref[...]).astype(BF16)
            h_all[pl.ds(pl.multiple_of(i * tm, 16), tm), :] = h
            h_ref[...] = h

        for j in range(N_CHIPS - 1):
            @pl.when((kk == 2 + 2 * j) & (i == 0))
            def _(j=j):
                gather.arrive(j, w4_ref, send_sems, recv_sems)

        @pl.when(i == 0)
        def _():
            seg = order_ref[kk]
            cp = pltpu.make_async_copy(
                w4_ref.at[seg // 2, :, pl.ds(pl.multiple_of((seg % 2) * D_MODEL, D_MODEL), D_MODEL)], w_buf, w_sem)
            cp.start()
            cp.wait()

        p_ref[0] = _dot(h_all[pl.ds(pl.multiple_of(i * tm, 16), tm), :], w_buf[...]) + b_ref[...]

        @pl.when((kk == N_SEG - 1) & (i == nt - 1))
        def _():
            gather.finish(w4_ref, send_sems, recv_sems)

    first_pass = lambda kk, i, order_ref: (jnp.where(kk == 0, i, nt - 1), 0)
    return pl.pallas_call(
        body, name="in_proj",
        grid_spec=pltpu.PrefetchScalarGridSpec(
            num_scalar_prefetch=1, grid=(N_SEG, nt),
            in_specs=[
                pl.BlockSpec((tm, D_MODEL), first_pass),
                pl.BlockSpec((1, D_MODEL), lambda kk, i, order_ref: (0, 0)),
                pl.BlockSpec((1, D_MODEL), lambda kk, i, order_ref: (0, order_ref[kk])),
                ANY,
            ],
            out_specs=[
                pl.BlockSpec((tm, D_MODEL), first_pass),
                pl.BlockSpec((1, tm, D_MODEL), lambda kk, i, order_ref: (order_ref[kk], i, 0)),
                ANY,
            ],
            scratch_shapes=[
                pltpu.VMEM((rows, D_MODEL), BF16),
                pltpu.VMEM((D_MODEL, D_MODEL), BF16),
                pltpu.SemaphoreType.DMA,
            ] + gather.semaphores()),
        out_shape=[
            jax.ShapeDtypeStruct((rows, D_MODEL), BF16),
            jax.ShapeDtypeStruct((N_SEG, rows, D_MODEL), F32),
            jax.ShapeDtypeStruct(w4.shape, w4.dtype),
        ],
        input_output_aliases={4: 2},
        compiler_params=_params(("arbitrary", "arbitrary")),
    )(seg_order, z, norm_w, b_in, w4)


def _hgrn_forward(p3, lb_logits, wexp2, masks2, blob4, rows):
    n_chunks = rows // CHUNK
    cpb = _tile(n_chunks, 13, mult=1)
    rb_rows = cpb * CHUNK
    n_rb = n_chunks // cpb
    lanes = cpb * HEAD_DIM
    gather = _ShardGather(blob4.shape[1])

    def body(q_ref, fz_ref, v_ref, lbl_ref, wexp_ref, mask_ref, b_in_ref, o_ref, s_ref, e16_ref, a2_ref, b4_ref,
             st_ref, e_ref, u_ref, q_s, kk_s, v_s, qb_s, oi_s, send_sems, recv_sems):
        rb = pl.program_id(1)

        @pl.when((pl.program_id(0) == 0) & (rb == 0))
        def _():
            gather.start(b4_ref, send_sems, recv_sems)

        @pl.when(rb == 0)
        def _():
            st_ref[...] = jnp.zeros_like(st_ref)

        lb = _lower_bound(lbl_ref[...])
        row = rb * rb_rows + lax.broadcasted_iota(jnp.int32, (rb_rows, 1), 0)
        valid = row >= PAD_ROWS
        sg, sn = _sigmoid_pair(fz_ref[0])
        g = jnp.where(valid, jnp.log(lb + (1.0 - lb) * sg), 0.0)
        kk_s[...] = jnp.where(valid, (1.0 - lb) * sn, 0.0)
        q_s[...] = jnp.where(valid, q_ref[0], 0.0)
        v_s[...] = jnp.where(valid, v_ref[0], 0.0).astype(BF16)
        hi = g.astype(BF16)
        mid = (g - hi.astype(F32)).astype(BF16)
        g2 = jnp.concatenate(
            [jnp.concatenate([hi[b * CHUNK:(b + 1) * CHUNK], mid[b * CHUNK:(b + 1) * CHUNK]], axis=0)
             for b in range(cpb)], axis=1)
        e_ref[...] = jnp.exp(_dot(wexp_ref[...], g2))
        e16_ref[0, 0] = e_ref[...].astype(BF16)

        zeros16 = jnp.zeros((CHUNK, HEAD_DIM), BF16)

        def local(b, carry):
            r0 = pl.multiple_of(b * CHUNK, CHUNK)
            l0 = pl.multiple_of(b * HEAD_DIM, HEAD_DIM)
            q = q_s[pl.ds(r0, CHUNK), :]
            kk = kk_s[pl.ds(r0, CHUNK), :]
            v16 = v_s[pl.ds(r0, CHUNK), :]

            def scaled(entry):
                if entry == 0:
                    return q.astype(BF16), kk.astype(BF16)
                e_m = e_ref[(1 + entry) * CHUNK:(2 + entry) * CHUNK, pl.ds(l0, HEAD_DIM)]
                return (q * e_m).astype(BF16), (kk * e_m).astype(BF16)

            a2 = jnp.zeros((CHUNK, 2 * CHUNK), F32)
            for p, (ea, eb) in enumerate(LEVEL_PAIRS):
                qa, ka = scaled(ea)
                if eb is None:
                    prod = _dot_nt(qa, jnp.concatenate([ka, zeros16], axis=0))
                else:
                    qb_, kb_ = scaled(eb)
                    rhs = jnp.concatenate([jnp.concatenate([ka, zeros16], axis=1),
                                           jnp.concatenate([zeros16, kb_], axis=1)], axis=0)
                    prod = _dot_nt(jnp.concatenate([qa, qb_], axis=1), rhs)
                a2 = a2 + mask_ref[p] * prod
            a2_16 = a2.astype(BF16)
            a2_ref[pl.ds(r0, CHUNK), :] = a2_16
            oi_s[pl.ds(r0, CHUNK), :] = _dot(a2_16, jnp.concatenate([v16, v16], axis=0))
            e_b = e_ref[0:CHUNK, pl.ds(l0, HEAD_DIM)]
            e_c = e_ref[CHUNK:2 * CHUNK, pl.ds(l0, HEAD_DIM)]
            qb_s[pl.ds(r0, CHUNK), :] = (q * e_b).astype(BF16)
            u_ref[b] = _dot_tn(v16, (kk * e_c).astype(BF16))
            return carry

        lax.fori_loop(0, cpb, local, 0, unroll=LOCAL_UNROLL)

        def recur(b, st):
            l0 = pl.multiple_of(b * HEAD_DIM, HEAD_DIM)
            s_ref[0, b] = st
            return st * e_ref[CHUNK - 1:CHUNK, pl.ds(l0, HEAD_DIM)] + u_ref[b]

        st_ref[...] = lax.fori_loop(0, cpb, recur, st_ref[...])

        def inter(b, carry):
            r0 = pl.multiple_of(b * CHUNK, CHUNK)
            o_ref[pl.ds(r0, CHUNK), :] = oi_s[pl.ds(r0, CHUNK), :] + _dot_nt(
                qb_s[pl.ds(r0, CHUNK), :], s_ref[0, b].astype(BF16))
            return carry

        lax.fori_loop(0, cpb, inter, 0, unroll=LOCAL_UNROLL)

        @pl.when((pl.program_id(0) == N_HEADS - 1) & (rb == n_rb - 1))
        def _():
            for j in range(N_CHIPS - 1):
                gather.arrive(j, b4_ref, send_sems, recv_sems)
            gather.finish(b4_ref, send_sems, recv_sems)

    head_block = lambda seg: pl.BlockSpec((1, rb_rows, HEAD_DIM), lambda h, r: (seg, r, h))
    return pl.pallas_call(
        body, name="hgrn_forward",
        grid=(N_HEADS, n_rb),
        in_specs=[
            head_block(0), head_block(1), head_block(2),
            pl.BlockSpec((2, HEAD_DIM), lambda h, r: (0, h)),
            pl.BlockSpec((N_EXP * CHUNK, 2 * CHUNK), lambda h, r: (0, 0)),
            pl.BlockSpec((len(LEVEL_PAIRS), CHUNK, 2 * CHUNK), lambda h, r: (0, 0, 0)),
            ANY,
        ],
        out_specs=[
            pl.BlockSpec((rb_rows, HEAD_DIM), lambda h, r: (r, h)),
            pl.BlockSpec((1, cpb, HEAD_DIM, HEAD_DIM), lambda h, r: (h, r, 0, 0)),
            pl.BlockSpec((1, 1, N_EXP * CHUNK, lanes), lambda h, r: (h, r, 0, 0)),
            pl.BlockSpec((rb_rows, HEAD_DIM), lambda h, r: (r, h)),
            ANY,
        ],
        out_shape=[
            jax.ShapeDtypeStruct((rows, D_MODEL), F32),
            jax.ShapeDtypeStruct((N_HEADS, n_chunks, HEAD_DIM, HEAD_DIM), F32),
            jax.ShapeDtypeStruct((N_HEADS, n_rb, N_EXP * CHUNK, lanes), BF16),
            jax.ShapeDtypeStruct((rows, D_MODEL), BF16),
            jax.ShapeDtypeStruct(blob4.shape, blob4.dtype),
        ],
        input_output_aliases={6: 4},
        scratch_shapes=[
            pltpu.VMEM((HEAD_DIM, HEAD_DIM), F32),
            pltpu.VMEM((N_EXP * CHUNK, lanes), F32),
            pltpu.VMEM((cpb, HEAD_DIM, HEAD_DIM), F32),
            pltpu.VMEM((rb_rows, HEAD_DIM), F32),
            pltpu.VMEM((rb_rows, HEAD_DIM), F32),
            pltpu.VMEM((rb_rows, HEAD_DIM), BF16),
            pltpu.VMEM((rb_rows, HEAD_DIM), BF16),
            pltpu.VMEM((rb_rows, HEAD_DIM), F32),
        ] + gather.semaphores(),
        compiler_params=_params(("arbitrary", "arbitrary")),
    )(p3, p3, p3, lb_logits, wexp2, masks2, blob4)


def _hgrn_forward_old(p3, lb_logits, wexp3, masks, rows):
    n_chunks = rows // CHUNK
    cpb = _tile(n_chunks, 13, mult=1)
    rb_rows = cpb * CHUNK
    hps = HEADS_PER_STEP
    width = hps * HEAD_DIM

    def body(q_ref, fz_ref, v_ref, lbl_ref, wexp_ref, mask_ref, o_ref, s_ref, st_ref):
        rb = pl.program_id(1)

        @pl.when(rb == 0)
        def _():
            st_ref[...] = jnp.zeros_like(st_ref)

        lb_all = _lower_bound(lbl_ref[...])
        wexp = wexp_ref[...]
        masks = mask_ref[...]

        def chunk(c, carry):
            r0 = pl.multiple_of(c * CHUNK, CHUNK)
            row = rb * rb_rows + r0 + lax.broadcasted_iota(jnp.int32, (CHUNK, 1), 0)
            valid = row >= PAD_ROWS
            q_all = q_ref[0, pl.ds(r0, CHUNK), :]
            fz_all = fz_ref[0, pl.ds(r0, CHUNK), :]
            v_all = jnp.where(valid, v_ref[0, pl.ds(r0, CHUNK), :], 0.0).astype(BF16)
            st_all = [st_ref[j] for j in range(hps)]
            o_all, st_new = [], []
            for j in range(hps):
                cols = slice(j * HEAD_DIM, (j + 1) * HEAD_DIM)
                cf = _chunk_forward(q_all[:, cols], fz_all[:, cols], lb_all[:, cols], valid, wexp, masks)
                v16 = v_all[:, cols]
                o = _dot_nt((cf["q"] * cf["e_b"]).astype(BF16), st_all[j].astype(BF16))
                o_all.append(o + _dot(cf["a"].astype(BF16), v16))
                kc16 = (cf["kk"] * cf["e_c"]).astype(BF16)
                st_new.append(st_all[j] * cf["e_b"][CHUNK - 1:CHUNK, :] + _dot_tn(v16, kc16))
            o_ref[pl.ds(r0, CHUNK), :] = jnp.concatenate(o_all, axis=1)
            for j in range(hps):
                s_ref[j, c] = st_all[j]
                st_ref[j] = st_new[j]
            return carry

        lax.fori_loop(0, cpb, chunk, 0)

    head_block = lambda seg: pl.BlockSpec((1, rb_rows, width), lambda h, r: (seg, r, h))
    return pl.pallas_call(
        body, name="hgrn_forward",
        grid=(N_HEADS // hps, n_chunks // cpb),
        in_specs=[
            head_block(0), head_block(1), head_block(2),
            pl.BlockSpec((2, width), lambda h, r: (0, h)),
            pl.BlockSpec((N_EXP * CHUNK, 3 * CHUNK), lambda h, r: (0, 0)),
            pl.BlockSpec((1 + len(LEVELS), CHUNK, CHUNK), lambda h, r: (0, 0, 0)),
        ],
        out_specs=[
            pl.BlockSpec((rb_rows, width), lambda h, r: (r, h)),
            pl.BlockSpec((hps, cpb, HEAD_DIM, HEAD_DIM), lambda h, r: (h, r, 0, 0)),
        ],
        out_shape=[
            jax.ShapeDtypeStruct((rows, D_MODEL), F32),
            jax.ShapeDtypeStruct((N_HEADS, n_chunks, HEAD_DIM, HEAD_DIM), F32),
        ],
        scratch_shapes=[pltpu.VMEM((hps, HEAD_DIM, HEAD_DIM), F32)],
        compiler_params=_params(("arbitrary", "arbitrary")),
    )(p3, p3, p3, lb_logits, wexp3, masks)


def _hgrn_backward(p3, d_o, states, e16, a2, lb_logits, wexp_t, masks2, dw16, blob16, rows):
    n_chunks = rows // CHUNK
    cpb = _tile(n_chunks, 13, mult=1)
    rb_rows = cpb * CHUNK
    n_rb = n_chunks // cpb
    lanes = cpb * HEAD_DIM
    exchange = _GradExchange(SEGS_MIX, with_blob=True)

    def body(q_ref, fz_ref, v_ref, do_ref, s_ref, e_ref, a2_ref, lbl_ref, wexpt_ref, mask_ref, dw_ref, blob_ref,
             dp_ref, dlb_ref, rxw_ref, rxb_ref,
             dst_ref, g_ref, dsn_ref, q_s, kk_s, v_s, do_s, dq_s, dkk_s, dg_s, dx_s, send_sems, recv_sems):
        step = pl.program_id(1)
        rb = n_rb - 1 - step

        @pl.when((pl.program_id(0) == 0) & (step == 0))
        def _():
            exchange.start(dw_ref, rxw_ref, blob_ref, rxb_ref, send_sems, recv_sems)

        @pl.when(step == 0)
        def _():
            dst_ref[...] = jnp.zeros_like(dst_ref)
            dlb_ref[...] = jnp.zeros_like(dlb_ref)

        lb = _lower_bound(lbl_ref[...])
        row = rb * rb_rows + lax.broadcasted_iota(jnp.int32, (rb_rows, 1), 0)
        valid = row >= PAD_ROWS
        sg, sn = _sigmoid_pair(fz_ref[0])
        f = lb + (1.0 - lb) * sg
        g = jnp.where(valid, jnp.log(f), 0.0)
        kk_s[...] = jnp.where(valid, (1.0 - lb) * sn, 0.0)
        q_s[...] = jnp.where(valid, q_ref[0], 0.0)
        v_s[...] = jnp.where(valid, v_ref[0], 0.0).astype(BF16)
        do_s[...] = do_ref[...].astype(BF16)
        e_last_all = jnp.exp(jnp.concatenate(
            [jnp.sum(g[b * CHUNK:(b + 1) * CHUNK], axis=0, keepdims=True) for b in range(cpb)], axis=0))
        last_row = lax.broadcasted_iota(jnp.int32, (CHUNK, 1), 0) == CHUNK - 1
        zeros16 = jnp.zeros((CHUNK, HEAD_DIM), BF16)

        def factor(block, l0):
            return e_ref[0, 0, block * CHUNK:(block + 1) * CHUNK, pl.ds(l0, HEAD_DIM)].astype(F32)

        def contribution(b, carry):
            r0 = pl.multiple_of(b * CHUNK, CHUNK)
            l0 = pl.multiple_of(b * HEAD_DIM, HEAD_DIM)
            qb16 = (q_s[pl.ds(r0, CHUNK), :] * factor(0, l0)).astype(BF16)
            g_ref[b] = _dot_tn(do_s[pl.ds(r0, CHUNK), :], qb16)
            return carry

        lax.fori_loop(0, cpb, contribution, 0, unroll=LOCAL_UNROLL)

        cur = dst_ref[...]
        for b in reversed(range(cpb)):
            dsn_ref[b] = cur
            cur = cur * e_last_all[b:b + 1, :] + g_ref[b]
        dst_ref[...] = cur

        def local(b, carry):
            r0 = pl.multiple_of(b * CHUNK, CHUNK)
            l0 = pl.multiple_of(b * HEAD_DIM, HEAD_DIM)
            q = q_s[pl.ds(r0, CHUNK), :]
            kk = kk_s[pl.ds(r0, CHUNK), :]
            v16 = v_s[pl.ds(r0, CHUNK), :]
            do16 = do_s[pl.ds(r0, CHUNK), :]
            st = s_ref[0, b]
            dsn = dsn_ref[b]
            dsn16 = dsn.astype(BF16)
            e_b, e_c = factor(0, l0), factor(1, l0)
            qb, kc = q * e_b, kk * e_c

            t = _dot_tn(a2_ref[pl.ds(r0, CHUNK), :], do16)
            dv = t[0:CHUNK] + t[CHUNK:2 * CHUNK] + _dot_nt(kc.astype(BF16), dsn16)
            dp_ref[2, pl.ds(r0, CHUNK), :] = dv.astype(BF16)
            da2 = _dot_nt(do16, jnp.concatenate([v16, v16], axis=0))
            dqb = _dot(do16, st.astype(BF16))
            dkc = _dot(v16, dsn16)
            de = jnp.sum(dsn * st, axis=0, keepdims=True) * e_b[CHUNK - 1:CHUNK, :]
            dq = e_b * dqb
            dkk = e_c * dkc
            dx_s[0:CHUNK, pl.ds(l0, HEAD_DIM)] = (qb * dqb + jnp.where(last_row, de, 0.0)).astype(BF16)
            dx_s[CHUNK:2 * CHUNK, pl.ds(l0, HEAD_DIM)] = (kc * dkc).astype(BF16)

            def scaled(entry):
                if entry == 0:
                    return q, kk, None
                e_m = factor(1 + entry, l0)
                return q * e_m, kk * e_m, e_m

            for p, (ea, eb) in enumerate(LEVEL_PAIRS):
                dm = (mask_ref[p] * da2).astype(BF16)
                qa, ka, e_a = scaled(ea)
                if eb is None:
                    lhs_q = jnp.concatenate([qa.astype(BF16), zeros16], axis=1)
                    rhs_k = jnp.concatenate([jnp.concatenate([ka.astype(BF16), zeros16], axis=1),
                                             jnp.concatenate([zeros16, zeros16], axis=1)], axis=0)
                else:
                    qb_, kb_, e_bb = scaled(eb)
                    lhs_q = jnp.concatenate([qa.astype(BF16), qb_.astype(BF16)], axis=1)
                    rhs_k = jnp.concatenate([jnp.concatenate([ka.astype(BF16), zeros16], axis=1),
                                             jnp.concatenate([zeros16, kb_.astype(BF16)], axis=1)], axis=0)
                dq2 = _dot(dm, rhs_k)
                dk2 = _dot_tn(dm, lhs_q)
                parts = [(ea, qa, ka, e_a, dq2[:, :HEAD_DIM], dk2[0:CHUNK, :HEAD_DIM])]
                if eb is not None:
                    parts.append((eb, qb_, kb_, e_bb, dq2[:, HEAD_DIM:], dk2[CHUNK:2 * CHUNK, HEAD_DIM:]))
                for entry, q_m, k_m, e_m, dq_m, dk_m in parts:
                    if entry == 0:
                        dq = dq + dq_m
                        dkk = dkk + dk_m
                    else:
                        dq = dq + e_m * dq_m
                        dkk = dkk + e_m * dk_m
                        dx_s[(1 + entry) * CHUNK:(2 + entry) * CHUNK, pl.ds(l0, HEAD_DIM)] = (
                            q_m * dq_m + k_m * dk_m).astype(BF16)
            dq_s[pl.ds(r0, CHUNK), :] = dq
            dkk_s[pl.ds(r0, CHUNK), :] = dkk
            return carry

        lax.fori_loop(0, cpb, local, 0, unroll=LOCAL_UNROLL)

        dg_all = _dot(wexpt_ref[...], dx_s[...])
        for b in range(cpb):
            dg_s[b * CHUNK:(b + 1) * CHUNK, :] = dg_all[:, b * HEAD_DIM:(b + 1) * HEAD_DIM]
        t = jnp.where(valid, dg_s[...] / f - dkk_s[...], 0.0)
        dlb_ref[...] += jnp.sum(sn * t, axis=0, keepdims=True)
        dp_ref[0] = jnp.where(valid, dq_s[...], 0.0).astype(BF16)
        dp_ref[1] = ((1.0 - lb) * sg * sn * t).astype(BF16)

        @pl.when((pl.program_id(0) == N_HEADS - 1) & (step == n_rb - 1))
        def _():
            exchange.wait(dw_ref, rxw_ref, blob_ref, rxb_ref, send_sems, recv_sems)

    head_block = lambda seg: pl.BlockSpec((1, rb_rows, HEAD_DIM), lambda h, s: (seg, n_rb - 1 - s, h))
    row_block = pl.BlockSpec((rb_rows, HEAD_DIM), lambda h, s: (n_rb - 1 - s, h))
    return pl.pallas_call(
        body, name="hgrn_backward",
        grid=(N_HEADS, n_rb),
        in_specs=[
            head_block(0), head_block(1), head_block(2),
            row_block,
            pl.BlockSpec((1, cpb, HEAD_DIM, HEAD_DIM), lambda h, s: (h, n_rb - 1 - s, 0, 0)),
            pl.BlockSpec((1, 1, N_EXP * CHUNK, lanes), lambda h, s: (h, n_rb - 1 - s, 0, 0)),
            row_block,
            pl.BlockSpec((2, HEAD_DIM), lambda h, s: (0, h)),
            pl.BlockSpec((CHUNK, N_EXP * CHUNK), lambda h, s: (0, 0)),
            pl.BlockSpec((len(LEVEL_PAIRS), CHUNK, 2 * CHUNK), lambda h, s: (0, 0, 0)),
            ANY, ANY,
        ],
        out_specs=[
            pl.BlockSpec((3, rb_rows, HEAD_DIM), lambda h, s: (0, n_rb - 1 - s, h)),
            pl.BlockSpec((1, HEAD_DIM), lambda h, s: (0, h)),
            ANY, ANY,
        ],
        out_shape=[
            jax.ShapeDtypeStruct((3, rows, D_MODEL), BF16),
            jax.ShapeDtypeStruct((1, D_MODEL), F32),
            exchange.landing_w(), exchange.landing_blob(blob16),
        ],
        scratch_shapes=[
            pltpu.VMEM((HEAD_DIM, HEAD_DIM), F32),
            pltpu.VMEM((cpb, HEAD_DIM, HEAD_DIM), F32),
            pltpu.VMEM((cpb, HEAD_DIM, HEAD_DIM), F32),
            pltpu.VMEM((rb_rows, HEAD_DIM), F32),
            pltpu.VMEM((rb_rows, HEAD_DIM), F32),
            pltpu.VMEM((rb_rows, HEAD_DIM), BF16),
            pltpu.VMEM((rb_rows, HEAD_DIM), BF16),
            pltpu.VMEM((rb_rows, HEAD_DIM), F32),
            pltpu.VMEM((rb_rows, HEAD_DIM), F32),
            pltpu.VMEM((rb_rows, HEAD_DIM), F32),
            pltpu.VMEM((N_EXP * CHUNK, lanes), BF16),
        ] + exchange.semaphores(),
        compiler_params=_params(("arbitrary", "arbitrary")),
    )(p3, p3, p3, d_o, states, e16, a2, lb_logits, wexp_t, masks2, dw16, blob16)


def _hgrn_backward_old(p3, d_o, states, lb_logits, wexp3, wexp_t2, masks, dw16, blob16, rows):
    n_chunks = rows // CHUNK
    cpb = _tile(n_chunks, 13, mult=1)
    rb_rows = cpb * CHUNK
    n_rb = n_chunks // cpb
    hps = HEADS_PER_STEP
    width = hps * HEAD_DIM
    n_hb = N_HEADS // hps
    exchange = _GradExchange(SEGS_MIX, with_blob=True)

    def body(q_ref, fz_ref, v_ref, do_ref, s_ref, lbl_ref, wexp_ref, wexpt_ref, mask_ref, dw_ref, blob_ref,
             dp_ref, dlb_ref, rxw_ref, rxb_ref, dst_ref, send_sems, recv_sems):
        step = pl.program_id(1)
        rb = n_rb - 1 - step

        @pl.when((pl.program_id(0) == 0) & (step == 0))
        def _():
            exchange.start(dw_ref, rxw_ref, blob_ref, rxb_ref, send_sems, recv_sems)

        @pl.when(step == 0)
        def _():
            dst_ref[...] = jnp.zeros_like(dst_ref)
            dlb_ref[...] = jnp.zeros_like(dlb_ref)

        lb_all = _lower_bound(lbl_ref[...])
        wexp = wexp_ref[...]
        wexp_t = wexpt_ref[...]
        masks = mask_ref[...]
        last_row = lax.broadcasted_iota(jnp.int32, (CHUNK, 1), 0) == CHUNK - 1

        def one_head(j, c, r0, valid):
            cols = slice(j * HEAD_DIM, (j + 1) * HEAD_DIM)
            lb = lb_all[:, cols]
            cf = _chunk_forward(q_ref[0, pl.ds(r0, CHUNK), cols], fz_ref[0, pl.ds(r0, CHUNK), cols],
                                lb, valid, wexp, masks)
            q, kk, e_b, e_c = cf["q"], cf["kk"], cf["e_b"], cf["e_c"]
            v16 = jnp.where(valid, v_ref[0, pl.ds(r0, CHUNK), cols], 0.0).astype(BF16)
            do16 = do_ref[pl.ds(r0, CHUNK), cols].astype(BF16)
            st = s_ref[j, c]
            dst = dst_ref[j]
            dst16 = dst.astype(BF16)
            qb = q * e_b
            kc = kk * e_c
            q16, kk16 = q.astype(BF16), kk.astype(BF16)

            dv = _dot_tn(cf["a"].astype(BF16), do16) + _dot_nt(kc.astype(BF16), dst16)
            da = _dot_nt(do16, v16)
            dqb = _dot(do16, st.astype(BF16))
            dkc = _dot(v16, dst16)
            e_last = e_b[CHUNK - 1:CHUNK, :]
            de = jnp.sum(dst * st, axis=0, keepdims=True)
            dst_ref[j] = dst * e_last + _dot_tn(do16, qb.astype(BF16))

            dq = e_b * dqb
            dkk = e_c * dkc
            dx = [qb * dqb + jnp.where(last_row, de * e_last, 0.0), kc * dkc]
            dm0 = (masks[0] * da).astype(BF16)
            dq = dq + _dot(dm0, kk16)
            dkk = dkk + _dot(dm0, q16)
            for l in range(len(LEVELS)):
                e_m = cf["e"][(2 + l) * CHUNK:(3 + l) * CHUNK]
                dm = (masks[1 + l] * da).astype(BF16)
                dqm = _dot(dm, cf["km"][l].astype(BF16))
                dkm = _dot_tn(dm, cf["qm"][l].astype(BF16))
                dq = dq + e_m * dqm
                dkk = dkk + e_m * dkm
                dx.append(cf["qm"][l] * dqm + cf["km"][l] * dkm)
            dxa = jnp.concatenate(dx, axis=0)
            hi = dxa.astype(BF16)
            mid = (dxa - hi.astype(F32)).astype(BF16)
            dg = _dot(wexp_t, jnp.concatenate([hi, mid], axis=0))

            t = jnp.where(valid, dg / cf["f"] - dkk, 0.0)
            dfz = (1.0 - lb) * cf["sg"] * cf["sn"] * t
            dlb_ref[:, cols] += jnp.sum(cf["sn"] * t, axis=0, keepdims=True)
            dp_ref[0, pl.ds(r0, CHUNK), cols] = jnp.where(valid, dq, 0.0).astype(BF16)
            dp_ref[1, pl.ds(r0, CHUNK), cols] = dfz.astype(BF16)
            dp_ref[2, pl.ds(r0, CHUNK), cols] = jnp.where(valid, dv, 0.0).astype(BF16)

        def chunk(i, carry):
            c = cpb - 1 - i
            r0 = pl.multiple_of(c * CHUNK, CHUNK)
            row = rb * rb_rows + r0 + lax.broadcasted_iota(jnp.int32, (CHUNK, 1), 0)
            for j in range(hps):
                one_head(j, c, r0, row >= PAD_ROWS)
            return carry

        lax.fori_loop(0, cpb, chunk, 0)

        @pl.when((pl.program_id(0) == n_hb - 1) & (step == n_rb - 1))
        def _():
            exchange.wait(dw_ref, rxw_ref, blob_ref, rxb_ref, send_sems, recv_sems)

    head_block = lambda seg: pl.BlockSpec((1, rb_rows, width), lambda h, s: (seg, n_rb - 1 - s, h))
    return pl.pallas_call(
        body, name="hgrn_backward",
        grid=(n_hb, n_rb),
        in_specs=[
            head_block(0), head_block(1), head_block(2),
            pl.BlockSpec((rb_rows, width), lambda h, s: (n_rb - 1 - s, h)),
            pl.BlockSpec((hps, cpb, HEAD_DIM, HEAD_DIM), lambda h, s: (h, n_rb - 1 - s, 0, 0)),
            pl.BlockSpec((2, width), lambda h, s: (0, h)),
            pl.BlockSpec((N_EXP * CHUNK, 3 * CHUNK), lambda h, s: (0, 0)),
            pl.BlockSpec((CHUNK, 2 * N_EXP * CHUNK), lambda h, s: (0, 0)),
            pl.BlockSpec((1 + len(LEVELS), CHUNK, CHUNK), lambda h, s: (0, 0, 0)),
            ANY, ANY,
        ],
        out_specs=[
            pl.BlockSpec((3, rb_rows, width), lambda h, s: (0, n_rb - 1 - s, h)),
            pl.BlockSpec((1, width), lambda h, s: (0, h)),
            ANY, ANY,
        ],
        out_shape=[
            jax.ShapeDtypeStruct((3, rows, D_MODEL), BF16),
            jax.ShapeDtypeStruct((1, D_MODEL), F32),
            exchange.landing_w(), exchange.landing_blob(blob16),
        ],
        scratch_shapes=[pltpu.VMEM((hps, HEAD_DIM, HEAD_DIM), F32)] + exchange.semaphores(),
        compiler_params=_params(("arbitrary", "arbitrary")),
    )(p3, p3, p3, d_o, states, lb_logits, wexp3, wexp_t2, masks, dw16, blob16)


def _silu_and_grad(x):
    s, _ = _sigmoid_pair(x)
    return x * s, s * (1.0 + x * (1.0 - s))


def _window_sum(ext, width, forward_looking):
    n = ext.shape[0]
    s = ext
    step = 1
    while step < width:
        s = s + pltpu.roll(s, (n - step) if forward_looking else step, 0)
        step *= 2
    return s


def _mixers(o, p3, z, tgt, wdh, wdp, wout, poolw, hg_w, pool_scale, final_w, rows):
    tm = _tile(rows, 160)
    nt = rows // tm
    halo_blocks = tm // HALO
    n_grp = len(POOL_WINDOWS)

    def body(o_ref, ghg_ref, u_ref, gpl_ref, mhg_ref, mpl_ref, uh_ref, z_ref, t_ref,
             wdh_ref, wdp_ref, wout_ref, pw_ref, hgw_ref, ps_ref, fw_ref,
             do_ref, dz2_ref, dp_ref, dwdh_ref, dwdp_ref, dwout_ref, dpw_ref, small_ref, carry_ref):
        step = pl.program_id(0)
        tile = nt - 1 - step

        @pl.when(step == 0)
        def _():
            dwdh_ref[...] = jnp.zeros_like(dwdh_ref)
            dwdp_ref[...] = jnp.zeros_like(dwdp_ref)
            dwout_ref[...] = jnp.zeros_like(dwout_ref)
            dpw_ref[...] = jnp.zeros_like(dpw_ref)
            small_ref[...] = jnp.zeros_like(small_ref)
            carry_ref[...] = jnp.zeros_like(carry_ref)

        row = tile * tm + lax.broadcasted_iota(jnp.int32, (tm, 1), 0)
        real = row >= PAD_ROWS
        pos1 = jnp.maximum(row - PAD_ROWS + 1, 1).astype(F32)

        u = jnp.where(real, u_ref[0], 0.0)
        halo_row = tile * tm - HALO + lax.broadcasted_iota(jnp.int32, (HALO, 1), 0)
        uh = jnp.where(halo_row >= PAD_ROWS, uh_ref[0], 0.0)
        ext = jnp.concatenate([uh, u], axis=0)
        pooled, inv_cnt, mixed = [], [], []
        for g, w in enumerate(POOL_WINDOWS):
            cols = slice(g * POOL_GDIM, (g + 1) * POOL_GDIM)
            inv = 1.0 / jnp.minimum(pos1, float(w))
            ws = _window_sum(ext[:, cols], w, False)[HALO:]
            pg = (ws * inv - u[:, cols]).astype(BF16)
            pooled.append(pg)
            inv_cnt.append(inv)
            mixed.append(_dot(pg, pw_ref[g]))
        mixed = jnp.concatenate(mixed, axis=1)
        gpl = gpl_ref[0]
        sp, dsp = _silu_and_grad(gpl)
        ps = ps_ref[...]
        a_pool = (mixed * ps * sp).astype(BF16)
        y_pool = _dot(a_pool, wdp_ref[...])

        o = o_ref[...]
        o_hat, rstd_h = [], []
        for h in range(N_HEADS):
            oh = o[:, h * HEAD_DIM:(h + 1) * HEAD_DIM]
            r = lax.rsqrt(jnp.mean(oh * oh, axis=-1, keepdims=True) + EPS)
            rstd_h.append(r)
            o_hat.append(oh * r)
        o_hat = jnp.concatenate(o_hat, axis=1)
        hgw = hgw_ref[...]
        o_n = o_hat * hgw
        ghg = ghg_ref[0]
        sh, dsh = _silu_and_grad(ghg)
        a_hg = (o_n * sh).astype(BF16)
        y_hg = _dot(a_hg, wdh_ref[...])

        s_mh, _ = _sigmoid_pair(mhg_ref[0])
        s_mp, _ = _sigmoid_pair(mpl_ref[0])
        merged = (s_mh * y_hg + s_mp * y_pool).astype(BF16)
        z2 = z_ref[...] + _dot(merged, wout_ref[...])
        rstd2 = lax.rsqrt(jnp.mean(z2 * z2, axis=-1, keepdims=True) + EPS)
        zh = z2 * rstd2
        fw = fw_ref[...]
        err = jnp.where(row >= FIRST_TOKEN_ROW, zh * fw - t_ref[...], 0.0)
        small_ref[ROW_LOSS:ROW_LOSS + 1, :] += jnp.sum(err * err, axis=0, keepdims=True) * (0.5 / D_MODEL)
        dy = err * (1.0 / D_MODEL)

        small_ref[ROW_FINAL_W:ROW_FINAL_W + 1, :] += jnp.sum(dy * zh, axis=0, keepdims=True)
        uu = dy * fw
        dz2 = rstd2 * (uu - zh * jnp.mean(uu * zh, axis=-1, keepdims=True))
        dz2_ref[...] = dz2
        dz2_16 = dz2.astype(BF16)
        dmerged = _dot_nt(dz2_16, wout_ref[...])
        dwout_ref[...] += _dot_tn(merged, dz2_16)
        dy_hg = (s_mh * dmerged).astype(BF16)
        dy_pool = (s_mp * dmerged).astype(BF16)
        dp_ref[3] = (dmerged * y_hg * s_mh * (1.0 - s_mh)).astype(BF16)
        dp_ref[4] = (dmerged * y_pool * s_mp * (1.0 - s_mp)).astype(BF16)

        da_hg = _dot_nt(dy_hg, wdh_ref[...])
        dwdh_ref[...] += _dot_tn(a_hg, dy_hg)
        dp_ref[0] = (da_hg * o_n * dsh).astype(BF16)
        do_n = da_hg * sh
        small_ref[ROW_HG_W:ROW_HG_W + 1, :] += jnp.sum(do_n * o_hat, axis=0, keepdims=True)
        d_hat = do_n * hgw
        for h in range(N_HEADS):
            cols = slice(h * HEAD_DIM, (h + 1) * HEAD_DIM)
            dh_, oh_ = d_hat[:, cols], o_hat[:, cols]
            do_ref[:, cols] = rstd_h[h] * (dh_ - oh_ * jnp.mean(dh_ * oh_, axis=-1, keepdims=True))

        da_pool = _dot_nt(dy_pool, wdp_ref[...])
        dwdp_ref[...] += _dot_tn(a_pool, dy_pool)
        small_ref[ROW_POOL_SCALE:ROW_POOL_SCALE + 1, :] += jnp.sum(da_pool * mixed * sp, axis=0, keepdims=True)
        dp_ref[2] = (da_pool * mixed * ps * dsp).astype(BF16)
        dmixed = (da_pool * ps * sp).astype(BF16)
        carry = carry_ref[...]
        du, new_carry = [], []
        for g, w in enumerate(POOL_WINDOWS):
            cols = slice(g * POOL_GDIM, (g + 1) * POOL_GDIM)
            dmg = dmixed[:, cols]
            dpooled = _dot_nt(dmg, pw_ref[g])
            dpw_ref[g] += _dot_tn(pooled[g], dmg)
            dps = dpooled * inv_cnt[g]
            ext_b = jnp.concatenate([dps, carry[:, cols]], axis=0)
            du.append(_window_sum(ext_b, w, True)[:tm] - dpooled)
            new_carry.append(dps[:HALO])
        dp_ref[1] = jnp.where(real, jnp.concatenate(du, axis=1), 0.0).astype(BF16)
        carry_ref[...] = jnp.concatenate(new_carry, axis=1)

    row_block = pl.BlockSpec((tm, D_MODEL), lambda s: (nt - 1 - s, 0))
    seg_block = lambda seg: pl.BlockSpec((1, tm, D_MODEL), lambda s: (seg, nt - 1 - s, 0))
    whole = pl.BlockSpec(memory_space=pltpu.VMEM)
    return pl.pallas_call(
        body, name="mixers",
        grid=(nt,),
        in_specs=[
            row_block, seg_block(3), seg_block(4), seg_block(5), seg_block(6), seg_block(7),
            pl.BlockSpec((1, HALO, D_MODEL),
                         lambda s: (4, jnp.maximum((nt - 1 - s) * halo_blocks - 1, 0), 0)),
            row_block, row_block,
            whole, whole, whole, whole, whole, whole, whole,
        ],
        out_specs=[
            row_block, row_block,
            pl.BlockSpec((5, tm, D_MODEL), lambda s: (0, nt - 1 - s, 0)),
            whole, whole, whole, whole, whole,
        ],
        out_shape=[
            jax.ShapeDtypeStruct((rows, D_MODEL), F32),
            jax.ShapeDtypeStruct((rows, D_MODEL), F32),
            jax.ShapeDtypeStruct((5, rows, D_MODEL), BF16),
            jax.ShapeDtypeStruct((D_MODEL, D_MODEL), F32),
            jax.ShapeDtypeStruct((D_MODEL, D_MODEL), F32),
            jax.ShapeDtypeStruct((D_MODEL, D_MODEL), F32),
            jax.ShapeDtypeStruct((n_grp, POOL_GDIM, POOL_GDIM), F32),
            jax.ShapeDtypeStruct((SMALL_ROWS, D_MODEL), F32),
        ],
        scratch_shapes=[pltpu.VMEM((HALO, D_MODEL), F32)],
        compiler_params=_params(("arbitrary",)),
    )(o, p3, p3, p3, p3, p3, p3, z, tgt, wdh, wdp, wout, poolw, hg_w, pool_scale, final_w)


def _seg_specs(tm, row_of, seg_of):
    def spec_a(*g):
        k = seg_of(*g)
        return (jnp.minimum(k, 2), jnp.where(k < 3, row_of(*g), 0), 0)

    def spec_b(*g):
        k = seg_of(*g)
        return (jnp.maximum(k - 3, 0), jnp.where(k >= 3, row_of(*g), 0), 0)

    return pl.BlockSpec((1, tm, D_MODEL), spec_a), pl.BlockSpec((1, tm, D_MODEL), spec_b)


def _in_proj_weight_grad(h, dp, rows, name):
    n_seg = dp.shape[0]
    tm = _tile(rows, 1040)
    nt = rows // tm

    def body(h_ref, dp_ref, dw_ref, dw16_ref, db_ref, acc_ref, bacc_ref):
        i = pl.program_id(1)

        @pl.when(i == 0)
        def _():
            acc_ref[...] = jnp.zeros_like(acc_ref)
            bacc_ref[...] = jnp.zeros_like(bacc_ref)

        dpt = dp_ref[0]
        acc_ref[...] += _dot_tn(h_ref[...], dpt)
        bacc_ref[...] += jnp.sum(dpt.astype(F32), axis=0, keepdims=True)

        @pl.when(i == nt - 1)
        def _():
            dw_ref[0] = acc_ref[...]
            dw16_ref[0] = acc_ref[...].astype(BF16)
            db_ref[0] = bacc_ref[...]

    w_block = pl.BlockSpec((1, D_MODEL, D_MODEL), lambda k, i: (k, 0, 0))
    return pl.pallas_call(
        body, name=name,
        grid=(n_seg, nt),
        in_specs=[pl.BlockSpec((tm, D_MODEL), lambda k, i: (i, 0)),
                  pl.BlockSpec((1, tm, D_MODEL), lambda k, i: (k, i, 0))],
        out_specs=[w_block, w_block, pl.BlockSpec((1, 1, D_MODEL), lambda k, i: (k, 0, 0))],
        out_shape=[
            jax.ShapeDtypeStruct((n_seg, D_MODEL, D_MODEL), F32),
            jax.ShapeDtypeStruct((n_seg, D_MODEL, D_MODEL), BF16),
            jax.ShapeDtypeStruct((n_seg, 1, D_MODEL), F32),
        ],
        scratch_shapes=[pltpu.VMEM((D_MODEL, D_MODEL), F32), pltpu.VMEM((1, D_MODEL), F32)],
        compiler_params=_params(("arbitrary", "arbitrary")),
    )(h, dp)


def _input_grad(dpa, dpb, w4, z, dz2, norm_w, dw16, rows):
    tm = _tile(rows, 1040)
    nt = rows // tm
    exchange = _GradExchange(SEGS_REC, with_blob=False)

    def body(dpa_ref, dpb_ref, w_ref, z_ref, dz2_ref, nw_ref, dw_ref, dz_ref, dnw_ref, rxw_ref,
             acc_ref, send_sems, recv_sems):
        i, k = pl.program_id(0), pl.program_id(1)

        @pl.when((i == 0) & (k == 0))
        def _():
            exchange.start(dw_ref, rxw_ref, None, None, send_sems, recv_sems)
            dnw_ref[...] = jnp.zeros_like(dnw_ref)

        @pl.when((i == nt - 1) & (k == N_SEG - 1))
        def _():
            exchange.wait(dw_ref, rxw_ref, None, None, send_sems, recv_sems)

        @pl.when(k == 0)
        def _():
            acc_ref[...] = jnp.zeros_like(acc_ref)

        @pl.when(k < 3)
        def _():
            acc_ref[...] += _dot_nt(dpa_ref[0], w_ref[0])

        @pl.when(k >= 3)
        def _():
            acc_ref[...] += _dot_nt(dpb_ref[0], w_ref[0])

        @pl.when(k == N_SEG - 1)
        def _():
            zt = z_ref[...]
            rstd = lax.rsqrt(jnp.mean(zt * zt, axis=-1, keepdims=True) + EPS)
            zh = zt * rstd
            dh = acc_ref[...]
            dnw_ref[...] += jnp.sum(dh * zh, axis=0, keepdims=True)
            uu = dh * nw_ref[...]
            dz_ref[...] = dz2_ref[...] + rstd * (uu - zh * jnp.mean(uu * zh, axis=-1, keepdims=True))

    spec_a, spec_b = _seg_specs(tm, lambda i, k: i, lambda i, k: k)
    last_only = pl.BlockSpec((tm, D_MODEL), lambda i, k: (jnp.where(k == N_SEG - 1, i, 0), 0))
    return pl.pallas_call(
        body, name="input_grad",
        grid=(nt, N_SEG),
        in_specs=[
            spec_a, spec_b,
            pl.BlockSpec((1, D_MODEL, D_MODEL), lambda i, k: (k // 2, 0, k % 2)),
            last_only, last_only,
            pl.BlockSpec((1, D_MODEL), lambda i, k: (0, 0)),
            ANY,
        ],
        out_specs=[
            pl.BlockSpec((tm, D_MODEL), lambda i, k: (i, 0)),
            pl.BlockSpec((1, D_MODEL), lambda i, k: (0, 0)),
            ANY,
        ],
        out_shape=[
            jax.ShapeDtypeStruct((rows, D_MODEL), F32),
            jax.ShapeDtypeStruct((1, D_MODEL), F32),
            exchange.landing_w(),
        ],
        scratch_shapes=[pltpu.VMEM((tm, D_MODEL), F32)] + exchange.semaphores(),
        compiler_params=_params(("arbitrary", "arbitrary")),
    )(dpa, dpb, w4, z, dz2, norm_w, dw16)


def _local_step(z, tgt, w4, blob4, seg_order, norm_w, b_in, lb_logits, hg_w, pool_scale, final_w):
    rows = z.shape[0]
    q = D_MODEL // N_CHIPS
    n_grp = len(POOL_WINDOWS)
    pg = POOL_GDIM // N_CHIPS

    wexp2 = jnp.asarray(np.tile(_exponent_matrix(), (1, 2)), BF16)
    wexp_t = jnp.asarray(_exponent_matrix().T, BF16)
    masks2 = jnp.asarray(_paired_masks(), F32)

    h, p3, w4 = _in_proj(z, norm_w, w4, b_in, seg_order, rows)
    o, states, e16, a2, blob4 = _hgrn_forward(p3, lb_logits, wexp2, masks2, blob4, rows)
    wdh = blob4[:, 0:q].reshape(D_MODEL, D_MODEL)
    wdp = blob4[:, q:2 * q].reshape(D_MODEL, D_MODEL)
    wout = blob4[:, 2 * q:3 * q].reshape(D_MODEL, D_MODEL)
    poolw = blob4[:, 3 * q:].reshape(N_CHIPS, n_grp, pg, POOL_GDIM).transpose(1, 0, 2, 3)
    poolw = poolw.reshape(n_grp, POOL_GDIM, POOL_GDIM)
    d_o, dz2, dpb, dwdh, dwdp, dwout, dpw, small = _mixers(
        o, p3, z, tgt, wdh, wdp, wout, poolw, hg_w, pool_scale, final_w, rows)
    dpw4 = dpw.reshape(n_grp, N_CHIPS, pg, POOL_GDIM).transpose(1, 0, 2, 3)
    dpw4 = dpw4.reshape(N_CHIPS, n_grp * pg * POOL_GDIM // D_MODEL, D_MODEL)
    dblob4 = jnp.concatenate([dwdh.reshape(N_CHIPS, q, D_MODEL), dwdp.reshape(N_CHIPS, q, D_MODEL),
                              dwout.reshape(N_CHIPS, q, D_MODEL), dpw4], axis=1)

    dw_mix, dw_mix16, db_mix = _in_proj_weight_grad(h, dpb, rows, "in_proj_weight_grad_mix")
    dpa, dlb, rxw_mix, rx_blob = _hgrn_backward(
        p3, d_o, states, e16, a2, lb_logits, wexp_t, masks2, dw_mix16, dblob4.astype(BF16), rows)
    dw_rec, dw_rec16, db_rec = _in_proj_weight_grad(h, dpa, rows, "in_proj_weight_grad_rec")
    dz, dnw, rxw_rec = _input_grad(dpa, dpb, w4, z, dz2, norm_w, dw_rec16, rows)

    small = jnp.concatenate([
        small[ROW_LOSS:ROW_LOSS + 1],
        dz[PAD_ROWS:PAD_ROWS + N_META],
        dnw,
        db_rec.reshape(len(SEGS_REC), D_MODEL), db_mix.reshape(len(SEGS_MIX), D_MODEL),
        dlb, jnp.zeros_like(dlb),
        small[ROW_HG_W:ROW_HG_W + 1], small[ROW_POOL_SCALE:ROW_POOL_SCALE + 1],
        small[ROW_FINAL_W:ROW_FINAL_W + 1],
        jnp.zeros((SMALL_ROWS - ROW_FINAL_W - 1, D_MODEL), F32),
    ], axis=0)
    return dz, (dw_rec, dw_mix, rxw_rec, rxw_mix), (dblob4, rx_blob), small


ANY = pl.BlockSpec(memory_space=pl.ANY)
MESH = pl.DeviceIdType.MESH


def _place():
    x, y, c = lax.axis_index("x"), lax.axis_index("y"), lax.axis_index("c")
    chips = [(1 - x, y), (x, 1 - y), (1 - x, 1 - y)]
    return x, y, c, chips


class _ShardGather:
    def __init__(self, rows):
        self.half = rows // 2

    def semaphores(self):
        return [pltpu.SemaphoreType.DMA((6,)), pltpu.SemaphoreType.DMA((6,))]

    def _copy(self, k, slot, to, send_sems, recv_sems):
        return pltpu.make_async_remote_copy(src_ref=slot, dst_ref=slot, send_sem=send_sems.at[k],
                                            recv_sem=recv_sems.at[k], device_id=to, device_id_type=MESH)

    def _half(self, ref4, chip, which):
        return ref4.at[chip, pl.ds(which * self.half, self.half), :]

    def start(self, ref4, send_sems, recv_sems):
        x, y, c, chips = _place()
        for j, (cx, cy) in enumerate(chips):
            self._copy(j, self._half(ref4, 2 * x + y, c), (cx, cy, c), send_sems, recv_sems).start()

    def arrive(self, j, ref4, send_sems, recv_sems):
        x, y, c, chips = _place()
        cx, cy = chips[j]
        landed = self._half(ref4, 2 * cx + cy, c)
        self._copy(j, landed, (cx, cy, c), send_sems, recv_sems).wait_recv()
        self._copy(3 + j, landed, (x, y, 1 - c), send_sems, recv_sems).start()
        self._copy(3 + j, self._half(ref4, 2 * cx + cy, 1 - c), (x, y, 1 - c), send_sems, recv_sems).wait_recv()

    def finish(self, ref4, send_sems, recv_sems):
        x, y, c, chips = _place()
        for j, (cx, cy) in enumerate(chips):
            self._copy(j, self._half(ref4, 2 * x + y, c), (cx, cy, c), send_sems, recv_sems).wait_send()
            self._copy(3 + j, self._half(ref4, 2 * cx + cy, c), (x, y, 1 - c), send_sems, recv_sems).wait_send()


def _gather_meta(m4):
    def body(m_in_ref, m4_ref, send_sems, recv_sems):
        x, y, c, chips = _place()

        def copy(j, slot, to):
            return pltpu.make_async_remote_copy(src_ref=slot, dst_ref=slot, send_sem=send_sems.at[j],
                                                recv_sem=recv_sems.at[j], device_id=to, device_id_type=MESH)

        sends = [copy(j, m4_ref.at[2 * x + y], (cx, cy, c)) for j, (cx, cy) in enumerate(chips)]
        for cp in sends:
            cp.start()
        for j, (cx, cy) in enumerate(chips):
            copy(j, m4_ref.at[2 * cx + cy], (x, y, c)).wait_recv()
        for cp in sends:
            cp.wait_send()

    return pl.pallas_call(
        body, name="gather_meta",
        in_specs=[ANY], out_specs=ANY, out_shape=jax.ShapeDtypeStruct(m4.shape, m4.dtype),
        input_output_aliases={0: 0},
        scratch_shapes=[pltpu.SemaphoreType.DMA((3,)), pltpu.SemaphoreType.DMA((3,))],
    )(m4)


class _GradExchange:
    def __init__(self, segs, with_blob):
        self.segs = tuple(segs)
        self.with_blob = with_blob

    def landing_w(self):
        return jax.ShapeDtypeStruct((N_DEV, 2, D_MODEL // 2, D_MODEL), BF16)

    def landing_blob(self, blob16):
        return jax.ShapeDtypeStruct((N_DEV, blob16.shape[1] // 2, D_MODEL), BF16)

    def semaphores(self):
        n_send = 2 * len(self.segs) + (2 * N_CHIPS if self.with_blob else 0)
        n_recv = 2 * N_DEV + (N_DEV if self.with_blob else 0)
        return [pltpu.SemaphoreType.DMA((n_send,)), pltpu.SemaphoreType.DMA((n_recv,))]

    def _copies(self, dw_ref, rxw_ref, blob_ref, rxb_ref, send_sems, recv_sems):
        x, y, c = lax.axis_index("x"), lax.axis_index("y"), lax.axis_index("c")
        chip = 2 * x + y
        half = D_MODEL // 2

        def relation(kx, ky, h):
            return (x ^ kx) * 4 + (y ^ ky) * 2 + (c ^ h)

        def copy(src, dst, send_k, recv_k, to):
            return functools.partial(pltpu.make_async_remote_copy, src_ref=src, dst_ref=dst,
                                     send_sem=send_sems.at[send_k], recv_sem=recv_sems.at[recv_k],
                                     device_id=to, device_id_type=MESH)

        sends, recvs = [], []
        for i, s in enumerate(self.segs):
            kx, ky = (s // 2) >> 1, (s // 2) & 1
            for h in range(2):
                r = relation(kx, ky, h)
                sends.append((r != 0, copy(dw_ref.at[i, pl.ds(h * half, half), :], rxw_ref.at[r, s % 2],
                                           2 * i + h, 2 * r + s % 2, (kx, ky, h))))
        for j in range(2):
            mine = [s // 2 for s in self.segs if s % 2 == j]
            if mine:
                cond = functools.reduce(lambda a, b: a | b, [chip == k for k in mine])
                for r in range(1, N_DEV):
                    slot = rxw_ref.at[r, j]
                    recvs.append((cond, copy(slot, slot, 0, 2 * r + j, (x, y, c))))
        if self.with_blob:
            hb = blob_ref.shape[1] // 2
            for k in range(N_CHIPS):
                for h in range(2):
                    r = relation(k >> 1, k & 1, h)
                    sends.append((r != 0, copy(blob_ref.at[k, pl.ds(h * hb, hb), :], rxb_ref.at[r],
                                               2 * len(self.segs) + 2 * k + h, 2 * N_DEV + r, (k >> 1, k & 1, h))))
            for r in range(1, N_DEV):
                slot = rxb_ref.at[r]
                recvs.append((None, copy(slot, slot, 0, 2 * N_DEV + r, (x, y, c))))
        return sends, recvs

    def start(self, *refs):
        sends, _ = self._copies(*refs)
        for cond, make in sends:
            pl.when(cond)(lambda make=make: make().start())

    def wait(self, *refs):
        sends, recvs = self._copies(*refs)
        for cond, make in sends:
            pl.when(cond)(lambda make=make: make().wait_send())
        for cond, make in recvs:
            if cond is None:
                make().wait_recv()
            else:
                pl.when(cond)(lambda make=make: make().wait_recv())


def _sum_landed(own, rx_ref):
    total = own
    for r in range(1, N_DEV):
        total = total + rx_ref[r, 0].astype(F32)
    return total


def _finish_w(dw_rec, dw_mix, rx_rec, rx_mix, place_arr):
    half = D_MODEL // 2
    tm = _tile(half, 256)
    n_rec = len(SEGS_REC)

    def body(place_ref, own_rec_ref, own_mix_ref, rx_rec_ref, rx_mix_ref, out_ref):
        seg = 2 * place_ref[0] + pl.program_id(0)

        @pl.when(seg < n_rec)
        def _():
            out_ref[0] = _sum_landed(own_rec_ref[0], rx_rec_ref)

        @pl.when(seg >= n_rec)
        def _():
            out_ref[0] = _sum_landed(own_mix_ref[0], rx_mix_ref)

    def own_spec(first, count):
        def index(j, i, place_ref):
            seg = 2 * place_ref[0] + j
            return (jnp.clip(seg - first, 0, count - 1), place_ref[1] * (half // tm) + i, 0)
        return pl.BlockSpec((1, tm, D_MODEL), index)

    rx_spec = pl.BlockSpec((N_DEV, 1, tm, D_MODEL), lambda j, i, place_ref: (0, j, i, 0))
    return pl.pallas_call(
        body, name="finish_w",
        grid_spec=pltpu.PrefetchScalarGridSpec(
            num_scalar_prefetch=1, grid=(2, half // tm),
            in_specs=[own_spec(0, n_rec), own_spec(n_rec, len(SEGS_MIX)), rx_spec, rx_spec],
            out_specs=pl.BlockSpec((1, tm, D_MODEL), lambda j, i, place_ref: (place_ref[1], i, j))),
        out_shape=jax.ShapeDtypeStruct((2, half, 2 * D_MODEL), F32),
        compiler_params=_params(("arbitrary", "arbitrary")),
    )(place_arr, dw_rec, dw_mix, rx_rec, rx_mix)


def _finish_blob(dblob4, rx_blob, place_arr):
    n, rows, cols = rx_blob.shape
    tm = _tile(rows, 256)

    def body(place_ref, own_ref, rx_ref, out_ref):
        out_ref[0] = _sum_landed(own_ref[0, 0], rx_ref)

    return pl.pallas_call(
        body, name="finish_blob",
        grid_spec=pltpu.PrefetchScalarGridSpec(
            num_scalar_prefetch=1, grid=(rows // tm,),
            in_specs=[pl.BlockSpec((1, 1, tm, cols), lambda i, place_ref: (place_ref[0], place_ref[1], i, 0)),
                      pl.BlockSpec((n, 1, tm, cols), lambda i, place_ref: (0, 0, i, 0))],
            out_specs=pl.BlockSpec((1, tm, cols), lambda i, place_ref: (place_ref[1], i, 0))),
        out_shape=jax.ShapeDtypeStruct((2, rows, cols), F32),
        compiler_params=_params(("arbitrary",)),
    )(place_arr, dblob4.reshape(N_CHIPS, 2, rows, cols), rx_blob.reshape(n, 1, rows, cols))


def _share_finished(fw2, fb2, slots):
    def body(w_in_ref, b_in_ref, s_in_ref, w_ref, b_ref, s_ref, send_sems, recv_sems):
        x, y, c, _ = _place()
        sibling = (x, y, 1 - c)

        def copy(k, src, dst, to):
            return pltpu.make_async_remote_copy(src_ref=src, dst_ref=dst, send_sem=send_sems.at[k],
                                                recv_sem=recv_sems.at[k], device_id=to, device_id_type=MESH)

        sends = [copy(0, w_ref.at[c], w_ref.at[c], sibling), copy(1, b_ref.at[c], b_ref.at[c], sibling)]
        for r in range(1, N_DEV):
            peer = (x ^ ((r >> 2) & 1), y ^ ((r >> 1) & 1), c ^ (r & 1))
            sends.append(copy(1 + r, s_ref.at[0], s_ref.at[r], peer))
        for cp in sends:
            cp.start()
        landed = [w_ref.at[1 - c], b_ref.at[1 - c]] + [s_ref.at[r] for r in range(1, N_DEV)]
        for k, slot in enumerate(landed):
            copy(k, slot, slot, (x, y, c)).wait_recv()
        for cp in sends:
            cp.wait_send()

    same = lambda a: jax.ShapeDtypeStruct(a.shape, a.dtype)
    n_sem = 2 + N_DEV - 1
    return pl.pallas_call(
        body, name="share_finished",
        in_specs=[ANY, ANY, ANY], out_specs=[ANY, ANY, ANY],
        out_shape=[same(fw2), same(fb2), same(slots)],
        input_output_aliases={0: 0, 1: 1, 2: 2},
        scratch_shapes=[pltpu.SemaphoreType.DMA((n_sem,)), pltpu.SemaphoreType.DMA((n_sem,))],
    )(fw2, fb2, slots)


def _sum_small(slots, lb_logits, me_arr):
    def body(me_ref, slots_ref, lbl_ref, out_ref):
        me = me_ref[0]
        total = slots_ref[me]
        for d in range(1, N_DEV):
            total = total + slots_ref[d ^ me]
        out_ref[...] = total
        out_ref[ROW_LOSS:ROW_LOSS + 1, :] = jnp.broadcast_to(
            jnp.sum(total[ROW_LOSS:ROW_LOSS + 1, :], axis=-1, keepdims=True), (1, D_MODEL))
        lb = _lower_bound(lbl_ref[...])
        g0 = total[ROW_LB:ROW_LB + 1, :] * lb * (1.0 - lb)
        out_ref[ROW_LB:ROW_LB + 1, :] = g0
        out_ref[ROW_LB + 1:ROW_LB + 2, :] = -g0

    return pl.pallas_call(
        body, name="sum_small",
        grid_spec=pltpu.PrefetchScalarGridSpec(
            num_scalar_prefetch=1, grid=(1,),
            in_specs=[pl.BlockSpec((N_DEV, SMALL_ROWS, D_MODEL), lambda i, me_ref: (0, 0, 0)),
                      pl.BlockSpec((2, D_MODEL), lambda i, me_ref: (0, 0))],
            out_specs=pl.BlockSpec((SMALL_ROWS, D_MODEL), lambda i, me_ref: (0, 0))),
        out_shape=jax.ShapeDtypeStruct((SMALL_ROWS, D_MODEL), F32),
        compiler_params=_params(("arbitrary",)),
    )(me_arr, slots, lb_logits)


def _adamw(w, g, m, v):
    rows, cols = w.shape
    tm = _tile(rows, 256, mult=8) if rows % 8 == 0 else rows
    c1 = 1.0 / (1.0 - ADAM_B1 ** ADAM_STEP)
    c2 = 1.0 / (1.0 - ADAM_B2 ** ADAM_STEP)

    def body(w_ref, g_ref, m_ref, v_ref, d_ref, nm_ref, nv_ref):
        gt = g_ref[...]
        nm = ADAM_B1 * m_ref[...] + (1.0 - ADAM_B1) * gt
        nv = ADAM_B2 * v_ref[...] + (1.0 - ADAM_B2) * (gt * gt)
        nm_ref[...] = nm
        nv_ref[...] = nv
        d_ref[...] = -ADAM_LR * ((nm * c1) / (jnp.sqrt(nv * c2) + ADAM_EPS) + ADAM_WD * w_ref[...])

    blk = pl.BlockSpec((tm, cols), lambda i: (i, 0))
    sds = jax.ShapeDtypeStruct((rows, cols), F32)
    return pl.pallas_call(
        body, name="adamw",
        grid=(rows // tm,), in_specs=[blk] * 4, out_specs=[blk] * 3, out_shape=[sds] * 3,
        compiler_params=_params(("arbitrary",)),
    )(w, g, m, v)


def kernel(x, meta_tokens, norm_w, w_in, b_in, lb_logits, hg_norm_w, pool_w, pool_scale, w_down_hg, w_down_pool, w_out, final_norm_w, loss_target, m_meta_tokens, m_norm_w, m_w_in, m_b_in, m_lb_logits, m_hg_norm_w, m_pool_w, m_pool_scale, m_w_down_hg, m_w_down_pool, m_w_out, m_final_norm_w, v_meta_tokens, v_norm_w, v_w_in, v_b_in, v_lb_logits, v_hg_norm_w, v_pool_w, v_pool_scale, v_w_down_hg, v_w_down_pool, v_w_out, v_final_norm_w):
    seq = x.shape[1]
    xi, yi, ci = lax.axis_index("x"), lax.axis_index("y"), lax.axis_index("c")
    chip = 2 * xi + yi
    place_arr = jnp.stack([chip, ci]).astype(jnp.int32)
    me_arr = jnp.reshape(4 * xi + 2 * yi + ci, (1,)).astype(jnp.int32)
    q = D_MODEL // N_CHIPS

    def blob_of(wdh, wdp, wo, pw):
        return jnp.concatenate([wdh[0], wdp[0], wo[0], pw[0].reshape(-1, D_MODEL)], axis=0)

    def in_every_slot(a):
        return jnp.broadcast_to(a[None], (N_CHIPS,) + a.shape)

    meta4 = _gather_meta(in_every_slot(meta_tokens))
    meta_full = meta4.transpose(1, 0, 2).reshape(N_META, D_MODEL)
    w4 = in_every_slot(w_in[0].astype(BF16))
    blob4 = in_every_slot(blob_of(w_down_hg, w_down_pool, w_out, pool_w).astype(BF16))
    seg_order = jnp.stack([2 * (chip ^ rel) + t for rel in (0, 2, 1, 3) for t in (0, 1)]).astype(jnp.int32)

    z = jnp.concatenate([jnp.zeros((PAD_ROWS, D_MODEL), F32), meta_full, x[0]], axis=0)
    tgt = jnp.concatenate([jnp.zeros((FIRST_TOKEN_ROW, D_MODEL), F32), loss_target[0]], axis=0)
    fw2 = final_norm_w.reshape(1, D_MODEL)
    dz, w_parts, blob_parts, small = _local_step(
        z, tgt, w4, blob4, seg_order, norm_w, b_in, lb_logits, hg_norm_w, pool_scale, fw2)
    grad_x = dz[FIRST_TOKEN_ROW:][None]

    fin_w = _finish_w(*w_parts, place_arr)
    fin_b = _finish_blob(*blob_parts, place_arr)
    gw2, gb2, slots = _share_finished(fin_w, fin_b, jnp.broadcast_to(small[None], (N_DEV,) + small.shape))
    tot = _sum_small(slots, lb_logits, me_arr)
    g_w_in = gw2.reshape(D_MODEL, 2 * D_MODEL)
    g_blob = gb2.reshape(-1, D_MODEL)

    d_win, nm_win, nv_win = _adamw(w_in[0], g_w_in, m_w_in[0], v_w_in[0])
    d_blob, nm_blob, nv_blob = _adamw(
        blob_of(w_down_hg, w_down_pool, w_out, pool_w), g_blob,
        blob_of(m_w_down_hg, m_w_down_pool, m_w_out, m_pool_w),
        blob_of(v_w_down_hg, v_w_down_pool, v_w_out, v_pool_w))
    g_meta = lax.dynamic_slice_in_dim(tot[ROW_META:ROW_META + N_META], chip * q, q, axis=1)
    d_meta, nm_meta, nv_meta = _adamw(meta_tokens, g_meta, m_meta_tokens, v_meta_tokens)

    def rows_of(nw, bi, lbl, hg, ps, fw):
        return jnp.concatenate([nw, bi.reshape(N_SEG, D_MODEL), lbl, hg, ps, fw.reshape(1, D_MODEL),
                                jnp.zeros((2, D_MODEL), F32)], axis=0)

    g_rows = jnp.concatenate([tot[ROW_NORM_W:ROW_FINAL_W + 1], jnp.zeros((2, D_MODEL), F32)], axis=0)
    d_rows, nm_rows, nv_rows = _adamw(
        rows_of(norm_w, b_in, lb_logits, hg_norm_w, pool_scale, final_norm_w), g_rows,
        rows_of(m_norm_w, m_b_in, m_lb_logits, m_hg_norm_w, m_pool_scale, m_final_norm_w),
        rows_of(v_norm_w, v_b_in, v_lb_logits, v_hg_norm_w, v_pool_scale, v_final_norm_w))

    def unblob(b):
        return (b[0:q][None], b[q:2 * q][None], b[2 * q:3 * q][None], b[3 * q:].reshape(pool_w.shape))

    def unrows(r):
        o = ROW_NORM_W
        return dict(norm_w=r[ROW_NORM_W - o:ROW_B_IN - o], b_in=r[ROW_B_IN - o:ROW_LB - o].reshape(1, -1),
                    lb_logits=r[ROW_LB - o:ROW_HG_W - o], hg_norm_w=r[ROW_HG_W - o:ROW_POOL_SCALE - o],
                    pool_scale=r[ROW_POOL_SCALE - o:ROW_FINAL_W - o], final_norm_w=r[ROW_FINAL_W - o])

    def leaves(meta_part, rows_part, win_part, blob_part):
        r = unrows(rows_part)
        wdh, wdp, wo, pw = unblob(blob_part)
        return [meta_part, r["norm_w"], win_part[None], r["b_in"], r["lb_logits"], r["hg_norm_w"], pw,
                r["pool_scale"], wdh, wdp, wo, r["final_norm_w"]]

    loss = tot[ROW_LOSS, 0]
    return (loss, grad_x,
            *leaves(g_meta, g_rows, g_w_in, g_blob),
            *leaves(d_meta, d_rows, d_win, d_blob),
            *leaves(nm_meta, nm_rows, nm_win, nm_blob),
            *leaves(nv_meta, nv_rows, nv_win, nv_blob))
```

```python
import functools

import numpy as np
import jax
import jax.numpy as jnp
from jax import lax
from jax.experimental import pallas as pl
from jax.experimental.pallas import tpu as pltpu

F32 = jnp.float32
BF16 = jnp.bfloat16

D_MODEL = 1024
N_SEG = 8
N_HEADS = 8
HEAD_DIM = 128
CHUNK = 64
N_META = 16
PAD_ROWS = CHUNK - N_META
FIRST_TOKEN_ROW = CHUNK
LEVELS = (32, 16, 8, 4, 2, 1)
N_EXP = 2 + len(LEVELS)
POOL_WINDOWS = (2, 4, 8, 16)
POOL_GDIM = D_MODEL // len(POOL_WINDOWS)
HALO = 16
LOCAL_UNROLL = 13
HEADS_PER_STEP = 4
EPS = 1e-6
N_CHIPS = 4
N_DEV = 8
SEGS_REC = (0, 1, 2)
SEGS_MIX = (3, 4, 5, 6, 7)

ADAM_LR = 0.001
ADAM_B1 = 0.9
ADAM_B2 = 0.999
ADAM_EPS = 1e-08
ADAM_WD = 0.01
ADAM_STEP = 10

VMEM_LIMIT_BYTES = 56 * 1024 * 1024

ROW_LOSS = 0
ROW_META = 1
ROW_NORM_W = ROW_META + N_META
ROW_B_IN = ROW_NORM_W + 1
ROW_LB = ROW_B_IN + N_SEG
ROW_HG_W = ROW_LB + 2
ROW_POOL_SCALE = ROW_HG_W + 1
ROW_FINAL_W = ROW_POOL_SCALE + 1
SMALL_ROWS = 32


def _tile(total, cap, mult=16):
    best = None
    for t in range(mult, min(total, cap) + 1, mult):
        if total % t == 0:
            best = t
    assert best is not None, (total, cap, mult)
    return best


def _params(sem=None):
    return pltpu.CompilerParams(dimension_semantics=sem, vmem_limit_bytes=VMEM_LIMIT_BYTES)


def _dot(a, b):
    return jnp.dot(a, b, preferred_element_type=F32)


def _dot_nt(a, b):
    return lax.dot_general(a, b, (((1,), (1,)), ((), ())), preferred_element_type=F32)


def _dot_tn(a, b):
    return lax.dot_general(a, b, (((0,), (0,)), ((), ())), preferred_element_type=F32)


def _sigmoid_pair(x):
    t = jnp.exp(-jnp.abs(x))
    r = 1.0 / (1.0 + t)
    pos = x >= 0
    return jnp.where(pos, r, t * r), jnp.where(pos, t * r, r)


def _exponent_matrix():
    t = np.arange(CHUNK)[:, None]
    j = np.arange(CHUNK)[None, :]
    blocks = [j <= t, j > t]
    for m in LEVELS:
        rho = (t // (2 * m)) * (2 * m) + m
        upper = (t >= rho) & (j > rho) & (j <= t)
        lower = (t < rho) & (j > t) & (j <= rho)
        blocks.append(upper | lower)
    return np.concatenate(blocks, axis=0).astype(np.float32)


def _pair_masks():
    t = np.arange(CHUNK)[:, None]
    s = np.arange(CHUNK)[None, :]
    masks = [t == s]
    for m in LEVELS:
        same = (t // (2 * m)) == (s // (2 * m))
        masks.append(same & ((t % (2 * m)) >= m) & ((s % (2 * m)) < m))
    return np.stack(masks).astype(np.float32)


LEVEL_PAIRS = ((0, 1), (2, 3), (4, 5), (6, None))


def _paired_masks():
    m = _pair_masks()
    zero = np.zeros_like(m[0])
    return np.stack([np.concatenate([m[a], zero if b is None else m[b]], axis=1) for a, b in LEVEL_PAIRS])


def _split3(x):
    hi = x.astype(BF16)
    r = x - hi.astype(F32)
    mid = r.astype(BF16)
    lo = (r - mid.astype(F32)).astype(BF16)
    return hi, mid, lo


def _chunk_forward(q, fz, lb, valid, wexp, masks):
    sg, sn = _sigmoid_pair(fz)
    f = lb + (1.0 - lb) * sg
    g = jnp.where(valid, jnp.log(f), 0.0)
    kk = jnp.where(valid, (1.0 - lb) * sn, 0.0)
    q = jnp.where(valid, q, 0.0)
    e = jnp.exp(_dot(wexp, jnp.concatenate(_split3(g), axis=0)))
    e_b = e[0:CHUNK]
    e_c = e[CHUNK:2 * CHUNK]
    a = masks[0] * _dot_nt(q.astype(BF16), kk.astype(BF16))
    qm, km = [], []
    for l in range(len(LEVELS)):
        e_m = e[(2 + l) * CHUNK:(3 + l) * CHUNK]
        qm.append(q * e_m)
        km.append(kk * e_m)
        a = a + masks[1 + l] * _dot_nt(qm[l].astype(BF16), km[l].astype(BF16))
    return dict(sg=sg, sn=sn, f=f, kk=kk, q=q, e=e, e_b=e_b, e_c=e_c, a=a, qm=qm, km=km)


def _lower_bound(lbl):
    return 1.0 / (1.0 + jnp.exp(lbl[1:2, :] - lbl[0:1, :]))


def _in_proj(z, norm_w, w4, b_in, seg_order, rows):
    tm = _tile(rows, 1040)
    nt = rows // tm
    gather = _ShardGather(w4.shape[1])

    def body(order_ref, z_ref, nw_ref, b_ref, w_in_ref, h_ref, p_ref, w4_ref,
             h_all, w_buf, w_sem, send_sems, recv_sems):
        kk, i = pl.program_id(0), pl.program_id(1)

        @pl.when((kk == 0) & (i == 0))
        def _():
            gather.start(w4_ref, send_sems, recv_sems)

        @pl.when(kk == 0)
        def _():
            zt = z_ref[...]
            rstd = lax.rsqrt(jnp.mean(zt * zt, axis=-1, keepdims=True) + EPS)
            h = (zt * rstd * nw_ref[...]).astype(BF16)
            h_all[pl.ds(pl.multiple_of(i * tm, 16), tm), :] = h
            h_ref[...] = h

        for j in range(N_CHIPS - 1):
            @pl.when((kk == 2 + 2 * j) & (i == 0))
            def _(j=j):
                gather.arrive(j, w4_ref, send_sems, recv_sems)

        @pl.when(i == 0)
        def _():
            seg = order_ref[kk]
            cp = pltpu.make_async_copy(
                w4_ref.at[seg // 2, :, pl.ds(pl.multiple_of((seg % 2) * D_MODEL, D_MODEL), D_MODEL)], w_buf, w_sem)
            cp.start()
            cp.wait()

        p_ref[0] = _dot(h_all[pl.ds(pl.multiple_of(i * tm, 16), tm), :], w_buf[...]) + b_ref[...]

        @pl.when((kk == N_SEG - 1) & (i == nt - 1))
        def _():
            gather.finish(w4_ref, send_sems, recv_sems)

    first_pass = lambda kk, i, order_ref: (jnp.where(kk == 0, i, nt - 1), 0)
    return pl.pallas_call(
        body, name="in_proj",
        grid_spec=pltpu.PrefetchScalarGridSpec(
            num_scalar_prefetch=1, grid=(N_SEG, nt),
            in_specs=[
                pl.BlockSpec((tm, D_MODEL), first_pass),
                pl.BlockSpec((1, D_MODEL), lambda kk, i, order_ref: (0, 0)),
                pl.BlockSpec((1, D_MODEL), lambda kk, i, order_ref: (0, order_ref[kk])),
                ANY,
            ],
            out_specs=[
                pl.BlockSpec((tm, D_MODEL), first_pass),
                pl.BlockSpec((1, tm, D_MODEL), lambda kk, i, order_ref: (order_ref[kk], i, 0)),
                ANY,
            ],
            scratch_shapes=[
                pltpu.VMEM((rows, D_MODEL), BF16),
                pltpu.VMEM((D_MODEL, D_MODEL), BF16),
                pltpu.SemaphoreType.DMA,
            ] + gather.semaphores()),
        out_shape=[
            jax.ShapeDtypeStruct((rows, D_MODEL), BF16),
            jax.ShapeDtypeStruct((N_SEG, rows, D_MODEL), F32),
            jax.ShapeDtypeStruct(w4.shape, w4.dtype),
        ],
        input_output_aliases={4: 2},
        compiler_params=_params(("arbitrary", "arbitrary")),
    )(seg_order, z, norm_w, b_in, w4)


def _hgrn_forward(p3, lb_logits, wexp2, masks2, blob4, rows):
    n_chunks = rows // CHUNK
    cpb = _tile(n_chunks, 13, mult=1)
    rb_rows = cpb * CHUNK
    n_rb = n_chunks // cpb
    lanes = cpb * HEAD_DIM
    gather = _ShardGather(blob4.shape[1])

    def body(q_ref, fz_ref, v_ref, lbl_ref, wexp_ref, mask_ref, b_in_ref, o_ref, s_ref, e16_ref, a2_ref, b4_ref,
             st_ref, e_ref, u_ref, q_s, kk_s, v_s, qb_s, oi_s, send_sems, recv_sems):
        rb = pl.program_id(1)

        @pl.when((pl.program_id(0) == 0) & (rb == 0))
        def _():
            gather.start(b4_ref, send_sems, recv_sems)

        @pl.when(rb == 0)
        def _():
            st_ref[...] = jnp.zeros_like(st_ref)

        lb = _lower_bound(lbl_ref[...])
        row = rb * rb_rows + lax.broadcasted_iota(jnp.int32, (rb_rows, 1), 0)
        valid = row >= PAD_ROWS
        sg, sn = _sigmoid_pair(fz_ref[0])
        g = jnp.where(valid, jnp.log(lb + (1.0 - lb) * sg), 0.0)
        kk_s[...] = jnp.where(valid, (1.0 - lb) * sn, 0.0)
        q_s[...] = jnp.where(valid, q_ref[0], 0.0)
        v_s[...] = jnp.where(valid, v_ref[0], 0.0).astype(BF16)
        hi = g.astype(BF16)
        mid = (g - hi.astype(F32)).astype(BF16)
        g2 = jnp.concatenate(
            [jnp.concatenate([hi[b * CHUNK:(b + 1) * CHUNK], mid[b * CHUNK:(b + 1) * CHUNK]], axis=0)
             for b in range(cpb)], axis=1)
        e_ref[...] = jnp.exp(_dot(wexp_ref[...], g2))
        e16_ref[0, 0] = e_ref[...].astype(BF16)

        zeros16 = jnp.zeros((CHUNK, HEAD_DIM), BF16)

        def local(b, carry):
            r0 = pl.multiple_of(b * CHUNK, CHUNK)
            l0 = pl.multiple_of(b * HEAD_DIM, HEAD_DIM)
            q = q_s[pl.ds(r0, CHUNK), :]
            kk = kk_s[pl.ds(r0, CHUNK), :]
            v16 = v_s[pl.ds(r0, CHUNK), :]

            def scaled(entry):
                if entry == 0:
                    return q.astype(BF16), kk.astype(BF16)
                e_m = e_ref[(1 + entry) * CHUNK:(2 + entry) * CHUNK, pl.ds(l0, HEAD_DIM)]
                return (q * e_m).astype(BF16), (kk * e_m).astype(BF16)

            a2 = jnp.zeros((CHUNK, 2 * CHUNK), F32)
            for p, (ea, eb) in enumerate(LEVEL_PAIRS):
                qa, ka = scaled(ea)
                if eb is None:
                    prod = _dot_nt(qa, jnp.concatenate([ka, zeros16], axis=0))
                else:
                    qb_, kb_ = scaled(eb)
                    rhs = jnp.concatenate([jnp.concatenate([ka, zeros16], axis=1),
                                           jnp.concatenate([zeros16, kb_], axis=1)], axis=0)
                    prod = _dot_nt(jnp.concatenate([qa, qb_], axis=1), rhs)
                a2 = a2 + mask_ref[p] * prod
            a2_16 = a2.astype(BF16)
            a2_ref[pl.ds(r0, CHUNK), :] = a2_16
            oi_s[pl.ds(r0, CHUNK), :] = _dot(a2_16, jnp.concatenate([v16, v16], axis=0))
            e_b = e_ref[0:CHUNK, pl.ds(l0, HEAD_DIM)]
            e_c = e_ref[CHUNK:2 * CHUNK, pl.ds(l0, HEAD_DIM)]
            qb_s[pl.ds(r0, CHUNK), :] = (q * e_b).astype(BF16)
            u_ref[b] = _dot_tn(v16, (kk * e_c).astype(BF16))
            return carry

        lax.fori_loop(0, cpb, local, 0, unroll=LOCAL_UNROLL)

        def recur(b, st):
            l0 = pl.multiple_of(b * HEAD_DIM, HEAD_DIM)
            s_ref[0, b] = st
            return st * e_ref[CHUNK - 1:CHUNK, pl.ds(l0, HEAD_DIM)] + u_ref[b]

        st_ref[...] = lax.fori_loop(0, cpb, recur, st_ref[...])

        def inter(b, carry):
            r0 = pl.multiple_of(b * CHUNK, CHUNK)
            o_ref[pl.ds(r0, CHUNK), :] = oi_s[pl.ds(r0, CHUNK), :] + _dot_nt(
                qb_s[pl.ds(r0, CHUNK), :], s_ref[0, b].astype(BF16))
            return carry

        lax.fori_loop(0, cpb, inter, 0, unroll=LOCAL_UNROLL)

        @pl.when((pl.program_id(0) == N_HEADS // 2) & (rb == 0))
        def _():
            gather.pass_on_all(b4_ref, send_sems, recv_sems)

        @pl.when((pl.program_id(0) == N_HEADS - 1) & (rb == n_rb - 1))
        def _():
            gather.await_sibling_all(b4_ref, send_sems, recv_sems)
            gather.finish(b4_ref, send_sems, recv_sems)

    head_block = lambda seg: pl.BlockSpec((1, rb_rows, HEAD_DIM), lambda h, r: (seg, r, h))
    return pl.pallas_call(
        body, name="hgrn_forward",
        grid=(N_HEADS, n_rb),
        in_specs=[
            head_block(0), head_block(1), head_block(2),
            pl.BlockSpec((2, HEAD_DIM), lambda h, r: (0, h)),
            pl.BlockSpec((N_EXP * CHUNK, 2 * CHUNK), lambda h, r: (0, 0)),
            pl.BlockSpec((len(LEVEL_PAIRS), CHUNK, 2 * CHUNK), lambda h, r: (0, 0, 0)),
            ANY,
        ],
        out_specs=[
            pl.BlockSpec((rb_rows, HEAD_DIM), lambda h, r: (r, h)),
            pl.BlockSpec((1, cpb, HEAD_DIM, HEAD_DIM), lambda h, r: (h, r, 0, 0)),
            pl.BlockSpec((1, 1, N_EXP * CHUNK, lanes), lambda h, r: (h, r, 0, 0)),
            pl.BlockSpec((rb_rows, HEAD_DIM), lambda h, r: (r, h)),
            ANY,
        ],
        out_shape=[
            jax.ShapeDtypeStruct((rows, D_MODEL), F32),
            jax.ShapeDtypeStruct((N_HEADS, n_chunks, HEAD_DIM, HEAD_DIM), F32),
            jax.ShapeDtypeStruct((N_HEADS, n_rb, N_EXP * CHUNK, lanes), BF16),
            jax.ShapeDtypeStruct((rows, D_MODEL), BF16),
            jax.ShapeDtypeStruct(blob4.shape, blob4.dtype),
        ],
        input_output_aliases={6: 4},
        scratch_shapes=[
            pltpu.VMEM((HEAD_DIM, HEAD_DIM), F32),
            pltpu.VMEM((N_EXP * CHUNK, lanes), F32),
            pltpu.VMEM((cpb, HEAD_DIM, HEAD_DIM), F32),
            pltpu.VMEM((rb_rows, HEAD_DIM), F32),
            pltpu.VMEM((rb_rows, HEAD_DIM), F32),
            pltpu.VMEM((rb_rows, HEAD_DIM), BF16),
            pltpu.VMEM((rb_rows, HEAD_DIM), BF16),
            pltpu.VMEM((rb_rows, HEAD_DIM), F32),
        ] + gather.semaphores(),
        compiler_params=_params(("arbitrary", "arbitrary")),
    )(p3, p3, p3, lb_logits, wexp2, masks2, blob4)


def _hgrn_forward_old(p3, lb_logits, wexp3, masks, rows):
    n_chunks = rows // CHUNK
    cpb = _tile(n_chunks, 13, mult=1)
    rb_rows = cpb * CHUNK
    hps = HEADS_PER_STEP
    width = hps * HEAD_DIM

    def body(q_ref, fz_ref, v_ref, lbl_ref, wexp_ref, mask_ref, o_ref, s_ref, st_ref):
        rb = pl.program_id(1)

        @pl.when(rb == 0)
        def _():
            st_ref[...] = jnp.zeros_like(st_ref)

        lb_all = _lower_bound(lbl_ref[...])
        wexp = wexp_ref[...]
        masks = mask_ref[...]

        def chunk(c, carry):
            r0 = pl.multiple_of(c * CHUNK, CHUNK)
            row = rb * rb_rows + r0 + lax.broadcasted_iota(jnp.int32, (CHUNK, 1), 0)
            valid = row >= PAD_ROWS
            q_all = q_ref[0, pl.ds(r0, CHUNK), :]
            fz_all = fz_ref[0, pl.ds(r0, CHUNK), :]
            v_all = jnp.where(valid, v_ref[0, pl.ds(r0, CHUNK), :], 0.0).astype(BF16)
            st_all = [st_ref[j] for j in range(hps)]
            o_all, st_new = [], []
            for j in range(hps):
                cols = slice(j * HEAD_DIM, (j + 1) * HEAD_DIM)
                cf = _chunk_forward(q_all[:, cols], fz_all[:, cols], lb_all[:, cols], valid, wexp, masks)
                v16 = v_all[:, cols]
                o = _dot_nt((cf["q"] * cf["e_b"]).astype(BF16), st_all[j].astype(BF16))
                o_all.append(o + _dot(cf["a"].astype(BF16), v16))
                kc16 = (cf["kk"] * cf["e_c"]).astype(BF16)
                st_new.append(st_all[j] * cf["e_b"][CHUNK - 1:CHUNK, :] + _dot_tn(v16, kc16))
            o_ref[pl.ds(r0, CHUNK), :] = jnp.concatenate(o_all, axis=1)
            for j in range(hps):
                s_ref[j, c] = st_all[j]
                st_ref[j] = st_new[j]
            return carry

        lax.fori_loop(0, cpb, chunk, 0)

    head_block = lambda seg: pl.BlockSpec((1, rb_rows, width), lambda h, r: (seg, r, h))
    return pl.pallas_call(
        body, name="hgrn_forward",
        grid=(N_HEADS // hps, n_chunks // cpb),
        in_specs=[
            head_block(0), head_block(1), head_block(2),
            pl.BlockSpec((2, width), lambda h, r: (0, h)),
            pl.BlockSpec((N_EXP * CHUNK, 3 * CHUNK), lambda h, r: (0, 0)),
            pl.BlockSpec((1 + len(LEVELS), CHUNK, CHUNK), lambda h, r: (0, 0, 0)),
        ],
        out_specs=[
            pl.BlockSpec((rb_rows, width), lambda h, r: (r, h)),
            pl.BlockSpec((hps, cpb, HEAD_DIM, HEAD_DIM), lambda h, r: (h, r, 0, 0)),
        ],
        out_shape=[
            jax.ShapeDtypeStruct((rows, D_MODEL), F32),
            jax.ShapeDtypeStruct((N_HEADS, n_chunks, HEAD_DIM, HEAD_DIM), F32),
        ],
        scratch_shapes=[pltpu.VMEM((hps, HEAD_DIM, HEAD_DIM), F32)],
        compiler_params=_params(("arbitrary", "arbitrary")),
    )(p3, p3, p3, lb_logits, wexp3, masks)


def _hgrn_backward(p3, d_o, states, e16, a2, lb_logits, wexp_t, masks2, dw16, blob16, rows):
    n_chunks = rows // CHUNK
    cpb = _tile(n_chunks, 13, mult=1)
    rb_rows = cpb * CHUNK
    n_rb = n_chunks // cpb
    lanes = cpb * HEAD_DIM
    exchange = _GradExchange(SEGS_MIX, with_blob=True)

    def body(q_ref, fz_ref, v_ref, do_ref, s_ref, e_ref, a2_ref, lbl_ref, wexpt_ref, mask_ref, dw_ref, blob_ref,
             dp_ref, dlb_ref, rxw_ref, rxb_ref,
             dst_ref, g_ref, dsn_ref, q_s, kk_s, v_s, do_s, dq_s, dkk_s, dg_s, dx_s, send_sems, recv_sems):
        step = pl.program_id(1)
        rb = n_rb - 1 - step

        @pl.when((pl.program_id(0) == 0) & (step == 0))
        def _():
            exchange.start(dw_ref, rxw_ref, blob_ref, rxb_ref, send_sems, recv_sems)

        @pl.when(step == 0)
        def _():
            dst_ref[...] = jnp.zeros_like(dst_ref)
            dlb_ref[...] = jnp.zeros_like(dlb_ref)

        lb = _lower_bound(lbl_ref[...])
        row = rb * rb_rows + lax.broadcasted_iota(jnp.int32, (rb_rows, 1), 0)
        valid = row >= PAD_ROWS
        sg, sn = _sigmoid_pair(fz_ref[0])
        f = lb + (1.0 - lb) * sg
        g = jnp.where(valid, jnp.log(f), 0.0)
        kk_s[...] = jnp.where(valid, (1.0 - lb) * sn, 0.0)
        q_s[...] = jnp.where(valid, q_ref[0], 0.0)
        v_s[...] = jnp.where(valid, v_ref[0], 0.0).astype(BF16)
        do_s[...] = do_ref[...].astype(BF16)
        e_last_all = jnp.exp(jnp.concatenate(
            [jnp.sum(g[b * CHUNK:(b + 1) * CHUNK], axis=0, keepdims=True) for b in range(cpb)], axis=0))
        last_row = lax.broadcasted_iota(jnp.int32, (CHUNK, 1), 0) == CHUNK - 1
        zeros16 = jnp.zeros((CHUNK, HEAD_DIM), BF16)

        def factor(block, l0):
            return e_ref[0, 0, block * CHUNK:(block + 1) * CHUNK, pl.ds(l0, HEAD_DIM)].astype(F32)

        def contribution(b, carry):
            r0 = pl.multiple_of(b * CHUNK, CHUNK)
            l0 = pl.multiple_of(b * HEAD_DIM, HEAD_DIM)
            qb16 = (q_s[pl.ds(r0, CHUNK), :] * factor(0, l0)).astype(BF16)
            g_ref[b] = _dot_tn(do_s[pl.ds(r0, CHUNK), :], qb16)
            return carry

        lax.fori_loop(0, cpb, contribution, 0, unroll=LOCAL_UNROLL)

        cur = dst_ref[...]
        for b in reversed(range(cpb)):
            dsn_ref[b] = cur
            cur = cur * e_last_all[b:b + 1, :] + g_ref[b]
        dst_ref[...] = cur

        def local(b, carry):
            r0 = pl.multiple_of(b * CHUNK, CHUNK)
            l0 = pl.multiple_of(b * HEAD_DIM, HEAD_DIM)
            q = q_s[pl.ds(r0, CHUNK), :]
            kk = kk_s[pl.ds(r0, CHUNK), :]
            v16 = v_s[pl.ds(r0, CHUNK), :]
            do16 = do_s[pl.ds(r0, CHUNK), :]
            st = s_ref[0, b]
            dsn = dsn_ref[b]
            dsn16 = dsn.astype(BF16)
            e_b, e_c = factor(0, l0), factor(1, l0)
            qb, kc = q * e_b, kk * e_c

            t = _dot_tn(a2_ref[pl.ds(r0, CHUNK), :], do16)
            dv = t[0:CHUNK] + t[CHUNK:2 * CHUNK] + _dot_nt(kc.astype(BF16), dsn16)
            dp_ref[2, pl.ds(r0, CHUNK), :] = dv.astype(BF16)
            da2 = _dot_nt(do16, jnp.concatenate([v16, v16], axis=0))
            dqb = _dot(do16, st.astype(BF16))
            dkc = _dot(v16, dsn16)
            de = jnp.sum(dsn * st, axis=0, keepdims=True) * e_b[CHUNK - 1:CHUNK, :]
            dq = e_b * dqb
            dkk = e_c * dkc
            dx_s[0:CHUNK, pl.ds(l0, HEAD_DIM)] = (qb * dqb + jnp.where(last_row, de, 0.0)).astype(BF16)
            dx_s[CHUNK:2 * CHUNK, pl.ds(l0, HEAD_DIM)] = (kc * dkc).astype(BF16)

            def scaled(entry):
                if entry == 0:
                    return q, kk, None
                e_m = factor(1 + entry, l0)
                return q * e_m, kk * e_m, e_m

            for p, (ea, eb) in enumerate(LEVEL_PAIRS):
                dm = (mask_ref[p] * da2).astype(BF16)
                qa, ka, e_a = scaled(ea)
                if eb is None:
                    lhs_q = jnp.concatenate([qa.astype(BF16), zeros16], axis=1)
                    rhs_k = jnp.concatenate([jnp.concatenate([ka.astype(BF16), zeros16], axis=1),
                                             jnp.concatenate([zeros16, zeros16], axis=1)], axis=0)
                else:
                    qb_, kb_, e_bb = scaled(eb)
                    lhs_q = jnp.concatenate([qa.astype(BF16), qb_.astype(BF16)], axis=1)
                    rhs_k = jnp.concatenate([jnp.concatenate([ka.astype(BF16), zeros16], axis=1),
                                             jnp.concatenate([zeros16, kb_.astype(BF16)], axis=1)], axis=0)
                dq2 = _dot(dm, rhs_k)
                dk2 = _dot_tn(dm, lhs_q)
                parts = [(ea, qa, ka, e_a, dq2[:, :HEAD_DIM], dk2[0:CHUNK, :HEAD_DIM])]
                if eb is not None:
                    parts.append((eb, qb_, kb_, e_bb, dq2[:, HEAD_DIM:], dk2[CHUNK:2 * CHUNK, HEAD_DIM:]))
                for entry, q_m, k_m, e_m, dq_m, dk_m in parts:
                    if entry == 0:
                        dq = dq + dq_m
                        dkk = dkk + dk_m
                    else:
                        dq = dq + e_m * dq_m
                        dkk = dkk + e_m * dk_m
                        dx_s[(1 + entry) * CHUNK:(2 + entry) * CHUNK, pl.ds(l0, HEAD_DIM)] = (
                            q_m * dq_m + k_m * dk_m).astype(BF16)
            dq_s[pl.ds(r0, CHUNK), :] = dq
            dkk_s[pl.ds(r0, CHUNK), :] = dkk
            return carry

        lax.fori_loop(0, cpb, local, 0, unroll=LOCAL_UNROLL)

        dg_all = _dot(wexpt_ref[...], dx_s[...])
        for b in range(cpb):
            dg_s[b * CHUNK:(b + 1) * CHUNK, :] = dg_all[:, b * HEAD_DIM:(b + 1) * HEAD_DIM]
        t = jnp.where(valid, dg_s[...] / f - dkk_s[...], 0.0)
        dlb_ref[...] += jnp.sum(sn * t, axis=0, keepdims=True)
        dp_ref[0] = jnp.where(valid, dq_s[...], 0.0).astype(BF16)
        dp_ref[1] = ((1.0 - lb) * sg * sn * t).astype(BF16)

        @pl.when((pl.program_id(0) == N_HEADS - 1) & (step == n_rb - 1))
        def _():
            exchange.wait(dw_ref, rxw_ref, blob_ref, rxb_ref, send_sems, recv_sems)

    head_block = lambda seg: pl.BlockSpec((1, rb_rows, HEAD_DIM), lambda h, s: (seg, n_rb - 1 - s, h))
    row_block = pl.BlockSpec((rb_rows, HEAD_DIM), lambda h, s: (n_rb - 1 - s, h))
    return pl.pallas_call(
        body, name="hgrn_backward",
        grid=(N_HEADS, n_rb),
        in_specs=[
            head_block(0), head_block(1), head_block(2),
            row_block,
            pl.BlockSpec((1, cpb, HEAD_DIM, HEAD_DIM), lambda h, s: (h, n_rb - 1 - s, 0, 0)),
            pl.BlockSpec((1, 1, N_EXP * CHUNK, lanes), lambda h, s: (h, n_rb - 1 - s, 0, 0)),
            row_block,
            pl.BlockSpec((2, HEAD_DIM), lambda h, s: (0, h)),
            pl.BlockSpec((CHUNK, N_EXP * CHUNK), lambda h, s: (0, 0)),
            pl.BlockSpec((len(LEVEL_PAIRS), CHUNK, 2 * CHUNK), lambda h, s: (0, 0, 0)),
            ANY, ANY,
        ],
        out_specs=[
            pl.BlockSpec((3, rb_rows, HEAD_DIM), lambda h, s: (0, n_rb - 1 - s, h)),
            pl.BlockSpec((1, HEAD_DIM), lambda h, s: (0, h)),
            ANY, ANY,
        ],
        out_shape=[
            jax.ShapeDtypeStruct((3, rows, D_MODEL), BF16),
            jax.ShapeDtypeStruct((1, D_MODEL), F32),
            exchange.landing_w(), exchange.landing_blob(blob16),
        ],
        scratch_shapes=[
            pltpu.VMEM((HEAD_DIM, HEAD_DIM), F32),
            pltpu.VMEM((cpb, HEAD_DIM, HEAD_DIM), F32),
            pltpu.VMEM((cpb, HEAD_DIM, HEAD_DIM), F32),
            pltpu.VMEM((rb_rows, HEAD_DIM), F32),
            pltpu.VMEM((rb_rows, HEAD_DIM), F32),
            pltpu.VMEM((rb_rows, HEAD_DIM), BF16),
            pltpu.VMEM((rb_rows, HEAD_DIM), BF16),
            pltpu.VMEM((rb_rows, HEAD_DIM), F32),
            pltpu.VMEM((rb_rows, HEAD_DIM), F32),
            pltpu.VMEM((rb_rows, HEAD_DIM), F32),
            pltpu.VMEM((N_EXP * CHUNK, lanes), BF16),
        ] + exchange.semaphores(),
        compiler_params=_params(("arbitrary", "arbitrary")),
    )(p3, p3, p3, d_o, states, e16, a2, lb_logits, wexp_t, masks2, dw16, blob16)


def _hgrn_backward_old(p3, d_o, states, lb_logits, wexp3, wexp_t2, masks, dw16, blob16, rows):
    n_chunks = rows // CHUNK
    cpb = _tile(n_chunks, 13, mult=1)
    rb_rows = cpb * CHUNK
    n_rb = n_chunks // cpb
    hps = HEADS_PER_STEP
    width = hps * HEAD_DIM
    n_hb = N_HEADS // hps
    exchange = _GradExchange(SEGS_MIX, with_blob=True)

    def body(q_ref, fz_ref, v_ref, do_ref, s_ref, lbl_ref, wexp_ref, wexpt_ref, mask_ref, dw_ref, blob_ref,
             dp_ref, dlb_ref, rxw_ref, rxb_ref, dst_ref, send_sems, recv_sems):
        step = pl.program_id(1)
        rb = n_rb - 1 - step

        @pl.when((pl.program_id(0) == 0) & (step == 0))
        def _():
            exchange.start(dw_ref, rxw_ref, blob_ref, rxb_ref, send_sems, recv_sems)

        @pl.when(step == 0)
        def _():
            dst_ref[...] = jnp.zeros_like(dst_ref)
            dlb_ref[...] = jnp.zeros_like(dlb_ref)

        lb_all = _lower_bound(lbl_ref[...])
        wexp = wexp_ref[...]
        wexp_t = wexpt_ref[...]
        masks = mask_ref[...]
        last_row = lax.broadcasted_iota(jnp.int32, (CHUNK, 1), 0) == CHUNK - 1

        def one_head(j, c, r0, valid):
            cols = slice(j * HEAD_DIM, (j + 1) * HEAD_DIM)
            lb = lb_all[:, cols]
            cf = _chunk_forward(q_ref[0, pl.ds(r0, CHUNK), cols], fz_ref[0, pl.ds(r0, CHUNK), cols],
                                lb, valid, wexp, masks)
            q, kk, e_b, e_c = cf["q"], cf["kk"], cf["e_b"], cf["e_c"]
            v16 = jnp.where(valid, v_ref[0, pl.ds(r0, CHUNK), cols], 0.0).astype(BF16)
            do16 = do_ref[pl.ds(r0, CHUNK), cols].astype(BF16)
            st = s_ref[j, c]
            dst = dst_ref[j]
            dst16 = dst.astype(BF16)
            qb = q * e_b
            kc = kk * e_c
            q16, kk16 = q.astype(BF16), kk.astype(BF16)

            dv = _dot_tn(cf["a"].astype(BF16), do16) + _dot_nt(kc.astype(BF16), dst16)
            da = _dot_nt(do16, v16)
            dqb = _dot(do16, st.astype(BF16))
            dkc = _dot(v16, dst16)
            e_last = e_b[CHUNK - 1:CHUNK, :]
            de = jnp.sum(dst * st, axis=0, keepdims=True)
            dst_ref[j] = dst * e_last + _dot_tn(do16, qb.astype(BF16))

            dq = e_b * dqb
            dkk = e_c * dkc
            dx = [qb * dqb + jnp.where(last_row, de * e_last, 0.0), kc * dkc]
            dm0 = (masks[0] * da).astype(BF16)
            dq = dq + _dot(dm0, kk16)
            dkk = dkk + _dot(dm0, q16)
            for l in range(len(LEVELS)):
                e_m = cf["e"][(2 + l) * CHUNK:(3 + l) * CHUNK]
                dm = (masks[1 + l] * da).astype(BF16)
                dqm = _dot(dm, cf["km"][l].astype(BF16))
                dkm = _dot_tn(dm, cf["qm"][l].astype(BF16))
                dq = dq + e_m * dqm
                dkk = dkk + e_m * dkm
                dx.append(cf["qm"][l] * dqm + cf["km"][l] * dkm)
            dxa = jnp.concatenate(dx, axis=0)
            hi = dxa.astype(BF16)
            mid = (dxa - hi.astype(F32)).astype(BF16)
            dg = _dot(wexp_t, jnp.concatenate([hi, mid], axis=0))

            t = jnp.where(valid, dg / cf["f"] - dkk, 0.0)
            dfz = (1.0 - lb) * cf["sg"] * cf["sn"] * t
            dlb_ref[:, cols] += jnp.sum(cf["sn"] * t, axis=0, keepdims=True)
            dp_ref[0, pl.ds(r0, CHUNK), cols] = jnp.where(valid, dq, 0.0).astype(BF16)
            dp_ref[1, pl.ds(r0, CHUNK), cols] = dfz.astype(BF16)
            dp_ref[2, pl.ds(r0, CHUNK), cols] = jnp.where(valid, dv, 0.0).astype(BF16)

        def chunk(i, carry):
            c = cpb - 1 - i
            r0 = pl.multiple_of(c * CHUNK, CHUNK)
            row = rb * rb_rows + r0 + lax.broadcasted_iota(jnp.int32, (CHUNK, 1), 0)
            for j in range(hps):
                one_head(j, c, r0, row >= PAD_ROWS)
            return carry

        lax.fori_loop(0, cpb, chunk, 0)

        @pl.when((pl.program_id(0) == n_hb - 1) & (step == n_rb - 1))
        def _():
            exchange.wait(dw_ref, rxw_ref, blob_ref, rxb_ref, send_sems, recv_sems)

    head_block = lambda seg: pl.BlockSpec((1, rb_rows, width), lambda h, s: (seg, n_rb - 1 - s, h))
    return pl.pallas_call(
        body, name="hgrn_backward",
        grid=(n_hb, n_rb),
        in_specs=[
            head_block(0), head_block(1), head_block(2),
            pl.BlockSpec((rb_rows, width), lambda h, s: (n_rb - 1 - s, h)),
            pl.BlockSpec((hps, cpb, HEAD_DIM, HEAD_DIM), lambda h, s: (h, n_rb - 1 - s, 0, 0)),
            pl.BlockSpec((2, width), lambda h, s: (0, h)),
            pl.BlockSpec((N_EXP * CHUNK, 3 * CHUNK), lambda h, s: (0, 0)),
            pl.BlockSpec((CHUNK, 2 * N_EXP * CHUNK), lambda h, s: (0, 0)),
            pl.BlockSpec((1 + len(LEVELS), CHUNK, CHUNK), lambda h, s: (0, 0, 0)),
            ANY, ANY,
        ],
        out_specs=[
            pl.BlockSpec((3, rb_rows, width), lambda h, s: (0, n_rb - 1 - s, h)),
            pl.BlockSpec((1, width), lambda h, s: (0, h)),
            ANY, ANY,
        ],
        out_shape=[
            jax.ShapeDtypeStruct((3, rows, D_MODEL), BF16),
            jax.ShapeDtypeStruct((1, D_MODEL), F32),
            exchange.landing_w(), exchange.landing_blob(blob16),
        ],
        scratch_shapes=[pltpu.VMEM((hps, HEAD_DIM, HEAD_DIM), F32)] + exchange.semaphores(),
        compiler_params=_params(("arbitrary", "arbitrary")),
    )(p3, p3, p3, d_o, states, lb_logits, wexp3, wexp_t2, masks, dw16, blob16)


def _silu_and_grad(x):
    s, _ = _sigmoid_pair(x)
    return x * s, s * (1.0 + x * (1.0 - s))


def _window_sum(ext, width, forward_looking):
    n = ext.shape[0]
    s = ext
    step = 1
    while step < width:
        s = s + pltpu.roll(s, (n - step) if forward_looking else step, 0)
        step *= 2
    return s


def _mixers(o, p3, z, tgt, wdh, wdp, wout, poolw, hg_w, pool_scale, final_w, rows):
    tm = _tile(rows, 160)
    nt = rows // tm
    halo_blocks = tm // HALO
    n_grp = len(POOL_WINDOWS)

    def body(o_ref, ghg_ref, u_ref, gpl_ref, mhg_ref, mpl_ref, uh_ref, z_ref, t_ref,
             wdh_ref, wdp_ref, wout_ref, pw_ref, hgw_ref, ps_ref, fw_ref,
             do_ref, dz2_ref, dp_ref, dwdh_ref, dwdp_ref, dwout_ref, dpw_ref, small_ref, carry_ref):
        step = pl.program_id(0)
        tile = nt - 1 - step

        @pl.when(step == 0)
        def _():
            dwdh_ref[...] = jnp.zeros_like(dwdh_ref)
            dwdp_ref[...] = jnp.zeros_like(dwdp_ref)
            dwout_ref[...] = jnp.zeros_like(dwout_ref)
            dpw_ref[...] = jnp.zeros_like(dpw_ref)
            small_ref[...] = jnp.zeros_like(small_ref)
            carry_ref[...] = jnp.zeros_like(carry_ref)

        row = tile * tm + lax.broadcasted_iota(jnp.int32, (tm, 1), 0)
        real = row >= PAD_ROWS
        pos1 = jnp.maximum(row - PAD_ROWS + 1, 1).astype(F32)

        u = jnp.where(real, u_ref[0], 0.0)
        halo_row = tile * tm - HALO + lax.broadcasted_iota(jnp.int32, (HALO, 1), 0)
        uh = jnp.where(halo_row >= PAD_ROWS, uh_ref[0], 0.0)
        ext = jnp.concatenate([uh, u], axis=0)
        pooled, inv_cnt, mixed = [], [], []
        for g, w in enumerate(POOL_WINDOWS):
            cols = slice(g * POOL_GDIM, (g + 1) * POOL_GDIM)
            inv = 1.0 / jnp.minimum(pos1, float(w))
            ws = _window_sum(ext[:, cols], w, False)[HALO:]
            pg = (ws * inv - u[:, cols]).astype(BF16)
            pooled.append(pg)
            inv_cnt.append(inv)
            mixed.append(_dot(pg, pw_ref[g]))
        mixed = jnp.concatenate(mixed, axis=1)
        gpl = gpl_ref[0]
        sp, dsp = _silu_and_grad(gpl)
        ps = ps_ref[...]
        a_pool = (mixed * ps * sp).astype(BF16)
        y_pool = _dot(a_pool, wdp_ref[...])

        o = o_ref[...]
        o_hat, rstd_h = [], []
        for h in range(N_HEADS):
            oh = o[:, h * HEAD_DIM:(h + 1) * HEAD_DIM]
            r = lax.rsqrt(jnp.mean(oh * oh, axis=-1, keepdims=True) + EPS)
            rstd_h.append(r)
            o_hat.append(oh * r)
        o_hat = jnp.concatenate(o_hat, axis=1)
        hgw = hgw_ref[...]
        o_n = o_hat * hgw
        ghg = ghg_ref[0]
        sh, dsh = _silu_and_grad(ghg)
        a_hg = (o_n * sh).astype(BF16)
        y_hg = _dot(a_hg, wdh_ref[...])

        s_mh, _ = _sigmoid_pair(mhg_ref[0])
        s_mp, _ = _sigmoid_pair(mpl_ref[0])
        merged = (s_mh * y_hg + s_mp * y_pool).astype(BF16)
        z2 = z_ref[...] + _dot(merged, wout_ref[...])
        rstd2 = lax.rsqrt(jnp.mean(z2 * z2, axis=-1, keepdims=True) + EPS)
        zh = z2 * rstd2
        fw = fw_ref[...]
        err = jnp.where(row >= FIRST_TOKEN_ROW, zh * fw - t_ref[...], 0.0)
        small_ref[ROW_LOSS:ROW_LOSS + 1, :] += jnp.sum(err * err, axis=0, keepdims=True) * (0.5 / D_MODEL)
        dy = err * (1.0 / D_MODEL)

        small_ref[ROW_FINAL_W:ROW_FINAL_W + 1, :] += jnp.sum(dy * zh, axis=0, keepdims=True)
        uu = dy * fw
        dz2 = rstd2 * (uu - zh * jnp.mean(uu * zh, axis=-1, keepdims=True))
        dz2_ref[...] = dz2
        dz2_16 = dz2.astype(BF16)
        dmerged = _dot_nt(dz2_16, wout_ref[...])
        dwout_ref[...] += _dot_tn(merged, dz2_16)
        dy_hg = (s_mh * dmerged).astype(BF16)
        dy_pool = (s_mp * dmerged).astype(BF16)
        dp_ref[3] = (dmerged * y_hg * s_mh * (1.0 - s_mh)).astype(BF16)
        dp_ref[4] = (dmerged * y_pool * s_mp * (1.0 - s_mp)).astype(BF16)

        da_hg = _dot_nt(dy_hg, wdh_ref[...])
        dwdh_ref[...] += _dot_tn(a_hg, dy_hg)
        dp_ref[0] = (da_hg * o_n * dsh).astype(BF16)
        do_n = da_hg * sh
        small_ref[ROW_HG_W:ROW_HG_W + 1, :] += jnp.sum(do_n * o_hat, axis=0, keepdims=True)
        d_hat = do_n * hgw
        for h in range(N_HEADS):
            cols = slice(h * HEAD_DIM, (h + 1) * HEAD_DIM)
            dh_, oh_ = d_hat[:, cols], o_hat[:, cols]
            do_ref[:, cols] = rstd_h[h] * (dh_ - oh_ * jnp.mean(dh_ * oh_, axis=-1, keepdims=True))

        da_pool = _dot_nt(dy_pool, wdp_ref[...])
        dwdp_ref[...] += _dot_tn(a_pool, dy_pool)
        small_ref[ROW_POOL_SCALE:ROW_POOL_SCALE + 1, :] += jnp.sum(da_pool * mixed * sp, axis=0, keepdims=True)
        dp_ref[2] = (da_pool * mixed * ps * dsp).astype(BF16)
        dmixed = (da_pool * ps * sp).astype(BF16)
        carry = carry_ref[...]
        du, new_carry = [], []
        for g, w in enumerate(POOL_WINDOWS):
            cols = slice(g * POOL_GDIM, (g + 1) * POOL_GDIM)
            dmg = dmixed[:, cols]
            dpooled = _dot_nt(dmg, pw_ref[g])
            dpw_ref[g] += _dot_tn(pooled[g], dmg)
            dps = dpooled * inv_cnt[g]
            ext_b = jnp.concatenate([dps, carry[:, cols]], axis=0)
            du.append(_window_sum(ext_b, w, True)[:tm] - dpooled)
            new_carry.append(dps[:HALO])
        dp_ref[1] = jnp.where(real, jnp.concatenate(du, axis=1), 0.0).astype(BF16)
        carry_ref[...] = jnp.concatenate(new_carry, axis=1)

    row_block = pl.BlockSpec((tm, D_MODEL), lambda s: (nt - 1 - s, 0))
    seg_block = lambda seg: pl.BlockSpec((1, tm, D_MODEL), lambda s: (seg, nt - 1 - s, 0))
    whole = pl.BlockSpec(memory_space=pltpu.VMEM)
    return pl.pallas_call(
        body, name="mixers",
        grid=(nt,),
        in_specs=[
            row_block, seg_block(3), seg_block(4), seg_block(5), seg_block(6), seg_block(7),
            pl.BlockSpec((1, HALO, D_MODEL),
                         lambda s: (4, jnp.maximum((nt - 1 - s) * halo_blocks - 1, 0), 0)),
            row_block, row_block,
            whole, whole, whole, whole, whole, whole, whole,
        ],
        out_specs=[
            row_block, row_block,
            pl.BlockSpec((5, tm, D_MODEL), lambda s: (0, nt - 1 - s, 0)),
            whole, whole, whole, whole, whole,
        ],
        out_shape=[
            jax.ShapeDtypeStruct((rows, D_MODEL), F32),
            jax.ShapeDtypeStruct((rows, D_MODEL), F32),
            jax.ShapeDtypeStruct((5, rows, D_MODEL), BF16),
            jax.ShapeDtypeStruct((D_MODEL, D_MODEL), F32),
            jax.ShapeDtypeStruct((D_MODEL, D_MODEL), F32),
            jax.ShapeDtypeStruct((D_MODEL, D_MODEL), F32),
            jax.ShapeDtypeStruct((n_grp, POOL_GDIM, POOL_GDIM), F32),
            jax.ShapeDtypeStruct((SMALL_ROWS, D_MODEL), F32),
        ],
        scratch_shapes=[pltpu.VMEM((HALO, D_MODEL), F32)],
        compiler_params=_params(("arbitrary",)),
    )(o, p3, p3, p3, p3, p3, p3, z, tgt, wdh, wdp, wout, poolw, hg_w, pool_scale, final_w)


def _seg_specs(tm, row_of, seg_of):
    def spec_a(*g):
        k = seg_of(*g)
        return (jnp.minimum(k, 2), jnp.where(k < 3, row_of(*g), 0), 0)

    def spec_b(*g):
        k = seg_of(*g)
        return (jnp.maximum(k - 3, 0), jnp.where(k >= 3, row_of(*g), 0), 0)

    return pl.BlockSpec((1, tm, D_MODEL), spec_a), pl.BlockSpec((1, tm, D_MODEL), spec_b)


def _in_proj_weight_grad(h, dp, rows, name):
    n_seg = dp.shape[0]
    tm = _tile(rows, 1040)
    nt = rows // tm
    half = D_MODEL // 2

    def body(h_ref, dp_ref, part_ref, part16_ref, db_ref, acc_ref, bacc_ref, stage_ref, land_ref,
             send_sems, recv_sems):
        k, i = pl.program_id(0), pl.program_id(1)
        x, y, c = lax.axis_index("x"), lax.axis_index("y"), lax.axis_index("c")

        def to_sibling(seg):
            return pltpu.make_async_remote_copy(
                src_ref=stage_ref.at[seg], dst_ref=land_ref.at[seg], send_sem=send_sems.at[seg],
                recv_sem=recv_sems.at[seg], device_id=(x, y, 1 - c), device_id_type=MESH)

        @pl.when(i == 0)
        def _():
            acc_ref[...] = jnp.zeros_like(acc_ref)
            bacc_ref[...] = jnp.zeros_like(bacc_ref)

        dpt = dp_ref[0]
        acc_ref[...] += _dot_tn(h_ref[...], dpt)
        bacc_ref[...] += jnp.sum(dpt.astype(F32), axis=0, keepdims=True)

        @pl.when(i == nt - 1)
        def _():
            db_ref[0] = bacc_ref[...]
            part_ref[k] = acc_ref[pl.ds(pl.multiple_of(c * half, half), half), :]
            stage_ref[k] = acc_ref[pl.ds(pl.multiple_of((1 - c) * half, half), half), :].astype(BF16)
            to_sibling(k).start()

        @pl.when((k == n_seg - 1) & (i == nt - 1))
        def _():
            for seg in range(n_seg):
                to_sibling(seg).wait_recv()
                total = part_ref[seg] + land_ref[seg].astype(F32)
                part_ref[seg] = total
                part16_ref[seg] = total.astype(BF16)
            for seg in range(n_seg):
                to_sibling(seg).wait_send()

    whole = pl.BlockSpec(memory_space=pltpu.VMEM)
    return pl.pallas_call(
        body, name=name,
        grid=(n_seg, nt),
        in_specs=[pl.BlockSpec((tm, D_MODEL), lambda k, i: (i, 0)),
                  pl.BlockSpec((1, tm, D_MODEL), lambda k, i: (k, i, 0))],
        out_specs=[whole, whole, pl.BlockSpec((1, 1, D_MODEL), lambda k, i: (k, 0, 0))],
        out_shape=[
            jax.ShapeDtypeStruct((n_seg, half, D_MODEL), F32),
            jax.ShapeDtypeStruct((n_seg, half, D_MODEL), BF16),
            jax.ShapeDtypeStruct((n_seg, 1, D_MODEL), F32),
        ],
        scratch_shapes=[
            pltpu.VMEM((D_MODEL, D_MODEL), F32), pltpu.VMEM((1, D_MODEL), F32),
            pltpu.VMEM((n_seg, half, D_MODEL), BF16),
            pltpu.VMEM((n_seg, half, D_MODEL), BF16),
            pltpu.SemaphoreType.DMA((n_seg,)), pltpu.SemaphoreType.DMA((n_seg,)),
        ],
        compiler_params=_params(("arbitrary", "arbitrary")),
    )(h, dp)


def _input_grad(dpa, dpb, w4, z, dz2, norm_w, dw16, rows):
    tm = _tile(rows, 1040)
    nt = rows // tm
    exchange = _GradExchange(SEGS_REC, with_blob=False)

    def body(dpa_ref, dpb_ref, w_ref, z_ref, dz2_ref, nw_ref, dw_ref, dz_ref, dnw_ref, rxw_ref,
             acc_ref, send_sems, recv_sems):
        i, k = pl.program_id(0), pl.program_id(1)

        @pl.when((i == 0) & (k == 0))
        def _():
            exchange.start(dw_ref, rxw_ref, None, None, send_sems, recv_sems)
            dnw_ref[...] = jnp.zeros_like(dnw_ref)

        @pl.when((i == nt - 1) & (k == N_SEG - 1))
        def _():
            exchange.wait(dw_ref, rxw_ref, None, None, send_sems, recv_sems)

        @pl.when(k == 0)
        def _():
            acc_ref[...] = jnp.zeros_like(acc_ref)

        @pl.when(k < 3)
        def _():
            acc_ref[...] += _dot_nt(dpa_ref[0], w_ref[0])

        @pl.when(k >= 3)
        def _():
            acc_ref[...] += _dot_nt(dpb_ref[0], w_ref[0])

        @pl.when(k == N_SEG - 1)
        def _():
            zt = z_ref[...]
            rstd = lax.rsqrt(jnp.mean(zt * zt, axis=-1, keepdims=True) + EPS)
            zh = zt * rstd
            dh = acc_ref[...]
            dnw_ref[...] += jnp.sum(dh * zh, axis=0, keepdims=True)
            uu = dh * nw_ref[...]
            dz_ref[...] = dz2_ref[...] + rstd * (uu - zh * jnp.mean(uu * zh, axis=-1, keepdims=True))

    spec_a, spec_b = _seg_specs(tm, lambda i, k: i, lambda i, k: k)
    last_only = pl.BlockSpec((tm, D_MODEL), lambda i, k: (jnp.where(k == N_SEG - 1, i, 0), 0))
    return pl.pallas_call(
        body, name="input_grad",
        grid=(nt, N_SEG),
        in_specs=[
            spec_a, spec_b,
            pl.BlockSpec((1, D_MODEL, D_MODEL), lambda i, k: (k // 2, 0, k % 2)),
            last_only, last_only,
            pl.BlockSpec((1, D_MODEL), lambda i, k: (0, 0)),
            ANY,
        ],
        out_specs=[
            pl.BlockSpec((tm, D_MODEL), lambda i, k: (i, 0)),
            pl.BlockSpec((1, D_MODEL), lambda i, k: (0, 0)),
            ANY,
        ],
        out_shape=[
            jax.ShapeDtypeStruct((rows, D_MODEL), F32),
            jax.ShapeDtypeStruct((1, D_MODEL), F32),
            exchange.landing_w(),
        ],
        scratch_shapes=[pltpu.VMEM((tm, D_MODEL), F32)] + exchange.semaphores(),
        compiler_params=_params(("arbitrary", "arbitrary")),
    )(dpa, dpb, w4, z, dz2, norm_w, dw16)


def _local_step(z, tgt, w4, blob4, seg_order, norm_w, b_in, lb_logits, hg_w, pool_scale, final_w):
    rows = z.shape[0]
    q = D_MODEL // N_CHIPS
    n_grp = len(POOL_WINDOWS)
    pg = POOL_GDIM // N_CHIPS

    wexp2 = jnp.asarray(np.tile(_exponent_matrix(), (1, 2)), BF16)
    wexp_t = jnp.asarray(_exponent_matrix().T, BF16)
    masks2 = jnp.asarray(_paired_masks(), F32)

    h, p3, w4 = _in_proj(z, norm_w, w4, b_in, seg_order, rows)
    o, states, e16, a2, blob4 = _hgrn_forward(p3, lb_logits, wexp2, masks2, blob4, rows)
    wdh = blob4[:, 0:q].reshape(D_MODEL, D_MODEL)
    wdp = blob4[:, q:2 * q].reshape(D_MODEL, D_MODEL)
    wout = blob4[:, 2 * q:3 * q].reshape(D_MODEL, D_MODEL)
    poolw = blob4[:, 3 * q:].reshape(N_CHIPS, n_grp, pg, POOL_GDIM).transpose(1, 0, 2, 3)
    poolw = poolw.reshape(n_grp, POOL_GDIM, POOL_GDIM)
    d_o, dz2, dpb, dwdh, dwdp, dwout, dpw, small = _mixers(
        o, p3, z, tgt, wdh, wdp, wout, poolw, hg_w, pool_scale, final_w, rows)
    dpw4 = dpw.reshape(n_grp, N_CHIPS, pg, POOL_GDIM).transpose(1, 0, 2, 3)
    dpw4 = dpw4.reshape(N_CHIPS, n_grp * pg * POOL_GDIM // D_MODEL, D_MODEL)
    dblob4 = jnp.concatenate([dwdh.reshape(N_CHIPS, q, D_MODEL), dwdp.reshape(N_CHIPS, q, D_MODEL),
                              dwout.reshape(N_CHIPS, q, D_MODEL), dpw4], axis=1)

    dw_mix, dw_mix16, db_mix = _in_proj_weight_grad(h, dpb, rows, "in_proj_weight_grad_mix")
    dpa, dlb, rxw_mix, rx_blob = _hgrn_backward(
        p3, d_o, states, e16, a2, lb_logits, wexp_t, masks2, dw_mix16, dblob4.astype(BF16), rows)
    dw_rec, dw_rec16, db_rec = _in_proj_weight_grad(h, dpa, rows, "in_proj_weight_grad_rec")
    dz, dnw, rxw_rec = _input_grad(dpa, dpb, w4, z, dz2, norm_w, dw_rec16, rows)

    small = jnp.concatenate([
        small[ROW_LOSS:ROW_LOSS + 1],
        dz[PAD_ROWS:PAD_ROWS + N_META],
        dnw,
        db_rec.reshape(len(SEGS_REC), D_MODEL), db_mix.reshape(len(SEGS_MIX), D_MODEL),
        dlb, jnp.zeros_like(dlb),
        small[ROW_HG_W:ROW_HG_W + 1], small[ROW_POOL_SCALE:ROW_POOL_SCALE + 1],
        small[ROW_FINAL_W:ROW_FINAL_W + 1],
        jnp.zeros((SMALL_ROWS - ROW_FINAL_W - 1, D_MODEL), F32),
    ], axis=0)
    return dz, (dw_rec, dw_mix, rxw_rec, rxw_mix), (dblob4, rx_blob), small


ANY = pl.BlockSpec(memory_space=pl.ANY)
MESH = pl.DeviceIdType.MESH


def _place():
    x, y, c = lax.axis_index("x"), lax.axis_index("y"), lax.axis_index("c")
    chips = [(1 - x, y), (x, 1 - y), (1 - x, 1 - y)]
    return x, y, c, chips


class _ShardGather:
    def __init__(self, rows):
        self.half = rows // 2

    def semaphores(self):
        return [pltpu.SemaphoreType.DMA((6,)), pltpu.SemaphoreType.DMA((6,))]

    def _copy(self, k, slot, to, send_sems, recv_sems):
        return pltpu.make_async_remote_copy(src_ref=slot, dst_ref=slot, send_sem=send_sems.at[k],
                                            recv_sem=recv_sems.at[k], device_id=to, device_id_type=MESH)

    def _half(self, ref4, chip, which):
        return ref4.at[chip, pl.ds(which * self.half, self.half), :]

    def start(self, ref4, send_sems, recv_sems):
        x, y, c, chips = _place()
        for j, (cx, cy) in enumerate(chips):
            self._copy(j, self._half(ref4, 2 * x + y, c), (cx, cy, c), send_sems, recv_sems).start()

    def arrive(self, j, ref4, send_sems, recv_sems):
        x, y, c, chips = _place()
        cx, cy = chips[j]
        landed = self._half(ref4, 2 * cx + cy, c)
        self._copy(j, landed, (cx, cy, c), send_sems, recv_sems).wait_recv()
        self._copy(3 + j, landed, (x, y, 1 - c), send_sems, recv_sems).start()
        self._copy(3 + j, self._half(ref4, 2 * cx + cy, 1 - c), (x, y, 1 - c), send_sems, recv_sems).wait_recv()

    def pass_on_all(self, ref4, send_sems, recv_sems):
        x, y, c, chips = _place()
        for j, (cx, cy) in enumerate(chips):
            landed = self._half(ref4, 2 * cx + cy, c)
            self._copy(j, landed, (cx, cy, c), send_sems, recv_sems).wait_recv()
            self._copy(3 + j, landed, (x, y, 1 - c), send_sems, recv_sems).start()

    def await_sibling_all(self, ref4, send_sems, recv_sems):
        x, y, c, chips = _place()
        for j, (cx, cy) in enumerate(chips):
            self._copy(3 + j, self._half(ref4, 2 * cx + cy, 1 - c), (x, y, 1 - c), send_sems, recv_sems).wait_recv()

    def finish(self, ref4, send_sems, recv_sems):
        x, y, c, chips = _place()
        for j, (cx, cy) in enumerate(chips):
            self._copy(j, self._half(ref4, 2 * x + y, c), (cx, cy, c), send_sems, recv_sems).wait_send()
            self._copy(3 + j, self._half(ref4, 2 * cx + cy, c), (x, y, 1 - c), send_sems, recv_sems).wait_send()


def _gather_meta(m4):
    def body(m_in_ref, m4_ref, send_sems, recv_sems):
        x, y, c, chips = _place()

        def copy(j, slot, to):
            return pltpu.make_async_remote_copy(src_ref=slot, dst_ref=slot, send_sem=send_sems.at[j],
                                                recv_sem=recv_sems.at[j], device_id=to, device_id_type=MESH)

        sends = [copy(j, m4_ref.at[2 * x + y], (cx, cy, c)) for j, (cx, cy) in enumerate(chips)]
        for cp in sends:
            cp.start()
        for j, (cx, cy) in enumerate(chips):
            copy(j, m4_ref.at[2 * cx + cy], (x, y, c)).wait_recv()
        for cp in sends:
            cp.wait_send()

    return pl.pallas_call(
        body, name="gather_meta",
        in_specs=[ANY], out_specs=ANY, out_shape=jax.ShapeDtypeStruct(m4.shape, m4.dtype),
        input_output_aliases={0: 0},
        scratch_shapes=[pltpu.SemaphoreType.DMA((3,)), pltpu.SemaphoreType.DMA((3,))],
    )(m4)


class _GradExchange:
    def __init__(self, segs, with_blob):
        self.segs = tuple(segs)
        self.with_blob = with_blob

    def landing_w(self):
        return jax.ShapeDtypeStruct((N_CHIPS, 2, D_MODEL // 2, D_MODEL), BF16)

    def landing_blob(self, blob16):
        return jax.ShapeDtypeStruct((N_DEV, blob16.shape[1] // 2, D_MODEL), BF16)

    def semaphores(self):
        n_send = len(self.segs) + (2 * N_CHIPS if self.with_blob else 0)
        n_recv = 2 * N_CHIPS + (N_DEV if self.with_blob else 0)
        return [pltpu.SemaphoreType.DMA((n_send,)), pltpu.SemaphoreType.DMA((n_recv,))]

    def _copies(self, dw_ref, rxw_ref, blob_ref, rxb_ref, send_sems, recv_sems):
        x, y, c = lax.axis_index("x"), lax.axis_index("y"), lax.axis_index("c")
        chip = 2 * x + y

        def relation(kx, ky, h):
            return (x ^ kx) * 4 + (y ^ ky) * 2 + (c ^ h)

        def copy(src, dst, send_k, recv_k, to):
            return functools.partial(pltpu.make_async_remote_copy, src_ref=src, dst_ref=dst,
                                     send_sem=send_sems.at[send_k], recv_sem=recv_sems.at[recv_k],
                                     device_id=to, device_id_type=MESH)

        sends, recvs = [], []
        for i, s in enumerate(self.segs):
            kx, ky = (s // 2) >> 1, (s // 2) & 1
            r = (x ^ kx) * 2 + (y ^ ky)
            sends.append((r != 0, copy(dw_ref.at[i], rxw_ref.at[r, s % 2], i, 2 * r + s % 2, (kx, ky, c))))
        for j in range(2):
            mine = [s // 2 for s in self.segs if s % 2 == j]
            if mine:
                cond = functools.reduce(lambda a, b: a | b, [chip == k for k in mine])
                for r in range(1, N_CHIPS):
                    slot = rxw_ref.at[r, j]
                    recvs.append((cond, copy(slot, slot, 0, 2 * r + j, (x, y, c))))
        if self.with_blob:
            hb = blob_ref.shape[1] // 2
            first_send, first_recv = len(self.segs), 2 * N_CHIPS
            for k in range(N_CHIPS):
                for h in range(2):
                    r = relation(k >> 1, k & 1, h)
                    sends.append((r != 0, copy(blob_ref.at[k, pl.ds(h * hb, hb), :], rxb_ref.at[r],
                                               first_send + 2 * k + h, first_recv + r, (k >> 1, k & 1, h))))
            for r in range(1, N_DEV):
                slot = rxb_ref.at[r]
                recvs.append((None, copy(slot, slot, 0, first_recv + r, (x, y, c))))
        return sends, recvs

    def start(self, *refs):
        sends, _ = self._copies(*refs)
        for cond, make in sends:
            pl.when(cond)(lambda make=make: make().start())

    def wait(self, *refs):
        sends, recvs = self._copies(*refs)
        for cond, make in sends:
            pl.when(cond)(lambda make=make: make().wait_send())
        for cond, make in recvs:
            if cond is None:
                make().wait_recv()
            else:
                pl.when(cond)(lambda make=make: make().wait_recv())


def _sum_landed(own, rx_ref):
    total = own
    for r in range(1, rx_ref.shape[0]):
        total = total + rx_ref[r, 0].astype(F32)
    return total


def _finish_w(dw_rec, dw_mix, rx_rec, rx_mix, place_arr):
    half = D_MODEL // 2
    tm = _tile(half, 256)
    n_rec = len(SEGS_REC)

    def body(place_ref, own_rec_ref, own_mix_ref, rx_rec_ref, rx_mix_ref, out_ref):
        seg = 2 * place_ref[0] + pl.program_id(0)

        @pl.when(seg < n_rec)
        def _():
            out_ref[0] = _sum_landed(own_rec_ref[0], rx_rec_ref)

        @pl.when(seg >= n_rec)
        def _():
            out_ref[0] = _sum_landed(own_mix_ref[0], rx_mix_ref)

    def own_spec(first, count):
        def index(j, i, place_ref):
            seg = 2 * place_ref[0] + j
            return (jnp.clip(seg - first, 0, count - 1), i, 0)
        return pl.BlockSpec((1, tm, D_MODEL), index)

    rx_spec = pl.BlockSpec((N_CHIPS, 1, tm, D_MODEL), lambda j, i, place_ref: (0, j, i, 0))
    return pl.pallas_call(
        body, name="finish_w",
        grid_spec=pltpu.PrefetchScalarGridSpec(
            num_scalar_prefetch=1, grid=(2, half // tm),
            in_specs=[own_spec(0, n_rec), own_spec(n_rec, len(SEGS_MIX)), rx_spec, rx_spec],
            out_specs=pl.BlockSpec((1, tm, D_MODEL), lambda j, i, place_ref: (place_ref[1], i, j))),
        out_shape=jax.ShapeDtypeStruct((2, half, 2 * D_MODEL), F32),
        compiler_params=_params(("arbitrary", "arbitrary")),
    )(place_arr, dw_rec, dw_mix, rx_rec, rx_mix)


def _finish_blob(dblob4, rx_blob, place_arr):
    n, rows, cols = rx_blob.shape
    tm = _tile(rows, 256)

    def body(place_ref, own_ref, rx_ref, out_ref):
        out_ref[0] = _sum_landed(own_ref[0, 0], rx_ref)

    return pl.pallas_call(
        body, name="finish_blob",
        grid_spec=pltpu.PrefetchScalarGridSpec(
            num_scalar_prefetch=1, grid=(rows // tm,),
            in_specs=[pl.BlockSpec((1, 1, tm, cols), lambda i, place_ref: (place_ref[0], place_ref[1], i, 0)),
                      pl.BlockSpec((n, 1, tm, cols), lambda i, place_ref: (0, 0, i, 0))],
            out_specs=pl.BlockSpec((1, tm, cols), lambda i, place_ref: (place_ref[1], i, 0))),
        out_shape=jax.ShapeDtypeStruct((2, rows, cols), F32),
        compiler_params=_params(("arbitrary",)),
    )(place_arr, dblob4.reshape(N_CHIPS, 2, rows, cols), rx_blob.reshape(n, 1, rows, cols))


def _share_finished(fw2, fb2, slots):
    def body(w_in_ref, b_in_ref, s_in_ref, w_ref, b_ref, s_ref, send_sems, recv_sems):
        x, y, c, _ = _place()
        sibling = (x, y, 1 - c)

        def copy(k, src, dst, to):
            return pltpu.make_async_remote_copy(src_ref=src, dst_ref=dst, send_sem=send_sems.at[k],
                                                recv_sem=recv_sems.at[k], device_id=to, device_id_type=MESH)

        sends = [copy(0, w_ref.at[c], w_ref.at[c], sibling), copy(1, b_ref.at[c], b_ref.at[c], sibling)]
        for r in range(1, N_DEV):
            peer = (x ^ ((r >> 2) & 1), y ^ ((r >> 1) & 1), c ^ (r & 1))
            sends.append(copy(1 + r, s_ref.at[0], s_ref.at[r], peer))
        for cp in sends:
            cp.start()
        landed = [w_ref.at[1 - c], b_ref.at[1 - c]] + [s_ref.at[r] for r in range(1, N_DEV)]
        for k, slot in enumerate(landed):
            copy(k, slot, slot, (x, y, c)).wait_recv()
        for cp in sends:
            cp.wait_send()

    same = lambda a: jax.ShapeDtypeStruct(a.shape, a.dtype)
    n_sem = 2 + N_DEV - 1
    return pl.pallas_call(
        body, name="share_finished",
        in_specs=[ANY, ANY, ANY], out_specs=[ANY, ANY, ANY],
        out_shape=[same(fw2), same(fb2), same(slots)],
        input_output_aliases={0: 0, 1: 1, 2: 2},
        scratch_shapes=[pltpu.SemaphoreType.DMA((n_sem,)), pltpu.SemaphoreType.DMA((n_sem,))],
    )(fw2, fb2, slots)


def _sum_small(slots, lb_logits, me_arr):
    def body(me_ref, slots_ref, lbl_ref, out_ref):
        me = me_ref[0]
        total = slots_ref[me]
        for d in range(1, N_DEV):
            total = total + slots_ref[d ^ me]
        out_ref[...] = total
        out_ref[ROW_LOSS:ROW_LOSS + 1, :] = jnp.broadcast_to(
            jnp.sum(total[ROW_LOSS:ROW_LOSS + 1, :], axis=-1, keepdims=True), (1, D_MODEL))
        lb = _lower_bound(lbl_ref[...])
        g0 = total[ROW_LB:ROW_LB + 1, :] * lb * (1.0 - lb)
        out_ref[ROW_LB:ROW_LB + 1, :] = g0
        out_ref[ROW_LB + 1:ROW_LB + 2, :] = -g0

    return pl.pallas_call(
        body, name="sum_small",
        grid_spec=pltpu.PrefetchScalarGridSpec(
            num_scalar_prefetch=1, grid=(1,),
            in_specs=[pl.BlockSpec((N_DEV, SMALL_ROWS, D_MODEL), lambda i, me_ref: (0, 0, 0)),
                      pl.BlockSpec((2, D_MODEL), lambda i, me_ref: (0, 0))],
            out_specs=pl.BlockSpec((SMALL_ROWS, D_MODEL), lambda i, me_ref: (0, 0))),
        out_shape=jax.ShapeDtypeStruct((SMALL_ROWS, D_MODEL), F32),
        compiler_params=_params(("arbitrary",)),
    )(me_arr, slots, lb_logits)


def _adamw(w, g, m, v):
    rows, cols = w.shape
    tm = _tile(rows, 256, mult=8) if rows % 8 == 0 else rows
    c1 = 1.0 / (1.0 - ADAM_B1 ** ADAM_STEP)
    c2 = 1.0 / (1.0 - ADAM_B2 ** ADAM_STEP)

    def body(w_ref, g_ref, m_ref, v_ref, d_ref, nm_ref, nv_ref):
        gt = g_ref[...]
        nm = ADAM_B1 * m_ref[...] + (1.0 - ADAM_B1) * gt
        nv = ADAM_B2 * v_ref[...] + (1.0 - ADAM_B2) * (gt * gt)
        nm_ref[...] = nm
        nv_ref[...] = nv
        d_ref[...] = -ADAM_LR * ((nm * c1) / (jnp.sqrt(nv * c2) + ADAM_EPS) + ADAM_WD * w_ref[...])

    blk = pl.BlockSpec((tm, cols), lambda i: (i, 0))
    sds = jax.ShapeDtypeStruct((rows, cols), F32)
    return pl.pallas_call(
        body, name="adamw",
        grid=(rows // tm,), in_specs=[blk] * 4, out_specs=[blk] * 3, out_shape=[sds] * 3,
        compiler_params=_params(("arbitrary",)),
    )(w, g, m, v)


def kernel(x, meta_tokens, norm_w, w_in, b_in, lb_logits, hg_norm_w, pool_w, pool_scale, w_down_hg, w_down_pool, w_out, final_norm_w, loss_target, m_meta_tokens, m_norm_w, m_w_in, m_b_in, m_lb_logits, m_hg_norm_w, m_pool_w, m_pool_scale, m_w_down_hg, m_w_down_pool, m_w_out, m_final_norm_w, v_meta_tokens, v_norm_w, v_w_in, v_b_in, v_lb_logits, v_hg_norm_w, v_pool_w, v_pool_scale, v_w_down_hg, v_w_down_pool, v_w_out, v_final_norm_w):
    seq = x.shape[1]
    xi, yi, ci = lax.axis_index("x"), lax.axis_index("y"), lax.axis_index("c")
    chip = 2 * xi + yi
    place_arr = jnp.stack([chip, ci]).astype(jnp.int32)
    me_arr = jnp.reshape(4 * xi + 2 * yi + ci, (1,)).astype(jnp.int32)
    q = D_MODEL // N_CHIPS

    def blob_of(wdh, wdp, wo, pw):
        return jnp.concatenate([wdh[0], wdp[0], wo[0], pw[0].reshape(-1, D_MODEL)], axis=0)

    def in_every_slot(a):
        return jnp.broadcast_to(a[None], (N_CHIPS,) + a.shape)

    meta4 = _gather_meta(in_every_slot(meta_tokens))
    meta_full = meta4.transpose(1, 0, 2).reshape(N_META, D_MODEL)
    w4 = in_every_slot(w_in[0].astype(BF16))
    blob4 = in_every_slot(blob_of(w_down_hg, w_down_pool, w_out, pool_w).astype(BF16))
    seg_order = jnp.stack([2 * (chip ^ rel) + t for rel in (0, 2, 1, 3) for t in (0, 1)]).astype(jnp.int32)

    z = jnp.concatenate([jnp.zeros((PAD_ROWS, D_MODEL), F32), meta_full, x[0]], axis=0)
    tgt = jnp.concatenate([jnp.zeros((FIRST_TOKEN_ROW, D_MODEL), F32), loss_target[0]], axis=0)
    fw2 = final_norm_w.reshape(1, D_MODEL)
    dz, w_parts, blob_parts, small = _local_step(
        z, tgt, w4, blob4, seg_order, norm_w, b_in, lb_logits, hg_norm_w, pool_scale, fw2)
    grad_x = dz[FIRST_TOKEN_ROW:][None]

    fin_w = _finish_w(*w_parts, place_arr)
    fin_b = _finish_blob(*blob_parts, place_arr)
    gw2, gb2, slots = _share_finished(fin_w, fin_b, jnp.broadcast_to(small[None], (N_DEV,) + small.shape))
    tot = _sum_small(slots, lb_logits, me_arr)
    g_w_in = gw2.reshape(D_MODEL, 2 * D_MODEL)
    g_blob = gb2.reshape(-1, D_MODEL)

    d_win, nm_win, nv_win = _adamw(w_in[0], g_w_in, m_w_in[0], v_w_in[0])
    d_blob, nm_blob, nv_blob = _adamw(
        blob_of(w_down_hg, w_down_pool, w_out, pool_w), g_blob,
        blob_of(m_w_down_hg, m_w_down_pool, m_w_out, m_pool_w),
        blob_of(v_w_down_hg, v_w_down_pool, v_w_out, v_pool_w))
    g_meta = lax.dynamic_slice_in_dim(tot[ROW_META:ROW_META + N_META], chip * q, q, axis=1)
    d_meta, nm_meta, nv_meta = _adamw(meta_tokens, g_meta, m_meta_tokens, v_meta_tokens)

    def rows_of(nw, bi, lbl, hg, ps, fw):
        return jnp.concatenate([nw, bi.reshape(N_SEG, D_MODEL), lbl, hg, ps, fw.reshape(1, D_MODEL),
                                jnp.zeros((2, D_MODEL), F32)], axis=0)

    g_rows = jnp.concatenate([tot[ROW_NORM_W:ROW_FINAL_W + 1], jnp.zeros((2, D_MODEL), F32)], axis=0)
    d_rows, nm_rows, nv_rows = _adamw(
        rows_of(norm_w, b_in, lb_logits, hg_norm_w, pool_scale, final_norm_w), g_rows,
        rows_of(m_norm_w, m_b_in, m_lb_logits, m_hg_norm_w, m_pool_scale, m_final_norm_w),
        rows_of(v_norm_w, v_b_in, v_lb_logits, v_hg_norm_w, v_pool_scale, v_final_norm_w))

    def unblob(b):
        return (b[0:q][None], b[q:2 * q][None], b[2 * q:3 * q][None], b[3 * q:].reshape(pool_w.shape))

    def unrows(r):
        o = ROW_NORM_W
        return dict(norm_w=r[ROW_NORM_W - o:ROW_B_IN - o], b_in=r[ROW_B_IN - o:ROW_LB - o].reshape(1, -1),
                    lb_logits=r[ROW_LB - o:ROW_HG_W - o], hg_norm_w=r[ROW_HG_W - o:ROW_POOL_SCALE - o],
                    pool_scale=r[ROW_POOL_SCALE - o:ROW_FINAL_W - o], final_norm_w=r[ROW_FINAL_W - o])

    def leaves(meta_part, rows_part, win_part, blob_part):
        r = unrows(rows_part)
        wdh, wdp, wo, pw = unblob(blob_part)
        return [meta_part, r["norm_w"], win_part[None], r["b_in"], r["lb_logits"], r["hg_norm_w"], pw,
                r["pool_scale"], wdh, wdp, wo, r["final_norm_w"]]

    loss = tot[ROW_LOSS, 0]
    return (loss, grad_x,
            *leaves(g_meta, g_rows, g_w_in, g_blob),
            *leaves(d_meta, d_rows, d_win, d_blob),
            *leaves(nm_meta, nm_rows, nm_win, nm_blob),
            *leaves(nv_meta, nv_rows, nv_win, nv_blob))
```

```python
import functools

import numpy as np
import jax
import jax.numpy as jnp
from jax import lax
from jax.experimental import pallas as pl
from jax.experimental.pallas import tpu as pltpu

F32 = jnp.float32
BF16 = jnp.bfloat16

D_MODEL = 1024
N_SEG = 8
N_HEADS = 8
HEAD_DIM = 128
CHUNK = 64
N_META = 16
PAD_ROWS = CHUNK - N_META
FIRST_TOKEN_ROW = CHUNK
LEVELS = (32, 16, 8, 4, 2, 1)
N_EXP = 2 + len(LEVELS)
POOL_WINDOWS = (2, 4, 8, 16)
POOL_GDIM = D_MODEL // len(POOL_WINDOWS)
HALO = 16
BACKWARD_UNROLL = 13
LOCAL_UNROLL = 13
HEADS_PER_STEP = 4
EPS = 1e-6
N_CHIPS = 4
N_DEV = 8
SEGS_REC = (0, 1, 2)
SEGS_MIX = (3, 4, 5, 6, 7)

ADAM_LR = 0.001
ADAM_B1 = 0.9
ADAM_B2 = 0.999
ADAM_EPS = 1e-08
ADAM_WD = 0.01
ADAM_STEP = 10

VMEM_LIMIT_BYTES = 56 * 1024 * 1024

ROW_LOSS = 0
ROW_META = 1
ROW_NORM_W = ROW_META + N_META
ROW_B_IN = ROW_NORM_W + 1
ROW_LB = ROW_B_IN + N_SEG
ROW_HG_W = ROW_LB + 2
ROW_POOL_SCALE = ROW_HG_W + 1
ROW_FINAL_W = ROW_POOL_SCALE + 1
SMALL_ROWS = 32


def _tile(total, cap, mult=16):
    best = None
    for t in range(mult, min(total, cap) + 1, mult):
        if total % t == 0:
            best = t
    assert best is not None, (total, cap, mult)
    return best


def _params(sem=None):
    return pltpu.CompilerParams(dimension_semantics=sem, vmem_limit_bytes=VMEM_LIMIT_BYTES)


def _dot(a, b):
    return jnp.dot(a, b, preferred_element_type=F32)


def _dot_nt(a, b):
    return lax.dot_general(a, b, (((1,), (1,)), ((), ())), preferred_element_type=F32)


def _dot_tn(a, b):
    return lax.dot_general(a, b, (((0,), (0,)), ((), ())), preferred_element_type=F32)


def _sigmoid_pair(x):
    t = jnp.exp(-jnp.abs(x))
    r = 1.0 / (1.0 + t)
    pos = x >= 0
    return jnp.where(pos, r, t * r), jnp.where(pos, t * r, r)


def _exponent_matrix():
    t = np.arange(CHUNK)[:, None]
    j = np.arange(CHUNK)[None, :]
    blocks = [j <= t, j > t]
    for m in LEVELS:
        rho = (t // (2 * m)) * (2 * m) + m
        upper = (t >= rho) & (j > rho) & (j <= t)
        lower = (t < rho) & (j > t) & (j <= rho)
        blocks.append(upper | lower)
    return np.concatenate(blocks, axis=0).astype(np.float32)


def _pair_masks():
    t = np.arange(CHUNK)[:, None]
    s = np.arange(CHUNK)[None, :]
    masks = [t == s]
    for m in LEVELS:
        same = (t // (2 * m)) == (s // (2 * m))
        masks.append(same & ((t % (2 * m)) >= m) & ((s % (2 * m)) < m))
    return np.stack(masks).astype(np.float32)


LEVEL_PAIRS = ((0, 1), (2, 3), (4, 5), (6, None))


def _paired_masks():
    m = _pair_masks()
    zero = np.zeros_like(m[0])
    return np.stack([np.concatenate([m[a], zero if b is None else m[b]], axis=1) for a, b in LEVEL_PAIRS])


def _split3(x):
    hi = x.astype(BF16)
    r = x - hi.astype(F32)
    mid = r.astype(BF16)
    lo = (r - mid.astype(F32)).astype(BF16)
    return hi, mid, lo


def _chunk_forward(q, fz, lb, valid, wexp, masks):
    sg, sn = _sigmoid_pair(fz)
    f = lb + (1.0 - lb) * sg
    g = jnp.where(valid, jnp.log(f), 0.0)
    kk = jnp.where(valid, (1.0 - lb) * sn, 0.0)
    q = jnp.where(valid, q, 0.0)
    e = jnp.exp(_dot(wexp, jnp.concatenate(_split3(g), axis=0)))
    e_b = e[0:CHUNK]
    e_c = e[CHUNK:2 * CHUNK]
    a = masks[0] * _dot_nt(q.astype(BF16), kk.astype(BF16))
    qm, km = [], []
    for l in range(len(LEVELS)):
        e_m = e[(2 + l) * CHUNK:(3 + l) * CHUNK]
        qm.append(q * e_m)
        km.append(kk * e_m)
        a = a + masks[1 + l] * _dot_nt(qm[l].astype(BF16), km[l].astype(BF16))
    return dict(sg=sg, sn=sn, f=f, kk=kk, q=q, e=e, e_b=e_b, e_c=e_c, a=a, qm=qm, km=km)


def _lower_bound(lbl):
    return 1.0 / (1.0 + jnp.exp(lbl[1:2, :] - lbl[0:1, :]))


def _in_proj(z, norm_w, w4, b_in, seg_order, rows):
    tm = _tile(rows, 1040)
    nt = rows // tm
    gather = _ShardGather(w4.shape[1])

    def body(order_ref, z_ref, nw_ref, b_ref, w_in_ref, h_ref, p_ref, w4_ref,
             h_all, w_buf, w_sem, send_sems, recv_sems):
        kk, i = pl.program_id(0), pl.program_id(1)

        @pl.when((kk == 0) & (i == 0))
        def _():
            gather.start(w4_ref, send_sems, recv_sems)

        @pl.when(kk == 0)
        def _():
            zt = z_ref[...]
            rstd = lax.rsqrt(jnp.mean(zt * zt, axis=-1, keepdims=True) + EPS)
            h = (zt * rstd * nw_ref[...]).astype(BF16)
            h_all[pl.ds(pl.multiple_of(i * tm, 16), tm), :] = h
            h_ref[...] = h

        for j in range(N_CHIPS - 1):
            @pl.when((kk == 2 + 2 * j) & (i == 0))
            def _(j=j):
                gather.arrive(j, w4_ref, send_sems, recv_sems)

        @pl.when(i == 0)
        def _():
            seg = order_ref[kk]
            cp = pltpu.make_async_copy(
                w4_ref.at[seg // 2, :, pl.ds(pl.multiple_of((seg % 2) * D_MODEL, D_MODEL), D_MODEL)], w_buf, w_sem)
            cp.start()
            cp.wait()

        p_ref[0] = _dot(h_all[pl.ds(pl.multiple_of(i * tm, 16), tm), :], w_buf[...]) + b_ref[...]

        @pl.when((kk == N_SEG - 1) & (i == nt - 1))
        def _():
            gather.finish(w4_ref, send_sems, recv_sems)

    first_pass = lambda kk, i, order_ref: (jnp.where(kk == 0, i, nt - 1), 0)
    return pl.pallas_call(
        body, name="in_proj",
        grid_spec=pltpu.PrefetchScalarGridSpec(
            num_scalar_prefetch=1, grid=(N_SEG, nt),
            in_specs=[
                pl.BlockSpec((tm, D_MODEL), first_pass),
                pl.BlockSpec((1, D_MODEL), lambda kk, i, order_ref: (0, 0)),
                pl.BlockSpec((1, D_MODEL), lambda kk, i, order_ref: (0, order_ref[kk])),
                ANY,
            ],
            out_specs=[
                pl.BlockSpec((tm, D_MODEL), first_pass),
                pl.BlockSpec((1, tm, D_MODEL), lambda kk, i, order_ref: (order_ref[kk], i, 0)),
                ANY,
            ],
            scratch_shapes=[
                pltpu.VMEM((rows, D_MODEL), BF16),
                pltpu.VMEM((D_MODEL, D_MODEL), BF16),
                pltpu.SemaphoreType.DMA,
            ] + gather.semaphores()),
        out_shape=[
            jax.ShapeDtypeStruct((rows, D_MODEL), BF16),
            jax.ShapeDtypeStruct((N_SEG, rows, D_MODEL), F32),
            jax.ShapeDtypeStruct(w4.shape, w4.dtype),
        ],
        input_output_aliases={4: 2},
        compiler_params=_params(("arbitrary", "arbitrary")),
    )(seg_order, z, norm_w, b_in, w4)


def _hgrn_forward(p3, lb_logits, wexp2, masks2, blob4, rows):
    n_chunks = rows // CHUNK
    cpb = _tile(n_chunks, 13, mult=1)
    rb_rows = cpb * CHUNK
    n_rb = n_chunks // cpb
    lanes = cpb * HEAD_DIM
    gather = _ShardGather(blob4.shape[1])

    def body(q_ref, fz_ref, v_ref, lbl_ref, wexp_ref, mask_ref, b_in_ref, o_ref, s_ref, e16_ref, a2_ref, b4_ref,
             st_ref, e_ref, u_ref, q_s, kk_s, v_s, qb_s, oi_s, send_sems, recv_sems):
        rb = pl.program_id(1)

        @pl.when((pl.program_id(0) == 0) & (rb == 0))
        def _():
            gather.start(b4_ref, send_sems, recv_sems)

        @pl.when(rb == 0)
        def _():
            st_ref[...] = jnp.zeros_like(st_ref)

        lb = _lower_bound(lbl_ref[...])
        row = rb * rb_rows + lax.broadcasted_iota(jnp.int32, (rb_rows, 1), 0)
        valid = row >= PAD_ROWS
        sg, sn = _sigmoid_pair(fz_ref[0])
        g = jnp.where(valid, jnp.log(lb + (1.0 - lb) * sg), 0.0)
        kk_s[...] = jnp.where(valid, (1.0 - lb) * sn, 0.0)
        q_s[...] = jnp.where(valid, q_ref[0], 0.0)
        v_s[...] = jnp.where(valid, v_ref[0], 0.0).astype(BF16)
        hi = g.astype(BF16)
        mid = (g - hi.astype(F32)).astype(BF16)
        g2 = jnp.concatenate(
            [jnp.concatenate([hi[b * CHUNK:(b + 1) * CHUNK], mid[b * CHUNK:(b + 1) * CHUNK]], axis=0)
             for b in range(cpb)], axis=1)
        e_ref[...] = jnp.exp(_dot(wexp_ref[...], g2))
        e16_ref[0, 0] = e_ref[...].astype(BF16)

        zeros16 = jnp.zeros((CHUNK, HEAD_DIM), BF16)

        def local(b, carry):
            r0 = pl.multiple_of(b * CHUNK, CHUNK)
            l0 = pl.multiple_of(b * HEAD_DIM, HEAD_DIM)
            q = q_s[pl.ds(r0, CHUNK), :]
            kk = kk_s[pl.ds(r0, CHUNK), :]
            v16 = v_s[pl.ds(r0, CHUNK), :]

            def scaled(entry):
                if entry == 0:
                    return q.astype(BF16), kk.astype(BF16)
                e_m = e_ref[(1 + entry) * CHUNK:(2 + entry) * CHUNK, pl.ds(l0, HEAD_DIM)]
                return (q * e_m).astype(BF16), (kk * e_m).astype(BF16)

            a2 = jnp.zeros((CHUNK, 2 * CHUNK), F32)
            for p, (ea, eb) in enumerate(LEVEL_PAIRS):
                qa, ka = scaled(ea)
                if eb is None:
                    prod = _dot_nt(qa, jnp.concatenate([ka, zeros16], axis=0))
                else:
                    qb_, kb_ = scaled(eb)
                    rhs = jnp.concatenate([jnp.concatenate([ka, zeros16], axis=1),
                                           jnp.concatenate([zeros16, kb_], axis=1)], axis=0)
                    prod = _dot_nt(jnp.concatenate([qa, qb_], axis=1), rhs)
                a2 = a2 + mask_ref[p] * prod
            a2_16 = a2.astype(BF16)
            a2_ref[pl.ds(r0, CHUNK), :] = a2_16
            oi_s[pl.ds(r0, CHUNK), :] = _dot(a2_16, jnp.concatenate([v16, v16], axis=0))
            e_b = e_ref[0:CHUNK, pl.ds(l0, HEAD_DIM)]
            e_c = e_ref[CHUNK:2 * CHUNK, pl.ds(l0, HEAD_DIM)]
            qb_s[pl.ds(r0, CHUNK), :] = (q * e_b).astype(BF16)
            u_ref[b] = _dot_tn(v16, (kk * e_c).astype(BF16))
            return carry

        lax.fori_loop(0, cpb, local, 0, unroll=LOCAL_UNROLL)

        def recur(b, st):
            l0 = pl.multiple_of(b * HEAD_DIM, HEAD_DIM)
            s_ref[0, b] = st
            return st * e_ref[CHUNK - 1:CHUNK, pl.ds(l0, HEAD_DIM)] + u_ref[b]

        st_ref[...] = lax.fori_loop(0, cpb, recur, st_ref[...])

        def inter(b, carry):
            r0 = pl.multiple_of(b * CHUNK, CHUNK)
            o_ref[pl.ds(r0, CHUNK), :] = oi_s[pl.ds(r0, CHUNK), :] + _dot_nt(
                qb_s[pl.ds(r0, CHUNK), :], s_ref[0, b].astype(BF16))
            return carry

        lax.fori_loop(0, cpb, inter, 0, unroll=LOCAL_UNROLL)

        @pl.when((pl.program_id(0) == N_HEADS // 2) & (rb == 0))
        def _():
            gather.pass_on_all(b4_ref, send_sems, recv_sems)

        @pl.when((pl.program_id(0) == N_HEADS - 1) & (rb == n_rb - 1))
        def _():
            gather.await_sibling_all(b4_ref, send_sems, recv_sems)
            gather.finish(b4_ref, send_sems, recv_sems)

    head_block = lambda seg: pl.BlockSpec((1, rb_rows, HEAD_DIM), lambda h, r: (seg, r, h))
    return pl.pallas_call(
        body, name="hgrn_forward",
        grid=(N_HEADS, n_rb),
        in_specs=[
            head_block(0), head_block(1), head_block(2),
            pl.BlockSpec((2, HEAD_DIM), lambda h, r: (0, h)),
            pl.BlockSpec((N_EXP * CHUNK, 2 * CHUNK), lambda h, r: (0, 0)),
            pl.BlockSpec((len(LEVEL_PAIRS), CHUNK, 2 * CHUNK), lambda h, r: (0, 0, 0)),
            ANY,
        ],
        out_specs=[
            pl.BlockSpec((rb_rows, HEAD_DIM), lambda h, r: (r, h)),
            pl.BlockSpec((1, cpb, HEAD_DIM, HEAD_DIM), lambda h, r: (h, r, 0, 0)),
            pl.BlockSpec((1, 1, N_EXP * CHUNK, lanes), lambda h, r: (h, r, 0, 0)),
            pl.BlockSpec((rb_rows, HEAD_DIM), lambda h, r: (r, h)),
            ANY,
        ],
        out_shape=[
            jax.ShapeDtypeStruct((rows, D_MODEL), F32),
            jax.ShapeDtypeStruct((N_HEADS, n_chunks, HEAD_DIM, HEAD_DIM), F32),
            jax.ShapeDtypeStruct((N_HEADS, n_rb, N_EXP * CHUNK, lanes), BF16),
            jax.ShapeDtypeStruct((rows, D_MODEL), BF16),
            jax.ShapeDtypeStruct(blob4.shape, blob4.dtype),
        ],
        input_output_aliases={6: 4},
        scratch_shapes=[
            pltpu.VMEM((HEAD_DIM, HEAD_DIM), F32),
            pltpu.VMEM((N_EXP * CHUNK, lanes), F32),
            pltpu.VMEM((cpb, HEAD_DIM, HEAD_DIM), F32),
            pltpu.VMEM((rb_rows, HEAD_DIM), F32),
            pltpu.VMEM((rb_rows, HEAD_DIM), F32),
            pltpu.VMEM((rb_rows, HEAD_DIM), BF16),
            pltpu.VMEM((rb_rows, HEAD_DIM), BF16),
            pltpu.VMEM((rb_rows, HEAD_DIM), F32),
        ] + gather.semaphores(),
        compiler_params=_params(("arbitrary", "arbitrary")),
    )(p3, p3, p3, lb_logits, wexp2, masks2, blob4)


def _hgrn_forward_old(p3, lb_logits, wexp3, masks, rows):
    n_chunks = rows // CHUNK
    cpb = _tile(n_chunks, 13, mult=1)
    rb_rows = cpb * CHUNK
    hps = HEADS_PER_STEP
    width = hps * HEAD_DIM

    def body(q_ref, fz_ref, v_ref, lbl_ref, wexp_ref, mask_ref, o_ref, s_ref, st_ref):
        rb = pl.program_id(1)

        @pl.when(rb == 0)
        def _():
            st_ref[...] = jnp.zeros_like(st_ref)

        lb_all = _lower_bound(lbl_ref[...])
        wexp = wexp_ref[...]
        masks = mask_ref[...]

        def chunk(c, carry):
            r0 = pl.multiple_of(c * CHUNK, CHUNK)
            row = rb * rb_rows + r0 + lax.broadcasted_iota(jnp.int32, (CHUNK, 1), 0)
            valid = row >= PAD_ROWS
            q_all = q_ref[0, pl.ds(r0, CHUNK), :]
            fz_all = fz_ref[0, pl.ds(r0, CHUNK), :]
            v_all = jnp.where(valid, v_ref[0, pl.ds(r0, CHUNK), :], 0.0).astype(BF16)
            st_all = [st_ref[j] for j in range(hps)]
            o_all, st_new = [], []
            for j in range(hps):
                cols = slice(j * HEAD_DIM, (j + 1) * HEAD_DIM)
                cf = _chunk_forward(q_all[:, cols], fz_all[:, cols], lb_all[:, cols], valid, wexp, masks)
                v16 = v_all[:, cols]
                o = _dot_nt((cf["q"] * cf["e_b"]).astype(BF16), st_all[j].astype(BF16))
                o_all.append(o + _dot(cf["a"].astype(BF16), v16))
                kc16 = (cf["kk"] * cf["e_c"]).astype(BF16)
                st_new.append(st_all[j] * cf["e_b"][CHUNK - 1:CHUNK, :] + _dot_tn(v16, kc16))
            o_ref[pl.ds(r0, CHUNK), :] = jnp.concatenate(o_all, axis=1)
            for j in range(hps):
                s_ref[j, c] = st_all[j]
                st_ref[j] = st_new[j]
            return carry

        lax.fori_loop(0, cpb, chunk, 0)

    head_block = lambda seg: pl.BlockSpec((1, rb_rows, width), lambda h, r: (seg, r, h))
    return pl.pallas_call(
        body, name="hgrn_forward",
        grid=(N_HEADS // hps, n_chunks // cpb),
        in_specs=[
            head_block(0), head_block(1), head_block(2),
            pl.BlockSpec((2, width), lambda h, r: (0, h)),
            pl.BlockSpec((N_EXP * CHUNK, 3 * CHUNK), lambda h, r: (0, 0)),
            pl.BlockSpec((1 + len(LEVELS), CHUNK, CHUNK), lambda h, r: (0, 0, 0)),
        ],
        out_specs=[
            pl.BlockSpec((rb_rows, width), lambda h, r: (r, h)),
            pl.BlockSpec((hps, cpb, HEAD_DIM, HEAD_DIM), lambda h, r: (h, r, 0, 0)),
        ],
        out_shape=[
            jax.ShapeDtypeStruct((rows, D_MODEL), F32),
            jax.ShapeDtypeStruct((N_HEADS, n_chunks, HEAD_DIM, HEAD_DIM), F32),
        ],
        scratch_shapes=[pltpu.VMEM((hps, HEAD_DIM, HEAD_DIM), F32)],
        compiler_params=_params(("arbitrary", "arbitrary")),
    )(p3, p3, p3, lb_logits, wexp3, masks)


def _hgrn_backward(p3, d_o, states, e16, a2, lb_logits, wexp_t, masks2, dw16, blob16, rows):
    n_chunks = rows // CHUNK
    cpb = _tile(n_chunks, 13, mult=1)
    rb_rows = cpb * CHUNK
    n_rb = n_chunks // cpb
    lanes = cpb * HEAD_DIM
    exchange = _GradExchange(SEGS_MIX, with_blob=True)

    def body(q_ref, fz_ref, v_ref, do_ref, s_ref, e_ref, a2_ref, lbl_ref, wexpt_ref, mask_ref, dw_ref, blob_ref,
             dp_ref, dlb_ref, rxw_ref, rxb_ref,
             dst_ref, g_ref, dsn_ref, q_s, kk_s, v_s, do_s, dq_s, dkk_s, dg_s, dx_s, send_sems, recv_sems):
        step = pl.program_id(1)
        rb = n_rb - 1 - step

        @pl.when((pl.program_id(0) == 0) & (step == 0))
        def _():
            exchange.start(dw_ref, rxw_ref, blob_ref, rxb_ref, send_sems, recv_sems)

        @pl.when(step == 0)
        def _():
            dst_ref[...] = jnp.zeros_like(dst_ref)
            dlb_ref[...] = jnp.zeros_like(dlb_ref)

        lb = _lower_bound(lbl_ref[...])
        row = rb * rb_rows + lax.broadcasted_iota(jnp.int32, (rb_rows, 1), 0)
        valid = row >= PAD_ROWS
        sg, sn = _sigmoid_pair(fz_ref[0])
        f = lb + (1.0 - lb) * sg
        g = jnp.where(valid, jnp.log(f), 0.0)
        kk_s[...] = jnp.where(valid, (1.0 - lb) * sn, 0.0)
        q_s[...] = jnp.where(valid, q_ref[0], 0.0)
        v_s[...] = jnp.where(valid, v_ref[0], 0.0).astype(BF16)
        do_s[...] = do_ref[...].astype(BF16)
        e_last_all = jnp.exp(jnp.concatenate(
            [jnp.sum(g[b * CHUNK:(b + 1) * CHUNK], axis=0, keepdims=True) for b in range(cpb)], axis=0))
        last_row = lax.broadcasted_iota(jnp.int32, (CHUNK, 1), 0) == CHUNK - 1
        zeros16 = jnp.zeros((CHUNK, HEAD_DIM), BF16)

        def factor(block, l0):
            return e_ref[0, 0, block * CHUNK:(block + 1) * CHUNK, pl.ds(l0, HEAD_DIM)].astype(F32)

        def contribution(b, carry):
            r0 = pl.multiple_of(b * CHUNK, CHUNK)
            l0 = pl.multiple_of(b * HEAD_DIM, HEAD_DIM)
            qb16 = (q_s[pl.ds(r0, CHUNK), :] * factor(0, l0)).astype(BF16)
            g_ref[b] = _dot_tn(do_s[pl.ds(r0, CHUNK), :], qb16)
            return carry

        lax.fori_loop(0, cpb, contribution, 0, unroll=LOCAL_UNROLL)

        cur = dst_ref[...]
        for b in reversed(range(cpb)):
            dsn_ref[b] = cur
            cur = cur * e_last_all[b:b + 1, :] + g_ref[b]
        dst_ref[...] = cur

        def local(b, carry):
            r0 = pl.multiple_of(b * CHUNK, CHUNK)
            l0 = pl.multiple_of(b * HEAD_DIM, HEAD_DIM)
            q = q_s[pl.ds(r0, CHUNK), :]
            kk = kk_s[pl.ds(r0, CHUNK), :]
            v16 = v_s[pl.ds(r0, CHUNK), :]
            do16 = do_s[pl.ds(r0, CHUNK), :]
            st = s_ref[0, b]
            dsn = dsn_ref[b]
            dsn16 = dsn.astype(BF16)
            e_b, e_c = factor(0, l0), factor(1, l0)
            qb, kc = q * e_b, kk * e_c

            t = _dot_tn(a2_ref[pl.ds(r0, CHUNK), :], do16)
            dv = t[0:CHUNK] + t[CHUNK:2 * CHUNK] + _dot_nt(kc.astype(BF16), dsn16)
            dp_ref[2, pl.ds(r0, CHUNK), :] = dv.astype(BF16)
            da2 = _dot_nt(do16, jnp.concatenate([v16, v16], axis=0))
            dqb = _dot(do16, st.astype(BF16))
            dkc = _dot(v16, dsn16)
            de = jnp.sum(dsn * st, axis=0, keepdims=True) * e_b[CHUNK - 1:CHUNK, :]
            dq = e_b * dqb
            dkk = e_c * dkc
            dx_s[0:CHUNK, pl.ds(l0, HEAD_DIM)] = (qb * dqb + jnp.where(last_row, de, 0.0)).astype(BF16)
            dx_s[CHUNK:2 * CHUNK, pl.ds(l0, HEAD_DIM)] = (kc * dkc).astype(BF16)

            def scaled(entry):
                if entry == 0:
                    return q, kk, None
                e_m = factor(1 + entry, l0)
                return q * e_m, kk * e_m, e_m

            for p, (ea, eb) in enumerate(LEVEL_PAIRS):
                dm = (mask_ref[p] * da2).astype(BF16)
                qa, ka, e_a = scaled(ea)
                if eb is None:
                    lhs_q = jnp.concatenate([qa.astype(BF16), zeros16], axis=1)
                    rhs_k = jnp.concatenate([jnp.concatenate([ka.astype(BF16), zeros16], axis=1),
                                             jnp.concatenate([zeros16, zeros16], axis=1)], axis=0)
                else:
                    qb_, kb_, e_bb = scaled(eb)
                    lhs_q = jnp.concatenate([qa.astype(BF16), qb_.astype(BF16)], axis=1)
                    rhs_k = jnp.concatenate([jnp.concatenate([ka.astype(BF16), zeros16], axis=1),
                                             jnp.concatenate([zeros16, kb_.astype(BF16)], axis=1)], axis=0)
                dq2 = _dot(dm, rhs_k)
                dk2 = _dot_tn(dm, lhs_q)
                parts = [(ea, qa, ka, e_a, dq2[:, :HEAD_DIM], dk2[0:CHUNK, :HEAD_DIM])]
                if eb is not None:
                    parts.append((eb, qb_, kb_, e_bb, dq2[:, HEAD_DIM:], dk2[CHUNK:2 * CHUNK, HEAD_DIM:]))
                for entry, q_m, k_m, e_m, dq_m, dk_m in parts:
                    if entry == 0:
                        dq = dq + dq_m
                        dkk = dkk + dk_m
                    else:
                        dq = dq + e_m * dq_m
                        dkk = dkk + e_m * dk_m
                        dx_s[(1 + entry) * CHUNK:(2 + entry) * CHUNK, pl.ds(l0, HEAD_DIM)] = (
                            q_m * dq_m + k_m * dk_m).astype(BF16)
            dq_s[pl.ds(r0, CHUNK), :] = dq
            dkk_s[pl.ds(r0, CHUNK), :] = dkk
            return carry

        lax.fori_loop(0, cpb, local, 0, unroll=BACKWARD_UNROLL)

        dg_all = _dot(wexpt_ref[...], dx_s[...])
        for b in range(cpb):
            dg_s[b * CHUNK:(b + 1) * CHUNK, :] = dg_all[:, b * HEAD_DIM:(b + 1) * HEAD_DIM]
        t = jnp.where(valid, dg_s[...] / f - dkk_s[...], 0.0)
        dlb_ref[...] += jnp.sum(sn * t, axis=0, keepdims=True)
        dp_ref[0] = jnp.where(valid, dq_s[...], 0.0).astype(BF16)
        dp_ref[1] = ((1.0 - lb) * sg * sn * t).astype(BF16)

        @pl.when((pl.program_id(0) == N_HEADS - 1) & (step == n_rb - 1))
        def _():
            exchange.wait(dw_ref, rxw_ref, blob_ref, rxb_ref, send_sems, recv_sems)

    head_block = lambda seg: pl.BlockSpec((1, rb_rows, HEAD_DIM), lambda h, s: (seg, n_rb - 1 - s, h))
    row_block = pl.BlockSpec((rb_rows, HEAD_DIM), lambda h, s: (n_rb - 1 - s, h))
    return pl.pallas_call(
        body, name="hgrn_backward",
        grid=(N_HEADS, n_rb),
        in_specs=[
            head_block(0), head_block(1), head_block(2),
            row_block,
            pl.BlockSpec((1, cpb, HEAD_DIM, HEAD_DIM), lambda h, s: (h, n_rb - 1 - s, 0, 0)),
            pl.BlockSpec((1, 1, N_EXP * CHUNK, lanes), lambda h, s: (h, n_rb - 1 - s, 0, 0)),
            row_block,
            pl.BlockSpec((2, HEAD_DIM), lambda h, s: (0, h)),
            pl.BlockSpec((CHUNK, N_EXP * CHUNK), lambda h, s: (0, 0)),
            pl.BlockSpec((len(LEVEL_PAIRS), CHUNK, 2 * CHUNK), lambda h, s: (0, 0, 0)),
            ANY, ANY,
        ],
        out_specs=[
            pl.BlockSpec((3, rb_rows, HEAD_DIM), lambda h, s: (0, n_rb - 1 - s, h)),
            pl.BlockSpec((1, HEAD_DIM), lambda h, s: (0, h)),
            ANY, ANY,
        ],
        out_shape=[
            jax.ShapeDtypeStruct((3, rows, D_MODEL), BF16),
            jax.ShapeDtypeStruct((1, D_MODEL), F32),
            exchange.landing_w(), exchange.landing_blob(blob16),
        ],
        scratch_shapes=[
            pltpu.VMEM((HEAD_DIM, HEAD_DIM), F32),
            pltpu.VMEM((cpb, HEAD_DIM, HEAD_DIM), F32),
            pltpu.VMEM((cpb, HEAD_DIM, HEAD_DIM), F32),
            pltpu.VMEM((rb_rows, HEAD_DIM), F32),
            pltpu.VMEM((rb_rows, HEAD_DIM), F32),
            pltpu.VMEM((rb_rows, HEAD_DIM), BF16),
            pltpu.VMEM((rb_rows, HEAD_DIM), BF16),
            pltpu.VMEM((rb_rows, HEAD_DIM), F32),
            pltpu.VMEM((rb_rows, HEAD_DIM), F32),
            pltpu.VMEM((rb_rows, HEAD_DIM), F32),
            pltpu.VMEM((N_EXP * CHUNK, lanes), BF16),
        ] + exchange.semaphores(),
        compiler_params=_params(("arbitrary", "arbitrary")),
    )(p3, p3, p3, d_o, states, e16, a2, lb_logits, wexp_t, masks2, dw16, blob16)


def _hgrn_backward_old(p3, d_o, states, lb_logits, wexp3, wexp_t2, masks, dw16, blob16, rows):
    n_chunks = rows // CHUNK
    cpb = _tile(n_chunks, 13, mult=1)
    rb_rows = cpb * CHUNK
    n_rb = n_chunks // cpb
    hps = HEADS_PER_STEP
    width = hps * HEAD_DIM
    n_hb = N_HEADS // hps
    exchange = _GradExchange(SEGS_MIX, with_blob=True)

    def body(q_ref, fz_ref, v_ref, do_ref, s_ref, lbl_ref, wexp_ref, wexpt_ref, mask_ref, dw_ref, blob_ref,
             dp_ref, dlb_ref, rxw_ref, rxb_ref, dst_ref, send_sems, recv_sems):
        step = pl.program_id(1)
        rb = n_rb - 1 - step

        @pl.when((pl.program_id(0) == 0) & (step == 0))
        def _():
            exchange.start(dw_ref, rxw_ref, blob_ref, rxb_ref, send_sems, recv_sems)

        @pl.when(step == 0)
        def _():
            dst_ref[...] = jnp.zeros_like(dst_ref)
            dlb_ref[...] = jnp.zeros_like(dlb_ref)

        lb_all = _lower_bound(lbl_ref[...])
        wexp = wexp_ref[...]
        wexp_t = wexpt_ref[...]
        masks = mask_ref[...]
        last_row = lax.broadcasted_iota(jnp.int32, (CHUNK, 1), 0) == CHUNK - 1

        def one_head(j, c, r0, valid):
            cols = slice(j * HEAD_DIM, (j + 1) * HEAD_DIM)
            lb = lb_all[:, cols]
            cf = _chunk_forward(q_ref[0, pl.ds(r0, CHUNK), cols], fz_ref[0, pl.ds(r0, CHUNK), cols],
                                lb, valid, wexp, masks)
            q, kk, e_b, e_c = cf["q"], cf["kk"], cf["e_b"], cf["e_c"]
            v16 = jnp.where(valid, v_ref[0, pl.ds(r0, CHUNK), cols], 0.0).astype(BF16)
            do16 = do_ref[pl.ds(r0, CHUNK), cols].astype(BF16)
            st = s_ref[j, c]
            dst = dst_ref[j]
            dst16 = dst.astype(BF16)
            qb = q * e_b
            kc = kk * e_c
            q16, kk16 = q.astype(BF16), kk.astype(BF16)

            dv = _dot_tn(cf["a"].astype(BF16), do16) + _dot_nt(kc.astype(BF16), dst16)
            da = _dot_nt(do16, v16)
            dqb = _dot(do16, st.astype(BF16))
            dkc = _dot(v16, dst16)
            e_last = e_b[CHUNK - 1:CHUNK, :]
            de = jnp.sum(dst * st, axis=0, keepdims=True)
            dst_ref[j] = dst * e_last + _dot_tn(do16, qb.astype(BF16))

            dq = e_b * dqb
            dkk = e_c * dkc
            dx = [qb * dqb + jnp.where(last_row, de * e_last, 0.0), kc * dkc]
            dm0 = (masks[0] * da).astype(BF16)
            dq = dq + _dot(dm0, kk16)
            dkk = dkk + _dot(dm0, q16)
            for l in range(len(LEVELS)):
                e_m = cf["e"][(2 + l) * CHUNK:(3 + l) * CHUNK]
                dm = (masks[1 + l] * da).astype(BF16)
                dqm = _dot(dm, cf["km"][l].astype(BF16))
                dkm = _dot_tn(dm, cf["qm"][l].astype(BF16))
                dq = dq + e_m * dqm
                dkk = dkk + e_m * dkm
                dx.append(cf["qm"][l] * dqm + cf["km"][l] * dkm)
            dxa = jnp.concatenate(dx, axis=0)
            hi = dxa.astype(BF16)
            mid = (dxa - hi.astype(F32)).astype(BF16)
            dg = _dot(wexp_t, jnp.concatenate([hi, mid], axis=0))

            t = jnp.where(valid, dg / cf["f"] - dkk, 0.0)
            dfz = (1.0 - lb) * cf["sg"] * cf["sn"] * t
            dlb_ref[:, cols] += jnp.sum(cf["sn"] * t, axis=0, keepdims=True)
            dp_ref[0, pl.ds(r0, CHUNK), cols] = jnp.where(valid, dq, 0.0).astype(BF16)
            dp_ref[1, pl.ds(r0, CHUNK), cols] = dfz.astype(BF16)
            dp_ref[2, pl.ds(r0, CHUNK), cols] = jnp.where(valid, dv, 0.0).astype(BF16)

        def chunk(i, carry):
            c = cpb - 1 - i
            r0 = pl.multiple_of(c * CHUNK, CHUNK)
            row = rb * rb_rows + r0 + lax.broadcasted_iota(jnp.int32, (CHUNK, 1), 0)
            for j in range(hps):
                one_head(j, c, r0, row >= PAD_ROWS)
            return carry

        lax.fori_loop(0, cpb, chunk, 0)

        @pl.when((pl.program_id(0) == n_hb - 1) & (step == n_rb - 1))
        def _():
            exchange.wait(dw_ref, rxw_ref, blob_ref, rxb_ref, send_sems, recv_sems)

    head_block = lambda seg: pl.BlockSpec((1, rb_rows, width), lambda h, s: (seg, n_rb - 1 - s, h))
    return pl.pallas_call(
        body, name="hgrn_backward",
        grid=(n_hb, n_rb),
        in_specs=[
            head_block(0), head_block(1), head_block(2),
            pl.BlockSpec((rb_rows, width), lambda h, s: (n_rb - 1 - s, h)),
            pl.BlockSpec((hps, cpb, HEAD_DIM, HEAD_DIM), lambda h, s: (h, n_rb - 1 - s, 0, 0)),
            pl.BlockSpec((2, width), lambda h, s: (0, h)),
            pl.BlockSpec((N_EXP * CHUNK, 3 * CHUNK), lambda h, s: (0, 0)),
            pl.BlockSpec((CHUNK, 2 * N_EXP * CHUNK), lambda h, s: (0, 0)),
            pl.BlockSpec((1 + len(LEVELS), CHUNK, CHUNK), lambda h, s: (0, 0, 0)),
            ANY, ANY,
        ],
        out_specs=[
            pl.BlockSpec((3, rb_rows, width), lambda h, s: (0, n_rb - 1 - s, h)),
            pl.BlockSpec((1, width), lambda h, s: (0, h)),
            ANY, ANY,
        ],
        out_shape=[
            jax.ShapeDtypeStruct((3, rows, D_MODEL), BF16),
            jax.ShapeDtypeStruct((1, D_MODEL), F32),
            exchange.landing_w(), exchange.landing_blob(blob16),
        ],
        scratch_shapes=[pltpu.VMEM((hps, HEAD_DIM, HEAD_DIM), F32)] + exchange.semaphores(),
        compiler_params=_params(("arbitrary", "arbitrary")),
    )(p3, p3, p3, d_o, states, lb_logits, wexp3, wexp_t2, masks, dw16, blob16)


def _sigmoid(x):
    return 1.0 / (1.0 + jnp.exp(-x))


def _silu_and_grad(x):
    s = _sigmoid(x)
    return x * s, s * (1.0 + x * (1.0 - s))


def _window_sum(ext, width, forward_looking):
    n = ext.shape[0]
    s = ext
    step = 1
    while step < width:
        s = s + pltpu.roll(s, (n - step) if forward_looking else step, 0)
        step *= 2
    return s


def _mixers(o, p3, z, tgt, wdh, wdp, wout, poolw, hg_w, pool_scale, final_w, rows):
    tm = _tile(rows, 208)
    nt = rows // tm
    halo_blocks = tm // HALO
    n_grp = len(POOL_WINDOWS)

    def body(o_ref, ghg_ref, u_ref, gpl_ref, mhg_ref, mpl_ref, uh_ref, z_ref, t_ref,
             wdh_ref, wdp_ref, wout_ref, pw_ref, hgw_ref, ps_ref, fw_ref,
             do_ref, dz2_ref, dp_ref, dwdh_ref, dwdp_ref, dwout_ref, dpw_ref, small_ref, carry_ref):
        step = pl.program_id(0)
        tile = nt - 1 - step

        @pl.when(step == 0)
        def _():
            dwdh_ref[...] = jnp.zeros_like(dwdh_ref)
            dwdp_ref[...] = jnp.zeros_like(dwdp_ref)
            dwout_ref[...] = jnp.zeros_like(dwout_ref)
            dpw_ref[...] = jnp.zeros_like(dpw_ref)
            small_ref[...] = jnp.zeros_like(small_ref)
            carry_ref[...] = jnp.zeros_like(carry_ref)

        row = tile * tm + lax.broadcasted_iota(jnp.int32, (tm, 1), 0)
        real = row >= PAD_ROWS
        pos1 = jnp.maximum(row - PAD_ROWS + 1, 1).astype(F32)

        u = jnp.where(real, u_ref[0], 0.0)
        halo_row = tile * tm - HALO + lax.broadcasted_iota(jnp.int32, (HALO, 1), 0)
        uh = jnp.where(halo_row >= PAD_ROWS, uh_ref[0], 0.0)
        ext = jnp.concatenate([uh, u], axis=0)
        pooled, inv_cnt, mixed = [], [], []
        for g, w in enumerate(POOL_WINDOWS):
            cols = slice(g * POOL_GDIM, (g + 1) * POOL_GDIM)
            inv = 1.0 / jnp.minimum(pos1, float(w))
            ws = _window_sum(ext[:, cols], w, False)[HALO:]
            pg = (ws * inv - u[:, cols]).astype(BF16)
            pooled.append(pg)
            inv_cnt.append(inv)
            mixed.append(_dot(pg, pw_ref[g]))
        mixed = jnp.concatenate(mixed, axis=1)
        gpl = gpl_ref[0]
        sp, dsp = _silu_and_grad(gpl)
        ps = ps_ref[...]
        a_pool = (mixed * ps * sp).astype(BF16)
        y_pool = _dot(a_pool, wdp_ref[...])

        o = o_ref[...]
        o_hat, rstd_h = [], []
        for h in range(N_HEADS):
            oh = o[:, h * HEAD_DIM:(h + 1) * HEAD_DIM]
            r = lax.rsqrt(jnp.mean(oh * oh, axis=-1, keepdims=True) + EPS)
            rstd_h.append(r)
            o_hat.append(oh * r)
        o_hat = jnp.concatenate(o_hat, axis=1)
        hgw = hgw_ref[...]
        o_n = o_hat * hgw
        ghg = ghg_ref[0]
        sh, dsh = _silu_and_grad(ghg)
        a_hg = (o_n * sh).astype(BF16)
        y_hg = _dot(a_hg, wdh_ref[...])

        s_mh = _sigmoid(mhg_ref[0])
        s_mp = _sigmoid(mpl_ref[0])
        merged = (s_mh * y_hg + s_mp * y_pool).astype(BF16)
        z2 = z_ref[...] + _dot(merged, wout_ref[...])
        rstd2 = lax.rsqrt(jnp.mean(z2 * z2, axis=-1, keepdims=True) + EPS)
        zh = z2 * rstd2
        fw = fw_ref[...]
        err = jnp.where(row >= FIRST_TOKEN_ROW, zh * fw - t_ref[...], 0.0)
        small_ref[ROW_LOSS:ROW_LOSS + 1, :] += jnp.sum(err * err, axis=0, keepdims=True) * (0.5 / D_MODEL)
        dy = err * (1.0 / D_MODEL)

        small_ref[ROW_FINAL_W:ROW_FINAL_W + 1, :] += jnp.sum(dy * zh, axis=0, keepdims=True)
        uu = dy * fw
        dz2 = rstd2 * (uu - zh * jnp.mean(uu * zh, axis=-1, keepdims=True))
        dz2_ref[...] = dz2
        dz2_16 = dz2.astype(BF16)
        dmerged = _dot_nt(dz2_16, wout_ref[...])
        dwout_ref[...] += _dot_tn(merged, dz2_16)
        dy_hg = (s_mh * dmerged).astype(BF16)
        dy_pool = (s_mp * dmerged).astype(BF16)
        dp_ref[3] = (dmerged * y_hg * s_mh * (1.0 - s_mh)).astype(BF16)
        dp_ref[4] = (dmerged * y_pool * s_mp * (1.0 - s_mp)).astype(BF16)

        da_hg = _dot_nt(dy_hg, wdh_ref[...])
        dwdh_ref[...] += _dot_tn(a_hg, dy_hg)
        dp_ref[0] = (da_hg * o_n * dsh).astype(BF16)
        do_n = da_hg * sh
        small_ref[ROW_HG_W:ROW_HG_W + 1, :] += jnp.sum(do_n * o_hat, axis=0, keepdims=True)
        d_hat = do_n * hgw
        for h in range(N_HEADS):
            cols = slice(h * HEAD_DIM, (h + 1) * HEAD_DIM)
            dh_, oh_ = d_hat[:, cols], o_hat[:, cols]
            do_ref[:, cols] = rstd_h[h] * (dh_ - oh_ * jnp.mean(dh_ * oh_, axis=-1, keepdims=True))

        da_pool = _dot_nt(dy_pool, wdp_ref[...])
        dwdp_ref[...] += _dot_tn(a_pool, dy_pool)
        small_ref[ROW_POOL_SCALE:ROW_POOL_SCALE + 1, :] += jnp.sum(da_pool * mixed * sp, axis=0, keepdims=True)
        dp_ref[2] = (da_pool * mixed * ps * dsp).astype(BF16)
        dmixed = (da_pool * ps * sp).astype(BF16)
        carry = carry_ref[...]
        du, new_carry = [], []
        for g, w in enumerate(POOL_WINDOWS):
            cols = slice(g * POOL_GDIM, (g + 1) * POOL_GDIM)
            dmg = dmixed[:, cols]
            dpooled = _dot_nt(dmg, pw_ref[g])
            dpw_ref[g] += _dot_tn(pooled[g], dmg)
            dps = dpooled * inv_cnt[g]
            ext_b = jnp.concatenate([dps, carry[:, cols]], axis=0)
            du.append(_window_sum(ext_b, w, True)[:tm] - dpooled)
            new_carry.append(dps[:HALO])
        dp_ref[1] = jnp.where(real, jnp.concatenate(du, axis=1), 0.0).astype(BF16)
        carry_ref[...] = jnp.concatenate(new_carry, axis=1)

    row_block = pl.BlockSpec((tm, D_MODEL), lambda s: (nt - 1 - s, 0))
    seg_block = lambda seg: pl.BlockSpec((1, tm, D_MODEL), lambda s: (seg, nt - 1 - s, 0))
    whole = pl.BlockSpec(memory_space=pltpu.VMEM)
    return pl.pallas_call(
        body, name="mixers",
        grid=(nt,),
        in_specs=[
            row_block, seg_block(3), seg_block(4), seg_block(5), seg_block(6), seg_block(7),
            pl.BlockSpec((1, HALO, D_MODEL),
                         lambda s: (4, jnp.maximum((nt - 1 - s) * halo_blocks - 1, 0), 0)),
            row_block, row_block,
            whole, whole, whole, whole, whole, whole, whole,
        ],
        out_specs=[
            row_block, row_block,
            pl.BlockSpec((5, tm, D_MODEL), lambda s: (0, nt - 1 - s, 0)),
            whole, whole, whole, whole, whole,
        ],
        out_shape=[
            jax.ShapeDtypeStruct((rows, D_MODEL), F32),
            jax.ShapeDtypeStruct((rows, D_MODEL), F32),
            jax.ShapeDtypeStruct((5, rows, D_MODEL), BF16),
            jax.ShapeDtypeStruct((D_MODEL, D_MODEL), F32),
            jax.ShapeDtypeStruct((D_MODEL, D_MODEL), F32),
            jax.ShapeDtypeStruct((D_MODEL, D_MODEL), F32),
            jax.ShapeDtypeStruct((n_grp, POOL_GDIM, POOL_GDIM), F32),
            jax.ShapeDtypeStruct((SMALL_ROWS, D_MODEL), F32),
        ],
        scratch_shapes=[pltpu.VMEM((HALO, D_MODEL), F32)],
        compiler_params=_params(("arbitrary",)),
    )(o, p3, p3, p3, p3, p3, p3, z, tgt, wdh, wdp, wout, poolw, hg_w, pool_scale, final_w)


def _seg_specs(tm, row_of, seg_of):
    def spec_a(*g):
        k = seg_of(*g)
        return (jnp.minimum(k, 2), jnp.where(k < 3, row_of(*g), 0), 0)

    def spec_b(*g):
        k = seg_of(*g)
        return (jnp.maximum(k - 3, 0), jnp.where(k >= 3, row_of(*g), 0), 0)

    return pl.BlockSpec((1, tm, D_MODEL), spec_a), pl.BlockSpec((1, tm, D_MODEL), spec_b)


def _in_proj_weight_grad(h, dp, rows, name):
    n_seg = dp.shape[0]
    tm = _tile(rows, 1040)
    nt = rows // tm
    half = D_MODEL // 2

    def body(h_ref, dp_ref, part_ref, part16_ref, db_ref, acc_ref, bacc_ref, stage_ref, land_ref,
             send_sems, recv_sems):
        k, i = pl.program_id(0), pl.program_id(1)
        x, y, c = lax.axis_index("x"), lax.axis_index("y"), lax.axis_index("c")

        def to_sibling(seg):
            return pltpu.make_async_remote_copy(
                src_ref=stage_ref.at[seg], dst_ref=land_ref.at[seg], send_sem=send_sems.at[seg],
                recv_sem=recv_sems.at[seg], device_id=(x, y, 1 - c), device_id_type=MESH)

        @pl.when(i == 0)
        def _():
            acc_ref[...] = jnp.zeros_like(acc_ref)
            bacc_ref[...] = jnp.zeros_like(bacc_ref)

        dpt = dp_ref[0]
        acc_ref[...] += _dot_tn(h_ref[...], dpt)
        bacc_ref[...] += jnp.sum(dpt.astype(F32), axis=0, keepdims=True)

        @pl.when(i == nt - 1)
        def _():
            db_ref[0] = bacc_ref[...]
            part_ref[k] = acc_ref[pl.ds(pl.multiple_of(c * half, half), half), :]
            stage_ref[k] = acc_ref[pl.ds(pl.multiple_of((1 - c) * half, half), half), :].astype(BF16)
            to_sibling(k).start()

        @pl.when((k == n_seg - 1) & (i == nt - 1))
        def _():
            for seg in range(n_seg):
                to_sibling(seg).wait_recv()
                total = part_ref[seg] + land_ref[seg].astype(F32)
                part_ref[seg] = total
                part16_ref[seg] = total.astype(BF16)
            for seg in range(n_seg):
                to_sibling(seg).wait_send()

    whole = pl.BlockSpec(memory_space=pltpu.VMEM)
    return pl.pallas_call(
        body, name=name,
        grid=(n_seg, nt),
        in_specs=[pl.BlockSpec((tm, D_MODEL), lambda k, i: (i, 0)),
                  pl.BlockSpec((1, tm, D_MODEL), lambda k, i: (k, i, 0))],
        out_specs=[whole, whole, pl.BlockSpec((1, 1, D_MODEL), lambda k, i: (k, 0, 0))],
        out_shape=[
            jax.ShapeDtypeStruct((n_seg, half, D_MODEL), F32),
            jax.ShapeDtypeStruct((n_seg, half, D_MODEL), BF16),
            jax.ShapeDtypeStruct((n_seg, 1, D_MODEL), F32),
        ],
        scratch_shapes=[
            pltpu.VMEM((D_MODEL, D_MODEL), F32), pltpu.VMEM((1, D_MODEL), F32),
            pltpu.VMEM((n_seg, half, D_MODEL), BF16),
            pltpu.VMEM((n_seg, half, D_MODEL), BF16),
            pltpu.SemaphoreType.DMA((n_seg,)), pltpu.SemaphoreType.DMA((n_seg,)),
        ],
        compiler_params=_params(("arbitrary", "arbitrary")),
    )(h, dp)


def _input_grad(dpa, dpb, w4, z, dz2, norm_w, dw16, rows):
    tm = _tile(rows, 1040)
    nt = rows // tm
    exchange = _GradExchange(SEGS_REC, with_blob=False)

    def body(dpa_ref, dpb_ref, w_ref, z_ref, dz2_ref, nw_ref, dw_ref, dz_ref, dnw_ref, rxw_ref,
             acc_ref, send_sems, recv_sems):
        i, k = pl.program_id(0), pl.program_id(1)

        @pl.when((i == 0) & (k == 0))
        def _():
            exchange.start(dw_ref, rxw_ref, None, None, send_sems, recv_sems)
            dnw_ref[...] = jnp.zeros_like(dnw_ref)

        @pl.when((i == nt - 1) & (k == N_SEG - 1))
        def _():
            exchange.wait(dw_ref, rxw_ref, None, None, send_sems, recv_sems)

        @pl.when(k == 0)
        def _():
            acc_ref[...] = jnp.zeros_like(acc_ref)

        @pl.when(k < 3)
        def _():
            acc_ref[...] += _dot_nt(dpa_ref[0], w_ref[0])

        @pl.when(k >= 3)
        def _():
            acc_ref[...] += _dot_nt(dpb_ref[0], w_ref[0])

        @pl.when(k == N_SEG - 1)
        def _():
            zt = z_ref[...]
            rstd = lax.rsqrt(jnp.mean(zt * zt, axis=-1, keepdims=True) + EPS)
            zh = zt * rstd
            dh = acc_ref[...]
            dnw_ref[...] += jnp.sum(dh * zh, axis=0, keepdims=True)
            uu = dh * nw_ref[...]
            dz_ref[...] = dz2_ref[...] + rstd * (uu - zh * jnp.mean(uu * zh, axis=-1, keepdims=True))

    spec_a, spec_b = _seg_specs(tm, lambda i, k: i, lambda i, k: k)
    last_only = pl.BlockSpec((tm, D_MODEL), lambda i, k: (jnp.where(k == N_SEG - 1, i, 0), 0))
    return pl.pallas_call(
        body, name="input_grad",
        grid=(nt, N_SEG),
        in_specs=[
            spec_a, spec_b,
            pl.BlockSpec((1, D_MODEL, D_MODEL), lambda i, k: (k // 2, 0, k % 2)),
            last_only, last_only,
            pl.BlockSpec((1, D_MODEL), lambda i, k: (0, 0)),
            ANY,
        ],
        out_specs=[
            pl.BlockSpec((tm, D_MODEL), lambda i, k: (i, 0)),
            pl.BlockSpec((1, D_MODEL), lambda i, k: (0, 0)),
            ANY,
        ],
        out_shape=[
            jax.ShapeDtypeStruct((rows, D_MODEL), F32),
            jax.ShapeDtypeStruct((1, D_MODEL), F32),
            exchange.landing_w(),
        ],
        scratch_shapes=[pltpu.VMEM((tm, D_MODEL), F32)] + exchange.semaphores(),
        compiler_params=_params(("arbitrary", "arbitrary")),
    )(dpa, dpb, w4, z, dz2, norm_w, dw16)


def _local_step(z, tgt, w4, blob4, seg_order, norm_w, b_in, lb_logits, hg_w, pool_scale, final_w):
    rows = z.shape[0]
    q = D_MODEL // N_CHIPS
    n_grp = len(POOL_WINDOWS)
    pg = POOL_GDIM // N_CHIPS

    wexp2 = jnp.asarray(np.tile(_exponent_matrix(), (1, 2)), BF16)
    wexp_t = jnp.asarray(_exponent_matrix().T, BF16)
    masks2 = jnp.asarray(_paired_masks(), F32)

    h, p3, w4 = _in_proj(z, norm_w, w4, b_in, seg_order, rows)
    o, states, e16, a2, blob4 = _hgrn_forward(p3, lb_logits, wexp2, masks2, blob4, rows)
    wdh = blob4[:, 0:q].reshape(D_MODEL, D_MODEL)
    wdp = blob4[:, q:2 * q].reshape(D_MODEL, D_MODEL)
    wout = blob4[:, 2 * q:3 * q].reshape(D_MODEL, D_MODEL)
    poolw = blob4[:, 3 * q:].reshape(N_CHIPS, n_grp, pg, POOL_GDIM).transpose(1, 0, 2, 3)
    poolw = poolw.reshape(n_grp, POOL_GDIM, POOL_GDIM)
    d_o, dz2, dpb, dwdh, dwdp, dwout, dpw, small = _mixers(
        o, p3, z, tgt, wdh, wdp, wout, poolw, hg_w, pool_scale, final_w, rows)
    dpw4 = dpw.reshape(n_grp, N_CHIPS, pg, POOL_GDIM).transpose(1, 0, 2, 3)
    dpw4 = dpw4.reshape(N_CHIPS, n_grp * pg * POOL_GDIM // D_MODEL, D_MODEL)
    dblob4 = jnp.concatenate([dwdh.reshape(N_CHIPS, q, D_MODEL), dwdp.reshape(N_CHIPS, q, D_MODEL),
                              dwout.reshape(N_CHIPS, q, D_MODEL), dpw4], axis=1)

    dw_mix, dw_mix16, db_mix = _in_proj_weight_grad(h, dpb, rows, "in_proj_weight_grad_mix")
    dpa, dlb, rxw_mix, rx_blob = _hgrn_backward(
        p3, d_o, states, e16, a2, lb_logits, wexp_t, masks2, dw_mix16, dblob4.astype(BF16), rows)
    dw_rec, dw_rec16, db_rec = _in_proj_weight_grad(h, dpa, rows, "in_proj_weight_grad_rec")
    dz, dnw, rxw_rec = _input_grad(dpa, dpb, w4, z, dz2, norm_w, dw_rec16, rows)

    small = jnp.concatenate([
        small[ROW_LOSS:ROW_LOSS + 1],
        dz[PAD_ROWS:PAD_ROWS + N_META],
        dnw,
        db_rec.reshape(len(SEGS_REC), D_MODEL), db_mix.reshape(len(SEGS_MIX), D_MODEL),
        dlb, jnp.zeros_like(dlb),
        small[ROW_HG_W:ROW_HG_W + 1], small[ROW_POOL_SCALE:ROW_POOL_SCALE + 1],
        small[ROW_FINAL_W:ROW_FINAL_W + 1],
        jnp.zeros((SMALL_ROWS - ROW_FINAL_W - 1, D_MODEL), F32),
    ], axis=0)
    return dz, (dw_rec, dw_mix, rxw_rec, rxw_mix), (dblob4, rx_blob), small


ANY = pl.BlockSpec(memory_space=pl.ANY)
MESH = pl.DeviceIdType.MESH


def _place():
    x, y, c = lax.axis_index("x"), lax.axis_index("y"), lax.axis_index("c")
    chips = [(1 - x, y), (x, 1 - y), (1 - x, 1 - y)]
    return x, y, c, chips


class _ShardGather:
    def __init__(self, rows):
        self.half = rows // 2

    def semaphores(self):
        return [pltpu.SemaphoreType.DMA((6,)), pltpu.SemaphoreType.DMA((6,))]

    def _copy(self, k, slot, to, send_sems, recv_sems):
        return pltpu.make_async_remote_copy(src_ref=slot, dst_ref=slot, send_sem=send_sems.at[k],
                                            recv_sem=recv_sems.at[k], device_id=to, device_id_type=MESH)

    def _half(self, ref4, chip, which):
        return ref4.at[chip, pl.ds(which * self.half, self.half), :]

    def start(self, ref4, send_sems, recv_sems):
        x, y, c, chips = _place()
        for j, (cx, cy) in enumerate(chips):
            self._copy(j, self._half(ref4, 2 * x + y, c), (cx, cy, c), send_sems, recv_sems).start()

    def arrive(self, j, ref4, send_sems, recv_sems):
        x, y, c, chips = _place()
        cx, cy = chips[j]
        landed = self._half(ref4, 2 * cx + cy, c)
        self._copy(j, landed, (cx, cy, c), send_sems, recv_sems).wait_recv()
        self._copy(3 + j, landed, (x, y, 1 - c), send_sems, recv_sems).start()
        self._copy(3 + j, self._half(ref4, 2 * cx + cy, 1 - c), (x, y, 1 - c), send_sems, recv_sems).wait_recv()

    def pass_on_all(self, ref4, send_sems, recv_sems):
        x, y, c, chips = _place()
        for j, (cx, cy) in enumerate(chips):
            landed = self._half(ref4, 2 * cx + cy, c)
            self._copy(j, landed, (cx, cy, c), send_sems, recv_sems).wait_recv()
            self._copy(3 + j, landed, (x, y, 1 - c), send_sems, recv_sems).start()

    def await_sibling_all(self, ref4, send_sems, recv_sems):
        x, y, c, chips = _place()
        for j, (cx, cy) in enumerate(chips):
            self._copy(3 + j, self._half(ref4, 2 * cx + cy, 1 - c), (x, y, 1 - c), send_sems, recv_sems).wait_recv()

    def finish(self, ref4, send_sems, recv_sems):
        x, y, c, chips = _place()
        for j, (cx, cy) in enumerate(chips):
            self._copy(j, self._half(ref4, 2 * x + y, c), (cx, cy, c), send_sems, recv_sems).wait_send()
            self._copy(3 + j, self._half(ref4, 2 * cx + cy, c), (x, y, 1 - c), send_sems, recv_sems).wait_send()


def _gather_meta(m4):
    def body(m_in_ref, m4_ref, send_sems, recv_sems):
        x, y, c, chips = _place()

        def copy(j, slot, to):
            return pltpu.make_async_remote_copy(src_ref=slot, dst_ref=slot, send_sem=send_sems.at[j],
                                                recv_sem=recv_sems.at[j], device_id=to, device_id_type=MESH)

        sends = [copy(j, m4_ref.at[2 * x + y], (cx, cy, c)) for j, (cx, cy) in enumerate(chips)]
        for cp in sends:
            cp.start()
        for j, (cx, cy) in enumerate(chips):
            copy(j, m4_ref.at[2 * cx + cy], (x, y, c)).wait_recv()
        for cp in sends:
            cp.wait_send()

    return pl.pallas_call(
        body, name="gather_meta",
        in_specs=[ANY], out_specs=ANY, out_shape=jax.ShapeDtypeStruct(m4.shape, m4.dtype),
        input_output_aliases={0: 0},
        scratch_shapes=[pltpu.SemaphoreType.DMA((3,)), pltpu.SemaphoreType.DMA((3,))],
    )(m4)


class _GradExchange:
    def __init__(self, segs, with_blob):
        self.segs = tuple(segs)
        self.with_blob = with_blob

    def landing_w(self):
        return jax.ShapeDtypeStruct((N_CHIPS, 2, D_MODEL // 2, D_MODEL), BF16)

    def landing_blob(self, blob16):
        return jax.ShapeDtypeStruct((N_DEV, blob16.shape[1] // 2, D_MODEL), BF16)

    def semaphores(self):
        n_send = len(self.segs) + (2 * N_CHIPS if self.with_blob else 0)
        n_recv = 2 * N_CHIPS + (N_DEV if self.with_blob else 0)
        return [pltpu.SemaphoreType.DMA((n_send,)), pltpu.SemaphoreType.DMA((n_recv,))]

    def _copies(self, dw_ref, rxw_ref, blob_ref, rxb_ref, send_sems, recv_sems):
        x, y, c = lax.axis_index("x"), lax.axis_index("y"), lax.axis_index("c")
        chip = 2 * x + y

        def relation(kx, ky, h):
            return (x ^ kx) * 4 + (y ^ ky) * 2 + (c ^ h)

        def copy(src, dst, send_k, recv_k, to):
            return functools.partial(pltpu.make_async_remote_copy, src_ref=src, dst_ref=dst,
                                     send_sem=send_sems.at[send_k], recv_sem=recv_sems.at[recv_k],
                                     device_id=to, device_id_type=MESH)

        sends, recvs = [], []
        for i, s in enumerate(self.segs):
            kx, ky = (s // 2) >> 1, (s // 2) & 1
            r = (x ^ kx) * 2 + (y ^ ky)
            sends.append((r != 0, copy(dw_ref.at[i], rxw_ref.at[r, s % 2], i, 2 * r + s % 2, (kx, ky, c))))
        for j in range(2):
            mine = [s // 2 for s in self.segs if s % 2 == j]
            if mine:
                cond = functools.reduce(lambda a, b: a | b, [chip == k for k in mine])
                for r in range(1, N_CHIPS):
                    slot = rxw_ref.at[r, j]
                    recvs.append((cond, copy(slot, slot, 0, 2 * r + j, (x, y, c))))
        if self.with_blob:
            hb = blob_ref.shape[1] // 2
            first_send, first_recv = len(self.segs), 2 * N_CHIPS
            for k in range(N_CHIPS):
                for h in range(2):
                    r = relation(k >> 1, k & 1, h)
                    sends.append((r != 0, copy(blob_ref.at[k, pl.ds(h * hb, hb), :], rxb_ref.at[r],
                                               first_send + 2 * k + h, first_recv + r, (k >> 1, k & 1, h))))
            for r in range(1, N_DEV):
                slot = rxb_ref.at[r]
                recvs.append((None, copy(slot, slot, 0, first_recv + r, (x, y, c))))
        return sends, recvs

    def start(self, *refs):
        sends, _ = self._copies(*refs)
        for cond, make in sends:
            pl.when(cond)(lambda make=make: make().start())

    def wait(self, *refs):
        sends, recvs = self._copies(*refs)
        for cond, make in sends:
            pl.when(cond)(lambda make=make: make().wait_send())
        for cond, make in recvs:
            if cond is None:
                make().wait_recv()
            else:
                pl.when(cond)(lambda make=make: make().wait_recv())


def _sum_landed(own, rx_ref):
    total = own
    for r in range(1, rx_ref.shape[0]):
        total = total + rx_ref[r, 0].astype(F32)
    return total


def _finish_w(dw_rec, dw_mix, rx_rec, rx_mix, place_arr):
    half = D_MODEL // 2
    tm = _tile(half, 256)
    n_rec = len(SEGS_REC)

    def body(place_ref, own_rec_ref, own_mix_ref, rx_rec_ref, rx_mix_ref, out_ref):
        seg = 2 * place_ref[0] + pl.program_id(0)

        @pl.when(seg < n_rec)
        def _():
            out_ref[0] = _sum_landed(own_rec_ref[0], rx_rec_ref)

        @pl.when(seg >= n_rec)
        def _():
            out_ref[0] = _sum_landed(own_mix_ref[0], rx_mix_ref)

    def own_spec(first, count):
        def index(j, i, place_ref):
            seg = 2 * place_ref[0] + j
            return (jnp.clip(seg - first, 0, count - 1), i, 0)
        return pl.BlockSpec((1, tm, D_MODEL), index)

    rx_spec = pl.BlockSpec((N_CHIPS, 1, tm, D_MODEL), lambda j, i, place_ref: (0, j, i, 0))
    return pl.pallas_call(
        body, name="finish_w",
        grid_spec=pltpu.PrefetchScalarGridSpec(
            num_scalar_prefetch=1, grid=(2, half // tm),
            in_specs=[own_spec(0, n_rec), own_spec(n_rec, len(SEGS_MIX)), rx_spec, rx_spec],
            out_specs=pl.BlockSpec((1, tm, D_MODEL), lambda j, i, place_ref: (place_ref[1], i, j))),
        out_shape=jax.ShapeDtypeStruct((2, half, 2 * D_MODEL), F32),
        compiler_params=_params(("arbitrary", "arbitrary")),
    )(place_arr, dw_rec, dw_mix, rx_rec, rx_mix)


def _finish_blob(dblob4, rx_blob, place_arr):
    n, rows, cols = rx_blob.shape
    tm = _tile(rows, 256)

    def body(place_ref, own_ref, rx_ref, out_ref):
        out_ref[0] = _sum_landed(own_ref[0, 0], rx_ref)

    return pl.pallas_call(
        body, name="finish_blob",
        grid_spec=pltpu.PrefetchScalarGridSpec(
            num_scalar_prefetch=1, grid=(rows // tm,),
            in_specs=[pl.BlockSpec((1, 1, tm, cols), lambda i, place_ref: (place_ref[0], place_ref[1], i, 0)),
                      pl.BlockSpec((n, 1, tm, cols), lambda i, place_ref: (0, 0, i, 0))],
            out_specs=pl.BlockSpec((1, tm, cols), lambda i, place_ref: (place_ref[1], i, 0))),
        out_shape=jax.ShapeDtypeStruct((2, rows, cols), F32),
        compiler_params=_params(("arbitrary",)),
    )(place_arr, dblob4.reshape(N_CHIPS, 2, rows, cols), rx_blob.reshape(n, 1, rows, cols))


def _share_finished(fw2, fb2, slots):
    def body(w_in_ref, b_in_ref, s_in_ref, w_ref, b_ref, s_ref, send_sems, recv_sems):
        x, y, c, _ = _place()
        sibling = (x, y, 1 - c)

        def copy(k, src, dst, to):
            return pltpu.make_async_remote_copy(src_ref=src, dst_ref=dst, send_sem=send_sems.at[k],
                                                recv_sem=recv_sems.at[k], device_id=to, device_id_type=MESH)

        sends = [copy(0, w_ref.at[c], w_ref.at[c], sibling), copy(1, b_ref.at[c], b_ref.at[c], sibling)]
        for r in range(1, N_DEV):
            peer = (x ^ ((r >> 2) & 1), y ^ ((r >> 1) & 1), c ^ (r & 1))
            sends.append(copy(1 + r, s_ref.at[0], s_ref.at[r], peer))
        for cp in sends:
            cp.start()
        landed = [w_ref.at[1 - c], b_ref.at[1 - c]] + [s_ref.at[r] for r in range(1, N_DEV)]
        for k, slot in enumerate(landed):
            copy(k, slot, slot, (x, y, c)).wait_recv()
        for cp in sends:
            cp.wait_send()

    same = lambda a: jax.ShapeDtypeStruct(a.shape, a.dtype)
    n_sem = 2 + N_DEV - 1
    return pl.pallas_call(
        body, name="share_finished",
        in_specs=[ANY, ANY, ANY], out_specs=[ANY, ANY, ANY],
        out_shape=[same(fw2), same(fb2), same(slots)],
        input_output_aliases={0: 0, 1: 1, 2: 2},
        scratch_shapes=[pltpu.SemaphoreType.DMA((n_sem,)), pltpu.SemaphoreType.DMA((n_sem,))],
    )(fw2, fb2, slots)


def _sum_small(slots, lb_logits, me_arr):
    def body(me_ref, slots_ref, lbl_ref, out_ref):
        me = me_ref[0]
        total = slots_ref[me]
        for d in range(1, N_DEV):
            total = total + slots_ref[d ^ me]
        out_ref[...] = total
        out_ref[ROW_LOSS:ROW_LOSS + 1, :] = jnp.broadcast_to(
            jnp.sum(total[ROW_LOSS:ROW_LOSS + 1, :], axis=-1, keepdims=True), (1, D_MODEL))
        lb = _lower_bound(lbl_ref[...])
        g0 = total[ROW_LB:ROW_LB + 1, :] * lb * (1.0 - lb)
        out_ref[ROW_LB:ROW_LB + 1, :] = g0
        out_ref[ROW_LB + 1:ROW_LB + 2, :] = -g0

    return pl.pallas_call(
        body, name="sum_small",
        grid_spec=pltpu.PrefetchScalarGridSpec(
            num_scalar_prefetch=1, grid=(1,),
            in_specs=[pl.BlockSpec((N_DEV, SMALL_ROWS, D_MODEL), lambda i, me_ref: (0, 0, 0)),
                      pl.BlockSpec((2, D_MODEL), lambda i, me_ref: (0, 0))],
            out_specs=pl.BlockSpec((SMALL_ROWS, D_MODEL), lambda i, me_ref: (0, 0))),
        out_shape=jax.ShapeDtypeStruct((SMALL_ROWS, D_MODEL), F32),
        compiler_params=_params(("arbitrary",)),
    )(me_arr, slots, lb_logits)


def _adamw(w, g, m, v):
    rows, cols = w.shape
    tm = _tile(rows, 256, mult=8) if rows % 8 == 0 else rows
    c1 = 1.0 / (1.0 - ADAM_B1 ** ADAM_STEP)
    c2 = 1.0 / (1.0 - ADAM_B2 ** ADAM_STEP)

    def body(w_ref, g_ref, m_ref, v_ref, d_ref, nm_ref, nv_ref):
        gt = g_ref[...]
        nm = ADAM_B1 * m_ref[...] + (1.0 - ADAM_B1) * gt
        nv = ADAM_B2 * v_ref[...] + (1.0 - ADAM_B2) * (gt * gt)
        nm_ref[...] = nm
        nv_ref[...] = nv
        d_ref[...] = -ADAM_LR * ((nm * c1) / (jnp.sqrt(nv * c2) + ADAM_EPS) + ADAM_WD * w_ref[...])

    blk = pl.BlockSpec((tm, cols), lambda i: (i, 0))
    sds = jax.ShapeDtypeStruct((rows, cols), F32)
    return pl.pallas_call(
        body, name="adamw",
        grid=(rows // tm,), in_specs=[blk] * 4, out_specs=[blk] * 3, out_shape=[sds] * 3,
        compiler_params=_params(("arbitrary",)),
    )(w, g, m, v)


def kernel(x, meta_tokens, norm_w, w_in, b_in, lb_logits, hg_norm_w, pool_w, pool_scale, w_down_hg, w_down_pool, w_out, final_norm_w, loss_target, m_meta_tokens, m_norm_w, m_w_in, m_b_in, m_lb_logits, m_hg_norm_w, m_pool_w, m_pool_scale, m_w_down_hg, m_w_down_pool, m_w_out, m_final_norm_w, v_meta_tokens, v_norm_w, v_w_in, v_b_in, v_lb_logits, v_hg_norm_w, v_pool_w, v_pool_scale, v_w_down_hg, v_w_down_pool, v_w_out, v_final_norm_w):
    seq = x.shape[1]
    xi, yi, ci = lax.axis_index("x"), lax.axis_index("y"), lax.axis_index("c")
    chip = 2 * xi + yi
    place_arr = jnp.stack([chip, ci]).astype(jnp.int32)
    me_arr = jnp.reshape(4 * xi + 2 * yi + ci, (1,)).astype(jnp.int32)
    q = D_MODEL // N_CHIPS

    def blob_of(wdh, wdp, wo, pw):
        return jnp.concatenate([wdh[0], wdp[0], wo[0], pw[0].reshape(-1, D_MODEL)], axis=0)

    def in_every_slot(a):
        return jnp.broadcast_to(a[None], (N_CHIPS,) + a.shape)

    meta4 = _gather_meta(in_every_slot(meta_tokens))
    meta_full = meta4.transpose(1, 0, 2).reshape(N_META, D_MODEL)
    w4 = in_every_slot(w_in[0].astype(BF16))
    blob4 = in_every_slot(blob_of(w_down_hg, w_down_pool, w_out, pool_w).astype(BF16))
    seg_order = jnp.stack([2 * (chip ^ rel) + t for rel in (0, 2, 1, 3) for t in (0, 1)]).astype(jnp.int32)

    z = jnp.concatenate([jnp.zeros((PAD_ROWS, D_MODEL), F32), meta_full, x[0]], axis=0)
    tgt = jnp.concatenate([jnp.zeros((FIRST_TOKEN_ROW, D_MODEL), F32), loss_target[0]], axis=0)
    fw2 = final_norm_w.reshape(1, D_MODEL)
    dz, w_parts, blob_parts, small = _local_step(
        z, tgt, w4, blob4, seg_order, norm_w, b_in, lb_logits, hg_norm_w, pool_scale, fw2)
    grad_x = dz[FIRST_TOKEN_ROW:][None]

    fin_w = _finish_w(*w_parts, place_arr)
    fin_b = _finish_blob(*blob_parts, place_arr)
    gw2, gb2, slots = _share_finished(fin_w, fin_b, jnp.broadcast_to(small[None], (N_DEV,) + small.shape))
    tot = _sum_small(slots, lb_logits, me_arr)
    g_w_in = gw2.reshape(D_MODEL, 2 * D_MODEL)
    g_blob = gb2.reshape(-1, D_MODEL)

    d_win, nm_win, nv_win = _adamw(w_in[0], g_w_in, m_w_in[0], v_w_in[0])
    d_blob, nm_blob, nv_blob = _adamw(
        blob_of(w_down_hg, w_down_pool, w_out, pool_w), g_blob,
        blob_of(m_w_down_hg, m_w_down_pool, m_w_out, m_pool_w),
        blob_of(v_w_down_hg, v_w_down_pool, v_w_out, v_pool_w))
    g_meta = lax.dynamic_slice_in_dim(tot[ROW_META:ROW_META + N_META], chip * q, q, axis=1)
    d_meta, nm_meta, nv_meta = _adamw(meta_tokens, g_meta, m_meta_tokens, v_meta_tokens)

    def rows_of(nw, bi, lbl, hg, ps, fw):
        return jnp.concatenate([nw, bi.reshape(N_SEG, D_MODEL), lbl, hg, ps, fw.reshape(1, D_MODEL),
                                jnp.zeros((2, D_MODEL), F32)], axis=0)

    g_rows = jnp.concatenate([tot[ROW_NORM_W:ROW_FINAL_W + 1], jnp.zeros((2, D_MODEL), F32)], axis=0)
    d_rows, nm_rows, nv_rows = _adamw(
        rows_of(norm_w, b_in, lb_logits, hg_norm_w, pool_scale, final_norm_w), g_rows,
        rows_of(m_norm_w, m_b_in, m_lb_logits, m_hg_norm_w, m_pool_scale, m_final_norm_w),
        rows_of(v_norm_w, v_b_in, v_lb_logits, v_hg_norm_w, v_pool_scale, v_final_norm_w))

    def unblob(b):
        return (b[0:q][None], b[q:2 * q][None], b[2 * q:3 * q][None], b[3 * q:].reshape(pool_w.shape))

    def unrows(r):
        o = ROW_NORM_W
        return dict(norm_w=r[ROW_NORM_W - o:ROW_B_IN - o], b_in=r[ROW_B_IN - o:ROW_LB - o].reshape(1, -1),
                    lb_logits=r[ROW_LB - o:ROW_HG_W - o], hg_norm_w=r[ROW_HG_W - o:ROW_POOL_SCALE - o],
                    pool_scale=r[ROW_POOL_SCALE - o:ROW_FINAL_W - o], final_norm_w=r[ROW_FINAL_W - o])

    def leaves(meta_part, rows_part, win_part, blob_part):
        r = unrows(rows_part)
        wdh, wdp, wo, pw = unblob(blob_part)
        return [meta_part, r["norm_w"], win_part[None], r["b_in"], r["lb_logits"], r["hg_norm_w"], pw,
                r["pool_scale"], wdh, wdp, wo, r["final_norm_w"]]

    loss = tot[ROW_LOSS, 0]
    return (loss, grad_x,
            *leaves(g_meta, g_rows, g_w_in, g_blob),
            *leaves(d_meta, d_rows, d_win, d_blob),
            *leaves(nm_meta, nm_rows, nm_win, nm_blob),
            *leaves(nv_meta, nv_rows, nv_win, nv_blob))
```

```python
import functools

import numpy as np
import jax
import jax.numpy as jnp
from jax import lax
from jax.experimental import pallas as pl
from jax.experimental.pallas import tpu as pltpu

F32 = jnp.float32
BF16 = jnp.bfloat16

D_MODEL = 1024
N_SEG = 8
N_HEADS = 8
HEAD_DIM = 128
CHUNK = 64
N_META = 16
PAD_ROWS = CHUNK - N_META
FIRST_TOKEN_ROW = CHUNK
LEVELS = (32, 16, 8, 4, 2, 1)
N_EXP = 2 + len(LEVELS)
POOL_WINDOWS = (2, 4, 8, 16)
POOL_GDIM = D_MODEL // len(POOL_WINDOWS)
HALO = 16
BACKWARD_UNROLL = 13
LOCAL_UNROLL = 13
HEADS_PER_STEP = 4
EPS = 1e-6
N_CHIPS = 4
N_DEV = 8
SEGS_REC = (0, 1, 2)
SEGS_MIX = (3, 4, 5, 6, 7)

ADAM_LR = 0.001
ADAM_B1 = 0.9
ADAM_B2 = 0.999
ADAM_EPS = 1e-08
ADAM_WD = 0.01
ADAM_STEP = 10

VMEM_LIMIT_BYTES = 56 * 1024 * 1024

ROW_LOSS = 0
ROW_META = 1
ROW_NORM_W = ROW_META + N_META
ROW_B_IN = ROW_NORM_W + 1
ROW_LB = ROW_B_IN + N_SEG
ROW_HG_W = ROW_LB + 2
ROW_POOL_SCALE = ROW_HG_W + 1
ROW_FINAL_W = ROW_POOL_SCALE + 1
SMALL_ROWS = 32


def _tile(total, cap, mult=16):
    best = None
    for t in range(mult, min(total, cap) + 1, mult):
        if total % t == 0:
            best = t
    assert best is not None, (total, cap, mult)
    return best


def _params(sem=None):
    return pltpu.CompilerParams(dimension_semantics=sem, vmem_limit_bytes=VMEM_LIMIT_BYTES)


def _dot(a, b):
    return jnp.dot(a, b, preferred_element_type=F32)


def _dot_nt(a, b):
    return lax.dot_general(a, b, (((1,), (1,)), ((), ())), preferred_element_type=F32)


def _dot_tn(a, b):
    return lax.dot_general(a, b, (((0,), (0,)), ((), ())), preferred_element_type=F32)


def _sigmoid_pair(x):
    t = jnp.exp(-jnp.abs(x))
    r = 1.0 / (1.0 + t)
    pos = x >= 0
    return jnp.where(pos, r, t * r), jnp.where(pos, t * r, r)


def _exponent_matrix():
    t = np.arange(CHUNK)[:, None]
    j = np.arange(CHUNK)[None, :]
    blocks = [j <= t, j > t]
    for m in LEVELS:
        rho = (t // (2 * m)) * (2 * m) + m
        upper = (t >= rho) & (j > rho) & (j <= t)
        lower = (t < rho) & (j > t) & (j <= rho)
        blocks.append(upper | lower)
    return np.concatenate(blocks, axis=0).astype(np.float32)


def _pair_masks():
    t = np.arange(CHUNK)[:, None]
    s = np.arange(CHUNK)[None, :]
    masks = [t == s]
    for m in LEVELS:
        same = (t // (2 * m)) == (s // (2 * m))
        masks.append(same & ((t % (2 * m)) >= m) & ((s % (2 * m)) < m))
    return np.stack(masks).astype(np.float32)


LEVEL_PAIRS = ((0, 1), (2, 3), (4, 5), (6, None))


def _paired_masks():
    m = _pair_masks()
    zero = np.zeros_like(m[0])
    return np.stack([np.concatenate([m[a], zero if b is None else m[b]], axis=1) for a, b in LEVEL_PAIRS])


def _split3(x):
    hi = x.astype(BF16)
    r = x - hi.astype(F32)
    mid = r.astype(BF16)
    lo = (r - mid.astype(F32)).astype(BF16)
    return hi, mid, lo


def _chunk_forward(q, fz, lb, valid, wexp, masks):
    sg, sn = _sigmoid_pair(fz)
    f = lb + (1.0 - lb) * sg
    g = jnp.where(valid, jnp.log(f), 0.0)
    kk = jnp.where(valid, (1.0 - lb) * sn, 0.0)
    q = jnp.where(valid, q, 0.0)
    e = jnp.exp(_dot(wexp, jnp.concatenate(_split3(g), axis=0)))
    e_b = e[0:CHUNK]
    e_c = e[CHUNK:2 * CHUNK]
    a = masks[0] * _dot_nt(q.astype(BF16), kk.astype(BF16))
    qm, km = [], []
    for l in range(len(LEVELS)):
        e_m = e[(2 + l) * CHUNK:(3 + l) * CHUNK]
        qm.append(q * e_m)
        km.append(kk * e_m)
        a = a + masks[1 + l] * _dot_nt(qm[l].astype(BF16), km[l].astype(BF16))
    return dict(sg=sg, sn=sn, f=f, kk=kk, q=q, e=e, e_b=e_b, e_c=e_c, a=a, qm=qm, km=km)


def _lower_bound(lbl):
    return 1.0 / (1.0 + jnp.exp(lbl[1:2, :] - lbl[0:1, :]))


def _in_proj(z, norm_w, w4, b_in, seg_order, rows):
    tm = _tile(rows, 1040)
    nt = rows // tm
    gather = _ShardGather(w4.shape[1])

    def body(order_ref, z_ref, nw_ref, b_ref, w_in_ref, h_ref, p_ref, w4_ref,
             h_all, w_buf, w_sem, send_sems, recv_sems):
        kk, i = pl.program_id(0), pl.program_id(1)

        @pl.when((kk == 0) & (i == 0))
        def _():
            gather.start(w4_ref, send_sems, recv_sems, which=(0, 1))

        @pl.when((kk == 2) & (i == 0))
        def _():
            gather.start_diagonal_after_neighbours(w4_ref, send_sems, recv_sems)

        @pl.when(kk == 0)
        def _():
            zt = z_ref[...]
            rstd = lax.rsqrt(jnp.mean(zt * zt, axis=-1, keepdims=True) + EPS)
            h = (zt * rstd * nw_ref[...]).astype(BF16)
            h_all[pl.ds(pl.multiple_of(i * tm, 16), tm), :] = h
            h_ref[...] = h

        for j in range(N_CHIPS - 1):
            @pl.when((kk == 2 + 2 * j) & (i == 0))
            def _(j=j):
                gather.arrive(j, w4_ref, send_sems, recv_sems)

        @pl.when(i == 0)
        def _():
            seg = order_ref[kk]
            cp = pltpu.make_async_copy(
                w4_ref.at[seg // 2, :, pl.ds(pl.multiple_of((seg % 2) * D_MODEL, D_MODEL), D_MODEL)], w_buf, w_sem)
            cp.start()
            cp.wait()

        p_ref[0] = _dot(h_all[pl.ds(pl.multiple_of(i * tm, 16), tm), :], w_buf[...]) + b_ref[...]

        @pl.when((kk == N_SEG - 1) & (i == nt - 1))
        def _():
            gather.finish(w4_ref, send_sems, recv_sems, which=(2,))

    first_pass = lambda kk, i, order_ref: (jnp.where(kk == 0, i, nt - 1), 0)
    return pl.pallas_call(
        body, name="in_proj",
        grid_spec=pltpu.PrefetchScalarGridSpec(
            num_scalar_prefetch=1, grid=(N_SEG, nt),
            in_specs=[
                pl.BlockSpec((tm, D_MODEL), first_pass),
                pl.BlockSpec((1, D_MODEL), lambda kk, i, order_ref: (0, 0)),
                pl.BlockSpec((1, D_MODEL), lambda kk, i, order_ref: (0, order_ref[kk])),
                ANY,
            ],
            out_specs=[
                pl.BlockSpec((tm, D_MODEL), first_pass),
                pl.BlockSpec((1, tm, D_MODEL), lambda kk, i, order_ref: (order_ref[kk], i, 0)),
                ANY,
            ],
            scratch_shapes=[
                pltpu.VMEM((rows, D_MODEL), BF16),
                pltpu.VMEM((D_MODEL, D_MODEL), BF16),
                pltpu.SemaphoreType.DMA,
            ] + gather.semaphores()),
        out_shape=[
            jax.ShapeDtypeStruct((rows, D_MODEL), BF16),
            jax.ShapeDtypeStruct((N_SEG, rows, D_MODEL), F32),
            jax.ShapeDtypeStruct(w4.shape, w4.dtype),
        ],
        input_output_aliases={4: 2},
        compiler_params=_params(("arbitrary", "arbitrary")),
    )(seg_order, z, norm_w, b_in, w4)


def _hgrn_forward(p3, lb_logits, wexp2, masks2, blob4, rows):
    n_chunks = rows // CHUNK
    cpb = _tile(n_chunks, 13, mult=1)
    rb_rows = cpb * CHUNK
    n_rb = n_chunks // cpb
    lanes = cpb * HEAD_DIM
    gather = _ShardGather(blob4.shape[1])

    def body(q_ref, fz_ref, v_ref, lbl_ref, wexp_ref, mask_ref, b_in_ref, o_ref, s_ref, e16_ref, a2_ref, b4_ref,
             st_ref, e_ref, u_ref, q_s, kk_s, v_s, qb_s, oi_s, send_sems, recv_sems):
        rb = pl.program_id(1)

        @pl.when((pl.program_id(0) == 0) & (rb == 0))
        def _():
            gather.start(b4_ref, send_sems, recv_sems)

        @pl.when(rb == 0)
        def _():
            st_ref[...] = jnp.zeros_like(st_ref)

        lb = _lower_bound(lbl_ref[...])
        row = rb * rb_rows + lax.broadcasted_iota(jnp.int32, (rb_rows, 1), 0)
        valid = row >= PAD_ROWS
        sg, sn = _sigmoid_pair(fz_ref[0])
        g = jnp.where(valid, jnp.log(lb + (1.0 - lb) * sg), 0.0)
        kk_s[...] = jnp.where(valid, (1.0 - lb) * sn, 0.0)
        q_s[...] = jnp.where(valid, q_ref[0], 0.0)
        v_s[...] = jnp.where(valid, v_ref[0], 0.0).astype(BF16)
        hi = g.astype(BF16)
        mid = (g - hi.astype(F32)).astype(BF16)
        g2 = jnp.concatenate(
            [jnp.concatenate([hi[b * CHUNK:(b + 1) * CHUNK], mid[b * CHUNK:(b + 1) * CHUNK]], axis=0)
             for b in range(cpb)], axis=1)
        e_ref[...] = jnp.exp(_dot(wexp_ref[...], g2))
        e16_ref[0, 0] = e_ref[...].astype(BF16)

        zeros16 = jnp.zeros((CHUNK, HEAD_DIM), BF16)

        def local(b, carry):
            r0 = pl.multiple_of(b * CHUNK, CHUNK)
            l0 = pl.multiple_of(b * HEAD_DIM, HEAD_DIM)
            q = q_s[pl.ds(r0, CHUNK), :]
            kk = kk_s[pl.ds(r0, CHUNK), :]
            v16 = v_s[pl.ds(r0, CHUNK), :]

            def scaled(entry):
                if entry == 0:
                    return q.astype(BF16), kk.astype(BF16)
                e_m = e_ref[(1 + entry) * CHUNK:(2 + entry) * CHUNK, pl.ds(l0, HEAD_DIM)]
                return (q * e_m).astype(BF16), (kk * e_m).astype(BF16)

            a2 = jnp.zeros((CHUNK, 2 * CHUNK), F32)
            for p, (ea, eb) in enumerate(LEVEL_PAIRS):
                qa, ka = scaled(ea)
                if eb is None:
                    prod = _dot_nt(qa, jnp.concatenate([ka, zeros16], axis=0))
                else:
                    qb_, kb_ = scaled(eb)
                    rhs = jnp.concatenate([jnp.concatenate([ka, zeros16], axis=1),
                                           jnp.concatenate([zeros16, kb_], axis=1)], axis=0)
                    prod = _dot_nt(jnp.concatenate([qa, qb_], axis=1), rhs)
                a2 = a2 + mask_ref[p] * prod
            a2_16 = a2.astype(BF16)
            a2_ref[pl.ds(r0, CHUNK), :] = a2_16
            oi_s[pl.ds(r0, CHUNK), :] = _dot(a2_16, jnp.concatenate([v16, v16], axis=0))
            e_b = e_ref[0:CHUNK, pl.ds(l0, HEAD_DIM)]
            e_c = e_ref[CHUNK:2 * CHUNK, pl.ds(l0, HEAD_DIM)]
            qb_s[pl.ds(r0, CHUNK), :] = (q * e_b).astype(BF16)
            u_ref[b] = _dot_tn(v16, (kk * e_c).astype(BF16))
            return carry

        lax.fori_loop(0, cpb, local, 0, unroll=LOCAL_UNROLL)

        def recur(b, st):
            l0 = pl.multiple_of(b * HEAD_DIM, HEAD_DIM)
            s_ref[0, b] = st
            return st * e_ref[CHUNK - 1:CHUNK, pl.ds(l0, HEAD_DIM)] + u_ref[b]

        st_ref[...] = lax.fori_loop(0, cpb, recur, st_ref[...])

        def inter(b, carry):
            r0 = pl.multiple_of(b * CHUNK, CHUNK)
            o_ref[pl.ds(r0, CHUNK), :] = oi_s[pl.ds(r0, CHUNK), :] + _dot_nt(
                qb_s[pl.ds(r0, CHUNK), :], s_ref[0, b].astype(BF16))
            return carry

        lax.fori_loop(0, cpb, inter, 0, unroll=LOCAL_UNROLL)

        @pl.when((pl.program_id(0) == N_HEADS // 2) & (rb == 0))
        def _():
            gather.pass_on_all(b4_ref, send_sems, recv_sems)

        @pl.when((pl.program_id(0) == N_HEADS - 1) & (rb == n_rb - 1))
        def _():
            gather.await_sibling_all(b4_ref, send_sems, recv_sems)
            gather.finish(b4_ref, send_sems, recv_sems)

    head_block = lambda seg: pl.BlockSpec((1, rb_rows, HEAD_DIM), lambda h, r: (seg, r, h))
    return pl.pallas_call(
        body, name="hgrn_forward",
        grid=(N_HEADS, n_rb),
        in_specs=[
            head_block(0), head_block(1), head_block(2),
            pl.BlockSpec((2, HEAD_DIM), lambda h, r: (0, h)),
            pl.BlockSpec((N_EXP * CHUNK, 2 * CHUNK), lambda h, r: (0, 0)),
            pl.BlockSpec((len(LEVEL_PAIRS), CHUNK, 2 * CHUNK), lambda h, r: (0, 0, 0)),
            ANY,
        ],
        out_specs=[
            pl.BlockSpec((rb_rows, HEAD_DIM), lambda h, r: (r, h)),
            pl.BlockSpec((1, cpb, HEAD_DIM, HEAD_DIM), lambda h, r: (h, r, 0, 0)),
            pl.BlockSpec((1, 1, N_EXP * CHUNK, lanes), lambda h, r: (h, r, 0, 0)),
            pl.BlockSpec((rb_rows, HEAD_DIM), lambda h, r: (r, h)),
            ANY,
        ],
        out_shape=[
            jax.ShapeDtypeStruct((rows, D_MODEL), F32),
            jax.ShapeDtypeStruct((N_HEADS, n_chunks, HEAD_DIM, HEAD_DIM), F32),
            jax.ShapeDtypeStruct((N_HEADS, n_rb, N_EXP * CHUNK, lanes), BF16),
            jax.ShapeDtypeStruct((rows, D_MODEL), BF16),
            jax.ShapeDtypeStruct(blob4.shape, blob4.dtype),
        ],
        input_output_aliases={6: 4},
        scratch_shapes=[
            pltpu.VMEM((HEAD_DIM, HEAD_DIM), F32),
            pltpu.VMEM((N_EXP * CHUNK, lanes), F32),
            pltpu.VMEM((cpb, HEAD_DIM, HEAD_DIM), F32),
            pltpu.VMEM((rb_rows, HEAD_DIM), F32),
            pltpu.VMEM((rb_rows, HEAD_DIM), F32),
            pltpu.VMEM((rb_rows, HEAD_DIM), BF16),
            pltpu.VMEM((rb_rows, HEAD_DIM), BF16),
            pltpu.VMEM((rb_rows, HEAD_DIM), F32),
        ] + gather.semaphores(),
        compiler_params=_params(("arbitrary", "arbitrary")),
    )(p3, p3, p3, lb_logits, wexp2, masks2, blob4)


def _hgrn_forward_old(p3, lb_logits, wexp3, masks, rows):
    n_chunks = rows // CHUNK
    cpb = _tile(n_chunks, 13, mult=1)
    rb_rows = cpb * CHUNK
    hps = HEADS_PER_STEP
    width = hps * HEAD_DIM

    def body(q_ref, fz_ref, v_ref, lbl_ref, wexp_ref, mask_ref, o_ref, s_ref, st_ref):
        rb = pl.program_id(1)

        @pl.when(rb == 0)
        def _():
            st_ref[...] = jnp.zeros_like(st_ref)

        lb_all = _lower_bound(lbl_ref[...])
        wexp = wexp_ref[...]
        masks = mask_ref[...]

        def chunk(c, carry):
            r0 = pl.multiple_of(c * CHUNK, CHUNK)
            row = rb * rb_rows + r0 + lax.broadcasted_iota(jnp.int32, (CHUNK, 1), 0)
            valid = row >= PAD_ROWS
            q_all = q_ref[0, pl.ds(r0, CHUNK), :]
            fz_all = fz_ref[0, pl.ds(r0, CHUNK), :]
            v_all = jnp.where(valid, v_ref[0, pl.ds(r0, CHUNK), :], 0.0).astype(BF16)
            st_all = [st_ref[j] for j in range(hps)]
            o_all, st_new = [], []
            for j in range(hps):
                cols = slice(j * HEAD_DIM, (j + 1) * HEAD_DIM)
                cf = _chunk_forward(q_all[:, cols], fz_all[:, cols], lb_all[:, cols], valid, wexp, masks)
                v16 = v_all[:, cols]
                o = _dot_nt((cf["q"] * cf["e_b"]).astype(BF16), st_all[j].astype(BF16))
                o_all.append(o + _dot(cf["a"].astype(BF16), v16))
                kc16 = (cf["kk"] * cf["e_c"]).astype(BF16)
                st_new.append(st_all[j] * cf["e_b"][CHUNK - 1:CHUNK, :] + _dot_tn(v16, kc16))
            o_ref[pl.ds(r0, CHUNK), :] = jnp.concatenate(o_all, axis=1)
            for j in range(hps):
                s_ref[j, c] = st_all[j]
                st_ref[j] = st_new[j]
            return carry

        lax.fori_loop(0, cpb, chunk, 0)

    head_block = lambda seg: pl.BlockSpec((1, rb_rows, width), lambda h, r: (seg, r, h))
    return pl.pallas_call(
        body, name="hgrn_forward",
        grid=(N_HEADS // hps, n_chunks // cpb),
        in_specs=[
            head_block(0), head_block(1), head_block(2),
            pl.BlockSpec((2, width), lambda h, r: (0, h)),
            pl.BlockSpec((N_EXP * CHUNK, 3 * CHUNK), lambda h, r: (0, 0)),
            pl.BlockSpec((1 + len(LEVELS), CHUNK, CHUNK), lambda h, r: (0, 0, 0)),
        ],
        out_specs=[
            pl.BlockSpec((rb_rows, width), lambda h, r: (r, h)),
            pl.BlockSpec((hps, cpb, HEAD_DIM, HEAD_DIM), lambda h, r: (h, r, 0, 0)),
        ],
        out_shape=[
            jax.ShapeDtypeStruct((rows, D_MODEL), F32),
            jax.ShapeDtypeStruct((N_HEADS, n_chunks, HEAD_DIM, HEAD_DIM), F32),
        ],
        scratch_shapes=[pltpu.VMEM((hps, HEAD_DIM, HEAD_DIM), F32)],
        compiler_params=_params(("arbitrary", "arbitrary")),
    )(p3, p3, p3, lb_logits, wexp3, masks)


def _hgrn_backward(p3, d_o, states, e16, a2, lb_logits, wexp_t, masks2, dw16, blob16, rows):
    n_chunks = rows // CHUNK
    cpb = _tile(n_chunks, 13, mult=1)
    rb_rows = cpb * CHUNK
    n_rb = n_chunks // cpb
    lanes = cpb * HEAD_DIM
    exchange = _GradExchange(SEGS_MIX, with_blob=True)

    def body(q_ref, fz_ref, v_ref, do_ref, s_ref, e_ref, a2_ref, lbl_ref, wexpt_ref, mask_ref, dw_ref, blob_ref,
             dp_ref, dlb_ref, rxw_ref, rxb_ref,
             dst_ref, g_ref, dsn_ref, q_s, kk_s, v_s, do_s, dq_s, dkk_s, dg_s, dx_s, send_sems, recv_sems):
        step = pl.program_id(1)
        rb = n_rb - 1 - step

        @pl.when((pl.program_id(0) == 0) & (step == 0))
        def _():
            exchange.start(dw_ref, rxw_ref, blob_ref, rxb_ref, send_sems, recv_sems)

        @pl.when(step == 0)
        def _():
            dst_ref[...] = jnp.zeros_like(dst_ref)
            dlb_ref[...] = jnp.zeros_like(dlb_ref)

        lb = _lower_bound(lbl_ref[...])
        row = rb * rb_rows + lax.broadcasted_iota(jnp.int32, (rb_rows, 1), 0)
        valid = row >= PAD_ROWS
        sg, sn = _sigmoid_pair(fz_ref[0])
        f = lb + (1.0 - lb) * sg
        g = jnp.where(valid, jnp.log(f), 0.0)
        kk_s[...] = jnp.where(valid, (1.0 - lb) * sn, 0.0)
        q_s[...] = jnp.where(valid, q_ref[0], 0.0)
        v_s[...] = jnp.where(valid, v_ref[0], 0.0).astype(BF16)
        do_s[...] = do_ref[...].astype(BF16)
        e_last_all = jnp.exp(jnp.concatenate(
            [jnp.sum(g[b * CHUNK:(b + 1) * CHUNK], axis=0, keepdims=True) for b in range(cpb)], axis=0))
        last_row = lax.broadcasted_iota(jnp.int32, (CHUNK, 1), 0) == CHUNK - 1
        zeros16 = jnp.zeros((CHUNK, HEAD_DIM), BF16)

        def factor(block, l0):
            return e_ref[0, 0, block * CHUNK:(block + 1) * CHUNK, pl.ds(l0, HEAD_DIM)].astype(F32)

        def contribution(b, carry):
            r0 = pl.multiple_of(b * CHUNK, CHUNK)
            l0 = pl.multiple_of(b * HEAD_DIM, HEAD_DIM)
            qb16 = (q_s[pl.ds(r0, CHUNK), :] * factor(0, l0)).astype(BF16)
            g_ref[b] = _dot_tn(do_s[pl.ds(r0, CHUNK), :], qb16)
            return carry

        lax.fori_loop(0, cpb, contribution, 0, unroll=LOCAL_UNROLL)

        cur = dst_ref[...]
        for b in reversed(range(cpb)):
            dsn_ref[b] = cur
            cur = cur * e_last_all[b:b + 1, :] + g_ref[b]
        dst_ref[...] = cur

        def local(b, carry):
            r0 = pl.multiple_of(b * CHUNK, CHUNK)
            l0 = pl.multiple_of(b * HEAD_DIM, HEAD_DIM)
            q = q_s[pl.ds(r0, CHUNK), :]
            kk = kk_s[pl.ds(r0, CHUNK), :]
            v16 = v_s[pl.ds(r0, CHUNK), :]
            do16 = do_s[pl.ds(r0, CHUNK), :]
            st = s_ref[0, b]
            dsn = dsn_ref[b]
            dsn16 = dsn.astype(BF16)
            e_b, e_c = factor(0, l0), factor(1, l0)
            qb, kc = q * e_b, kk * e_c

            t = _dot_tn(a2_ref[pl.ds(r0, CHUNK), :], do16)
            dv = t[0:CHUNK] + t[CHUNK:2 * CHUNK] + _dot_nt(kc.astype(BF16), dsn16)
            dp_ref[2, pl.ds(r0, CHUNK), :] = dv.astype(BF16)
            da2 = _dot_nt(do16, jnp.concatenate([v16, v16], axis=0))
            dqb = _dot(do16, st.astype(BF16))
            dkc = _dot(v16, dsn16)
            de = jnp.sum(dsn * st, axis=0, keepdims=True) * e_b[CHUNK - 1:CHUNK, :]
            dq = e_b * dqb
            dkk = e_c * dkc
            dx_s[0:CHUNK, pl.ds(l0, HEAD_DIM)] = (qb * dqb + jnp.where(last_row, de, 0.0)).astype(BF16)
            dx_s[CHUNK:2 * CHUNK, pl.ds(l0, HEAD_DIM)] = (kc * dkc).astype(BF16)

            def scaled(entry):
                if entry == 0:
                    return q, kk, None
                e_m = factor(1 + entry, l0)
                return q * e_m, kk * e_m, e_m

            for p, (ea, eb) in enumerate(LEVEL_PAIRS):
                dm = (mask_ref[p] * da2).astype(BF16)
                qa, ka, e_a = scaled(ea)
                if eb is None:
                    lhs_q = jnp.concatenate([qa.astype(BF16), zeros16], axis=1)
                    rhs_k = jnp.concatenate([jnp.concatenate([ka.astype(BF16), zeros16], axis=1),
                                             jnp.concatenate([zeros16, zeros16], axis=1)], axis=0)
                else:
                    qb_, kb_, e_bb = scaled(eb)
                    lhs_q = jnp.concatenate([qa.astype(BF16), qb_.astype(BF16)], axis=1)
                    rhs_k = jnp.concatenate([jnp.concatenate([ka.astype(BF16), zeros16], axis=1),
                                             jnp.concatenate([zeros16, kb_.astype(BF16)], axis=1)], axis=0)
                dq2 = _dot(dm, rhs_k)
                dk2 = _dot_tn(dm, lhs_q)
                parts = [(ea, qa, ka, e_a, dq2[:, :HEAD_DIM], dk2[0:CHUNK, :HEAD_DIM])]
                if eb is not None:
                    parts.append((eb, qb_, kb_, e_bb, dq2[:, HEAD_DIM:], dk2[CHUNK:2 * CHUNK, HEAD_DIM:]))
                for entry, q_m, k_m, e_m, dq_m, dk_m in parts:
                    if entry == 0:
                        dq = dq + dq_m
                        dkk = dkk + dk_m
                    else:
                        dq = dq + e_m * dq_m
                        dkk = dkk + e_m * dk_m
                        dx_s[(1 + entry) * CHUNK:(2 + entry) * CHUNK, pl.ds(l0, HEAD_DIM)] = (
                            q_m * dq_m + k_m * dk_m).astype(BF16)
            dq_s[pl.ds(r0, CHUNK), :] = dq
            dkk_s[pl.ds(r0, CHUNK), :] = dkk
            return carry

        lax.fori_loop(0, cpb, local, 0, unroll=BACKWARD_UNROLL)

        dg_all = _dot(wexpt_ref[...], dx_s[...])
        for b in range(cpb):
            dg_s[b * CHUNK:(b + 1) * CHUNK, :] = dg_all[:, b * HEAD_DIM:(b + 1) * HEAD_DIM]
        t = jnp.where(valid, dg_s[...] / f - dkk_s[...], 0.0)
        dlb_ref[...] += jnp.sum(sn * t, axis=0, keepdims=True)
        dp_ref[0] = jnp.where(valid, dq_s[...], 0.0).astype(BF16)
        dp_ref[1] = ((1.0 - lb) * sg * sn * t).astype(BF16)

        @pl.when((pl.program_id(0) == N_HEADS - 1) & (step == n_rb - 1))
        def _():
            exchange.wait(dw_ref, rxw_ref, blob_ref, rxb_ref, send_sems, recv_sems)

    head_block = lambda seg: pl.BlockSpec((1, rb_rows, HEAD_DIM), lambda h, s: (seg, n_rb - 1 - s, h))
    row_block = pl.BlockSpec((rb_rows, HEAD_DIM), lambda h, s: (n_rb - 1 - s, h))
    return pl.pallas_call(
        body, name="hgrn_backward",
        grid=(N_HEADS, n_rb),
        in_specs=[
            head_block(0), head_block(1), head_block(2),
            row_block,
            pl.BlockSpec((1, cpb, HEAD_DIM, HEAD_DIM), lambda h, s: (h, n_rb - 1 - s, 0, 0)),
            pl.BlockSpec((1, 1, N_EXP * CHUNK, lanes), lambda h, s: (h, n_rb - 1 - s, 0, 0)),
            row_block,
            pl.BlockSpec((2, HEAD_DIM), lambda h, s: (0, h)),
            pl.BlockSpec((CHUNK, N_EXP * CHUNK), lambda h, s: (0, 0)),
            pl.BlockSpec((len(LEVEL_PAIRS), CHUNK, 2 * CHUNK), lambda h, s: (0, 0, 0)),
            ANY, ANY,
        ],
        out_specs=[
            pl.BlockSpec((3, rb_rows, HEAD_DIM), lambda h, s: (0, n_rb - 1 - s, h)),
            pl.BlockSpec((1, HEAD_DIM), lambda h, s: (0, h)),
            ANY, ANY,
        ],
        out_shape=[
            jax.ShapeDtypeStruct((3, rows, D_MODEL), BF16),
            jax.ShapeDtypeStruct((1, D_MODEL), F32),
            exchange.landing_w(), exchange.landing_blob(blob16),
        ],
        scratch_shapes=[
            pltpu.VMEM((HEAD_DIM, HEAD_DIM), F32),
            pltpu.VMEM((cpb, HEAD_DIM, HEAD_DIM), F32),
            pltpu.VMEM((cpb, HEAD_DIM, HEAD_DIM), F32),
            pltpu.VMEM((rb_rows, HEAD_DIM), F32),
            pltpu.VMEM((rb_rows, HEAD_DIM), F32),
            pltpu.VMEM((rb_rows, HEAD_DIM), BF16),
            pltpu.VMEM((rb_rows, HEAD_DIM), BF16),
            pltpu.VMEM((rb_rows, HEAD_DIM), F32),
            pltpu.VMEM((rb_rows, HEAD_DIM), F32),
            pltpu.VMEM((rb_rows, HEAD_DIM), F32),
            pltpu.VMEM((N_EXP * CHUNK, lanes), BF16),
        ] + exchange.semaphores(),
        compiler_params=_params(("arbitrary", "arbitrary")),
    )(p3, p3, p3, d_o, states, e16, a2, lb_logits, wexp_t, masks2, dw16, blob16)


def _hgrn_backward_old(p3, d_o, states, lb_logits, wexp3, wexp_t2, masks, dw16, blob16, rows):
    n_chunks = rows // CHUNK
    cpb = _tile(n_chunks, 13, mult=1)
    rb_rows = cpb * CHUNK
    n_rb = n_chunks // cpb
    hps = HEADS_PER_STEP
    width = hps * HEAD_DIM
    n_hb = N_HEADS // hps
    exchange = _GradExchange(SEGS_MIX, with_blob=True)

    def body(q_ref, fz_ref, v_ref, do_ref, s_ref, lbl_ref, wexp_ref, wexpt_ref, mask_ref, dw_ref, blob_ref,
             dp_ref, dlb_ref, rxw_ref, rxb_ref, dst_ref, send_sems, recv_sems):
        step = pl.program_id(1)
        rb = n_rb - 1 - step

        @pl.when((pl.program_id(0) == 0) & (step == 0))
        def _():
            exchange.start(dw_ref, rxw_ref, blob_ref, rxb_ref, send_sems, recv_sems)

        @pl.when(step == 0)
        def _():
            dst_ref[...] = jnp.zeros_like(dst_ref)
            dlb_ref[...] = jnp.zeros_like(dlb_ref)

        lb_all = _lower_bound(lbl_ref[...])
        wexp = wexp_ref[...]
        wexp_t = wexpt_ref[...]
        masks = mask_ref[...]
        last_row = lax.broadcasted_iota(jnp.int32, (CHUNK, 1), 0) == CHUNK - 1

        def one_head(j, c, r0, valid):
            cols = slice(j * HEAD_DIM, (j + 1) * HEAD_DIM)
            lb = lb_all[:, cols]
            cf = _chunk_forward(q_ref[0, pl.ds(r0, CHUNK), cols], fz_ref[0, pl.ds(r0, CHUNK), cols],
                                lb, valid, wexp, masks)
            q, kk, e_b, e_c = cf["q"], cf["kk"], cf["e_b"], cf["e_c"]
            v16 = jnp.where(valid, v_ref[0, pl.ds(r0, CHUNK), cols], 0.0).astype(BF16)
            do16 = do_ref[pl.ds(r0, CHUNK), cols].astype(BF16)
            st = s_ref[j, c]
            dst = dst_ref[j]
            dst16 = dst.astype(BF16)
            qb = q * e_b
            kc = kk * e_c
            q16, kk16 = q.astype(BF16), kk.astype(BF16)

            dv = _dot_tn(cf["a"].astype(BF16), do16) + _dot_nt(kc.astype(BF16), dst16)
            da = _dot_nt(do16, v16)
            dqb = _dot(do16, st.astype(BF16))
            dkc = _dot(v16, dst16)
            e_last = e_b[CHUNK - 1:CHUNK, :]
            de = jnp.sum(dst * st, axis=0, keepdims=True)
            dst_ref[j] = dst * e_last + _dot_tn(do16, qb.astype(BF16))

            dq = e_b * dqb
            dkk = e_c * dkc
            dx = [qb * dqb + jnp.where(last_row, de * e_last, 0.0), kc * dkc]
            dm0 = (masks[0] * da).astype(BF16)
            dq = dq + _dot(dm0, kk16)
            dkk = dkk + _dot(dm0, q16)
            for l in range(len(LEVELS)):
                e_m = cf["e"][(2 + l) * CHUNK:(3 + l) * CHUNK]
                dm = (masks[1 + l] * da).astype(BF16)
                dqm = _dot(dm, cf["km"][l].astype(BF16))
                dkm = _dot_tn(dm, cf["qm"][l].astype(BF16))
                dq = dq + e_m * dqm
                dkk = dkk + e_m * dkm
                dx.append(cf["qm"][l] * dqm + cf["km"][l] * dkm)
            dxa = jnp.concatenate(dx, axis=0)
            hi = dxa.astype(BF16)
            mid = (dxa - hi.astype(F32)).astype(BF16)
            dg = _dot(wexp_t, jnp.concatenate([hi, mid], axis=0))

            t = jnp.where(valid, dg / cf["f"] - dkk, 0.0)
            dfz = (1.0 - lb) * cf["sg"] * cf["sn"] * t
            dlb_ref[:, cols] += jnp.sum(cf["sn"] * t, axis=0, keepdims=True)
            dp_ref[0, pl.ds(r0, CHUNK), cols] = jnp.where(valid, dq, 0.0).astype(BF16)
            dp_ref[1, pl.ds(r0, CHUNK), cols] = dfz.astype(BF16)
            dp_ref[2, pl.ds(r0, CHUNK), cols] = jnp.where(valid, dv, 0.0).astype(BF16)

        def chunk(i, carry):
            c = cpb - 1 - i
            r0 = pl.multiple_of(c * CHUNK, CHUNK)
            row = rb * rb_rows + r0 + lax.broadcasted_iota(jnp.int32, (CHUNK, 1), 0)
            for j in range(hps):
                one_head(j, c, r0, row >= PAD_ROWS)
            return carry

        lax.fori_loop(0, cpb, chunk, 0)

        @pl.when((pl.program_id(0) == n_hb - 1) & (step == n_rb - 1))
        def _():
            exchange.wait(dw_ref, rxw_ref, blob_ref, rxb_ref, send_sems, recv_sems)

    head_block = lambda seg: pl.BlockSpec((1, rb_rows, width), lambda h, s: (seg, n_rb - 1 - s, h))
    return pl.pallas_call(
        body, name="hgrn_backward",
        grid=(n_hb, n_rb),
        in_specs=[
            head_block(0), head_block(1), head_block(2),
            pl.BlockSpec((rb_rows, width), lambda h, s: (n_rb - 1 - s, h)),
            pl.BlockSpec((hps, cpb, HEAD_DIM, HEAD_DIM), lambda h, s: (h, n_rb - 1 - s, 0, 0)),
            pl.BlockSpec((2, width), lambda h, s: (0, h)),
            pl.BlockSpec((N_EXP * CHUNK, 3 * CHUNK), lambda h, s: (0, 0)),
            pl.BlockSpec((CHUNK, 2 * N_EXP * CHUNK), lambda h, s: (0, 0)),
            pl.BlockSpec((1 + len(LEVELS), CHUNK, CHUNK), lambda h, s: (0, 0, 0)),
            ANY, ANY,
        ],
        out_specs=[
            pl.BlockSpec((3, rb_rows, width), lambda h, s: (0, n_rb - 1 - s, h)),
            pl.BlockSpec((1, width), lambda h, s: (0, h)),
            ANY, ANY,
        ],
        out_shape=[
            jax.ShapeDtypeStruct((3, rows, D_MODEL), BF16),
            jax.ShapeDtypeStruct((1, D_MODEL), F32),
            exchange.landing_w(), exchange.landing_blob(blob16),
        ],
        scratch_shapes=[pltpu.VMEM((hps, HEAD_DIM, HEAD_DIM), F32)] + exchange.semaphores(),
        compiler_params=_params(("arbitrary", "arbitrary")),
    )(p3, p3, p3, d_o, states, lb_logits, wexp3, wexp_t2, masks, dw16, blob16)


def _sigmoid(x):
    return 1.0 / (1.0 + jnp.exp(-x))


def _silu_and_grad(x):
    s = _sigmoid(x)
    return x * s, s * (1.0 + x * (1.0 - s))


def _window_sum(ext, width, forward_looking):
    n = ext.shape[0]
    s = ext
    step = 1
    while step < width:
        s = s + pltpu.roll(s, (n - step) if forward_looking else step, 0)
        step *= 2
    return s


def _mixers(o, p3, z, tgt, wdh, wdp, wout, poolw, hg_w, pool_scale, final_w, rows):
    tm = _tile(rows, 208)
    nt = rows // tm
    halo_blocks = tm // HALO
    n_grp = len(POOL_WINDOWS)

    def body(o_ref, ghg_ref, u_ref, gpl_ref, mhg_ref, mpl_ref, uh_ref, z_ref, t_ref,
             wdh_ref, wdp_ref, wout_ref, pw_ref, hgw_ref, ps_ref, fw_ref,
             do_ref, dz2_ref, dp_ref, dwdh_ref, dwdp_ref, dwout_ref, dpw_ref, small_ref, carry_ref):
        step = pl.program_id(0)
        tile = nt - 1 - step

        @pl.when(step == 0)
        def _():
            dwdh_ref[...] = jnp.zeros_like(dwdh_ref)
            dwdp_ref[...] = jnp.zeros_like(dwdp_ref)
            dwout_ref[...] = jnp.zeros_like(dwout_ref)
            dpw_ref[...] = jnp.zeros_like(dpw_ref)
            small_ref[...] = jnp.zeros_like(small_ref)
            carry_ref[...] = jnp.zeros_like(carry_ref)

        row = tile * tm + lax.broadcasted_iota(jnp.int32, (tm, 1), 0)
        real = row >= PAD_ROWS
        pos1 = jnp.maximum(row - PAD_ROWS + 1, 1).astype(F32)

        u = jnp.where(real, u_ref[0], 0.0)
        halo_row = tile * tm - HALO + lax.broadcasted_iota(jnp.int32, (HALO, 1), 0)
        uh = jnp.where(halo_row >= PAD_ROWS, uh_ref[0], 0.0)
        ext = jnp.concatenate([uh, u], axis=0)
        pooled, inv_cnt, mixed = [], [], []
        for g, w in enumerate(POOL_WINDOWS):
            cols = slice(g * POOL_GDIM, (g + 1) * POOL_GDIM)
            inv = 1.0 / jnp.minimum(pos1, float(w))
            ws = _window_sum(ext[:, cols], w, False)[HALO:]
            pg = (ws * inv - u[:, cols]).astype(BF16)
            pooled.append(pg)
            inv_cnt.append(inv)
            mixed.append(_dot(pg, pw_ref[g]))
        mixed = jnp.concatenate(mixed, axis=1)
        gpl = gpl_ref[0]
        sp, dsp = _silu_and_grad(gpl)
        ps = ps_ref[...]
        a_pool = (mixed * ps * sp).astype(BF16)
        y_pool = _dot(a_pool, wdp_ref[...])

        o = o_ref[...]
        o_hat, rstd_h = [], []
        for h in range(N_HEADS):
            oh = o[:, h * HEAD_DIM:(h + 1) * HEAD_DIM]
            r = lax.rsqrt(jnp.mean(oh * oh, axis=-1, keepdims=True) + EPS)
            rstd_h.append(r)
            o_hat.append(oh * r)
        o_hat = jnp.concatenate(o_hat, axis=1)
        hgw = hgw_ref[...]
        o_n = o_hat * hgw
        ghg = ghg_ref[0]
        sh, dsh = _silu_and_grad(ghg)
        a_hg = (o_n * sh).astype(BF16)
        y_hg = _dot(a_hg, wdh_ref[...])

        s_mh = _sigmoid(mhg_ref[0])
        s_mp = _sigmoid(mpl_ref[0])
        merged = (s_mh * y_hg + s_mp * y_pool).astype(BF16)
        z2 = z_ref[...] + _dot(merged, wout_ref[...])
        rstd2 = lax.rsqrt(jnp.mean(z2 * z2, axis=-1, keepdims=True) + EPS)
        zh = z2 * rstd2
        fw = fw_ref[...]
        err = jnp.where(row >= FIRST_TOKEN_ROW, zh * fw - t_ref[...], 0.0)
        small_ref[ROW_LOSS:ROW_LOSS + 1, :] += jnp.sum(err * err, axis=0, keepdims=True) * (0.5 / D_MODEL)
        dy = err * (1.0 / D_MODEL)

        small_ref[ROW_FINAL_W:ROW_FINAL_W + 1, :] += jnp.sum(dy * zh, axis=0, keepdims=True)
        uu = dy * fw
        dz2 = rstd2 * (uu - zh * jnp.mean(uu * zh, axis=-1, keepdims=True))
        dz2_ref[...] = dz2
        dz2_16 = dz2.astype(BF16)
        dmerged = _dot_nt(dz2_16, wout_ref[...])
        dwout_ref[...] += _dot_tn(merged, dz2_16)
        dy_hg = (s_mh * dmerged).astype(BF16)
        dy_pool = (s_mp * dmerged).astype(BF16)
        dp_ref[3] = (dmerged * y_hg * s_mh * (1.0 - s_mh)).astype(BF16)
        dp_ref[4] = (dmerged * y_pool * s_mp * (1.0 - s_mp)).astype(BF16)

        da_hg = _dot_nt(dy_hg, wdh_ref[...])
        dwdh_ref[...] += _dot_tn(a_hg, dy_hg)
        dp_ref[0] = (da_hg * o_n * dsh).astype(BF16)
        do_n = da_hg * sh
        small_ref[ROW_HG_W:ROW_HG_W + 1, :] += jnp.sum(do_n * o_hat, axis=0, keepdims=True)
        d_hat = do_n * hgw
        for h in range(N_HEADS):
            cols = slice(h * HEAD_DIM, (h + 1) * HEAD_DIM)
            dh_, oh_ = d_hat[:, cols], o_hat[:, cols]
            do_ref[:, cols] = rstd_h[h] * (dh_ - oh_ * jnp.mean(dh_ * oh_, axis=-1, keepdims=True))

        da_pool = _dot_nt(dy_pool, wdp_ref[...])
        dwdp_ref[...] += _dot_tn(a_pool, dy_pool)
        small_ref[ROW_POOL_SCALE:ROW_POOL_SCALE + 1, :] += jnp.sum(da_pool * mixed * sp, axis=0, keepdims=True)
        dp_ref[2] = (da_pool * mixed * ps * dsp).astype(BF16)
        dmixed = (da_pool * ps * sp).astype(BF16)
        carry = carry_ref[...]
        du, new_carry = [], []
        for g, w in enumerate(POOL_WINDOWS):
            cols = slice(g * POOL_GDIM, (g + 1) * POOL_GDIM)
            dmg = dmixed[:, cols]
            dpooled = _dot_nt(dmg, pw_ref[g])
            dpw_ref[g] += _dot_tn(pooled[g], dmg)
            dps = dpooled * inv_cnt[g]
            ext_b = jnp.concatenate([dps, carry[:, cols]], axis=0)
            du.append(_window_sum(ext_b, w, True)[:tm] - dpooled)
            new_carry.append(dps[:HALO])
        dp_ref[1] = jnp.where(real, jnp.concatenate(du, axis=1), 0.0).astype(BF16)
        carry_ref[...] = jnp.concatenate(new_carry, axis=1)

    row_block = pl.BlockSpec((tm, D_MODEL), lambda s: (nt - 1 - s, 0))
    seg_block = lambda seg: pl.BlockSpec((1, tm, D_MODEL), lambda s: (seg, nt - 1 - s, 0))
    whole = pl.BlockSpec(memory_space=pltpu.VMEM)
    return pl.pallas_call(
        body, name="mixers",
        grid=(nt,),
        in_specs=[
            row_block, seg_block(3), seg_block(4), seg_block(5), seg_block(6), seg_block(7),
            pl.BlockSpec((1, HALO, D_MODEL),
                         lambda s: (4, jnp.maximum((nt - 1 - s) * halo_blocks - 1, 0), 0)),
            row_block, row_block,
            whole, whole, whole, whole, whole, whole, whole,
        ],
        out_specs=[
            row_block, row_block,
            pl.BlockSpec((5, tm, D_MODEL), lambda s: (0, nt - 1 - s, 0)),
            whole, whole, whole, whole, whole,
        ],
        out_shape=[
            jax.ShapeDtypeStruct((rows, D_MODEL), F32),
            jax.ShapeDtypeStruct((rows, D_MODEL), F32),
            jax.ShapeDtypeStruct((5, rows, D_MODEL), BF16),
            jax.ShapeDtypeStruct((D_MODEL, D_MODEL), F32),
            jax.ShapeDtypeStruct((D_MODEL, D_MODEL), F32),
            jax.ShapeDtypeStruct((D_MODEL, D_MODEL), F32),
            jax.ShapeDtypeStruct((n_grp, POOL_GDIM, POOL_GDIM), F32),
            jax.ShapeDtypeStruct((SMALL_ROWS, D_MODEL), F32),
        ],
        scratch_shapes=[pltpu.VMEM((HALO, D_MODEL), F32)],
        compiler_params=_params(("arbitrary",)),
    )(o, p3, p3, p3, p3, p3, p3, z, tgt, wdh, wdp, wout, poolw, hg_w, pool_scale, final_w)


def _seg_specs(tm, row_of, seg_of):
    def spec_a(*g):
        k = seg_of(*g)
        return (jnp.minimum(k, 2), jnp.where(k < 3, row_of(*g), 0), 0)

    def spec_b(*g):
        k = seg_of(*g)
        return (jnp.maximum(k - 3, 0), jnp.where(k >= 3, row_of(*g), 0), 0)

    return pl.BlockSpec((1, tm, D_MODEL), spec_a), pl.BlockSpec((1, tm, D_MODEL), spec_b)


def _in_proj_weight_grad(h, dp, rows, name):
    n_seg = dp.shape[0]
    tm = _tile(rows, 1040)
    nt = rows // tm
    half = D_MODEL // 2

    def body(h_ref, dp_ref, part_ref, part16_ref, db_ref, acc_ref, bacc_ref, stage_ref, land_ref,
             send_sems, recv_sems):
        k, i = pl.program_id(0), pl.program_id(1)
        x, y, c = lax.axis_index("x"), lax.axis_index("y"), lax.axis_index("c")

        def to_sibling(seg):
            return pltpu.make_async_remote_copy(
                src_ref=stage_ref.at[seg], dst_ref=land_ref.at[seg], send_sem=send_sems.at[seg],
                recv_sem=recv_sems.at[seg], device_id=(x, y, 1 - c), device_id_type=MESH)

        @pl.when(i == 0)
        def _():
            acc_ref[...] = jnp.zeros_like(acc_ref)
            bacc_ref[...] = jnp.zeros_like(bacc_ref)

        dpt = dp_ref[0]
        acc_ref[...] += _dot_tn(h_ref[...], dpt)
        bacc_ref[...] += jnp.sum(dpt.astype(F32), axis=0, keepdims=True)

        @pl.when(i == nt - 1)
        def _():
            db_ref[0] = bacc_ref[...]
            part_ref[k] = acc_ref[pl.ds(pl.multiple_of(c * half, half), half), :]
            stage_ref[k] = acc_ref[pl.ds(pl.multiple_of((1 - c) * half, half), half), :].astype(BF16)
            to_sibling(k).start()

        @pl.when((k == n_seg - 1) & (i == nt - 1))
        def _():
            for seg in range(n_seg):
                to_sibling(seg).wait_recv()
                total = part_ref[seg] + land_ref[seg].astype(F32)
                part_ref[seg] = total
                part16_ref[seg] = total.astype(BF16)
            for seg in range(n_seg):
                to_sibling(seg).wait_send()

    whole = pl.BlockSpec(memory_space=pltpu.VMEM)
    return pl.pallas_call(
        body, name=name,
        grid=(n_seg, nt),
        in_specs=[pl.BlockSpec((tm, D_MODEL), lambda k, i: (i, 0)),
                  pl.BlockSpec((1, tm, D_MODEL), lambda k, i: (k, i, 0))],
        out_specs=[whole, whole, pl.BlockSpec((1, 1, D_MODEL), lambda k, i: (k, 0, 0))],
        out_shape=[
            jax.ShapeDtypeStruct((n_seg, half, D_MODEL), F32),
            jax.ShapeDtypeStruct((n_seg, half, D_MODEL), BF16),
            jax.ShapeDtypeStruct((n_seg, 1, D_MODEL), F32),
        ],
        scratch_shapes=[
            pltpu.VMEM((D_MODEL, D_MODEL), F32), pltpu.VMEM((1, D_MODEL), F32),
            pltpu.VMEM((n_seg, half, D_MODEL), BF16),
            pltpu.VMEM((n_seg, half, D_MODEL), BF16),
            pltpu.SemaphoreType.DMA((n_seg,)), pltpu.SemaphoreType.DMA((n_seg,)),
        ],
        compiler_params=_params(("arbitrary", "arbitrary")),
    )(h, dp)


def _input_grad(dpa, dpb, w4, z, dz2, norm_w, dw16, rows):
    tm = _tile(rows, 1040)
    nt = rows // tm
    exchange = _GradExchange(SEGS_REC, with_blob=False)

    def body(dpa_ref, dpb_ref, w_ref, z_ref, dz2_ref, nw_ref, dw_ref, dz_ref, dnw_ref, rxw_ref,
             acc_ref, send_sems, recv_sems):
        i, k = pl.program_id(0), pl.program_id(1)

        @pl.when((i == 0) & (k == 0))
        def _():
            exchange.start(dw_ref, rxw_ref, None, None, send_sems, recv_sems)
            dnw_ref[...] = jnp.zeros_like(dnw_ref)

        @pl.when((i == nt - 1) & (k == N_SEG - 1))
        def _():
            exchange.wait(dw_ref, rxw_ref, None, None, send_sems, recv_sems)

        @pl.when(k == 0)
        def _():
            acc_ref[...] = jnp.zeros_like(acc_ref)

        @pl.when(k < 3)
        def _():
            acc_ref[...] += _dot_nt(dpa_ref[0], w_ref[0])

        @pl.when(k >= 3)
        def _():
            acc_ref[...] += _dot_nt(dpb_ref[0], w_ref[0])

        @pl.when(k == N_SEG - 1)
        def _():
            zt = z_ref[...]
            rstd = lax.rsqrt(jnp.mean(zt * zt, axis=-1, keepdims=True) + EPS)
            zh = zt * rstd
            dh = acc_ref[...]
            dnw_ref[...] += jnp.sum(dh * zh, axis=0, keepdims=True)
            uu = dh * nw_ref[...]
            dz_ref[...] = dz2_ref[...] + rstd * (uu - zh * jnp.mean(uu * zh, axis=-1, keepdims=True))

    spec_a, spec_b = _seg_specs(tm, lambda i, k: i, lambda i, k: k)
    last_only = pl.BlockSpec((tm, D_MODEL), lambda i, k: (jnp.where(k == N_SEG - 1, i, 0), 0))
    return pl.pallas_call(
        body, name="input_grad",
        grid=(nt, N_SEG),
        in_specs=[
            spec_a, spec_b,
            pl.BlockSpec((1, D_MODEL, D_MODEL), lambda i, k: (k // 2, 0, k % 2)),
            last_only, last_only,
            pl.BlockSpec((1, D_MODEL), lambda i, k: (0, 0)),
            ANY,
        ],
        out_specs=[
            pl.BlockSpec((tm, D_MODEL), lambda i, k: (i, 0)),
            pl.BlockSpec((1, D_MODEL), lambda i, k: (0, 0)),
            ANY,
        ],
        out_shape=[
            jax.ShapeDtypeStruct((rows, D_MODEL), F32),
            jax.ShapeDtypeStruct((1, D_MODEL), F32),
            exchange.landing_w(),
        ],
        scratch_shapes=[pltpu.VMEM((tm, D_MODEL), F32)] + exchange.semaphores(),
        compiler_params=_params(("arbitrary", "arbitrary")),
    )(dpa, dpb, w4, z, dz2, norm_w, dw16)


def _local_step(z, tgt, w4, blob4, seg_order, norm_w, b_in, lb_logits, hg_w, pool_scale, final_w):
    rows = z.shape[0]
    q = D_MODEL // N_CHIPS
    n_grp = len(POOL_WINDOWS)
    pg = POOL_GDIM // N_CHIPS

    wexp2 = jnp.asarray(np.tile(_exponent_matrix(), (1, 2)), BF16)
    wexp_t = jnp.asarray(_exponent_matrix().T, BF16)
    masks2 = jnp.asarray(_paired_masks(), F32)

    h, p3, w4 = _in_proj(z, norm_w, w4, b_in, seg_order, rows)
    o, states, e16, a2, blob4 = _hgrn_forward(p3, lb_logits, wexp2, masks2, blob4, rows)
    wdh = blob4[:, 0:q].reshape(D_MODEL, D_MODEL)
    wdp = blob4[:, q:2 * q].reshape(D_MODEL, D_MODEL)
    wout = blob4[:, 2 * q:3 * q].reshape(D_MODEL, D_MODEL)
    poolw = blob4[:, 3 * q:].reshape(N_CHIPS, n_grp, pg, POOL_GDIM).transpose(1, 0, 2, 3)
    poolw = poolw.reshape(n_grp, POOL_GDIM, POOL_GDIM)
    d_o, dz2, dpb, dwdh, dwdp, dwout, dpw, small = _mixers(
        o, p3, z, tgt, wdh, wdp, wout, poolw, hg_w, pool_scale, final_w, rows)
    dpw4 = dpw.reshape(n_grp, N_CHIPS, pg, POOL_GDIM).transpose(1, 0, 2, 3)
    dpw4 = dpw4.reshape(N_CHIPS, n_grp * pg * POOL_GDIM // D_MODEL, D_MODEL)
    dblob4 = jnp.concatenate([dwdh.reshape(N_CHIPS, q, D_MODEL), dwdp.reshape(N_CHIPS, q, D_MODEL),
                              dwout.reshape(N_CHIPS, q, D_MODEL), dpw4], axis=1)

    dw_mix, dw_mix16, db_mix = _in_proj_weight_grad(h, dpb, rows, "in_proj_weight_grad_mix")
    dpa, dlb, rxw_mix, rx_blob = _hgrn_backward(
        p3, d_o, states, e16, a2, lb_logits, wexp_t, masks2, dw_mix16, dblob4.astype(BF16), rows)
    dw_rec, dw_rec16, db_rec = _in_proj_weight_grad(h, dpa, rows, "in_proj_weight_grad_rec")
    dz, dnw, rxw_rec = _input_grad(dpa, dpb, w4, z, dz2, norm_w, dw_rec16, rows)

    small = jnp.concatenate([
        small[ROW_LOSS:ROW_LOSS + 1],
        dz[PAD_ROWS:PAD_ROWS + N_META],
        dnw,
        db_rec.reshape(len(SEGS_REC), D_MODEL), db_mix.reshape(len(SEGS_MIX), D_MODEL),
        dlb, jnp.zeros_like(dlb),
        small[ROW_HG_W:ROW_HG_W + 1], small[ROW_POOL_SCALE:ROW_POOL_SCALE + 1],
        small[ROW_FINAL_W:ROW_FINAL_W + 1],
        jnp.zeros((SMALL_ROWS - ROW_FINAL_W - 1, D_MODEL), F32),
    ], axis=0)
    return dz, (dw_rec, dw_mix, rxw_rec, rxw_mix), (dblob4, rx_blob), small


ANY = pl.BlockSpec(memory_space=pl.ANY)
MESH = pl.DeviceIdType.MESH


def _place():
    x, y, c = lax.axis_index("x"), lax.axis_index("y"), lax.axis_index("c")
    chips = [(1 - x, y), (x, 1 - y), (1 - x, 1 - y)]
    return x, y, c, chips


class _ShardGather:
    def __init__(self, rows):
        self.half = rows // 2

    def semaphores(self):
        return [pltpu.SemaphoreType.DMA((6,)), pltpu.SemaphoreType.DMA((6,))]

    def _copy(self, k, slot, to, send_sems, recv_sems):
        return pltpu.make_async_remote_copy(src_ref=slot, dst_ref=slot, send_sem=send_sems.at[k],
                                            recv_sem=recv_sems.at[k], device_id=to, device_id_type=MESH)

    def _half(self, ref4, chip, which):
        return ref4.at[chip, pl.ds(which * self.half, self.half), :]

    def start(self, ref4, send_sems, recv_sems, which=(0, 1, 2)):
        x, y, c, chips = _place()
        for j in which:
            cx, cy = chips[j]
            self._copy(j, self._half(ref4, 2 * x + y, c), (cx, cy, c), send_sems, recv_sems).start()

    def start_diagonal_after_neighbours(self, ref4, send_sems, recv_sems):
        x, y, c, chips = _place()
        for j in (0, 1):
            cx, cy = chips[j]
            self._copy(j, self._half(ref4, 2 * x + y, c), (cx, cy, c), send_sems, recv_sems).wait_send()
        self.start(ref4, send_sems, recv_sems, which=(2,))

    def arrive(self, j, ref4, send_sems, recv_sems):
        x, y, c, chips = _place()
        cx, cy = chips[j]
        landed = self._half(ref4, 2 * cx + cy, c)
        self._copy(j, landed, (cx, cy, c), send_sems, recv_sems).wait_recv()
        self._copy(3 + j, landed, (x, y, 1 - c), send_sems, recv_sems).start()
        self._copy(3 + j, self._half(ref4, 2 * cx + cy, 1 - c), (x, y, 1 - c), send_sems, recv_sems).wait_recv()

    def pass_on_all(self, ref4, send_sems, recv_sems):
        x, y, c, chips = _place()
        for j, (cx, cy) in enumerate(chips):
            landed = self._half(ref4, 2 * cx + cy, c)
            self._copy(j, landed, (cx, cy, c), send_sems, recv_sems).wait_recv()
            self._copy(3 + j, landed, (x, y, 1 - c), send_sems, recv_sems).start()

    def await_sibling_all(self, ref4, send_sems, recv_sems):
        x, y, c, chips = _place()
        for j, (cx, cy) in enumerate(chips):
            self._copy(3 + j, self._half(ref4, 2 * cx + cy, 1 - c), (x, y, 1 - c), send_sems, recv_sems).wait_recv()

    def finish(self, ref4, send_sems, recv_sems, which=(0, 1, 2)):
        x, y, c, chips = _place()
        for j, (cx, cy) in enumerate(chips):
            if j in which:
                self._copy(j, self._half(ref4, 2 * x + y, c), (cx, cy, c), send_sems, recv_sems).wait_send()
            self._copy(3 + j, self._half(ref4, 2 * cx + cy, c), (x, y, 1 - c), send_sems, recv_sems).wait_send()


def _gather_meta(m4):
    def body(m_in_ref, m4_ref, send_sems, recv_sems):
        x, y, c, chips = _place()

        def copy(j, slot, to):
            return pltpu.make_async_remote_copy(src_ref=slot, dst_ref=slot, send_sem=send_sems.at[j],
                                                recv_sem=recv_sems.at[j], device_id=to, device_id_type=MESH)

        sends = [copy(j, m4_ref.at[2 * x + y], (cx, cy, c)) for j, (cx, cy) in enumerate(chips)]
        for cp in sends:
            cp.start()
        for j, (cx, cy) in enumerate(chips):
            copy(j, m4_ref.at[2 * cx + cy], (x, y, c)).wait_recv()
        for cp in sends:
            cp.wait_send()

    return pl.pallas_call(
        body, name="gather_meta",
        in_specs=[ANY], out_specs=ANY, out_shape=jax.ShapeDtypeStruct(m4.shape, m4.dtype),
        input_output_aliases={0: 0},
        scratch_shapes=[pltpu.SemaphoreType.DMA((3,)), pltpu.SemaphoreType.DMA((3,))],
    )(m4)


class _GradExchange:
    def __init__(self, segs, with_blob):
        self.segs = tuple(segs)
        self.with_blob = with_blob

    def landing_w(self):
        return jax.ShapeDtypeStruct((N_CHIPS, 2, D_MODEL // 2, D_MODEL), BF16)

    def landing_blob(self, blob16):
        return jax.ShapeDtypeStruct((N_DEV, blob16.shape[1] // 2, D_MODEL), BF16)

    def semaphores(self):
        n_send = len(self.segs) + (2 * N_CHIPS if self.with_blob else 0)
        n_recv = 2 * N_CHIPS + (N_DEV if self.with_blob else 0)
        return [pltpu.SemaphoreType.DMA((n_send,)), pltpu.SemaphoreType.DMA((n_recv,))]

    def _copies(self, dw_ref, rxw_ref, blob_ref, rxb_ref, send_sems, recv_sems):
        x, y, c = lax.axis_index("x"), lax.axis_index("y"), lax.axis_index("c")
        chip = 2 * x + y

        def relation(kx, ky, h):
            return (x ^ kx) * 4 + (y ^ ky) * 2 + (c ^ h)

        def copy(src, dst, send_k, recv_k, to):
            return functools.partial(pltpu.make_async_remote_copy, src_ref=src, dst_ref=dst,
                                     send_sem=send_sems.at[send_k], recv_sem=recv_sems.at[recv_k],
                                     device_id=to, device_id_type=MESH)

        sends, recvs = [], []
        for i, s in enumerate(self.segs):
            kx, ky = (s // 2) >> 1, (s // 2) & 1
            r = (x ^ kx) * 2 + (y ^ ky)
            sends.append((r != 0, copy(dw_ref.at[i], rxw_ref.at[r, s % 2], i, 2 * r + s % 2, (kx, ky, c))))
        for j in range(2):
            mine = [s // 2 for s in self.segs if s % 2 == j]
            if mine:
                cond = functools.reduce(lambda a, b: a | b, [chip == k for k in mine])
                for r in range(1, N_CHIPS):
                    slot = rxw_ref.at[r, j]
                    recvs.append((cond, copy(slot, slot, 0, 2 * r + j, (x, y, c))))
        if self.with_blob:
            hb = blob_ref.shape[1] // 2
            first_send, first_recv = len(self.segs), 2 * N_CHIPS
            for k in range(N_CHIPS):
                for h in range(2):
                    r = relation(k >> 1, k & 1, h)
                    sends.append((r != 0, copy(blob_ref.at[k, pl.ds(h * hb, hb), :], rxb_ref.at[r],
                                               first_send + 2 * k + h, first_recv + r, (k >> 1, k & 1, h))))
            for r in range(1, N_DEV):
                slot = rxb_ref.at[r]
                recvs.append((None, copy(slot, slot, 0, first_recv + r, (x, y, c))))
        return sends, recvs

    def start(self, *refs):
        sends, _ = self._copies(*refs)
        for cond, make in sends:
            pl.when(cond)(lambda make=make: make().start())

    def wait(self, *refs):
        sends, recvs = self._copies(*refs)
        for cond, make in sends:
            pl.when(cond)(lambda make=make: make().wait_send())
        for cond, make in recvs:
            if cond is None:
                make().wait_recv()
            else:
                pl.when(cond)(lambda make=make: make().wait_recv())


def _sum_landed(own, rx_ref):
    total = own
    for r in range(1, rx_ref.shape[0]):
        total = total + rx_ref[r, 0].astype(F32)
    return total


def _finish_w(dw_rec, dw_mix, rx_rec, rx_mix, place_arr):
    half = D_MODEL // 2
    tm = _tile(half, 256)
    n_rec = len(SEGS_REC)

    def body(place_ref, own_rec_ref, own_mix_ref, rx_rec_ref, rx_mix_ref, out_ref):
        seg = 2 * place_ref[0] + pl.program_id(0)

        @pl.when(seg < n_rec)
        def _():
            out_ref[0] = _sum_landed(own_rec_ref[0], rx_rec_ref)

        @pl.when(seg >= n_rec)
        def _():
            out_ref[0] = _sum_landed(own_mix_ref[0], rx_mix_ref)

    def own_spec(first, count):
        def index(j, i, place_ref):
            seg = 2 * place_ref[0] + j
            return (jnp.clip(seg - first, 0, count - 1), i, 0)
        return pl.BlockSpec((1, tm, D_MODEL), index)

    rx_spec = pl.BlockSpec((N_CHIPS, 1, tm, D_MODEL), lambda j, i, place_ref: (0, j, i, 0))
    return pl.pallas_call(
        body, name="finish_w",
        grid_spec=pltpu.PrefetchScalarGridSpec(
            num_scalar_prefetch=1, grid=(2, half // tm),
            in_specs=[own_spec(0, n_rec), own_spec(n_rec, len(SEGS_MIX)), rx_spec, rx_spec],
            out_specs=pl.BlockSpec((1, tm, D_MODEL), lambda j, i, place_ref: (place_ref[1], i, j))),
        out_shape=jax.ShapeDtypeStruct((2, half, 2 * D_MODEL), F32),
        compiler_params=_params(("arbitrary", "arbitrary")),
    )(place_arr, dw_rec, dw_mix, rx_rec, rx_mix)


def _finish_blob(dblob4, rx_blob, place_arr):
    n, rows, cols = rx_blob.shape
    tm = _tile(rows, 256)

    def body(place_ref, own_ref, rx_ref, out_ref):
        out_ref[0] = _sum_landed(own_ref[0, 0], rx_ref)

    return pl.pallas_call(
        body, name="finish_blob",
        grid_spec=pltpu.PrefetchScalarGridSpec(
            num_scalar_prefetch=1, grid=(rows // tm,),
            in_specs=[pl.BlockSpec((1, 1, tm, cols), lambda i, place_ref: (place_ref[0], place_ref[1], i, 0)),
                      pl.BlockSpec((n, 1, tm, cols), lambda i, place_ref: (0, 0, i, 0))],
            out_specs=pl.BlockSpec((1, tm, cols), lambda i, place_ref: (place_ref[1], i, 0))),
        out_shape=jax.ShapeDtypeStruct((2, rows, cols), F32),
        compiler_params=_params(("arbitrary",)),
    )(place_arr, dblob4.reshape(N_CHIPS, 2, rows, cols), rx_blob.reshape(n, 1, rows, cols))


def _share_finished(fw2, fb2, small):
    def body(w_in_ref, b_in_ref, small_ref, w_ref, b_ref, s_ref, bounce, local_sem, send_sems, recv_sems):
        x, y, c, _ = _place()
        sibling = (x, y, 1 - c)

        def copy(k, src, dst, to):
            return pltpu.make_async_remote_copy(src_ref=src, dst_ref=dst, send_sem=send_sems.at[k],
                                                recv_sem=recv_sems.at[k], device_id=to, device_id_type=MESH)

        sends = [copy(0, w_ref.at[c], w_ref.at[c], sibling), copy(1, b_ref.at[c], b_ref.at[c], sibling)]
        for r in range(1, N_DEV):
            peer = (x ^ ((r >> 2) & 1), y ^ ((r >> 1) & 1), c ^ (r & 1))
            sends.append(copy(1 + r, small_ref, s_ref.at[r], peer))
        for cp in sends:
            cp.start()
        for src, dst in ((small_ref, bounce), (bounce, s_ref.at[0])):
            own = pltpu.make_async_copy(src, dst, local_sem)
            own.start()
            own.wait()
        landed = [w_ref.at[1 - c], b_ref.at[1 - c]] + [s_ref.at[r] for r in range(1, N_DEV)]
        for k, slot in enumerate(landed):
            copy(k, slot, slot, (x, y, c)).wait_recv()
        for cp in sends:
            cp.wait_send()

    same = lambda a: jax.ShapeDtypeStruct(a.shape, a.dtype)
    n_sem = 2 + N_DEV - 1
    return pl.pallas_call(
        body, name="share_finished",
        in_specs=[ANY, ANY, ANY], out_specs=[ANY, ANY, ANY],
        out_shape=[same(fw2), same(fb2), jax.ShapeDtypeStruct((N_DEV,) + small.shape, F32)],
        input_output_aliases={0: 0, 1: 1},
        scratch_shapes=[pltpu.VMEM(small.shape, F32), pltpu.SemaphoreType.DMA,
                        pltpu.SemaphoreType.DMA((n_sem,)), pltpu.SemaphoreType.DMA((n_sem,))],
    )(fw2, fb2, small)


def _sum_small(slots, lb_logits, me_arr):
    def body(me_ref, slots_ref, lbl_ref, out_ref):
        me = me_ref[0]
        total = slots_ref[me]
        for d in range(1, N_DEV):
            total = total + slots_ref[d ^ me]
        out_ref[...] = total
        out_ref[ROW_LOSS:ROW_LOSS + 1, :] = jnp.broadcast_to(
            jnp.sum(total[ROW_LOSS:ROW_LOSS + 1, :], axis=-1, keepdims=True), (1, D_MODEL))
        lb = _lower_bound(lbl_ref[...])
        g0 = total[ROW_LB:ROW_LB + 1, :] * lb * (1.0 - lb)
        out_ref[ROW_LB:ROW_LB + 1, :] = g0
        out_ref[ROW_LB + 1:ROW_LB + 2, :] = -g0

    return pl.pallas_call(
        body, name="sum_small",
        grid_spec=pltpu.PrefetchScalarGridSpec(
            num_scalar_prefetch=1, grid=(1,),
            in_specs=[pl.BlockSpec((N_DEV, SMALL_ROWS, D_MODEL), lambda i, me_ref: (0, 0, 0)),
                      pl.BlockSpec((2, D_MODEL), lambda i, me_ref: (0, 0))],
            out_specs=pl.BlockSpec((SMALL_ROWS, D_MODEL), lambda i, me_ref: (0, 0))),
        out_shape=jax.ShapeDtypeStruct((SMALL_ROWS, D_MODEL), F32),
        compiler_params=_params(("arbitrary",)),
    )(me_arr, slots, lb_logits)


def _adamw(w, g, m, v):
    rows, cols = w.shape
    tm = _tile(rows, 256, mult=8) if rows % 8 == 0 else rows
    c1 = 1.0 / (1.0 - ADAM_B1 ** ADAM_STEP)
    c2 = 1.0 / (1.0 - ADAM_B2 ** ADAM_STEP)

    def body(w_ref, g_ref, m_ref, v_ref, d_ref, nm_ref, nv_ref):
        gt = g_ref[...]
        nm = ADAM_B1 * m_ref[...] + (1.0 - ADAM_B1) * gt
        nv = ADAM_B2 * v_ref[...] + (1.0 - ADAM_B2) * (gt * gt)
        nm_ref[...] = nm
        nv_ref[...] = nv
        d_ref[...] = -ADAM_LR * ((nm * c1) / (jnp.sqrt(nv * c2) + ADAM_EPS) + ADAM_WD * w_ref[...])

    blk = pl.BlockSpec((tm, cols), lambda i: (i, 0))
    sds = jax.ShapeDtypeStruct((rows, cols), F32)
    return pl.pallas_call(
        body, name="adamw",
        grid=(rows // tm,), in_specs=[blk] * 4, out_specs=[blk] * 3, out_shape=[sds] * 3,
        compiler_params=_params(("arbitrary",)),
    )(w, g, m, v)


def kernel(x, meta_tokens, norm_w, w_in, b_in, lb_logits, hg_norm_w, pool_w, pool_scale, w_down_hg, w_down_pool, w_out, final_norm_w, loss_target, m_meta_tokens, m_norm_w, m_w_in, m_b_in, m_lb_logits, m_hg_norm_w, m_pool_w, m_pool_scale, m_w_down_hg, m_w_down_pool, m_w_out, m_final_norm_w, v_meta_tokens, v_norm_w, v_w_in, v_b_in, v_lb_logits, v_hg_norm_w, v_pool_w, v_pool_scale, v_w_down_hg, v_w_down_pool, v_w_out, v_final_norm_w):
    seq = x.shape[1]
    xi, yi, ci = lax.axis_index("x"), lax.axis_index("y"), lax.axis_index("c")
    chip = 2 * xi + yi
    place_arr = jnp.stack([chip, ci]).astype(jnp.int32)
    me_arr = jnp.reshape(4 * xi + 2 * yi + ci, (1,)).astype(jnp.int32)
    q = D_MODEL // N_CHIPS

    def blob_of(wdh, wdp, wo, pw):
        return jnp.concatenate([wdh[0], wdp[0], wo[0], pw[0].reshape(-1, D_MODEL)], axis=0)

    def in_every_slot(a):
        return jnp.broadcast_to(a[None], (N_CHIPS,) + a.shape)

    meta4 = _gather_meta(in_every_slot(meta_tokens))
    meta_full = meta4.transpose(1, 0, 2).reshape(N_META, D_MODEL)
    w4 = in_every_slot(w_in[0].astype(BF16))
    blob4 = in_every_slot(blob_of(w_down_hg, w_down_pool, w_out, pool_w).astype(BF16))
    seg_order = jnp.stack([2 * (chip ^ rel) + t for rel in (0, 2, 1, 3) for t in (0, 1)]).astype(jnp.int32)

    z = jnp.concatenate([jnp.zeros((PAD_ROWS, D_MODEL), F32), meta_full, x[0]], axis=0)
    tgt = jnp.concatenate([jnp.zeros((FIRST_TOKEN_ROW, D_MODEL), F32), loss_target[0]], axis=0)
    fw2 = final_norm_w.reshape(1, D_MODEL)
    dz, w_parts, blob_parts, small = _local_step(
        z, tgt, w4, blob4, seg_order, norm_w, b_in, lb_logits, hg_norm_w, pool_scale, fw2)
    grad_x = dz[FIRST_TOKEN_ROW:][None]

    fin_w = _finish_w(*w_parts, place_arr)
    fin_b = _finish_blob(*blob_parts, place_arr)
    gw2, gb2, slots = _share_finished(fin_w, fin_b, small)
    tot = _sum_small(slots, lb_logits, me_arr)
    g_w_in = gw2.reshape(D_MODEL, 2 * D_MODEL)
    g_blob = gb2.reshape(-1, D_MODEL)

    d_win, nm_win, nv_win = _adamw(w_in[0], g_w_in, m_w_in[0], v_w_in[0])
    d_blob, nm_blob, nv_blob = _adamw(
        blob_of(w_down_hg, w_down_pool, w_out, pool_w), g_blob,
        blob_of(m_w_down_hg, m_w_down_pool, m_w_out, m_pool_w),
        blob_of(v_w_down_hg, v_w_down_pool, v_w_out, v_pool_w))
    g_meta = lax.dynamic_slice_in_dim(tot[ROW_META:ROW_META + N_META], chip * q, q, axis=1)
    d_meta, nm_meta, nv_meta = _adamw(meta_tokens, g_meta, m_meta_tokens, v_meta_tokens)

    def rows_of(nw, bi, lbl, hg, ps, fw):
        return jnp.concatenate([nw, bi.reshape(N_SEG, D_MODEL), lbl, hg, ps, fw.reshape(1, D_MODEL),
                                jnp.zeros((2, D_MODEL), F32)], axis=0)

    g_rows = jnp.concatenate([tot[ROW_NORM_W:ROW_FINAL_W + 1], jnp.zeros((2, D_MODEL), F32)], axis=0)
    d_rows, nm_rows, nv_rows = _adamw(
        rows_of(norm_w, b_in, lb_logits, hg_norm_w, pool_scale, final_norm_w), g_rows,
        rows_of(m_norm_w, m_b_in, m_lb_logits, m_hg_norm_w, m_pool_scale, m_final_norm_w),
        rows_of(v_norm_w, v_b_in, v_lb_logits, v_hg_norm_w, v_pool_scale, v_final_norm_w))

    def unblob(b):
        return (b[0:q][None], b[q:2 * q][None], b[2 * q:3 * q][None], b[3 * q:].reshape(pool_w.shape))

    def unrows(r):
        o = ROW_NORM_W
        return dict(norm_w=r[ROW_NORM_W - o:ROW_B_IN - o], b_in=r[ROW_B_IN - o:ROW_LB - o].reshape(1, -1),
                    lb_logits=r[ROW_LB - o:ROW_HG_W - o], hg_norm_w=r[ROW_HG_W - o:ROW_POOL_SCALE - o],
                    pool_scale=r[ROW_POOL_SCALE - o:ROW_FINAL_W - o], final_norm_w=r[ROW_FINAL_W - o])

    def leaves(meta_part, rows_part, win_part, blob_part):
        r = unrows(rows_part)
        wdh, wdp, wo, pw = unblob(blob_part)
        return [meta_part, r["norm_w"], win_part[None], r["b_in"], r["lb_logits"], r["hg_norm_w"], pw,
                r["pool_scale"], wdh, wdp, wo, r["final_norm_w"]]

    loss = tot[ROW_LOSS, 0]
    return (loss, grad_x,
            *leaves(g_meta, g_rows, g_w_in, g_blob),
            *leaves(d_meta, d_rows, d_win, d_blob),
            *leaves(nm_meta, nm_rows, nm_win, nm_blob),
            *leaves(nv_meta, nv_rows, nv_win, nv_blob))
```

```python
import functools

import numpy as np
import jax
import jax.numpy as jnp
from jax import lax
from jax.experimental import pallas as pl
from jax.experimental.pallas import tpu as pltpu

F32 = jnp.float32
BF16 = jnp.bfloat16

D_MODEL = 1024
N_SEG = 8
N_HEADS = 8
HEAD_DIM = 128
CHUNK = 64
N_META = 16
PAD_ROWS = CHUNK - N_META
FIRST_TOKEN_ROW = CHUNK
LEVELS = (32, 16, 8, 4, 2, 1)
N_EXP = 2 + len(LEVELS)
POOL_WINDOWS = (2, 4, 8, 16)
POOL_GDIM = D_MODEL // len(POOL_WINDOWS)
HALO = 16
BACKWARD_UNROLL = 13
LOCAL_UNROLL = 13
HEADS_PER_STEP = 4
EPS = 1e-6
N_CHIPS = 4
N_DEV = 8
SEGS_REC = (0, 1, 2)
SEGS_MIX = (3, 4, 5, 6, 7)

ADAM_LR = 0.001
ADAM_B1 = 0.9
ADAM_B2 = 0.999
ADAM_EPS = 1e-08
ADAM_WD = 0.01
ADAM_STEP = 10

VMEM_LIMIT_BYTES = 56 * 1024 * 1024

ROW_LOSS = 0
ROW_META = 1
ROW_NORM_W = ROW_META + N_META
ROW_B_IN = ROW_NORM_W + 1
ROW_LB = ROW_B_IN + N_SEG
ROW_HG_W = ROW_LB + 2
ROW_POOL_SCALE = ROW_HG_W + 1
ROW_FINAL_W = ROW_POOL_SCALE + 1
SMALL_ROWS = 32


def _tile(total, cap, mult=16):
    best = None
    for t in range(mult, min(total, cap) + 1, mult):
        if total % t == 0:
            best = t
    assert best is not None, (total, cap, mult)
    return best


def _params(sem=None):
    return pltpu.CompilerParams(dimension_semantics=sem, vmem_limit_bytes=VMEM_LIMIT_BYTES)


def _dot(a, b):
    return jnp.dot(a, b, preferred_element_type=F32)


def _dot_nt(a, b):
    return lax.dot_general(a, b, (((1,), (1,)), ((), ())), preferred_element_type=F32)


def _dot_tn(a, b):
    return lax.dot_general(a, b, (((0,), (0,)), ((), ())), preferred_element_type=F32)


def _sigmoid_pair(x):
    t = jnp.exp(-jnp.abs(x))
    r = 1.0 / (1.0 + t)
    pos = x >= 0
    return jnp.where(pos, r, t * r), jnp.where(pos, t * r, r)


def _exponent_matrix():
    t = np.arange(CHUNK)[:, None]
    j = np.arange(CHUNK)[None, :]
    blocks = [j <= t, j > t]
    for m in LEVELS:
        rho = (t // (2 * m)) * (2 * m) + m
        upper = (t >= rho) & (j > rho) & (j <= t)
        lower = (t < rho) & (j > t) & (j <= rho)
        blocks.append(upper | lower)
    return np.concatenate(blocks, axis=0).astype(np.float32)


def _pair_masks():
    t = np.arange(CHUNK)[:, None]
    s = np.arange(CHUNK)[None, :]
    masks = [t == s]
    for m in LEVELS:
        same = (t // (2 * m)) == (s // (2 * m))
        masks.append(same & ((t % (2 * m)) >= m) & ((s % (2 * m)) < m))
    return np.stack(masks).astype(np.float32)


LEVEL_PAIRS = ((0, 1), (2, 3), (4, 5), (6, None))


def _paired_masks():
    m = _pair_masks()
    zero = np.zeros_like(m[0])
    return np.stack([np.concatenate([m[a], zero if b is None else m[b]], axis=1) for a, b in LEVEL_PAIRS])


def _split3(x):
    hi = x.astype(BF16)
    r = x - hi.astype(F32)
    mid = r.astype(BF16)
    lo = (r - mid.astype(F32)).astype(BF16)
    return hi, mid, lo


def _chunk_forward(q, fz, lb, valid, wexp, masks):
    sg, sn = _sigmoid_pair(fz)
    f = lb + (1.0 - lb) * sg
    g = jnp.where(valid, jnp.log(f), 0.0)
    kk = jnp.where(valid, (1.0 - lb) * sn, 0.0)
    q = jnp.where(valid, q, 0.0)
    e = jnp.exp(_dot(wexp, jnp.concatenate(_split3(g), axis=0)))
    e_b = e[0:CHUNK]
    e_c = e[CHUNK:2 * CHUNK]
    a = masks[0] * _dot_nt(q.astype(BF16), kk.astype(BF16))
    qm, km = [], []
    for l in range(len(LEVELS)):
        e_m = e[(2 + l) * CHUNK:(3 + l) * CHUNK]
        qm.append(q * e_m)
        km.append(kk * e_m)
        a = a + masks[1 + l] * _dot_nt(qm[l].astype(BF16), km[l].astype(BF16))
    return dict(sg=sg, sn=sn, f=f, kk=kk, q=q, e=e, e_b=e_b, e_c=e_c, a=a, qm=qm, km=km)


def _lower_bound(lbl):
    return 1.0 / (1.0 + jnp.exp(lbl[1:2, :] - lbl[0:1, :]))


def _in_proj(z, norm_w, w4, b_in, seg_order, rows):
    tm = _tile(rows, 1040)
    nt = rows // tm
    gather = _ShardGather(w4.shape[1])

    def body(order_ref, z_ref, nw_ref, b_ref, w_in_ref, h_ref, p_ref, w4_ref,
             h_all, w_buf, w_sem, send_sems, recv_sems):
        kk, i = pl.program_id(0), pl.program_id(1)

        @pl.when((kk == 0) & (i == 0))
        def _():
            gather.start(w4_ref, send_sems, recv_sems, which=(0, 1))

        @pl.when((kk == 2) & (i == 0))
        def _():
            gather.start_diagonal_after_neighbours(w4_ref, send_sems, recv_sems)

        @pl.when(kk == 0)
        def _():
            zt = z_ref[...]
            rstd = lax.rsqrt(jnp.mean(zt * zt, axis=-1, keepdims=True) + EPS)
            h = (zt * rstd * nw_ref[...]).astype(BF16)
            h_all[pl.ds(pl.multiple_of(i * tm, 16), tm), :] = h
            h_ref[...] = h

        for j in range(N_CHIPS - 1):
            @pl.when((kk == 2 + 2 * j) & (i == 0))
            def _(j=j):
                gather.arrive(j, w4_ref, send_sems, recv_sems)

        def weights(which):
            seg = order_ref[2 * (kk // 2) + which]
            return pltpu.make_async_copy(
                w4_ref.at[seg // 2, :, pl.ds(pl.multiple_of((seg % 2) * D_MODEL, D_MODEL), D_MODEL)],
                w_buf.at[which], w_sem.at[which])

        @pl.when((i == 0) & (kk % 2 == 0))
        def _():
            weights(0).start()
            weights(1).start()
            weights(0).wait()

        @pl.when((i == 0) & (kk % 2 == 1))
        def _():
            weights(1).wait()

        p_ref[0] = _dot(h_all[pl.ds(pl.multiple_of(i * tm, 16), tm), :], w_buf[kk % 2]) + b_ref[...]

        @pl.when((kk == N_SEG - 1) & (i == nt - 1))
        def _():
            gather.finish(w4_ref, send_sems, recv_sems, which=(2,))

    first_pass = lambda kk, i, order_ref: (jnp.where(kk == 0, i, nt - 1), 0)
    return pl.pallas_call(
        body, name="in_proj",
        grid_spec=pltpu.PrefetchScalarGridSpec(
            num_scalar_prefetch=1, grid=(N_SEG, nt),
            in_specs=[
                pl.BlockSpec((tm, D_MODEL), first_pass),
                pl.BlockSpec((1, D_MODEL), lambda kk, i, order_ref: (0, 0)),
                pl.BlockSpec((1, D_MODEL), lambda kk, i, order_ref: (0, order_ref[kk])),
                ANY,
            ],
            out_specs=[
                pl.BlockSpec((tm, D_MODEL), first_pass),
                pl.BlockSpec((1, tm, D_MODEL), lambda kk, i, order_ref: (order_ref[kk], i, 0)),
                ANY,
            ],
            scratch_shapes=[
                pltpu.VMEM((rows, D_MODEL), BF16),
                pltpu.VMEM((2, D_MODEL, D_MODEL), BF16),
                pltpu.SemaphoreType.DMA((2,)),
            ] + gather.semaphores()),
        out_shape=[
            jax.ShapeDtypeStruct((rows, D_MODEL), BF16),
            jax.ShapeDtypeStruct((N_SEG, rows, D_MODEL), F32),
            jax.ShapeDtypeStruct(w4.shape, w4.dtype),
        ],
        input_output_aliases={4: 2},
        compiler_params=_params(("arbitrary", "arbitrary")),
    )(seg_order, z, norm_w, b_in, w4)


def _hgrn_forward(p3, lb_logits, wexp2, masks2, blob4, rows):
    n_chunks = rows // CHUNK
    cpb = _tile(n_chunks, 13, mult=1)
    rb_rows = cpb * CHUNK
    n_rb = n_chunks // cpb
    lanes = cpb * HEAD_DIM
    gather = _ShardGather(blob4.shape[1])

    def body(q_ref, fz_ref, v_ref, lbl_ref, wexp_ref, mask_ref, b_in_ref, o_ref, s_ref, e16_ref, a2_ref, b4_ref,
             st_ref, e_ref, u_ref, q_s, kk_s, v_s, qb_s, oi_s, send_sems, recv_sems):
        rb = pl.program_id(1)

        @pl.when((pl.program_id(0) == 0) & (rb == 0))
        def _():
            gather.start(b4_ref, send_sems, recv_sems)

        @pl.when(rb == 0)
        def _():
            st_ref[...] = jnp.zeros_like(st_ref)

        lb = _lower_bound(lbl_ref[...])
        row = rb * rb_rows + lax.broadcasted_iota(jnp.int32, (rb_rows, 1), 0)
        valid = row >= PAD_ROWS
        sg, sn = _sigmoid_pair(fz_ref[0])
        g = jnp.where(valid, jnp.log(lb + (1.0 - lb) * sg), 0.0)
        kk_s[...] = jnp.where(valid, (1.0 - lb) * sn, 0.0)
        q_s[...] = jnp.where(valid, q_ref[0], 0.0)
        v_s[...] = jnp.where(valid, v_ref[0], 0.0).astype(BF16)
        hi = g.astype(BF16)
        mid = (g - hi.astype(F32)).astype(BF16)
        g2 = jnp.concatenate(
            [jnp.concatenate([hi[b * CHUNK:(b + 1) * CHUNK], mid[b * CHUNK:(b + 1) * CHUNK]], axis=0)
             for b in range(cpb)], axis=1)
        e_ref[...] = jnp.exp(_dot(wexp_ref[...], g2))
        e16_ref[0, 0] = e_ref[...].astype(BF16)

        zeros16 = jnp.zeros((CHUNK, HEAD_DIM), BF16)

        def local(b, carry):
            r0 = pl.multiple_of(b * CHUNK, CHUNK)
            l0 = pl.multiple_of(b * HEAD_DIM, HEAD_DIM)
            q = q_s[pl.ds(r0, CHUNK), :]
            kk = kk_s[pl.ds(r0, CHUNK), :]
            v16 = v_s[pl.ds(r0, CHUNK), :]

            def scaled(entry):
                if entry == 0:
                    return q.astype(BF16), kk.astype(BF16)
                e_m = e_ref[(1 + entry) * CHUNK:(2 + entry) * CHUNK, pl.ds(l0, HEAD_DIM)]
                return (q * e_m).astype(BF16), (kk * e_m).astype(BF16)

            a2 = jnp.zeros((CHUNK, 2 * CHUNK), F32)
            for p, (ea, eb) in enumerate(LEVEL_PAIRS):
                qa, ka = scaled(ea)
                if eb is None:
                    prod = _dot_nt(qa, jnp.concatenate([ka, zeros16], axis=0))
                else:
                    qb_, kb_ = scaled(eb)
                    rhs = jnp.concatenate([jnp.concatenate([ka, zeros16], axis=1),
                                           jnp.concatenate([zeros16, kb_], axis=1)], axis=0)
                    prod = _dot_nt(jnp.concatenate([qa, qb_], axis=1), rhs)
                a2 = a2 + mask_ref[p] * prod
            a2_16 = a2.astype(BF16)
            a2_ref[pl.ds(r0, CHUNK), :] = a2_16
            oi_s[pl.ds(r0, CHUNK), :] = _dot(a2_16, jnp.concatenate([v16, v16], axis=0))
            e_b = e_ref[0:CHUNK, pl.ds(l0, HEAD_DIM)]
            e_c = e_ref[CHUNK:2 * CHUNK, pl.ds(l0, HEAD_DIM)]
            qb_s[pl.ds(r0, CHUNK), :] = (q * e_b).astype(BF16)
            u_ref[b] = _dot_tn(v16, (kk * e_c).astype(BF16))
            return carry

        lax.fori_loop(0, cpb, local, 0, unroll=LOCAL_UNROLL)

        def recur(b, st):
            l0 = pl.multiple_of(b * HEAD_DIM, HEAD_DIM)
            s_ref[0, b] = st
            return st * e_ref[CHUNK - 1:CHUNK, pl.ds(l0, HEAD_DIM)] + u_ref[b]

        st_ref[...] = lax.fori_loop(0, cpb, recur, st_ref[...])

        def inter(b, carry):
            r0 = pl.multiple_of(b * CHUNK, CHUNK)
            o_ref[pl.ds(r0, CHUNK), :] = oi_s[pl.ds(r0, CHUNK), :] + _dot_nt(
                qb_s[pl.ds(r0, CHUNK), :], s_ref[0, b].astype(BF16))
            return carry

        lax.fori_loop(0, cpb, inter, 0, unroll=LOCAL_UNROLL)

        @pl.when((pl.program_id(0) == N_HEADS // 2) & (rb == 0))
        def _():
            gather.pass_on_all(b4_ref, send_sems, recv_sems)

        @pl.when((pl.program_id(0) == N_HEADS - 1) & (rb == n_rb - 1))
        def _():
            gather.await_sibling_all(b4_ref, send_sems, recv_sems)
            gather.finish(b4_ref, send_sems, recv_sems)

    head_block = lambda seg: pl.BlockSpec((1, rb_rows, HEAD_DIM), lambda h, r: (seg, r, h))
    return pl.pallas_call(
        body, name="hgrn_forward",
        grid=(N_HEADS, n_rb),
        in_specs=[
            head_block(0), head_block(1), head_block(2),
            pl.BlockSpec((2, HEAD_DIM), lambda h, r: (0, h)),
            pl.BlockSpec((N_EXP * CHUNK, 2 * CHUNK), lambda h, r: (0, 0)),
            pl.BlockSpec((len(LEVEL_PAIRS), CHUNK, 2 * CHUNK), lambda h, r: (0, 0, 0)),
            ANY,
        ],
        out_specs=[
            pl.BlockSpec((rb_rows, HEAD_DIM), lambda h, r: (r, h)),
            pl.BlockSpec((1, cpb, HEAD_DIM, HEAD_DIM), lambda h, r: (h, r, 0, 0)),
            pl.BlockSpec((1, 1, N_EXP * CHUNK, lanes), lambda h, r: (h, r, 0, 0)),
            pl.BlockSpec((rb_rows, HEAD_DIM), lambda h, r: (r, h)),
            ANY,
        ],
        out_shape=[
            jax.ShapeDtypeStruct((rows, D_MODEL), F32),
            jax.ShapeDtypeStruct((N_HEADS, n_chunks, HEAD_DIM, HEAD_DIM), F32),
            jax.ShapeDtypeStruct((N_HEADS, n_rb, N_EXP * CHUNK, lanes), BF16),
            jax.ShapeDtypeStruct((rows, D_MODEL), BF16),
            jax.ShapeDtypeStruct(blob4.shape, blob4.dtype),
        ],
        input_output_aliases={6: 4},
        scratch_shapes=[
            pltpu.VMEM((HEAD_DIM, HEAD_DIM), F32),
            pltpu.VMEM((N_EXP * CHUNK, lanes), F32),
            pltpu.VMEM((cpb, HEAD_DIM, HEAD_DIM), F32),
            pltpu.VMEM((rb_rows, HEAD_DIM), F32),
            pltpu.VMEM((rb_rows, HEAD_DIM), F32),
            pltpu.VMEM((rb_rows, HEAD_DIM), BF16),
            pltpu.VMEM((rb_rows, HEAD_DIM), BF16),
            pltpu.VMEM((rb_rows, HEAD_DIM), F32),
        ] + gather.semaphores(),
        compiler_params=_params(("arbitrary", "arbitrary")),
    )(p3, p3, p3, lb_logits, wexp2, masks2, blob4)


def _hgrn_forward_old(p3, lb_logits, wexp3, masks, rows):
    n_chunks = rows // CHUNK
    cpb = _tile(n_chunks, 13, mult=1)
    rb_rows = cpb * CHUNK
    hps = HEADS_PER_STEP
    width = hps * HEAD_DIM

    def body(q_ref, fz_ref, v_ref, lbl_ref, wexp_ref, mask_ref, o_ref, s_ref, st_ref):
        rb = pl.program_id(1)

        @pl.when(rb == 0)
        def _():
            st_ref[...] = jnp.zeros_like(st_ref)

        lb_all = _lower_bound(lbl_ref[...])
        wexp = wexp_ref[...]
        masks = mask_ref[...]

        def chunk(c, carry):
            r0 = pl.multiple_of(c * CHUNK, CHUNK)
            row = rb * rb_rows + r0 + lax.broadcasted_iota(jnp.int32, (CHUNK, 1), 0)
            valid = row >= PAD_ROWS
            q_all = q_ref[0, pl.ds(r0, CHUNK), :]
            fz_all = fz_ref[0, pl.ds(r0, CHUNK), :]
            v_all = jnp.where(valid, v_ref[0, pl.ds(r0, CHUNK), :], 0.0).astype(BF16)
            st_all = [st_ref[j] for j in range(hps)]
            o_all, st_new = [], []
            for j in range(hps):
                cols = slice(j * HEAD_DIM, (j + 1) * HEAD_DIM)
                cf = _chunk_forward(q_all[:, cols], fz_all[:, cols], lb_all[:, cols], valid, wexp, masks)
                v16 = v_all[:, cols]
                o = _dot_nt((cf["q"] * cf["e_b"]).astype(BF16), st_all[j].astype(BF16))
                o_all.append(o + _dot(cf["a"].astype(BF16), v16))
                kc16 = (cf["kk"] * cf["e_c"]).astype(BF16)
                st_new.append(st_all[j] * cf["e_b"][CHUNK - 1:CHUNK, :] + _dot_tn(v16, kc16))
            o_ref[pl.ds(r0, CHUNK), :] = jnp.concatenate(o_all, axis=1)
            for j in range(hps):
                s_ref[j, c] = st_all[j]
                st_ref[j] = st_new[j]
            return carry

        lax.fori_loop(0, cpb, chunk, 0)

    head_block = lambda seg: pl.BlockSpec((1, rb_rows, width), lambda h, r: (seg, r, h))
    return pl.pallas_call(
        body, name="hgrn_forward",
        grid=(N_HEADS // hps, n_chunks // cpb),
        in_specs=[
            head_block(0), head_block(1), head_block(2),
            pl.BlockSpec((2, width), lambda h, r: (0, h)),
            pl.BlockSpec((N_EXP * CHUNK, 3 * CHUNK), lambda h, r: (0, 0)),
            pl.BlockSpec((1 + len(LEVELS), CHUNK, CHUNK), lambda h, r: (0, 0, 0)),
        ],
        out_specs=[
            pl.BlockSpec((rb_rows, width), lambda h, r: (r, h)),
            pl.BlockSpec((hps, cpb, HEAD_DIM, HEAD_DIM), lambda h, r: (h, r, 0, 0)),
        ],
        out_shape=[
            jax.ShapeDtypeStruct((rows, D_MODEL), F32),
            jax.ShapeDtypeStruct((N_HEADS, n_chunks, HEAD_DIM, HEAD_DIM), F32),
        ],
        scratch_shapes=[pltpu.VMEM((hps, HEAD_DIM, HEAD_DIM), F32)],
        compiler_params=_params(("arbitrary", "arbitrary")),
    )(p3, p3, p3, lb_logits, wexp3, masks)


def _hgrn_backward(p3, d_o, states, e16, a2, lb_logits, wexp_t, masks2, dw16, blob16, rows):
    n_chunks = rows // CHUNK
    cpb = _tile(n_chunks, 13, mult=1)
    rb_rows = cpb * CHUNK
    n_rb = n_chunks // cpb
    lanes = cpb * HEAD_DIM
    exchange = _GradExchange(SEGS_MIX, with_blob=True)

    def body(q_ref, fz_ref, v_ref, do_ref, s_ref, e_ref, a2_ref, lbl_ref, wexpt_ref, mask_ref, dw_ref, blob_ref,
             dp_ref, dlb_ref, rxw_ref, rxb_ref,
             dst_ref, g_ref, dsn_ref, q_s, kk_s, v_s, do_s, dq_s, dkk_s, dg_s, dx_s, send_sems, recv_sems):
        step = pl.program_id(1)
        rb = n_rb - 1 - step

        @pl.when((pl.program_id(0) == 0) & (step == 0))
        def _():
            exchange.start(dw_ref, rxw_ref, blob_ref, rxb_ref, send_sems, recv_sems)

        @pl.when(step == 0)
        def _():
            dst_ref[...] = jnp.zeros_like(dst_ref)
            dlb_ref[...] = jnp.zeros_like(dlb_ref)

        lb = _lower_bound(lbl_ref[...])
        row = rb * rb_rows + lax.broadcasted_iota(jnp.int32, (rb_rows, 1), 0)
        valid = row >= PAD_ROWS
        sg, sn = _sigmoid_pair(fz_ref[0])
        f = lb + (1.0 - lb) * sg
        g = jnp.where(valid, jnp.log(f), 0.0)
        kk_s[...] = jnp.where(valid, (1.0 - lb) * sn, 0.0)
        q_s[...] = jnp.where(valid, q_ref[0], 0.0)
        v_s[...] = jnp.where(valid, v_ref[0], 0.0).astype(BF16)
        do_s[...] = do_ref[...].astype(BF16)
        e_last_all = jnp.exp(jnp.concatenate(
            [jnp.sum(g[b * CHUNK:(b + 1) * CHUNK], axis=0, keepdims=True) for b in range(cpb)], axis=0))
        last_row = lax.broadcasted_iota(jnp.int32, (CHUNK, 1), 0) == CHUNK - 1
        zeros16 = jnp.zeros((CHUNK, HEAD_DIM), BF16)

        def factor(block, l0):
            return e_ref[0, 0, block * CHUNK:(block + 1) * CHUNK, pl.ds(l0, HEAD_DIM)].astype(F32)

        def contribution(b, carry):
            r0 = pl.multiple_of(b * CHUNK, CHUNK)
            l0 = pl.multiple_of(b * HEAD_DIM, HEAD_DIM)
            qb16 = (q_s[pl.ds(r0, CHUNK), :] * factor(0, l0)).astype(BF16)
            g_ref[b] = _dot_tn(do_s[pl.ds(r0, CHUNK), :], qb16)
            return carry

        lax.fori_loop(0, cpb, contribution, 0, unroll=LOCAL_UNROLL)

        cur = dst_ref[...]
        for b in reversed(range(cpb)):
            dsn_ref[b] = cur
            cur = cur * e_last_all[b:b + 1, :] + g_ref[b]
        dst_ref[...] = cur

        def local(b, carry):
            r0 = pl.multiple_of(b * CHUNK, CHUNK)
            l0 = pl.multiple_of(b * HEAD_DIM, HEAD_DIM)
            q = q_s[pl.ds(r0, CHUNK), :]
            kk = kk_s[pl.ds(r0, CHUNK), :]
            v16 = v_s[pl.ds(r0, CHUNK), :]
            do16 = do_s[pl.ds(r0, CHUNK), :]
            st = s_ref[0, b]
            dsn = dsn_ref[b]
            dsn16 = dsn.astype(BF16)
            e_b, e_c = factor(0, l0), factor(1, l0)
            qb, kc = q * e_b, kk * e_c

            t = _dot_tn(a2_ref[pl.ds(r0, CHUNK), :], do16)
            dv = t[0:CHUNK] + t[CHUNK:2 * CHUNK] + _dot_nt(kc.astype(BF16), dsn16)
            dp_ref[2, pl.ds(r0, CHUNK), :] = dv.astype(BF16)
            da2 = _dot_nt(do16, jnp.concatenate([v16, v16], axis=0))
            dqb = _dot(do16, st.astype(BF16))
            dkc = _dot(v16, dsn16)
            de = jnp.sum(dsn * st, axis=0, keepdims=True) * e_b[CHUNK - 1:CHUNK, :]
            dq = e_b * dqb
            dkk = e_c * dkc
            dx_s[0:CHUNK, pl.ds(l0, HEAD_DIM)] = (qb * dqb + jnp.where(last_row, de, 0.0)).astype(BF16)
            dx_s[CHUNK:2 * CHUNK, pl.ds(l0, HEAD_DIM)] = (kc * dkc).astype(BF16)

            def scaled(entry):
                if entry == 0:
                    return q, kk, None
                e_m = factor(1 + entry, l0)
                return q * e_m, kk * e_m, e_m

            for p, (ea, eb) in enumerate(LEVEL_PAIRS):
                dm = (mask_ref[p] * da2).astype(BF16)
                qa, ka, e_a = scaled(ea)
                if eb is None:
                    lhs_q = jnp.concatenate([qa.astype(BF16), zeros16], axis=1)
                    rhs_k = jnp.concatenate([jnp.concatenate([ka.astype(BF16), zeros16], axis=1),
                                             jnp.concatenate([zeros16, zeros16], axis=1)], axis=0)
                else:
                    qb_, kb_, e_bb = scaled(eb)
                    lhs_q = jnp.concatenate([qa.astype(BF16), qb_.astype(BF16)], axis=1)
                    rhs_k = jnp.concatenate([jnp.concatenate([ka.astype(BF16), zeros16], axis=1),
                                             jnp.concatenate([zeros16, kb_.astype(BF16)], axis=1)], axis=0)
                dq2 = _dot(dm, rhs_k)
                dk2 = _dot_tn(dm, lhs_q)
                parts = [(ea, qa, ka, e_a, dq2[:, :HEAD_DIM], dk2[0:CHUNK, :HEAD_DIM])]
                if eb is not None:
                    parts.append((eb, qb_, kb_, e_bb, dq2[:, HEAD_DIM:], dk2[CHUNK:2 * CHUNK, HEAD_DIM:]))
                for entry, q_m, k_m, e_m, dq_m, dk_m in parts:
                    if entry == 0:
                        dq = dq + dq_m
                        dkk = dkk + dk_m
                    else:
                        dq = dq + e_m * dq_m
                        dkk = dkk + e_m * dk_m
                        dx_s[(1 + entry) * CHUNK:(2 + entry) * CHUNK, pl.ds(l0, HEAD_DIM)] = (
                            q_m * dq_m + k_m * dk_m).astype(BF16)
            dq_s[pl.ds(r0, CHUNK), :] = dq
            dkk_s[pl.ds(r0, CHUNK), :] = dkk
            return carry

        lax.fori_loop(0, cpb, local, 0, unroll=BACKWARD_UNROLL)

        dg_all = _dot(wexpt_ref[...], dx_s[...])
        for b in range(cpb):
            dg_s[b * CHUNK:(b + 1) * CHUNK, :] = dg_all[:, b * HEAD_DIM:(b + 1) * HEAD_DIM]
        t = jnp.where(valid, dg_s[...] / f - dkk_s[...], 0.0)
        dlb_ref[...] += jnp.sum(sn * t, axis=0, keepdims=True)
        dp_ref[0] = jnp.where(valid, dq_s[...], 0.0).astype(BF16)
        dp_ref[1] = ((1.0 - lb) * sg * sn * t).astype(BF16)

        @pl.when((pl.program_id(0) == N_HEADS - 1) & (step == n_rb - 1))
        def _():
            exchange.wait(dw_ref, rxw_ref, blob_ref, rxb_ref, send_sems, recv_sems)

    head_block = lambda seg: pl.BlockSpec((1, rb_rows, HEAD_DIM), lambda h, s: (seg, n_rb - 1 - s, h))
    row_block = pl.BlockSpec((rb_rows, HEAD_DIM), lambda h, s: (n_rb - 1 - s, h))
    return pl.pallas_call(
        body, name="hgrn_backward",
        grid=(N_HEADS, n_rb),
        in_specs=[
            head_block(0), head_block(1), head_block(2),
            row_block,
            pl.BlockSpec((1, cpb, HEAD_DIM, HEAD_DIM), lambda h, s: (h, n_rb - 1 - s, 0, 0)),
            pl.BlockSpec((1, 1, N_EXP * CHUNK, lanes), lambda h, s: (h, n_rb - 1 - s, 0, 0)),
            row_block,
            pl.BlockSpec((2, HEAD_DIM), lambda h, s: (0, h)),
            pl.BlockSpec((CHUNK, N_EXP * CHUNK), lambda h, s: (0, 0)),
            pl.BlockSpec((len(LEVEL_PAIRS), CHUNK, 2 * CHUNK), lambda h, s: (0, 0, 0)),
            ANY, ANY,
        ],
        out_specs=[
            pl.BlockSpec((3, rb_rows, HEAD_DIM), lambda h, s: (0, n_rb - 1 - s, h)),
            pl.BlockSpec((1, HEAD_DIM), lambda h, s: (0, h)),
            ANY, ANY,
        ],
        out_shape=[
            jax.ShapeDtypeStruct((3, rows, D_MODEL), BF16),
            jax.ShapeDtypeStruct((1, D_MODEL), F32),
            exchange.landing_w(), exchange.landing_blob(blob16),
        ],
        scratch_shapes=[
            pltpu.VMEM((HEAD_DIM, HEAD_DIM), F32),
            pltpu.VMEM((cpb, HEAD_DIM, HEAD_DIM), F32),
            pltpu.VMEM((cpb, HEAD_DIM, HEAD_DIM), F32),
            pltpu.VMEM((rb_rows, HEAD_DIM), F32),
            pltpu.VMEM((rb_rows, HEAD_DIM), F32),
            pltpu.VMEM((rb_rows, HEAD_DIM), BF16),
            pltpu.VMEM((rb_rows, HEAD_DIM), BF16),
            pltpu.VMEM((rb_rows, HEAD_DIM), F32),
            pltpu.VMEM((rb_rows, HEAD_DIM), F32),
            pltpu.VMEM((rb_rows, HEAD_DIM), F32),
            pltpu.VMEM((N_EXP * CHUNK, lanes), BF16),
        ] + exchange.semaphores(),
        compiler_params=_params(("arbitrary", "arbitrary")),
    )(p3, p3, p3, d_o, states, e16, a2, lb_logits, wexp_t, masks2, dw16, blob16)


def _hgrn_backward_old(p3, d_o, states, lb_logits, wexp3, wexp_t2, masks, dw16, blob16, rows):
    n_chunks = rows // CHUNK
    cpb = _tile(n_chunks, 13, mult=1)
    rb_rows = cpb * CHUNK
    n_rb = n_chunks // cpb
    hps = HEADS_PER_STEP
    width = hps * HEAD_DIM
    n_hb = N_HEADS // hps
    exchange = _GradExchange(SEGS_MIX, with_blob=True)

    def body(q_ref, fz_ref, v_ref, do_ref, s_ref, lbl_ref, wexp_ref, wexpt_ref, mask_ref, dw_ref, blob_ref,
             dp_ref, dlb_ref, rxw_ref, rxb_ref, dst_ref, send_sems, recv_sems):
        step = pl.program_id(1)
        rb = n_rb - 1 - step

        @pl.when((pl.program_id(0) == 0) & (step == 0))
        def _():
            exchange.start(dw_ref, rxw_ref, blob_ref, rxb_ref, send_sems, recv_sems)

        @pl.when(step == 0)
        def _():
            dst_ref[...] = jnp.zeros_like(dst_ref)
            dlb_ref[...] = jnp.zeros_like(dlb_ref)

        lb_all = _lower_bound(lbl_ref[...])
        wexp = wexp_ref[...]
        wexp_t = wexpt_ref[...]
        masks = mask_ref[...]
        last_row = lax.broadcasted_iota(jnp.int32, (CHUNK, 1), 0) == CHUNK - 1

        def one_head(j, c, r0, valid):
            cols = slice(j * HEAD_DIM, (j + 1) * HEAD_DIM)
            lb = lb_all[:, cols]
            cf = _chunk_forward(q_ref[0, pl.ds(r0, CHUNK), cols], fz_ref[0, pl.ds(r0, CHUNK), cols],
                                lb, valid, wexp, masks)
            q, kk, e_b, e_c = cf["q"], cf["kk"], cf["e_b"], cf["e_c"]
            v16 = jnp.where(valid, v_ref[0, pl.ds(r0, CHUNK), cols], 0.0).astype(BF16)
            do16 = do_ref[pl.ds(r0, CHUNK), cols].astype(BF16)
            st = s_ref[j, c]
            dst = dst_ref[j]
            dst16 = dst.astype(BF16)
            qb = q * e_b
            kc = kk * e_c
            q16, kk16 = q.astype(BF16), kk.astype(BF16)

            dv = _dot_tn(cf["a"].astype(BF16), do16) + _dot_nt(kc.astype(BF16), dst16)
            da = _dot_nt(do16, v16)
            dqb = _dot(do16, st.astype(BF16))
            dkc = _dot(v16, dst16)
            e_last = e_b[CHUNK - 1:CHUNK, :]
            de = jnp.sum(dst * st, axis=0, keepdims=True)
            dst_ref[j] = dst * e_last + _dot_tn(do16, qb.astype(BF16))

            dq = e_b * dqb
            dkk = e_c * dkc
            dx = [qb * dqb + jnp.where(last_row, de * e_last, 0.0), kc * dkc]
            dm0 = (masks[0] * da).astype(BF16)
            dq = dq + _dot(dm0, kk16)
            dkk = dkk + _dot(dm0, q16)
            for l in range(len(LEVELS)):
                e_m = cf["e"][(2 + l) * CHUNK:(3 + l) * CHUNK]
                dm = (masks[1 + l] * da).astype(BF16)
                dqm = _dot(dm, cf["km"][l].astype(BF16))
                dkm = _dot_tn(dm, cf["qm"][l].astype(BF16))
                dq = dq + e_m * dqm
                dkk = dkk + e_m * dkm
                dx.append(cf["qm"][l] * dqm + cf["km"][l] * dkm)
            dxa = jnp.concatenate(dx, axis=0)
            hi = dxa.astype(BF16)
            mid = (dxa - hi.astype(F32)).astype(BF16)
            dg = _dot(wexp_t, jnp.concatenate([hi, mid], axis=0))

            t = jnp.where(valid, dg / cf["f"] - dkk, 0.0)
            dfz = (1.0 - lb) * cf["sg"] * cf["sn"] * t
            dlb_ref[:, cols] += jnp.sum(cf["sn"] * t, axis=0, keepdims=True)
            dp_ref[0, pl.ds(r0, CHUNK), cols] = jnp.where(valid, dq, 0.0).astype(BF16)
            dp_ref[1, pl.ds(r0, CHUNK), cols] = dfz.astype(BF16)
            dp_ref[2, pl.ds(r0, CHUNK), cols] = jnp.where(valid, dv, 0.0).astype(BF16)

        def chunk(i, carry):
            c = cpb - 1 - i
            r0 = pl.multiple_of(c * CHUNK, CHUNK)
            row = rb * rb_rows + r0 + lax.broadcasted_iota(jnp.int32, (CHUNK, 1), 0)
            for j in range(hps):
                one_head(j, c, r0, row >= PAD_ROWS)
            return carry

        lax.fori_loop(0, cpb, chunk, 0)

        @pl.when((pl.program_id(0) == n_hb - 1) & (step == n_rb - 1))
        def _():
            exchange.wait(dw_ref, rxw_ref, blob_ref, rxb_ref, send_sems, recv_sems)

    head_block = lambda seg: pl.BlockSpec((1, rb_rows, width), lambda h, s: (seg, n_rb - 1 - s, h))
    return pl.pallas_call(
        body, name="hgrn_backward",
        grid=(n_hb, n_rb),
        in_specs=[
            head_block(0), head_block(1), head_block(2),
            pl.BlockSpec((rb_rows, width), lambda h, s: (n_rb - 1 - s, h)),
            pl.BlockSpec((hps, cpb, HEAD_DIM, HEAD_DIM), lambda h, s: (h, n_rb - 1 - s, 0, 0)),
            pl.BlockSpec((2, width), lambda h, s: (0, h)),
            pl.BlockSpec((N_EXP * CHUNK, 3 * CHUNK), lambda h, s: (0, 0)),
            pl.BlockSpec((CHUNK, 2 * N_EXP * CHUNK), lambda h, s: (0, 0)),
            pl.BlockSpec((1 + len(LEVELS), CHUNK, CHUNK), lambda h, s: (0, 0, 0)),
            ANY, ANY,
        ],
        out_specs=[
            pl.BlockSpec((3, rb_rows, width), lambda h, s: (0, n_rb - 1 - s, h)),
            pl.BlockSpec((1, width), lambda h, s: (0, h)),
            ANY, ANY,
        ],
        out_shape=[
            jax.ShapeDtypeStruct((3, rows, D_MODEL), BF16),
            jax.ShapeDtypeStruct((1, D_MODEL), F32),
            exchange.landing_w(), exchange.landing_blob(blob16),
        ],
        scratch_shapes=[pltpu.VMEM((hps, HEAD_DIM, HEAD_DIM), F32)] + exchange.semaphores(),
        compiler_params=_params(("arbitrary", "arbitrary")),
    )(p3, p3, p3, d_o, states, lb_logits, wexp3, wexp_t2, masks, dw16, blob16)


def _sigmoid(x):
    return 1.0 / (1.0 + jnp.exp(-x))


def _silu_and_grad(x):
    s = _sigmoid(x)
    return x * s, s * (1.0 + x * (1.0 - s))


def _window_sum(ext, width, forward_looking):
    n = ext.shape[0]
    s = ext
    step = 1
    while step < width:
        s = s + pltpu.roll(s, (n - step) if forward_looking else step, 0)
        step *= 2
    return s


def _mixers(o, p3, z, tgt, wdh, wdp, wout, poolw, hg_w, pool_scale, final_w, rows):
    tm = _tile(rows, 208)
    nt = rows // tm
    halo_blocks = tm // HALO
    n_grp = len(POOL_WINDOWS)

    def body(o_ref, ghg_ref, u_ref, gpl_ref, mhg_ref, mpl_ref, uh_ref, z_ref, t_ref,
             wdh_ref, wdp_ref, wout_ref, pw_ref, hgw_ref, ps_ref, fw_ref,
             do_ref, dz2_ref, dp_ref, dwdh_ref, dwdp_ref, dwout_ref, dpw_ref, small_ref, carry_ref):
        step = pl.program_id(0)
        tile = nt - 1 - step

        @pl.when(step == 0)
        def _():
            dwdh_ref[...] = jnp.zeros_like(dwdh_ref)
            dwdp_ref[...] = jnp.zeros_like(dwdp_ref)
            dwout_ref[...] = jnp.zeros_like(dwout_ref)
            dpw_ref[...] = jnp.zeros_like(dpw_ref)
            small_ref[...] = jnp.zeros_like(small_ref)
            carry_ref[...] = jnp.zeros_like(carry_ref)

        row = tile * tm + lax.broadcasted_iota(jnp.int32, (tm, 1), 0)
        real = row >= PAD_ROWS
        pos1 = jnp.maximum(row - PAD_ROWS + 1, 1).astype(F32)

        u = jnp.where(real, u_ref[0], 0.0)
        halo_row = tile * tm - HALO + lax.broadcasted_iota(jnp.int32, (HALO, 1), 0)
        uh = jnp.where(halo_row >= PAD_ROWS, uh_ref[0], 0.0)
        ext = jnp.concatenate([uh, u], axis=0)
        pooled, inv_cnt, mixed = [], [], []
        for g, w in enumerate(POOL_WINDOWS):
            cols = slice(g * POOL_GDIM, (g + 1) * POOL_GDIM)
            inv = 1.0 / jnp.minimum(pos1, float(w))
            ws = _window_sum(ext[:, cols], w, False)[HALO:]
            pg = (ws * inv - u[:, cols]).astype(BF16)
            pooled.append(pg)
            inv_cnt.append(inv)
            mixed.append(_dot(pg, pw_ref[g]))
        mixed = jnp.concatenate(mixed, axis=1)
        gpl = gpl_ref[0]
        sp, dsp = _silu_and_grad(gpl)
        ps = ps_ref[...]
        a_pool = (mixed * ps * sp).astype(BF16)
        y_pool = _dot(a_pool, wdp_ref[...])

        o = o_ref[...]
        o_hat, rstd_h = [], []
        for h in range(N_HEADS):
            oh = o[:, h * HEAD_DIM:(h + 1) * HEAD_DIM]
            r = lax.rsqrt(jnp.mean(oh * oh, axis=-1, keepdims=True) + EPS)
            rstd_h.append(r)
            o_hat.append(oh * r)
        o_hat = jnp.concatenate(o_hat, axis=1)
        hgw = hgw_ref[...]
        o_n = o_hat * hgw
        ghg = ghg_ref[0]
        sh, dsh = _silu_and_grad(ghg)
        a_hg = (o_n * sh).astype(BF16)
        y_hg = _dot(a_hg, wdh_ref[...])

        s_mh = _sigmoid(mhg_ref[0])
        s_mp = _sigmoid(mpl_ref[0])
        merged = (s_mh * y_hg + s_mp * y_pool).astype(BF16)
        z2 = z_ref[...] + _dot(merged, wout_ref[...])
        rstd2 = lax.rsqrt(jnp.mean(z2 * z2, axis=-1, keepdims=True) + EPS)
        zh = z2 * rstd2
        fw = fw_ref[...]
        err = jnp.where(row >= FIRST_TOKEN_ROW, zh * fw - t_ref[...], 0.0)
        small_ref[ROW_LOSS:ROW_LOSS + 1, :] += jnp.sum(err * err, axis=0, keepdims=True) * (0.5 / D_MODEL)
        dy = err * (1.0 / D_MODEL)

        small_ref[ROW_FINAL_W:ROW_FINAL_W + 1, :] += jnp.sum(dy * zh, axis=0, keepdims=True)
        uu = dy * fw
        dz2 = rstd2 * (uu - zh * jnp.mean(uu * zh, axis=-1, keepdims=True))
        dz2_ref[...] = dz2
        dz2_16 = dz2.astype(BF16)
        dmerged = _dot_nt(dz2_16, wout_ref[...])
        dwout_ref[...] += _dot_tn(merged, dz2_16)
        dy_hg = (s_mh * dmerged).astype(BF16)
        dy_pool = (s_mp * dmerged).astype(BF16)
        dp_ref[3] = (dmerged * y_hg * s_mh * (1.0 - s_mh)).astype(BF16)
        dp_ref[4] = (dmerged * y_pool * s_mp * (1.0 - s_mp)).astype(BF16)

        da_hg = _dot_nt(dy_hg, wdh_ref[...])
        dwdh_ref[...] += _dot_tn(a_hg, dy_hg)
        dp_ref[0] = (da_hg * o_n * dsh).astype(BF16)
        do_n = da_hg * sh
        small_ref[ROW_HG_W:ROW_HG_W + 1, :] += jnp.sum(do_n * o_hat, axis=0, keepdims=True)
        d_hat = do_n * hgw
        for h in range(N_HEADS):
            cols = slice(h * HEAD_DIM, (h + 1) * HEAD_DIM)
            dh_, oh_ = d_hat[:, cols], o_hat[:, cols]
            do_ref[:, cols] = rstd_h[h] * (dh_ - oh_ * jnp.mean(dh_ * oh_, axis=-1, keepdims=True))

        da_pool = _dot_nt(dy_pool, wdp_ref[...])
        dwdp_ref[...] += _dot_tn(a_pool, dy_pool)
        small_ref[ROW_POOL_SCALE:ROW_POOL_SCALE + 1, :] += jnp.sum(da_pool * mixed * sp, axis=0, keepdims=True)
        dp_ref[2] = (da_pool * mixed * ps * dsp).astype(BF16)
        dmixed = (da_pool * ps * sp).astype(BF16)
        carry = carry_ref[...]
        du, new_carry = [], []
        for g, w in enumerate(POOL_WINDOWS):
            cols = slice(g * POOL_GDIM, (g + 1) * POOL_GDIM)
            dmg = dmixed[:, cols]
            dpooled = _dot_nt(dmg, pw_ref[g])
            dpw_ref[g] += _dot_tn(pooled[g], dmg)
            dps = dpooled * inv_cnt[g]
            ext_b = jnp.concatenate([dps, carry[:, cols]], axis=0)
            du.append(_window_sum(ext_b, w, True)[:tm] - dpooled)
            new_carry.append(dps[:HALO])
        dp_ref[1] = jnp.where(real, jnp.concatenate(du, axis=1), 0.0).astype(BF16)
        carry_ref[...] = jnp.concatenate(new_carry, axis=1)

    row_block = pl.BlockSpec((tm, D_MODEL), lambda s: (nt - 1 - s, 0))
    seg_block = lambda seg: pl.BlockSpec((1, tm, D_MODEL), lambda s: (seg, nt - 1 - s, 0))
    whole = pl.BlockSpec(memory_space=pltpu.VMEM)
    return pl.pallas_call(
        body, name="mixers",
        grid=(nt,),
        in_specs=[
            row_block, seg_block(3), seg_block(4), seg_block(5), seg_block(6), seg_block(7),
            pl.BlockSpec((1, HALO, D_MODEL),
                         lambda s: (4, jnp.maximum((nt - 1 - s) * halo_blocks - 1, 0), 0)),
            row_block, row_block,
            whole, whole, whole, whole, whole, whole, whole,
        ],
        out_specs=[
            row_block, row_block,
            pl.BlockSpec((5, tm, D_MODEL), lambda s: (0, nt - 1 - s, 0)),
            whole, whole, whole, whole, whole,
        ],
        out_shape=[
            jax.ShapeDtypeStruct((rows, D_MODEL), F32),
            jax.ShapeDtypeStruct((rows, D_MODEL), F32),
            jax.ShapeDtypeStruct((5, rows, D_MODEL), BF16),
            jax.ShapeDtypeStruct((D_MODEL, D_MODEL), F32),
            jax.ShapeDtypeStruct((D_MODEL, D_MODEL), F32),
            jax.ShapeDtypeStruct((D_MODEL, D_MODEL), F32),
            jax.ShapeDtypeStruct((n_grp, POOL_GDIM, POOL_GDIM), F32),
            jax.ShapeDtypeStruct((SMALL_ROWS, D_MODEL), F32),
        ],
        scratch_shapes=[pltpu.VMEM((HALO, D_MODEL), F32)],
        compiler_params=_params(("arbitrary",)),
    )(o, p3, p3, p3, p3, p3, p3, z, tgt, wdh, wdp, wout, poolw, hg_w, pool_scale, final_w)


def _seg_specs(tm, row_of, seg_of):
    def spec_a(*g):
        k = seg_of(*g)
        return (jnp.minimum(k, 2), jnp.where(k < 3, row_of(*g), 0), 0)

    def spec_b(*g):
        k = seg_of(*g)
        return (jnp.maximum(k - 3, 0), jnp.where(k >= 3, row_of(*g), 0), 0)

    return pl.BlockSpec((1, tm, D_MODEL), spec_a), pl.BlockSpec((1, tm, D_MODEL), spec_b)


def _in_proj_weight_grad(h, dp, rows, name):
    n_seg = dp.shape[0]
    tm = _tile(rows, 1040)
    nt = rows // tm
    half = D_MODEL // 2

    def body(h_ref, dp_ref, part_ref, part16_ref, db_ref, acc_ref, bacc_ref, stage_ref, land_ref,
             send_sems, recv_sems):
        k, i = pl.program_id(0), pl.program_id(1)
        x, y, c = lax.axis_index("x"), lax.axis_index("y"), lax.axis_index("c")

        def to_sibling(seg):
            return pltpu.make_async_remote_copy(
                src_ref=stage_ref.at[seg], dst_ref=land_ref.at[seg], send_sem=send_sems.at[seg],
                recv_sem=recv_sems.at[seg], device_id=(x, y, 1 - c), device_id_type=MESH)

        @pl.when(i == 0)
        def _():
            acc_ref[...] = jnp.zeros_like(acc_ref)
            bacc_ref[...] = jnp.zeros_like(bacc_ref)

        dpt = dp_ref[0]
        acc_ref[...] += _dot_tn(h_ref[...], dpt)
        bacc_ref[...] += jnp.sum(dpt.astype(F32), axis=0, keepdims=True)

        @pl.when(i == nt - 1)
        def _():
            db_ref[0] = bacc_ref[...]
            part_ref[k] = acc_ref[pl.ds(pl.multiple_of(c * half, half), half), :]
            stage_ref[k] = acc_ref[pl.ds(pl.multiple_of((1 - c) * half, half), half), :].astype(BF16)
            to_sibling(k).start()

        @pl.when((k == n_seg - 1) & (i == nt - 1))
        def _():
            for seg in range(n_seg):
                to_sibling(seg).wait_recv()
                total = part_ref[seg] + land_ref[seg].astype(F32)
                part_ref[seg] = total
                part16_ref[seg] = total.astype(BF16)
            for seg in range(n_seg):
                to_sibling(seg).wait_send()

    whole = pl.BlockSpec(memory_space=pltpu.VMEM)
    return pl.pallas_call(
        body, name=name,
        grid=(n_seg, nt),
        in_specs=[pl.BlockSpec((tm, D_MODEL), lambda k, i: (i, 0)),
                  pl.BlockSpec((1, tm, D_MODEL), lambda k, i: (k, i, 0))],
        out_specs=[whole, whole, pl.BlockSpec((1, 1, D_MODEL), lambda k, i: (k, 0, 0))],
        out_shape=[
            jax.ShapeDtypeStruct((n_seg, half, D_MODEL), F32),
            jax.ShapeDtypeStruct((n_seg, half, D_MODEL), BF16),
            jax.ShapeDtypeStruct((n_seg, 1, D_MODEL), F32),
        ],
        scratch_shapes=[
            pltpu.VMEM((D_MODEL, D_MODEL), F32), pltpu.VMEM((1, D_MODEL), F32),
            pltpu.VMEM((n_seg, half, D_MODEL), BF16),
            pltpu.VMEM((n_seg, half, D_MODEL), BF16),
            pltpu.SemaphoreType.DMA((n_seg,)), pltpu.SemaphoreType.DMA((n_seg,)),
        ],
        compiler_params=_params(("arbitrary", "arbitrary")),
    )(h, dp)


def _input_grad(dpa, dpb, w4, z, dz2, norm_w, dw16, rows):
    tm = _tile(rows, 1040)
    nt = rows // tm
    exchange = _GradExchange(SEGS_REC, with_blob=False)

    def body(dpa_ref, dpb_ref, w_ref, z_ref, dz2_ref, nw_ref, dw_ref, dz_ref, dnw_ref, rxw_ref,
             acc_ref, send_sems, recv_sems):
        i, k = pl.program_id(0), pl.program_id(1)

        @pl.when((i == 0) & (k == 0))
        def _():
            exchange.start(dw_ref, rxw_ref, None, None, send_sems, recv_sems)
            dnw_ref[...] = jnp.zeros_like(dnw_ref)

        @pl.when((i == nt - 1) & (k == N_SEG - 1))
        def _():
            exchange.wait(dw_ref, rxw_ref, None, None, send_sems, recv_sems)

        @pl.when(k == 0)
        def _():
            acc_ref[...] = jnp.zeros_like(acc_ref)

        @pl.when(k < 3)
        def _():
            acc_ref[...] += _dot_nt(dpa_ref[0], w_ref[0])

        @pl.when(k >= 3)
        def _():
            acc_ref[...] += _dot_nt(dpb_ref[0], w_ref[0])

        @pl.when(k == N_SEG - 1)
        def _():
            zt = z_ref[...]
            rstd = lax.rsqrt(jnp.mean(zt * zt, axis=-1, keepdims=True) + EPS)
            zh = zt * rstd
            dh = acc_ref[...]
            dnw_ref[...] += jnp.sum(dh * zh, axis=0, keepdims=True)
            uu = dh * nw_ref[...]
            dz_ref[...] = dz2_ref[...] + rstd * (uu - zh * jnp.mean(uu * zh, axis=-1, keepdims=True))

    spec_a, spec_b = _seg_specs(tm, lambda i, k: i, lambda i, k: k)
    last_only = pl.BlockSpec((tm, D_MODEL), lambda i, k: (jnp.where(k == N_SEG - 1, i, 0), 0))
    return pl.pallas_call(
        body, name="input_grad",
        grid=(nt, N_SEG),
        in_specs=[
            spec_a, spec_b,
            pl.BlockSpec((1, D_MODEL, D_MODEL), lambda i, k: (k // 2, 0, k % 2)),
            last_only, last_only,
            pl.BlockSpec((1, D_MODEL), lambda i, k: (0, 0)),
            ANY,
        ],
        out_specs=[
            pl.BlockSpec((tm, D_MODEL), lambda i, k: (i, 0)),
            pl.BlockSpec((1, D_MODEL), lambda i, k: (0, 0)),
            ANY,
        ],
        out_shape=[
            jax.ShapeDtypeStruct((rows, D_MODEL), F32),
            jax.ShapeDtypeStruct((1, D_MODEL), F32),
            exchange.landing_w(),
        ],
        scratch_shapes=[pltpu.VMEM((tm, D_MODEL), F32)] + exchange.semaphores(),
        compiler_params=_params(("arbitrary", "arbitrary")),
    )(dpa, dpb, w4, z, dz2, norm_w, dw16)


def _local_step(z, tgt, w4, blob4, seg_order, norm_w, b_in, lb_logits, hg_w, pool_scale, final_w):
    rows = z.shape[0]
    q = D_MODEL // N_CHIPS
    n_grp = len(POOL_WINDOWS)
    pg = POOL_GDIM // N_CHIPS

    wexp2 = jnp.asarray(np.tile(_exponent_matrix(), (1, 2)), BF16)
    wexp_t = jnp.asarray(_exponent_matrix().T, BF16)
    masks2 = jnp.asarray(_paired_masks(), F32)

    h, p3, w4 = _in_proj(z, norm_w, w4, b_in, seg_order, rows)
    o, states, e16, a2, blob4 = _hgrn_forward(p3, lb_logits, wexp2, masks2, blob4, rows)
    wdh = blob4[:, 0:q].reshape(D_MODEL, D_MODEL)
    wdp = blob4[:, q:2 * q].reshape(D_MODEL, D_MODEL)
    wout = blob4[:, 2 * q:3 * q].reshape(D_MODEL, D_MODEL)
    poolw = blob4[:, 3 * q:].reshape(N_CHIPS, n_grp, pg, POOL_GDIM).transpose(1, 0, 2, 3)
    poolw = poolw.reshape(n_grp, POOL_GDIM, POOL_GDIM)
    d_o, dz2, dpb, dwdh, dwdp, dwout, dpw, small = _mixers(
        o, p3, z, tgt, wdh, wdp, wout, poolw, hg_w, pool_scale, final_w, rows)
    dpw4 = dpw.reshape(n_grp, N_CHIPS, pg, POOL_GDIM).transpose(1, 0, 2, 3)
    dpw4 = dpw4.reshape(N_CHIPS, n_grp * pg * POOL_GDIM // D_MODEL, D_MODEL)
    dblob4 = jnp.concatenate([dwdh.reshape(N_CHIPS, q, D_MODEL), dwdp.reshape(N_CHIPS, q, D_MODEL),
                              dwout.reshape(N_CHIPS, q, D_MODEL), dpw4], axis=1)

    dw_mix, dw_mix16, db_mix = _in_proj_weight_grad(h, dpb, rows, "in_proj_weight_grad_mix")
    dpa, dlb, rxw_mix, rx_blob = _hgrn_backward(
        p3, d_o, states, e16, a2, lb_logits, wexp_t, masks2, dw_mix16, dblob4.astype(BF16), rows)
    dw_rec, dw_rec16, db_rec = _in_proj_weight_grad(h, dpa, rows, "in_proj_weight_grad_rec")
    dz, dnw, rxw_rec = _input_grad(dpa, dpb, w4, z, dz2, norm_w, dw_rec16, rows)

    small = jnp.concatenate([
        small[ROW_LOSS:ROW_LOSS + 1],
        dz[PAD_ROWS:PAD_ROWS + N_META],
        dnw,
        db_rec.reshape(len(SEGS_REC), D_MODEL), db_mix.reshape(len(SEGS_MIX), D_MODEL),
        dlb, jnp.zeros_like(dlb),
        small[ROW_HG_W:ROW_HG_W + 1], small[ROW_POOL_SCALE:ROW_POOL_SCALE + 1],
        small[ROW_FINAL_W:ROW_FINAL_W + 1],
        jnp.zeros((SMALL_ROWS - ROW_FINAL_W - 1, D_MODEL), F32),
    ], axis=0)
    return dz, (dw_rec, dw_mix, rxw_rec, rxw_mix), (dblob4, rx_blob), small


ANY = pl.BlockSpec(memory_space=pl.ANY)
MESH = pl.DeviceIdType.MESH


def _place():
    x, y, c = lax.axis_index("x"), lax.axis_index("y"), lax.axis_index("c")
    chips = [(1 - x, y), (x, 1 - y), (1 - x, 1 - y)]
    return x, y, c, chips


class _ShardGather:
    def __init__(self, rows):
        self.half = rows // 2

    def semaphores(self):
        return [pltpu.SemaphoreType.DMA((6,)), pltpu.SemaphoreType.DMA((6,))]

    def _copy(self, k, slot, to, send_sems, recv_sems):
        return pltpu.make_async_remote_copy(src_ref=slot, dst_ref=slot, send_sem=send_sems.at[k],
                                            recv_sem=recv_sems.at[k], device_id=to, device_id_type=MESH)

    def _half(self, ref4, chip, which):
        return ref4.at[chip, pl.ds(which * self.half, self.half), :]

    def start(self, ref4, send_sems, recv_sems, which=(0, 1, 2)):
        x, y, c, chips = _place()
        for j in which:
            cx, cy = chips[j]
            self._copy(j, self._half(ref4, 2 * x + y, c), (cx, cy, c), send_sems, recv_sems).start()

    def start_diagonal_after_neighbours(self, ref4, send_sems, recv_sems):
        x, y, c, chips = _place()
        for j in (0, 1):
            cx, cy = chips[j]
            self._copy(j, self._half(ref4, 2 * x + y, c), (cx, cy, c), send_sems, recv_sems).wait_send()
        self.start(ref4, send_sems, recv_sems, which=(2,))

    def arrive(self, j, ref4, send_sems, recv_sems):
        x, y, c, chips = _place()
        cx, cy = chips[j]
        landed = self._half(ref4, 2 * cx + cy, c)
        self._copy(j, landed, (cx, cy, c), send_sems, recv_sems).wait_recv()
        self._copy(3 + j, landed, (x, y, 1 - c), send_sems, recv_sems).start()
        self._copy(3 + j, self._half(ref4, 2 * cx + cy, 1 - c), (x, y, 1 - c), send_sems, recv_sems).wait_recv()

    def pass_on_all(self, ref4, send_sems, recv_sems):
        x, y, c, chips = _place()
        for j, (cx, cy) in enumerate(chips):
            landed = self._half(ref4, 2 * cx + cy, c)
            self._copy(j, landed, (cx, cy, c), send_sems, recv_sems).wait_recv()
            self._copy(3 + j, landed, (x, y, 1 - c), send_sems, recv_sems).start()

    def await_sibling_all(self, ref4, send_sems, recv_sems):
        x, y, c, chips = _place()
        for j, (cx, cy) in enumerate(chips):
            self._copy(3 + j, self._half(ref4, 2 * cx + cy, 1 - c), (x, y, 1 - c), send_sems, recv_sems).wait_recv()

    def finish(self, ref4, send_sems, recv_sems, which=(0, 1, 2)):
        x, y, c, chips = _place()
        for j, (cx, cy) in enumerate(chips):
            if j in which:
                self._copy(j, self._half(ref4, 2 * x + y, c), (cx, cy, c), send_sems, recv_sems).wait_send()
            self._copy(3 + j, self._half(ref4, 2 * cx + cy, c), (x, y, 1 - c), send_sems, recv_sems).wait_send()


def _gather_meta(m4):
    def body(m_in_ref, m4_ref, send_sems, recv_sems):
        x, y, c, chips = _place()

        def copy(j, slot, to):
            return pltpu.make_async_remote_copy(src_ref=slot, dst_ref=slot, send_sem=send_sems.at[j],
                                                recv_sem=recv_sems.at[j], device_id=to, device_id_type=MESH)

        sends = [copy(j, m4_ref.at[2 * x + y], (cx, cy, c)) for j, (cx, cy) in enumerate(chips)]
        for cp in sends:
            cp.start()
        for j, (cx, cy) in enumerate(chips):
            copy(j, m4_ref.at[2 * cx + cy], (x, y, c)).wait_recv()
        for cp in sends:
            cp.wait_send()

    return pl.pallas_call(
        body, name="gather_meta",
        in_specs=[ANY], out_specs=ANY, out_shape=jax.ShapeDtypeStruct(m4.shape, m4.dtype),
        input_output_aliases={0: 0},
        scratch_shapes=[pltpu.SemaphoreType.DMA((3,)), pltpu.SemaphoreType.DMA((3,))],
    )(m4)


class _GradExchange:
    def __init__(self, segs, with_blob):
        self.segs = tuple(segs)
        self.with_blob = with_blob

    def landing_w(self):
        return jax.ShapeDtypeStruct((N_CHIPS, 2, D_MODEL // 2, D_MODEL), BF16)

    def landing_blob(self, blob16):
        return jax.ShapeDtypeStruct((N_DEV, blob16.shape[1] // 2, D_MODEL), BF16)

    def semaphores(self):
        n_send = len(self.segs) + (2 * N_CHIPS if self.with_blob else 0)
        n_recv = 2 * N_CHIPS + (N_DEV if self.with_blob else 0)
        return [pltpu.SemaphoreType.DMA((n_send,)), pltpu.SemaphoreType.DMA((n_recv,))]

    def _copies(self, dw_ref, rxw_ref, blob_ref, rxb_ref, send_sems, recv_sems):
        x, y, c = lax.axis_index("x"), lax.axis_index("y"), lax.axis_index("c")
        chip = 2 * x + y

        def relation(kx, ky, h):
            return (x ^ kx) * 4 + (y ^ ky) * 2 + (c ^ h)

        def copy(src, dst, send_k, recv_k, to):
            return functools.partial(pltpu.make_async_remote_copy, src_ref=src, dst_ref=dst,
                                     send_sem=send_sems.at[send_k], recv_sem=recv_sems.at[recv_k],
                                     device_id=to, device_id_type=MESH)

        sends, recvs = [], []
        for i, s in enumerate(self.segs):
            kx, ky = (s // 2) >> 1, (s // 2) & 1
            r = (x ^ kx) * 2 + (y ^ ky)
            sends.append((r != 0, copy(dw_ref.at[i], rxw_ref.at[r, s % 2], i, 2 * r + s % 2, (kx, ky, c))))
        for j in range(2):
            mine = [s // 2 for s in self.segs if s % 2 == j]
            if mine:
                cond = functools.reduce(lambda a, b: a | b, [chip == k for k in mine])
                for r in range(1, N_CHIPS):
                    slot = rxw_ref.at[r, j]
                    recvs.append((cond, copy(slot, slot, 0, 2 * r + j, (x, y, c))))
        if self.with_blob:
            hb = blob_ref.shape[1] // 2
            first_send, first_recv = len(self.segs), 2 * N_CHIPS
            for k in range(N_CHIPS):
                for h in range(2):
                    r = relation(k >> 1, k & 1, h)
                    sends.append((r != 0, copy(blob_ref.at[k, pl.ds(h * hb, hb), :], rxb_ref.at[r],
                                               first_send + 2 * k + h, first_recv + r, (k >> 1, k & 1, h))))
            for r in range(1, N_DEV):
                slot = rxb_ref.at[r]
                recvs.append((None, copy(slot, slot, 0, first_recv + r, (x, y, c))))
        return sends, recvs

    def start(self, *refs):
        sends, _ = self._copies(*refs)
        for cond, make in sends:
            pl.when(cond)(lambda make=make: make().start())

    def wait(self, *refs):
        sends, recvs = self._copies(*refs)
        for cond, make in sends:
            pl.when(cond)(lambda make=make: make().wait_send())
        for cond, make in recvs:
            if cond is None:
                make().wait_recv()
            else:
                pl.when(cond)(lambda make=make: make().wait_recv())


def _sum_landed(own, rx_ref):
    total = own
    for r in range(1, rx_ref.shape[0]):
        total = total + rx_ref[r, 0].astype(F32)
    return total


def _finish_w(dw_rec, dw_mix, rx_rec, rx_mix, place_arr):
    half = D_MODEL // 2
    tm = _tile(half, 256)
    n_rec = len(SEGS_REC)

    def body(place_ref, own_rec_ref, own_mix_ref, rx_rec_ref, rx_mix_ref, out_ref):
        seg = 2 * place_ref[0] + pl.program_id(0)

        @pl.when(seg < n_rec)
        def _():
            out_ref[0] = _sum_landed(own_rec_ref[0], rx_rec_ref)

        @pl.when(seg >= n_rec)
        def _():
            out_ref[0] = _sum_landed(own_mix_ref[0], rx_mix_ref)

    def own_spec(first, count):
        def index(j, i, place_ref):
            seg = 2 * place_ref[0] + j
            return (jnp.clip(seg - first, 0, count - 1), i, 0)
        return pl.BlockSpec((1, tm, D_MODEL), index)

    rx_spec = pl.BlockSpec((N_CHIPS, 1, tm, D_MODEL), lambda j, i, place_ref: (0, j, i, 0))
    return pl.pallas_call(
        body, name="finish_w",
        grid_spec=pltpu.PrefetchScalarGridSpec(
            num_scalar_prefetch=1, grid=(2, half // tm),
            in_specs=[own_spec(0, n_rec), own_spec(n_rec, len(SEGS_MIX)), rx_spec, rx_spec],
            out_specs=pl.BlockSpec((1, tm, D_MODEL), lambda j, i, place_ref: (place_ref[1], i, j))),
        out_shape=jax.ShapeDtypeStruct((2, half, 2 * D_MODEL), F32),
        compiler_params=_params(("arbitrary", "arbitrary")),
    )(place_arr, dw_rec, dw_mix, rx_rec, rx_mix)


def _finish_blob(dblob4, rx_blob, place_arr):
    n, rows, cols = rx_blob.shape
    tm = _tile(rows, 256)

    def body(place_ref, own_ref, rx_ref, out_ref):
        out_ref[0] = _sum_landed(own_ref[0, 0], rx_ref)

    return pl.pallas_call(
        body, name="finish_blob",
        grid_spec=pltpu.PrefetchScalarGridSpec(
            num_scalar_prefetch=1, grid=(rows // tm,),
            in_specs=[pl.BlockSpec((1, 1, tm, cols), lambda i, place_ref: (place_ref[0], place_ref[1], i, 0)),
                      pl.BlockSpec((n, 1, tm, cols), lambda i, place_ref: (0, 0, i, 0))],
            out_specs=pl.BlockSpec((1, tm, cols), lambda i, place_ref: (place_ref[1], i, 0))),
        out_shape=jax.ShapeDtypeStruct((2, rows, cols), F32),
        compiler_params=_params(("arbitrary",)),
    )(place_arr, dblob4.reshape(N_CHIPS, 2, rows, cols), rx_blob.reshape(n, 1, rows, cols))


def _share_finished(fw2, fb2, small):
    def body(w_in_ref, b_in_ref, small_ref, w_ref, b_ref, s_ref, bounce, local_sem, send_sems, recv_sems):
        x, y, c, _ = _place()
        sibling = (x, y, 1 - c)

        def copy(k, src, dst, to):
            return pltpu.make_async_remote_copy(src_ref=src, dst_ref=dst, send_sem=send_sems.at[k],
                                                recv_sem=recv_sems.at[k], device_id=to, device_id_type=MESH)

        sends = [copy(0, w_ref.at[c], w_ref.at[c], sibling), copy(1, b_ref.at[c], b_ref.at[c], sibling)]
        for r in range(1, N_DEV):
            peer = (x ^ ((r >> 2) & 1), y ^ ((r >> 1) & 1), c ^ (r & 1))
            sends.append(copy(1 + r, small_ref, s_ref.at[r], peer))
        for cp in sends:
            cp.start()
        for src, dst in ((small_ref, bounce), (bounce, s_ref.at[0])):
            own = pltpu.make_async_copy(src, dst, local_sem)
            own.start()
            own.wait()
        landed = [w_ref.at[1 - c], b_ref.at[1 - c]] + [s_ref.at[r] for r in range(1, N_DEV)]
        for k, slot in enumerate(landed):
            copy(k, slot, slot, (x, y, c)).wait_recv()
        for cp in sends:
            cp.wait_send()

    same = lambda a: jax.ShapeDtypeStruct(a.shape, a.dtype)
    n_sem = 2 + N_DEV - 1
    return pl.pallas_call(
        body, name="share_finished",
        in_specs=[ANY, ANY, ANY], out_specs=[ANY, ANY, ANY],
        out_shape=[same(fw2), same(fb2), jax.ShapeDtypeStruct((N_DEV,) + small.shape, F32)],
        input_output_aliases={0: 0, 1: 1},
        scratch_shapes=[pltpu.VMEM(small.shape, F32), pltpu.SemaphoreType.DMA,
                        pltpu.SemaphoreType.DMA((n_sem,)), pltpu.SemaphoreType.DMA((n_sem,))],
    )(fw2, fb2, small)


def _sum_small(slots, lb_logits, me_arr):
    def body(me_ref, slots_ref, lbl_ref, out_ref):
        me = me_ref[0]
        total = slots_ref[me]
        for d in range(1, N_DEV):
            total = total + slots_ref[d ^ me]
        out_ref[...] = total
        out_ref[ROW_LOSS:ROW_LOSS + 1, :] = jnp.broadcast_to(
            jnp.sum(total[ROW_LOSS:ROW_LOSS + 1, :], axis=-1, keepdims=True), (1, D_MODEL))
        lb = _lower_bound(lbl_ref[...])
        g0 = total[ROW_LB:ROW_LB + 1, :] * lb * (1.0 - lb)
        out_ref[ROW_LB:ROW_LB + 1, :] = g0
        out_ref[ROW_LB + 1:ROW_LB + 2, :] = -g0

    return pl.pallas_call(
        body, name="sum_small",
        grid_spec=pltpu.PrefetchScalarGridSpec(
            num_scalar_prefetch=1, grid=(1,),
            in_specs=[pl.BlockSpec((N_DEV, SMALL_ROWS, D_MODEL), lambda i, me_ref: (0, 0, 0)),
                      pl.BlockSpec((2, D_MODEL), lambda i, me_ref: (0, 0))],
            out_specs=pl.BlockSpec((SMALL_ROWS, D_MODEL), lambda i, me_ref: (0, 0))),
        out_shape=jax.ShapeDtypeStruct((SMALL_ROWS, D_MODEL), F32),
        compiler_params=_params(("arbitrary",)),
    )(me_arr, slots, lb_logits)


def _adamw(w, g, m, v):
    rows, cols = w.shape
    tm = _tile(rows, 256, mult=8) if rows % 8 == 0 else rows
    c1 = 1.0 / (1.0 - ADAM_B1 ** ADAM_STEP)
    c2 = 1.0 / (1.0 - ADAM_B2 ** ADAM_STEP)

    def body(w_ref, g_ref, m_ref, v_ref, d_ref, nm_ref, nv_ref):
        gt = g_ref[...]
        nm = ADAM_B1 * m_ref[...] + (1.0 - ADAM_B1) * gt
        nv = ADAM_B2 * v_ref[...] + (1.0 - ADAM_B2) * (gt * gt)
        nm_ref[...] = nm
        nv_ref[...] = nv
        d_ref[...] = -ADAM_LR * ((nm * c1) / (jnp.sqrt(nv * c2) + ADAM_EPS) + ADAM_WD * w_ref[...])

    blk = pl.BlockSpec((tm, cols), lambda i: (i, 0))
    sds = jax.ShapeDtypeStruct((rows, cols), F32)
    return pl.pallas_call(
        body, name="adamw",
        grid=(rows // tm,), in_specs=[blk] * 4, out_specs=[blk] * 3, out_shape=[sds] * 3,
        compiler_params=_params(("arbitrary",)),
    )(w, g, m, v)


def kernel(x, meta_tokens, norm_w, w_in, b_in, lb_logits, hg_norm_w, pool_w, pool_scale, w_down_hg, w_down_pool, w_out, final_norm_w, loss_target, m_meta_tokens, m_norm_w, m_w_in, m_b_in, m_lb_logits, m_hg_norm_w, m_pool_w, m_pool_scale, m_w_down_hg, m_w_down_pool, m_w_out, m_final_norm_w, v_meta_tokens, v_norm_w, v_w_in, v_b_in, v_lb_logits, v_hg_norm_w, v_pool_w, v_pool_scale, v_w_down_hg, v_w_down_pool, v_w_out, v_final_norm_w):
    seq = x.shape[1]
    xi, yi, ci = lax.axis_index("x"), lax.axis_index("y"), lax.axis_index("c")
    chip = 2 * xi + yi
    place_arr = jnp.stack([chip, ci]).astype(jnp.int32)
    me_arr = jnp.reshape(4 * xi + 2 * yi + ci, (1,)).astype(jnp.int32)
    q = D_MODEL // N_CHIPS

    def blob_of(wdh, wdp, wo, pw):
        return jnp.concatenate([wdh[0], wdp[0], wo[0], pw[0].reshape(-1, D_MODEL)], axis=0)

    def in_every_slot(a):
        return jnp.broadcast_to(a[None], (N_CHIPS,) + a.shape)

    meta4 = _gather_meta(in_every_slot(meta_tokens))
    meta_full = meta4.transpose(1, 0, 2).reshape(N_META, D_MODEL)
    w4 = in_every_slot(w_in[0].astype(BF16))
    blob4 = in_every_slot(blob_of(w_down_hg, w_down_pool, w_out, pool_w).astype(BF16))
    seg_order = jnp.stack([2 * (chip ^ rel) + t for rel in (0, 2, 1, 3) for t in (0, 1)]).astype(jnp.int32)

    z = jnp.concatenate([jnp.zeros((PAD_ROWS, D_MODEL), F32), meta_full, x[0]], axis=0)
    tgt = jnp.concatenate([jnp.zeros((FIRST_TOKEN_ROW, D_MODEL), F32), loss_target[0]], axis=0)
    fw2 = final_norm_w.reshape(1, D_MODEL)
    dz, w_parts, blob_parts, small = _local_step(
        z, tgt, w4, blob4, seg_order, norm_w, b_in, lb_logits, hg_norm_w, pool_scale, fw2)
    grad_x = dz[FIRST_TOKEN_ROW:][None]

    fin_w = _finish_w(*w_parts, place_arr)
    fin_b = _finish_blob(*blob_parts, place_arr)
    gw2, gb2, slots = _share_finished(fin_w, fin_b, small)
    tot = _sum_small(slots, lb_logits, me_arr)
    g_w_in = gw2.reshape(D_MODEL, 2 * D_MODEL)
    g_blob = gb2.reshape(-1, D_MODEL)

    d_win, nm_win, nv_win = _adamw(w_in[0], g_w_in, m_w_in[0], v_w_in[0])
    d_blob, nm_blob, nv_blob = _adamw(
        blob_of(w_down_hg, w_down_pool, w_out, pool_w), g_blob,
        blob_of(m_w_down_hg, m_w_down_pool, m_w_out, m_pool_w),
        blob_of(v_w_down_hg, v_w_down_pool, v_w_out, v_pool_w))
    g_meta = lax.dynamic_slice_in_dim(tot[ROW_META:ROW_META + N_META], chip * q, q, axis=1)
    d_meta, nm_meta, nv_meta = _adamw(meta_tokens, g_meta, m_meta_tokens, v_meta_tokens)

    def rows_of(nw, bi, lbl, hg, ps, fw):
        return jnp.concatenate([nw, bi.reshape(N_SEG, D_MODEL), lbl, hg, ps, fw.reshape(1, D_MODEL),
                                jnp.zeros((2, D_MODEL), F32)], axis=0)

    g_rows = jnp.concatenate([tot[ROW_NORM_W:ROW_FINAL_W + 1], jnp.zeros((2, D_MODEL), F32)], axis=0)
    d_rows, nm_rows, nv_rows = _adamw(
        rows_of(norm_w, b_in, lb_logits, hg_norm_w, pool_scale, final_norm_w), g_rows,
        rows_of(m_norm_w, m_b_in, m_lb_logits, m_hg_norm_w, m_pool_scale, m_final_norm_w),
        rows_of(v_norm_w, v_b_in, v_lb_logits, v_hg_norm_w, v_pool_scale, v_final_norm_w))

    def unblob(b):
        return (b[0:q][None], b[q:2 * q][None], b[2 * q:3 * q][None], b[3 * q:].reshape(pool_w.shape))

    def unrows(r):
        o = ROW_NORM_W
        return dict(norm_w=r[ROW_NORM_W - o:ROW_B_IN - o], b_in=r[ROW_B_IN - o:ROW_LB - o].reshape(1, -1),
                    lb_logits=r[ROW_LB - o:ROW_HG_W - o], hg_norm_w=r[ROW_HG_W - o:ROW_POOL_SCALE - o],
                    pool_scale=r[ROW_POOL_SCALE - o:ROW_FINAL_W - o], final_norm_w=r[ROW_FINAL_W - o])

    def leaves(meta_part, rows_part, win_part, blob_part):
        r = unrows(rows_part)
        wdh, wdp, wo, pw = unblob(blob_part)
        return [meta_part, r["norm_w"], win_part[None], r["b_in"], r["lb_logits"], r["hg_norm_w"], pw,
                r["pool_scale"], wdh, wdp, wo, r["final_norm_w"]]

    loss = tot[ROW_LOSS, 0]
    return (loss, grad_x,
            *leaves(g_meta, g_rows, g_w_in, g_blob),
            *leaves(d_meta, d_rows, d_win, d_blob),
            *leaves(nm_meta, nm_rows, nm_win, nm_blob),
            *leaves(nv_meta, nv_rows, nv_win, nv_blob))
```

```python
import functools

import numpy as np
import jax
import jax.numpy as jnp
from jax import lax
from jax.experimental import pallas as pl
from jax.experimental.pallas import tpu as pltpu

F32 = jnp.float32
BF16 = jnp.bfloat16

D_MODEL = 1024
N_SEG = 8
N_HEADS = 8
HEAD_DIM = 128
CHUNK = 64
N_META = 16
PAD_ROWS = CHUNK - N_META
FIRST_TOKEN_ROW = CHUNK
LEVELS = (32, 16, 8, 4, 2, 1)
N_EXP = 2 + len(LEVELS)
POOL_WINDOWS = (2, 4, 8, 16)
POOL_GDIM = D_MODEL // len(POOL_WINDOWS)
HALO = 16
LOCAL_UNROLL = 13
BACKWARD_UNROLL = 13
EPS = 1e-6
N_CHIPS = 4
N_DEV = 8
SEGS_REC = (0, 1, 2)
SEGS_MIX = (3, 4, 5, 6, 7)

ADAM_LR = 0.001
ADAM_B1 = 0.9
ADAM_B2 = 0.999
ADAM_EPS = 1e-08
ADAM_WD = 0.01
ADAM_STEP = 10

VMEM_LIMIT_BYTES = 56 * 1024 * 1024

ROW_LOSS = 0
ROW_META = 1
ROW_NORM_W = ROW_META + N_META
ROW_B_IN = ROW_NORM_W + 1
ROW_LB = ROW_B_IN + N_SEG
ROW_HG_W = ROW_LB + 2
ROW_POOL_SCALE = ROW_HG_W + 1
ROW_FINAL_W = ROW_POOL_SCALE + 1
SMALL_ROWS = 32


def _tile(total, cap, mult=16):
    best = None
    for t in range(mult, min(total, cap) + 1, mult):
        if total % t == 0:
            best = t
    assert best is not None, (total, cap, mult)
    return best


def _params(sem=None):
    return pltpu.CompilerParams(dimension_semantics=sem, vmem_limit_bytes=VMEM_LIMIT_BYTES)


def _dot(a, b):
    return jnp.dot(a, b, preferred_element_type=F32)


def _dot_nt(a, b):
    return lax.dot_general(a, b, (((1,), (1,)), ((), ())), preferred_element_type=F32)


def _dot_tn(a, b):
    return lax.dot_general(a, b, (((0,), (0,)), ((), ())), preferred_element_type=F32)


def _sigmoid_pair(x):
    t = jnp.exp(-jnp.abs(x))
    r = 1.0 / (1.0 + t)
    pos = x >= 0
    return jnp.where(pos, r, t * r), jnp.where(pos, t * r, r)


def _exponent_matrix():
    t = np.arange(CHUNK)[:, None]
    j = np.arange(CHUNK)[None, :]
    blocks = [j <= t, j > t]
    for m in LEVELS:
        rho = (t // (2 * m)) * (2 * m) + m
        upper = (t >= rho) & (j > rho) & (j <= t)
        lower = (t < rho) & (j > t) & (j <= rho)
        blocks.append(upper | lower)
    return np.concatenate(blocks, axis=0).astype(np.float32)


def _pair_masks():
    t = np.arange(CHUNK)[:, None]
    s = np.arange(CHUNK)[None, :]
    masks = [t == s]
    for m in LEVELS:
        same = (t // (2 * m)) == (s // (2 * m))
        masks.append(same & ((t % (2 * m)) >= m) & ((s % (2 * m)) < m))
    return np.stack(masks).astype(np.float32)


LEVEL_PAIRS = ((0, 1), (2, 3), (4, 5), (6, None))


def _paired_masks():
    m = _pair_masks()
    zero = np.zeros_like(m[0])
    return np.stack([np.concatenate([m[a], zero if b is None else m[b]], axis=1) for a, b in LEVEL_PAIRS])


def _lower_bound(lbl):
    return 1.0 / (1.0 + jnp.exp(lbl[1:2, :] - lbl[0:1, :]))


def _in_proj(z, norm_w, w4, b_in, seg_order, rows):
    tm = _tile(rows, 1040)
    nt = rows // tm
    gather = _ShardGather(w4.shape[1])

    def body(order_ref, z_ref, nw_ref, b_ref, w_in_ref, h_ref, p_ref, w4_ref,
             h_all, w_buf, w_sem, send_sems, recv_sems):
        kk, i = pl.program_id(0), pl.program_id(1)

        @pl.when((kk == 0) & (i == 0))
        def _():
            gather.start(w4_ref, send_sems, recv_sems, which=(0, 1))

        @pl.when((kk == 2) & (i == 0))
        def _():
            gather.start_diagonal_after_neighbours(w4_ref, send_sems, recv_sems)

        @pl.when(kk == 0)
        def _():
            zt = z_ref[...]
            rstd = lax.rsqrt(jnp.mean(zt * zt, axis=-1, keepdims=True) + EPS)
            h = (zt * rstd * nw_ref[...]).astype(BF16)
            h_all[pl.ds(pl.multiple_of(i * tm, 16), tm), :] = h
            h_ref[...] = h

        @pl.when((kk == 2) & (i == 0))
        def _():
            gather.pass_on(0, w4_ref, send_sems, recv_sems)
            gather.pass_on(1, w4_ref, send_sems, recv_sems)
            gather.await_sibling(0, w4_ref, send_sems, recv_sems)

        @pl.when((kk == 4) & (i == 0))
        def _():
            gather.await_sibling(1, w4_ref, send_sems, recv_sems)

        @pl.when((kk == 5) & (i == 0))
        def _():
            gather.pass_on(2, w4_ref, send_sems, recv_sems)

        @pl.when((kk == 6) & (i == 0))
        def _():
            gather.await_sibling(2, w4_ref, send_sems, recv_sems)

        def weights(which):
            seg = order_ref[2 * (kk // 2) + which]
            return pltpu.make_async_copy(
                w4_ref.at[seg // 2, :, pl.ds(pl.multiple_of((seg % 2) * D_MODEL, D_MODEL), D_MODEL)],
                w_buf.at[which], w_sem.at[which])

        @pl.when((i == 0) & (kk % 2 == 0))
        def _():
            weights(0).start()
            weights(1).start()
            weights(0).wait()

        @pl.when((i == 0) & (kk % 2 == 1))
        def _():
            weights(1).wait()

        p_ref[0] = _dot(h_all[pl.ds(pl.multiple_of(i * tm, 16), tm), :], w_buf[kk % 2]) + b_ref[...]

        @pl.when((kk == N_SEG - 1) & (i == nt - 1))
        def _():
            gather.finish(w4_ref, send_sems, recv_sems, which=(2,))

    first_pass = lambda kk, i, order_ref: (jnp.where(kk == 0, i, nt - 1), 0)
    return pl.pallas_call(
        body, name="in_proj",
        grid_spec=pltpu.PrefetchScalarGridSpec(
            num_scalar_prefetch=1, grid=(N_SEG, nt),
            in_specs=[
                pl.BlockSpec((tm, D_MODEL), first_pass),
                pl.BlockSpec((1, D_MODEL), lambda kk, i, order_ref: (0, 0)),
                pl.BlockSpec((1, D_MODEL), lambda kk, i, order_ref: (0, order_ref[kk])),
                ANY,
            ],
            out_specs=[
                pl.BlockSpec((tm, D_MODEL), first_pass),
                pl.BlockSpec((1, tm, D_MODEL), lambda kk, i, order_ref: (order_ref[kk], i, 0)),
                ANY,
            ],
            scratch_shapes=[
                pltpu.VMEM((rows, D_MODEL), BF16),
                pltpu.VMEM((2, D_MODEL, D_MODEL), BF16),
                pltpu.SemaphoreType.DMA((2,)),
            ] + gather.semaphores()),
        out_shape=[
            jax.ShapeDtypeStruct((rows, D_MODEL), BF16),
            jax.ShapeDtypeStruct((N_SEG, rows, D_MODEL), F32),
            jax.ShapeDtypeStruct(w4.shape, w4.dtype),
        ],
        input_output_aliases={4: 2},
        compiler_params=_params(("arbitrary", "arbitrary")),
    )(seg_order, z, norm_w, b_in, w4)


def _hgrn_forward(p3, lb_logits, wexp2, masks2, blob4, rows):
    n_chunks = rows // CHUNK
    cpb = _tile(n_chunks, 13, mult=1)
    rb_rows = cpb * CHUNK
    n_rb = n_chunks // cpb
    lanes = cpb * HEAD_DIM
    gather = _ShardGather(blob4.shape[1])

    def body(q_ref, fz_ref, v_ref, lbl_ref, wexp_ref, mask_ref, b_in_ref, o_ref, s_ref, e16_ref, a2_ref, b4_ref,
             st_ref, e_ref, u_ref, q_s, kk_s, v_s, qb_s, oi_s, send_sems, recv_sems):
        rb = pl.program_id(1)

        @pl.when((pl.program_id(0) == 0) & (rb == 0))
        def _():
            gather.start(b4_ref, send_sems, recv_sems)

        @pl.when(rb == 0)
        def _():
            st_ref[...] = jnp.zeros_like(st_ref)

        lb = _lower_bound(lbl_ref[...])
        row = rb * rb_rows + lax.broadcasted_iota(jnp.int32, (rb_rows, 1), 0)
        valid = row >= PAD_ROWS
        sg, sn = _sigmoid_pair(fz_ref[0])
        g = jnp.where(valid, jnp.log(lb + (1.0 - lb) * sg), 0.0)
        kk_s[...] = jnp.where(valid, (1.0 - lb) * sn, 0.0)
        q_s[...] = jnp.where(valid, q_ref[0], 0.0)
        v_s[...] = jnp.where(valid, v_ref[0], 0.0).astype(BF16)
        hi = g.astype(BF16)
        mid = (g - hi.astype(F32)).astype(BF16)
        g2 = jnp.concatenate(
            [jnp.concatenate([hi[b * CHUNK:(b + 1) * CHUNK], mid[b * CHUNK:(b + 1) * CHUNK]], axis=0)
             for b in range(cpb)], axis=1)
        e_ref[...] = jnp.exp(_dot(wexp_ref[...], g2))
        e16_ref[0, 0] = e_ref[...].astype(BF16)

        zeros16 = jnp.zeros((CHUNK, HEAD_DIM), BF16)

        def local(b, carry):
            r0 = pl.multiple_of(b * CHUNK, CHUNK)
            l0 = pl.multiple_of(b * HEAD_DIM, HEAD_DIM)
            q = q_s[pl.ds(r0, CHUNK), :]
            kk = kk_s[pl.ds(r0, CHUNK), :]
            v16 = v_s[pl.ds(r0, CHUNK), :]

            def scaled(entry):
                if entry == 0:
                    return q.astype(BF16), kk.astype(BF16)
                e_m = e_ref[(1 + entry) * CHUNK:(2 + entry) * CHUNK, pl.ds(l0, HEAD_DIM)]
                return (q * e_m).astype(BF16), (kk * e_m).astype(BF16)

            a2 = jnp.zeros((CHUNK, 2 * CHUNK), F32)
            for p, (ea, eb) in enumerate(LEVEL_PAIRS):
                qa, ka = scaled(ea)
                if eb is None:
                    prod = _dot_nt(qa, jnp.concatenate([ka, zeros16], axis=0))
                else:
                    qb_, kb_ = scaled(eb)
                    rhs = jnp.concatenate([jnp.concatenate([ka, zeros16], axis=1),
                                           jnp.concatenate([zeros16, kb_], axis=1)], axis=0)
                    prod = _dot_nt(jnp.concatenate([qa, qb_], axis=1), rhs)
                a2 = a2 + mask_ref[p] * prod
            a2_16 = a2.astype(BF16)
            a2_ref[pl.ds(r0, CHUNK), :] = a2_16
            oi_s[pl.ds(r0, CHUNK), :] = _dot(a2_16, jnp.concatenate([v16, v16], axis=0))
            e_b = e_ref[0:CHUNK, pl.ds(l0, HEAD_DIM)]
            e_c = e_ref[CHUNK:2 * CHUNK, pl.ds(l0, HEAD_DIM)]
            qb_s[pl.ds(r0, CHUNK), :] = (q * e_b).astype(BF16)
            u_ref[b] = _dot_tn(v16, (kk * e_c).astype(BF16))
            return carry

        lax.fori_loop(0, cpb, local, 0, unroll=LOCAL_UNROLL)

        def recur(b, st):
            l0 = pl.multiple_of(b * HEAD_DIM, HEAD_DIM)
            s_ref[0, b] = st
            return st * e_ref[CHUNK - 1:CHUNK, pl.ds(l0, HEAD_DIM)] + u_ref[b]

        st_ref[...] = lax.fori_loop(0, cpb, recur, st_ref[...])

        def inter(b, carry):
            r0 = pl.multiple_of(b * CHUNK, CHUNK)
            o_ref[pl.ds(r0, CHUNK), :] = oi_s[pl.ds(r0, CHUNK), :] + _dot_nt(
                qb_s[pl.ds(r0, CHUNK), :], s_ref[0, b].astype(BF16))
            return carry

        lax.fori_loop(0, cpb, inter, 0, unroll=LOCAL_UNROLL)

        @pl.when((pl.program_id(0) == N_HEADS // 2) & (rb == 0))
        def _():
            for j in range(N_CHIPS - 1):
                gather.pass_on(j, b4_ref, send_sems, recv_sems)

        @pl.when((pl.program_id(0) == N_HEADS - 1) & (rb == n_rb - 1))
        def _():
            for j in range(N_CHIPS - 1):
                gather.await_sibling(j, b4_ref, send_sems, recv_sems)
            gather.finish(b4_ref, send_sems, recv_sems)

    head_block = lambda seg: pl.BlockSpec((1, rb_rows, HEAD_DIM), lambda h, r: (seg, r, h))
    return pl.pallas_call(
        body, name="hgrn_forward",
        grid=(N_HEADS, n_rb),
        in_specs=[
            head_block(0), head_block(1), head_block(2),
            pl.BlockSpec((2, HEAD_DIM), lambda h, r: (0, h)),
            pl.BlockSpec((N_EXP * CHUNK, 2 * CHUNK), lambda h, r: (0, 0)),
            pl.BlockSpec((len(LEVEL_PAIRS), CHUNK, 2 * CHUNK), lambda h, r: (0, 0, 0)),
            ANY,
        ],
        out_specs=[
            pl.BlockSpec((rb_rows, HEAD_DIM), lambda h, r: (r, h)),
            pl.BlockSpec((1, cpb, HEAD_DIM, HEAD_DIM), lambda h, r: (h, r, 0, 0)),
            pl.BlockSpec((1, 1, N_EXP * CHUNK, lanes), lambda h, r: (h, r, 0, 0)),
            pl.BlockSpec((rb_rows, HEAD_DIM), lambda h, r: (r, h)),
            ANY,
        ],
        out_shape=[
            jax.ShapeDtypeStruct((rows, D_MODEL), F32),
            jax.ShapeDtypeStruct((N_HEADS, n_chunks, HEAD_DIM, HEAD_DIM), F32),
            jax.ShapeDtypeStruct((N_HEADS, n_rb, N_EXP * CHUNK, lanes), BF16),
            jax.ShapeDtypeStruct((rows, D_MODEL), BF16),
            jax.ShapeDtypeStruct(blob4.shape, blob4.dtype),
        ],
        input_output_aliases={6: 4},
        scratch_shapes=[
            pltpu.VMEM((HEAD_DIM, HEAD_DIM), F32),
            pltpu.VMEM((N_EXP * CHUNK, lanes), F32),
            pltpu.VMEM((cpb, HEAD_DIM, HEAD_DIM), F32),
            pltpu.VMEM((rb_rows, HEAD_DIM), F32),
            pltpu.VMEM((rb_rows, HEAD_DIM), F32),
            pltpu.VMEM((rb_rows, HEAD_DIM), BF16),
            pltpu.VMEM((rb_rows, HEAD_DIM), BF16),
            pltpu.VMEM((rb_rows, HEAD_DIM), F32),
        ] + gather.semaphores(),
        compiler_params=_params(("arbitrary", "arbitrary")),
    )(p3, p3, p3, lb_logits, wexp2, masks2, blob4)


def _hgrn_backward(p3, d_o, states, e16, a2, lb_logits, wexp_t, masks2, dw16, blob16, rows):
    n_chunks = rows // CHUNK
    cpb = _tile(n_chunks, 13, mult=1)
    rb_rows = cpb * CHUNK
    n_rb = n_chunks // cpb
    lanes = cpb * HEAD_DIM
    exchange = _GradExchange(SEGS_MIX, with_blob=True)

    def body(q_ref, fz_ref, v_ref, do_ref, s_ref, e_ref, a2_ref, lbl_ref, wexpt_ref, mask_ref, dw_ref, blob_ref,
             dp_ref, dlb_ref, rxw_ref, rxb_ref,
             dst_ref, g_ref, dsn_ref, q_s, kk_s, v_s, do_s, dq_s, dkk_s, dg_s, dx_s, send_sems, recv_sems):
        step = pl.program_id(1)
        rb = n_rb - 1 - step

        @pl.when((pl.program_id(0) == 0) & (step == 0))
        def _():
            exchange.start(dw_ref, rxw_ref, blob_ref, rxb_ref, send_sems, recv_sems)

        @pl.when(step == 0)
        def _():
            dst_ref[...] = jnp.zeros_like(dst_ref)
            dlb_ref[...] = jnp.zeros_like(dlb_ref)

        lb = _lower_bound(lbl_ref[...])
        row = rb * rb_rows + lax.broadcasted_iota(jnp.int32, (rb_rows, 1), 0)
        valid = row >= PAD_ROWS
        sg, sn = _sigmoid_pair(fz_ref[0])
        f = lb + (1.0 - lb) * sg
        g = jnp.where(valid, jnp.log(f), 0.0)
        kk_s[...] = jnp.where(valid, (1.0 - lb) * sn, 0.0)
        q_s[...] = jnp.where(valid, q_ref[0], 0.0)
        v_s[...] = jnp.where(valid, v_ref[0], 0.0).astype(BF16)
        do_s[...] = do_ref[...].astype(BF16)
        e_last_all = jnp.exp(jnp.concatenate(
            [jnp.sum(g[b * CHUNK:(b + 1) * CHUNK], axis=0, keepdims=True) for b in range(cpb)], axis=0))
        last_row = lax.broadcasted_iota(jnp.int32, (CHUNK, 1), 0) == CHUNK - 1
        zeros16 = jnp.zeros((CHUNK, HEAD_DIM), BF16)

        def factor(block, l0):
            return e_ref[0, 0, block * CHUNK:(block + 1) * CHUNK, pl.ds(l0, HEAD_DIM)].astype(F32)

        def contribution(b, carry):
            r0 = pl.multiple_of(b * CHUNK, CHUNK)
            l0 = pl.multiple_of(b * HEAD_DIM, HEAD_DIM)
            qb16 = (q_s[pl.ds(r0, CHUNK), :] * factor(0, l0)).astype(BF16)
            g_ref[b] = _dot_tn(do_s[pl.ds(r0, CHUNK), :], qb16)
            return carry

        lax.fori_loop(0, cpb, contribution, 0, unroll=LOCAL_UNROLL)

        cur = dst_ref[...]
        for b in reversed(range(cpb)):
            dsn_ref[b] = cur
            cur = cur * e_last_all[b:b + 1, :] + g_ref[b]
        dst_ref[...] = cur

        def local(b, carry):
            r0 = pl.multiple_of(b * CHUNK, CHUNK)
            l0 = pl.multiple_of(b * HEAD_DIM, HEAD_DIM)
            q = q_s[pl.ds(r0, CHUNK), :]
            kk = kk_s[pl.ds(r0, CHUNK), :]
            v16 = v_s[pl.ds(r0, CHUNK), :]
            do16 = do_s[pl.ds(r0, CHUNK), :]
            st = s_ref[0, b]
            dsn = dsn_ref[b]
            dsn16 = dsn.astype(BF16)
            e_b, e_c = factor(0, l0), factor(1, l0)
            qb, kc = q * e_b, kk * e_c

            t = _dot_tn(a2_ref[pl.ds(r0, CHUNK), :], do16)
            dv = t[0:CHUNK] + t[CHUNK:2 * CHUNK] + _dot_nt(kc.astype(BF16), dsn16)
            dp_ref[2, pl.ds(r0, CHUNK), :] = dv.astype(BF16)
            da2 = _dot_nt(do16, jnp.concatenate([v16, v16], axis=0))
            dqb = _dot(do16, st.astype(BF16))
            dkc = _dot(v16, dsn16)
            de = jnp.sum(dsn * st, axis=0, keepdims=True) * e_b[CHUNK - 1:CHUNK, :]
            dq = e_b * dqb
            dkk = e_c * dkc
            dx_s[0:CHUNK, pl.ds(l0, HEAD_DIM)] = (qb * dqb + jnp.where(last_row, de, 0.0)).astype(BF16)
            dx_s[CHUNK:2 * CHUNK, pl.ds(l0, HEAD_DIM)] = (kc * dkc).astype(BF16)

            def scaled(entry):
                if entry == 0:
                    return q, kk, None
                e_m = factor(1 + entry, l0)
                return q * e_m, kk * e_m, e_m

            for p, (ea, eb) in enumerate(LEVEL_PAIRS):
                dm = (mask_ref[p] * da2).astype(BF16)
                qa, ka, e_a = scaled(ea)
                if eb is None:
                    lhs_q = jnp.concatenate([qa.astype(BF16), zeros16], axis=1)
                    rhs_k = jnp.concatenate([jnp.concatenate([ka.astype(BF16), zeros16], axis=1),
                                             jnp.concatenate([zeros16, zeros16], axis=1)], axis=0)
                else:
                    qb_, kb_, e_bb = scaled(eb)
                    lhs_q = jnp.concatenate([qa.astype(BF16), qb_.astype(BF16)], axis=1)
                    rhs_k = jnp.concatenate([jnp.concatenate([ka.astype(BF16), zeros16], axis=1),
                                             jnp.concatenate([zeros16, kb_.astype(BF16)], axis=1)], axis=0)
                dq2 = _dot(dm, rhs_k)
                dk2 = _dot_tn(dm, lhs_q)
                parts = [(ea, qa, ka, e_a, dq2[:, :HEAD_DIM], dk2[0:CHUNK, :HEAD_DIM])]
                if eb is not None:
                    parts.append((eb, qb_, kb_, e_bb, dq2[:, HEAD_DIM:], dk2[CHUNK:2 * CHUNK, HEAD_DIM:]))
                for entry, q_m, k_m, e_m, dq_m, dk_m in parts:
                    if entry == 0:
                        dq = dq + dq_m
                        dkk = dkk + dk_m
                    else:
                        dq = dq + e_m * dq_m
                        dkk = dkk + e_m * dk_m
                        dx_s[(1 + entry) * CHUNK:(2 + entry) * CHUNK, pl.ds(l0, HEAD_DIM)] = (
                            q_m * dq_m + k_m * dk_m).astype(BF16)
            dq_s[pl.ds(r0, CHUNK), :] = dq
            dkk_s[pl.ds(r0, CHUNK), :] = dkk
            return carry

        lax.fori_loop(0, cpb, local, 0, unroll=BACKWARD_UNROLL)

        dg_all = _dot(wexpt_ref[...], dx_s[...])
        for b in range(cpb):
            dg_s[b * CHUNK:(b + 1) * CHUNK, :] = dg_all[:, b * HEAD_DIM:(b + 1) * HEAD_DIM]
        t = jnp.where(valid, dg_s[...] / f - dkk_s[...], 0.0)
        dlb_ref[...] += jnp.sum(sn * t, axis=0, keepdims=True)
        dp_ref[0] = jnp.where(valid, dq_s[...], 0.0).astype(BF16)
        dp_ref[1] = ((1.0 - lb) * sg * sn * t).astype(BF16)

        @pl.when((pl.program_id(0) == N_HEADS - 1) & (step == n_rb - 1))
        def _():
            exchange.wait(dw_ref, rxw_ref, blob_ref, rxb_ref, send_sems, recv_sems)

    head_block = lambda seg: pl.BlockSpec((1, rb_rows, HEAD_DIM), lambda h, s: (seg, n_rb - 1 - s, h))
    row_block = pl.BlockSpec((rb_rows, HEAD_DIM), lambda h, s: (n_rb - 1 - s, h))
    return pl.pallas_call(
        body, name="hgrn_backward",
        grid=(N_HEADS, n_rb),
        in_specs=[
            head_block(0), head_block(1), head_block(2),
            row_block,
            pl.BlockSpec((1, cpb, HEAD_DIM, HEAD_DIM), lambda h, s: (h, n_rb - 1 - s, 0, 0)),
            pl.BlockSpec((1, 1, N_EXP * CHUNK, lanes), lambda h, s: (h, n_rb - 1 - s, 0, 0)),
            row_block,
            pl.BlockSpec((2, HEAD_DIM), lambda h, s: (0, h)),
            pl.BlockSpec((CHUNK, N_EXP * CHUNK), lambda h, s: (0, 0)),
            pl.BlockSpec((len(LEVEL_PAIRS), CHUNK, 2 * CHUNK), lambda h, s: (0, 0, 0)),
            ANY, ANY,
        ],
        out_specs=[
            pl.BlockSpec((3, rb_rows, HEAD_DIM), lambda h, s: (0, n_rb - 1 - s, h)),
            pl.BlockSpec((1, HEAD_DIM), lambda h, s: (0, h)),
            ANY, ANY,
        ],
        out_shape=[
            jax.ShapeDtypeStruct((3, rows, D_MODEL), BF16),
            jax.ShapeDtypeStruct((1, D_MODEL), F32),
            exchange.landing_w(), exchange.landing_blob(blob16),
        ],
        scratch_shapes=[
            pltpu.VMEM((HEAD_DIM, HEAD_DIM), F32),
            pltpu.VMEM((cpb, HEAD_DIM, HEAD_DIM), F32),
            pltpu.VMEM((cpb, HEAD_DIM, HEAD_DIM), F32),
            pltpu.VMEM((rb_rows, HEAD_DIM), F32),
            pltpu.VMEM((rb_rows, HEAD_DIM), F32),
            pltpu.VMEM((rb_rows, HEAD_DIM), BF16),
            pltpu.VMEM((rb_rows, HEAD_DIM), BF16),
            pltpu.VMEM((rb_rows, HEAD_DIM), F32),
            pltpu.VMEM((rb_rows, HEAD_DIM), F32),
            pltpu.VMEM((rb_rows, HEAD_DIM), F32),
            pltpu.VMEM((N_EXP * CHUNK, lanes), BF16),
        ] + exchange.semaphores(),
        compiler_params=_params(("arbitrary", "arbitrary")),
    )(p3, p3, p3, d_o, states, e16, a2, lb_logits, wexp_t, masks2, dw16, blob16)


def _sigmoid(x):
    return 1.0 / (1.0 + jnp.exp(-x))


def _silu_and_grad(x):
    s = _sigmoid(x)
    return x * s, s * (1.0 + x * (1.0 - s))


def _window_sum(ext, width, forward_looking):
    n = ext.shape[0]
    s = ext
    step = 1
    while step < width:
        s = s + pltpu.roll(s, (n - step) if forward_looking else step, 0)
        step *= 2
    return s


def _mixers(o, p3, z, tgt, wdh, wdp, wout, poolw, hg_w, pool_scale, final_w, rows):
    tm = _tile(rows, 208)
    nt = rows // tm
    halo_blocks = tm // HALO
    n_grp = len(POOL_WINDOWS)

    def body(o_ref, ghg_ref, u_ref, gpl_ref, mhg_ref, mpl_ref, uh_ref, z_ref, t_ref,
             wdh_ref, wdp_ref, wout_ref, pw_ref, hgw_ref, ps_ref, fw_ref,
             do_ref, dz2_ref, dp_ref, dwdh_ref, dwdp_ref, dwout_ref, dpw_ref, small_ref, carry_ref):
        step = pl.program_id(0)
        tile = nt - 1 - step

        @pl.when(step == 0)
        def _():
            dwdh_ref[...] = jnp.zeros_like(dwdh_ref)
            dwdp_ref[...] = jnp.zeros_like(dwdp_ref)
            dwout_ref[...] = jnp.zeros_like(dwout_ref)
            dpw_ref[...] = jnp.zeros_like(dpw_ref)
            small_ref[...] = jnp.zeros_like(small_ref)
            carry_ref[...] = jnp.zeros_like(carry_ref)

        row = tile * tm + lax.broadcasted_iota(jnp.int32, (tm, 1), 0)
        real = row >= PAD_ROWS
        pos1 = jnp.maximum(row - PAD_ROWS + 1, 1).astype(F32)

        u = jnp.where(real, u_ref[0], 0.0)
        halo_row = tile * tm - HALO + lax.broadcasted_iota(jnp.int32, (HALO, 1), 0)
        uh = jnp.where(halo_row >= PAD_ROWS, uh_ref[0], 0.0)
        ext = jnp.concatenate([uh, u], axis=0)
        pooled, inv_cnt, mixed = [], [], []
        for g, w in enumerate(POOL_WINDOWS):
            cols = slice(g * POOL_GDIM, (g + 1) * POOL_GDIM)
            inv = 1.0 / jnp.minimum(pos1, float(w))
            ws = _window_sum(ext[:, cols], w, False)[HALO:]
            pg = (ws * inv - u[:, cols]).astype(BF16)
            pooled.append(pg)
            inv_cnt.append(inv)
            mixed.append(_dot(pg, pw_ref[g]))
        mixed = jnp.concatenate(mixed, axis=1)
        gpl = gpl_ref[0]
        sp, dsp = _silu_and_grad(gpl)
        ps = ps_ref[...]
        a_pool = (mixed * ps * sp).astype(BF16)
        y_pool = _dot(a_pool, wdp_ref[...])

        o = o_ref[...]
        o_hat, rstd_h = [], []
        for h in range(N_HEADS):
            oh = o[:, h * HEAD_DIM:(h + 1) * HEAD_DIM]
            r = lax.rsqrt(jnp.mean(oh * oh, axis=-1, keepdims=True) + EPS)
            rstd_h.append(r)
            o_hat.append(oh * r)
        o_hat = jnp.concatenate(o_hat, axis=1)
        hgw = hgw_ref[...]
        o_n = o_hat * hgw
        ghg = ghg_ref[0]
        sh, dsh = _silu_and_grad(ghg)
        a_hg = (o_n * sh).astype(BF16)
        y_hg = _dot(a_hg, wdh_ref[...])

        s_mh = _sigmoid(mhg_ref[0])
        s_mp = _sigmoid(mpl_ref[0])
        merged = (s_mh * y_hg + s_mp * y_pool).astype(BF16)
        z2 = z_ref[...] + _dot(merged, wout_ref[...])
        rstd2 = lax.rsqrt(jnp.mean(z2 * z2, axis=-1, keepdims=True) + EPS)
        zh = z2 * rstd2
        fw = fw_ref[...]
        err = jnp.where(row >= FIRST_TOKEN_ROW, zh * fw - t_ref[...], 0.0)
        small_ref[ROW_LOSS:ROW_LOSS + 1, :] += jnp.sum(err * err, axis=0, keepdims=True) * (0.5 / D_MODEL)
        dy = err * (1.0 / D_MODEL)

        small_ref[ROW_FINAL_W:ROW_FINAL_W + 1, :] += jnp.sum(dy * zh, axis=0, keepdims=True)
        uu = dy * fw
        dz2 = rstd2 * (uu - zh * jnp.mean(uu * zh, axis=-1, keepdims=True))
        dz2_ref[...] = dz2
        dz2_16 = dz2.astype(BF16)
        dmerged = _dot_nt(dz2_16, wout_ref[...])
        dwout_ref[...] += _dot_tn(merged, dz2_16)
        dy_hg = (s_mh * dmerged).astype(BF16)
        dy_pool = (s_mp * dmerged).astype(BF16)
        dp_ref[3] = (dmerged * y_hg * s_mh * (1.0 - s_mh)).astype(BF16)
        dp_ref[4] = (dmerged * y_pool * s_mp * (1.0 - s_mp)).astype(BF16)

        da_hg = _dot_nt(dy_hg, wdh_ref[...])
        dwdh_ref[...] += _dot_tn(a_hg, dy_hg)
        dp_ref[0] = (da_hg * o_n * dsh).astype(BF16)
        do_n = da_hg * sh
        small_ref[ROW_HG_W:ROW_HG_W + 1, :] += jnp.sum(do_n * o_hat, axis=0, keepdims=True)
        d_hat = do_n * hgw
        for h in range(N_HEADS):
            cols = slice(h * HEAD_DIM, (h + 1) * HEAD_DIM)
            dh_, oh_ = d_hat[:, cols], o_hat[:, cols]
            do_ref[:, cols] = rstd_h[h] * (dh_ - oh_ * jnp.mean(dh_ * oh_, axis=-1, keepdims=True))

        da_pool = _dot_nt(dy_pool, wdp_ref[...])
        dwdp_ref[...] += _dot_tn(a_pool, dy_pool)
        small_ref[ROW_POOL_SCALE:ROW_POOL_SCALE + 1, :] += jnp.sum(da_pool * mixed * sp, axis=0, keepdims=True)
        dp_ref[2] = (da_pool * mixed * ps * dsp).astype(BF16)
        dmixed = (da_pool * ps * sp).astype(BF16)
        carry = carry_ref[...]
        du, new_carry = [], []
        for g, w in enumerate(POOL_WINDOWS):
            cols = slice(g * POOL_GDIM, (g + 1) * POOL_GDIM)
            dmg = dmixed[:, cols]
            dpooled = _dot_nt(dmg, pw_ref[g])
            dpw_ref[g] += _dot_tn(pooled[g], dmg)
            dps = dpooled * inv_cnt[g]
            ext_b = jnp.concatenate([dps, carry[:, cols]], axis=0)
            du.append(_window_sum(ext_b, w, True)[:tm] - dpooled)
            new_carry.append(dps[:HALO])
        dp_ref[1] = jnp.where(real, jnp.concatenate(du, axis=1), 0.0).astype(BF16)
        carry_ref[...] = jnp.concatenate(new_carry, axis=1)

    row_block = pl.BlockSpec((tm, D_MODEL), lambda s: (nt - 1 - s, 0))
    seg_block = lambda seg: pl.BlockSpec((1, tm, D_MODEL), lambda s: (seg, nt - 1 - s, 0))
    whole = pl.BlockSpec(memory_space=pltpu.VMEM)
    return pl.pallas_call(
        body, name="mixers",
        grid=(nt,),
        in_specs=[
            row_block, seg_block(3), seg_block(4), seg_block(5), seg_block(6), seg_block(7),
            pl.BlockSpec((1, HALO, D_MODEL),
                         lambda s: (4, jnp.maximum((nt - 1 - s) * halo_blocks - 1, 0), 0)),
            row_block, row_block,
            whole, whole, whole, whole, whole, whole, whole,
        ],
        out_specs=[
            row_block, row_block,
            pl.BlockSpec((5, tm, D_MODEL), lambda s: (0, nt - 1 - s, 0)),
            whole, whole, whole, whole, whole,
        ],
        out_shape=[
            jax.ShapeDtypeStruct((rows, D_MODEL), F32),
            jax.ShapeDtypeStruct((rows, D_MODEL), F32),
            jax.ShapeDtypeStruct((5, rows, D_MODEL), BF16),
            jax.ShapeDtypeStruct((D_MODEL, D_MODEL), F32),
            jax.ShapeDtypeStruct((D_MODEL, D_MODEL), F32),
            jax.ShapeDtypeStruct((D_MODEL, D_MODEL), F32),
            jax.ShapeDtypeStruct((n_grp, POOL_GDIM, POOL_GDIM), F32),
            jax.ShapeDtypeStruct((SMALL_ROWS, D_MODEL), F32),
        ],
        scratch_shapes=[pltpu.VMEM((HALO, D_MODEL), F32)],
        compiler_params=_params(("arbitrary",)),
    )(o, p3, p3, p3, p3, p3, p3, z, tgt, wdh, wdp, wout, poolw, hg_w, pool_scale, final_w)


def _seg_specs(tm, row_of, seg_of):
    def spec_a(*g):
        k = seg_of(*g)
        return (jnp.minimum(k, 2), jnp.where(k < 3, row_of(*g), 0), 0)

    def spec_b(*g):
        k = seg_of(*g)
        return (jnp.maximum(k - 3, 0), jnp.where(k >= 3, row_of(*g), 0), 0)

    return pl.BlockSpec((1, tm, D_MODEL), spec_a), pl.BlockSpec((1, tm, D_MODEL), spec_b)


def _in_proj_weight_grad(h, dp, rows, name):
    n_seg = dp.shape[0]
    tm = _tile(rows, 1040)
    nt = rows // tm
    half = D_MODEL // 2

    def body(h_ref, dp_ref, part_ref, part16_ref, db_ref, acc_ref, bacc_ref, stage_ref, land_ref,
             send_sems, recv_sems):
        k, i = pl.program_id(0), pl.program_id(1)
        x, y, c = lax.axis_index("x"), lax.axis_index("y"), lax.axis_index("c")

        def to_sibling(seg):
            return pltpu.make_async_remote_copy(
                src_ref=stage_ref.at[seg], dst_ref=land_ref.at[seg], send_sem=send_sems.at[seg],
                recv_sem=recv_sems.at[seg], device_id=(x, y, 1 - c), device_id_type=MESH)

        @pl.when(i == 0)
        def _():
            acc_ref[...] = jnp.zeros_like(acc_ref)
            bacc_ref[...] = jnp.zeros_like(bacc_ref)

        dpt = dp_ref[0]
        acc_ref[...] += _dot_tn(h_ref[...], dpt)
        bacc_ref[...] += jnp.sum(dpt.astype(F32), axis=0, keepdims=True)

        @pl.when(i == nt - 1)
        def _():
            db_ref[0] = bacc_ref[...]
            part_ref[k] = acc_ref[pl.ds(pl.multiple_of(c * half, half), half), :]
            stage_ref[k] = acc_ref[pl.ds(pl.multiple_of((1 - c) * half, half), half), :].astype(BF16)
            to_sibling(k).start()

        @pl.when((k == n_seg - 1) & (i == nt - 1))
        def _():
            for seg in range(n_seg):
                to_sibling(seg).wait_recv()
                total = part_ref[seg] + land_ref[seg].astype(F32)
                part_ref[seg] = total
                part16_ref[seg] = total.astype(BF16)
            for seg in range(n_seg):
                to_sibling(seg).wait_send()

    whole = pl.BlockSpec(memory_space=pltpu.VMEM)
    return pl.pallas_call(
        body, name=name,
        grid=(n_seg, nt),
        in_specs=[pl.BlockSpec((tm, D_MODEL), lambda k, i: (i, 0)),
                  pl.BlockSpec((1, tm, D_MODEL), lambda k, i: (k, i, 0))],
        out_specs=[whole, whole, pl.BlockSpec((1, 1, D_MODEL), lambda k, i: (k, 0, 0))],
        out_shape=[
            jax.ShapeDtypeStruct((n_seg, half, D_MODEL), F32),
            jax.ShapeDtypeStruct((n_seg, half, D_MODEL), BF16),
            jax.ShapeDtypeStruct((n_seg, 1, D_MODEL), F32),
        ],
        scratch_shapes=[
            pltpu.VMEM((D_MODEL, D_MODEL), F32), pltpu.VMEM((1, D_MODEL), F32),
            pltpu.VMEM((n_seg, half, D_MODEL), BF16),
            pltpu.VMEM((n_seg, half, D_MODEL), BF16),
            pltpu.SemaphoreType.DMA((n_seg,)), pltpu.SemaphoreType.DMA((n_seg,)),
        ],
        compiler_params=_params(("arbitrary", "arbitrary")),
    )(h, dp)


def _input_grad(dpa, dpb, w4, z, dz2, norm_w, dw16, rows):
    tm = _tile(rows, 1040)
    nt = rows // tm
    exchange = _GradExchange(SEGS_REC, with_blob=False)

    def body(dpa_ref, dpb_ref, w_ref, z_ref, dz2_ref, nw_ref, dw_ref, dz_ref, dnw_ref, rxw_ref,
             acc_ref, send_sems, recv_sems):
        i, k = pl.program_id(0), pl.program_id(1)

        @pl.when((i == 0) & (k == 0))
        def _():
            exchange.start(dw_ref, rxw_ref, None, None, send_sems, recv_sems)
            dnw_ref[...] = jnp.zeros_like(dnw_ref)

        @pl.when((i == nt - 1) & (k == N_SEG - 1))
        def _():
            exchange.wait(dw_ref, rxw_ref, None, None, send_sems, recv_sems)

        @pl.when(k == 0)
        def _():
            acc_ref[...] = jnp.zeros_like(acc_ref)

        @pl.when(k < 3)
        def _():
            acc_ref[...] += _dot_nt(dpa_ref[0], w_ref[0])

        @pl.when(k >= 3)
        def _():
            acc_ref[...] += _dot_nt(dpb_ref[0], w_ref[0])

        @pl.when(k == N_SEG - 1)
        def _():
            zt = z_ref[...]
            rstd = lax.rsqrt(jnp.mean(zt * zt, axis=-1, keepdims=True) + EPS)
            zh = zt * rstd
            dh = acc_ref[...]
            dnw_ref[...] += jnp.sum(dh * zh, axis=0, keepdims=True)
            uu = dh * nw_ref[...]
            dz_ref[...] = dz2_ref[...] + rstd * (uu - zh * jnp.mean(uu * zh, axis=-1, keepdims=True))

    spec_a, spec_b = _seg_specs(tm, lambda i, k: i, lambda i, k: k)
    last_only = pl.BlockSpec((tm, D_MODEL), lambda i, k: (jnp.where(k == N_SEG - 1, i, 0), 0))
    return pl.pallas_call(
        body, name="input_grad",
        grid=(nt, N_SEG),
        in_specs=[
            spec_a, spec_b,
            pl.BlockSpec((1, D_MODEL, D_MODEL), lambda i, k: (k // 2, 0, k % 2)),
            last_only, last_only,
            pl.BlockSpec((1, D_MODEL), lambda i, k: (0, 0)),
            ANY,
        ],
        out_specs=[
            pl.BlockSpec((tm, D_MODEL), lambda i, k: (i, 0)),
            pl.BlockSpec((1, D_MODEL), lambda i, k: (0, 0)),
            ANY,
        ],
        out_shape=[
            jax.ShapeDtypeStruct((rows, D_MODEL), F32),
            jax.ShapeDtypeStruct((1, D_MODEL), F32),
            exchange.landing_w(),
        ],
        scratch_shapes=[pltpu.VMEM((tm, D_MODEL), F32)] + exchange.semaphores(),
        compiler_params=_params(("arbitrary", "arbitrary")),
    )(dpa, dpb, w4, z, dz2, norm_w, dw16)


def _local_step(z, tgt, w4, blob4, seg_order, norm_w, b_in, lb_logits, hg_w, pool_scale, final_w):
    rows = z.shape[0]
    q = D_MODEL // N_CHIPS
    n_grp = len(POOL_WINDOWS)
    pg = POOL_GDIM // N_CHIPS

    wexp2 = jnp.asarray(np.tile(_exponent_matrix(), (1, 2)), BF16)
    wexp_t = jnp.asarray(_exponent_matrix().T, BF16)
    masks2 = jnp.asarray(_paired_masks(), F32)

    h, p3, w4 = _in_proj(z, norm_w, w4, b_in, seg_order, rows)
    o, states, e16, a2, blob4 = _hgrn_forward(p3, lb_logits, wexp2, masks2, blob4, rows)
    wdh = blob4[:, 0:q].reshape(D_MODEL, D_MODEL)
    wdp = blob4[:, q:2 * q].reshape(D_MODEL, D_MODEL)
    wout = blob4[:, 2 * q:3 * q].reshape(D_MODEL, D_MODEL)
    poolw = blob4[:, 3 * q:].reshape(N_CHIPS, n_grp, pg, POOL_GDIM).transpose(1, 0, 2, 3)
    poolw = poolw.reshape(n_grp, POOL_GDIM, POOL_GDIM)
    d_o, dz2, dpb, dwdh, dwdp, dwout, dpw, small = _mixers(
        o, p3, z, tgt, wdh, wdp, wout, poolw, hg_w, pool_scale, final_w, rows)
    dpw4 = dpw.reshape(n_grp, N_CHIPS, pg, POOL_GDIM).transpose(1, 0, 2, 3)
    dpw4 = dpw4.reshape(N_CHIPS, n_grp * pg * POOL_GDIM // D_MODEL, D_MODEL)
    dblob4 = jnp.concatenate([dwdh.reshape(N_CHIPS, q, D_MODEL), dwdp.reshape(N_CHIPS, q, D_MODEL),
                              dwout.reshape(N_CHIPS, q, D_MODEL), dpw4], axis=1)

    dw_mix, dw_mix16, db_mix = _in_proj_weight_grad(h, dpb, rows, "in_proj_weight_grad_mix")
    dpa, dlb, rxw_mix, rx_blob = _hgrn_backward(
        p3, d_o, states, e16, a2, lb_logits, wexp_t, masks2, dw_mix16, dblob4.astype(BF16), rows)
    dw_rec, dw_rec16, db_rec = _in_proj_weight_grad(h, dpa, rows, "in_proj_weight_grad_rec")
    dz, dnw, rxw_rec = _input_grad(dpa, dpb, w4, z, dz2, norm_w, dw_rec16, rows)

    small = jnp.concatenate([
        small[ROW_LOSS:ROW_LOSS + 1],
        dz[PAD_ROWS:PAD_ROWS + N_META],
        dnw,
        db_rec.reshape(len(SEGS_REC), D_MODEL), db_mix.reshape(len(SEGS_MIX), D_MODEL),
        dlb, jnp.zeros_like(dlb),
        small[ROW_HG_W:ROW_HG_W + 1], small[ROW_POOL_SCALE:ROW_POOL_SCALE + 1],
        small[ROW_FINAL_W:ROW_FINAL_W + 1],
        jnp.zeros((SMALL_ROWS - ROW_FINAL_W - 1, D_MODEL), F32),
    ], axis=0)
    return dz, (dw_rec, dw_mix, rxw_rec, rxw_mix), (dblob4, rx_blob), small


ANY = pl.BlockSpec(memory_space=pl.ANY)
MESH = pl.DeviceIdType.MESH


def _place():
    x, y, c = lax.axis_index("x"), lax.axis_index("y"), lax.axis_index("c")
    chips = [(1 - x, y), (x, 1 - y), (1 - x, 1 - y)]
    return x, y, c, chips


class _ShardGather:
    def __init__(self, rows):
        self.half = rows // 2

    def semaphores(self):
        return [pltpu.SemaphoreType.DMA((6,)), pltpu.SemaphoreType.DMA((6,))]

    def _copy(self, k, slot, to, send_sems, recv_sems):
        return pltpu.make_async_remote_copy(src_ref=slot, dst_ref=slot, send_sem=send_sems.at[k],
                                            recv_sem=recv_sems.at[k], device_id=to, device_id_type=MESH)

    def _half(self, ref4, chip, which):
        return ref4.at[chip, pl.ds(which * self.half, self.half), :]

    def start(self, ref4, send_sems, recv_sems, which=(0, 1, 2)):
        x, y, c, chips = _place()
        for j in which:
            cx, cy = chips[j]
            self._copy(j, self._half(ref4, 2 * x + y, c), (cx, cy, c), send_sems, recv_sems).start()

    def start_diagonal_after_neighbours(self, ref4, send_sems, recv_sems):
        x, y, c, chips = _place()
        for j in (0, 1):
            cx, cy = chips[j]
            self._copy(j, self._half(ref4, 2 * x + y, c), (cx, cy, c), send_sems, recv_sems).wait_send()
        self.start(ref4, send_sems, recv_sems, which=(2,))

    def pass_on(self, j, ref4, send_sems, recv_sems):
        x, y, c, chips = _place()
        cx, cy = chips[j]
        landed = self._half(ref4, 2 * cx + cy, c)
        self._copy(j, landed, (cx, cy, c), send_sems, recv_sems).wait_recv()
        self._copy(3 + j, landed, (x, y, 1 - c), send_sems, recv_sems).start()

    def await_sibling(self, j, ref4, send_sems, recv_sems):
        x, y, c, chips = _place()
        cx, cy = chips[j]
        self._copy(3 + j, self._half(ref4, 2 * cx + cy, 1 - c), (x, y, 1 - c), send_sems, recv_sems).wait_recv()

    def finish(self, ref4, send_sems, recv_sems, which=(0, 1, 2)):
        x, y, c, chips = _place()
        for j, (cx, cy) in enumerate(chips):
            if j in which:
                self._copy(j, self._half(ref4, 2 * x + y, c), (cx, cy, c), send_sems, recv_sems).wait_send()
            self._copy(3 + j, self._half(ref4, 2 * cx + cy, c), (x, y, 1 - c), send_sems, recv_sems).wait_send()


def _gather_meta(m4):
    def body(m_in_ref, m4_ref, send_sems, recv_sems):
        x, y, c, chips = _place()

        def copy(j, slot, to):
            return pltpu.make_async_remote_copy(src_ref=slot, dst_ref=slot, send_sem=send_sems.at[j],
                                                recv_sem=recv_sems.at[j], device_id=to, device_id_type=MESH)

        sends = [copy(j, m4_ref.at[2 * x + y], (cx, cy, c)) for j, (cx, cy) in enumerate(chips)]
        for cp in sends:
            cp.start()
        for j, (cx, cy) in enumerate(chips):
            copy(j, m4_ref.at[2 * cx + cy], (x, y, c)).wait_recv()
        for cp in sends:
            cp.wait_send()

    return pl.pallas_call(
        body, name="gather_meta",
        in_specs=[ANY], out_specs=ANY, out_shape=jax.ShapeDtypeStruct(m4.shape, m4.dtype),
        input_output_aliases={0: 0},
        scratch_shapes=[pltpu.SemaphoreType.DMA((3,)), pltpu.SemaphoreType.DMA((3,))],
    )(m4)


class _GradExchange:
    def __init__(self, segs, with_blob):
        self.segs = tuple(segs)
        self.with_blob = with_blob

    def landing_w(self):
        return jax.ShapeDtypeStruct((N_CHIPS, 2, D_MODEL // 2, D_MODEL), BF16)

    def landing_blob(self, blob16):
        return jax.ShapeDtypeStruct((N_DEV, blob16.shape[1] // 2, D_MODEL), BF16)

    def semaphores(self):
        n_send = len(self.segs) + (2 * N_CHIPS if self.with_blob else 0)
        n_recv = 2 * N_CHIPS + (N_DEV if self.with_blob else 0)
        return [pltpu.SemaphoreType.DMA((n_send,)), pltpu.SemaphoreType.DMA((n_recv,))]

    def _copies(self, dw_ref, rxw_ref, blob_ref, rxb_ref, send_sems, recv_sems):
        x, y, c = lax.axis_index("x"), lax.axis_index("y"), lax.axis_index("c")
        chip = 2 * x + y

        def relation(kx, ky, h):
            return (x ^ kx) * 4 + (y ^ ky) * 2 + (c ^ h)

        def copy(src, dst, send_k, recv_k, to):
            return functools.partial(pltpu.make_async_remote_copy, src_ref=src, dst_ref=dst,
                                     send_sem=send_sems.at[send_k], recv_sem=recv_sems.at[recv_k],
                                     device_id=to, device_id_type=MESH)

        sends, recvs = [], []
        for i, s in enumerate(self.segs):
            kx, ky = (s // 2) >> 1, (s // 2) & 1
            r = (x ^ kx) * 2 + (y ^ ky)
            sends.append((r != 0, copy(dw_ref.at[i], rxw_ref.at[r, s % 2], i, 2 * r + s % 2, (kx, ky, c))))
        for j in range(2):
            mine = [s // 2 for s in self.segs if s % 2 == j]
            if mine:
                cond = functools.reduce(lambda a, b: a | b, [chip == k for k in mine])
                for r in range(1, N_CHIPS):
                    slot = rxw_ref.at[r, j]
                    recvs.append((cond, copy(slot, slot, 0, 2 * r + j, (x, y, c))))
        if self.with_blob:
            hb = blob_ref.shape[1] // 2
            first_send, first_recv = len(self.segs), 2 * N_CHIPS
            for k in range(N_CHIPS):
                for h in range(2):
                    r = relation(k >> 1, k & 1, h)
                    sends.append((r != 0, copy(blob_ref.at[k, pl.ds(h * hb, hb), :], rxb_ref.at[r],
                                               first_send + 2 * k + h, first_recv + r, (k >> 1, k & 1, h))))
            for r in range(1, N_DEV):
                slot = rxb_ref.at[r]
                recvs.append((None, copy(slot, slot, 0, first_recv + r, (x, y, c))))
        return sends, recvs

    def start(self, *refs):
        sends, _ = self._copies(*refs)
        for cond, make in sends:
            pl.when(cond)(lambda make=make: make().start())

    def wait(self, *refs):
        sends, recvs = self._copies(*refs)
        for cond, make in sends:
            pl.when(cond)(lambda make=make: make().wait_send())
        for cond, make in recvs:
            if cond is None:
                make().wait_recv()
            else:
                pl.when(cond)(lambda make=make: make().wait_recv())


def _sum_landed(own, rx_ref):
    total = own
    for r in range(1, rx_ref.shape[0]):
        total = total + rx_ref[r, 0].astype(F32)
    return total


def _finish_w(dw_rec, dw_mix, rx_rec, rx_mix, place_arr):
    half = D_MODEL // 2
    tm = _tile(half, 256)
    n_rec = len(SEGS_REC)

    def body(place_ref, own_rec_ref, own_mix_ref, rx_rec_ref, rx_mix_ref, out_ref):
        seg = 2 * place_ref[0] + pl.program_id(0)

        @pl.when(seg < n_rec)
        def _():
            out_ref[0] = _sum_landed(own_rec_ref[0], rx_rec_ref)

        @pl.when(seg >= n_rec)
        def _():
            out_ref[0] = _sum_landed(own_mix_ref[0], rx_mix_ref)

    def own_spec(first, count):
        def index(j, i, place_ref):
            seg = 2 * place_ref[0] + j
            return (jnp.clip(seg - first, 0, count - 1), i, 0)
        return pl.BlockSpec((1, tm, D_MODEL), index)

    rx_spec = pl.BlockSpec((N_CHIPS, 1, tm, D_MODEL), lambda j, i, place_ref: (0, j, i, 0))
    return pl.pallas_call(
        body, name="finish_w",
        grid_spec=pltpu.PrefetchScalarGridSpec(
            num_scalar_prefetch=1, grid=(2, half // tm),
            in_specs=[own_spec(0, n_rec), own_spec(n_rec, len(SEGS_MIX)), rx_spec, rx_spec],
            out_specs=pl.BlockSpec((1, tm, D_MODEL), lambda j, i, place_ref: (place_ref[1], i, j))),
        out_shape=jax.ShapeDtypeStruct((2, half, 2 * D_MODEL), F32),
        compiler_params=_params(("arbitrary", "arbitrary")),
    )(place_arr, dw_rec, dw_mix, rx_rec, rx_mix)


def _finish_blob(dblob4, rx_blob, place_arr):
    n, rows, cols = rx_blob.shape
    tm = _tile(rows, 256)

    def body(place_ref, own_ref, rx_ref, out_ref):
        out_ref[0] = _sum_landed(own_ref[0, 0], rx_ref)

    return pl.pallas_call(
        body, name="finish_blob",
        grid_spec=pltpu.PrefetchScalarGridSpec(
            num_scalar_prefetch=1, grid=(rows // tm,),
            in_specs=[pl.BlockSpec((1, 1, tm, cols), lambda i, place_ref: (place_ref[0], place_ref[1], i, 0)),
                      pl.BlockSpec((n, 1, tm, cols), lambda i, place_ref: (0, 0, i, 0))],
            out_specs=pl.BlockSpec((1, tm, cols), lambda i, place_ref: (place_ref[1], i, 0))),
        out_shape=jax.ShapeDtypeStruct((2, rows, cols), F32),
        compiler_params=_params(("arbitrary",)),
    )(place_arr, dblob4.reshape(N_CHIPS, 2, rows, cols), rx_blob.reshape(n, 1, rows, cols))


def _share_finished(fw2, fb2, small):
    def body(w_in_ref, b_in_ref, small_ref, w_ref, b_ref, s_ref, bounce, local_sem, send_sems, recv_sems):
        x, y, c, _ = _place()
        sibling = (x, y, 1 - c)

        def copy(k, src, dst, to):
            return pltpu.make_async_remote_copy(src_ref=src, dst_ref=dst, send_sem=send_sems.at[k],
                                                recv_sem=recv_sems.at[k], device_id=to, device_id_type=MESH)

        sends = [copy(0, w_ref.at[c], w_ref.at[c], sibling), copy(1, b_ref.at[c], b_ref.at[c], sibling)]
        for r in range(1, N_DEV):
            peer = (x ^ ((r >> 2) & 1), y ^ ((r >> 1) & 1), c ^ (r & 1))
            sends.append(copy(1 + r, small_ref, s_ref.at[r], peer))
        for cp in sends:
            cp.start()
        for src, dst in ((small_ref, bounce), (bounce, s_ref.at[0])):
            own = pltpu.make_async_copy(src, dst, local_sem)
            own.start()
            own.wait()
        landed = [w_ref.at[1 - c], b_ref.at[1 - c]] + [s_ref.at[r] for r in range(1, N_DEV)]
        for k, slot in enumerate(landed):
            copy(k, slot, slot, (x, y, c)).wait_recv()
        for cp in sends:
            cp.wait_send()

    same = lambda a: jax.ShapeDtypeStruct(a.shape, a.dtype)
    n_sem = 2 + N_DEV - 1
    return pl.pallas_call(
        body, name="share_finished",
        in_specs=[ANY, ANY, ANY], out_specs=[ANY, ANY, ANY],
        out_shape=[same(fw2), same(fb2), jax.ShapeDtypeStruct((N_DEV,) + small.shape, F32)],
        input_output_aliases={0: 0, 1: 1},
        scratch_shapes=[pltpu.VMEM(small.shape, F32), pltpu.SemaphoreType.DMA,
                        pltpu.SemaphoreType.DMA((n_sem,)), pltpu.SemaphoreType.DMA((n_sem,))],
    )(fw2, fb2, small)


def _sum_small(slots, lb_logits, me_arr):
    def body(me_ref, slots_ref, lbl_ref, out_ref):
        me = me_ref[0]
        total = slots_ref[me]
        for d in range(1, N_DEV):
            total = total + slots_ref[d ^ me]
        out_ref[...] = total
        out_ref[ROW_LOSS:ROW_LOSS + 1, :] = jnp.broadcast_to(
            jnp.sum(total[ROW_LOSS:ROW_LOSS + 1, :], axis=-1, keepdims=True), (1, D_MODEL))
        lb = _lower_bound(lbl_ref[...])
        g0 = total[ROW_LB:ROW_LB + 1, :] * lb * (1.0 - lb)
        out_ref[ROW_LB:ROW_LB + 1, :] = g0
        out_ref[ROW_LB + 1:ROW_LB + 2, :] = -g0

    return pl.pallas_call(
        body, name="sum_small",
        grid_spec=pltpu.PrefetchScalarGridSpec(
            num_scalar_prefetch=1, grid=(1,),
            in_specs=[pl.BlockSpec((N_DEV, SMALL_ROWS, D_MODEL), lambda i, me_ref: (0, 0, 0)),
                      pl.BlockSpec((2, D_MODEL), lambda i, me_ref: (0, 0))],
            out_specs=pl.BlockSpec((SMALL_ROWS, D_MODEL), lambda i, me_ref: (0, 0))),
        out_shape=jax.ShapeDtypeStruct((SMALL_ROWS, D_MODEL), F32),
        compiler_params=_params(("arbitrary",)),
    )(me_arr, slots, lb_logits)


def _adamw(w, g, m, v):
    rows, cols = w.shape
    tm = _tile(rows, 256, mult=8) if rows % 8 == 0 else rows
    c1 = 1.0 / (1.0 - ADAM_B1 ** ADAM_STEP)
    c2 = 1.0 / (1.0 - ADAM_B2 ** ADAM_STEP)

    def body(w_ref, g_ref, m_ref, v_ref, d_ref, nm_ref, nv_ref):
        gt = g_ref[...]
        nm = ADAM_B1 * m_ref[...] + (1.0 - ADAM_B1) * gt
        nv = ADAM_B2 * v_ref[...] + (1.0 - ADAM_B2) * (gt * gt)
        nm_ref[...] = nm
        nv_ref[...] = nv
        d_ref[...] = -ADAM_LR * ((nm * c1) / (jnp.sqrt(nv * c2) + ADAM_EPS) + ADAM_WD * w_ref[...])

    blk = pl.BlockSpec((tm, cols), lambda i: (i, 0))
    sds = jax.ShapeDtypeStruct((rows, cols), F32)
    return pl.pallas_call(
        body, name="adamw",
        grid=(rows // tm,), in_specs=[blk] * 4, out_specs=[blk] * 3, out_shape=[sds] * 3,
        compiler_params=_params(("arbitrary",)),
    )(w, g, m, v)


def kernel(x, meta_tokens, norm_w, w_in, b_in, lb_logits, hg_norm_w, pool_w, pool_scale, w_down_hg, w_down_pool, w_out, final_norm_w, loss_target, m_meta_tokens, m_norm_w, m_w_in, m_b_in, m_lb_logits, m_hg_norm_w, m_pool_w, m_pool_scale, m_w_down_hg, m_w_down_pool, m_w_out, m_final_norm_w, v_meta_tokens, v_norm_w, v_w_in, v_b_in, v_lb_logits, v_hg_norm_w, v_pool_w, v_pool_scale, v_w_down_hg, v_w_down_pool, v_w_out, v_final_norm_w):
    seq = x.shape[1]
    xi, yi, ci = lax.axis_index("x"), lax.axis_index("y"), lax.axis_index("c")
    chip = 2 * xi + yi
    place_arr = jnp.stack([chip, ci]).astype(jnp.int32)
    me_arr = jnp.reshape(4 * xi + 2 * yi + ci, (1,)).astype(jnp.int32)
    q = D_MODEL // N_CHIPS

    def blob_of(wdh, wdp, wo, pw):
        return jnp.concatenate([wdh[0], wdp[0], wo[0], pw[0].reshape(-1, D_MODEL)], axis=0)

    def in_every_slot(a):
        return jnp.broadcast_to(a[None], (N_CHIPS,) + a.shape)

    meta4 = _gather_meta(in_every_slot(meta_tokens))
    meta_full = meta4.transpose(1, 0, 2).reshape(N_META, D_MODEL)
    w4 = in_every_slot(w_in[0].astype(BF16))
    blob4 = in_every_slot(blob_of(w_down_hg, w_down_pool, w_out, pool_w).astype(BF16))
    seg_order = jnp.stack([2 * (chip ^ rel) + t for rel in (0, 2, 1, 3) for t in (0, 1)]).astype(jnp.int32)

    z = jnp.concatenate([jnp.zeros((PAD_ROWS, D_MODEL), F32), meta_full, x[0]], axis=0)
    tgt = jnp.concatenate([jnp.zeros((FIRST_TOKEN_ROW, D_MODEL), F32), loss_target[0]], axis=0)
    fw2 = final_norm_w.reshape(1, D_MODEL)
    dz, w_parts, blob_parts, small = _local_step(
        z, tgt, w4, blob4, seg_order, norm_w, b_in, lb_logits, hg_norm_w, pool_scale, fw2)
    grad_x = dz[FIRST_TOKEN_ROW:][None]

    fin_w = _finish_w(*w_parts, place_arr)
    fin_b = _finish_blob(*blob_parts, place_arr)
    gw2, gb2, slots = _share_finished(fin_w, fin_b, small)
    tot = _sum_small(slots, lb_logits, me_arr)
    g_w_in = gw2.reshape(D_MODEL, 2 * D_MODEL)
    g_blob = gb2.reshape(-1, D_MODEL)

    d_win, nm_win, nv_win = _adamw(w_in[0], g_w_in, m_w_in[0], v_w_in[0])
    d_blob, nm_blob, nv_blob = _adamw(
        blob_of(w_down_hg, w_down_pool, w_out, pool_w), g_blob,
        blob_of(m_w_down_hg, m_w_down_pool, m_w_out, m_pool_w),
        blob_of(v_w_down_hg, v_w_down_pool, v_w_out, v_pool_w))
    g_meta = lax.dynamic_slice_in_dim(tot[ROW_META:ROW_META + N_META], chip * q, q, axis=1)
    d_meta, nm_meta, nv_meta = _adamw(meta_tokens, g_meta, m_meta_tokens, v_meta_tokens)

    def rows_of(nw, bi, lbl, hg, ps, fw):
        return jnp.concatenate([nw, bi.reshape(N_SEG, D_MODEL), lbl, hg, ps, fw.reshape(1, D_MODEL),
                                jnp.zeros((2, D_MODEL), F32)], axis=0)

    g_rows = jnp.concatenate([tot[ROW_NORM_W:ROW_FINAL_W + 1], jnp.zeros((2, D_MODEL), F32)], axis=0)
    d_rows, nm_rows, nv_rows = _adamw(
        rows_of(norm_w, b_in, lb_logits, hg_norm_w, pool_scale, final_norm_w), g_rows,
        rows_of(m_norm_w, m_b_in, m_lb_logits, m_hg_norm_w, m_pool_scale, m_final_norm_w),
        rows_of(v_norm_w, v_b_in, v_lb_logits, v_hg_norm_w, v_pool_scale, v_final_norm_w))

    def unblob(b):
        return (b[0:q][None], b[q:2 * q][None], b[2 * q:3 * q][None], b[3 * q:].reshape(pool_w.shape))

    def unrows(r):
        o = ROW_NORM_W
        return dict(norm_w=r[ROW_NORM_W - o:ROW_B_IN - o], b_in=r[ROW_B_IN - o:ROW_LB - o].reshape(1, -1),
                    lb_logits=r[ROW_LB - o:ROW_HG_W - o], hg_norm_w=r[ROW_HG_W - o:ROW_POOL_SCALE - o],
                    pool_scale=r[ROW_POOL_SCALE - o:ROW_FINAL_W - o], final_norm_w=r[ROW_FINAL_W - o])

    def leaves(meta_part, rows_part, win_part, blob_part):
        r = unrows(rows_part)
        wdh, wdp, wo, pw = unblob(blob_part)
        return [meta_part, r["norm_w"], win_part[None], r["b_in"], r["lb_logits"], r["hg_norm_w"], pw,
                r["pool_scale"], wdh, wdp, wo, r["final_norm_w"]]

    loss = tot[ROW_LOSS, 0]
    return (loss, grad_x,
            *leaves(g_meta, g_rows, g_w_in, g_blob),
            *leaves(d_meta, d_rows, d_win, d_blob),
            *leaves(nm_meta, nm_rows, nm_win, nm_blob),
            *leaves(nv_meta, nv_rows, nv_win, nv_blob))
```

```python
import functools

import numpy as np
import jax
import jax.numpy as jnp
from jax import lax
from jax.experimental import pallas as pl
from jax.experimental.pallas import tpu as pltpu

F32 = jnp.float32
BF16 = jnp.bfloat16

D_MODEL = 1024
N_SEG = 8
N_HEADS = 8
HEAD_DIM = 128
CHUNK = 64
N_META = 16
PAD_ROWS = CHUNK - N_META
FIRST_TOKEN_ROW = CHUNK
LEVELS = (32, 16, 8, 4, 2, 1)
N_EXP = 2 + len(LEVELS)
POOL_WINDOWS = (2, 4, 8, 16)
POOL_GDIM = D_MODEL // len(POOL_WINDOWS)
HALO = 16
LOCAL_UNROLL = 13
BACKWARD_UNROLL = 13
EPS = 1e-6
N_CHIPS = 4
N_DEV = 8
SEGS_REC = (0, 1, 2)
SEGS_MIX = (3, 4, 5, 6, 7)

ADAM_LR = 0.001
ADAM_B1 = 0.9
ADAM_B2 = 0.999
ADAM_EPS = 1e-08
ADAM_WD = 0.01
ADAM_STEP = 10

VMEM_LIMIT_BYTES = 56 * 1024 * 1024

ROW_LOSS = 0
ROW_META = 1
ROW_NORM_W = ROW_META + N_META
ROW_B_IN = ROW_NORM_W + 1
ROW_LB = ROW_B_IN + N_SEG
ROW_HG_W = ROW_LB + 2
ROW_POOL_SCALE = ROW_HG_W + 1
ROW_FINAL_W = ROW_POOL_SCALE + 1
SMALL_ROWS = 32


def _tile(total, cap, mult=16):
    best = None
    for t in range(mult, min(total, cap) + 1, mult):
        if total % t == 0:
            best = t
    assert best is not None, (total, cap, mult)
    return best


def _token_window(tm, tile_of):
    def index(*grid):
        return (pl.multiple_of(jnp.maximum(tile_of(*grid) * tm - FIRST_TOKEN_ROW, 0), HALO), 0)
    return pl.BlockSpec((pl.Element(tm), pl.Element(D_MODEL)), index)


def _padded_tile(window, head, tile):
    first = jnp.concatenate([head, pltpu.roll(window, FIRST_TOKEN_ROW, 0)[FIRST_TOKEN_ROW:]], axis=0)
    return jnp.where(tile == 0, first, window)


def _params(sem=None):
    return pltpu.CompilerParams(dimension_semantics=sem, vmem_limit_bytes=VMEM_LIMIT_BYTES)


def _dot(a, b):
    return jnp.dot(a, b, preferred_element_type=F32)


def _dot_nt(a, b):
    return lax.dot_general(a, b, (((1,), (1,)), ((), ())), preferred_element_type=F32)


def _dot_tn(a, b):
    return lax.dot_general(a, b, (((0,), (0,)), ((), ())), preferred_element_type=F32)


def _sigmoid_pair(x):
    t = jnp.exp(-jnp.abs(x))
    r = 1.0 / (1.0 + t)
    pos = x >= 0
    return jnp.where(pos, r, t * r), jnp.where(pos, t * r, r)


def _exponent_matrix():
    t = np.arange(CHUNK)[:, None]
    j = np.arange(CHUNK)[None, :]
    blocks = [j <= t, j > t]
    for m in LEVELS:
        rho = (t // (2 * m)) * (2 * m) + m
        upper = (t >= rho) & (j > rho) & (j <= t)
        lower = (t < rho) & (j > t) & (j <= rho)
        blocks.append(upper | lower)
    return np.concatenate(blocks, axis=0).astype(np.float32)


def _pair_masks():
    t = np.arange(CHUNK)[:, None]
    s = np.arange(CHUNK)[None, :]
    masks = [t == s]
    for m in LEVELS:
        same = (t // (2 * m)) == (s // (2 * m))
        masks.append(same & ((t % (2 * m)) >= m) & ((s % (2 * m)) < m))
    return np.stack(masks).astype(np.float32)


LEVEL_PAIRS = ((0, 1), (2, 3), (4, 5), (6, None))


def _paired_masks():
    m = _pair_masks()
    zero = np.zeros_like(m[0])
    return np.stack([np.concatenate([m[a], zero if b is None else m[b]], axis=1) for a, b in LEVEL_PAIRS])


def _lower_bound(lbl):
    return 1.0 / (1.0 + jnp.exp(lbl[1:2, :] - lbl[0:1, :]))


def _in_proj(tokens, head, norm_w, w4, b_in, seg_order, rows):
    tm = _tile(rows, 1040)
    nt = rows // tm
    gather = _ShardGather(w4.shape[1])

    def body(order_ref, z_ref, head_ref, nw_ref, b_ref, w_in_ref, h_ref, p_ref, w4_ref,
             h_all, w_buf, w_sem, send_sems, recv_sems):
        kk, i = pl.program_id(0), pl.program_id(1)

        @pl.when((kk == 0) & (i == 0))
        def _():
            gather.start(w4_ref, send_sems, recv_sems, which=(0, 1))

        @pl.when((kk == 2) & (i == 0))
        def _():
            gather.start_diagonal_after_neighbours(w4_ref, send_sems, recv_sems)

        @pl.when(kk == 0)
        def _():
            zt = _padded_tile(z_ref[...], head_ref[...], i)
            rstd = lax.rsqrt(jnp.mean(zt * zt, axis=-1, keepdims=True) + EPS)
            h = (zt * rstd * nw_ref[...]).astype(BF16)
            h_all[pl.ds(pl.multiple_of(i * tm, 16), tm), :] = h
            h_ref[...] = h

        @pl.when((kk == 2) & (i == 0))
        def _():
            gather.pass_on(0, w4_ref, send_sems, recv_sems)
            gather.pass_on(1, w4_ref, send_sems, recv_sems)
            gather.await_sibling(0, w4_ref, send_sems, recv_sems)

        @pl.when((kk == 4) & (i == 0))
        def _():
            gather.await_sibling(1, w4_ref, send_sems, recv_sems)

        @pl.when((kk == 5) & (i == 0))
        def _():
            gather.pass_on(2, w4_ref, send_sems, recv_sems)

        @pl.when((kk == 6) & (i == 0))
        def _():
            gather.await_sibling(2, w4_ref, send_sems, recv_sems)

        def weights(which):
            seg = order_ref[2 * (kk // 2) + which]
            return pltpu.make_async_copy(
                w4_ref.at[seg // 2, :, pl.ds(pl.multiple_of((seg % 2) * D_MODEL, D_MODEL), D_MODEL)],
                w_buf.at[which], w_sem.at[which])

        @pl.when((i == 0) & (kk % 2 == 0))
        def _():
            weights(0).start()
            weights(1).start()
            weights(0).wait()

        @pl.when((i == 0) & (kk % 2 == 1))
        def _():
            weights(1).wait()

        p_ref[0] = _dot(h_all[pl.ds(pl.multiple_of(i * tm, 16), tm), :], w_buf[kk % 2]) + b_ref[...]

        @pl.when((kk == N_SEG - 1) & (i == nt - 1))
        def _():
            gather.finish(w4_ref, send_sems, recv_sems, which=(2,))

    first_pass = lambda kk, i, order_ref: (jnp.where(kk == 0, i, nt - 1), 0)
    return pl.pallas_call(
        body, name="in_proj",
        grid_spec=pltpu.PrefetchScalarGridSpec(
            num_scalar_prefetch=1, grid=(N_SEG, nt),
            in_specs=[
                _token_window(tm, lambda kk, i, order_ref: jnp.where(kk == 0, i, nt - 1)),
                pl.BlockSpec((FIRST_TOKEN_ROW, D_MODEL), lambda kk, i, order_ref: (0, 0)),
                pl.BlockSpec((1, D_MODEL), lambda kk, i, order_ref: (0, 0)),
                pl.BlockSpec((1, D_MODEL), lambda kk, i, order_ref: (0, order_ref[kk])),
                ANY,
            ],
            out_specs=[
                pl.BlockSpec((tm, D_MODEL), first_pass),
                pl.BlockSpec((1, tm, D_MODEL), lambda kk, i, order_ref: (order_ref[kk], i, 0)),
                ANY,
            ],
            scratch_shapes=[
                pltpu.VMEM((rows, D_MODEL), BF16),
                pltpu.VMEM((2, D_MODEL, D_MODEL), BF16),
                pltpu.SemaphoreType.DMA((2,)),
            ] + gather.semaphores()),
        out_shape=[
            jax.ShapeDtypeStruct((rows, D_MODEL), BF16),
            jax.ShapeDtypeStruct((N_SEG, rows, D_MODEL), F32),
            jax.ShapeDtypeStruct(w4.shape, w4.dtype),
        ],
        input_output_aliases={5: 2},
        compiler_params=_params(("arbitrary", "arbitrary")),
    )(seg_order, tokens, head, norm_w, b_in, w4)


def _hgrn_forward(p3, lb_logits, wexp2, masks2, blob4, rows):
    n_chunks = rows // CHUNK
    cpb = _tile(n_chunks, 13, mult=1)
    rb_rows = cpb * CHUNK
    n_rb = n_chunks // cpb
    lanes = cpb * HEAD_DIM
    gather = _ShardGather(blob4.shape[1])

    def body(q_ref, fz_ref, v_ref, lbl_ref, wexp_ref, mask_ref, b_in_ref, o_ref, s_ref, e16_ref, a2_ref, b4_ref,
             st_ref, e_ref, u_ref, q_s, kk_s, v_s, qb_s, oi_s, send_sems, recv_sems):
        rb = pl.program_id(1)

        @pl.when((pl.program_id(0) == 0) & (rb == 0))
        def _():
            gather.start(b4_ref, send_sems, recv_sems)

        @pl.when(rb == 0)
        def _():
            st_ref[...] = jnp.zeros_like(st_ref)

        lb = _lower_bound(lbl_ref[...])
        row = rb * rb_rows + lax.broadcasted_iota(jnp.int32, (rb_rows, 1), 0)
        valid = row >= PAD_ROWS
        sg, sn = _sigmoid_pair(fz_ref[0])
        g = jnp.where(valid, jnp.log(lb + (1.0 - lb) * sg), 0.0)
        kk_s[...] = jnp.where(valid, (1.0 - lb) * sn, 0.0)
        q_s[...] = jnp.where(valid, q_ref[0], 0.0)
        v_s[...] = jnp.where(valid, v_ref[0], 0.0).astype(BF16)
        hi = g.astype(BF16)
        mid = (g - hi.astype(F32)).astype(BF16)
        g2 = jnp.concatenate(
            [jnp.concatenate([hi[b * CHUNK:(b + 1) * CHUNK], mid[b * CHUNK:(b + 1) * CHUNK]], axis=0)
             for b in range(cpb)], axis=1)
        e_ref[...] = jnp.exp(_dot(wexp_ref[...], g2))
        e16_ref[0, 0] = e_ref[...].astype(BF16)

        zeros16 = jnp.zeros((CHUNK, HEAD_DIM), BF16)

        def local(b, carry):
            r0 = pl.multiple_of(b * CHUNK, CHUNK)
            l0 = pl.multiple_of(b * HEAD_DIM, HEAD_DIM)
            q = q_s[pl.ds(r0, CHUNK), :]
            kk = kk_s[pl.ds(r0, CHUNK), :]
            v16 = v_s[pl.ds(r0, CHUNK), :]

            def scaled(entry):
                if entry == 0:
                    return q.astype(BF16), kk.astype(BF16)
                e_m = e_ref[(1 + entry) * CHUNK:(2 + entry) * CHUNK, pl.ds(l0, HEAD_DIM)]
                return (q * e_m).astype(BF16), (kk * e_m).astype(BF16)

            a2 = jnp.zeros((CHUNK, 2 * CHUNK), F32)
            for p, (ea, eb) in enumerate(LEVEL_PAIRS):
                qa, ka = scaled(ea)
                if eb is None:
                    prod = _dot_nt(qa, jnp.concatenate([ka, zeros16], axis=0))
                else:
                    qb_, kb_ = scaled(eb)
                    rhs = jnp.concatenate([jnp.concatenate([ka, zeros16], axis=1),
                                           jnp.concatenate([zeros16, kb_], axis=1)], axis=0)
                    prod = _dot_nt(jnp.concatenate([qa, qb_], axis=1), rhs)
                a2 = a2 + mask_ref[p] * prod
            a2_16 = a2.astype(BF16)
            a2_ref[pl.ds(r0, CHUNK), :] = a2_16
            oi_s[pl.ds(r0, CHUNK), :] = _dot(a2_16, jnp.concatenate([v16, v16], axis=0))
            e_b = e_ref[0:CHUNK, pl.ds(l0, HEAD_DIM)]
            e_c = e_ref[CHUNK:2 * CHUNK, pl.ds(l0, HEAD_DIM)]
            qb_s[pl.ds(r0, CHUNK), :] = (q * e_b).astype(BF16)
            u_ref[b] = _dot_tn(v16, (kk * e_c).astype(BF16))
            return carry

        lax.fori_loop(0, cpb, local, 0, unroll=LOCAL_UNROLL)

        def recur(b, st):
            l0 = pl.multiple_of(b * HEAD_DIM, HEAD_DIM)
            s_ref[0, b] = st
            return st * e_ref[CHUNK - 1:CHUNK, pl.ds(l0, HEAD_DIM)] + u_ref[b]

        st_ref[...] = lax.fori_loop(0, cpb, recur, st_ref[...])

        def inter(b, carry):
            r0 = pl.multiple_of(b * CHUNK, CHUNK)
            o_ref[pl.ds(r0, CHUNK), :] = oi_s[pl.ds(r0, CHUNK), :] + _dot_nt(
                qb_s[pl.ds(r0, CHUNK), :], s_ref[0, b].astype(BF16))
            return carry

        lax.fori_loop(0, cpb, inter, 0, unroll=LOCAL_UNROLL)

        @pl.when((pl.program_id(0) == N_HEADS // 2) & (rb == 0))
        def _():
            for j in range(N_CHIPS - 1):
                gather.pass_on(j, b4_ref, send_sems, recv_sems)

        @pl.when((pl.program_id(0) == N_HEADS - 1) & (rb == n_rb - 1))
        def _():
            for j in range(N_CHIPS - 1):
                gather.await_sibling(j, b4_ref, send_sems, recv_sems)
            gather.finish(b4_ref, send_sems, recv_sems)

    head_block = lambda seg: pl.BlockSpec((1, rb_rows, HEAD_DIM), lambda h, r: (seg, r, h))
    return pl.pallas_call(
        body, name="hgrn_forward",
        grid=(N_HEADS, n_rb),
        in_specs=[
            head_block(0), head_block(1), head_block(2),
            pl.BlockSpec((2, HEAD_DIM), lambda h, r: (0, h)),
            pl.BlockSpec((N_EXP * CHUNK, 2 * CHUNK), lambda h, r: (0, 0)),
            pl.BlockSpec((len(LEVEL_PAIRS), CHUNK, 2 * CHUNK), lambda h, r: (0, 0, 0)),
            ANY,
        ],
        out_specs=[
            pl.BlockSpec((rb_rows, HEAD_DIM), lambda h, r: (r, h)),
            pl.BlockSpec((1, cpb, HEAD_DIM, HEAD_DIM), lambda h, r: (h, r, 0, 0)),
            pl.BlockSpec((1, 1, N_EXP * CHUNK, lanes), lambda h, r: (h, r, 0, 0)),
            pl.BlockSpec((rb_rows, HEAD_DIM), lambda h, r: (r, h)),
            ANY,
        ],
        out_shape=[
            jax.ShapeDtypeStruct((rows, D_MODEL), F32),
            jax.ShapeDtypeStruct((N_HEADS, n_chunks, HEAD_DIM, HEAD_DIM), F32),
            jax.ShapeDtypeStruct((N_HEADS, n_rb, N_EXP * CHUNK, lanes), BF16),
            jax.ShapeDtypeStruct((rows, D_MODEL), BF16),
            jax.ShapeDtypeStruct(blob4.shape, blob4.dtype),
        ],
        input_output_aliases={6: 4},
        scratch_shapes=[
            pltpu.VMEM((HEAD_DIM, HEAD_DIM), F32),
            pltpu.VMEM((N_EXP * CHUNK, lanes), F32),
            pltpu.VMEM((cpb, HEAD_DIM, HEAD_DIM), F32),
            pltpu.VMEM((rb_rows, HEAD_DIM), F32),
            pltpu.VMEM((rb_rows, HEAD_DIM), F32),
            pltpu.VMEM((rb_rows, HEAD_DIM), BF16),
            pltpu.VMEM((rb_rows, HEAD_DIM), BF16),
            pltpu.VMEM((rb_rows, HEAD_DIM), F32),
        ] + gather.semaphores(),
        compiler_params=_params(("arbitrary", "arbitrary")),
    )(p3, p3, p3, lb_logits, wexp2, masks2, blob4)


def _hgrn_backward(p3, d_o, states, e16, a2, lb_logits, wexp_t, masks2, dw16, blob16, rows):
    n_chunks = rows // CHUNK
    cpb = _tile(n_chunks, 13, mult=1)
    rb_rows = cpb * CHUNK
    n_rb = n_chunks // cpb
    lanes = cpb * HEAD_DIM
    exchange = _GradExchange(SEGS_MIX, with_blob=True)

    def body(q_ref, fz_ref, v_ref, do_ref, s_ref, e_ref, a2_ref, lbl_ref, wexpt_ref, mask_ref, dw_ref, blob_ref,
             dp_ref, dlb_ref, rxw_ref, rxb_ref,
             dst_ref, g_ref, dsn_ref, q_s, kk_s, v_s, do_s, dq_s, dkk_s, dg_s, dx_s, send_sems, recv_sems):
        step = pl.program_id(1)
        rb = n_rb - 1 - step

        @pl.when((pl.program_id(0) == 0) & (step == 0))
        def _():
            exchange.start(dw_ref, rxw_ref, blob_ref, rxb_ref, send_sems, recv_sems)

        @pl.when(step == 0)
        def _():
            dst_ref[...] = jnp.zeros_like(dst_ref)
            dlb_ref[...] = jnp.zeros_like(dlb_ref)

        lb = _lower_bound(lbl_ref[...])
        row = rb * rb_rows + lax.broadcasted_iota(jnp.int32, (rb_rows, 1), 0)
        valid = row >= PAD_ROWS
        sg, sn = _sigmoid_pair(fz_ref[0])
        f = lb + (1.0 - lb) * sg
        g = jnp.where(valid, jnp.log(f), 0.0)
        kk_s[...] = jnp.where(valid, (1.0 - lb) * sn, 0.0)
        q_s[...] = jnp.where(valid, q_ref[0], 0.0)
        v_s[...] = jnp.where(valid, v_ref[0], 0.0).astype(BF16)
        do_s[...] = do_ref[...].astype(BF16)
        e_last_all = jnp.exp(jnp.concatenate(
            [jnp.sum(g[b * CHUNK:(b + 1) * CHUNK], axis=0, keepdims=True) for b in range(cpb)], axis=0))
        last_row = lax.broadcasted_iota(jnp.int32, (CHUNK, 1), 0) == CHUNK - 1
        zeros16 = jnp.zeros((CHUNK, HEAD_DIM), BF16)

        def factor(block, l0):
            return e_ref[0, 0, block * CHUNK:(block + 1) * CHUNK, pl.ds(l0, HEAD_DIM)].astype(F32)

        def contribution(b, carry):
            r0 = pl.multiple_of(b * CHUNK, CHUNK)
            l0 = pl.multiple_of(b * HEAD_DIM, HEAD_DIM)
            qb16 = (q_s[pl.ds(r0, CHUNK), :] * factor(0, l0)).astype(BF16)
            g_ref[b] = _dot_tn(do_s[pl.ds(r0, CHUNK), :], qb16)
            return carry

        lax.fori_loop(0, cpb, contribution, 0, unroll=LOCAL_UNROLL)

        cur = dst_ref[...]
        for b in reversed(range(cpb)):
            dsn_ref[b] = cur
            cur = cur * e_last_all[b:b + 1, :] + g_ref[b]
        dst_ref[...] = cur

        def local(b, carry):
            r0 = pl.multiple_of(b * CHUNK, CHUNK)
            l0 = pl.multiple_of(b * HEAD_DIM, HEAD_DIM)
            q = q_s[pl.ds(r0, CHUNK), :]
            kk = kk_s[pl.ds(r0, CHUNK), :]
            v16 = v_s[pl.ds(r0, CHUNK), :]
            do16 = do_s[pl.ds(r0, CHUNK), :]
            st = s_ref[0, b]
            dsn = dsn_ref[b]
            dsn16 = dsn.astype(BF16)
            e_b, e_c = factor(0, l0), factor(1, l0)
            qb, kc = q * e_b, kk * e_c

            t = _dot_tn(a2_ref[pl.ds(r0, CHUNK), :], do16)
            dv = t[0:CHUNK] + t[CHUNK:2 * CHUNK] + _dot_nt(kc.astype(BF16), dsn16)
            dp_ref[2, pl.ds(r0, CHUNK), :] = dv.astype(BF16)
            da2 = _dot_nt(do16, jnp.concatenate([v16, v16], axis=0))
            dqb = _dot(do16, st.astype(BF16))
            dkc = _dot(v16, dsn16)
            de = jnp.sum(dsn * st, axis=0, keepdims=True) * e_b[CHUNK - 1:CHUNK, :]
            dq = e_b * dqb
            dkk = e_c * dkc
            dx_s[0:CHUNK, pl.ds(l0, HEAD_DIM)] = (qb * dqb + jnp.where(last_row, de, 0.0)).astype(BF16)
            dx_s[CHUNK:2 * CHUNK, pl.ds(l0, HEAD_DIM)] = (kc * dkc).astype(BF16)

            def scaled(entry):
                if entry == 0:
                    return q, kk, None
                e_m = factor(1 + entry, l0)
                return q * e_m, kk * e_m, e_m

            for p, (ea, eb) in enumerate(LEVEL_PAIRS):
                dm = mask_ref[p] * da2
                dm_t = dm.T.astype(BF16)
                qa, ka, e_a = scaled(ea)
                if eb is None:
                    rhs_k = jnp.concatenate([jnp.concatenate([ka.astype(BF16), zeros16], axis=1),
                                             jnp.concatenate([zeros16, zeros16], axis=1)], axis=0)
                else:
                    qb_, kb_, e_bb = scaled(eb)
                    rhs_k = jnp.concatenate([jnp.concatenate([ka.astype(BF16), zeros16], axis=1),
                                             jnp.concatenate([zeros16, kb_.astype(BF16)], axis=1)], axis=0)
                dq2 = _dot(dm.astype(BF16), rhs_k)
                parts = [(ea, qa, ka, e_a, dq2[:, :HEAD_DIM], _dot(dm_t[0:CHUNK], qa.astype(BF16)))]
                if eb is not None:
                    parts.append((eb, qb_, kb_, e_bb, dq2[:, HEAD_DIM:],
                                  _dot(dm_t[CHUNK:2 * CHUNK], qb_.astype(BF16))))
                for entry, q_m, k_m, e_m, dq_m, dk_m in parts:
                    if entry == 0:
                        dq = dq + dq_m
                        dkk = dkk + dk_m
                    else:
                        dq = dq + e_m * dq_m
                        dkk = dkk + e_m * dk_m
                        dx_s[(1 + entry) * CHUNK:(2 + entry) * CHUNK, pl.ds(l0, HEAD_DIM)] = (
                            q_m * dq_m + k_m * dk_m).astype(BF16)
            dq_s[pl.ds(r0, CHUNK), :] = dq
            dkk_s[pl.ds(r0, CHUNK), :] = dkk
            return carry

        lax.fori_loop(0, cpb, local, 0, unroll=BACKWARD_UNROLL)

        dg_all = _dot(wexpt_ref[...], dx_s[...])
        for b in range(cpb):
            dg_s[b * CHUNK:(b + 1) * CHUNK, :] = dg_all[:, b * HEAD_DIM:(b + 1) * HEAD_DIM]
        t = jnp.where(valid, dg_s[...] / f - dkk_s[...], 0.0)
        dlb_ref[...] += jnp.sum(sn * t, axis=0, keepdims=True)
        dp_ref[0] = jnp.where(valid, dq_s[...], 0.0).astype(BF16)
        dp_ref[1] = ((1.0 - lb) * sg * sn * t).astype(BF16)

        @pl.when((pl.program_id(0) == N_HEADS - 1) & (step == n_rb - 1))
        def _():
            exchange.wait(dw_ref, rxw_ref, blob_ref, rxb_ref, send_sems, recv_sems)

    head_block = lambda seg: pl.BlockSpec((1, rb_rows, HEAD_DIM), lambda h, s: (seg, n_rb - 1 - s, h))
    row_block = pl.BlockSpec((rb_rows, HEAD_DIM), lambda h, s: (n_rb - 1 - s, h))
    return pl.pallas_call(
        body, name="hgrn_backward",
        grid=(N_HEADS, n_rb),
        in_specs=[
            head_block(0), head_block(1), head_block(2),
            row_block,
            pl.BlockSpec((1, cpb, HEAD_DIM, HEAD_DIM), lambda h, s: (h, n_rb - 1 - s, 0, 0)),
            pl.BlockSpec((1, 1, N_EXP * CHUNK, lanes), lambda h, s: (h, n_rb - 1 - s, 0, 0)),
            row_block,
            pl.BlockSpec((2, HEAD_DIM), lambda h, s: (0, h)),
            pl.BlockSpec((CHUNK, N_EXP * CHUNK), lambda h, s: (0, 0)),
            pl.BlockSpec((len(LEVEL_PAIRS), CHUNK, 2 * CHUNK), lambda h, s: (0, 0, 0)),
            ANY, ANY,
        ],
        out_specs=[
            pl.BlockSpec((3, rb_rows, HEAD_DIM), lambda h, s: (0, n_rb - 1 - s, h)),
            pl.BlockSpec((1, HEAD_DIM), lambda h, s: (0, h)),
            ANY, ANY,
        ],
        out_shape=[
            jax.ShapeDtypeStruct((3, rows, D_MODEL), BF16),
            jax.ShapeDtypeStruct((1, D_MODEL), F32),
            exchange.landing_w(), exchange.landing_blob(blob16),
        ],
        scratch_shapes=[
            pltpu.VMEM((HEAD_DIM, HEAD_DIM), F32),
            pltpu.VMEM((cpb, HEAD_DIM, HEAD_DIM), F32),
            pltpu.VMEM((cpb, HEAD_DIM, HEAD_DIM), F32),
            pltpu.VMEM((rb_rows, HEAD_DIM), F32),
            pltpu.VMEM((rb_rows, HEAD_DIM), F32),
            pltpu.VMEM((rb_rows, HEAD_DIM), BF16),
            pltpu.VMEM((rb_rows, HEAD_DIM), BF16),
            pltpu.VMEM((rb_rows, HEAD_DIM), F32),
            pltpu.VMEM((rb_rows, HEAD_DIM), F32),
            pltpu.VMEM((rb_rows, HEAD_DIM), F32),
            pltpu.VMEM((N_EXP * CHUNK, lanes), BF16),
        ] + exchange.semaphores(),
        compiler_params=_params(("arbitrary", "arbitrary")),
    )(p3, p3, p3, d_o, states, e16, a2, lb_logits, wexp_t, masks2, dw16, blob16)


def _sigmoid(x):
    return 1.0 / (1.0 + jnp.exp(-x))


def _silu_and_grad(x):
    s = _sigmoid(x)
    return x * s, s * (1.0 + x * (1.0 - s))


def _window_sum(ext, width, forward_looking):
    n = ext.shape[0]
    s = ext
    step = 1
    while step < width:
        s = s + pltpu.roll(s, (n - step) if forward_looking else step, 0)
        step *= 2
    return s


def _mixers(o, p3, tokens, head, tgt, wdh, wdp, wout, poolw, hg_w, pool_scale, final_w, rows):
    tm = _tile(rows, 208)
    nt = rows // tm
    halo_blocks = tm // HALO
    n_grp = len(POOL_WINDOWS)

    def body(o_ref, ghg_ref, u_ref, gpl_ref, mhg_ref, mpl_ref, uh_ref, z_ref, t_ref,
             wdh_ref, wdp_ref, wout_ref, pw_ref, hgw_ref, ps_ref, fw_ref, head_ref,
             do_ref, dz2_ref, dp_ref, dwdh_ref, dwdp_ref, dwout_ref, dpw_ref, small_ref, carry_ref):
        step = pl.program_id(0)
        tile = nt - 1 - step

        @pl.when(step == 0)
        def _():
            dwdh_ref[...] = jnp.zeros_like(dwdh_ref)
            dwdp_ref[...] = jnp.zeros_like(dwdp_ref)
            dwout_ref[...] = jnp.zeros_like(dwout_ref)
            dpw_ref[...] = jnp.zeros_like(dpw_ref)
            small_ref[...] = jnp.zeros_like(small_ref)
            carry_ref[...] = jnp.zeros_like(carry_ref)

        row = tile * tm + lax.broadcasted_iota(jnp.int32, (tm, 1), 0)
        real = row >= PAD_ROWS
        pos1 = jnp.maximum(row - PAD_ROWS + 1, 1).astype(F32)

        u = jnp.where(real, u_ref[0], 0.0)
        halo_row = tile * tm - HALO + lax.broadcasted_iota(jnp.int32, (HALO, 1), 0)
        uh = jnp.where(halo_row >= PAD_ROWS, uh_ref[0], 0.0)
        ext = jnp.concatenate([uh, u], axis=0)
        pooled, inv_cnt, mixed = [], [], []
        for g, w in enumerate(POOL_WINDOWS):
            cols = slice(g * POOL_GDIM, (g + 1) * POOL_GDIM)
            inv = 1.0 / jnp.minimum(pos1, float(w))
            ws = _window_sum(ext[:, cols], w, False)[HALO:]
            pg = (ws * inv - u[:, cols]).astype(BF16)
            pooled.append(pg)
            inv_cnt.append(inv)
            mixed.append(_dot(pg, pw_ref[g]))
        mixed = jnp.concatenate(mixed, axis=1)
        gpl = gpl_ref[0]
        sp, dsp = _silu_and_grad(gpl)
        ps = ps_ref[...]
        a_pool = (mixed * ps * sp).astype(BF16)
        y_pool = _dot(a_pool, wdp_ref[...])

        o = o_ref[...]
        o_hat, rstd_h = [], []
        for h in range(N_HEADS):
            oh = o[:, h * HEAD_DIM:(h + 1) * HEAD_DIM]
            r = lax.rsqrt(jnp.mean(oh * oh, axis=-1, keepdims=True) + EPS)
            rstd_h.append(r)
            o_hat.append(oh * r)
        o_hat = jnp.concatenate(o_hat, axis=1)
        hgw = hgw_ref[...]
        o_n = o_hat * hgw
        ghg = ghg_ref[0]
        sh, dsh = _silu_and_grad(ghg)
        a_hg = (o_n * sh).astype(BF16)
        y_hg = _dot(a_hg, wdh_ref[...])

        s_mh = _sigmoid(mhg_ref[0])
        s_mp = _sigmoid(mpl_ref[0])
        merged = (s_mh * y_hg + s_mp * y_pool).astype(BF16)
        z2 = _padded_tile(z_ref[...], head_ref[...], tile) + _dot(merged, wout_ref[...])
        rstd2 = lax.rsqrt(jnp.mean(z2 * z2, axis=-1, keepdims=True) + EPS)
        zh = z2 * rstd2
        fw = fw_ref[...]
        target = _padded_tile(t_ref[...], jnp.zeros((FIRST_TOKEN_ROW, D_MODEL), F32), tile)
        err = jnp.where(row >= FIRST_TOKEN_ROW, zh * fw - target, 0.0)
        small_ref[ROW_LOSS:ROW_LOSS + 1, :] += jnp.sum(err * err, axis=0, keepdims=True) * (0.5 / D_MODEL)
        dy = err * (1.0 / D_MODEL)

        small_ref[ROW_FINAL_W:ROW_FINAL_W + 1, :] += jnp.sum(dy * zh, axis=0, keepdims=True)
        uu = dy * fw
        dz2 = rstd2 * (uu - zh * jnp.mean(uu * zh, axis=-1, keepdims=True))
        dz2_ref[...] = dz2
        dz2_16 = dz2.astype(BF16)
        dmerged = _dot_nt(dz2_16, wout_ref[...])
        dwout_ref[...] += _dot_tn(merged, dz2_16)
        dy_hg = (s_mh * dmerged).astype(BF16)
        dy_pool = (s_mp * dmerged).astype(BF16)
        dp_ref[3] = (dmerged * y_hg * s_mh * (1.0 - s_mh)).astype(BF16)
        dp_ref[4] = (dmerged * y_pool * s_mp * (1.0 - s_mp)).astype(BF16)

        da_hg = _dot_nt(dy_hg, wdh_ref[...])
        dwdh_ref[...] += _dot_tn(a_hg, dy_hg)
        dp_ref[0] = (da_hg * o_n * dsh).astype(BF16)
        do_n = da_hg * sh
        small_ref[ROW_HG_W:ROW_HG_W + 1, :] += jnp.sum(do_n * o_hat, axis=0, keepdims=True)
        d_hat = do_n * hgw
        for h in range(N_HEADS):
            cols = slice(h * HEAD_DIM, (h + 1) * HEAD_DIM)
            dh_, oh_ = d_hat[:, cols], o_hat[:, cols]
            do_ref[:, cols] = rstd_h[h] * (dh_ - oh_ * jnp.mean(dh_ * oh_, axis=-1, keepdims=True))

        da_pool = _dot_nt(dy_pool, wdp_ref[...])
        dwdp_ref[...] += _dot_tn(a_pool, dy_pool)
        small_ref[ROW_POOL_SCALE:ROW_POOL_SCALE + 1, :] += jnp.sum(da_pool * mixed * sp, axis=0, keepdims=True)
        dp_ref[2] = (da_pool * mixed * ps * dsp).astype(BF16)
        dmixed = (da_pool * ps * sp).astype(BF16)
        carry = carry_ref[...]
        du, new_carry = [], []
        for g, w in enumerate(POOL_WINDOWS):
            cols = slice(g * POOL_GDIM, (g + 1) * POOL_GDIM)
            dmg = dmixed[:, cols]
            dpooled = _dot_nt(dmg, pw_ref[g])
            dpw_ref[g] += _dot_tn(pooled[g], dmg)
            dps = dpooled * inv_cnt[g]
            ext_b = jnp.concatenate([dps, carry[:, cols]], axis=0)
            du.append(_window_sum(ext_b, w, True)[:tm] - dpooled)
            new_carry.append(dps[:HALO])
        dp_ref[1] = jnp.where(real, jnp.concatenate(du, axis=1), 0.0).astype(BF16)
        carry_ref[...] = jnp.concatenate(new_carry, axis=1)

    row_block = pl.BlockSpec((tm, D_MODEL), lambda s: (nt - 1 - s, 0))
    seg_block = lambda seg: pl.BlockSpec((1, tm, D_MODEL), lambda s: (seg, nt - 1 - s, 0))
    whole = pl.BlockSpec(memory_space=pltpu.VMEM)
    return pl.pallas_call(
        body, name="mixers",
        grid=(nt,),
        in_specs=[
            row_block, seg_block(3), seg_block(4), seg_block(5), seg_block(6), seg_block(7),
            pl.BlockSpec((1, HALO, D_MODEL),
                         lambda s: (4, jnp.maximum((nt - 1 - s) * halo_blocks - 1, 0), 0)),
            _token_window(tm, lambda s: nt - 1 - s), _token_window(tm, lambda s: nt - 1 - s),
            whole, whole, whole, whole, whole, whole, whole, whole,
        ],
        out_specs=[
            row_block, row_block,
            pl.BlockSpec((5, tm, D_MODEL), lambda s: (0, nt - 1 - s, 0)),
            whole, whole, whole, whole, whole,
        ],
        out_shape=[
            jax.ShapeDtypeStruct((rows, D_MODEL), F32),
            jax.ShapeDtypeStruct((rows, D_MODEL), F32),
            jax.ShapeDtypeStruct((5, rows, D_MODEL), BF16),
            jax.ShapeDtypeStruct((D_MODEL, D_MODEL), F32),
            jax.ShapeDtypeStruct((D_MODEL, D_MODEL), F32),
            jax.ShapeDtypeStruct((D_MODEL, D_MODEL), F32),
            jax.ShapeDtypeStruct((n_grp, POOL_GDIM, POOL_GDIM), F32),
            jax.ShapeDtypeStruct((SMALL_ROWS, D_MODEL), F32),
        ],
        scratch_shapes=[pltpu.VMEM((HALO, D_MODEL), F32)],
        compiler_params=_params(("arbitrary",)),
    )(o, p3, p3, p3, p3, p3, p3, tokens, tgt, wdh, wdp, wout, poolw, hg_w, pool_scale, final_w, head)


def _seg_specs(tm, row_of, seg_of):
    def spec_a(*g):
        k = seg_of(*g)
        return (jnp.minimum(k, 2), jnp.where(k < 3, row_of(*g), 0), 0)

    def spec_b(*g):
        k = seg_of(*g)
        return (jnp.maximum(k - 3, 0), jnp.where(k >= 3, row_of(*g), 0), 0)

    return pl.BlockSpec((1, tm, D_MODEL), spec_a), pl.BlockSpec((1, tm, D_MODEL), spec_b)


def _in_proj_weight_grad(h, dp, rows, name):
    n_seg = dp.shape[0]
    tm = _tile(rows, 1040)
    nt = rows // tm
    half = D_MODEL // 2

    def body(h_ref, dp_ref, part_ref, part16_ref, db_ref, acc_ref, bacc_ref, stage_ref, land_ref,
             send_sems, recv_sems):
        k, i = pl.program_id(0), pl.program_id(1)
        x, y, c = lax.axis_index("x"), lax.axis_index("y"), lax.axis_index("c")

        def to_sibling(seg):
            return pltpu.make_async_remote_copy(
                src_ref=stage_ref.at[seg], dst_ref=land_ref.at[seg], send_sem=send_sems.at[seg],
                recv_sem=recv_sems.at[seg], device_id=(x, y, 1 - c), device_id_type=MESH)

        @pl.when(i == 0)
        def _():
            acc_ref[...] = jnp.zeros_like(acc_ref)
            bacc_ref[...] = jnp.zeros_like(bacc_ref)

        dpt = dp_ref[0]
        acc_ref[...] += _dot_tn(h_ref[...], dpt)
        bacc_ref[...] += jnp.sum(dpt.astype(F32), axis=0, keepdims=True)

        @pl.when(i == nt - 1)
        def _():
            db_ref[0] = bacc_ref[...]
            part_ref[k] = acc_ref[pl.ds(pl.multiple_of(c * half, half), half), :]
            stage_ref[k] = acc_ref[pl.ds(pl.multiple_of((1 - c) * half, half), half), :].astype(BF16)
            to_sibling(k).start()

        @pl.when((k == n_seg - 1) & (i == nt - 1))
        def _():
            for seg in range(n_seg):
                to_sibling(seg).wait_recv()
                total = part_ref[seg] + land_ref[seg].astype(F32)
                part_ref[seg] = total
                part16_ref[seg] = total.astype(BF16)
            for seg in range(n_seg):
                to_sibling(seg).wait_send()

    whole = pl.BlockSpec(memory_space=pltpu.VMEM)
    return pl.pallas_call(
        body, name=name,
        grid=(n_seg, nt),
        in_specs=[pl.BlockSpec((tm, D_MODEL), lambda k, i: (i, 0)),
                  pl.BlockSpec((1, tm, D_MODEL), lambda k, i: (k, i, 0))],
        out_specs=[whole, whole, pl.BlockSpec((1, 1, D_MODEL), lambda k, i: (k, 0, 0))],
        out_shape=[
            jax.ShapeDtypeStruct((n_seg, half, D_MODEL), F32),
            jax.ShapeDtypeStruct((n_seg, half, D_MODEL), BF16),
            jax.ShapeDtypeStruct((n_seg, 1, D_MODEL), F32),
        ],
        scratch_shapes=[
            pltpu.VMEM((D_MODEL, D_MODEL), F32), pltpu.VMEM((1, D_MODEL), F32),
            pltpu.VMEM((n_seg, half, D_MODEL), BF16),
            pltpu.VMEM((n_seg, half, D_MODEL), BF16),
            pltpu.SemaphoreType.DMA((n_seg,)), pltpu.SemaphoreType.DMA((n_seg,)),
        ],
        compiler_params=_params(("arbitrary", "arbitrary")),
    )(h, dp)


def _input_grad(dpa, dpb, w4, tokens, head, dz2, norm_w, dw16, rows):
    tm = _tile(rows, 1040)
    nt = rows // tm
    exchange = _GradExchange(SEGS_REC, with_blob=False)

    def body(dpa_ref, dpb_ref, w_ref, z_ref, head_ref, dz2_ref, nw_ref, dw_ref, dz_ref, dnw_ref, rxw_ref,
             acc_ref, send_sems, recv_sems):
        i, k = pl.program_id(0), pl.program_id(1)

        @pl.when((i == 0) & (k == 0))
        def _():
            exchange.start(dw_ref, rxw_ref, None, None, send_sems, recv_sems)
            dnw_ref[...] = jnp.zeros_like(dnw_ref)

        @pl.when((i == nt - 1) & (k == N_SEG - 1))
        def _():
            exchange.wait(dw_ref, rxw_ref, None, None, send_sems, recv_sems)

        @pl.when(k == 0)
        def _():
            acc_ref[...] = jnp.zeros_like(acc_ref)

        @pl.when(k < 3)
        def _():
            acc_ref[...] += _dot_nt(dpa_ref[0], w_ref[0])

        @pl.when(k >= 3)
        def _():
            acc_ref[...] += _dot_nt(dpb_ref[0], w_ref[0])

        @pl.when(k == N_SEG - 1)
        def _():
            zt = _padded_tile(z_ref[...], head_ref[...], i)
            rstd = lax.rsqrt(jnp.mean(zt * zt, axis=-1, keepdims=True) + EPS)
            zh = zt * rstd
            dh = acc_ref[...]
            dnw_ref[...] += jnp.sum(dh * zh, axis=0, keepdims=True)
            uu = dh * nw_ref[...]
            dz_ref[...] = dz2_ref[...] + rstd * (uu - zh * jnp.mean(uu * zh, axis=-1, keepdims=True))

    spec_a, spec_b = _seg_specs(tm, lambda i, k: i, lambda i, k: k)
    last_only = pl.BlockSpec((tm, D_MODEL), lambda i, k: (jnp.where(k == N_SEG - 1, i, 0), 0))
    return pl.pallas_call(
        body, name="input_grad",
        grid=(nt, N_SEG),
        in_specs=[
            spec_a, spec_b,
            pl.BlockSpec((1, D_MODEL, D_MODEL), lambda i, k: (k // 2, 0, k % 2)),
            _token_window(tm, lambda i, k: jnp.where(k == N_SEG - 1, i, 0)),
            pl.BlockSpec((FIRST_TOKEN_ROW, D_MODEL), lambda i, k: (0, 0)),
            last_only,
            pl.BlockSpec((1, D_MODEL), lambda i, k: (0, 0)),
            ANY,
        ],
        out_specs=[
            pl.BlockSpec((tm, D_MODEL), lambda i, k: (i, 0)),
            pl.BlockSpec((1, D_MODEL), lambda i, k: (0, 0)),
            ANY,
        ],
        out_shape=[
            jax.ShapeDtypeStruct((rows, D_MODEL), F32),
            jax.ShapeDtypeStruct((1, D_MODEL), F32),
            exchange.landing_w(),
        ],
        scratch_shapes=[pltpu.VMEM((tm, D_MODEL), F32)] + exchange.semaphores(),
        compiler_params=_params(("arbitrary", "arbitrary")),
    )(dpa, dpb, w4, tokens, head, dz2, norm_w, dw16)


def _local_step(tokens, head, tgt, w4, blob4, seg_order, norm_w, b_in, lb_logits, hg_w, pool_scale, final_w):
    rows = FIRST_TOKEN_ROW + tokens.shape[0]
    q = D_MODEL // N_CHIPS
    n_grp = len(POOL_WINDOWS)
    pg = POOL_GDIM // N_CHIPS

    wexp2 = jnp.asarray(np.tile(_exponent_matrix(), (1, 2)), BF16)
    wexp_t = jnp.asarray(_exponent_matrix().T, BF16)
    masks2 = jnp.asarray(_paired_masks(), F32)

    h, p3, w4 = _in_proj(tokens, head, norm_w, w4, b_in, seg_order, rows)
    o, states, e16, a2, blob4 = _hgrn_forward(p3, lb_logits, wexp2, masks2, blob4, rows)
    wdh = blob4[:, 0:q].reshape(D_MODEL, D_MODEL)
    wdp = blob4[:, q:2 * q].reshape(D_MODEL, D_MODEL)
    wout = blob4[:, 2 * q:3 * q].reshape(D_MODEL, D_MODEL)
    poolw = blob4[:, 3 * q:].reshape(N_CHIPS, n_grp, pg, POOL_GDIM).transpose(1, 0, 2, 3)
    poolw = poolw.reshape(n_grp, POOL_GDIM, POOL_GDIM)
    d_o, dz2, dpb, dwdh, dwdp, dwout, dpw, small = _mixers(
        o, p3, tokens, head, tgt, wdh, wdp, wout, poolw, hg_w, pool_scale, final_w, rows)
    dpw4 = dpw.reshape(n_grp, N_CHIPS, pg, POOL_GDIM).transpose(1, 0, 2, 3)
    dpw4 = dpw4.reshape(N_CHIPS, n_grp * pg * POOL_GDIM // D_MODEL, D_MODEL)
    dblob4 = jnp.concatenate([dwdh.reshape(N_CHIPS, q, D_MODEL), dwdp.reshape(N_CHIPS, q, D_MODEL),
                              dwout.reshape(N_CHIPS, q, D_MODEL), dpw4], axis=1)

    dw_mix, dw_mix16, db_mix = _in_proj_weight_grad(h, dpb, rows, "in_proj_weight_grad_mix")
    dpa, dlb, rxw_mix, rx_blob = _hgrn_backward(
        p3, d_o, states, e16, a2, lb_logits, wexp_t, masks2, dw_mix16, dblob4.astype(BF16), rows)
    dw_rec, dw_rec16, db_rec = _in_proj_weight_grad(h, dpa, rows, "in_proj_weight_grad_rec")
    dz, dnw, rxw_rec = _input_grad(dpa, dpb, w4, tokens, head, dz2, norm_w, dw_rec16, rows)

    small = jnp.concatenate([
        small[ROW_LOSS:ROW_LOSS + 1],
        dz[PAD_ROWS:PAD_ROWS + N_META],
        dnw,
        db_rec.reshape(len(SEGS_REC), D_MODEL), db_mix.reshape(len(SEGS_MIX), D_MODEL),
        dlb, jnp.zeros_like(dlb),
        small[ROW_HG_W:ROW_HG_W + 1], small[ROW_POOL_SCALE:ROW_POOL_SCALE + 1],
        small[ROW_FINAL_W:ROW_FINAL_W + 1],
        jnp.zeros((SMALL_ROWS - ROW_FINAL_W - 1, D_MODEL), F32),
    ], axis=0)
    return dz, (dw_rec, dw_mix, rxw_rec, rxw_mix), (dblob4, rx_blob), small


ANY = pl.BlockSpec(memory_space=pl.ANY)
MESH = pl.DeviceIdType.MESH


def _place():
    x, y, c = lax.axis_index("x"), lax.axis_index("y"), lax.axis_index("c")
    chips = [(1 - x, y), (x, 1 - y), (1 - x, 1 - y)]
    return x, y, c, chips


class _ShardGather:
    def __init__(self, rows):
        self.half = rows // 2

    def semaphores(self):
        return [pltpu.SemaphoreType.DMA((6,)), pltpu.SemaphoreType.DMA((6,))]

    def _copy(self, k, slot, to, send_sems, recv_sems):
        return pltpu.make_async_remote_copy(src_ref=slot, dst_ref=slot, send_sem=send_sems.at[k],
                                            recv_sem=recv_sems.at[k], device_id=to, device_id_type=MESH)

    def _half(self, ref4, chip, which):
        return ref4.at[chip, pl.ds(which * self.half, self.half), :]

    def start(self, ref4, send_sems, recv_sems, which=(0, 1, 2)):
        x, y, c, chips = _place()
        for j in which:
            cx, cy = chips[j]
            self._copy(j, self._half(ref4, 2 * x + y, c), (cx, cy, c), send_sems, recv_sems).start()

    def start_diagonal_after_neighbours(self, ref4, send_sems, recv_sems):
        x, y, c, chips = _place()
        for j in (0, 1):
            cx, cy = chips[j]
            self._copy(j, self._half(ref4, 2 * x + y, c), (cx, cy, c), send_sems, recv_sems).wait_send()
        self.start(ref4, send_sems, recv_sems, which=(2,))

    def pass_on(self, j, ref4, send_sems, recv_sems):
        x, y, c, chips = _place()
        cx, cy = chips[j]
        landed = self._half(ref4, 2 * cx + cy, c)
        self._copy(j, landed, (cx, cy, c), send_sems, recv_sems).wait_recv()
        self._copy(3 + j, landed, (x, y, 1 - c), send_sems, recv_sems).start()

    def await_sibling(self, j, ref4, send_sems, recv_sems):
        x, y, c, chips = _place()
        cx, cy = chips[j]
        self._copy(3 + j, self._half(ref4, 2 * cx + cy, 1 - c), (x, y, 1 - c), send_sems, recv_sems).wait_recv()

    def finish(self, ref4, send_sems, recv_sems, which=(0, 1, 2)):
        x, y, c, chips = _place()
        for j, (cx, cy) in enumerate(chips):
            if j in which:
                self._copy(j, self._half(ref4, 2 * x + y, c), (cx, cy, c), send_sems, recv_sems).wait_send()
            self._copy(3 + j, self._half(ref4, 2 * cx + cy, c), (x, y, 1 - c), send_sems, recv_sems).wait_send()


def _gather_meta(m4):
    def body(m_in_ref, m4_ref, send_sems, recv_sems):
        x, y, c, chips = _place()

        def copy(j, slot, to):
            return pltpu.make_async_remote_copy(src_ref=slot, dst_ref=slot, send_sem=send_sems.at[j],
                                                recv_sem=recv_sems.at[j], device_id=to, device_id_type=MESH)

        sends = [copy(j, m4_ref.at[2 * x + y], (cx, cy, c)) for j, (cx, cy) in enumerate(chips)]
        for cp in sends:
            cp.start()
        for j, (cx, cy) in enumerate(chips):
            copy(j, m4_ref.at[2 * cx + cy], (x, y, c)).wait_recv()
        for cp in sends:
            cp.wait_send()

    return pl.pallas_call(
        body, name="gather_meta",
        in_specs=[ANY], out_specs=ANY, out_shape=jax.ShapeDtypeStruct(m4.shape, m4.dtype),
        input_output_aliases={0: 0},
        scratch_shapes=[pltpu.SemaphoreType.DMA((3,)), pltpu.SemaphoreType.DMA((3,))],
    )(m4)


class _GradExchange:
    def __init__(self, segs, with_blob):
        self.segs = tuple(segs)
        self.with_blob = with_blob

    def landing_w(self):
        return jax.ShapeDtypeStruct((N_CHIPS, 2, D_MODEL // 2, D_MODEL), BF16)

    def landing_blob(self, blob16):
        return jax.ShapeDtypeStruct((N_DEV, blob16.shape[1] // 2, D_MODEL), BF16)

    def semaphores(self):
        n_send = len(self.segs) + (2 * N_CHIPS if self.with_blob else 0)
        n_recv = 2 * N_CHIPS + (N_DEV if self.with_blob else 0)
        return [pltpu.SemaphoreType.DMA((n_send,)), pltpu.SemaphoreType.DMA((n_recv,))]

    def _copies(self, dw_ref, rxw_ref, blob_ref, rxb_ref, send_sems, recv_sems):
        x, y, c = lax.axis_index("x"), lax.axis_index("y"), lax.axis_index("c")
        chip = 2 * x + y

        def relation(kx, ky, h):
            return (x ^ kx) * 4 + (y ^ ky) * 2 + (c ^ h)

        def copy(src, dst, send_k, recv_k, to):
            return functools.partial(pltpu.make_async_remote_copy, src_ref=src, dst_ref=dst,
                                     send_sem=send_sems.at[send_k], recv_sem=recv_sems.at[recv_k],
                                     device_id=to, device_id_type=MESH)

        sends, recvs = [], []
        for i, s in enumerate(self.segs):
            kx, ky = (s // 2) >> 1, (s // 2) & 1
            r = (x ^ kx) * 2 + (y ^ ky)
            sends.append((r != 0, copy(dw_ref.at[i], rxw_ref.at[r, s % 2], i, 2 * r + s % 2, (kx, ky, c))))
        for j in range(2):
            mine = [s // 2 for s in self.segs if s % 2 == j]
            if mine:
                cond = functools.reduce(lambda a, b: a | b, [chip == k for k in mine])
                for r in range(1, N_CHIPS):
                    slot = rxw_ref.at[r, j]
                    recvs.append((cond, copy(slot, slot, 0, 2 * r + j, (x, y, c))))
        if self.with_blob:
            hb = blob_ref.shape[1] // 2
            first_send, first_recv = len(self.segs), 2 * N_CHIPS
            for k in range(N_CHIPS):
                for h in range(2):
                    r = relation(k >> 1, k & 1, h)
                    sends.append((r != 0, copy(blob_ref.at[k, pl.ds(h * hb, hb), :], rxb_ref.at[r],
                                               first_send + 2 * k + h, first_recv + r, (k >> 1, k & 1, h))))
            for r in range(1, N_DEV):
                slot = rxb_ref.at[r]
                recvs.append((None, copy(slot, slot, 0, first_recv + r, (x, y, c))))
        return sends, recvs

    def start(self, *refs):
        sends, _ = self._copies(*refs)
        for cond, make in sends:
            pl.when(cond)(lambda make=make: make().start())

    def wait(self, *refs):
        sends, recvs = self._copies(*refs)
        for cond, make in sends:
            pl.when(cond)(lambda make=make: make().wait_send())
        for cond, make in recvs:
            if cond is None:
                make().wait_recv()
            else:
                pl.when(cond)(lambda make=make: make().wait_recv())


def _sum_landed(own, rx_ref):
    total = own
    for r in range(1, rx_ref.shape[0]):
        total = total + rx_ref[r, 0].astype(F32)
    return total


def _finish_w(dw_rec, dw_mix, rx_rec, rx_mix, place_arr):
    half = D_MODEL // 2
    tm = _tile(half, 256)
    n_rec = len(SEGS_REC)

    def body(place_ref, own_rec_ref, own_mix_ref, rx_rec_ref, rx_mix_ref, out_ref):
        seg = 2 * place_ref[0] + pl.program_id(0)

        @pl.when(seg < n_rec)
        def _():
            out_ref[0] = _sum_landed(own_rec_ref[0], rx_rec_ref)

        @pl.when(seg >= n_rec)
        def _():
            out_ref[0] = _sum_landed(own_mix_ref[0], rx_mix_ref)

    def own_spec(first, count):
        def index(j, i, place_ref):
            seg = 2 * place_ref[0] + j
            return (jnp.clip(seg - first, 0, count - 1), i, 0)
        return pl.BlockSpec((1, tm, D_MODEL), index)

    rx_spec = pl.BlockSpec((N_CHIPS, 1, tm, D_MODEL), lambda j, i, place_ref: (0, j, i, 0))
    return pl.pallas_call(
        body, name="finish_w",
        grid_spec=pltpu.PrefetchScalarGridSpec(
            num_scalar_prefetch=1, grid=(2, half // tm),
            in_specs=[own_spec(0, n_rec), own_spec(n_rec, len(SEGS_MIX)), rx_spec, rx_spec],
            out_specs=pl.BlockSpec((1, tm, D_MODEL), lambda j, i, place_ref: (place_ref[1], i, j))),
        out_shape=jax.ShapeDtypeStruct((2, half, 2 * D_MODEL), F32),
        compiler_params=_params(("arbitrary", "arbitrary")),
    )(place_arr, dw_rec, dw_mix, rx_rec, rx_mix)


def _finish_blob(dblob4, rx_blob, place_arr):
    n, rows, cols = rx_blob.shape
    tm = _tile(rows, 256)

    def body(place_ref, own_ref, rx_ref, out_ref):
        out_ref[0] = _sum_landed(own_ref[0, 0], rx_ref)

    return pl.pallas_call(
        body, name="finish_blob",
        grid_spec=pltpu.PrefetchScalarGridSpec(
            num_scalar_prefetch=1, grid=(rows // tm,),
            in_specs=[pl.BlockSpec((1, 1, tm, cols), lambda i, place_ref: (place_ref[0], place_ref[1], i, 0)),
                      pl.BlockSpec((n, 1, tm, cols), lambda i, place_ref: (0, 0, i, 0))],
            out_specs=pl.BlockSpec((1, tm, cols), lambda i, place_ref: (place_ref[1], i, 0))),
        out_shape=jax.ShapeDtypeStruct((2, rows, cols), F32),
        compiler_params=_params(("arbitrary",)),
    )(place_arr, dblob4.reshape(N_CHIPS, 2, rows, cols), rx_blob.reshape(n, 1, rows, cols))


def _share_finished(fw2, fb2, small):
    def body(w_in_ref, b_in_ref, small_ref, w_ref, b_ref, s_ref, bounce, local_sem, send_sems, recv_sems):
        x, y, c, _ = _place()
        sibling = (x, y, 1 - c)

        def copy(k, src, dst, to):
            return pltpu.make_async_remote_copy(src_ref=src, dst_ref=dst, send_sem=send_sems.at[k],
                                                recv_sem=recv_sems.at[k], device_id=to, device_id_type=MESH)

        sends = [copy(0, w_ref.at[c], w_ref.at[c], sibling), copy(1, b_ref.at[c], b_ref.at[c], sibling)]
        for r in range(1, N_DEV):
            peer = (x ^ ((r >> 2) & 1), y ^ ((r >> 1) & 1), c ^ (r & 1))
            sends.append(copy(1 + r, small_ref, s_ref.at[r], peer))
        for cp in sends:
            cp.start()
        for src, dst in ((small_ref, bounce), (bounce, s_ref.at[0])):
            own = pltpu.make_async_copy(src, dst, local_sem)
            own.start()
            own.wait()
        landed = [w_ref.at[1 - c], b_ref.at[1 - c]] + [s_ref.at[r] for r in range(1, N_DEV)]
        for k, slot in enumerate(landed):
            copy(k, slot, slot, (x, y, c)).wait_recv()
        for cp in sends:
            cp.wait_send()

    same = lambda a: jax.ShapeDtypeStruct(a.shape, a.dtype)
    n_sem = 2 + N_DEV - 1
    return pl.pallas_call(
        body, name="share_finished",
        in_specs=[ANY, ANY, ANY], out_specs=[ANY, ANY, ANY],
        out_shape=[same(fw2), same(fb2), jax.ShapeDtypeStruct((N_DEV,) + small.shape, F32)],
        input_output_aliases={0: 0, 1: 1},
        scratch_shapes=[pltpu.VMEM(small.shape, F32), pltpu.SemaphoreType.DMA,
                        pltpu.SemaphoreType.DMA((n_sem,)), pltpu.SemaphoreType.DMA((n_sem,))],
    )(fw2, fb2, small)


def _sum_small(slots, lb_logits, me_arr):
    def body(me_ref, slots_ref, lbl_ref, out_ref):
        me = me_ref[0]
        total = slots_ref[me]
        for d in range(1, N_DEV):
            total = total + slots_ref[d ^ me]
        out_ref[...] = total
        out_ref[ROW_LOSS:ROW_LOSS + 1, :] = jnp.broadcast_to(
            jnp.sum(total[ROW_LOSS:ROW_LOSS + 1, :], axis=-1, keepdims=True), (1, D_MODEL))
        lb = _lower_bound(lbl_ref[...])
        g0 = total[ROW_LB:ROW_LB + 1, :] * lb * (1.0 - lb)
        out_ref[ROW_LB:ROW_LB + 1, :] = g0
        out_ref[ROW_LB + 1:ROW_LB + 2, :] = -g0

    return pl.pallas_call(
        body, name="sum_small",
        grid_spec=pltpu.PrefetchScalarGridSpec(
            num_scalar_prefetch=1, grid=(1,),
            in_specs=[pl.BlockSpec((N_DEV, SMALL_ROWS, D_MODEL), lambda i, me_ref: (0, 0, 0)),
                      pl.BlockSpec((2, D_MODEL), lambda i, me_ref: (0, 0))],
            out_specs=pl.BlockSpec((SMALL_ROWS, D_MODEL), lambda i, me_ref: (0, 0))),
        out_shape=jax.ShapeDtypeStruct((SMALL_ROWS, D_MODEL), F32),
        compiler_params=_params(("arbitrary",)),
    )(me_arr, slots, lb_logits)


def _adamw(w, g, m, v):
    rows, cols = w.shape
    tm = _tile(rows, 256, mult=8) if rows % 8 == 0 else rows
    c1 = 1.0 / (1.0 - ADAM_B1 ** ADAM_STEP)
    c2 = 1.0 / (1.0 - ADAM_B2 ** ADAM_STEP)

    def body(w_ref, g_ref, m_ref, v_ref, d_ref, nm_ref, nv_ref):
        gt = g_ref[...]
        nm = ADAM_B1 * m_ref[...] + (1.0 - ADAM_B1) * gt
        nv = ADAM_B2 * v_ref[...] + (1.0 - ADAM_B2) * (gt * gt)
        nm_ref[...] = nm
        nv_ref[...] = nv
        d_ref[...] = -ADAM_LR * ((nm * c1) / (jnp.sqrt(nv * c2) + ADAM_EPS) + ADAM_WD * w_ref[...])

    blk = pl.BlockSpec((tm, cols), lambda i: (i, 0))
    sds = jax.ShapeDtypeStruct((rows, cols), F32)
    return pl.pallas_call(
        body, name="adamw",
        grid=(rows // tm,), in_specs=[blk] * 4, out_specs=[blk] * 3, out_shape=[sds] * 3,
        compiler_params=_params(("arbitrary",)),
    )(w, g, m, v)


def kernel(x, meta_tokens, norm_w, w_in, b_in, lb_logits, hg_norm_w, pool_w, pool_scale, w_down_hg, w_down_pool, w_out, final_norm_w, loss_target, m_meta_tokens, m_norm_w, m_w_in, m_b_in, m_lb_logits, m_hg_norm_w, m_pool_w, m_pool_scale, m_w_down_hg, m_w_down_pool, m_w_out, m_final_norm_w, v_meta_tokens, v_norm_w, v_w_in, v_b_in, v_lb_logits, v_hg_norm_w, v_pool_w, v_pool_scale, v_w_down_hg, v_w_down_pool, v_w_out, v_final_norm_w):
    seq = x.shape[1]
    xi, yi, ci = lax.axis_index("x"), lax.axis_index("y"), lax.axis_index("c")
    chip = 2 * xi + yi
    place_arr = jnp.stack([chip, ci]).astype(jnp.int32)
    me_arr = jnp.reshape(4 * xi + 2 * yi + ci, (1,)).astype(jnp.int32)
    q = D_MODEL // N_CHIPS

    def blob_of(wdh, wdp, wo, pw):
        return jnp.concatenate([wdh[0], wdp[0], wo[0], pw[0].reshape(-1, D_MODEL)], axis=0)

    def in_every_slot(a):
        return jnp.broadcast_to(a[None], (N_CHIPS,) + a.shape)

    meta4 = _gather_meta(in_every_slot(meta_tokens))
    meta_full = meta4.transpose(1, 0, 2).reshape(N_META, D_MODEL)
    w4 = in_every_slot(w_in[0].astype(BF16))
    blob4 = in_every_slot(blob_of(w_down_hg, w_down_pool, w_out, pool_w).astype(BF16))
    seg_order = jnp.stack([2 * (chip ^ rel) + t for rel in (0, 2, 1, 3) for t in (0, 1)]).astype(jnp.int32)

    head = jnp.concatenate([jnp.zeros((PAD_ROWS, D_MODEL), F32), meta_full], axis=0)
    fw2 = final_norm_w.reshape(1, D_MODEL)
    dz, w_parts, blob_parts, small = _local_step(
        x[0], head, loss_target[0], w4, blob4, seg_order, norm_w, b_in, lb_logits, hg_norm_w, pool_scale, fw2)
    grad_x = dz[FIRST_TOKEN_ROW:][None]

    fin_w = _finish_w(*w_parts, place_arr)
    fin_b = _finish_blob(*blob_parts, place_arr)
    gw2, gb2, slots = _share_finished(fin_w, fin_b, small)
    tot = _sum_small(slots, lb_logits, me_arr)
    g_w_in = gw2.reshape(D_MODEL, 2 * D_MODEL)
    g_blob = gb2.reshape(-1, D_MODEL)

    d_win, nm_win, nv_win = _adamw(w_in[0], g_w_in, m_w_in[0], v_w_in[0])
    d_blob, nm_blob, nv_blob = _adamw(
        blob_of(w_down_hg, w_down_pool, w_out, pool_w), g_blob,
        blob_of(m_w_down_hg, m_w_down_pool, m_w_out, m_pool_w),
        blob_of(v_w_down_hg, v_w_down_pool, v_w_out, v_pool_w))
    g_meta = lax.dynamic_slice_in_dim(tot[ROW_META:ROW_META + N_META], chip * q, q, axis=1)
    d_meta, nm_meta, nv_meta = _adamw(meta_tokens, g_meta, m_meta_tokens, v_meta_tokens)

    def rows_of(nw, bi, lbl, hg, ps, fw):
        return jnp.concatenate([nw, bi.reshape(N_SEG, D_MODEL), lbl, hg, ps, fw.reshape(1, D_MODEL),
                                jnp.zeros((2, D_MODEL), F32)], axis=0)

    g_rows = jnp.concatenate([tot[ROW_NORM_W:ROW_FINAL_W + 1], jnp.zeros((2, D_MODEL), F32)], axis=0)
    d_rows, nm_rows, nv_rows = _adamw(
        rows_of(norm_w, b_in, lb_logits, hg_norm_w, pool_scale, final_norm_w), g_rows,
        rows_of(m_norm_w, m_b_in, m_lb_logits, m_hg_norm_w, m_pool_scale, m_final_norm_w),
        rows_of(v_norm_w, v_b_in, v_lb_logits, v_hg_norm_w, v_pool_scale, v_final_norm_w))

    def unblob(b):
        return (b[0:q][None], b[q:2 * q][None], b[2 * q:3 * q][None], b[3 * q:].reshape(pool_w.shape))

    def unrows(r):
        o = ROW_NORM_W
        return dict(norm_w=r[ROW_NORM_W - o:ROW_B_IN - o], b_in=r[ROW_B_IN - o:ROW_LB - o].reshape(1, -1),
                    lb_logits=r[ROW_LB - o:ROW_HG_W - o], hg_norm_w=r[ROW_HG_W - o:ROW_POOL_SCALE - o],
                    pool_scale=r[ROW_POOL_SCALE - o:ROW_FINAL_W - o], final_norm_w=r[ROW_FINAL_W - o])

    def leaves(meta_part, rows_part, win_part, blob_part):
        r = unrows(rows_part)
        wdh, wdp, wo, pw = unblob(blob_part)
        return [meta_part, r["norm_w"], win_part[None], r["b_in"], r["lb_logits"], r["hg_norm_w"], pw,
                r["pool_scale"], wdh, wdp, wo, r["final_norm_w"]]

    loss = tot[ROW_LOSS, 0]
    return (loss, grad_x,
            *leaves(g_meta, g_rows, g_w_in, g_blob),
            *leaves(d_meta, d_rows, d_win, d_blob),
            *leaves(nm_meta, nm_rows, nm_win, nm_blob),
            *leaves(nv_meta, nv_rows, nv_win, nv_blob))
```

```python
import functools

import numpy as np
import jax
import jax.numpy as jnp
from jax import lax
from jax.experimental import pallas as pl
from jax.experimental.pallas import tpu as pltpu

F32 = jnp.float32
BF16 = jnp.bfloat16

D_MODEL = 1024
N_SEG = 8
N_HEADS = 8
HEAD_DIM = 128
CHUNK = 64
N_META = 16
PAD_ROWS = CHUNK - N_META
FIRST_TOKEN_ROW = CHUNK
LEVELS = (32, 16, 8, 4, 2, 1)
N_EXP = 2 + len(LEVELS)
POOL_WINDOWS = (2, 4, 8, 16)
POOL_GDIM = D_MODEL // len(POOL_WINDOWS)
HALO = 16
LOCAL_UNROLL = 13
BACKWARD_UNROLL = 13
EPS = 1e-6
N_CHIPS = 4
N_DEV = 8
SEGS_REC = (0, 1, 2)
SEGS_MIX = (3, 4, 5, 6, 7)

ADAM_LR = 0.001
ADAM_B1 = 0.9
ADAM_B2 = 0.999
ADAM_EPS = 1e-08
ADAM_WD = 0.01
ADAM_STEP = 10

VMEM_LIMIT_BYTES = 56 * 1024 * 1024

ROW_LOSS = 0
ROW_META = 1
ROW_NORM_W = ROW_META + N_META
ROW_B_IN = ROW_NORM_W + 1
ROW_LB = ROW_B_IN + N_SEG
ROW_HG_W = ROW_LB + 2
ROW_POOL_SCALE = ROW_HG_W + 1
ROW_FINAL_W = ROW_POOL_SCALE + 1
SMALL_ROWS = 32


def _tile(total, cap, mult=16):
    best = None
    for t in range(mult, min(total, cap) + 1, mult):
        if total % t == 0:
            best = t
    assert best is not None, (total, cap, mult)
    return best


def _token_window(tm, tile_of):
    def index(*grid):
        return (pl.multiple_of(jnp.maximum(tile_of(*grid) * tm - FIRST_TOKEN_ROW, 0), HALO), 0)
    return pl.BlockSpec((pl.Element(tm), pl.Element(D_MODEL)), index)


def _padded_tile(window, head, tile):
    first = jnp.concatenate([head, pltpu.roll(window, FIRST_TOKEN_ROW, 0)[FIRST_TOKEN_ROW:]], axis=0)
    return jnp.where(tile == 0, first, window)


def _params(sem=None):
    return pltpu.CompilerParams(dimension_semantics=sem, vmem_limit_bytes=VMEM_LIMIT_BYTES)


def _dot(a, b):
    return jnp.dot(a, b, preferred_element_type=F32)


def _dot_nt(a, b):
    return lax.dot_general(a, b, (((1,), (1,)), ((), ())), preferred_element_type=F32)


def _dot_tn(a, b):
    return lax.dot_general(a, b, (((0,), (0,)), ((), ())), preferred_element_type=F32)


def _sigmoid_pair(x):
    t = jnp.exp(-jnp.abs(x))
    r = 1.0 / (1.0 + t)
    pos = x >= 0
    return jnp.where(pos, r, t * r), jnp.where(pos, t * r, r)


def _exponent_matrix():
    t = np.arange(CHUNK)[:, None]
    j = np.arange(CHUNK)[None, :]
    blocks = [j <= t, j > t]
    for m in LEVELS:
        rho = (t // (2 * m)) * (2 * m) + m
        upper = (t >= rho) & (j > rho) & (j <= t)
        lower = (t < rho) & (j > t) & (j <= rho)
        blocks.append(upper | lower)
    return np.concatenate(blocks, axis=0).astype(np.float32)


def _pair_masks():
    t = np.arange(CHUNK)[:, None]
    s = np.arange(CHUNK)[None, :]
    masks = [t == s]
    for m in LEVELS:
        same = (t // (2 * m)) == (s // (2 * m))
        masks.append(same & ((t % (2 * m)) >= m) & ((s % (2 * m)) < m))
    return np.stack(masks).astype(np.float32)


LEVEL_PAIRS = ((0, 1), (2, 3), (4, 5), (6, None))


def _paired_masks():
    m = _pair_masks()
    zero = np.zeros_like(m[0])
    return np.stack([np.concatenate([m[a], zero if b is None else m[b]], axis=1) for a, b in LEVEL_PAIRS])


def _lower_bound(lbl):
    return 1.0 / (1.0 + jnp.exp(lbl[1:2, :] - lbl[0:1, :]))


def _in_proj(tokens, head, norm_w, w4, b_in, seg_order, rows):
    tm = _tile(rows, 1040)
    nt = rows // tm
    gather = _ShardGather(w4.shape[1])

    def body(order_ref, z_ref, head_ref, nw_ref, b_ref, w_in_ref, h_ref, p_ref, w4_ref,
             h_all, w_buf, w_sem, send_sems, recv_sems):
        kk, i = pl.program_id(0), pl.program_id(1)

        @pl.when((kk == 0) & (i == 0))
        def _():
            gather.start(w4_ref, send_sems, recv_sems, which=(0, 1))

        @pl.when((kk == 2) & (i == 0))
        def _():
            gather.start_diagonal_after_neighbours(w4_ref, send_sems, recv_sems)

        @pl.when(kk == 0)
        def _():
            zt = _padded_tile(z_ref[...], head_ref[...], i)
            rstd = lax.rsqrt(jnp.mean(zt * zt, axis=-1, keepdims=True) + EPS)
            h = (zt * rstd * nw_ref[...]).astype(BF16)
            h_all[pl.ds(pl.multiple_of(i * tm, 16), tm), :] = h
            h_ref[...] = h

        @pl.when((kk == 2) & (i == 0))
        def _():
            gather.pass_on(0, w4_ref, send_sems, recv_sems)
            gather.pass_on(1, w4_ref, send_sems, recv_sems)
            gather.await_sibling(0, w4_ref, send_sems, recv_sems)

        @pl.when((kk == 4) & (i == 0))
        def _():
            gather.await_sibling(1, w4_ref, send_sems, recv_sems)

        @pl.when((kk == 5) & (i == 0))
        def _():
            gather.pass_on(2, w4_ref, send_sems, recv_sems)

        @pl.when((kk == 6) & (i == 0))
        def _():
            gather.await_sibling(2, w4_ref, send_sems, recv_sems)

        def weights(which):
            seg = order_ref[2 * (kk // 2) + which]
            return pltpu.make_async_copy(
                w4_ref.at[seg // 2, :, pl.ds(pl.multiple_of((seg % 2) * D_MODEL, D_MODEL), D_MODEL)],
                w_buf.at[which], w_sem.at[which])

        @pl.when((i == 0) & (kk % 2 == 0))
        def _():
            weights(0).start()
            weights(1).start()
            weights(0).wait()

        @pl.when((i == 0) & (kk % 2 == 1))
        def _():
            weights(1).wait()

        p_ref[0] = _dot(h_all[pl.ds(pl.multiple_of(i * tm, 16), tm), :], w_buf[kk % 2]) + b_ref[...]

        @pl.when((kk == N_SEG - 1) & (i == nt - 1))
        def _():
            gather.finish(w4_ref, send_sems, recv_sems, which=(2,))

    first_pass = lambda kk, i, order_ref: (jnp.where(kk == 0, i, nt - 1), 0)
    return pl.pallas_call(
        body, name="in_proj",
        grid_spec=pltpu.PrefetchScalarGridSpec(
            num_scalar_prefetch=1, grid=(N_SEG, nt),
            in_specs=[
                _token_window(tm, lambda kk, i, order_ref: jnp.where(kk == 0, i, nt - 1)),
                pl.BlockSpec((FIRST_TOKEN_ROW, D_MODEL), lambda kk, i, order_ref: (0, 0)),
                pl.BlockSpec((1, D_MODEL), lambda kk, i, order_ref: (0, 0)),
                pl.BlockSpec((1, D_MODEL), lambda kk, i, order_ref: (0, order_ref[kk])),
                ANY,
            ],
            out_specs=[
                pl.BlockSpec((tm, D_MODEL), first_pass),
                pl.BlockSpec((1, tm, D_MODEL), lambda kk, i, order_ref: (order_ref[kk], i, 0)),
                ANY,
            ],
            scratch_shapes=[
                pltpu.VMEM((rows, D_MODEL), BF16),
                pltpu.VMEM((2, D_MODEL, D_MODEL), BF16),
                pltpu.SemaphoreType.DMA((2,)),
            ] + gather.semaphores()),
        out_shape=[
            jax.ShapeDtypeStruct((rows, D_MODEL), BF16),
            jax.ShapeDtypeStruct((N_SEG, rows, D_MODEL), F32),
            jax.ShapeDtypeStruct(w4.shape, w4.dtype),
        ],
        input_output_aliases={5: 2},
        compiler_params=_params(("arbitrary", "arbitrary")),
    )(seg_order, tokens, head, norm_w, b_in, w4)


def _hgrn_forward(p3, lb_logits, wexp2, masks2, blob4, rows):
    n_chunks = rows // CHUNK
    cpb = _tile(n_chunks, 13, mult=1)
    rb_rows = cpb * CHUNK
    n_rb = n_chunks // cpb
    lanes = cpb * HEAD_DIM
    gather = _ShardGather(blob4.shape[1])

    def body(q_ref, fz_ref, v_ref, lbl_ref, wexp_ref, mask_ref, b_in_ref, o_ref, s_ref, e16_ref, a2_ref, b4_ref,
             st_ref, e_ref, u_ref, q_s, kk_s, v_s, qb_s, oi_s, send_sems, recv_sems):
        rb = pl.program_id(1)

        @pl.when((pl.program_id(0) == 0) & (rb == 0))
        def _():
            gather.start(b4_ref, send_sems, recv_sems)

        @pl.when(rb == 0)
        def _():
            st_ref[...] = jnp.zeros_like(st_ref)

        lb = _lower_bound(lbl_ref[...])
        row = rb * rb_rows + lax.broadcasted_iota(jnp.int32, (rb_rows, 1), 0)
        valid = row >= PAD_ROWS
        sg, sn = _sigmoid_pair(fz_ref[0])
        g = jnp.where(valid, jnp.log(lb + (1.0 - lb) * sg), 0.0)
        kk_s[...] = jnp.where(valid, (1.0 - lb) * sn, 0.0)
        q_s[...] = jnp.where(valid, q_ref[0], 0.0)
        v_s[...] = jnp.where(valid, v_ref[0], 0.0).astype(BF16)
        hi = g.astype(BF16)
        mid = (g - hi.astype(F32)).astype(BF16)
        g2 = jnp.concatenate(
            [jnp.concatenate([hi[b * CHUNK:(b + 1) * CHUNK], mid[b * CHUNK:(b + 1) * CHUNK]], axis=0)
             for b in range(cpb)], axis=1)
        e_ref[...] = jnp.exp(_dot(wexp_ref[...], g2))
        e16_ref[0, 0] = e_ref[...].astype(BF16)

        zeros16 = jnp.zeros((CHUNK, HEAD_DIM), BF16)

        def local(b, carry):
            r0 = pl.multiple_of(b * CHUNK, CHUNK)
            l0 = pl.multiple_of(b * HEAD_DIM, HEAD_DIM)
            q = q_s[pl.ds(r0, CHUNK), :]
            kk = kk_s[pl.ds(r0, CHUNK), :]
            v16 = v_s[pl.ds(r0, CHUNK), :]

            def scaled(entry):
                if entry == 0:
                    return q.astype(BF16), kk.astype(BF16)
                e_m = e_ref[(1 + entry) * CHUNK:(2 + entry) * CHUNK, pl.ds(l0, HEAD_DIM)]
                return (q * e_m).astype(BF16), (kk * e_m).astype(BF16)

            a2 = jnp.zeros((CHUNK, 2 * CHUNK), F32)
            for p, (ea, eb) in enumerate(LEVEL_PAIRS):
                qa, ka = scaled(ea)
                if eb is None:
                    prod = _dot_nt(qa, jnp.concatenate([ka, zeros16], axis=0))
                else:
                    qb_, kb_ = scaled(eb)
                    rhs = jnp.concatenate([jnp.concatenate([ka, zeros16], axis=1),
                                           jnp.concatenate([zeros16, kb_], axis=1)], axis=0)
                    prod = _dot_nt(jnp.concatenate([qa, qb_], axis=1), rhs)
                a2 = a2 + mask_ref[p] * prod
            a2_16 = a2.astype(BF16)
            a2_ref[pl.ds(r0, CHUNK), :] = a2_16
            oi_s[pl.ds(r0, CHUNK), :] = _dot(a2_16, jnp.concatenate([v16, v16], axis=0))
            e_b = e_ref[0:CHUNK, pl.ds(l0, HEAD_DIM)]
            e_c = e_ref[CHUNK:2 * CHUNK, pl.ds(l0, HEAD_DIM)]
            qb_s[pl.ds(r0, CHUNK), :] = (q * e_b).astype(BF16)
            u_ref[b] = _dot_tn(v16, (kk * e_c).astype(BF16))
            return carry

        lax.fori_loop(0, cpb, local, 0, unroll=LOCAL_UNROLL)

        def recur(b, st):
            l0 = pl.multiple_of(b * HEAD_DIM, HEAD_DIM)
            s_ref[0, b] = st
            return st * e_ref[CHUNK - 1:CHUNK, pl.ds(l0, HEAD_DIM)] + u_ref[b]

        st_ref[...] = lax.fori_loop(0, cpb, recur, st_ref[...])

        def inter(b, carry):
            r0 = pl.multiple_of(b * CHUNK, CHUNK)
            o_ref[pl.ds(r0, CHUNK), :] = oi_s[pl.ds(r0, CHUNK), :] + _dot_nt(
                qb_s[pl.ds(r0, CHUNK), :], s_ref[0, b].astype(BF16))
            return carry

        lax.fori_loop(0, cpb, inter, 0, unroll=LOCAL_UNROLL)

        @pl.when((pl.program_id(0) == N_HEADS // 2) & (rb == 0))
        def _():
            for j in range(N_CHIPS - 1):
                gather.pass_on(j, b4_ref, send_sems, recv_sems)

        @pl.when((pl.program_id(0) == N_HEADS - 1) & (rb == n_rb - 1))
        def _():
            for j in range(N_CHIPS - 1):
                gather.await_sibling(j, b4_ref, send_sems, recv_sems)
            gather.finish(b4_ref, send_sems, recv_sems)

    head_block = lambda seg: pl.BlockSpec((1, rb_rows, HEAD_DIM), lambda h, r: (seg, r, h))
    return pl.pallas_call(
        body, name="hgrn_forward",
        grid=(N_HEADS, n_rb),
        in_specs=[
            head_block(0), head_block(1), head_block(2),
            pl.BlockSpec((2, HEAD_DIM), lambda h, r: (0, h)),
            pl.BlockSpec((N_EXP * CHUNK, 2 * CHUNK), lambda h, r: (0, 0)),
            pl.BlockSpec((len(LEVEL_PAIRS), CHUNK, 2 * CHUNK), lambda h, r: (0, 0, 0)),
            ANY,
        ],
        out_specs=[
            pl.BlockSpec((rb_rows, HEAD_DIM), lambda h, r: (r, h)),
            pl.BlockSpec((1, cpb, HEAD_DIM, HEAD_DIM), lambda h, r: (h, r, 0, 0)),
            pl.BlockSpec((1, 1, N_EXP * CHUNK, lanes), lambda h, r: (h, r, 0, 0)),
            pl.BlockSpec((rb_rows, HEAD_DIM), lambda h, r: (r, h)),
            ANY,
        ],
        out_shape=[
            jax.ShapeDtypeStruct((rows, D_MODEL), F32),
            jax.ShapeDtypeStruct((N_HEADS, n_chunks, HEAD_DIM, HEAD_DIM), F32),
            jax.ShapeDtypeStruct((N_HEADS, n_rb, N_EXP * CHUNK, lanes), BF16),
            jax.ShapeDtypeStruct((rows, D_MODEL), BF16),
            jax.ShapeDtypeStruct(blob4.shape, blob4.dtype),
        ],
        input_output_aliases={6: 4},
        scratch_shapes=[
            pltpu.VMEM((HEAD_DIM, HEAD_DIM), F32),
            pltpu.VMEM((N_EXP * CHUNK, lanes), F32),
            pltpu.VMEM((cpb, HEAD_DIM, HEAD_DIM), F32),
            pltpu.VMEM((rb_rows, HEAD_DIM), F32),
            pltpu.VMEM((rb_rows, HEAD_DIM), F32),
            pltpu.VMEM((rb_rows, HEAD_DIM), BF16),
            pltpu.VMEM((rb_rows, HEAD_DIM), BF16),
            pltpu.VMEM((rb_rows, HEAD_DIM), F32),
        ] + gather.semaphores(),
        compiler_params=_params(("arbitrary", "arbitrary")),
    )(p3, p3, p3, lb_logits, wexp2, masks2, blob4)


def _hgrn_backward(p3, d_o, states, e16, a2, lb_logits, wexp_t, masks2, dw16, blob16, rows):
    n_chunks = rows // CHUNK
    cpb = _tile(n_chunks, 13, mult=1)
    rb_rows = cpb * CHUNK
    n_rb = n_chunks // cpb
    lanes = cpb * HEAD_DIM
    exchange = _GradExchange(SEGS_MIX, with_blob=True)

    def body(q_ref, fz_ref, v_ref, do_ref, s_ref, e_ref, a2_ref, lbl_ref, wexpt_ref, mask_ref, dw_ref, blob_ref,
             dp_ref, dlb_ref, rxw_ref, rxb_ref,
             dst_ref, g_ref, dsn_ref, q_s, kk_s, v_s, do_s, dq_s, dkk_s, dg_s, dx_s, send_sems, recv_sems):
        step = pl.program_id(1)
        rb = n_rb - 1 - step

        @pl.when((pl.program_id(0) == 0) & (step == 0))
        def _():
            exchange.start(dw_ref, rxw_ref, blob_ref, rxb_ref, send_sems, recv_sems)

        @pl.when(step == 0)
        def _():
            dst_ref[...] = jnp.zeros_like(dst_ref)
            dlb_ref[...] = jnp.zeros_like(dlb_ref)

        lb = _lower_bound(lbl_ref[...])
        row = rb * rb_rows + lax.broadcasted_iota(jnp.int32, (rb_rows, 1), 0)
        valid = row >= PAD_ROWS
        sg, sn = _sigmoid_pair(fz_ref[0])
        f = lb + (1.0 - lb) * sg
        g = jnp.where(valid, jnp.log(f), 0.0)
        kk_s[...] = jnp.where(valid, (1.0 - lb) * sn, 0.0)
        q_s[...] = jnp.where(valid, q_ref[0], 0.0)
        v_s[...] = jnp.where(valid, v_ref[0], 0.0).astype(BF16)
        do_s[...] = do_ref[...].astype(BF16)
        e_last_all = jnp.exp(jnp.concatenate(
            [jnp.sum(g[b * CHUNK:(b + 1) * CHUNK], axis=0, keepdims=True) for b in range(cpb)], axis=0))
        last_row = lax.broadcasted_iota(jnp.int32, (CHUNK, 1), 0) == CHUNK - 1
        zeros16 = jnp.zeros((CHUNK, HEAD_DIM), BF16)

        def factor(block, l0):
            return e_ref[0, 0, block * CHUNK:(block + 1) * CHUNK, pl.ds(l0, HEAD_DIM)].astype(F32)

        def contribution(b, carry):
            r0 = pl.multiple_of(b * CHUNK, CHUNK)
            l0 = pl.multiple_of(b * HEAD_DIM, HEAD_DIM)
            qb16 = (q_s[pl.ds(r0, CHUNK), :] * factor(0, l0)).astype(BF16)
            g_ref[b] = _dot_tn(do_s[pl.ds(r0, CHUNK), :], qb16)
            return carry

        lax.fori_loop(0, cpb, contribution, 0, unroll=LOCAL_UNROLL)

        cur = dst_ref[...]
        for b in reversed(range(cpb)):
            dsn_ref[b] = cur
            cur = cur * e_last_all[b:b + 1, :] + g_ref[b]
        dst_ref[...] = cur

        def local(b, carry):
            r0 = pl.multiple_of(b * CHUNK, CHUNK)
            l0 = pl.multiple_of(b * HEAD_DIM, HEAD_DIM)
            q = q_s[pl.ds(r0, CHUNK), :]
            kk = kk_s[pl.ds(r0, CHUNK), :]
            v16 = v_s[pl.ds(r0, CHUNK), :]
            do16 = do_s[pl.ds(r0, CHUNK), :]
            st = s_ref[0, b]
            dsn = dsn_ref[b]
            dsn16 = dsn.astype(BF16)
            e_b, e_c = factor(0, l0), factor(1, l0)
            qb, kc = q * e_b, kk * e_c

            t = _dot_tn(a2_ref[pl.ds(r0, CHUNK), :], do16)
            dv = t[0:CHUNK] + t[CHUNK:2 * CHUNK] + _dot_nt(kc.astype(BF16), dsn16)
            dp_ref[2, pl.ds(r0, CHUNK), :] = dv.astype(BF16)
            da2 = _dot_nt(do16, jnp.concatenate([v16, v16], axis=0))
            dqb = _dot(do16, st.astype(BF16))
            dkc = _dot(v16, dsn16)
            de = jnp.sum(dsn * st, axis=0, keepdims=True) * e_b[CHUNK - 1:CHUNK, :]
            dq = e_b * dqb
            dkk = e_c * dkc
            dx_s[0:CHUNK, pl.ds(l0, HEAD_DIM)] = (qb * dqb + jnp.where(last_row, de, 0.0)).astype(BF16)
            dx_s[CHUNK:2 * CHUNK, pl.ds(l0, HEAD_DIM)] = (kc * dkc).astype(BF16)

            def scaled(entry):
                if entry == 0:
                    return q, kk, None
                e_m = factor(1 + entry, l0)
                return q * e_m, kk * e_m, e_m

            for p, (ea, eb) in enumerate(LEVEL_PAIRS):
                dm = mask_ref[p] * da2
                dm_t = dm.T.astype(BF16)
                qa, ka, e_a = scaled(ea)
                if eb is None:
                    rhs_k = jnp.concatenate([jnp.concatenate([ka.astype(BF16), zeros16], axis=1),
                                             jnp.concatenate([zeros16, zeros16], axis=1)], axis=0)
                else:
                    qb_, kb_, e_bb = scaled(eb)
                    rhs_k = jnp.concatenate([jnp.concatenate([ka.astype(BF16), zeros16], axis=1),
                                             jnp.concatenate([zeros16, kb_.astype(BF16)], axis=1)], axis=0)
                dq2 = _dot(dm.astype(BF16), rhs_k)
                parts = [(ea, qa, ka, e_a, dq2[:, :HEAD_DIM], _dot(dm_t[0:CHUNK], qa.astype(BF16)))]
                if eb is not None:
                    parts.append((eb, qb_, kb_, e_bb, dq2[:, HEAD_DIM:],
                                  _dot(dm_t[CHUNK:2 * CHUNK], qb_.astype(BF16))))
                for entry, q_m, k_m, e_m, dq_m, dk_m in parts:
                    if entry == 0:
                        dq = dq + dq_m
                        dkk = dkk + dk_m
                    else:
                        dq = dq + e_m * dq_m
                        dkk = dkk + e_m * dk_m
                        dx_s[(1 + entry) * CHUNK:(2 + entry) * CHUNK, pl.ds(l0, HEAD_DIM)] = (
                            q_m * dq_m + k_m * dk_m).astype(BF16)
            dq_s[pl.ds(r0, CHUNK), :] = dq
            dkk_s[pl.ds(r0, CHUNK), :] = dkk
            return carry

        lax.fori_loop(0, cpb, local, 0, unroll=BACKWARD_UNROLL)

        dg_all = _dot(wexpt_ref[...], dx_s[...])
        for b in range(cpb):
            dg_s[b * CHUNK:(b + 1) * CHUNK, :] = dg_all[:, b * HEAD_DIM:(b + 1) * HEAD_DIM]
        t = jnp.where(valid, dg_s[...] / f - dkk_s[...], 0.0)
        dlb_ref[...] += jnp.sum(sn * t, axis=0, keepdims=True)
        dp_ref[0] = jnp.where(valid, dq_s[...], 0.0).astype(BF16)
        dp_ref[1] = ((1.0 - lb) * sg * sn * t).astype(BF16)

        @pl.when((pl.program_id(0) == N_HEADS - 1) & (step == n_rb - 1))
        def _():
            exchange.wait(dw_ref, rxw_ref, blob_ref, rxb_ref, send_sems, recv_sems)

    head_block = lambda seg: pl.BlockSpec((1, rb_rows, HEAD_DIM), lambda h, s: (seg, n_rb - 1 - s, h))
    row_block = pl.BlockSpec((rb_rows, HEAD_DIM), lambda h, s: (n_rb - 1 - s, h))
    return pl.pallas_call(
        body, name="hgrn_backward",
        grid=(N_HEADS, n_rb),
        in_specs=[
            head_block(0), head_block(1), head_block(2),
            row_block,
            pl.BlockSpec((1, cpb, HEAD_DIM, HEAD_DIM), lambda h, s: (h, n_rb - 1 - s, 0, 0)),
            pl.BlockSpec((1, 1, N_EXP * CHUNK, lanes), lambda h, s: (h, n_rb - 1 - s, 0, 0)),
            row_block,
            pl.BlockSpec((2, HEAD_DIM), lambda h, s: (0, h)),
            pl.BlockSpec((CHUNK, N_EXP * CHUNK), lambda h, s: (0, 0)),
            pl.BlockSpec((len(LEVEL_PAIRS), CHUNK, 2 * CHUNK), lambda h, s: (0, 0, 0)),
            ANY, ANY,
        ],
        out_specs=[
            pl.BlockSpec((3, rb_rows, HEAD_DIM), lambda h, s: (0, n_rb - 1 - s, h)),
            pl.BlockSpec((1, HEAD_DIM), lambda h, s: (0, h)),
            ANY, ANY,
        ],
        out_shape=[
            jax.ShapeDtypeStruct((3, rows, D_MODEL), BF16),
            jax.ShapeDtypeStruct((1, D_MODEL), F32),
            exchange.landing_w(), exchange.landing_blob(blob16),
        ],
        scratch_shapes=[
            pltpu.VMEM((HEAD_DIM, HEAD_DIM), F32),
            pltpu.VMEM((cpb, HEAD_DIM, HEAD_DIM), F32),
            pltpu.VMEM((cpb, HEAD_DIM, HEAD_DIM), F32),
            pltpu.VMEM((rb_rows, HEAD_DIM), F32),
            pltpu.VMEM((rb_rows, HEAD_DIM), F32),
            pltpu.VMEM((rb_rows, HEAD_DIM), BF16),
            pltpu.VMEM((rb_rows, HEAD_DIM), BF16),
            pltpu.VMEM((rb_rows, HEAD_DIM), F32),
            pltpu.VMEM((rb_rows, HEAD_DIM), F32),
            pltpu.VMEM((rb_rows, HEAD_DIM), F32),
            pltpu.VMEM((N_EXP * CHUNK, lanes), BF16),
        ] + exchange.semaphores(),
        compiler_params=_params(("arbitrary", "arbitrary")),
    )(p3, p3, p3, d_o, states, e16, a2, lb_logits, wexp_t, masks2, dw16, blob16)


def _sigmoid(x):
    return 1.0 / (1.0 + jnp.exp(-x))


def _silu_and_grad(x):
    s = _sigmoid(x)
    return x * s, s * (1.0 + x * (1.0 - s))


def _window_sum(ext, width, forward_looking):
    n = ext.shape[0]
    s = ext
    step = 1
    while step < width:
        s = s + pltpu.roll(s, (n - step) if forward_looking else step, 0)
        step *= 2
    return s


def _mixers(o, p3, tokens, head, tgt, wdh, wdp, wout, poolw, hg_w, pool_scale, final_w, rows):
    tm = _tile(rows, 208)
    nt = rows // tm
    halo_blocks = tm // HALO
    n_grp = len(POOL_WINDOWS)

    def body(o_ref, ghg_ref, u_ref, gpl_ref, mhg_ref, mpl_ref, uh_ref, z_ref, t_ref,
             wdh_ref, wdp_ref, wout_ref, pw_ref, hgw_ref, ps_ref, fw_ref, head_ref,
             do_ref, dz2_ref, dp_ref, dwdh_ref, dwdp_ref, dwout_ref, dpw_ref, small_ref, carry_ref):
        step = pl.program_id(0)
        tile = nt - 1 - step

        @pl.when(step == 0)
        def _():
            dwdh_ref[...] = jnp.zeros_like(dwdh_ref)
            dwdp_ref[...] = jnp.zeros_like(dwdp_ref)
            dwout_ref[...] = jnp.zeros_like(dwout_ref)
            dpw_ref[...] = jnp.zeros_like(dpw_ref)
            small_ref[...] = jnp.zeros_like(small_ref)
            carry_ref[...] = jnp.zeros_like(carry_ref)

        row = tile * tm + lax.broadcasted_iota(jnp.int32, (tm, 1), 0)
        real = row >= PAD_ROWS
        pos1 = jnp.maximum(row - PAD_ROWS + 1, 1).astype(F32)

        u = jnp.where(real, u_ref[0], 0.0)
        halo_row = tile * tm - HALO + lax.broadcasted_iota(jnp.int32, (HALO, 1), 0)
        uh = jnp.where(halo_row >= PAD_ROWS, uh_ref[0], 0.0)
        ext = jnp.concatenate([uh, u], axis=0)
        pooled, inv_cnt, mixed = [], [], []
        for g, w in enumerate(POOL_WINDOWS):
            cols = slice(g * POOL_GDIM, (g + 1) * POOL_GDIM)
            inv = 1.0 / jnp.minimum(pos1, float(w))
            ws = _window_sum(ext[:, cols], w, False)[HALO:]
            pg = (ws * inv - u[:, cols]).astype(BF16)
            pooled.append(pg)
            inv_cnt.append(inv)
            mixed.append(_dot(pg, pw_ref[g]))
        mixed = jnp.concatenate(mixed, axis=1)
        gpl = gpl_ref[0]
        sp, dsp = _silu_and_grad(gpl)
        ps = ps_ref[...]
        a_pool = (mixed * ps * sp).astype(BF16)
        y_pool = _dot(a_pool, wdp_ref[...])

        o = o_ref[...]
        o_hat, rstd_h = [], []
        for h in range(N_HEADS):
            oh = o[:, h * HEAD_DIM:(h + 1) * HEAD_DIM]
            r = lax.rsqrt(jnp.mean(oh * oh, axis=-1, keepdims=True) + EPS)
            rstd_h.append(r)
            o_hat.append(oh * r)
        o_hat = jnp.concatenate(o_hat, axis=1)
        hgw = hgw_ref[...]
        o_n = o_hat * hgw
        ghg = ghg_ref[0]
        sh, dsh = _silu_and_grad(ghg)
        a_hg = (o_n * sh).astype(BF16)
        y_hg = _dot(a_hg, wdh_ref[...])

        s_mh = _sigmoid(mhg_ref[0])
        s_mp = _sigmoid(mpl_ref[0])
        merged = (s_mh * y_hg + s_mp * y_pool).astype(BF16)
        z2 = _padded_tile(z_ref[...], head_ref[...], tile) + _dot(merged, wout_ref[...])
        rstd2 = lax.rsqrt(jnp.mean(z2 * z2, axis=-1, keepdims=True) + EPS)
        zh = z2 * rstd2
        fw = fw_ref[...]
        target = _padded_tile(t_ref[...], jnp.zeros((FIRST_TOKEN_ROW, D_MODEL), F32), tile)
        err = jnp.where(row >= FIRST_TOKEN_ROW, zh * fw - target, 0.0)
        small_ref[ROW_LOSS:ROW_LOSS + 1, :] += jnp.sum(err * err, axis=0, keepdims=True) * (0.5 / D_MODEL)
        dy = err * (1.0 / D_MODEL)

        small_ref[ROW_FINAL_W:ROW_FINAL_W + 1, :] += jnp.sum(dy * zh, axis=0, keepdims=True)
        uu = dy * fw
        dz2 = rstd2 * (uu - zh * jnp.mean(uu * zh, axis=-1, keepdims=True))
        dz2_ref[...] = dz2
        dz2_16 = dz2.astype(BF16)
        dmerged = _dot_nt(dz2_16, wout_ref[...])
        dwout_ref[...] += _dot_tn(merged, dz2_16)
        dy_hg = (s_mh * dmerged).astype(BF16)
        dy_pool = (s_mp * dmerged).astype(BF16)
        dp_ref[3] = (dmerged * y_hg * s_mh * (1.0 - s_mh)).astype(BF16)
        dp_ref[4] = (dmerged * y_pool * s_mp * (1.0 - s_mp)).astype(BF16)

        da_hg = _dot_nt(dy_hg, wdh_ref[...])
        dwdh_ref[...] += _dot_tn(a_hg, dy_hg)
        dp_ref[0] = (da_hg * o_n * dsh).astype(BF16)
        do_n = da_hg * sh
        small_ref[ROW_HG_W:ROW_HG_W + 1, :] += jnp.sum(do_n * o_hat, axis=0, keepdims=True)
        d_hat = do_n * hgw
        for h in range(N_HEADS):
            cols = slice(h * HEAD_DIM, (h + 1) * HEAD_DIM)
            dh_, oh_ = d_hat[:, cols], o_hat[:, cols]
            do_ref[:, cols] = rstd_h[h] * (dh_ - oh_ * jnp.mean(dh_ * oh_, axis=-1, keepdims=True))

        da_pool = _dot_nt(dy_pool, wdp_ref[...])
        dwdp_ref[...] += _dot_tn(a_pool, dy_pool)
        small_ref[ROW_POOL_SCALE:ROW_POOL_SCALE + 1, :] += jnp.sum(da_pool * mixed * sp, axis=0, keepdims=True)
        dp_ref[2] = (da_pool * mixed * ps * dsp).astype(BF16)
        dmixed = (da_pool * ps * sp).astype(BF16)
        carry = carry_ref[...]
        du, new_carry = [], []
        for g, w in enumerate(POOL_WINDOWS):
            cols = slice(g * POOL_GDIM, (g + 1) * POOL_GDIM)
            dmg = dmixed[:, cols]
            dpooled = _dot_nt(dmg, pw_ref[g])
            dpw_ref[g] += _dot_tn(pooled[g], dmg)
            dps = dpooled * inv_cnt[g]
            ext_b = jnp.concatenate([dps, carry[:, cols]], axis=0)
            du.append(_window_sum(ext_b, w, True)[:tm] - dpooled)
            new_carry.append(dps[:HALO])
        dp_ref[1] = jnp.where(real, jnp.concatenate(du, axis=1), 0.0).astype(BF16)
        carry_ref[...] = jnp.concatenate(new_carry, axis=1)

    row_block = pl.BlockSpec((tm, D_MODEL), lambda s: (nt - 1 - s, 0))
    seg_block = lambda seg: pl.BlockSpec((1, tm, D_MODEL), lambda s: (seg, nt - 1 - s, 0))
    whole = pl.BlockSpec(memory_space=pltpu.VMEM)
    return pl.pallas_call(
        body, name="mixers",
        grid=(nt,),
        in_specs=[
            row_block, seg_block(3), seg_block(4), seg_block(5), seg_block(6), seg_block(7),
            pl.BlockSpec((1, HALO, D_MODEL),
                         lambda s: (4, jnp.maximum((nt - 1 - s) * halo_blocks - 1, 0), 0)),
            _token_window(tm, lambda s: nt - 1 - s), _token_window(tm, lambda s: nt - 1 - s),
            whole, whole, whole, whole, whole, whole, whole, whole,
        ],
        out_specs=[
            row_block, row_block,
            pl.BlockSpec((5, tm, D_MODEL), lambda s: (0, nt - 1 - s, 0)),
            whole, whole, whole, whole, whole,
        ],
        out_shape=[
            jax.ShapeDtypeStruct((rows, D_MODEL), F32),
            jax.ShapeDtypeStruct((rows, D_MODEL), F32),
            jax.ShapeDtypeStruct((5, rows, D_MODEL), BF16),
            jax.ShapeDtypeStruct((D_MODEL, D_MODEL), F32),
            jax.ShapeDtypeStruct((D_MODEL, D_MODEL), F32),
            jax.ShapeDtypeStruct((D_MODEL, D_MODEL), F32),
            jax.ShapeDtypeStruct((n_grp, POOL_GDIM, POOL_GDIM), F32),
            jax.ShapeDtypeStruct((SMALL_ROWS, D_MODEL), F32),
        ],
        scratch_shapes=[pltpu.VMEM((HALO, D_MODEL), F32)],
        compiler_params=_params(("arbitrary",)),
    )(o, p3, p3, p3, p3, p3, p3, tokens, tgt, wdh, wdp, wout, poolw, hg_w, pool_scale, final_w, head)


def _seg_specs(tm, row_of, seg_of):
    def spec_a(*g):
        k = seg_of(*g)
        return (jnp.minimum(k, 2), jnp.where(k < 3, row_of(*g), 0), 0)

    def spec_b(*g):
        k = seg_of(*g)
        return (jnp.maximum(k - 3, 0), jnp.where(k >= 3, row_of(*g), 0), 0)

    return pl.BlockSpec((1, tm, D_MODEL), spec_a), pl.BlockSpec((1, tm, D_MODEL), spec_b)


def _in_proj_weight_grad(h, dp, rows, name):
    n_seg = dp.shape[0]
    tm = _tile(rows, 1040)
    nt = rows // tm
    half = D_MODEL // 2

    def body(h_ref, dp_ref, part_ref, part16_ref, db_ref, acc_ref, bacc_ref, stage_ref, land_ref,
             send_sems, recv_sems):
        k, i = pl.program_id(0), pl.program_id(1)
        x, y, c = lax.axis_index("x"), lax.axis_index("y"), lax.axis_index("c")

        def to_sibling(seg):
            return pltpu.make_async_remote_copy(
                src_ref=stage_ref.at[seg], dst_ref=land_ref.at[seg], send_sem=send_sems.at[seg],
                recv_sem=recv_sems.at[seg], device_id=(x, y, 1 - c), device_id_type=MESH)

        @pl.when(i == 0)
        def _():
            acc_ref[...] = jnp.zeros_like(acc_ref)
            bacc_ref[...] = jnp.zeros_like(bacc_ref)

        dpt = dp_ref[0]
        acc_ref[...] += _dot_tn(h_ref[...], dpt)
        bacc_ref[...] += jnp.sum(dpt.astype(F32), axis=0, keepdims=True)

        @pl.when(i == nt - 1)
        def _():
            db_ref[0] = bacc_ref[...]
            part_ref[k] = acc_ref[pl.ds(pl.multiple_of(c * half, half), half), :]
            stage_ref[k] = acc_ref[pl.ds(pl.multiple_of((1 - c) * half, half), half), :].astype(BF16)
            to_sibling(k).start()

        @pl.when((k == n_seg - 1) & (i == nt - 1))
        def _():
            for seg in range(n_seg):
                to_sibling(seg).wait_recv()
                total = part_ref[seg] + land_ref[seg].astype(F32)
                part_ref[seg] = total
                part16_ref[seg] = total.astype(BF16)
            for seg in range(n_seg):
                to_sibling(seg).wait_send()

    whole = pl.BlockSpec(memory_space=pltpu.VMEM)
    return pl.pallas_call(
        body, name=name,
        grid=(n_seg, nt),
        in_specs=[pl.BlockSpec((tm, D_MODEL), lambda k, i: (i, 0)),
                  pl.BlockSpec((1, tm, D_MODEL), lambda k, i: (k, i, 0))],
        out_specs=[whole, whole, pl.BlockSpec((1, 1, D_MODEL), lambda k, i: (k, 0, 0))],
        out_shape=[
            jax.ShapeDtypeStruct((n_seg, half, D_MODEL), F32),
            jax.ShapeDtypeStruct((n_seg, half, D_MODEL), BF16),
            jax.ShapeDtypeStruct((n_seg, 1, D_MODEL), F32),
        ],
        scratch_shapes=[
            pltpu.VMEM((D_MODEL, D_MODEL), F32), pltpu.VMEM((1, D_MODEL), F32),
            pltpu.VMEM((n_seg, half, D_MODEL), BF16),
            pltpu.VMEM((n_seg, half, D_MODEL), BF16),
            pltpu.SemaphoreType.DMA((n_seg,)), pltpu.SemaphoreType.DMA((n_seg,)),
        ],
        compiler_params=_params(("arbitrary", "arbitrary")),
    )(h, dp)


def _input_grad(dpa, dpb, w4, tokens, head, dz2, norm_w, dw16, rows):
    tm = _tile(rows, 1040)
    nt = rows // tm
    assert nt >= 2, rows
    exchange = _GradExchange(SEGS_REC, with_blob=False)

    def body(dpa_ref, dpb_ref, w_ref, z_ref, head_ref, dz2_ref, nw_ref, dw_ref, gx_ref, dmeta_ref, dnw_ref, rxw_ref,
             acc_ref, dz_buf, out_sem, send_sems, recv_sems):
        i, k = pl.program_id(0), pl.program_id(1)

        def first_tile_out():
            return pltpu.make_async_copy(dz_buf.at[pl.ds(FIRST_TOKEN_ROW, tm - FIRST_TOKEN_ROW), :],
                                         gx_ref.at[pl.ds(0, tm - FIRST_TOKEN_ROW), :], out_sem)

        def tile_out(tile):
            start = pl.multiple_of(tile * tm - FIRST_TOKEN_ROW, HALO)
            return pltpu.make_async_copy(dz_buf, gx_ref.at[pl.ds(start, tm), :], out_sem)

        @pl.when((i == 0) & (k == 0))
        def _():
            exchange.start(dw_ref, rxw_ref, None, None, send_sems, recv_sems)
            dnw_ref[...] = jnp.zeros_like(dnw_ref)

        @pl.when((i == nt - 1) & (k == N_SEG - 1))
        def _():
            exchange.wait(dw_ref, rxw_ref, None, None, send_sems, recv_sems)

        @pl.when(k == 0)
        def _():
            acc_ref[...] = jnp.zeros_like(acc_ref)

        @pl.when(k < 3)
        def _():
            acc_ref[...] += _dot_nt(dpa_ref[0], w_ref[0])

        @pl.when(k >= 3)
        def _():
            acc_ref[...] += _dot_nt(dpb_ref[0], w_ref[0])

        @pl.when(k == N_SEG - 1)
        def _():
            zt = _padded_tile(z_ref[...], head_ref[...], i)
            rstd = lax.rsqrt(jnp.mean(zt * zt, axis=-1, keepdims=True) + EPS)
            zh = zt * rstd
            dh = acc_ref[...]
            dnw_ref[...] += jnp.sum(dh * zh, axis=0, keepdims=True)
            uu = dh * nw_ref[...]
            dz = dz2_ref[...] + rstd * (uu - zh * jnp.mean(uu * zh, axis=-1, keepdims=True))

            @pl.when(i == 1)
            def _():
                first_tile_out().wait()

            @pl.when(i >= 2)
            def _():
                tile_out(i - 1).wait()

            dz_buf[...] = dz

            @pl.when(i == 0)
            def _():
                dmeta_ref[...] = dz[PAD_ROWS:FIRST_TOKEN_ROW]
                first_tile_out().start()

            @pl.when(i > 0)
            def _():
                tile_out(i).start()

            @pl.when(i == nt - 1)
            def _():
                tile_out(i).wait()

    spec_a, spec_b = _seg_specs(tm, lambda i, k: i, lambda i, k: k)
    last_only = pl.BlockSpec((tm, D_MODEL), lambda i, k: (jnp.where(k == N_SEG - 1, i, 0), 0))
    return pl.pallas_call(
        body, name="input_grad",
        grid=(nt, N_SEG),
        in_specs=[
            spec_a, spec_b,
            pl.BlockSpec((1, D_MODEL, D_MODEL), lambda i, k: (k // 2, 0, k % 2)),
            _token_window(tm, lambda i, k: jnp.where(k == N_SEG - 1, i, 0)),
            pl.BlockSpec((FIRST_TOKEN_ROW, D_MODEL), lambda i, k: (0, 0)),
            last_only,
            pl.BlockSpec((1, D_MODEL), lambda i, k: (0, 0)),
            ANY,
        ],
        out_specs=[
            ANY,
            pl.BlockSpec((N_META, D_MODEL), lambda i, k: (0, 0)),
            pl.BlockSpec((1, D_MODEL), lambda i, k: (0, 0)),
            ANY,
        ],
        out_shape=[
            jax.ShapeDtypeStruct((rows - FIRST_TOKEN_ROW, D_MODEL), F32),
            jax.ShapeDtypeStruct((N_META, D_MODEL), F32),
            jax.ShapeDtypeStruct((1, D_MODEL), F32),
            exchange.landing_w(),
        ],
        scratch_shapes=[pltpu.VMEM((tm, D_MODEL), F32), pltpu.VMEM((tm, D_MODEL), F32),
                        pltpu.SemaphoreType.DMA] + exchange.semaphores(),
        compiler_params=_params(("arbitrary", "arbitrary")),
    )(dpa, dpb, w4, tokens, head, dz2, norm_w, dw16)


def _local_step(tokens, head, tgt, w4, blob4, seg_order, norm_w, b_in, lb_logits, hg_w, pool_scale, final_w):
    rows = FIRST_TOKEN_ROW + tokens.shape[0]
    q = D_MODEL // N_CHIPS
    n_grp = len(POOL_WINDOWS)
    pg = POOL_GDIM // N_CHIPS

    wexp2 = jnp.asarray(np.tile(_exponent_matrix(), (1, 2)), BF16)
    wexp_t = jnp.asarray(_exponent_matrix().T, BF16)
    masks2 = jnp.asarray(_paired_masks(), F32)

    h, p3, w4 = _in_proj(tokens, head, norm_w, w4, b_in, seg_order, rows)
    o, states, e16, a2, blob4 = _hgrn_forward(p3, lb_logits, wexp2, masks2, blob4, rows)
    wdh = blob4[:, 0:q].reshape(D_MODEL, D_MODEL)
    wdp = blob4[:, q:2 * q].reshape(D_MODEL, D_MODEL)
    wout = blob4[:, 2 * q:3 * q].reshape(D_MODEL, D_MODEL)
    poolw = blob4[:, 3 * q:].reshape(N_CHIPS, n_grp, pg, POOL_GDIM).transpose(1, 0, 2, 3)
    poolw = poolw.reshape(n_grp, POOL_GDIM, POOL_GDIM)
    d_o, dz2, dpb, dwdh, dwdp, dwout, dpw, small = _mixers(
        o, p3, tokens, head, tgt, wdh, wdp, wout, poolw, hg_w, pool_scale, final_w, rows)
    dpw4 = dpw.reshape(n_grp, N_CHIPS, pg, POOL_GDIM).transpose(1, 0, 2, 3)
    dpw4 = dpw4.reshape(N_CHIPS, n_grp * pg * POOL_GDIM // D_MODEL, D_MODEL)
    dblob4 = jnp.concatenate([dwdh.reshape(N_CHIPS, q, D_MODEL), dwdp.reshape(N_CHIPS, q, D_MODEL),
                              dwout.reshape(N_CHIPS, q, D_MODEL), dpw4], axis=1)

    dw_mix, dw_mix16, db_mix = _in_proj_weight_grad(h, dpb, rows, "in_proj_weight_grad_mix")
    dpa, dlb, rxw_mix, rx_blob = _hgrn_backward(
        p3, d_o, states, e16, a2, lb_logits, wexp_t, masks2, dw_mix16, dblob4.astype(BF16), rows)
    dw_rec, dw_rec16, db_rec = _in_proj_weight_grad(h, dpa, rows, "in_proj_weight_grad_rec")
    d_tokens, d_meta, dnw, rxw_rec = _input_grad(dpa, dpb, w4, tokens, head, dz2, norm_w, dw_rec16, rows)

    small = jnp.concatenate([
        small[ROW_LOSS:ROW_LOSS + 1],
        d_meta,
        dnw,
        db_rec.reshape(len(SEGS_REC), D_MODEL), db_mix.reshape(len(SEGS_MIX), D_MODEL),
        dlb, jnp.zeros_like(dlb),
        small[ROW_HG_W:ROW_HG_W + 1], small[ROW_POOL_SCALE:ROW_POOL_SCALE + 1],
        small[ROW_FINAL_W:ROW_FINAL_W + 1],
        jnp.zeros((SMALL_ROWS - ROW_FINAL_W - 1, D_MODEL), F32),
    ], axis=0)
    return d_tokens, (dw_rec, dw_mix, rxw_rec, rxw_mix), (dblob4, rx_blob), small


ANY = pl.BlockSpec(memory_space=pl.ANY)
MESH = pl.DeviceIdType.MESH


def _place():
    x, y, c = lax.axis_index("x"), lax.axis_index("y"), lax.axis_index("c")
    chips = [(1 - x, y), (x, 1 - y), (1 - x, 1 - y)]
    return x, y, c, chips


class _ShardGather:
    def __init__(self, rows):
        self.half = rows // 2

    def semaphores(self):
        return [pltpu.SemaphoreType.DMA((6,)), pltpu.SemaphoreType.DMA((6,))]

    def _copy(self, k, slot, to, send_sems, recv_sems):
        return pltpu.make_async_remote_copy(src_ref=slot, dst_ref=slot, send_sem=send_sems.at[k],
                                            recv_sem=recv_sems.at[k], device_id=to, device_id_type=MESH)

    def _half(self, ref4, chip, which):
        return ref4.at[chip, pl.ds(which * self.half, self.half), :]

    def start(self, ref4, send_sems, recv_sems, which=(0, 1, 2)):
        x, y, c, chips = _place()
        for j in which:
            cx, cy = chips[j]
            self._copy(j, self._half(ref4, 2 * x + y, c), (cx, cy, c), send_sems, recv_sems).start()

    def start_diagonal_after_neighbours(self, ref4, send_sems, recv_sems):
        x, y, c, chips = _place()
        for j in (0, 1):
            cx, cy = chips[j]
            self._copy(j, self._half(ref4, 2 * x + y, c), (cx, cy, c), send_sems, recv_sems).wait_send()
        self.start(ref4, send_sems, recv_sems, which=(2,))

    def pass_on(self, j, ref4, send_sems, recv_sems):
        x, y, c, chips = _place()
        cx, cy = chips[j]
        landed = self._half(ref4, 2 * cx + cy, c)
        self._copy(j, landed, (cx, cy, c), send_sems, recv_sems).wait_recv()
        self._copy(3 + j, landed, (x, y, 1 - c), send_sems, recv_sems).start()

    def await_sibling(self, j, ref4, send_sems, recv_sems):
        x, y, c, chips = _place()
        cx, cy = chips[j]
        self._copy(3 + j, self._half(ref4, 2 * cx + cy, 1 - c), (x, y, 1 - c), send_sems, recv_sems).wait_recv()

    def finish(self, ref4, send_sems, recv_sems, which=(0, 1, 2)):
        x, y, c, chips = _place()
        for j, (cx, cy) in enumerate(chips):
            if j in which:
                self._copy(j, self._half(ref4, 2 * x + y, c), (cx, cy, c), send_sems, recv_sems).wait_send()
            self._copy(3 + j, self._half(ref4, 2 * cx + cy, c), (x, y, 1 - c), send_sems, recv_sems).wait_send()


def _gather_meta(m4):
    def body(m_in_ref, m4_ref, send_sems, recv_sems):
        x, y, c, chips = _place()

        def copy(j, slot, to):
            return pltpu.make_async_remote_copy(src_ref=slot, dst_ref=slot, send_sem=send_sems.at[j],
                                                recv_sem=recv_sems.at[j], device_id=to, device_id_type=MESH)

        sends = [copy(j, m4_ref.at[2 * x + y], (cx, cy, c)) for j, (cx, cy) in enumerate(chips)]
        for cp in sends:
            cp.start()
        for j, (cx, cy) in enumerate(chips):
            copy(j, m4_ref.at[2 * cx + cy], (x, y, c)).wait_recv()
        for cp in sends:
            cp.wait_send()

    return pl.pallas_call(
        body, name="gather_meta",
        in_specs=[ANY], out_specs=ANY, out_shape=jax.ShapeDtypeStruct(m4.shape, m4.dtype),
        input_output_aliases={0: 0},
        scratch_shapes=[pltpu.SemaphoreType.DMA((3,)), pltpu.SemaphoreType.DMA((3,))],
    )(m4)


class _GradExchange:
    def __init__(self, segs, with_blob):
        self.segs = tuple(segs)
        self.with_blob = with_blob

    def landing_w(self):
        return jax.ShapeDtypeStruct((N_CHIPS, 2, D_MODEL // 2, D_MODEL), BF16)

    def landing_blob(self, blob16):
        return jax.ShapeDtypeStruct((N_DEV, blob16.shape[1] // 2, D_MODEL), BF16)

    def semaphores(self):
        n_send = len(self.segs) + (2 * N_CHIPS if self.with_blob else 0)
        n_recv = 2 * N_CHIPS + (N_DEV if self.with_blob else 0)
        return [pltpu.SemaphoreType.DMA((n_send,)), pltpu.SemaphoreType.DMA((n_recv,))]

    def _copies(self, dw_ref, rxw_ref, blob_ref, rxb_ref, send_sems, recv_sems):
        x, y, c = lax.axis_index("x"), lax.axis_index("y"), lax.axis_index("c")
        chip = 2 * x + y

        def relation(kx, ky, h):
            return (x ^ kx) * 4 + (y ^ ky) * 2 + (c ^ h)

        def copy(src, dst, send_k, recv_k, to):
            return functools.partial(pltpu.make_async_remote_copy, src_ref=src, dst_ref=dst,
                                     send_sem=send_sems.at[send_k], recv_sem=recv_sems.at[recv_k],
                                     device_id=to, device_id_type=MESH)

        sends, recvs = [], []
        for i, s in enumerate(self.segs):
            kx, ky = (s // 2) >> 1, (s // 2) & 1
            r = (x ^ kx) * 2 + (y ^ ky)
            sends.append((r != 0, copy(dw_ref.at[i], rxw_ref.at[r, s % 2], i, 2 * r + s % 2, (kx, ky, c))))
        for j in range(2):
            mine = [s // 2 for s in self.segs if s % 2 == j]
            if mine:
                cond = functools.reduce(lambda a, b: a | b, [chip == k for k in mine])
                for r in range(1, N_CHIPS):
                    slot = rxw_ref.at[r, j]
                    recvs.append((cond, copy(slot, slot, 0, 2 * r + j, (x, y, c))))
        if self.with_blob:
            hb = blob_ref.shape[1] // 2
            first_send, first_recv = len(self.segs), 2 * N_CHIPS
            for k in range(N_CHIPS):
                for h in range(2):
                    r = relation(k >> 1, k & 1, h)
                    sends.append((r != 0, copy(blob_ref.at[k, pl.ds(h * hb, hb), :], rxb_ref.at[r],
                                               first_send + 2 * k + h, first_recv + r, (k >> 1, k & 1, h))))
            for r in range(1, N_DEV):
                slot = rxb_ref.at[r]
                recvs.append((None, copy(slot, slot, 0, first_recv + r, (x, y, c))))
        return sends, recvs

    def start(self, *refs):
        sends, _ = self._copies(*refs)
        for cond, make in sends:
            pl.when(cond)(lambda make=make: make().start())

    def wait(self, *refs):
        sends, recvs = self._copies(*refs)
        for cond, make in sends:
            pl.when(cond)(lambda make=make: make().wait_send())
        for cond, make in recvs:
            if cond is None:
                make().wait_recv()
            else:
                pl.when(cond)(lambda make=make: make().wait_recv())


def _sum_landed(own, rx_ref):
    total = own
    for r in range(1, rx_ref.shape[0]):
        total = total + rx_ref[r, 0].astype(F32)
    return total


def _finish_w(dw_rec, dw_mix, rx_rec, rx_mix, place_arr):
    half = D_MODEL // 2
    tm = _tile(half, 256)
    n_rec = len(SEGS_REC)

    def body(place_ref, own_rec_ref, own_mix_ref, rx_rec_ref, rx_mix_ref, out_ref):
        seg = 2 * place_ref[0] + pl.program_id(0)

        @pl.when(seg < n_rec)
        def _():
            out_ref[0] = _sum_landed(own_rec_ref[0], rx_rec_ref)

        @pl.when(seg >= n_rec)
        def _():
            out_ref[0] = _sum_landed(own_mix_ref[0], rx_mix_ref)

    def own_spec(first, count):
        def index(j, i, place_ref):
            seg = 2 * place_ref[0] + j
            return (jnp.clip(seg - first, 0, count - 1), i, 0)
        return pl.BlockSpec((1, tm, D_MODEL), index)

    rx_spec = pl.BlockSpec((N_CHIPS, 1, tm, D_MODEL), lambda j, i, place_ref: (0, j, i, 0))
    return pl.pallas_call(
        body, name="finish_w",
        grid_spec=pltpu.PrefetchScalarGridSpec(
            num_scalar_prefetch=1, grid=(2, half // tm),
            in_specs=[own_spec(0, n_rec), own_spec(n_rec, len(SEGS_MIX)), rx_spec, rx_spec],
            out_specs=pl.BlockSpec((1, tm, D_MODEL), lambda j, i, place_ref: (place_ref[1], i, j))),
        out_shape=jax.ShapeDtypeStruct((2, half, 2 * D_MODEL), F32),
        compiler_params=_params(("arbitrary", "arbitrary")),
    )(place_arr, dw_rec, dw_mix, rx_rec, rx_mix)


def _finish_blob(dblob4, rx_blob, place_arr):
    n, rows, cols = rx_blob.shape
    tm = _tile(rows, 256)

    def body(place_ref, own_ref, rx_ref, out_ref):
        out_ref[0] = _sum_landed(own_ref[0, 0], rx_ref)

    return pl.pallas_call(
        body, name="finish_blob",
        grid_spec=pltpu.PrefetchScalarGridSpec(
            num_scalar_prefetch=1, grid=(rows // tm,),
            in_specs=[pl.BlockSpec((1, 1, tm, cols), lambda i, place_ref: (place_ref[0], place_ref[1], i, 0)),
                      pl.BlockSpec((n, 1, tm, cols), lambda i, place_ref: (0, 0, i, 0))],
            out_specs=pl.BlockSpec((1, tm, cols), lambda i, place_ref: (place_ref[1], i, 0))),
        out_shape=jax.ShapeDtypeStruct((2, rows, cols), F32),
        compiler_params=_params(("arbitrary",)),
    )(place_arr, dblob4.reshape(N_CHIPS, 2, rows, cols), rx_blob.reshape(n, 1, rows, cols))


def _share_finished(fw2, fb2, small):
    def body(w_in_ref, b_in_ref, small_ref, w_ref, b_ref, s_ref, bounce, local_sem, send_sems, recv_sems):
        x, y, c, _ = _place()
        sibling = (x, y, 1 - c)

        def copy(k, src, dst, to):
            return pltpu.make_async_remote_copy(src_ref=src, dst_ref=dst, send_sem=send_sems.at[k],
                                                recv_sem=recv_sems.at[k], device_id=to, device_id_type=MESH)

        sends = [copy(0, w_ref.at[c], w_ref.at[c], sibling), copy(1, b_ref.at[c], b_ref.at[c], sibling)]
        for r in range(1, N_DEV):
            peer = (x ^ ((r >> 2) & 1), y ^ ((r >> 1) & 1), c ^ (r & 1))
            sends.append(copy(1 + r, small_ref, s_ref.at[r], peer))
        for cp in sends:
            cp.start()
        for src, dst in ((small_ref, bounce), (bounce, s_ref.at[0])):
            own = pltpu.make_async_copy(src, dst, local_sem)
            own.start()
            own.wait()
        landed = [w_ref.at[1 - c], b_ref.at[1 - c]] + [s_ref.at[r] for r in range(1, N_DEV)]
        for k, slot in enumerate(landed):
            copy(k, slot, slot, (x, y, c)).wait_recv()
        for cp in sends:
            cp.wait_send()

    same = lambda a: jax.ShapeDtypeStruct(a.shape, a.dtype)
    n_sem = 2 + N_DEV - 1
    return pl.pallas_call(
        body, name="share_finished",
        in_specs=[ANY, ANY, ANY], out_specs=[ANY, ANY, ANY],
        out_shape=[same(fw2), same(fb2), jax.ShapeDtypeStruct((N_DEV,) + small.shape, F32)],
        input_output_aliases={0: 0, 1: 1},
        scratch_shapes=[pltpu.VMEM(small.shape, F32), pltpu.SemaphoreType.DMA,
                        pltpu.SemaphoreType.DMA((n_sem,)), pltpu.SemaphoreType.DMA((n_sem,))],
    )(fw2, fb2, small)


def _sum_small(slots, lb_logits, me_arr):
    def body(me_ref, slots_ref, lbl_ref, out_ref):
        me = me_ref[0]
        total = slots_ref[me]
        for d in range(1, N_DEV):
            total = total + slots_ref[d ^ me]
        out_ref[...] = total
        out_ref[ROW_LOSS:ROW_LOSS + 1, :] = jnp.broadcast_to(
            jnp.sum(total[ROW_LOSS:ROW_LOSS + 1, :], axis=-1, keepdims=True), (1, D_MODEL))
        lb = _lower_bound(lbl_ref[...])
        g0 = total[ROW_LB:ROW_LB + 1, :] * lb * (1.0 - lb)
        out_ref[ROW_LB:ROW_LB + 1, :] = g0
        out_ref[ROW_LB + 1:ROW_LB + 2, :] = -g0

    return pl.pallas_call(
        body, name="sum_small",
        grid_spec=pltpu.PrefetchScalarGridSpec(
            num_scalar_prefetch=1, grid=(1,),
            in_specs=[pl.BlockSpec((N_DEV, SMALL_ROWS, D_MODEL), lambda i, me_ref: (0, 0, 0)),
                      pl.BlockSpec((2, D_MODEL), lambda i, me_ref: (0, 0))],
            out_specs=pl.BlockSpec((SMALL_ROWS, D_MODEL), lambda i, me_ref: (0, 0))),
        out_shape=jax.ShapeDtypeStruct((SMALL_ROWS, D_MODEL), F32),
        compiler_params=_params(("arbitrary",)),
    )(me_arr, slots, lb_logits)


def _adamw(w, g, m, v):
    rows, cols = w.shape
    tm = _tile(rows, 256, mult=8) if rows % 8 == 0 else rows
    c1 = 1.0 / (1.0 - ADAM_B1 ** ADAM_STEP)
    c2 = 1.0 / (1.0 - ADAM_B2 ** ADAM_STEP)

    def body(w_ref, g_ref, m_ref, v_ref, d_ref, nm_ref, nv_ref):
        gt = g_ref[...]
        nm = ADAM_B1 * m_ref[...] + (1.0 - ADAM_B1) * gt
        nv = ADAM_B2 * v_ref[...] + (1.0 - ADAM_B2) * (gt * gt)
        nm_ref[...] = nm
        nv_ref[...] = nv
        d_ref[...] = -ADAM_LR * ((nm * c1) / (jnp.sqrt(nv * c2) + ADAM_EPS) + ADAM_WD * w_ref[...])

    blk = pl.BlockSpec((tm, cols), lambda i: (i, 0))
    sds = jax.ShapeDtypeStruct((rows, cols), F32)
    return pl.pallas_call(
        body, name="adamw",
        grid=(rows // tm,), in_specs=[blk] * 4, out_specs=[blk] * 3, out_shape=[sds] * 3,
        compiler_params=_params(("arbitrary",)),
    )(w, g, m, v)


def kernel(x, meta_tokens, norm_w, w_in, b_in, lb_logits, hg_norm_w, pool_w, pool_scale, w_down_hg, w_down_pool, w_out, final_norm_w, loss_target, m_meta_tokens, m_norm_w, m_w_in, m_b_in, m_lb_logits, m_hg_norm_w, m_pool_w, m_pool_scale, m_w_down_hg, m_w_down_pool, m_w_out, m_final_norm_w, v_meta_tokens, v_norm_w, v_w_in, v_b_in, v_lb_logits, v_hg_norm_w, v_pool_w, v_pool_scale, v_w_down_hg, v_w_down_pool, v_w_out, v_final_norm_w):
    seq = x.shape[1]
    xi, yi, ci = lax.axis_index("x"), lax.axis_index("y"), lax.axis_index("c")
    chip = 2 * xi + yi
    place_arr = jnp.stack([chip, ci]).astype(jnp.int32)
    me_arr = jnp.reshape(4 * xi + 2 * yi + ci, (1,)).astype(jnp.int32)
    q = D_MODEL // N_CHIPS

    def blob_of(wdh, wdp, wo, pw):
        return jnp.concatenate([wdh[0], wdp[0], wo[0], pw[0].reshape(-1, D_MODEL)], axis=0)

    def in_every_slot(a):
        return jnp.broadcast_to(a[None], (N_CHIPS,) + a.shape)

    meta4 = _gather_meta(in_every_slot(meta_tokens))
    meta_full = meta4.transpose(1, 0, 2).reshape(N_META, D_MODEL)
    w4 = in_every_slot(w_in[0].astype(BF16))
    blob4 = in_every_slot(blob_of(w_down_hg, w_down_pool, w_out, pool_w).astype(BF16))
    seg_order = jnp.stack([2 * (chip ^ rel) + t for rel in (0, 2, 1, 3) for t in (0, 1)]).astype(jnp.int32)

    head = jnp.concatenate([jnp.zeros((PAD_ROWS, D_MODEL), F32), meta_full], axis=0)
    fw2 = final_norm_w.reshape(1, D_MODEL)
    d_tokens, w_parts, blob_parts, small = _local_step(
        x[0], head, loss_target[0], w4, blob4, seg_order, norm_w, b_in, lb_logits, hg_norm_w, pool_scale, fw2)
    grad_x = d_tokens[None]

    fin_w = _finish_w(*w_parts, place_arr)
    fin_b = _finish_blob(*blob_parts, place_arr)
    gw2, gb2, slots = _share_finished(fin_w, fin_b, small)
    tot = _sum_small(slots, lb_logits, me_arr)
    g_w_in = gw2.reshape(D_MODEL, 2 * D_MODEL)
    g_blob = gb2.reshape(-1, D_MODEL)

    d_win, nm_win, nv_win = _adamw(w_in[0], g_w_in, m_w_in[0], v_w_in[0])
    d_blob, nm_blob, nv_blob = _adamw(
        blob_of(w_down_hg, w_down_pool, w_out, pool_w), g_blob,
        blob_of(m_w_down_hg, m_w_down_pool, m_w_out, m_pool_w),
        blob_of(v_w_down_hg, v_w_down_pool, v_w_out, v_pool_w))
    g_meta = lax.dynamic_slice_in_dim(tot[ROW_META:ROW_META + N_META], chip * q, q, axis=1)
    d_meta, nm_meta, nv_meta = _adamw(meta_tokens, g_meta, m_meta_tokens, v_meta_tokens)

    def rows_of(nw, bi, lbl, hg, ps, fw):
        return jnp.concatenate([nw, bi.reshape(N_SEG, D_MODEL), lbl, hg, ps, fw.reshape(1, D_MODEL),
                                jnp.zeros((2, D_MODEL), F32)], axis=0)

    g_rows = jnp.concatenate([tot[ROW_NORM_W:ROW_FINAL_W + 1], jnp.zeros((2, D_MODEL), F32)], axis=0)
    d_rows, nm_rows, nv_rows = _adamw(
        rows_of(norm_w, b_in, lb_logits, hg_norm_w, pool_scale, final_norm_w), g_rows,
        rows_of(m_norm_w, m_b_in, m_lb_logits, m_hg_norm_w, m_pool_scale, m_final_norm_w),
        rows_of(v_norm_w, v_b_in, v_lb_logits, v_hg_norm_w, v_pool_scale, v_final_norm_w))

    def unblob(b):
        return (b[0:q][None], b[q:2 * q][None], b[2 * q:3 * q][None], b[3 * q:].reshape(pool_w.shape))

    def unrows(r):
        o = ROW_NORM_W
        return dict(norm_w=r[ROW_NORM_W - o:ROW_B_IN - o], b_in=r[ROW_B_IN - o:ROW_LB - o].reshape(1, -1),
                    lb_logits=r[ROW_LB - o:ROW_HG_W - o], hg_norm_w=r[ROW_HG_W - o:ROW_POOL_SCALE - o],
                    pool_scale=r[ROW_POOL_SCALE - o:ROW_FINAL_W - o], final_norm_w=r[ROW_FINAL_W - o])

    def leaves(meta_part, rows_part, win_part, blob_part):
        r = unrows(rows_part)
        wdh, wdp, wo, pw = unblob(blob_part)
        return [meta_part, r["norm_w"], win_part[None], r["b_in"], r["lb_logits"], r["hg_norm_w"], pw,
                r["pool_scale"], wdh, wdp, wo, r["final_norm_w"]]

    loss = tot[ROW_LOSS, 0]
    return (loss, grad_x,
            *leaves(g_meta, g_rows, g_w_in, g_blob),
            *leaves(d_meta, d_rows, d_win, d_blob),
            *leaves(nm_meta, nm_rows, nm_win, nm_blob),
            *leaves(nv_meta, nv_rows, nv_win, nv_blob))
```

```python
import functools

import numpy as np
import jax
import jax.numpy as jnp
from jax import lax
from jax.experimental import pallas as pl
from jax.experimental.pallas import tpu as pltpu

F32 = jnp.float32
BF16 = jnp.bfloat16

D_MODEL = 1024
N_SEG = 8
N_HEADS = 8
HEAD_DIM = 128
CHUNK = 64
N_META = 16
PAD_ROWS = CHUNK - N_META
FIRST_TOKEN_ROW = CHUNK
LEVELS = (32, 16, 8, 4, 2, 1)
N_EXP = 2 + len(LEVELS)
POOL_WINDOWS = (2, 4, 8, 16)
POOL_GDIM = D_MODEL // len(POOL_WINDOWS)
HALO = 16
LOCAL_UNROLL = 13
BACKWARD_UNROLL = 13
EPS = 1e-6
N_CHIPS = 4
N_DEV = 8
SEGS_REC = (0, 1, 2)
SEGS_MIX = (3, 4, 5, 6, 7)

ADAM_LR = 0.001
ADAM_B1 = 0.9
ADAM_B2 = 0.999
ADAM_EPS = 1e-08
ADAM_WD = 0.01
ADAM_STEP = 10

VMEM_LIMIT_BYTES = 56 * 1024 * 1024

ROW_LOSS = 0
ROW_META = 1
ROW_NORM_W = ROW_META + N_META
ROW_B_IN = ROW_NORM_W + 1
ROW_LB = ROW_B_IN + N_SEG
ROW_HG_W = ROW_LB + 2
ROW_POOL_SCALE = ROW_HG_W + 1
ROW_FINAL_W = ROW_POOL_SCALE + 1
SMALL_ROWS = 32


def _tile(total, cap, mult=16):
    best = None
    for t in range(mult, min(total, cap) + 1, mult):
        if total % t == 0:
            best = t
    assert best is not None, (total, cap, mult)
    return best


def _token_window(tm, tile_of):
    def index(*grid):
        return (pl.multiple_of(jnp.maximum(tile_of(*grid) * tm - FIRST_TOKEN_ROW, 0), HALO), 0)
    return pl.BlockSpec((pl.Element(tm), pl.Element(D_MODEL)), index)


def _padded_tile(window, head, tile):
    first = jnp.concatenate([head, pltpu.roll(window, FIRST_TOKEN_ROW, 0)[FIRST_TOKEN_ROW:]], axis=0)
    return jnp.where(tile == 0, first, window)


def _params(sem=None):
    return pltpu.CompilerParams(dimension_semantics=sem, vmem_limit_bytes=VMEM_LIMIT_BYTES)


def _dot(a, b):
    return jnp.dot(a, b, preferred_element_type=F32)


def _dot_nt(a, b):
    return lax.dot_general(a, b, (((1,), (1,)), ((), ())), preferred_element_type=F32)


def _dot_tn(a, b):
    return lax.dot_general(a, b, (((0,), (0,)), ((), ())), preferred_element_type=F32)


def _sigmoid_pair(x):
    t = jnp.exp(-jnp.abs(x))
    r = 1.0 / (1.0 + t)
    pos = x >= 0
    return jnp.where(pos, r, t * r), jnp.where(pos, t * r, r)


def _exponent_matrix():
    t = np.arange(CHUNK)[:, None]
    j = np.arange(CHUNK)[None, :]
    blocks = [j <= t, j > t]
    for m in LEVELS:
        rho = (t // (2 * m)) * (2 * m) + m
        upper = (t >= rho) & (j > rho) & (j <= t)
        lower = (t < rho) & (j > t) & (j <= rho)
        blocks.append(upper | lower)
    return np.concatenate(blocks, axis=0).astype(np.float32)


def _pair_masks():
    t = np.arange(CHUNK)[:, None]
    s = np.arange(CHUNK)[None, :]
    masks = [t == s]
    for m in LEVELS:
        same = (t // (2 * m)) == (s // (2 * m))
        masks.append(same & ((t % (2 * m)) >= m) & ((s % (2 * m)) < m))
    return np.stack(masks).astype(np.float32)


LEVEL_PAIRS = ((0, 1), (2, 3), (4, 5), (6, None))


def _paired_masks():
    m = _pair_masks()
    zero = np.zeros_like(m[0])
    return np.stack([np.concatenate([m[a], zero if b is None else m[b]], axis=1) for a, b in LEVEL_PAIRS])


def _lower_bound(lbl):
    return 1.0 / (1.0 + jnp.exp(lbl[1:2, :] - lbl[0:1, :]))


def _in_proj(tokens, head, norm_w, w4, b_in, seg_order, rows):
    tm = _tile(rows, 1040)
    nt = rows // tm
    gather = _ShardGather(w4.shape[1])

    def body(order_ref, z_ref, head_ref, nw_ref, b_ref, w_in_ref, h_ref, p_ref, w4_ref,
             h_all, w_buf, w_sem, send_sems, recv_sems):
        kk, i = pl.program_id(0), pl.program_id(1)

        @pl.when((kk == 0) & (i == 0))
        def _():
            gather.start(w4_ref, send_sems, recv_sems, which=(0, 1))

        @pl.when((kk == 2) & (i == 0))
        def _():
            gather.start_diagonal_after_neighbours(w4_ref, send_sems, recv_sems)

        @pl.when(kk == 0)
        def _():
            zt = _padded_tile(z_ref[...], head_ref[...], i)
            rstd = lax.rsqrt(jnp.mean(zt * zt, axis=-1, keepdims=True) + EPS)
            h = (zt * rstd * nw_ref[...]).astype(BF16)
            h_all[pl.ds(pl.multiple_of(i * tm, 16), tm), :] = h
            h_ref[...] = h

        @pl.when((kk == 2) & (i == 0))
        def _():
            gather.pass_on(0, w4_ref, send_sems, recv_sems)
            gather.pass_on(1, w4_ref, send_sems, recv_sems)
            gather.await_sibling(0, w4_ref, send_sems, recv_sems)

        @pl.when((kk == 4) & (i == 0))
        def _():
            gather.await_sibling(1, w4_ref, send_sems, recv_sems)

        @pl.when((kk == 5) & (i == 0))
        def _():
            gather.pass_on(2, w4_ref, send_sems, recv_sems)

        @pl.when((kk == 6) & (i == 0))
        def _():
            gather.await_sibling(2, w4_ref, send_sems, recv_sems)

        def weights(which):
            seg = order_ref[2 * (kk // 2) + which]
            return pltpu.make_async_copy(
                w4_ref.at[seg // 2, :, pl.ds(pl.multiple_of((seg % 2) * D_MODEL, D_MODEL), D_MODEL)],
                w_buf.at[which], w_sem.at[which])

        @pl.when((i == 0) & (kk % 2 == 0))
        def _():
            weights(0).start()
            weights(1).start()
            weights(0).wait()

        @pl.when((i == 0) & (kk % 2 == 1))
        def _():
            weights(1).wait()

        p_ref[0] = _dot(h_all[pl.ds(pl.multiple_of(i * tm, 16), tm), :], w_buf[kk % 2]) + b_ref[...]

        @pl.when((kk == N_SEG - 1) & (i == nt - 1))
        def _():
            gather.finish(w4_ref, send_sems, recv_sems, which=(2,))

    first_pass = lambda kk, i, order_ref: (jnp.where(kk == 0, i, nt - 1), 0)
    return pl.pallas_call(
        body, name="in_proj",
        grid_spec=pltpu.PrefetchScalarGridSpec(
            num_scalar_prefetch=1, grid=(N_SEG, nt),
            in_specs=[
                _token_window(tm, lambda kk, i, order_ref: jnp.where(kk == 0, i, nt - 1)),
                pl.BlockSpec((FIRST_TOKEN_ROW, D_MODEL), lambda kk, i, order_ref: (0, 0)),
                pl.BlockSpec((1, D_MODEL), lambda kk, i, order_ref: (0, 0)),
                pl.BlockSpec((1, D_MODEL), lambda kk, i, order_ref: (0, order_ref[kk])),
                ANY,
            ],
            out_specs=[
                pl.BlockSpec((tm, D_MODEL), first_pass),
                pl.BlockSpec((1, tm, D_MODEL), lambda kk, i, order_ref: (order_ref[kk], i, 0)),
                ANY,
            ],
            scratch_shapes=[
                pltpu.VMEM((rows, D_MODEL), BF16),
                pltpu.VMEM((2, D_MODEL, D_MODEL), BF16),
                pltpu.SemaphoreType.DMA((2,)),
            ] + gather.semaphores()),
        out_shape=[
            jax.ShapeDtypeStruct((rows, D_MODEL), BF16),
            jax.ShapeDtypeStruct((N_SEG, rows, D_MODEL), F32),
            jax.ShapeDtypeStruct(w4.shape, w4.dtype),
        ],
        input_output_aliases={5: 2},
        compiler_params=_params(("arbitrary", "arbitrary")),
    )(seg_order, tokens, head, norm_w, b_in, w4)


def _hgrn_forward(p3, lb_logits, wexp2, masks2, blob4, rows):
    n_chunks = rows // CHUNK
    cpb = _tile(n_chunks, 13, mult=1)
    rb_rows = cpb * CHUNK
    n_rb = n_chunks // cpb
    lanes = cpb * HEAD_DIM
    gather = _ShardGather(blob4.shape[1])

    def body(q_ref, fz_ref, v_ref, lbl_ref, wexp_ref, mask_ref, b_in_ref, o_ref, s_ref, e16_ref, a2_ref, b4_ref,
             st_ref, e_ref, u_ref, q_s, kk_s, v_s, send_sems, recv_sems):
        rb = pl.program_id(1)

        @pl.when((pl.program_id(0) == 0) & (rb == 0))
        def _():
            gather.start(b4_ref, send_sems, recv_sems)

        @pl.when(rb == 0)
        def _():
            st_ref[...] = jnp.zeros_like(st_ref)

        lb = _lower_bound(lbl_ref[...])
        row = rb * rb_rows + lax.broadcasted_iota(jnp.int32, (rb_rows, 1), 0)
        valid = row >= PAD_ROWS
        sg, sn = _sigmoid_pair(fz_ref[0])
        g = jnp.where(valid, jnp.log(lb + (1.0 - lb) * sg), 0.0)
        kk_s[...] = jnp.where(valid, (1.0 - lb) * sn, 0.0)
        q_s[...] = jnp.where(valid, q_ref[0], 0.0)
        v_s[...] = jnp.where(valid, v_ref[0], 0.0).astype(BF16)
        hi = g.astype(BF16)
        mid = (g - hi.astype(F32)).astype(BF16)
        g2 = jnp.concatenate(
            [jnp.concatenate([hi[b * CHUNK:(b + 1) * CHUNK], mid[b * CHUNK:(b + 1) * CHUNK]], axis=0)
             for b in range(cpb)], axis=1)
        e_ref[...] = jnp.exp(_dot(wexp_ref[...], g2))
        e16_ref[0, 0] = e_ref[...].astype(BF16)

        def contribution(b, carry):
            r0 = pl.multiple_of(b * CHUNK, CHUNK)
            l0 = pl.multiple_of(b * HEAD_DIM, HEAD_DIM)
            kc16 = (kk_s[pl.ds(r0, CHUNK), :] * e_ref[CHUNK:2 * CHUNK, pl.ds(l0, HEAD_DIM)]).astype(BF16)
            u_ref[b] = _dot_tn(v_s[pl.ds(r0, CHUNK), :], kc16)
            return carry

        lax.fori_loop(0, cpb, contribution, 0, unroll=LOCAL_UNROLL)

        def recur(b, st):
            l0 = pl.multiple_of(b * HEAD_DIM, HEAD_DIM)
            s_ref[0, b] = st
            return st * e_ref[CHUNK - 1:CHUNK, pl.ds(l0, HEAD_DIM)] + u_ref[b]

        st_ref[...] = lax.fori_loop(0, cpb, recur, st_ref[...])

        zeros16 = jnp.zeros((CHUNK, HEAD_DIM), BF16)

        def local(b, carry):
            r0 = pl.multiple_of(b * CHUNK, CHUNK)
            l0 = pl.multiple_of(b * HEAD_DIM, HEAD_DIM)
            q = q_s[pl.ds(r0, CHUNK), :]
            kk = kk_s[pl.ds(r0, CHUNK), :]
            v16 = v_s[pl.ds(r0, CHUNK), :]

            def scaled(entry):
                if entry == 0:
                    return q.astype(BF16), kk.astype(BF16)
                e_m = e_ref[(1 + entry) * CHUNK:(2 + entry) * CHUNK, pl.ds(l0, HEAD_DIM)]
                return (q * e_m).astype(BF16), (kk * e_m).astype(BF16)

            a2 = jnp.zeros((CHUNK, 2 * CHUNK), F32)
            for p, (ea, eb) in enumerate(LEVEL_PAIRS):
                qa, ka = scaled(ea)
                if eb is None:
                    prod = _dot_nt(qa, jnp.concatenate([ka, zeros16], axis=0))
                else:
                    qb_, kb_ = scaled(eb)
                    rhs = jnp.concatenate([jnp.concatenate([ka, zeros16], axis=1),
                                           jnp.concatenate([zeros16, kb_], axis=1)], axis=0)
                    prod = _dot_nt(jnp.concatenate([qa, qb_], axis=1), rhs)
                a2 = a2 + mask_ref[p] * prod
            a2_16 = a2.astype(BF16)
            a2_ref[pl.ds(r0, CHUNK), :] = a2_16
            qb16 = (q * e_ref[0:CHUNK, pl.ds(l0, HEAD_DIM)]).astype(BF16)
            o_ref[pl.ds(r0, CHUNK), :] = (_dot(a2_16, jnp.concatenate([v16, v16], axis=0))
                                          + _dot_nt(qb16, s_ref[0, b].astype(BF16)))
            return carry

        lax.fori_loop(0, cpb, local, 0, unroll=LOCAL_UNROLL)

        @pl.when((pl.program_id(0) == N_HEADS // 2) & (rb == 0))
        def _():
            for j in range(N_CHIPS - 1):
                gather.pass_on(j, b4_ref, send_sems, recv_sems)

        @pl.when((pl.program_id(0) == N_HEADS - 1) & (rb == n_rb - 1))
        def _():
            for j in range(N_CHIPS - 1):
                gather.await_sibling(j, b4_ref, send_sems, recv_sems)
            gather.finish(b4_ref, send_sems, recv_sems)

    head_block = lambda seg: pl.BlockSpec((1, rb_rows, HEAD_DIM), lambda h, r: (seg, r, h))
    return pl.pallas_call(
        body, name="hgrn_forward",
        grid=(N_HEADS, n_rb),
        in_specs=[
            head_block(0), head_block(1), head_block(2),
            pl.BlockSpec((2, HEAD_DIM), lambda h, r: (0, h)),
            pl.BlockSpec((N_EXP * CHUNK, 2 * CHUNK), lambda h, r: (0, 0)),
            pl.BlockSpec((len(LEVEL_PAIRS), CHUNK, 2 * CHUNK), lambda h, r: (0, 0, 0)),
            ANY,
        ],
        out_specs=[
            pl.BlockSpec((rb_rows, HEAD_DIM), lambda h, r: (r, h)),
            pl.BlockSpec((1, cpb, HEAD_DIM, HEAD_DIM), lambda h, r: (h, r, 0, 0)),
            pl.BlockSpec((1, 1, N_EXP * CHUNK, lanes), lambda h, r: (h, r, 0, 0)),
            pl.BlockSpec((rb_rows, HEAD_DIM), lambda h, r: (r, h)),
            ANY,
        ],
        out_shape=[
            jax.ShapeDtypeStruct((rows, D_MODEL), F32),
            jax.ShapeDtypeStruct((N_HEADS, n_chunks, HEAD_DIM, HEAD_DIM), F32),
            jax.ShapeDtypeStruct((N_HEADS, n_rb, N_EXP * CHUNK, lanes), BF16),
            jax.ShapeDtypeStruct((rows, D_MODEL), BF16),
            jax.ShapeDtypeStruct(blob4.shape, blob4.dtype),
        ],
        input_output_aliases={6: 4},
        scratch_shapes=[
            pltpu.VMEM((HEAD_DIM, HEAD_DIM), F32),
            pltpu.VMEM((N_EXP * CHUNK, lanes), F32),
            pltpu.VMEM((cpb, HEAD_DIM, HEAD_DIM), F32),
            pltpu.VMEM((rb_rows, HEAD_DIM), F32),
            pltpu.VMEM((rb_rows, HEAD_DIM), F32),
            pltpu.VMEM((rb_rows, HEAD_DIM), BF16),
        ] + gather.semaphores(),
        compiler_params=_params(("arbitrary", "arbitrary")),
    )(p3, p3, p3, lb_logits, wexp2, masks2, blob4)


def _hgrn_backward(p3, d_o, states, e16, a2, lb_logits, wexp_t, masks2, dw16, blob16, rows):
    n_chunks = rows // CHUNK
    cpb = _tile(n_chunks, 13, mult=1)
    rb_rows = cpb * CHUNK
    n_rb = n_chunks // cpb
    lanes = cpb * HEAD_DIM
    exchange = _GradExchange(SEGS_MIX, with_blob=True)

    def body(q_ref, fz_ref, v_ref, do_ref, s_ref, e_ref, a2_ref, lbl_ref, wexpt_ref, mask_ref, dw_ref, blob_ref,
             dp_ref, dlb_ref, rxw_ref, rxb_ref,
             dst_ref, g_ref, dsn_ref, q_s, kk_s, v_s, do_s, dq_s, dkk_s, dg_s, dx_s, send_sems, recv_sems):
        step = pl.program_id(1)
        rb = n_rb - 1 - step

        @pl.when((pl.program_id(0) == 0) & (step == 0))
        def _():
            exchange.start(dw_ref, rxw_ref, blob_ref, rxb_ref, send_sems, recv_sems)

        @pl.when(step == 0)
        def _():
            dst_ref[...] = jnp.zeros_like(dst_ref)
            dlb_ref[...] = jnp.zeros_like(dlb_ref)

        lb = _lower_bound(lbl_ref[...])
        row = rb * rb_rows + lax.broadcasted_iota(jnp.int32, (rb_rows, 1), 0)
        valid = row >= PAD_ROWS
        sg, sn = _sigmoid_pair(fz_ref[0])
        f = lb + (1.0 - lb) * sg
        g = jnp.where(valid, jnp.log(f), 0.0)
        kk_s[...] = jnp.where(valid, (1.0 - lb) * sn, 0.0)
        q_s[...] = jnp.where(valid, q_ref[0], 0.0)
        v_s[...] = jnp.where(valid, v_ref[0], 0.0).astype(BF16)
        do_s[...] = do_ref[...].astype(BF16)
        e_last_all = jnp.exp(jnp.concatenate(
            [jnp.sum(g[b * CHUNK:(b + 1) * CHUNK], axis=0, keepdims=True) for b in range(cpb)], axis=0))
        last_row = lax.broadcasted_iota(jnp.int32, (CHUNK, 1), 0) == CHUNK - 1
        zeros16 = jnp.zeros((CHUNK, HEAD_DIM), BF16)

        def factor(block, l0):
            return e_ref[0, 0, block * CHUNK:(block + 1) * CHUNK, pl.ds(l0, HEAD_DIM)].astype(F32)

        def contribution(b, carry):
            r0 = pl.multiple_of(b * CHUNK, CHUNK)
            l0 = pl.multiple_of(b * HEAD_DIM, HEAD_DIM)
            qb16 = (q_s[pl.ds(r0, CHUNK), :] * factor(0, l0)).astype(BF16)
            g_ref[b] = _dot_tn(do_s[pl.ds(r0, CHUNK), :], qb16)
            return carry

        lax.fori_loop(0, cpb, contribution, 0, unroll=LOCAL_UNROLL)

        cur = dst_ref[...]
        for b in reversed(range(cpb)):
            dsn_ref[b] = cur
            cur = cur * e_last_all[b:b + 1, :] + g_ref[b]
        dst_ref[...] = cur

        def local(b, carry):
            r0 = pl.multiple_of(b * CHUNK, CHUNK)
            l0 = pl.multiple_of(b * HEAD_DIM, HEAD_DIM)
            q = q_s[pl.ds(r0, CHUNK), :]
            kk = kk_s[pl.ds(r0, CHUNK), :]
            v16 = v_s[pl.ds(r0, CHUNK), :]
            do16 = do_s[pl.ds(r0, CHUNK), :]
            st = s_ref[0, b]
            dsn = dsn_ref[b]
            dsn16 = dsn.astype(BF16)
            e_b, e_c = factor(0, l0), factor(1, l0)
            qb, kc = q * e_b, kk * e_c

            t = _dot_tn(a2_ref[pl.ds(r0, CHUNK), :], do16)
            dv = t[0:CHUNK] + t[CHUNK:2 * CHUNK] + _dot_nt(kc.astype(BF16), dsn16)
            dp_ref[2, pl.ds(r0, CHUNK), :] = dv.astype(BF16)
            da2 = _dot_nt(do16, jnp.concatenate([v16, v16], axis=0))
            dqb = _dot(do16, st.astype(BF16))
            dkc = _dot(v16, dsn16)
            de = jnp.sum(dsn * st, axis=0, keepdims=True) * e_b[CHUNK - 1:CHUNK, :]
            dq = e_b * dqb
            dkk = e_c * dkc
            dx_s[0:CHUNK, pl.ds(l0, HEAD_DIM)] = (qb * dqb + jnp.where(last_row, de, 0.0)).astype(BF16)
            dx_s[CHUNK:2 * CHUNK, pl.ds(l0, HEAD_DIM)] = (kc * dkc).astype(BF16)

            def scaled(entry):
                if entry == 0:
                    return q, kk, None
                e_m = factor(1 + entry, l0)
                return q * e_m, kk * e_m, e_m

            for p, (ea, eb) in enumerate(LEVEL_PAIRS):
                dm = mask_ref[p] * da2
                dm_t = dm.T.astype(BF16)
                qa, ka, e_a = scaled(ea)
                if eb is None:
                    rhs_k = jnp.concatenate([jnp.concatenate([ka.astype(BF16), zeros16], axis=1),
                                             jnp.concatenate([zeros16, zeros16], axis=1)], axis=0)
                else:
                    qb_, kb_, e_bb = scaled(eb)
                    rhs_k = jnp.concatenate([jnp.concatenate([ka.astype(BF16), zeros16], axis=1),
                                             jnp.concatenate([zeros16, kb_.astype(BF16)], axis=1)], axis=0)
                dq2 = _dot(dm.astype(BF16), rhs_k)
                parts = [(ea, qa, ka, e_a, dq2[:, :HEAD_DIM], _dot(dm_t[0:CHUNK], qa.astype(BF16)))]
                if eb is not None:
                    parts.append((eb, qb_, kb_, e_bb, dq2[:, HEAD_DIM:],
                                  _dot(dm_t[CHUNK:2 * CHUNK], qb_.astype(BF16))))
                for entry, q_m, k_m, e_m, dq_m, dk_m in parts:
                    if entry == 0:
                        dq = dq + dq_m
                        dkk = dkk + dk_m
                    else:
                        dq = dq + e_m * dq_m
                        dkk = dkk + e_m * dk_m
                        dx_s[(1 + entry) * CHUNK:(2 + entry) * CHUNK, pl.ds(l0, HEAD_DIM)] = (
                            q_m * dq_m + k_m * dk_m).astype(BF16)
            dq_s[pl.ds(r0, CHUNK), :] = dq
            dkk_s[pl.ds(r0, CHUNK), :] = dkk
            return carry

        lax.fori_loop(0, cpb, local, 0, unroll=BACKWARD_UNROLL)

        dg_all = _dot(wexpt_ref[...], dx_s[...])
        for b in range(cpb):
            dg_s[b * CHUNK:(b + 1) * CHUNK, :] = dg_all[:, b * HEAD_DIM:(b + 1) * HEAD_DIM]
        t = jnp.where(valid, dg_s[...] / f - dkk_s[...], 0.0)
        dlb_ref[...] += jnp.sum(sn * t, axis=0, keepdims=True)
        dp_ref[0] = jnp.where(valid, dq_s[...], 0.0).astype(BF16)
        dp_ref[1] = ((1.0 - lb) * sg * sn * t).astype(BF16)

        @pl.when((pl.program_id(0) == N_HEADS - 1) & (step == n_rb - 1))
        def _():
            exchange.wait(dw_ref, rxw_ref, blob_ref, rxb_ref, send_sems, recv_sems)

    head_block = lambda seg: pl.BlockSpec((1, rb_rows, HEAD_DIM), lambda h, s: (seg, n_rb - 1 - s, h))
    row_block = pl.BlockSpec((rb_rows, HEAD_DIM), lambda h, s: (n_rb - 1 - s, h))
    return pl.pallas_call(
        body, name="hgrn_backward",
        grid=(N_HEADS, n_rb),
        in_specs=[
            head_block(0), head_block(1), head_block(2),
            row_block,
            pl.BlockSpec((1, cpb, HEAD_DIM, HEAD_DIM), lambda h, s: (h, n_rb - 1 - s, 0, 0)),
            pl.BlockSpec((1, 1, N_EXP * CHUNK, lanes), lambda h, s: (h, n_rb - 1 - s, 0, 0)),
            row_block,
            pl.BlockSpec((2, HEAD_DIM), lambda h, s: (0, h)),
            pl.BlockSpec((CHUNK, N_EXP * CHUNK), lambda h, s: (0, 0)),
            pl.BlockSpec((len(LEVEL_PAIRS), CHUNK, 2 * CHUNK), lambda h, s: (0, 0, 0)),
            ANY, ANY,
        ],
        out_specs=[
            pl.BlockSpec((3, rb_rows, HEAD_DIM), lambda h, s: (0, n_rb - 1 - s, h)),
            pl.BlockSpec((1, HEAD_DIM), lambda h, s: (0, h)),
            ANY, ANY,
        ],
        out_shape=[
            jax.ShapeDtypeStruct((3, rows, D_MODEL), BF16),
            jax.ShapeDtypeStruct((1, D_MODEL), F32),
            exchange.landing_w(), exchange.landing_blob(blob16),
        ],
        scratch_shapes=[
            pltpu.VMEM((HEAD_DIM, HEAD_DIM), F32),
            pltpu.VMEM((cpb, HEAD_DIM, HEAD_DIM), F32),
            pltpu.VMEM((cpb, HEAD_DIM, HEAD_DIM), F32),
            pltpu.VMEM((rb_rows, HEAD_DIM), F32),
            pltpu.VMEM((rb_rows, HEAD_DIM), F32),
            pltpu.VMEM((rb_rows, HEAD_DIM), BF16),
            pltpu.VMEM((rb_rows, HEAD_DIM), BF16),
            pltpu.VMEM((rb_rows, HEAD_DIM), F32),
            pltpu.VMEM((rb_rows, HEAD_DIM), F32),
            pltpu.VMEM((rb_rows, HEAD_DIM), F32),
            pltpu.VMEM((N_EXP * CHUNK, lanes), BF16),
        ] + exchange.semaphores(),
        compiler_params=_params(("arbitrary", "arbitrary")),
    )(p3, p3, p3, d_o, states, e16, a2, lb_logits, wexp_t, masks2, dw16, blob16)


def _sigmoid(x):
    return 1.0 / (1.0 + jnp.exp(-x))


def _silu_and_grad(x):
    s = _sigmoid(x)
    return x * s, s * (1.0 + x * (1.0 - s))


def _window_sum(ext, width, forward_looking):
    n = ext.shape[0]
    s = ext
    step = 1
    while step < width:
        s = s + pltpu.roll(s, (n - step) if forward_looking else step, 0)
        step *= 2
    return s


def _mixers(o, p3, tokens, head, tgt, wdh, wdp, wout, poolw, hg_w, pool_scale, final_w, rows):
    tm = _tile(rows, 208)
    nt = rows // tm
    halo_blocks = tm // HALO
    n_grp = len(POOL_WINDOWS)

    def body(o_ref, ghg_ref, u_ref, gpl_ref, mhg_ref, mpl_ref, uh_ref, z_ref, t_ref,
             wdh_ref, wdp_ref, wout_ref, pw_ref, hgw_ref, ps_ref, fw_ref, head_ref,
             do_ref, dz2_ref, dp_ref, dwdh_ref, dwdp_ref, dwout_ref, dpw_ref, small_ref, carry_ref):
        step = pl.program_id(0)
        tile = nt - 1 - step

        @pl.when(step == 0)
        def _():
            dwdh_ref[...] = jnp.zeros_like(dwdh_ref)
            dwdp_ref[...] = jnp.zeros_like(dwdp_ref)
            dwout_ref[...] = jnp.zeros_like(dwout_ref)
            dpw_ref[...] = jnp.zeros_like(dpw_ref)
            small_ref[...] = jnp.zeros_like(small_ref)
            carry_ref[...] = jnp.zeros_like(carry_ref)

        row = tile * tm + lax.broadcasted_iota(jnp.int32, (tm, 1), 0)
        real = row >= PAD_ROWS
        pos1 = jnp.maximum(row - PAD_ROWS + 1, 1).astype(F32)

        u = jnp.where(real, u_ref[0], 0.0)
        halo_row = tile * tm - HALO + lax.broadcasted_iota(jnp.int32, (HALO, 1), 0)
        uh = jnp.where(halo_row >= PAD_ROWS, uh_ref[0], 0.0)
        ext = jnp.concatenate([uh, u], axis=0)
        pooled, inv_cnt, mixed = [], [], []
        for g, w in enumerate(POOL_WINDOWS):
            cols = slice(g * POOL_GDIM, (g + 1) * POOL_GDIM)
            inv = 1.0 / jnp.minimum(pos1, float(w))
            ws = _window_sum(ext[:, cols], w, False)[HALO:]
            pg = (ws * inv - u[:, cols]).astype(BF16)
            pooled.append(pg)
            inv_cnt.append(inv)
            mixed.append(_dot(pg, pw_ref[g]))
        mixed = jnp.concatenate(mixed, axis=1)
        gpl = gpl_ref[0]
        sp, dsp = _silu_and_grad(gpl)
        ps = ps_ref[...]
        a_pool = (mixed * ps * sp).astype(BF16)
        y_pool = _dot(a_pool, wdp_ref[...])

        o = o_ref[...]
        o_hat, rstd_h = [], []
        for h in range(N_HEADS):
            oh = o[:, h * HEAD_DIM:(h + 1) * HEAD_DIM]
            r = lax.rsqrt(jnp.mean(oh * oh, axis=-1, keepdims=True) + EPS)
            rstd_h.append(r)
            o_hat.append(oh * r)
        o_hat = jnp.concatenate(o_hat, axis=1)
        hgw = hgw_ref[...]
        o_n = o_hat * hgw
        ghg = ghg_ref[0]
        sh, dsh = _silu_and_grad(ghg)
        a_hg = (o_n * sh).astype(BF16)
        y_hg = _dot(a_hg, wdh_ref[...])

        s_mh = _sigmoid(mhg_ref[0])
        s_mp = _sigmoid(mpl_ref[0])
        merged = (s_mh * y_hg + s_mp * y_pool).astype(BF16)
        z2 = _padded_tile(z_ref[...], head_ref[...], tile) + _dot(merged, wout_ref[...])
        rstd2 = lax.rsqrt(jnp.mean(z2 * z2, axis=-1, keepdims=True) + EPS)
        zh = z2 * rstd2
        fw = fw_ref[...]
        target = _padded_tile(t_ref[...], jnp.zeros((FIRST_TOKEN_ROW, D_MODEL), F32), tile)
        err = jnp.where(row >= FIRST_TOKEN_ROW, zh * fw - target, 0.0)
        small_ref[ROW_LOSS:ROW_LOSS + 1, :] += jnp.sum(err * err, axis=0, keepdims=True) * (0.5 / D_MODEL)
        dy = err * (1.0 / D_MODEL)

        small_ref[ROW_FINAL_W:ROW_FINAL_W + 1, :] += jnp.sum(dy * zh, axis=0, keepdims=True)
        uu = dy * fw
        dz2 = rstd2 * (uu - zh * jnp.mean(uu * zh, axis=-1, keepdims=True))
        dz2_ref[...] = dz2
        dz2_16 = dz2.astype(BF16)
        dmerged = _dot_nt(dz2_16, wout_ref[...])
        dwout_ref[...] += _dot_tn(merged, dz2_16)
        dy_hg = (s_mh * dmerged).astype(BF16)
        dy_pool = (s_mp * dmerged).astype(BF16)
        dp_ref[3] = (dmerged * y_hg * s_mh * (1.0 - s_mh)).astype(BF16)
        dp_ref[4] = (dmerged * y_pool * s_mp * (1.0 - s_mp)).astype(BF16)

        da_hg = _dot_nt(dy_hg, wdh_ref[...])
        dwdh_ref[...] += _dot_tn(a_hg, dy_hg)
        dp_ref[0] = (da_hg * o_n * dsh).astype(BF16)
        do_n = da_hg * sh
        small_ref[ROW_HG_W:ROW_HG_W + 1, :] += jnp.sum(do_n * o_hat, axis=0, keepdims=True)
        d_hat = do_n * hgw
        for h in range(N_HEADS):
            cols = slice(h * HEAD_DIM, (h + 1) * HEAD_DIM)
            dh_, oh_ = d_hat[:, cols], o_hat[:, cols]
            do_ref[:, cols] = rstd_h[h] * (dh_ - oh_ * jnp.mean(dh_ * oh_, axis=-1, keepdims=True))

        da_pool = _dot_nt(dy_pool, wdp_ref[...])
        dwdp_ref[...] += _dot_tn(a_pool, dy_pool)
        small_ref[ROW_POOL_SCALE:ROW_POOL_SCALE + 1, :] += jnp.sum(da_pool * mixed * sp, axis=0, keepdims=True)
        dp_ref[2] = (da_pool * mixed * ps * dsp).astype(BF16)
        dmixed = (da_pool * ps * sp).astype(BF16)
        carry = carry_ref[...]
        du, new_carry = [], []
        for g, w in enumerate(POOL_WINDOWS):
            cols = slice(g * POOL_GDIM, (g + 1) * POOL_GDIM)
            dmg = dmixed[:, cols]
            dpooled = _dot_nt(dmg, pw_ref[g])
            dpw_ref[g] += _dot_tn(pooled[g], dmg)
            dps = dpooled * inv_cnt[g]
            ext_b = jnp.concatenate([dps, carry[:, cols]], axis=0)
            du.append(_window_sum(ext_b, w, True)[:tm] - dpooled)
            new_carry.append(dps[:HALO])
        dp_ref[1] = jnp.where(real, jnp.concatenate(du, axis=1), 0.0).astype(BF16)
        carry_ref[...] = jnp.concatenate(new_carry, axis=1)

    row_block = pl.BlockSpec((tm, D_MODEL), lambda s: (nt - 1 - s, 0))
    seg_block = lambda seg: pl.BlockSpec((1, tm, D_MODEL), lambda s: (seg, nt - 1 - s, 0))
    whole = pl.BlockSpec(memory_space=pltpu.VMEM)
    return pl.pallas_call(
        body, name="mixers",
        grid=(nt,),
        in_specs=[
            row_block, seg_block(3), seg_block(4), seg_block(5), seg_block(6), seg_block(7),
            pl.BlockSpec((1, HALO, D_MODEL),
                         lambda s: (4, jnp.maximum((nt - 1 - s) * halo_blocks - 1, 0), 0)),
            _token_window(tm, lambda s: nt - 1 - s), _token_window(tm, lambda s: nt - 1 - s),
            whole, whole, whole, whole, whole, whole, whole, whole,
        ],
        out_specs=[
            row_block, row_block,
            pl.BlockSpec((5, tm, D_MODEL), lambda s: (0, nt - 1 - s, 0)),
            whole, whole, whole, whole, whole,
        ],
        out_shape=[
            jax.ShapeDtypeStruct((rows, D_MODEL), F32),
            jax.ShapeDtypeStruct((rows, D_MODEL), F32),
            jax.ShapeDtypeStruct((5, rows, D_MODEL), BF16),
            jax.ShapeDtypeStruct((D_MODEL, D_MODEL), F32),
            jax.ShapeDtypeStruct((D_MODEL, D_MODEL), F32),
            jax.ShapeDtypeStruct((D_MODEL, D_MODEL), F32),
            jax.ShapeDtypeStruct((n_grp, POOL_GDIM, POOL_GDIM), F32),
            jax.ShapeDtypeStruct((SMALL_ROWS, D_MODEL), F32),
        ],
        scratch_shapes=[pltpu.VMEM((HALO, D_MODEL), F32)],
        compiler_params=_params(("arbitrary",)),
    )(o, p3, p3, p3, p3, p3, p3, tokens, tgt, wdh, wdp, wout, poolw, hg_w, pool_scale, final_w, head)


def _seg_specs(tm, row_of, seg_of):
    def spec_a(*g):
        k = seg_of(*g)
        return (jnp.minimum(k, 2), jnp.where(k < 3, row_of(*g), 0), 0)

    def spec_b(*g):
        k = seg_of(*g)
        return (jnp.maximum(k - 3, 0), jnp.where(k >= 3, row_of(*g), 0), 0)

    return pl.BlockSpec((1, tm, D_MODEL), spec_a), pl.BlockSpec((1, tm, D_MODEL), spec_b)


def _in_proj_weight_grad(h, dp, rows, name):
    n_seg = dp.shape[0]
    tm = _tile(rows, 1040)
    nt = rows // tm
    half = D_MODEL // 2

    def body(h_ref, dp_ref, part_ref, part16_ref, db_ref, acc_ref, bacc_ref, stage_ref, land_ref,
             send_sems, recv_sems):
        k, i = pl.program_id(0), pl.program_id(1)
        x, y, c = lax.axis_index("x"), lax.axis_index("y"), lax.axis_index("c")

        def to_sibling(seg):
            return pltpu.make_async_remote_copy(
                src_ref=stage_ref.at[seg], dst_ref=land_ref.at[seg], send_sem=send_sems.at[seg],
                recv_sem=recv_sems.at[seg], device_id=(x, y, 1 - c), device_id_type=MESH)

        @pl.when(i == 0)
        def _():
            acc_ref[...] = jnp.zeros_like(acc_ref)
            bacc_ref[...] = jnp.zeros_like(bacc_ref)

        dpt = dp_ref[0]
        acc_ref[...] += _dot_tn(h_ref[...], dpt)
        bacc_ref[...] += jnp.sum(dpt.astype(F32), axis=0, keepdims=True)

        @pl.when(i == nt - 1)
        def _():
            db_ref[0] = bacc_ref[...]
            part_ref[k] = acc_ref[pl.ds(pl.multiple_of(c * half, half), half), :]
            stage_ref[k] = acc_ref[pl.ds(pl.multiple_of((1 - c) * half, half), half), :].astype(BF16)
            to_sibling(k).start()

        @pl.when((k == n_seg - 1) & (i == nt - 1))
        def _():
            for seg in range(n_seg):
                to_sibling(seg).wait_recv()
                total = part_ref[seg] + land_ref[seg].astype(F32)
                part_ref[seg] = total
                part16_ref[seg] = total.astype(BF16)
            for seg in range(n_seg):
                to_sibling(seg).wait_send()

    whole = pl.BlockSpec(memory_space=pltpu.VMEM)
    return pl.pallas_call(
        body, name=name,
        grid=(n_seg, nt),
        in_specs=[pl.BlockSpec((tm, D_MODEL), lambda k, i: (i, 0)),
                  pl.BlockSpec((1, tm, D_MODEL), lambda k, i: (k, i, 0))],
        out_specs=[whole, whole, pl.BlockSpec((1, 1, D_MODEL), lambda k, i: (k, 0, 0))],
        out_shape=[
            jax.ShapeDtypeStruct((n_seg, half, D_MODEL), F32),
            jax.ShapeDtypeStruct((n_seg, half, D_MODEL), BF16),
            jax.ShapeDtypeStruct((n_seg, 1, D_MODEL), F32),
        ],
        scratch_shapes=[
            pltpu.VMEM((D_MODEL, D_MODEL), F32), pltpu.VMEM((1, D_MODEL), F32),
            pltpu.VMEM((n_seg, half, D_MODEL), BF16),
            pltpu.VMEM((n_seg, half, D_MODEL), BF16),
            pltpu.SemaphoreType.DMA((n_seg,)), pltpu.SemaphoreType.DMA((n_seg,)),
        ],
        compiler_params=_params(("arbitrary", "arbitrary")),
    )(h, dp)


def _input_grad(dpa, dpb, w4, tokens, head, dz2, norm_w, dw16, rows):
    tm = _tile(rows, 1040)
    nt = rows // tm
    assert nt >= 2, rows
    exchange = _GradExchange(SEGS_REC, with_blob=False)

    def body(dpa_ref, dpb_ref, w_ref, z_ref, head_ref, dz2_ref, nw_ref, dw_ref, gx_ref, dmeta_ref, dnw_ref, rxw_ref,
             acc_ref, dz_buf, out_sem, send_sems, recv_sems):
        i, k = pl.program_id(0), pl.program_id(1)

        def first_tile_out():
            return pltpu.make_async_copy(dz_buf.at[pl.ds(FIRST_TOKEN_ROW, tm - FIRST_TOKEN_ROW), :],
                                         gx_ref.at[pl.ds(0, tm - FIRST_TOKEN_ROW), :], out_sem)

        def tile_out(tile):
            start = pl.multiple_of(tile * tm - FIRST_TOKEN_ROW, HALO)
            return pltpu.make_async_copy(dz_buf, gx_ref.at[pl.ds(start, tm), :], out_sem)

        @pl.when((i == 0) & (k == 0))
        def _():
            exchange.start(dw_ref, rxw_ref, None, None, send_sems, recv_sems)
            dnw_ref[...] = jnp.zeros_like(dnw_ref)

        @pl.when((i == nt - 1) & (k == N_SEG - 1))
        def _():
            exchange.wait(dw_ref, rxw_ref, None, None, send_sems, recv_sems)

        @pl.when(k == 0)
        def _():
            acc_ref[...] = jnp.zeros_like(acc_ref)

        @pl.when(k < 3)
        def _():
            acc_ref[...] += _dot_nt(dpa_ref[0], w_ref[0])

        @pl.when(k >= 3)
        def _():
            acc_ref[...] += _dot_nt(dpb_ref[0], w_ref[0])

        @pl.when(k == N_SEG - 1)
        def _():
            zt = _padded_tile(z_ref[...], head_ref[...], i)
            rstd = lax.rsqrt(jnp.mean(zt * zt, axis=-1, keepdims=True) + EPS)
            zh = zt * rstd
            dh = acc_ref[...]
            dnw_ref[...] += jnp.sum(dh * zh, axis=0, keepdims=True)
            uu = dh * nw_ref[...]
            dz = dz2_ref[...] + rstd * (uu - zh * jnp.mean(uu * zh, axis=-1, keepdims=True))

            @pl.when(i == 1)
            def _():
                first_tile_out().wait()

            @pl.when(i >= 2)
            def _():
                tile_out(i - 1).wait()

            dz_buf[...] = dz

            @pl.when(i == 0)
            def _():
                dmeta_ref[...] = dz[PAD_ROWS:FIRST_TOKEN_ROW]
                first_tile_out().start()

            @pl.when(i > 0)
            def _():
                tile_out(i).start()

            @pl.when(i == nt - 1)
            def _():
                tile_out(i).wait()

    spec_a, spec_b = _seg_specs(tm, lambda i, k: i, lambda i, k: k)
    last_only = pl.BlockSpec((tm, D_MODEL), lambda i, k: (jnp.where(k == N_SEG - 1, i, 0), 0))
    return pl.pallas_call(
        body, name="input_grad",
        grid=(nt, N_SEG),
        in_specs=[
            spec_a, spec_b,
            pl.BlockSpec((1, D_MODEL, D_MODEL), lambda i, k: (k // 2, 0, k % 2)),
            _token_window(tm, lambda i, k: jnp.where(k == N_SEG - 1, i, 0)),
            pl.BlockSpec((FIRST_TOKEN_ROW, D_MODEL), lambda i, k: (0, 0)),
            last_only,
            pl.BlockSpec((1, D_MODEL), lambda i, k: (0, 0)),
            ANY,
        ],
        out_specs=[
            ANY,
            pl.BlockSpec((N_META, D_MODEL), lambda i, k: (0, 0)),
            pl.BlockSpec((1, D_MODEL), lambda i, k: (0, 0)),
            ANY,
        ],
        out_shape=[
            jax.ShapeDtypeStruct((rows - FIRST_TOKEN_ROW, D_MODEL), F32),
            jax.ShapeDtypeStruct((N_META, D_MODEL), F32),
            jax.ShapeDtypeStruct((1, D_MODEL), F32),
            exchange.landing_w(),
        ],
        scratch_shapes=[pltpu.VMEM((tm, D_MODEL), F32), pltpu.VMEM((tm, D_MODEL), F32),
                        pltpu.SemaphoreType.DMA] + exchange.semaphores(),
        compiler_params=_params(("arbitrary", "arbitrary")),
    )(dpa, dpb, w4, tokens, head, dz2, norm_w, dw16)


def _local_step(tokens, head, tgt, w4, blob4, seg_order, norm_w, b_in, lb_logits, hg_w, pool_scale, final_w):
    rows = FIRST_TOKEN_ROW + tokens.shape[0]
    q = D_MODEL // N_CHIPS
    n_grp = len(POOL_WINDOWS)
    pg = POOL_GDIM // N_CHIPS

    wexp2 = jnp.asarray(np.tile(_exponent_matrix(), (1, 2)), BF16)
    wexp_t = jnp.asarray(_exponent_matrix().T, BF16)
    masks2 = jnp.asarray(_paired_masks(), F32)

    h, p3, w4 = _in_proj(tokens, head, norm_w, w4, b_in, seg_order, rows)
    o, states, e16, a2, blob4 = _hgrn_forward(p3, lb_logits, wexp2, masks2, blob4, rows)
    wdh = blob4[:, 0:q].reshape(D_MODEL, D_MODEL)
    wdp = blob4[:, q:2 * q].reshape(D_MODEL, D_MODEL)
    wout = blob4[:, 2 * q:3 * q].reshape(D_MODEL, D_MODEL)
    poolw = blob4[:, 3 * q:].reshape(N_CHIPS, n_grp, pg, POOL_GDIM).transpose(1, 0, 2, 3)
    poolw = poolw.reshape(n_grp, POOL_GDIM, POOL_GDIM)
    d_o, dz2, dpb, dwdh, dwdp, dwout, dpw, small = _mixers(
        o, p3, tokens, head, tgt, wdh, wdp, wout, poolw, hg_w, pool_scale, final_w, rows)
    dpw4 = dpw.reshape(n_grp, N_CHIPS, pg, POOL_GDIM).transpose(1, 0, 2, 3)
    dpw4 = dpw4.reshape(N_CHIPS, n_grp * pg * POOL_GDIM // D_MODEL, D_MODEL)
    dblob4 = jnp.concatenate([dwdh.reshape(N_CHIPS, q, D_MODEL), dwdp.reshape(N_CHIPS, q, D_MODEL),
                              dwout.reshape(N_CHIPS, q, D_MODEL), dpw4], axis=1)

    dw_mix, dw_mix16, db_mix = _in_proj_weight_grad(h, dpb, rows, "in_proj_weight_grad_mix")
    dpa, dlb, rxw_mix, rx_blob = _hgrn_backward(
        p3, d_o, states, e16, a2, lb_logits, wexp_t, masks2, dw_mix16, dblob4.astype(BF16), rows)
    dw_rec, dw_rec16, db_rec = _in_proj_weight_grad(h, dpa, rows, "in_proj_weight_grad_rec")
    d_tokens, d_meta, dnw, rxw_rec = _input_grad(dpa, dpb, w4, tokens, head, dz2, norm_w, dw_rec16, rows)

    small = jnp.concatenate([
        small[ROW_LOSS:ROW_LOSS + 1],
        d_meta,
        dnw,
        db_rec.reshape(len(SEGS_REC), D_MODEL), db_mix.reshape(len(SEGS_MIX), D_MODEL),
        dlb, jnp.zeros_like(dlb),
        small[ROW_HG_W:ROW_HG_W + 1], small[ROW_POOL_SCALE:ROW_POOL_SCALE + 1],
        small[ROW_FINAL_W:ROW_FINAL_W + 1],
        jnp.zeros((SMALL_ROWS - ROW_FINAL_W - 1, D_MODEL), F32),
    ], axis=0)
    return d_tokens, (dw_rec, dw_mix, rxw_rec, rxw_mix), (dblob4, rx_blob), small


ANY = pl.BlockSpec(memory_space=pl.ANY)
MESH = pl.DeviceIdType.MESH


def _place():
    x, y, c = lax.axis_index("x"), lax.axis_index("y"), lax.axis_index("c")
    chips = [(1 - x, y), (x, 1 - y), (1 - x, 1 - y)]
    return x, y, c, chips


class _ShardGather:
    def __init__(self, rows):
        self.half = rows // 2

    def semaphores(self):
        return [pltpu.SemaphoreType.DMA((6,)), pltpu.SemaphoreType.DMA((6,))]

    def _copy(self, k, slot, to, send_sems, recv_sems):
        return pltpu.make_async_remote_copy(src_ref=slot, dst_ref=slot, send_sem=send_sems.at[k],
                                            recv_sem=recv_sems.at[k], device_id=to, device_id_type=MESH)

    def _half(self, ref4, chip, which):
        return ref4.at[chip, pl.ds(which * self.half, self.half), :]

    def start(self, ref4, send_sems, recv_sems, which=(0, 1, 2)):
        x, y, c, chips = _place()
        for j in which:
            cx, cy = chips[j]
            self._copy(j, self._half(ref4, 2 * x + y, c), (cx, cy, c), send_sems, recv_sems).start()

    def start_diagonal_after_neighbours(self, ref4, send_sems, recv_sems):
        x, y, c, chips = _place()
        for j in (0, 1):
            cx, cy = chips[j]
            self._copy(j, self._half(ref4, 2 * x + y, c), (cx, cy, c), send_sems, recv_sems).wait_send()
        self.start(ref4, send_sems, recv_sems, which=(2,))

    def pass_on(self, j, ref4, send_sems, recv_sems):
        x, y, c, chips = _place()
        cx, cy = chips[j]
        landed = self._half(ref4, 2 * cx + cy, c)
        self._copy(j, landed, (cx, cy, c), send_sems, recv_sems).wait_recv()
        self._copy(3 + j, landed, (x, y, 1 - c), send_sems, recv_sems).start()

    def await_sibling(self, j, ref4, send_sems, recv_sems):
        x, y, c, chips = _place()
        cx, cy = chips[j]
        self._copy(3 + j, self._half(ref4, 2 * cx + cy, 1 - c), (x, y, 1 - c), send_sems, recv_sems).wait_recv()

    def finish(self, ref4, send_sems, recv_sems, which=(0, 1, 2)):
        x, y, c, chips = _place()
        for j, (cx, cy) in enumerate(chips):
            if j in which:
                self._copy(j, self._half(ref4, 2 * x + y, c), (cx, cy, c), send_sems, recv_sems).wait_send()
            self._copy(3 + j, self._half(ref4, 2 * cx + cy, c), (x, y, 1 - c), send_sems, recv_sems).wait_send()


def _gather_meta(m4):
    def body(m_in_ref, m4_ref, send_sems, recv_sems):
        x, y, c, chips = _place()

        def copy(j, slot, to):
            return pltpu.make_async_remote_copy(src_ref=slot, dst_ref=slot, send_sem=send_sems.at[j],
                                                recv_sem=recv_sems.at[j], device_id=to, device_id_type=MESH)

        sends = [copy(j, m4_ref.at[2 * x + y], (cx, cy, c)) for j, (cx, cy) in enumerate(chips)]
        for cp in sends:
            cp.start()
        for j, (cx, cy) in enumerate(chips):
            copy(j, m4_ref.at[2 * cx + cy], (x, y, c)).wait_recv()
        for cp in sends:
            cp.wait_send()

    return pl.pallas_call(
        body, name="gather_meta",
        in_specs=[ANY], out_specs=ANY, out_shape=jax.ShapeDtypeStruct(m4.shape, m4.dtype),
        input_output_aliases={0: 0},
        scratch_shapes=[pltpu.SemaphoreType.DMA((3,)), pltpu.SemaphoreType.DMA((3,))],
    )(m4)


class _GradExchange:
    def __init__(self, segs, with_blob):
        self.segs = tuple(segs)
        self.with_blob = with_blob

    def landing_w(self):
        return jax.ShapeDtypeStruct((N_CHIPS, 2, D_MODEL // 2, D_MODEL), BF16)

    def landing_blob(self, blob16):
        return jax.ShapeDtypeStruct((N_DEV, blob16.shape[1] // 2, D_MODEL), BF16)

    def semaphores(self):
        n_send = len(self.segs) + (2 * N_CHIPS if self.with_blob else 0)
        n_recv = 2 * N_CHIPS + (N_DEV if self.with_blob else 0)
        return [pltpu.SemaphoreType.DMA((n_send,)), pltpu.SemaphoreType.DMA((n_recv,))]

    def _copies(self, dw_ref, rxw_ref, blob_ref, rxb_ref, send_sems, recv_sems):
        x, y, c = lax.axis_index("x"), lax.axis_index("y"), lax.axis_index("c")
        chip = 2 * x + y

        def relation(kx, ky, h):
            return (x ^ kx) * 4 + (y ^ ky) * 2 + (c ^ h)

        def copy(src, dst, send_k, recv_k, to):
            return functools.partial(pltpu.make_async_remote_copy, src_ref=src, dst_ref=dst,
                                     send_sem=send_sems.at[send_k], recv_sem=recv_sems.at[recv_k],
                                     device_id=to, device_id_type=MESH)

        sends, recvs = [], []
        for i, s in enumerate(self.segs):
            kx, ky = (s // 2) >> 1, (s // 2) & 1
            r = (x ^ kx) * 2 + (y ^ ky)
            sends.append((r != 0, copy(dw_ref.at[i], rxw_ref.at[r, s % 2], i, 2 * r + s % 2, (kx, ky, c))))
        for j in range(2):
            mine = [s // 2 for s in self.segs if s % 2 == j]
            if mine:
                cond = functools.reduce(lambda a, b: a | b, [chip == k for k in mine])
                for r in range(1, N_CHIPS):
                    slot = rxw_ref.at[r, j]
                    recvs.append((cond, copy(slot, slot, 0, 2 * r + j, (x, y, c))))
        if self.with_blob:
            hb = blob_ref.shape[1] // 2
            first_send, first_recv = len(self.segs), 2 * N_CHIPS
            for k in range(N_CHIPS):
                for h in range(2):
                    r = relation(k >> 1, k & 1, h)
                    sends.append((r != 0, copy(blob_ref.at[k, pl.ds(h * hb, hb), :], rxb_ref.at[r],
                                               first_send + 2 * k + h, first_recv + r, (k >> 1, k & 1, h))))
            for r in range(1, N_DEV):
                slot = rxb_ref.at[r]
                recvs.append((None, copy(slot, slot, 0, first_recv + r, (x, y, c))))
        return sends, recvs

    def start(self, *refs):
        sends, _ = self._copies(*refs)
        for cond, make in sends:
            pl.when(cond)(lambda make=make: make().start())

    def wait(self, *refs):
        sends, recvs = self._copies(*refs)
        for cond, make in sends:
            pl.when(cond)(lambda make=make: make().wait_send())
        for cond, make in recvs:
            if cond is None:
                make().wait_recv()
            else:
                pl.when(cond)(lambda make=make: make().wait_recv())


def _sum_landed(own, rx_ref):
    total = own
    for r in range(1, rx_ref.shape[0]):
        total = total + rx_ref[r, 0].astype(F32)
    return total


def _finish_w(dw_rec, dw_mix, rx_rec, rx_mix, place_arr):
    half = D_MODEL // 2
    tm = _tile(half, 256)
    n_rec = len(SEGS_REC)

    def body(place_ref, own_rec_ref, own_mix_ref, rx_rec_ref, rx_mix_ref, out_ref):
        seg = 2 * place_ref[0] + pl.program_id(0)

        @pl.when(seg < n_rec)
        def _():
            out_ref[0] = _sum_landed(own_rec_ref[0], rx_rec_ref)

        @pl.when(seg >= n_rec)
        def _():
            out_ref[0] = _sum_landed(own_mix_ref[0], rx_mix_ref)

    def own_spec(first, count):
        def index(j, i, place_ref):
            seg = 2 * place_ref[0] + j
            return (jnp.clip(seg - first, 0, count - 1), i, 0)
        return pl.BlockSpec((1, tm, D_MODEL), index)

    rx_spec = pl.BlockSpec((N_CHIPS, 1, tm, D_MODEL), lambda j, i, place_ref: (0, j, i, 0))
    return pl.pallas_call(
        body, name="finish_w",
        grid_spec=pltpu.PrefetchScalarGridSpec(
            num_scalar_prefetch=1, grid=(2, half // tm),
            in_specs=[own_spec(0, n_rec), own_spec(n_rec, len(SEGS_MIX)), rx_spec, rx_spec],
            out_specs=pl.BlockSpec((1, tm, D_MODEL), lambda j, i, place_ref: (place_ref[1], i, j))),
        out_shape=jax.ShapeDtypeStruct((2, half, 2 * D_MODEL), F32),
        compiler_params=_params(("arbitrary", "arbitrary")),
    )(place_arr, dw_rec, dw_mix, rx_rec, rx_mix)


def _finish_blob(dblob4, rx_blob, place_arr):
    n, rows, cols = rx_blob.shape
    tm = _tile(rows, 256)

    def body(place_ref, own_ref, rx_ref, out_ref):
        out_ref[0] = _sum_landed(own_ref[0, 0], rx_ref)

    return pl.pallas_call(
        body, name="finish_blob",
        grid_spec=pltpu.PrefetchScalarGridSpec(
            num_scalar_prefetch=1, grid=(rows // tm,),
            in_specs=[pl.BlockSpec((1, 1, tm, cols), lambda i, place_ref: (place_ref[0], place_ref[1], i, 0)),
                      pl.BlockSpec((n, 1, tm, cols), lambda i, place_ref: (0, 0, i, 0))],
            out_specs=pl.BlockSpec((1, tm, cols), lambda i, place_ref: (place_ref[1], i, 0))),
        out_shape=jax.ShapeDtypeStruct((2, rows, cols), F32),
        compiler_params=_params(("arbitrary",)),
    )(place_arr, dblob4.reshape(N_CHIPS, 2, rows, cols), rx_blob.reshape(n, 1, rows, cols))


def _share_finished(fw2, fb2, small):
    def body(w_in_ref, b_in_ref, small_ref, w_ref, b_ref, s_ref, bounce, local_sem, send_sems, recv_sems):
        x, y, c, _ = _place()
        sibling = (x, y, 1 - c)

        def copy(k, src, dst, to):
            return pltpu.make_async_remote_copy(src_ref=src, dst_ref=dst, send_sem=send_sems.at[k],
                                                recv_sem=recv_sems.at[k], device_id=to, device_id_type=MESH)

        sends = [copy(0, w_ref.at[c], w_ref.at[c], sibling), copy(1, b_ref.at[c], b_ref.at[c], sibling)]
        for r in range(1, N_DEV):
            peer = (x ^ ((r >> 2) & 1), y ^ ((r >> 1) & 1), c ^ (r & 1))
            sends.append(copy(1 + r, small_ref, s_ref.at[r], peer))
        for cp in sends:
            cp.start()
        for src, dst in ((small_ref, bounce), (bounce, s_ref.at[0])):
            own = pltpu.make_async_copy(src, dst, local_sem)
            own.start()
            own.wait()
        landed = [w_ref.at[1 - c], b_ref.at[1 - c]] + [s_ref.at[r] for r in range(1, N_DEV)]
        for k, slot in enumerate(landed):
            copy(k, slot, slot, (x, y, c)).wait_recv()
        for cp in sends:
            cp.wait_send()

    same = lambda a: jax.ShapeDtypeStruct(a.shape, a.dtype)
    n_sem = 2 + N_DEV - 1
    return pl.pallas_call(
        body, name="share_finished",
        in_specs=[ANY, ANY, ANY], out_specs=[ANY, ANY, ANY],
        out_shape=[same(fw2), same(fb2), jax.ShapeDtypeStruct((N_DEV,) + small.shape, F32)],
        input_output_aliases={0: 0, 1: 1},
        scratch_shapes=[pltpu.VMEM(small.shape, F32), pltpu.SemaphoreType.DMA,
                        pltpu.SemaphoreType.DMA((n_sem,)), pltpu.SemaphoreType.DMA((n_sem,))],
    )(fw2, fb2, small)


def _sum_small(slots, lb_logits, me_arr):
    def body(me_ref, slots_ref, lbl_ref, out_ref):
        me = me_ref[0]
        total = slots_ref[me]
        for d in range(1, N_DEV):
            total = total + slots_ref[d ^ me]
        out_ref[...] = total
        out_ref[ROW_LOSS:ROW_LOSS + 1, :] = jnp.broadcast_to(
            jnp.sum(total[ROW_LOSS:ROW_LOSS + 1, :], axis=-1, keepdims=True), (1, D_MODEL))
        lb = _lower_bound(lbl_ref[...])
        g0 = total[ROW_LB:ROW_LB + 1, :] * lb * (1.0 - lb)
        out_ref[ROW_LB:ROW_LB + 1, :] = g0
        out_ref[ROW_LB + 1:ROW_LB + 2, :] = -g0

    return pl.pallas_call(
        body, name="sum_small",
        grid_spec=pltpu.PrefetchScalarGridSpec(
            num_scalar_prefetch=1, grid=(1,),
            in_specs=[pl.BlockSpec((N_DEV, SMALL_ROWS, D_MODEL), lambda i, me_ref: (0, 0, 0)),
                      pl.BlockSpec((2, D_MODEL), lambda i, me_ref: (0, 0))],
            out_specs=pl.BlockSpec((SMALL_ROWS, D_MODEL), lambda i, me_ref: (0, 0))),
        out_shape=jax.ShapeDtypeStruct((SMALL_ROWS, D_MODEL), F32),
        compiler_params=_params(("arbitrary",)),
    )(me_arr, slots, lb_logits)


def _adamw(w, g, m, v):
    rows, cols = w.shape
    tm = _tile(rows, 256, mult=8) if rows % 8 == 0 else rows
    c1 = 1.0 / (1.0 - ADAM_B1 ** ADAM_STEP)
    c2 = 1.0 / (1.0 - ADAM_B2 ** ADAM_STEP)

    def body(w_ref, g_ref, m_ref, v_ref, d_ref, nm_ref, nv_ref):
        gt = g_ref[...]
        nm = ADAM_B1 * m_ref[...] + (1.0 - ADAM_B1) * gt
        nv = ADAM_B2 * v_ref[...] + (1.0 - ADAM_B2) * (gt * gt)
        nm_ref[...] = nm
        nv_ref[...] = nv
        d_ref[...] = -ADAM_LR * ((nm * c1) / (jnp.sqrt(nv * c2) + ADAM_EPS) + ADAM_WD * w_ref[...])

    blk = pl.BlockSpec((tm, cols), lambda i: (i, 0))
    sds = jax.ShapeDtypeStruct((rows, cols), F32)
    return pl.pallas_call(
        body, name="adamw",
        grid=(rows // tm,), in_specs=[blk] * 4, out_specs=[blk] * 3, out_shape=[sds] * 3,
        compiler_params=_params(("arbitrary",)),
    )(w, g, m, v)


def kernel(x, meta_tokens, norm_w, w_in, b_in, lb_logits, hg_norm_w, pool_w, pool_scale, w_down_hg, w_down_pool, w_out, final_norm_w, loss_target, m_meta_tokens, m_norm_w, m_w_in, m_b_in, m_lb_logits, m_hg_norm_w, m_pool_w, m_pool_scale, m_w_down_hg, m_w_down_pool, m_w_out, m_final_norm_w, v_meta_tokens, v_norm_w, v_w_in, v_b_in, v_lb_logits, v_hg_norm_w, v_pool_w, v_pool_scale, v_w_down_hg, v_w_down_pool, v_w_out, v_final_norm_w):
    seq = x.shape[1]
    xi, yi, ci = lax.axis_index("x"), lax.axis_index("y"), lax.axis_index("c")
    chip = 2 * xi + yi
    place_arr = jnp.stack([chip, ci]).astype(jnp.int32)
    me_arr = jnp.reshape(4 * xi + 2 * yi + ci, (1,)).astype(jnp.int32)
    q = D_MODEL // N_CHIPS

    def blob_of(wdh, wdp, wo, pw):
        return jnp.concatenate([wdh[0], wdp[0], wo[0], pw[0].reshape(-1, D_MODEL)], axis=0)

    def in_every_slot(a):
        return jnp.broadcast_to(a[None], (N_CHIPS,) + a.shape)

    meta4 = _gather_meta(in_every_slot(meta_tokens))
    meta_full = meta4.transpose(1, 0, 2).reshape(N_META, D_MODEL)
    w4 = in_every_slot(w_in[0].astype(BF16))
    blob4 = in_every_slot(blob_of(w_down_hg, w_down_pool, w_out, pool_w).astype(BF16))
    seg_order = jnp.stack([2 * (chip ^ rel) + t for rel in (0, 2, 1, 3) for t in (0, 1)]).astype(jnp.int32)

    head = jnp.concatenate([jnp.zeros((PAD_ROWS, D_MODEL), F32), meta_full], axis=0)
    fw2 = final_norm_w.reshape(1, D_MODEL)
    d_tokens, w_parts, blob_parts, small = _local_step(
        x[0], head, loss_target[0], w4, blob4, seg_order, norm_w, b_in, lb_logits, hg_norm_w, pool_scale, fw2)
    grad_x = d_tokens[None]

    fin_w = _finish_w(*w_parts, place_arr)
    fin_b = _finish_blob(*blob_parts, place_arr)
    gw2, gb2, slots = _share_finished(fin_w, fin_b, small)
    tot = _sum_small(slots, lb_logits, me_arr)
    g_w_in = gw2.reshape(D_MODEL, 2 * D_MODEL)
    g_blob = gb2.reshape(-1, D_MODEL)

    d_win, nm_win, nv_win = _adamw(w_in[0], g_w_in, m_w_in[0], v_w_in[0])
    d_blob, nm_blob, nv_blob = _adamw(
        blob_of(w_down_hg, w_down_pool, w_out, pool_w), g_blob,
        blob_of(m_w_down_hg, m_w_down_pool, m_w_out, m_pool_w),
        blob_of(v_w_down_hg, v_w_down_pool, v_w_out, v_pool_w))
    g_meta = lax.dynamic_slice_in_dim(tot[ROW_META:ROW_META + N_META], chip * q, q, axis=1)
    d_meta, nm_meta, nv_meta = _adamw(meta_tokens, g_meta, m_meta_tokens, v_meta_tokens)

    def rows_of(nw, bi, lbl, hg, ps, fw):
        return jnp.concatenate([nw, bi.reshape(N_SEG, D_MODEL), lbl, hg, ps, fw.reshape(1, D_MODEL),
                                jnp.zeros((2, D_MODEL), F32)], axis=0)

    g_rows = jnp.concatenate([tot[ROW_NORM_W:ROW_FINAL_W + 1], jnp.zeros((2, D_MODEL), F32)], axis=0)
    d_rows, nm_rows, nv_rows = _adamw(
        rows_of(norm_w, b_in, lb_logits, hg_norm_w, pool_scale, final_norm_w), g_rows,
        rows_of(m_norm_w, m_b_in, m_lb_logits, m_hg_norm_w, m_pool_scale, m_final_norm_w),
        rows_of(v_norm_w, v_b_in, v_lb_logits, v_hg_norm_w, v_pool_scale, v_final_norm_w))

    def unblob(b):
        return (b[0:q][None], b[q:2 * q][None], b[2 * q:3 * q][None], b[3 * q:].reshape(pool_w.shape))

    def unrows(r):
        o = ROW_NORM_W
        return dict(norm_w=r[ROW_NORM_W - o:ROW_B_IN - o], b_in=r[ROW_B_IN - o:ROW_LB - o].reshape(1, -1),
                    lb_logits=r[ROW_LB - o:ROW_HG_W - o], hg_norm_w=r[ROW_HG_W - o:ROW_POOL_SCALE - o],
                    pool_scale=r[ROW_POOL_SCALE - o:ROW_FINAL_W - o], final_norm_w=r[ROW_FINAL_W - o])

    def leaves(meta_part, rows_part, win_part, blob_part):
        r = unrows(rows_part)
        wdh, wdp, wo, pw = unblob(blob_part)
        return [meta_part, r["norm_w"], win_part[None], r["b_in"], r["lb_logits"], r["hg_norm_w"], pw,
                r["pool_scale"], wdh, wdp, wo, r["final_norm_w"]]

    loss = tot[ROW_LOSS, 0]
    return (loss, grad_x,
            *leaves(g_meta, g_rows, g_w_in, g_blob),
            *leaves(d_meta, d_rows, d_win, d_blob),
            *leaves(nm_meta, nm_rows, nm_win, nm_blob),
            *leaves(nv_meta, nv_rows, nv_win, nv_blob))
```

```python
import functools

import numpy as np
import jax
import jax.numpy as jnp
from jax import lax
from jax.experimental import pallas as pl
from jax.experimental.pallas import tpu as pltpu

F32 = jnp.float32
BF16 = jnp.bfloat16

D_MODEL = 1024
N_SEG = 8
N_HEADS = 8
HEAD_DIM = 128
CHUNK = 64
N_META = 16
PAD_ROWS = CHUNK - N_META
FIRST_TOKEN_ROW = CHUNK
LEVELS = (32, 16, 8, 4, 2, 1)
N_EXP = 2 + len(LEVELS)
POOL_WINDOWS = (2, 4, 8, 16)
POOL_GDIM = D_MODEL // len(POOL_WINDOWS)
HALO = 16
LOCAL_UNROLL = 13
BACKWARD_UNROLL = 13
EPS = 1e-6
N_CHIPS = 4
N_DEV = 8
SEGS_REC = (0, 1, 2)
SEGS_MIX = (3, 4, 5, 6, 7)

ADAM_LR = 0.001
ADAM_B1 = 0.9
ADAM_B2 = 0.999
ADAM_EPS = 1e-08
ADAM_WD = 0.01
ADAM_STEP = 10

VMEM_LIMIT_BYTES = 56 * 1024 * 1024

ROW_LOSS = 0
ROW_META = 1
ROW_NORM_W = ROW_META + N_META
ROW_B_IN = ROW_NORM_W + 1
ROW_LB = ROW_B_IN + N_SEG
ROW_HG_W = ROW_LB + 2
ROW_POOL_SCALE = ROW_HG_W + 1
ROW_FINAL_W = ROW_POOL_SCALE + 1
SMALL_ROWS = 32


def _tile(total, cap, mult=16):
    best = None
    for t in range(mult, min(total, cap) + 1, mult):
        if total % t == 0:
            best = t
    assert best is not None, (total, cap, mult)
    return best


def _token_window(tm, tile_of):
    def index(*grid):
        return (pl.multiple_of(jnp.maximum(tile_of(*grid) * tm - FIRST_TOKEN_ROW, 0), HALO), 0)
    return pl.BlockSpec((pl.Element(tm), pl.Element(D_MODEL)), index)


def _padded_tile(window, head, tile):
    first = jnp.concatenate([head, pltpu.roll(window, FIRST_TOKEN_ROW, 0)[FIRST_TOKEN_ROW:]], axis=0)
    return jnp.where(tile == 0, first, window)


def _params(sem=None):
    return pltpu.CompilerParams(dimension_semantics=sem, vmem_limit_bytes=VMEM_LIMIT_BYTES)


def _dot(a, b):
    return jnp.dot(a, b, preferred_element_type=F32)


def _dot_nt(a, b):
    return lax.dot_general(a, b, (((1,), (1,)), ((), ())), preferred_element_type=F32)


def _dot_tn(a, b):
    return lax.dot_general(a, b, (((0,), (0,)), ((), ())), preferred_element_type=F32)


def _sigmoid_pair(x):
    t = jnp.exp(-jnp.abs(x))
    r = 1.0 / (1.0 + t)
    pos = x >= 0
    return jnp.where(pos, r, t * r), jnp.where(pos, t * r, r)


def _exponent_matrix():
    t = np.arange(CHUNK)[:, None]
    j = np.arange(CHUNK)[None, :]
    blocks = [j <= t, j > t]
    for m in LEVELS:
        rho = (t // (2 * m)) * (2 * m) + m
        upper = (t >= rho) & (j > rho) & (j <= t)
        lower = (t < rho) & (j > t) & (j <= rho)
        blocks.append(upper | lower)
    return np.concatenate(blocks, axis=0).astype(np.float32)


def _pair_masks():
    t = np.arange(CHUNK)[:, None]
    s = np.arange(CHUNK)[None, :]
    masks = [t == s]
    for m in LEVELS:
        same = (t // (2 * m)) == (s // (2 * m))
        masks.append(same & ((t % (2 * m)) >= m) & ((s % (2 * m)) < m))
    return np.stack(masks).astype(np.float32)


LEVEL_PAIRS = ((0, 1), (2, 3), (4, 5), (6, None))


def _paired_masks():
    m = _pair_masks()
    zero = np.zeros_like(m[0])
    return np.stack([np.concatenate([m[a], zero if b is None else m[b]], axis=1) for a, b in LEVEL_PAIRS])


def _lower_bound(lbl):
    return 1.0 / (1.0 + jnp.exp(lbl[1:2, :] - lbl[0:1, :]))


def _in_proj(tokens, head, norm_w, w4, b_in, seg_order, rows):
    tm = _tile(rows, 1040)
    nt = rows // tm
    gather = _ShardGather(w4.shape[1])

    def body(order_ref, z_ref, head_ref, nw_ref, b_ref, w_in_ref, h_ref, p_ref, w4_ref,
             h_all, w_buf, w_sem, send_sems, recv_sems):
        kk, i = pl.program_id(0), pl.program_id(1)

        @pl.when((kk == 0) & (i == 0))
        def _():
            gather.start(w4_ref, send_sems, recv_sems, which=(0, 1))

        @pl.when((kk == 2) & (i == 0))
        def _():
            gather.start_diagonal_after_neighbours(w4_ref, send_sems, recv_sems)

        @pl.when(kk == 0)
        def _():
            zt = _padded_tile(z_ref[...], head_ref[...], i)
            rstd = lax.rsqrt(jnp.mean(zt * zt, axis=-1, keepdims=True) + EPS)
            h = (zt * rstd * nw_ref[...]).astype(BF16)
            h_all[pl.ds(pl.multiple_of(i * tm, 16), tm), :] = h
            h_ref[...] = h

        @pl.when((kk == 2) & (i == 0))
        def _():
            gather.pass_on(0, w4_ref, send_sems, recv_sems)
            gather.pass_on(1, w4_ref, send_sems, recv_sems)
            gather.await_sibling(0, w4_ref, send_sems, recv_sems)

        @pl.when((kk == 4) & (i == 0))
        def _():
            gather.await_sibling(1, w4_ref, send_sems, recv_sems)

        @pl.when((kk == 5) & (i == 0))
        def _():
            gather.pass_on(2, w4_ref, send_sems, recv_sems)

        @pl.when((kk == 6) & (i == 0))
        def _():
            gather.await_sibling(2, w4_ref, send_sems, recv_sems)

        def weights(which):
            seg = order_ref[2 * (kk // 2) + which]
            return pltpu.make_async_copy(
                w4_ref.at[seg // 2, :, pl.ds(pl.multiple_of((seg % 2) * D_MODEL, D_MODEL), D_MODEL)],
                w_buf.at[which], w_sem.at[which])

        @pl.when((i == 0) & (kk % 2 == 0))
        def _():
            weights(0).start()
            weights(1).start()
            weights(0).wait()

        @pl.when((i == 0) & (kk % 2 == 1))
        def _():
            weights(1).wait()

        p_ref[0] = _dot(h_all[pl.ds(pl.multiple_of(i * tm, 16), tm), :], w_buf[kk % 2]) + b_ref[...]

        @pl.when((kk == N_SEG - 1) & (i == nt - 1))
        def _():
            gather.finish(w4_ref, send_sems, recv_sems, which=(2,))

    first_pass = lambda kk, i, order_ref: (jnp.where(kk == 0, i, nt - 1), 0)
    return pl.pallas_call(
        body, name="in_proj",
        grid_spec=pltpu.PrefetchScalarGridSpec(
            num_scalar_prefetch=1, grid=(N_SEG, nt),
            in_specs=[
                _token_window(tm, lambda kk, i, order_ref: jnp.where(kk == 0, i, nt - 1)),
                pl.BlockSpec((FIRST_TOKEN_ROW, D_MODEL), lambda kk, i, order_ref: (0, 0)),
                pl.BlockSpec((1, D_MODEL), lambda kk, i, order_ref: (0, 0)),
                pl.BlockSpec((1, D_MODEL), lambda kk, i, order_ref: (0, order_ref[kk])),
                ANY,
            ],
            out_specs=[
                pl.BlockSpec((tm, D_MODEL), first_pass),
                pl.BlockSpec((1, tm, D_MODEL), lambda kk, i, order_ref: (order_ref[kk], i, 0)),
                ANY,
            ],
            scratch_shapes=[
                pltpu.VMEM((rows, D_MODEL), BF16),
                pltpu.VMEM((2, D_MODEL, D_MODEL), BF16),
                pltpu.SemaphoreType.DMA((2,)),
            ] + gather.semaphores()),
        out_shape=[
            jax.ShapeDtypeStruct((rows, D_MODEL), BF16),
            jax.ShapeDtypeStruct((N_SEG, rows, D_MODEL), F32),
            jax.ShapeDtypeStruct(w4.shape, w4.dtype),
        ],
        input_output_aliases={5: 2},
        compiler_params=_params(("arbitrary", "arbitrary")),
    )(seg_order, tokens, head, norm_w, b_in, w4)


def _hgrn_forward(p3, lb_logits, wexp2, masks2, blob4, rows):
    n_chunks = rows // CHUNK
    cpb = _tile(n_chunks, 13, mult=1)
    rb_rows = cpb * CHUNK
    n_rb = n_chunks // cpb
    lanes = cpb * HEAD_DIM
    gather = _ShardGather(blob4.shape[1])

    def body(q_ref, fz_ref, v_ref, lbl_ref, wexp_ref, mask_ref, b_in_ref, o_ref, s_ref, e16_ref, a2_ref, b4_ref,
             st_ref, e_ref, u_ref, q_s, kk_s, v_s, send_sems, recv_sems):
        rb = pl.program_id(1)

        @pl.when((pl.program_id(0) == 0) & (rb == 0))
        def _():
            gather.start(b4_ref, send_sems, recv_sems)

        @pl.when(rb == 0)
        def _():
            st_ref[...] = jnp.zeros_like(st_ref)

        lb = _lower_bound(lbl_ref[...])
        row = rb * rb_rows + lax.broadcasted_iota(jnp.int32, (rb_rows, 1), 0)
        valid = row >= PAD_ROWS
        sg, sn = _sigmoid_pair(fz_ref[0])
        g = jnp.where(valid, jnp.log(lb + (1.0 - lb) * sg), 0.0)
        kk_s[...] = jnp.where(valid, (1.0 - lb) * sn, 0.0)
        q_s[...] = jnp.where(valid, q_ref[0], 0.0)
        v_s[...] = jnp.where(valid, v_ref[0], 0.0).astype(BF16)
        hi = g.astype(BF16)
        mid = (g - hi.astype(F32)).astype(BF16)
        g2 = jnp.concatenate(
            [jnp.concatenate([hi[b * CHUNK:(b + 1) * CHUNK], mid[b * CHUNK:(b + 1) * CHUNK]], axis=0)
             for b in range(cpb)], axis=1)
        e_ref[...] = jnp.exp(_dot(wexp_ref[...], g2))
        e16_ref[0, 0] = e_ref[...].astype(BF16)

        def contribution(b, carry):
            r0 = pl.multiple_of(b * CHUNK, CHUNK)
            l0 = pl.multiple_of(b * HEAD_DIM, HEAD_DIM)
            kc16 = (kk_s[pl.ds(r0, CHUNK), :] * e_ref[CHUNK:2 * CHUNK, pl.ds(l0, HEAD_DIM)]).astype(BF16)
            u_ref[b] = _dot_tn(v_s[pl.ds(r0, CHUNK), :], kc16)
            return carry

        lax.fori_loop(0, cpb, contribution, 0, unroll=LOCAL_UNROLL)

        def recur(b, st):
            l0 = pl.multiple_of(b * HEAD_DIM, HEAD_DIM)
            s_ref[0, b] = st
            return st * e_ref[CHUNK - 1:CHUNK, pl.ds(l0, HEAD_DIM)] + u_ref[b]

        st_ref[...] = lax.fori_loop(0, cpb, recur, st_ref[...])

        zeros16 = jnp.zeros((CHUNK, HEAD_DIM), BF16)

        def local(b, carry):
            r0 = pl.multiple_of(b * CHUNK, CHUNK)
            l0 = pl.multiple_of(b * HEAD_DIM, HEAD_DIM)
            q = q_s[pl.ds(r0, CHUNK), :]
            kk = kk_s[pl.ds(r0, CHUNK), :]
            v16 = v_s[pl.ds(r0, CHUNK), :]

            def scaled(entry):
                if entry == 0:
                    return q.astype(BF16), kk.astype(BF16)
                e_m = e_ref[(1 + entry) * CHUNK:(2 + entry) * CHUNK, pl.ds(l0, HEAD_DIM)]
                return (q * e_m).astype(BF16), (kk * e_m).astype(BF16)

            a2 = jnp.zeros((CHUNK, 2 * CHUNK), F32)
            for p, (ea, eb) in enumerate(LEVEL_PAIRS):
                qa, ka = scaled(ea)
                if eb is None:
                    prod = _dot_nt(qa, jnp.concatenate([ka, zeros16], axis=0))
                else:
                    qb_, kb_ = scaled(eb)
                    rhs = jnp.concatenate([jnp.concatenate([ka, zeros16], axis=1),
                                           jnp.concatenate([zeros16, kb_], axis=1)], axis=0)
                    prod = _dot_nt(jnp.concatenate([qa, qb_], axis=1), rhs)
                a2 = a2 + mask_ref[p] * prod
            a2_16 = a2.astype(BF16)
            a2_ref[pl.ds(r0, CHUNK), :] = a2_16
            qb16 = (q * e_ref[0:CHUNK, pl.ds(l0, HEAD_DIM)]).astype(BF16)
            o_ref[pl.ds(r0, CHUNK), :] = (_dot(a2_16, jnp.concatenate([v16, v16], axis=0))
                                          + _dot_nt(qb16, s_ref[0, b].astype(BF16)))
            return carry

        lax.fori_loop(0, cpb, local, 0, unroll=LOCAL_UNROLL)

        @pl.when((pl.program_id(0) == N_HEADS // 2) & (rb == 0))
        def _():
            for j in range(N_CHIPS - 1):
                gather.pass_on(j, b4_ref, send_sems, recv_sems)

        @pl.when((pl.program_id(0) == N_HEADS - 1) & (rb == n_rb - 1))
        def _():
            for j in range(N_CHIPS - 1):
                gather.await_sibling(j, b4_ref, send_sems, recv_sems)
            gather.finish(b4_ref, send_sems, recv_sems)

    head_block = lambda seg: pl.BlockSpec((1, rb_rows, HEAD_DIM), lambda h, r: (seg, r, h))
    return pl.pallas_call(
        body, name="hgrn_forward",
        grid=(N_HEADS, n_rb),
        in_specs=[
            head_block(0), head_block(1), head_block(2),
            pl.BlockSpec((2, HEAD_DIM), lambda h, r: (0, h)),
            pl.BlockSpec((N_EXP * CHUNK, 2 * CHUNK), lambda h, r: (0, 0)),
            pl.BlockSpec((len(LEVEL_PAIRS), CHUNK, 2 * CHUNK), lambda h, r: (0, 0, 0)),
            ANY,
        ],
        out_specs=[
            pl.BlockSpec((rb_rows, HEAD_DIM), lambda h, r: (r, h)),
            pl.BlockSpec((1, cpb, HEAD_DIM, HEAD_DIM), lambda h, r: (h, r, 0, 0)),
            pl.BlockSpec((1, 1, N_EXP * CHUNK, lanes), lambda h, r: (h, r, 0, 0)),
            pl.BlockSpec((rb_rows, HEAD_DIM), lambda h, r: (r, h)),
            ANY,
        ],
        out_shape=[
            jax.ShapeDtypeStruct((rows, D_MODEL), F32),
            jax.ShapeDtypeStruct((N_HEADS, n_chunks, HEAD_DIM, HEAD_DIM), F32),
            jax.ShapeDtypeStruct((N_HEADS, n_rb, N_EXP * CHUNK, lanes), BF16),
            jax.ShapeDtypeStruct((rows, D_MODEL), BF16),
            jax.ShapeDtypeStruct(blob4.shape, blob4.dtype),
        ],
        input_output_aliases={6: 4},
        scratch_shapes=[
            pltpu.VMEM((HEAD_DIM, HEAD_DIM), F32),
            pltpu.VMEM((N_EXP * CHUNK, lanes), F32),
            pltpu.VMEM((cpb, HEAD_DIM, HEAD_DIM), F32),
            pltpu.VMEM((rb_rows, HEAD_DIM), F32),
            pltpu.VMEM((rb_rows, HEAD_DIM), F32),
            pltpu.VMEM((rb_rows, HEAD_DIM), BF16),
        ] + gather.semaphores(),
        compiler_params=_params(("arbitrary", "arbitrary")),
    )(p3, p3, p3, lb_logits, wexp2, masks2, blob4)


def _hgrn_backward(p3, d_o, states, e16, a2, lb_logits, wexp_t, masks2, dw16, blob16, rows):
    n_chunks = rows // CHUNK
    cpb = _tile(n_chunks, 13, mult=1)
    rb_rows = cpb * CHUNK
    n_rb = n_chunks // cpb
    lanes = cpb * HEAD_DIM
    exchange = _GradExchange(SEGS_MIX, with_blob=True)

    def body(q_ref, fz_ref, v_ref, do_ref, s_ref, e_ref, a2_ref, lbl_ref, wexpt_ref, mask_ref, dw_ref, blob_ref,
             dp_ref, dlb_ref, rxw_ref, rxb_ref,
             dst_ref, g_ref, dsn_ref, q_s, kk_s, v_s, do_s, dq_s, dkk_s, dg_s, dx_s, da2_s, send_sems, recv_sems):
        step = pl.program_id(1)
        rb = n_rb - 1 - step

        @pl.when((pl.program_id(0) == 0) & (step == 0))
        def _():
            exchange.start(dw_ref, rxw_ref, blob_ref, rxb_ref, send_sems, recv_sems)

        @pl.when(step == 0)
        def _():
            dst_ref[...] = jnp.zeros_like(dst_ref)
            dlb_ref[...] = jnp.zeros_like(dlb_ref)

        lb = _lower_bound(lbl_ref[...])
        row = rb * rb_rows + lax.broadcasted_iota(jnp.int32, (rb_rows, 1), 0)
        valid = row >= PAD_ROWS
        sg, sn = _sigmoid_pair(fz_ref[0])
        f = lb + (1.0 - lb) * sg
        g = jnp.where(valid, jnp.log(f), 0.0)
        kk_s[...] = jnp.where(valid, (1.0 - lb) * sn, 0.0)
        q_s[...] = jnp.where(valid, q_ref[0], 0.0)
        v_s[...] = jnp.where(valid, v_ref[0], 0.0).astype(BF16)
        do_s[...] = do_ref[...].astype(BF16)
        e_last_all = jnp.exp(jnp.concatenate(
            [jnp.sum(g[b * CHUNK:(b + 1) * CHUNK], axis=0, keepdims=True) for b in range(cpb)], axis=0))
        last_row = lax.broadcasted_iota(jnp.int32, (CHUNK, 1), 0) == CHUNK - 1
        zeros16 = jnp.zeros((CHUNK, HEAD_DIM), BF16)

        def factor(block, l0):
            return e_ref[0, 0, block * CHUNK:(block + 1) * CHUNK, pl.ds(l0, HEAD_DIM)].astype(F32)

        def contribution(b, carry):
            r0 = pl.multiple_of(b * CHUNK, CHUNK)
            l0 = pl.multiple_of(b * HEAD_DIM, HEAD_DIM)
            qb16 = (q_s[pl.ds(r0, CHUNK), :] * factor(0, l0)).astype(BF16)
            g_ref[b] = _dot_tn(do_s[pl.ds(r0, CHUNK), :], qb16)
            return carry

        lax.fori_loop(0, cpb, contribution, 0, unroll=LOCAL_UNROLL)

        cur = dst_ref[...]
        for b in reversed(range(cpb)):
            dsn_ref[b] = cur
            cur = cur * e_last_all[b:b + 1, :] + g_ref[b]
        dst_ref[...] = cur

        def through_state(b, carry):
            r0 = pl.multiple_of(b * CHUNK, CHUNK)
            l0 = pl.multiple_of(b * HEAD_DIM, HEAD_DIM)
            v16 = v_s[pl.ds(r0, CHUNK), :]
            do16 = do_s[pl.ds(r0, CHUNK), :]
            st = s_ref[0, b]
            dsn = dsn_ref[b]
            dsn16 = dsn.astype(BF16)
            e_b, e_c = factor(0, l0), factor(1, l0)
            qb, kc = q_s[pl.ds(r0, CHUNK), :] * e_b, kk_s[pl.ds(r0, CHUNK), :] * e_c

            t = _dot_tn(a2_ref[pl.ds(r0, CHUNK), :], do16)
            dv = t[0:CHUNK] + t[CHUNK:2 * CHUNK] + _dot_nt(kc.astype(BF16), dsn16)
            dp_ref[2, pl.ds(r0, CHUNK), :] = dv.astype(BF16)
            da2_s[pl.ds(r0, CHUNK), :] = _dot_nt(do16, jnp.concatenate([v16, v16], axis=0))
            dqb = _dot(do16, st.astype(BF16))
            dkc = _dot(v16, dsn16)
            de = jnp.sum(dsn * st, axis=0, keepdims=True) * e_b[CHUNK - 1:CHUNK, :]
            dq_s[pl.ds(r0, CHUNK), :] = e_b * dqb
            dkk_s[pl.ds(r0, CHUNK), :] = e_c * dkc
            dx_s[0:CHUNK, pl.ds(l0, HEAD_DIM)] = (qb * dqb + jnp.where(last_row, de, 0.0)).astype(BF16)
            dx_s[CHUNK:2 * CHUNK, pl.ds(l0, HEAD_DIM)] = (kc * dkc).astype(BF16)
            return carry

        lax.fori_loop(0, cpb, through_state, 0, unroll=BACKWARD_UNROLL)

        def local(b, carry):
            r0 = pl.multiple_of(b * CHUNK, CHUNK)
            l0 = pl.multiple_of(b * HEAD_DIM, HEAD_DIM)
            q = q_s[pl.ds(r0, CHUNK), :]
            kk = kk_s[pl.ds(r0, CHUNK), :]
            da2 = da2_s[pl.ds(r0, CHUNK), :]
            dq = dq_s[pl.ds(r0, CHUNK), :]
            dkk = dkk_s[pl.ds(r0, CHUNK), :]

            def scaled(entry):
                if entry == 0:
                    return q, kk, None
                e_m = factor(1 + entry, l0)
                return q * e_m, kk * e_m, e_m

            for p, (ea, eb) in enumerate(LEVEL_PAIRS):
                dm = mask_ref[p] * da2
                dm_t = dm.T.astype(BF16)
                qa, ka, e_a = scaled(ea)
                if eb is None:
                    rhs_k = jnp.concatenate([jnp.concatenate([ka.astype(BF16), zeros16], axis=1),
                                             jnp.concatenate([zeros16, zeros16], axis=1)], axis=0)
                else:
                    qb_, kb_, e_bb = scaled(eb)
                    rhs_k = jnp.concatenate([jnp.concatenate([ka.astype(BF16), zeros16], axis=1),
                                             jnp.concatenate([zeros16, kb_.astype(BF16)], axis=1)], axis=0)
                dq2 = _dot(dm.astype(BF16), rhs_k)
                parts = [(ea, qa, ka, e_a, dq2[:, :HEAD_DIM], _dot(dm_t[0:CHUNK], qa.astype(BF16)))]
                if eb is not None:
                    parts.append((eb, qb_, kb_, e_bb, dq2[:, HEAD_DIM:],
                                  _dot(dm_t[CHUNK:2 * CHUNK], qb_.astype(BF16))))
                for entry, q_m, k_m, e_m, dq_m, dk_m in parts:
                    if entry == 0:
                        dq = dq + dq_m
                        dkk = dkk + dk_m
                    else:
                        dq = dq + e_m * dq_m
                        dkk = dkk + e_m * dk_m
                        dx_s[(1 + entry) * CHUNK:(2 + entry) * CHUNK, pl.ds(l0, HEAD_DIM)] = (
                            q_m * dq_m + k_m * dk_m).astype(BF16)
            dq_s[pl.ds(r0, CHUNK), :] = dq
            dkk_s[pl.ds(r0, CHUNK), :] = dkk
            return carry

        lax.fori_loop(0, cpb, local, 0, unroll=BACKWARD_UNROLL)

        dg_all = _dot(wexpt_ref[...], dx_s[...])
        for b in range(cpb):
            dg_s[b * CHUNK:(b + 1) * CHUNK, :] = dg_all[:, b * HEAD_DIM:(b + 1) * HEAD_DIM]
        t = jnp.where(valid, dg_s[...] / f - dkk_s[...], 0.0)
        dlb_ref[...] += jnp.sum(sn * t, axis=0, keepdims=True)
        dp_ref[0] = jnp.where(valid, dq_s[...], 0.0).astype(BF16)
        dp_ref[1] = ((1.0 - lb) * sg * sn * t).astype(BF16)

        @pl.when((pl.program_id(0) == N_HEADS - 1) & (step == n_rb - 1))
        def _():
            exchange.wait(dw_ref, rxw_ref, blob_ref, rxb_ref, send_sems, recv_sems)

    head_block = lambda seg: pl.BlockSpec((1, rb_rows, HEAD_DIM), lambda h, s: (seg, n_rb - 1 - s, h))
    row_block = pl.BlockSpec((rb_rows, HEAD_DIM), lambda h, s: (n_rb - 1 - s, h))
    return pl.pallas_call(
        body, name="hgrn_backward",
        grid=(N_HEADS, n_rb),
        in_specs=[
            head_block(0), head_block(1), head_block(2),
            row_block,
            pl.BlockSpec((1, cpb, HEAD_DIM, HEAD_DIM), lambda h, s: (h, n_rb - 1 - s, 0, 0)),
            pl.BlockSpec((1, 1, N_EXP * CHUNK, lanes), lambda h, s: (h, n_rb - 1 - s, 0, 0)),
            row_block,
            pl.BlockSpec((2, HEAD_DIM), lambda h, s: (0, h)),
            pl.BlockSpec((CHUNK, N_EXP * CHUNK), lambda h, s: (0, 0)),
            pl.BlockSpec((len(LEVEL_PAIRS), CHUNK, 2 * CHUNK), lambda h, s: (0, 0, 0)),
            ANY, ANY,
        ],
        out_specs=[
            pl.BlockSpec((3, rb_rows, HEAD_DIM), lambda h, s: (0, n_rb - 1 - s, h)),
            pl.BlockSpec((1, HEAD_DIM), lambda h, s: (0, h)),
            ANY, ANY,
        ],
        out_shape=[
            jax.ShapeDtypeStruct((3, rows, D_MODEL), BF16),
            jax.ShapeDtypeStruct((1, D_MODEL), F32),
            exchange.landing_w(), exchange.landing_blob(blob16),
        ],
        scratch_shapes=[
            pltpu.VMEM((HEAD_DIM, HEAD_DIM), F32),
            pltpu.VMEM((cpb, HEAD_DIM, HEAD_DIM), F32),
            pltpu.VMEM((cpb, HEAD_DIM, HEAD_DIM), F32),
            pltpu.VMEM((rb_rows, HEAD_DIM), F32),
            pltpu.VMEM((rb_rows, HEAD_DIM), F32),
            pltpu.VMEM((rb_rows, HEAD_DIM), BF16),
            pltpu.VMEM((rb_rows, HEAD_DIM), BF16),
            pltpu.VMEM((rb_rows, HEAD_DIM), F32),
            pltpu.VMEM((rb_rows, HEAD_DIM), F32),
            pltpu.VMEM((rb_rows, HEAD_DIM), F32),
            pltpu.VMEM((N_EXP * CHUNK, lanes), BF16),
            pltpu.VMEM((rb_rows, 2 * CHUNK), F32),
        ] + exchange.semaphores(),
        compiler_params=_params(("arbitrary", "arbitrary")),
    )(p3, p3, p3, d_o, states, e16, a2, lb_logits, wexp_t, masks2, dw16, blob16)


def _sigmoid(x):
    return 1.0 / (1.0 + jnp.exp(-x))


def _silu_and_grad(x):
    s = _sigmoid(x)
    return x * s, s * (1.0 + x * (1.0 - s))


def _window_sum(ext, width, forward_looking):
    n = ext.shape[0]
    s = ext
    step = 1
    while step < width:
        s = s + pltpu.roll(s, (n - step) if forward_looking else step, 0)
        step *= 2
    return s


def _mixers(o, p3, tokens, head, tgt, wdh, wdp, wout, poolw, hg_w, pool_scale, final_w, rows):
    tm = _tile(rows, 208)
    nt = rows // tm
    halo_blocks = tm // HALO
    n_grp = len(POOL_WINDOWS)

    def body(o_ref, ghg_ref, u_ref, gpl_ref, mhg_ref, mpl_ref, uh_ref, z_ref, t_ref,
             wdh_ref, wdp_ref, wout_ref, pw_ref, hgw_ref, ps_ref, fw_ref, head_ref,
             do_ref, dz2_ref, dp_ref, dwdh_ref, dwdp_ref, dwout_ref, dpw_ref, small_ref, carry_ref):
        step = pl.program_id(0)
        tile = nt - 1 - step

        @pl.when(step == 0)
        def _():
            dwdh_ref[...] = jnp.zeros_like(dwdh_ref)
            dwdp_ref[...] = jnp.zeros_like(dwdp_ref)
            dwout_ref[...] = jnp.zeros_like(dwout_ref)
            dpw_ref[...] = jnp.zeros_like(dpw_ref)
            small_ref[...] = jnp.zeros_like(small_ref)
            carry_ref[...] = jnp.zeros_like(carry_ref)

        row = tile * tm + lax.broadcasted_iota(jnp.int32, (tm, 1), 0)
        real = row >= PAD_ROWS
        pos1 = jnp.maximum(row - PAD_ROWS + 1, 1).astype(F32)

        u = jnp.where(real, u_ref[0], 0.0)
        halo_row = tile * tm - HALO + lax.broadcasted_iota(jnp.int32, (HALO, 1), 0)
        uh = jnp.where(halo_row >= PAD_ROWS, uh_ref[0], 0.0)
        ext = jnp.concatenate([uh, u], axis=0)
        pooled, inv_cnt, mixed = [], [], []
        for g, w in enumerate(POOL_WINDOWS):
            cols = slice(g * POOL_GDIM, (g + 1) * POOL_GDIM)
            inv = 1.0 / jnp.minimum(pos1, float(w))
            ws = _window_sum(ext[:, cols], w, False)[HALO:]
            pg = (ws * inv - u[:, cols]).astype(BF16)
            pooled.append(pg)
            inv_cnt.append(inv)
            mixed.append(_dot(pg, pw_ref[g]))
        mixed = jnp.concatenate(mixed, axis=1)
        gpl = gpl_ref[0]
        sp, dsp = _silu_and_grad(gpl)
        ps = ps_ref[...]
        a_pool = (mixed * ps * sp).astype(BF16)
        y_pool = _dot(a_pool, wdp_ref[...])

        o = o_ref[...]
        o_hat, rstd_h = [], []
        for h in range(N_HEADS):
            oh = o[:, h * HEAD_DIM:(h + 1) * HEAD_DIM]
            r = lax.rsqrt(jnp.mean(oh * oh, axis=-1, keepdims=True) + EPS)
            rstd_h.append(r)
            o_hat.append(oh * r)
        o_hat = jnp.concatenate(o_hat, axis=1)
        hgw = hgw_ref[...]
        o_n = o_hat * hgw
        ghg = ghg_ref[0]
        sh, dsh = _silu_and_grad(ghg)
        a_hg = (o_n * sh).astype(BF16)
        y_hg = _dot(a_hg, wdh_ref[...])

        s_mh = _sigmoid(mhg_ref[0])
        s_mp = _sigmoid(mpl_ref[0])
        merged = (s_mh * y_hg + s_mp * y_pool).astype(BF16)
        z2 = _padded_tile(z_ref[...], head_ref[...], tile) + _dot(merged, wout_ref[...])
        rstd2 = lax.rsqrt(jnp.mean(z2 * z2, axis=-1, keepdims=True) + EPS)
        zh = z2 * rstd2
        fw = fw_ref[...]
        target = _padded_tile(t_ref[...], jnp.zeros((FIRST_TOKEN_ROW, D_MODEL), F32), tile)
        err = jnp.where(row >= FIRST_TOKEN_ROW, zh * fw - target, 0.0)
        small_ref[ROW_LOSS:ROW_LOSS + 1, :] += jnp.sum(err * err, axis=0, keepdims=True) * (0.5 / D_MODEL)
        dy = err * (1.0 / D_MODEL)

        small_ref[ROW_FINAL_W:ROW_FINAL_W + 1, :] += jnp.sum(dy * zh, axis=0, keepdims=True)
        uu = dy * fw
        dz2 = rstd2 * (uu - zh * jnp.mean(uu * zh, axis=-1, keepdims=True))
        dz2_ref[...] = dz2
        dz2_16 = dz2.astype(BF16)
        dmerged = _dot_nt(dz2_16, wout_ref[...])
        dwout_ref[...] += _dot_tn(merged, dz2_16)
        dy_hg = (s_mh * dmerged).astype(BF16)
        dy_pool = (s_mp * dmerged).astype(BF16)
        dp_ref[3] = (dmerged * y_hg * s_mh * (1.0 - s_mh)).astype(BF16)
        dp_ref[4] = (dmerged * y_pool * s_mp * (1.0 - s_mp)).astype(BF16)

        da_hg = _dot_nt(dy_hg, wdh_ref[...])
        dwdh_ref[...] += _dot_tn(a_hg, dy_hg)
        dp_ref[0] = (da_hg * o_n * dsh).astype(BF16)
        do_n = da_hg * sh
        small_ref[ROW_HG_W:ROW_HG_W + 1, :] += jnp.sum(do_n * o_hat, axis=0, keepdims=True)
        d_hat = do_n * hgw
        for h in range(N_HEADS):
            cols = slice(h * HEAD_DIM, (h + 1) * HEAD_DIM)
            dh_, oh_ = d_hat[:, cols], o_hat[:, cols]
            do_ref[:, cols] = rstd_h[h] * (dh_ - oh_ * jnp.mean(dh_ * oh_, axis=-1, keepdims=True))

        da_pool = _dot_nt(dy_pool, wdp_ref[...])
        dwdp_ref[...] += _dot_tn(a_pool, dy_pool)
        small_ref[ROW_POOL_SCALE:ROW_POOL_SCALE + 1, :] += jnp.sum(da_pool * mixed * sp, axis=0, keepdims=True)
        dp_ref[2] = (da_pool * mixed * ps * dsp).astype(BF16)
        dmixed = (da_pool * ps * sp).astype(BF16)
        carry = carry_ref[...]
        du, new_carry = [], []
        for g, w in enumerate(POOL_WINDOWS):
            cols = slice(g * POOL_GDIM, (g + 1) * POOL_GDIM)
            dmg = dmixed[:, cols]
            dpooled = _dot_nt(dmg, pw_ref[g])
            dpw_ref[g] += _dot_tn(pooled[g], dmg)
            dps = dpooled * inv_cnt[g]
            ext_b = jnp.concatenate([dps, carry[:, cols]], axis=0)
            du.append(_window_sum(ext_b, w, True)[:tm] - dpooled)
            new_carry.append(dps[:HALO])
        dp_ref[1] = jnp.where(real, jnp.concatenate(du, axis=1), 0.0).astype(BF16)
        carry_ref[...] = jnp.concatenate(new_carry, axis=1)

    row_block = pl.BlockSpec((tm, D_MODEL), lambda s: (nt - 1 - s, 0))
    seg_block = lambda seg: pl.BlockSpec((1, tm, D_MODEL), lambda s: (seg, nt - 1 - s, 0))
    whole = pl.BlockSpec(memory_space=pltpu.VMEM)
    return pl.pallas_call(
        body, name="mixers",
        grid=(nt,),
        in_specs=[
            row_block, seg_block(3), seg_block(4), seg_block(5), seg_block(6), seg_block(7),
            pl.BlockSpec((1, HALO, D_MODEL),
                         lambda s: (4, jnp.maximum((nt - 1 - s) * halo_blocks - 1, 0), 0)),
            _token_window(tm, lambda s: nt - 1 - s), _token_window(tm, lambda s: nt - 1 - s),
            whole, whole, whole, whole, whole, whole, whole, whole,
        ],
        out_specs=[
            row_block, row_block,
            pl.BlockSpec((5, tm, D_MODEL), lambda s: (0, nt - 1 - s, 0)),
            whole, whole, whole, whole, whole,
        ],
        out_shape=[
            jax.ShapeDtypeStruct((rows, D_MODEL), F32),
            jax.ShapeDtypeStruct((rows, D_MODEL), F32),
            jax.ShapeDtypeStruct((5, rows, D_MODEL), BF16),
            jax.ShapeDtypeStruct((D_MODEL, D_MODEL), F32),
            jax.ShapeDtypeStruct((D_MODEL, D_MODEL), F32),
            jax.ShapeDtypeStruct((D_MODEL, D_MODEL), F32),
            jax.ShapeDtypeStruct((n_grp, POOL_GDIM, POOL_GDIM), F32),
            jax.ShapeDtypeStruct((SMALL_ROWS, D_MODEL), F32),
        ],
        scratch_shapes=[pltpu.VMEM((HALO, D_MODEL), F32)],
        compiler_params=_params(("arbitrary",)),
    )(o, p3, p3, p3, p3, p3, p3, tokens, tgt, wdh, wdp, wout, poolw, hg_w, pool_scale, final_w, head)


def _seg_specs(tm, row_of, seg_of):
    def spec_a(*g):
        k = seg_of(*g)
        return (jnp.minimum(k, 2), jnp.where(k < 3, row_of(*g), 0), 0)

    def spec_b(*g):
        k = seg_of(*g)
        return (jnp.maximum(k - 3, 0), jnp.where(k >= 3, row_of(*g), 0), 0)

    return pl.BlockSpec((1, tm, D_MODEL), spec_a), pl.BlockSpec((1, tm, D_MODEL), spec_b)


def _in_proj_weight_grad(h, dp, rows, name):
    n_seg = dp.shape[0]
    tm = _tile(rows, 1040)
    nt = rows // tm
    half = D_MODEL // 2

    def body(h_ref, dp_ref, part_ref, part16_ref, db_ref, acc_ref, bacc_ref, stage_ref, land_ref,
             send_sems, recv_sems):
        k, i = pl.program_id(0), pl.program_id(1)
        x, y, c = lax.axis_index("x"), lax.axis_index("y"), lax.axis_index("c")

        def to_sibling(seg):
            return pltpu.make_async_remote_copy(
                src_ref=stage_ref.at[seg], dst_ref=land_ref.at[seg], send_sem=send_sems.at[seg],
                recv_sem=recv_sems.at[seg], device_id=(x, y, 1 - c), device_id_type=MESH)

        @pl.when(i == 0)
        def _():
            acc_ref[...] = jnp.zeros_like(acc_ref)
            bacc_ref[...] = jnp.zeros_like(bacc_ref)

        dpt = dp_ref[0]
        acc_ref[...] += _dot_tn(h_ref[...], dpt)
        bacc_ref[...] += jnp.sum(dpt.astype(F32), axis=0, keepdims=True)

        @pl.when(i == nt - 1)
        def _():
            db_ref[0] = bacc_ref[...]
            part_ref[k] = acc_ref[pl.ds(pl.multiple_of(c * half, half), half), :]
            stage_ref[k] = acc_ref[pl.ds(pl.multiple_of((1 - c) * half, half), half), :].astype(BF16)
            to_sibling(k).start()

        @pl.when((k == n_seg - 1) & (i == nt - 1))
        def _():
            for seg in range(n_seg):
                to_sibling(seg).wait_recv()
                total = part_ref[seg] + land_ref[seg].astype(F32)
                part_ref[seg] = total
                part16_ref[seg] = total.astype(BF16)
            for seg in range(n_seg):
                to_sibling(seg).wait_send()

    whole = pl.BlockSpec(memory_space=pltpu.VMEM)
    return pl.pallas_call(
        body, name=name,
        grid=(n_seg, nt),
        in_specs=[pl.BlockSpec((tm, D_MODEL), lambda k, i: (i, 0)),
                  pl.BlockSpec((1, tm, D_MODEL), lambda k, i: (k, i, 0))],
        out_specs=[whole, whole, pl.BlockSpec((1, 1, D_MODEL), lambda k, i: (k, 0, 0))],
        out_shape=[
            jax.ShapeDtypeStruct((n_seg, half, D_MODEL), F32),
            jax.ShapeDtypeStruct((n_seg, half, D_MODEL), BF16),
            jax.ShapeDtypeStruct((n_seg, 1, D_MODEL), F32),
        ],
        scratch_shapes=[
            pltpu.VMEM((D_MODEL, D_MODEL), F32), pltpu.VMEM((1, D_MODEL), F32),
            pltpu.VMEM((n_seg, half, D_MODEL), BF16),
            pltpu.VMEM((n_seg, half, D_MODEL), BF16),
            pltpu.SemaphoreType.DMA((n_seg,)), pltpu.SemaphoreType.DMA((n_seg,)),
        ],
        compiler_params=_params(("arbitrary", "arbitrary")),
    )(h, dp)


def _input_grad(dpa, dpb, w4, tokens, head, dz2, norm_w, dw16, rows):
    tm = _tile(rows, 1040)
    nt = rows // tm
    assert nt >= 2, rows
    exchange = _GradExchange(SEGS_REC, with_blob=False)

    def body(dpa_ref, dpb_ref, w_ref, z_ref, head_ref, dz2_ref, nw_ref, dw_ref, gx_ref, dmeta_ref, dnw_ref, rxw_ref,
             acc_ref, dz_buf, out_sem, send_sems, recv_sems):
        i, k = pl.program_id(0), pl.program_id(1)

        def first_tile_out():
            return pltpu.make_async_copy(dz_buf.at[pl.ds(FIRST_TOKEN_ROW, tm - FIRST_TOKEN_ROW), :],
                                         gx_ref.at[pl.ds(0, tm - FIRST_TOKEN_ROW), :], out_sem)

        def tile_out(tile):
            start = pl.multiple_of(tile * tm - FIRST_TOKEN_ROW, HALO)
            return pltpu.make_async_copy(dz_buf, gx_ref.at[pl.ds(start, tm), :], out_sem)

        @pl.when((i == 0) & (k == 0))
        def _():
            exchange.start(dw_ref, rxw_ref, None, None, send_sems, recv_sems)
            dnw_ref[...] = jnp.zeros_like(dnw_ref)

        @pl.when((i == nt - 1) & (k == N_SEG - 1))
        def _():
            exchange.wait(dw_ref, rxw_ref, None, None, send_sems, recv_sems)

        @pl.when(k == 0)
        def _():
            acc_ref[...] = jnp.zeros_like(acc_ref)

        @pl.when(k < 3)
        def _():
            acc_ref[...] += _dot_nt(dpa_ref[0], w_ref[0])

        @pl.when(k >= 3)
        def _():
            acc_ref[...] += _dot_nt(dpb_ref[0], w_ref[0])

        @pl.when(k == N_SEG - 1)
        def _():
            zt = _padded_tile(z_ref[...], head_ref[...], i)
            rstd = lax.rsqrt(jnp.mean(zt * zt, axis=-1, keepdims=True) + EPS)
            zh = zt * rstd
            dh = acc_ref[...]
            dnw_ref[...] += jnp.sum(dh * zh, axis=0, keepdims=True)
            uu = dh * nw_ref[...]
            dz = dz2_ref[...] + rstd * (uu - zh * jnp.mean(uu * zh, axis=-1, keepdims=True))

            @pl.when(i == 1)
            def _():
                first_tile_out().wait()

            @pl.when(i >= 2)
            def _():
                tile_out(i - 1).wait()

            dz_buf[...] = dz

            @pl.when(i == 0)
            def _():
                dmeta_ref[...] = dz[PAD_ROWS:FIRST_TOKEN_ROW]
                first_tile_out().start()

            @pl.when(i > 0)
            def _():
                tile_out(i).start()

            @pl.when(i == nt - 1)
            def _():
                tile_out(i).wait()

    spec_a, spec_b = _seg_specs(tm, lambda i, k: i, lambda i, k: k)
    last_only = pl.BlockSpec((tm, D_MODEL), lambda i, k: (jnp.where(k == N_SEG - 1, i, 0), 0))
    return pl.pallas_call(
        body, name="input_grad",
        grid=(nt, N_SEG),
        in_specs=[
            spec_a, spec_b,
            pl.BlockSpec((1, D_MODEL, D_MODEL), lambda i, k: (k // 2, 0, k % 2)),
            _token_window(tm, lambda i, k: jnp.where(k == N_SEG - 1, i, 0)),
            pl.BlockSpec((FIRST_TOKEN_ROW, D_MODEL), lambda i, k: (0, 0)),
            last_only,
            pl.BlockSpec((1, D_MODEL), lambda i, k: (0, 0)),
            ANY,
        ],
        out_specs=[
            ANY,
            pl.BlockSpec((N_META, D_MODEL), lambda i, k: (0, 0)),
            pl.BlockSpec((1, D_MODEL), lambda i, k: (0, 0)),
            ANY,
        ],
        out_shape=[
            jax.ShapeDtypeStruct((rows - FIRST_TOKEN_ROW, D_MODEL), F32),
            jax.ShapeDtypeStruct((N_META, D_MODEL), F32),
            jax.ShapeDtypeStruct((1, D_MODEL), F32),
            exchange.landing_w(),
        ],
        scratch_shapes=[pltpu.VMEM((tm, D_MODEL), F32), pltpu.VMEM((tm, D_MODEL), F32),
                        pltpu.SemaphoreType.DMA] + exchange.semaphores(),
        compiler_params=_params(("arbitrary", "arbitrary")),
    )(dpa, dpb, w4, tokens, head, dz2, norm_w, dw16)


def _local_step(tokens, head, tgt, w4, blob4, seg_order, norm_w, b_in, lb_logits, hg_w, pool_scale, final_w):
    rows = FIRST_TOKEN_ROW + tokens.shape[0]
    q = D_MODEL // N_CHIPS
    n_grp = len(POOL_WINDOWS)
    pg = POOL_GDIM // N_CHIPS

    wexp2 = jnp.asarray(np.tile(_exponent_matrix(), (1, 2)), BF16)
    wexp_t = jnp.asarray(_exponent_matrix().T, BF16)
    masks2 = jnp.asarray(_paired_masks(), F32)

    h, p3, w4 = _in_proj(tokens, head, norm_w, w4, b_in, seg_order, rows)
    o, states, e16, a2, blob4 = _hgrn_forward(p3, lb_logits, wexp2, masks2, blob4, rows)
    wdh = blob4[:, 0:q].reshape(D_MODEL, D_MODEL)
    wdp = blob4[:, q:2 * q].reshape(D_MODEL, D_MODEL)
    wout = blob4[:, 2 * q:3 * q].reshape(D_MODEL, D_MODEL)
    poolw = blob4[:, 3 * q:].reshape(N_CHIPS, n_grp, pg, POOL_GDIM).transpose(1, 0, 2, 3)
    poolw = poolw.reshape(n_grp, POOL_GDIM, POOL_GDIM)
    d_o, dz2, dpb, dwdh, dwdp, dwout, dpw, small = _mixers(
        o, p3, tokens, head, tgt, wdh, wdp, wout, poolw, hg_w, pool_scale, final_w, rows)
    dpw4 = dpw.reshape(n_grp, N_CHIPS, pg, POOL_GDIM).transpose(1, 0, 2, 3)
    dpw4 = dpw4.reshape(N_CHIPS, n_grp * pg * POOL_GDIM // D_MODEL, D_MODEL)
    dblob4 = jnp.concatenate([dwdh.reshape(N_CHIPS, q, D_MODEL), dwdp.reshape(N_CHIPS, q, D_MODEL),
                              dwout.reshape(N_CHIPS, q, D_MODEL), dpw4], axis=1)

    dw_mix, dw_mix16, db_mix = _in_proj_weight_grad(h, dpb, rows, "in_proj_weight_grad_mix")
    dpa, dlb, rxw_mix, rx_blob = _hgrn_backward(
        p3, d_o, states, e16, a2, lb_logits, wexp_t, masks2, dw_mix16, dblob4.astype(BF16), rows)
    dw_rec, dw_rec16, db_rec = _in_proj_weight_grad(h, dpa, rows, "in_proj_weight_grad_rec")
    d_tokens, d_meta, dnw, rxw_rec = _input_grad(dpa, dpb, w4, tokens, head, dz2, norm_w, dw_rec16, rows)

    small = jnp.concatenate([
        small[ROW_LOSS:ROW_LOSS + 1],
        d_meta,
        dnw,
        db_rec.reshape(len(SEGS_REC), D_MODEL), db_mix.reshape(len(SEGS_MIX), D_MODEL),
        dlb, jnp.zeros_like(dlb),
        small[ROW_HG_W:ROW_HG_W + 1], small[ROW_POOL_SCALE:ROW_POOL_SCALE + 1],
        small[ROW_FINAL_W:ROW_FINAL_W + 1],
        jnp.zeros((SMALL_ROWS - ROW_FINAL_W - 1, D_MODEL), F32),
    ], axis=0)
    return d_tokens, (dw_rec, dw_mix, rxw_rec, rxw_mix), (dblob4, rx_blob), small


ANY = pl.BlockSpec(memory_space=pl.ANY)
MESH = pl.DeviceIdType.MESH


def _place():
    x, y, c = lax.axis_index("x"), lax.axis_index("y"), lax.axis_index("c")
    chips = [(1 - x, y), (x, 1 - y), (1 - x, 1 - y)]
    return x, y, c, chips


class _ShardGather:
    def __init__(self, rows):
        self.half = rows // 2

    def semaphores(self):
        return [pltpu.SemaphoreType.DMA((6,)), pltpu.SemaphoreType.DMA((6,))]

    def _copy(self, k, slot, to, send_sems, recv_sems):
        return pltpu.make_async_remote_copy(src_ref=slot, dst_ref=slot, send_sem=send_sems.at[k],
                                            recv_sem=recv_sems.at[k], device_id=to, device_id_type=MESH)

    def _half(self, ref4, chip, which):
        return ref4.at[chip, pl.ds(which * self.half, self.half), :]

    def start(self, ref4, send_sems, recv_sems, which=(0, 1, 2)):
        x, y, c, chips = _place()
        for j in which:
            cx, cy = chips[j]
            self._copy(j, self._half(ref4, 2 * x + y, c), (cx, cy, c), send_sems, recv_sems).start()

    def start_diagonal_after_neighbours(self, ref4, send_sems, recv_sems):
        x, y, c, chips = _place()
        for j in (0, 1):
            cx, cy = chips[j]
            self._copy(j, self._half(ref4, 2 * x + y, c), (cx, cy, c), send_sems, recv_sems).wait_send()
        self.start(ref4, send_sems, recv_sems, which=(2,))

    def pass_on(self, j, ref4, send_sems, recv_sems):
        x, y, c, chips = _place()
        cx, cy = chips[j]
        landed = self._half(ref4, 2 * cx + cy, c)
        self._copy(j, landed, (cx, cy, c), send_sems, recv_sems).wait_recv()
        self._copy(3 + j, landed, (x, y, 1 - c), send_sems, recv_sems).start()

    def await_sibling(self, j, ref4, send_sems, recv_sems):
        x, y, c, chips = _place()
        cx, cy = chips[j]
        self._copy(3 + j, self._half(ref4, 2 * cx + cy, 1 - c), (x, y, 1 - c), send_sems, recv_sems).wait_recv()

    def finish(self, ref4, send_sems, recv_sems, which=(0, 1, 2)):
        x, y, c, chips = _place()
        for j, (cx, cy) in enumerate(chips):
            if j in which:
                self._copy(j, self._half(ref4, 2 * x + y, c), (cx, cy, c), send_sems, recv_sems).wait_send()
            self._copy(3 + j, self._half(ref4, 2 * cx + cy, c), (x, y, 1 - c), send_sems, recv_sems).wait_send()


def _gather_meta(m4):
    def body(m_in_ref, m4_ref, send_sems, recv_sems):
        x, y, c, chips = _place()

        def copy(j, slot, to):
            return pltpu.make_async_remote_copy(src_ref=slot, dst_ref=slot, send_sem=send_sems.at[j],
                                                recv_sem=recv_sems.at[j], device_id=to, device_id_type=MESH)

        sends = [copy(j, m4_ref.at[2 * x + y], (cx, cy, c)) for j, (cx, cy) in enumerate(chips)]
        for cp in sends:
            cp.start()
        for j, (cx, cy) in enumerate(chips):
            copy(j, m4_ref.at[2 * cx + cy], (x, y, c)).wait_recv()
        for cp in sends:
            cp.wait_send()

    return pl.pallas_call(
        body, name="gather_meta",
        in_specs=[ANY], out_specs=ANY, out_shape=jax.ShapeDtypeStruct(m4.shape, m4.dtype),
        input_output_aliases={0: 0},
        scratch_shapes=[pltpu.SemaphoreType.DMA((3,)), pltpu.SemaphoreType.DMA((3,))],
    )(m4)


class _GradExchange:
    def __init__(self, segs, with_blob):
        self.segs = tuple(segs)
        self.with_blob = with_blob

    def landing_w(self):
        return jax.ShapeDtypeStruct((N_CHIPS, 2, D_MODEL // 2, D_MODEL), BF16)

    def landing_blob(self, blob16):
        return jax.ShapeDtypeStruct((N_DEV, blob16.shape[1] // 2, D_MODEL), BF16)

    def semaphores(self):
        n_send = len(self.segs) + (2 * N_CHIPS if self.with_blob else 0)
        n_recv = 2 * N_CHIPS + (N_DEV if self.with_blob else 0)
        return [pltpu.SemaphoreType.DMA((n_send,)), pltpu.SemaphoreType.DMA((n_recv,))]

    def _copies(self, dw_ref, rxw_ref, blob_ref, rxb_ref, send_sems, recv_sems):
        x, y, c = lax.axis_index("x"), lax.axis_index("y"), lax.axis_index("c")
        chip = 2 * x + y

        def relation(kx, ky, h):
            return (x ^ kx) * 4 + (y ^ ky) * 2 + (c ^ h)

        def copy(src, dst, send_k, recv_k, to):
            return functools.partial(pltpu.make_async_remote_copy, src_ref=src, dst_ref=dst,
                                     send_sem=send_sems.at[send_k], recv_sem=recv_sems.at[recv_k],
                                     device_id=to, device_id_type=MESH)

        sends, recvs = [], []
        for i, s in enumerate(self.segs):
            kx, ky = (s // 2) >> 1, (s // 2) & 1
            r = (x ^ kx) * 2 + (y ^ ky)
            sends.append((r != 0, copy(dw_ref.at[i], rxw_ref.at[r, s % 2], i, 2 * r + s % 2, (kx, ky, c))))
        for j in range(2):
            mine = [s // 2 for s in self.segs if s % 2 == j]
            if mine:
                cond = functools.reduce(lambda a, b: a | b, [chip == k for k in mine])
                for r in range(1, N_CHIPS):
                    slot = rxw_ref.at[r, j]
                    recvs.append((cond, copy(slot, slot, 0, 2 * r + j, (x, y, c))))
        if self.with_blob:
            hb = blob_ref.shape[1] // 2
            first_send, first_recv = len(self.segs), 2 * N_CHIPS
            for k in range(N_CHIPS):
                for h in range(2):
                    r = relation(k >> 1, k & 1, h)
                    sends.append((r != 0, copy(blob_ref.at[k, pl.ds(h * hb, hb), :], rxb_ref.at[r],
                                               first_send + 2 * k + h, first_recv + r, (k >> 1, k & 1, h))))
            for r in range(1, N_DEV):
                slot = rxb_ref.at[r]
                recvs.append((None, copy(slot, slot, 0, first_recv + r, (x, y, c))))
        return sends, recvs

    def start(self, *refs):
        sends, _ = self._copies(*refs)
        for cond, make in sends:
            pl.when(cond)(lambda make=make: make().start())

    def wait(self, *refs):
        sends, recvs = self._copies(*refs)
        for cond, make in sends:
            pl.when(cond)(lambda make=make: make().wait_send())
        for cond, make in recvs:
            if cond is None:
                make().wait_recv()
            else:
                pl.when(cond)(lambda make=make: make().wait_recv())


def _sum_landed(own, rx_ref):
    total = own
    for r in range(1, rx_ref.shape[0]):
        total = total + rx_ref[r, 0].astype(F32)
    return total


def _finish_w(dw_rec, dw_mix, rx_rec, rx_mix, place_arr):
    half = D_MODEL // 2
    tm = _tile(half, 256)
    n_rec = len(SEGS_REC)

    def body(place_ref, own_rec_ref, own_mix_ref, rx_rec_ref, rx_mix_ref, out_ref):
        seg = 2 * place_ref[0] + pl.program_id(0)

        @pl.when(seg < n_rec)
        def _():
            out_ref[0] = _sum_landed(own_rec_ref[0], rx_rec_ref)

        @pl.when(seg >= n_rec)
        def _():
            out_ref[0] = _sum_landed(own_mix_ref[0], rx_mix_ref)

    def own_spec(first, count):
        def index(j, i, place_ref):
            seg = 2 * place_ref[0] + j
            return (jnp.clip(seg - first, 0, count - 1), i, 0)
        return pl.BlockSpec((1, tm, D_MODEL), index)

    rx_spec = pl.BlockSpec((N_CHIPS, 1, tm, D_MODEL), lambda j, i, place_ref: (0, j, i, 0))
    return pl.pallas_call(
        body, name="finish_w",
        grid_spec=pltpu.PrefetchScalarGridSpec(
            num_scalar_prefetch=1, grid=(2, half // tm),
            in_specs=[own_spec(0, n_rec), own_spec(n_rec, len(SEGS_MIX)), rx_spec, rx_spec],
            out_specs=pl.BlockSpec((1, tm, D_MODEL), lambda j, i, place_ref: (place_ref[1], i, j))),
        out_shape=jax.ShapeDtypeStruct((2, half, 2 * D_MODEL), F32),
        compiler_params=_params(("arbitrary", "arbitrary")),
    )(place_arr, dw_rec, dw_mix, rx_rec, rx_mix)


def _finish_blob(dblob4, rx_blob, place_arr):
    n, rows, cols = rx_blob.shape
    tm = _tile(rows, 256)

    def body(place_ref, own_ref, rx_ref, out_ref):
        out_ref[0] = _sum_landed(own_ref[0, 0], rx_ref)

    return pl.pallas_call(
        body, name="finish_blob",
        grid_spec=pltpu.PrefetchScalarGridSpec(
            num_scalar_prefetch=1, grid=(rows // tm,),
            in_specs=[pl.BlockSpec((1, 1, tm, cols), lambda i, place_ref: (place_ref[0], place_ref[1], i, 0)),
                      pl.BlockSpec((n, 1, tm, cols), lambda i, place_ref: (0, 0, i, 0))],
            out_specs=pl.BlockSpec((1, tm, cols), lambda i, place_ref: (place_ref[1], i, 0))),
        out_shape=jax.ShapeDtypeStruct((2, rows, cols), F32),
        compiler_params=_params(("arbitrary",)),
    )(place_arr, dblob4.reshape(N_CHIPS, 2, rows, cols), rx_blob.reshape(n, 1, rows, cols))


def _share_finished(fw2, fb2, small):
    def body(w_in_ref, b_in_ref, small_ref, w_ref, b_ref, s_ref, bounce, local_sem, send_sems, recv_sems):
        x, y, c, _ = _place()
        sibling = (x, y, 1 - c)

        def copy(k, src, dst, to):
            return pltpu.make_async_remote_copy(src_ref=src, dst_ref=dst, send_sem=send_sems.at[k],
                                                recv_sem=recv_sems.at[k], device_id=to, device_id_type=MESH)

        sends = [copy(0, w_ref.at[c], w_ref.at[c], sibling), copy(1, b_ref.at[c], b_ref.at[c], sibling)]
        for r in range(1, N_DEV):
            peer = (x ^ ((r >> 2) & 1), y ^ ((r >> 1) & 1), c ^ (r & 1))
            sends.append(copy(1 + r, small_ref, s_ref.at[r], peer))
        for cp in sends:
            cp.start()
        for src, dst in ((small_ref, bounce), (bounce, s_ref.at[0])):
            own = pltpu.make_async_copy(src, dst, local_sem)
            own.start()
            own.wait()
        landed = [w_ref.at[1 - c], b_ref.at[1 - c]] + [s_ref.at[r] for r in range(1, N_DEV)]
        for k, slot in enumerate(landed):
            copy(k, slot, slot, (x, y, c)).wait_recv()
        for cp in sends:
            cp.wait_send()

    same = lambda a: jax.ShapeDtypeStruct(a.shape, a.dtype)
    n_sem = 2 + N_DEV - 1
    return pl.pallas_call(
        body, name="share_finished",
        in_specs=[ANY, ANY, ANY], out_specs=[ANY, ANY, ANY],
        out_shape=[same(fw2), same(fb2), jax.ShapeDtypeStruct((N_DEV,) + small.shape, F32)],
        input_output_aliases={0: 0, 1: 1},
        scratch_shapes=[pltpu.VMEM(small.shape, F32), pltpu.SemaphoreType.DMA,
                        pltpu.SemaphoreType.DMA((n_sem,)), pltpu.SemaphoreType.DMA((n_sem,))],
    )(fw2, fb2, small)


def _sum_small(slots, lb_logits, me_arr):
    def body(me_ref, slots_ref, lbl_ref, out_ref):
        me = me_ref[0]
        total = slots_ref[me]
        for d in range(1, N_DEV):
            total = total + slots_ref[d ^ me]
        out_ref[...] = total
        out_ref[ROW_LOSS:ROW_LOSS + 1, :] = jnp.broadcast_to(
            jnp.sum(total[ROW_LOSS:ROW_LOSS + 1, :], axis=-1, keepdims=True), (1, D_MODEL))
        lb = _lower_bound(lbl_ref[...])
        g0 = total[ROW_LB:ROW_LB + 1, :] * lb * (1.0 - lb)
        out_ref[ROW_LB:ROW_LB + 1, :] = g0
        out_ref[ROW_LB + 1:ROW_LB + 2, :] = -g0

    return pl.pallas_call(
        body, name="sum_small",
        grid_spec=pltpu.PrefetchScalarGridSpec(
            num_scalar_prefetch=1, grid=(1,),
            in_specs=[pl.BlockSpec((N_DEV, SMALL_ROWS, D_MODEL), lambda i, me_ref: (0, 0, 0)),
                      pl.BlockSpec((2, D_MODEL), lambda i, me_ref: (0, 0))],
            out_specs=pl.BlockSpec((SMALL_ROWS, D_MODEL), lambda i, me_ref: (0, 0))),
        out_shape=jax.ShapeDtypeStruct((SMALL_ROWS, D_MODEL), F32),
        compiler_params=_params(("arbitrary",)),
    )(me_arr, slots, lb_logits)


def _adamw(w, g, m, v):
    rows, cols = w.shape
    tm = _tile(rows, 256, mult=8) if rows % 8 == 0 else rows
    c1 = 1.0 / (1.0 - ADAM_B1 ** ADAM_STEP)
    c2 = 1.0 / (1.0 - ADAM_B2 ** ADAM_STEP)

    def body(w_ref, g_ref, m_ref, v_ref, d_ref, nm_ref, nv_ref):
        gt = g_ref[...]
        nm = ADAM_B1 * m_ref[...] + (1.0 - ADAM_B1) * gt
        nv = ADAM_B2 * v_ref[...] + (1.0 - ADAM_B2) * (gt * gt)
        nm_ref[...] = nm
        nv_ref[...] = nv
        d_ref[...] = -ADAM_LR * ((nm * c1) / (jnp.sqrt(nv * c2) + ADAM_EPS) + ADAM_WD * w_ref[...])

    blk = pl.BlockSpec((tm, cols), lambda i: (i, 0))
    sds = jax.ShapeDtypeStruct((rows, cols), F32)
    return pl.pallas_call(
        body, name="adamw",
        grid=(rows // tm,), in_specs=[blk] * 4, out_specs=[blk] * 3, out_shape=[sds] * 3,
        compiler_params=_params(("arbitrary",)),
    )(w, g, m, v)


def kernel(x, meta_tokens, norm_w, w_in, b_in, lb_logits, hg_norm_w, pool_w, pool_scale, w_down_hg, w_down_pool, w_out, final_norm_w, loss_target, m_meta_tokens, m_norm_w, m_w_in, m_b_in, m_lb_logits, m_hg_norm_w, m_pool_w, m_pool_scale, m_w_down_hg, m_w_down_pool, m_w_out, m_final_norm_w, v_meta_tokens, v_norm_w, v_w_in, v_b_in, v_lb_logits, v_hg_norm_w, v_pool_w, v_pool_scale, v_w_down_hg, v_w_down_pool, v_w_out, v_final_norm_w):
    seq = x.shape[1]
    xi, yi, ci = lax.axis_index("x"), lax.axis_index("y"), lax.axis_index("c")
    chip = 2 * xi + yi
    place_arr = jnp.stack([chip, ci]).astype(jnp.int32)
    me_arr = jnp.reshape(4 * xi + 2 * yi + ci, (1,)).astype(jnp.int32)
    q = D_MODEL // N_CHIPS

    def blob_of(wdh, wdp, wo, pw):
        return jnp.concatenate([wdh[0], wdp[0], wo[0], pw[0].reshape(-1, D_MODEL)], axis=0)

    def in_every_slot(a):
        return jnp.broadcast_to(a[None], (N_CHIPS,) + a.shape)

    meta4 = _gather_meta(in_every_slot(meta_tokens))
    meta_full = meta4.transpose(1, 0, 2).reshape(N_META, D_MODEL)
    w4 = in_every_slot(w_in[0].astype(BF16))
    blob4 = in_every_slot(blob_of(w_down_hg, w_down_pool, w_out, pool_w).astype(BF16))
    seg_order = jnp.stack([2 * (chip ^ rel) + t for rel in (0, 2, 1, 3) for t in (0, 1)]).astype(jnp.int32)

    head = jnp.concatenate([jnp.zeros((PAD_ROWS, D_MODEL), F32), meta_full], axis=0)
    fw2 = final_norm_w.reshape(1, D_MODEL)
    d_tokens, w_parts, blob_parts, small = _local_step(
        x[0], head, loss_target[0], w4, blob4, seg_order, norm_w, b_in, lb_logits, hg_norm_w, pool_scale, fw2)
    grad_x = d_tokens[None]

    fin_w = _finish_w(*w_parts, place_arr)
    fin_b = _finish_blob(*blob_parts, place_arr)
    gw2, gb2, slots = _share_finished(fin_w, fin_b, small)
    tot = _sum_small(slots, lb_logits, me_arr)
    g_w_in = gw2.reshape(D_MODEL, 2 * D_MODEL)
    g_blob = gb2.reshape(-1, D_MODEL)

    d_win, nm_win, nv_win = _adamw(w_in[0], g_w_in, m_w_in[0], v_w_in[0])
    d_blob, nm_blob, nv_blob = _adamw(
        blob_of(w_down_hg, w_down_pool, w_out, pool_w), g_blob,
        blob_of(m_w_down_hg, m_w_down_pool, m_w_out, m_pool_w),
        blob_of(v_w_down_hg, v_w_down_pool, v_w_out, v_pool_w))
    g_meta = lax.dynamic_slice_in_dim(tot[ROW_META:ROW_META + N_META], chip * q, q, axis=1)
    d_meta, nm_meta, nv_meta = _adamw(meta_tokens, g_meta, m_meta_tokens, v_meta_tokens)

    def rows_of(nw, bi, lbl, hg, ps, fw):
        return jnp.concatenate([nw, bi.reshape(N_SEG, D_MODEL), lbl, hg, ps, fw.reshape(1, D_MODEL),
                                jnp.zeros((2, D_MODEL), F32)], axis=0)

    g_rows = jnp.concatenate([tot[ROW_NORM_W:ROW_FINAL_W + 1], jnp.zeros((2, D_MODEL), F32)], axis=0)
    d_rows, nm_rows, nv_rows = _adamw(
        rows_of(norm_w, b_in, lb_logits, hg_norm_w, pool_scale, final_norm_w), g_rows,
        rows_of(m_norm_w, m_b_in, m_lb_logits, m_hg_norm_w, m_pool_scale, m_final_norm_w),
        rows_of(v_norm_w, v_b_in, v_lb_logits, v_hg_norm_w, v_pool_scale, v_final_norm_w))

    def unblob(b):
        return (b[0:q][None], b[q:2 * q][None], b[2 * q:3 * q][None], b[3 * q:].reshape(pool_w.shape))

    def unrows(r):
        o = ROW_NORM_W
        return dict(norm_w=r[ROW_NORM_W - o:ROW_B_IN - o], b_in=r[ROW_B_IN - o:ROW_LB - o].reshape(1, -1),
                    lb_logits=r[ROW_LB - o:ROW_HG_W - o], hg_norm_w=r[ROW_HG_W - o:ROW_POOL_SCALE - o],
                    pool_scale=r[ROW_POOL_SCALE - o:ROW_FINAL_W - o], final_norm_w=r[ROW_FINAL_W - o])

    def leaves(meta_part, rows_part, win_part, blob_part):
        r = unrows(rows_part)
        wdh, wdp, wo, pw = unblob(blob_part)
        return [meta_part, r["norm_w"], win_part[None], r["b_in"], r["lb_logits"], r["hg_norm_w"], pw,
                r["pool_scale"], wdh, wdp, wo, r["final_norm_w"]]

    loss = tot[ROW_LOSS, 0]
    return (loss, grad_x,
            *leaves(g_meta, g_rows, g_w_in, g_blob),
            *leaves(d_meta, d_rows, d_win, d_blob),
            *leaves(nm_meta, nm_rows, nm_win, nm_blob),
            *leaves(nv_meta, nv_rows, nv_win, nv_blob))
```

```python
import functools

import numpy as np
import jax
import jax.numpy as jnp
from jax import lax
from jax.experimental import pallas as pl
from jax.experimental.pallas import tpu as pltpu

F32 = jnp.float32
BF16 = jnp.bfloat16

D_MODEL = 1024
N_SEG = 8
N_HEADS = 8
HEAD_DIM = 128
CHUNK = 64
N_META = 16
PAD_ROWS = CHUNK - N_META
FIRST_TOKEN_ROW = CHUNK
LEVELS = (32, 16, 8, 4, 2, 1)
N_EXP = 2 + len(LEVELS)
POOL_WINDOWS = (2, 4, 8, 16)
POOL_GDIM = D_MODEL // len(POOL_WINDOWS)
HALO = 16
LOCAL_UNROLL = 13
BACKWARD_UNROLL = 13
EPS = 1e-6
N_CHIPS = 4
N_DEV = 8
SEGS_REC = (0, 1, 2)
SEGS_MIX = (3, 4, 5, 6, 7)

ADAM_LR = 0.001
ADAM_B1 = 0.9
ADAM_B2 = 0.999
ADAM_EPS = 1e-08
ADAM_WD = 0.01
ADAM_STEP = 10

VMEM_LIMIT_BYTES = 56 * 1024 * 1024

ROW_LOSS = 0
ROW_META = 1
ROW_NORM_W = ROW_META + N_META
ROW_B_IN = ROW_NORM_W + 1
ROW_LB = ROW_B_IN + N_SEG
ROW_HG_W = ROW_LB + 2
ROW_POOL_SCALE = ROW_HG_W + 1
ROW_FINAL_W = ROW_POOL_SCALE + 1
SMALL_ROWS = 32


def _tile(total, cap, mult=16):
    best = None
    for t in range(mult, min(total, cap) + 1, mult):
        if total % t == 0:
            best = t
    assert best is not None, (total, cap, mult)
    return best


def _token_window(tm, tile_of):
    def index(*grid):
        return (pl.multiple_of(jnp.maximum(tile_of(*grid) * tm - FIRST_TOKEN_ROW, 0), HALO), 0)
    return pl.BlockSpec((pl.Element(tm), pl.Element(D_MODEL)), index)


def _padded_tile(window, head, tile):
    first = jnp.concatenate([head, pltpu.roll(window, FIRST_TOKEN_ROW, 0)[FIRST_TOKEN_ROW:]], axis=0)
    return jnp.where(tile == 0, first, window)


def _params(sem=None):
    return pltpu.CompilerParams(dimension_semantics=sem, vmem_limit_bytes=VMEM_LIMIT_BYTES)


def _dot(a, b):
    return jnp.dot(a, b, preferred_element_type=F32)


def _dot_nt(a, b):
    return lax.dot_general(a, b, (((1,), (1,)), ((), ())), preferred_element_type=F32)


def _dot_tn(a, b):
    return lax.dot_general(a, b, (((0,), (0,)), ((), ())), preferred_element_type=F32)


def _sigmoid_pair(x):
    t = jnp.exp(-jnp.abs(x))
    r = 1.0 / (1.0 + t)
    pos = x >= 0
    return jnp.where(pos, r, t * r), jnp.where(pos, t * r, r)


def _exponent_matrix():
    t = np.arange(CHUNK)[:, None]
    j = np.arange(CHUNK)[None, :]
    blocks = [j <= t, j > t]
    for m in LEVELS:
        rho = (t // (2 * m)) * (2 * m) + m
        upper = (t >= rho) & (j > rho) & (j <= t)
        lower = (t < rho) & (j > t) & (j <= rho)
        blocks.append(upper | lower)
    return np.concatenate(blocks, axis=0).astype(np.float32)


def _pair_masks():
    t = np.arange(CHUNK)[:, None]
    s = np.arange(CHUNK)[None, :]
    masks = [t == s]
    for m in LEVELS:
        same = (t // (2 * m)) == (s // (2 * m))
        masks.append(same & ((t % (2 * m)) >= m) & ((s % (2 * m)) < m))
    return np.stack(masks).astype(np.float32)


LEVEL_PAIRS = ((0, 1), (2, 3), (4, 5), (6, None))


def _paired_masks():
    m = _pair_masks()
    zero = np.zeros_like(m[0])
    return np.stack([np.concatenate([m[a], zero if b is None else m[b]], axis=1) for a, b in LEVEL_PAIRS])


def _lower_bound(lbl):
    return 1.0 / (1.0 + jnp.exp(lbl[1:2, :] - lbl[0:1, :]))


def _in_proj(tokens, head, norm_w, w4, b_in, seg_order, rows):
    tm = _tile(rows, 1040)
    nt = rows // tm
    gather = _ShardGather(w4.shape[1])

    def body(order_ref, z_ref, head_ref, nw_ref, b_ref, w_in_ref, h_ref, p_ref, w4_ref,
             h_all, w_buf, w_sem, send_sems, recv_sems):
        kk, i = pl.program_id(0), pl.program_id(1)

        @pl.when((kk == 0) & (i == 0))
        def _():
            gather.start(w4_ref, send_sems, recv_sems, which=(0, 1))

        @pl.when((kk == 2) & (i == 0))
        def _():
            gather.start_diagonal_after_neighbours(w4_ref, send_sems, recv_sems)

        @pl.when(kk == 0)
        def _():
            zt = _padded_tile(z_ref[...], head_ref[...], i)
            rstd = lax.rsqrt(jnp.mean(zt * zt, axis=-1, keepdims=True) + EPS)
            h = (zt * rstd * nw_ref[...]).astype(BF16)
            h_all[pl.ds(pl.multiple_of(i * tm, 16), tm), :] = h
            h_ref[...] = h

        @pl.when((kk == 2) & (i == 0))
        def _():
            gather.pass_on(0, w4_ref, send_sems, recv_sems)
            gather.pass_on(1, w4_ref, send_sems, recv_sems)
            gather.await_sibling(0, w4_ref, send_sems, recv_sems)

        @pl.when((kk == 4) & (i == 0))
        def _():
            gather.await_sibling(1, w4_ref, send_sems, recv_sems)

        @pl.when((kk == 5) & (i == 0))
        def _():
            gather.pass_on(2, w4_ref, send_sems, recv_sems)

        @pl.when((kk == 6) & (i == 0))
        def _():
            gather.await_sibling(2, w4_ref, send_sems, recv_sems)

        def weights(which):
            seg = order_ref[2 * (kk // 2) + which]
            return pltpu.make_async_copy(
                w4_ref.at[seg // 2, :, pl.ds(pl.multiple_of((seg % 2) * D_MODEL, D_MODEL), D_MODEL)],
                w_buf.at[which], w_sem.at[which])

        @pl.when((i == 0) & (kk % 2 == 0))
        def _():
            weights(0).start()
            weights(1).start()
            weights(0).wait()

        @pl.when((i == 0) & (kk % 2 == 1))
        def _():
            weights(1).wait()

        p_ref[0] = _dot(h_all[pl.ds(pl.multiple_of(i * tm, 16), tm), :], w_buf[kk % 2]) + b_ref[...]

        @pl.when((kk == N_SEG - 1) & (i == nt - 1))
        def _():
            gather.finish(w4_ref, send_sems, recv_sems, which=(2,))

    first_pass = lambda kk, i, order_ref: (jnp.where(kk == 0, i, nt - 1), 0)
    return pl.pallas_call(
        body, name="in_proj",
        grid_spec=pltpu.PrefetchScalarGridSpec(
            num_scalar_prefetch=1, grid=(N_SEG, nt),
            in_specs=[
                _token_window(tm, lambda kk, i, order_ref: jnp.where(kk == 0, i, nt - 1)),
                pl.BlockSpec((FIRST_TOKEN_ROW, D_MODEL), lambda kk, i, order_ref: (0, 0)),
                pl.BlockSpec((1, D_MODEL), lambda kk, i, order_ref: (0, 0)),
                pl.BlockSpec((1, D_MODEL), lambda kk, i, order_ref: (0, order_ref[kk])),
                ANY,
            ],
            out_specs=[
                pl.BlockSpec((tm, D_MODEL), first_pass),
                pl.BlockSpec((1, tm, D_MODEL), lambda kk, i, order_ref: (order_ref[kk], i, 0)),
                ANY,
            ],
            scratch_shapes=[
                pltpu.VMEM((rows, D_MODEL), BF16),
                pltpu.VMEM((2, D_MODEL, D_MODEL), BF16),
                pltpu.SemaphoreType.DMA((2,)),
            ] + gather.semaphores()),
        out_shape=[
            jax.ShapeDtypeStruct((rows, D_MODEL), BF16),
            jax.ShapeDtypeStruct((N_SEG, rows, D_MODEL), F32),
            jax.ShapeDtypeStruct(w4.shape, w4.dtype),
        ],
        input_output_aliases={5: 2},
        compiler_params=_params(("arbitrary", "arbitrary")),
    )(seg_order, tokens, head, norm_w, b_in, w4)


def _hgrn_forward(p3, lb_logits, wexp2, masks2, blob4, rows):
    n_chunks = rows // CHUNK
    cpb = _tile(n_chunks, 13, mult=1)
    rb_rows = cpb * CHUNK
    n_rb = n_chunks // cpb
    lanes = cpb * HEAD_DIM
    gather = _ShardGather(blob4.shape[1])

    def body(q_ref, fz_ref, v_ref, lbl_ref, wexp_ref, mask_ref, b_in_ref, o_ref, s_ref, e16_ref, a2_ref, b4_ref,
             st_ref, e_ref, u_ref, q_s, kk_s, v_s, send_sems, recv_sems):
        rb = pl.program_id(1)

        @pl.when((pl.program_id(0) == 0) & (rb == 0))
        def _():
            gather.start(b4_ref, send_sems, recv_sems)

        @pl.when(rb == 0)
        def _():
            st_ref[...] = jnp.zeros_like(st_ref)

        lb = _lower_bound(lbl_ref[...])
        row = rb * rb_rows + lax.broadcasted_iota(jnp.int32, (rb_rows, 1), 0)
        valid = row >= PAD_ROWS
        sg, sn = _sigmoid_pair(fz_ref[0])
        g = jnp.where(valid, jnp.log(lb + (1.0 - lb) * sg), 0.0)
        kk_s[...] = jnp.where(valid, (1.0 - lb) * sn, 0.0)
        q_s[...] = jnp.where(valid, q_ref[0], 0.0)
        v_s[...] = jnp.where(valid, v_ref[0], 0.0).astype(BF16)
        hi = g.astype(BF16)
        mid = (g - hi.astype(F32)).astype(BF16)
        g2 = jnp.concatenate(
            [jnp.concatenate([hi[b * CHUNK:(b + 1) * CHUNK], mid[b * CHUNK:(b + 1) * CHUNK]], axis=0)
             for b in range(cpb)], axis=1)
        e_ref[...] = jnp.exp(_dot(wexp_ref[...], g2))
        e16_ref[0, 0] = e_ref[...].astype(BF16)

        def contribution(b, carry):
            r0 = pl.multiple_of(b * CHUNK, CHUNK)
            l0 = pl.multiple_of(b * HEAD_DIM, HEAD_DIM)
            kc16 = (kk_s[pl.ds(r0, CHUNK), :] * e_ref[CHUNK:2 * CHUNK, pl.ds(l0, HEAD_DIM)]).astype(BF16)
            u_ref[b] = _dot_tn(v_s[pl.ds(r0, CHUNK), :], kc16)
            return carry

        lax.fori_loop(0, cpb, contribution, 0, unroll=LOCAL_UNROLL)

        def recur(b, st):
            l0 = pl.multiple_of(b * HEAD_DIM, HEAD_DIM)
            s_ref[0, b] = st
            return st * e_ref[CHUNK - 1:CHUNK, pl.ds(l0, HEAD_DIM)] + u_ref[b]

        st_ref[...] = lax.fori_loop(0, cpb, recur, st_ref[...])

        zeros16 = jnp.zeros((CHUNK, HEAD_DIM), BF16)

        def local(b, carry):
            r0 = pl.multiple_of(b * CHUNK, CHUNK)
            l0 = pl.multiple_of(b * HEAD_DIM, HEAD_DIM)
            q = q_s[pl.ds(r0, CHUNK), :]
            kk = kk_s[pl.ds(r0, CHUNK), :]
            v16 = v_s[pl.ds(r0, CHUNK), :]

            def scaled(entry):
                if entry == 0:
                    return q.astype(BF16), kk.astype(BF16)
                e_m = e_ref[(1 + entry) * CHUNK:(2 + entry) * CHUNK, pl.ds(l0, HEAD_DIM)]
                return (q * e_m).astype(BF16), (kk * e_m).astype(BF16)

            a2 = jnp.zeros((CHUNK, 2 * CHUNK), F32)
            for p, (ea, eb) in enumerate(LEVEL_PAIRS):
                qa, ka = scaled(ea)
                if eb is None:
                    prod = _dot_nt(qa, jnp.concatenate([ka, zeros16], axis=0))
                else:
                    qb_, kb_ = scaled(eb)
                    rhs = jnp.concatenate([jnp.concatenate([ka, zeros16], axis=1),
                                           jnp.concatenate([zeros16, kb_], axis=1)], axis=0)
                    prod = _dot_nt(jnp.concatenate([qa, qb_], axis=1), rhs)
                a2 = a2 + mask_ref[p] * prod
            a2_16 = a2.astype(BF16)
            a2_ref[pl.ds(r0, CHUNK), :] = a2_16
            qb16 = (q * e_ref[0:CHUNK, pl.ds(l0, HEAD_DIM)]).astype(BF16)
            o_ref[pl.ds(r0, CHUNK), :] = (_dot(a2_16, jnp.concatenate([v16, v16], axis=0))
                                          + _dot_nt(qb16, s_ref[0, b].astype(BF16)))
            return carry

        lax.fori_loop(0, cpb, local, 0, unroll=LOCAL_UNROLL)

        @pl.when((pl.program_id(0) == N_HEADS // 2) & (rb == 0))
        def _():
            for j in range(N_CHIPS - 1):
                gather.pass_on(j, b4_ref, send_sems, recv_sems)

        @pl.when((pl.program_id(0) == N_HEADS - 1) & (rb == n_rb - 1))
        def _():
            for j in range(N_CHIPS - 1):
                gather.await_sibling(j, b4_ref, send_sems, recv_sems)
            gather.finish(b4_ref, send_sems, recv_sems)

    head_block = lambda seg: pl.BlockSpec((1, rb_rows, HEAD_DIM), lambda h, r: (seg, r, h))
    return pl.pallas_call(
        body, name="hgrn_forward",
        grid=(N_HEADS, n_rb),
        in_specs=[
            head_block(0), head_block(1), head_block(2),
            pl.BlockSpec((2, HEAD_DIM), lambda h, r: (0, h)),
            pl.BlockSpec((N_EXP * CHUNK, 2 * CHUNK), lambda h, r: (0, 0)),
            pl.BlockSpec((len(LEVEL_PAIRS), CHUNK, 2 * CHUNK), lambda h, r: (0, 0, 0)),
            ANY,
        ],
        out_specs=[
            pl.BlockSpec((rb_rows, HEAD_DIM), lambda h, r: (r, h)),
            pl.BlockSpec((1, cpb, HEAD_DIM, HEAD_DIM), lambda h, r: (h, r, 0, 0)),
            pl.BlockSpec((1, 1, N_EXP * CHUNK, lanes), lambda h, r: (h, r, 0, 0)),
            pl.BlockSpec((rb_rows, HEAD_DIM), lambda h, r: (r, h)),
            ANY,
        ],
        out_shape=[
            jax.ShapeDtypeStruct((rows, D_MODEL), F32),
            jax.ShapeDtypeStruct((N_HEADS, n_chunks, HEAD_DIM, HEAD_DIM), F32),
            jax.ShapeDtypeStruct((N_HEADS, n_rb, N_EXP * CHUNK, lanes), BF16),
            jax.ShapeDtypeStruct((rows, D_MODEL), BF16),
            jax.ShapeDtypeStruct(blob4.shape, blob4.dtype),
        ],
        input_output_aliases={6: 4},
        scratch_shapes=[
            pltpu.VMEM((HEAD_DIM, HEAD_DIM), F32),
            pltpu.VMEM((N_EXP * CHUNK, lanes), F32),
            pltpu.VMEM((cpb, HEAD_DIM, HEAD_DIM), F32),
            pltpu.VMEM((rb_rows, HEAD_DIM), F32),
            pltpu.VMEM((rb_rows, HEAD_DIM), F32),
            pltpu.VMEM((rb_rows, HEAD_DIM), BF16),
        ] + gather.semaphores(),
        compiler_params=_params(("arbitrary", "arbitrary")),
    )(p3, p3, p3, lb_logits, wexp2, masks2, blob4)


def _hgrn_backward(p3, d_o, states, e16, a2, lb_logits, wexp_t, masks2, dw16, blob16, rows):
    n_chunks = rows // CHUNK
    cpb = _tile(n_chunks, 13, mult=1)
    rb_rows = cpb * CHUNK
    n_rb = n_chunks // cpb
    lanes = cpb * HEAD_DIM
    exchange = _GradExchange(SEGS_MIX, with_blob=True)

    def body(q_ref, fz_ref, v_ref, do_ref, s_ref, e_ref, a2_ref, lbl_ref, wexpt_ref, mask_ref, dw_ref, blob_ref,
             dp_ref, dlb_ref, rxw_ref, rxb_ref,
             dst_ref, g_ref, dsn_ref, q_s, kk_s, v_s, do_s, dq_s, dkk_s, dg_s, dx_s, da2_s, send_sems, recv_sems):
        step = pl.program_id(1)
        rb = n_rb - 1 - step

        @pl.when((pl.program_id(0) == 0) & (step == 0))
        def _():
            exchange.start(dw_ref, rxw_ref, blob_ref, rxb_ref, send_sems, recv_sems)

        @pl.when(step == 0)
        def _():
            dst_ref[...] = jnp.zeros_like(dst_ref)
            dlb_ref[...] = jnp.zeros_like(dlb_ref)

        lb = _lower_bound(lbl_ref[...])
        row = rb * rb_rows + lax.broadcasted_iota(jnp.int32, (rb_rows, 1), 0)
        valid = row >= PAD_ROWS
        sg, sn = _sigmoid_pair(fz_ref[0])
        f = lb + (1.0 - lb) * sg
        g = jnp.where(valid, jnp.log(f), 0.0)
        kk_s[...] = jnp.where(valid, (1.0 - lb) * sn, 0.0)
        q_s[...] = jnp.where(valid, q_ref[0], 0.0)
        v_s[...] = jnp.where(valid, v_ref[0], 0.0).astype(BF16)
        do_s[...] = do_ref[...].astype(BF16)
        e_last_all = jnp.exp(jnp.concatenate(
            [jnp.sum(g[b * CHUNK:(b + 1) * CHUNK], axis=0, keepdims=True) for b in range(cpb)], axis=0))
        last_row = lax.broadcasted_iota(jnp.int32, (CHUNK, 1), 0) == CHUNK - 1
        zeros16 = jnp.zeros((CHUNK, HEAD_DIM), BF16)

        def factor(block, l0):
            return e_ref[0, 0, block * CHUNK:(block + 1) * CHUNK, pl.ds(l0, HEAD_DIM)].astype(F32)

        def contribution(b, carry):
            r0 = pl.multiple_of(b * CHUNK, CHUNK)
            l0 = pl.multiple_of(b * HEAD_DIM, HEAD_DIM)
            qb16 = (q_s[pl.ds(r0, CHUNK), :] * factor(0, l0)).astype(BF16)
            g_ref[b] = _dot_tn(do_s[pl.ds(r0, CHUNK), :], qb16)
            return carry

        lax.fori_loop(0, cpb, contribution, 0, unroll=LOCAL_UNROLL)

        cur = dst_ref[...]
        for b in reversed(range(cpb)):
            dsn_ref[b] = cur
            cur = cur * e_last_all[b:b + 1, :] + g_ref[b]
        dst_ref[...] = cur

        def through_state(b, carry):
            r0 = pl.multiple_of(b * CHUNK, CHUNK)
            l0 = pl.multiple_of(b * HEAD_DIM, HEAD_DIM)
            v16 = v_s[pl.ds(r0, CHUNK), :]
            do16 = do_s[pl.ds(r0, CHUNK), :]
            st = s_ref[0, b]
            dsn = dsn_ref[b]
            dsn16 = dsn.astype(BF16)
            e_b, e_c = factor(0, l0), factor(1, l0)
            qb, kc = q_s[pl.ds(r0, CHUNK), :] * e_b, kk_s[pl.ds(r0, CHUNK), :] * e_c

            t = _dot_tn(a2_ref[pl.ds(r0, CHUNK), :], do16)
            dv = t[0:CHUNK] + t[CHUNK:2 * CHUNK] + _dot_nt(kc.astype(BF16), dsn16)
            dp_ref[2, pl.ds(r0, CHUNK), :] = dv.astype(BF16)
            da2_s[pl.ds(r0, CHUNK), :] = _dot_nt(do16, jnp.concatenate([v16, v16], axis=0))
            dqb = _dot(do16, st.astype(BF16))
            dkc = _dot(v16, dsn16)
            de = jnp.sum(dsn * st, axis=0, keepdims=True) * e_b[CHUNK - 1:CHUNK, :]
            dq_s[pl.ds(r0, CHUNK), :] = e_b * dqb
            dkk_s[pl.ds(r0, CHUNK), :] = e_c * dkc
            dx_s[0:CHUNK, pl.ds(l0, HEAD_DIM)] = (qb * dqb + jnp.where(last_row, de, 0.0)).astype(BF16)
            dx_s[CHUNK:2 * CHUNK, pl.ds(l0, HEAD_DIM)] = (kc * dkc).astype(BF16)
            return carry

        lax.fori_loop(0, cpb, through_state, 0, unroll=BACKWARD_UNROLL)

        def local(b, carry):
            r0 = pl.multiple_of(b * CHUNK, CHUNK)
            l0 = pl.multiple_of(b * HEAD_DIM, HEAD_DIM)
            q = q_s[pl.ds(r0, CHUNK), :]
            kk = kk_s[pl.ds(r0, CHUNK), :]
            da2 = da2_s[pl.ds(r0, CHUNK), :]
            dq = dq_s[pl.ds(r0, CHUNK), :]
            dkk = dkk_s[pl.ds(r0, CHUNK), :]

            def scaled(entry):
                if entry == 0:
                    return q, kk, None
                e_m = factor(1 + entry, l0)
                return q * e_m, kk * e_m, e_m

            for p, (ea, eb) in enumerate(LEVEL_PAIRS):
                dm = mask_ref[p] * da2
                dm_t = dm.T.astype(BF16)
                qa, ka, e_a = scaled(ea)
                if eb is None:
                    rhs_k = jnp.concatenate([jnp.concatenate([ka.astype(BF16), zeros16], axis=1),
                                             jnp.concatenate([zeros16, zeros16], axis=1)], axis=0)
                else:
                    qb_, kb_, e_bb = scaled(eb)
                    rhs_k = jnp.concatenate([jnp.concatenate([ka.astype(BF16), zeros16], axis=1),
                                             jnp.concatenate([zeros16, kb_.astype(BF16)], axis=1)], axis=0)
                dq2 = _dot(dm.astype(BF16), rhs_k)
                parts = [(ea, qa, ka, e_a, dq2[:, :HEAD_DIM], _dot(dm_t[0:CHUNK], qa.astype(BF16)))]
                if eb is not None:
                    parts.append((eb, qb_, kb_, e_bb, dq2[:, HEAD_DIM:],
                                  _dot(dm_t[CHUNK:2 * CHUNK], qb_.astype(BF16))))
                for entry, q_m, k_m, e_m, dq_m, dk_m in parts:
                    if entry == 0:
                        dq = dq + dq_m
                        dkk = dkk + dk_m
                    else:
                        dq = dq + e_m * dq_m
                        dkk = dkk + e_m * dk_m
                        dx_s[(1 + entry) * CHUNK:(2 + entry) * CHUNK, pl.ds(l0, HEAD_DIM)] = (
                            q_m * dq_m + k_m * dk_m).astype(BF16)
            dq_s[pl.ds(r0, CHUNK), :] = dq
            dkk_s[pl.ds(r0, CHUNK), :] = dkk
            return carry

        lax.fori_loop(0, cpb, local, 0, unroll=BACKWARD_UNROLL)

        dg_all = _dot(wexpt_ref[...], dx_s[...])
        for b in range(cpb):
            dg_s[b * CHUNK:(b + 1) * CHUNK, :] = dg_all[:, b * HEAD_DIM:(b + 1) * HEAD_DIM]
        t = jnp.where(valid, dg_s[...] / f - dkk_s[...], 0.0)
        dlb_ref[...] += jnp.sum(sn * t, axis=0, keepdims=True)
        dp_ref[0] = jnp.where(valid, dq_s[...], 0.0).astype(BF16)
        dp_ref[1] = ((1.0 - lb) * sg * sn * t).astype(BF16)

        @pl.when((pl.program_id(0) == N_HEADS - 1) & (step == n_rb - 1))
        def _():
            exchange.wait(dw_ref, rxw_ref, blob_ref, rxb_ref, send_sems, recv_sems)

    head_block = lambda seg: pl.BlockSpec((1, rb_rows, HEAD_DIM), lambda h, s: (seg, n_rb - 1 - s, h))
    row_block = pl.BlockSpec((rb_rows, HEAD_DIM), lambda h, s: (n_rb - 1 - s, h))
    return pl.pallas_call(
        body, name="hgrn_backward",
        grid=(N_HEADS, n_rb),
        in_specs=[
            head_block(0), head_block(1), head_block(2),
            row_block,
            pl.BlockSpec((1, cpb, HEAD_DIM, HEAD_DIM), lambda h, s: (h, n_rb - 1 - s, 0, 0)),
            pl.BlockSpec((1, 1, N_EXP * CHUNK, lanes), lambda h, s: (h, n_rb - 1 - s, 0, 0)),
            row_block,
            pl.BlockSpec((2, HEAD_DIM), lambda h, s: (0, h)),
            pl.BlockSpec((CHUNK, N_EXP * CHUNK), lambda h, s: (0, 0)),
            pl.BlockSpec((len(LEVEL_PAIRS), CHUNK, 2 * CHUNK), lambda h, s: (0, 0, 0)),
            ANY, ANY,
        ],
        out_specs=[
            pl.BlockSpec((3, rb_rows, HEAD_DIM), lambda h, s: (0, n_rb - 1 - s, h)),
            pl.BlockSpec((1, HEAD_DIM), lambda h, s: (0, h)),
            ANY, ANY,
        ],
        out_shape=[
            jax.ShapeDtypeStruct((3, rows, D_MODEL), BF16),
            jax.ShapeDtypeStruct((1, D_MODEL), F32),
            exchange.landing_w(), exchange.landing_blob(blob16),
        ],
        scratch_shapes=[
            pltpu.VMEM((HEAD_DIM, HEAD_DIM), F32),
            pltpu.VMEM((cpb, HEAD_DIM, HEAD_DIM), F32),
            pltpu.VMEM((cpb, HEAD_DIM, HEAD_DIM), F32),
            pltpu.VMEM((rb_rows, HEAD_DIM), F32),
            pltpu.VMEM((rb_rows, HEAD_DIM), F32),
            pltpu.VMEM((rb_rows, HEAD_DIM), BF16),
            pltpu.VMEM((rb_rows, HEAD_DIM), BF16),
            pltpu.VMEM((rb_rows, HEAD_DIM), F32),
            pltpu.VMEM((rb_rows, HEAD_DIM), F32),
            pltpu.VMEM((rb_rows, HEAD_DIM), F32),
            pltpu.VMEM((N_EXP * CHUNK, lanes), BF16),
            pltpu.VMEM((rb_rows, 2 * CHUNK), F32),
        ] + exchange.semaphores(),
        compiler_params=_params(("arbitrary", "arbitrary")),
    )(p3, p3, p3, d_o, states, e16, a2, lb_logits, wexp_t, masks2, dw16, blob16)


def _sigmoid(x):
    return 1.0 / (1.0 + jnp.exp(-x))


def _silu_and_grad(x):
    s = _sigmoid(x)
    return x * s, s * (1.0 + x * (1.0 - s))


def _window_sum(ext, width, forward_looking):
    n = ext.shape[0]
    s = ext
    step = 1
    while step < width:
        s = s + pltpu.roll(s, (n - step) if forward_looking else step, 0)
        step *= 2
    return s


def _mixers(o, p3, tokens, head, tgt, wdh, wdp, wout, poolw, hg_w, pool_scale, final_w, rows):
    tm = _tile(rows, 208)
    nt = rows // tm
    halo_blocks = tm // HALO
    n_grp = len(POOL_WINDOWS)
    q_rows = D_MODEL // N_CHIPS
    blob_rows = 3 * q_rows + n_grp * POOL_GDIM * POOL_GDIM // (N_CHIPS * D_MODEL)

    def body(o_ref, ghg_ref, u_ref, gpl_ref, mhg_ref, mpl_ref, uh_ref, z_ref, t_ref,
             wdh_ref, wdp_ref, wout_ref, pw_ref, hgw_ref, ps_ref, fw_ref, head_ref,
             do_ref, dz2_ref, dp_ref, blob_ref, dpw_ref, small_ref, carry_ref):
        step = pl.program_id(0)
        tile = nt - 1 - step

        def add_to_blob(piece, dw):
            for k in range(N_CHIPS):
                blob_ref[k, piece * q_rows:(piece + 1) * q_rows, :] += dw[k * q_rows:(k + 1) * q_rows]

        @pl.when(step == 0)
        def _():
            blob_ref[...] = jnp.zeros_like(blob_ref)
            dpw_ref[...] = jnp.zeros_like(dpw_ref)
            small_ref[...] = jnp.zeros_like(small_ref)
            carry_ref[...] = jnp.zeros_like(carry_ref)

        row = tile * tm + lax.broadcasted_iota(jnp.int32, (tm, 1), 0)
        real = row >= PAD_ROWS
        pos1 = jnp.maximum(row - PAD_ROWS + 1, 1).astype(F32)

        u = jnp.where(real, u_ref[0], 0.0)
        halo_row = tile * tm - HALO + lax.broadcasted_iota(jnp.int32, (HALO, 1), 0)
        uh = jnp.where(halo_row >= PAD_ROWS, uh_ref[0], 0.0)
        ext = jnp.concatenate([uh, u], axis=0)
        pooled, inv_cnt, mixed = [], [], []
        for g, w in enumerate(POOL_WINDOWS):
            cols = slice(g * POOL_GDIM, (g + 1) * POOL_GDIM)
            inv = 1.0 / jnp.minimum(pos1, float(w))
            ws = _window_sum(ext[:, cols], w, False)[HALO:]
            pg = (ws * inv - u[:, cols]).astype(BF16)
            pooled.append(pg)
            inv_cnt.append(inv)
            mixed.append(_dot(pg, pw_ref[g]))
        mixed = jnp.concatenate(mixed, axis=1)
        gpl = gpl_ref[0]
        sp, dsp = _silu_and_grad(gpl)
        ps = ps_ref[...]
        a_pool = (mixed * ps * sp).astype(BF16)
        y_pool = _dot(a_pool, wdp_ref[...])

        o = o_ref[...]
        o_hat, rstd_h = [], []
        for h in range(N_HEADS):
            oh = o[:, h * HEAD_DIM:(h + 1) * HEAD_DIM]
            r = lax.rsqrt(jnp.mean(oh * oh, axis=-1, keepdims=True) + EPS)
            rstd_h.append(r)
            o_hat.append(oh * r)
        o_hat = jnp.concatenate(o_hat, axis=1)
        hgw = hgw_ref[...]
        o_n = o_hat * hgw
        ghg = ghg_ref[0]
        sh, dsh = _silu_and_grad(ghg)
        a_hg = (o_n * sh).astype(BF16)
        y_hg = _dot(a_hg, wdh_ref[...])

        s_mh = _sigmoid(mhg_ref[0])
        s_mp = _sigmoid(mpl_ref[0])
        merged = (s_mh * y_hg + s_mp * y_pool).astype(BF16)
        z2 = _padded_tile(z_ref[...], head_ref[...], tile) + _dot(merged, wout_ref[...])
        rstd2 = lax.rsqrt(jnp.mean(z2 * z2, axis=-1, keepdims=True) + EPS)
        zh = z2 * rstd2
        fw = fw_ref[...]
        target = _padded_tile(t_ref[...], jnp.zeros((FIRST_TOKEN_ROW, D_MODEL), F32), tile)
        err = jnp.where(row >= FIRST_TOKEN_ROW, zh * fw - target, 0.0)
        small_ref[ROW_LOSS:ROW_LOSS + 1, :] += jnp.sum(err * err, axis=0, keepdims=True) * (0.5 / D_MODEL)
        dy = err * (1.0 / D_MODEL)

        small_ref[ROW_FINAL_W:ROW_FINAL_W + 1, :] += jnp.sum(dy * zh, axis=0, keepdims=True)
        uu = dy * fw
        dz2 = rstd2 * (uu - zh * jnp.mean(uu * zh, axis=-1, keepdims=True))
        dz2_ref[...] = dz2
        dz2_16 = dz2.astype(BF16)
        dmerged = _dot_nt(dz2_16, wout_ref[...])
        add_to_blob(2, _dot_tn(merged, dz2_16))
        dy_hg = (s_mh * dmerged).astype(BF16)
        dy_pool = (s_mp * dmerged).astype(BF16)
        dp_ref[3] = (dmerged * y_hg * s_mh * (1.0 - s_mh)).astype(BF16)
        dp_ref[4] = (dmerged * y_pool * s_mp * (1.0 - s_mp)).astype(BF16)

        da_hg = _dot_nt(dy_hg, wdh_ref[...])
        add_to_blob(0, _dot_tn(a_hg, dy_hg))
        dp_ref[0] = (da_hg * o_n * dsh).astype(BF16)
        do_n = da_hg * sh
        small_ref[ROW_HG_W:ROW_HG_W + 1, :] += jnp.sum(do_n * o_hat, axis=0, keepdims=True)
        d_hat = do_n * hgw
        for h in range(N_HEADS):
            cols = slice(h * HEAD_DIM, (h + 1) * HEAD_DIM)
            dh_, oh_ = d_hat[:, cols], o_hat[:, cols]
            do_ref[:, cols] = rstd_h[h] * (dh_ - oh_ * jnp.mean(dh_ * oh_, axis=-1, keepdims=True))

        da_pool = _dot_nt(dy_pool, wdp_ref[...])
        add_to_blob(1, _dot_tn(a_pool, dy_pool))
        small_ref[ROW_POOL_SCALE:ROW_POOL_SCALE + 1, :] += jnp.sum(da_pool * mixed * sp, axis=0, keepdims=True)
        dp_ref[2] = (da_pool * mixed * ps * dsp).astype(BF16)
        dmixed = (da_pool * ps * sp).astype(BF16)
        carry = carry_ref[...]
        du, new_carry = [], []
        for g, w in enumerate(POOL_WINDOWS):
            cols = slice(g * POOL_GDIM, (g + 1) * POOL_GDIM)
            dmg = dmixed[:, cols]
            dpooled = _dot_nt(dmg, pw_ref[g])
            dpw_ref[g] += _dot_tn(pooled[g], dmg)
            dps = dpooled * inv_cnt[g]
            ext_b = jnp.concatenate([dps, carry[:, cols]], axis=0)
            du.append(_window_sum(ext_b, w, True)[:tm] - dpooled)
            new_carry.append(dps[:HALO])
        dp_ref[1] = jnp.where(real, jnp.concatenate(du, axis=1), 0.0).astype(BF16)
        carry_ref[...] = jnp.concatenate(new_carry, axis=1)

    row_block = pl.BlockSpec((tm, D_MODEL), lambda s: (nt - 1 - s, 0))
    seg_block = lambda seg: pl.BlockSpec((1, tm, D_MODEL), lambda s: (seg, nt - 1 - s, 0))
    whole = pl.BlockSpec(memory_space=pltpu.VMEM)
    return pl.pallas_call(
        body, name="mixers",
        grid=(nt,),
        in_specs=[
            row_block, seg_block(3), seg_block(4), seg_block(5), seg_block(6), seg_block(7),
            pl.BlockSpec((1, HALO, D_MODEL),
                         lambda s: (4, jnp.maximum((nt - 1 - s) * halo_blocks - 1, 0), 0)),
            _token_window(tm, lambda s: nt - 1 - s), _token_window(tm, lambda s: nt - 1 - s),
            whole, whole, whole, whole, whole, whole, whole, whole,
        ],
        out_specs=[
            row_block, row_block,
            pl.BlockSpec((5, tm, D_MODEL), lambda s: (0, nt - 1 - s, 0)),
            whole, whole, whole,
        ],
        out_shape=[
            jax.ShapeDtypeStruct((rows, D_MODEL), F32),
            jax.ShapeDtypeStruct((rows, D_MODEL), F32),
            jax.ShapeDtypeStruct((5, rows, D_MODEL), BF16),
            jax.ShapeDtypeStruct((N_CHIPS, blob_rows, D_MODEL), F32),
            jax.ShapeDtypeStruct((n_grp, POOL_GDIM, POOL_GDIM), F32),
            jax.ShapeDtypeStruct((SMALL_ROWS, D_MODEL), F32),
        ],
        scratch_shapes=[pltpu.VMEM((HALO, D_MODEL), F32)],
        compiler_params=_params(("arbitrary",)),
    )(o, p3, p3, p3, p3, p3, p3, tokens, tgt, wdh, wdp, wout, poolw, hg_w, pool_scale, final_w, head)


def _seg_specs(tm, row_of, seg_of):
    def spec_a(*g):
        k = seg_of(*g)
        return (jnp.minimum(k, 2), jnp.where(k < 3, row_of(*g), 0), 0)

    def spec_b(*g):
        k = seg_of(*g)
        return (jnp.maximum(k - 3, 0), jnp.where(k >= 3, row_of(*g), 0), 0)

    return pl.BlockSpec((1, tm, D_MODEL), spec_a), pl.BlockSpec((1, tm, D_MODEL), spec_b)


def _in_proj_weight_grad(h, dp, rows, name):
    n_seg = dp.shape[0]
    tm = _tile(rows, 1040)
    nt = rows // tm
    half = D_MODEL // 2

    def body(h_ref, dp_ref, part_ref, part16_ref, db_ref, acc_ref, bacc_ref, stage_ref, land_ref,
             send_sems, recv_sems):
        k, i = pl.program_id(0), pl.program_id(1)
        x, y, c = lax.axis_index("x"), lax.axis_index("y"), lax.axis_index("c")

        def to_sibling(seg):
            return pltpu.make_async_remote_copy(
                src_ref=stage_ref.at[seg], dst_ref=land_ref.at[seg], send_sem=send_sems.at[seg],
                recv_sem=recv_sems.at[seg], device_id=(x, y, 1 - c), device_id_type=MESH)

        @pl.when(i == 0)
        def _():
            acc_ref[...] = jnp.zeros_like(acc_ref)
            bacc_ref[...] = jnp.zeros_like(bacc_ref)

        dpt = dp_ref[0]
        acc_ref[...] += _dot_tn(h_ref[...], dpt)
        bacc_ref[...] += jnp.sum(dpt.astype(F32), axis=0, keepdims=True)

        @pl.when(i == nt - 1)
        def _():
            db_ref[0] = bacc_ref[...]
            part_ref[k] = acc_ref[pl.ds(pl.multiple_of(c * half, half), half), :]
            stage_ref[k] = acc_ref[pl.ds(pl.multiple_of((1 - c) * half, half), half), :].astype(BF16)
            to_sibling(k).start()

        @pl.when((k == n_seg - 1) & (i == nt - 1))
        def _():
            for seg in range(n_seg):
                to_sibling(seg).wait_recv()
                total = part_ref[seg] + land_ref[seg].astype(F32)
                part_ref[seg] = total
                part16_ref[seg] = total.astype(BF16)
            for seg in range(n_seg):
                to_sibling(seg).wait_send()

    whole = pl.BlockSpec(memory_space=pltpu.VMEM)
    return pl.pallas_call(
        body, name=name,
        grid=(n_seg, nt),
        in_specs=[pl.BlockSpec((tm, D_MODEL), lambda k, i: (i, 0)),
                  pl.BlockSpec((1, tm, D_MODEL), lambda k, i: (k, i, 0))],
        out_specs=[whole, whole, pl.BlockSpec((1, 1, D_MODEL), lambda k, i: (k, 0, 0))],
        out_shape=[
            jax.ShapeDtypeStruct((n_seg, half, D_MODEL), F32),
            jax.ShapeDtypeStruct((n_seg, half, D_MODEL), BF16),
            jax.ShapeDtypeStruct((n_seg, 1, D_MODEL), F32),
        ],
        scratch_shapes=[
            pltpu.VMEM((D_MODEL, D_MODEL), F32), pltpu.VMEM((1, D_MODEL), F32),
            pltpu.VMEM((n_seg, half, D_MODEL), BF16),
            pltpu.VMEM((n_seg, half, D_MODEL), BF16),
            pltpu.SemaphoreType.DMA((n_seg,)), pltpu.SemaphoreType.DMA((n_seg,)),
        ],
        compiler_params=_params(("arbitrary", "arbitrary")),
    )(h, dp)


def _input_grad(dpa, dpb, w4, tokens, head, dz2, norm_w, dw16, rows):
    tm = _tile(rows, 1040)
    nt = rows // tm
    assert nt >= 2, rows
    exchange = _GradExchange(SEGS_REC, with_blob=False)

    def body(dpa_ref, dpb_ref, w_ref, z_ref, head_ref, dz2_ref, nw_ref, dw_ref, gx_ref, dmeta_ref, dnw_ref, rxw_ref,
             acc_ref, dz_buf, out_sem, send_sems, recv_sems):
        i, k = pl.program_id(0), pl.program_id(1)

        def first_tile_out():
            return pltpu.make_async_copy(dz_buf.at[pl.ds(FIRST_TOKEN_ROW, tm - FIRST_TOKEN_ROW), :],
                                         gx_ref.at[pl.ds(0, tm - FIRST_TOKEN_ROW), :], out_sem)

        def tile_out(tile):
            start = pl.multiple_of(tile * tm - FIRST_TOKEN_ROW, HALO)
            return pltpu.make_async_copy(dz_buf, gx_ref.at[pl.ds(start, tm), :], out_sem)

        @pl.when((i == 0) & (k == 0))
        def _():
            exchange.start(dw_ref, rxw_ref, None, None, send_sems, recv_sems)
            dnw_ref[...] = jnp.zeros_like(dnw_ref)

        @pl.when((i == nt - 1) & (k == N_SEG - 1))
        def _():
            exchange.wait(dw_ref, rxw_ref, None, None, send_sems, recv_sems)

        @pl.when(k == 0)
        def _():
            acc_ref[...] = jnp.zeros_like(acc_ref)

        @pl.when(k < 3)
        def _():
            acc_ref[...] += _dot_nt(dpa_ref[0], w_ref[0])

        @pl.when(k >= 3)
        def _():
            acc_ref[...] += _dot_nt(dpb_ref[0], w_ref[0])

        @pl.when(k == N_SEG - 1)
        def _():
            zt = _padded_tile(z_ref[...], head_ref[...], i)
            rstd = lax.rsqrt(jnp.mean(zt * zt, axis=-1, keepdims=True) + EPS)
            zh = zt * rstd
            dh = acc_ref[...]
            dnw_ref[...] += jnp.sum(dh * zh, axis=0, keepdims=True)
            uu = dh * nw_ref[...]
            dz = dz2_ref[...] + rstd * (uu - zh * jnp.mean(uu * zh, axis=-1, keepdims=True))

            @pl.when(i == 1)
            def _():
                first_tile_out().wait()

            @pl.when(i >= 2)
            def _():
                tile_out(i - 1).wait()

            dz_buf[...] = dz

            @pl.when(i == 0)
            def _():
                dmeta_ref[...] = dz[PAD_ROWS:FIRST_TOKEN_ROW]
                first_tile_out().start()

            @pl.when(i > 0)
            def _():
                tile_out(i).start()

            @pl.when(i == nt - 1)
            def _():
                tile_out(i).wait()

    spec_a, spec_b = _seg_specs(tm, lambda i, k: i, lambda i, k: k)
    last_only = pl.BlockSpec((tm, D_MODEL), lambda i, k: (jnp.where(k == N_SEG - 1, i, 0), 0))
    return pl.pallas_call(
        body, name="input_grad",
        grid=(nt, N_SEG),
        in_specs=[
            spec_a, spec_b,
            pl.BlockSpec((1, D_MODEL, D_MODEL), lambda i, k: (k // 2, 0, k % 2)),
            _token_window(tm, lambda i, k: jnp.where(k == N_SEG - 1, i, 0)),
            pl.BlockSpec((FIRST_TOKEN_ROW, D_MODEL), lambda i, k: (0, 0)),
            last_only,
            pl.BlockSpec((1, D_MODEL), lambda i, k: (0, 0)),
            ANY,
        ],
        out_specs=[
            ANY,
            pl.BlockSpec((N_META, D_MODEL), lambda i, k: (0, 0)),
            pl.BlockSpec((1, D_MODEL), lambda i, k: (0, 0)),
            ANY,
        ],
        out_shape=[
            jax.ShapeDtypeStruct((rows - FIRST_TOKEN_ROW, D_MODEL), F32),
            jax.ShapeDtypeStruct((N_META, D_MODEL), F32),
            jax.ShapeDtypeStruct((1, D_MODEL), F32),
            exchange.landing_w(),
        ],
        scratch_shapes=[pltpu.VMEM((tm, D_MODEL), F32), pltpu.VMEM((tm, D_MODEL), F32),
                        pltpu.SemaphoreType.DMA] + exchange.semaphores(),
        compiler_params=_params(("arbitrary", "arbitrary")),
    )(dpa, dpb, w4, tokens, head, dz2, norm_w, dw16)


def _local_step(tokens, head, tgt, w4, blob4, seg_order, norm_w, b_in, lb_logits, hg_w, pool_scale, final_w):
    rows = FIRST_TOKEN_ROW + tokens.shape[0]
    q = D_MODEL // N_CHIPS
    n_grp = len(POOL_WINDOWS)
    pg = POOL_GDIM // N_CHIPS

    wexp2 = jnp.asarray(np.tile(_exponent_matrix(), (1, 2)), BF16)
    wexp_t = jnp.asarray(_exponent_matrix().T, BF16)
    masks2 = jnp.asarray(_paired_masks(), F32)

    h, p3, w4 = _in_proj(tokens, head, norm_w, w4, b_in, seg_order, rows)
    o, states, e16, a2, blob4 = _hgrn_forward(p3, lb_logits, wexp2, masks2, blob4, rows)
    wdh = blob4[:, 0:q].reshape(D_MODEL, D_MODEL)
    wdp = blob4[:, q:2 * q].reshape(D_MODEL, D_MODEL)
    wout = blob4[:, 2 * q:3 * q].reshape(D_MODEL, D_MODEL)
    poolw = blob4[:, 3 * q:].reshape(N_CHIPS, n_grp, pg, POOL_GDIM).transpose(1, 0, 2, 3)
    poolw = poolw.reshape(n_grp, POOL_GDIM, POOL_GDIM)
    d_o, dz2, dpb, dblob4, dpw, small = _mixers(
        o, p3, tokens, head, tgt, wdh, wdp, wout, poolw, hg_w, pool_scale, final_w, rows)
    dpw4 = dpw.reshape(n_grp, N_CHIPS, pg, POOL_GDIM).transpose(1, 0, 2, 3)
    dpw4 = dpw4.reshape(N_CHIPS, n_grp * pg * POOL_GDIM // D_MODEL, D_MODEL)
    dblob4 = dblob4.at[:, 3 * q:, :].set(dpw4)

    dw_mix, dw_mix16, db_mix = _in_proj_weight_grad(h, dpb, rows, "in_proj_weight_grad_mix")
    dpa, dlb, rxw_mix, rx_blob = _hgrn_backward(
        p3, d_o, states, e16, a2, lb_logits, wexp_t, masks2, dw_mix16, dblob4.astype(BF16), rows)
    dw_rec, dw_rec16, db_rec = _in_proj_weight_grad(h, dpa, rows, "in_proj_weight_grad_rec")
    d_tokens, d_meta, dnw, rxw_rec = _input_grad(dpa, dpb, w4, tokens, head, dz2, norm_w, dw_rec16, rows)

    small = jnp.concatenate([
        small[ROW_LOSS:ROW_LOSS + 1],
        d_meta,
        dnw,
        db_rec.reshape(len(SEGS_REC), D_MODEL), db_mix.reshape(len(SEGS_MIX), D_MODEL),
        dlb, jnp.zeros_like(dlb),
        small[ROW_HG_W:ROW_HG_W + 1], small[ROW_POOL_SCALE:ROW_POOL_SCALE + 1],
        small[ROW_FINAL_W:ROW_FINAL_W + 1],
        jnp.zeros((SMALL_ROWS - ROW_FINAL_W - 1, D_MODEL), F32),
    ], axis=0)
    return d_tokens, (dw_rec, dw_mix, rxw_rec, rxw_mix), (dblob4, rx_blob), small


ANY = pl.BlockSpec(memory_space=pl.ANY)
MESH = pl.DeviceIdType.MESH


def _place():
    x, y, c = lax.axis_index("x"), lax.axis_index("y"), lax.axis_index("c")
    chips = [(1 - x, y), (x, 1 - y), (1 - x, 1 - y)]
    return x, y, c, chips


class _ShardGather:
    def __init__(self, rows):
        self.half = rows // 2

    def semaphores(self):
        return [pltpu.SemaphoreType.DMA((6,)), pltpu.SemaphoreType.DMA((6,))]

    def _copy(self, k, slot, to, send_sems, recv_sems):
        return pltpu.make_async_remote_copy(src_ref=slot, dst_ref=slot, send_sem=send_sems.at[k],
                                            recv_sem=recv_sems.at[k], device_id=to, device_id_type=MESH)

    def _half(self, ref4, chip, which):
        return ref4.at[chip, pl.ds(which * self.half, self.half), :]

    def start(self, ref4, send_sems, recv_sems, which=(0, 1, 2)):
        x, y, c, chips = _place()
        for j in which:
            cx, cy = chips[j]
            self._copy(j, self._half(ref4, 2 * x + y, c), (cx, cy, c), send_sems, recv_sems).start()

    def start_diagonal_after_neighbours(self, ref4, send_sems, recv_sems):
        x, y, c, chips = _place()
        for j in (0, 1):
            cx, cy = chips[j]
            self._copy(j, self._half(ref4, 2 * x + y, c), (cx, cy, c), send_sems, recv_sems).wait_send()
        self.start(ref4, send_sems, recv_sems, which=(2,))

    def pass_on(self, j, ref4, send_sems, recv_sems):
        x, y, c, chips = _place()
        cx, cy = chips[j]
        landed = self._half(ref4, 2 * cx + cy, c)
        self._copy(j, landed, (cx, cy, c), send_sems, recv_sems).wait_recv()
        self._copy(3 + j, landed, (x, y, 1 - c), send_sems, recv_sems).start()

    def await_sibling(self, j, ref4, send_sems, recv_sems):
        x, y, c, chips = _place()
        cx, cy = chips[j]
        self._copy(3 + j, self._half(ref4, 2 * cx + cy, 1 - c), (x, y, 1 - c), send_sems, recv_sems).wait_recv()

    def finish(self, ref4, send_sems, recv_sems, which=(0, 1, 2)):
        x, y, c, chips = _place()
        for j, (cx, cy) in enumerate(chips):
            if j in which:
                self._copy(j, self._half(ref4, 2 * x + y, c), (cx, cy, c), send_sems, recv_sems).wait_send()
            self._copy(3 + j, self._half(ref4, 2 * cx + cy, c), (x, y, 1 - c), send_sems, recv_sems).wait_send()


def _gather_meta(m4):
    def body(m_in_ref, m4_ref, send_sems, recv_sems):
        x, y, c, chips = _place()

        def copy(j, slot, to):
            return pltpu.make_async_remote_copy(src_ref=slot, dst_ref=slot, send_sem=send_sems.at[j],
                                                recv_sem=recv_sems.at[j], device_id=to, device_id_type=MESH)

        sends = [copy(j, m4_ref.at[2 * x + y], (cx, cy, c)) for j, (cx, cy) in enumerate(chips)]
        for cp in sends:
            cp.start()
        for j, (cx, cy) in enumerate(chips):
            copy(j, m4_ref.at[2 * cx + cy], (x, y, c)).wait_recv()
        for cp in sends:
            cp.wait_send()

    return pl.pallas_call(
        body, name="gather_meta",
        in_specs=[ANY], out_specs=ANY, out_shape=jax.ShapeDtypeStruct(m4.shape, m4.dtype),
        input_output_aliases={0: 0},
        scratch_shapes=[pltpu.SemaphoreType.DMA((3,)), pltpu.SemaphoreType.DMA((3,))],
    )(m4)


class _GradExchange:
    def __init__(self, segs, with_blob):
        self.segs = tuple(segs)
        self.with_blob = with_blob

    def landing_w(self):
        return jax.ShapeDtypeStruct((N_CHIPS, 2, D_MODEL // 2, D_MODEL), BF16)

    def landing_blob(self, blob16):
        return jax.ShapeDtypeStruct((N_DEV, blob16.shape[1] // 2, D_MODEL), BF16)

    def semaphores(self):
        n_send = len(self.segs) + (2 * N_CHIPS if self.with_blob else 0)
        n_recv = 2 * N_CHIPS + (N_DEV if self.with_blob else 0)
        return [pltpu.SemaphoreType.DMA((n_send,)), pltpu.SemaphoreType.DMA((n_recv,))]

    def _copies(self, dw_ref, rxw_ref, blob_ref, rxb_ref, send_sems, recv_sems):
        x, y, c = lax.axis_index("x"), lax.axis_index("y"), lax.axis_index("c")
        chip = 2 * x + y

        def relation(kx, ky, h):
            return (x ^ kx) * 4 + (y ^ ky) * 2 + (c ^ h)

        def copy(src, dst, send_k, recv_k, to):
            return functools.partial(pltpu.make_async_remote_copy, src_ref=src, dst_ref=dst,
                                     send_sem=send_sems.at[send_k], recv_sem=recv_sems.at[recv_k],
                                     device_id=to, device_id_type=MESH)

        sends, recvs = [], []
        for i, s in enumerate(self.segs):
            kx, ky = (s // 2) >> 1, (s // 2) & 1
            r = (x ^ kx) * 2 + (y ^ ky)
            sends.append((r != 0, copy(dw_ref.at[i], rxw_ref.at[r, s % 2], i, 2 * r + s % 2, (kx, ky, c))))
        for j in range(2):
            mine = [s // 2 for s in self.segs if s % 2 == j]
            if mine:
                cond = functools.reduce(lambda a, b: a | b, [chip == k for k in mine])
                for r in range(1, N_CHIPS):
                    slot = rxw_ref.at[r, j]
                    recvs.append((cond, copy(slot, slot, 0, 2 * r + j, (x, y, c))))
        if self.with_blob:
            hb = blob_ref.shape[1] // 2
            first_send, first_recv = len(self.segs), 2 * N_CHIPS
            for k in range(N_CHIPS):
                for h in range(2):
                    r = relation(k >> 1, k & 1, h)
                    sends.append((r != 0, copy(blob_ref.at[k, pl.ds(h * hb, hb), :], rxb_ref.at[r],
                                               first_send + 2 * k + h, first_recv + r, (k >> 1, k & 1, h))))
            for r in range(1, N_DEV):
                slot = rxb_ref.at[r]
                recvs.append((None, copy(slot, slot, 0, first_recv + r, (x, y, c))))
        return sends, recvs

    def start(self, *refs):
        sends, _ = self._copies(*refs)
        for cond, make in sends:
            pl.when(cond)(lambda make=make: make().start())

    def wait(self, *refs):
        sends, recvs = self._copies(*refs)
        for cond, make in sends:
            pl.when(cond)(lambda make=make: make().wait_send())
        for cond, make in recvs:
            if cond is None:
                make().wait_recv()
            else:
                pl.when(cond)(lambda make=make: make().wait_recv())


def _sum_landed(own, rx_ref):
    total = own
    for r in range(1, rx_ref.shape[0]):
        total = total + rx_ref[r, 0].astype(F32)
    return total


def _finish_w(dw_rec, dw_mix, rx_rec, rx_mix, place_arr):
    half = D_MODEL // 2
    tm = _tile(half, 256)
    n_rec = len(SEGS_REC)

    def body(place_ref, own_rec_ref, own_mix_ref, rx_rec_ref, rx_mix_ref, out_ref):
        seg = 2 * place_ref[0] + pl.program_id(0)

        @pl.when(seg < n_rec)
        def _():
            out_ref[0] = _sum_landed(own_rec_ref[0], rx_rec_ref)

        @pl.when(seg >= n_rec)
        def _():
            out_ref[0] = _sum_landed(own_mix_ref[0], rx_mix_ref)

    def own_spec(first, count):
        def index(j, i, place_ref):
            seg = 2 * place_ref[0] + j
            return (jnp.clip(seg - first, 0, count - 1), i, 0)
        return pl.BlockSpec((1, tm, D_MODEL), index)

    rx_spec = pl.BlockSpec((N_CHIPS, 1, tm, D_MODEL), lambda j, i, place_ref: (0, j, i, 0))
    return pl.pallas_call(
        body, name="finish_w",
        grid_spec=pltpu.PrefetchScalarGridSpec(
            num_scalar_prefetch=1, grid=(2, half // tm),
            in_specs=[own_spec(0, n_rec), own_spec(n_rec, len(SEGS_MIX)), rx_spec, rx_spec],
            out_specs=pl.BlockSpec((1, tm, D_MODEL), lambda j, i, place_ref: (place_ref[1], i, j))),
        out_shape=jax.ShapeDtypeStruct((2, half, 2 * D_MODEL), F32),
        compiler_params=_params(("arbitrary", "arbitrary")),
    )(place_arr, dw_rec, dw_mix, rx_rec, rx_mix)


def _finish_blob(dblob4, rx_blob, place_arr):
    n, rows, cols = rx_blob.shape
    tm = _tile(rows, 256)

    def body(place_ref, own_ref, rx_ref, out_ref):
        out_ref[0] = _sum_landed(own_ref[0, 0], rx_ref)

    return pl.pallas_call(
        body, name="finish_blob",
        grid_spec=pltpu.PrefetchScalarGridSpec(
            num_scalar_prefetch=1, grid=(rows // tm,),
            in_specs=[pl.BlockSpec((1, 1, tm, cols), lambda i, place_ref: (place_ref[0], place_ref[1], i, 0)),
                      pl.BlockSpec((n, 1, tm, cols), lambda i, place_ref: (0, 0, i, 0))],
            out_specs=pl.BlockSpec((1, tm, cols), lambda i, place_ref: (place_ref[1], i, 0))),
        out_shape=jax.ShapeDtypeStruct((2, rows, cols), F32),
        compiler_params=_params(("arbitrary",)),
    )(place_arr, dblob4.reshape(N_CHIPS, 2, rows, cols), rx_blob.reshape(n, 1, rows, cols))


def _share_finished(fw2, fb2, small):
    def body(w_in_ref, b_in_ref, small_ref, w_ref, b_ref, s_ref, bounce, local_sem, send_sems, recv_sems):
        x, y, c, _ = _place()
        sibling = (x, y, 1 - c)

        def copy(k, src, dst, to):
            return pltpu.make_async_remote_copy(src_ref=src, dst_ref=dst, send_sem=send_sems.at[k],
                                                recv_sem=recv_sems.at[k], device_id=to, device_id_type=MESH)

        sends = [copy(0, w_ref.at[c], w_ref.at[c], sibling), copy(1, b_ref.at[c], b_ref.at[c], sibling)]
        for r in range(1, N_DEV):
            peer = (x ^ ((r >> 2) & 1), y ^ ((r >> 1) & 1), c ^ (r & 1))
            sends.append(copy(1 + r, small_ref, s_ref.at[r], peer))
        for cp in sends:
            cp.start()
        for src, dst in ((small_ref, bounce), (bounce, s_ref.at[0])):
            own = pltpu.make_async_copy(src, dst, local_sem)
            own.start()
            own.wait()
        landed = [w_ref.at[1 - c], b_ref.at[1 - c]] + [s_ref.at[r] for r in range(1, N_DEV)]
        for k, slot in enumerate(landed):
            copy(k, slot, slot, (x, y, c)).wait_recv()
        for cp in sends:
            cp.wait_send()

    same = lambda a: jax.ShapeDtypeStruct(a.shape, a.dtype)
    n_sem = 2 + N_DEV - 1
    return pl.pallas_call(
        body, name="share_finished",
        in_specs=[ANY, ANY, ANY], out_specs=[ANY, ANY, ANY],
        out_shape=[same(fw2), same(fb2), jax.ShapeDtypeStruct((N_DEV,) + small.shape, F32)],
        input_output_aliases={0: 0, 1: 1},
        scratch_shapes=[pltpu.VMEM(small.shape, F32), pltpu.SemaphoreType.DMA,
                        pltpu.SemaphoreType.DMA((n_sem,)), pltpu.SemaphoreType.DMA((n_sem,))],
    )(fw2, fb2, small)


def _sum_small(slots, lb_logits, me_arr):
    def body(me_ref, slots_ref, lbl_ref, out_ref):
        me = me_ref[0]
        total = slots_ref[me]
        for d in range(1, N_DEV):
            total = total + slots_ref[d ^ me]
        out_ref[...] = total
        out_ref[ROW_LOSS:ROW_LOSS + 1, :] = jnp.broadcast_to(
            jnp.sum(total[ROW_LOSS:ROW_LOSS + 1, :], axis=-1, keepdims=True), (1, D_MODEL))
        lb = _lower_bound(lbl_ref[...])
        g0 = total[ROW_LB:ROW_LB + 1, :] * lb * (1.0 - lb)
        out_ref[ROW_LB:ROW_LB + 1, :] = g0
        out_ref[ROW_LB + 1:ROW_LB + 2, :] = -g0

    return pl.pallas_call(
        body, name="sum_small",
        grid_spec=pltpu.PrefetchScalarGridSpec(
            num_scalar_prefetch=1, grid=(1,),
            in_specs=[pl.BlockSpec((N_DEV, SMALL_ROWS, D_MODEL), lambda i, me_ref: (0, 0, 0)),
                      pl.BlockSpec((2, D_MODEL), lambda i, me_ref: (0, 0))],
            out_specs=pl.BlockSpec((SMALL_ROWS, D_MODEL), lambda i, me_ref: (0, 0))),
        out_shape=jax.ShapeDtypeStruct((SMALL_ROWS, D_MODEL), F32),
        compiler_params=_params(("arbitrary",)),
    )(me_arr, slots, lb_logits)


def _adamw(w, g, m, v):
    rows, cols = w.shape
    tm = _tile(rows, 256, mult=8) if rows % 8 == 0 else rows
    c1 = 1.0 / (1.0 - ADAM_B1 ** ADAM_STEP)
    c2 = 1.0 / (1.0 - ADAM_B2 ** ADAM_STEP)

    def body(w_ref, g_ref, m_ref, v_ref, d_ref, nm_ref, nv_ref):
        gt = g_ref[...]
        nm = ADAM_B1 * m_ref[...] + (1.0 - ADAM_B1) * gt
        nv = ADAM_B2 * v_ref[...] + (1.0 - ADAM_B2) * (gt * gt)
        nm_ref[...] = nm
        nv_ref[...] = nv
        d_ref[...] = -ADAM_LR * ((nm * c1) / (jnp.sqrt(nv * c2) + ADAM_EPS) + ADAM_WD * w_ref[...])

    blk = pl.BlockSpec((tm, cols), lambda i: (i, 0))
    sds = jax.ShapeDtypeStruct((rows, cols), F32)
    return pl.pallas_call(
        body, name="adamw",
        grid=(rows // tm,), in_specs=[blk] * 4, out_specs=[blk] * 3, out_shape=[sds] * 3,
        compiler_params=_params(("arbitrary",)),
    )(w, g, m, v)


def kernel(x, meta_tokens, norm_w, w_in, b_in, lb_logits, hg_norm_w, pool_w, pool_scale, w_down_hg, w_down_pool, w_out, final_norm_w, loss_target, m_meta_tokens, m_norm_w, m_w_in, m_b_in, m_lb_logits, m_hg_norm_w, m_pool_w, m_pool_scale, m_w_down_hg, m_w_down_pool, m_w_out, m_final_norm_w, v_meta_tokens, v_norm_w, v_w_in, v_b_in, v_lb_logits, v_hg_norm_w, v_pool_w, v_pool_scale, v_w_down_hg, v_w_down_pool, v_w_out, v_final_norm_w):
    seq = x.shape[1]
    xi, yi, ci = lax.axis_index("x"), lax.axis_index("y"), lax.axis_index("c")
    chip = 2 * xi + yi
    place_arr = jnp.stack([chip, ci]).astype(jnp.int32)
    me_arr = jnp.reshape(4 * xi + 2 * yi + ci, (1,)).astype(jnp.int32)
    q = D_MODEL // N_CHIPS

    def blob_of(wdh, wdp, wo, pw):
        return jnp.concatenate([wdh[0], wdp[0], wo[0], pw[0].reshape(-1, D_MODEL)], axis=0)

    def in_every_slot(a):
        return jnp.broadcast_to(a[None], (N_CHIPS,) + a.shape)

    meta4 = _gather_meta(in_every_slot(meta_tokens))
    meta_full = meta4.transpose(1, 0, 2).reshape(N_META, D_MODEL)
    w4 = in_every_slot(w_in[0].astype(BF16))
    blob4 = in_every_slot(blob_of(w_down_hg, w_down_pool, w_out, pool_w).astype(BF16))
    seg_order = jnp.stack([2 * (chip ^ rel) + t for rel in (0, 2, 1, 3) for t in (0, 1)]).astype(jnp.int32)

    head = jnp.concatenate([jnp.zeros((PAD_ROWS, D_MODEL), F32), meta_full], axis=0)
    fw2 = final_norm_w.reshape(1, D_MODEL)
    d_tokens, w_parts, blob_parts, small = _local_step(
        x[0], head, loss_target[0], w4, blob4, seg_order, norm_w, b_in, lb_logits, hg_norm_w, pool_scale, fw2)
    grad_x = d_tokens[None]

    fin_w = _finish_w(*w_parts, place_arr)
    fin_b = _finish_blob(*blob_parts, place_arr)
    gw2, gb2, slots = _share_finished(fin_w, fin_b, small)
    tot = _sum_small(slots, lb_logits, me_arr)
    g_w_in = gw2.reshape(D_MODEL, 2 * D_MODEL)
    g_blob = gb2.reshape(-1, D_MODEL)

    d_win, nm_win, nv_win = _adamw(w_in[0], g_w_in, m_w_in[0], v_w_in[0])
    d_blob, nm_blob, nv_blob = _adamw(
        blob_of(w_down_hg, w_down_pool, w_out, pool_w), g_blob,
        blob_of(m_w_down_hg, m_w_down_pool, m_w_out, m_pool_w),
        blob_of(v_w_down_hg, v_w_down_pool, v_w_out, v_pool_w))
    g_meta = lax.dynamic_slice_in_dim(tot[ROW_META:ROW_META + N_META], chip * q, q, axis=1)
    d_meta, nm_meta, nv_meta = _adamw(meta_tokens, g_meta, m_meta_tokens, v_meta_tokens)

    def rows_of(nw, bi, lbl, hg, ps, fw):
        return jnp.concatenate([nw, bi.reshape(N_SEG, D_MODEL), lbl, hg, ps, fw.reshape(1, D_MODEL),
                                jnp.zeros((2, D_MODEL), F32)], axis=0)

    g_rows = jnp.concatenate([tot[ROW_NORM_W:ROW_FINAL_W + 1], jnp.zeros((2, D_MODEL), F32)], axis=0)
    d_rows, nm_rows, nv_rows = _adamw(
        rows_of(norm_w, b_in, lb_logits, hg_norm_w, pool_scale, final_norm_w), g_rows,
        rows_of(m_norm_w, m_b_in, m_lb_logits, m_hg_norm_w, m_pool_scale, m_final_norm_w),
        rows_of(v_norm_w, v_b_in, v_lb_logits, v_hg_norm_w, v_pool_scale, v_final_norm_w))

    def unblob(b):
        return (b[0:q][None], b[q:2 * q][None], b[2 * q:3 * q][None], b[3 * q:].reshape(pool_w.shape))

    def unrows(r):
        o = ROW_NORM_W
        return dict(norm_w=r[ROW_NORM_W - o:ROW_B_IN - o], b_in=r[ROW_B_IN - o:ROW_LB - o].reshape(1, -1),
                    lb_logits=r[ROW_LB - o:ROW_HG_W - o], hg_norm_w=r[ROW_HG_W - o:ROW_POOL_SCALE - o],
                    pool_scale=r[ROW_POOL_SCALE - o:ROW_FINAL_W - o], final_norm_w=r[ROW_FINAL_W - o])

    def leaves(meta_part, rows_part, win_part, blob_part):
        r = unrows(rows_part)
        wdh, wdp, wo, pw = unblob(blob_part)
        return [meta_part, r["norm_w"], win_part[None], r["b_in"], r["lb_logits"], r["hg_norm_w"], pw,
                r["pool_scale"], wdh, wdp, wo, r["final_norm_w"]]

    loss = tot[ROW_LOSS, 0]
    return (loss, grad_x,
            *leaves(g_meta, g_rows, g_w_in, g_blob),
            *leaves(d_meta, d_rows, d_win, d_blob),
            *leaves(nm_meta, nm_rows, nm_win, nm_blob),
            *leaves(nv_meta, nv_rows, nv_win, nv_blob))
```

```python
import functools

import numpy as np
import jax
import jax.numpy as jnp
from jax import lax
from jax.experimental import pallas as pl
from jax.experimental.pallas import tpu as pltpu

F32 = jnp.float32
BF16 = jnp.bfloat16

D_MODEL = 1024
N_SEG = 8
N_HEADS = 8
HEAD_DIM = 128
CHUNK = 64
N_META = 16
PAD_ROWS = CHUNK - N_META
FIRST_TOKEN_ROW = CHUNK
LEVELS = (32, 16, 8, 4, 2, 1)
N_EXP = 2 + len(LEVELS)
POOL_WINDOWS = (2, 4, 8, 16)
POOL_GDIM = D_MODEL // len(POOL_WINDOWS)
HALO = 16
LOCAL_UNROLL = 13
BACKWARD_UNROLL = 13
EPS = 1e-6
N_CHIPS = 4
N_DEV = 8
SEGS_REC = (0, 1, 2)
SEGS_MIX = (3, 4, 5, 6, 7)

ADAM_LR = 0.001
ADAM_B1 = 0.9
ADAM_B2 = 0.999
ADAM_EPS = 1e-08
ADAM_WD = 0.01
ADAM_STEP = 10

VMEM_LIMIT_BYTES = 56 * 1024 * 1024

ROW_LOSS = 0
ROW_META = 1
ROW_NORM_W = ROW_META + N_META
ROW_B_IN = ROW_NORM_W + 1
ROW_LB = ROW_B_IN + N_SEG
ROW_HG_W = ROW_LB + 2
ROW_POOL_SCALE = ROW_HG_W + 1
ROW_FINAL_W = ROW_POOL_SCALE + 1
SMALL_ROWS = 32


def _tile(total, cap, mult=16):
    best = None
    for t in range(mult, min(total, cap) + 1, mult):
        if total % t == 0:
            best = t
    assert best is not None, (total, cap, mult)
    return best


def _token_window(tm, tile_of):
    def index(*grid):
        return (pl.multiple_of(jnp.maximum(tile_of(*grid) * tm - FIRST_TOKEN_ROW, 0), HALO), 0)
    return pl.BlockSpec((pl.Element(tm), pl.Element(D_MODEL)), index)


def _padded_tile(window, head, tile):
    first = jnp.concatenate([head, pltpu.roll(window, FIRST_TOKEN_ROW, 0)[FIRST_TOKEN_ROW:]], axis=0)
    return jnp.where(tile == 0, first, window)


def _params(sem=None):
    return pltpu.CompilerParams(dimension_semantics=sem, vmem_limit_bytes=VMEM_LIMIT_BYTES)


def _dot(a, b):
    return jnp.dot(a, b, preferred_element_type=F32)


def _dot_nt(a, b):
    return lax.dot_general(a, b, (((1,), (1,)), ((), ())), preferred_element_type=F32)


def _dot_tn(a, b):
    return lax.dot_general(a, b, (((0,), (0,)), ((), ())), preferred_element_type=F32)


def _sigmoid_pair(x):
    t = jnp.exp(-jnp.abs(x))
    r = 1.0 / (1.0 + t)
    pos = x >= 0
    return jnp.where(pos, r, t * r), jnp.where(pos, t * r, r)


def _exponent_matrix():
    t = np.arange(CHUNK)[:, None]
    j = np.arange(CHUNK)[None, :]
    blocks = [j <= t, j > t]
    for m in LEVELS:
        rho = (t // (2 * m)) * (2 * m) + m
        upper = (t >= rho) & (j > rho) & (j <= t)
        lower = (t < rho) & (j > t) & (j <= rho)
        blocks.append(upper | lower)
    return np.concatenate(blocks, axis=0).astype(np.float32)


def _pair_masks():
    t = np.arange(CHUNK)[:, None]
    s = np.arange(CHUNK)[None, :]
    masks = [t == s]
    for m in LEVELS:
        same = (t // (2 * m)) == (s // (2 * m))
        masks.append(same & ((t % (2 * m)) >= m) & ((s % (2 * m)) < m))
    return np.stack(masks).astype(np.float32)


LEVEL_PAIRS = ((0, 1), (2, 3), (4, 5), (6, None))


def _paired_masks():
    m = _pair_masks()
    zero = np.zeros_like(m[0])
    return np.stack([np.concatenate([m[a], zero if b is None else m[b]], axis=1) for a, b in LEVEL_PAIRS])


def _lower_bound(lbl):
    return 1.0 / (1.0 + jnp.exp(lbl[1:2, :] - lbl[0:1, :]))


def _in_proj(tokens, head, norm_w, w4, b_in, seg_order, rows):
    tm = _tile(rows, 1040)
    nt = rows // tm
    gather = _ShardGather(w4.shape[1])

    def body(order_ref, z_ref, head_ref, nw_ref, b_ref, w_in_ref, h_ref, p_ref, w4_ref,
             h_all, w_buf, w_sem, send_sems, recv_sems):
        kk, i = pl.program_id(0), pl.program_id(1)

        @pl.when((kk == 0) & (i == 0))
        def _():
            gather.start(w4_ref, send_sems, recv_sems, which=(0, 1))

        @pl.when((kk == 2) & (i == 0))
        def _():
            gather.start_diagonal_after_neighbours(w4_ref, send_sems, recv_sems)

        @pl.when(kk == 0)
        def _():
            zt = _padded_tile(z_ref[...], head_ref[...], i)
            rstd = lax.rsqrt(jnp.mean(zt * zt, axis=-1, keepdims=True) + EPS)
            h = (zt * rstd * nw_ref[...]).astype(BF16)
            h_all[pl.ds(pl.multiple_of(i * tm, 16), tm), :] = h
            h_ref[...] = h

        @pl.when((kk == 2) & (i == 0))
        def _():
            gather.pass_on(0, w4_ref, send_sems, recv_sems)
            gather.pass_on(1, w4_ref, send_sems, recv_sems)
            gather.await_sibling(0, w4_ref, send_sems, recv_sems)

        @pl.when((kk == 4) & (i == 0))
        def _():
            gather.await_sibling(1, w4_ref, send_sems, recv_sems)

        @pl.when((kk == 5) & (i == 0))
        def _():
            gather.pass_on(2, w4_ref, send_sems, recv_sems)

        @pl.when((kk == 6) & (i == 0))
        def _():
            gather.await_sibling(2, w4_ref, send_sems, recv_sems)

        def weights(which):
            seg = order_ref[2 * (kk // 2) + which]
            return pltpu.make_async_copy(
                w4_ref.at[seg // 2, :, pl.ds(pl.multiple_of((seg % 2) * D_MODEL, D_MODEL), D_MODEL)],
                w_buf.at[which], w_sem.at[which])

        @pl.when((i == 0) & (kk % 2 == 0))
        def _():
            weights(0).start()
            weights(1).start()
            weights(0).wait()

        @pl.when((i == 0) & (kk % 2 == 1))
        def _():
            weights(1).wait()

        p_ref[0] = _dot(h_all[pl.ds(pl.multiple_of(i * tm, 16), tm), :], w_buf[kk % 2]) + b_ref[...]

        @pl.when((kk == N_SEG - 1) & (i == nt - 1))
        def _():
            gather.finish(w4_ref, send_sems, recv_sems, which=(2,))

    first_pass = lambda kk, i, order_ref: (jnp.where(kk == 0, i, nt - 1), 0)
    return pl.pallas_call(
        body, name="in_proj",
        grid_spec=pltpu.PrefetchScalarGridSpec(
            num_scalar_prefetch=1, grid=(N_SEG, nt),
            in_specs=[
                _token_window(tm, lambda kk, i, order_ref: jnp.where(kk == 0, i, nt - 1)),
                pl.BlockSpec((FIRST_TOKEN_ROW, D_MODEL), lambda kk, i, order_ref: (0, 0)),
                pl.BlockSpec((1, D_MODEL), lambda kk, i, order_ref: (0, 0)),
                pl.BlockSpec((1, D_MODEL), lambda kk, i, order_ref: (0, order_ref[kk])),
                ANY,
            ],
            out_specs=[
                pl.BlockSpec((tm, D_MODEL), first_pass),
                pl.BlockSpec((1, tm, D_MODEL), lambda kk, i, order_ref: (order_ref[kk], i, 0)),
                ANY,
            ],
            scratch_shapes=[
                pltpu.VMEM((rows, D_MODEL), BF16),
                pltpu.VMEM((2, D_MODEL, D_MODEL), BF16),
                pltpu.SemaphoreType.DMA((2,)),
            ] + gather.semaphores()),
        out_shape=[
            jax.ShapeDtypeStruct((rows, D_MODEL), BF16),
            jax.ShapeDtypeStruct((N_SEG, rows, D_MODEL), F32),
            jax.ShapeDtypeStruct(w4.shape, w4.dtype),
        ],
        input_output_aliases={5: 2},
        compiler_params=_params(("arbitrary", "arbitrary")),
    )(seg_order, tokens, head, norm_w, b_in, w4)


def _hgrn_forward(p3, lb_logits, wexp2, masks2, blob4, rows):
    n_chunks = rows // CHUNK
    cpb = _tile(n_chunks, 13, mult=1)
    rb_rows = cpb * CHUNK
    n_rb = n_chunks // cpb
    lanes = cpb * HEAD_DIM
    gather = _ShardGather(blob4.shape[1])

    def body(q_ref, fz_ref, v_ref, lbl_ref, wexp_ref, mask_ref, b_in_ref, o_ref, s_ref, e16_ref, a2_ref, b4_ref,
             st_ref, e_ref, u_ref, q_s, kk_s, v_s, send_sems, recv_sems):
        rb = pl.program_id(1)

        @pl.when((pl.program_id(0) == 0) & (rb == 0))
        def _():
            gather.start(b4_ref, send_sems, recv_sems)

        @pl.when(rb == 0)
        def _():
            st_ref[...] = jnp.zeros_like(st_ref)

        lb = _lower_bound(lbl_ref[...])
        row = rb * rb_rows + lax.broadcasted_iota(jnp.int32, (rb_rows, 1), 0)
        valid = row >= PAD_ROWS
        sg, sn = _sigmoid_pair(fz_ref[0])
        g = jnp.where(valid, jnp.log(lb + (1.0 - lb) * sg), 0.0)
        kk_s[...] = jnp.where(valid, (1.0 - lb) * sn, 0.0)
        q_s[...] = jnp.where(valid, q_ref[0], 0.0)
        v_s[...] = jnp.where(valid, v_ref[0], 0.0).astype(BF16)
        hi = g.astype(BF16)
        mid = (g - hi.astype(F32)).astype(BF16)
        g2 = jnp.concatenate(
            [jnp.concatenate([hi[b * CHUNK:(b + 1) * CHUNK], mid[b * CHUNK:(b + 1) * CHUNK]], axis=0)
             for b in range(cpb)], axis=1)
        e_ref[...] = jnp.exp(_dot(wexp_ref[...], g2))
        e16_ref[0, 0] = e_ref[...].astype(BF16)

        def contribution(b, carry):
            r0 = pl.multiple_of(b * CHUNK, CHUNK)
            l0 = pl.multiple_of(b * HEAD_DIM, HEAD_DIM)
            kc16 = (kk_s[pl.ds(r0, CHUNK), :] * e_ref[CHUNK:2 * CHUNK, pl.ds(l0, HEAD_DIM)]).astype(BF16)
            u_ref[b] = _dot_tn(v_s[pl.ds(r0, CHUNK), :], kc16)
            return carry

        lax.fori_loop(0, cpb, contribution, 0, unroll=LOCAL_UNROLL)

        def recur(b, st):
            l0 = pl.multiple_of(b * HEAD_DIM, HEAD_DIM)
            s_ref[0, b] = st
            return st * e_ref[CHUNK - 1:CHUNK, pl.ds(l0, HEAD_DIM)] + u_ref[b]

        st_ref[...] = lax.fori_loop(0, cpb, recur, st_ref[...])

        zeros16 = jnp.zeros((CHUNK, HEAD_DIM), BF16)

        def local(b, carry):
            r0 = pl.multiple_of(b * CHUNK, CHUNK)
            l0 = pl.multiple_of(b * HEAD_DIM, HEAD_DIM)
            q = q_s[pl.ds(r0, CHUNK), :]
            kk = kk_s[pl.ds(r0, CHUNK), :]
            v16 = v_s[pl.ds(r0, CHUNK), :]

            def scaled(entry):
                if entry == 0:
                    return q.astype(BF16), kk.astype(BF16)
                e_m = e_ref[(1 + entry) * CHUNK:(2 + entry) * CHUNK, pl.ds(l0, HEAD_DIM)]
                return (q * e_m).astype(BF16), (kk * e_m).astype(BF16)

            a2 = jnp.zeros((CHUNK, 2 * CHUNK), F32)
            for p, (ea, eb) in enumerate(LEVEL_PAIRS):
                qa, ka = scaled(ea)
                if eb is None:
                    prod = _dot_nt(qa, jnp.concatenate([ka, zeros16], axis=0))
                else:
                    qb_, kb_ = scaled(eb)
                    rhs = jnp.concatenate([jnp.concatenate([ka, zeros16], axis=1),
                                           jnp.concatenate([zeros16, kb_], axis=1)], axis=0)
                    prod = _dot_nt(jnp.concatenate([qa, qb_], axis=1), rhs)
                a2 = a2 + mask_ref[p] * prod
            a2_16 = a2.astype(BF16)
            a2_ref[pl.ds(r0, CHUNK), :] = a2_16
            qb16 = (q * e_ref[0:CHUNK, pl.ds(l0, HEAD_DIM)]).astype(BF16)
            o_ref[pl.ds(r0, CHUNK), :] = (_dot(a2_16, jnp.concatenate([v16, v16], axis=0))
                                          + _dot_nt(qb16, s_ref[0, b].astype(BF16)))
            return carry

        lax.fori_loop(0, cpb, local, 0, unroll=LOCAL_UNROLL)

        @pl.when((pl.program_id(0) == N_HEADS // 2) & (rb == 0))
        def _():
            for j in range(N_CHIPS - 1):
                gather.pass_on(j, b4_ref, send_sems, recv_sems)

        @pl.when((pl.program_id(0) == N_HEADS - 1) & (rb == n_rb - 1))
        def _():
            for j in range(N_CHIPS - 1):
                gather.await_sibling(j, b4_ref, send_sems, recv_sems)
            gather.finish(b4_ref, send_sems, recv_sems)

    head_block = lambda seg: pl.BlockSpec((1, rb_rows, HEAD_DIM), lambda h, r: (seg, r, h))
    return pl.pallas_call(
        body, name="hgrn_forward",
        grid=(N_HEADS, n_rb),
        in_specs=[
            head_block(0), head_block(1), head_block(2),
            pl.BlockSpec((2, HEAD_DIM), lambda h, r: (0, h)),
            pl.BlockSpec((N_EXP * CHUNK, 2 * CHUNK), lambda h, r: (0, 0)),
            pl.BlockSpec((len(LEVEL_PAIRS), CHUNK, 2 * CHUNK), lambda h, r: (0, 0, 0)),
            ANY,
        ],
        out_specs=[
            pl.BlockSpec((rb_rows, HEAD_DIM), lambda h, r: (r, h)),
            pl.BlockSpec((1, cpb, HEAD_DIM, HEAD_DIM), lambda h, r: (h, r, 0, 0)),
            pl.BlockSpec((1, 1, N_EXP * CHUNK, lanes), lambda h, r: (h, r, 0, 0)),
            pl.BlockSpec((rb_rows, HEAD_DIM), lambda h, r: (r, h)),
            ANY,
        ],
        out_shape=[
            jax.ShapeDtypeStruct((rows, D_MODEL), F32),
            jax.ShapeDtypeStruct((N_HEADS, n_chunks, HEAD_DIM, HEAD_DIM), F32),
            jax.ShapeDtypeStruct((N_HEADS, n_rb, N_EXP * CHUNK, lanes), BF16),
            jax.ShapeDtypeStruct((rows, D_MODEL), BF16),
            jax.ShapeDtypeStruct(blob4.shape, blob4.dtype),
        ],
        input_output_aliases={6: 4},
        scratch_shapes=[
            pltpu.VMEM((HEAD_DIM, HEAD_DIM), F32),
            pltpu.VMEM((N_EXP * CHUNK, lanes), F32),
            pltpu.VMEM((cpb, HEAD_DIM, HEAD_DIM), F32),
            pltpu.VMEM((rb_rows, HEAD_DIM), F32),
            pltpu.VMEM((rb_rows, HEAD_DIM), F32),
            pltpu.VMEM((rb_rows, HEAD_DIM), BF16),
        ] + gather.semaphores(),
        compiler_params=_params(("arbitrary", "arbitrary")),
    )(p3, p3, p3, lb_logits, wexp2, masks2, blob4)


def _hgrn_backward(p3, d_o, states, e16, a2, lb_logits, wexp_t, masks2, dw16, blob16, rows):
    n_chunks = rows // CHUNK
    cpb = _tile(n_chunks, 13, mult=1)
    rb_rows = cpb * CHUNK
    n_rb = n_chunks // cpb
    lanes = cpb * HEAD_DIM
    exchange = _GradExchange(SEGS_MIX, with_blob=True)

    def body(q_ref, fz_ref, v_ref, do_ref, s_ref, e_ref, a2_ref, lbl_ref, wexpt_ref, mask_ref, dw_ref, blob_ref,
             dp_ref, dlb_ref, rxw_ref, rxb_ref,
             dst_ref, g_ref, dsn_ref, q_s, kk_s, v_s, do_s, dq_s, dkk_s, dg_s, dx_s, da2_s, send_sems, recv_sems):
        step = pl.program_id(1)
        rb = n_rb - 1 - step

        @pl.when((pl.program_id(0) == 0) & (step == 0))
        def _():
            exchange.start(dw_ref, rxw_ref, blob_ref, rxb_ref, send_sems, recv_sems)

        @pl.when(step == 0)
        def _():
            dst_ref[...] = jnp.zeros_like(dst_ref)
            dlb_ref[...] = jnp.zeros_like(dlb_ref)

        lb = _lower_bound(lbl_ref[...])
        row = rb * rb_rows + lax.broadcasted_iota(jnp.int32, (rb_rows, 1), 0)
        valid = row >= PAD_ROWS
        sg, sn = _sigmoid_pair(fz_ref[0])
        f = lb + (1.0 - lb) * sg
        g = jnp.where(valid, jnp.log(f), 0.0)
        kk_s[...] = jnp.where(valid, (1.0 - lb) * sn, 0.0)
        q_s[...] = jnp.where(valid, q_ref[0], 0.0)
        v_s[...] = jnp.where(valid, v_ref[0], 0.0).astype(BF16)
        do_s[...] = do_ref[...].astype(BF16)
        e_last_all = jnp.exp(jnp.concatenate(
            [jnp.sum(g[b * CHUNK:(b + 1) * CHUNK], axis=0, keepdims=True) for b in range(cpb)], axis=0))
        last_row = lax.broadcasted_iota(jnp.int32, (CHUNK, 1), 0) == CHUNK - 1
        zeros16 = jnp.zeros((CHUNK, HEAD_DIM), BF16)

        def factor(block, l0):
            return e_ref[0, 0, block * CHUNK:(block + 1) * CHUNK, pl.ds(l0, HEAD_DIM)].astype(F32)

        def contribution(b, carry):
            r0 = pl.multiple_of(b * CHUNK, CHUNK)
            l0 = pl.multiple_of(b * HEAD_DIM, HEAD_DIM)
            qb16 = (q_s[pl.ds(r0, CHUNK), :] * factor(0, l0)).astype(BF16)
            g_ref[b] = _dot_tn(do_s[pl.ds(r0, CHUNK), :], qb16)
            return carry

        lax.fori_loop(0, cpb, contribution, 0, unroll=LOCAL_UNROLL)

        cur = dst_ref[...]
        for b in reversed(range(cpb)):
            dsn_ref[b] = cur
            cur = cur * e_last_all[b:b + 1, :] + g_ref[b]
        dst_ref[...] = cur

        def through_state(b, carry):
            r0 = pl.multiple_of(b * CHUNK, CHUNK)
            l0 = pl.multiple_of(b * HEAD_DIM, HEAD_DIM)
            v16 = v_s[pl.ds(r0, CHUNK), :]
            do16 = do_s[pl.ds(r0, CHUNK), :]
            st = s_ref[0, b]
            dsn = dsn_ref[b]
            dsn16 = dsn.astype(BF16)
            e_b, e_c = factor(0, l0), factor(1, l0)
            qb, kc = q_s[pl.ds(r0, CHUNK), :] * e_b, kk_s[pl.ds(r0, CHUNK), :] * e_c

            t = _dot_tn(a2_ref[pl.ds(r0, CHUNK), :], do16)
            dv = t[0:CHUNK] + t[CHUNK:2 * CHUNK] + _dot_nt(kc.astype(BF16), dsn16)
            dp_ref[2, pl.ds(r0, CHUNK), :] = dv.astype(BF16)
            da2_s[pl.ds(r0, CHUNK), :] = _dot_nt(do16, jnp.concatenate([v16, v16], axis=0))
            dqb = _dot(do16, st.astype(BF16))
            dkc = _dot(v16, dsn16)
            de = jnp.sum(dsn * st, axis=0, keepdims=True) * e_b[CHUNK - 1:CHUNK, :]
            dq_s[pl.ds(r0, CHUNK), :] = e_b * dqb
            dkk_s[pl.ds(r0, CHUNK), :] = e_c * dkc
            dx_s[0:CHUNK, pl.ds(l0, HEAD_DIM)] = (qb * dqb + jnp.where(last_row, de, 0.0)).astype(BF16)
            dx_s[CHUNK:2 * CHUNK, pl.ds(l0, HEAD_DIM)] = (kc * dkc).astype(BF16)
            return carry

        lax.fori_loop(0, cpb, through_state, 0, unroll=BACKWARD_UNROLL)

        def local(b, carry):
            r0 = pl.multiple_of(b * CHUNK, CHUNK)
            l0 = pl.multiple_of(b * HEAD_DIM, HEAD_DIM)
            q = q_s[pl.ds(r0, CHUNK), :]
            kk = kk_s[pl.ds(r0, CHUNK), :]
            da2 = da2_s[pl.ds(r0, CHUNK), :]
            dq = dq_s[pl.ds(r0, CHUNK), :]
            dkk = dkk_s[pl.ds(r0, CHUNK), :]

            def scaled(entry):
                if entry == 0:
                    return q, kk, None
                e_m = factor(1 + entry, l0)
                return q * e_m, kk * e_m, e_m

            for p, (ea, eb) in enumerate(LEVEL_PAIRS):
                dm = mask_ref[p] * da2
                dm_t = dm.T.astype(BF16)
                qa, ka, e_a = scaled(ea)
                if eb is None:
                    rhs_k = jnp.concatenate([jnp.concatenate([ka.astype(BF16), zeros16], axis=1),
                                             jnp.concatenate([zeros16, zeros16], axis=1)], axis=0)
                else:
                    qb_, kb_, e_bb = scaled(eb)
                    rhs_k = jnp.concatenate([jnp.concatenate([ka.astype(BF16), zeros16], axis=1),
                                             jnp.concatenate([zeros16, kb_.astype(BF16)], axis=1)], axis=0)
                dq2 = _dot(dm.astype(BF16), rhs_k)
                parts = [(ea, qa, ka, e_a, dq2[:, :HEAD_DIM], _dot(dm_t[0:CHUNK], qa.astype(BF16)))]
                if eb is not None:
                    parts.append((eb, qb_, kb_, e_bb, dq2[:, HEAD_DIM:],
                                  _dot(dm_t[CHUNK:2 * CHUNK], qb_.astype(BF16))))
                for entry, q_m, k_m, e_m, dq_m, dk_m in parts:
                    if entry == 0:
                        dq = dq + dq_m
                        dkk = dkk + dk_m
                    else:
                        dq = dq + e_m * dq_m
                        dkk = dkk + e_m * dk_m
                        dx_s[(1 + entry) * CHUNK:(2 + entry) * CHUNK, pl.ds(l0, HEAD_DIM)] = (
                            q_m * dq_m + k_m * dk_m).astype(BF16)
            dq_s[pl.ds(r0, CHUNK), :] = dq
            dkk_s[pl.ds(r0, CHUNK), :] = dkk
            return carry

        lax.fori_loop(0, cpb, local, 0, unroll=BACKWARD_UNROLL)

        dg_all = _dot(wexpt_ref[...], dx_s[...])
        for b in range(cpb):
            dg_s[b * CHUNK:(b + 1) * CHUNK, :] = dg_all[:, b * HEAD_DIM:(b + 1) * HEAD_DIM]
        t = jnp.where(valid, dg_s[...] / f - dkk_s[...], 0.0)
        dlb_ref[...] += jnp.sum(sn * t, axis=0, keepdims=True)
        dp_ref[0] = jnp.where(valid, dq_s[...], 0.0).astype(BF16)
        dp_ref[1] = ((1.0 - lb) * sg * sn * t).astype(BF16)

        @pl.when((pl.program_id(0) == N_HEADS - 1) & (step == n_rb - 1))
        def _():
            exchange.wait(dw_ref, rxw_ref, blob_ref, rxb_ref, send_sems, recv_sems)

    head_block = lambda seg: pl.BlockSpec((1, rb_rows, HEAD_DIM), lambda h, s: (seg, n_rb - 1 - s, h))
    row_block = pl.BlockSpec((rb_rows, HEAD_DIM), lambda h, s: (n_rb - 1 - s, h))
    return pl.pallas_call(
        body, name="hgrn_backward",
        grid=(N_HEADS, n_rb),
        in_specs=[
            head_block(0), head_block(1), head_block(2),
            row_block,
            pl.BlockSpec((1, cpb, HEAD_DIM, HEAD_DIM), lambda h, s: (h, n_rb - 1 - s, 0, 0)),
            pl.BlockSpec((1, 1, N_EXP * CHUNK, lanes), lambda h, s: (h, n_rb - 1 - s, 0, 0)),
            row_block,
            pl.BlockSpec((2, HEAD_DIM), lambda h, s: (0, h)),
            pl.BlockSpec((CHUNK, N_EXP * CHUNK), lambda h, s: (0, 0)),
            pl.BlockSpec((len(LEVEL_PAIRS), CHUNK, 2 * CHUNK), lambda h, s: (0, 0, 0)),
            ANY, ANY,
        ],
        out_specs=[
            pl.BlockSpec((3, rb_rows, HEAD_DIM), lambda h, s: (0, n_rb - 1 - s, h)),
            pl.BlockSpec((1, HEAD_DIM), lambda h, s: (0, h)),
            ANY, ANY,
        ],
        out_shape=[
            jax.ShapeDtypeStruct((3, rows, D_MODEL), BF16),
            jax.ShapeDtypeStruct((1, D_MODEL), F32),
            exchange.landing_w(), exchange.landing_blob(blob16),
        ],
        scratch_shapes=[
            pltpu.VMEM((HEAD_DIM, HEAD_DIM), F32),
            pltpu.VMEM((cpb, HEAD_DIM, HEAD_DIM), F32),
            pltpu.VMEM((cpb, HEAD_DIM, HEAD_DIM), F32),
            pltpu.VMEM((rb_rows, HEAD_DIM), F32),
            pltpu.VMEM((rb_rows, HEAD_DIM), F32),
            pltpu.VMEM((rb_rows, HEAD_DIM), BF16),
            pltpu.VMEM((rb_rows, HEAD_DIM), BF16),
            pltpu.VMEM((rb_rows, HEAD_DIM), F32),
            pltpu.VMEM((rb_rows, HEAD_DIM), F32),
            pltpu.VMEM((rb_rows, HEAD_DIM), F32),
            pltpu.VMEM((N_EXP * CHUNK, lanes), BF16),
            pltpu.VMEM((rb_rows, 2 * CHUNK), F32),
        ] + exchange.semaphores(),
        compiler_params=_params(("arbitrary", "arbitrary")),
    )(p3, p3, p3, d_o, states, e16, a2, lb_logits, wexp_t, masks2, dw16, blob16)


def _sigmoid(x):
    return 1.0 / (1.0 + jnp.exp(-x))


def _silu_and_grad(x):
    s = _sigmoid(x)
    return x * s, s * (1.0 + x * (1.0 - s))


def _window_sum(ext, width, forward_looking):
    n = ext.shape[0]
    s = ext
    step = 1
    while step < width:
        s = s + pltpu.roll(s, (n - step) if forward_looking else step, 0)
        step *= 2
    return s


def _mixers(o, p3, tokens, head, tgt, wdh, wdp, wout, poolw, hg_w, pool_scale, final_w, rows):
    tm = _tile(rows, 208)
    nt = rows // tm
    halo_blocks = tm // HALO
    n_grp = len(POOL_WINDOWS)
    q_rows = D_MODEL // N_CHIPS
    blob_rows = 3 * q_rows + n_grp * POOL_GDIM * POOL_GDIM // (N_CHIPS * D_MODEL)

    def body(o_ref, ghg_ref, u_ref, gpl_ref, mhg_ref, mpl_ref, uh_ref, z_ref, t_ref,
             wdh_ref, wdp_ref, wout_ref, pw_ref, hgw_ref, ps_ref, fw_ref, head_ref,
             do_ref, dz2_ref, dp_ref, blob_ref, dpw_ref, small_ref, carry_ref):
        step = pl.program_id(0)
        tile = nt - 1 - step

        def add_to_blob(piece, dw):
            for k in range(N_CHIPS):
                blob_ref[k, piece * q_rows:(piece + 1) * q_rows, :] += dw[k * q_rows:(k + 1) * q_rows]

        @pl.when(step == 0)
        def _():
            blob_ref[...] = jnp.zeros_like(blob_ref)
            dpw_ref[...] = jnp.zeros_like(dpw_ref)
            small_ref[...] = jnp.zeros_like(small_ref)
            carry_ref[...] = jnp.zeros_like(carry_ref)

        row = tile * tm + lax.broadcasted_iota(jnp.int32, (tm, 1), 0)
        real = row >= PAD_ROWS
        pos1 = jnp.maximum(row - PAD_ROWS + 1, 1).astype(F32)

        u = jnp.where(real, u_ref[0], 0.0)
        halo_row = tile * tm - HALO + lax.broadcasted_iota(jnp.int32, (HALO, 1), 0)
        uh = jnp.where(halo_row >= PAD_ROWS, uh_ref[0], 0.0)
        ext = jnp.concatenate([uh, u], axis=0)
        pooled, inv_cnt, mixed = [], [], []
        for g, w in enumerate(POOL_WINDOWS):
            cols = slice(g * POOL_GDIM, (g + 1) * POOL_GDIM)
            inv = 1.0 / jnp.minimum(pos1, float(w))
            ws = _window_sum(ext[:, cols], w, False)[HALO:]
            pg = (ws * inv - u[:, cols]).astype(BF16)
            pooled.append(pg)
            inv_cnt.append(inv)
            mixed.append(_dot(pg, pw_ref[g]))
        mixed = jnp.concatenate(mixed, axis=1)
        gpl = gpl_ref[0]
        sp, dsp = _silu_and_grad(gpl)
        ps = ps_ref[...]
        a_pool = (mixed * ps * sp).astype(BF16)
        y_pool = _dot(a_pool, wdp_ref[...])

        o = o_ref[...]
        o_hat, rstd_h = [], []
        for h in range(N_HEADS):
            oh = o[:, h * HEAD_DIM:(h + 1) * HEAD_DIM]
            r = lax.rsqrt(jnp.mean(oh * oh, axis=-1, keepdims=True) + EPS)
            rstd_h.append(r)
            o_hat.append(oh * r)
        o_hat = jnp.concatenate(o_hat, axis=1)
        hgw = hgw_ref[...]
        o_n = o_hat * hgw
        ghg = ghg_ref[0]
        sh, dsh = _silu_and_grad(ghg)
        a_hg = (o_n * sh).astype(BF16)
        y_hg = _dot(a_hg, wdh_ref[...])

        s_mh = _sigmoid(mhg_ref[0])
        s_mp = _sigmoid(mpl_ref[0])
        merged = (s_mh * y_hg + s_mp * y_pool).astype(BF16)
        z2 = _padded_tile(z_ref[...], head_ref[...], tile) + _dot(merged, wout_ref[...])
        rstd2 = lax.rsqrt(jnp.mean(z2 * z2, axis=-1, keepdims=True) + EPS)
        zh = z2 * rstd2
        fw = fw_ref[...]
        target = _padded_tile(t_ref[...], jnp.zeros((FIRST_TOKEN_ROW, D_MODEL), F32), tile)
        err = jnp.where(row >= FIRST_TOKEN_ROW, zh * fw - target, 0.0)
        small_ref[ROW_LOSS:ROW_LOSS + 1, :] += jnp.sum(err * err, axis=0, keepdims=True) * (0.5 / D_MODEL)
        dy = err * (1.0 / D_MODEL)

        small_ref[ROW_FINAL_W:ROW_FINAL_W + 1, :] += jnp.sum(dy * zh, axis=0, keepdims=True)
        uu = dy * fw
        dz2 = rstd2 * (uu - zh * jnp.mean(uu * zh, axis=-1, keepdims=True))
        dz2_ref[...] = dz2
        dz2_16 = dz2.astype(BF16)
        dmerged = _dot_nt(dz2_16, wout_ref[...])
        add_to_blob(2, _dot_tn(merged, dz2_16))
        dy_hg = (s_mh * dmerged).astype(BF16)
        dy_pool = (s_mp * dmerged).astype(BF16)
        dp_ref[3] = (dmerged * y_hg * s_mh * (1.0 - s_mh)).astype(BF16)
        dp_ref[4] = (dmerged * y_pool * s_mp * (1.0 - s_mp)).astype(BF16)

        da_hg = _dot_nt(dy_hg, wdh_ref[...])
        add_to_blob(0, _dot_tn(a_hg, dy_hg))
        dp_ref[0] = (da_hg * o_n * dsh).astype(BF16)
        do_n = da_hg * sh
        small_ref[ROW_HG_W:ROW_HG_W + 1, :] += jnp.sum(do_n * o_hat, axis=0, keepdims=True)
        d_hat = do_n * hgw
        for h in range(N_HEADS):
            cols = slice(h * HEAD_DIM, (h + 1) * HEAD_DIM)
            dh_, oh_ = d_hat[:, cols], o_hat[:, cols]
            do_ref[:, cols] = rstd_h[h] * (dh_ - oh_ * jnp.mean(dh_ * oh_, axis=-1, keepdims=True))

        da_pool = _dot_nt(dy_pool, wdp_ref[...])
        add_to_blob(1, _dot_tn(a_pool, dy_pool))
        small_ref[ROW_POOL_SCALE:ROW_POOL_SCALE + 1, :] += jnp.sum(da_pool * mixed * sp, axis=0, keepdims=True)
        dp_ref[2] = (da_pool * mixed * ps * dsp).astype(BF16)
        dmixed = (da_pool * ps * sp).astype(BF16)
        carry = carry_ref[...]
        du, new_carry = [], []
        for g, w in enumerate(POOL_WINDOWS):
            cols = slice(g * POOL_GDIM, (g + 1) * POOL_GDIM)
            dmg = dmixed[:, cols]
            dpooled = _dot_nt(dmg, pw_ref[g])
            dpw_ref[g] += _dot_tn(pooled[g], dmg)
            dps = dpooled * inv_cnt[g]
            ext_b = jnp.concatenate([dps, carry[:, cols]], axis=0)
            du.append(_window_sum(ext_b, w, True)[:tm] - dpooled)
            new_carry.append(dps[:HALO])
        dp_ref[1] = jnp.where(real, jnp.concatenate(du, axis=1), 0.0).astype(BF16)
        carry_ref[...] = jnp.concatenate(new_carry, axis=1)

    row_block = pl.BlockSpec((tm, D_MODEL), lambda s: (nt - 1 - s, 0))
    seg_block = lambda seg: pl.BlockSpec((1, tm, D_MODEL), lambda s: (seg, nt - 1 - s, 0))
    whole = pl.BlockSpec(memory_space=pltpu.VMEM)
    return pl.pallas_call(
        body, name="mixers",
        grid=(nt,),
        in_specs=[
            row_block, seg_block(3), seg_block(4), seg_block(5), seg_block(6), seg_block(7),
            pl.BlockSpec((1, HALO, D_MODEL),
                         lambda s: (4, jnp.maximum((nt - 1 - s) * halo_blocks - 1, 0), 0)),
            _token_window(tm, lambda s: nt - 1 - s), _token_window(tm, lambda s: nt - 1 - s),
            whole, whole, whole, whole, whole, whole, whole, whole,
        ],
        out_specs=[
            row_block, row_block,
            pl.BlockSpec((5, tm, D_MODEL), lambda s: (0, nt - 1 - s, 0)),
            whole, whole, whole,
        ],
        out_shape=[
            jax.ShapeDtypeStruct((rows, D_MODEL), F32),
            jax.ShapeDtypeStruct((rows, D_MODEL), F32),
            jax.ShapeDtypeStruct((5, rows, D_MODEL), BF16),
            jax.ShapeDtypeStruct((N_CHIPS, blob_rows, D_MODEL), F32),
            jax.ShapeDtypeStruct((n_grp, POOL_GDIM, POOL_GDIM), F32),
            jax.ShapeDtypeStruct((SMALL_ROWS, D_MODEL), F32),
        ],
        scratch_shapes=[pltpu.VMEM((HALO, D_MODEL), F32)],
        compiler_params=_params(("arbitrary",)),
    )(o, p3, p3, p3, p3, p3, p3, tokens, tgt, wdh, wdp, wout, poolw, hg_w, pool_scale, final_w, head)


def _seg_specs(tm, row_of, seg_of):
    def spec_a(*g):
        k = seg_of(*g)
        return (jnp.minimum(k, 2), jnp.where(k < 3, row_of(*g), 0), 0)

    def spec_b(*g):
        k = seg_of(*g)
        return (jnp.maximum(k - 3, 0), jnp.where(k >= 3, row_of(*g), 0), 0)

    return pl.BlockSpec((1, tm, D_MODEL), spec_a), pl.BlockSpec((1, tm, D_MODEL), spec_b)


def _in_proj_weight_grad(h, dp, rows, name):
    n_seg = dp.shape[0]
    tm = _tile(rows, 1040)
    nt = rows // tm
    half = D_MODEL // 2

    def body(h_ref, dp_ref, part_ref, part16_ref, db_ref, acc_ref, bacc_ref, stage_ref, land_ref,
             send_sems, recv_sems):
        k, i = pl.program_id(0), pl.program_id(1)
        x, y, c = lax.axis_index("x"), lax.axis_index("y"), lax.axis_index("c")

        def to_sibling(seg):
            return pltpu.make_async_remote_copy(
                src_ref=stage_ref.at[seg], dst_ref=land_ref.at[seg], send_sem=send_sems.at[seg],
                recv_sem=recv_sems.at[seg], device_id=(x, y, 1 - c), device_id_type=MESH)

        @pl.when(i == 0)
        def _():
            acc_ref[...] = jnp.zeros_like(acc_ref)
            bacc_ref[...] = jnp.zeros_like(bacc_ref)

        dpt = dp_ref[0]
        acc_ref[...] += _dot_tn(h_ref[...], dpt)
        bacc_ref[...] += jnp.sum(dpt.astype(F32), axis=0, keepdims=True)

        @pl.when(i == nt - 1)
        def _():
            db_ref[0] = bacc_ref[...]
            part_ref[k] = acc_ref[pl.ds(pl.multiple_of(c * half, half), half), :]
            stage_ref[k] = acc_ref[pl.ds(pl.multiple_of((1 - c) * half, half), half), :].astype(BF16)
            to_sibling(k).start()

        @pl.when((k == n_seg - 1) & (i == nt - 1))
        def _():
            for seg in range(n_seg):
                to_sibling(seg).wait_recv()
                total = part_ref[seg] + land_ref[seg].astype(F32)
                part_ref[seg] = total
                part16_ref[seg] = total.astype(BF16)
            for seg in range(n_seg):
                to_sibling(seg).wait_send()

    whole = pl.BlockSpec(memory_space=pltpu.VMEM)
    return pl.pallas_call(
        body, name=name,
        grid=(n_seg, nt),
        in_specs=[pl.BlockSpec((tm, D_MODEL), lambda k, i: (i, 0)),
                  pl.BlockSpec((1, tm, D_MODEL), lambda k, i: (k, i, 0))],
        out_specs=[whole, whole, pl.BlockSpec((1, 1, D_MODEL), lambda k, i: (k, 0, 0))],
        out_shape=[
            jax.ShapeDtypeStruct((n_seg, half, D_MODEL), F32),
            jax.ShapeDtypeStruct((n_seg, half, D_MODEL), BF16),
            jax.ShapeDtypeStruct((n_seg, 1, D_MODEL), F32),
        ],
        scratch_shapes=[
            pltpu.VMEM((D_MODEL, D_MODEL), F32), pltpu.VMEM((1, D_MODEL), F32),
            pltpu.VMEM((n_seg, half, D_MODEL), BF16),
            pltpu.VMEM((n_seg, half, D_MODEL), BF16),
            pltpu.SemaphoreType.DMA((n_seg,)), pltpu.SemaphoreType.DMA((n_seg,)),
        ],
        compiler_params=_params(("arbitrary", "arbitrary")),
    )(h, dp)


def _input_grad(dpa, dpb, w4, tokens, head, dz2, norm_w, dw16, rows):
    tm = _tile(rows, 1040)
    nt = rows // tm
    assert nt >= 2, rows
    exchange = _GradExchange(SEGS_REC, with_blob=False)

    def body(dpa_ref, dpb_ref, w_ref, z_ref, head_ref, dz2_ref, nw_ref, dw_ref, gx_ref, dmeta_ref, dnw_ref, rxw_ref,
             acc_ref, dz_buf, out_sem, send_sems, recv_sems):
        i, k = pl.program_id(0), pl.program_id(1)

        def first_tile_out():
            return pltpu.make_async_copy(dz_buf.at[pl.ds(FIRST_TOKEN_ROW, tm - FIRST_TOKEN_ROW), :],
                                         gx_ref.at[pl.ds(0, tm - FIRST_TOKEN_ROW), :], out_sem)

        def tile_out(tile):
            start = pl.multiple_of(tile * tm - FIRST_TOKEN_ROW, HALO)
            return pltpu.make_async_copy(dz_buf, gx_ref.at[pl.ds(start, tm), :], out_sem)

        @pl.when((i == 0) & (k == 0))
        def _():
            exchange.start(dw_ref, rxw_ref, None, None, send_sems, recv_sems)
            dnw_ref[...] = jnp.zeros_like(dnw_ref)

        @pl.when((i == nt - 1) & (k == N_SEG - 1))
        def _():
            exchange.wait(dw_ref, rxw_ref, None, None, send_sems, recv_sems)

        @pl.when(k == 0)
        def _():
            acc_ref[...] = jnp.zeros_like(acc_ref)

        @pl.when(k < 3)
        def _():
            acc_ref[...] += _dot_nt(dpa_ref[0], w_ref[0])

        @pl.when(k >= 3)
        def _():
            acc_ref[...] += _dot_nt(dpb_ref[0], w_ref[0])

        @pl.when(k == N_SEG - 1)
        def _():
            zt = _padded_tile(z_ref[...], head_ref[...], i)
            rstd = lax.rsqrt(jnp.mean(zt * zt, axis=-1, keepdims=True) + EPS)
            zh = zt * rstd
            dh = acc_ref[...]
            dnw_ref[...] += jnp.sum(dh * zh, axis=0, keepdims=True)
            uu = dh * nw_ref[...]
            dz = dz2_ref[...] + rstd * (uu - zh * jnp.mean(uu * zh, axis=-1, keepdims=True))

            @pl.when(i == 1)
            def _():
                first_tile_out().wait()

            @pl.when(i >= 2)
            def _():
                tile_out(i - 1).wait()

            dz_buf[...] = dz

            @pl.when(i == 0)
            def _():
                dmeta_ref[...] = dz[PAD_ROWS:FIRST_TOKEN_ROW]
                first_tile_out().start()

            @pl.when(i > 0)
            def _():
                tile_out(i).start()

            @pl.when(i == nt - 1)
            def _():
                tile_out(i).wait()

    spec_a, spec_b = _seg_specs(tm, lambda i, k: i, lambda i, k: k)
    last_only = pl.BlockSpec((tm, D_MODEL), lambda i, k: (jnp.where(k == N_SEG - 1, i, 0), 0))
    return pl.pallas_call(
        body, name="input_grad",
        grid=(nt, N_SEG),
        in_specs=[
            spec_a, spec_b,
            pl.BlockSpec((1, D_MODEL, D_MODEL), lambda i, k: (k // 2, 0, k % 2)),
            _token_window(tm, lambda i, k: jnp.where(k == N_SEG - 1, i, 0)),
            pl.BlockSpec((FIRST_TOKEN_ROW, D_MODEL), lambda i, k: (0, 0)),
            last_only,
            pl.BlockSpec((1, D_MODEL), lambda i, k: (0, 0)),
            ANY,
        ],
        out_specs=[
            ANY,
            pl.BlockSpec((N_META, D_MODEL), lambda i, k: (0, 0)),
            pl.BlockSpec((1, D_MODEL), lambda i, k: (0, 0)),
            ANY,
        ],
        out_shape=[
            jax.ShapeDtypeStruct((rows - FIRST_TOKEN_ROW, D_MODEL), F32),
            jax.ShapeDtypeStruct((N_META, D_MODEL), F32),
            jax.ShapeDtypeStruct((1, D_MODEL), F32),
            exchange.landing_w(),
        ],
        scratch_shapes=[pltpu.VMEM((tm, D_MODEL), F32), pltpu.VMEM((tm, D_MODEL), F32),
                        pltpu.SemaphoreType.DMA] + exchange.semaphores(),
        compiler_params=_params(("arbitrary", "arbitrary")),
    )(dpa, dpb, w4, tokens, head, dz2, norm_w, dw16)


def _local_step(tokens, head, tgt, w4, blob4, seg_order, norm_w, b_in, lb_logits, hg_w, pool_scale, final_w):
    rows = FIRST_TOKEN_ROW + tokens.shape[0]
    q = D_MODEL // N_CHIPS
    n_grp = len(POOL_WINDOWS)
    pg = POOL_GDIM // N_CHIPS

    wexp2 = jnp.asarray(np.tile(_exponent_matrix(), (1, 2)), BF16)
    wexp_t = jnp.asarray(_exponent_matrix().T, BF16)
    masks2 = jnp.asarray(_paired_masks(), F32)

    h, p3, w4 = _in_proj(tokens, head, norm_w, w4, b_in, seg_order, rows)
    o, states, e16, a2, blob4 = _hgrn_forward(p3, lb_logits, wexp2, masks2, blob4, rows)
    wdh = blob4[:, 0:q].reshape(D_MODEL, D_MODEL)
    wdp = blob4[:, q:2 * q].reshape(D_MODEL, D_MODEL)
    wout = blob4[:, 2 * q:3 * q].reshape(D_MODEL, D_MODEL)
    poolw = blob4[:, 3 * q:].reshape(N_CHIPS, n_grp, pg, POOL_GDIM).transpose(1, 0, 2, 3)
    poolw = poolw.reshape(n_grp, POOL_GDIM, POOL_GDIM)
    d_o, dz2, dpb, dblob4, dpw, small = _mixers(
        o, p3, tokens, head, tgt, wdh, wdp, wout, poolw, hg_w, pool_scale, final_w, rows)
    dpw4 = dpw.reshape(n_grp, N_CHIPS, pg, POOL_GDIM).transpose(1, 0, 2, 3)
    dpw4 = dpw4.reshape(N_CHIPS, n_grp * pg * POOL_GDIM // D_MODEL, D_MODEL)
    dblob4 = dblob4.at[:, 3 * q:, :].set(dpw4)

    dw_mix, dw_mix16, db_mix = _in_proj_weight_grad(h, dpb, rows, "in_proj_weight_grad_mix")
    dpa, dlb, rxw_mix, rx_blob = _hgrn_backward(
        p3, d_o, states, e16, a2, lb_logits, wexp_t, masks2, dw_mix16, dblob4.astype(BF16), rows)
    dw_rec, dw_rec16, db_rec = _in_proj_weight_grad(h, dpa, rows, "in_proj_weight_grad_rec")
    d_tokens, d_meta, dnw, rxw_rec = _input_grad(dpa, dpb, w4, tokens, head, dz2, norm_w, dw_rec16, rows)

    small = jnp.concatenate([
        small[ROW_LOSS:ROW_LOSS + 1],
        d_meta,
        dnw,
        db_rec.reshape(len(SEGS_REC), D_MODEL), db_mix.reshape(len(SEGS_MIX), D_MODEL),
        dlb, jnp.zeros_like(dlb),
        small[ROW_HG_W:ROW_HG_W + 1], small[ROW_POOL_SCALE:ROW_POOL_SCALE + 1],
        small[ROW_FINAL_W:ROW_FINAL_W + 1],
        jnp.zeros((SMALL_ROWS - ROW_FINAL_W - 1, D_MODEL), F32),
    ], axis=0)
    return d_tokens, (dw_rec, dw_mix, rxw_rec, rxw_mix), (dblob4, rx_blob), small


ANY = pl.BlockSpec(memory_space=pl.ANY)
MESH = pl.DeviceIdType.MESH


def _place():
    x, y, c = lax.axis_index("x"), lax.axis_index("y"), lax.axis_index("c")
    chips = [(1 - x, y), (x, 1 - y), (1 - x, 1 - y)]
    return x, y, c, chips


class _ShardGather:
    def __init__(self, rows):
        self.half = rows // 2

    def semaphores(self):
        return [pltpu.SemaphoreType.DMA((6,)), pltpu.SemaphoreType.DMA((6,))]

    def _copy(self, k, slot, to, send_sems, recv_sems):
        return pltpu.make_async_remote_copy(src_ref=slot, dst_ref=slot, send_sem=send_sems.at[k],
                                            recv_sem=recv_sems.at[k], device_id=to, device_id_type=MESH)

    def _half(self, ref4, chip, which):
        return ref4.at[chip, pl.ds(which * self.half, self.half), :]

    def start(self, ref4, send_sems, recv_sems, which=(0, 1, 2)):
        x, y, c, chips = _place()
        for j in which:
            cx, cy = chips[j]
            self._copy(j, self._half(ref4, 2 * x + y, c), (cx, cy, c), send_sems, recv_sems).start()

    def start_diagonal_after_neighbours(self, ref4, send_sems, recv_sems):
        x, y, c, chips = _place()
        for j in (0, 1):
            cx, cy = chips[j]
            self._copy(j, self._half(ref4, 2 * x + y, c), (cx, cy, c), send_sems, recv_sems).wait_send()
        self.start(ref4, send_sems, recv_sems, which=(2,))

    def pass_on(self, j, ref4, send_sems, recv_sems):
        x, y, c, chips = _place()
        cx, cy = chips[j]
        landed = self._half(ref4, 2 * cx + cy, c)
        self._copy(j, landed, (cx, cy, c), send_sems, recv_sems).wait_recv()
        self._copy(3 + j, landed, (x, y, 1 - c), send_sems, recv_sems).start()

    def await_sibling(self, j, ref4, send_sems, recv_sems):
        x, y, c, chips = _place()
        cx, cy = chips[j]
        self._copy(3 + j, self._half(ref4, 2 * cx + cy, 1 - c), (x, y, 1 - c), send_sems, recv_sems).wait_recv()

    def finish(self, ref4, send_sems, recv_sems, which=(0, 1, 2)):
        x, y, c, chips = _place()
        for j, (cx, cy) in enumerate(chips):
            if j in which:
                self._copy(j, self._half(ref4, 2 * x + y, c), (cx, cy, c), send_sems, recv_sems).wait_send()
            self._copy(3 + j, self._half(ref4, 2 * cx + cy, c), (x, y, 1 - c), send_sems, recv_sems).wait_send()


def _gather_meta(m4):
    def body(m_in_ref, m4_ref, send_sems, recv_sems):
        x, y, c, chips = _place()

        def copy(j, slot, to):
            return pltpu.make_async_remote_copy(src_ref=slot, dst_ref=slot, send_sem=send_sems.at[j],
                                                recv_sem=recv_sems.at[j], device_id=to, device_id_type=MESH)

        sends = [copy(j, m4_ref.at[2 * x + y], (cx, cy, c)) for j, (cx, cy) in enumerate(chips)]
        for cp in sends:
            cp.start()
        for j, (cx, cy) in enumerate(chips):
            copy(j, m4_ref.at[2 * cx + cy], (x, y, c)).wait_recv()
        for cp in sends:
            cp.wait_send()

    return pl.pallas_call(
        body, name="gather_meta",
        in_specs=[ANY], out_specs=ANY, out_shape=jax.ShapeDtypeStruct(m4.shape, m4.dtype),
        input_output_aliases={0: 0},
        scratch_shapes=[pltpu.SemaphoreType.DMA((3,)), pltpu.SemaphoreType.DMA((3,))],
    )(m4)


class _GradExchange:
    def __init__(self, segs, with_blob):
        self.segs = tuple(segs)
        self.with_blob = with_blob

    def landing_w(self):
        return jax.ShapeDtypeStruct((N_CHIPS, 2, D_MODEL // 2, D_MODEL), BF16)

    def landing_blob(self, blob16):
        return jax.ShapeDtypeStruct((N_DEV, blob16.shape[1] // 2, D_MODEL), BF16)

    def semaphores(self):
        n_send = len(self.segs) + (2 * N_CHIPS if self.with_blob else 0)
        n_recv = 2 * N_CHIPS + (N_DEV if self.with_blob else 0)
        return [pltpu.SemaphoreType.DMA((n_send,)), pltpu.SemaphoreType.DMA((n_recv,))]

    def _copies(self, dw_ref, rxw_ref, blob_ref, rxb_ref, send_sems, recv_sems):
        x, y, c = lax.axis_index("x"), lax.axis_index("y"), lax.axis_index("c")
        chip = 2 * x + y

        def relation(kx, ky, h):
            return (x ^ kx) * 4 + (y ^ ky) * 2 + (c ^ h)

        def copy(src, dst, send_k, recv_k, to):
            return functools.partial(pltpu.make_async_remote_copy, src_ref=src, dst_ref=dst,
                                     send_sem=send_sems.at[send_k], recv_sem=recv_sems.at[recv_k],
                                     device_id=to, device_id_type=MESH)

        sends, recvs = [], []
        for i, s in enumerate(self.segs):
            kx, ky = (s // 2) >> 1, (s // 2) & 1
            r = (x ^ kx) * 2 + (y ^ ky)
            sends.append((r != 0, copy(dw_ref.at[i], rxw_ref.at[r, s % 2], i, 2 * r + s % 2, (kx, ky, c))))
        for j in range(2):
            mine = [s // 2 for s in self.segs if s % 2 == j]
            if mine:
                cond = functools.reduce(lambda a, b: a | b, [chip == k for k in mine])
                for r in range(1, N_CHIPS):
                    slot = rxw_ref.at[r, j]
                    recvs.append((cond, copy(slot, slot, 0, 2 * r + j, (x, y, c))))
        if self.with_blob:
            hb = blob_ref.shape[1] // 2
            first_send, first_recv = len(self.segs), 2 * N_CHIPS
            for k in range(N_CHIPS):
                for h in range(2):
                    r = relation(k >> 1, k & 1, h)
                    sends.append((r != 0, copy(blob_ref.at[k, pl.ds(h * hb, hb), :], rxb_ref.at[r],
                                               first_send + 2 * k + h, first_recv + r, (k >> 1, k & 1, h))))
            for r in range(1, N_DEV):
                slot = rxb_ref.at[r]
                recvs.append((None, copy(slot, slot, 0, first_recv + r, (x, y, c))))
        return sends, recvs

    def start(self, *refs):
        sends, _ = self._copies(*refs)
        for cond, make in sends:
            pl.when(cond)(lambda make=make: make().start())

    def wait(self, *refs):
        sends, recvs = self._copies(*refs)
        for cond, make in sends:
            pl.when(cond)(lambda make=make: make().wait_send())
        for cond, make in recvs:
            if cond is None:
                make().wait_recv()
            else:
                pl.when(cond)(lambda make=make: make().wait_recv())


def _sum_landed(own, rx_ref):
    total = own
    for r in range(1, rx_ref.shape[0]):
        total = total + rx_ref[r, 0].astype(F32)
    return total


def _finish_w(dw_rec, dw_mix, rx_rec, rx_mix, place_arr):
    half = D_MODEL // 2
    tm = _tile(half, 256)
    n_rec = len(SEGS_REC)

    def body(place_ref, own_rec_ref, own_mix_ref, rx_rec_ref, rx_mix_ref, out_ref):
        seg = 2 * place_ref[0] + pl.program_id(0)

        @pl.when(seg < n_rec)
        def _():
            out_ref[0] = _sum_landed(own_rec_ref[0], rx_rec_ref)

        @pl.when(seg >= n_rec)
        def _():
            out_ref[0] = _sum_landed(own_mix_ref[0], rx_mix_ref)

    def own_spec(first, count):
        def index(j, i, place_ref):
            seg = 2 * place_ref[0] + j
            return (jnp.clip(seg - first, 0, count - 1), i, 0)
        return pl.BlockSpec((1, tm, D_MODEL), index)

    rx_spec = pl.BlockSpec((N_CHIPS, 1, tm, D_MODEL), lambda j, i, place_ref: (0, j, i, 0))
    return pl.pallas_call(
        body, name="finish_w",
        grid_spec=pltpu.PrefetchScalarGridSpec(
            num_scalar_prefetch=1, grid=(2, half // tm),
            in_specs=[own_spec(0, n_rec), own_spec(n_rec, len(SEGS_MIX)), rx_spec, rx_spec],
            out_specs=pl.BlockSpec((1, tm, D_MODEL), lambda j, i, place_ref: (place_ref[1], i, j))),
        out_shape=jax.ShapeDtypeStruct((2, half, 2 * D_MODEL), F32),
        compiler_params=_params(("arbitrary", "arbitrary")),
    )(place_arr, dw_rec, dw_mix, rx_rec, rx_mix)


def _finish_blob(dblob4, rx_blob, place_arr):
    n, rows, cols = rx_blob.shape
    tm = _tile(rows, 256)

    def body(place_ref, own_ref, rx_ref, out_ref):
        out_ref[0] = _sum_landed(own_ref[0, 0], rx_ref)

    return pl.pallas_call(
        body, name="finish_blob",
        grid_spec=pltpu.PrefetchScalarGridSpec(
            num_scalar_prefetch=1, grid=(rows // tm,),
            in_specs=[pl.BlockSpec((1, 1, tm, cols), lambda i, place_ref: (place_ref[0], place_ref[1], i, 0)),
                      pl.BlockSpec((n, 1, tm, cols), lambda i, place_ref: (0, 0, i, 0))],
            out_specs=pl.BlockSpec((1, tm, cols), lambda i, place_ref: (place_ref[1], i, 0))),
        out_shape=jax.ShapeDtypeStruct((2, rows, cols), F32),
        compiler_params=_params(("arbitrary",)),
    )(place_arr, dblob4.reshape(N_CHIPS, 2, rows, cols), rx_blob.reshape(n, 1, rows, cols))


def _share_finished(fw2, fb2, small):
    def body(w_in_ref, b_in_ref, small_ref, w_ref, b_ref, s_ref, bounce, local_sem, send_sems, recv_sems):
        x, y, c, _ = _place()
        sibling = (x, y, 1 - c)

        def copy(k, src, dst, to):
            return pltpu.make_async_remote_copy(src_ref=src, dst_ref=dst, send_sem=send_sems.at[k],
                                                recv_sem=recv_sems.at[k], device_id=to, device_id_type=MESH)

        sends = [copy(0, w_ref.at[c], w_ref.at[c], sibling), copy(1, b_ref.at[c], b_ref.at[c], sibling)]
        for r in range(1, N_DEV):
            peer = (x ^ ((r >> 2) & 1), y ^ ((r >> 1) & 1), c ^ (r & 1))
            sends.append(copy(1 + r, small_ref, s_ref.at[r], peer))
        for cp in sends:
            cp.start()
        for src, dst in ((small_ref, bounce), (bounce, s_ref.at[0])):
            own = pltpu.make_async_copy(src, dst, local_sem)
            own.start()
            own.wait()
        landed = [w_ref.at[1 - c], b_ref.at[1 - c]] + [s_ref.at[r] for r in range(1, N_DEV)]
        for k, slot in enumerate(landed):
            copy(k, slot, slot, (x, y, c)).wait_recv()
        for cp in sends:
            cp.wait_send()

    same = lambda a: jax.ShapeDtypeStruct(a.shape, a.dtype)
    n_sem = 2 + N_DEV - 1
    return pl.pallas_call(
        body, name="share_finished",
        in_specs=[ANY, ANY, ANY], out_specs=[ANY, ANY, ANY],
        out_shape=[same(fw2), same(fb2), jax.ShapeDtypeStruct((N_DEV,) + small.shape, F32)],
        input_output_aliases={0: 0, 1: 1},
        scratch_shapes=[pltpu.VMEM(small.shape, F32), pltpu.SemaphoreType.DMA,
                        pltpu.SemaphoreType.DMA((n_sem,)), pltpu.SemaphoreType.DMA((n_sem,))],
    )(fw2, fb2, small)


def _sum_small(slots, lb_logits, me_arr):
    def body(me_ref, slots_ref, lbl_ref, out_ref):
        me = me_ref[0]
        total = slots_ref[me]
        for d in range(1, N_DEV):
            total = total + slots_ref[d ^ me]
        out_ref[...] = total
        out_ref[ROW_LOSS:ROW_LOSS + 1, :] = jnp.broadcast_to(
            jnp.sum(total[ROW_LOSS:ROW_LOSS + 1, :], axis=-1, keepdims=True), (1, D_MODEL))
        lb = _lower_bound(lbl_ref[...])
        g0 = total[ROW_LB:ROW_LB + 1, :] * lb * (1.0 - lb)
        out_ref[ROW_LB:ROW_LB + 1, :] = g0
        out_ref[ROW_LB + 1:ROW_LB + 2, :] = -g0

    return pl.pallas_call(
        body, name="sum_small",
        grid_spec=pltpu.PrefetchScalarGridSpec(
            num_scalar_prefetch=1, grid=(1,),
            in_specs=[pl.BlockSpec((N_DEV, SMALL_ROWS, D_MODEL), lambda i, me_ref: (0, 0, 0)),
                      pl.BlockSpec((2, D_MODEL), lambda i, me_ref: (0, 0))],
            out_specs=pl.BlockSpec((SMALL_ROWS, D_MODEL), lambda i, me_ref: (0, 0))),
        out_shape=jax.ShapeDtypeStruct((SMALL_ROWS, D_MODEL), F32),
        compiler_params=_params(("arbitrary",)),
    )(me_arr, slots, lb_logits)


def _adamw_step(w, g, m, v):
    c1 = 1.0 / (1.0 - ADAM_B1 ** ADAM_STEP)
    c2 = 1.0 / (1.0 - ADAM_B2 ** ADAM_STEP)
    nm = ADAM_B1 * m + (1.0 - ADAM_B1) * g
    nv = ADAM_B2 * v + (1.0 - ADAM_B2) * (g * g)
    return -ADAM_LR * ((nm * c1) / (jnp.sqrt(nv * c2) + ADAM_EPS) + ADAM_WD * w), nm, nv


SMALL_PARAMS = (("norm_w", ROW_NORM_W, 1), ("b_in", ROW_B_IN, N_SEG), ("lb_logits", ROW_LB, 2),
                ("hg_norm_w", ROW_HG_W, 1), ("pool_scale", ROW_POOL_SCALE, 1), ("final_norm_w", ROW_FINAL_W, 1))


def _update_small(tot, triples):
    n = len(SMALL_PARAMS)

    def body(tot_ref, *refs):
        ins, outs = refs[:3 * n], refs[3 * n:]
        for p, (name, row, n_rows) in enumerate(SMALL_PARAMS):
            w_ref, m_ref, v_ref = ins[3 * p:3 * p + 3]
            g_ref, d_ref, nm_ref, nv_ref = outs[4 * p:4 * p + 4]
            if w_ref.shape[0] == n_rows:
                pieces = [(slice(None), slice(None), tot_ref[row:row + n_rows, :])]
            else:
                pieces = [(slice(None), slice(k * D_MODEL, (k + 1) * D_MODEL), tot_ref[row + k:row + k + 1, :])
                          for k in range(n_rows)]
            for rows_, cols_, g in pieces:
                d, nm, nv = _adamw_step(w_ref[rows_, cols_], g, m_ref[rows_, cols_], v_ref[rows_, cols_])
                g_ref[rows_, cols_] = g
                d_ref[rows_, cols_] = d
                nm_ref[rows_, cols_] = nm
                nv_ref[rows_, cols_] = nv

    whole = pl.BlockSpec(memory_space=pltpu.VMEM)
    flat = [a for t in triples for a in t]
    out_shape = [jax.ShapeDtypeStruct(t[0].shape, F32) for t in triples for _ in range(4)]
    outs = pl.pallas_call(
        body, name="update_small",
        in_specs=[whole] * (1 + len(flat)), out_specs=[whole] * len(out_shape), out_shape=out_shape,
        compiler_params=_params(),
    )(tot, *flat)
    return [tuple(outs[4 * p:4 * p + 4]) for p in range(n)]


def _update_blob(g_blob, triples):
    q_rows = triples[0][0].shape[0]
    pool_rows = triples[3][0].shape[0]
    steps = q_rows // pool_rows

    def body(*refs):
        ins, outs = refs[:16], refs[16:]
        for p in range(4):
            g_ref, (w_ref, m_ref, v_ref) = ins[p], ins[4 + 3 * p:7 + 3 * p]
            go_ref, d_ref, nm_ref, nv_ref = outs[4 * p:4 * p + 4]

            def update():
                g = g_ref[...]
                go_ref[...] = g
                d_ref[...], nm_ref[...], nv_ref[...] = _adamw_step(w_ref[...], g, m_ref[...], v_ref[...])

            if p < 3:
                update()
            else:
                pl.when(pl.program_id(0) == 0)(update)

    blk = pl.BlockSpec((pool_rows, D_MODEL), lambda i: (i, 0))
    once = pl.BlockSpec((pool_rows, D_MODEL), lambda i: (0, 0))
    g_specs = [pl.BlockSpec((pool_rows, D_MODEL), lambda i, p=p: (steps * p + i, 0)) for p in range(3)]
    g_specs.append(pl.BlockSpec((pool_rows, D_MODEL), lambda i: (3 * steps, 0)))
    piece_specs = [blk] * 9 + [once] * 3
    out_specs = [blk] * 12 + [once] * 4
    out_shape = [jax.ShapeDtypeStruct(t[0].shape, F32) for t in triples for _ in range(4)]
    outs = pl.pallas_call(
        body, name="update_blob",
        grid=(steps,), in_specs=g_specs + piece_specs, out_specs=out_specs, out_shape=out_shape,
        compiler_params=_params(("arbitrary",)),
    )(g_blob, g_blob, g_blob, g_blob, *[a for t in triples for a in t])
    return [tuple(outs[4 * p:4 * p + 4]) for p in range(4)]


def _adamw(w, g, m, v):
    rows, cols = w.shape
    tm = _tile(rows, 256, mult=8) if rows % 8 == 0 else rows

    def body(w_ref, g_ref, m_ref, v_ref, d_ref, nm_ref, nv_ref):
        d_ref[...], nm_ref[...], nv_ref[...] = _adamw_step(w_ref[...], g_ref[...], m_ref[...], v_ref[...])

    blk = pl.BlockSpec((tm, cols), lambda i: (i, 0))
    sds = jax.ShapeDtypeStruct((rows, cols), F32)
    return pl.pallas_call(
        body, name="adamw",
        grid=(rows // tm,), in_specs=[blk] * 4, out_specs=[blk] * 3, out_shape=[sds] * 3,
        compiler_params=_params(("arbitrary",)),
    )(w, g, m, v)


def kernel(x, meta_tokens, norm_w, w_in, b_in, lb_logits, hg_norm_w, pool_w, pool_scale, w_down_hg, w_down_pool, w_out, final_norm_w, loss_target, m_meta_tokens, m_norm_w, m_w_in, m_b_in, m_lb_logits, m_hg_norm_w, m_pool_w, m_pool_scale, m_w_down_hg, m_w_down_pool, m_w_out, m_final_norm_w, v_meta_tokens, v_norm_w, v_w_in, v_b_in, v_lb_logits, v_hg_norm_w, v_pool_w, v_pool_scale, v_w_down_hg, v_w_down_pool, v_w_out, v_final_norm_w):
    seq = x.shape[1]
    xi, yi, ci = lax.axis_index("x"), lax.axis_index("y"), lax.axis_index("c")
    chip = 2 * xi + yi
    place_arr = jnp.stack([chip, ci]).astype(jnp.int32)
    me_arr = jnp.reshape(4 * xi + 2 * yi + ci, (1,)).astype(jnp.int32)
    q = D_MODEL // N_CHIPS

    def blob_of(wdh, wdp, wo, pw):
        return jnp.concatenate([wdh[0], wdp[0], wo[0], pw[0].reshape(-1, D_MODEL)], axis=0)

    def in_every_slot(a):
        return jnp.broadcast_to(a[None], (N_CHIPS,) + a.shape)

    meta4 = _gather_meta(in_every_slot(meta_tokens))
    meta_full = meta4.transpose(1, 0, 2).reshape(N_META, D_MODEL)
    w4 = in_every_slot(w_in[0].astype(BF16))
    blob4 = in_every_slot(blob_of(w_down_hg, w_down_pool, w_out, pool_w).astype(BF16))
    seg_order = jnp.stack([2 * (chip ^ rel) + t for rel in (0, 2, 1, 3) for t in (0, 1)]).astype(jnp.int32)

    head = jnp.concatenate([jnp.zeros((PAD_ROWS, D_MODEL), F32), meta_full], axis=0)
    fw2 = final_norm_w.reshape(1, D_MODEL)
    d_tokens, w_parts, blob_parts, small = _local_step(
        x[0], head, loss_target[0], w4, blob4, seg_order, norm_w, b_in, lb_logits, hg_norm_w, pool_scale, fw2)
    grad_x = d_tokens[None]

    fin_w = _finish_w(*w_parts, place_arr)
    fin_b = _finish_blob(*blob_parts, place_arr)
    gw2, gb2, slots = _share_finished(fin_w, fin_b, small)
    tot = _sum_small(slots, lb_logits, me_arr)
    g_w_in = gw2.reshape(D_MODEL, 2 * D_MODEL)
    g_blob = gb2.reshape(-1, D_MODEL)

    d_win, nm_win, nv_win = _adamw(w_in[0], g_w_in, m_w_in[0], v_w_in[0])
    pool_rows = lambda a: a[0].reshape(-1, D_MODEL)
    blob_results = _update_blob(g_blob, [
        (w_down_hg[0], m_w_down_hg[0], v_w_down_hg[0]), (w_down_pool[0], m_w_down_pool[0], v_w_down_pool[0]),
        (w_out[0], m_w_out[0], v_w_out[0]), (pool_rows(pool_w), pool_rows(m_pool_w), pool_rows(v_pool_w))])
    g_meta = lax.dynamic_slice_in_dim(tot[ROW_META:ROW_META + N_META], chip * q, q, axis=1)
    d_meta, nm_meta, nv_meta = _adamw(meta_tokens, g_meta, m_meta_tokens, v_meta_tokens)

    as_row = lambda a: a.reshape(1, D_MODEL)
    small_results = _update_small(tot, [
        (norm_w, m_norm_w, v_norm_w), (b_in, m_b_in, v_b_in), (lb_logits, m_lb_logits, v_lb_logits),
        (hg_norm_w, m_hg_norm_w, v_hg_norm_w), (pool_scale, m_pool_scale, v_pool_scale),
        (as_row(final_norm_w), as_row(m_final_norm_w), as_row(v_final_norm_w))])

    def leaves(kind, meta_part, win_part):
        nw, bi, lbl, hg, ps, fw = [r[kind] for r in small_results]
        wdh, wdp, wo, pw = [r[kind] for r in blob_results]
        return [meta_part, nw, win_part[None], bi, lbl, hg, pw.reshape(pool_w.shape), ps,
                wdh[None], wdp[None], wo[None], fw.reshape(D_MODEL)]

    loss = tot[ROW_LOSS, 0]
    return (loss, grad_x,
            *leaves(0, g_meta, g_w_in),
            *leaves(1, d_meta, d_win),
            *leaves(2, nm_meta, nm_win),
            *leaves(3, nv_meta, nv_win))
```

```python
import functools

import numpy as np
import jax
import jax.numpy as jnp
from jax import lax
from jax.experimental import pallas as pl
from jax.experimental.pallas import tpu as pltpu

F32 = jnp.float32
BF16 = jnp.bfloat16

D_MODEL = 1024
N_SEG = 8
N_HEADS = 8
HEAD_DIM = 128
CHUNK = 64
N_META = 16
PAD_ROWS = CHUNK - N_META
FIRST_TOKEN_ROW = CHUNK
LEVELS = (32, 16, 8, 4, 2, 1)
N_EXP = 2 + len(LEVELS)
POOL_WINDOWS = (2, 4, 8, 16)
POOL_GDIM = D_MODEL // len(POOL_WINDOWS)
HALO = 16
LOCAL_UNROLL = 13
BACKWARD_UNROLL = 13
EPS = 1e-6
N_CHIPS = 4
N_DEV = 8
SEGS_REC = (0, 1, 2)
SEGS_MIX = (3, 4, 5, 6, 7)

ADAM_LR = 0.001
ADAM_B1 = 0.9
ADAM_B2 = 0.999
ADAM_EPS = 1e-08
ADAM_WD = 0.01
ADAM_STEP = 10

VMEM_LIMIT_BYTES = 56 * 1024 * 1024

ROW_LOSS = 0
ROW_META = 1
ROW_NORM_W = ROW_META + N_META
ROW_B_IN = ROW_NORM_W + 1
ROW_LB = ROW_B_IN + N_SEG
ROW_HG_W = ROW_LB + 2
ROW_POOL_SCALE = ROW_HG_W + 1
ROW_FINAL_W = ROW_POOL_SCALE + 1
SMALL_ROWS = 32


def _tile(total, cap, mult=16):
    best = None
    for t in range(mult, min(total, cap) + 1, mult):
        if total % t == 0:
            best = t
    assert best is not None, (total, cap, mult)
    return best


def _token_window(tm, tile_of):
    def index(*grid):
        return (pl.multiple_of(jnp.maximum(tile_of(*grid) * tm - FIRST_TOKEN_ROW, 0), HALO), 0)
    return pl.BlockSpec((pl.Element(tm), pl.Element(D_MODEL)), index)


def _padded_tile(window, head, tile):
    first = jnp.concatenate([head, pltpu.roll(window, FIRST_TOKEN_ROW, 0)[FIRST_TOKEN_ROW:]], axis=0)
    return jnp.where(tile == 0, first, window)


def _params(sem=None):
    return pltpu.CompilerParams(dimension_semantics=sem, vmem_limit_bytes=VMEM_LIMIT_BYTES)


def _dot(a, b):
    return jnp.dot(a, b, preferred_element_type=F32)


def _dot_nt(a, b):
    return lax.dot_general(a, b, (((1,), (1,)), ((), ())), preferred_element_type=F32)


def _dot_tn(a, b):
    return lax.dot_general(a, b, (((0,), (0,)), ((), ())), preferred_element_type=F32)


def _sigmoid_pair(x):
    t = jnp.exp(-jnp.abs(x))
    r = 1.0 / (1.0 + t)
    pos = x >= 0
    return jnp.where(pos, r, t * r), jnp.where(pos, t * r, r)


def _exponent_matrix():
    t = np.arange(CHUNK)[:, None]
    j = np.arange(CHUNK)[None, :]
    blocks = [j <= t, j > t]
    for m in LEVELS:
        rho = (t // (2 * m)) * (2 * m) + m
        upper = (t >= rho) & (j > rho) & (j <= t)
        lower = (t < rho) & (j > t) & (j <= rho)
        blocks.append(upper | lower)
    return np.concatenate(blocks, axis=0).astype(np.float32)


def _pair_masks():
    t = np.arange(CHUNK)[:, None]
    s = np.arange(CHUNK)[None, :]
    masks = [t == s]
    for m in LEVELS:
        same = (t // (2 * m)) == (s // (2 * m))
        masks.append(same & ((t % (2 * m)) >= m) & ((s % (2 * m)) < m))
    return np.stack(masks).astype(np.float32)


LEVEL_PAIRS = ((0, 1), (2, 3), (4, 5), (6, None))


def _paired_masks():
    m = _pair_masks()
    zero = np.zeros_like(m[0])
    return np.stack([np.concatenate([m[a], zero if b is None else m[b]], axis=1) for a, b in LEVEL_PAIRS])


def _lower_bound(lbl):
    return 1.0 / (1.0 + jnp.exp(lbl[1:2, :] - lbl[0:1, :]))


def _in_proj(tokens, m4, norm_w, w4, b_in, seg_order, rows):
    tm = _tile(rows, 1040)
    nt = rows // tm
    gather = _ShardGather(w4.shape[1])

    def body(order_ref, z_ref, nw_ref, b_ref, w_in_ref, m_in_ref, h_ref, p_ref, w4_ref, head_ref, m4_ref,
             h_all, w_buf, w_sem, send_sems, recv_sems, meta_send, meta_recv, meta_sem):
        kk, i = pl.program_id(0), pl.program_id(1)

        @pl.when((kk == 0) & (i == 0))
        def _():
            x, y, c, chips = _place()

            def meta_copy(j, chip, to):
                return pltpu.make_async_remote_copy(
                    src_ref=m4_ref.at[chip], dst_ref=m4_ref.at[chip], send_sem=meta_send.at[j],
                    recv_sem=meta_recv.at[j], device_id=to, device_id_type=MESH)

            sends = [meta_copy(j, 2 * x + y, (cx, cy, c)) for j, (cx, cy) in enumerate(chips)]
            for cp in sends:
                cp.start()
            gather.start(w4_ref, send_sems, recv_sems, which=(0, 1))
            for j, (cx, cy) in enumerate(chips):
                meta_copy(j, 2 * cx + cy, (x, y, c)).wait_recv()
            for cp in sends:
                cp.wait_send()
            head_ref[0:PAD_ROWS, :] = jnp.zeros((PAD_ROWS, D_MODEL), F32)
            q_cols = D_MODEL // N_CHIPS
            for k in range(N_CHIPS):
                cp = pltpu.make_async_copy(
                    m4_ref.at[k], head_ref.at[pl.ds(PAD_ROWS, N_META), pl.ds(k * q_cols, q_cols)], meta_sem)
                cp.start()
                cp.wait()

        @pl.when((kk == 2) & (i == 0))
        def _():
            gather.start_diagonal_after_neighbours(w4_ref, send_sems, recv_sems)

        @pl.when(kk == 0)
        def _():
            zt = _padded_tile(z_ref[...], head_ref[...], i)
            rstd = lax.rsqrt(jnp.mean(zt * zt, axis=-1, keepdims=True) + EPS)
            h = (zt * rstd * nw_ref[...]).astype(BF16)
            h_all[pl.ds(pl.multiple_of(i * tm, 16), tm), :] = h
            h_ref[...] = h

        @pl.when((kk == 2) & (i == 0))
        def _():
            gather.pass_on(0, w4_ref, send_sems, recv_sems)
            gather.pass_on(1, w4_ref, send_sems, recv_sems)
            gather.await_sibling(0, w4_ref, send_sems, recv_sems)

        @pl.when((kk == 4) & (i == 0))
        def _():
            gather.await_sibling(1, w4_ref, send_sems, recv_sems)

        @pl.when((kk == 5) & (i == 0))
        def _():
            gather.pass_on(2, w4_ref, send_sems, recv_sems)

        @pl.when((kk == 6) & (i == 0))
        def _():
            gather.await_sibling(2, w4_ref, send_sems, recv_sems)

        def weights(which):
            seg = order_ref[2 * (kk // 2) + which]
            return pltpu.make_async_copy(
                w4_ref.at[seg // 2, :, pl.ds(pl.multiple_of((seg % 2) * D_MODEL, D_MODEL), D_MODEL)],
                w_buf.at[which], w_sem.at[which])

        @pl.when((i == 0) & (kk % 2 == 0))
        def _():
            weights(0).start()
            weights(1).start()
            weights(0).wait()

        @pl.when((i == 0) & (kk % 2 == 1))
        def _():
            weights(1).wait()

        p_ref[0] = _dot(h_all[pl.ds(pl.multiple_of(i * tm, 16), tm), :], w_buf[kk % 2]) + b_ref[...]

        @pl.when((kk == N_SEG - 1) & (i == nt - 1))
        def _():
            gather.finish(w4_ref, send_sems, recv_sems, which=(2,))

    first_pass = lambda kk, i, order_ref: (jnp.where(kk == 0, i, nt - 1), 0)
    return pl.pallas_call(
        body, name="in_proj",
        grid_spec=pltpu.PrefetchScalarGridSpec(
            num_scalar_prefetch=1, grid=(N_SEG, nt),
            in_specs=[
                _token_window(tm, lambda kk, i, order_ref: jnp.where(kk == 0, i, nt - 1)),
                pl.BlockSpec((1, D_MODEL), lambda kk, i, order_ref: (0, 0)),
                pl.BlockSpec((1, D_MODEL), lambda kk, i, order_ref: (0, order_ref[kk])),
                ANY, ANY,
            ],
            out_specs=[
                pl.BlockSpec((tm, D_MODEL), first_pass),
                pl.BlockSpec((1, tm, D_MODEL), lambda kk, i, order_ref: (order_ref[kk], i, 0)),
                ANY,
                pl.BlockSpec((FIRST_TOKEN_ROW, D_MODEL), lambda kk, i, order_ref: (0, 0)),
                ANY,
            ],
            scratch_shapes=[
                pltpu.VMEM((rows, D_MODEL), BF16),
                pltpu.VMEM((2, D_MODEL, D_MODEL), BF16),
                pltpu.SemaphoreType.DMA((2,)),
            ] + gather.semaphores() + [
                pltpu.SemaphoreType.DMA((N_CHIPS - 1,)), pltpu.SemaphoreType.DMA((N_CHIPS - 1,)),
                pltpu.SemaphoreType.DMA,
            ]),
        out_shape=[
            jax.ShapeDtypeStruct((rows, D_MODEL), BF16),
            jax.ShapeDtypeStruct((N_SEG, rows, D_MODEL), F32),
            jax.ShapeDtypeStruct(w4.shape, w4.dtype),
            jax.ShapeDtypeStruct((FIRST_TOKEN_ROW, D_MODEL), F32),
            jax.ShapeDtypeStruct(m4.shape, m4.dtype),
        ],
        input_output_aliases={4: 2, 5: 4},
        compiler_params=_params(("arbitrary", "arbitrary")),
    )(seg_order, tokens, norm_w, b_in, w4, m4)


def _hgrn_forward(p3, lb_logits, wexp2, masks2, blob4, rows):
    n_chunks = rows // CHUNK
    cpb = _tile(n_chunks, 13, mult=1)
    rb_rows = cpb * CHUNK
    n_rb = n_chunks // cpb
    lanes = cpb * HEAD_DIM
    gather = _ShardGather(blob4.shape[1])

    def body(q_ref, fz_ref, v_ref, lbl_ref, wexp_ref, mask_ref, b_in_ref, o_ref, s_ref, e16_ref, a2_ref, b4_ref,
             st_ref, e_ref, u_ref, q_s, kk_s, v_s, send_sems, recv_sems):
        rb = pl.program_id(1)

        @pl.when((pl.program_id(0) == 0) & (rb == 0))
        def _():
            gather.start(b4_ref, send_sems, recv_sems)

        @pl.when(rb == 0)
        def _():
            st_ref[...] = jnp.zeros_like(st_ref)

        lb = _lower_bound(lbl_ref[...])
        row = rb * rb_rows + lax.broadcasted_iota(jnp.int32, (rb_rows, 1), 0)
        valid = row >= PAD_ROWS
        sg, sn = _sigmoid_pair(fz_ref[0])
        g = jnp.where(valid, jnp.log(lb + (1.0 - lb) * sg), 0.0)
        kk_s[...] = jnp.where(valid, (1.0 - lb) * sn, 0.0)
        q_s[...] = jnp.where(valid, q_ref[0], 0.0)
        v_s[...] = jnp.where(valid, v_ref[0], 0.0).astype(BF16)
        hi = g.astype(BF16)
        mid = (g - hi.astype(F32)).astype(BF16)
        g2 = jnp.concatenate(
            [jnp.concatenate([hi[b * CHUNK:(b + 1) * CHUNK], mid[b * CHUNK:(b + 1) * CHUNK]], axis=0)
             for b in range(cpb)], axis=1)
        e_ref[...] = jnp.exp(_dot(wexp_ref[...], g2))
        e16_ref[0, 0] = e_ref[...].astype(BF16)

        def contribution(b, carry):
            r0 = pl.multiple_of(b * CHUNK, CHUNK)
            l0 = pl.multiple_of(b * HEAD_DIM, HEAD_DIM)
            kc16 = (kk_s[pl.ds(r0, CHUNK), :] * e_ref[CHUNK:2 * CHUNK, pl.ds(l0, HEAD_DIM)]).astype(BF16)
            u_ref[b] = _dot_tn(v_s[pl.ds(r0, CHUNK), :], kc16)
            return carry

        lax.fori_loop(0, cpb, contribution, 0, unroll=LOCAL_UNROLL)

        def recur(b, st):
            l0 = pl.multiple_of(b * HEAD_DIM, HEAD_DIM)
            s_ref[0, b] = st
            return st * e_ref[CHUNK - 1:CHUNK, pl.ds(l0, HEAD_DIM)] + u_ref[b]

        st_ref[...] = lax.fori_loop(0, cpb, recur, st_ref[...])

        zeros16 = jnp.zeros((CHUNK, HEAD_DIM), BF16)

        def local(b, carry):
            r0 = pl.multiple_of(b * CHUNK, CHUNK)
            l0 = pl.multiple_of(b * HEAD_DIM, HEAD_DIM)
            q = q_s[pl.ds(r0, CHUNK), :]
            kk = kk_s[pl.ds(r0, CHUNK), :]
            v16 = v_s[pl.ds(r0, CHUNK), :]

            def scaled(entry):
                if entry == 0:
                    return q.astype(BF16), kk.astype(BF16)
                e_m = e_ref[(1 + entry) * CHUNK:(2 + entry) * CHUNK, pl.ds(l0, HEAD_DIM)]
                return (q * e_m).astype(BF16), (kk * e_m).astype(BF16)

            a2 = jnp.zeros((CHUNK, 2 * CHUNK), F32)
            for p, (ea, eb) in enumerate(LEVEL_PAIRS):
                qa, ka = scaled(ea)
                if eb is None:
                    prod = _dot_nt(qa, jnp.concatenate([ka, zeros16], axis=0))
                else:
                    qb_, kb_ = scaled(eb)
                    rhs = jnp.concatenate([jnp.concatenate([ka, zeros16], axis=1),
                                           jnp.concatenate([zeros16, kb_], axis=1)], axis=0)
                    prod = _dot_nt(jnp.concatenate([qa, qb_], axis=1), rhs)
                a2 = a2 + mask_ref[p] * prod
            a2_16 = a2.astype(BF16)
            a2_ref[pl.ds(r0, CHUNK), :] = a2_16
            qb16 = (q * e_ref[0:CHUNK, pl.ds(l0, HEAD_DIM)]).astype(BF16)
            o_ref[pl.ds(r0, CHUNK), :] = (_dot(a2_16, jnp.concatenate([v16, v16], axis=0))
                                          + _dot_nt(qb16, s_ref[0, b].astype(BF16)))
            return carry

        lax.fori_loop(0, cpb, local, 0, unroll=LOCAL_UNROLL)

        @pl.when((pl.program_id(0) == N_HEADS // 2) & (rb == 0))
        def _():
            for j in range(N_CHIPS - 1):
                gather.pass_on(j, b4_ref, send_sems, recv_sems)

        @pl.when((pl.program_id(0) == N_HEADS - 1) & (rb == n_rb - 1))
        def _():
            for j in range(N_CHIPS - 1):
                gather.await_sibling(j, b4_ref, send_sems, recv_sems)
            gather.finish(b4_ref, send_sems, recv_sems)

    head_block = lambda seg: pl.BlockSpec((1, rb_rows, HEAD_DIM), lambda h, r: (seg, r, h))
    return pl.pallas_call(
        body, name="hgrn_forward",
        grid=(N_HEADS, n_rb),
        in_specs=[
            head_block(0), head_block(1), head_block(2),
            pl.BlockSpec((2, HEAD_DIM), lambda h, r: (0, h)),
            pl.BlockSpec((N_EXP * CHUNK, 2 * CHUNK), lambda h, r: (0, 0)),
            pl.BlockSpec((len(LEVEL_PAIRS), CHUNK, 2 * CHUNK), lambda h, r: (0, 0, 0)),
            ANY,
        ],
        out_specs=[
            pl.BlockSpec((rb_rows, HEAD_DIM), lambda h, r: (r, h)),
            pl.BlockSpec((1, cpb, HEAD_DIM, HEAD_DIM), lambda h, r: (h, r, 0, 0)),
            pl.BlockSpec((1, 1, N_EXP * CHUNK, lanes), lambda h, r: (h, r, 0, 0)),
            pl.BlockSpec((rb_rows, HEAD_DIM), lambda h, r: (r, h)),
            ANY,
        ],
        out_shape=[
            jax.ShapeDtypeStruct((rows, D_MODEL), F32),
            jax.ShapeDtypeStruct((N_HEADS, n_chunks, HEAD_DIM, HEAD_DIM), F32),
            jax.ShapeDtypeStruct((N_HEADS, n_rb, N_EXP * CHUNK, lanes), BF16),
            jax.ShapeDtypeStruct((rows, D_MODEL), BF16),
            jax.ShapeDtypeStruct(blob4.shape, blob4.dtype),
        ],
        input_output_aliases={6: 4},
        scratch_shapes=[
            pltpu.VMEM((HEAD_DIM, HEAD_DIM), F32),
            pltpu.VMEM((N_EXP * CHUNK, lanes), F32),
            pltpu.VMEM((cpb, HEAD_DIM, HEAD_DIM), F32),
            pltpu.VMEM((rb_rows, HEAD_DIM), F32),
            pltpu.VMEM((rb_rows, HEAD_DIM), F32),
            pltpu.VMEM((rb_rows, HEAD_DIM), BF16),
        ] + gather.semaphores(),
        compiler_params=_params(("arbitrary", "arbitrary")),
    )(p3, p3, p3, lb_logits, wexp2, masks2, blob4)


def _hgrn_backward(p3, d_o, states, e16, a2, lb_logits, wexp_t, masks2, dw16, blob16, rows):
    n_chunks = rows // CHUNK
    cpb = _tile(n_chunks, 13, mult=1)
    rb_rows = cpb * CHUNK
    n_rb = n_chunks // cpb
    lanes = cpb * HEAD_DIM
    exchange = _GradExchange(SEGS_MIX, with_blob=True)

    def body(q_ref, fz_ref, v_ref, do_ref, s_ref, e_ref, a2_ref, lbl_ref, wexpt_ref, mask_ref, dw_ref, blob_ref,
             dp_ref, dlb_ref, rxw_ref, rxb_ref,
             dst_ref, g_ref, dsn_ref, q_s, kk_s, v_s, do_s, dq_s, dkk_s, dg_s, dx_s, da2_s, send_sems, recv_sems):
        step = pl.program_id(1)
        rb = n_rb - 1 - step

        @pl.when((pl.program_id(0) == 0) & (step == 0))
        def _():
            exchange.start(dw_ref, rxw_ref, blob_ref, rxb_ref, send_sems, recv_sems)

        @pl.when(step == 0)
        def _():
            dst_ref[...] = jnp.zeros_like(dst_ref)
            dlb_ref[...] = jnp.zeros_like(dlb_ref)

        lb = _lower_bound(lbl_ref[...])
        row = rb * rb_rows + lax.broadcasted_iota(jnp.int32, (rb_rows, 1), 0)
        valid = row >= PAD_ROWS
        sg, sn = _sigmoid_pair(fz_ref[0])
        f = lb + (1.0 - lb) * sg
        g = jnp.where(valid, jnp.log(f), 0.0)
        kk_s[...] = jnp.where(valid, (1.0 - lb) * sn, 0.0)
        q_s[...] = jnp.where(valid, q_ref[0], 0.0)
        v_s[...] = jnp.where(valid, v_ref[0], 0.0).astype(BF16)
        do_s[...] = do_ref[...].astype(BF16)
        e_last_all = jnp.exp(jnp.concatenate(
            [jnp.sum(g[b * CHUNK:(b + 1) * CHUNK], axis=0, keepdims=True) for b in range(cpb)], axis=0))
        last_row = lax.broadcasted_iota(jnp.int32, (CHUNK, 1), 0) == CHUNK - 1
        zeros16 = jnp.zeros((CHUNK, HEAD_DIM), BF16)

        def factor(block, l0):
            return e_ref[0, 0, block * CHUNK:(block + 1) * CHUNK, pl.ds(l0, HEAD_DIM)].astype(F32)

        def contribution(b, carry):
            r0 = pl.multiple_of(b * CHUNK, CHUNK)
            l0 = pl.multiple_of(b * HEAD_DIM, HEAD_DIM)
            qb16 = (q_s[pl.ds(r0, CHUNK), :] * factor(0, l0)).astype(BF16)
            g_ref[b] = _dot_tn(do_s[pl.ds(r0, CHUNK), :], qb16)
            return carry

        lax.fori_loop(0, cpb, contribution, 0, unroll=LOCAL_UNROLL)

        cur = dst_ref[...]
        for b in reversed(range(cpb)):
            dsn_ref[b] = cur
            cur = cur * e_last_all[b:b + 1, :] + g_ref[b]
        dst_ref[...] = cur

        def through_state(b, carry):
            r0 = pl.multiple_of(b * CHUNK, CHUNK)
            l0 = pl.multiple_of(b * HEAD_DIM, HEAD_DIM)
            v16 = v_s[pl.ds(r0, CHUNK), :]
            do16 = do_s[pl.ds(r0, CHUNK), :]
            st = s_ref[0, b]
            dsn = dsn_ref[b]
            dsn16 = dsn.astype(BF16)
            e_b, e_c = factor(0, l0), factor(1, l0)
            qb, kc = q_s[pl.ds(r0, CHUNK), :] * e_b, kk_s[pl.ds(r0, CHUNK), :] * e_c

            t = _dot_tn(a2_ref[pl.ds(r0, CHUNK), :], do16)
            dv = t[0:CHUNK] + t[CHUNK:2 * CHUNK] + _dot_nt(kc.astype(BF16), dsn16)
            dp_ref[2, pl.ds(r0, CHUNK), :] = dv.astype(BF16)
            da2_s[pl.ds(r0, CHUNK), :] = _dot_nt(do16, jnp.concatenate([v16, v16], axis=0))
            dqb = _dot(do16, st.astype(BF16))
            dkc = _dot(v16, dsn16)
            de = jnp.sum(dsn * st, axis=0, keepdims=True) * e_b[CHUNK - 1:CHUNK, :]
            dq_s[pl.ds(r0, CHUNK), :] = e_b * dqb
            dkk_s[pl.ds(r0, CHUNK), :] = e_c * dkc
            dx_s[0:CHUNK, pl.ds(l0, HEAD_DIM)] = (qb * dqb + jnp.where(last_row, de, 0.0)).astype(BF16)
            dx_s[CHUNK:2 * CHUNK, pl.ds(l0, HEAD_DIM)] = (kc * dkc).astype(BF16)
            return carry

        lax.fori_loop(0, cpb, through_state, 0, unroll=BACKWARD_UNROLL)

        def local(b, carry):
            r0 = pl.multiple_of(b * CHUNK, CHUNK)
            l0 = pl.multiple_of(b * HEAD_DIM, HEAD_DIM)
            q = q_s[pl.ds(r0, CHUNK), :]
            kk = kk_s[pl.ds(r0, CHUNK), :]
            da2 = da2_s[pl.ds(r0, CHUNK), :]
            dq = dq_s[pl.ds(r0, CHUNK), :]
            dkk = dkk_s[pl.ds(r0, CHUNK), :]

            def scaled(entry):
                if entry == 0:
                    return q, kk, None
                e_m = factor(1 + entry, l0)
                return q * e_m, kk * e_m, e_m

            for p, (ea, eb) in enumerate(LEVEL_PAIRS):
                dm = mask_ref[p] * da2
                dm_t = dm.T.astype(BF16)
                qa, ka, e_a = scaled(ea)
                if eb is None:
                    rhs_k = jnp.concatenate([jnp.concatenate([ka.astype(BF16), zeros16], axis=1),
                                             jnp.concatenate([zeros16, zeros16], axis=1)], axis=0)
                else:
                    qb_, kb_, e_bb = scaled(eb)
                    rhs_k = jnp.concatenate([jnp.concatenate([ka.astype(BF16), zeros16], axis=1),
                                             jnp.concatenate([zeros16, kb_.astype(BF16)], axis=1)], axis=0)
                dq2 = _dot(dm.astype(BF16), rhs_k)
                parts = [(ea, qa, ka, e_a, dq2[:, :HEAD_DIM], _dot(dm_t[0:CHUNK], qa.astype(BF16)))]
                if eb is not None:
                    parts.append((eb, qb_, kb_, e_bb, dq2[:, HEAD_DIM:],
                                  _dot(dm_t[CHUNK:2 * CHUNK], qb_.astype(BF16))))
                for entry, q_m, k_m, e_m, dq_m, dk_m in parts:
                    if entry == 0:
                        dq = dq + dq_m
                        dkk = dkk + dk_m
                    else:
                        dq = dq + e_m * dq_m
                        dkk = dkk + e_m * dk_m
                        dx_s[(1 + entry) * CHUNK:(2 + entry) * CHUNK, pl.ds(l0, HEAD_DIM)] = (
                            q_m * dq_m + k_m * dk_m).astype(BF16)
            dq_s[pl.ds(r0, CHUNK), :] = dq
            dkk_s[pl.ds(r0, CHUNK), :] = dkk
            return carry

        lax.fori_loop(0, cpb, local, 0, unroll=BACKWARD_UNROLL)

        dg_all = _dot(wexpt_ref[...], dx_s[...])
        for b in range(cpb):
            dg_s[b * CHUNK:(b + 1) * CHUNK, :] = dg_all[:, b * HEAD_DIM:(b + 1) * HEAD_DIM]
        t = jnp.where(valid, dg_s[...] / f - dkk_s[...], 0.0)
        dlb_ref[...] += jnp.sum(sn * t, axis=0, keepdims=True)
        dp_ref[0] = jnp.where(valid, dq_s[...], 0.0).astype(BF16)
        dp_ref[1] = ((1.0 - lb) * sg * sn * t).astype(BF16)

        @pl.when((pl.program_id(0) == N_HEADS - 1) & (step == n_rb - 1))
        def _():
            exchange.wait(dw_ref, rxw_ref, blob_ref, rxb_ref, send_sems, recv_sems)

    head_block = lambda seg: pl.BlockSpec((1, rb_rows, HEAD_DIM), lambda h, s: (seg, n_rb - 1 - s, h))
    row_block = pl.BlockSpec((rb_rows, HEAD_DIM), lambda h, s: (n_rb - 1 - s, h))
    return pl.pallas_call(
        body, name="hgrn_backward",
        grid=(N_HEADS, n_rb),
        in_specs=[
            head_block(0), head_block(1), head_block(2),
            row_block,
            pl.BlockSpec((1, cpb, HEAD_DIM, HEAD_DIM), lambda h, s: (h, n_rb - 1 - s, 0, 0)),
            pl.BlockSpec((1, 1, N_EXP * CHUNK, lanes), lambda h, s: (h, n_rb - 1 - s, 0, 0)),
            row_block,
            pl.BlockSpec((2, HEAD_DIM), lambda h, s: (0, h)),
            pl.BlockSpec((CHUNK, N_EXP * CHUNK), lambda h, s: (0, 0)),
            pl.BlockSpec((len(LEVEL_PAIRS), CHUNK, 2 * CHUNK), lambda h, s: (0, 0, 0)),
            ANY, ANY,
        ],
        out_specs=[
            pl.BlockSpec((3, rb_rows, HEAD_DIM), lambda h, s: (0, n_rb - 1 - s, h)),
            pl.BlockSpec((1, HEAD_DIM), lambda h, s: (0, h)),
            ANY, ANY,
        ],
        out_shape=[
            jax.ShapeDtypeStruct((3, rows, D_MODEL), BF16),
            jax.ShapeDtypeStruct((1, D_MODEL), F32),
            exchange.landing_w(), exchange.landing_blob(blob16),
        ],
        scratch_shapes=[
            pltpu.VMEM((HEAD_DIM, HEAD_DIM), F32),
            pltpu.VMEM((cpb, HEAD_DIM, HEAD_DIM), F32),
            pltpu.VMEM((cpb, HEAD_DIM, HEAD_DIM), F32),
            pltpu.VMEM((rb_rows, HEAD_DIM), F32),
            pltpu.VMEM((rb_rows, HEAD_DIM), F32),
            pltpu.VMEM((rb_rows, HEAD_DIM), BF16),
            pltpu.VMEM((rb_rows, HEAD_DIM), BF16),
            pltpu.VMEM((rb_rows, HEAD_DIM), F32),
            pltpu.VMEM((rb_rows, HEAD_DIM), F32),
            pltpu.VMEM((rb_rows, HEAD_DIM), F32),
            pltpu.VMEM((N_EXP * CHUNK, lanes), BF16),
            pltpu.VMEM((rb_rows, 2 * CHUNK), F32),
        ] + exchange.semaphores(),
        compiler_params=_params(("arbitrary", "arbitrary")),
    )(p3, p3, p3, d_o, states, e16, a2, lb_logits, wexp_t, masks2, dw16, blob16)


def _sigmoid(x):
    return 1.0 / (1.0 + jnp.exp(-x))


def _silu_and_grad(x):
    s = _sigmoid(x)
    return x * s, s * (1.0 + x * (1.0 - s))


def _window_sum(ext, width, forward_looking):
    n = ext.shape[0]
    s = ext
    step = 1
    while step < width:
        s = s + pltpu.roll(s, (n - step) if forward_looking else step, 0)
        step *= 2
    return s


def _mixers(o, p3, tokens, head, tgt, wdh, wdp, wout, poolw, hg_w, pool_scale, final_w, rows):
    tm = _tile(rows, 208)
    nt = rows // tm
    halo_blocks = tm // HALO
    n_grp = len(POOL_WINDOWS)
    q_rows = D_MODEL // N_CHIPS
    blob_rows = 3 * q_rows + n_grp * POOL_GDIM * POOL_GDIM // (N_CHIPS * D_MODEL)

    def body(o_ref, ghg_ref, u_ref, gpl_ref, mhg_ref, mpl_ref, uh_ref, z_ref, t_ref,
             wdh_ref, wdp_ref, wout_ref, pw_ref, hgw_ref, ps_ref, fw_ref, head_ref,
             do_ref, dz2_ref, dp_ref, blob_ref, dpw_ref, small_ref, carry_ref):
        step = pl.program_id(0)
        tile = nt - 1 - step

        def add_to_blob(piece, dw):
            for k in range(N_CHIPS):
                blob_ref[k, piece * q_rows:(piece + 1) * q_rows, :] += dw[k * q_rows:(k + 1) * q_rows]

        @pl.when(step == 0)
        def _():
            blob_ref[...] = jnp.zeros_like(blob_ref)
            dpw_ref[...] = jnp.zeros_like(dpw_ref)
            small_ref[...] = jnp.zeros_like(small_ref)
            carry_ref[...] = jnp.zeros_like(carry_ref)

        row = tile * tm + lax.broadcasted_iota(jnp.int32, (tm, 1), 0)
        real = row >= PAD_ROWS
        pos1 = jnp.maximum(row - PAD_ROWS + 1, 1).astype(F32)

        u = jnp.where(real, u_ref[0], 0.0)
        halo_row = tile * tm - HALO + lax.broadcasted_iota(jnp.int32, (HALO, 1), 0)
        uh = jnp.where(halo_row >= PAD_ROWS, uh_ref[0], 0.0)
        ext = jnp.concatenate([uh, u], axis=0)
        pooled, inv_cnt, mixed = [], [], []
        for g, w in enumerate(POOL_WINDOWS):
            cols = slice(g * POOL_GDIM, (g + 1) * POOL_GDIM)
            inv = 1.0 / jnp.minimum(pos1, float(w))
            ws = _window_sum(ext[:, cols], w, False)[HALO:]
            pg = (ws * inv - u[:, cols]).astype(BF16)
            pooled.append(pg)
            inv_cnt.append(inv)
            mixed.append(_dot(pg, pw_ref[g]))
        mixed = jnp.concatenate(mixed, axis=1)
        gpl = gpl_ref[0]
        sp, dsp = _silu_and_grad(gpl)
        ps = ps_ref[...]
        a_pool = (mixed * ps * sp).astype(BF16)
        y_pool = _dot(a_pool, wdp_ref[...])

        o = o_ref[...]
        o_hat, rstd_h = [], []
        for h in range(N_HEADS):
            oh = o[:, h * HEAD_DIM:(h + 1) * HEAD_DIM]
            r = lax.rsqrt(jnp.mean(oh * oh, axis=-1, keepdims=True) + EPS)
            rstd_h.append(r)
            o_hat.append(oh * r)
        o_hat = jnp.concatenate(o_hat, axis=1)
        hgw = hgw_ref[...]
        o_n = o_hat * hgw
        ghg = ghg_ref[0]
        sh, dsh = _silu_and_grad(ghg)
        a_hg = (o_n * sh).astype(BF16)
        y_hg = _dot(a_hg, wdh_ref[...])

        s_mh = _sigmoid(mhg_ref[0])
        s_mp = _sigmoid(mpl_ref[0])
        merged = (s_mh * y_hg + s_mp * y_pool).astype(BF16)
        z2 = _padded_tile(z_ref[...], head_ref[...], tile) + _dot(merged, wout_ref[...])
        rstd2 = lax.rsqrt(jnp.mean(z2 * z2, axis=-1, keepdims=True) + EPS)
        zh = z2 * rstd2
        fw = fw_ref[...]
        target = _padded_tile(t_ref[...], jnp.zeros((FIRST_TOKEN_ROW, D_MODEL), F32), tile)
        err = jnp.where(row >= FIRST_TOKEN_ROW, zh * fw - target, 0.0)
        small_ref[ROW_LOSS:ROW_LOSS + 1, :] += jnp.sum(err * err, axis=0, keepdims=True) * (0.5 / D_MODEL)
        dy = err * (1.0 / D_MODEL)

        small_ref[ROW_FINAL_W:ROW_FINAL_W + 1, :] += jnp.sum(dy * zh, axis=0, keepdims=True)
        uu = dy * fw
        dz2 = rstd2 * (uu - zh * jnp.mean(uu * zh, axis=-1, keepdims=True))
        dz2_ref[...] = dz2
        dz2_16 = dz2.astype(BF16)
        dmerged = _dot_nt(dz2_16, wout_ref[...])
        add_to_blob(2, _dot_tn(merged, dz2_16))
        dy_hg = (s_mh * dmerged).astype(BF16)
        dy_pool = (s_mp * dmerged).astype(BF16)
        dp_ref[3] = (dmerged * y_hg * s_mh * (1.0 - s_mh)).astype(BF16)
        dp_ref[4] = (dmerged * y_pool * s_mp * (1.0 - s_mp)).astype(BF16)

        da_hg = _dot_nt(dy_hg, wdh_ref[...])
        add_to_blob(0, _dot_tn(a_hg, dy_hg))
        dp_ref[0] = (da_hg * o_n * dsh).astype(BF16)
        do_n = da_hg * sh
        small_ref[ROW_HG_W:ROW_HG_W + 1, :] += jnp.sum(do_n * o_hat, axis=0, keepdims=True)
        d_hat = do_n * hgw
        for h in range(N_HEADS):
            cols = slice(h * HEAD_DIM, (h + 1) * HEAD_DIM)
            dh_, oh_ = d_hat[:, cols], o_hat[:, cols]
            do_ref[:, cols] = rstd_h[h] * (dh_ - oh_ * jnp.mean(dh_ * oh_, axis=-1, keepdims=True))

        da_pool = _dot_nt(dy_pool, wdp_ref[...])
        add_to_blob(1, _dot_tn(a_pool, dy_pool))
        small_ref[ROW_POOL_SCALE:ROW_POOL_SCALE + 1, :] += jnp.sum(da_pool * mixed * sp, axis=0, keepdims=True)
        dp_ref[2] = (da_pool * mixed * ps * dsp).astype(BF16)
        dmixed = (da_pool * ps * sp).astype(BF16)
        carry = carry_ref[...]
        du, new_carry = [], []
        for g, w in enumerate(POOL_WINDOWS):
            cols = slice(g * POOL_GDIM, (g + 1) * POOL_GDIM)
            dmg = dmixed[:, cols]
            dpooled = _dot_nt(dmg, pw_ref[g])
            dpw_ref[g] += _dot_tn(pooled[g], dmg)
            dps = dpooled * inv_cnt[g]
            ext_b = jnp.concatenate([dps, carry[:, cols]], axis=0)
            du.append(_window_sum(ext_b, w, True)[:tm] - dpooled)
            new_carry.append(dps[:HALO])
        dp_ref[1] = jnp.where(real, jnp.concatenate(du, axis=1), 0.0).astype(BF16)
        carry_ref[...] = jnp.concatenate(new_carry, axis=1)

    row_block = pl.BlockSpec((tm, D_MODEL), lambda s: (nt - 1 - s, 0))
    seg_block = lambda seg: pl.BlockSpec((1, tm, D_MODEL), lambda s: (seg, nt - 1 - s, 0))
    whole = pl.BlockSpec(memory_space=pltpu.VMEM)
    return pl.pallas_call(
        body, name="mixers",
        grid=(nt,),
        in_specs=[
            row_block, seg_block(3), seg_block(4), seg_block(5), seg_block(6), seg_block(7),
            pl.BlockSpec((1, HALO, D_MODEL),
                         lambda s: (4, jnp.maximum((nt - 1 - s) * halo_blocks - 1, 0), 0)),
            _token_window(tm, lambda s: nt - 1 - s), _token_window(tm, lambda s: nt - 1 - s),
            whole, whole, whole, whole, whole, whole, whole, whole,
        ],
        out_specs=[
            row_block, row_block,
            pl.BlockSpec((5, tm, D_MODEL), lambda s: (0, nt - 1 - s, 0)),
            whole, whole, whole,
        ],
        out_shape=[
            jax.ShapeDtypeStruct((rows, D_MODEL), F32),
            jax.ShapeDtypeStruct((rows, D_MODEL), F32),
            jax.ShapeDtypeStruct((5, rows, D_MODEL), BF16),
            jax.ShapeDtypeStruct((N_CHIPS, blob_rows, D_MODEL), F32),
            jax.ShapeDtypeStruct((n_grp, POOL_GDIM, POOL_GDIM), F32),
            jax.ShapeDtypeStruct((SMALL_ROWS, D_MODEL), F32),
        ],
        scratch_shapes=[pltpu.VMEM((HALO, D_MODEL), F32)],
        compiler_params=_params(("arbitrary",)),
    )(o, p3, p3, p3, p3, p3, p3, tokens, tgt, wdh, wdp, wout, poolw, hg_w, pool_scale, final_w, head)


def _seg_specs(tm, row_of, seg_of):
    def spec_a(*g):
        k = seg_of(*g)
        return (jnp.minimum(k, 2), jnp.where(k < 3, row_of(*g), 0), 0)

    def spec_b(*g):
        k = seg_of(*g)
        return (jnp.maximum(k - 3, 0), jnp.where(k >= 3, row_of(*g), 0), 0)

    return pl.BlockSpec((1, tm, D_MODEL), spec_a), pl.BlockSpec((1, tm, D_MODEL), spec_b)


def _in_proj_weight_grad(h, dp, rows, name):
    n_seg = dp.shape[0]
    tm = _tile(rows, 1040)
    nt = rows // tm
    half = D_MODEL // 2

    def body(h_ref, dp_ref, part_ref, part16_ref, db_ref, acc_ref, bacc_ref, stage_ref, land_ref,
             send_sems, recv_sems):
        k, i = pl.program_id(0), pl.program_id(1)
        x, y, c = lax.axis_index("x"), lax.axis_index("y"), lax.axis_index("c")

        def to_sibling(seg):
            return pltpu.make_async_remote_copy(
                src_ref=stage_ref.at[seg], dst_ref=land_ref.at[seg], send_sem=send_sems.at[seg],
                recv_sem=recv_sems.at[seg], device_id=(x, y, 1 - c), device_id_type=MESH)

        @pl.when(i == 0)
        def _():
            acc_ref[...] = jnp.zeros_like(acc_ref)
            bacc_ref[...] = jnp.zeros_like(bacc_ref)

        dpt = dp_ref[0]
        acc_ref[...] += _dot_tn(h_ref[...], dpt)
        bacc_ref[...] += jnp.sum(dpt.astype(F32), axis=0, keepdims=True)

        @pl.when(i == nt - 1)
        def _():
            db_ref[0] = bacc_ref[...]
            part_ref[k] = acc_ref[pl.ds(pl.multiple_of(c * half, half), half), :]
            stage_ref[k] = acc_ref[pl.ds(pl.multiple_of((1 - c) * half, half), half), :].astype(BF16)
            to_sibling(k).start()

        @pl.when((k == n_seg - 1) & (i == nt - 1))
        def _():
            for seg in range(n_seg):
                to_sibling(seg).wait_recv()
                total = part_ref[seg] + land_ref[seg].astype(F32)
                part_ref[seg] = total
                part16_ref[seg] = total.astype(BF16)
            for seg in range(n_seg):
                to_sibling(seg).wait_send()

    whole = pl.BlockSpec(memory_space=pltpu.VMEM)
    return pl.pallas_call(
        body, name=name,
        grid=(n_seg, nt),
        in_specs=[pl.BlockSpec((tm, D_MODEL), lambda k, i: (i, 0)),
                  pl.BlockSpec((1, tm, D_MODEL), lambda k, i: (k, i, 0))],
        out_specs=[whole, whole, pl.BlockSpec((1, 1, D_MODEL), lambda k, i: (k, 0, 0))],
        out_shape=[
            jax.ShapeDtypeStruct((n_seg, half, D_MODEL), F32),
            jax.ShapeDtypeStruct((n_seg, half, D_MODEL), BF16),
            jax.ShapeDtypeStruct((n_seg, 1, D_MODEL), F32),
        ],
        scratch_shapes=[
            pltpu.VMEM((D_MODEL, D_MODEL), F32), pltpu.VMEM((1, D_MODEL), F32),
            pltpu.VMEM((n_seg, half, D_MODEL), BF16),
            pltpu.VMEM((n_seg, half, D_MODEL), BF16),
            pltpu.SemaphoreType.DMA((n_seg,)), pltpu.SemaphoreType.DMA((n_seg,)),
        ],
        compiler_params=_params(("arbitrary", "arbitrary")),
    )(h, dp)


def _input_grad(dpa, dpb, w4, tokens, head, dz2, norm_w, dw16, rows):
    tm = _tile(rows, 1040)
    nt = rows // tm
    assert nt >= 2, rows
    exchange = _GradExchange(SEGS_REC, with_blob=False)

    def body(dpa_ref, dpb_ref, w_ref, z_ref, head_ref, dz2_ref, nw_ref, dw_ref, gx_ref, dmeta_ref, dnw_ref, rxw_ref,
             acc_ref, dz_buf, out_sem, send_sems, recv_sems):
        i, k = pl.program_id(0), pl.program_id(1)

        def first_tile_out():
            return pltpu.make_async_copy(dz_buf.at[pl.ds(FIRST_TOKEN_ROW, tm - FIRST_TOKEN_ROW), :],
                                         gx_ref.at[pl.ds(0, tm - FIRST_TOKEN_ROW), :], out_sem)

        def tile_out(tile):
            start = pl.multiple_of(tile * tm - FIRST_TOKEN_ROW, HALO)
            return pltpu.make_async_copy(dz_buf, gx_ref.at[pl.ds(start, tm), :], out_sem)

        @pl.when((i == 0) & (k == 0))
        def _():
            exchange.start(dw_ref, rxw_ref, None, None, send_sems, recv_sems)
            dnw_ref[...] = jnp.zeros_like(dnw_ref)

        @pl.when((i == nt - 1) & (k == N_SEG - 1))
        def _():
            exchange.wait(dw_ref, rxw_ref, None, None, send_sems, recv_sems)

        @pl.when(k == 0)
        def _():
            acc_ref[...] = jnp.zeros_like(acc_ref)

        @pl.when(k < 3)
        def _():
            acc_ref[...] += _dot_nt(dpa_ref[0], w_ref[0])

        @pl.when(k >= 3)
        def _():
            acc_ref[...] += _dot_nt(dpb_ref[0], w_ref[0])

        @pl.when(k == N_SEG - 1)
        def _():
            zt = _padded_tile(z_ref[...], head_ref[...], i)
            rstd = lax.rsqrt(jnp.mean(zt * zt, axis=-1, keepdims=True) + EPS)
            zh = zt * rstd
            dh = acc_ref[...]
            dnw_ref[...] += jnp.sum(dh * zh, axis=0, keepdims=True)
            uu = dh * nw_ref[...]
            dz = dz2_ref[...] + rstd * (uu - zh * jnp.mean(uu * zh, axis=-1, keepdims=True))

            @pl.when(i == 1)
            def _():
                first_tile_out().wait()

            @pl.when(i >= 2)
            def _():
                tile_out(i - 1).wait()

            dz_buf[...] = dz

            @pl.when(i == 0)
            def _():
                dmeta_ref[...] = dz[PAD_ROWS:FIRST_TOKEN_ROW]
                first_tile_out().start()

            @pl.when(i > 0)
            def _():
                tile_out(i).start()

            @pl.when(i == nt - 1)
            def _():
                tile_out(i).wait()

    spec_a, spec_b = _seg_specs(tm, lambda i, k: i, lambda i, k: k)
    last_only = pl.BlockSpec((tm, D_MODEL), lambda i, k: (jnp.where(k == N_SEG - 1, i, 0), 0))
    return pl.pallas_call(
        body, name="input_grad",
        grid=(nt, N_SEG),
        in_specs=[
            spec_a, spec_b,
            pl.BlockSpec((1, D_MODEL, D_MODEL), lambda i, k: (k // 2, 0, k % 2)),
            _token_window(tm, lambda i, k: jnp.where(k == N_SEG - 1, i, 0)),
            pl.BlockSpec((FIRST_TOKEN_ROW, D_MODEL), lambda i, k: (0, 0)),
            last_only,
            pl.BlockSpec((1, D_MODEL), lambda i, k: (0, 0)),
            ANY,
        ],
        out_specs=[
            ANY,
            pl.BlockSpec((N_META, D_MODEL), lambda i, k: (0, 0)),
            pl.BlockSpec((1, D_MODEL), lambda i, k: (0, 0)),
            ANY,
        ],
        out_shape=[
            jax.ShapeDtypeStruct((rows - FIRST_TOKEN_ROW, D_MODEL), F32),
            jax.ShapeDtypeStruct((N_META, D_MODEL), F32),
            jax.ShapeDtypeStruct((1, D_MODEL), F32),
            exchange.landing_w(),
        ],
        scratch_shapes=[pltpu.VMEM((tm, D_MODEL), F32), pltpu.VMEM((tm, D_MODEL), F32),
                        pltpu.SemaphoreType.DMA] + exchange.semaphores(),
        compiler_params=_params(("arbitrary", "arbitrary")),
    )(dpa, dpb, w4, tokens, head, dz2, norm_w, dw16)


def _local_step(tokens, m4, tgt, w4, blob4, seg_order, norm_w, b_in, lb_logits, hg_w, pool_scale, final_w):
    rows = FIRST_TOKEN_ROW + tokens.shape[0]
    q = D_MODEL // N_CHIPS
    n_grp = len(POOL_WINDOWS)
    pg = POOL_GDIM // N_CHIPS

    wexp2 = jnp.asarray(np.tile(_exponent_matrix(), (1, 2)), BF16)
    wexp_t = jnp.asarray(_exponent_matrix().T, BF16)
    masks2 = jnp.asarray(_paired_masks(), F32)

    h, p3, w4, head, _ = _in_proj(tokens, m4, norm_w, w4, b_in, seg_order, rows)
    o, states, e16, a2, blob4 = _hgrn_forward(p3, lb_logits, wexp2, masks2, blob4, rows)
    wdh = blob4[:, 0:q].reshape(D_MODEL, D_MODEL)
    wdp = blob4[:, q:2 * q].reshape(D_MODEL, D_MODEL)
    wout = blob4[:, 2 * q:3 * q].reshape(D_MODEL, D_MODEL)
    poolw = blob4[:, 3 * q:].reshape(N_CHIPS, n_grp, pg, POOL_GDIM).transpose(1, 0, 2, 3)
    poolw = poolw.reshape(n_grp, POOL_GDIM, POOL_GDIM)
    d_o, dz2, dpb, dblob4, dpw, small = _mixers(
        o, p3, tokens, head, tgt, wdh, wdp, wout, poolw, hg_w, pool_scale, final_w, rows)
    dpw4 = dpw.reshape(n_grp, N_CHIPS, pg, POOL_GDIM).transpose(1, 0, 2, 3)
    dpw4 = dpw4.reshape(N_CHIPS, n_grp * pg * POOL_GDIM // D_MODEL, D_MODEL)
    dblob4 = dblob4.at[:, 3 * q:, :].set(dpw4)

    dw_mix, dw_mix16, db_mix = _in_proj_weight_grad(h, dpb, rows, "in_proj_weight_grad_mix")
    dpa, dlb, rxw_mix, rx_blob = _hgrn_backward(
        p3, d_o, states, e16, a2, lb_logits, wexp_t, masks2, dw_mix16, dblob4.astype(BF16), rows)
    dw_rec, dw_rec16, db_rec = _in_proj_weight_grad(h, dpa, rows, "in_proj_weight_grad_rec")
    d_tokens, d_meta, dnw, rxw_rec = _input_grad(dpa, dpb, w4, tokens, head, dz2, norm_w, dw_rec16, rows)

    small = jnp.concatenate([
        small[ROW_LOSS:ROW_LOSS + 1],
        d_meta,
        dnw,
        db_rec.reshape(len(SEGS_REC), D_MODEL), db_mix.reshape(len(SEGS_MIX), D_MODEL),
        dlb, jnp.zeros_like(dlb),
        small[ROW_HG_W:ROW_HG_W + 1], small[ROW_POOL_SCALE:ROW_POOL_SCALE + 1],
        small[ROW_FINAL_W:ROW_FINAL_W + 1],
        jnp.zeros((SMALL_ROWS - ROW_FINAL_W - 1, D_MODEL), F32),
    ], axis=0)
    return d_tokens, (dw_rec, dw_mix, rxw_rec, rxw_mix), (dblob4, rx_blob), small


ANY = pl.BlockSpec(memory_space=pl.ANY)
MESH = pl.DeviceIdType.MESH


def _place():
    x, y, c = lax.axis_index("x"), lax.axis_index("y"), lax.axis_index("c")
    chips = [(1 - x, y), (x, 1 - y), (1 - x, 1 - y)]
    return x, y, c, chips


class _ShardGather:
    def __init__(self, rows):
        self.half = rows // 2

    def semaphores(self):
        return [pltpu.SemaphoreType.DMA((6,)), pltpu.SemaphoreType.DMA((6,))]

    def _copy(self, k, slot, to, send_sems, recv_sems):
        return pltpu.make_async_remote_copy(src_ref=slot, dst_ref=slot, send_sem=send_sems.at[k],
                                            recv_sem=recv_sems.at[k], device_id=to, device_id_type=MESH)

    def _half(self, ref4, chip, which):
        return ref4.at[chip, pl.ds(which * self.half, self.half), :]

    def start(self, ref4, send_sems, recv_sems, which=(0, 1, 2)):
        x, y, c, chips = _place()
        for j in which:
            cx, cy = chips[j]
            self._copy(j, self._half(ref4, 2 * x + y, c), (cx, cy, c), send_sems, recv_sems).start()

    def start_diagonal_after_neighbours(self, ref4, send_sems, recv_sems):
        x, y, c, chips = _place()
        for j in (0, 1):
            cx, cy = chips[j]
            self._copy(j, self._half(ref4, 2 * x + y, c), (cx, cy, c), send_sems, recv_sems).wait_send()
        self.start(ref4, send_sems, recv_sems, which=(2,))

    def pass_on(self, j, ref4, send_sems, recv_sems):
        x, y, c, chips = _place()
        cx, cy = chips[j]
        landed = self._half(ref4, 2 * cx + cy, c)
        self._copy(j, landed, (cx, cy, c), send_sems, recv_sems).wait_recv()
        self._copy(3 + j, landed, (x, y, 1 - c), send_sems, recv_sems).start()

    def await_sibling(self, j, ref4, send_sems, recv_sems):
        x, y, c, chips = _place()
        cx, cy = chips[j]
        self._copy(3 + j, self._half(ref4, 2 * cx + cy, 1 - c), (x, y, 1 - c), send_sems, recv_sems).wait_recv()

    def finish(self, ref4, send_sems, recv_sems, which=(0, 1, 2)):
        x, y, c, chips = _place()
        for j, (cx, cy) in enumerate(chips):
            if j in which:
                self._copy(j, self._half(ref4, 2 * x + y, c), (cx, cy, c), send_sems, recv_sems).wait_send()
            self._copy(3 + j, self._half(ref4, 2 * cx + cy, c), (x, y, 1 - c), send_sems, recv_sems).wait_send()


class _GradExchange:
    def __init__(self, segs, with_blob):
        self.segs = tuple(segs)
        self.with_blob = with_blob

    def landing_w(self):
        return jax.ShapeDtypeStruct((N_CHIPS, 2, D_MODEL // 2, D_MODEL), BF16)

    def landing_blob(self, blob16):
        return jax.ShapeDtypeStruct((N_DEV, blob16.shape[1] // 2, D_MODEL), BF16)

    def semaphores(self):
        n_send = len(self.segs) + (2 * N_CHIPS if self.with_blob else 0)
        n_recv = 2 * N_CHIPS + (N_DEV if self.with_blob else 0)
        return [pltpu.SemaphoreType.DMA((n_send,)), pltpu.SemaphoreType.DMA((n_recv,))]

    def _copies(self, dw_ref, rxw_ref, blob_ref, rxb_ref, send_sems, recv_sems):
        x, y, c = lax.axis_index("x"), lax.axis_index("y"), lax.axis_index("c")
        chip = 2 * x + y

        def relation(kx, ky, h):
            return (x ^ kx) * 4 + (y ^ ky) * 2 + (c ^ h)

        def copy(src, dst, send_k, recv_k, to):
            return functools.partial(pltpu.make_async_remote_copy, src_ref=src, dst_ref=dst,
                                     send_sem=send_sems.at[send_k], recv_sem=recv_sems.at[recv_k],
                                     device_id=to, device_id_type=MESH)

        sends, recvs = [], []
        for i, s in enumerate(self.segs):
            kx, ky = (s // 2) >> 1, (s // 2) & 1
            r = (x ^ kx) * 2 + (y ^ ky)
            sends.append((r != 0, copy(dw_ref.at[i], rxw_ref.at[r, s % 2], i, 2 * r + s % 2, (kx, ky, c))))
        for j in range(2):
            mine = [s // 2 for s in self.segs if s % 2 == j]
            if mine:
                cond = functools.reduce(lambda a, b: a | b, [chip == k for k in mine])
                for r in range(1, N_CHIPS):
                    slot = rxw_ref.at[r, j]
                    recvs.append((cond, copy(slot, slot, 0, 2 * r + j, (x, y, c))))
        if self.with_blob:
            hb = blob_ref.shape[1] // 2
            first_send, first_recv = len(self.segs), 2 * N_CHIPS
            for k in range(N_CHIPS):
                for h in range(2):
                    r = relation(k >> 1, k & 1, h)
                    sends.append((r != 0, copy(blob_ref.at[k, pl.ds(h * hb, hb), :], rxb_ref.at[r],
                                               first_send + 2 * k + h, first_recv + r, (k >> 1, k & 1, h))))
            for r in range(1, N_DEV):
                slot = rxb_ref.at[r]
                recvs.append((None, copy(slot, slot, 0, first_recv + r, (x, y, c))))
        return sends, recvs

    def start(self, *refs):
        sends, _ = self._copies(*refs)
        for cond, make in sends:
            pl.when(cond)(lambda make=make: make().start())

    def wait(self, *refs):
        sends, recvs = self._copies(*refs)
        for cond, make in sends:
            pl.when(cond)(lambda make=make: make().wait_send())
        for cond, make in recvs:
            if cond is None:
                make().wait_recv()
            else:
                pl.when(cond)(lambda make=make: make().wait_recv())


def _sum_landed(own, rx_ref):
    total = own
    for r in range(1, rx_ref.shape[0]):
        total = total + rx_ref[r, 0].astype(F32)
    return total


def _finish_w(dw_rec, dw_mix, rx_rec, rx_mix, place_arr):
    half = D_MODEL // 2
    tm = _tile(half, 256)
    n_rec = len(SEGS_REC)

    def body(place_ref, own_rec_ref, own_mix_ref, rx_rec_ref, rx_mix_ref, out_ref):
        seg = 2 * place_ref[0] + pl.program_id(0)

        @pl.when(seg < n_rec)
        def _():
            out_ref[0] = _sum_landed(own_rec_ref[0], rx_rec_ref)

        @pl.when(seg >= n_rec)
        def _():
            out_ref[0] = _sum_landed(own_mix_ref[0], rx_mix_ref)

    def own_spec(first, count):
        def index(j, i, place_ref):
            seg = 2 * place_ref[0] + j
            return (jnp.clip(seg - first, 0, count - 1), i, 0)
        return pl.BlockSpec((1, tm, D_MODEL), index)

    rx_spec = pl.BlockSpec((N_CHIPS, 1, tm, D_MODEL), lambda j, i, place_ref: (0, j, i, 0))
    return pl.pallas_call(
        body, name="finish_w",
        grid_spec=pltpu.PrefetchScalarGridSpec(
            num_scalar_prefetch=1, grid=(2, half // tm),
            in_specs=[own_spec(0, n_rec), own_spec(n_rec, len(SEGS_MIX)), rx_spec, rx_spec],
            out_specs=pl.BlockSpec((1, tm, D_MODEL), lambda j, i, place_ref: (place_ref[1], i, j))),
        out_shape=jax.ShapeDtypeStruct((2, half, 2 * D_MODEL), F32),
        compiler_params=_params(("arbitrary", "arbitrary")),
    )(place_arr, dw_rec, dw_mix, rx_rec, rx_mix)


def _finish_blob(dblob4, rx_blob, place_arr):
    n, rows, cols = rx_blob.shape
    tm = _tile(rows, 256)

    def body(place_ref, own_ref, rx_ref, out_ref):
        out_ref[0] = _sum_landed(own_ref[0, 0], rx_ref)

    return pl.pallas_call(
        body, name="finish_blob",
        grid_spec=pltpu.PrefetchScalarGridSpec(
            num_scalar_prefetch=1, grid=(rows // tm,),
            in_specs=[pl.BlockSpec((1, 1, tm, cols), lambda i, place_ref: (place_ref[0], place_ref[1], i, 0)),
                      pl.BlockSpec((n, 1, tm, cols), lambda i, place_ref: (0, 0, i, 0))],
            out_specs=pl.BlockSpec((1, tm, cols), lambda i, place_ref: (place_ref[1], i, 0))),
        out_shape=jax.ShapeDtypeStruct((2, rows, cols), F32),
        compiler_params=_params(("arbitrary",)),
    )(place_arr, dblob4.reshape(N_CHIPS, 2, rows, cols), rx_blob.reshape(n, 1, rows, cols))


def _share_finished(fw2, fb2, small):
    def body(w_in_ref, b_in_ref, small_ref, w_ref, b_ref, s_ref, bounce, local_sem, send_sems, recv_sems):
        x, y, c, _ = _place()
        sibling = (x, y, 1 - c)

        def copy(k, src, dst, to):
            return pltpu.make_async_remote_copy(src_ref=src, dst_ref=dst, send_sem=send_sems.at[k],
                                                recv_sem=recv_sems.at[k], device_id=to, device_id_type=MESH)

        sends = [copy(0, w_ref.at[c], w_ref.at[c], sibling), copy(1, b_ref.at[c], b_ref.at[c], sibling)]
        for r in range(1, N_DEV):
            peer = (x ^ ((r >> 2) & 1), y ^ ((r >> 1) & 1), c ^ (r & 1))
            sends.append(copy(1 + r, small_ref, s_ref.at[r], peer))
        for cp in sends:
            cp.start()
        for src, dst in ((small_ref, bounce), (bounce, s_ref.at[0])):
            own = pltpu.make_async_copy(src, dst, local_sem)
            own.start()
            own.wait()
        landed = [w_ref.at[1 - c], b_ref.at[1 - c]] + [s_ref.at[r] for r in range(1, N_DEV)]
        for k, slot in enumerate(landed):
            copy(k, slot, slot, (x, y, c)).wait_recv()
        for cp in sends:
            cp.wait_send()

    same = lambda a: jax.ShapeDtypeStruct(a.shape, a.dtype)
    n_sem = 2 + N_DEV - 1
    return pl.pallas_call(
        body, name="share_finished",
        in_specs=[ANY, ANY, ANY], out_specs=[ANY, ANY, ANY],
        out_shape=[same(fw2), same(fb2), jax.ShapeDtypeStruct((N_DEV,) + small.shape, F32)],
        input_output_aliases={0: 0, 1: 1},
        scratch_shapes=[pltpu.VMEM(small.shape, F32), pltpu.SemaphoreType.DMA,
                        pltpu.SemaphoreType.DMA((n_sem,)), pltpu.SemaphoreType.DMA((n_sem,))],
    )(fw2, fb2, small)


def _sum_small(slots, lb_logits, me_arr):
    def body(me_ref, slots_ref, lbl_ref, out_ref):
        me = me_ref[0]
        total = slots_ref[me]
        for d in range(1, N_DEV):
            total = total + slots_ref[d ^ me]
        out_ref[...] = total
        out_ref[ROW_LOSS:ROW_LOSS + 1, :] = jnp.broadcast_to(
            jnp.sum(total[ROW_LOSS:ROW_LOSS + 1, :], axis=-1, keepdims=True), (1, D_MODEL))
        lb = _lower_bound(lbl_ref[...])
        g0 = total[ROW_LB:ROW_LB + 1, :] * lb * (1.0 - lb)
        out_ref[ROW_LB:ROW_LB + 1, :] = g0
        out_ref[ROW_LB + 1:ROW_LB + 2, :] = -g0

    return pl.pallas_call(
        body, name="sum_small",
        grid_spec=pltpu.PrefetchScalarGridSpec(
            num_scalar_prefetch=1, grid=(1,),
            in_specs=[pl.BlockSpec((N_DEV, SMALL_ROWS, D_MODEL), lambda i, me_ref: (0, 0, 0)),
                      pl.BlockSpec((2, D_MODEL), lambda i, me_ref: (0, 0))],
            out_specs=pl.BlockSpec((SMALL_ROWS, D_MODEL), lambda i, me_ref: (0, 0))),
        out_shape=jax.ShapeDtypeStruct((SMALL_ROWS, D_MODEL), F32),
        compiler_params=_params(("arbitrary",)),
    )(me_arr, slots, lb_logits)


def _adamw_step(w, g, m, v):
    c1 = 1.0 / (1.0 - ADAM_B1 ** ADAM_STEP)
    c2 = 1.0 / (1.0 - ADAM_B2 ** ADAM_STEP)
    nm = ADAM_B1 * m + (1.0 - ADAM_B1) * g
    nv = ADAM_B2 * v + (1.0 - ADAM_B2) * (g * g)
    return -ADAM_LR * ((nm * c1) / (jnp.sqrt(nv * c2) + ADAM_EPS) + ADAM_WD * w), nm, nv


SMALL_PARAMS = (("norm_w", ROW_NORM_W, 1), ("b_in", ROW_B_IN, N_SEG), ("lb_logits", ROW_LB, 2),
                ("hg_norm_w", ROW_HG_W, 1), ("pool_scale", ROW_POOL_SCALE, 1), ("final_norm_w", ROW_FINAL_W, 1))


def _update_small(tot, triples):
    n = len(SMALL_PARAMS)

    def body(tot_ref, *refs):
        ins, outs = refs[:3 * n], refs[3 * n:]
        for p, (name, row, n_rows) in enumerate(SMALL_PARAMS):
            w_ref, m_ref, v_ref = ins[3 * p:3 * p + 3]
            g_ref, d_ref, nm_ref, nv_ref = outs[4 * p:4 * p + 4]
            if w_ref.shape[0] == n_rows:
                pieces = [(slice(None), slice(None), tot_ref[row:row + n_rows, :])]
            else:
                pieces = [(slice(None), slice(k * D_MODEL, (k + 1) * D_MODEL), tot_ref[row + k:row + k + 1, :])
                          for k in range(n_rows)]
            for rows_, cols_, g in pieces:
                d, nm, nv = _adamw_step(w_ref[rows_, cols_], g, m_ref[rows_, cols_], v_ref[rows_, cols_])
                g_ref[rows_, cols_] = g
                d_ref[rows_, cols_] = d
                nm_ref[rows_, cols_] = nm
                nv_ref[rows_, cols_] = nv

    whole = pl.BlockSpec(memory_space=pltpu.VMEM)
    flat = [a for t in triples for a in t]
    out_shape = [jax.ShapeDtypeStruct(t[0].shape, F32) for t in triples for _ in range(4)]
    outs = pl.pallas_call(
        body, name="update_small",
        in_specs=[whole] * (1 + len(flat)), out_specs=[whole] * len(out_shape), out_shape=out_shape,
        compiler_params=_params(),
    )(tot, *flat)
    return [tuple(outs[4 * p:4 * p + 4]) for p in range(n)]


def _update_blob(g_blob, triples):
    q_rows = triples[0][0].shape[0]
    pool_rows = triples[3][0].shape[0]
    steps = q_rows // pool_rows

    def body(*refs):
        ins, outs = refs[:16], refs[16:]
        for p in range(4):
            g_ref, (w_ref, m_ref, v_ref) = ins[p], ins[4 + 3 * p:7 + 3 * p]
            go_ref, d_ref, nm_ref, nv_ref = outs[4 * p:4 * p + 4]

            def update():
                g = g_ref[...]
                go_ref[...] = g
                d_ref[...], nm_ref[...], nv_ref[...] = _adamw_step(w_ref[...], g, m_ref[...], v_ref[...])

            if p < 3:
                update()
            else:
                pl.when(pl.program_id(0) == 0)(update)

    blk = pl.BlockSpec((pool_rows, D_MODEL), lambda i: (i, 0))
    once = pl.BlockSpec((pool_rows, D_MODEL), lambda i: (0, 0))
    g_specs = [pl.BlockSpec((pool_rows, D_MODEL), lambda i, p=p: (steps * p + i, 0)) for p in range(3)]
    g_specs.append(pl.BlockSpec((pool_rows, D_MODEL), lambda i: (3 * steps, 0)))
    piece_specs = [blk] * 9 + [once] * 3
    out_specs = [blk] * 12 + [once] * 4
    out_shape = [jax.ShapeDtypeStruct(t[0].shape, F32) for t in triples for _ in range(4)]
    outs = pl.pallas_call(
        body, name="update_blob",
        grid=(steps,), in_specs=g_specs + piece_specs, out_specs=out_specs, out_shape=out_shape,
        compiler_params=_params(("arbitrary",)),
    )(g_blob, g_blob, g_blob, g_blob, *[a for t in triples for a in t])
    return [tuple(outs[4 * p:4 * p + 4]) for p in range(4)]


def _adamw(w, g, m, v):
    rows, cols = w.shape
    tm = _tile(rows, 256, mult=8) if rows % 8 == 0 else rows

    def body(w_ref, g_ref, m_ref, v_ref, d_ref, nm_ref, nv_ref):
        d_ref[...], nm_ref[...], nv_ref[...] = _adamw_step(w_ref[...], g_ref[...], m_ref[...], v_ref[...])

    blk = pl.BlockSpec((tm, cols), lambda i: (i, 0))
    sds = jax.ShapeDtypeStruct((rows, cols), F32)
    return pl.pallas_call(
        body, name="adamw",
        grid=(rows // tm,), in_specs=[blk] * 4, out_specs=[blk] * 3, out_shape=[sds] * 3,
        compiler_params=_params(("arbitrary",)),
    )(w, g, m, v)


def kernel(x, meta_tokens, norm_w, w_in, b_in, lb_logits, hg_norm_w, pool_w, pool_scale, w_down_hg, w_down_pool, w_out, final_norm_w, loss_target, m_meta_tokens, m_norm_w, m_w_in, m_b_in, m_lb_logits, m_hg_norm_w, m_pool_w, m_pool_scale, m_w_down_hg, m_w_down_pool, m_w_out, m_final_norm_w, v_meta_tokens, v_norm_w, v_w_in, v_b_in, v_lb_logits, v_hg_norm_w, v_pool_w, v_pool_scale, v_w_down_hg, v_w_down_pool, v_w_out, v_final_norm_w):
    seq = x.shape[1]
    xi, yi, ci = lax.axis_index("x"), lax.axis_index("y"), lax.axis_index("c")
    chip = 2 * xi + yi
    place_arr = jnp.stack([chip, ci]).astype(jnp.int32)
    me_arr = jnp.reshape(4 * xi + 2 * yi + ci, (1,)).astype(jnp.int32)
    q = D_MODEL // N_CHIPS

    def blob_of(wdh, wdp, wo, pw):
        return jnp.concatenate([wdh[0], wdp[0], wo[0], pw[0].reshape(-1, D_MODEL)], axis=0)

    def in_every_slot(a):
        return jnp.broadcast_to(a[None], (N_CHIPS,) + a.shape)

    m4 = in_every_slot(meta_tokens)
    w4 = in_every_slot(w_in[0].astype(BF16))
    blob4 = in_every_slot(blob_of(w_down_hg, w_down_pool, w_out, pool_w).astype(BF16))
    seg_order = jnp.stack([2 * (chip ^ rel) + t for rel in (0, 2, 1, 3) for t in (0, 1)]).astype(jnp.int32)

    fw2 = final_norm_w.reshape(1, D_MODEL)
    d_tokens, w_parts, blob_parts, small = _local_step(
        x[0], m4, loss_target[0], w4, blob4, seg_order, norm_w, b_in, lb_logits, hg_norm_w, pool_scale, fw2)
    grad_x = d_tokens[None]

    fin_w = _finish_w(*w_parts, place_arr)
    fin_b = _finish_blob(*blob_parts, place_arr)
    gw2, gb2, slots = _share_finished(fin_w, fin_b, small)
    tot = _sum_small(slots, lb_logits, me_arr)
    g_w_in = gw2.reshape(D_MODEL, 2 * D_MODEL)
    g_blob = gb2.reshape(-1, D_MODEL)

    d_win, nm_win, nv_win = _adamw(w_in[0], g_w_in, m_w_in[0], v_w_in[0])
    pool_rows = lambda a: a[0].reshape(-1, D_MODEL)
    blob_results = _update_blob(g_blob, [
        (w_down_hg[0], m_w_down_hg[0], v_w_down_hg[0]), (w_down_pool[0], m_w_down_pool[0], v_w_down_pool[0]),
        (w_out[0], m_w_out[0], v_w_out[0]), (pool_rows(pool_w), pool_rows(m_pool_w), pool_rows(v_pool_w))])
    g_meta = lax.dynamic_slice_in_dim(tot[ROW_META:ROW_META + N_META], chip * q, q, axis=1)
    d_meta, nm_meta, nv_meta = _adamw(meta_tokens, g_meta, m_meta_tokens, v_meta_tokens)

    as_row = lambda a: a.reshape(1, D_MODEL)
    small_results = _update_small(tot, [
        (norm_w, m_norm_w, v_norm_w), (b_in, m_b_in, v_b_in), (lb_logits, m_lb_logits, v_lb_logits),
        (hg_norm_w, m_hg_norm_w, v_hg_norm_w), (pool_scale, m_pool_scale, v_pool_scale),
        (as_row(final_norm_w), as_row(m_final_norm_w), as_row(v_final_norm_w))])

    def leaves(kind, meta_part, win_part):
        nw, bi, lbl, hg, ps, fw = [r[kind] for r in small_results]
        wdh, wdp, wo, pw = [r[kind] for r in blob_results]
        return [meta_part, nw, win_part[None], bi, lbl, hg, pw.reshape(pool_w.shape), ps,
                wdh[None], wdp[None], wo[None], fw.reshape(D_MODEL)]

    loss = tot[ROW_LOSS, 0]
    return (loss, grad_x,
            *leaves(0, g_meta, g_w_in),
            *leaves(1, d_meta, d_win),
            *leaves(2, nm_meta, nm_win),
            *leaves(3, nv_meta, nv_win))
```

```python
import functools

import numpy as np
import jax
import jax.numpy as jnp
from jax import lax
from jax.experimental import pallas as pl
from jax.experimental.pallas import tpu as pltpu

F32 = jnp.float32
BF16 = jnp.bfloat16

D_MODEL = 1024
N_SEG = 8
N_HEADS = 8
HEAD_DIM = 128
CHUNK = 64
N_META = 16
PAD_ROWS = CHUNK - N_META
FIRST_TOKEN_ROW = CHUNK
LEVELS = (32, 16, 8, 4, 2, 1)
N_EXP = 2 + len(LEVELS)
POOL_WINDOWS = (2, 4, 8, 16)
POOL_GDIM = D_MODEL // len(POOL_WINDOWS)
HALO = 16
FORWARD_HEADS_PER_STEP = 4
LOCAL_UNROLL = 13
BACKWARD_UNROLL = 13
EPS = 1e-6
N_CHIPS = 4
N_DEV = 8
SEGS_REC = (0, 1, 2)
SEGS_MIX = (3, 4, 5, 6, 7)

ADAM_LR = 0.001
ADAM_B1 = 0.9
ADAM_B2 = 0.999
ADAM_EPS = 1e-08
ADAM_WD = 0.01
ADAM_STEP = 10

VMEM_LIMIT_BYTES = 56 * 1024 * 1024

ROW_LOSS = 0
ROW_META = 1
ROW_NORM_W = ROW_META + N_META
ROW_B_IN = ROW_NORM_W + 1
ROW_LB = ROW_B_IN + N_SEG
ROW_HG_W = ROW_LB + 2
ROW_POOL_SCALE = ROW_HG_W + 1
ROW_FINAL_W = ROW_POOL_SCALE + 1
SMALL_ROWS = 32


def _tile(total, cap, mult=16):
    best = None
    for t in range(mult, min(total, cap) + 1, mult):
        if total % t == 0:
            best = t
    assert best is not None, (total, cap, mult)
    return best


def _token_window(tm, tile_of):
    def index(*grid):
        return (pl.multiple_of(jnp.maximum(tile_of(*grid) * tm - FIRST_TOKEN_ROW, 0), HALO), 0)
    return pl.BlockSpec((pl.Element(tm), pl.Element(D_MODEL)), index)


def _padded_tile(window, head, tile):
    first = jnp.concatenate([head, pltpu.roll(window, FIRST_TOKEN_ROW, 0)[FIRST_TOKEN_ROW:]], axis=0)
    return jnp.where(tile == 0, first, window)


def _params(sem=None):
    return pltpu.CompilerParams(dimension_semantics=sem, vmem_limit_bytes=VMEM_LIMIT_BYTES)


def _dot(a, b):
    return jnp.dot(a, b, preferred_element_type=F32)


def _dot_nt(a, b):
    return lax.dot_general(a, b, (((1,), (1,)), ((), ())), preferred_element_type=F32)


def _dot_tn(a, b):
    return lax.dot_general(a, b, (((0,), (0,)), ((), ())), preferred_element_type=F32)


def _sigmoid_pair(x):
    t = jnp.exp(-jnp.abs(x))
    r = 1.0 / (1.0 + t)
    pos = x >= 0
    return jnp.where(pos, r, t * r), jnp.where(pos, t * r, r)


def _exponent_matrix():
    t = np.arange(CHUNK)[:, None]
    j = np.arange(CHUNK)[None, :]
    blocks = [j <= t, j > t]
    for m in LEVELS:
        rho = (t // (2 * m)) * (2 * m) + m
        upper = (t >= rho) & (j > rho) & (j <= t)
        lower = (t < rho) & (j > t) & (j <= rho)
        blocks.append(upper | lower)
    return np.concatenate(blocks, axis=0).astype(np.float32)


def _pair_masks():
    t = np.arange(CHUNK)[:, None]
    s = np.arange(CHUNK)[None, :]
    masks = [t == s]
    for m in LEVELS:
        same = (t // (2 * m)) == (s // (2 * m))
        masks.append(same & ((t % (2 * m)) >= m) & ((s % (2 * m)) < m))
    return np.stack(masks).astype(np.float32)


LEVEL_PAIRS = ((0, 1), (2, 3), (4, 5), (6, None))


def _paired_masks():
    m = _pair_masks()
    zero = np.zeros_like(m[0])
    return np.stack([np.concatenate([m[a], zero if b is None else m[b]], axis=1) for a, b in LEVEL_PAIRS])


def _lower_bound(lbl):
    return 1.0 / (1.0 + jnp.exp(lbl[1:2, :] - lbl[0:1, :]))


def _in_proj(tokens, m4, norm_w, w4, b_in, seg_order, rows):
    tm = _tile(rows, 1040)
    nt = rows // tm
    gather = _ShardGather(w4.shape[1])

    def body(order_ref, z_ref, nw_ref, b_ref, w_in_ref, m_in_ref, h_ref, p_ref, w4_ref, head_ref, m4_ref,
             h_all, w_buf, w_sem, send_sems, recv_sems, meta_send, meta_recv, meta_sem):
        kk, i = pl.program_id(0), pl.program_id(1)

        @pl.when((kk == 0) & (i == 0))
        def _():
            x, y, c, chips = _place()

            def meta_copy(j, chip, to):
                return pltpu.make_async_remote_copy(
                    src_ref=m4_ref.at[chip], dst_ref=m4_ref.at[chip], send_sem=meta_send.at[j],
                    recv_sem=meta_recv.at[j], device_id=to, device_id_type=MESH)

            sends = [meta_copy(j, 2 * x + y, (cx, cy, c)) for j, (cx, cy) in enumerate(chips)]
            for cp in sends:
                cp.start()
            gather.start(w4_ref, send_sems, recv_sems, which=(0, 1))
            for j, (cx, cy) in enumerate(chips):
                meta_copy(j, 2 * cx + cy, (x, y, c)).wait_recv()
            for cp in sends:
                cp.wait_send()
            head_ref[0:PAD_ROWS, :] = jnp.zeros((PAD_ROWS, D_MODEL), F32)
            q_cols = D_MODEL // N_CHIPS
            for k in range(N_CHIPS):
                cp = pltpu.make_async_copy(
                    m4_ref.at[k], head_ref.at[pl.ds(PAD_ROWS, N_META), pl.ds(k * q_cols, q_cols)], meta_sem)
                cp.start()
                cp.wait()

        @pl.when((kk == 2) & (i == 0))
        def _():
            gather.start_diagonal_after_neighbours(w4_ref, send_sems, recv_sems)

        @pl.when(kk == 0)
        def _():
            zt = _padded_tile(z_ref[...], head_ref[...], i)
            rstd = lax.rsqrt(jnp.mean(zt * zt, axis=-1, keepdims=True) + EPS)
            h = (zt * rstd * nw_ref[...]).astype(BF16)
            h_all[pl.ds(pl.multiple_of(i * tm, 16), tm), :] = h
            h_ref[...] = h

        @pl.when((kk == 2) & (i == 0))
        def _():
            gather.pass_on(0, w4_ref, send_sems, recv_sems)
            gather.pass_on(1, w4_ref, send_sems, recv_sems)
            gather.await_sibling(0, w4_ref, send_sems, recv_sems)

        @pl.when((kk == 4) & (i == 0))
        def _():
            gather.await_sibling(1, w4_ref, send_sems, recv_sems)

        @pl.when((kk == 5) & (i == 0))
        def _():
            gather.pass_on(2, w4_ref, send_sems, recv_sems)

        @pl.when((kk == 6) & (i == 0))
        def _():
            gather.await_sibling(2, w4_ref, send_sems, recv_sems)

        def weights(which):
            seg = order_ref[2 * (kk // 2) + which]
            return pltpu.make_async_copy(
                w4_ref.at[seg // 2, :, pl.ds(pl.multiple_of((seg % 2) * D_MODEL, D_MODEL), D_MODEL)],
                w_buf.at[which], w_sem.at[which])

        @pl.when((i == 0) & (kk % 2 == 0))
        def _():
            weights(0).start()
            weights(1).start()
            weights(0).wait()

        @pl.when((i == 0) & (kk % 2 == 1))
        def _():
            weights(1).wait()

        p_ref[0] = _dot(h_all[pl.ds(pl.multiple_of(i * tm, 16), tm), :], w_buf[kk % 2]) + b_ref[...]

        @pl.when((kk == N_SEG - 1) & (i == nt - 1))
        def _():
            gather.finish(w4_ref, send_sems, recv_sems, which=(2,))

    first_pass = lambda kk, i, order_ref: (jnp.where(kk == 0, i, nt - 1), 0)
    return pl.pallas_call(
        body, name="in_proj",
        grid_spec=pltpu.PrefetchScalarGridSpec(
            num_scalar_prefetch=1, grid=(N_SEG, nt),
            in_specs=[
                _token_window(tm, lambda kk, i, order_ref: jnp.where(kk == 0, i, nt - 1)),
                pl.BlockSpec((1, D_MODEL), lambda kk, i, order_ref: (0, 0)),
                pl.BlockSpec((1, D_MODEL), lambda kk, i, order_ref: (0, order_ref[kk])),
                ANY, ANY,
            ],
            out_specs=[
                pl.BlockSpec((tm, D_MODEL), first_pass),
                pl.BlockSpec((1, tm, D_MODEL), lambda kk, i, order_ref: (order_ref[kk], i, 0)),
                ANY,
                pl.BlockSpec((FIRST_TOKEN_ROW, D_MODEL), lambda kk, i, order_ref: (0, 0)),
                ANY,
            ],
            scratch_shapes=[
                pltpu.VMEM((rows, D_MODEL), BF16),
                pltpu.VMEM((2, D_MODEL, D_MODEL), BF16),
                pltpu.SemaphoreType.DMA((2,)),
            ] + gather.semaphores() + [
                pltpu.SemaphoreType.DMA((N_CHIPS - 1,)), pltpu.SemaphoreType.DMA((N_CHIPS - 1,)),
                pltpu.SemaphoreType.DMA,
            ]),
        out_shape=[
            jax.ShapeDtypeStruct((rows, D_MODEL), BF16),
            jax.ShapeDtypeStruct((N_SEG, rows, D_MODEL), F32),
            jax.ShapeDtypeStruct(w4.shape, w4.dtype),
            jax.ShapeDtypeStruct((FIRST_TOKEN_ROW, D_MODEL), F32),
            jax.ShapeDtypeStruct(m4.shape, m4.dtype),
        ],
        input_output_aliases={4: 2, 5: 4},
        compiler_params=_params(("arbitrary", "arbitrary")),
    )(seg_order, tokens, norm_w, b_in, w4, m4)


def _hgrn_forward(p3, lb_logits, wexp2, masks2, blob4, rows):
    n_chunks = rows // CHUNK
    cpb = _tile(n_chunks, 13, mult=1)
    rb_rows = cpb * CHUNK
    n_rb = n_chunks // cpb
    lanes = cpb * HEAD_DIM
    hps = FORWARD_HEADS_PER_STEP
    n_hb = N_HEADS // hps
    width = hps * HEAD_DIM
    gather = _ShardGather(blob4.shape[1])

    def body(q_ref, fz_ref, v_ref, lbl_ref, wexp_ref, mask_ref, b_in_ref, o_ref, s_ref, e16_ref, a2_ref, b4_ref,
             st_all, e_all, u_all, q_all, kk_all, v_all, send_sems, recv_sems):
        rb = pl.program_id(1)

        @pl.when((pl.program_id(0) == 0) & (rb == 0))
        def _():
            gather.start(b4_ref, send_sems, recv_sems)

        @pl.when(rb == 0)
        def _():
            st_all[...] = jnp.zeros_like(st_all)

        for j in range(hps):
            cols = pl.ds(j * HEAD_DIM, HEAD_DIM)
            one_head(rb, q_ref.at[0, :, cols], fz_ref.at[0, :, cols], v_ref.at[0, :, cols], lbl_ref.at[:, cols],
                     wexp_ref, mask_ref, o_ref.at[:, cols], s_ref.at[j], e16_ref.at[j, 0], a2_ref.at[:, cols],
                     st_all.at[j], e_all.at[j], u_all.at[j], q_all.at[j], kk_all.at[j], v_all.at[j])

        @pl.when((pl.program_id(0) == n_hb // 2) & (rb == 0))
        def _():
            for j in range(N_CHIPS - 1):
                gather.pass_on(j, b4_ref, send_sems, recv_sems)

        @pl.when((pl.program_id(0) == n_hb - 1) & (rb == n_rb - 1))
        def _():
            for j in range(N_CHIPS - 1):
                gather.await_sibling(j, b4_ref, send_sems, recv_sems)
            gather.finish(b4_ref, send_sems, recv_sems)

    def one_head(rb, q_ref, fz_ref, v_ref, lbl_ref, wexp_ref, mask_ref, o_ref, s_ref, e16_ref, a2_ref,
                 st_ref, e_ref, u_ref, q_s, kk_s, v_s):
        lb = _lower_bound(lbl_ref[...])
        row = rb * rb_rows + lax.broadcasted_iota(jnp.int32, (rb_rows, 1), 0)
        valid = row >= PAD_ROWS
        sg, sn = _sigmoid_pair(fz_ref[...])
        g = jnp.where(valid, jnp.log(lb + (1.0 - lb) * sg), 0.0)
        kk_s[...] = jnp.where(valid, (1.0 - lb) * sn, 0.0)
        q_s[...] = jnp.where(valid, q_ref[...], 0.0)
        v_s[...] = jnp.where(valid, v_ref[...], 0.0).astype(BF16)
        hi = g.astype(BF16)
        mid = (g - hi.astype(F32)).astype(BF16)
        g2 = jnp.concatenate(
            [jnp.concatenate([hi[b * CHUNK:(b + 1) * CHUNK], mid[b * CHUNK:(b + 1) * CHUNK]], axis=0)
             for b in range(cpb)], axis=1)
        e_ref[...] = jnp.exp(_dot(wexp_ref[...], g2))
        e16_ref[...] = e_ref[...].astype(BF16)

        def contribution(b, carry):
            r0 = pl.multiple_of(b * CHUNK, CHUNK)
            l0 = pl.multiple_of(b * HEAD_DIM, HEAD_DIM)
            kc16 = (kk_s[pl.ds(r0, CHUNK), :] * e_ref[CHUNK:2 * CHUNK, pl.ds(l0, HEAD_DIM)]).astype(BF16)
            u_ref[b] = _dot_tn(v_s[pl.ds(r0, CHUNK), :], kc16)
            return carry

        lax.fori_loop(0, cpb, contribution, 0, unroll=LOCAL_UNROLL)

        def recur(b, st):
            l0 = pl.multiple_of(b * HEAD_DIM, HEAD_DIM)
            s_ref[b] = st
            return st * e_ref[CHUNK - 1:CHUNK, pl.ds(l0, HEAD_DIM)] + u_ref[b]

        st_ref[...] = lax.fori_loop(0, cpb, recur, st_ref[...], unroll=LOCAL_UNROLL)

        zeros16 = jnp.zeros((CHUNK, HEAD_DIM), BF16)

        def local(b, carry):
            r0 = pl.multiple_of(b * CHUNK, CHUNK)
            l0 = pl.multiple_of(b * HEAD_DIM, HEAD_DIM)
            q = q_s[pl.ds(r0, CHUNK), :]
            kk = kk_s[pl.ds(r0, CHUNK), :]
            v16 = v_s[pl.ds(r0, CHUNK), :]

            def scaled(entry):
                if entry == 0:
                    return q.astype(BF16), kk.astype(BF16)
                e_m = e_ref[(1 + entry) * CHUNK:(2 + entry) * CHUNK, pl.ds(l0, HEAD_DIM)]
                return (q * e_m).astype(BF16), (kk * e_m).astype(BF16)

            a2 = jnp.zeros((CHUNK, 2 * CHUNK), F32)
            for p, (ea, eb) in enumerate(LEVEL_PAIRS):
                qa, ka = scaled(ea)
                if eb is None:
                    prod = _dot_nt(qa, jnp.concatenate([ka, zeros16], axis=0))
                else:
                    qb_, kb_ = scaled(eb)
                    rhs = jnp.concatenate([jnp.concatenate([ka, zeros16], axis=1),
                                           jnp.concatenate([zeros16, kb_], axis=1)], axis=0)
                    prod = _dot_nt(jnp.concatenate([qa, qb_], axis=1), rhs)
                a2 = a2 + mask_ref[p] * prod
            a2_16 = a2.astype(BF16)
            a2_ref[pl.ds(r0, CHUNK), :] = a2_16
            qb16 = (q * e_ref[0:CHUNK, pl.ds(l0, HEAD_DIM)]).astype(BF16)
            o_ref[pl.ds(r0, CHUNK), :] = (_dot(a2_16, jnp.concatenate([v16, v16], axis=0))
                                          + _dot_nt(qb16, s_ref[b].astype(BF16)))
            return carry

        lax.fori_loop(0, cpb, local, 0, unroll=LOCAL_UNROLL)

    head_block = lambda seg: pl.BlockSpec((1, rb_rows, width), lambda h, r: (seg, r, h))
    return pl.pallas_call(
        body, name="hgrn_forward",
        grid=(n_hb, n_rb),
        in_specs=[
            head_block(0), head_block(1), head_block(2),
            pl.BlockSpec((2, width), lambda h, r: (0, h)),
            pl.BlockSpec((N_EXP * CHUNK, 2 * CHUNK), lambda h, r: (0, 0)),
            pl.BlockSpec((len(LEVEL_PAIRS), CHUNK, 2 * CHUNK), lambda h, r: (0, 0, 0)),
            ANY,
        ],
        out_specs=[
            pl.BlockSpec((rb_rows, width), lambda h, r: (r, h)),
            pl.BlockSpec((hps, cpb, HEAD_DIM, HEAD_DIM), lambda h, r: (h, r, 0, 0)),
            pl.BlockSpec((hps, 1, N_EXP * CHUNK, lanes), lambda h, r: (h, r, 0, 0)),
            pl.BlockSpec((rb_rows, width), lambda h, r: (r, h)),
            ANY,
        ],
        out_shape=[
            jax.ShapeDtypeStruct((rows, D_MODEL), F32),
            jax.ShapeDtypeStruct((N_HEADS, n_chunks, HEAD_DIM, HEAD_DIM), F32),
            jax.ShapeDtypeStruct((N_HEADS, n_rb, N_EXP * CHUNK, lanes), BF16),
            jax.ShapeDtypeStruct((rows, D_MODEL), BF16),
            jax.ShapeDtypeStruct(blob4.shape, blob4.dtype),
        ],
        input_output_aliases={6: 4},
        scratch_shapes=[
            pltpu.VMEM((hps, HEAD_DIM, HEAD_DIM), F32),
            pltpu.VMEM((hps, N_EXP * CHUNK, lanes), F32),
            pltpu.VMEM((hps, cpb, HEAD_DIM, HEAD_DIM), F32),
            pltpu.VMEM((hps, rb_rows, HEAD_DIM), F32),
            pltpu.VMEM((hps, rb_rows, HEAD_DIM), F32),
            pltpu.VMEM((hps, rb_rows, HEAD_DIM), BF16),
        ] + gather.semaphores(),
        compiler_params=_params(("arbitrary", "arbitrary")),
    )(p3, p3, p3, lb_logits, wexp2, masks2, blob4)


def _hgrn_backward(p3, d_o, states, e16, a2, lb_logits, wexp_t, masks2, dw16, blob16, rows):
    n_chunks = rows // CHUNK
    cpb = _tile(n_chunks, 13, mult=1)
    rb_rows = cpb * CHUNK
    n_rb = n_chunks // cpb
    lanes = cpb * HEAD_DIM
    exchange = _GradExchange(SEGS_MIX, with_blob=True)

    def body(q_ref, fz_ref, v_ref, do_ref, s_ref, e_ref, a2_ref, lbl_ref, wexpt_ref, mask_ref, dw_ref, blob_ref,
             dp_ref, dlb_ref, rxw_ref, rxb_ref,
             dst_ref, g_ref, dsn_ref, q_s, kk_s, v_s, do_s, dq_s, dkk_s, dg_s, dx_s, da2_s, send_sems, recv_sems):
        step = pl.program_id(1)
        rb = n_rb - 1 - step

        @pl.when((pl.program_id(0) == 0) & (step == 0))
        def _():
            exchange.start(dw_ref, rxw_ref, blob_ref, rxb_ref, send_sems, recv_sems)

        @pl.when(step == 0)
        def _():
            dst_ref[...] = jnp.zeros_like(dst_ref)
            dlb_ref[...] = jnp.zeros_like(dlb_ref)

        lb = _lower_bound(lbl_ref[...])
        row = rb * rb_rows + lax.broadcasted_iota(jnp.int32, (rb_rows, 1), 0)
        valid = row >= PAD_ROWS
        sg, sn = _sigmoid_pair(fz_ref[0])
        f = lb + (1.0 - lb) * sg
        g = jnp.where(valid, jnp.log(f), 0.0)
        kk_s[...] = jnp.where(valid, (1.0 - lb) * sn, 0.0)
        q_s[...] = jnp.where(valid, q_ref[0], 0.0)
        v_s[...] = jnp.where(valid, v_ref[0], 0.0).astype(BF16)
        do_s[...] = do_ref[...].astype(BF16)
        e_last_all = jnp.exp(jnp.concatenate(
            [jnp.sum(g[b * CHUNK:(b + 1) * CHUNK], axis=0, keepdims=True) for b in range(cpb)], axis=0))
        last_row = lax.broadcasted_iota(jnp.int32, (CHUNK, 1), 0) == CHUNK - 1
        zeros16 = jnp.zeros((CHUNK, HEAD_DIM), BF16)

        def factor(block, l0):
            return e_ref[0, 0, block * CHUNK:(block + 1) * CHUNK, pl.ds(l0, HEAD_DIM)].astype(F32)

        def contribution(b, carry):
            r0 = pl.multiple_of(b * CHUNK, CHUNK)
            l0 = pl.multiple_of(b * HEAD_DIM, HEAD_DIM)
            qb16 = (q_s[pl.ds(r0, CHUNK), :] * factor(0, l0)).astype(BF16)
            g_ref[b] = _dot_tn(do_s[pl.ds(r0, CHUNK), :], qb16)
            return carry

        lax.fori_loop(0, cpb, contribution, 0, unroll=LOCAL_UNROLL)

        cur = dst_ref[...]
        for b in reversed(range(cpb)):
            dsn_ref[b] = cur
            cur = cur * e_last_all[b:b + 1, :] + g_ref[b]
        dst_ref[...] = cur

        def through_state(b, carry):
            r0 = pl.multiple_of(b * CHUNK, CHUNK)
            l0 = pl.multiple_of(b * HEAD_DIM, HEAD_DIM)
            v16 = v_s[pl.ds(r0, CHUNK), :]
            do16 = do_s[pl.ds(r0, CHUNK), :]
            st = s_ref[0, b]
            dsn = dsn_ref[b]
            dsn16 = dsn.astype(BF16)
            e_b, e_c = factor(0, l0), factor(1, l0)
            qb, kc = q_s[pl.ds(r0, CHUNK), :] * e_b, kk_s[pl.ds(r0, CHUNK), :] * e_c

            t = _dot_tn(a2_ref[pl.ds(r0, CHUNK), :], do16)
            dv = t[0:CHUNK] + t[CHUNK:2 * CHUNK] + _dot_nt(kc.astype(BF16), dsn16)
            dp_ref[2, pl.ds(r0, CHUNK), :] = dv.astype(BF16)
            da2_s[pl.ds(r0, CHUNK), :] = _dot_nt(do16, jnp.concatenate([v16, v16], axis=0))
            dqb = _dot(do16, st.astype(BF16))
            dkc = _dot(v16, dsn16)
            de = jnp.sum(dsn * st, axis=0, keepdims=True) * e_b[CHUNK - 1:CHUNK, :]
            dq_s[pl.ds(r0, CHUNK), :] = e_b * dqb
            dkk_s[pl.ds(r0, CHUNK), :] = e_c * dkc
            dx_s[0:CHUNK, pl.ds(l0, HEAD_DIM)] = (qb * dqb + jnp.where(last_row, de, 0.0)).astype(BF16)
            dx_s[CHUNK:2 * CHUNK, pl.ds(l0, HEAD_DIM)] = (kc * dkc).astype(BF16)
            return carry

        lax.fori_loop(0, cpb, through_state, 0, unroll=BACKWARD_UNROLL)

        def local(b, carry):
            r0 = pl.multiple_of(b * CHUNK, CHUNK)
            l0 = pl.multiple_of(b * HEAD_DIM, HEAD_DIM)
            q = q_s[pl.ds(r0, CHUNK), :]
            kk = kk_s[pl.ds(r0, CHUNK), :]
            da2 = da2_s[pl.ds(r0, CHUNK), :]
            dq = dq_s[pl.ds(r0, CHUNK), :]
            dkk = dkk_s[pl.ds(r0, CHUNK), :]

            def scaled(entry):
                if entry == 0:
                    return q, kk, None
                e_m = factor(1 + entry, l0)
                return q * e_m, kk * e_m, e_m

            for p, (ea, eb) in enumerate(LEVEL_PAIRS):
                dm = mask_ref[p] * da2
                dm_t = dm.T.astype(BF16)
                qa, ka, e_a = scaled(ea)
                if eb is None:
                    rhs_k = jnp.concatenate([jnp.concatenate([ka.astype(BF16), zeros16], axis=1),
                                             jnp.concatenate([zeros16, zeros16], axis=1)], axis=0)
                else:
                    qb_, kb_, e_bb = scaled(eb)
                    rhs_k = jnp.concatenate([jnp.concatenate([ka.astype(BF16), zeros16], axis=1),
                                             jnp.concatenate([zeros16, kb_.astype(BF16)], axis=1)], axis=0)
                dq2 = _dot(dm.astype(BF16), rhs_k)
                parts = [(ea, qa, ka, e_a, dq2[:, :HEAD_DIM], _dot(dm_t[0:CHUNK], qa.astype(BF16)))]
                if eb is not None:
                    parts.append((eb, qb_, kb_, e_bb, dq2[:, HEAD_DIM:],
                                  _dot(dm_t[CHUNK:2 * CHUNK], qb_.astype(BF16))))
                for entry, q_m, k_m, e_m, dq_m, dk_m in parts:
                    if entry == 0:
                        dq = dq + dq_m
                        dkk = dkk + dk_m
                    else:
                        dq = dq + e_m * dq_m
                        dkk = dkk + e_m * dk_m
                        dx_s[(1 + entry) * CHUNK:(2 + entry) * CHUNK, pl.ds(l0, HEAD_DIM)] = (
                            q_m * dq_m + k_m * dk_m).astype(BF16)
            dq_s[pl.ds(r0, CHUNK), :] = dq
            dkk_s[pl.ds(r0, CHUNK), :] = dkk
            return carry

        lax.fori_loop(0, cpb, local, 0, unroll=BACKWARD_UNROLL)

        dg_all = _dot(wexpt_ref[...], dx_s[...])
        for b in range(cpb):
            dg_s[b * CHUNK:(b + 1) * CHUNK, :] = dg_all[:, b * HEAD_DIM:(b + 1) * HEAD_DIM]
        t = jnp.where(valid, dg_s[...] / f - dkk_s[...], 0.0)
        dlb_ref[...] += jnp.sum(sn * t, axis=0, keepdims=True)
        dp_ref[0] = jnp.where(valid, dq_s[...], 0.0).astype(BF16)
        dp_ref[1] = ((1.0 - lb) * sg * sn * t).astype(BF16)

        @pl.when((pl.program_id(0) == N_HEADS - 1) & (step == n_rb - 1))
        def _():
            exchange.wait(dw_ref, rxw_ref, blob_ref, rxb_ref, send_sems, recv_sems)

    head_block = lambda seg: pl.BlockSpec((1, rb_rows, HEAD_DIM), lambda h, s: (seg, n_rb - 1 - s, h))
    row_block = pl.BlockSpec((rb_rows, HEAD_DIM), lambda h, s: (n_rb - 1 - s, h))
    return pl.pallas_call(
        body, name="hgrn_backward",
        grid=(N_HEADS, n_rb),
        in_specs=[
            head_block(0), head_block(1), head_block(2),
            row_block,
            pl.BlockSpec((1, cpb, HEAD_DIM, HEAD_DIM), lambda h, s: (h, n_rb - 1 - s, 0, 0)),
            pl.BlockSpec((1, 1, N_EXP * CHUNK, lanes), lambda h, s: (h, n_rb - 1 - s, 0, 0)),
            row_block,
            pl.BlockSpec((2, HEAD_DIM), lambda h, s: (0, h)),
            pl.BlockSpec((CHUNK, N_EXP * CHUNK), lambda h, s: (0, 0)),
            pl.BlockSpec((len(LEVEL_PAIRS), CHUNK, 2 * CHUNK), lambda h, s: (0, 0, 0)),
            ANY, ANY,
        ],
        out_specs=[
            pl.BlockSpec((3, rb_rows, HEAD_DIM), lambda h, s: (0, n_rb - 1 - s, h)),
            pl.BlockSpec((1, HEAD_DIM), lambda h, s: (0, h)),
            ANY, ANY,
        ],
        out_shape=[
            jax.ShapeDtypeStruct((3, rows, D_MODEL), BF16),
            jax.ShapeDtypeStruct((1, D_MODEL), F32),
            exchange.landing_w(), exchange.landing_blob(blob16),
        ],
        scratch_shapes=[
            pltpu.VMEM((HEAD_DIM, HEAD_DIM), F32),
            pltpu.VMEM((cpb, HEAD_DIM, HEAD_DIM), F32),
            pltpu.VMEM((cpb, HEAD_DIM, HEAD_DIM), F32),
            pltpu.VMEM((rb_rows, HEAD_DIM), F32),
            pltpu.VMEM((rb_rows, HEAD_DIM), F32),
            pltpu.VMEM((rb_rows, HEAD_DIM), BF16),
            pltpu.VMEM((rb_rows, HEAD_DIM), BF16),
            pltpu.VMEM((rb_rows, HEAD_DIM), F32),
            pltpu.VMEM((rb_rows, HEAD_DIM), F32),
            pltpu.VMEM((rb_rows, HEAD_DIM), F32),
            pltpu.VMEM((N_EXP * CHUNK, lanes), BF16),
            pltpu.VMEM((rb_rows, 2 * CHUNK), F32),
        ] + exchange.semaphores(),
        compiler_params=_params(("arbitrary", "arbitrary")),
    )(p3, p3, p3, d_o, states, e16, a2, lb_logits, wexp_t, masks2, dw16, blob16)


def _sigmoid(x):
    return 1.0 / (1.0 + jnp.exp(-x))


def _silu_and_grad(x):
    s = _sigmoid(x)
    return x * s, s * (1.0 + x * (1.0 - s))


def _window_sum(ext, width, forward_looking):
    n = ext.shape[0]
    s = ext
    step = 1
    while step < width:
        s = s + pltpu.roll(s, (n - step) if forward_looking else step, 0)
        step *= 2
    return s


def _mixers(o, p3, tokens, head, tgt, wdh, wdp, wout, poolw, hg_w, pool_scale, final_w, rows):
    tm = _tile(rows, 208)
    nt = rows // tm
    halo_blocks = tm // HALO
    n_grp = len(POOL_WINDOWS)
    q_rows = D_MODEL // N_CHIPS
    blob_rows = 3 * q_rows + n_grp * POOL_GDIM * POOL_GDIM // (N_CHIPS * D_MODEL)

    def body(o_ref, ghg_ref, u_ref, gpl_ref, mhg_ref, mpl_ref, uh_ref, z_ref, t_ref,
             wdh_ref, wdp_ref, wout_ref, pw_ref, hgw_ref, ps_ref, fw_ref, head_ref,
             do_ref, dz2_ref, dp_ref, blob_ref, dpw_ref, small_ref, carry_ref):
        step = pl.program_id(0)
        tile = nt - 1 - step

        def add_to_blob(piece, dw):
            for k in range(N_CHIPS):
                blob_ref[k, piece * q_rows:(piece + 1) * q_rows, :] += dw[k * q_rows:(k + 1) * q_rows]

        @pl.when(step == 0)
        def _():
            blob_ref[...] = jnp.zeros_like(blob_ref)
            dpw_ref[...] = jnp.zeros_like(dpw_ref)
            small_ref[...] = jnp.zeros_like(small_ref)
            carry_ref[...] = jnp.zeros_like(carry_ref)

        row = tile * tm + lax.broadcasted_iota(jnp.int32, (tm, 1), 0)
        real = row >= PAD_ROWS
        pos1 = jnp.maximum(row - PAD_ROWS + 1, 1).astype(F32)

        u = jnp.where(real, u_ref[0], 0.0)
        halo_row = tile * tm - HALO + lax.broadcasted_iota(jnp.int32, (HALO, 1), 0)
        uh = jnp.where(halo_row >= PAD_ROWS, uh_ref[0], 0.0)
        ext = jnp.concatenate([uh, u], axis=0)
        pooled, inv_cnt, mixed = [], [], []
        for g, w in enumerate(POOL_WINDOWS):
            cols = slice(g * POOL_GDIM, (g + 1) * POOL_GDIM)
            inv = 1.0 / jnp.minimum(pos1, float(w))
            ws = _window_sum(ext[:, cols], w, False)[HALO:]
            pg = (ws * inv - u[:, cols]).astype(BF16)
            pooled.append(pg)
            inv_cnt.append(inv)
            mixed.append(_dot(pg, pw_ref[g]))
        mixed = jnp.concatenate(mixed, axis=1)
        gpl = gpl_ref[0]
        sp, dsp = _silu_and_grad(gpl)
        ps = ps_ref[...]
        a_pool = (mixed * ps * sp).astype(BF16)
        y_pool = _dot(a_pool, wdp_ref[...])

        o = o_ref[...]
        o_hat, rstd_h = [], []
        for h in range(N_HEADS):
            oh = o[:, h * HEAD_DIM:(h + 1) * HEAD_DIM]
            r = lax.rsqrt(jnp.mean(oh * oh, axis=-1, keepdims=True) + EPS)
            rstd_h.append(r)
            o_hat.append(oh * r)
        o_hat = jnp.concatenate(o_hat, axis=1)
        hgw = hgw_ref[...]
        o_n = o_hat * hgw
        ghg = ghg_ref[0]
        sh, dsh = _silu_and_grad(ghg)
        a_hg = (o_n * sh).astype(BF16)
        y_hg = _dot(a_hg, wdh_ref[...])

        s_mh = _sigmoid(mhg_ref[0])
        s_mp = _sigmoid(mpl_ref[0])
        merged = (s_mh * y_hg + s_mp * y_pool).astype(BF16)
        z2 = _padded_tile(z_ref[...], head_ref[...], tile) + _dot(merged, wout_ref[...])
        rstd2 = lax.rsqrt(jnp.mean(z2 * z2, axis=-1, keepdims=True) + EPS)
        zh = z2 * rstd2
        fw = fw_ref[...]
        target = _padded_tile(t_ref[...], jnp.zeros((FIRST_TOKEN_ROW, D_MODEL), F32), tile)
        err = jnp.where(row >= FIRST_TOKEN_ROW, zh * fw - target, 0.0)
        small_ref[ROW_LOSS:ROW_LOSS + 1, :] += jnp.sum(err * err, axis=0, keepdims=True) * (0.5 / D_MODEL)
        dy = err * (1.0 / D_MODEL)

        small_ref[ROW_FINAL_W:ROW_FINAL_W + 1, :] += jnp.sum(dy * zh, axis=0, keepdims=True)
        uu = dy * fw
        dz2 = rstd2 * (uu - zh * jnp.mean(uu * zh, axis=-1, keepdims=True))
        dz2_ref[...] = dz2
        dz2_16 = dz2.astype(BF16)
        dmerged = _dot_nt(dz2_16, wout_ref[...])
        add_to_blob(2, _dot_tn(merged, dz2_16))
        dy_hg = (s_mh * dmerged).astype(BF16)
        dy_pool = (s_mp * dmerged).astype(BF16)
        dp_ref[3] = (dmerged * y_hg * s_mh * (1.0 - s_mh)).astype(BF16)
        dp_ref[4] = (dmerged * y_pool * s_mp * (1.0 - s_mp)).astype(BF16)

        da_hg = _dot_nt(dy_hg, wdh_ref[...])
        add_to_blob(0, _dot_tn(a_hg, dy_hg))
        dp_ref[0] = (da_hg * o_n * dsh).astype(BF16)
        do_n = da_hg * sh
        small_ref[ROW_HG_W:ROW_HG_W + 1, :] += jnp.sum(do_n * o_hat, axis=0, keepdims=True)
        d_hat = do_n * hgw
        for h in range(N_HEADS):
            cols = slice(h * HEAD_DIM, (h + 1) * HEAD_DIM)
            dh_, oh_ = d_hat[:, cols], o_hat[:, cols]
            do_ref[:, cols] = rstd_h[h] * (dh_ - oh_ * jnp.mean(dh_ * oh_, axis=-1, keepdims=True))

        da_pool = _dot_nt(dy_pool, wdp_ref[...])
        add_to_blob(1, _dot_tn(a_pool, dy_pool))
        small_ref[ROW_POOL_SCALE:ROW_POOL_SCALE + 1, :] += jnp.sum(da_pool * mixed * sp, axis=0, keepdims=True)
        dp_ref[2] = (da_pool * mixed * ps * dsp).astype(BF16)
        dmixed = (da_pool * ps * sp).astype(BF16)
        carry = carry_ref[...]
        du, new_carry = [], []
        for g, w in enumerate(POOL_WINDOWS):
            cols = slice(g * POOL_GDIM, (g + 1) * POOL_GDIM)
            dmg = dmixed[:, cols]
            dpooled = _dot_nt(dmg, pw_ref[g])
            dpw_ref[g] += _dot_tn(pooled[g], dmg)
            dps = dpooled * inv_cnt[g]
            ext_b = jnp.concatenate([dps, carry[:, cols]], axis=0)
            du.append(_window_sum(ext_b, w, True)[:tm] - dpooled)
            new_carry.append(dps[:HALO])
        dp_ref[1] = jnp.where(real, jnp.concatenate(du, axis=1), 0.0).astype(BF16)
        carry_ref[...] = jnp.concatenate(new_carry, axis=1)

    row_block = pl.BlockSpec((tm, D_MODEL), lambda s: (nt - 1 - s, 0))
    seg_block = lambda seg: pl.BlockSpec((1, tm, D_MODEL), lambda s: (seg, nt - 1 - s, 0))
    whole = pl.BlockSpec(memory_space=pltpu.VMEM)
    return pl.pallas_call(
        body, name="mixers",
        grid=(nt,),
        in_specs=[
            row_block, seg_block(3), seg_block(4), seg_block(5), seg_block(6), seg_block(7),
            pl.BlockSpec((1, HALO, D_MODEL),
                         lambda s: (4, jnp.maximum((nt - 1 - s) * halo_blocks - 1, 0), 0)),
            _token_window(tm, lambda s: nt - 1 - s), _token_window(tm, lambda s: nt - 1 - s),
            whole, whole, whole, whole, whole, whole, whole, whole,
        ],
        out_specs=[
            row_block, row_block,
            pl.BlockSpec((5, tm, D_MODEL), lambda s: (0, nt - 1 - s, 0)),
            whole, whole, whole,
        ],
        out_shape=[
            jax.ShapeDtypeStruct((rows, D_MODEL), F32),
            jax.ShapeDtypeStruct((rows, D_MODEL), F32),
            jax.ShapeDtypeStruct((5, rows, D_MODEL), BF16),
            jax.ShapeDtypeStruct((N_CHIPS, blob_rows, D_MODEL), F32),
            jax.ShapeDtypeStruct((n_grp, POOL_GDIM, POOL_GDIM), F32),
            jax.ShapeDtypeStruct((SMALL_ROWS, D_MODEL), F32),
        ],
        scratch_shapes=[pltpu.VMEM((HALO, D_MODEL), F32)],
        compiler_params=_params(("arbitrary",)),
    )(o, p3, p3, p3, p3, p3, p3, tokens, tgt, wdh, wdp, wout, poolw, hg_w, pool_scale, final_w, head)


def _seg_specs(tm, row_of, seg_of):
    def spec_a(*g):
        k = seg_of(*g)
        return (jnp.minimum(k, 2), jnp.where(k < 3, row_of(*g), 0), 0)

    def spec_b(*g):
        k = seg_of(*g)
        return (jnp.maximum(k - 3, 0), jnp.where(k >= 3, row_of(*g), 0), 0)

    return pl.BlockSpec((1, tm, D_MODEL), spec_a), pl.BlockSpec((1, tm, D_MODEL), spec_b)


def _in_proj_weight_grad(h, dp, rows, name):
    n_seg = dp.shape[0]
    tm = _tile(rows, 1040)
    nt = rows // tm
    half = D_MODEL // 2

    def body(h_ref, dp_ref, part_ref, part16_ref, db_ref, acc_ref, bacc_ref, stage_ref, land_ref,
             send_sems, recv_sems):
        k, i = pl.program_id(0), pl.program_id(1)
        x, y, c = lax.axis_index("x"), lax.axis_index("y"), lax.axis_index("c")

        def to_sibling(seg):
            return pltpu.make_async_remote_copy(
                src_ref=stage_ref.at[seg], dst_ref=land_ref.at[seg], send_sem=send_sems.at[seg],
                recv_sem=recv_sems.at[seg], device_id=(x, y, 1 - c), device_id_type=MESH)

        @pl.when(i == 0)
        def _():
            acc_ref[...] = jnp.zeros_like(acc_ref)
            bacc_ref[...] = jnp.zeros_like(bacc_ref)

        dpt = dp_ref[0]
        acc_ref[...] += _dot_tn(h_ref[...], dpt)
        bacc_ref[...] += jnp.sum(dpt.astype(F32), axis=0, keepdims=True)

        @pl.when(i == nt - 1)
        def _():
            db_ref[0] = bacc_ref[...]
            part_ref[k] = acc_ref[pl.ds(pl.multiple_of(c * half, half), half), :]
            stage_ref[k] = acc_ref[pl.ds(pl.multiple_of((1 - c) * half, half), half), :].astype(BF16)
            to_sibling(k).start()

        @pl.when((k == n_seg - 1) & (i == nt - 1))
        def _():
            for seg in range(n_seg):
                to_sibling(seg).wait_recv()
                total = part_ref[seg] + land_ref[seg].astype(F32)
                part_ref[seg] = total
                part16_ref[seg] = total.astype(BF16)
            for seg in range(n_seg):
                to_sibling(seg).wait_send()

    whole = pl.BlockSpec(memory_space=pltpu.VMEM)
    return pl.pallas_call(
        body, name=name,
        grid=(n_seg, nt),
        in_specs=[pl.BlockSpec((tm, D_MODEL), lambda k, i: (i, 0)),
                  pl.BlockSpec((1, tm, D_MODEL), lambda k, i: (k, i, 0))],
        out_specs=[whole, whole, pl.BlockSpec((1, 1, D_MODEL), lambda k, i: (k, 0, 0))],
        out_shape=[
            jax.ShapeDtypeStruct((n_seg, half, D_MODEL), F32),
            jax.ShapeDtypeStruct((n_seg, half, D_MODEL), BF16),
            jax.ShapeDtypeStruct((n_seg, 1, D_MODEL), F32),
        ],
        scratch_shapes=[
            pltpu.VMEM((D_MODEL, D_MODEL), F32), pltpu.VMEM((1, D_MODEL), F32),
            pltpu.VMEM((n_seg, half, D_MODEL), BF16),
            pltpu.VMEM((n_seg, half, D_MODEL), BF16),
            pltpu.SemaphoreType.DMA((n_seg,)), pltpu.SemaphoreType.DMA((n_seg,)),
        ],
        compiler_params=_params(("arbitrary", "arbitrary")),
    )(h, dp)


def _input_grad(dpa, dpb, w4, tokens, head, dz2, norm_w, dw16, rows):
    tm = _tile(rows, 1040)
    nt = rows // tm
    assert nt >= 2, rows
    exchange = _GradExchange(SEGS_REC, with_blob=False)

    def body(dpa_ref, dpb_ref, w_ref, z_ref, head_ref, dz2_ref, nw_ref, dw_ref, gx_ref, dmeta_ref, dnw_ref, rxw_ref,
             acc_ref, dz_buf, out_sem, send_sems, recv_sems):
        i, k = pl.program_id(0), pl.program_id(1)

        def first_tile_out():
            return pltpu.make_async_copy(dz_buf.at[pl.ds(FIRST_TOKEN_ROW, tm - FIRST_TOKEN_ROW), :],
                                         gx_ref.at[pl.ds(0, tm - FIRST_TOKEN_ROW), :], out_sem)

        def tile_out(tile):
            start = pl.multiple_of(tile * tm - FIRST_TOKEN_ROW, HALO)
            return pltpu.make_async_copy(dz_buf, gx_ref.at[pl.ds(start, tm), :], out_sem)

        @pl.when((i == 0) & (k == 0))
        def _():
            exchange.start(dw_ref, rxw_ref, None, None, send_sems, recv_sems)
            dnw_ref[...] = jnp.zeros_like(dnw_ref)

        @pl.when((i == nt - 1) & (k == N_SEG - 1))
        def _():
            exchange.wait(dw_ref, rxw_ref, None, None, send_sems, recv_sems)

        @pl.when(k == 0)
        def _():
            acc_ref[...] = jnp.zeros_like(acc_ref)

        @pl.when(k < 3)
        def _():
            acc_ref[...] += _dot_nt(dpa_ref[0], w_ref[0])

        @pl.when(k >= 3)
        def _():
            acc_ref[...] += _dot_nt(dpb_ref[0], w_ref[0])

        @pl.when(k == N_SEG - 1)
        def _():
            zt = _padded_tile(z_ref[...], head_ref[...], i)
            rstd = lax.rsqrt(jnp.mean(zt * zt, axis=-1, keepdims=True) + EPS)
            zh = zt * rstd
            dh = acc_ref[...]
            dnw_ref[...] += jnp.sum(dh * zh, axis=0, keepdims=True)
            uu = dh * nw_ref[...]
            dz = dz2_ref[...] + rstd * (uu - zh * jnp.mean(uu * zh, axis=-1, keepdims=True))

            @pl.when(i == 1)
            def _():
                first_tile_out().wait()

            @pl.when(i >= 2)
            def _():
                tile_out(i - 1).wait()

            dz_buf[...] = dz

            @pl.when(i == 0)
            def _():
                dmeta_ref[...] = dz[PAD_ROWS:FIRST_TOKEN_ROW]
                first_tile_out().start()

            @pl.when(i > 0)
            def _():
                tile_out(i).start()

            @pl.when(i == nt - 1)
            def _():
                tile_out(i).wait()

    spec_a, spec_b = _seg_specs(tm, lambda i, k: i, lambda i, k: k)
    last_only = pl.BlockSpec((tm, D_MODEL), lambda i, k: (jnp.where(k == N_SEG - 1, i, 0), 0))
    return pl.pallas_call(
        body, name="input_grad",
        grid=(nt, N_SEG),
        in_specs=[
            spec_a, spec_b,
            pl.BlockSpec((1, D_MODEL, D_MODEL), lambda i, k: (k // 2, 0, k % 2)),
            _token_window(tm, lambda i, k: jnp.where(k == N_SEG - 1, i, 0)),
            pl.BlockSpec((FIRST_TOKEN_ROW, D_MODEL), lambda i, k: (0, 0)),
            last_only,
            pl.BlockSpec((1, D_MODEL), lambda i, k: (0, 0)),
            ANY,
        ],
        out_specs=[
            ANY,
            pl.BlockSpec((N_META, D_MODEL), lambda i, k: (0, 0)),
            pl.BlockSpec((1, D_MODEL), lambda i, k: (0, 0)),
            ANY,
        ],
        out_shape=[
            jax.ShapeDtypeStruct((rows - FIRST_TOKEN_ROW, D_MODEL), F32),
            jax.ShapeDtypeStruct((N_META, D_MODEL), F32),
            jax.ShapeDtypeStruct((1, D_MODEL), F32),
            exchange.landing_w(),
        ],
        scratch_shapes=[pltpu.VMEM((tm, D_MODEL), F32), pltpu.VMEM((tm, D_MODEL), F32),
                        pltpu.SemaphoreType.DMA] + exchange.semaphores(),
        compiler_params=_params(("arbitrary", "arbitrary")),
    )(dpa, dpb, w4, tokens, head, dz2, norm_w, dw16)


def _local_step(tokens, m4, tgt, w4, blob4, seg_order, norm_w, b_in, lb_logits, hg_w, pool_scale, final_w):
    rows = FIRST_TOKEN_ROW + tokens.shape[0]
    q = D_MODEL // N_CHIPS
    n_grp = len(POOL_WINDOWS)
    pg = POOL_GDIM // N_CHIPS

    wexp2 = jnp.asarray(np.tile(_exponent_matrix(), (1, 2)), BF16)
    wexp_t = jnp.asarray(_exponent_matrix().T, BF16)
    masks2 = jnp.asarray(_paired_masks(), F32)

    h, p3, w4, head, _ = _in_proj(tokens, m4, norm_w, w4, b_in, seg_order, rows)
    o, states, e16, a2, blob4 = _hgrn_forward(p3, lb_logits, wexp2, masks2, blob4, rows)
    wdh = blob4[:, 0:q].reshape(D_MODEL, D_MODEL)
    wdp = blob4[:, q:2 * q].reshape(D_MODEL, D_MODEL)
    wout = blob4[:, 2 * q:3 * q].reshape(D_MODEL, D_MODEL)
    poolw = blob4[:, 3 * q:].reshape(N_CHIPS, n_grp, pg, POOL_GDIM).transpose(1, 0, 2, 3)
    poolw = poolw.reshape(n_grp, POOL_GDIM, POOL_GDIM)
    d_o, dz2, dpb, dblob4, dpw, small = _mixers(
        o, p3, tokens, head, tgt, wdh, wdp, wout, poolw, hg_w, pool_scale, final_w, rows)
    dpw4 = dpw.reshape(n_grp, N_CHIPS, pg, POOL_GDIM).transpose(1, 0, 2, 3)
    dpw4 = dpw4.reshape(N_CHIPS, n_grp * pg * POOL_GDIM // D_MODEL, D_MODEL)
    dblob4 = dblob4.at[:, 3 * q:, :].set(dpw4)

    dw_mix, dw_mix16, db_mix = _in_proj_weight_grad(h, dpb, rows, "in_proj_weight_grad_mix")
    dpa, dlb, rxw_mix, rx_blob = _hgrn_backward(
        p3, d_o, states, e16, a2, lb_logits, wexp_t, masks2, dw_mix16, dblob4.astype(BF16), rows)
    dw_rec, dw_rec16, db_rec = _in_proj_weight_grad(h, dpa, rows, "in_proj_weight_grad_rec")
    d_tokens, d_meta, dnw, rxw_rec = _input_grad(dpa, dpb, w4, tokens, head, dz2, norm_w, dw_rec16, rows)

    small = jnp.concatenate([
        small[ROW_LOSS:ROW_LOSS + 1],
        d_meta,
        dnw,
        db_rec.reshape(len(SEGS_REC), D_MODEL), db_mix.reshape(len(SEGS_MIX), D_MODEL),
        dlb, jnp.zeros_like(dlb),
        small[ROW_HG_W:ROW_HG_W + 1], small[ROW_POOL_SCALE:ROW_POOL_SCALE + 1],
        small[ROW_FINAL_W:ROW_FINAL_W + 1],
        jnp.zeros((SMALL_ROWS - ROW_FINAL_W - 1, D_MODEL), F32),
    ], axis=0)
    return d_tokens, (dw_rec, dw_mix, rxw_rec, rxw_mix), (dblob4, rx_blob), small


ANY = pl.BlockSpec(memory_space=pl.ANY)
MESH = pl.DeviceIdType.MESH


def _place():
    x, y, c = lax.axis_index("x"), lax.axis_index("y"), lax.axis_index("c")
    chips = [(1 - x, y), (x, 1 - y), (1 - x, 1 - y)]
    return x, y, c, chips


class _ShardGather:
    def __init__(self, rows):
        self.half = rows // 2

    def semaphores(self):
        return [pltpu.SemaphoreType.DMA((6,)), pltpu.SemaphoreType.DMA((6,))]

    def _copy(self, k, slot, to, send_sems, recv_sems):
        return pltpu.make_async_remote_copy(src_ref=slot, dst_ref=slot, send_sem=send_sems.at[k],
                                            recv_sem=recv_sems.at[k], device_id=to, device_id_type=MESH)

    def _half(self, ref4, chip, which):
        return ref4.at[chip, pl.ds(which * self.half, self.half), :]

    def start(self, ref4, send_sems, recv_sems, which=(0, 1, 2)):
        x, y, c, chips = _place()
        for j in which:
            cx, cy = chips[j]
            self._copy(j, self._half(ref4, 2 * x + y, c), (cx, cy, c), send_sems, recv_sems).start()

    def start_diagonal_after_neighbours(self, ref4, send_sems, recv_sems):
        x, y, c, chips = _place()
        for j in (0, 1):
            cx, cy = chips[j]
            self._copy(j, self._half(ref4, 2 * x + y, c), (cx, cy, c), send_sems, recv_sems).wait_send()
        self.start(ref4, send_sems, recv_sems, which=(2,))

    def pass_on(self, j, ref4, send_sems, recv_sems):
        x, y, c, chips = _place()
        cx, cy = chips[j]
        landed = self._half(ref4, 2 * cx + cy, c)
        self._copy(j, landed, (cx, cy, c), send_sems, recv_sems).wait_recv()
        self._copy(3 + j, landed, (x, y, 1 - c), send_sems, recv_sems).start()

    def await_sibling(self, j, ref4, send_sems, recv_sems):
        x, y, c, chips = _place()
        cx, cy = chips[j]
        self._copy(3 + j, self._half(ref4, 2 * cx + cy, 1 - c), (x, y, 1 - c), send_sems, recv_sems).wait_recv()

    def finish(self, ref4, send_sems, recv_sems, which=(0, 1, 2)):
        x, y, c, chips = _place()
        for j, (cx, cy) in enumerate(chips):
            if j in which:
                self._copy(j, self._half(ref4, 2 * x + y, c), (cx, cy, c), send_sems, recv_sems).wait_send()
            self._copy(3 + j, self._half(ref4, 2 * cx + cy, c), (x, y, 1 - c), send_sems, recv_sems).wait_send()


class _GradExchange:
    def __init__(self, segs, with_blob):
        self.segs = tuple(segs)
        self.with_blob = with_blob

    def landing_w(self):
        return jax.ShapeDtypeStruct((N_CHIPS, 2, D_MODEL // 2, D_MODEL), BF16)

    def landing_blob(self, blob16):
        return jax.ShapeDtypeStruct((N_DEV, blob16.shape[1] // 2, D_MODEL), BF16)

    def semaphores(self):
        n_send = len(self.segs) + (2 * N_CHIPS if self.with_blob else 0)
        n_recv = 2 * N_CHIPS + (N_DEV if self.with_blob else 0)
        return [pltpu.SemaphoreType.DMA((n_send,)), pltpu.SemaphoreType.DMA((n_recv,))]

    def _copies(self, dw_ref, rxw_ref, blob_ref, rxb_ref, send_sems, recv_sems):
        x, y, c = lax.axis_index("x"), lax.axis_index("y"), lax.axis_index("c")
        chip = 2 * x + y

        def relation(kx, ky, h):
            return (x ^ kx) * 4 + (y ^ ky) * 2 + (c ^ h)

        def copy(src, dst, send_k, recv_k, to):
            return functools.partial(pltpu.make_async_remote_copy, src_ref=src, dst_ref=dst,
                                     send_sem=send_sems.at[send_k], recv_sem=recv_sems.at[recv_k],
                                     device_id=to, device_id_type=MESH)

        sends, recvs = [], []
        for i, s in enumerate(self.segs):
            kx, ky = (s // 2) >> 1, (s // 2) & 1
            r = (x ^ kx) * 2 + (y ^ ky)
            sends.append((r != 0, copy(dw_ref.at[i], rxw_ref.at[r, s % 2], i, 2 * r + s % 2, (kx, ky, c))))
        for j in range(2):
            mine = [s // 2 for s in self.segs if s % 2 == j]
            if mine:
                cond = functools.reduce(lambda a, b: a | b, [chip == k for k in mine])
                for r in range(1, N_CHIPS):
                    slot = rxw_ref.at[r, j]
                    recvs.append((cond, copy(slot, slot, 0, 2 * r + j, (x, y, c))))
        if self.with_blob:
            hb = blob_ref.shape[1] // 2
            first_send, first_recv = len(self.segs), 2 * N_CHIPS
            for k in range(N_CHIPS):
                for h in range(2):
                    r = relation(k >> 1, k & 1, h)
                    sends.append((r != 0, copy(blob_ref.at[k, pl.ds(h * hb, hb), :], rxb_ref.at[r],
                                               first_send + 2 * k + h, first_recv + r, (k >> 1, k & 1, h))))
            for r in range(1, N_DEV):
                slot = rxb_ref.at[r]
                recvs.append((None, copy(slot, slot, 0, first_recv + r, (x, y, c))))
        return sends, recvs

    def start(self, *refs):
        sends, _ = self._copies(*refs)
        for cond, make in sends:
            pl.when(cond)(lambda make=make: make().start())

    def wait(self, *refs):
        sends, recvs = self._copies(*refs)
        for cond, make in sends:
            pl.when(cond)(lambda make=make: make().wait_send())
        for cond, make in recvs:
            if cond is None:
                make().wait_recv()
            else:
                pl.when(cond)(lambda make=make: make().wait_recv())


def _sum_landed(own, rx_ref):
    total = own
    for r in range(1, rx_ref.shape[0]):
        total = total + rx_ref[r, 0].astype(F32)
    return total


def _finish_w(dw_rec, dw_mix, rx_rec, rx_mix, place_arr):
    half = D_MODEL // 2
    tm = _tile(half, 256)
    n_rec = len(SEGS_REC)

    def body(place_ref, own_rec_ref, own_mix_ref, rx_rec_ref, rx_mix_ref, out_ref):
        seg = 2 * place_ref[0] + pl.program_id(0)

        @pl.when(seg < n_rec)
        def _():
            out_ref[0] = _sum_landed(own_rec_ref[0], rx_rec_ref)

        @pl.when(seg >= n_rec)
        def _():
            out_ref[0] = _sum_landed(own_mix_ref[0], rx_mix_ref)

    def own_spec(first, count):
        def index(j, i, place_ref):
            seg = 2 * place_ref[0] + j
            return (jnp.clip(seg - first, 0, count - 1), i, 0)
        return pl.BlockSpec((1, tm, D_MODEL), index)

    rx_spec = pl.BlockSpec((N_CHIPS, 1, tm, D_MODEL), lambda j, i, place_ref: (0, j, i, 0))
    return pl.pallas_call(
        body, name="finish_w",
        grid_spec=pltpu.PrefetchScalarGridSpec(
            num_scalar_prefetch=1, grid=(2, half // tm),
            in_specs=[own_spec(0, n_rec), own_spec(n_rec, len(SEGS_MIX)), rx_spec, rx_spec],
            out_specs=pl.BlockSpec((1, tm, D_MODEL), lambda j, i, place_ref: (place_ref[1], i, j))),
        out_shape=jax.ShapeDtypeStruct((2, half, 2 * D_MODEL), F32),
        compiler_params=_params(("arbitrary", "arbitrary")),
    )(place_arr, dw_rec, dw_mix, rx_rec, rx_mix)


def _finish_blob(dblob4, rx_blob, place_arr):
    n, rows, cols = rx_blob.shape
    tm = _tile(rows, 256)

    def body(place_ref, own_ref, rx_ref, out_ref):
        out_ref[0] = _sum_landed(own_ref[0, 0], rx_ref)

    return pl.pallas_call(
        body, name="finish_blob",
        grid_spec=pltpu.PrefetchScalarGridSpec(
            num_scalar_prefetch=1, grid=(rows // tm,),
            in_specs=[pl.BlockSpec((1, 1, tm, cols), lambda i, place_ref: (place_ref[0], place_ref[1], i, 0)),
                      pl.BlockSpec((n, 1, tm, cols), lambda i, place_ref: (0, 0, i, 0))],
            out_specs=pl.BlockSpec((1, tm, cols), lambda i, place_ref: (place_ref[1], i, 0))),
        out_shape=jax.ShapeDtypeStruct((2, rows, cols), F32),
        compiler_params=_params(("arbitrary",)),
    )(place_arr, dblob4.reshape(N_CHIPS, 2, rows, cols), rx_blob.reshape(n, 1, rows, cols))


def _share_finished(fw2, fb2, small):
    def body(w_in_ref, b_in_ref, small_ref, w_ref, b_ref, s_ref, bounce, local_sem, send_sems, recv_sems):
        x, y, c, _ = _place()
        sibling = (x, y, 1 - c)

        def copy(k, src, dst, to):
            return pltpu.make_async_remote_copy(src_ref=src, dst_ref=dst, send_sem=send_sems.at[k],
                                                recv_sem=recv_sems.at[k], device_id=to, device_id_type=MESH)

        sends = [copy(0, w_ref.at[c], w_ref.at[c], sibling), copy(1, b_ref.at[c], b_ref.at[c], sibling)]
        for r in range(1, N_DEV):
            peer = (x ^ ((r >> 2) & 1), y ^ ((r >> 1) & 1), c ^ (r & 1))
            sends.append(copy(1 + r, small_ref, s_ref.at[r], peer))
        for cp in sends:
            cp.start()
        for src, dst in ((small_ref, bounce), (bounce, s_ref.at[0])):
            own = pltpu.make_async_copy(src, dst, local_sem)
            own.start()
            own.wait()
        landed = [w_ref.at[1 - c], b_ref.at[1 - c]] + [s_ref.at[r] for r in range(1, N_DEV)]
        for k, slot in enumerate(landed):
            copy(k, slot, slot, (x, y, c)).wait_recv()
        for cp in sends:
            cp.wait_send()

    same = lambda a: jax.ShapeDtypeStruct(a.shape, a.dtype)
    n_sem = 2 + N_DEV - 1
    return pl.pallas_call(
        body, name="share_finished",
        in_specs=[ANY, ANY, ANY], out_specs=[ANY, ANY, ANY],
        out_shape=[same(fw2), same(fb2), jax.ShapeDtypeStruct((N_DEV,) + small.shape, F32)],
        input_output_aliases={0: 0, 1: 1},
        scratch_shapes=[pltpu.VMEM(small.shape, F32), pltpu.SemaphoreType.DMA,
                        pltpu.SemaphoreType.DMA((n_sem,)), pltpu.SemaphoreType.DMA((n_sem,))],
    )(fw2, fb2, small)


def _sum_small(slots, lb_logits, me_arr):
    def body(me_ref, slots_ref, lbl_ref, out_ref):
        me = me_ref[0]
        total = slots_ref[me]
        for d in range(1, N_DEV):
            total = total + slots_ref[d ^ me]
        out_ref[...] = total
        out_ref[ROW_LOSS:ROW_LOSS + 1, :] = jnp.broadcast_to(
            jnp.sum(total[ROW_LOSS:ROW_LOSS + 1, :], axis=-1, keepdims=True), (1, D_MODEL))
        lb = _lower_bound(lbl_ref[...])
        g0 = total[ROW_LB:ROW_LB + 1, :] * lb * (1.0 - lb)
        out_ref[ROW_LB:ROW_LB + 1, :] = g0
        out_ref[ROW_LB + 1:ROW_LB + 2, :] = -g0

    return pl.pallas_call(
        body, name="sum_small",
        grid_spec=pltpu.PrefetchScalarGridSpec(
            num_scalar_prefetch=1, grid=(1,),
            in_specs=[pl.BlockSpec((N_DEV, SMALL_ROWS, D_MODEL), lambda i, me_ref: (0, 0, 0)),
                      pl.BlockSpec((2, D_MODEL), lambda i, me_ref: (0, 0))],
            out_specs=pl.BlockSpec((SMALL_ROWS, D_MODEL), lambda i, me_ref: (0, 0))),
        out_shape=jax.ShapeDtypeStruct((SMALL_ROWS, D_MODEL), F32),
        compiler_params=_params(("arbitrary",)),
    )(me_arr, slots, lb_logits)


def _adamw_step(w, g, m, v):
    c1 = 1.0 / (1.0 - ADAM_B1 ** ADAM_STEP)
    c2 = 1.0 / (1.0 - ADAM_B2 ** ADAM_STEP)
    nm = ADAM_B1 * m + (1.0 - ADAM_B1) * g
    nv = ADAM_B2 * v + (1.0 - ADAM_B2) * (g * g)
    return -ADAM_LR * ((nm * c1) / (jnp.sqrt(nv * c2) + ADAM_EPS) + ADAM_WD * w), nm, nv


SMALL_PARAMS = (("norm_w", ROW_NORM_W, 1), ("b_in", ROW_B_IN, N_SEG), ("lb_logits", ROW_LB, 2),
                ("hg_norm_w", ROW_HG_W, 1), ("pool_scale", ROW_POOL_SCALE, 1), ("final_norm_w", ROW_FINAL_W, 1))


def _update_small(tot, triples):
    n = len(SMALL_PARAMS)

    def body(tot_ref, *refs):
        ins, outs = refs[:3 * n], refs[3 * n:]
        for p, (name, row, n_rows) in enumerate(SMALL_PARAMS):
            w_ref, m_ref, v_ref = ins[3 * p:3 * p + 3]
            g_ref, d_ref, nm_ref, nv_ref = outs[4 * p:4 * p + 4]
            if w_ref.shape[0] == n_rows:
                pieces = [(slice(None), slice(None), tot_ref[row:row + n_rows, :])]
            else:
                pieces = [(slice(None), slice(k * D_MODEL, (k + 1) * D_MODEL), tot_ref[row + k:row + k + 1, :])
                          for k in range(n_rows)]
            for rows_, cols_, g in pieces:
                d, nm, nv = _adamw_step(w_ref[rows_, cols_], g, m_ref[rows_, cols_], v_ref[rows_, cols_])
                g_ref[rows_, cols_] = g
                d_ref[rows_, cols_] = d
                nm_ref[rows_, cols_] = nm
                nv_ref[rows_, cols_] = nv

    whole = pl.BlockSpec(memory_space=pltpu.VMEM)
    flat = [a for t in triples for a in t]
    out_shape = [jax.ShapeDtypeStruct(t[0].shape, F32) for t in triples for _ in range(4)]
    outs = pl.pallas_call(
        body, name="update_small",
        in_specs=[whole] * (1 + len(flat)), out_specs=[whole] * len(out_shape), out_shape=out_shape,
        compiler_params=_params(),
    )(tot, *flat)
    return [tuple(outs[4 * p:4 * p + 4]) for p in range(n)]


def _update_blob(g_blob, triples):
    q_rows = triples[0][0].shape[0]
    pool_rows = triples[3][0].shape[0]
    steps = q_rows // pool_rows

    def body(*refs):
        ins, outs = refs[:16], refs[16:]
        for p in range(4):
            g_ref, (w_ref, m_ref, v_ref) = ins[p], ins[4 + 3 * p:7 + 3 * p]
            go_ref, d_ref, nm_ref, nv_ref = outs[4 * p:4 * p + 4]

            def update():
                g = g_ref[...]
                go_ref[...] = g
                d_ref[...], nm_ref[...], nv_ref[...] = _adamw_step(w_ref[...], g, m_ref[...], v_ref[...])

            if p < 3:
                update()
            else:
                pl.when(pl.program_id(0) == 0)(update)

    blk = pl.BlockSpec((pool_rows, D_MODEL), lambda i: (i, 0))
    once = pl.BlockSpec((pool_rows, D_MODEL), lambda i: (0, 0))
    g_specs = [pl.BlockSpec((pool_rows, D_MODEL), lambda i, p=p: (steps * p + i, 0)) for p in range(3)]
    g_specs.append(pl.BlockSpec((pool_rows, D_MODEL), lambda i: (3 * steps, 0)))
    piece_specs = [blk] * 9 + [once] * 3
    out_specs = [blk] * 12 + [once] * 4
    out_shape = [jax.ShapeDtypeStruct(t[0].shape, F32) for t in triples for _ in range(4)]
    outs = pl.pallas_call(
        body, name="update_blob",
        grid=(steps,), in_specs=g_specs + piece_specs, out_specs=out_specs, out_shape=out_shape,
        compiler_params=_params(("arbitrary",)),
    )(g_blob, g_blob, g_blob, g_blob, *[a for t in triples for a in t])
    return [tuple(outs[4 * p:4 * p + 4]) for p in range(4)]


def _adamw(w, g, m, v):
    rows, cols = w.shape
    tm = _tile(rows, 256, mult=8) if rows % 8 == 0 else rows

    def body(w_ref, g_ref, m_ref, v_ref, d_ref, nm_ref, nv_ref):
        d_ref[...], nm_ref[...], nv_ref[...] = _adamw_step(w_ref[...], g_ref[...], m_ref[...], v_ref[...])

    blk = pl.BlockSpec((tm, cols), lambda i: (i, 0))
    sds = jax.ShapeDtypeStruct((rows, cols), F32)
    return pl.pallas_call(
        body, name="adamw",
        grid=(rows // tm,), in_specs=[blk] * 4, out_specs=[blk] * 3, out_shape=[sds] * 3,
        compiler_params=_params(("arbitrary",)),
    )(w, g, m, v)


def kernel(x, meta_tokens, norm_w, w_in, b_in, lb_logits, hg_norm_w, pool_w, pool_scale, w_down_hg, w_down_pool, w_out, final_norm_w, loss_target, m_meta_tokens, m_norm_w, m_w_in, m_b_in, m_lb_logits, m_hg_norm_w, m_pool_w, m_pool_scale, m_w_down_hg, m_w_down_pool, m_w_out, m_final_norm_w, v_meta_tokens, v_norm_w, v_w_in, v_b_in, v_lb_logits, v_hg_norm_w, v_pool_w, v_pool_scale, v_w_down_hg, v_w_down_pool, v_w_out, v_final_norm_w):
    seq = x.shape[1]
    xi, yi, ci = lax.axis_index("x"), lax.axis_index("y"), lax.axis_index("c")
    chip = 2 * xi + yi
    place_arr = jnp.stack([chip, ci]).astype(jnp.int32)
    me_arr = jnp.reshape(4 * xi + 2 * yi + ci, (1,)).astype(jnp.int32)
    q = D_MODEL // N_CHIPS

    def blob_of(wdh, wdp, wo, pw):
        return jnp.concatenate([wdh[0], wdp[0], wo[0], pw[0].reshape(-1, D_MODEL)], axis=0)

    def in_every_slot(a):
        return jnp.broadcast_to(a[None], (N_CHIPS,) + a.shape)

    m4 = in_every_slot(meta_tokens)
    w4 = in_every_slot(w_in[0].astype(BF16))
    blob4 = in_every_slot(blob_of(w_down_hg, w_down_pool, w_out, pool_w).astype(BF16))
    seg_order = jnp.stack([2 * (chip ^ rel) + t for rel in (0, 2, 1, 3) for t in (0, 1)]).astype(jnp.int32)

    fw2 = final_norm_w.reshape(1, D_MODEL)
    d_tokens, w_parts, blob_parts, small = _local_step(
        x[0], m4, loss_target[0], w4, blob4, seg_order, norm_w, b_in, lb_logits, hg_norm_w, pool_scale, fw2)
    grad_x = d_tokens[None]

    fin_w = _finish_w(*w_parts, place_arr)
    fin_b = _finish_blob(*blob_parts, place_arr)
    gw2, gb2, slots = _share_finished(fin_w, fin_b, small)
    tot = _sum_small(slots, lb_logits, me_arr)
    g_w_in = gw2.reshape(D_MODEL, 2 * D_MODEL)
    g_blob = gb2.reshape(-1, D_MODEL)

    d_win, nm_win, nv_win = _adamw(w_in[0], g_w_in, m_w_in[0], v_w_in[0])
    pool_rows = lambda a: a[0].reshape(-1, D_MODEL)
    blob_results = _update_blob(g_blob, [
        (w_down_hg[0], m_w_down_hg[0], v_w_down_hg[0]), (w_down_pool[0], m_w_down_pool[0], v_w_down_pool[0]),
        (w_out[0], m_w_out[0], v_w_out[0]), (pool_rows(pool_w), pool_rows(m_pool_w), pool_rows(v_pool_w))])
    g_meta = lax.dynamic_slice_in_dim(tot[ROW_META:ROW_META + N_META], chip * q, q, axis=1)
    d_meta, nm_meta, nv_meta = _adamw(meta_tokens, g_meta, m_meta_tokens, v_meta_tokens)

    as_row = lambda a: a.reshape(1, D_MODEL)
    small_results = _update_small(tot, [
        (norm_w, m_norm_w, v_norm_w), (b_in, m_b_in, v_b_in), (lb_logits, m_lb_logits, v_lb_logits),
        (hg_norm_w, m_hg_norm_w, v_hg_norm_w), (pool_scale, m_pool_scale, v_pool_scale),
        (as_row(final_norm_w), as_row(m_final_norm_w), as_row(v_final_norm_w))])

    def leaves(kind, meta_part, win_part):
        nw, bi, lbl, hg, ps, fw = [r[kind] for r in small_results]
        wdh, wdp, wo, pw = [r[kind] for r in blob_results]
        return [meta_part, nw, win_part[None], bi, lbl, hg, pw.reshape(pool_w.shape), ps,
                wdh[None], wdp[None], wo[None], fw.reshape(D_MODEL)]

    loss = tot[ROW_LOSS, 0]
    return (loss, grad_x,
            *leaves(0, g_meta, g_w_in),
            *leaves(1, d_meta, d_win),
            *leaves(2, nm_meta, nm_win),
            *leaves(3, nv_meta, nv_win))
```

```python
import functools

import numpy as np
import jax
import jax.numpy as jnp
from jax import lax
from jax.experimental import pallas as pl
from jax.experimental.pallas import tpu as pltpu

F32 = jnp.float32
BF16 = jnp.bfloat16

D_MODEL = 1024
N_SEG = 8
N_HEADS = 8
HEAD_DIM = 128
CHUNK = 64
N_META = 16
PAD_ROWS = CHUNK - N_META
FIRST_TOKEN_ROW = CHUNK
LEVELS = (32, 16, 8, 4, 2, 1)
N_EXP = 2 + len(LEVELS)
POOL_WINDOWS = (2, 4, 8, 16)
POOL_GDIM = D_MODEL // len(POOL_WINDOWS)
HALO = 16
FORWARD_HEADS_PER_STEP = 4
BACKWARD_HEADS_PER_STEP = 2
LOCAL_UNROLL = 13
BACKWARD_UNROLL = 13
EPS = 1e-6
N_CHIPS = 4
N_DEV = 8
SEGS_REC = (0, 1, 2)
SEGS_MIX = (3, 4, 5, 6, 7)

ADAM_LR = 0.001
ADAM_B1 = 0.9
ADAM_B2 = 0.999
ADAM_EPS = 1e-08
ADAM_WD = 0.01
ADAM_STEP = 10

VMEM_LIMIT_BYTES = 56 * 1024 * 1024

ROW_LOSS = 0
ROW_META = 1
ROW_NORM_W = ROW_META + N_META
ROW_B_IN = ROW_NORM_W + 1
ROW_LB = ROW_B_IN + N_SEG
ROW_HG_W = ROW_LB + 2
ROW_POOL_SCALE = ROW_HG_W + 1
ROW_FINAL_W = ROW_POOL_SCALE + 1
SMALL_ROWS = 32


def _tile(total, cap, mult=16):
    best = None
    for t in range(mult, min(total, cap) + 1, mult):
        if total % t == 0:
            best = t
    assert best is not None, (total, cap, mult)
    return best


def _token_window(tm, tile_of):
    def index(*grid):
        return (pl.multiple_of(jnp.maximum(tile_of(*grid) * tm - FIRST_TOKEN_ROW, 0), HALO), 0)
    return pl.BlockSpec((pl.Element(tm), pl.Element(D_MODEL)), index)


def _padded_tile(window, head, tile):
    first = jnp.concatenate([head, pltpu.roll(window, FIRST_TOKEN_ROW, 0)[FIRST_TOKEN_ROW:]], axis=0)
    return jnp.where(tile == 0, first, window)


def _params(sem=None):
    return pltpu.CompilerParams(dimension_semantics=sem, vmem_limit_bytes=VMEM_LIMIT_BYTES)


def _dot(a, b):
    return jnp.dot(a, b, preferred_element_type=F32)


def _dot_nt(a, b):
    return lax.dot_general(a, b, (((1,), (1,)), ((), ())), preferred_element_type=F32)


def _dot_tn(a, b):
    return lax.dot_general(a, b, (((0,), (0,)), ((), ())), preferred_element_type=F32)


def _sigmoid_pair(x):
    t = jnp.exp(-jnp.abs(x))
    r = 1.0 / (1.0 + t)
    pos = x >= 0
    return jnp.where(pos, r, t * r), jnp.where(pos, t * r, r)


def _exponent_matrix():
    t = np.arange(CHUNK)[:, None]
    j = np.arange(CHUNK)[None, :]
    blocks = [j <= t, j > t]
    for m in LEVELS:
        rho = (t // (2 * m)) * (2 * m) + m
        upper = (t >= rho) & (j > rho) & (j <= t)
        lower = (t < rho) & (j > t) & (j <= rho)
        blocks.append(upper | lower)
    return np.concatenate(blocks, axis=0).astype(np.float32)


def _pair_masks():
    t = np.arange(CHUNK)[:, None]
    s = np.arange(CHUNK)[None, :]
    masks = [t == s]
    for m in LEVELS:
        same = (t // (2 * m)) == (s // (2 * m))
        masks.append(same & ((t % (2 * m)) >= m) & ((s % (2 * m)) < m))
    return np.stack(masks).astype(np.float32)


LEVEL_PAIRS = ((0, 1), (2, 3), (4, 5), (6, None))


def _paired_masks():
    m = _pair_masks()
    zero = np.zeros_like(m[0])
    return np.stack([np.concatenate([m[a], zero if b is None else m[b]], axis=1) for a, b in LEVEL_PAIRS])


def _lower_bound(lbl):
    return 1.0 / (1.0 + jnp.exp(lbl[1:2, :] - lbl[0:1, :]))


def _in_proj(tokens, m4, norm_w, w4, b_in, seg_order, rows):
    tm = _tile(rows, 1040)
    nt = rows // tm
    gather = _ShardGather(w4.shape[1])

    def body(order_ref, z_ref, nw_ref, b_ref, w_in_ref, m_in_ref, h_ref, p_ref, w4_ref, head_ref, m4_ref,
             h_all, w_buf, w_sem, send_sems, recv_sems, meta_send, meta_recv, meta_sem):
        kk, i = pl.program_id(0), pl.program_id(1)

        @pl.when((kk == 0) & (i == 0))
        def _():
            x, y, c, chips = _place()

            def meta_copy(j, chip, to):
                return pltpu.make_async_remote_copy(
                    src_ref=m4_ref.at[chip], dst_ref=m4_ref.at[chip], send_sem=meta_send.at[j],
                    recv_sem=meta_recv.at[j], device_id=to, device_id_type=MESH)

            sends = [meta_copy(j, 2 * x + y, (cx, cy, c)) for j, (cx, cy) in enumerate(chips)]
            for cp in sends:
                cp.start()
            gather.start(w4_ref, send_sems, recv_sems, which=(0, 1))
            for j, (cx, cy) in enumerate(chips):
                meta_copy(j, 2 * cx + cy, (x, y, c)).wait_recv()
            for cp in sends:
                cp.wait_send()
            head_ref[0:PAD_ROWS, :] = jnp.zeros((PAD_ROWS, D_MODEL), F32)
            q_cols = D_MODEL // N_CHIPS
            for k in range(N_CHIPS):
                cp = pltpu.make_async_copy(
                    m4_ref.at[k], head_ref.at[pl.ds(PAD_ROWS, N_META), pl.ds(k * q_cols, q_cols)], meta_sem)
                cp.start()
                cp.wait()

        @pl.when((kk == 2) & (i == 0))
        def _():
            gather.start_diagonal_after_neighbours(w4_ref, send_sems, recv_sems)

        @pl.when(kk == 0)
        def _():
            zt = _padded_tile(z_ref[...], head_ref[...], i)
            rstd = lax.rsqrt(jnp.mean(zt * zt, axis=-1, keepdims=True) + EPS)
            h = (zt * rstd * nw_ref[...]).astype(BF16)
            h_all[pl.ds(pl.multiple_of(i * tm, 16), tm), :] = h
            h_ref[...] = h

        @pl.when((kk == 2) & (i == 0))
        def _():
            gather.pass_on(0, w4_ref, send_sems, recv_sems)
            gather.pass_on(1, w4_ref, send_sems, recv_sems)
            gather.await_sibling(0, w4_ref, send_sems, recv_sems)

        @pl.when((kk == 4) & (i == 0))
        def _():
            gather.await_sibling(1, w4_ref, send_sems, recv_sems)

        @pl.when((kk == 5) & (i == 0))
        def _():
            gather.pass_on(2, w4_ref, send_sems, recv_sems)

        @pl.when((kk == 6) & (i == 0))
        def _():
            gather.await_sibling(2, w4_ref, send_sems, recv_sems)

        def weights(which):
            seg = order_ref[2 * (kk // 2) + which]
            return pltpu.make_async_copy(
                w4_ref.at[seg // 2, :, pl.ds(pl.multiple_of((seg % 2) * D_MODEL, D_MODEL), D_MODEL)],
                w_buf.at[which], w_sem.at[which])

        @pl.when((i == 0) & (kk % 2 == 0))
        def _():
            weights(0).start()
            weights(1).start()
            weights(0).wait()

        @pl.when((i == 0) & (kk % 2 == 1))
        def _():
            weights(1).wait()

        p_ref[0] = _dot(h_all[pl.ds(pl.multiple_of(i * tm, 16), tm), :], w_buf[kk % 2]) + b_ref[...]

        @pl.when((kk == N_SEG - 1) & (i == nt - 1))
        def _():
            gather.finish(w4_ref, send_sems, recv_sems, which=(2,))

    first_pass = lambda kk, i, order_ref: (jnp.where(kk == 0, i, nt - 1), 0)
    return pl.pallas_call(
        body, name="in_proj",
        grid_spec=pltpu.PrefetchScalarGridSpec(
            num_scalar_prefetch=1, grid=(N_SEG, nt),
            in_specs=[
                _token_window(tm, lambda kk, i, order_ref: jnp.where(kk == 0, i, nt - 1)),
                pl.BlockSpec((1, D_MODEL), lambda kk, i, order_ref: (0, 0)),
                pl.BlockSpec((1, D_MODEL), lambda kk, i, order_ref: (0, order_ref[kk])),
                ANY, ANY,
            ],
            out_specs=[
                pl.BlockSpec((tm, D_MODEL), first_pass),
                pl.BlockSpec((1, tm, D_MODEL), lambda kk, i, order_ref: (order_ref[kk], i, 0)),
                ANY,
                pl.BlockSpec((FIRST_TOKEN_ROW, D_MODEL), lambda kk, i, order_ref: (0, 0)),
                ANY,
            ],
            scratch_shapes=[
                pltpu.VMEM((rows, D_MODEL), BF16),
                pltpu.VMEM((2, D_MODEL, D_MODEL), BF16),
                pltpu.SemaphoreType.DMA((2,)),
            ] + gather.semaphores() + [
                pltpu.SemaphoreType.DMA((N_CHIPS - 1,)), pltpu.SemaphoreType.DMA((N_CHIPS - 1,)),
                pltpu.SemaphoreType.DMA,
            ]),
        out_shape=[
            jax.ShapeDtypeStruct((rows, D_MODEL), BF16),
            jax.ShapeDtypeStruct((N_SEG, rows, D_MODEL), F32),
            jax.ShapeDtypeStruct(w4.shape, w4.dtype),
            jax.ShapeDtypeStruct((FIRST_TOKEN_ROW, D_MODEL), F32),
            jax.ShapeDtypeStruct(m4.shape, m4.dtype),
        ],
        input_output_aliases={4: 2, 5: 4},
        compiler_params=_params(("arbitrary", "arbitrary")),
    )(seg_order, tokens, norm_w, b_in, w4, m4)


def _hgrn_forward(p3, lb_logits, wexp2, masks2, blob4, rows):
    n_chunks = rows // CHUNK
    cpb = _tile(n_chunks, 13, mult=1)
    rb_rows = cpb * CHUNK
    n_rb = n_chunks // cpb
    lanes = cpb * HEAD_DIM
    hps = FORWARD_HEADS_PER_STEP
    n_hb = N_HEADS // hps
    width = hps * HEAD_DIM
    gather = _ShardGather(blob4.shape[1])

    def body(q_ref, fz_ref, v_ref, lbl_ref, wexp_ref, mask_ref, b_in_ref, o_ref, s_ref, e16_ref, a2_ref, b4_ref,
             st_all, e_all, u_all, q_all, kk_all, v_all, send_sems, recv_sems):
        rb = pl.program_id(1)

        @pl.when((pl.program_id(0) == 0) & (rb == 0))
        def _():
            gather.start(b4_ref, send_sems, recv_sems)

        @pl.when(rb == 0)
        def _():
            st_all[...] = jnp.zeros_like(st_all)

        for j in range(hps):
            cols = pl.ds(j * HEAD_DIM, HEAD_DIM)
            one_head(rb, q_ref.at[0, :, cols], fz_ref.at[0, :, cols], v_ref.at[0, :, cols], lbl_ref.at[:, cols],
                     wexp_ref, mask_ref, o_ref.at[:, cols], s_ref.at[j], e16_ref.at[j, 0], a2_ref.at[:, cols],
                     st_all.at[j], e_all.at[j], u_all.at[j], q_all.at[j], kk_all.at[j], v_all.at[j])

        @pl.when((pl.program_id(0) == n_hb // 2) & (rb == 0))
        def _():
            for j in range(N_CHIPS - 1):
                gather.pass_on(j, b4_ref, send_sems, recv_sems)

        @pl.when((pl.program_id(0) == n_hb - 1) & (rb == n_rb - 1))
        def _():
            for j in range(N_CHIPS - 1):
                gather.await_sibling(j, b4_ref, send_sems, recv_sems)
            gather.finish(b4_ref, send_sems, recv_sems)

    def one_head(rb, q_ref, fz_ref, v_ref, lbl_ref, wexp_ref, mask_ref, o_ref, s_ref, e16_ref, a2_ref,
                 st_ref, e_ref, u_ref, q_s, kk_s, v_s):
        lb = _lower_bound(lbl_ref[...])
        row = rb * rb_rows + lax.broadcasted_iota(jnp.int32, (rb_rows, 1), 0)
        valid = row >= PAD_ROWS
        sg, sn = _sigmoid_pair(fz_ref[...])
        g = jnp.where(valid, jnp.log(lb + (1.0 - lb) * sg), 0.0)
        kk_s[...] = jnp.where(valid, (1.0 - lb) * sn, 0.0)
        q_s[...] = jnp.where(valid, q_ref[...], 0.0)
        v_s[...] = jnp.where(valid, v_ref[...], 0.0).astype(BF16)
        hi = g.astype(BF16)
        mid = (g - hi.astype(F32)).astype(BF16)
        g2 = jnp.concatenate(
            [jnp.concatenate([hi[b * CHUNK:(b + 1) * CHUNK], mid[b * CHUNK:(b + 1) * CHUNK]], axis=0)
             for b in range(cpb)], axis=1)
        e_ref[...] = jnp.exp(_dot(wexp_ref[...], g2))
        e16_ref[...] = e_ref[...].astype(BF16)

        def contribution(b, carry):
            r0 = pl.multiple_of(b * CHUNK, CHUNK)
            l0 = pl.multiple_of(b * HEAD_DIM, HEAD_DIM)
            kc16 = (kk_s[pl.ds(r0, CHUNK), :] * e_ref[CHUNK:2 * CHUNK, pl.ds(l0, HEAD_DIM)]).astype(BF16)
            u_ref[b] = _dot_tn(v_s[pl.ds(r0, CHUNK), :], kc16)
            return carry

        lax.fori_loop(0, cpb, contribution, 0, unroll=LOCAL_UNROLL)

        def recur(b, st):
            l0 = pl.multiple_of(b * HEAD_DIM, HEAD_DIM)
            s_ref[b] = st
            return st * e_ref[CHUNK - 1:CHUNK, pl.ds(l0, HEAD_DIM)] + u_ref[b]

        st_ref[...] = lax.fori_loop(0, cpb, recur, st_ref[...], unroll=LOCAL_UNROLL)

        zeros16 = jnp.zeros((CHUNK, HEAD_DIM), BF16)

        def local(b, carry):
            r0 = pl.multiple_of(b * CHUNK, CHUNK)
            l0 = pl.multiple_of(b * HEAD_DIM, HEAD_DIM)
            q = q_s[pl.ds(r0, CHUNK), :]
            kk = kk_s[pl.ds(r0, CHUNK), :]
            v16 = v_s[pl.ds(r0, CHUNK), :]

            def scaled(entry):
                if entry == 0:
                    return q.astype(BF16), kk.astype(BF16)
                e_m = e_ref[(1 + entry) * CHUNK:(2 + entry) * CHUNK, pl.ds(l0, HEAD_DIM)]
                return (q * e_m).astype(BF16), (kk * e_m).astype(BF16)

            a2 = jnp.zeros((CHUNK, 2 * CHUNK), F32)
            for p, (ea, eb) in enumerate(LEVEL_PAIRS):
                qa, ka = scaled(ea)
                if eb is None:
                    prod = _dot_nt(qa, jnp.concatenate([ka, zeros16], axis=0))
                else:
                    qb_, kb_ = scaled(eb)
                    rhs = jnp.concatenate([jnp.concatenate([ka, zeros16], axis=1),
                                           jnp.concatenate([zeros16, kb_], axis=1)], axis=0)
                    prod = _dot_nt(jnp.concatenate([qa, qb_], axis=1), rhs)
                a2 = a2 + mask_ref[p] * prod
            a2_16 = a2.astype(BF16)
            a2_ref[pl.ds(r0, CHUNK), :] = a2_16
            qb16 = (q * e_ref[0:CHUNK, pl.ds(l0, HEAD_DIM)]).astype(BF16)
            o_ref[pl.ds(r0, CHUNK), :] = (_dot(a2_16, jnp.concatenate([v16, v16], axis=0))
                                          + _dot_nt(qb16, s_ref[b].astype(BF16)))
            return carry

        lax.fori_loop(0, cpb, local, 0, unroll=LOCAL_UNROLL)

    head_block = lambda seg: pl.BlockSpec((1, rb_rows, width), lambda h, r: (seg, r, h))
    return pl.pallas_call(
        body, name="hgrn_forward",
        grid=(n_hb, n_rb),
        in_specs=[
            head_block(0), head_block(1), head_block(2),
            pl.BlockSpec((2, width), lambda h, r: (0, h)),
            pl.BlockSpec((N_EXP * CHUNK, 2 * CHUNK), lambda h, r: (0, 0)),
            pl.BlockSpec((len(LEVEL_PAIRS), CHUNK, 2 * CHUNK), lambda h, r: (0, 0, 0)),
            ANY,
        ],
        out_specs=[
            pl.BlockSpec((rb_rows, width), lambda h, r: (r, h)),
            pl.BlockSpec((hps, cpb, HEAD_DIM, HEAD_DIM), lambda h, r: (h, r, 0, 0)),
            pl.BlockSpec((hps, 1, N_EXP * CHUNK, lanes), lambda h, r: (h, r, 0, 0)),
            pl.BlockSpec((rb_rows, width), lambda h, r: (r, h)),
            ANY,
        ],
        out_shape=[
            jax.ShapeDtypeStruct((rows, D_MODEL), F32),
            jax.ShapeDtypeStruct((N_HEADS, n_chunks, HEAD_DIM, HEAD_DIM), F32),
            jax.ShapeDtypeStruct((N_HEADS, n_rb, N_EXP * CHUNK, lanes), BF16),
            jax.ShapeDtypeStruct((rows, D_MODEL), BF16),
            jax.ShapeDtypeStruct(blob4.shape, blob4.dtype),
        ],
        input_output_aliases={6: 4},
        scratch_shapes=[
            pltpu.VMEM((hps, HEAD_DIM, HEAD_DIM), F32),
            pltpu.VMEM((hps, N_EXP * CHUNK, lanes), F32),
            pltpu.VMEM((hps, cpb, HEAD_DIM, HEAD_DIM), F32),
            pltpu.VMEM((hps, rb_rows, HEAD_DIM), F32),
            pltpu.VMEM((hps, rb_rows, HEAD_DIM), F32),
            pltpu.VMEM((hps, rb_rows, HEAD_DIM), BF16),
        ] + gather.semaphores(),
        compiler_params=_params(("arbitrary", "arbitrary")),
    )(p3, p3, p3, lb_logits, wexp2, masks2, blob4)


def _hgrn_backward(p3, d_o, states, e16, a2, lb_logits, wexp_t, masks2, dw16, blob16, rows):
    n_chunks = rows // CHUNK
    cpb = _tile(n_chunks, 13, mult=1)
    rb_rows = cpb * CHUNK
    n_rb = n_chunks // cpb
    lanes = cpb * HEAD_DIM
    exchange = _GradExchange(SEGS_MIX, with_blob=True)

    hps = BACKWARD_HEADS_PER_STEP
    n_hb = N_HEADS // hps
    width = hps * HEAD_DIM

    def body(q_ref, fz_ref, v_ref, do_ref, s_ref, e_ref, a2_ref, lbl_ref, wexpt_ref, mask_ref, dw_ref, blob_ref,
             dp_ref, dlb_ref, rxw_ref, rxb_ref, *scratch):
        per_head, (send_sems, recv_sems) = scratch[:-2], scratch[-2:]
        step = pl.program_id(1)
        rb = n_rb - 1 - step

        @pl.when((pl.program_id(0) == 0) & (step == 0))
        def _():
            exchange.start(dw_ref, rxw_ref, blob_ref, rxb_ref, send_sems, recv_sems)

        for j in range(hps):
            cols = pl.ds(j * HEAD_DIM, HEAD_DIM)
            one_head(step, rb, q_ref.at[0, :, cols], fz_ref.at[0, :, cols], v_ref.at[0, :, cols], do_ref.at[:, cols],
                     s_ref.at[j], e_ref.at[j, 0], a2_ref.at[:, cols], lbl_ref.at[:, cols], wexpt_ref, mask_ref,
                     dp_ref.at[:, :, cols], dlb_ref.at[:, cols], *[ref.at[j] for ref in per_head])

        @pl.when((pl.program_id(0) == n_hb - 1) & (step == n_rb - 1))
        def _():
            exchange.wait(dw_ref, rxw_ref, blob_ref, rxb_ref, send_sems, recv_sems)

    def one_head(step, rb, q_ref, fz_ref, v_ref, do_ref, s_ref, e_ref, a2_ref, lbl_ref, wexpt_ref, mask_ref,
                 dp_ref, dlb_ref, dst_ref, g_ref, dsn_ref, q_s, kk_s, v_s, do_s, dq_s, dkk_s, dg_s, dx_s, da2_s):
        @pl.when(step == 0)
        def _():
            dst_ref[...] = jnp.zeros_like(dst_ref)
            dlb_ref[...] = jnp.zeros_like(dlb_ref)

        lb = _lower_bound(lbl_ref[...])
        row = rb * rb_rows + lax.broadcasted_iota(jnp.int32, (rb_rows, 1), 0)
        valid = row >= PAD_ROWS
        sg, sn = _sigmoid_pair(fz_ref[...])
        f = lb + (1.0 - lb) * sg
        g = jnp.where(valid, jnp.log(f), 0.0)
        kk_s[...] = jnp.where(valid, (1.0 - lb) * sn, 0.0)
        q_s[...] = jnp.where(valid, q_ref[...], 0.0)
        v_s[...] = jnp.where(valid, v_ref[...], 0.0).astype(BF16)
        do_s[...] = do_ref[...].astype(BF16)
        e_last_all = jnp.exp(jnp.concatenate(
            [jnp.sum(g[b * CHUNK:(b + 1) * CHUNK], axis=0, keepdims=True) for b in range(cpb)], axis=0))
        last_row = lax.broadcasted_iota(jnp.int32, (CHUNK, 1), 0) == CHUNK - 1
        zeros16 = jnp.zeros((CHUNK, HEAD_DIM), BF16)

        def factor(block, l0):
            return e_ref[block * CHUNK:(block + 1) * CHUNK, pl.ds(l0, HEAD_DIM)].astype(F32)

        def contribution(b, carry):
            r0 = pl.multiple_of(b * CHUNK, CHUNK)
            l0 = pl.multiple_of(b * HEAD_DIM, HEAD_DIM)
            qb16 = (q_s[pl.ds(r0, CHUNK), :] * factor(0, l0)).astype(BF16)
            g_ref[b] = _dot_tn(do_s[pl.ds(r0, CHUNK), :], qb16)
            return carry

        lax.fori_loop(0, cpb, contribution, 0, unroll=LOCAL_UNROLL)

        cur = dst_ref[...]
        for b in reversed(range(cpb)):
            dsn_ref[b] = cur
            cur = cur * e_last_all[b:b + 1, :] + g_ref[b]
        dst_ref[...] = cur

        def through_state(b, carry):
            r0 = pl.multiple_of(b * CHUNK, CHUNK)
            l0 = pl.multiple_of(b * HEAD_DIM, HEAD_DIM)
            v16 = v_s[pl.ds(r0, CHUNK), :]
            do16 = do_s[pl.ds(r0, CHUNK), :]
            st = s_ref[b]
            dsn = dsn_ref[b]
            dsn16 = dsn.astype(BF16)
            e_b, e_c = factor(0, l0), factor(1, l0)
            qb, kc = q_s[pl.ds(r0, CHUNK), :] * e_b, kk_s[pl.ds(r0, CHUNK), :] * e_c

            t = _dot_tn(a2_ref[pl.ds(r0, CHUNK), :], do16)
            dv = t[0:CHUNK] + t[CHUNK:2 * CHUNK] + _dot_nt(kc.astype(BF16), dsn16)
            dp_ref[2, pl.ds(r0, CHUNK), :] = dv.astype(BF16)
            da2_s[pl.ds(r0, CHUNK), :] = _dot_nt(do16, jnp.concatenate([v16, v16], axis=0))
            dqb = _dot(do16, st.astype(BF16))
            dkc = _dot(v16, dsn16)
            de = jnp.sum(dsn * st, axis=0, keepdims=True) * e_b[CHUNK - 1:CHUNK, :]
            dq_s[pl.ds(r0, CHUNK), :] = e_b * dqb
            dkk_s[pl.ds(r0, CHUNK), :] = e_c * dkc
            dx_s[0:CHUNK, pl.ds(l0, HEAD_DIM)] = (qb * dqb + jnp.where(last_row, de, 0.0)).astype(BF16)
            dx_s[CHUNK:2 * CHUNK, pl.ds(l0, HEAD_DIM)] = (kc * dkc).astype(BF16)
            return carry

        lax.fori_loop(0, cpb, through_state, 0, unroll=BACKWARD_UNROLL)

        def local(b, carry):
            r0 = pl.multiple_of(b * CHUNK, CHUNK)
            l0 = pl.multiple_of(b * HEAD_DIM, HEAD_DIM)
            q = q_s[pl.ds(r0, CHUNK), :]
            kk = kk_s[pl.ds(r0, CHUNK), :]
            da2 = da2_s[pl.ds(r0, CHUNK), :]
            dq = dq_s[pl.ds(r0, CHUNK), :]
            dkk = dkk_s[pl.ds(r0, CHUNK), :]

            def scaled(entry):
                if entry == 0:
                    return q, kk, None
                e_m = factor(1 + entry, l0)
                return q * e_m, kk * e_m, e_m

            for p, (ea, eb) in enumerate(LEVEL_PAIRS):
                dm = mask_ref[p] * da2
                dm_t = dm.T.astype(BF16)
                qa, ka, e_a = scaled(ea)
                if eb is None:
                    rhs_k = jnp.concatenate([jnp.concatenate([ka.astype(BF16), zeros16], axis=1),
                                             jnp.concatenate([zeros16, zeros16], axis=1)], axis=0)
                else:
                    qb_, kb_, e_bb = scaled(eb)
                    rhs_k = jnp.concatenate([jnp.concatenate([ka.astype(BF16), zeros16], axis=1),
                                             jnp.concatenate([zeros16, kb_.astype(BF16)], axis=1)], axis=0)
                dq2 = _dot(dm.astype(BF16), rhs_k)
                parts = [(ea, qa, ka, e_a, dq2[:, :HEAD_DIM], _dot(dm_t[0:CHUNK], qa.astype(BF16)))]
                if eb is not None:
                    parts.append((eb, qb_, kb_, e_bb, dq2[:, HEAD_DIM:],
                                  _dot(dm_t[CHUNK:2 * CHUNK], qb_.astype(BF16))))
                for entry, q_m, k_m, e_m, dq_m, dk_m in parts:
                    if entry == 0:
                        dq = dq + dq_m
                        dkk = dkk + dk_m
                    else:
                        dq = dq + e_m * dq_m
                        dkk = dkk + e_m * dk_m
                        dx_s[(1 + entry) * CHUNK:(2 + entry) * CHUNK, pl.ds(l0, HEAD_DIM)] = (
                            q_m * dq_m + k_m * dk_m).astype(BF16)
            dq_s[pl.ds(r0, CHUNK), :] = dq
            dkk_s[pl.ds(r0, CHUNK), :] = dkk
            return carry

        lax.fori_loop(0, cpb, local, 0, unroll=BACKWARD_UNROLL)

        dg_all = _dot(wexpt_ref[...], dx_s[...])
        for b in range(cpb):
            dg_s[b * CHUNK:(b + 1) * CHUNK, :] = dg_all[:, b * HEAD_DIM:(b + 1) * HEAD_DIM]
        t = jnp.where(valid, dg_s[...] / f - dkk_s[...], 0.0)
        dlb_ref[...] += jnp.sum(sn * t, axis=0, keepdims=True)
        dp_ref[0] = jnp.where(valid, dq_s[...], 0.0).astype(BF16)
        dp_ref[1] = ((1.0 - lb) * sg * sn * t).astype(BF16)

    head_block = lambda seg: pl.BlockSpec((1, rb_rows, width), lambda h, s: (seg, n_rb - 1 - s, h))
    row_block = pl.BlockSpec((rb_rows, width), lambda h, s: (n_rb - 1 - s, h))
    return pl.pallas_call(
        body, name="hgrn_backward",
        grid=(n_hb, n_rb),
        in_specs=[
            head_block(0), head_block(1), head_block(2),
            row_block,
            pl.BlockSpec((hps, cpb, HEAD_DIM, HEAD_DIM), lambda h, s: (h, n_rb - 1 - s, 0, 0)),
            pl.BlockSpec((hps, 1, N_EXP * CHUNK, lanes), lambda h, s: (h, n_rb - 1 - s, 0, 0)),
            row_block,
            pl.BlockSpec((2, width), lambda h, s: (0, h)),
            pl.BlockSpec((CHUNK, N_EXP * CHUNK), lambda h, s: (0, 0)),
            pl.BlockSpec((len(LEVEL_PAIRS), CHUNK, 2 * CHUNK), lambda h, s: (0, 0, 0)),
            ANY, ANY,
        ],
        out_specs=[
            pl.BlockSpec((3, rb_rows, width), lambda h, s: (0, n_rb - 1 - s, h)),
            pl.BlockSpec((1, width), lambda h, s: (0, h)),
            ANY, ANY,
        ],
        out_shape=[
            jax.ShapeDtypeStruct((3, rows, D_MODEL), BF16),
            jax.ShapeDtypeStruct((1, D_MODEL), F32),
            exchange.landing_w(), exchange.landing_blob(blob16),
        ],
        scratch_shapes=[
            pltpu.VMEM((hps, HEAD_DIM, HEAD_DIM), F32),
            pltpu.VMEM((hps, cpb, HEAD_DIM, HEAD_DIM), F32),
            pltpu.VMEM((hps, cpb, HEAD_DIM, HEAD_DIM), F32),
            pltpu.VMEM((hps, rb_rows, HEAD_DIM), F32),
            pltpu.VMEM((hps, rb_rows, HEAD_DIM), F32),
            pltpu.VMEM((hps, rb_rows, HEAD_DIM), BF16),
            pltpu.VMEM((hps, rb_rows, HEAD_DIM), BF16),
            pltpu.VMEM((hps, rb_rows, HEAD_DIM), F32),
            pltpu.VMEM((hps, rb_rows, HEAD_DIM), F32),
            pltpu.VMEM((hps, rb_rows, HEAD_DIM), F32),
            pltpu.VMEM((hps, N_EXP * CHUNK, lanes), BF16),
            pltpu.VMEM((hps, rb_rows, 2 * CHUNK), F32),
        ] + exchange.semaphores(),
        compiler_params=_params(("arbitrary", "arbitrary")),
    )(p3, p3, p3, d_o, states, e16, a2, lb_logits, wexp_t, masks2, dw16, blob16)


def _sigmoid(x):
    return 1.0 / (1.0 + jnp.exp(-x))


def _silu_and_grad(x):
    s = _sigmoid(x)
    return x * s, s * (1.0 + x * (1.0 - s))


def _window_sum(ext, width, forward_looking):
    n = ext.shape[0]
    s = ext
    step = 1
    while step < width:
        s = s + pltpu.roll(s, (n - step) if forward_looking else step, 0)
        step *= 2
    return s


def _mixers(o, p3, tokens, head, tgt, wdh, wdp, wout, poolw, hg_w, pool_scale, final_w, rows):
    tm = _tile(rows, 208)
    nt = rows // tm
    halo_blocks = tm // HALO
    n_grp = len(POOL_WINDOWS)
    q_rows = D_MODEL // N_CHIPS
    blob_rows = 3 * q_rows + n_grp * POOL_GDIM * POOL_GDIM // (N_CHIPS * D_MODEL)

    def body(o_ref, ghg_ref, u_ref, gpl_ref, mhg_ref, mpl_ref, uh_ref, z_ref, t_ref,
             wdh_ref, wdp_ref, wout_ref, pw_ref, hgw_ref, ps_ref, fw_ref, head_ref,
             do_ref, dz2_ref, dp_ref, blob_ref, dpw_ref, small_ref, carry_ref):
        step = pl.program_id(0)
        tile = nt - 1 - step

        def add_to_blob(piece, dw):
            for k in range(N_CHIPS):
                blob_ref[k, piece * q_rows:(piece + 1) * q_rows, :] += dw[k * q_rows:(k + 1) * q_rows]

        @pl.when(step == 0)
        def _():
            blob_ref[...] = jnp.zeros_like(blob_ref)
            dpw_ref[...] = jnp.zeros_like(dpw_ref)
            small_ref[...] = jnp.zeros_like(small_ref)
            carry_ref[...] = jnp.zeros_like(carry_ref)

        row = tile * tm + lax.broadcasted_iota(jnp.int32, (tm, 1), 0)
        real = row >= PAD_ROWS
        pos1 = jnp.maximum(row - PAD_ROWS + 1, 1).astype(F32)

        u = jnp.where(real, u_ref[0], 0.0)
        halo_row = tile * tm - HALO + lax.broadcasted_iota(jnp.int32, (HALO, 1), 0)
        uh = jnp.where(halo_row >= PAD_ROWS, uh_ref[0], 0.0)
        ext = jnp.concatenate([uh, u], axis=0)
        pooled, inv_cnt, mixed = [], [], []
        for g, w in enumerate(POOL_WINDOWS):
            cols = slice(g * POOL_GDIM, (g + 1) * POOL_GDIM)
            inv = 1.0 / jnp.minimum(pos1, float(w))
            ws = _window_sum(ext[:, cols], w, False)[HALO:]
            pg = (ws * inv - u[:, cols]).astype(BF16)
            pooled.append(pg)
            inv_cnt.append(inv)
            mixed.append(_dot(pg, pw_ref[g]))
        mixed = jnp.concatenate(mixed, axis=1)
        gpl = gpl_ref[0]
        sp, dsp = _silu_and_grad(gpl)
        ps = ps_ref[...]
        a_pool = (mixed * ps * sp).astype(BF16)
        y_pool = _dot(a_pool, wdp_ref[...])

        o = o_ref[...]
        o_hat, rstd_h = [], []
        for h in range(N_HEADS):
            oh = o[:, h * HEAD_DIM:(h + 1) * HEAD_DIM]
            r = lax.rsqrt(jnp.mean(oh * oh, axis=-1, keepdims=True) + EPS)
            rstd_h.append(r)
            o_hat.append(oh * r)
        o_hat = jnp.concatenate(o_hat, axis=1)
        hgw = hgw_ref[...]
        o_n = o_hat * hgw
        ghg = ghg_ref[0]
        sh, dsh = _silu_and_grad(ghg)
        a_hg = (o_n * sh).astype(BF16)
        y_hg = _dot(a_hg, wdh_ref[...])

        s_mh = _sigmoid(mhg_ref[0])
        s_mp = _sigmoid(mpl_ref[0])
        merged = (s_mh * y_hg + s_mp * y_pool).astype(BF16)
        z2 = _padded_tile(z_ref[...], head_ref[...], tile) + _dot(merged, wout_ref[...])
        rstd2 = lax.rsqrt(jnp.mean(z2 * z2, axis=-1, keepdims=True) + EPS)
        zh = z2 * rstd2
        fw = fw_ref[...]
        target = _padded_tile(t_ref[...], jnp.zeros((FIRST_TOKEN_ROW, D_MODEL), F32), tile)
        err = jnp.where(row >= FIRST_TOKEN_ROW, zh * fw - target, 0.0)
        small_ref[ROW_LOSS:ROW_LOSS + 1, :] += jnp.sum(err * err, axis=0, keepdims=True) * (0.5 / D_MODEL)
        dy = err * (1.0 / D_MODEL)

        small_ref[ROW_FINAL_W:ROW_FINAL_W + 1, :] += jnp.sum(dy * zh, axis=0, keepdims=True)
        uu = dy * fw
        dz2 = rstd2 * (uu - zh * jnp.mean(uu * zh, axis=-1, keepdims=True))
        dz2_ref[...] = dz2
        dz2_16 = dz2.astype(BF16)
        dmerged = _dot_nt(dz2_16, wout_ref[...])
        add_to_blob(2, _dot_tn(merged, dz2_16))
        dy_hg = (s_mh * dmerged).astype(BF16)
        dy_pool = (s_mp * dmerged).astype(BF16)
        dp_ref[3] = (dmerged * y_hg * s_mh * (1.0 - s_mh)).astype(BF16)
        dp_ref[4] = (dmerged * y_pool * s_mp * (1.0 - s_mp)).astype(BF16)

        da_hg = _dot_nt(dy_hg, wdh_ref[...])
        add_to_blob(0, _dot_tn(a_hg, dy_hg))
        dp_ref[0] = (da_hg * o_n * dsh).astype(BF16)
        do_n = da_hg * sh
        small_ref[ROW_HG_W:ROW_HG_W + 1, :] += jnp.sum(do_n * o_hat, axis=0, keepdims=True)
        d_hat = do_n * hgw
        for h in range(N_HEADS):
            cols = slice(h * HEAD_DIM, (h + 1) * HEAD_DIM)
            dh_, oh_ = d_hat[:, cols], o_hat[:, cols]
            do_ref[:, cols] = rstd_h[h] * (dh_ - oh_ * jnp.mean(dh_ * oh_, axis=-1, keepdims=True))

        da_pool = _dot_nt(dy_pool, wdp_ref[...])
        add_to_blob(1, _dot_tn(a_pool, dy_pool))
        small_ref[ROW_POOL_SCALE:ROW_POOL_SCALE + 1, :] += jnp.sum(da_pool * mixed * sp, axis=0, keepdims=True)
        dp_ref[2] = (da_pool * mixed * ps * dsp).astype(BF16)
        dmixed = (da_pool * ps * sp).astype(BF16)
        carry = carry_ref[...]
        du, new_carry = [], []
        for g, w in enumerate(POOL_WINDOWS):
            cols = slice(g * POOL_GDIM, (g + 1) * POOL_GDIM)
            dmg = dmixed[:, cols]
            dpooled = _dot_nt(dmg, pw_ref[g])
            dpw_ref[g] += _dot_tn(pooled[g], dmg)
            dps = dpooled * inv_cnt[g]
            ext_b = jnp.concatenate([dps, carry[:, cols]], axis=0)
            du.append(_window_sum(ext_b, w, True)[:tm] - dpooled)
            new_carry.append(dps[:HALO])
        dp_ref[1] = jnp.where(real, jnp.concatenate(du, axis=1), 0.0).astype(BF16)
        carry_ref[...] = jnp.concatenate(new_carry, axis=1)

    row_block = pl.BlockSpec((tm, D_MODEL), lambda s: (nt - 1 - s, 0))
    seg_block = lambda seg: pl.BlockSpec((1, tm, D_MODEL), lambda s: (seg, nt - 1 - s, 0))
    whole = pl.BlockSpec(memory_space=pltpu.VMEM)
    return pl.pallas_call(
        body, name="mixers",
        grid=(nt,),
        in_specs=[
            row_block, seg_block(3), seg_block(4), seg_block(5), seg_block(6), seg_block(7),
            pl.BlockSpec((1, HALO, D_MODEL),
                         lambda s: (4, jnp.maximum((nt - 1 - s) * halo_blocks - 1, 0), 0)),
            _token_window(tm, lambda s: nt - 1 - s), _token_window(tm, lambda s: nt - 1 - s),
            whole, whole, whole, whole, whole, whole, whole, whole,
        ],
        out_specs=[
            row_block, row_block,
            pl.BlockSpec((5, tm, D_MODEL), lambda s: (0, nt - 1 - s, 0)),
            whole, whole, whole,
        ],
        out_shape=[
            jax.ShapeDtypeStruct((rows, D_MODEL), F32),
            jax.ShapeDtypeStruct((rows, D_MODEL), F32),
            jax.ShapeDtypeStruct((5, rows, D_MODEL), BF16),
            jax.ShapeDtypeStruct((N_CHIPS, blob_rows, D_MODEL), F32),
            jax.ShapeDtypeStruct((n_grp, POOL_GDIM, POOL_GDIM), F32),
            jax.ShapeDtypeStruct((SMALL_ROWS, D_MODEL), F32),
        ],
        scratch_shapes=[pltpu.VMEM((HALO, D_MODEL), F32)],
        compiler_params=_params(("arbitrary",)),
    )(o, p3, p3, p3, p3, p3, p3, tokens, tgt, wdh, wdp, wout, poolw, hg_w, pool_scale, final_w, head)


def _seg_specs(tm, row_of, seg_of):
    def spec_a(*g):
        k = seg_of(*g)
        return (jnp.minimum(k, 2), jnp.where(k < 3, row_of(*g), 0), 0)

    def spec_b(*g):
        k = seg_of(*g)
        return (jnp.maximum(k - 3, 0), jnp.where(k >= 3, row_of(*g), 0), 0)

    return pl.BlockSpec((1, tm, D_MODEL), spec_a), pl.BlockSpec((1, tm, D_MODEL), spec_b)


def _in_proj_weight_grad(h, dp, rows, name):
    n_seg = dp.shape[0]
    tm = _tile(rows, 1040)
    nt = rows // tm
    half = D_MODEL // 2

    def body(h_ref, dp_ref, part_ref, part16_ref, db_ref, acc_ref, bacc_ref, stage_ref, land_ref,
             send_sems, recv_sems):
        k, i = pl.program_id(0), pl.program_id(1)
        x, y, c = lax.axis_index("x"), lax.axis_index("y"), lax.axis_index("c")

        def to_sibling(seg):
            return pltpu.make_async_remote_copy(
                src_ref=stage_ref.at[seg], dst_ref=land_ref.at[seg], send_sem=send_sems.at[seg],
                recv_sem=recv_sems.at[seg], device_id=(x, y, 1 - c), device_id_type=MESH)

        @pl.when(i == 0)
        def _():
            acc_ref[...] = jnp.zeros_like(acc_ref)
            bacc_ref[...] = jnp.zeros_like(bacc_ref)

        dpt = dp_ref[0]
        acc_ref[...] += _dot_tn(h_ref[...], dpt)
        bacc_ref[...] += jnp.sum(dpt.astype(F32), axis=0, keepdims=True)

        @pl.when(i == nt - 1)
        def _():
            db_ref[0] = bacc_ref[...]
            part_ref[k] = acc_ref[pl.ds(pl.multiple_of(c * half, half), half), :]
            stage_ref[k] = acc_ref[pl.ds(pl.multiple_of((1 - c) * half, half), half), :].astype(BF16)
            to_sibling(k).start()

        @pl.when((k == n_seg - 1) & (i == nt - 1))
        def _():
            for seg in range(n_seg):
                to_sibling(seg).wait_recv()
                total = part_ref[seg] + land_ref[seg].astype(F32)
                part_ref[seg] = total
                part16_ref[seg] = total.astype(BF16)
            for seg in range(n_seg):
                to_sibling(seg).wait_send()

    whole = pl.BlockSpec(memory_space=pltpu.VMEM)
    return pl.pallas_call(
        body, name=name,
        grid=(n_seg, nt),
        in_specs=[pl.BlockSpec((tm, D_MODEL), lambda k, i: (i, 0)),
                  pl.BlockSpec((1, tm, D_MODEL), lambda k, i: (k, i, 0))],
        out_specs=[whole, whole, pl.BlockSpec((1, 1, D_MODEL), lambda k, i: (k, 0, 0))],
        out_shape=[
            jax.ShapeDtypeStruct((n_seg, half, D_MODEL), F32),
            jax.ShapeDtypeStruct((n_seg, half, D_MODEL), BF16),
            jax.ShapeDtypeStruct((n_seg, 1, D_MODEL), F32),
        ],
        scratch_shapes=[
            pltpu.VMEM((D_MODEL, D_MODEL), F32), pltpu.VMEM((1, D_MODEL), F32),
            pltpu.VMEM((n_seg, half, D_MODEL), BF16),
            pltpu.VMEM((n_seg, half, D_MODEL), BF16),
            pltpu.SemaphoreType.DMA((n_seg,)), pltpu.SemaphoreType.DMA((n_seg,)),
        ],
        compiler_params=_params(("arbitrary", "arbitrary")),
    )(h, dp)


def _input_grad(dpa, dpb, w4, tokens, head, dz2, norm_w, dw16, rows):
    tm = _tile(rows, 1040)
    nt = rows // tm
    assert nt >= 2, rows
    exchange = _GradExchange(SEGS_REC, with_blob=False)

    def body(dpa_ref, dpb_ref, w_ref, z_ref, head_ref, dz2_ref, nw_ref, dw_ref, gx_ref, dmeta_ref, dnw_ref, rxw_ref,
             acc_ref, dz_buf, out_sem, send_sems, recv_sems):
        i, k = pl.program_id(0), pl.program_id(1)

        def first_tile_out():
            return pltpu.make_async_copy(dz_buf.at[pl.ds(FIRST_TOKEN_ROW, tm - FIRST_TOKEN_ROW), :],
                                         gx_ref.at[pl.ds(0, tm - FIRST_TOKEN_ROW), :], out_sem)

        def tile_out(tile):
            start = pl.multiple_of(tile * tm - FIRST_TOKEN_ROW, HALO)
            return pltpu.make_async_copy(dz_buf, gx_ref.at[pl.ds(start, tm), :], out_sem)

        @pl.when((i == 0) & (k == 0))
        def _():
            exchange.start(dw_ref, rxw_ref, None, None, send_sems, recv_sems)
            dnw_ref[...] = jnp.zeros_like(dnw_ref)

        @pl.when((i == nt - 1) & (k == N_SEG - 1))
        def _():
            exchange.wait(dw_ref, rxw_ref, None, None, send_sems, recv_sems)

        @pl.when(k == 0)
        def _():
            acc_ref[...] = jnp.zeros_like(acc_ref)

        @pl.when(k < 3)
        def _():
            acc_ref[...] += _dot_nt(dpa_ref[0], w_ref[0])

        @pl.when(k >= 3)
        def _():
            acc_ref[...] += _dot_nt(dpb_ref[0], w_ref[0])

        @pl.when(k == N_SEG - 1)
        def _():
            zt = _padded_tile(z_ref[...], head_ref[...], i)
            rstd = lax.rsqrt(jnp.mean(zt * zt, axis=-1, keepdims=True) + EPS)
            zh = zt * rstd
            dh = acc_ref[...]
            dnw_ref[...] += jnp.sum(dh * zh, axis=0, keepdims=True)
            uu = dh * nw_ref[...]
            dz = dz2_ref[...] + rstd * (uu - zh * jnp.mean(uu * zh, axis=-1, keepdims=True))

            @pl.when(i == 1)
            def _():
                first_tile_out().wait()

            @pl.when(i >= 2)
            def _():
                tile_out(i - 1).wait()

            dz_buf[...] = dz

            @pl.when(i == 0)
            def _():
                dmeta_ref[...] = dz[PAD_ROWS:FIRST_TOKEN_ROW]
                first_tile_out().start()

            @pl.when(i > 0)
            def _():
                tile_out(i).start()

            @pl.when(i == nt - 1)
            def _():
                tile_out(i).wait()

    spec_a, spec_b = _seg_specs(tm, lambda i, k: i, lambda i, k: k)
    last_only = pl.BlockSpec((tm, D_MODEL), lambda i, k: (jnp.where(k == N_SEG - 1, i, 0), 0))
    return pl.pallas_call(
        body, name="input_grad",
        grid=(nt, N_SEG),
        in_specs=[
            spec_a, spec_b,
            pl.BlockSpec((1, D_MODEL, D_MODEL), lambda i, k: (k // 2, 0, k % 2)),
            _token_window(tm, lambda i, k: jnp.where(k == N_SEG - 1, i, 0)),
            pl.BlockSpec((FIRST_TOKEN_ROW, D_MODEL), lambda i, k: (0, 0)),
            last_only,
            pl.BlockSpec((1, D_MODEL), lambda i, k: (0, 0)),
            ANY,
        ],
        out_specs=[
            ANY,
            pl.BlockSpec((N_META, D_MODEL), lambda i, k: (0, 0)),
            pl.BlockSpec((1, D_MODEL), lambda i, k: (0, 0)),
            ANY,
        ],
        out_shape=[
            jax.ShapeDtypeStruct((rows - FIRST_TOKEN_ROW, D_MODEL), F32),
            jax.ShapeDtypeStruct((N_META, D_MODEL), F32),
            jax.ShapeDtypeStruct((1, D_MODEL), F32),
            exchange.landing_w(),
        ],
        scratch_shapes=[pltpu.VMEM((tm, D_MODEL), F32), pltpu.VMEM((tm, D_MODEL), F32),
                        pltpu.SemaphoreType.DMA] + exchange.semaphores(),
        compiler_params=_params(("arbitrary", "arbitrary")),
    )(dpa, dpb, w4, tokens, head, dz2, norm_w, dw16)


def _local_step(tokens, m4, tgt, w4, blob4, seg_order, norm_w, b_in, lb_logits, hg_w, pool_scale, final_w):
    rows = FIRST_TOKEN_ROW + tokens.shape[0]
    q = D_MODEL // N_CHIPS
    n_grp = len(POOL_WINDOWS)
    pg = POOL_GDIM // N_CHIPS

    wexp2 = jnp.asarray(np.tile(_exponent_matrix(), (1, 2)), BF16)
    wexp_t = jnp.asarray(_exponent_matrix().T, BF16)
    masks2 = jnp.asarray(_paired_masks(), F32)

    h, p3, w4, head, _ = _in_proj(tokens, m4, norm_w, w4, b_in, seg_order, rows)
    o, states, e16, a2, blob4 = _hgrn_forward(p3, lb_logits, wexp2, masks2, blob4, rows)
    wdh = blob4[:, 0:q].reshape(D_MODEL, D_MODEL)
    wdp = blob4[:, q:2 * q].reshape(D_MODEL, D_MODEL)
    wout = blob4[:, 2 * q:3 * q].reshape(D_MODEL, D_MODEL)
    poolw = blob4[:, 3 * q:].reshape(N_CHIPS, n_grp, pg, POOL_GDIM).transpose(1, 0, 2, 3)
    poolw = poolw.reshape(n_grp, POOL_GDIM, POOL_GDIM)
    d_o, dz2, dpb, dblob4, dpw, small = _mixers(
        o, p3, tokens, head, tgt, wdh, wdp, wout, poolw, hg_w, pool_scale, final_w, rows)
    dpw4 = dpw.reshape(n_grp, N_CHIPS, pg, POOL_GDIM).transpose(1, 0, 2, 3)
    dpw4 = dpw4.reshape(N_CHIPS, n_grp * pg * POOL_GDIM // D_MODEL, D_MODEL)
    dblob4 = dblob4.at[:, 3 * q:, :].set(dpw4)

    dw_mix, dw_mix16, db_mix = _in_proj_weight_grad(h, dpb, rows, "in_proj_weight_grad_mix")
    dpa, dlb, rxw_mix, rx_blob = _hgrn_backward(
        p3, d_o, states, e16, a2, lb_logits, wexp_t, masks2, dw_mix16, dblob4.astype(BF16), rows)
    dw_rec, dw_rec16, db_rec = _in_proj_weight_grad(h, dpa, rows, "in_proj_weight_grad_rec")
    d_tokens, d_meta, dnw, rxw_rec = _input_grad(dpa, dpb, w4, tokens, head, dz2, norm_w, dw_rec16, rows)

    small = jnp.concatenate([
        small[ROW_LOSS:ROW_LOSS + 1],
        d_meta,
        dnw,
        db_rec.reshape(len(SEGS_REC), D_MODEL), db_mix.reshape(len(SEGS_MIX), D_MODEL),
        dlb, jnp.zeros_like(dlb),
        small[ROW_HG_W:ROW_HG_W + 1], small[ROW_POOL_SCALE:ROW_POOL_SCALE + 1],
        small[ROW_FINAL_W:ROW_FINAL_W + 1],
        jnp.zeros((SMALL_ROWS - ROW_FINAL_W - 1, D_MODEL), F32),
    ], axis=0)
    return d_tokens, (dw_rec, dw_mix, rxw_rec, rxw_mix), (dblob4, rx_blob), small


ANY = pl.BlockSpec(memory_space=pl.ANY)
MESH = pl.DeviceIdType.MESH


def _place():
    x, y, c = lax.axis_index("x"), lax.axis_index("y"), lax.axis_index("c")
    chips = [(1 - x, y), (x, 1 - y), (1 - x, 1 - y)]
    return x, y, c, chips


class _ShardGather:
    def __init__(self, rows):
        self.half = rows // 2

    def semaphores(self):
        return [pltpu.SemaphoreType.DMA((6,)), pltpu.SemaphoreType.DMA((6,))]

    def _copy(self, k, slot, to, send_sems, recv_sems):
        return pltpu.make_async_remote_copy(src_ref=slot, dst_ref=slot, send_sem=send_sems.at[k],
                                            recv_sem=recv_sems.at[k], device_id=to, device_id_type=MESH)

    def _half(self, ref4, chip, which):
        return ref4.at[chip, pl.ds(which * self.half, self.half), :]

    def start(self, ref4, send_sems, recv_sems, which=(0, 1, 2)):
        x, y, c, chips = _place()
        for j in which:
            cx, cy = chips[j]
            self._copy(j, self._half(ref4, 2 * x + y, c), (cx, cy, c), send_sems, recv_sems).start()

    def start_diagonal_after_neighbours(self, ref4, send_sems, recv_sems):
        x, y, c, chips = _place()
        for j in (0, 1):
            cx, cy = chips[j]
            self._copy(j, self._half(ref4, 2 * x + y, c), (cx, cy, c), send_sems, recv_sems).wait_send()
        self.start(ref4, send_sems, recv_sems, which=(2,))

    def pass_on(self, j, ref4, send_sems, recv_sems):
        x, y, c, chips = _place()
        cx, cy = chips[j]
        landed = self._half(ref4, 2 * cx + cy, c)
        self._copy(j, landed, (cx, cy, c), send_sems, recv_sems).wait_recv()
        self._copy(3 + j, landed, (x, y, 1 - c), send_sems, recv_sems).start()

    def await_sibling(self, j, ref4, send_sems, recv_sems):
        x, y, c, chips = _place()
        cx, cy = chips[j]
        self._copy(3 + j, self._half(ref4, 2 * cx + cy, 1 - c), (x, y, 1 - c), send_sems, recv_sems).wait_recv()

    def finish(self, ref4, send_sems, recv_sems, which=(0, 1, 2)):
        x, y, c, chips = _place()
        for j, (cx, cy) in enumerate(chips):
            if j in which:
                self._copy(j, self._half(ref4, 2 * x + y, c), (cx, cy, c), send_sems, recv_sems).wait_send()
            self._copy(3 + j, self._half(ref4, 2 * cx + cy, c), (x, y, 1 - c), send_sems, recv_sems).wait_send()


class _GradExchange:
    def __init__(self, segs, with_blob):
        self.segs = tuple(segs)
        self.with_blob = with_blob

    def landing_w(self):
        return jax.ShapeDtypeStruct((N_CHIPS, 2, D_MODEL // 2, D_MODEL), BF16)

    def landing_blob(self, blob16):
        return jax.ShapeDtypeStruct((N_DEV, blob16.shape[1] // 2, D_MODEL), BF16)

    def semaphores(self):
        n_send = len(self.segs) + (2 * N_CHIPS if self.with_blob else 0)
        n_recv = 2 * N_CHIPS + (N_DEV if self.with_blob else 0)
        return [pltpu.SemaphoreType.DMA((n_send,)), pltpu.SemaphoreType.DMA((n_recv,))]

    def _copies(self, dw_ref, rxw_ref, blob_ref, rxb_ref, send_sems, recv_sems):
        x, y, c = lax.axis_index("x"), lax.axis_index("y"), lax.axis_index("c")
        chip = 2 * x + y

        def relation(kx, ky, h):
            return (x ^ kx) * 4 + (y ^ ky) * 2 + (c ^ h)

        def copy(src, dst, send_k, recv_k, to):
            return functools.partial(pltpu.make_async_remote_copy, src_ref=src, dst_ref=dst,
                                     send_sem=send_sems.at[send_k], recv_sem=recv_sems.at[recv_k],
                                     device_id=to, device_id_type=MESH)

        sends, recvs = [], []
        for i, s in enumerate(self.segs):
            kx, ky = (s // 2) >> 1, (s // 2) & 1
            r = (x ^ kx) * 2 + (y ^ ky)
            sends.append((r != 0, copy(dw_ref.at[i], rxw_ref.at[r, s % 2], i, 2 * r + s % 2, (kx, ky, c))))
        for j in range(2):
            mine = [s // 2 for s in self.segs if s % 2 == j]
            if mine:
                cond = functools.reduce(lambda a, b: a | b, [chip == k for k in mine])
                for r in range(1, N_CHIPS):
                    slot = rxw_ref.at[r, j]
                    recvs.append((cond, copy(slot, slot, 0, 2 * r + j, (x, y, c))))
        if self.with_blob:
            hb = blob_ref.shape[1] // 2
            first_send, first_recv = len(self.segs), 2 * N_CHIPS
            for k in range(N_CHIPS):
                for h in range(2):
                    r = relation(k >> 1, k & 1, h)
                    sends.append((r != 0, copy(blob_ref.at[k, pl.ds(h * hb, hb), :], rxb_ref.at[r],
                                               first_send + 2 * k + h, first_recv + r, (k >> 1, k & 1, h))))
            for r in range(1, N_DEV):
                slot = rxb_ref.at[r]
                recvs.append((None, copy(slot, slot, 0, first_recv + r, (x, y, c))))
        return sends, recvs

    def start(self, *refs):
        sends, _ = self._copies(*refs)
        for cond, make in sends:
            pl.when(cond)(lambda make=make: make().start())

    def wait(self, *refs):
        sends, recvs = self._copies(*refs)
        for cond, make in sends:
            pl.when(cond)(lambda make=make: make().wait_send())
        for cond, make in recvs:
            if cond is None:
                make().wait_recv()
            else:
                pl.when(cond)(lambda make=make: make().wait_recv())


def _sum_landed(own, rx_ref):
    total = own
    for r in range(1, rx_ref.shape[0]):
        total = total + rx_ref[r, 0].astype(F32)
    return total


def _finish_w(dw_rec, dw_mix, rx_rec, rx_mix, place_arr):
    half = D_MODEL // 2
    tm = _tile(half, 256)
    n_rec = len(SEGS_REC)

    def body(place_ref, own_rec_ref, own_mix_ref, rx_rec_ref, rx_mix_ref, out_ref):
        seg = 2 * place_ref[0] + pl.program_id(0)

        @pl.when(seg < n_rec)
        def _():
            out_ref[0] = _sum_landed(own_rec_ref[0], rx_rec_ref)

        @pl.when(seg >= n_rec)
        def _():
            out_ref[0] = _sum_landed(own_mix_ref[0], rx_mix_ref)

    def own_spec(first, count):
        def index(j, i, place_ref):
            seg = 2 * place_ref[0] + j
            return (jnp.clip(seg - first, 0, count - 1), i, 0)
        return pl.BlockSpec((1, tm, D_MODEL), index)

    rx_spec = pl.BlockSpec((N_CHIPS, 1, tm, D_MODEL), lambda j, i, place_ref: (0, j, i, 0))
    return pl.pallas_call(
        body, name="finish_w",
        grid_spec=pltpu.PrefetchScalarGridSpec(
            num_scalar_prefetch=1, grid=(2, half // tm),
            in_specs=[own_spec(0, n_rec), own_spec(n_rec, len(SEGS_MIX)), rx_spec, rx_spec],
            out_specs=pl.BlockSpec((1, tm, D_MODEL), lambda j, i, place_ref: (place_ref[1], i, j))),
        out_shape=jax.ShapeDtypeStruct((2, half, 2 * D_MODEL), F32),
        compiler_params=_params(("arbitrary", "arbitrary")),
    )(place_arr, dw_rec, dw_mix, rx_rec, rx_mix)


def _finish_blob(dblob4, rx_blob, place_arr):
    n, rows, cols = rx_blob.shape
    tm = _tile(rows, 256)

    def body(place_ref, own_ref, rx_ref, out_ref):
        out_ref[0] = _sum_landed(own_ref[0, 0], rx_ref)

    return pl.pallas_call(
        body, name="finish_blob",
        grid_spec=pltpu.PrefetchScalarGridSpec(
            num_scalar_prefetch=1, grid=(rows // tm,),
            in_specs=[pl.BlockSpec((1, 1, tm, cols), lambda i, place_ref: (place_ref[0], place_ref[1], i, 0)),
                      pl.BlockSpec((n, 1, tm, cols), lambda i, place_ref: (0, 0, i, 0))],
            out_specs=pl.BlockSpec((1, tm, cols), lambda i, place_ref: (place_ref[1], i, 0))),
        out_shape=jax.ShapeDtypeStruct((2, rows, cols), F32),
        compiler_params=_params(("arbitrary",)),
    )(place_arr, dblob4.reshape(N_CHIPS, 2, rows, cols), rx_blob.reshape(n, 1, rows, cols))


def _share_finished(fw2, fb2, small):
    def body(w_in_ref, b_in_ref, small_ref, w_ref, b_ref, s_ref, bounce, local_sem, send_sems, recv_sems):
        x, y, c, _ = _place()
        sibling = (x, y, 1 - c)

        def copy(k, src, dst, to):
            return pltpu.make_async_remote_copy(src_ref=src, dst_ref=dst, send_sem=send_sems.at[k],
                                                recv_sem=recv_sems.at[k], device_id=to, device_id_type=MESH)

        sends = [copy(0, w_ref.at[c], w_ref.at[c], sibling), copy(1, b_ref.at[c], b_ref.at[c], sibling)]
        for r in range(1, N_DEV):
            peer = (x ^ ((r >> 2) & 1), y ^ ((r >> 1) & 1), c ^ (r & 1))
            sends.append(copy(1 + r, small_ref, s_ref.at[r], peer))
        for cp in sends:
            cp.start()
        for src, dst in ((small_ref, bounce), (bounce, s_ref.at[0])):
            own = pltpu.make_async_copy(src, dst, local_sem)
            own.start()
            own.wait()
        landed = [w_ref.at[1 - c], b_ref.at[1 - c]] + [s_ref.at[r] for r in range(1, N_DEV)]
        for k, slot in enumerate(landed):
            copy(k, slot, slot, (x, y, c)).wait_recv()
        for cp in sends:
            cp.wait_send()

    same = lambda a: jax.ShapeDtypeStruct(a.shape, a.dtype)
    n_sem = 2 + N_DEV - 1
    return pl.pallas_call(
        body, name="share_finished",
        in_specs=[ANY, ANY, ANY], out_specs=[ANY, ANY, ANY],
        out_shape=[same(fw2), same(fb2), jax.ShapeDtypeStruct((N_DEV,) + small.shape, F32)],
        input_output_aliases={0: 0, 1: 1},
        scratch_shapes=[pltpu.VMEM(small.shape, F32), pltpu.SemaphoreType.DMA,
                        pltpu.SemaphoreType.DMA((n_sem,)), pltpu.SemaphoreType.DMA((n_sem,))],
    )(fw2, fb2, small)


def _sum_small(slots, lb_logits, me_arr):
    def body(me_ref, slots_ref, lbl_ref, out_ref):
        me = me_ref[0]
        total = slots_ref[me]
        for d in range(1, N_DEV):
            total = total + slots_ref[d ^ me]
        out_ref[...] = total
        out_ref[ROW_LOSS:ROW_LOSS + 1, :] = jnp.broadcast_to(
            jnp.sum(total[ROW_LOSS:ROW_LOSS + 1, :], axis=-1, keepdims=True), (1, D_MODEL))
        lb = _lower_bound(lbl_ref[...])
        g0 = total[ROW_LB:ROW_LB + 1, :] * lb * (1.0 - lb)
        out_ref[ROW_LB:ROW_LB + 1, :] = g0
        out_ref[ROW_LB + 1:ROW_LB + 2, :] = -g0

    return pl.pallas_call(
        body, name="sum_small",
        grid_spec=pltpu.PrefetchScalarGridSpec(
            num_scalar_prefetch=1, grid=(1,),
            in_specs=[pl.BlockSpec((N_DEV, SMALL_ROWS, D_MODEL), lambda i, me_ref: (0, 0, 0)),
                      pl.BlockSpec((2, D_MODEL), lambda i, me_ref: (0, 0))],
            out_specs=pl.BlockSpec((SMALL_ROWS, D_MODEL), lambda i, me_ref: (0, 0))),
        out_shape=jax.ShapeDtypeStruct((SMALL_ROWS, D_MODEL), F32),
        compiler_params=_params(("arbitrary",)),
    )(me_arr, slots, lb_logits)


def _adamw_step(w, g, m, v):
    c1 = 1.0 / (1.0 - ADAM_B1 ** ADAM_STEP)
    c2 = 1.0 / (1.0 - ADAM_B2 ** ADAM_STEP)
    nm = ADAM_B1 * m + (1.0 - ADAM_B1) * g
    nv = ADAM_B2 * v + (1.0 - ADAM_B2) * (g * g)
    return -ADAM_LR * ((nm * c1) / (jnp.sqrt(nv * c2) + ADAM_EPS) + ADAM_WD * w), nm, nv


SMALL_PARAMS = (("norm_w", ROW_NORM_W, 1), ("b_in", ROW_B_IN, N_SEG), ("lb_logits", ROW_LB, 2),
                ("hg_norm_w", ROW_HG_W, 1), ("pool_scale", ROW_POOL_SCALE, 1), ("final_norm_w", ROW_FINAL_W, 1))


def _update_small(tot, triples):
    n = len(SMALL_PARAMS)

    def body(tot_ref, *refs):
        ins, outs = refs[:3 * n], refs[3 * n:]
        for p, (name, row, n_rows) in enumerate(SMALL_PARAMS):
            w_ref, m_ref, v_ref = ins[3 * p:3 * p + 3]
            g_ref, d_ref, nm_ref, nv_ref = outs[4 * p:4 * p + 4]
            if w_ref.shape[0] == n_rows:
                pieces = [(slice(None), slice(None), tot_ref[row:row + n_rows, :])]
            else:
                pieces = [(slice(None), slice(k * D_MODEL, (k + 1) * D_MODEL), tot_ref[row + k:row + k + 1, :])
                          for k in range(n_rows)]
            for rows_, cols_, g in pieces:
                d, nm, nv = _adamw_step(w_ref[rows_, cols_], g, m_ref[rows_, cols_], v_ref[rows_, cols_])
                g_ref[rows_, cols_] = g
                d_ref[rows_, cols_] = d
                nm_ref[rows_, cols_] = nm
                nv_ref[rows_, cols_] = nv

    whole = pl.BlockSpec(memory_space=pltpu.VMEM)
    flat = [a for t in triples for a in t]
    out_shape = [jax.ShapeDtypeStruct(t[0].shape, F32) for t in triples for _ in range(4)]
    outs = pl.pallas_call(
        body, name="update_small",
        in_specs=[whole] * (1 + len(flat)), out_specs=[whole] * len(out_shape), out_shape=out_shape,
        compiler_params=_params(),
    )(tot, *flat)
    return [tuple(outs[4 * p:4 * p + 4]) for p in range(n)]


def _update_blob(g_blob, triples):
    q_rows = triples[0][0].shape[0]
    pool_rows = triples[3][0].shape[0]
    steps = q_rows // pool_rows

    def body(*refs):
        ins, outs = refs[:16], refs[16:]
        for p in range(4):
            g_ref, (w_ref, m_ref, v_ref) = ins[p], ins[4 + 3 * p:7 + 3 * p]
            go_ref, d_ref, nm_ref, nv_ref = outs[4 * p:4 * p + 4]

            def update():
                g = g_ref[...]
                go_ref[...] = g
                d_ref[...], nm_ref[...], nv_ref[...] = _adamw_step(w_ref[...], g, m_ref[...], v_ref[...])

            if p < 3:
                update()
            else:
                pl.when(pl.program_id(0) == 0)(update)

    blk = pl.BlockSpec((pool_rows, D_MODEL), lambda i: (i, 0))
    once = pl.BlockSpec((pool_rows, D_MODEL), lambda i: (0, 0))
    g_specs = [pl.BlockSpec((pool_rows, D_MODEL), lambda i, p=p: (steps * p + i, 0)) for p in range(3)]
    g_specs.append(pl.BlockSpec((pool_rows, D_MODEL), lambda i: (3 * steps, 0)))
    piece_specs = [blk] * 9 + [once] * 3
    out_specs = [blk] * 12 + [once] * 4
    out_shape = [jax.ShapeDtypeStruct(t[0].shape, F32) for t in triples for _ in range(4)]
    outs = pl.pallas_call(
        body, name="update_blob",
        grid=(steps,), in_specs=g_specs + piece_specs, out_specs=out_specs, out_shape=out_shape,
        compiler_params=_params(("arbitrary",)),
    )(g_blob, g_blob, g_blob, g_blob, *[a for t in triples for a in t])
    return [tuple(outs[4 * p:4 * p + 4]) for p in range(4)]


def _adamw(w, g, m, v):
    rows, cols = w.shape
    tm = _tile(rows, 256, mult=8) if rows % 8 == 0 else rows

    def body(w_ref, g_ref, m_ref, v_ref, d_ref, nm_ref, nv_ref):
        d_ref[...], nm_ref[...], nv_ref[...] = _adamw_step(w_ref[...], g_ref[...], m_ref[...], v_ref[...])

    blk = pl.BlockSpec((tm, cols), lambda i: (i, 0))
    sds = jax.ShapeDtypeStruct((rows, cols), F32)
    return pl.pallas_call(
        body, name="adamw",
        grid=(rows // tm,), in_specs=[blk] * 4, out_specs=[blk] * 3, out_shape=[sds] * 3,
        compiler_params=_params(("arbitrary",)),
    )(w, g, m, v)


def kernel(x, meta_tokens, norm_w, w_in, b_in, lb_logits, hg_norm_w, pool_w, pool_scale, w_down_hg, w_down_pool, w_out, final_norm_w, loss_target, m_meta_tokens, m_norm_w, m_w_in, m_b_in, m_lb_logits, m_hg_norm_w, m_pool_w, m_pool_scale, m_w_down_hg, m_w_down_pool, m_w_out, m_final_norm_w, v_meta_tokens, v_norm_w, v_w_in, v_b_in, v_lb_logits, v_hg_norm_w, v_pool_w, v_pool_scale, v_w_down_hg, v_w_down_pool, v_w_out, v_final_norm_w):
    seq = x.shape[1]
    xi, yi, ci = lax.axis_index("x"), lax.axis_index("y"), lax.axis_index("c")
    chip = 2 * xi + yi
    place_arr = jnp.stack([chip, ci]).astype(jnp.int32)
    me_arr = jnp.reshape(4 * xi + 2 * yi + ci, (1,)).astype(jnp.int32)
    q = D_MODEL // N_CHIPS

    def blob_of(wdh, wdp, wo, pw):
        return jnp.concatenate([wdh[0], wdp[0], wo[0], pw[0].reshape(-1, D_MODEL)], axis=0)

    def in_every_slot(a):
        return jnp.broadcast_to(a[None], (N_CHIPS,) + a.shape)

    m4 = in_every_slot(meta_tokens)
    w4 = in_every_slot(w_in[0].astype(BF16))
    blob4 = in_every_slot(blob_of(w_down_hg, w_down_pool, w_out, pool_w).astype(BF16))
    seg_order = jnp.stack([2 * (chip ^ rel) + t for rel in (0, 2, 1, 3) for t in (0, 1)]).astype(jnp.int32)

    fw2 = final_norm_w.reshape(1, D_MODEL)
    d_tokens, w_parts, blob_parts, small = _local_step(
        x[0], m4, loss_target[0], w4, blob4, seg_order, norm_w, b_in, lb_logits, hg_norm_w, pool_scale, fw2)
    grad_x = d_tokens[None]

    fin_w = _finish_w(*w_parts, place_arr)
    fin_b = _finish_blob(*blob_parts, place_arr)
    gw2, gb2, slots = _share_finished(fin_w, fin_b, small)
    tot = _sum_small(slots, lb_logits, me_arr)
    g_w_in = gw2.reshape(D_MODEL, 2 * D_MODEL)
    g_blob = gb2.reshape(-1, D_MODEL)

    d_win, nm_win, nv_win = _adamw(w_in[0], g_w_in, m_w_in[0], v_w_in[0])
    pool_rows = lambda a: a[0].reshape(-1, D_MODEL)
    blob_results = _update_blob(g_blob, [
        (w_down_hg[0], m_w_down_hg[0], v_w_down_hg[0]), (w_down_pool[0], m_w_down_pool[0], v_w_down_pool[0]),
        (w_out[0], m_w_out[0], v_w_out[0]), (pool_rows(pool_w), pool_rows(m_pool_w), pool_rows(v_pool_w))])
    g_meta = lax.dynamic_slice_in_dim(tot[ROW_META:ROW_META + N_META], chip * q, q, axis=1)
    d_meta, nm_meta, nv_meta = _adamw(meta_tokens, g_meta, m_meta_tokens, v_meta_tokens)

    as_row = lambda a: a.reshape(1, D_MODEL)
    small_results = _update_small(tot, [
        (norm_w, m_norm_w, v_norm_w), (b_in, m_b_in, v_b_in), (lb_logits, m_lb_logits, v_lb_logits),
        (hg_norm_w, m_hg_norm_w, v_hg_norm_w), (pool_scale, m_pool_scale, v_pool_scale),
        (as_row(final_norm_w), as_row(m_final_norm_w), as_row(v_final_norm_w))])

    def leaves(kind, meta_part, win_part):
        nw, bi, lbl, hg, ps, fw = [r[kind] for r in small_results]
        wdh, wdp, wo, pw = [r[kind] for r in blob_results]
        return [meta_part, nw, win_part[None], bi, lbl, hg, pw.reshape(pool_w.shape), ps,
                wdh[None], wdp[None], wo[None], fw.reshape(D_MODEL)]

    loss = tot[ROW_LOSS, 0]
    return (loss, grad_x,
            *leaves(0, g_meta, g_w_in),
            *leaves(1, d_meta, d_win),
            *leaves(2, nm_meta, nm_win),
            *leaves(3, nv_meta, nv_win))
```

```python
import functools

import numpy as np
import jax
import jax.numpy as jnp
from jax import lax
from jax.experimental import pallas as pl
from jax.experimental.pallas import tpu as pltpu

F32 = jnp.float32
BF16 = jnp.bfloat16

D_MODEL = 1024
N_SEG = 8
N_HEADS = 8
HEAD_DIM = 128
CHUNK = 64
N_META = 16
PAD_ROWS = CHUNK - N_META
FIRST_TOKEN_ROW = CHUNK
LEVELS = (32, 16, 8, 4, 2, 1)
N_EXP = 2 + len(LEVELS)
POOL_WINDOWS = (2, 4, 8, 16)
POOL_GDIM = D_MODEL // len(POOL_WINDOWS)
HALO = 16
FORWARD_HEADS_PER_STEP = 4
BACKWARD_HEADS_PER_STEP = 1
LOCAL_UNROLL = 13
BACKWARD_UNROLL = 13
EPS = 1e-6
N_CHIPS = 4
N_DEV = 8
SEGS_REC = (0, 1, 2)
SEGS_MIX = (3, 4, 5, 6, 7)

ADAM_LR = 0.001
ADAM_B1 = 0.9
ADAM_B2 = 0.999
ADAM_EPS = 1e-08
ADAM_WD = 0.01
ADAM_STEP = 10

VMEM_LIMIT_BYTES = 56 * 1024 * 1024

ROW_LOSS = 0
ROW_META = 1
ROW_NORM_W = ROW_META + N_META
ROW_B_IN = ROW_NORM_W + 1
ROW_LB = ROW_B_IN + N_SEG
ROW_HG_W = ROW_LB + 2
ROW_POOL_SCALE = ROW_HG_W + 1
ROW_FINAL_W = ROW_POOL_SCALE + 1
SMALL_ROWS = 32


def _tile(total, cap, mult=16):
    best = None
    for t in range(mult, min(total, cap) + 1, mult):
        if total % t == 0:
            best = t
    assert best is not None, (total, cap, mult)
    return best


def _token_window(tm, tile_of):
    def index(*grid):
        return (pl.multiple_of(jnp.maximum(tile_of(*grid) * tm - FIRST_TOKEN_ROW, 0), HALO), 0)
    return pl.BlockSpec((pl.Element(tm), pl.Element(D_MODEL)), index)


def _padded_tile(window, head, tile):
    first = jnp.concatenate([head, pltpu.roll(window, FIRST_TOKEN_ROW, 0)[FIRST_TOKEN_ROW:]], axis=0)
    return jnp.where(tile == 0, first, window)


def _params(sem=None):
    return pltpu.CompilerParams(dimension_semantics=sem, vmem_limit_bytes=VMEM_LIMIT_BYTES)


def _dot(a, b):
    return jnp.dot(a, b, preferred_element_type=F32)


def _dot_nt(a, b):
    return lax.dot_general(a, b, (((1,), (1,)), ((), ())), preferred_element_type=F32)


def _dot_tn(a, b):
    return lax.dot_general(a, b, (((0,), (0,)), ((), ())), preferred_element_type=F32)


def _sigmoid_pair(x):
    t = jnp.exp(-jnp.abs(x))
    r = 1.0 / (1.0 + t)
    pos = x >= 0
    return jnp.where(pos, r, t * r), jnp.where(pos, t * r, r)


def _exponent_matrix():
    t = np.arange(CHUNK)[:, None]
    j = np.arange(CHUNK)[None, :]
    blocks = [j <= t, j > t]
    for m in LEVELS:
        rho = (t // (2 * m)) * (2 * m) + m
        upper = (t >= rho) & (j > rho) & (j <= t)
        lower = (t < rho) & (j > t) & (j <= rho)
        blocks.append(upper | lower)
    return np.concatenate(blocks, axis=0).astype(np.float32)


def _pair_masks():
    t = np.arange(CHUNK)[:, None]
    s = np.arange(CHUNK)[None, :]
    masks = [t == s]
    for m in LEVELS:
        same = (t // (2 * m)) == (s // (2 * m))
        masks.append(same & ((t % (2 * m)) >= m) & ((s % (2 * m)) < m))
    return np.stack(masks).astype(np.float32)


LEVEL_PAIRS = ((0, 1), (2, 3), (4, 5), (6, None))


def _paired_masks():
    m = _pair_masks()
    zero = np.zeros_like(m[0])
    return np.stack([np.concatenate([m[a], zero if b is None else m[b]], axis=1) for a, b in LEVEL_PAIRS])


def _lower_bound(lbl):
    return 1.0 / (1.0 + jnp.exp(lbl[1:2, :] - lbl[0:1, :]))


def _in_proj(tokens, m4, norm_w, w4, b_in, seg_order, rows):
    tm = _tile(rows, 1040)
    nt = rows // tm
    gather = _ShardGather(w4.shape[1])

    def body(order_ref, z_ref, nw_ref, b_ref, w_in_ref, m_in_ref, h_ref, p_ref, w4_ref, head_ref, m4_ref,
             h_all, w_buf, w_sem, send_sems, recv_sems, meta_send, meta_recv, meta_sem):
        kk, i = pl.program_id(0), pl.program_id(1)

        @pl.when((kk == 0) & (i == 0))
        def _():
            x, y, c, chips = _place()

            def meta_copy(j, chip, to):
                return pltpu.make_async_remote_copy(
                    src_ref=m4_ref.at[chip], dst_ref=m4_ref.at[chip], send_sem=meta_send.at[j],
                    recv_sem=meta_recv.at[j], device_id=to, device_id_type=MESH)

            sends = [meta_copy(j, 2 * x + y, (cx, cy, c)) for j, (cx, cy) in enumerate(chips)]
            for cp in sends:
                cp.start()
            gather.start(w4_ref, send_sems, recv_sems, which=(0, 1))
            for j, (cx, cy) in enumerate(chips):
                meta_copy(j, 2 * cx + cy, (x, y, c)).wait_recv()
            for cp in sends:
                cp.wait_send()
            head_ref[0:PAD_ROWS, :] = jnp.zeros((PAD_ROWS, D_MODEL), F32)
            q_cols = D_MODEL // N_CHIPS
            for k in range(N_CHIPS):
                cp = pltpu.make_async_copy(
                    m4_ref.at[k], head_ref.at[pl.ds(PAD_ROWS, N_META), pl.ds(k * q_cols, q_cols)], meta_sem)
                cp.start()
                cp.wait()

        @pl.when((kk == 2) & (i == 0))
        def _():
            gather.start_diagonal_after_neighbours(w4_ref, send_sems, recv_sems)

        @pl.when(kk == 0)
        def _():
            zt = _padded_tile(z_ref[...], head_ref[...], i)
            rstd = lax.rsqrt(jnp.mean(zt * zt, axis=-1, keepdims=True) + EPS)
            h = (zt * rstd * nw_ref[...]).astype(BF16)
            h_all[pl.ds(pl.multiple_of(i * tm, 16), tm), :] = h
            h_ref[...] = h

        @pl.when((kk == 2) & (i == 0))
        def _():
            gather.pass_on(0, w4_ref, send_sems, recv_sems)
            gather.pass_on(1, w4_ref, send_sems, recv_sems)
            gather.await_sibling(0, w4_ref, send_sems, recv_sems)

        @pl.when((kk == 4) & (i == 0))
        def _():
            gather.await_sibling(1, w4_ref, send_sems, recv_sems)

        @pl.when((kk == 5) & (i == 0))
        def _():
            gather.pass_on(2, w4_ref, send_sems, recv_sems)

        @pl.when((kk == 6) & (i == 0))
        def _():
            gather.await_sibling(2, w4_ref, send_sems, recv_sems)

        def weights(which):
            seg = order_ref[2 * (kk // 2) + which]
            return pltpu.make_async_copy(
                w4_ref.at[seg // 2, :, pl.ds(pl.multiple_of((seg % 2) * D_MODEL, D_MODEL), D_MODEL)],
                w_buf.at[which], w_sem.at[which])

        @pl.when((i == 0) & (kk % 2 == 0))
        def _():
            weights(0).start()
            weights(1).start()
            weights(0).wait()

        @pl.when((i == 0) & (kk % 2 == 1))
        def _():
            weights(1).wait()

        p_ref[0] = _dot(h_all[pl.ds(pl.multiple_of(i * tm, 16), tm), :], w_buf[kk % 2]) + b_ref[...]

        @pl.when((kk == N_SEG - 1) & (i == nt - 1))
        def _():
            gather.finish(w4_ref, send_sems, recv_sems, which=(2,))

    first_pass = lambda kk, i, order_ref: (jnp.where(kk == 0, i, nt - 1), 0)
    return pl.pallas_call(
        body, name="in_proj",
        grid_spec=pltpu.PrefetchScalarGridSpec(
            num_scalar_prefetch=1, grid=(N_SEG, nt),
            in_specs=[
                _token_window(tm, lambda kk, i, order_ref: jnp.where(kk == 0, i, nt - 1)),
                pl.BlockSpec((1, D_MODEL), lambda kk, i, order_ref: (0, 0)),
                pl.BlockSpec((1, D_MODEL), lambda kk, i, order_ref: (0, order_ref[kk])),
                ANY, ANY,
            ],
            out_specs=[
                pl.BlockSpec((tm, D_MODEL), first_pass),
                pl.BlockSpec((1, tm, D_MODEL), lambda kk, i, order_ref: (order_ref[kk], i, 0)),
                ANY,
                pl.BlockSpec((FIRST_TOKEN_ROW, D_MODEL), lambda kk, i, order_ref: (0, 0)),
                ANY,
            ],
            scratch_shapes=[
                pltpu.VMEM((rows, D_MODEL), BF16),
                pltpu.VMEM((2, D_MODEL, D_MODEL), BF16),
                pltpu.SemaphoreType.DMA((2,)),
            ] + gather.semaphores() + [
                pltpu.SemaphoreType.DMA((N_CHIPS - 1,)), pltpu.SemaphoreType.DMA((N_CHIPS - 1,)),
                pltpu.SemaphoreType.DMA,
            ]),
        out_shape=[
            jax.ShapeDtypeStruct((rows, D_MODEL), BF16),
            jax.ShapeDtypeStruct((N_SEG, rows, D_MODEL), F32),
            jax.ShapeDtypeStruct(w4.shape, w4.dtype),
            jax.ShapeDtypeStruct((FIRST_TOKEN_ROW, D_MODEL), F32),
            jax.ShapeDtypeStruct(m4.shape, m4.dtype),
        ],
        input_output_aliases={4: 2, 5: 4},
        compiler_params=_params(("arbitrary", "arbitrary")),
    )(seg_order, tokens, norm_w, b_in, w4, m4)


def _hgrn_forward(p3, lb_logits, wexp2, masks2, blob4, rows):
    n_chunks = rows // CHUNK
    cpb = _tile(n_chunks, 13, mult=1)
    rb_rows = cpb * CHUNK
    n_rb = n_chunks // cpb
    lanes = cpb * HEAD_DIM
    hps = FORWARD_HEADS_PER_STEP
    n_hb = N_HEADS // hps
    width = hps * HEAD_DIM
    gather = _ShardGather(blob4.shape[1])

    def body(q_ref, fz_ref, v_ref, lbl_ref, wexp_ref, mask_ref, b_in_ref, o_ref, s_ref, e16_ref, a2_ref, b4_ref,
             st_all, e_all, u_all, q_all, kk_all, v_all, send_sems, recv_sems):
        rb = pl.program_id(1)

        @pl.when((pl.program_id(0) == 0) & (rb == 0))
        def _():
            gather.start(b4_ref, send_sems, recv_sems)

        @pl.when(rb == 0)
        def _():
            st_all[...] = jnp.zeros_like(st_all)

        for j in range(hps):
            cols = pl.ds(j * HEAD_DIM, HEAD_DIM)
            one_head(rb, q_ref.at[0, :, cols], fz_ref.at[0, :, cols], v_ref.at[0, :, cols], lbl_ref.at[:, cols],
                     wexp_ref, mask_ref, o_ref.at[:, cols], s_ref.at[j], e16_ref.at[j, 0], a2_ref.at[:, cols],
                     st_all.at[j], e_all.at[j], u_all.at[j], q_all.at[j], kk_all.at[j], v_all.at[j])

        @pl.when((pl.program_id(0) == n_hb // 2) & (rb == 0))
        def _():
            for j in range(N_CHIPS - 1):
                gather.pass_on(j, b4_ref, send_sems, recv_sems)

        @pl.when((pl.program_id(0) == n_hb - 1) & (rb == n_rb - 1))
        def _():
            for j in range(N_CHIPS - 1):
                gather.await_sibling(j, b4_ref, send_sems, recv_sems)
            gather.finish(b4_ref, send_sems, recv_sems)

    def one_head(rb, q_ref, fz_ref, v_ref, lbl_ref, wexp_ref, mask_ref, o_ref, s_ref, e16_ref, a2_ref,
                 st_ref, e_ref, u_ref, q_s, kk_s, v_s):
        lb = _lower_bound(lbl_ref[...])
        row = rb * rb_rows + lax.broadcasted_iota(jnp.int32, (rb_rows, 1), 0)
        valid = row >= PAD_ROWS
        sg, sn = _sigmoid_pair(fz_ref[...])
        g = jnp.where(valid, jnp.log(lb + (1.0 - lb) * sg), 0.0)
        kk_s[...] = jnp.where(valid, (1.0 - lb) * sn, 0.0)
        q_s[...] = jnp.where(valid, q_ref[...], 0.0)
        v_s[...] = jnp.where(valid, v_ref[...], 0.0).astype(BF16)
        hi = g.astype(BF16)
        mid = (g - hi.astype(F32)).astype(BF16)
        g2 = jnp.concatenate(
            [jnp.concatenate([hi[b * CHUNK:(b + 1) * CHUNK], mid[b * CHUNK:(b + 1) * CHUNK]], axis=0)
             for b in range(cpb)], axis=1)
        e_ref[...] = jnp.exp(_dot(wexp_ref[...], g2))
        e16_ref[...] = e_ref[...].astype(BF16)

        def contribution(b, carry):
            r0 = pl.multiple_of(b * CHUNK, CHUNK)
            l0 = pl.multiple_of(b * HEAD_DIM, HEAD_DIM)
            kc16 = (kk_s[pl.ds(r0, CHUNK), :] * e_ref[CHUNK:2 * CHUNK, pl.ds(l0, HEAD_DIM)]).astype(BF16)
            u_ref[b] = _dot_tn(v_s[pl.ds(r0, CHUNK), :], kc16)
            return carry

        lax.fori_loop(0, cpb, contribution, 0, unroll=LOCAL_UNROLL)

        def recur(b, st):
            l0 = pl.multiple_of(b * HEAD_DIM, HEAD_DIM)
            s_ref[b] = st
            return st * e_ref[CHUNK - 1:CHUNK, pl.ds(l0, HEAD_DIM)] + u_ref[b]

        st_ref[...] = lax.fori_loop(0, cpb, recur, st_ref[...], unroll=LOCAL_UNROLL)

        zeros16 = jnp.zeros((CHUNK, HEAD_DIM), BF16)

        def local(b, carry):
            r0 = pl.multiple_of(b * CHUNK, CHUNK)
            l0 = pl.multiple_of(b * HEAD_DIM, HEAD_DIM)
            q = q_s[pl.ds(r0, CHUNK), :]
            kk = kk_s[pl.ds(r0, CHUNK), :]
            v16 = v_s[pl.ds(r0, CHUNK), :]

            def scaled(entry):
                if entry == 0:
                    return q.astype(BF16), kk.astype(BF16)
                e_m = e_ref[(1 + entry) * CHUNK:(2 + entry) * CHUNK, pl.ds(l0, HEAD_DIM)]
                return (q * e_m).astype(BF16), (kk * e_m).astype(BF16)

            a2 = jnp.zeros((CHUNK, 2 * CHUNK), F32)
            for p, (ea, eb) in enumerate(LEVEL_PAIRS):
                qa, ka = scaled(ea)
                if eb is None:
                    prod = _dot_nt(qa, jnp.concatenate([ka, zeros16], axis=0))
                else:
                    qb_, kb_ = scaled(eb)
                    rhs = jnp.concatenate([jnp.concatenate([ka, zeros16], axis=1),
                                           jnp.concatenate([zeros16, kb_], axis=1)], axis=0)
                    prod = _dot_nt(jnp.concatenate([qa, qb_], axis=1), rhs)
                a2 = a2 + mask_ref[p] * prod
            a2_16 = a2.astype(BF16)
            a2_ref[pl.ds(r0, CHUNK), :] = a2_16
            qb16 = (q * e_ref[0:CHUNK, pl.ds(l0, HEAD_DIM)]).astype(BF16)
            o_ref[pl.ds(r0, CHUNK), :] = (_dot(a2_16, jnp.concatenate([v16, v16], axis=0))
                                          + _dot_nt(qb16, s_ref[b].astype(BF16)))
            return carry

        lax.fori_loop(0, cpb, local, 0, unroll=LOCAL_UNROLL)

    head_block = lambda seg: pl.BlockSpec((1, rb_rows, width), lambda h, r: (seg, r, h))
    return pl.pallas_call(
        body, name="hgrn_forward",
        grid=(n_hb, n_rb),
        in_specs=[
            head_block(0), head_block(1), head_block(2),
            pl.BlockSpec((2, width), lambda h, r: (0, h)),
            pl.BlockSpec((N_EXP * CHUNK, 2 * CHUNK), lambda h, r: (0, 0)),
            pl.BlockSpec((len(LEVEL_PAIRS), CHUNK, 2 * CHUNK), lambda h, r: (0, 0, 0)),
            ANY,
        ],
        out_specs=[
            pl.BlockSpec((rb_rows, width), lambda h, r: (r, h)),
            pl.BlockSpec((hps, cpb, HEAD_DIM, HEAD_DIM), lambda h, r: (h, r, 0, 0)),
            pl.BlockSpec((hps, 1, N_EXP * CHUNK, lanes), lambda h, r: (h, r, 0, 0)),
            pl.BlockSpec((rb_rows, width), lambda h, r: (r, h)),
            ANY,
        ],
        out_shape=[
            jax.ShapeDtypeStruct((rows, D_MODEL), F32),
            jax.ShapeDtypeStruct((N_HEADS, n_chunks, HEAD_DIM, HEAD_DIM), F32),
            jax.ShapeDtypeStruct((N_HEADS, n_rb, N_EXP * CHUNK, lanes), BF16),
            jax.ShapeDtypeStruct((rows, D_MODEL), BF16),
            jax.ShapeDtypeStruct(blob4.shape, blob4.dtype),
        ],
        input_output_aliases={6: 4},
        scratch_shapes=[
            pltpu.VMEM((hps, HEAD_DIM, HEAD_DIM), F32),
            pltpu.VMEM((hps, N_EXP * CHUNK, lanes), F32),
            pltpu.VMEM((hps, cpb, HEAD_DIM, HEAD_DIM), F32),
            pltpu.VMEM((hps, rb_rows, HEAD_DIM), F32),
            pltpu.VMEM((hps, rb_rows, HEAD_DIM), F32),
            pltpu.VMEM((hps, rb_rows, HEAD_DIM), BF16),
        ] + gather.semaphores(),
        compiler_params=_params(("arbitrary", "arbitrary")),
    )(p3, p3, p3, lb_logits, wexp2, masks2, blob4)


def _hgrn_backward(p3, d_o, states, e16, a2, lb_logits, wexp_t, masks2, dw16, blob16, rows):
    n_chunks = rows // CHUNK
    cpb = _tile(n_chunks, 13, mult=1)
    rb_rows = cpb * CHUNK
    n_rb = n_chunks // cpb
    lanes = cpb * HEAD_DIM
    exchange = _GradExchange(SEGS_MIX, with_blob=True)

    hps = BACKWARD_HEADS_PER_STEP
    n_hb = N_HEADS // hps
    width = hps * HEAD_DIM

    def body(q_ref, fz_ref, v_ref, do_ref, s_ref, e_ref, a2_ref, lbl_ref, wexpt_ref, mask_ref, dw_ref, blob_ref,
             dp_ref, dlb_ref, rxw_ref, rxb_ref, *scratch):
        per_head, (send_sems, recv_sems) = scratch[:-2], scratch[-2:]
        step = pl.program_id(1)
        rb = n_rb - 1 - step

        @pl.when((pl.program_id(0) == 0) & (step == 0))
        def _():
            exchange.start(dw_ref, rxw_ref, blob_ref, rxb_ref, send_sems, recv_sems)

        for j in range(hps):
            cols = pl.ds(j * HEAD_DIM, HEAD_DIM)
            one_head(step, rb, q_ref.at[0, :, cols], fz_ref.at[0, :, cols], v_ref.at[0, :, cols], do_ref.at[:, cols],
                     s_ref.at[j], e_ref.at[j, 0], a2_ref.at[:, cols], lbl_ref.at[:, cols], wexpt_ref, mask_ref,
                     dp_ref.at[:, :, cols], dlb_ref.at[:, cols], *[ref.at[j] for ref in per_head])

        @pl.when((pl.program_id(0) == n_hb - 1) & (step == n_rb - 1))
        def _():
            exchange.wait(dw_ref, rxw_ref, blob_ref, rxb_ref, send_sems, recv_sems)

    def one_head(step, rb, q_ref, fz_ref, v_ref, do_ref, s_ref, e_ref, a2_ref, lbl_ref, wexpt_ref, mask_ref,
                 dp_ref, dlb_ref, dst_ref, g_ref, dsn_ref, q_s, kk_s, v_s, do_s, dq_s, dkk_s, dg_s, dx_s, da2_s):
        @pl.when(step == 0)
        def _():
            dst_ref[...] = jnp.zeros_like(dst_ref)
            dlb_ref[...] = jnp.zeros_like(dlb_ref)

        lb = _lower_bound(lbl_ref[...])
        row = rb * rb_rows + lax.broadcasted_iota(jnp.int32, (rb_rows, 1), 0)
        valid = row >= PAD_ROWS
        sg, sn = _sigmoid_pair(fz_ref[...])
        f = lb + (1.0 - lb) * sg
        g = jnp.where(valid, jnp.log(f), 0.0)
        kk_s[...] = jnp.where(valid, (1.0 - lb) * sn, 0.0)
        q_s[...] = jnp.where(valid, q_ref[...], 0.0)
        v_s[...] = jnp.where(valid, v_ref[...], 0.0).astype(BF16)
        do_s[...] = do_ref[...].astype(BF16)
        e_last_all = jnp.exp(jnp.concatenate(
            [jnp.sum(g[b * CHUNK:(b + 1) * CHUNK], axis=0, keepdims=True) for b in range(cpb)], axis=0))
        last_row = lax.broadcasted_iota(jnp.int32, (CHUNK, 1), 0) == CHUNK - 1
        zeros16 = jnp.zeros((CHUNK, HEAD_DIM), BF16)

        def factor(block, l0):
            return e_ref[block * CHUNK:(block + 1) * CHUNK, pl.ds(l0, HEAD_DIM)].astype(F32)

        def contribution(b, carry):
            r0 = pl.multiple_of(b * CHUNK, CHUNK)
            l0 = pl.multiple_of(b * HEAD_DIM, HEAD_DIM)
            qb16 = (q_s[pl.ds(r0, CHUNK), :] * factor(0, l0)).astype(BF16)
            g_ref[b] = _dot_tn(do_s[pl.ds(r0, CHUNK), :], qb16)
            return carry

        lax.fori_loop(0, cpb, contribution, 0, unroll=LOCAL_UNROLL)

        cur = dst_ref[...]
        for b in reversed(range(cpb)):
            dsn_ref[b] = cur
            cur = cur * e_last_all[b:b + 1, :] + g_ref[b]
        dst_ref[...] = cur

        def through_state(b, carry):
            r0 = pl.multiple_of(b * CHUNK, CHUNK)
            l0 = pl.multiple_of(b * HEAD_DIM, HEAD_DIM)
            v16 = v_s[pl.ds(r0, CHUNK), :]
            do16 = do_s[pl.ds(r0, CHUNK), :]
            st = s_ref[b]
            dsn = dsn_ref[b]
            dsn16 = dsn.astype(BF16)
            e_b, e_c = factor(0, l0), factor(1, l0)
            qb, kc = q_s[pl.ds(r0, CHUNK), :] * e_b, kk_s[pl.ds(r0, CHUNK), :] * e_c

            t = _dot_tn(a2_ref[pl.ds(r0, CHUNK), :], do16)
            dv = t[0:CHUNK] + t[CHUNK:2 * CHUNK] + _dot_nt(kc.astype(BF16), dsn16)
            dp_ref[2, pl.ds(r0, CHUNK), :] = dv.astype(BF16)
            da2_s[pl.ds(r0, CHUNK), :] = _dot_nt(do16, jnp.concatenate([v16, v16], axis=0))
            dqb = _dot(do16, st.astype(BF16))
            dkc = _dot(v16, dsn16)
            de = jnp.sum(dsn * st, axis=0, keepdims=True) * e_b[CHUNK - 1:CHUNK, :]
            dq_s[pl.ds(r0, CHUNK), :] = e_b * dqb
            dkk_s[pl.ds(r0, CHUNK), :] = e_c * dkc
            dx_s[0:CHUNK, pl.ds(l0, HEAD_DIM)] = (qb * dqb + jnp.where(last_row, de, 0.0)).astype(BF16)
            dx_s[CHUNK:2 * CHUNK, pl.ds(l0, HEAD_DIM)] = (kc * dkc).astype(BF16)
            return carry

        lax.fori_loop(0, cpb, through_state, 0, unroll=BACKWARD_UNROLL)

        def local(b, carry):
            r0 = pl.multiple_of(b * CHUNK, CHUNK)
            l0 = pl.multiple_of(b * HEAD_DIM, HEAD_DIM)
            q = q_s[pl.ds(r0, CHUNK), :]
            kk = kk_s[pl.ds(r0, CHUNK), :]
            da2 = da2_s[pl.ds(r0, CHUNK), :]
            dq = dq_s[pl.ds(r0, CHUNK), :]
            dkk = dkk_s[pl.ds(r0, CHUNK), :]

            def scaled(entry):
                if entry == 0:
                    return q, kk, None
                e_m = factor(1 + entry, l0)
                return q * e_m, kk * e_m, e_m

            for p, (ea, eb) in enumerate(LEVEL_PAIRS):
                dm = mask_ref[p] * da2
                dm_t = dm.T.astype(BF16)
                qa, ka, e_a = scaled(ea)
                if eb is None:
                    rhs_k = jnp.concatenate([jnp.concatenate([ka.astype(BF16), zeros16], axis=1),
                                             jnp.concatenate([zeros16, zeros16], axis=1)], axis=0)
                else:
                    qb_, kb_, e_bb = scaled(eb)
                    rhs_k = jnp.concatenate([jnp.concatenate([ka.astype(BF16), zeros16], axis=1),
                                             jnp.concatenate([zeros16, kb_.astype(BF16)], axis=1)], axis=0)
                dq2 = _dot(dm.astype(BF16), rhs_k)
                parts = [(ea, qa, ka, e_a, dq2[:, :HEAD_DIM], _dot(dm_t[0:CHUNK], qa.astype(BF16)))]
                if eb is not None:
                    parts.append((eb, qb_, kb_, e_bb, dq2[:, HEAD_DIM:],
                                  _dot(dm_t[CHUNK:2 * CHUNK], qb_.astype(BF16))))
                for entry, q_m, k_m, e_m, dq_m, dk_m in parts:
                    if entry == 0:
                        dq = dq + dq_m
                        dkk = dkk + dk_m
                    else:
                        dq = dq + e_m * dq_m
                        dkk = dkk + e_m * dk_m
                        dx_s[(1 + entry) * CHUNK:(2 + entry) * CHUNK, pl.ds(l0, HEAD_DIM)] = (
                            q_m * dq_m + k_m * dk_m).astype(BF16)
            dq_s[pl.ds(r0, CHUNK), :] = dq
            dkk_s[pl.ds(r0, CHUNK), :] = dkk
            return carry

        lax.fori_loop(0, cpb, local, 0, unroll=BACKWARD_UNROLL)

        dg_all = _dot(wexpt_ref[...], dx_s[...])
        for b in range(cpb):
            dg_s[b * CHUNK:(b + 1) * CHUNK, :] = dg_all[:, b * HEAD_DIM:(b + 1) * HEAD_DIM]
        t = jnp.where(valid, dg_s[...] / f - dkk_s[...], 0.0)
        dlb_ref[...] += jnp.sum(sn * t, axis=0, keepdims=True)
        dp_ref[0] = jnp.where(valid, dq_s[...], 0.0).astype(BF16)
        dp_ref[1] = ((1.0 - lb) * sg * sn * t).astype(BF16)

    head_block = lambda seg: pl.BlockSpec((1, rb_rows, width), lambda h, s: (seg, n_rb - 1 - s, h))
    row_block = pl.BlockSpec((rb_rows, width), lambda h, s: (n_rb - 1 - s, h))
    return pl.pallas_call(
        body, name="hgrn_backward",
        grid=(n_hb, n_rb),
        in_specs=[
            head_block(0), head_block(1), head_block(2),
            row_block,
            pl.BlockSpec((hps, cpb, HEAD_DIM, HEAD_DIM), lambda h, s: (h, n_rb - 1 - s, 0, 0)),
            pl.BlockSpec((hps, 1, N_EXP * CHUNK, lanes), lambda h, s: (h, n_rb - 1 - s, 0, 0)),
            row_block,
            pl.BlockSpec((2, width), lambda h, s: (0, h)),
            pl.BlockSpec((CHUNK, N_EXP * CHUNK), lambda h, s: (0, 0)),
            pl.BlockSpec((len(LEVEL_PAIRS), CHUNK, 2 * CHUNK), lambda h, s: (0, 0, 0)),
            ANY, ANY,
        ],
        out_specs=[
            pl.BlockSpec((3, rb_rows, width), lambda h, s: (0, n_rb - 1 - s, h)),
            pl.BlockSpec((1, width), lambda h, s: (0, h)),
            ANY, ANY,
        ],
        out_shape=[
            jax.ShapeDtypeStruct((3, rows, D_MODEL), BF16),
            jax.ShapeDtypeStruct((1, D_MODEL), F32),
            exchange.landing_w(), exchange.landing_blob(blob16),
        ],
        scratch_shapes=[
            pltpu.VMEM((hps, HEAD_DIM, HEAD_DIM), F32),
            pltpu.VMEM((hps, cpb, HEAD_DIM, HEAD_DIM), F32),
            pltpu.VMEM((hps, cpb, HEAD_DIM, HEAD_DIM), F32),
            pltpu.VMEM((hps, rb_rows, HEAD_DIM), F32),
            pltpu.VMEM((hps, rb_rows, HEAD_DIM), F32),
            pltpu.VMEM((hps, rb_rows, HEAD_DIM), BF16),
            pltpu.VMEM((hps, rb_rows, HEAD_DIM), BF16),
            pltpu.VMEM((hps, rb_rows, HEAD_DIM), F32),
            pltpu.VMEM((hps, rb_rows, HEAD_DIM), F32),
            pltpu.VMEM((hps, rb_rows, HEAD_DIM), F32),
            pltpu.VMEM((hps, N_EXP * CHUNK, lanes), BF16),
            pltpu.VMEM((hps, rb_rows, 2 * CHUNK), F32),
        ] + exchange.semaphores(),
        compiler_params=_params(("arbitrary", "arbitrary")),
    )(p3, p3, p3, d_o, states, e16, a2, lb_logits, wexp_t, masks2, dw16, blob16)


def _sigmoid(x):
    return 1.0 / (1.0 + jnp.exp(-x))


def _silu_and_grad(x):
    s = _sigmoid(x)
    return x * s, s * (1.0 + x * (1.0 - s))


def _window_sum(ext, width, forward_looking):
    n = ext.shape[0]
    s = ext
    step = 1
    while step < width:
        s = s + pltpu.roll(s, (n - step) if forward_looking else step, 0)
        step *= 2
    return s


def _mixers(o, p3, tokens, head, tgt, wdh, wdp, wout, poolw, hg_w, pool_scale, final_w, rows):
    tm = _tile(rows, 208)
    nt = rows // tm
    halo_blocks = tm // HALO
    n_grp = len(POOL_WINDOWS)
    q_rows = D_MODEL // N_CHIPS
    blob_rows = 3 * q_rows + n_grp * POOL_GDIM * POOL_GDIM // (N_CHIPS * D_MODEL)

    def body(o_ref, ghg_ref, u_ref, gpl_ref, mhg_ref, mpl_ref, uh_ref, z_ref, t_ref,
             wdh_ref, wdp_ref, wout_ref, pw_ref, hgw_ref, ps_ref, fw_ref, head_ref,
             do_ref, dz2_ref, dp_ref, blob_ref, dpw_ref, small_ref, carry_ref):
        step = pl.program_id(0)
        tile = nt - 1 - step

        def add_to_blob(piece, dw):
            for k in range(N_CHIPS):
                blob_ref[k, piece * q_rows:(piece + 1) * q_rows, :] += dw[k * q_rows:(k + 1) * q_rows]

        @pl.when(step == 0)
        def _():
            blob_ref[...] = jnp.zeros_like(blob_ref)
            dpw_ref[...] = jnp.zeros_like(dpw_ref)
            small_ref[...] = jnp.zeros_like(small_ref)
            carry_ref[...] = jnp.zeros_like(carry_ref)

        row = tile * tm + lax.broadcasted_iota(jnp.int32, (tm, 1), 0)
        real = row >= PAD_ROWS
        pos1 = jnp.maximum(row - PAD_ROWS + 1, 1).astype(F32)

        u = jnp.where(real, u_ref[0], 0.0)
        halo_row = tile * tm - HALO + lax.broadcasted_iota(jnp.int32, (HALO, 1), 0)
        uh = jnp.where(halo_row >= PAD_ROWS, uh_ref[0], 0.0)
        ext = jnp.concatenate([uh, u], axis=0)
        pooled, inv_cnt, mixed = [], [], []
        for g, w in enumerate(POOL_WINDOWS):
            cols = slice(g * POOL_GDIM, (g + 1) * POOL_GDIM)
            inv = 1.0 / jnp.minimum(pos1, float(w))
            ws = _window_sum(ext[:, cols], w, False)[HALO:]
            pg = (ws * inv - u[:, cols]).astype(BF16)
            pooled.append(pg)
            inv_cnt.append(inv)
            mixed.append(_dot(pg, pw_ref[g]))
        mixed = jnp.concatenate(mixed, axis=1)
        gpl = gpl_ref[0]
        sp, dsp = _silu_and_grad(gpl)
        ps = ps_ref[...]
        a_pool = (mixed * ps * sp).astype(BF16)
        y_pool = _dot(a_pool, wdp_ref[...])

        o = o_ref[...]
        o_hat, rstd_h = [], []
        for h in range(N_HEADS):
            oh = o[:, h * HEAD_DIM:(h + 1) * HEAD_DIM]
            r = lax.rsqrt(jnp.mean(oh * oh, axis=-1, keepdims=True) + EPS)
            rstd_h.append(r)
            o_hat.append(oh * r)
        o_hat = jnp.concatenate(o_hat, axis=1)
        hgw = hgw_ref[...]
        o_n = o_hat * hgw
        ghg = ghg_ref[0]
        sh, dsh = _silu_and_grad(ghg)
        a_hg = (o_n * sh).astype(BF16)
        y_hg = _dot(a_hg, wdh_ref[...])

        s_mh = _sigmoid(mhg_ref[0])
        s_mp = _sigmoid(mpl_ref[0])
        merged = (s_mh * y_hg + s_mp * y_pool).astype(BF16)
        z2 = _padded_tile(z_ref[...], head_ref[...], tile) + _dot(merged, wout_ref[...])
        rstd2 = lax.rsqrt(jnp.mean(z2 * z2, axis=-1, keepdims=True) + EPS)
        zh = z2 * rstd2
        fw = fw_ref[...]
        target = _padded_tile(t_ref[...], jnp.zeros((FIRST_TOKEN_ROW, D_MODEL), F32), tile)
        err = jnp.where(row >= FIRST_TOKEN_ROW, zh * fw - target, 0.0)
        small_ref[ROW_LOSS:ROW_LOSS + 1, :] += jnp.sum(err * err, axis=0, keepdims=True) * (0.5 / D_MODEL)
        dy = err * (1.0 / D_MODEL)

        small_ref[ROW_FINAL_W:ROW_FINAL_W + 1, :] += jnp.sum(dy * zh, axis=0, keepdims=True)
        uu = dy * fw
        dz2 = rstd2 * (uu - zh * jnp.mean(uu * zh, axis=-1, keepdims=True))
        dz2_ref[...] = dz2
        dz2_16 = dz2.astype(BF16)
        dmerged = _dot_nt(dz2_16, wout_ref[...])
        add_to_blob(2, _dot_tn(merged, dz2_16))
        dy_hg = (s_mh * dmerged).astype(BF16)
        dy_pool = (s_mp * dmerged).astype(BF16)
        dp_ref[3] = (dmerged * y_hg * s_mh * (1.0 - s_mh)).astype(BF16)
        dp_ref[4] = (dmerged * y_pool * s_mp * (1.0 - s_mp)).astype(BF16)

        da_hg = _dot_nt(dy_hg, wdh_ref[...])
        add_to_blob(0, _dot_tn(a_hg, dy_hg))
        dp_ref[0] = (da_hg * o_n * dsh).astype(BF16)
        do_n = da_hg * sh
        small_ref[ROW_HG_W:ROW_HG_W + 1, :] += jnp.sum(do_n * o_hat, axis=0, keepdims=True)
        d_hat = do_n * hgw
        for h in range(N_HEADS):
            cols = slice(h * HEAD_DIM, (h + 1) * HEAD_DIM)
            dh_, oh_ = d_hat[:, cols], o_hat[:, cols]
            do_ref[:, cols] = rstd_h[h] * (dh_ - oh_ * jnp.mean(dh_ * oh_, axis=-1, keepdims=True))

        da_pool = _dot_nt(dy_pool, wdp_ref[...])
        add_to_blob(1, _dot_tn(a_pool, dy_pool))
        small_ref[ROW_POOL_SCALE:ROW_POOL_SCALE + 1, :] += jnp.sum(da_pool * mixed * sp, axis=0, keepdims=True)
        dp_ref[2] = (da_pool * mixed * ps * dsp).astype(BF16)
        dmixed = (da_pool * ps * sp).astype(BF16)
        carry = carry_ref[...]
        du, new_carry = [], []
        for g, w in enumerate(POOL_WINDOWS):
            cols = slice(g * POOL_GDIM, (g + 1) * POOL_GDIM)
            dmg = dmixed[:, cols]
            dpooled = _dot_nt(dmg, pw_ref[g])
            dpw_ref[g] += _dot_tn(pooled[g], dmg)
            dps = dpooled * inv_cnt[g]
            ext_b = jnp.concatenate([dps, carry[:, cols]], axis=0)
            du.append(_window_sum(ext_b, w, True)[:tm] - dpooled)
            new_carry.append(dps[:HALO])
        dp_ref[1] = jnp.where(real, jnp.concatenate(du, axis=1), 0.0).astype(BF16)
        carry_ref[...] = jnp.concatenate(new_carry, axis=1)

    row_block = pl.BlockSpec((tm, D_MODEL), lambda s: (nt - 1 - s, 0))
    seg_block = lambda seg: pl.BlockSpec((1, tm, D_MODEL), lambda s: (seg, nt - 1 - s, 0))
    whole = pl.BlockSpec(memory_space=pltpu.VMEM)
    return pl.pallas_call(
        body, name="mixers",
        grid=(nt,),
        in_specs=[
            row_block, seg_block(3), seg_block(4), seg_block(5), seg_block(6), seg_block(7),
            pl.BlockSpec((1, HALO, D_MODEL),
                         lambda s: (4, jnp.maximum((nt - 1 - s) * halo_blocks - 1, 0), 0)),
            _token_window(tm, lambda s: nt - 1 - s), _token_window(tm, lambda s: nt - 1 - s),
            whole, whole, whole, whole, whole, whole, whole, whole,
        ],
        out_specs=[
            row_block, row_block,
            pl.BlockSpec((5, tm, D_MODEL), lambda s: (0, nt - 1 - s, 0)),
            whole, whole, whole,
        ],
        out_shape=[
            jax.ShapeDtypeStruct((rows, D_MODEL), F32),
            jax.ShapeDtypeStruct((rows, D_MODEL), F32),
            jax.ShapeDtypeStruct((5, rows, D_MODEL), BF16),
            jax.ShapeDtypeStruct((N_CHIPS, blob_rows, D_MODEL), F32),
            jax.ShapeDtypeStruct((n_grp, POOL_GDIM, POOL_GDIM), F32),
            jax.ShapeDtypeStruct((SMALL_ROWS, D_MODEL), F32),
        ],
        scratch_shapes=[pltpu.VMEM((HALO, D_MODEL), F32)],
        compiler_params=_params(("arbitrary",)),
    )(o, p3, p3, p3, p3, p3, p3, tokens, tgt, wdh, wdp, wout, poolw, hg_w, pool_scale, final_w, head)


def _seg_specs(tm, row_of, seg_of):
    def spec_a(*g):
        k = seg_of(*g)
        return (jnp.minimum(k, 2), jnp.where(k < 3, row_of(*g), 0), 0)

    def spec_b(*g):
        k = seg_of(*g)
        return (jnp.maximum(k - 3, 0), jnp.where(k >= 3, row_of(*g), 0), 0)

    return pl.BlockSpec((1, tm, D_MODEL), spec_a), pl.BlockSpec((1, tm, D_MODEL), spec_b)


def _in_proj_weight_grad(h, dp, rows, name):
    n_seg = dp.shape[0]
    tm = _tile(rows, 1040)
    nt = rows // tm
    half = D_MODEL // 2

    def body(h_ref, dp_ref, part_ref, part16_ref, db_ref, acc_ref, bacc_ref, stage_ref, land_ref,
             send_sems, recv_sems):
        k, i = pl.program_id(0), pl.program_id(1)
        x, y, c = lax.axis_index("x"), lax.axis_index("y"), lax.axis_index("c")

        def to_sibling(seg):
            return pltpu.make_async_remote_copy(
                src_ref=stage_ref.at[seg], dst_ref=land_ref.at[seg], send_sem=send_sems.at[seg],
                recv_sem=recv_sems.at[seg], device_id=(x, y, 1 - c), device_id_type=MESH)

        @pl.when(i == 0)
        def _():
            acc_ref[...] = jnp.zeros_like(acc_ref)
            bacc_ref[...] = jnp.zeros_like(bacc_ref)

        dpt = dp_ref[0]
        acc_ref[...] += _dot_tn(h_ref[...], dpt)
        bacc_ref[...] += jnp.sum(dpt.astype(F32), axis=0, keepdims=True)

        @pl.when(i == nt - 1)
        def _():
            db_ref[0] = bacc_ref[...]
            part_ref[k] = acc_ref[pl.ds(pl.multiple_of(c * half, half), half), :]
            stage_ref[k] = acc_ref[pl.ds(pl.multiple_of((1 - c) * half, half), half), :].astype(BF16)
            to_sibling(k).start()

        @pl.when((k == n_seg - 1) & (i == nt - 1))
        def _():
            for seg in range(n_seg):
                to_sibling(seg).wait_recv()
                total = part_ref[seg] + land_ref[seg].astype(F32)
                part_ref[seg] = total
                part16_ref[seg] = total.astype(BF16)
            for seg in range(n_seg):
                to_sibling(seg).wait_send()

    whole = pl.BlockSpec(memory_space=pltpu.VMEM)
    return pl.pallas_call(
        body, name=name,
        grid=(n_seg, nt),
        in_specs=[pl.BlockSpec((tm, D_MODEL), lambda k, i: (i, 0)),
                  pl.BlockSpec((1, tm, D_MODEL), lambda k, i: (k, i, 0))],
        out_specs=[whole, whole, pl.BlockSpec((1, 1, D_MODEL), lambda k, i: (k, 0, 0))],
        out_shape=[
            jax.ShapeDtypeStruct((n_seg, half, D_MODEL), F32),
            jax.ShapeDtypeStruct((n_seg, half, D_MODEL), BF16),
            jax.ShapeDtypeStruct((n_seg, 1, D_MODEL), F32),
        ],
        scratch_shapes=[
            pltpu.VMEM((D_MODEL, D_MODEL), F32), pltpu.VMEM((1, D_MODEL), F32),
            pltpu.VMEM((n_seg, half, D_MODEL), BF16),
            pltpu.VMEM((n_seg, half, D_MODEL), BF16),
            pltpu.SemaphoreType.DMA((n_seg,)), pltpu.SemaphoreType.DMA((n_seg,)),
        ],
        compiler_params=_params(("arbitrary", "arbitrary")),
    )(h, dp)


def _input_grad(dpa, dpb, w4, tokens, head, dz2, norm_w, dw16, rows):
    tm = _tile(rows, 1040)
    nt = rows // tm
    assert nt >= 2, rows
    exchange = _GradExchange(SEGS_REC, with_blob=False)

    def body(dpa_ref, dpb_ref, w_ref, z_ref, head_ref, dz2_ref, nw_ref, dw_ref, gx_ref, dmeta_ref, dnw_ref, rxw_ref,
             acc_ref, dz_buf, out_sem, send_sems, recv_sems):
        i, k = pl.program_id(0), pl.program_id(1)

        def first_tile_out():
            return pltpu.make_async_copy(dz_buf.at[pl.ds(FIRST_TOKEN_ROW, tm - FIRST_TOKEN_ROW), :],
                                         gx_ref.at[pl.ds(0, tm - FIRST_TOKEN_ROW), :], out_sem)

        def tile_out(tile):
            start = pl.multiple_of(tile * tm - FIRST_TOKEN_ROW, HALO)
            return pltpu.make_async_copy(dz_buf, gx_ref.at[pl.ds(start, tm), :], out_sem)

        @pl.when((i == 0) & (k == 0))
        def _():
            exchange.start(dw_ref, rxw_ref, None, None, send_sems, recv_sems)
            dnw_ref[...] = jnp.zeros_like(dnw_ref)

        @pl.when((i == nt - 1) & (k == N_SEG - 1))
        def _():
            exchange.wait(dw_ref, rxw_ref, None, None, send_sems, recv_sems)

        @pl.when(k == 0)
        def _():
            acc_ref[...] = jnp.zeros_like(acc_ref)

        @pl.when(k < 3)
        def _():
            acc_ref[...] += _dot_nt(dpa_ref[0], w_ref[0])

        @pl.when(k >= 3)
        def _():
            acc_ref[...] += _dot_nt(dpb_ref[0], w_ref[0])

        @pl.when(k == N_SEG - 1)
        def _():
            zt = _padded_tile(z_ref[...], head_ref[...], i)
            rstd = lax.rsqrt(jnp.mean(zt * zt, axis=-1, keepdims=True) + EPS)
            zh = zt * rstd
            dh = acc_ref[...]
            dnw_ref[...] += jnp.sum(dh * zh, axis=0, keepdims=True)
            uu = dh * nw_ref[...]
            dz = dz2_ref[...] + rstd * (uu - zh * jnp.mean(uu * zh, axis=-1, keepdims=True))

            @pl.when(i == 1)
            def _():
                first_tile_out().wait()

            @pl.when(i >= 2)
            def _():
                tile_out(i - 1).wait()

            dz_buf[...] = dz

            @pl.when(i == 0)
            def _():
                dmeta_ref[...] = dz[PAD_ROWS:FIRST_TOKEN_ROW]
                first_tile_out().start()

            @pl.when(i > 0)
            def _():
                tile_out(i).start()

            @pl.when(i == nt - 1)
            def _():
                tile_out(i).wait()

    spec_a, spec_b = _seg_specs(tm, lambda i, k: i, lambda i, k: k)
    last_only = pl.BlockSpec((tm, D_MODEL), lambda i, k: (jnp.where(k == N_SEG - 1, i, 0), 0))
    return pl.pallas_call(
        body, name="input_grad",
        grid=(nt, N_SEG),
        in_specs=[
            spec_a, spec_b,
            pl.BlockSpec((1, D_MODEL, D_MODEL), lambda i, k: (k // 2, 0, k % 2)),
            _token_window(tm, lambda i, k: jnp.where(k == N_SEG - 1, i, 0)),
            pl.BlockSpec((FIRST_TOKEN_ROW, D_MODEL), lambda i, k: (0, 0)),
            last_only,
            pl.BlockSpec((1, D_MODEL), lambda i, k: (0, 0)),
            ANY,
        ],
        out_specs=[
            ANY,
            pl.BlockSpec((N_META, D_MODEL), lambda i, k: (0, 0)),
            pl.BlockSpec((1, D_MODEL), lambda i, k: (0, 0)),
            ANY,
        ],
        out_shape=[
            jax.ShapeDtypeStruct((rows - FIRST_TOKEN_ROW, D_MODEL), F32),
            jax.ShapeDtypeStruct((N_META, D_MODEL), F32),
            jax.ShapeDtypeStruct((1, D_MODEL), F32),
            exchange.landing_w(),
        ],
        scratch_shapes=[pltpu.VMEM((tm, D_MODEL), F32), pltpu.VMEM((tm, D_MODEL), F32),
                        pltpu.SemaphoreType.DMA] + exchange.semaphores(),
        compiler_params=_params(("arbitrary", "arbitrary")),
    )(dpa, dpb, w4, tokens, head, dz2, norm_w, dw16)


def _local_step(tokens, m4, tgt, w4, blob4, seg_order, norm_w, b_in, lb_logits, hg_w, pool_scale, final_w):
    rows = FIRST_TOKEN_ROW + tokens.shape[0]
    q = D_MODEL // N_CHIPS
    n_grp = len(POOL_WINDOWS)
    pg = POOL_GDIM // N_CHIPS

    wexp2 = jnp.asarray(np.tile(_exponent_matrix(), (1, 2)), BF16)
    wexp_t = jnp.asarray(_exponent_matrix().T, BF16)
    masks2 = jnp.asarray(_paired_masks(), F32)

    h, p3, w4, head, _ = _in_proj(tokens, m4, norm_w, w4, b_in, seg_order, rows)
    o, states, e16, a2, blob4 = _hgrn_forward(p3, lb_logits, wexp2, masks2, blob4, rows)
    wdh = blob4[:, 0:q].reshape(D_MODEL, D_MODEL)
    wdp = blob4[:, q:2 * q].reshape(D_MODEL, D_MODEL)
    wout = blob4[:, 2 * q:3 * q].reshape(D_MODEL, D_MODEL)
    poolw = blob4[:, 3 * q:].reshape(N_CHIPS, n_grp, pg, POOL_GDIM).transpose(1, 0, 2, 3)
    poolw = poolw.reshape(n_grp, POOL_GDIM, POOL_GDIM)
    d_o, dz2, dpb, dblob4, dpw, small = _mixers(
        o, p3, tokens, head, tgt, wdh, wdp, wout, poolw, hg_w, pool_scale, final_w, rows)
    dpw4 = dpw.reshape(n_grp, N_CHIPS, pg, POOL_GDIM).transpose(1, 0, 2, 3)
    dpw4 = dpw4.reshape(N_CHIPS, n_grp * pg * POOL_GDIM // D_MODEL, D_MODEL)
    dblob4 = dblob4.at[:, 3 * q:, :].set(dpw4)

    dw_mix, dw_mix16, db_mix = _in_proj_weight_grad(h, dpb, rows, "in_proj_weight_grad_mix")
    dpa, dlb, rxw_mix, rx_blob = _hgrn_backward(
        p3, d_o, states, e16, a2, lb_logits, wexp_t, masks2, dw_mix16, dblob4.astype(BF16), rows)
    dw_rec, dw_rec16, db_rec = _in_proj_weight_grad(h, dpa, rows, "in_proj_weight_grad_rec")
    d_tokens, d_meta, dnw, rxw_rec = _input_grad(dpa, dpb, w4, tokens, head, dz2, norm_w, dw_rec16, rows)

    small = jnp.concatenate([
        small[ROW_LOSS:ROW_LOSS + 1],
        d_meta,
        dnw,
        db_rec.reshape(len(SEGS_REC), D_MODEL), db_mix.reshape(len(SEGS_MIX), D_MODEL),
        dlb, jnp.zeros_like(dlb),
        small[ROW_HG_W:ROW_HG_W + 1], small[ROW_POOL_SCALE:ROW_POOL_SCALE + 1],
        small[ROW_FINAL_W:ROW_FINAL_W + 1],
        jnp.zeros((SMALL_ROWS - ROW_FINAL_W - 1, D_MODEL), F32),
    ], axis=0)
    return d_tokens, (dw_rec, dw_mix, rxw_rec, rxw_mix), (dblob4, rx_blob), small


ANY = pl.BlockSpec(memory_space=pl.ANY)
MESH = pl.DeviceIdType.MESH


def _place():
    x, y, c = lax.axis_index("x"), lax.axis_index("y"), lax.axis_index("c")
    chips = [(1 - x, y), (x, 1 - y), (1 - x, 1 - y)]
    return x, y, c, chips


class _ShardGather:
    def __init__(self, rows):
        self.half = rows // 2

    def semaphores(self):
        return [pltpu.SemaphoreType.DMA((6,)), pltpu.SemaphoreType.DMA((6,))]

    def _copy(self, k, slot, to, send_sems, recv_sems):
        return pltpu.make_async_remote_copy(src_ref=slot, dst_ref=slot, send_sem=send_sems.at[k],
                                            recv_sem=recv_sems.at[k], device_id=to, device_id_type=MESH)

    def _half(self, ref4, chip, which):
        return ref4.at[chip, pl.ds(which * self.half, self.half), :]

    def start(self, ref4, send_sems, recv_sems, which=(0, 1, 2)):
        x, y, c, chips = _place()
        for j in which:
            cx, cy = chips[j]
            self._copy(j, self._half(ref4, 2 * x + y, c), (cx, cy, c), send_sems, recv_sems).start()

    def start_diagonal_after_neighbours(self, ref4, send_sems, recv_sems):
        x, y, c, chips = _place()
        for j in (0, 1):
            cx, cy = chips[j]
            self._copy(j, self._half(ref4, 2 * x + y, c), (cx, cy, c), send_sems, recv_sems).wait_send()
        self.start(ref4, send_sems, recv_sems, which=(2,))

    def pass_on(self, j, ref4, send_sems, recv_sems):
        x, y, c, chips = _place()
        cx, cy = chips[j]
        landed = self._half(ref4, 2 * cx + cy, c)
        self._copy(j, landed, (cx, cy, c), send_sems, recv_sems).wait_recv()
        self._copy(3 + j, landed, (x, y, 1 - c), send_sems, recv_sems).start()

    def await_sibling(self, j, ref4, send_sems, recv_sems):
        x, y, c, chips = _place()
        cx, cy = chips[j]
        self._copy(3 + j, self._half(ref4, 2 * cx + cy, 1 - c), (x, y, 1 - c), send_sems, recv_sems).wait_recv()

    def finish(self, ref4, send_sems, recv_sems, which=(0, 1, 2)):
        x, y, c, chips = _place()
        for j, (cx, cy) in enumerate(chips):
            if j in which:
                self._copy(j, self._half(ref4, 2 * x + y, c), (cx, cy, c), send_sems, recv_sems).wait_send()
            self._copy(3 + j, self._half(ref4, 2 * cx + cy, c), (x, y, 1 - c), send_sems, recv_sems).wait_send()


class _GradExchange:
    def __init__(self, segs, with_blob):
        self.segs = tuple(segs)
        self.with_blob = with_blob

    def landing_w(self):
        return jax.ShapeDtypeStruct((N_CHIPS, 2, D_MODEL // 2, D_MODEL), BF16)

    def landing_blob(self, blob16):
        return jax.ShapeDtypeStruct((N_DEV, blob16.shape[1] // 2, D_MODEL), BF16)

    def semaphores(self):
        n_send = len(self.segs) + (2 * N_CHIPS if self.with_blob else 0)
        n_recv = 2 * N_CHIPS + (N_DEV if self.with_blob else 0)
        return [pltpu.SemaphoreType.DMA((n_send,)), pltpu.SemaphoreType.DMA((n_recv,))]

    def _copies(self, dw_ref, rxw_ref, blob_ref, rxb_ref, send_sems, recv_sems):
        x, y, c = lax.axis_index("x"), lax.axis_index("y"), lax.axis_index("c")
        chip = 2 * x + y

        def relation(kx, ky, h):
            return (x ^ kx) * 4 + (y ^ ky) * 2 + (c ^ h)

        def copy(src, dst, send_k, recv_k, to):
            return functools.partial(pltpu.make_async_remote_copy, src_ref=src, dst_ref=dst,
                                     send_sem=send_sems.at[send_k], recv_sem=recv_sems.at[recv_k],
                                     device_id=to, device_id_type=MESH)

        sends, recvs = [], []
        for i, s in enumerate(self.segs):
            kx, ky = (s // 2) >> 1, (s // 2) & 1
            r = (x ^ kx) * 2 + (y ^ ky)
            sends.append((r != 0, copy(dw_ref.at[i], rxw_ref.at[r, s % 2], i, 2 * r + s % 2, (kx, ky, c))))
        for j in range(2):
            mine = [s // 2 for s in self.segs if s % 2 == j]
            if mine:
                cond = functools.reduce(lambda a, b: a | b, [chip == k for k in mine])
                for r in range(1, N_CHIPS):
                    slot = rxw_ref.at[r, j]
                    recvs.append((cond, copy(slot, slot, 0, 2 * r + j, (x, y, c))))
        if self.with_blob:
            hb = blob_ref.shape[1] // 2
            first_send, first_recv = len(self.segs), 2 * N_CHIPS
            for k in range(N_CHIPS):
                for h in range(2):
                    r = relation(k >> 1, k & 1, h)
                    sends.append((r != 0, copy(blob_ref.at[k, pl.ds(h * hb, hb), :], rxb_ref.at[r],
                                               first_send + 2 * k + h, first_recv + r, (k >> 1, k & 1, h))))
            for r in range(1, N_DEV):
                slot = rxb_ref.at[r]
                recvs.append((None, copy(slot, slot, 0, first_recv + r, (x, y, c))))
        return sends, recvs

    def start(self, *refs):
        sends, _ = self._copies(*refs)
        for cond, make in sends:
            pl.when(cond)(lambda make=make: make().start())

    def wait(self, *refs):
        sends, recvs = self._copies(*refs)
        for cond, make in sends:
            pl.when(cond)(lambda make=make: make().wait_send())
        for cond, make in recvs:
            if cond is None:
                make().wait_recv()
            else:
                pl.when(cond)(lambda make=make: make().wait_recv())


def _sum_landed(own, rx_ref):
    total = own
    for r in range(1, rx_ref.shape[0]):
        total = total + rx_ref[r, 0].astype(F32)
    return total


def _finish_w(dw_rec, dw_mix, rx_rec, rx_mix, place_arr):
    half = D_MODEL // 2
    tm = _tile(half, 256)
    n_rec = len(SEGS_REC)

    def body(place_ref, own_rec_ref, own_mix_ref, rx_rec_ref, rx_mix_ref, out_ref):
        seg = 2 * place_ref[0] + pl.program_id(0)

        @pl.when(seg < n_rec)
        def _():
            out_ref[0] = _sum_landed(own_rec_ref[0], rx_rec_ref)

        @pl.when(seg >= n_rec)
        def _():
            out_ref[0] = _sum_landed(own_mix_ref[0], rx_mix_ref)

    def own_spec(first, count):
        def index(j, i, place_ref):
            seg = 2 * place_ref[0] + j
            return (jnp.clip(seg - first, 0, count - 1), i, 0)
        return pl.BlockSpec((1, tm, D_MODEL), index)

    rx_spec = pl.BlockSpec((N_CHIPS, 1, tm, D_MODEL), lambda j, i, place_ref: (0, j, i, 0))
    return pl.pallas_call(
        body, name="finish_w",
        grid_spec=pltpu.PrefetchScalarGridSpec(
            num_scalar_prefetch=1, grid=(2, half // tm),
            in_specs=[own_spec(0, n_rec), own_spec(n_rec, len(SEGS_MIX)), rx_spec, rx_spec],
            out_specs=pl.BlockSpec((1, tm, D_MODEL), lambda j, i, place_ref: (place_ref[1], i, j))),
        out_shape=jax.ShapeDtypeStruct((2, half, 2 * D_MODEL), F32),
        compiler_params=_params(("arbitrary", "arbitrary")),
    )(place_arr, dw_rec, dw_mix, rx_rec, rx_mix)


def _finish_blob(dblob4, rx_blob, place_arr):
    n, rows, cols = rx_blob.shape
    tm = _tile(rows, 256)

    def body(place_ref, own_ref, rx_ref, out_ref):
        out_ref[0] = _sum_landed(own_ref[0, 0], rx_ref)

    return pl.pallas_call(
        body, name="finish_blob",
        grid_spec=pltpu.PrefetchScalarGridSpec(
            num_scalar_prefetch=1, grid=(rows // tm,),
            in_specs=[pl.BlockSpec((1, 1, tm, cols), lambda i, place_ref: (place_ref[0], place_ref[1], i, 0)),
                      pl.BlockSpec((n, 1, tm, cols), lambda i, place_ref: (0, 0, i, 0))],
            out_specs=pl.BlockSpec((1, tm, cols), lambda i, place_ref: (place_ref[1], i, 0))),
        out_shape=jax.ShapeDtypeStruct((2, rows, cols), F32),
        compiler_params=_params(("arbitrary",)),
    )(place_arr, dblob4.reshape(N_CHIPS, 2, rows, cols), rx_blob.reshape(n, 1, rows, cols))


def _share_finished(fw2, fb2, small):
    def body(w_in_ref, b_in_ref, small_ref, w_ref, b_ref, s_ref, bounce, local_sem, send_sems, recv_sems):
        x, y, c, _ = _place()
        sibling = (x, y, 1 - c)

        def copy(k, src, dst, to):
            return pltpu.make_async_remote_copy(src_ref=src, dst_ref=dst, send_sem=send_sems.at[k],
                                                recv_sem=recv_sems.at[k], device_id=to, device_id_type=MESH)

        sends = [copy(0, w_ref.at[c], w_ref.at[c], sibling), copy(1, b_ref.at[c], b_ref.at[c], sibling)]
        for r in range(1, N_DEV):
            peer = (x ^ ((r >> 2) & 1), y ^ ((r >> 1) & 1), c ^ (r & 1))
            sends.append(copy(1 + r, small_ref, s_ref.at[r], peer))
        for cp in sends:
            cp.start()
        for src, dst in ((small_ref, bounce), (bounce, s_ref.at[0])):
            own = pltpu.make_async_copy(src, dst, local_sem)
            own.start()
            own.wait()
        landed = [w_ref.at[1 - c], b_ref.at[1 - c]] + [s_ref.at[r] for r in range(1, N_DEV)]
        for k, slot in enumerate(landed):
            copy(k, slot, slot, (x, y, c)).wait_recv()
        for cp in sends:
            cp.wait_send()

    same = lambda a: jax.ShapeDtypeStruct(a.shape, a.dtype)
    n_sem = 2 + N_DEV - 1
    return pl.pallas_call(
        body, name="share_finished",
        in_specs=[ANY, ANY, ANY], out_specs=[ANY, ANY, ANY],
        out_shape=[same(fw2), same(fb2), jax.ShapeDtypeStruct((N_DEV,) + small.shape, F32)],
        input_output_aliases={0: 0, 1: 1},
        scratch_shapes=[pltpu.VMEM(small.shape, F32), pltpu.SemaphoreType.DMA,
                        pltpu.SemaphoreType.DMA((n_sem,)), pltpu.SemaphoreType.DMA((n_sem,))],
    )(fw2, fb2, small)


def _sum_small(slots, lb_logits, me_arr):
    def body(me_ref, slots_ref, lbl_ref, out_ref):
        me = me_ref[0]
        total = slots_ref[me]
        for d in range(1, N_DEV):
            total = total + slots_ref[d ^ me]
        out_ref[...] = total
        out_ref[ROW_LOSS:ROW_LOSS + 1, :] = jnp.broadcast_to(
            jnp.sum(total[ROW_LOSS:ROW_LOSS + 1, :], axis=-1, keepdims=True), (1, D_MODEL))
        lb = _lower_bound(lbl_ref[...])
        g0 = total[ROW_LB:ROW_LB + 1, :] * lb * (1.0 - lb)
        out_ref[ROW_LB:ROW_LB + 1, :] = g0
        out_ref[ROW_LB + 1:ROW_LB + 2, :] = -g0

    return pl.pallas_call(
        body, name="sum_small",
        grid_spec=pltpu.PrefetchScalarGridSpec(
            num_scalar_prefetch=1, grid=(1,),
            in_specs=[pl.BlockSpec((N_DEV, SMALL_ROWS, D_MODEL), lambda i, me_ref: (0, 0, 0)),
                      pl.BlockSpec((2, D_MODEL), lambda i, me_ref: (0, 0))],
            out_specs=pl.BlockSpec((SMALL_ROWS, D_MODEL), lambda i, me_ref: (0, 0))),
        out_shape=jax.ShapeDtypeStruct((SMALL_ROWS, D_MODEL), F32),
        compiler_params=_params(("arbitrary",)),
    )(me_arr, slots, lb_logits)


def _adamw_step(w, g, m, v):
    c1 = 1.0 / (1.0 - ADAM_B1 ** ADAM_STEP)
    c2 = 1.0 / (1.0 - ADAM_B2 ** ADAM_STEP)
    nm = ADAM_B1 * m + (1.0 - ADAM_B1) * g
    nv = ADAM_B2 * v + (1.0 - ADAM_B2) * (g * g)
    return -ADAM_LR * ((nm * c1) / (jnp.sqrt(nv * c2) + ADAM_EPS) + ADAM_WD * w), nm, nv


SMALL_PARAMS = (("norm_w", ROW_NORM_W, 1), ("b_in", ROW_B_IN, N_SEG), ("lb_logits", ROW_LB, 2),
                ("hg_norm_w", ROW_HG_W, 1), ("pool_scale", ROW_POOL_SCALE, 1), ("final_norm_w", ROW_FINAL_W, 1))


def _update_small(tot, triples):
    n = len(SMALL_PARAMS)

    def body(tot_ref, *refs):
        ins, outs = refs[:3 * n], refs[3 * n:]
        for p, (name, row, n_rows) in enumerate(SMALL_PARAMS):
            w_ref, m_ref, v_ref = ins[3 * p:3 * p + 3]
            g_ref, d_ref, nm_ref, nv_ref = outs[4 * p:4 * p + 4]
            if w_ref.shape[0] == n_rows:
                pieces = [(slice(None), slice(None), tot_ref[row:row + n_rows, :])]
            else:
                pieces = [(slice(None), slice(k * D_MODEL, (k + 1) * D_MODEL), tot_ref[row + k:row + k + 1, :])
                          for k in range(n_rows)]
            for rows_, cols_, g in pieces:
                d, nm, nv = _adamw_step(w_ref[rows_, cols_], g, m_ref[rows_, cols_], v_ref[rows_, cols_])
                g_ref[rows_, cols_] = g
                d_ref[rows_, cols_] = d
                nm_ref[rows_, cols_] = nm
                nv_ref[rows_, cols_] = nv

    whole = pl.BlockSpec(memory_space=pltpu.VMEM)
    flat = [a for t in triples for a in t]
    out_shape = [jax.ShapeDtypeStruct(t[0].shape, F32) for t in triples for _ in range(4)]
    outs = pl.pallas_call(
        body, name="update_small",
        in_specs=[whole] * (1 + len(flat)), out_specs=[whole] * len(out_shape), out_shape=out_shape,
        compiler_params=_params(),
    )(tot, *flat)
    return [tuple(outs[4 * p:4 * p + 4]) for p in range(n)]


def _update_blob(g_blob, triples):
    q_rows = triples[0][0].shape[0]
    pool_rows = triples[3][0].shape[0]
    steps = q_rows // pool_rows

    def body(*refs):
        ins, outs = refs[:16], refs[16:]
        for p in range(4):
            g_ref, (w_ref, m_ref, v_ref) = ins[p], ins[4 + 3 * p:7 + 3 * p]
            go_ref, d_ref, nm_ref, nv_ref = outs[4 * p:4 * p + 4]

            def update():
                g = g_ref[...]
                go_ref[...] = g
                d_ref[...], nm_ref[...], nv_ref[...] = _adamw_step(w_ref[...], g, m_ref[...], v_ref[...])

            if p < 3:
                update()
            else:
                pl.when(pl.program_id(0) == 0)(update)

    blk = pl.BlockSpec((pool_rows, D_MODEL), lambda i: (i, 0))
    once = pl.BlockSpec((pool_rows, D_MODEL), lambda i: (0, 0))
    g_specs = [pl.BlockSpec((pool_rows, D_MODEL), lambda i, p=p: (steps * p + i, 0)) for p in range(3)]
    g_specs.append(pl.BlockSpec((pool_rows, D_MODEL), lambda i: (3 * steps, 0)))
    piece_specs = [blk] * 9 + [once] * 3
    out_specs = [blk] * 12 + [once] * 4
    out_shape = [jax.ShapeDtypeStruct(t[0].shape, F32) for t in triples for _ in range(4)]
    outs = pl.pallas_call(
        body, name="update_blob",
        grid=(steps,), in_specs=g_specs + piece_specs, out_specs=out_specs, out_shape=out_shape,
        compiler_params=_params(("arbitrary",)),
    )(g_blob, g_blob, g_blob, g_blob, *[a for t in triples for a in t])
    return [tuple(outs[4 * p:4 * p + 4]) for p in range(4)]


def _adamw(w, g, m, v):
    rows, cols = w.shape
    tm = _tile(rows, 256, mult=8) if rows % 8 == 0 else rows

    def body(w_ref, g_ref, m_ref, v_ref, d_ref, nm_ref, nv_ref):
        d_ref[...], nm_ref[...], nv_ref[...] = _adamw_step(w_ref[...], g_ref[...], m_ref[...], v_ref[...])

    blk = pl.BlockSpec((tm, cols), lambda i: (i, 0))
    sds = jax.ShapeDtypeStruct((rows, cols), F32)
    return pl.pallas_call(
        body, name="adamw",
        grid=(rows // tm,), in_specs=[blk] * 4, out_specs=[blk] * 3, out_shape=[sds] * 3,
        compiler_params=_params(("arbitrary",)),
    )(w, g, m, v)


def kernel(x, meta_tokens, norm_w, w_in, b_in, lb_logits, hg_norm_w, pool_w, pool_scale, w_down_hg, w_down_pool, w_out, final_norm_w, loss_target, m_meta_tokens, m_norm_w, m_w_in, m_b_in, m_lb_logits, m_hg_norm_w, m_pool_w, m_pool_scale, m_w_down_hg, m_w_down_pool, m_w_out, m_final_norm_w, v_meta_tokens, v_norm_w, v_w_in, v_b_in, v_lb_logits, v_hg_norm_w, v_pool_w, v_pool_scale, v_w_down_hg, v_w_down_pool, v_w_out, v_final_norm_w):
    seq = x.shape[1]
    xi, yi, ci = lax.axis_index("x"), lax.axis_index("y"), lax.axis_index("c")
    chip = 2 * xi + yi
    place_arr = jnp.stack([chip, ci]).astype(jnp.int32)
    me_arr = jnp.reshape(4 * xi + 2 * yi + ci, (1,)).astype(jnp.int32)
    q = D_MODEL // N_CHIPS

    def blob_of(wdh, wdp, wo, pw):
        return jnp.concatenate([wdh[0], wdp[0], wo[0], pw[0].reshape(-1, D_MODEL)], axis=0)

    def in_every_slot(a):
        return jnp.broadcast_to(a[None], (N_CHIPS,) + a.shape)

    m4 = in_every_slot(meta_tokens)
    w4 = in_every_slot(w_in[0].astype(BF16))
    blob4 = in_every_slot(blob_of(w_down_hg, w_down_pool, w_out, pool_w).astype(BF16))
    seg_order = jnp.stack([2 * (chip ^ rel) + t for rel in (0, 2, 1, 3) for t in (0, 1)]).astype(jnp.int32)

    fw2 = final_norm_w.reshape(1, D_MODEL)
    d_tokens, w_parts, blob_parts, small = _local_step(
        x[0], m4, loss_target[0], w4, blob4, seg_order, norm_w, b_in, lb_logits, hg_norm_w, pool_scale, fw2)
    grad_x = d_tokens[None]

    fin_w = _finish_w(*w_parts, place_arr)
    fin_b = _finish_blob(*blob_parts, place_arr)
    gw2, gb2, slots = _share_finished(fin_w, fin_b, small)
    tot = _sum_small(slots, lb_logits, me_arr)
    g_w_in = gw2.reshape(D_MODEL, 2 * D_MODEL)
    g_blob = gb2.reshape(-1, D_MODEL)

    d_win, nm_win, nv_win = _adamw(w_in[0], g_w_in, m_w_in[0], v_w_in[0])
    pool_rows = lambda a: a[0].reshape(-1, D_MODEL)
    blob_results = _update_blob(g_blob, [
        (w_down_hg[0], m_w_down_hg[0], v_w_down_hg[0]), (w_down_pool[0], m_w_down_pool[0], v_w_down_pool[0]),
        (w_out[0], m_w_out[0], v_w_out[0]), (pool_rows(pool_w), pool_rows(m_pool_w), pool_rows(v_pool_w))])
    g_meta = lax.dynamic_slice_in_dim(tot[ROW_META:ROW_META + N_META], chip * q, q, axis=1)
    d_meta, nm_meta, nv_meta = _adamw(meta_tokens, g_meta, m_meta_tokens, v_meta_tokens)

    as_row = lambda a: a.reshape(1, D_MODEL)
    small_results = _update_small(tot, [
        (norm_w, m_norm_w, v_norm_w), (b_in, m_b_in, v_b_in), (lb_logits, m_lb_logits, v_lb_logits),
        (hg_norm_w, m_hg_norm_w, v_hg_norm_w), (pool_scale, m_pool_scale, v_pool_scale),
        (as_row(final_norm_w), as_row(m_final_norm_w), as_row(v_final_norm_w))])

    def leaves(kind, meta_part, win_part):
        nw, bi, lbl, hg, ps, fw = [r[kind] for r in small_results]
        wdh, wdp, wo, pw = [r[kind] for r in blob_results]
        return [meta_part, nw, win_part[None], bi, lbl, hg, pw.reshape(pool_w.shape), ps,
                wdh[None], wdp[None], wo[None], fw.reshape(D_MODEL)]

    loss = tot[ROW_LOSS, 0]
    return (loss, grad_x,
            *leaves(0, g_meta, g_w_in),
            *leaves(1, d_meta, d_win),
            *leaves(2, nm_meta, nm_win),
            *leaves(3, nv_meta, nv_win))
```

```python
import functools

import numpy as np
import jax
import jax.numpy as jnp
from jax import lax
from jax.experimental import pallas as pl
from jax.experimental.pallas import tpu as pltpu

F32 = jnp.float32
BF16 = jnp.bfloat16

D_MODEL = 1024
N_SEG = 8
N_HEADS = 8
HEAD_DIM = 128
CHUNK = 64
N_META = 16
PAD_ROWS = CHUNK - N_META
FIRST_TOKEN_ROW = CHUNK
LEVELS = (32, 16, 8, 4, 2, 1)
N_EXP = 2 + len(LEVELS)
POOL_WINDOWS = (2, 4, 8, 16)
POOL_GDIM = D_MODEL // len(POOL_WINDOWS)
HALO = 16
FORWARD_HEADS_PER_STEP = 4
BACKWARD_HEADS_PER_STEP = 2
LOCAL_UNROLL = 13
BACKWARD_UNROLL = 13
EPS = 1e-6
N_CHIPS = 4
N_DEV = 8
SEGS_REC = (0, 1, 2)
SEGS_MIX = (3, 4, 5, 6, 7)

ADAM_LR = 0.001
ADAM_B1 = 0.9
ADAM_B2 = 0.999
ADAM_EPS = 1e-08
ADAM_WD = 0.01
ADAM_STEP = 10

VMEM_LIMIT_BYTES = 56 * 1024 * 1024

ROW_LOSS = 0
ROW_META = 1
ROW_NORM_W = ROW_META + N_META
ROW_B_IN = ROW_NORM_W + 1
ROW_LB = ROW_B_IN + N_SEG
ROW_HG_W = ROW_LB + 2
ROW_POOL_SCALE = ROW_HG_W + 1
ROW_FINAL_W = ROW_POOL_SCALE + 1
SMALL_ROWS = 32


def _tile(total, cap, mult=16):
    best = None
    for t in range(mult, min(total, cap) + 1, mult):
        if total % t == 0:
            best = t
    assert best is not None, (total, cap, mult)
    return best


def _token_window(tm, tile_of):
    def index(*grid):
        return (pl.multiple_of(jnp.maximum(tile_of(*grid) * tm - FIRST_TOKEN_ROW, 0), HALO), 0)
    return pl.BlockSpec((pl.Element(tm), pl.Element(D_MODEL)), index)


def _padded_tile(window, head, tile):
    first = jnp.concatenate([head, pltpu.roll(window, FIRST_TOKEN_ROW, 0)[FIRST_TOKEN_ROW:]], axis=0)
    return jnp.where(tile == 0, first, window)


def _params(sem=None):
    return pltpu.CompilerParams(dimension_semantics=sem, vmem_limit_bytes=VMEM_LIMIT_BYTES)


def _dot(a, b):
    return jnp.dot(a, b, preferred_element_type=F32)


def _dot_nt(a, b):
    return lax.dot_general(a, b, (((1,), (1,)), ((), ())), preferred_element_type=F32)


def _dot_tn(a, b):
    return lax.dot_general(a, b, (((0,), (0,)), ((), ())), preferred_element_type=F32)


def _sigmoid_pair(x):
    t = jnp.exp(-jnp.abs(x))
    r = 1.0 / (1.0 + t)
    pos = x >= 0
    return jnp.where(pos, r, t * r), jnp.where(pos, t * r, r)


def _exponent_matrix():
    t = np.arange(CHUNK)[:, None]
    j = np.arange(CHUNK)[None, :]
    blocks = [j <= t, j > t]
    for m in LEVELS:
        rho = (t // (2 * m)) * (2 * m) + m
        upper = (t >= rho) & (j > rho) & (j <= t)
        lower = (t < rho) & (j > t) & (j <= rho)
        blocks.append(upper | lower)
    return np.concatenate(blocks, axis=0).astype(np.float32)


def _pair_masks():
    t = np.arange(CHUNK)[:, None]
    s = np.arange(CHUNK)[None, :]
    masks = [t == s]
    for m in LEVELS:
        same = (t // (2 * m)) == (s // (2 * m))
        masks.append(same & ((t % (2 * m)) >= m) & ((s % (2 * m)) < m))
    return np.stack(masks).astype(np.float32)


LEVEL_PAIRS = ((0, 1), (2, 3), (4, 5), (6, None))


def _paired_masks():
    m = _pair_masks()
    zero = np.zeros_like(m[0])
    return np.stack([np.concatenate([m[a], zero if b is None else m[b]], axis=1) for a, b in LEVEL_PAIRS])


def _lower_bound(lbl):
    return 1.0 / (1.0 + jnp.exp(lbl[1:2, :] - lbl[0:1, :]))


def _in_proj(tokens, m4, norm_w, w4, b_in, seg_order, rows):
    tm = _tile(rows, 1040)
    nt = rows // tm
    gather = _ShardGather(w4.shape[1])

    def body(order_ref, z_ref, nw_ref, b_ref, w_in_ref, m_in_ref, h_ref, p_ref, w4_ref, head_ref, m4_ref,
             h_all, w_buf, w_sem, send_sems, recv_sems, meta_send, meta_recv, meta_sem):
        kk, i = pl.program_id(0), pl.program_id(1)

        @pl.when((kk == 0) & (i == 0))
        def _():
            x, y, c, chips = _place()

            def meta_copy(j, chip, to):
                return pltpu.make_async_remote_copy(
                    src_ref=m4_ref.at[chip], dst_ref=m4_ref.at[chip], send_sem=meta_send.at[j],
                    recv_sem=meta_recv.at[j], device_id=to, device_id_type=MESH)

            sends = [meta_copy(j, 2 * x + y, (cx, cy, c)) for j, (cx, cy) in enumerate(chips)]
            for cp in sends:
                cp.start()
            gather.start(w4_ref, send_sems, recv_sems, which=(0, 1))
            for j, (cx, cy) in enumerate(chips):
                meta_copy(j, 2 * cx + cy, (x, y, c)).wait_recv()
            for cp in sends:
                cp.wait_send()
            head_ref[0:PAD_ROWS, :] = jnp.zeros((PAD_ROWS, D_MODEL), F32)
            q_cols = D_MODEL // N_CHIPS
            for k in range(N_CHIPS):
                cp = pltpu.make_async_copy(
                    m4_ref.at[k], head_ref.at[pl.ds(PAD_ROWS, N_META), pl.ds(k * q_cols, q_cols)], meta_sem)
                cp.start()
                cp.wait()

        @pl.when((kk == 2) & (i == 0))
        def _():
            gather.start_diagonal_after_neighbours(w4_ref, send_sems, recv_sems)

        @pl.when(kk == 0)
        def _():
            zt = _padded_tile(z_ref[...], head_ref[...], i)
            rstd = lax.rsqrt(jnp.mean(zt * zt, axis=-1, keepdims=True) + EPS)
            h = (zt * rstd * nw_ref[...]).astype(BF16)
            h_all[pl.ds(pl.multiple_of(i * tm, 16), tm), :] = h
            h_ref[...] = h

        @pl.when((kk == 2) & (i == 0))
        def _():
            gather.pass_on(0, w4_ref, send_sems, recv_sems)
            gather.pass_on(1, w4_ref, send_sems, recv_sems)
            gather.await_sibling(0, w4_ref, send_sems, recv_sems)

        @pl.when((kk == 4) & (i == 0))
        def _():
            gather.await_sibling(1, w4_ref, send_sems, recv_sems)

        @pl.when((kk == 5) & (i == 0))
        def _():
            gather.pass_on(2, w4_ref, send_sems, recv_sems)

        @pl.when((kk == 6) & (i == 0))
        def _():
            gather.await_sibling(2, w4_ref, send_sems, recv_sems)

        def weights(which):
            seg = order_ref[2 * (kk // 2) + which]
            return pltpu.make_async_copy(
                w4_ref.at[seg // 2, :, pl.ds(pl.multiple_of((seg % 2) * D_MODEL, D_MODEL), D_MODEL)],
                w_buf.at[which], w_sem.at[which])

        @pl.when((i == 0) & (kk % 2 == 0))
        def _():
            weights(0).start()
            weights(1).start()
            weights(0).wait()

        @pl.when((i == 0) & (kk % 2 == 1))
        def _():
            weights(1).wait()

        p_ref[0] = _dot(h_all[pl.ds(pl.multiple_of(i * tm, 16), tm), :], w_buf[kk % 2]) + b_ref[...]

        @pl.when((kk == N_SEG - 1) & (i == nt - 1))
        def _():
            gather.finish(w4_ref, send_sems, recv_sems, which=(2,))

    first_pass = lambda kk, i, order_ref: (jnp.where(kk == 0, i, nt - 1), 0)
    return pl.pallas_call(
        body, name="in_proj",
        grid_spec=pltpu.PrefetchScalarGridSpec(
            num_scalar_prefetch=1, grid=(N_SEG, nt),
            in_specs=[
                _token_window(tm, lambda kk, i, order_ref: jnp.where(kk == 0, i, nt - 1)),
                pl.BlockSpec((1, D_MODEL), lambda kk, i, order_ref: (0, 0)),
                pl.BlockSpec((1, D_MODEL), lambda kk, i, order_ref: (0, order_ref[kk])),
                ANY, ANY,
            ],
            out_specs=[
                pl.BlockSpec((tm, D_MODEL), first_pass),
                pl.BlockSpec((1, tm, D_MODEL), lambda kk, i, order_ref: (order_ref[kk], i, 0)),
                ANY,
                pl.BlockSpec((FIRST_TOKEN_ROW, D_MODEL), lambda kk, i, order_ref: (0, 0)),
                ANY,
            ],
            scratch_shapes=[
                pltpu.VMEM((rows, D_MODEL), BF16),
                pltpu.VMEM((2, D_MODEL, D_MODEL), BF16),
                pltpu.SemaphoreType.DMA((2,)),
            ] + gather.semaphores() + [
                pltpu.SemaphoreType.DMA((N_CHIPS - 1,)), pltpu.SemaphoreType.DMA((N_CHIPS - 1,)),
                pltpu.SemaphoreType.DMA,
            ]),
        out_shape=[
            jax.ShapeDtypeStruct((rows, D_MODEL), BF16),
            jax.ShapeDtypeStruct((N_SEG, rows, D_MODEL), F32),
            jax.ShapeDtypeStruct(w4.shape, w4.dtype),
            jax.ShapeDtypeStruct((FIRST_TOKEN_ROW, D_MODEL), F32),
            jax.ShapeDtypeStruct(m4.shape, m4.dtype),
        ],
        input_output_aliases={4: 2, 5: 4},
        compiler_params=_params(("arbitrary", "arbitrary")),
    )(seg_order, tokens, norm_w, b_in, w4, m4)


def _hgrn_forward(p3, lb_logits, wexp2, masks2, blob4, rows):
    n_chunks = rows // CHUNK
    cpb = _tile(n_chunks, 13, mult=1)
    rb_rows = cpb * CHUNK
    n_rb = n_chunks // cpb
    lanes = cpb * HEAD_DIM
    hps = FORWARD_HEADS_PER_STEP
    n_hb = N_HEADS // hps
    width = hps * HEAD_DIM
    gather = _ShardGather(blob4.shape[1])

    def body(q_ref, fz_ref, v_ref, lbl_ref, wexp_ref, mask_ref, b_in_ref, o_ref, s_ref, e16_ref, a2_ref, b4_ref,
             st_all, e_all, u_all, q_all, kk_all, v_all, send_sems, recv_sems):
        rb = pl.program_id(1)

        @pl.when((pl.program_id(0) == 0) & (rb == 0))
        def _():
            gather.start(b4_ref, send_sems, recv_sems)

        @pl.when(rb == 0)
        def _():
            st_all[...] = jnp.zeros_like(st_all)

        for j in range(hps):
            cols = pl.ds(j * HEAD_DIM, HEAD_DIM)
            one_head(rb, q_ref.at[0, :, cols], fz_ref.at[0, :, cols], v_ref.at[0, :, cols], lbl_ref.at[:, cols],
                     wexp_ref, mask_ref, o_ref.at[:, cols], s_ref.at[j], e16_ref.at[j, 0], a2_ref.at[:, cols],
                     st_all.at[j], e_all.at[j], u_all.at[j], q_all.at[j], kk_all.at[j], v_all.at[j])

        @pl.when((pl.program_id(0) == n_hb // 2) & (rb == 0))
        def _():
            for j in range(N_CHIPS - 1):
                gather.pass_on(j, b4_ref, send_sems, recv_sems)

        @pl.when((pl.program_id(0) == n_hb - 1) & (rb == n_rb - 1))
        def _():
            for j in range(N_CHIPS - 1):
                gather.await_sibling(j, b4_ref, send_sems, recv_sems)
            gather.finish(b4_ref, send_sems, recv_sems)

    def one_head(rb, q_ref, fz_ref, v_ref, lbl_ref, wexp_ref, mask_ref, o_ref, s_ref, e16_ref, a2_ref,
                 st_ref, e_ref, u_ref, q_s, kk_s, v_s):
        lb = _lower_bound(lbl_ref[...])
        row = rb * rb_rows + lax.broadcasted_iota(jnp.int32, (rb_rows, 1), 0)
        valid = row >= PAD_ROWS
        sg, sn = _sigmoid_pair(fz_ref[...])
        g = jnp.where(valid, jnp.log(lb + (1.0 - lb) * sg), 0.0)
        kk_s[...] = jnp.where(valid, (1.0 - lb) * sn, 0.0)
        q_s[...] = jnp.where(valid, q_ref[...], 0.0)
        v_s[...] = jnp.where(valid, v_ref[...], 0.0).astype(BF16)
        hi = g.astype(BF16)
        mid = (g - hi.astype(F32)).astype(BF16)
        g2 = jnp.concatenate(
            [jnp.concatenate([hi[b * CHUNK:(b + 1) * CHUNK], mid[b * CHUNK:(b + 1) * CHUNK]], axis=0)
             for b in range(cpb)], axis=1)
        e_ref[...] = jnp.exp(_dot(wexp_ref[...], g2))
        e16_ref[...] = e_ref[...].astype(BF16)

        def contribution(b, carry):
            r0 = pl.multiple_of(b * CHUNK, CHUNK)
            l0 = pl.multiple_of(b * HEAD_DIM, HEAD_DIM)
            kc16 = (kk_s[pl.ds(r0, CHUNK), :] * e_ref[CHUNK:2 * CHUNK, pl.ds(l0, HEAD_DIM)]).astype(BF16)
            u_ref[b] = _dot_tn(v_s[pl.ds(r0, CHUNK), :], kc16)
            return carry

        lax.fori_loop(0, cpb, contribution, 0, unroll=LOCAL_UNROLL)

        def recur(b, st):
            l0 = pl.multiple_of(b * HEAD_DIM, HEAD_DIM)
            s_ref[b] = st
            return st * e_ref[CHUNK - 1:CHUNK, pl.ds(l0, HEAD_DIM)] + u_ref[b]

        st_ref[...] = lax.fori_loop(0, cpb, recur, st_ref[...], unroll=LOCAL_UNROLL)

        zeros16 = jnp.zeros((CHUNK, HEAD_DIM), BF16)

        def local(b, carry):
            r0 = pl.multiple_of(b * CHUNK, CHUNK)
            l0 = pl.multiple_of(b * HEAD_DIM, HEAD_DIM)
            q = q_s[pl.ds(r0, CHUNK), :]
            kk = kk_s[pl.ds(r0, CHUNK), :]
            v16 = v_s[pl.ds(r0, CHUNK), :]

            def scaled(entry):
                if entry == 0:
                    return q.astype(BF16), kk.astype(BF16)
                e_m = e_ref[(1 + entry) * CHUNK:(2 + entry) * CHUNK, pl.ds(l0, HEAD_DIM)]
                return (q * e_m).astype(BF16), (kk * e_m).astype(BF16)

            a2 = jnp.zeros((CHUNK, 2 * CHUNK), F32)
            for p, (ea, eb) in enumerate(LEVEL_PAIRS):
                qa, ka = scaled(ea)
                if eb is None:
                    prod = _dot_nt(qa, jnp.concatenate([ka, zeros16], axis=0))
                else:
                    qb_, kb_ = scaled(eb)
                    rhs = jnp.concatenate([jnp.concatenate([ka, zeros16], axis=1),
                                           jnp.concatenate([zeros16, kb_], axis=1)], axis=0)
                    prod = _dot_nt(jnp.concatenate([qa, qb_], axis=1), rhs)
                a2 = a2 + mask_ref[p] * prod
            a2_16 = a2.astype(BF16)
            a2_ref[pl.ds(r0, CHUNK), :] = a2_16
            qb16 = (q * e_ref[0:CHUNK, pl.ds(l0, HEAD_DIM)]).astype(BF16)
            o_ref[pl.ds(r0, CHUNK), :] = (_dot(a2_16, jnp.concatenate([v16, v16], axis=0))
                                          + _dot_nt(qb16, s_ref[b].astype(BF16)))
            return carry

        lax.fori_loop(0, cpb, local, 0, unroll=LOCAL_UNROLL)

    head_block = lambda seg: pl.BlockSpec((1, rb_rows, width), lambda h, r: (seg, r, h))
    return pl.pallas_call(
        body, name="hgrn_forward",
        grid=(n_hb, n_rb),
        in_specs=[
            head_block(0), head_block(1), head_block(2),
            pl.BlockSpec((2, width), lambda h, r: (0, h)),
            pl.BlockSpec((N_EXP * CHUNK, 2 * CHUNK), lambda h, r: (0, 0)),
            pl.BlockSpec((len(LEVEL_PAIRS), CHUNK, 2 * CHUNK), lambda h, r: (0, 0, 0)),
            ANY,
        ],
        out_specs=[
            pl.BlockSpec((rb_rows, width), lambda h, r: (r, h)),
            pl.BlockSpec((hps, cpb, HEAD_DIM, HEAD_DIM), lambda h, r: (h, r, 0, 0)),
            pl.BlockSpec((hps, 1, N_EXP * CHUNK, lanes), lambda h, r: (h, r, 0, 0)),
            pl.BlockSpec((rb_rows, width), lambda h, r: (r, h)),
            ANY,
        ],
        out_shape=[
            jax.ShapeDtypeStruct((rows, D_MODEL), F32),
            jax.ShapeDtypeStruct((N_HEADS, n_chunks, HEAD_DIM, HEAD_DIM), F32),
            jax.ShapeDtypeStruct((N_HEADS, n_rb, N_EXP * CHUNK, lanes), BF16),
            jax.ShapeDtypeStruct((rows, D_MODEL), BF16),
            jax.ShapeDtypeStruct(blob4.shape, blob4.dtype),
        ],
        input_output_aliases={6: 4},
        scratch_shapes=[
            pltpu.VMEM((hps, HEAD_DIM, HEAD_DIM), F32),
            pltpu.VMEM((hps, N_EXP * CHUNK, lanes), F32),
            pltpu.VMEM((hps, cpb, HEAD_DIM, HEAD_DIM), F32),
            pltpu.VMEM((hps, rb_rows, HEAD_DIM), F32),
            pltpu.VMEM((hps, rb_rows, HEAD_DIM), F32),
            pltpu.VMEM((hps, rb_rows, HEAD_DIM), BF16),
        ] + gather.semaphores(),
        compiler_params=_params(("arbitrary", "arbitrary")),
    )(p3, p3, p3, lb_logits, wexp2, masks2, blob4)


def _hgrn_backward(p3, d_o, states, e16, a2, lb_logits, wexp_t, masks2, dw16, rows):
    n_chunks = rows // CHUNK
    cpb = _tile(n_chunks, 13, mult=1)
    rb_rows = cpb * CHUNK
    n_rb = n_chunks // cpb
    lanes = cpb * HEAD_DIM
    exchange = _GradExchange(SEGS_MIX, with_blob=False)

    hps = BACKWARD_HEADS_PER_STEP
    n_hb = N_HEADS // hps
    width = hps * HEAD_DIM

    def body(q_ref, fz_ref, v_ref, do_ref, s_ref, e_ref, a2_ref, lbl_ref, wexpt_ref, mask_ref, dw_ref,
             dp_ref, dlb_ref, rxw_ref, *scratch):
        per_head, (send_sems, recv_sems) = scratch[:-2], scratch[-2:]
        step = pl.program_id(1)
        rb = n_rb - 1 - step

        @pl.when((pl.program_id(0) == 0) & (step == 0))
        def _():
            exchange.start(dw_ref, rxw_ref, None, None, send_sems, recv_sems)

        for j in range(hps):
            cols = pl.ds(j * HEAD_DIM, HEAD_DIM)
            one_head(step, rb, q_ref.at[0, :, cols], fz_ref.at[0, :, cols], v_ref.at[0, :, cols], do_ref.at[:, cols],
                     s_ref.at[j], e_ref.at[j, 0], a2_ref.at[:, cols], lbl_ref.at[:, cols], wexpt_ref, mask_ref,
                     dp_ref.at[:, :, cols], dlb_ref.at[:, cols], *[ref.at[j] for ref in per_head])

        @pl.when((pl.program_id(0) == n_hb - 1) & (step == n_rb - 1))
        def _():
            exchange.wait(dw_ref, rxw_ref, None, None, send_sems, recv_sems)

    def one_head(step, rb, q_ref, fz_ref, v_ref, do_ref, s_ref, e_ref, a2_ref, lbl_ref, wexpt_ref, mask_ref,
                 dp_ref, dlb_ref, dst_ref, g_ref, dsn_ref, q_s, kk_s, v_s, do_s, dq_s, dkk_s, dg_s, dx_s, da2_s):
        @pl.when(step == 0)
        def _():
            dst_ref[...] = jnp.zeros_like(dst_ref)
            dlb_ref[...] = jnp.zeros_like(dlb_ref)

        lb = _lower_bound(lbl_ref[...])
        row = rb * rb_rows + lax.broadcasted_iota(jnp.int32, (rb_rows, 1), 0)
        valid = row >= PAD_ROWS
        sg, sn = _sigmoid_pair(fz_ref[...])
        f = lb + (1.0 - lb) * sg
        g = jnp.where(valid, jnp.log(f), 0.0)
        kk_s[...] = jnp.where(valid, (1.0 - lb) * sn, 0.0)
        q_s[...] = jnp.where(valid, q_ref[...], 0.0)
        v_s[...] = jnp.where(valid, v_ref[...], 0.0).astype(BF16)
        do_s[...] = do_ref[...].astype(BF16)
        e_last_all = jnp.exp(jnp.concatenate(
            [jnp.sum(g[b * CHUNK:(b + 1) * CHUNK], axis=0, keepdims=True) for b in range(cpb)], axis=0))
        last_row = lax.broadcasted_iota(jnp.int32, (CHUNK, 1), 0) == CHUNK - 1
        zeros16 = jnp.zeros((CHUNK, HEAD_DIM), BF16)

        def factor(block, l0):
            return e_ref[block * CHUNK:(block + 1) * CHUNK, pl.ds(l0, HEAD_DIM)].astype(F32)

        def contribution(b, carry):
            r0 = pl.multiple_of(b * CHUNK, CHUNK)
            l0 = pl.multiple_of(b * HEAD_DIM, HEAD_DIM)
            qb16 = (q_s[pl.ds(r0, CHUNK), :] * factor(0, l0)).astype(BF16)
            g_ref[b] = _dot_tn(do_s[pl.ds(r0, CHUNK), :], qb16)
            return carry

        lax.fori_loop(0, cpb, contribution, 0, unroll=LOCAL_UNROLL)

        cur = dst_ref[...]
        for b in reversed(range(cpb)):
            dsn_ref[b] = cur
            cur = cur * e_last_all[b:b + 1, :] + g_ref[b]
        dst_ref[...] = cur

        def through_state(b, carry):
            r0 = pl.multiple_of(b * CHUNK, CHUNK)
            l0 = pl.multiple_of(b * HEAD_DIM, HEAD_DIM)
            v16 = v_s[pl.ds(r0, CHUNK), :]
            do16 = do_s[pl.ds(r0, CHUNK), :]
            st = s_ref[b]
            dsn = dsn_ref[b]
            dsn16 = dsn.astype(BF16)
            e_b, e_c = factor(0, l0), factor(1, l0)
            qb, kc = q_s[pl.ds(r0, CHUNK), :] * e_b, kk_s[pl.ds(r0, CHUNK), :] * e_c

            t = _dot_tn(a2_ref[pl.ds(r0, CHUNK), :], do16)
            dv = t[0:CHUNK] + t[CHUNK:2 * CHUNK] + _dot_nt(kc.astype(BF16), dsn16)
            dp_ref[2, pl.ds(r0, CHUNK), :] = dv.astype(BF16)
            da2_s[pl.ds(r0, CHUNK), :] = _dot_nt(do16, jnp.concatenate([v16, v16], axis=0))
            dqb = _dot(do16, st.astype(BF16))
            dkc = _dot(v16, dsn16)
            de = jnp.sum(dsn * st, axis=0, keepdims=True) * e_b[CHUNK - 1:CHUNK, :]
            dq_s[pl.ds(r0, CHUNK), :] = e_b * dqb
            dkk_s[pl.ds(r0, CHUNK), :] = e_c * dkc
            dx_s[0:CHUNK, pl.ds(l0, HEAD_DIM)] = (qb * dqb + jnp.where(last_row, de, 0.0)).astype(BF16)
            dx_s[CHUNK:2 * CHUNK, pl.ds(l0, HEAD_DIM)] = (kc * dkc).astype(BF16)
            return carry

        lax.fori_loop(0, cpb, through_state, 0, unroll=BACKWARD_UNROLL)

        def local(b, carry):
            r0 = pl.multiple_of(b * CHUNK, CHUNK)
            l0 = pl.multiple_of(b * HEAD_DIM, HEAD_DIM)
            q = q_s[pl.ds(r0, CHUNK), :]
            kk = kk_s[pl.ds(r0, CHUNK), :]
            da2 = da2_s[pl.ds(r0, CHUNK), :]
            dq = dq_s[pl.ds(r0, CHUNK), :]
            dkk = dkk_s[pl.ds(r0, CHUNK), :]

            def scaled(entry):
                if entry == 0:
                    return q, kk, None
                e_m = factor(1 + entry, l0)
                return q * e_m, kk * e_m, e_m

            for p, (ea, eb) in enumerate(LEVEL_PAIRS):
                dm = mask_ref[p] * da2
                dm_t = dm.T.astype(BF16)
                qa, ka, e_a = scaled(ea)
                if eb is None:
                    rhs_k = jnp.concatenate([jnp.concatenate([ka.astype(BF16), zeros16], axis=1),
                                             jnp.concatenate([zeros16, zeros16], axis=1)], axis=0)
                else:
                    qb_, kb_, e_bb = scaled(eb)
                    rhs_k = jnp.concatenate([jnp.concatenate([ka.astype(BF16), zeros16], axis=1),
                                             jnp.concatenate([zeros16, kb_.astype(BF16)], axis=1)], axis=0)
                dq2 = _dot(dm.astype(BF16), rhs_k)
                parts = [(ea, qa, ka, e_a, dq2[:, :HEAD_DIM], _dot(dm_t[0:CHUNK], qa.astype(BF16)))]
                if eb is not None:
                    parts.append((eb, qb_, kb_, e_bb, dq2[:, HEAD_DIM:],
                                  _dot(dm_t[CHUNK:2 * CHUNK], qb_.astype(BF16))))
                for entry, q_m, k_m, e_m, dq_m, dk_m in parts:
                    if entry == 0:
                        dq = dq + dq_m
                        dkk = dkk + dk_m
                    else:
                        dq = dq + e_m * dq_m
                        dkk = dkk + e_m * dk_m
                        dx_s[(1 + entry) * CHUNK:(2 + entry) * CHUNK, pl.ds(l0, HEAD_DIM)] = (
                            q_m * dq_m + k_m * dk_m).astype(BF16)
            dq_s[pl.ds(r0, CHUNK), :] = dq
            dkk_s[pl.ds(r0, CHUNK), :] = dkk
            return carry

        lax.fori_loop(0, cpb, local, 0, unroll=BACKWARD_UNROLL)

        dg_all = _dot(wexpt_ref[...], dx_s[...])
        for b in range(cpb):
            dg_s[b * CHUNK:(b + 1) * CHUNK, :] = dg_all[:, b * HEAD_DIM:(b + 1) * HEAD_DIM]
        t = jnp.where(valid, dg_s[...] / f - dkk_s[...], 0.0)
        dlb_ref[...] += jnp.sum(sn * t, axis=0, keepdims=True)
        dp_ref[0] = jnp.where(valid, dq_s[...], 0.0).astype(BF16)
        dp_ref[1] = ((1.0 - lb) * sg * sn * t).astype(BF16)

    head_block = lambda seg: pl.BlockSpec((1, rb_rows, width), lambda h, s: (seg, n_rb - 1 - s, h))
    row_block = pl.BlockSpec((rb_rows, width), lambda h, s: (n_rb - 1 - s, h))
    return pl.pallas_call(
        body, name="hgrn_backward",
        grid=(n_hb, n_rb),
        in_specs=[
            head_block(0), head_block(1), head_block(2),
            row_block,
            pl.BlockSpec((hps, cpb, HEAD_DIM, HEAD_DIM), lambda h, s: (h, n_rb - 1 - s, 0, 0)),
            pl.BlockSpec((hps, 1, N_EXP * CHUNK, lanes), lambda h, s: (h, n_rb - 1 - s, 0, 0)),
            row_block,
            pl.BlockSpec((2, width), lambda h, s: (0, h)),
            pl.BlockSpec((CHUNK, N_EXP * CHUNK), lambda h, s: (0, 0)),
            pl.BlockSpec((len(LEVEL_PAIRS), CHUNK, 2 * CHUNK), lambda h, s: (0, 0, 0)),
            ANY,
        ],
        out_specs=[
            pl.BlockSpec((3, rb_rows, width), lambda h, s: (0, n_rb - 1 - s, h)),
            pl.BlockSpec((1, width), lambda h, s: (0, h)),
            ANY,
        ],
        out_shape=[
            jax.ShapeDtypeStruct((3, rows, D_MODEL), BF16),
            jax.ShapeDtypeStruct((1, D_MODEL), F32),
            exchange.landing_w(),
        ],
        scratch_shapes=[
            pltpu.VMEM((hps, HEAD_DIM, HEAD_DIM), F32),
            pltpu.VMEM((hps, cpb, HEAD_DIM, HEAD_DIM), F32),
            pltpu.VMEM((hps, cpb, HEAD_DIM, HEAD_DIM), F32),
            pltpu.VMEM((hps, rb_rows, HEAD_DIM), F32),
            pltpu.VMEM((hps, rb_rows, HEAD_DIM), F32),
            pltpu.VMEM((hps, rb_rows, HEAD_DIM), BF16),
            pltpu.VMEM((hps, rb_rows, HEAD_DIM), BF16),
            pltpu.VMEM((hps, rb_rows, HEAD_DIM), F32),
            pltpu.VMEM((hps, rb_rows, HEAD_DIM), F32),
            pltpu.VMEM((hps, rb_rows, HEAD_DIM), F32),
            pltpu.VMEM((hps, N_EXP * CHUNK, lanes), BF16),
            pltpu.VMEM((hps, rb_rows, 2 * CHUNK), F32),
        ] + exchange.semaphores(),
        compiler_params=_params(("arbitrary", "arbitrary")),
    )(p3, p3, p3, d_o, states, e16, a2, lb_logits, wexp_t, masks2, dw16)


def _sigmoid(x):
    return 1.0 / (1.0 + jnp.exp(-x))


def _silu_and_grad(x):
    s = _sigmoid(x)
    return x * s, s * (1.0 + x * (1.0 - s))


def _window_sum(ext, width, forward_looking):
    n = ext.shape[0]
    s = ext
    step = 1
    while step < width:
        s = s + pltpu.roll(s, (n - step) if forward_looking else step, 0)
        step *= 2
    return s


def _mixers(o, p3, tokens, head, tgt, wdh, wdp, wout, poolw, hg_w, pool_scale, final_w, rows):
    tm = _tile(rows, 208)
    nt = rows // tm
    halo_blocks = tm // HALO
    n_grp = len(POOL_WINDOWS)
    q_rows = D_MODEL // N_CHIPS
    blob_rows = 3 * q_rows + n_grp * POOL_GDIM * POOL_GDIM // (N_CHIPS * D_MODEL)

    def body(o_ref, ghg_ref, u_ref, gpl_ref, mhg_ref, mpl_ref, uh_ref, z_ref, t_ref,
             wdh_ref, wdp_ref, wout_ref, pw_ref, hgw_ref, ps_ref, fw_ref, head_ref,
             do_ref, dz2_ref, dp_ref, blob_ref, dpw_ref, small_ref, carry_ref):
        step = pl.program_id(0)
        tile = nt - 1 - step

        def add_to_blob(piece, dw):
            for k in range(N_CHIPS):
                blob_ref[k, piece * q_rows:(piece + 1) * q_rows, :] += dw[k * q_rows:(k + 1) * q_rows]

        @pl.when(step == 0)
        def _():
            blob_ref[...] = jnp.zeros_like(blob_ref)
            dpw_ref[...] = jnp.zeros_like(dpw_ref)
            small_ref[...] = jnp.zeros_like(small_ref)
            carry_ref[...] = jnp.zeros_like(carry_ref)

        row = tile * tm + lax.broadcasted_iota(jnp.int32, (tm, 1), 0)
        real = row >= PAD_ROWS
        pos1 = jnp.maximum(row - PAD_ROWS + 1, 1).astype(F32)

        u = jnp.where(real, u_ref[0], 0.0)
        halo_row = tile * tm - HALO + lax.broadcasted_iota(jnp.int32, (HALO, 1), 0)
        uh = jnp.where(halo_row >= PAD_ROWS, uh_ref[0], 0.0)
        ext = jnp.concatenate([uh, u], axis=0)
        pooled, inv_cnt, mixed = [], [], []
        for g, w in enumerate(POOL_WINDOWS):
            cols = slice(g * POOL_GDIM, (g + 1) * POOL_GDIM)
            inv = 1.0 / jnp.minimum(pos1, float(w))
            ws = _window_sum(ext[:, cols], w, False)[HALO:]
            pg = (ws * inv - u[:, cols]).astype(BF16)
            pooled.append(pg)
            inv_cnt.append(inv)
            mixed.append(_dot(pg, pw_ref[g]))
        mixed = jnp.concatenate(mixed, axis=1)
        gpl = gpl_ref[0]
        sp, dsp = _silu_and_grad(gpl)
        ps = ps_ref[...]
        a_pool = (mixed * ps * sp).astype(BF16)
        y_pool = _dot(a_pool, wdp_ref[...])

        o = o_ref[...]
        o_hat, rstd_h = [], []
        for h in range(N_HEADS):
            oh = o[:, h * HEAD_DIM:(h + 1) * HEAD_DIM]
            r = lax.rsqrt(jnp.mean(oh * oh, axis=-1, keepdims=True) + EPS)
            rstd_h.append(r)
            o_hat.append(oh * r)
        o_hat = jnp.concatenate(o_hat, axis=1)
        hgw = hgw_ref[...]
        o_n = o_hat * hgw
        ghg = ghg_ref[0]
        sh, dsh = _silu_and_grad(ghg)
        a_hg = (o_n * sh).astype(BF16)
        y_hg = _dot(a_hg, wdh_ref[...])

        s_mh = _sigmoid(mhg_ref[0])
        s_mp = _sigmoid(mpl_ref[0])
        merged = (s_mh * y_hg + s_mp * y_pool).astype(BF16)
        z2 = _padded_tile(z_ref[...], head_ref[...], tile) + _dot(merged, wout_ref[...])
        rstd2 = lax.rsqrt(jnp.mean(z2 * z2, axis=-1, keepdims=True) + EPS)
        zh = z2 * rstd2
        fw = fw_ref[...]
        target = _padded_tile(t_ref[...], jnp.zeros((FIRST_TOKEN_ROW, D_MODEL), F32), tile)
        err = jnp.where(row >= FIRST_TOKEN_ROW, zh * fw - target, 0.0)
        small_ref[ROW_LOSS:ROW_LOSS + 1, :] += jnp.sum(err * err, axis=0, keepdims=True) * (0.5 / D_MODEL)
        dy = err * (1.0 / D_MODEL)

        small_ref[ROW_FINAL_W:ROW_FINAL_W + 1, :] += jnp.sum(dy * zh, axis=0, keepdims=True)
        uu = dy * fw
        dz2 = rstd2 * (uu - zh * jnp.mean(uu * zh, axis=-1, keepdims=True))
        dz2_ref[...] = dz2
        dz2_16 = dz2.astype(BF16)
        dmerged = _dot_nt(dz2_16, wout_ref[...])
        add_to_blob(2, _dot_tn(merged, dz2_16))
        dy_hg = (s_mh * dmerged).astype(BF16)
        dy_pool = (s_mp * dmerged).astype(BF16)
        dp_ref[3] = (dmerged * y_hg * s_mh * (1.0 - s_mh)).astype(BF16)
        dp_ref[4] = (dmerged * y_pool * s_mp * (1.0 - s_mp)).astype(BF16)

        da_hg = _dot_nt(dy_hg, wdh_ref[...])
        add_to_blob(0, _dot_tn(a_hg, dy_hg))
        dp_ref[0] = (da_hg * o_n * dsh).astype(BF16)
        do_n = da_hg * sh
        small_ref[ROW_HG_W:ROW_HG_W + 1, :] += jnp.sum(do_n * o_hat, axis=0, keepdims=True)
        d_hat = do_n * hgw
        for h in range(N_HEADS):
            cols = slice(h * HEAD_DIM, (h + 1) * HEAD_DIM)
            dh_, oh_ = d_hat[:, cols], o_hat[:, cols]
            do_ref[:, cols] = rstd_h[h] * (dh_ - oh_ * jnp.mean(dh_ * oh_, axis=-1, keepdims=True))

        da_pool = _dot_nt(dy_pool, wdp_ref[...])
        add_to_blob(1, _dot_tn(a_pool, dy_pool))
        small_ref[ROW_POOL_SCALE:ROW_POOL_SCALE + 1, :] += jnp.sum(da_pool * mixed * sp, axis=0, keepdims=True)
        dp_ref[2] = (da_pool * mixed * ps * dsp).astype(BF16)
        dmixed = (da_pool * ps * sp).astype(BF16)
        carry = carry_ref[...]
        du, new_carry = [], []
        for g, w in enumerate(POOL_WINDOWS):
            cols = slice(g * POOL_GDIM, (g + 1) * POOL_GDIM)
            dmg = dmixed[:, cols]
            dpooled = _dot_nt(dmg, pw_ref[g])
            dpw_ref[g] += _dot_tn(pooled[g], dmg)
            dps = dpooled * inv_cnt[g]
            ext_b = jnp.concatenate([dps, carry[:, cols]], axis=0)
            du.append(_window_sum(ext_b, w, True)[:tm] - dpooled)
            new_carry.append(dps[:HALO])
        dp_ref[1] = jnp.where(real, jnp.concatenate(du, axis=1), 0.0).astype(BF16)
        carry_ref[...] = jnp.concatenate(new_carry, axis=1)

    row_block = pl.BlockSpec((tm, D_MODEL), lambda s: (nt - 1 - s, 0))
    seg_block = lambda seg: pl.BlockSpec((1, tm, D_MODEL), lambda s: (seg, nt - 1 - s, 0))
    whole = pl.BlockSpec(memory_space=pltpu.VMEM)
    return pl.pallas_call(
        body, name="mixers",
        grid=(nt,),
        in_specs=[
            row_block, seg_block(3), seg_block(4), seg_block(5), seg_block(6), seg_block(7),
            pl.BlockSpec((1, HALO, D_MODEL),
                         lambda s: (4, jnp.maximum((nt - 1 - s) * halo_blocks - 1, 0), 0)),
            _token_window(tm, lambda s: nt - 1 - s), _token_window(tm, lambda s: nt - 1 - s),
            whole, whole, whole, whole, whole, whole, whole, whole,
        ],
        out_specs=[
            row_block, row_block,
            pl.BlockSpec((5, tm, D_MODEL), lambda s: (0, nt - 1 - s, 0)),
            whole, whole, whole,
        ],
        out_shape=[
            jax.ShapeDtypeStruct((rows, D_MODEL), F32),
            jax.ShapeDtypeStruct((rows, D_MODEL), F32),
            jax.ShapeDtypeStruct((5, rows, D_MODEL), BF16),
            jax.ShapeDtypeStruct((N_CHIPS, blob_rows, D_MODEL), F32),
            jax.ShapeDtypeStruct((n_grp, POOL_GDIM, POOL_GDIM), F32),
            jax.ShapeDtypeStruct((SMALL_ROWS, D_MODEL), F32),
        ],
        scratch_shapes=[pltpu.VMEM((HALO, D_MODEL), F32)],
        compiler_params=_params(("arbitrary",)),
    )(o, p3, p3, p3, p3, p3, p3, tokens, tgt, wdh, wdp, wout, poolw, hg_w, pool_scale, final_w, head)


def _seg_specs(tm, row_of, seg_of):
    def spec_a(*g):
        k = seg_of(*g)
        return (jnp.minimum(k, 2), jnp.where(k < 3, row_of(*g), 0), 0)

    def spec_b(*g):
        k = seg_of(*g)
        return (jnp.maximum(k - 3, 0), jnp.where(k >= 3, row_of(*g), 0), 0)

    return pl.BlockSpec((1, tm, D_MODEL), spec_a), pl.BlockSpec((1, tm, D_MODEL), spec_b)


def _in_proj_weight_grad(h, dp, rows, name, blob16=None):
    n_seg = dp.shape[0]
    tm = _tile(rows, 1040)
    nt = rows // tm
    half = D_MODEL // 2
    exchange = _GradExchange((), with_blob=True) if blob16 is not None else None

    def body(*refs):
        if exchange is None:
            (h_ref, dp_ref, part_ref, part16_ref, db_ref, acc_ref, bacc_ref, stage_ref, land_ref,
             send_sems, recv_sems) = refs
        else:
            (h_ref, dp_ref, blob_ref, part_ref, part16_ref, db_ref, rxb_ref, acc_ref, bacc_ref, stage_ref, land_ref,
             send_sems, recv_sems, blob_send, blob_recv) = refs
        k, i = pl.program_id(0), pl.program_id(1)
        x, y, c = lax.axis_index("x"), lax.axis_index("y"), lax.axis_index("c")

        if exchange is not None:
            @pl.when((k == 0) & (i == 0))
            def _():
                exchange.start(None, None, blob_ref, rxb_ref, blob_send, blob_recv)

        def to_sibling(seg):
            return pltpu.make_async_remote_copy(
                src_ref=stage_ref.at[seg], dst_ref=land_ref.at[seg], send_sem=send_sems.at[seg],
                recv_sem=recv_sems.at[seg], device_id=(x, y, 1 - c), device_id_type=MESH)

        @pl.when(i == 0)
        def _():
            acc_ref[...] = jnp.zeros_like(acc_ref)
            bacc_ref[...] = jnp.zeros_like(bacc_ref)

        dpt = dp_ref[0]
        acc_ref[...] += _dot_tn(h_ref[...], dpt)
        bacc_ref[...] += jnp.sum(dpt.astype(F32), axis=0, keepdims=True)

        @pl.when(i == nt - 1)
        def _():
            db_ref[0] = bacc_ref[...]
            part_ref[k] = acc_ref[pl.ds(pl.multiple_of(c * half, half), half), :]
            stage_ref[k] = acc_ref[pl.ds(pl.multiple_of((1 - c) * half, half), half), :].astype(BF16)
            to_sibling(k).start()

        @pl.when((k == n_seg - 1) & (i == nt - 1))
        def _():
            for seg in range(n_seg):
                to_sibling(seg).wait_recv()
                total = part_ref[seg] + land_ref[seg].astype(F32)
                part_ref[seg] = total
                part16_ref[seg] = total.astype(BF16)
            for seg in range(n_seg):
                to_sibling(seg).wait_send()
            if exchange is not None:
                exchange.wait(None, None, blob_ref, rxb_ref, blob_send, blob_recv)

    whole = pl.BlockSpec(memory_space=pltpu.VMEM)
    with_blob = exchange is not None
    return pl.pallas_call(
        body, name=name,
        grid=(n_seg, nt),
        in_specs=[pl.BlockSpec((tm, D_MODEL), lambda k, i: (i, 0)),
                  pl.BlockSpec((1, tm, D_MODEL), lambda k, i: (k, i, 0))] + [ANY] * with_blob,
        out_specs=[whole, whole, pl.BlockSpec((1, 1, D_MODEL), lambda k, i: (k, 0, 0))] + [ANY] * with_blob,
        out_shape=[
            jax.ShapeDtypeStruct((n_seg, half, D_MODEL), F32),
            jax.ShapeDtypeStruct((n_seg, half, D_MODEL), BF16),
            jax.ShapeDtypeStruct((n_seg, 1, D_MODEL), F32),
        ] + ([exchange.landing_blob(blob16)] if with_blob else []),
        scratch_shapes=[
            pltpu.VMEM((D_MODEL, D_MODEL), F32), pltpu.VMEM((1, D_MODEL), F32),
            pltpu.VMEM((n_seg, half, D_MODEL), BF16),
            pltpu.VMEM((n_seg, half, D_MODEL), BF16),
            pltpu.SemaphoreType.DMA((n_seg,)), pltpu.SemaphoreType.DMA((n_seg,)),
        ] + (exchange.semaphores() if with_blob else []),
        compiler_params=_params(("arbitrary", "arbitrary")),
    )(h, dp, *([blob16] if with_blob else []))


def _input_grad(dpa, dpb, w4, tokens, head, dz2, norm_w, dw16, rows):
    tm = _tile(rows, 1040)
    nt = rows // tm
    assert nt >= 2, rows
    exchange = _GradExchange(SEGS_REC, with_blob=False)

    def body(dpa_ref, dpb_ref, w_ref, z_ref, head_ref, dz2_ref, nw_ref, dw_ref, gx_ref, dmeta_ref, dnw_ref, rxw_ref,
             acc_ref, dz_buf, out_sem, send_sems, recv_sems):
        i, k = pl.program_id(0), pl.program_id(1)

        def first_tile_out():
            return pltpu.make_async_copy(dz_buf.at[pl.ds(FIRST_TOKEN_ROW, tm - FIRST_TOKEN_ROW), :],
                                         gx_ref.at[pl.ds(0, tm - FIRST_TOKEN_ROW), :], out_sem)

        def tile_out(tile):
            start = pl.multiple_of(tile * tm - FIRST_TOKEN_ROW, HALO)
            return pltpu.make_async_copy(dz_buf, gx_ref.at[pl.ds(start, tm), :], out_sem)

        @pl.when((i == 0) & (k == 0))
        def _():
            exchange.start(dw_ref, rxw_ref, None, None, send_sems, recv_sems)
            dnw_ref[...] = jnp.zeros_like(dnw_ref)

        @pl.when((i == nt - 1) & (k == N_SEG - 1))
        def _():
            exchange.wait(dw_ref, rxw_ref, None, None, send_sems, recv_sems)

        @pl.when(k == 0)
        def _():
            acc_ref[...] = jnp.zeros_like(acc_ref)

        @pl.when(k < 3)
        def _():
            acc_ref[...] += _dot_nt(dpa_ref[0], w_ref[0])

        @pl.when(k >= 3)
        def _():
            acc_ref[...] += _dot_nt(dpb_ref[0], w_ref[0])

        @pl.when(k == N_SEG - 1)
        def _():
            zt = _padded_tile(z_ref[...], head_ref[...], i)
            rstd = lax.rsqrt(jnp.mean(zt * zt, axis=-1, keepdims=True) + EPS)
            zh = zt * rstd
            dh = acc_ref[...]
            dnw_ref[...] += jnp.sum(dh * zh, axis=0, keepdims=True)
            uu = dh * nw_ref[...]
            dz = dz2_ref[...] + rstd * (uu - zh * jnp.mean(uu * zh, axis=-1, keepdims=True))

            @pl.when(i == 1)
            def _():
                first_tile_out().wait()

            @pl.when(i >= 2)
            def _():
                tile_out(i - 1).wait()

            dz_buf[...] = dz

            @pl.when(i == 0)
            def _():
                dmeta_ref[...] = dz[PAD_ROWS:FIRST_TOKEN_ROW]
                first_tile_out().start()

            @pl.when(i > 0)
            def _():
                tile_out(i).start()

            @pl.when(i == nt - 1)
            def _():
                tile_out(i).wait()

    spec_a, spec_b = _seg_specs(tm, lambda i, k: i, lambda i, k: k)
    last_only = pl.BlockSpec((tm, D_MODEL), lambda i, k: (jnp.where(k == N_SEG - 1, i, 0), 0))
    return pl.pallas_call(
        body, name="input_grad",
        grid=(nt, N_SEG),
        in_specs=[
            spec_a, spec_b,
            pl.BlockSpec((1, D_MODEL, D_MODEL), lambda i, k: (k // 2, 0, k % 2)),
            _token_window(tm, lambda i, k: jnp.where(k == N_SEG - 1, i, 0)),
            pl.BlockSpec((FIRST_TOKEN_ROW, D_MODEL), lambda i, k: (0, 0)),
            last_only,
            pl.BlockSpec((1, D_MODEL), lambda i, k: (0, 0)),
            ANY,
        ],
        out_specs=[
            ANY,
            pl.BlockSpec((N_META, D_MODEL), lambda i, k: (0, 0)),
            pl.BlockSpec((1, D_MODEL), lambda i, k: (0, 0)),
            ANY,
        ],
        out_shape=[
            jax.ShapeDtypeStruct((rows - FIRST_TOKEN_ROW, D_MODEL), F32),
            jax.ShapeDtypeStruct((N_META, D_MODEL), F32),
            jax.ShapeDtypeStruct((1, D_MODEL), F32),
            exchange.landing_w(),
        ],
        scratch_shapes=[pltpu.VMEM((tm, D_MODEL), F32), pltpu.VMEM((tm, D_MODEL), F32),
                        pltpu.SemaphoreType.DMA] + exchange.semaphores(),
        compiler_params=_params(("arbitrary", "arbitrary")),
    )(dpa, dpb, w4, tokens, head, dz2, norm_w, dw16)


def _local_step(tokens, m4, tgt, w4, blob4, seg_order, norm_w, b_in, lb_logits, hg_w, pool_scale, final_w):
    rows = FIRST_TOKEN_ROW + tokens.shape[0]
    q = D_MODEL // N_CHIPS
    n_grp = len(POOL_WINDOWS)
    pg = POOL_GDIM // N_CHIPS

    wexp2 = jnp.asarray(np.tile(_exponent_matrix(), (1, 2)), BF16)
    wexp_t = jnp.asarray(_exponent_matrix().T, BF16)
    masks2 = jnp.asarray(_paired_masks(), F32)

    h, p3, w4, head, _ = _in_proj(tokens, m4, norm_w, w4, b_in, seg_order, rows)
    o, states, e16, a2, blob4 = _hgrn_forward(p3, lb_logits, wexp2, masks2, blob4, rows)
    wdh = blob4[:, 0:q].reshape(D_MODEL, D_MODEL)
    wdp = blob4[:, q:2 * q].reshape(D_MODEL, D_MODEL)
    wout = blob4[:, 2 * q:3 * q].reshape(D_MODEL, D_MODEL)
    poolw = blob4[:, 3 * q:].reshape(N_CHIPS, n_grp, pg, POOL_GDIM).transpose(1, 0, 2, 3)
    poolw = poolw.reshape(n_grp, POOL_GDIM, POOL_GDIM)
    d_o, dz2, dpb, dblob4, dpw, small = _mixers(
        o, p3, tokens, head, tgt, wdh, wdp, wout, poolw, hg_w, pool_scale, final_w, rows)
    dpw4 = dpw.reshape(n_grp, N_CHIPS, pg, POOL_GDIM).transpose(1, 0, 2, 3)
    dpw4 = dpw4.reshape(N_CHIPS, n_grp * pg * POOL_GDIM // D_MODEL, D_MODEL)
    dblob4 = dblob4.at[:, 3 * q:, :].set(dpw4)

    dw_mix, dw_mix16, db_mix, rx_blob = _in_proj_weight_grad(
        h, dpb, rows, "in_proj_weight_grad_mix", dblob4.astype(BF16))
    dpa, dlb, rxw_mix = _hgrn_backward(p3, d_o, states, e16, a2, lb_logits, wexp_t, masks2, dw_mix16, rows)
    dw_rec, dw_rec16, db_rec = _in_proj_weight_grad(h, dpa, rows, "in_proj_weight_grad_rec")
    d_tokens, d_meta, dnw, rxw_rec = _input_grad(dpa, dpb, w4, tokens, head, dz2, norm_w, dw_rec16, rows)

    small = jnp.concatenate([
        small[ROW_LOSS:ROW_LOSS + 1],
        d_meta,
        dnw,
        db_rec.reshape(len(SEGS_REC), D_MODEL), db_mix.reshape(len(SEGS_MIX), D_MODEL),
        dlb, jnp.zeros_like(dlb),
        small[ROW_HG_W:ROW_HG_W + 1], small[ROW_POOL_SCALE:ROW_POOL_SCALE + 1],
        small[ROW_FINAL_W:ROW_FINAL_W + 1],
        jnp.zeros((SMALL_ROWS - ROW_FINAL_W - 1, D_MODEL), F32),
    ], axis=0)
    return d_tokens, (dw_rec, dw_mix, rxw_rec, rxw_mix), (dblob4, rx_blob), small


ANY = pl.BlockSpec(memory_space=pl.ANY)
MESH = pl.DeviceIdType.MESH


def _place():
    x, y, c = lax.axis_index("x"), lax.axis_index("y"), lax.axis_index("c")
    chips = [(1 - x, y), (x, 1 - y), (1 - x, 1 - y)]
    return x, y, c, chips


class _ShardGather:
    def __init__(self, rows):
        self.half = rows // 2

    def semaphores(self):
        return [pltpu.SemaphoreType.DMA((6,)), pltpu.SemaphoreType.DMA((6,))]

    def _copy(self, k, slot, to, send_sems, recv_sems):
        return pltpu.make_async_remote_copy(src_ref=slot, dst_ref=slot, send_sem=send_sems.at[k],
                                            recv_sem=recv_sems.at[k], device_id=to, device_id_type=MESH)

    def _half(self, ref4, chip, which):
        return ref4.at[chip, pl.ds(which * self.half, self.half), :]

    def start(self, ref4, send_sems, recv_sems, which=(0, 1, 2)):
        x, y, c, chips = _place()
        for j in which:
            cx, cy = chips[j]
            self._copy(j, self._half(ref4, 2 * x + y, c), (cx, cy, c), send_sems, recv_sems).start()

    def start_diagonal_after_neighbours(self, ref4, send_sems, recv_sems):
        x, y, c, chips = _place()
        for j in (0, 1):
            cx, cy = chips[j]
            self._copy(j, self._half(ref4, 2 * x + y, c), (cx, cy, c), send_sems, recv_sems).wait_send()
        self.start(ref4, send_sems, recv_sems, which=(2,))

    def pass_on(self, j, ref4, send_sems, recv_sems):
        x, y, c, chips = _place()
        cx, cy = chips[j]
        landed = self._half(ref4, 2 * cx + cy, c)
        self._copy(j, landed, (cx, cy, c), send_sems, recv_sems).wait_recv()
        self._copy(3 + j, landed, (x, y, 1 - c), send_sems, recv_sems).start()

    def await_sibling(self, j, ref4, send_sems, recv_sems):
        x, y, c, chips = _place()
        cx, cy = chips[j]
        self._copy(3 + j, self._half(ref4, 2 * cx + cy, 1 - c), (x, y, 1 - c), send_sems, recv_sems).wait_recv()

    def finish(self, ref4, send_sems, recv_sems, which=(0, 1, 2)):
        x, y, c, chips = _place()
        for j, (cx, cy) in enumerate(chips):
            if j in which:
                self._copy(j, self._half(ref4, 2 * x + y, c), (cx, cy, c), send_sems, recv_sems).wait_send()
            self._copy(3 + j, self._half(ref4, 2 * cx + cy, c), (x, y, 1 - c), send_sems, recv_sems).wait_send()


class _GradExchange:
    def __init__(self, segs, with_blob):
        self.segs = tuple(segs)
        self.with_blob = with_blob

    def landing_w(self):
        return jax.ShapeDtypeStruct((N_CHIPS, 2, D_MODEL // 2, D_MODEL), BF16)

    def landing_blob(self, blob16):
        return jax.ShapeDtypeStruct((N_DEV, blob16.shape[1] // 2, D_MODEL), BF16)

    def semaphores(self):
        n_send = len(self.segs) + (2 * N_CHIPS if self.with_blob else 0)
        n_recv = 2 * N_CHIPS + (N_DEV if self.with_blob else 0)
        return [pltpu.SemaphoreType.DMA((n_send,)), pltpu.SemaphoreType.DMA((n_recv,))]

    def _copies(self, dw_ref, rxw_ref, blob_ref, rxb_ref, send_sems, recv_sems):
        x, y, c = lax.axis_index("x"), lax.axis_index("y"), lax.axis_index("c")
        chip = 2 * x + y

        def relation(kx, ky, h):
            return (x ^ kx) * 4 + (y ^ ky) * 2 + (c ^ h)

        def copy(src, dst, send_k, recv_k, to):
            return functools.partial(pltpu.make_async_remote_copy, src_ref=src, dst_ref=dst,
                                     send_sem=send_sems.at[send_k], recv_sem=recv_sems.at[recv_k],
                                     device_id=to, device_id_type=MESH)

        sends, recvs = [], []
        for i, s in enumerate(self.segs):
            kx, ky = (s // 2) >> 1, (s // 2) & 1
            r = (x ^ kx) * 2 + (y ^ ky)
            sends.append((r != 0, copy(dw_ref.at[i], rxw_ref.at[r, s % 2], i, 2 * r + s % 2, (kx, ky, c))))
        for j in range(2):
            mine = [s // 2 for s in self.segs if s % 2 == j]
            if mine:
                cond = functools.reduce(lambda a, b: a | b, [chip == k for k in mine])
                for r in range(1, N_CHIPS):
                    slot = rxw_ref.at[r, j]
                    recvs.append((cond, copy(slot, slot, 0, 2 * r + j, (x, y, c))))
        if self.with_blob:
            hb = blob_ref.shape[1] // 2
            first_send, first_recv = len(self.segs), 2 * N_CHIPS
            for k in range(N_CHIPS):
                for h in range(2):
                    r = relation(k >> 1, k & 1, h)
                    sends.append((r != 0, copy(blob_ref.at[k, pl.ds(h * hb, hb), :], rxb_ref.at[r],
                                               first_send + 2 * k + h, first_recv + r, (k >> 1, k & 1, h))))
            for r in range(1, N_DEV):
                slot = rxb_ref.at[r]
                recvs.append((None, copy(slot, slot, 0, first_recv + r, (x, y, c))))
        return sends, recvs

    def start(self, *refs):
        sends, _ = self._copies(*refs)
        for cond, make in sends:
            pl.when(cond)(lambda make=make: make().start())

    def wait(self, *refs):
        sends, recvs = self._copies(*refs)
        for cond, make in sends:
            pl.when(cond)(lambda make=make: make().wait_send())
        for cond, make in recvs:
            if cond is None:
                make().wait_recv()
            else:
                pl.when(cond)(lambda make=make: make().wait_recv())


def _sum_landed(own, rx_ref):
    total = own
    for r in range(1, rx_ref.shape[0]):
        total = total + rx_ref[r, 0].astype(F32)
    return total


def _finish_w(dw_rec, dw_mix, rx_rec, rx_mix, place_arr):
    half = D_MODEL // 2
    tm = _tile(half, 256)
    n_rec = len(SEGS_REC)

    def body(place_ref, own_rec_ref, own_mix_ref, rx_rec_ref, rx_mix_ref, out_ref):
        seg = 2 * place_ref[0] + pl.program_id(0)

        @pl.when(seg < n_rec)
        def _():
            out_ref[0] = _sum_landed(own_rec_ref[0], rx_rec_ref)

        @pl.when(seg >= n_rec)
        def _():
            out_ref[0] = _sum_landed(own_mix_ref[0], rx_mix_ref)

    def own_spec(first, count):
        def index(j, i, place_ref):
            seg = 2 * place_ref[0] + j
            return (jnp.clip(seg - first, 0, count - 1), i, 0)
        return pl.BlockSpec((1, tm, D_MODEL), index)

    rx_spec = pl.BlockSpec((N_CHIPS, 1, tm, D_MODEL), lambda j, i, place_ref: (0, j, i, 0))
    return pl.pallas_call(
        body, name="finish_w",
        grid_spec=pltpu.PrefetchScalarGridSpec(
            num_scalar_prefetch=1, grid=(2, half // tm),
            in_specs=[own_spec(0, n_rec), own_spec(n_rec, len(SEGS_MIX)), rx_spec, rx_spec],
            out_specs=pl.BlockSpec((1, tm, D_MODEL), lambda j, i, place_ref: (place_ref[1], i, j))),
        out_shape=jax.ShapeDtypeStruct((2, half, 2 * D_MODEL), F32),
        compiler_params=_params(("arbitrary", "arbitrary")),
    )(place_arr, dw_rec, dw_mix, rx_rec, rx_mix)


def _finish_blob(dblob4, rx_blob, place_arr):
    n, rows, cols = rx_blob.shape
    tm = _tile(rows, 256)

    def body(place_ref, own_ref, rx_ref, out_ref):
        out_ref[0] = _sum_landed(own_ref[0, 0], rx_ref)

    return pl.pallas_call(
        body, name="finish_blob",
        grid_spec=pltpu.PrefetchScalarGridSpec(
            num_scalar_prefetch=1, grid=(rows // tm,),
            in_specs=[pl.BlockSpec((1, 1, tm, cols), lambda i, place_ref: (place_ref[0], place_ref[1], i, 0)),
                      pl.BlockSpec((n, 1, tm, cols), lambda i, place_ref: (0, 0, i, 0))],
            out_specs=pl.BlockSpec((1, tm, cols), lambda i, place_ref: (place_ref[1], i, 0))),
        out_shape=jax.ShapeDtypeStruct((2, rows, cols), F32),
        compiler_params=_params(("arbitrary",)),
    )(place_arr, dblob4.reshape(N_CHIPS, 2, rows, cols), rx_blob.reshape(n, 1, rows, cols))


def _share_finished(fw2, fb2, small):
    def body(w_in_ref, b_in_ref, small_ref, w_ref, b_ref, s_ref, bounce, local_sem, send_sems, recv_sems):
        x, y, c, _ = _place()
        sibling = (x, y, 1 - c)

        def copy(k, src, dst, to):
            return pltpu.make_async_remote_copy(src_ref=src, dst_ref=dst, send_sem=send_sems.at[k],
                                                recv_sem=recv_sems.at[k], device_id=to, device_id_type=MESH)

        sends = [copy(0, w_ref.at[c], w_ref.at[c], sibling), copy(1, b_ref.at[c], b_ref.at[c], sibling)]
        for r in range(1, N_DEV):
            peer = (x ^ ((r >> 2) & 1), y ^ ((r >> 1) & 1), c ^ (r & 1))
            sends.append(copy(1 + r, small_ref, s_ref.at[r], peer))
        for cp in sends:
            cp.start()
        for src, dst in ((small_ref, bounce), (bounce, s_ref.at[0])):
            own = pltpu.make_async_copy(src, dst, local_sem)
            own.start()
            own.wait()
        landed = [w_ref.at[1 - c], b_ref.at[1 - c]] + [s_ref.at[r] for r in range(1, N_DEV)]
        for k, slot in enumerate(landed):
            copy(k, slot, slot, (x, y, c)).wait_recv()
        for cp in sends:
            cp.wait_send()

    same = lambda a: jax.ShapeDtypeStruct(a.shape, a.dtype)
    n_sem = 2 + N_DEV - 1
    return pl.pallas_call(
        body, name="share_finished",
        in_specs=[ANY, ANY, ANY], out_specs=[ANY, ANY, ANY],
        out_shape=[same(fw2), same(fb2), jax.ShapeDtypeStruct((N_DEV,) + small.shape, F32)],
        input_output_aliases={0: 0, 1: 1},
        scratch_shapes=[pltpu.VMEM(small.shape, F32), pltpu.SemaphoreType.DMA,
                        pltpu.SemaphoreType.DMA((n_sem,)), pltpu.SemaphoreType.DMA((n_sem,))],
    )(fw2, fb2, small)


def _sum_small(slots, lb_logits, me_arr):
    def body(me_ref, slots_ref, lbl_ref, out_ref):
        me = me_ref[0]
        total = slots_ref[me]
        for d in range(1, N_DEV):
            total = total + slots_ref[d ^ me]
        out_ref[...] = total
        out_ref[ROW_LOSS:ROW_LOSS + 1, :] = jnp.broadcast_to(
            jnp.sum(total[ROW_LOSS:ROW_LOSS + 1, :], axis=-1, keepdims=True), (1, D_MODEL))
        lb = _lower_bound(lbl_ref[...])
        g0 = total[ROW_LB:ROW_LB + 1, :] * lb * (1.0 - lb)
        out_ref[ROW_LB:ROW_LB + 1, :] = g0
        out_ref[ROW_LB + 1:ROW_LB + 2, :] = -g0

    return pl.pallas_call(
        body, name="sum_small",
        grid_spec=pltpu.PrefetchScalarGridSpec(
            num_scalar_prefetch=1, grid=(1,),
            in_specs=[pl.BlockSpec((N_DEV, SMALL_ROWS, D_MODEL), lambda i, me_ref: (0, 0, 0)),
                      pl.BlockSpec((2, D_MODEL), lambda i, me_ref: (0, 0))],
            out_specs=pl.BlockSpec((SMALL_ROWS, D_MODEL), lambda i, me_ref: (0, 0))),
        out_shape=jax.ShapeDtypeStruct((SMALL_ROWS, D_MODEL), F32),
        compiler_params=_params(("arbitrary",)),
    )(me_arr, slots, lb_logits)


def _adamw_step(w, g, m, v):
    c1 = 1.0 / (1.0 - ADAM_B1 ** ADAM_STEP)
    c2 = 1.0 / (1.0 - ADAM_B2 ** ADAM_STEP)
    nm = ADAM_B1 * m + (1.0 - ADAM_B1) * g
    nv = ADAM_B2 * v + (1.0 - ADAM_B2) * (g * g)
    return -ADAM_LR * ((nm * c1) / (jnp.sqrt(nv * c2) + ADAM_EPS) + ADAM_WD * w), nm, nv


SMALL_PARAMS = (("norm_w", ROW_NORM_W, 1), ("b_in", ROW_B_IN, N_SEG), ("lb_logits", ROW_LB, 2),
                ("hg_norm_w", ROW_HG_W, 1), ("pool_scale", ROW_POOL_SCALE, 1), ("final_norm_w", ROW_FINAL_W, 1))


def _update_small(tot, triples):
    n = len(SMALL_PARAMS)

    def body(tot_ref, *refs):
        ins, outs = refs[:3 * n], refs[3 * n:]
        for p, (name, row, n_rows) in enumerate(SMALL_PARAMS):
            w_ref, m_ref, v_ref = ins[3 * p:3 * p + 3]
            g_ref, d_ref, nm_ref, nv_ref = outs[4 * p:4 * p + 4]
            if w_ref.shape[0] == n_rows:
                pieces = [(slice(None), slice(None), tot_ref[row:row + n_rows, :])]
            else:
                pieces = [(slice(None), slice(k * D_MODEL, (k + 1) * D_MODEL), tot_ref[row + k:row + k + 1, :])
                          for k in range(n_rows)]
            for rows_, cols_, g in pieces:
                d, nm, nv = _adamw_step(w_ref[rows_, cols_], g, m_ref[rows_, cols_], v_ref[rows_, cols_])
                g_ref[rows_, cols_] = g
                d_ref[rows_, cols_] = d
                nm_ref[rows_, cols_] = nm
                nv_ref[rows_, cols_] = nv

    whole = pl.BlockSpec(memory_space=pltpu.VMEM)
    flat = [a for t in triples for a in t]
    out_shape = [jax.ShapeDtypeStruct(t[0].shape, F32) for t in triples for _ in range(4)]
    outs = pl.pallas_call(
        body, name="update_small",
        in_specs=[whole] * (1 + len(flat)), out_specs=[whole] * len(out_shape), out_shape=out_shape,
        compiler_params=_params(),
    )(tot, *flat)
    return [tuple(outs[4 * p:4 * p + 4]) for p in range(n)]


def _update_blob(g_blob, triples):
    q_rows = triples[0][0].shape[0]
    pool_rows = triples[3][0].shape[0]
    steps = q_rows // pool_rows

    def body(*refs):
        ins, outs = refs[:16], refs[16:]
        for p in range(4):
            g_ref, (w_ref, m_ref, v_ref) = ins[p], ins[4 + 3 * p:7 + 3 * p]
            go_ref, d_ref, nm_ref, nv_ref = outs[4 * p:4 * p + 4]

            def update():
                g = g_ref[...]
                go_ref[...] = g
                d_ref[...], nm_ref[...], nv_ref[...] = _adamw_step(w_ref[...], g, m_ref[...], v_ref[...])

            if p < 3:
                update()
            else:
                pl.when(pl.program_id(0) == 0)(update)

    blk = pl.BlockSpec((pool_rows, D_MODEL), lambda i: (i, 0))
    once = pl.BlockSpec((pool_rows, D_MODEL), lambda i: (0, 0))
    g_specs = [pl.BlockSpec((pool_rows, D_MODEL), lambda i, p=p: (steps * p + i, 0)) for p in range(3)]
    g_specs.append(pl.BlockSpec((pool_rows, D_MODEL), lambda i: (3 * steps, 0)))
    piece_specs = [blk] * 9 + [once] * 3
    out_specs = [blk] * 12 + [once] * 4
    out_shape = [jax.ShapeDtypeStruct(t[0].shape, F32) for t in triples for _ in range(4)]
    outs = pl.pallas_call(
        body, name="update_blob",
        grid=(steps,), in_specs=g_specs + piece_specs, out_specs=out_specs, out_shape=out_shape,
        compiler_params=_params(("arbitrary",)),
    )(g_blob, g_blob, g_blob, g_blob, *[a for t in triples for a in t])
    return [tuple(outs[4 * p:4 * p + 4]) for p in range(4)]


def _adamw(w, g, m, v):
    rows, cols = w.shape
    tm = _tile(rows, 256, mult=8) if rows % 8 == 0 else rows

    def body(w_ref, g_ref, m_ref, v_ref, d_ref, nm_ref, nv_ref):
        d_ref[...], nm_ref[...], nv_ref[...] = _adamw_step(w_ref[...], g_ref[...], m_ref[...], v_ref[...])

    blk = pl.BlockSpec((tm, cols), lambda i: (i, 0))
    sds = jax.ShapeDtypeStruct((rows, cols), F32)
    return pl.pallas_call(
        body, name="adamw",
        grid=(rows // tm,), in_specs=[blk] * 4, out_specs=[blk] * 3, out_shape=[sds] * 3,
        compiler_params=_params(("arbitrary",)),
    )(w, g, m, v)


def kernel(x, meta_tokens, norm_w, w_in, b_in, lb_logits, hg_norm_w, pool_w, pool_scale, w_down_hg, w_down_pool, w_out, final_norm_w, loss_target, m_meta_tokens, m_norm_w, m_w_in, m_b_in, m_lb_logits, m_hg_norm_w, m_pool_w, m_pool_scale, m_w_down_hg, m_w_down_pool, m_w_out, m_final_norm_w, v_meta_tokens, v_norm_w, v_w_in, v_b_in, v_lb_logits, v_hg_norm_w, v_pool_w, v_pool_scale, v_w_down_hg, v_w_down_pool, v_w_out, v_final_norm_w):
    seq = x.shape[1]
    xi, yi, ci = lax.axis_index("x"), lax.axis_index("y"), lax.axis_index("c")
    chip = 2 * xi + yi
    place_arr = jnp.stack([chip, ci]).astype(jnp.int32)
    me_arr = jnp.reshape(4 * xi + 2 * yi + ci, (1,)).astype(jnp.int32)
    q = D_MODEL // N_CHIPS

    def blob_of(wdh, wdp, wo, pw):
        return jnp.concatenate([wdh[0], wdp[0], wo[0], pw[0].reshape(-1, D_MODEL)], axis=0)

    def in_every_slot(a):
        return jnp.broadcast_to(a[None], (N_CHIPS,) + a.shape)

    m4 = in_every_slot(meta_tokens)
    w4 = in_every_slot(w_in[0].astype(BF16))
    blob4 = in_every_slot(blob_of(w_down_hg, w_down_pool, w_out, pool_w).astype(BF16))
    seg_order = jnp.stack([2 * (chip ^ rel) + t for rel in (0, 2, 1, 3) for t in (0, 1)]).astype(jnp.int32)

    fw2 = final_norm_w.reshape(1, D_MODEL)
    d_tokens, w_parts, blob_parts, small = _local_step(
        x[0], m4, loss_target[0], w4, blob4, seg_order, norm_w, b_in, lb_logits, hg_norm_w, pool_scale, fw2)
    grad_x = d_tokens[None]

    fin_w = _finish_w(*w_parts, place_arr)
    fin_b = _finish_blob(*blob_parts, place_arr)
    gw2, gb2, slots = _share_finished(fin_w, fin_b, small)
    tot = _sum_small(slots, lb_logits, me_arr)
    g_w_in = gw2.reshape(D_MODEL, 2 * D_MODEL)
    g_blob = gb2.reshape(-1, D_MODEL)

    d_win, nm_win, nv_win = _adamw(w_in[0], g_w_in, m_w_in[0], v_w_in[0])
    pool_rows = lambda a: a[0].reshape(-1, D_MODEL)
    blob_results = _update_blob(g_blob, [
        (w_down_hg[0], m_w_down_hg[0], v_w_down_hg[0]), (w_down_pool[0], m_w_down_pool[0], v_w_down_pool[0]),
        (w_out[0], m_w_out[0], v_w_out[0]), (pool_rows(pool_w), pool_rows(m_pool_w), pool_rows(v_pool_w))])
    g_meta = lax.dynamic_slice_in_dim(tot[ROW_META:ROW_META + N_META], chip * q, q, axis=1)
    d_meta, nm_meta, nv_meta = _adamw(meta_tokens, g_meta, m_meta_tokens, v_meta_tokens)

    as_row = lambda a: a.reshape(1, D_MODEL)
    small_results = _update_small(tot, [
        (norm_w, m_norm_w, v_norm_w), (b_in, m_b_in, v_b_in), (lb_logits, m_lb_logits, v_lb_logits),
        (hg_norm_w, m_hg_norm_w, v_hg_norm_w), (pool_scale, m_pool_scale, v_pool_scale),
        (as_row(final_norm_w), as_row(m_final_norm_w), as_row(v_final_norm_w))])

    def leaves(kind, meta_part, win_part):
        nw, bi, lbl, hg, ps, fw = [r[kind] for r in small_results]
        wdh, wdp, wo, pw = [r[kind] for r in blob_results]
        return [meta_part, nw, win_part[None], bi, lbl, hg, pw.reshape(pool_w.shape), ps,
                wdh[None], wdp[None], wo[None], fw.reshape(D_MODEL)]

    loss = tot[ROW_LOSS, 0]
    return (loss, grad_x,
            *leaves(0, g_meta, g_w_in),
            *leaves(1, d_meta, d_win),
            *leaves(2, nm_meta, nm_win),
            *leaves(3, nv_meta, nv_win))
```

```python
import functools

import numpy as np
import jax
import jax.numpy as jnp
from jax import lax
from jax.experimental import pallas as pl
from jax.experimental.pallas import tpu as pltpu

F32 = jnp.float32
BF16 = jnp.bfloat16

D_MODEL = 1024
N_SEG = 8
N_HEADS = 8
HEAD_DIM = 128
CHUNK = 64
N_META = 16
PAD_ROWS = CHUNK - N_META
FIRST_TOKEN_ROW = CHUNK
LEVELS = (32, 16, 8, 4, 2, 1)
N_EXP = 2 + len(LEVELS)
POOL_WINDOWS = (2, 4, 8, 16)
POOL_GDIM = D_MODEL // len(POOL_WINDOWS)
HALO = 16
FORWARD_HEADS_PER_STEP = 4
BACKWARD_HEADS_PER_STEP = 2
LOCAL_UNROLL = 13
BACKWARD_UNROLL = 13
EPS = 1e-6
N_CHIPS = 4
N_DEV = 8
BLOB_RELATIONS_DIRECT = (0, 1, 2)
BLOB_RELATIONS_DIAGONAL = (3,)
SEGS_REC = (0, 1, 2)
SEGS_MIX = (3, 4, 5, 6, 7)

ADAM_LR = 0.001
ADAM_B1 = 0.9
ADAM_B2 = 0.999
ADAM_EPS = 1e-08
ADAM_WD = 0.01
ADAM_STEP = 10

VMEM_LIMIT_BYTES = 56 * 1024 * 1024

ROW_LOSS = 0
ROW_META = 1
ROW_NORM_W = ROW_META + N_META
ROW_B_IN = ROW_NORM_W + 1
ROW_LB = ROW_B_IN + N_SEG
ROW_HG_W = ROW_LB + 2
ROW_POOL_SCALE = ROW_HG_W + 1
ROW_FINAL_W = ROW_POOL_SCALE + 1
SMALL_ROWS = 32


def _tile(total, cap, mult=16):
    best = None
    for t in range(mult, min(total, cap) + 1, mult):
        if total % t == 0:
            best = t
    assert best is not None, (total, cap, mult)
    return best


def _token_window(tm, tile_of):
    def index(*grid):
        return (pl.multiple_of(jnp.maximum(tile_of(*grid) * tm - FIRST_TOKEN_ROW, 0), HALO), 0)
    return pl.BlockSpec((pl.Element(tm), pl.Element(D_MODEL)), index)


def _padded_tile(window, head, tile):
    first = jnp.concatenate([head, pltpu.roll(window, FIRST_TOKEN_ROW, 0)[FIRST_TOKEN_ROW:]], axis=0)
    return jnp.where(tile == 0, first, window)


def _params(sem=None):
    return pltpu.CompilerParams(dimension_semantics=sem, vmem_limit_bytes=VMEM_LIMIT_BYTES)


def _dot(a, b):
    return jnp.dot(a, b, preferred_element_type=F32)


def _dot_nt(a, b):
    return lax.dot_general(a, b, (((1,), (1,)), ((), ())), preferred_element_type=F32)


def _dot_tn(a, b):
    return lax.dot_general(a, b, (((0,), (0,)), ((), ())), preferred_element_type=F32)


def _sigmoid_pair(x):
    t = jnp.exp(-jnp.abs(x))
    r = 1.0 / (1.0 + t)
    pos = x >= 0
    return jnp.where(pos, r, t * r), jnp.where(pos, t * r, r)


def _exponent_matrix():
    t = np.arange(CHUNK)[:, None]
    j = np.arange(CHUNK)[None, :]
    blocks = [j <= t, j > t]
    for m in LEVELS:
        rho = (t // (2 * m)) * (2 * m) + m
        upper = (t >= rho) & (j > rho) & (j <= t)
        lower = (t < rho) & (j > t) & (j <= rho)
        blocks.append(upper | lower)
    return np.concatenate(blocks, axis=0).astype(np.float32)


def _pair_masks():
    t = np.arange(CHUNK)[:, None]
    s = np.arange(CHUNK)[None, :]
    masks = [t == s]
    for m in LEVELS:
        same = (t // (2 * m)) == (s // (2 * m))
        masks.append(same & ((t % (2 * m)) >= m) & ((s % (2 * m)) < m))
    return np.stack(masks).astype(np.float32)


LEVEL_PAIRS = ((0, 1), (2, 3), (4, 5), (6, None))


def _paired_masks():
    m = _pair_masks()
    zero = np.zeros_like(m[0])
    return np.stack([np.concatenate([m[a], zero if b is None else m[b]], axis=1) for a, b in LEVEL_PAIRS])


def _lower_bound(lbl):
    return 1.0 / (1.0 + jnp.exp(lbl[1:2, :] - lbl[0:1, :]))


def _in_proj(tokens, m4, norm_w, w4, b_in, seg_order, rows):
    tm = _tile(rows, 1040)
    nt = rows // tm
    gather = _ShardGather(w4.shape[1])

    def body(order_ref, z_ref, nw_ref, b_ref, w_in_ref, m_in_ref, h_ref, p_ref, w4_ref, head_ref, m4_ref,
             h_all, w_buf, w_sem, send_sems, recv_sems, meta_send, meta_recv, meta_sem):
        kk, i = pl.program_id(0), pl.program_id(1)

        @pl.when((kk == 0) & (i == 0))
        def _():
            x, y, c, chips = _place()

            def meta_copy(j, chip, to):
                return pltpu.make_async_remote_copy(
                    src_ref=m4_ref.at[chip], dst_ref=m4_ref.at[chip], send_sem=meta_send.at[j],
                    recv_sem=meta_recv.at[j], device_id=to, device_id_type=MESH)

            sends = [meta_copy(j, 2 * x + y, (cx, cy, c)) for j, (cx, cy) in enumerate(chips)]
            for cp in sends:
                cp.start()
            gather.start(w4_ref, send_sems, recv_sems, which=(0, 1))
            for j, (cx, cy) in enumerate(chips):
                meta_copy(j, 2 * cx + cy, (x, y, c)).wait_recv()
            for cp in sends:
                cp.wait_send()
            head_ref[0:PAD_ROWS, :] = jnp.zeros((PAD_ROWS, D_MODEL), F32)
            q_cols = D_MODEL // N_CHIPS
            for k in range(N_CHIPS):
                cp = pltpu.make_async_copy(
                    m4_ref.at[k], head_ref.at[pl.ds(PAD_ROWS, N_META), pl.ds(k * q_cols, q_cols)], meta_sem)
                cp.start()
                cp.wait()

        @pl.when((kk == 2) & (i == 0))
        def _():
            gather.start_diagonal_after_neighbours(w4_ref, send_sems, recv_sems)

        @pl.when(kk == 0)
        def _():
            zt = _padded_tile(z_ref[...], head_ref[...], i)
            rstd = lax.rsqrt(jnp.mean(zt * zt, axis=-1, keepdims=True) + EPS)
            h = (zt * rstd * nw_ref[...]).astype(BF16)
            h_all[pl.ds(pl.multiple_of(i * tm, 16), tm), :] = h
            h_ref[...] = h

        @pl.when((kk == 2) & (i == 0))
        def _():
            gather.pass_on(0, w4_ref, send_sems, recv_sems)
            gather.pass_on(1, w4_ref, send_sems, recv_sems)
            gather.await_sibling(0, w4_ref, send_sems, recv_sems)

        @pl.when((kk == 4) & (i == 0))
        def _():
            gather.await_sibling(1, w4_ref, send_sems, recv_sems)

        @pl.when((kk == 5) & (i == 0))
        def _():
            gather.pass_on(2, w4_ref, send_sems, recv_sems)

        @pl.when((kk == 6) & (i == 0))
        def _():
            gather.await_sibling(2, w4_ref, send_sems, recv_sems)

        def weights(which):
            seg = order_ref[2 * (kk // 2) + which]
            return pltpu.make_async_copy(
                w4_ref.at[seg // 2, :, pl.ds(pl.multiple_of((seg % 2) * D_MODEL, D_MODEL), D_MODEL)],
                w_buf.at[which], w_sem.at[which])

        @pl.when((i == 0) & (kk % 2 == 0))
        def _():
            weights(0).start()
            weights(1).start()
            weights(0).wait()

        @pl.when((i == 0) & (kk % 2 == 1))
        def _():
            weights(1).wait()

        p_ref[0] = _dot(h_all[pl.ds(pl.multiple_of(i * tm, 16), tm), :], w_buf[kk % 2]) + b_ref[...]

        @pl.when((kk == N_SEG - 1) & (i == nt - 1))
        def _():
            gather.finish(w4_ref, send_sems, recv_sems, which=(2,))

    first_pass = lambda kk, i, order_ref: (jnp.where(kk == 0, i, nt - 1), 0)
    return pl.pallas_call(
        body, name="in_proj",
        grid_spec=pltpu.PrefetchScalarGridSpec(
            num_scalar_prefetch=1, grid=(N_SEG, nt),
            in_specs=[
                _token_window(tm, lambda kk, i, order_ref: jnp.where(kk == 0, i, nt - 1)),
                pl.BlockSpec((1, D_MODEL), lambda kk, i, order_ref: (0, 0)),
                pl.BlockSpec((1, D_MODEL), lambda kk, i, order_ref: (0, order_ref[kk])),
                ANY, ANY,
            ],
            out_specs=[
                pl.BlockSpec((tm, D_MODEL), first_pass),
                pl.BlockSpec((1, tm, D_MODEL), lambda kk, i, order_ref: (order_ref[kk], i, 0)),
                ANY,
                pl.BlockSpec((FIRST_TOKEN_ROW, D_MODEL), lambda kk, i, order_ref: (0, 0)),
                ANY,
            ],
            scratch_shapes=[
                pltpu.VMEM((rows, D_MODEL), BF16),
                pltpu.VMEM((2, D_MODEL, D_MODEL), BF16),
                pltpu.SemaphoreType.DMA((2,)),
            ] + gather.semaphores() + [
                pltpu.SemaphoreType.DMA((N_CHIPS - 1,)), pltpu.SemaphoreType.DMA((N_CHIPS - 1,)),
                pltpu.SemaphoreType.DMA,
            ]),
        out_shape=[
            jax.ShapeDtypeStruct((rows, D_MODEL), BF16),
            jax.ShapeDtypeStruct((N_SEG, rows, D_MODEL), F32),
            jax.ShapeDtypeStruct(w4.shape, w4.dtype),
            jax.ShapeDtypeStruct((FIRST_TOKEN_ROW, D_MODEL), F32),
            jax.ShapeDtypeStruct(m4.shape, m4.dtype),
        ],
        input_output_aliases={4: 2, 5: 4},
        compiler_params=_params(("arbitrary", "arbitrary")),
    )(seg_order, tokens, norm_w, b_in, w4, m4)


def _hgrn_forward(p3, lb_logits, wexp2, masks2, blob4, rows):
    n_chunks = rows // CHUNK
    cpb = _tile(n_chunks, 13, mult=1)
    rb_rows = cpb * CHUNK
    n_rb = n_chunks // cpb
    lanes = cpb * HEAD_DIM
    hps = FORWARD_HEADS_PER_STEP
    n_hb = N_HEADS // hps
    width = hps * HEAD_DIM
    gather = _ShardGather(blob4.shape[1])

    def body(q_ref, fz_ref, v_ref, lbl_ref, wexp_ref, mask_ref, b_in_ref, o_ref, s_ref, e16_ref, a2_ref, b4_ref,
             st_all, e_all, u_all, q_all, kk_all, v_all, send_sems, recv_sems):
        rb = pl.program_id(1)

        @pl.when((pl.program_id(0) == 0) & (rb == 0))
        def _():
            gather.start(b4_ref, send_sems, recv_sems)

        @pl.when(rb == 0)
        def _():
            st_all[...] = jnp.zeros_like(st_all)

        for j in range(hps):
            cols = pl.ds(j * HEAD_DIM, HEAD_DIM)
            one_head(rb, q_ref.at[0, :, cols], fz_ref.at[0, :, cols], v_ref.at[0, :, cols], lbl_ref.at[:, cols],
                     wexp_ref, mask_ref, o_ref.at[:, cols], s_ref.at[j], e16_ref.at[j, 0], a2_ref.at[:, cols],
                     st_all.at[j], e_all.at[j], u_all.at[j], q_all.at[j], kk_all.at[j], v_all.at[j])

        @pl.when((pl.program_id(0) == n_hb // 2) & (rb == 0))
        def _():
            for j in range(N_CHIPS - 1):
                gather.pass_on(j, b4_ref, send_sems, recv_sems)

        @pl.when((pl.program_id(0) == n_hb - 1) & (rb == n_rb - 1))
        def _():
            for j in range(N_CHIPS - 1):
                gather.await_sibling(j, b4_ref, send_sems, recv_sems)
            gather.finish(b4_ref, send_sems, recv_sems)

    def one_head(rb, q_ref, fz_ref, v_ref, lbl_ref, wexp_ref, mask_ref, o_ref, s_ref, e16_ref, a2_ref,
                 st_ref, e_ref, u_ref, q_s, kk_s, v_s):
        lb = _lower_bound(lbl_ref[...])
        row = rb * rb_rows + lax.broadcasted_iota(jnp.int32, (rb_rows, 1), 0)
        valid = row >= PAD_ROWS
        sg, sn = _sigmoid_pair(fz_ref[...])
        g = jnp.where(valid, jnp.log(lb + (1.0 - lb) * sg), 0.0)
        kk_s[...] = jnp.where(valid, (1.0 - lb) * sn, 0.0)
        q_s[...] = jnp.where(valid, q_ref[...], 0.0)
        v_s[...] = jnp.where(valid, v_ref[...], 0.0).astype(BF16)
        hi = g.astype(BF16)
        mid = (g - hi.astype(F32)).astype(BF16)
        g2 = jnp.concatenate(
            [jnp.concatenate([hi[b * CHUNK:(b + 1) * CHUNK], mid[b * CHUNK:(b + 1) * CHUNK]], axis=0)
             for b in range(cpb)], axis=1)
        e_ref[...] = jnp.exp(_dot(wexp_ref[...], g2))
        e16_ref[...] = e_ref[...].astype(BF16)

        def contribution(b, carry):
            r0 = pl.multiple_of(b * CHUNK, CHUNK)
            l0 = pl.multiple_of(b * HEAD_DIM, HEAD_DIM)
            kc16 = (kk_s[pl.ds(r0, CHUNK), :] * e_ref[CHUNK:2 * CHUNK, pl.ds(l0, HEAD_DIM)]).astype(BF16)
            u_ref[b] = _dot_tn(v_s[pl.ds(r0, CHUNK), :], kc16)
            return carry

        lax.fori_loop(0, cpb, contribution, 0, unroll=LOCAL_UNROLL)

        def recur(b, st):
            l0 = pl.multiple_of(b * HEAD_DIM, HEAD_DIM)
            s_ref[b] = st
            return st * e_ref[CHUNK - 1:CHUNK, pl.ds(l0, HEAD_DIM)] + u_ref[b]

        st_ref[...] = lax.fori_loop(0, cpb, recur, st_ref[...], unroll=LOCAL_UNROLL)

        zeros16 = jnp.zeros((CHUNK, HEAD_DIM), BF16)

        def local(b, carry):
            r0 = pl.multiple_of(b * CHUNK, CHUNK)
            l0 = pl.multiple_of(b * HEAD_DIM, HEAD_DIM)
            q = q_s[pl.ds(r0, CHUNK), :]
            kk = kk_s[pl.ds(r0, CHUNK), :]
            v16 = v_s[pl.ds(r0, CHUNK), :]

            def scaled(entry):
                if entry == 0:
                    return q.astype(BF16), kk.astype(BF16)
                e_m = e_ref[(1 + entry) * CHUNK:(2 + entry) * CHUNK, pl.ds(l0, HEAD_DIM)]
                return (q * e_m).astype(BF16), (kk * e_m).astype(BF16)

            a2 = jnp.zeros((CHUNK, 2 * CHUNK), F32)
            for p, (ea, eb) in enumerate(LEVEL_PAIRS):
                qa, ka = scaled(ea)
                if eb is None:
                    prod = _dot_nt(qa, jnp.concatenate([ka, zeros16], axis=0))
                else:
                    qb_, kb_ = scaled(eb)
                    rhs = jnp.concatenate([jnp.concatenate([ka, zeros16], axis=1),
                                           jnp.concatenate([zeros16, kb_], axis=1)], axis=0)
                    prod = _dot_nt(jnp.concatenate([qa, qb_], axis=1), rhs)
                a2 = a2 + mask_ref[p] * prod
            a2_16 = a2.astype(BF16)
            a2_ref[pl.ds(r0, CHUNK), :] = a2_16
            qb16 = (q * e_ref[0:CHUNK, pl.ds(l0, HEAD_DIM)]).astype(BF16)
            o_ref[pl.ds(r0, CHUNK), :] = (_dot(a2_16, jnp.concatenate([v16, v16], axis=0))
                                          + _dot_nt(qb16, s_ref[b].astype(BF16)))
            return carry

        lax.fori_loop(0, cpb, local, 0, unroll=LOCAL_UNROLL)

    head_block = lambda seg: pl.BlockSpec((1, rb_rows, width), lambda h, r: (seg, r, h))
    return pl.pallas_call(
        body, name="hgrn_forward",
        grid=(n_hb, n_rb),
        in_specs=[
            head_block(0), head_block(1), head_block(2),
            pl.BlockSpec((2, width), lambda h, r: (0, h)),
            pl.BlockSpec((N_EXP * CHUNK, 2 * CHUNK), lambda h, r: (0, 0)),
            pl.BlockSpec((len(LEVEL_PAIRS), CHUNK, 2 * CHUNK), lambda h, r: (0, 0, 0)),
            ANY,
        ],
        out_specs=[
            pl.BlockSpec((rb_rows, width), lambda h, r: (r, h)),
            pl.BlockSpec((hps, cpb, HEAD_DIM, HEAD_DIM), lambda h, r: (h, r, 0, 0)),
            pl.BlockSpec((hps, 1, N_EXP * CHUNK, lanes), lambda h, r: (h, r, 0, 0)),
            pl.BlockSpec((rb_rows, width), lambda h, r: (r, h)),
            ANY,
        ],
        out_shape=[
            jax.ShapeDtypeStruct((rows, D_MODEL), F32),
            jax.ShapeDtypeStruct((N_HEADS, n_chunks, HEAD_DIM, HEAD_DIM), F32),
            jax.ShapeDtypeStruct((N_HEADS, n_rb, N_EXP * CHUNK, lanes), BF16),
            jax.ShapeDtypeStruct((rows, D_MODEL), BF16),
            jax.ShapeDtypeStruct(blob4.shape, blob4.dtype),
        ],
        input_output_aliases={6: 4},
        scratch_shapes=[
            pltpu.VMEM((hps, HEAD_DIM, HEAD_DIM), F32),
            pltpu.VMEM((hps, N_EXP * CHUNK, lanes), F32),
            pltpu.VMEM((hps, cpb, HEAD_DIM, HEAD_DIM), F32),
            pltpu.VMEM((hps, rb_rows, HEAD_DIM), F32),
            pltpu.VMEM((hps, rb_rows, HEAD_DIM), F32),
            pltpu.VMEM((hps, rb_rows, HEAD_DIM), BF16),
        ] + gather.semaphores(),
        compiler_params=_params(("arbitrary", "arbitrary")),
    )(p3, p3, p3, lb_logits, wexp2, masks2, blob4)


def _hgrn_backward(p3, d_o, states, e16, a2, lb_logits, wexp_t, masks2, dw16, blob16, rx_blob, rows):
    n_chunks = rows // CHUNK
    cpb = _tile(n_chunks, 13, mult=1)
    rb_rows = cpb * CHUNK
    n_rb = n_chunks // cpb
    lanes = cpb * HEAD_DIM
    exchange = _GradExchange(SEGS_MIX, True, BLOB_RELATIONS_DIAGONAL)

    hps = BACKWARD_HEADS_PER_STEP
    n_hb = N_HEADS // hps
    width = hps * HEAD_DIM

    def body(q_ref, fz_ref, v_ref, do_ref, s_ref, e_ref, a2_ref, lbl_ref, wexpt_ref, mask_ref, dw_ref, blob_ref,
             rxb_in_ref, dp_ref, dlb_ref, rxw_ref, rxb_ref, *scratch):
        per_head, (send_sems, recv_sems) = scratch[:-2], scratch[-2:]
        step = pl.program_id(1)
        rb = n_rb - 1 - step

        @pl.when((pl.program_id(0) == 0) & (step == 0))
        def _():
            exchange.start(dw_ref, rxw_ref, blob_ref, rxb_ref, send_sems, recv_sems)

        for j in range(hps):
            cols = pl.ds(j * HEAD_DIM, HEAD_DIM)
            one_head(step, rb, q_ref.at[0, :, cols], fz_ref.at[0, :, cols], v_ref.at[0, :, cols], do_ref.at[:, cols],
                     s_ref.at[j], e_ref.at[j, 0], a2_ref.at[:, cols], lbl_ref.at[:, cols], wexpt_ref, mask_ref,
                     dp_ref.at[:, :, cols], dlb_ref.at[:, cols], *[ref.at[j] for ref in per_head])

        @pl.when((pl.program_id(0) == n_hb - 1) & (step == n_rb - 1))
        def _():
            exchange.wait(dw_ref, rxw_ref, blob_ref, rxb_ref, send_sems, recv_sems)

    def one_head(step, rb, q_ref, fz_ref, v_ref, do_ref, s_ref, e_ref, a2_ref, lbl_ref, wexpt_ref, mask_ref,
                 dp_ref, dlb_ref, dst_ref, g_ref, dsn_ref, q_s, kk_s, v_s, do_s, dq_s, dkk_s, dg_s, dx_s, da2_s):
        @pl.when(step == 0)
        def _():
            dst_ref[...] = jnp.zeros_like(dst_ref)
            dlb_ref[...] = jnp.zeros_like(dlb_ref)

        lb = _lower_bound(lbl_ref[...])
        row = rb * rb_rows + lax.broadcasted_iota(jnp.int32, (rb_rows, 1), 0)
        valid = row >= PAD_ROWS
        sg, sn = _sigmoid_pair(fz_ref[...])
        f = lb + (1.0 - lb) * sg
        g = jnp.where(valid, jnp.log(f), 0.0)
        kk_s[...] = jnp.where(valid, (1.0 - lb) * sn, 0.0)
        q_s[...] = jnp.where(valid, q_ref[...], 0.0)
        v_s[...] = jnp.where(valid, v_ref[...], 0.0).astype(BF16)
        do_s[...] = do_ref[...].astype(BF16)
        e_last_all = jnp.exp(jnp.concatenate(
            [jnp.sum(g[b * CHUNK:(b + 1) * CHUNK], axis=0, keepdims=True) for b in range(cpb)], axis=0))
        last_row = lax.broadcasted_iota(jnp.int32, (CHUNK, 1), 0) == CHUNK - 1
        zeros16 = jnp.zeros((CHUNK, HEAD_DIM), BF16)

        def factor(block, l0):
            return e_ref[block * CHUNK:(block + 1) * CHUNK, pl.ds(l0, HEAD_DIM)].astype(F32)

        def contribution(b, carry):
            r0 = pl.multiple_of(b * CHUNK, CHUNK)
            l0 = pl.multiple_of(b * HEAD_DIM, HEAD_DIM)
            qb16 = (q_s[pl.ds(r0, CHUNK), :] * factor(0, l0)).astype(BF16)
            g_ref[b] = _dot_tn(do_s[pl.ds(r0, CHUNK), :], qb16)
            return carry

        lax.fori_loop(0, cpb, contribution, 0, unroll=LOCAL_UNROLL)

        cur = dst_ref[...]
        for b in reversed(range(cpb)):
            dsn_ref[b] = cur
            cur = cur * e_last_all[b:b + 1, :] + g_ref[b]
        dst_ref[...] = cur

        def through_state(b, carry):
            r0 = pl.multiple_of(b * CHUNK, CHUNK)
            l0 = pl.multiple_of(b * HEAD_DIM, HEAD_DIM)
            v16 = v_s[pl.ds(r0, CHUNK), :]
            do16 = do_s[pl.ds(r0, CHUNK), :]
            st = s_ref[b]
            dsn = dsn_ref[b]
            dsn16 = dsn.astype(BF16)
            e_b, e_c = factor(0, l0), factor(1, l0)
            qb, kc = q_s[pl.ds(r0, CHUNK), :] * e_b, kk_s[pl.ds(r0, CHUNK), :] * e_c

            t = _dot_tn(a2_ref[pl.ds(r0, CHUNK), :], do16)
            dv = t[0:CHUNK] + t[CHUNK:2 * CHUNK] + _dot_nt(kc.astype(BF16), dsn16)
            dp_ref[2, pl.ds(r0, CHUNK), :] = dv.astype(BF16)
            da2_s[pl.ds(r0, CHUNK), :] = _dot_nt(do16, jnp.concatenate([v16, v16], axis=0))
            dqb = _dot(do16, st.astype(BF16))
            dkc = _dot(v16, dsn16)
            de = jnp.sum(dsn * st, axis=0, keepdims=True) * e_b[CHUNK - 1:CHUNK, :]
            dq_s[pl.ds(r0, CHUNK), :] = e_b * dqb
            dkk_s[pl.ds(r0, CHUNK), :] = e_c * dkc
            dx_s[0:CHUNK, pl.ds(l0, HEAD_DIM)] = (qb * dqb + jnp.where(last_row, de, 0.0)).astype(BF16)
            dx_s[CHUNK:2 * CHUNK, pl.ds(l0, HEAD_DIM)] = (kc * dkc).astype(BF16)
            return carry

        lax.fori_loop(0, cpb, through_state, 0, unroll=BACKWARD_UNROLL)

        def local(b, carry):
            r0 = pl.multiple_of(b * CHUNK, CHUNK)
            l0 = pl.multiple_of(b * HEAD_DIM, HEAD_DIM)
            q = q_s[pl.ds(r0, CHUNK), :]
            kk = kk_s[pl.ds(r0, CHUNK), :]
            da2 = da2_s[pl.ds(r0, CHUNK), :]
            dq = dq_s[pl.ds(r0, CHUNK), :]
            dkk = dkk_s[pl.ds(r0, CHUNK), :]

            def scaled(entry):
                if entry == 0:
                    return q, kk, None
                e_m = factor(1 + entry, l0)
                return q * e_m, kk * e_m, e_m

            for p, (ea, eb) in enumerate(LEVEL_PAIRS):
                dm = mask_ref[p] * da2
                dm_t = dm.T.astype(BF16)
                qa, ka, e_a = scaled(ea)
                if eb is None:
                    rhs_k = jnp.concatenate([jnp.concatenate([ka.astype(BF16), zeros16], axis=1),
                                             jnp.concatenate([zeros16, zeros16], axis=1)], axis=0)
                else:
                    qb_, kb_, e_bb = scaled(eb)
                    rhs_k = jnp.concatenate([jnp.concatenate([ka.astype(BF16), zeros16], axis=1),
                                             jnp.concatenate([zeros16, kb_.astype(BF16)], axis=1)], axis=0)
                dq2 = _dot(dm.astype(BF16), rhs_k)
                parts = [(ea, qa, ka, e_a, dq2[:, :HEAD_DIM], _dot(dm_t[0:CHUNK], qa.astype(BF16)))]
                if eb is not None:
                    parts.append((eb, qb_, kb_, e_bb, dq2[:, HEAD_DIM:],
                                  _dot(dm_t[CHUNK:2 * CHUNK], qb_.astype(BF16))))
                for entry, q_m, k_m, e_m, dq_m, dk_m in parts:
                    if entry == 0:
                        dq = dq + dq_m
                        dkk = dkk + dk_m
                    else:
                        dq = dq + e_m * dq_m
                        dkk = dkk + e_m * dk_m
                        dx_s[(1 + entry) * CHUNK:(2 + entry) * CHUNK, pl.ds(l0, HEAD_DIM)] = (
                            q_m * dq_m + k_m * dk_m).astype(BF16)
            dq_s[pl.ds(r0, CHUNK), :] = dq
            dkk_s[pl.ds(r0, CHUNK), :] = dkk
            return carry

        lax.fori_loop(0, cpb, local, 0, unroll=BACKWARD_UNROLL)

        dg_all = _dot(wexpt_ref[...], dx_s[...])
        for b in range(cpb):
            dg_s[b * CHUNK:(b + 1) * CHUNK, :] = dg_all[:, b * HEAD_DIM:(b + 1) * HEAD_DIM]
        t = jnp.where(valid, dg_s[...] / f - dkk_s[...], 0.0)
        dlb_ref[...] += jnp.sum(sn * t, axis=0, keepdims=True)
        dp_ref[0] = jnp.where(valid, dq_s[...], 0.0).astype(BF16)
        dp_ref[1] = ((1.0 - lb) * sg * sn * t).astype(BF16)

    head_block = lambda seg: pl.BlockSpec((1, rb_rows, width), lambda h, s: (seg, n_rb - 1 - s, h))
    row_block = pl.BlockSpec((rb_rows, width), lambda h, s: (n_rb - 1 - s, h))
    return pl.pallas_call(
        body, name="hgrn_backward",
        grid=(n_hb, n_rb),
        in_specs=[
            head_block(0), head_block(1), head_block(2),
            row_block,
            pl.BlockSpec((hps, cpb, HEAD_DIM, HEAD_DIM), lambda h, s: (h, n_rb - 1 - s, 0, 0)),
            pl.BlockSpec((hps, 1, N_EXP * CHUNK, lanes), lambda h, s: (h, n_rb - 1 - s, 0, 0)),
            row_block,
            pl.BlockSpec((2, width), lambda h, s: (0, h)),
            pl.BlockSpec((CHUNK, N_EXP * CHUNK), lambda h, s: (0, 0)),
            pl.BlockSpec((len(LEVEL_PAIRS), CHUNK, 2 * CHUNK), lambda h, s: (0, 0, 0)),
            ANY, ANY, ANY,
        ],
        out_specs=[
            pl.BlockSpec((3, rb_rows, width), lambda h, s: (0, n_rb - 1 - s, h)),
            pl.BlockSpec((1, width), lambda h, s: (0, h)),
            ANY, ANY,
        ],
        out_shape=[
            jax.ShapeDtypeStruct((3, rows, D_MODEL), BF16),
            jax.ShapeDtypeStruct((1, D_MODEL), F32),
            exchange.landing_w(), jax.ShapeDtypeStruct(rx_blob.shape, rx_blob.dtype),
        ],
        input_output_aliases={12: 3},
        scratch_shapes=[
            pltpu.VMEM((hps, HEAD_DIM, HEAD_DIM), F32),
            pltpu.VMEM((hps, cpb, HEAD_DIM, HEAD_DIM), F32),
            pltpu.VMEM((hps, cpb, HEAD_DIM, HEAD_DIM), F32),
            pltpu.VMEM((hps, rb_rows, HEAD_DIM), F32),
            pltpu.VMEM((hps, rb_rows, HEAD_DIM), F32),
            pltpu.VMEM((hps, rb_rows, HEAD_DIM), BF16),
            pltpu.VMEM((hps, rb_rows, HEAD_DIM), BF16),
            pltpu.VMEM((hps, rb_rows, HEAD_DIM), F32),
            pltpu.VMEM((hps, rb_rows, HEAD_DIM), F32),
            pltpu.VMEM((hps, rb_rows, HEAD_DIM), F32),
            pltpu.VMEM((hps, N_EXP * CHUNK, lanes), BF16),
            pltpu.VMEM((hps, rb_rows, 2 * CHUNK), F32),
        ] + exchange.semaphores(),
        compiler_params=_params(("arbitrary", "arbitrary")),
    )(p3, p3, p3, d_o, states, e16, a2, lb_logits, wexp_t, masks2, dw16, blob16, rx_blob)


def _sigmoid(x):
    return 1.0 / (1.0 + jnp.exp(-x))


def _silu_and_grad(x):
    s = _sigmoid(x)
    return x * s, s * (1.0 + x * (1.0 - s))


def _window_sum(ext, width, forward_looking):
    n = ext.shape[0]
    s = ext
    step = 1
    while step < width:
        s = s + pltpu.roll(s, (n - step) if forward_looking else step, 0)
        step *= 2
    return s


def _mixers(o, p3, tokens, head, tgt, wdh, wdp, wout, poolw, hg_w, pool_scale, final_w, rows):
    tm = _tile(rows, 208)
    nt = rows // tm
    halo_blocks = tm // HALO
    n_grp = len(POOL_WINDOWS)
    q_rows = D_MODEL // N_CHIPS
    blob_rows = 3 * q_rows + n_grp * POOL_GDIM * POOL_GDIM // (N_CHIPS * D_MODEL)

    def body(o_ref, ghg_ref, u_ref, gpl_ref, mhg_ref, mpl_ref, uh_ref, z_ref, t_ref,
             wdh_ref, wdp_ref, wout_ref, pw_ref, hgw_ref, ps_ref, fw_ref, head_ref,
             do_ref, dz2_ref, dp_ref, blob_ref, dpw_ref, small_ref, carry_ref):
        step = pl.program_id(0)
        tile = nt - 1 - step

        def add_to_blob(piece, dw):
            for k in range(N_CHIPS):
                blob_ref[k, piece * q_rows:(piece + 1) * q_rows, :] += dw[k * q_rows:(k + 1) * q_rows]

        @pl.when(step == 0)
        def _():
            blob_ref[...] = jnp.zeros_like(blob_ref)
            dpw_ref[...] = jnp.zeros_like(dpw_ref)
            small_ref[...] = jnp.zeros_like(small_ref)
            carry_ref[...] = jnp.zeros_like(carry_ref)

        row = tile * tm + lax.broadcasted_iota(jnp.int32, (tm, 1), 0)
        real = row >= PAD_ROWS
        pos1 = jnp.maximum(row - PAD_ROWS + 1, 1).astype(F32)

        u = jnp.where(real, u_ref[0], 0.0)
        halo_row = tile * tm - HALO + lax.broadcasted_iota(jnp.int32, (HALO, 1), 0)
        uh = jnp.where(halo_row >= PAD_ROWS, uh_ref[0], 0.0)
        ext = jnp.concatenate([uh, u], axis=0)
        pooled, inv_cnt, mixed = [], [], []
        for g, w in enumerate(POOL_WINDOWS):
            cols = slice(g * POOL_GDIM, (g + 1) * POOL_GDIM)
            inv = 1.0 / jnp.minimum(pos1, float(w))
            ws = _window_sum(ext[:, cols], w, False)[HALO:]
            pg = (ws * inv - u[:, cols]).astype(BF16)
            pooled.append(pg)
            inv_cnt.append(inv)
            mixed.append(_dot(pg, pw_ref[g]))
        mixed = jnp.concatenate(mixed, axis=1)
        gpl = gpl_ref[0]
        sp, dsp = _silu_and_grad(gpl)
        ps = ps_ref[...]
        a_pool = (mixed * ps * sp).astype(BF16)
        y_pool = _dot(a_pool, wdp_ref[...])

        o = o_ref[...]
        o_hat, rstd_h = [], []
        for h in range(N_HEADS):
            oh = o[:, h * HEAD_DIM:(h + 1) * HEAD_DIM]
            r = lax.rsqrt(jnp.mean(oh * oh, axis=-1, keepdims=True) + EPS)
            rstd_h.append(r)
            o_hat.append(oh * r)
        o_hat = jnp.concatenate(o_hat, axis=1)
        hgw = hgw_ref[...]
        o_n = o_hat * hgw
        ghg = ghg_ref[0]
        sh, dsh = _silu_and_grad(ghg)
        a_hg = (o_n * sh).astype(BF16)
        y_hg = _dot(a_hg, wdh_ref[...])

        s_mh = _sigmoid(mhg_ref[0])
        s_mp = _sigmoid(mpl_ref[0])
        merged = (s_mh * y_hg + s_mp * y_pool).astype(BF16)
        z2 = _padded_tile(z_ref[...], head_ref[...], tile) + _dot(merged, wout_ref[...])
        rstd2 = lax.rsqrt(jnp.mean(z2 * z2, axis=-1, keepdims=True) + EPS)
        zh = z2 * rstd2
        fw = fw_ref[...]
        target = _padded_tile(t_ref[...], jnp.zeros((FIRST_TOKEN_ROW, D_MODEL), F32), tile)
        err = jnp.where(row >= FIRST_TOKEN_ROW, zh * fw - target, 0.0)
        small_ref[ROW_LOSS:ROW_LOSS + 1, :] += jnp.sum(err * err, axis=0, keepdims=True) * (0.5 / D_MODEL)
        dy = err * (1.0 / D_MODEL)

        small_ref[ROW_FINAL_W:ROW_FINAL_W + 1, :] += jnp.sum(dy * zh, axis=0, keepdims=True)
        uu = dy * fw
        dz2 = rstd2 * (uu - zh * jnp.mean(uu * zh, axis=-1, keepdims=True))
        dz2_ref[...] = dz2
        dz2_16 = dz2.astype(BF16)
        dmerged = _dot_nt(dz2_16, wout_ref[...])
        add_to_blob(2, _dot_tn(merged, dz2_16))
        dy_hg = (s_mh * dmerged).astype(BF16)
        dy_pool = (s_mp * dmerged).astype(BF16)
        dp_ref[3] = (dmerged * y_hg * s_mh * (1.0 - s_mh)).astype(BF16)
        dp_ref[4] = (dmerged * y_pool * s_mp * (1.0 - s_mp)).astype(BF16)

        da_hg = _dot_nt(dy_hg, wdh_ref[...])
        add_to_blob(0, _dot_tn(a_hg, dy_hg))
        dp_ref[0] = (da_hg * o_n * dsh).astype(BF16)
        do_n = da_hg * sh
        small_ref[ROW_HG_W:ROW_HG_W + 1, :] += jnp.sum(do_n * o_hat, axis=0, keepdims=True)
        d_hat = do_n * hgw
        for h in range(N_HEADS):
            cols = slice(h * HEAD_DIM, (h + 1) * HEAD_DIM)
            dh_, oh_ = d_hat[:, cols], o_hat[:, cols]
            do_ref[:, cols] = rstd_h[h] * (dh_ - oh_ * jnp.mean(dh_ * oh_, axis=-1, keepdims=True))

        da_pool = _dot_nt(dy_pool, wdp_ref[...])
        add_to_blob(1, _dot_tn(a_pool, dy_pool))
        small_ref[ROW_POOL_SCALE:ROW_POOL_SCALE + 1, :] += jnp.sum(da_pool * mixed * sp, axis=0, keepdims=True)
        dp_ref[2] = (da_pool * mixed * ps * dsp).astype(BF16)
        dmixed = (da_pool * ps * sp).astype(BF16)
        carry = carry_ref[...]
        du, new_carry = [], []
        for g, w in enumerate(POOL_WINDOWS):
            cols = slice(g * POOL_GDIM, (g + 1) * POOL_GDIM)
            dmg = dmixed[:, cols]
            dpooled = _dot_nt(dmg, pw_ref[g])
            dpw_ref[g] += _dot_tn(pooled[g], dmg)
            dps = dpooled * inv_cnt[g]
            ext_b = jnp.concatenate([dps, carry[:, cols]], axis=0)
            du.append(_window_sum(ext_b, w, True)[:tm] - dpooled)
            new_carry.append(dps[:HALO])
        dp_ref[1] = jnp.where(real, jnp.concatenate(du, axis=1), 0.0).astype(BF16)
        carry_ref[...] = jnp.concatenate(new_carry, axis=1)

    row_block = pl.BlockSpec((tm, D_MODEL), lambda s: (nt - 1 - s, 0))
    seg_block = lambda seg: pl.BlockSpec((1, tm, D_MODEL), lambda s: (seg, nt - 1 - s, 0))
    whole = pl.BlockSpec(memory_space=pltpu.VMEM)
    return pl.pallas_call(
        body, name="mixers",
        grid=(nt,),
        in_specs=[
            row_block, seg_block(3), seg_block(4), seg_block(5), seg_block(6), seg_block(7),
            pl.BlockSpec((1, HALO, D_MODEL),
                         lambda s: (4, jnp.maximum((nt - 1 - s) * halo_blocks - 1, 0), 0)),
            _token_window(tm, lambda s: nt - 1 - s), _token_window(tm, lambda s: nt - 1 - s),
            whole, whole, whole, whole, whole, whole, whole, whole,
        ],
        out_specs=[
            row_block, row_block,
            pl.BlockSpec((5, tm, D_MODEL), lambda s: (0, nt - 1 - s, 0)),
            whole, whole, whole,
        ],
        out_shape=[
            jax.ShapeDtypeStruct((rows, D_MODEL), F32),
            jax.ShapeDtypeStruct((rows, D_MODEL), F32),
            jax.ShapeDtypeStruct((5, rows, D_MODEL), BF16),
            jax.ShapeDtypeStruct((N_CHIPS, blob_rows, D_MODEL), F32),
            jax.ShapeDtypeStruct((n_grp, POOL_GDIM, POOL_GDIM), F32),
            jax.ShapeDtypeStruct((SMALL_ROWS, D_MODEL), F32),
        ],
        scratch_shapes=[pltpu.VMEM((HALO, D_MODEL), F32)],
        compiler_params=_params(("arbitrary",)),
    )(o, p3, p3, p3, p3, p3, p3, tokens, tgt, wdh, wdp, wout, poolw, hg_w, pool_scale, final_w, head)


def _seg_specs(tm, row_of, seg_of):
    def spec_a(*g):
        k = seg_of(*g)
        return (jnp.minimum(k, 2), jnp.where(k < 3, row_of(*g), 0), 0)

    def spec_b(*g):
        k = seg_of(*g)
        return (jnp.maximum(k - 3, 0), jnp.where(k >= 3, row_of(*g), 0), 0)

    return pl.BlockSpec((1, tm, D_MODEL), spec_a), pl.BlockSpec((1, tm, D_MODEL), spec_b)


def _in_proj_weight_grad(h, dp, rows, name, blob16=None):
    n_seg = dp.shape[0]
    tm = _tile(rows, 1040)
    nt = rows // tm
    half = D_MODEL // 2
    exchange = _GradExchange((), True, BLOB_RELATIONS_DIRECT) if blob16 is not None else None

    def body(*refs):
        if exchange is None:
            (h_ref, dp_ref, part_ref, part16_ref, db_ref, acc_ref, bacc_ref, stage_ref, land_ref,
             send_sems, recv_sems) = refs
        else:
            (h_ref, dp_ref, blob_ref, part_ref, part16_ref, db_ref, rxb_ref, acc_ref, bacc_ref, stage_ref, land_ref,
             send_sems, recv_sems, blob_send, blob_recv) = refs
        k, i = pl.program_id(0), pl.program_id(1)
        x, y, c = lax.axis_index("x"), lax.axis_index("y"), lax.axis_index("c")

        if exchange is not None:
            @pl.when((k == 0) & (i == 0))
            def _():
                exchange.start(None, None, blob_ref, rxb_ref, blob_send, blob_recv)

        def to_sibling(seg):
            return pltpu.make_async_remote_copy(
                src_ref=stage_ref.at[seg], dst_ref=land_ref.at[seg], send_sem=send_sems.at[seg],
                recv_sem=recv_sems.at[seg], device_id=(x, y, 1 - c), device_id_type=MESH)

        @pl.when(i == 0)
        def _():
            acc_ref[...] = jnp.zeros_like(acc_ref)
            bacc_ref[...] = jnp.zeros_like(bacc_ref)

        dpt = dp_ref[0]
        acc_ref[...] += _dot_tn(h_ref[...], dpt)
        bacc_ref[...] += jnp.sum(dpt.astype(F32), axis=0, keepdims=True)

        @pl.when(i == nt - 1)
        def _():
            db_ref[0] = bacc_ref[...]
            part_ref[k] = acc_ref[pl.ds(pl.multiple_of(c * half, half), half), :]
            stage_ref[k] = acc_ref[pl.ds(pl.multiple_of((1 - c) * half, half), half), :].astype(BF16)
            to_sibling(k).start()

        @pl.when((k == n_seg - 1) & (i == nt - 1))
        def _():
            for seg in range(n_seg):
                to_sibling(seg).wait_recv()
                total = part_ref[seg] + land_ref[seg].astype(F32)
                part_ref[seg] = total
                part16_ref[seg] = total.astype(BF16)
            for seg in range(n_seg):
                to_sibling(seg).wait_send()
            if exchange is not None:
                exchange.wait(None, None, blob_ref, rxb_ref, blob_send, blob_recv)

    whole = pl.BlockSpec(memory_space=pltpu.VMEM)
    with_blob = exchange is not None
    return pl.pallas_call(
        body, name=name,
        grid=(n_seg, nt),
        in_specs=[pl.BlockSpec((tm, D_MODEL), lambda k, i: (i, 0)),
                  pl.BlockSpec((1, tm, D_MODEL), lambda k, i: (k, i, 0))] + [ANY] * with_blob,
        out_specs=[whole, whole, pl.BlockSpec((1, 1, D_MODEL), lambda k, i: (k, 0, 0))] + [ANY] * with_blob,
        out_shape=[
            jax.ShapeDtypeStruct((n_seg, half, D_MODEL), F32),
            jax.ShapeDtypeStruct((n_seg, half, D_MODEL), BF16),
            jax.ShapeDtypeStruct((n_seg, 1, D_MODEL), F32),
        ] + ([exchange.landing_blob(blob16)] if with_blob else []),
        scratch_shapes=[
            pltpu.VMEM((D_MODEL, D_MODEL), F32), pltpu.VMEM((1, D_MODEL), F32),
            pltpu.VMEM((n_seg, half, D_MODEL), BF16),
            pltpu.VMEM((n_seg, half, D_MODEL), BF16),
            pltpu.SemaphoreType.DMA((n_seg,)), pltpu.SemaphoreType.DMA((n_seg,)),
        ] + (exchange.semaphores() if with_blob else []),
        compiler_params=_params(("arbitrary", "arbitrary")),
    )(h, dp, *([blob16] if with_blob else []))


def _input_grad(dpa, dpb, w4, tokens, head, dz2, norm_w, dw16, rows):
    tm = _tile(rows, 1040)
    nt = rows // tm
    assert nt >= 2, rows
    exchange = _GradExchange(SEGS_REC, with_blob=False)

    def body(dpa_ref, dpb_ref, w_ref, z_ref, head_ref, dz2_ref, nw_ref, dw_ref, gx_ref, dmeta_ref, dnw_ref, rxw_ref,
             acc_ref, dz_buf, out_sem, send_sems, recv_sems):
        i, k = pl.program_id(0), pl.program_id(1)

        def first_tile_out():
            return pltpu.make_async_copy(dz_buf.at[pl.ds(FIRST_TOKEN_ROW, tm - FIRST_TOKEN_ROW), :],
                                         gx_ref.at[pl.ds(0, tm - FIRST_TOKEN_ROW), :], out_sem)

        def tile_out(tile):
            start = pl.multiple_of(tile * tm - FIRST_TOKEN_ROW, HALO)
            return pltpu.make_async_copy(dz_buf, gx_ref.at[pl.ds(start, tm), :], out_sem)

        @pl.when((i == 0) & (k == 0))
        def _():
            exchange.start(dw_ref, rxw_ref, None, None, send_sems, recv_sems)
            dnw_ref[...] = jnp.zeros_like(dnw_ref)

        @pl.when((i == nt - 1) & (k == N_SEG - 1))
        def _():
            exchange.wait(dw_ref, rxw_ref, None, None, send_sems, recv_sems)

        @pl.when(k == 0)
        def _():
            acc_ref[...] = jnp.zeros_like(acc_ref)

        @pl.when(k < 3)
        def _():
            acc_ref[...] += _dot_nt(dpa_ref[0], w_ref[0])

        @pl.when(k >= 3)
        def _():
            acc_ref[...] += _dot_nt(dpb_ref[0], w_ref[0])

        @pl.when(k == N_SEG - 1)
        def _():
            zt = _padded_tile(z_ref[...], head_ref[...], i)
            rstd = lax.rsqrt(jnp.mean(zt * zt, axis=-1, keepdims=True) + EPS)
            zh = zt * rstd
            dh = acc_ref[...]
            dnw_ref[...] += jnp.sum(dh * zh, axis=0, keepdims=True)
            uu = dh * nw_ref[...]
            dz = dz2_ref[...] + rstd * (uu - zh * jnp.mean(uu * zh, axis=-1, keepdims=True))

            @pl.when(i == 1)
            def _():
                first_tile_out().wait()

            @pl.when(i >= 2)
            def _():
                tile_out(i - 1).wait()

            dz_buf[...] = dz

            @pl.when(i == 0)
            def _():
                dmeta_ref[...] = dz[PAD_ROWS:FIRST_TOKEN_ROW]
                first_tile_out().start()

            @pl.when(i > 0)
            def _():
                tile_out(i).start()

            @pl.when(i == nt - 1)
            def _():
                tile_out(i).wait()

    spec_a, spec_b = _seg_specs(tm, lambda i, k: i, lambda i, k: k)
    last_only = pl.BlockSpec((tm, D_MODEL), lambda i, k: (jnp.where(k == N_SEG - 1, i, 0), 0))
    return pl.pallas_call(
        body, name="input_grad",
        grid=(nt, N_SEG),
        in_specs=[
            spec_a, spec_b,
            pl.BlockSpec((1, D_MODEL, D_MODEL), lambda i, k: (k // 2, 0, k % 2)),
            _token_window(tm, lambda i, k: jnp.where(k == N_SEG - 1, i, 0)),
            pl.BlockSpec((FIRST_TOKEN_ROW, D_MODEL), lambda i, k: (0, 0)),
            last_only,
            pl.BlockSpec((1, D_MODEL), lambda i, k: (0, 0)),
            ANY,
        ],
        out_specs=[
            ANY,
            pl.BlockSpec((N_META, D_MODEL), lambda i, k: (0, 0)),
            pl.BlockSpec((1, D_MODEL), lambda i, k: (0, 0)),
            ANY,
        ],
        out_shape=[
            jax.ShapeDtypeStruct((rows - FIRST_TOKEN_ROW, D_MODEL), F32),
            jax.ShapeDtypeStruct((N_META, D_MODEL), F32),
            jax.ShapeDtypeStruct((1, D_MODEL), F32),
            exchange.landing_w(),
        ],
        scratch_shapes=[pltpu.VMEM((tm, D_MODEL), F32), pltpu.VMEM((tm, D_MODEL), F32),
                        pltpu.SemaphoreType.DMA] + exchange.semaphores(),
        compiler_params=_params(("arbitrary", "arbitrary")),
    )(dpa, dpb, w4, tokens, head, dz2, norm_w, dw16)


def _local_step(tokens, m4, tgt, w4, blob4, seg_order, norm_w, b_in, lb_logits, hg_w, pool_scale, final_w):
    rows = FIRST_TOKEN_ROW + tokens.shape[0]
    q = D_MODEL // N_CHIPS
    n_grp = len(POOL_WINDOWS)
    pg = POOL_GDIM // N_CHIPS

    wexp2 = jnp.asarray(np.tile(_exponent_matrix(), (1, 2)), BF16)
    wexp_t = jnp.asarray(_exponent_matrix().T, BF16)
    masks2 = jnp.asarray(_paired_masks(), F32)

    h, p3, w4, head, _ = _in_proj(tokens, m4, norm_w, w4, b_in, seg_order, rows)
    o, states, e16, a2, blob4 = _hgrn_forward(p3, lb_logits, wexp2, masks2, blob4, rows)
    wdh = blob4[:, 0:q].reshape(D_MODEL, D_MODEL)
    wdp = blob4[:, q:2 * q].reshape(D_MODEL, D_MODEL)
    wout = blob4[:, 2 * q:3 * q].reshape(D_MODEL, D_MODEL)
    poolw = blob4[:, 3 * q:].reshape(N_CHIPS, n_grp, pg, POOL_GDIM).transpose(1, 0, 2, 3)
    poolw = poolw.reshape(n_grp, POOL_GDIM, POOL_GDIM)
    d_o, dz2, dpb, dblob4, dpw, small = _mixers(
        o, p3, tokens, head, tgt, wdh, wdp, wout, poolw, hg_w, pool_scale, final_w, rows)
    dpw4 = dpw.reshape(n_grp, N_CHIPS, pg, POOL_GDIM).transpose(1, 0, 2, 3)
    dpw4 = dpw4.reshape(N_CHIPS, n_grp * pg * POOL_GDIM // D_MODEL, D_MODEL)
    dblob4 = dblob4.at[:, 3 * q:, :].set(dpw4)

    blob16 = dblob4.astype(BF16)
    dw_mix, dw_mix16, db_mix, rx_blob = _in_proj_weight_grad(h, dpb, rows, "in_proj_weight_grad_mix", blob16)
    dpa, dlb, rxw_mix, rx_blob = _hgrn_backward(
        p3, d_o, states, e16, a2, lb_logits, wexp_t, masks2, dw_mix16, blob16, rx_blob, rows)
    dw_rec, dw_rec16, db_rec = _in_proj_weight_grad(h, dpa, rows, "in_proj_weight_grad_rec")
    d_tokens, d_meta, dnw, rxw_rec = _input_grad(dpa, dpb, w4, tokens, head, dz2, norm_w, dw_rec16, rows)

    small = jnp.concatenate([
        small[ROW_LOSS:ROW_LOSS + 1],
        d_meta,
        dnw,
        db_rec.reshape(len(SEGS_REC), D_MODEL), db_mix.reshape(len(SEGS_MIX), D_MODEL),
        dlb, jnp.zeros_like(dlb),
        small[ROW_HG_W:ROW_HG_W + 1], small[ROW_POOL_SCALE:ROW_POOL_SCALE + 1],
        small[ROW_FINAL_W:ROW_FINAL_W + 1],
        jnp.zeros((SMALL_ROWS - ROW_FINAL_W - 1, D_MODEL), F32),
    ], axis=0)
    return d_tokens, (dw_rec, dw_mix, rxw_rec, rxw_mix), (dblob4, rx_blob), small


ANY = pl.BlockSpec(memory_space=pl.ANY)
MESH = pl.DeviceIdType.MESH


def _place():
    x, y, c = lax.axis_index("x"), lax.axis_index("y"), lax.axis_index("c")
    chips = [(1 - x, y), (x, 1 - y), (1 - x, 1 - y)]
    return x, y, c, chips


class _ShardGather:
    def __init__(self, rows):
        self.half = rows // 2

    def semaphores(self):
        return [pltpu.SemaphoreType.DMA((6,)), pltpu.SemaphoreType.DMA((6,))]

    def _copy(self, k, slot, to, send_sems, recv_sems):
        return pltpu.make_async_remote_copy(src_ref=slot, dst_ref=slot, send_sem=send_sems.at[k],
                                            recv_sem=recv_sems.at[k], device_id=to, device_id_type=MESH)

    def _half(self, ref4, chip, which):
        return ref4.at[chip, pl.ds(which * self.half, self.half), :]

    def start(self, ref4, send_sems, recv_sems, which=(0, 1, 2)):
        x, y, c, chips = _place()
        for j in which:
            cx, cy = chips[j]
            self._copy(j, self._half(ref4, 2 * x + y, c), (cx, cy, c), send_sems, recv_sems).start()

    def start_diagonal_after_neighbours(self, ref4, send_sems, recv_sems):
        x, y, c, chips = _place()
        for j in (0, 1):
            cx, cy = chips[j]
            self._copy(j, self._half(ref4, 2 * x + y, c), (cx, cy, c), send_sems, recv_sems).wait_send()
        self.start(ref4, send_sems, recv_sems, which=(2,))

    def pass_on(self, j, ref4, send_sems, recv_sems):
        x, y, c, chips = _place()
        cx, cy = chips[j]
        landed = self._half(ref4, 2 * cx + cy, c)
        self._copy(j, landed, (cx, cy, c), send_sems, recv_sems).wait_recv()
        self._copy(3 + j, landed, (x, y, 1 - c), send_sems, recv_sems).start()

    def await_sibling(self, j, ref4, send_sems, recv_sems):
        x, y, c, chips = _place()
        cx, cy = chips[j]
        self._copy(3 + j, self._half(ref4, 2 * cx + cy, 1 - c), (x, y, 1 - c), send_sems, recv_sems).wait_recv()

    def finish(self, ref4, send_sems, recv_sems, which=(0, 1, 2)):
        x, y, c, chips = _place()
        for j, (cx, cy) in enumerate(chips):
            if j in which:
                self._copy(j, self._half(ref4, 2 * x + y, c), (cx, cy, c), send_sems, recv_sems).wait_send()
            self._copy(3 + j, self._half(ref4, 2 * cx + cy, c), (x, y, 1 - c), send_sems, recv_sems).wait_send()


class _GradExchange:
    def __init__(self, segs, with_blob, blob_chip_relations=tuple(range(N_CHIPS))):
        self.segs = tuple(segs)
        self.with_blob = with_blob
        self.blob_chip_relations = tuple(blob_chip_relations)

    def landing_w(self):
        return jax.ShapeDtypeStruct((N_CHIPS, 2, D_MODEL // 2, D_MODEL), BF16)

    def landing_blob(self, blob16):
        return jax.ShapeDtypeStruct((N_DEV, blob16.shape[1] // 2, D_MODEL), BF16)

    def semaphores(self):
        n_send = len(self.segs) + (2 * N_CHIPS if self.with_blob else 0)
        n_recv = 2 * N_CHIPS + (N_DEV if self.with_blob else 0)
        return [pltpu.SemaphoreType.DMA((n_send,)), pltpu.SemaphoreType.DMA((n_recv,))]

    def _copies(self, dw_ref, rxw_ref, blob_ref, rxb_ref, send_sems, recv_sems):
        x, y, c = lax.axis_index("x"), lax.axis_index("y"), lax.axis_index("c")
        chip = 2 * x + y

        def relation(kx, ky, h):
            return (x ^ kx) * 4 + (y ^ ky) * 2 + (c ^ h)

        def copy(src, dst, send_k, recv_k, to):
            return functools.partial(pltpu.make_async_remote_copy, src_ref=src, dst_ref=dst,
                                     send_sem=send_sems.at[send_k], recv_sem=recv_sems.at[recv_k],
                                     device_id=to, device_id_type=MESH)

        sends, recvs = [], []
        for i, s in enumerate(self.segs):
            kx, ky = (s // 2) >> 1, (s // 2) & 1
            r = (x ^ kx) * 2 + (y ^ ky)
            sends.append((r != 0, copy(dw_ref.at[i], rxw_ref.at[r, s % 2], i, 2 * r + s % 2, (kx, ky, c))))
        for j in range(2):
            mine = [s // 2 for s in self.segs if s % 2 == j]
            if mine:
                cond = functools.reduce(lambda a, b: a | b, [chip == k for k in mine])
                for r in range(1, N_CHIPS):
                    slot = rxw_ref.at[r, j]
                    recvs.append((cond, copy(slot, slot, 0, 2 * r + j, (x, y, c))))
        if self.with_blob:
            hb = blob_ref.shape[1] // 2
            first_send, first_recv = len(self.segs), 2 * N_CHIPS
            for k in range(N_CHIPS):
                for h in range(2):
                    r = relation(k >> 1, k & 1, h)
                    travels = functools.reduce(lambda a, b: a | b, [r // 2 == q for q in self.blob_chip_relations])
                    sends.append(((r != 0) & travels,
                                  copy(blob_ref.at[k, pl.ds(h * hb, hb), :], rxb_ref.at[r],
                                       first_send + 2 * k + h, first_recv + r, (k >> 1, k & 1, h))))
            for r in range(1, N_DEV):
                if r // 2 in self.blob_chip_relations:
                    slot = rxb_ref.at[r]
                    recvs.append((None, copy(slot, slot, 0, first_recv + r, (x, y, c))))
        return sends, recvs

    def start(self, *refs):
        sends, _ = self._copies(*refs)
        for cond, make in sends:
            pl.when(cond)(lambda make=make: make().start())

    def wait(self, *refs):
        sends, recvs = self._copies(*refs)
        for cond, make in sends:
            pl.when(cond)(lambda make=make: make().wait_send())
        for cond, make in recvs:
            if cond is None:
                make().wait_recv()
            else:
                pl.when(cond)(lambda make=make: make().wait_recv())


def _sum_landed(own, rx_ref):
    total = own
    for r in range(1, rx_ref.shape[0]):
        total = total + rx_ref[r, 0].astype(F32)
    return total


def _finish_w(dw_rec, dw_mix, rx_rec, rx_mix, place_arr):
    half = D_MODEL // 2
    tm = _tile(half, 256)
    n_rec = len(SEGS_REC)

    def body(place_ref, own_rec_ref, own_mix_ref, rx_rec_ref, rx_mix_ref, out_ref):
        seg = 2 * place_ref[0] + pl.program_id(0)

        @pl.when(seg < n_rec)
        def _():
            out_ref[0] = _sum_landed(own_rec_ref[0], rx_rec_ref)

        @pl.when(seg >= n_rec)
        def _():
            out_ref[0] = _sum_landed(own_mix_ref[0], rx_mix_ref)

    def own_spec(first, count):
        def index(j, i, place_ref):
            seg = 2 * place_ref[0] + j
            return (jnp.clip(seg - first, 0, count - 1), i, 0)
        return pl.BlockSpec((1, tm, D_MODEL), index)

    rx_spec = pl.BlockSpec((N_CHIPS, 1, tm, D_MODEL), lambda j, i, place_ref: (0, j, i, 0))
    return pl.pallas_call(
        body, name="finish_w",
        grid_spec=pltpu.PrefetchScalarGridSpec(
            num_scalar_prefetch=1, grid=(2, half // tm),
            in_specs=[own_spec(0, n_rec), own_spec(n_rec, len(SEGS_MIX)), rx_spec, rx_spec],
            out_specs=pl.BlockSpec((1, tm, D_MODEL), lambda j, i, place_ref: (place_ref[1], i, j))),
        out_shape=jax.ShapeDtypeStruct((2, half, 2 * D_MODEL), F32),
        compiler_params=_params(("arbitrary", "arbitrary")),
    )(place_arr, dw_rec, dw_mix, rx_rec, rx_mix)


def _finish_blob(dblob4, rx_blob, place_arr):
    n, rows, cols = rx_blob.shape
    tm = _tile(rows, 256)

    def body(place_ref, own_ref, rx_ref, out_ref):
        out_ref[0] = _sum_landed(own_ref[0, 0], rx_ref)

    return pl.pallas_call(
        body, name="finish_blob",
        grid_spec=pltpu.PrefetchScalarGridSpec(
            num_scalar_prefetch=1, grid=(rows // tm,),
            in_specs=[pl.BlockSpec((1, 1, tm, cols), lambda i, place_ref: (place_ref[0], place_ref[1], i, 0)),
                      pl.BlockSpec((n, 1, tm, cols), lambda i, place_ref: (0, 0, i, 0))],
            out_specs=pl.BlockSpec((1, tm, cols), lambda i, place_ref: (place_ref[1], i, 0))),
        out_shape=jax.ShapeDtypeStruct((2, rows, cols), F32),
        compiler_params=_params(("arbitrary",)),
    )(place_arr, dblob4.reshape(N_CHIPS, 2, rows, cols), rx_blob.reshape(n, 1, rows, cols))


def _share_finished(fw2, fb2, small):
    def body(w_in_ref, b_in_ref, small_ref, w_ref, b_ref, s_ref, bounce, local_sem, send_sems, recv_sems):
        x, y, c, _ = _place()
        sibling = (x, y, 1 - c)

        def copy(k, src, dst, to):
            return pltpu.make_async_remote_copy(src_ref=src, dst_ref=dst, send_sem=send_sems.at[k],
                                                recv_sem=recv_sems.at[k], device_id=to, device_id_type=MESH)

        sends = [copy(0, w_ref.at[c], w_ref.at[c], sibling), copy(1, b_ref.at[c], b_ref.at[c], sibling)]
        for r in range(1, N_DEV):
            peer = (x ^ ((r >> 2) & 1), y ^ ((r >> 1) & 1), c ^ (r & 1))
            sends.append(copy(1 + r, small_ref, s_ref.at[r], peer))
        for cp in sends:
            cp.start()
        for src, dst in ((small_ref, bounce), (bounce, s_ref.at[0])):
            own = pltpu.make_async_copy(src, dst, local_sem)
            own.start()
            own.wait()
        landed = [w_ref.at[1 - c], b_ref.at[1 - c]] + [s_ref.at[r] for r in range(1, N_DEV)]
        for k, slot in enumerate(landed):
            copy(k, slot, slot, (x, y, c)).wait_recv()
        for cp in sends:
            cp.wait_send()

    same = lambda a: jax.ShapeDtypeStruct(a.shape, a.dtype)
    n_sem = 2 + N_DEV - 1
    return pl.pallas_call(
        body, name="share_finished",
        in_specs=[ANY, ANY, ANY], out_specs=[ANY, ANY, ANY],
        out_shape=[same(fw2), same(fb2), jax.ShapeDtypeStruct((N_DEV,) + small.shape, F32)],
        input_output_aliases={0: 0, 1: 1},
        scratch_shapes=[pltpu.VMEM(small.shape, F32), pltpu.SemaphoreType.DMA,
                        pltpu.SemaphoreType.DMA((n_sem,)), pltpu.SemaphoreType.DMA((n_sem,))],
    )(fw2, fb2, small)


def _sum_small(slots, lb_logits, me_arr):
    def body(me_ref, slots_ref, lbl_ref, out_ref):
        me = me_ref[0]
        total = slots_ref[me]
        for d in range(1, N_DEV):
            total = total + slots_ref[d ^ me]
        out_ref[...] = total
        out_ref[ROW_LOSS:ROW_LOSS + 1, :] = jnp.broadcast_to(
            jnp.sum(total[ROW_LOSS:ROW_LOSS + 1, :], axis=-1, keepdims=True), (1, D_MODEL))
        lb = _lower_bound(lbl_ref[...])
        g0 = total[ROW_LB:ROW_LB + 1, :] * lb * (1.0 - lb)
        out_ref[ROW_LB:ROW_LB + 1, :] = g0
        out_ref[ROW_LB + 1:ROW_LB + 2, :] = -g0

    return pl.pallas_call(
        body, name="sum_small",
        grid_spec=pltpu.PrefetchScalarGridSpec(
            num_scalar_prefetch=1, grid=(1,),
            in_specs=[pl.BlockSpec((N_DEV, SMALL_ROWS, D_MODEL), lambda i, me_ref: (0, 0, 0)),
                      pl.BlockSpec((2, D_MODEL), lambda i, me_ref: (0, 0))],
            out_specs=pl.BlockSpec((SMALL_ROWS, D_MODEL), lambda i, me_ref: (0, 0))),
        out_shape=jax.ShapeDtypeStruct((SMALL_ROWS, D_MODEL), F32),
        compiler_params=_params(("arbitrary",)),
    )(me_arr, slots, lb_logits)


def _adamw_step(w, g, m, v):
    c1 = 1.0 / (1.0 - ADAM_B1 ** ADAM_STEP)
    c2 = 1.0 / (1.0 - ADAM_B2 ** ADAM_STEP)
    nm = ADAM_B1 * m + (1.0 - ADAM_B1) * g
    nv = ADAM_B2 * v + (1.0 - ADAM_B2) * (g * g)
    return -ADAM_LR * ((nm * c1) / (jnp.sqrt(nv * c2) + ADAM_EPS) + ADAM_WD * w), nm, nv


SMALL_PARAMS = (("norm_w", ROW_NORM_W, 1), ("b_in", ROW_B_IN, N_SEG), ("lb_logits", ROW_LB, 2),
                ("hg_norm_w", ROW_HG_W, 1), ("pool_scale", ROW_POOL_SCALE, 1), ("final_norm_w", ROW_FINAL_W, 1))


def _update_small(tot, triples):
    n = len(SMALL_PARAMS)

    def body(tot_ref, *refs):
        ins, outs = refs[:3 * n], refs[3 * n:]
        for p, (name, row, n_rows) in enumerate(SMALL_PARAMS):
            w_ref, m_ref, v_ref = ins[3 * p:3 * p + 3]
            g_ref, d_ref, nm_ref, nv_ref = outs[4 * p:4 * p + 4]
            if w_ref.shape[0] == n_rows:
                pieces = [(slice(None), slice(None), tot_ref[row:row + n_rows, :])]
            else:
                pieces = [(slice(None), slice(k * D_MODEL, (k + 1) * D_MODEL), tot_ref[row + k:row + k + 1, :])
                          for k in range(n_rows)]
            for rows_, cols_, g in pieces:
                d, nm, nv = _adamw_step(w_ref[rows_, cols_], g, m_ref[rows_, cols_], v_ref[rows_, cols_])
                g_ref[rows_, cols_] = g
                d_ref[rows_, cols_] = d
                nm_ref[rows_, cols_] = nm
                nv_ref[rows_, cols_] = nv

    whole = pl.BlockSpec(memory_space=pltpu.VMEM)
    flat = [a for t in triples for a in t]
    out_shape = [jax.ShapeDtypeStruct(t[0].shape, F32) for t in triples for _ in range(4)]
    outs = pl.pallas_call(
        body, name="update_small",
        in_specs=[whole] * (1 + len(flat)), out_specs=[whole] * len(out_shape), out_shape=out_shape,
        compiler_params=_params(),
    )(tot, *flat)
    return [tuple(outs[4 * p:4 * p + 4]) for p in range(n)]


def _update_blob(g_blob, triples):
    q_rows = triples[0][0].shape[0]
    pool_rows = triples[3][0].shape[0]
    steps = q_rows // pool_rows

    def body(*refs):
        ins, outs = refs[:16], refs[16:]
        for p in range(4):
            g_ref, (w_ref, m_ref, v_ref) = ins[p], ins[4 + 3 * p:7 + 3 * p]
            go_ref, d_ref, nm_ref, nv_ref = outs[4 * p:4 * p + 4]

            def update():
                g = g_ref[...]
                go_ref[...] = g
                d_ref[...], nm_ref[...], nv_ref[...] = _adamw_step(w_ref[...], g, m_ref[...], v_ref[...])

            if p < 3:
                update()
            else:
                pl.when(pl.program_id(0) == 0)(update)

    blk = pl.BlockSpec((pool_rows, D_MODEL), lambda i: (i, 0))
    once = pl.BlockSpec((pool_rows, D_MODEL), lambda i: (0, 0))
    g_specs = [pl.BlockSpec((pool_rows, D_MODEL), lambda i, p=p: (steps * p + i, 0)) for p in range(3)]
    g_specs.append(pl.BlockSpec((pool_rows, D_MODEL), lambda i: (3 * steps, 0)))
    piece_specs = [blk] * 9 + [once] * 3
    out_specs = [blk] * 12 + [once] * 4
    out_shape = [jax.ShapeDtypeStruct(t[0].shape, F32) for t in triples for _ in range(4)]
    outs = pl.pallas_call(
        body, name="update_blob",
        grid=(steps,), in_specs=g_specs + piece_specs, out_specs=out_specs, out_shape=out_shape,
        compiler_params=_params(("arbitrary",)),
    )(g_blob, g_blob, g_blob, g_blob, *[a for t in triples for a in t])
    return [tuple(outs[4 * p:4 * p + 4]) for p in range(4)]


def _adamw(w, g, m, v):
    rows, cols = w.shape
    tm = _tile(rows, 256, mult=8) if rows % 8 == 0 else rows

    def body(w_ref, g_ref, m_ref, v_ref, d_ref, nm_ref, nv_ref):
        d_ref[...], nm_ref[...], nv_ref[...] = _adamw_step(w_ref[...], g_ref[...], m_ref[...], v_ref[...])

    blk = pl.BlockSpec((tm, cols), lambda i: (i, 0))
    sds = jax.ShapeDtypeStruct((rows, cols), F32)
    return pl.pallas_call(
        body, name="adamw",
        grid=(rows // tm,), in_specs=[blk] * 4, out_specs=[blk] * 3, out_shape=[sds] * 3,
        compiler_params=_params(("arbitrary",)),
    )(w, g, m, v)


def kernel(x, meta_tokens, norm_w, w_in, b_in, lb_logits, hg_norm_w, pool_w, pool_scale, w_down_hg, w_down_pool, w_out, final_norm_w, loss_target, m_meta_tokens, m_norm_w, m_w_in, m_b_in, m_lb_logits, m_hg_norm_w, m_pool_w, m_pool_scale, m_w_down_hg, m_w_down_pool, m_w_out, m_final_norm_w, v_meta_tokens, v_norm_w, v_w_in, v_b_in, v_lb_logits, v_hg_norm_w, v_pool_w, v_pool_scale, v_w_down_hg, v_w_down_pool, v_w_out, v_final_norm_w):
    seq = x.shape[1]
    xi, yi, ci = lax.axis_index("x"), lax.axis_index("y"), lax.axis_index("c")
    chip = 2 * xi + yi
    place_arr = jnp.stack([chip, ci]).astype(jnp.int32)
    me_arr = jnp.reshape(4 * xi + 2 * yi + ci, (1,)).astype(jnp.int32)
    q = D_MODEL // N_CHIPS

    def blob_of(wdh, wdp, wo, pw):
        return jnp.concatenate([wdh[0], wdp[0], wo[0], pw[0].reshape(-1, D_MODEL)], axis=0)

    def in_every_slot(a):
        return jnp.broadcast_to(a[None], (N_CHIPS,) + a.shape)

    m4 = in_every_slot(meta_tokens)
    w4 = in_every_slot(w_in[0].astype(BF16))
    blob4 = in_every_slot(blob_of(w_down_hg, w_down_pool, w_out, pool_w).astype(BF16))
    seg_order = jnp.stack([2 * (chip ^ rel) + t for rel in (0, 2, 1, 3) for t in (0, 1)]).astype(jnp.int32)

    fw2 = final_norm_w.reshape(1, D_MODEL)
    d_tokens, w_parts, blob_parts, small = _local_step(
        x[0], m4, loss_target[0], w4, blob4, seg_order, norm_w, b_in, lb_logits, hg_norm_w, pool_scale, fw2)
    grad_x = d_tokens[None]

    fin_w = _finish_w(*w_parts, place_arr)
    fin_b = _finish_blob(*blob_parts, place_arr)
    gw2, gb2, slots = _share_finished(fin_w, fin_b, small)
    tot = _sum_small(slots, lb_logits, me_arr)
    g_w_in = gw2.reshape(D_MODEL, 2 * D_MODEL)
    g_blob = gb2.reshape(-1, D_MODEL)

    d_win, nm_win, nv_win = _adamw(w_in[0], g_w_in, m_w_in[0], v_w_in[0])
    pool_rows = lambda a: a[0].reshape(-1, D_MODEL)
    blob_results = _update_blob(g_blob, [
        (w_down_hg[0], m_w_down_hg[0], v_w_down_hg[0]), (w_down_pool[0], m_w_down_pool[0], v_w_down_pool[0]),
        (w_out[0], m_w_out[0], v_w_out[0]), (pool_rows(pool_w), pool_rows(m_pool_w), pool_rows(v_pool_w))])
    g_meta = lax.dynamic_slice_in_dim(tot[ROW_META:ROW_META + N_META], chip * q, q, axis=1)
    d_meta, nm_meta, nv_meta = _adamw(meta_tokens, g_meta, m_meta_tokens, v_meta_tokens)

    as_row = lambda a: a.reshape(1, D_MODEL)
    small_results = _update_small(tot, [
        (norm_w, m_norm_w, v_norm_w), (b_in, m_b_in, v_b_in), (lb_logits, m_lb_logits, v_lb_logits),
        (hg_norm_w, m_hg_norm_w, v_hg_norm_w), (pool_scale, m_pool_scale, v_pool_scale),
        (as_row(final_norm_w), as_row(m_final_norm_w), as_row(v_final_norm_w))])

    def leaves(kind, meta_part, win_part):
        nw, bi, lbl, hg, ps, fw = [r[kind] for r in small_results]
        wdh, wdp, wo, pw = [r[kind] for r in blob_results]
        return [meta_part, nw, win_part[None], bi, lbl, hg, pw.reshape(pool_w.shape), ps,
                wdh[None], wdp[None], wo[None], fw.reshape(D_MODEL)]

    loss = tot[ROW_LOSS, 0]
    return (loss, grad_x,
            *leaves(0, g_meta, g_w_in),
            *leaves(1, d_meta, d_win),
            *leaves(2, nm_meta, nm_win),
            *leaves(3, nv_meta, nv_win))
```

```python
import functools

import numpy as np
import jax
import jax.numpy as jnp
from jax import lax
from jax.experimental import pallas as pl
from jax.experimental.pallas import tpu as pltpu

F32 = jnp.float32
BF16 = jnp.bfloat16

D_MODEL = 1024
N_SEG = 8
N_HEADS = 8
HEAD_DIM = 128
CHUNK = 64
N_META = 16
PAD_ROWS = CHUNK - N_META
FIRST_TOKEN_ROW = CHUNK
LEVELS = (32, 16, 8, 4, 2, 1)
N_EXP = 2 + len(LEVELS)
POOL_WINDOWS = (2, 4, 8, 16)
POOL_GDIM = D_MODEL // len(POOL_WINDOWS)
HALO = 16
FORWARD_HEADS_PER_STEP = 4
BACKWARD_HEADS_PER_STEP = 2
LOCAL_UNROLL = 13
BACKWARD_UNROLL = 13
EPS = 1e-6
N_CHIPS = 4
N_DEV = 8
BLOB_RELATIONS_DIRECT = (0, 1, 2)
BLOB_RELATIONS_DIAGONAL = (3,)
SEGS_REC = (0, 1, 2)
SEGS_MIX = (3, 4, 5, 6, 7)

ADAM_LR = 0.001
ADAM_B1 = 0.9
ADAM_B2 = 0.999
ADAM_EPS = 1e-08
ADAM_WD = 0.01
ADAM_STEP = 10

VMEM_LIMIT_BYTES = 56 * 1024 * 1024

ROW_LOSS = 0
ROW_META = 1
ROW_NORM_W = ROW_META + N_META
ROW_B_IN = ROW_NORM_W + 1
ROW_LB = ROW_B_IN + N_SEG
ROW_HG_W = ROW_LB + 2
ROW_POOL_SCALE = ROW_HG_W + 1
ROW_FINAL_W = ROW_POOL_SCALE + 1
SMALL_ROWS = 32


def _tile(total, cap, mult=16):
    best = None
    for t in range(mult, min(total, cap) + 1, mult):
        if total % t == 0:
            best = t
    assert best is not None, (total, cap, mult)
    return best


def _token_window(tm, tile_of):
    def index(*grid):
        return (pl.multiple_of(jnp.maximum(tile_of(*grid) * tm - FIRST_TOKEN_ROW, 0), HALO), 0)
    return pl.BlockSpec((pl.Element(tm), pl.Element(D_MODEL)), index)


def _padded_tile(window, head, tile):
    first = jnp.concatenate([head, pltpu.roll(window, FIRST_TOKEN_ROW, 0)[FIRST_TOKEN_ROW:]], axis=0)
    return jnp.where(tile == 0, first, window)


def _params(sem=None):
    return pltpu.CompilerParams(dimension_semantics=sem, vmem_limit_bytes=VMEM_LIMIT_BYTES)


def _dot(a, b):
    return jnp.dot(a, b, preferred_element_type=F32)


def _dot_nt(a, b):
    return lax.dot_general(a, b, (((1,), (1,)), ((), ())), preferred_element_type=F32)


def _dot_tn(a, b):
    return lax.dot_general(a, b, (((0,), (0,)), ((), ())), preferred_element_type=F32)


def _sigmoid_pair(x):
    t = jnp.exp(-jnp.abs(x))
    r = 1.0 / (1.0 + t)
    pos = x >= 0
    return jnp.where(pos, r, t * r), jnp.where(pos, t * r, r)


def _exponent_matrix():
    t = np.arange(CHUNK)[:, None]
    j = np.arange(CHUNK)[None, :]
    blocks = [j <= t, j > t]
    for m in LEVELS:
        rho = (t // (2 * m)) * (2 * m) + m
        upper = (t >= rho) & (j > rho) & (j <= t)
        lower = (t < rho) & (j > t) & (j <= rho)
        blocks.append(upper | lower)
    return np.concatenate(blocks, axis=0).astype(np.float32)


def _pair_masks():
    t = np.arange(CHUNK)[:, None]
    s = np.arange(CHUNK)[None, :]
    masks = [t == s]
    for m in LEVELS:
        same = (t // (2 * m)) == (s // (2 * m))
        masks.append(same & ((t % (2 * m)) >= m) & ((s % (2 * m)) < m))
    return np.stack(masks).astype(np.float32)


LEVEL_PAIRS = ((0, 1), (2, 3), (4, 5), (6, None))


def _paired_masks():
    m = _pair_masks()
    zero = np.zeros_like(m[0])
    return np.stack([np.concatenate([m[a], zero if b is None else m[b]], axis=1) for a, b in LEVEL_PAIRS])


def _lower_bound(lbl):
    return 1.0 / (1.0 + jnp.exp(lbl[1:2, :] - lbl[0:1, :]))


def _in_proj(tokens, m4, norm_w, w4, b_in, seg_order, rows):
    tm = _tile(rows, 1040)
    nt = rows // tm
    gather = _ShardGather(w4.shape[1])

    def body(order_ref, z_ref, nw_ref, b_ref, w_in_ref, m_in_ref, h_ref, p_ref, w4_ref, head_ref, m4_ref,
             h_all, w_buf, w_sem, send_sems, recv_sems, meta_send, meta_recv, meta_sem):
        kk, i = pl.program_id(0), pl.program_id(1)

        @pl.when((kk == 0) & (i == 0))
        def _():
            x, y, c, chips = _place()

            def meta_copy(j, chip, to):
                return pltpu.make_async_remote_copy(
                    src_ref=m4_ref.at[chip], dst_ref=m4_ref.at[chip], send_sem=meta_send.at[j],
                    recv_sem=meta_recv.at[j], device_id=to, device_id_type=MESH)

            sends = [meta_copy(j, 2 * x + y, (cx, cy, c)) for j, (cx, cy) in enumerate(chips)]
            for cp in sends:
                cp.start()
            gather.start(w4_ref, send_sems, recv_sems, which=(0, 1))
            for j, (cx, cy) in enumerate(chips):
                meta_copy(j, 2 * cx + cy, (x, y, c)).wait_recv()
            for cp in sends:
                cp.wait_send()
            head_ref[0:PAD_ROWS, :] = jnp.zeros((PAD_ROWS, D_MODEL), F32)
            q_cols = D_MODEL // N_CHIPS
            for k in range(N_CHIPS):
                cp = pltpu.make_async_copy(
                    m4_ref.at[k], head_ref.at[pl.ds(PAD_ROWS, N_META), pl.ds(k * q_cols, q_cols)], meta_sem)
                cp.start()
                cp.wait()

        @pl.when((kk == 2) & (i == 0))
        def _():
            gather.start_diagonal_after_neighbours(w4_ref, send_sems, recv_sems)

        @pl.when(kk == 0)
        def _():
            zt = _padded_tile(z_ref[...], head_ref[...], i)
            rstd = lax.rsqrt(jnp.mean(zt * zt, axis=-1, keepdims=True) + EPS)
            h = (zt * rstd * nw_ref[...]).astype(BF16)
            h_all[pl.ds(pl.multiple_of(i * tm, 16), tm), :] = h
            h_ref[...] = h

        @pl.when((kk == 2) & (i == 0))
        def _():
            gather.pass_on(0, w4_ref, send_sems, recv_sems)
            gather.pass_on(1, w4_ref, send_sems, recv_sems)
            gather.await_sibling(0, w4_ref, send_sems, recv_sems)

        @pl.when((kk == 4) & (i == 0))
        def _():
            gather.await_sibling(1, w4_ref, send_sems, recv_sems)

        @pl.when((kk == 5) & (i == 0))
        def _():
            gather.pass_on(2, w4_ref, send_sems, recv_sems)

        @pl.when((kk == 6) & (i == 0))
        def _():
            gather.await_sibling(2, w4_ref, send_sems, recv_sems)

        def weights(which):
            seg = order_ref[2 * (kk // 2) + which]
            return pltpu.make_async_copy(
                w4_ref.at[seg // 2, :, pl.ds(pl.multiple_of((seg % 2) * D_MODEL, D_MODEL), D_MODEL)],
                w_buf.at[which], w_sem.at[which])

        @pl.when((i == 0) & (kk % 2 == 0))
        def _():
            weights(0).start()
            weights(1).start()
            weights(0).wait()

        @pl.when((i == 0) & (kk % 2 == 1))
        def _():
            weights(1).wait()

        p_ref[0] = _dot(h_all[pl.ds(pl.multiple_of(i * tm, 16), tm), :], w_buf[kk % 2]) + b_ref[...]

        @pl.when((kk == N_SEG - 1) & (i == nt - 1))
        def _():
            gather.finish(w4_ref, send_sems, recv_sems, which=(2,))

    first_pass = lambda kk, i, order_ref: (jnp.where(kk == 0, i, nt - 1), 0)
    return pl.pallas_call(
        body, name="in_proj",
        grid_spec=pltpu.PrefetchScalarGridSpec(
            num_scalar_prefetch=1, grid=(N_SEG, nt),
            in_specs=[
                _token_window(tm, lambda kk, i, order_ref: jnp.where(kk == 0, i, nt - 1)),
                pl.BlockSpec((1, D_MODEL), lambda kk, i, order_ref: (0, 0)),
                pl.BlockSpec((1, D_MODEL), lambda kk, i, order_ref: (0, order_ref[kk])),
                ANY, ANY,
            ],
            out_specs=[
                pl.BlockSpec((tm, D_MODEL), first_pass),
                pl.BlockSpec((1, tm, D_MODEL), lambda kk, i, order_ref: (order_ref[kk], i, 0)),
                ANY,
                pl.BlockSpec((FIRST_TOKEN_ROW, D_MODEL), lambda kk, i, order_ref: (0, 0)),
                ANY,
            ],
            scratch_shapes=[
                pltpu.VMEM((rows, D_MODEL), BF16),
                pltpu.VMEM((2, D_MODEL, D_MODEL), BF16),
                pltpu.SemaphoreType.DMA((2,)),
            ] + gather.semaphores() + [
                pltpu.SemaphoreType.DMA((N_CHIPS - 1,)), pltpu.SemaphoreType.DMA((N_CHIPS - 1,)),
                pltpu.SemaphoreType.DMA,
            ]),
        out_shape=[
            jax.ShapeDtypeStruct((rows, D_MODEL), BF16),
            jax.ShapeDtypeStruct((N_SEG, rows, D_MODEL), F32),
            jax.ShapeDtypeStruct(w4.shape, w4.dtype),
            jax.ShapeDtypeStruct((FIRST_TOKEN_ROW, D_MODEL), F32),
            jax.ShapeDtypeStruct(m4.shape, m4.dtype),
        ],
        input_output_aliases={4: 2, 5: 4},
        compiler_params=_params(("arbitrary", "arbitrary")),
    )(seg_order, tokens, norm_w, b_in, w4, m4)


def _hgrn_forward(p3, lb_logits, wexp2, masks2, blob4, rows):
    n_chunks = rows // CHUNK
    cpb = _tile(n_chunks, 13, mult=1)
    rb_rows = cpb * CHUNK
    n_rb = n_chunks // cpb
    lanes = cpb * HEAD_DIM
    hps = FORWARD_HEADS_PER_STEP
    n_hb = N_HEADS // hps
    width = hps * HEAD_DIM
    gather = _ShardGather(blob4.shape[1])

    def body(q_ref, fz_ref, v_ref, lbl_ref, wexp_ref, mask_ref, b_in_ref, o_ref, s_ref, e16_ref, a2_ref, b4_ref,
             st_all, e_all, u_all, q_all, kk_all, v_all, send_sems, recv_sems):
        rb = pl.program_id(1)

        @pl.when((pl.program_id(0) == 0) & (rb == 0))
        def _():
            gather.start(b4_ref, send_sems, recv_sems)

        @pl.when(rb == 0)
        def _():
            st_all[...] = jnp.zeros_like(st_all)

        for j in range(hps):
            cols = pl.ds(j * HEAD_DIM, HEAD_DIM)
            one_head(rb, q_ref.at[0, :, cols], fz_ref.at[0, :, cols], v_ref.at[0, :, cols], lbl_ref.at[:, cols],
                     wexp_ref, mask_ref, o_ref.at[:, cols], s_ref.at[j], e16_ref.at[j, 0], a2_ref.at[:, cols],
                     st_all.at[j], e_all.at[j], u_all.at[j], q_all.at[j], kk_all.at[j], v_all.at[j])

        @pl.when((pl.program_id(0) == n_hb // 2) & (rb == 0))
        def _():
            for j in range(N_CHIPS - 1):
                gather.pass_on(j, b4_ref, send_sems, recv_sems)

        @pl.when((pl.program_id(0) == n_hb - 1) & (rb == n_rb - 1))
        def _():
            for j in range(N_CHIPS - 1):
                gather.await_sibling(j, b4_ref, send_sems, recv_sems)
            gather.finish(b4_ref, send_sems, recv_sems)

    def one_head(rb, q_ref, fz_ref, v_ref, lbl_ref, wexp_ref, mask_ref, o_ref, s_ref, e16_ref, a2_ref,
                 st_ref, e_ref, u_ref, q_s, kk_s, v_s):
        lb = _lower_bound(lbl_ref[...])
        row = rb * rb_rows + lax.broadcasted_iota(jnp.int32, (rb_rows, 1), 0)
        valid = row >= PAD_ROWS
        sg, sn = _sigmoid_pair(fz_ref[...])
        g = jnp.where(valid, jnp.log(lb + (1.0 - lb) * sg), 0.0)
        kk_s[...] = jnp.where(valid, (1.0 - lb) * sn, 0.0)
        q_s[...] = jnp.where(valid, q_ref[...], 0.0)
        v_s[...] = jnp.where(valid, v_ref[...], 0.0).astype(BF16)
        hi = g.astype(BF16)
        mid = (g - hi.astype(F32)).astype(BF16)
        g2 = jnp.concatenate(
            [jnp.concatenate([hi[b * CHUNK:(b + 1) * CHUNK], mid[b * CHUNK:(b + 1) * CHUNK]], axis=0)
             for b in range(cpb)], axis=1)
        e_ref[...] = jnp.exp(_dot(wexp_ref[...], g2))
        e16_ref[...] = e_ref[...].astype(BF16)

        def contribution(b, carry):
            r0 = pl.multiple_of(b * CHUNK, CHUNK)
            l0 = pl.multiple_of(b * HEAD_DIM, HEAD_DIM)
            kc16 = (kk_s[pl.ds(r0, CHUNK), :] * e_ref[CHUNK:2 * CHUNK, pl.ds(l0, HEAD_DIM)]).astype(BF16)
            u_ref[b] = _dot_tn(v_s[pl.ds(r0, CHUNK), :], kc16)
            return carry

        lax.fori_loop(0, cpb, contribution, 0, unroll=LOCAL_UNROLL)

        def recur(b, st):
            l0 = pl.multiple_of(b * HEAD_DIM, HEAD_DIM)
            s_ref[b] = st
            return st * e_ref[CHUNK - 1:CHUNK, pl.ds(l0, HEAD_DIM)] + u_ref[b]

        st_ref[...] = lax.fori_loop(0, cpb, recur, st_ref[...], unroll=LOCAL_UNROLL)

        zeros16 = jnp.zeros((CHUNK, HEAD_DIM), BF16)

        def local(b, carry):
            r0 = pl.multiple_of(b * CHUNK, CHUNK)
            l0 = pl.multiple_of(b * HEAD_DIM, HEAD_DIM)
            q = q_s[pl.ds(r0, CHUNK), :]
            kk = kk_s[pl.ds(r0, CHUNK), :]
            v16 = v_s[pl.ds(r0, CHUNK), :]

            def scaled(entry):
                if entry == 0:
                    return q.astype(BF16), kk.astype(BF16)
                e_m = e_ref[(1 + entry) * CHUNK:(2 + entry) * CHUNK, pl.ds(l0, HEAD_DIM)]
                return (q * e_m).astype(BF16), (kk * e_m).astype(BF16)

            a2 = jnp.zeros((CHUNK, 2 * CHUNK), F32)
            for p, (ea, eb) in enumerate(LEVEL_PAIRS):
                qa, ka = scaled(ea)
                if eb is None:
                    prod = _dot_nt(qa, jnp.concatenate([ka, zeros16], axis=0))
                else:
                    qb_, kb_ = scaled(eb)
                    rhs = jnp.concatenate([jnp.concatenate([ka, zeros16], axis=1),
                                           jnp.concatenate([zeros16, kb_], axis=1)], axis=0)
                    prod = _dot_nt(jnp.concatenate([qa, qb_], axis=1), rhs)
                a2 = a2 + mask_ref[p] * prod
            a2_16 = a2.astype(BF16)
            a2_ref[pl.ds(r0, CHUNK), :] = a2_16
            qb16 = (q * e_ref[0:CHUNK, pl.ds(l0, HEAD_DIM)]).astype(BF16)
            o_ref[pl.ds(r0, CHUNK), :] = (_dot(a2_16, jnp.concatenate([v16, v16], axis=0))
                                          + _dot_nt(qb16, s_ref[b].astype(BF16)))
            return carry

        lax.fori_loop(0, cpb, local, 0, unroll=LOCAL_UNROLL)

    head_block = lambda seg: pl.BlockSpec((1, rb_rows, width), lambda h, r: (seg, r, h))
    return pl.pallas_call(
        body, name="hgrn_forward",
        grid=(n_hb, n_rb),
        in_specs=[
            head_block(0), head_block(1), head_block(2),
            pl.BlockSpec((2, width), lambda h, r: (0, h)),
            pl.BlockSpec((N_EXP * CHUNK, 2 * CHUNK), lambda h, r: (0, 0)),
            pl.BlockSpec((len(LEVEL_PAIRS), CHUNK, 2 * CHUNK), lambda h, r: (0, 0, 0)),
            ANY,
        ],
        out_specs=[
            pl.BlockSpec((rb_rows, width), lambda h, r: (r, h)),
            pl.BlockSpec((hps, cpb, HEAD_DIM, HEAD_DIM), lambda h, r: (h, r, 0, 0)),
            pl.BlockSpec((hps, 1, N_EXP * CHUNK, lanes), lambda h, r: (h, r, 0, 0)),
            pl.BlockSpec((rb_rows, width), lambda h, r: (r, h)),
            ANY,
        ],
        out_shape=[
            jax.ShapeDtypeStruct((rows, D_MODEL), F32),
            jax.ShapeDtypeStruct((N_HEADS, n_chunks, HEAD_DIM, HEAD_DIM), F32),
            jax.ShapeDtypeStruct((N_HEADS, n_rb, N_EXP * CHUNK, lanes), BF16),
            jax.ShapeDtypeStruct((rows, D_MODEL), BF16),
            jax.ShapeDtypeStruct(blob4.shape, blob4.dtype),
        ],
        input_output_aliases={6: 4},
        scratch_shapes=[
            pltpu.VMEM((hps, HEAD_DIM, HEAD_DIM), F32),
            pltpu.VMEM((hps, N_EXP * CHUNK, lanes), F32),
            pltpu.VMEM((hps, cpb, HEAD_DIM, HEAD_DIM), F32),
            pltpu.VMEM((hps, rb_rows, HEAD_DIM), F32),
            pltpu.VMEM((hps, rb_rows, HEAD_DIM), F32),
            pltpu.VMEM((hps, rb_rows, HEAD_DIM), BF16),
        ] + gather.semaphores(),
        compiler_params=_params(("arbitrary", "arbitrary")),
    )(p3, p3, p3, lb_logits, wexp2, masks2, blob4)


def _hgrn_backward(p3, d_o, states, e16, a2, lb_logits, wexp_t, masks2, dw16, blob16, rx_blob, rows):
    n_chunks = rows // CHUNK
    cpb = _tile(n_chunks, 13, mult=1)
    rb_rows = cpb * CHUNK
    n_rb = n_chunks // cpb
    lanes = cpb * HEAD_DIM
    exchange = _GradExchange(SEGS_MIX, True, BLOB_RELATIONS_DIAGONAL)

    hps = BACKWARD_HEADS_PER_STEP
    n_hb = N_HEADS // hps
    width = hps * HEAD_DIM

    def body(q_ref, fz_ref, v_ref, do_ref, s_ref, e_ref, a2_ref, lbl_ref, wexpt_ref, mask_ref, dw_ref, blob_ref,
             rxb_in_ref, dp_ref, dlb_ref, rxw_ref, rxb_ref, *scratch):
        per_head, (send_sems, recv_sems) = scratch[:-2], scratch[-2:]
        step = pl.program_id(1)
        rb = n_rb - 1 - step

        @pl.when((pl.program_id(0) == 0) & (step == 0))
        def _():
            exchange.start(dw_ref, rxw_ref, blob_ref, rxb_ref, send_sems, recv_sems)

        for j in range(hps):
            cols = pl.ds(j * HEAD_DIM, HEAD_DIM)
            one_head(step, rb, q_ref.at[0, :, cols], fz_ref.at[0, :, cols], v_ref.at[0, :, cols], do_ref.at[:, cols],
                     s_ref.at[j], e_ref.at[j, 0], a2_ref.at[:, cols], lbl_ref.at[:, cols], wexpt_ref, mask_ref,
                     dp_ref.at[:, :, cols], dlb_ref.at[:, cols], *[ref.at[j] for ref in per_head])

        @pl.when((pl.program_id(0) == n_hb - 1) & (step == n_rb - 1))
        def _():
            exchange.wait(dw_ref, rxw_ref, blob_ref, rxb_ref, send_sems, recv_sems)

    def one_head(step, rb, q_ref, fz_ref, v_ref, do_ref, s_ref, e_ref, a2_ref, lbl_ref, wexpt_ref, mask_ref,
                 dp_ref, dlb_ref, dst_ref, g_ref, dsn_ref, q_s, kk_s, v_s, do_s, dq_s, dkk_s, dg_s, dx_s, da2_s):
        @pl.when(step == 0)
        def _():
            dst_ref[...] = jnp.zeros_like(dst_ref)
            dlb_ref[...] = jnp.zeros_like(dlb_ref)

        lb = _lower_bound(lbl_ref[...])
        row = rb * rb_rows + lax.broadcasted_iota(jnp.int32, (rb_rows, 1), 0)
        valid = row >= PAD_ROWS
        sg, sn = _sigmoid_pair(fz_ref[...])
        f = lb + (1.0 - lb) * sg
        g = jnp.where(valid, jnp.log(f), 0.0)
        kk_s[...] = jnp.where(valid, (1.0 - lb) * sn, 0.0)
        q_s[...] = jnp.where(valid, q_ref[...], 0.0)
        v_s[...] = jnp.where(valid, v_ref[...], 0.0).astype(BF16)
        do_s[...] = do_ref[...].astype(BF16)
        e_last_all = jnp.exp(jnp.concatenate(
            [jnp.sum(g[b * CHUNK:(b + 1) * CHUNK], axis=0, keepdims=True) for b in range(cpb)], axis=0))
        last_row = lax.broadcasted_iota(jnp.int32, (CHUNK, 1), 0) == CHUNK - 1
        zeros16 = jnp.zeros((CHUNK, HEAD_DIM), BF16)

        def factor(block, l0):
            return e_ref[block * CHUNK:(block + 1) * CHUNK, pl.ds(l0, HEAD_DIM)].astype(F32)

        def contribution(b, carry):
            r0 = pl.multiple_of(b * CHUNK, CHUNK)
            l0 = pl.multiple_of(b * HEAD_DIM, HEAD_DIM)
            qb16 = (q_s[pl.ds(r0, CHUNK), :] * factor(0, l0)).astype(BF16)
            g_ref[b] = _dot_tn(do_s[pl.ds(r0, CHUNK), :], qb16)
            return carry

        lax.fori_loop(0, cpb, contribution, 0, unroll=LOCAL_UNROLL)

        cur = dst_ref[...]
        for b in reversed(range(cpb)):
            dsn_ref[b] = cur
            cur = cur * e_last_all[b:b + 1, :] + g_ref[b]
        dst_ref[...] = cur

        def through_state(b, carry):
            r0 = pl.multiple_of(b * CHUNK, CHUNK)
            l0 = pl.multiple_of(b * HEAD_DIM, HEAD_DIM)
            v16 = v_s[pl.ds(r0, CHUNK), :]
            do16 = do_s[pl.ds(r0, CHUNK), :]
            st = s_ref[b]
            dsn = dsn_ref[b]
            dsn16 = dsn.astype(BF16)
            e_b, e_c = factor(0, l0), factor(1, l0)
            qb, kc = q_s[pl.ds(r0, CHUNK), :] * e_b, kk_s[pl.ds(r0, CHUNK), :] * e_c

            t = _dot_tn(a2_ref[pl.ds(r0, CHUNK), :], do16)
            dv = t[0:CHUNK] + t[CHUNK:2 * CHUNK] + _dot_nt(kc.astype(BF16), dsn16)
            dp_ref[2, pl.ds(r0, CHUNK), :] = dv.astype(BF16)
            da2_s[pl.ds(r0, CHUNK), :] = _dot_nt(do16, jnp.concatenate([v16, v16], axis=0))
            dqb = _dot(do16, st.astype(BF16))
            dkc = _dot(v16, dsn16)
            de = jnp.sum(dsn * st, axis=0, keepdims=True) * e_b[CHUNK - 1:CHUNK, :]
            dq_s[pl.ds(r0, CHUNK), :] = e_b * dqb
            dkk_s[pl.ds(r0, CHUNK), :] = e_c * dkc
            dx_s[0:CHUNK, pl.ds(l0, HEAD_DIM)] = (qb * dqb + jnp.where(last_row, de, 0.0)).astype(BF16)
            dx_s[CHUNK:2 * CHUNK, pl.ds(l0, HEAD_DIM)] = (kc * dkc).astype(BF16)
            return carry

        lax.fori_loop(0, cpb, through_state, 0, unroll=BACKWARD_UNROLL)

        def local(b, carry):
            r0 = pl.multiple_of(b * CHUNK, CHUNK)
            l0 = pl.multiple_of(b * HEAD_DIM, HEAD_DIM)
            q = q_s[pl.ds(r0, CHUNK), :]
            kk = kk_s[pl.ds(r0, CHUNK), :]
            da2 = da2_s[pl.ds(r0, CHUNK), :]
            dq = dq_s[pl.ds(r0, CHUNK), :]
            dkk = dkk_s[pl.ds(r0, CHUNK), :]

            def scaled(entry):
                if entry == 0:
                    return q, kk, None
                e_m = factor(1 + entry, l0)
                return q * e_m, kk * e_m, e_m

            for p, (ea, eb) in enumerate(LEVEL_PAIRS):
                dm = mask_ref[p] * da2
                dm_t = dm.T.astype(BF16)
                qa, ka, e_a = scaled(ea)
                if eb is None:
                    rhs_k = jnp.concatenate([jnp.concatenate([ka.astype(BF16), zeros16], axis=1),
                                             jnp.concatenate([zeros16, zeros16], axis=1)], axis=0)
                else:
                    qb_, kb_, e_bb = scaled(eb)
                    rhs_k = jnp.concatenate([jnp.concatenate([ka.astype(BF16), zeros16], axis=1),
                                             jnp.concatenate([zeros16, kb_.astype(BF16)], axis=1)], axis=0)
                dq2 = _dot(dm.astype(BF16), rhs_k)
                parts = [(ea, qa, ka, e_a, dq2[:, :HEAD_DIM], _dot(dm_t[0:CHUNK], qa.astype(BF16)))]
                if eb is not None:
                    parts.append((eb, qb_, kb_, e_bb, dq2[:, HEAD_DIM:],
                                  _dot(dm_t[CHUNK:2 * CHUNK], qb_.astype(BF16))))
                for entry, q_m, k_m, e_m, dq_m, dk_m in parts:
                    if entry == 0:
                        dq = dq + dq_m
                        dkk = dkk + dk_m
                    else:
                        dq = dq + e_m * dq_m
                        dkk = dkk + e_m * dk_m
                        dx_s[(1 + entry) * CHUNK:(2 + entry) * CHUNK, pl.ds(l0, HEAD_DIM)] = (
                            q_m * dq_m + k_m * dk_m).astype(BF16)
            dq_s[pl.ds(r0, CHUNK), :] = dq
            dkk_s[pl.ds(r0, CHUNK), :] = dkk
            return carry

        lax.fori_loop(0, cpb, local, 0, unroll=BACKWARD_UNROLL)

        dg_all = _dot(wexpt_ref[...], dx_s[...])
        for b in range(cpb):
            dg_s[b * CHUNK:(b + 1) * CHUNK, :] = dg_all[:, b * HEAD_DIM:(b + 1) * HEAD_DIM]
        t = jnp.where(valid, dg_s[...] / f - dkk_s[...], 0.0)
        dlb_ref[...] += jnp.sum(sn * t, axis=0, keepdims=True)
        dp_ref[0] = jnp.where(valid, dq_s[...], 0.0).astype(BF16)
        dp_ref[1] = ((1.0 - lb) * sg * sn * t).astype(BF16)

    head_block = lambda seg: pl.BlockSpec((1, rb_rows, width), lambda h, s: (seg, n_rb - 1 - s, h))
    row_block = pl.BlockSpec((rb_rows, width), lambda h, s: (n_rb - 1 - s, h))
    return pl.pallas_call(
        body, name="hgrn_backward",
        grid=(n_hb, n_rb),
        in_specs=[
            head_block(0), head_block(1), head_block(2),
            row_block,
            pl.BlockSpec((hps, cpb, HEAD_DIM, HEAD_DIM), lambda h, s: (h, n_rb - 1 - s, 0, 0)),
            pl.BlockSpec((hps, 1, N_EXP * CHUNK, lanes), lambda h, s: (h, n_rb - 1 - s, 0, 0)),
            row_block,
            pl.BlockSpec((2, width), lambda h, s: (0, h)),
            pl.BlockSpec((CHUNK, N_EXP * CHUNK), lambda h, s: (0, 0)),
            pl.BlockSpec((len(LEVEL_PAIRS), CHUNK, 2 * CHUNK), lambda h, s: (0, 0, 0)),
            ANY, ANY, ANY,
        ],
        out_specs=[
            pl.BlockSpec((3, rb_rows, width), lambda h, s: (0, n_rb - 1 - s, h)),
            pl.BlockSpec((1, width), lambda h, s: (0, h)),
            ANY, ANY,
        ],
        out_shape=[
            jax.ShapeDtypeStruct((3, rows, D_MODEL), BF16),
            jax.ShapeDtypeStruct((1, D_MODEL), F32),
            exchange.landing_w(), jax.ShapeDtypeStruct(rx_blob.shape, rx_blob.dtype),
        ],
        input_output_aliases={12: 3},
        scratch_shapes=[
            pltpu.VMEM((hps, HEAD_DIM, HEAD_DIM), F32),
            pltpu.VMEM((hps, cpb, HEAD_DIM, HEAD_DIM), F32),
            pltpu.VMEM((hps, cpb, HEAD_DIM, HEAD_DIM), F32),
            pltpu.VMEM((hps, rb_rows, HEAD_DIM), F32),
            pltpu.VMEM((hps, rb_rows, HEAD_DIM), F32),
            pltpu.VMEM((hps, rb_rows, HEAD_DIM), BF16),
            pltpu.VMEM((hps, rb_rows, HEAD_DIM), BF16),
            pltpu.VMEM((hps, rb_rows, HEAD_DIM), F32),
            pltpu.VMEM((hps, rb_rows, HEAD_DIM), F32),
            pltpu.VMEM((hps, rb_rows, HEAD_DIM), F32),
            pltpu.VMEM((hps, N_EXP * CHUNK, lanes), BF16),
            pltpu.VMEM((hps, rb_rows, 2 * CHUNK), F32),
        ] + exchange.semaphores(),
        compiler_params=_params(("arbitrary", "arbitrary")),
    )(p3, p3, p3, d_o, states, e16, a2, lb_logits, wexp_t, masks2, dw16, blob16, rx_blob)


def _sigmoid(x):
    return 1.0 / (1.0 + jnp.exp(-x))


def _silu_and_grad(x):
    s = _sigmoid(x)
    return x * s, s * (1.0 + x * (1.0 - s))


def _window_sum(ext, width, forward_looking):
    n = ext.shape[0]
    s = ext
    step = 1
    while step < width:
        s = s + pltpu.roll(s, (n - step) if forward_looking else step, 0)
        step *= 2
    return s


def _mixers(o, p3, tokens, head, tgt, wdh, wdp, wout, poolw, hg_w, pool_scale, final_w, rows):
    tm = _tile(rows, 208)
    nt = rows // tm
    halo_blocks = tm // HALO
    n_grp = len(POOL_WINDOWS)
    q_rows = D_MODEL // N_CHIPS
    blob_rows = 3 * q_rows + n_grp * POOL_GDIM * POOL_GDIM // (N_CHIPS * D_MODEL)

    def body(o_ref, ghg_ref, u_ref, gpl_ref, mhg_ref, mpl_ref, uh_ref, z_ref, t_ref,
             wdh_ref, wdp_ref, wout_ref, pw_ref, hgw_ref, ps_ref, fw_ref, head_ref,
             do_ref, dz2_ref, dp_ref, blob_ref, dpw_ref, small_ref, carry_ref):
        step = pl.program_id(0)
        tile = nt - 1 - step

        def add_to_blob(piece, dw):
            for k in range(N_CHIPS):
                blob_ref[k, piece * q_rows:(piece + 1) * q_rows, :] += dw[k * q_rows:(k + 1) * q_rows]

        @pl.when(step == 0)
        def _():
            blob_ref[...] = jnp.zeros_like(blob_ref)
            dpw_ref[...] = jnp.zeros_like(dpw_ref)
            small_ref[...] = jnp.zeros_like(small_ref)
            carry_ref[...] = jnp.zeros_like(carry_ref)

        row = tile * tm + lax.broadcasted_iota(jnp.int32, (tm, 1), 0)
        real = row >= PAD_ROWS
        pos1 = jnp.maximum(row - PAD_ROWS + 1, 1).astype(F32)

        u = jnp.where(real, u_ref[0], 0.0)
        halo_row = tile * tm - HALO + lax.broadcasted_iota(jnp.int32, (HALO, 1), 0)
        uh = jnp.where(halo_row >= PAD_ROWS, uh_ref[0], 0.0)
        ext = jnp.concatenate([uh, u], axis=0)
        pooled, inv_cnt, mixed = [], [], []
        for g, w in enumerate(POOL_WINDOWS):
            cols = slice(g * POOL_GDIM, (g + 1) * POOL_GDIM)
            inv = 1.0 / jnp.minimum(pos1, float(w))
            ws = _window_sum(ext[:, cols], w, False)[HALO:]
            pg = (ws * inv - u[:, cols]).astype(BF16)
            pooled.append(pg)
            inv_cnt.append(inv)
            mixed.append(_dot(pg, pw_ref[g]))
        mixed = jnp.concatenate(mixed, axis=1)
        gpl = gpl_ref[0]
        sp, dsp = _silu_and_grad(gpl)
        ps = ps_ref[...]
        a_pool = (mixed * ps * sp).astype(BF16)
        y_pool = _dot(a_pool, wdp_ref[...])

        o = o_ref[...]
        o_hat, rstd_h = [], []
        for h in range(N_HEADS):
            oh = o[:, h * HEAD_DIM:(h + 1) * HEAD_DIM]
            r = lax.rsqrt(jnp.mean(oh * oh, axis=-1, keepdims=True) + EPS)
            rstd_h.append(r)
            o_hat.append(oh * r)
        o_hat = jnp.concatenate(o_hat, axis=1)
        hgw = hgw_ref[...]
        o_n = o_hat * hgw
        ghg = ghg_ref[0]
        sh, dsh = _silu_and_grad(ghg)
        a_hg = (o_n * sh).astype(BF16)
        y_hg = _dot(a_hg, wdh_ref[...])

        s_mh = _sigmoid(mhg_ref[0])
        s_mp = _sigmoid(mpl_ref[0])
        merged = (s_mh * y_hg + s_mp * y_pool).astype(BF16)
        z2 = _padded_tile(z_ref[...], head_ref[...], tile) + _dot(merged, wout_ref[...])
        rstd2 = lax.rsqrt(jnp.mean(z2 * z2, axis=-1, keepdims=True) + EPS)
        zh = z2 * rstd2
        fw = fw_ref[...]
        target = _padded_tile(t_ref[...], jnp.zeros((FIRST_TOKEN_ROW, D_MODEL), F32), tile)
        err = jnp.where(row >= FIRST_TOKEN_ROW, zh * fw - target, 0.0)
        small_ref[ROW_LOSS:ROW_LOSS + 1, :] += jnp.sum(err * err, axis=0, keepdims=True) * (0.5 / D_MODEL)
        dy = err * (1.0 / D_MODEL)

        small_ref[ROW_FINAL_W:ROW_FINAL_W + 1, :] += jnp.sum(dy * zh, axis=0, keepdims=True)
        uu = dy * fw
        dz2 = rstd2 * (uu - zh * jnp.mean(uu * zh, axis=-1, keepdims=True))
        dz2_ref[...] = dz2
        dz2_16 = dz2.astype(BF16)
        dmerged = _dot_nt(dz2_16, wout_ref[...])
        add_to_blob(2, _dot_tn(merged, dz2_16))
        dy_hg = (s_mh * dmerged).astype(BF16)
        dy_pool = (s_mp * dmerged).astype(BF16)
        dp_ref[3] = (dmerged * y_hg * s_mh * (1.0 - s_mh)).astype(BF16)
        dp_ref[4] = (dmerged * y_pool * s_mp * (1.0 - s_mp)).astype(BF16)

        da_hg = _dot_nt(dy_hg, wdh_ref[...])
        add_to_blob(0, _dot_tn(a_hg, dy_hg))
        dp_ref[0] = (da_hg * o_n * dsh).astype(BF16)
        do_n = da_hg * sh
        small_ref[ROW_HG_W:ROW_HG_W + 1, :] += jnp.sum(do_n * o_hat, axis=0, keepdims=True)
        d_hat = do_n * hgw
        for h in range(N_HEADS):
            cols = slice(h * HEAD_DIM, (h + 1) * HEAD_DIM)
            dh_, oh_ = d_hat[:, cols], o_hat[:, cols]
            do_ref[:, cols] = rstd_h[h] * (dh_ - oh_ * jnp.mean(dh_ * oh_, axis=-1, keepdims=True))

        da_pool = _dot_nt(dy_pool, wdp_ref[...])
        add_to_blob(1, _dot_tn(a_pool, dy_pool))
        small_ref[ROW_POOL_SCALE:ROW_POOL_SCALE + 1, :] += jnp.sum(da_pool * mixed * sp, axis=0, keepdims=True)
        dp_ref[2] = (da_pool * mixed * ps * dsp).astype(BF16)
        dmixed = (da_pool * ps * sp).astype(BF16)
        carry = carry_ref[...]
        du, new_carry = [], []
        for g, w in enumerate(POOL_WINDOWS):
            cols = slice(g * POOL_GDIM, (g + 1) * POOL_GDIM)
            dmg = dmixed[:, cols]
            dpooled = _dot_nt(dmg, pw_ref[g])
            dpw_ref[g] += _dot_tn(pooled[g], dmg)
            dps = dpooled * inv_cnt[g]
            ext_b = jnp.concatenate([dps, carry[:, cols]], axis=0)
            du.append(_window_sum(ext_b, w, True)[:tm] - dpooled)
            new_carry.append(dps[:HALO])
        dp_ref[1] = jnp.where(real, jnp.concatenate(du, axis=1), 0.0).astype(BF16)
        carry_ref[...] = jnp.concatenate(new_carry, axis=1)

    row_block = pl.BlockSpec((tm, D_MODEL), lambda s: (nt - 1 - s, 0))
    seg_block = lambda seg: pl.BlockSpec((1, tm, D_MODEL), lambda s: (seg, nt - 1 - s, 0))
    whole = pl.BlockSpec(memory_space=pltpu.VMEM)
    return pl.pallas_call(
        body, name="mixers",
        grid=(nt,),
        in_specs=[
            row_block, seg_block(3), seg_block(4), seg_block(5), seg_block(6), seg_block(7),
            pl.BlockSpec((1, HALO, D_MODEL),
                         lambda s: (4, jnp.maximum((nt - 1 - s) * halo_blocks - 1, 0), 0)),
            _token_window(tm, lambda s: nt - 1 - s), _token_window(tm, lambda s: nt - 1 - s),
            whole, whole, whole, whole, whole, whole, whole, whole,
        ],
        out_specs=[
            row_block, row_block,
            pl.BlockSpec((5, tm, D_MODEL), lambda s: (0, nt - 1 - s, 0)),
            whole, whole, whole,
        ],
        out_shape=[
            jax.ShapeDtypeStruct((rows, D_MODEL), F32),
            jax.ShapeDtypeStruct((rows, D_MODEL), F32),
            jax.ShapeDtypeStruct((5, rows, D_MODEL), BF16),
            jax.ShapeDtypeStruct((N_CHIPS, blob_rows, D_MODEL), F32),
            jax.ShapeDtypeStruct((n_grp, POOL_GDIM, POOL_GDIM), F32),
            jax.ShapeDtypeStruct((SMALL_ROWS, D_MODEL), F32),
        ],
        scratch_shapes=[pltpu.VMEM((HALO, D_MODEL), F32)],
        compiler_params=_params(("arbitrary",)),
    )(o, p3, p3, p3, p3, p3, p3, tokens, tgt, wdh, wdp, wout, poolw, hg_w, pool_scale, final_w, head)


def _seg_specs(tm, row_of, seg_of):
    def spec_a(*g):
        k = seg_of(*g)
        return (jnp.minimum(k, 2), jnp.where(k < 3, row_of(*g), 0), 0)

    def spec_b(*g):
        k = seg_of(*g)
        return (jnp.maximum(k - 3, 0), jnp.where(k >= 3, row_of(*g), 0), 0)

    return pl.BlockSpec((1, tm, D_MODEL), spec_a), pl.BlockSpec((1, tm, D_MODEL), spec_b)


def _in_proj_weight_grad(h, dp, rows, name, blob16=None):
    n_seg = dp.shape[0]
    tm = _tile(rows, 1040)
    nt = rows // tm
    half = D_MODEL // 2
    exchange = _GradExchange((), True, BLOB_RELATIONS_DIRECT) if blob16 is not None else None

    def body(*refs):
        if exchange is None:
            (h_ref, dp_ref, part_ref, part16_ref, db_ref, acc_ref, bacc_ref, stage_ref, land_ref,
             send_sems, recv_sems) = refs
        else:
            (h_ref, dp_ref, blob_ref, part_ref, part16_ref, db_ref, rxb_ref, acc_ref, bacc_ref, stage_ref, land_ref,
             send_sems, recv_sems, blob_send, blob_recv) = refs
        k, i = pl.program_id(0), pl.program_id(1)
        x, y, c = lax.axis_index("x"), lax.axis_index("y"), lax.axis_index("c")

        if exchange is not None:
            @pl.when((k == 0) & (i == 0))
            def _():
                exchange.start(None, None, blob_ref, rxb_ref, blob_send, blob_recv)

        def to_sibling(seg):
            return pltpu.make_async_remote_copy(
                src_ref=stage_ref.at[seg], dst_ref=land_ref.at[seg], send_sem=send_sems.at[seg],
                recv_sem=recv_sems.at[seg], device_id=(x, y, 1 - c), device_id_type=MESH)

        @pl.when(i == 0)
        def _():
            acc_ref[...] = jnp.zeros_like(acc_ref)
            bacc_ref[...] = jnp.zeros_like(bacc_ref)

        dpt = dp_ref[0]
        acc_ref[...] += _dot_tn(h_ref[...], dpt)
        bacc_ref[...] += jnp.sum(dpt.astype(F32), axis=0, keepdims=True)

        @pl.when(i == nt - 1)
        def _():
            db_ref[0] = bacc_ref[...]
            part_ref[k] = acc_ref[pl.ds(pl.multiple_of(c * half, half), half), :]
            stage_ref[k] = acc_ref[pl.ds(pl.multiple_of((1 - c) * half, half), half), :].astype(BF16)
            to_sibling(k).start()

        @pl.when((k == n_seg - 1) & (i == nt - 1))
        def _():
            for seg in range(n_seg):
                to_sibling(seg).wait_recv()
                total = part_ref[seg] + land_ref[seg].astype(F32)
                part_ref[seg] = total
                part16_ref[seg] = total.astype(BF16)
            for seg in range(n_seg):
                to_sibling(seg).wait_send()
            if exchange is not None:
                exchange.wait(None, None, blob_ref, rxb_ref, blob_send, blob_recv)

    whole = pl.BlockSpec(memory_space=pltpu.VMEM)
    with_blob = exchange is not None
    return pl.pallas_call(
        body, name=name,
        grid=(n_seg, nt),
        in_specs=[pl.BlockSpec((tm, D_MODEL), lambda k, i: (i, 0)),
                  pl.BlockSpec((1, tm, D_MODEL), lambda k, i: (k, i, 0))] + [ANY] * with_blob,
        out_specs=[whole, whole, pl.BlockSpec((1, 1, D_MODEL), lambda k, i: (k, 0, 0))] + [ANY] * with_blob,
        out_shape=[
            jax.ShapeDtypeStruct((n_seg, half, D_MODEL), F32),
            jax.ShapeDtypeStruct((n_seg, half, D_MODEL), BF16),
            jax.ShapeDtypeStruct((n_seg, 1, D_MODEL), F32),
        ] + ([exchange.landing_blob(blob16)] if with_blob else []),
        scratch_shapes=[
            pltpu.VMEM((D_MODEL, D_MODEL), F32), pltpu.VMEM((1, D_MODEL), F32),
            pltpu.VMEM((n_seg, half, D_MODEL), BF16),
            pltpu.VMEM((n_seg, half, D_MODEL), BF16),
            pltpu.SemaphoreType.DMA((n_seg,)), pltpu.SemaphoreType.DMA((n_seg,)),
        ] + (exchange.semaphores() if with_blob else []),
        compiler_params=_params(("arbitrary", "arbitrary")),
    )(h, dp, *([blob16] if with_blob else []))


def _input_grad(dpa, dpb, w4, tokens, head, dz2, norm_w, dw16, rows):
    tm = _tile(rows, 1040)
    nt = rows // tm
    assert nt >= 2, rows
    exchange = _GradExchange(SEGS_REC, with_blob=False)

    def body(dpa_ref, dpb_ref, w_ref, z_ref, head_ref, dz2_ref, nw_ref, dw_ref, gx_ref, dmeta_ref, dnw_ref, rxw_ref,
             acc_ref, dz_buf, out_sem, send_sems, recv_sems):
        i, k = pl.program_id(0), pl.program_id(1)

        def first_tile_out():
            return pltpu.make_async_copy(dz_buf.at[pl.ds(FIRST_TOKEN_ROW, tm - FIRST_TOKEN_ROW), :],
                                         gx_ref.at[pl.ds(0, tm - FIRST_TOKEN_ROW), :], out_sem)

        def tile_out(tile):
            start = pl.multiple_of(tile * tm - FIRST_TOKEN_ROW, HALO)
            return pltpu.make_async_copy(dz_buf, gx_ref.at[pl.ds(start, tm), :], out_sem)

        @pl.when((i == 0) & (k == 0))
        def _():
            exchange.start(dw_ref, rxw_ref, None, None, send_sems, recv_sems)
            dnw_ref[...] = jnp.zeros_like(dnw_ref)

        @pl.when((i == nt - 1) & (k == N_SEG - 1))
        def _():
            exchange.wait(dw_ref, rxw_ref, None, None, send_sems, recv_sems)

        @pl.when(k == 0)
        def _():
            acc_ref[...] = jnp.zeros_like(acc_ref)

        @pl.when(k < 3)
        def _():
            acc_ref[...] += _dot_nt(dpa_ref[0], w_ref[0])

        @pl.when(k >= 3)
        def _():
            acc_ref[...] += _dot_nt(dpb_ref[0], w_ref[0])

        @pl.when(k == N_SEG - 1)
        def _():
            zt = _padded_tile(z_ref[...], head_ref[...], i)
            rstd = lax.rsqrt(jnp.mean(zt * zt, axis=-1, keepdims=True) + EPS)
            zh = zt * rstd
            dh = acc_ref[...]
            dnw_ref[...] += jnp.sum(dh * zh, axis=0, keepdims=True)
            uu = dh * nw_ref[...]
            dz = dz2_ref[...] + rstd * (uu - zh * jnp.mean(uu * zh, axis=-1, keepdims=True))

            @pl.when(i == 1)
            def _():
                first_tile_out().wait()

            @pl.when(i >= 2)
            def _():
                tile_out(i - 1).wait()

            dz_buf[...] = dz

            @pl.when(i == 0)
            def _():
                dmeta_ref[...] = dz[PAD_ROWS:FIRST_TOKEN_ROW]
                first_tile_out().start()

            @pl.when(i > 0)
            def _():
                tile_out(i).start()

            @pl.when(i == nt - 1)
            def _():
                tile_out(i).wait()

    spec_a, spec_b = _seg_specs(tm, lambda i, k: i, lambda i, k: k)
    last_only = pl.BlockSpec((tm, D_MODEL), lambda i, k: (jnp.where(k == N_SEG - 1, i, 0), 0))
    return pl.pallas_call(
        body, name="input_grad",
        grid=(nt, N_SEG),
        in_specs=[
            spec_a, spec_b,
            pl.BlockSpec((1, D_MODEL, D_MODEL), lambda i, k: (k // 2, 0, k % 2)),
            _token_window(tm, lambda i, k: jnp.where(k == N_SEG - 1, i, 0)),
            pl.BlockSpec((FIRST_TOKEN_ROW, D_MODEL), lambda i, k: (0, 0)),
            last_only,
            pl.BlockSpec((1, D_MODEL), lambda i, k: (0, 0)),
            ANY,
        ],
        out_specs=[
            ANY,
            pl.BlockSpec((N_META, D_MODEL), lambda i, k: (0, 0)),
            pl.BlockSpec((1, D_MODEL), lambda i, k: (0, 0)),
            ANY,
        ],
        out_shape=[
            jax.ShapeDtypeStruct((rows - FIRST_TOKEN_ROW, D_MODEL), F32),
            jax.ShapeDtypeStruct((N_META, D_MODEL), F32),
            jax.ShapeDtypeStruct((1, D_MODEL), F32),
            exchange.landing_w(),
        ],
        scratch_shapes=[pltpu.VMEM((tm, D_MODEL), F32), pltpu.VMEM((tm, D_MODEL), F32),
                        pltpu.SemaphoreType.DMA] + exchange.semaphores(),
        compiler_params=_params(("arbitrary", "arbitrary")),
    )(dpa, dpb, w4, tokens, head, dz2, norm_w, dw16)


def _local_step(tokens, m4, tgt, w4, blob4, seg_order, norm_w, b_in, lb_logits, hg_w, pool_scale, final_w):
    rows = FIRST_TOKEN_ROW + tokens.shape[0]
    q = D_MODEL // N_CHIPS
    n_grp = len(POOL_WINDOWS)
    pg = POOL_GDIM // N_CHIPS

    wexp2 = jnp.asarray(np.tile(_exponent_matrix(), (1, 2)), BF16)
    wexp_t = jnp.asarray(_exponent_matrix().T, BF16)
    masks2 = jnp.asarray(_paired_masks(), F32)

    h, p3, w4, head, _ = _in_proj(tokens, m4, norm_w, w4, b_in, seg_order, rows)
    o, states, e16, a2, blob4 = _hgrn_forward(p3, lb_logits, wexp2, masks2, blob4, rows)
    wdh = blob4[:, 0:q].reshape(D_MODEL, D_MODEL)
    wdp = blob4[:, q:2 * q].reshape(D_MODEL, D_MODEL)
    wout = blob4[:, 2 * q:3 * q].reshape(D_MODEL, D_MODEL)
    poolw = blob4[:, 3 * q:].reshape(N_CHIPS, n_grp, pg, POOL_GDIM).transpose(1, 0, 2, 3)
    poolw = poolw.reshape(n_grp, POOL_GDIM, POOL_GDIM)
    d_o, dz2, dpb, dblob4, dpw, small = _mixers(
        o, p3, tokens, head, tgt, wdh, wdp, wout, poolw, hg_w, pool_scale, final_w, rows)
    dpw4 = dpw.reshape(n_grp, N_CHIPS, pg, POOL_GDIM).transpose(1, 0, 2, 3)
    dpw4 = dpw4.reshape(N_CHIPS, n_grp * pg * POOL_GDIM // D_MODEL, D_MODEL)
    dblob4 = dblob4.at[:, 3 * q:, :].set(dpw4)

    blob16 = dblob4.astype(BF16)
    dw_mix, dw_mix16, db_mix, rx_blob = _in_proj_weight_grad(h, dpb, rows, "in_proj_weight_grad_mix", blob16)
    dpa, dlb, rxw_mix, rx_blob = _hgrn_backward(
        p3, d_o, states, e16, a2, lb_logits, wexp_t, masks2, dw_mix16, blob16, rx_blob, rows)
    dw_rec, dw_rec16, db_rec = _in_proj_weight_grad(h, dpa, rows, "in_proj_weight_grad_rec")
    d_tokens, d_meta, dnw, rxw_rec = _input_grad(dpa, dpb, w4, tokens, head, dz2, norm_w, dw_rec16, rows)

    small_parts = (small, d_meta, dnw, db_rec, db_mix, dlb)
    return d_tokens, (dw_rec, dw_mix, rxw_rec, rxw_mix), (dblob4, rx_blob), small_parts


ANY = pl.BlockSpec(memory_space=pl.ANY)
MESH = pl.DeviceIdType.MESH


def _place():
    x, y, c = lax.axis_index("x"), lax.axis_index("y"), lax.axis_index("c")
    chips = [(1 - x, y), (x, 1 - y), (1 - x, 1 - y)]
    return x, y, c, chips


class _ShardGather:
    def __init__(self, rows):
        self.half = rows // 2

    def semaphores(self):
        return [pltpu.SemaphoreType.DMA((6,)), pltpu.SemaphoreType.DMA((6,))]

    def _copy(self, k, slot, to, send_sems, recv_sems):
        return pltpu.make_async_remote_copy(src_ref=slot, dst_ref=slot, send_sem=send_sems.at[k],
                                            recv_sem=recv_sems.at[k], device_id=to, device_id_type=MESH)

    def _half(self, ref4, chip, which):
        return ref4.at[chip, pl.ds(which * self.half, self.half), :]

    def start(self, ref4, send_sems, recv_sems, which=(0, 1, 2)):
        x, y, c, chips = _place()
        for j in which:
            cx, cy = chips[j]
            self._copy(j, self._half(ref4, 2 * x + y, c), (cx, cy, c), send_sems, recv_sems).start()

    def start_diagonal_after_neighbours(self, ref4, send_sems, recv_sems):
        x, y, c, chips = _place()
        for j in (0, 1):
            cx, cy = chips[j]
            self._copy(j, self._half(ref4, 2 * x + y, c), (cx, cy, c), send_sems, recv_sems).wait_send()
        self.start(ref4, send_sems, recv_sems, which=(2,))

    def pass_on(self, j, ref4, send_sems, recv_sems):
        x, y, c, chips = _place()
        cx, cy = chips[j]
        landed = self._half(ref4, 2 * cx + cy, c)
        self._copy(j, landed, (cx, cy, c), send_sems, recv_sems).wait_recv()
        self._copy(3 + j, landed, (x, y, 1 - c), send_sems, recv_sems).start()

    def await_sibling(self, j, ref4, send_sems, recv_sems):
        x, y, c, chips = _place()
        cx, cy = chips[j]
        self._copy(3 + j, self._half(ref4, 2 * cx + cy, 1 - c), (x, y, 1 - c), send_sems, recv_sems).wait_recv()

    def finish(self, ref4, send_sems, recv_sems, which=(0, 1, 2)):
        x, y, c, chips = _place()
        for j, (cx, cy) in enumerate(chips):
            if j in which:
                self._copy(j, self._half(ref4, 2 * x + y, c), (cx, cy, c), send_sems, recv_sems).wait_send()
            self._copy(3 + j, self._half(ref4, 2 * cx + cy, c), (x, y, 1 - c), send_sems, recv_sems).wait_send()


class _GradExchange:
    def __init__(self, segs, with_blob, blob_chip_relations=tuple(range(N_CHIPS))):
        self.segs = tuple(segs)
        self.with_blob = with_blob
        self.blob_chip_relations = tuple(blob_chip_relations)

    def landing_w(self):
        return jax.ShapeDtypeStruct((N_CHIPS, 2, D_MODEL // 2, D_MODEL), BF16)

    def landing_blob(self, blob16):
        return jax.ShapeDtypeStruct((N_DEV, blob16.shape[1] // 2, D_MODEL), BF16)

    def semaphores(self):
        n_send = len(self.segs) + (2 * N_CHIPS if self.with_blob else 0)
        n_recv = 2 * N_CHIPS + (N_DEV if self.with_blob else 0)
        return [pltpu.SemaphoreType.DMA((n_send,)), pltpu.SemaphoreType.DMA((n_recv,))]

    def _copies(self, dw_ref, rxw_ref, blob_ref, rxb_ref, send_sems, recv_sems):
        x, y, c = lax.axis_index("x"), lax.axis_index("y"), lax.axis_index("c")
        chip = 2 * x + y

        def relation(kx, ky, h):
            return (x ^ kx) * 4 + (y ^ ky) * 2 + (c ^ h)

        def copy(src, dst, send_k, recv_k, to):
            return functools.partial(pltpu.make_async_remote_copy, src_ref=src, dst_ref=dst,
                                     send_sem=send_sems.at[send_k], recv_sem=recv_sems.at[recv_k],
                                     device_id=to, device_id_type=MESH)

        sends, recvs = [], []
        for i, s in enumerate(self.segs):
            kx, ky = (s // 2) >> 1, (s // 2) & 1
            r = (x ^ kx) * 2 + (y ^ ky)
            sends.append((r != 0, copy(dw_ref.at[i], rxw_ref.at[r, s % 2], i, 2 * r + s % 2, (kx, ky, c))))
        for j in range(2):
            mine = [s // 2 for s in self.segs if s % 2 == j]
            if mine:
                cond = functools.reduce(lambda a, b: a | b, [chip == k for k in mine])
                for r in range(1, N_CHIPS):
                    slot = rxw_ref.at[r, j]
                    recvs.append((cond, copy(slot, slot, 0, 2 * r + j, (x, y, c))))
        if self.with_blob:
            hb = blob_ref.shape[1] // 2
            first_send, first_recv = len(self.segs), 2 * N_CHIPS
            for k in range(N_CHIPS):
                for h in range(2):
                    r = relation(k >> 1, k & 1, h)
                    travels = functools.reduce(lambda a, b: a | b, [r // 2 == q for q in self.blob_chip_relations])
                    sends.append(((r != 0) & travels,
                                  copy(blob_ref.at[k, pl.ds(h * hb, hb), :], rxb_ref.at[r],
                                       first_send + 2 * k + h, first_recv + r, (k >> 1, k & 1, h))))
            for r in range(1, N_DEV):
                if r // 2 in self.blob_chip_relations:
                    slot = rxb_ref.at[r]
                    recvs.append((None, copy(slot, slot, 0, first_recv + r, (x, y, c))))
        return sends, recvs

    def start(self, *refs):
        sends, _ = self._copies(*refs)
        for cond, make in sends:
            pl.when(cond)(lambda make=make: make().start())

    def wait(self, *refs):
        sends, recvs = self._copies(*refs)
        for cond, make in sends:
            pl.when(cond)(lambda make=make: make().wait_send())
        for cond, make in recvs:
            if cond is None:
                make().wait_recv()
            else:
                pl.when(cond)(lambda make=make: make().wait_recv())


def _sum_landed(own, rx_ref):
    total = own
    for r in range(1, rx_ref.shape[0]):
        total = total + rx_ref[r, 0].astype(F32)
    return total


def _finish_w(dw_rec, dw_mix, rx_rec, rx_mix, place_arr):
    half = D_MODEL // 2
    tm = _tile(half, 256)
    n_rec = len(SEGS_REC)

    def body(place_ref, own_rec_ref, own_mix_ref, rx_rec_ref, rx_mix_ref, out_ref):
        seg = 2 * place_ref[0] + pl.program_id(0)

        @pl.when(seg < n_rec)
        def _():
            out_ref[0] = _sum_landed(own_rec_ref[0], rx_rec_ref)

        @pl.when(seg >= n_rec)
        def _():
            out_ref[0] = _sum_landed(own_mix_ref[0], rx_mix_ref)

    def own_spec(first, count):
        def index(j, i, place_ref):
            seg = 2 * place_ref[0] + j
            return (jnp.clip(seg - first, 0, count - 1), i, 0)
        return pl.BlockSpec((1, tm, D_MODEL), index)

    rx_spec = pl.BlockSpec((N_CHIPS, 1, tm, D_MODEL), lambda j, i, place_ref: (0, j, i, 0))
    return pl.pallas_call(
        body, name="finish_w",
        grid_spec=pltpu.PrefetchScalarGridSpec(
            num_scalar_prefetch=1, grid=(2, half // tm),
            in_specs=[own_spec(0, n_rec), own_spec(n_rec, len(SEGS_MIX)), rx_spec, rx_spec],
            out_specs=pl.BlockSpec((1, tm, D_MODEL), lambda j, i, place_ref: (place_ref[1], i, j))),
        out_shape=jax.ShapeDtypeStruct((2, half, 2 * D_MODEL), F32),
        compiler_params=_params(("arbitrary", "arbitrary")),
    )(place_arr, dw_rec, dw_mix, rx_rec, rx_mix)


def _finish_blob(dblob4, rx_blob, place_arr):
    n, rows, cols = rx_blob.shape
    tm = _tile(rows, 256)

    def body(place_ref, own_ref, rx_ref, out_ref):
        out_ref[0] = _sum_landed(own_ref[0, 0], rx_ref)

    return pl.pallas_call(
        body, name="finish_blob",
        grid_spec=pltpu.PrefetchScalarGridSpec(
            num_scalar_prefetch=1, grid=(rows // tm,),
            in_specs=[pl.BlockSpec((1, 1, tm, cols), lambda i, place_ref: (place_ref[0], place_ref[1], i, 0)),
                      pl.BlockSpec((n, 1, tm, cols), lambda i, place_ref: (0, 0, i, 0))],
            out_specs=pl.BlockSpec((1, tm, cols), lambda i, place_ref: (place_ref[1], i, 0))),
        out_shape=jax.ShapeDtypeStruct((2, rows, cols), F32),
        compiler_params=_params(("arbitrary",)),
    )(place_arr, dblob4.reshape(N_CHIPS, 2, rows, cols), rx_blob.reshape(n, 1, rows, cols))


def _share_finished(fw2, fb2, small_parts):
    def body(w_in_ref, b_in_ref, mix_ref, dmeta_ref, dnw_ref, dbrec_ref, dbmix_ref, dlb_ref, w_ref, b_ref, s_ref,
             rows_ref, local_sem, send_sems, recv_sems):
        x, y, c, _ = _place()
        sibling = (x, y, 1 - c)

        def copy(k, src, dst, to):
            return pltpu.make_async_remote_copy(src_ref=src, dst_ref=dst, send_sem=send_sems.at[k],
                                                recv_sem=recv_sems.at[k], device_id=to, device_id_type=MESH)

        sends = [copy(0, w_ref.at[c], w_ref.at[c], sibling), copy(1, b_ref.at[c], b_ref.at[c], sibling)]
        for cp in sends:
            cp.start()
        rows_ref[...] = mix_ref[...]
        rows_ref[ROW_META:ROW_META + N_META, :] = dmeta_ref[...]
        rows_ref[ROW_NORM_W:ROW_NORM_W + 1, :] = dnw_ref[...]
        for k, seg in enumerate(SEGS_REC):
            rows_ref[ROW_B_IN + seg:ROW_B_IN + seg + 1, :] = dbrec_ref[k]
        for k, seg in enumerate(SEGS_MIX):
            rows_ref[ROW_B_IN + seg:ROW_B_IN + seg + 1, :] = dbmix_ref[k]
        rows_ref[ROW_LB:ROW_LB + 1, :] = dlb_ref[...]
        for r in range(1, N_DEV):
            peer = (x ^ ((r >> 2) & 1), y ^ ((r >> 1) & 1), c ^ (r & 1))
            sends.append(copy(1 + r, rows_ref, s_ref.at[r], peer))
            sends[-1].start()
        own = pltpu.make_async_copy(rows_ref, s_ref.at[0], local_sem)
        own.start()
        own.wait()
        landed = [w_ref.at[1 - c], b_ref.at[1 - c]] + [s_ref.at[r] for r in range(1, N_DEV)]
        for k, slot in enumerate(landed):
            copy(k, slot, slot, (x, y, c)).wait_recv()
        for cp in sends:
            cp.wait_send()

    same = lambda a: jax.ShapeDtypeStruct(a.shape, a.dtype)
    n_sem = 2 + N_DEV - 1
    whole = pl.BlockSpec(memory_space=pltpu.VMEM)
    return pl.pallas_call(
        body, name="share_finished",
        in_specs=[ANY, ANY] + [whole] * len(small_parts), out_specs=[ANY, ANY, ANY],
        out_shape=[same(fw2), same(fb2), jax.ShapeDtypeStruct((N_DEV, SMALL_ROWS, D_MODEL), F32)],
        input_output_aliases={0: 0, 1: 1},
        scratch_shapes=[pltpu.VMEM((SMALL_ROWS, D_MODEL), F32), pltpu.SemaphoreType.DMA,
                        pltpu.SemaphoreType.DMA((n_sem,)), pltpu.SemaphoreType.DMA((n_sem,))],
    )(fw2, fb2, *small_parts)


def _sum_small(slots, lb_logits, me_arr):
    def body(me_ref, slots_ref, lbl_ref, out_ref):
        me = me_ref[0]
        total = slots_ref[me]
        for d in range(1, N_DEV):
            total = total + slots_ref[d ^ me]
        out_ref[...] = total
        out_ref[ROW_LOSS:ROW_LOSS + 1, :] = jnp.broadcast_to(
            jnp.sum(total[ROW_LOSS:ROW_LOSS + 1, :], axis=-1, keepdims=True), (1, D_MODEL))
        lb = _lower_bound(lbl_ref[...])
        g0 = total[ROW_LB:ROW_LB + 1, :] * lb * (1.0 - lb)
        out_ref[ROW_LB:ROW_LB + 1, :] = g0
        out_ref[ROW_LB + 1:ROW_LB + 2, :] = -g0

    return pl.pallas_call(
        body, name="sum_small",
        grid_spec=pltpu.PrefetchScalarGridSpec(
            num_scalar_prefetch=1, grid=(1,),
            in_specs=[pl.BlockSpec((N_DEV, SMALL_ROWS, D_MODEL), lambda i, me_ref: (0, 0, 0)),
                      pl.BlockSpec((2, D_MODEL), lambda i, me_ref: (0, 0))],
            out_specs=pl.BlockSpec((SMALL_ROWS, D_MODEL), lambda i, me_ref: (0, 0))),
        out_shape=jax.ShapeDtypeStruct((SMALL_ROWS, D_MODEL), F32),
        compiler_params=_params(("arbitrary",)),
    )(me_arr, slots, lb_logits)


def _adamw_step(w, g, m, v):
    c1 = 1.0 / (1.0 - ADAM_B1 ** ADAM_STEP)
    c2 = 1.0 / (1.0 - ADAM_B2 ** ADAM_STEP)
    nm = ADAM_B1 * m + (1.0 - ADAM_B1) * g
    nv = ADAM_B2 * v + (1.0 - ADAM_B2) * (g * g)
    return -ADAM_LR * ((nm * c1) / (jnp.sqrt(nv * c2) + ADAM_EPS) + ADAM_WD * w), nm, nv


SMALL_PARAMS = (("norm_w", ROW_NORM_W, 1), ("b_in", ROW_B_IN, N_SEG), ("lb_logits", ROW_LB, 2),
                ("hg_norm_w", ROW_HG_W, 1), ("pool_scale", ROW_POOL_SCALE, 1), ("final_norm_w", ROW_FINAL_W, 1))


def _update_small(tot, triples):
    n = len(SMALL_PARAMS)

    def body(tot_ref, *refs):
        ins, outs = refs[:3 * n], refs[3 * n:]
        for p, (name, row, n_rows) in enumerate(SMALL_PARAMS):
            w_ref, m_ref, v_ref = ins[3 * p:3 * p + 3]
            g_ref, d_ref, nm_ref, nv_ref = outs[4 * p:4 * p + 4]
            if w_ref.shape[0] == n_rows:
                pieces = [(slice(None), slice(None), tot_ref[row:row + n_rows, :])]
            else:
                pieces = [(slice(None), slice(k * D_MODEL, (k + 1) * D_MODEL), tot_ref[row + k:row + k + 1, :])
                          for k in range(n_rows)]
            for rows_, cols_, g in pieces:
                d, nm, nv = _adamw_step(w_ref[rows_, cols_], g, m_ref[rows_, cols_], v_ref[rows_, cols_])
                g_ref[rows_, cols_] = g
                d_ref[rows_, cols_] = d
                nm_ref[rows_, cols_] = nm
                nv_ref[rows_, cols_] = nv

    whole = pl.BlockSpec(memory_space=pltpu.VMEM)
    flat = [a for t in triples for a in t]
    out_shape = [jax.ShapeDtypeStruct(t[0].shape, F32) for t in triples for _ in range(4)]
    outs = pl.pallas_call(
        body, name="update_small",
        in_specs=[whole] * (1 + len(flat)), out_specs=[whole] * len(out_shape), out_shape=out_shape,
        compiler_params=_params(),
    )(tot, *flat)
    return [tuple(outs[4 * p:4 * p + 4]) for p in range(n)]


def _update_blob(g_blob, triples):
    q_rows = triples[0][0].shape[0]
    pool_rows = triples[3][0].shape[0]
    steps = q_rows // pool_rows

    def body(*refs):
        ins, outs = refs[:16], refs[16:]
        for p in range(4):
            g_ref, (w_ref, m_ref, v_ref) = ins[p], ins[4 + 3 * p:7 + 3 * p]
            go_ref, d_ref, nm_ref, nv_ref = outs[4 * p:4 * p + 4]

            def update():
                g = g_ref[...]
                go_ref[...] = g
                d_ref[...], nm_ref[...], nv_ref[...] = _adamw_step(w_ref[...], g, m_ref[...], v_ref[...])

            if p < 3:
                update()
            else:
                pl.when(pl.program_id(0) == 0)(update)

    blk = pl.BlockSpec((pool_rows, D_MODEL), lambda i: (i, 0))
    once = pl.BlockSpec((pool_rows, D_MODEL), lambda i: (0, 0))
    g_specs = [pl.BlockSpec((pool_rows, D_MODEL), lambda i, p=p: (steps * p + i, 0)) for p in range(3)]
    g_specs.append(pl.BlockSpec((pool_rows, D_MODEL), lambda i: (3 * steps, 0)))
    piece_specs = [blk] * 9 + [once] * 3
    out_specs = [blk] * 12 + [once] * 4
    out_shape = [jax.ShapeDtypeStruct(t[0].shape, F32) for t in triples for _ in range(4)]
    outs = pl.pallas_call(
        body, name="update_blob",
        grid=(steps,), in_specs=g_specs + piece_specs, out_specs=out_specs, out_shape=out_shape,
        compiler_params=_params(("arbitrary",)),
    )(g_blob, g_blob, g_blob, g_blob, *[a for t in triples for a in t])
    return [tuple(outs[4 * p:4 * p + 4]) for p in range(4)]


def _adamw(w, g, m, v):
    rows, cols = w.shape
    tm = _tile(rows, 256, mult=8) if rows % 8 == 0 else rows

    def body(w_ref, g_ref, m_ref, v_ref, d_ref, nm_ref, nv_ref):
        d_ref[...], nm_ref[...], nv_ref[...] = _adamw_step(w_ref[...], g_ref[...], m_ref[...], v_ref[...])

    blk = pl.BlockSpec((tm, cols), lambda i: (i, 0))
    sds = jax.ShapeDtypeStruct((rows, cols), F32)
    return pl.pallas_call(
        body, name="adamw",
        grid=(rows // tm,), in_specs=[blk] * 4, out_specs=[blk] * 3, out_shape=[sds] * 3,
        compiler_params=_params(("arbitrary",)),
    )(w, g, m, v)


def kernel(x, meta_tokens, norm_w, w_in, b_in, lb_logits, hg_norm_w, pool_w, pool_scale, w_down_hg, w_down_pool, w_out, final_norm_w, loss_target, m_meta_tokens, m_norm_w, m_w_in, m_b_in, m_lb_logits, m_hg_norm_w, m_pool_w, m_pool_scale, m_w_down_hg, m_w_down_pool, m_w_out, m_final_norm_w, v_meta_tokens, v_norm_w, v_w_in, v_b_in, v_lb_logits, v_hg_norm_w, v_pool_w, v_pool_scale, v_w_down_hg, v_w_down_pool, v_w_out, v_final_norm_w):
    seq = x.shape[1]
    xi, yi, ci = lax.axis_index("x"), lax.axis_index("y"), lax.axis_index("c")
    chip = 2 * xi + yi
    place_arr = jnp.stack([chip, ci]).astype(jnp.int32)
    me_arr = jnp.reshape(4 * xi + 2 * yi + ci, (1,)).astype(jnp.int32)
    q = D_MODEL // N_CHIPS

    def blob_of(wdh, wdp, wo, pw):
        return jnp.concatenate([wdh[0], wdp[0], wo[0], pw[0].reshape(-1, D_MODEL)], axis=0)

    def in_every_slot(a):
        return jnp.broadcast_to(a[None], (N_CHIPS,) + a.shape)

    m4 = in_every_slot(meta_tokens)
    w4 = in_every_slot(w_in[0].astype(BF16))
    blob4 = in_every_slot(blob_of(w_down_hg, w_down_pool, w_out, pool_w).astype(BF16))
    seg_order = jnp.stack([2 * (chip ^ rel) + t for rel in (0, 2, 1, 3) for t in (0, 1)]).astype(jnp.int32)

    fw2 = final_norm_w.reshape(1, D_MODEL)
    d_tokens, w_parts, blob_parts, small = _local_step(
        x[0], m4, loss_target[0], w4, blob4, seg_order, norm_w, b_in, lb_logits, hg_norm_w, pool_scale, fw2)
    grad_x = d_tokens[None]

    fin_w = _finish_w(*w_parts, place_arr)
    fin_b = _finish_blob(*blob_parts, place_arr)
    gw2, gb2, slots = _share_finished(fin_w, fin_b, small)
    tot = _sum_small(slots, lb_logits, me_arr)
    g_w_in = gw2.reshape(D_MODEL, 2 * D_MODEL)
    g_blob = gb2.reshape(-1, D_MODEL)

    d_win, nm_win, nv_win = _adamw(w_in[0], g_w_in, m_w_in[0], v_w_in[0])
    pool_rows = lambda a: a[0].reshape(-1, D_MODEL)
    blob_results = _update_blob(g_blob, [
        (w_down_hg[0], m_w_down_hg[0], v_w_down_hg[0]), (w_down_pool[0], m_w_down_pool[0], v_w_down_pool[0]),
        (w_out[0], m_w_out[0], v_w_out[0]), (pool_rows(pool_w), pool_rows(m_pool_w), pool_rows(v_pool_w))])
    g_meta = lax.dynamic_slice_in_dim(tot[ROW_META:ROW_META + N_META], chip * q, q, axis=1)
    d_meta, nm_meta, nv_meta = _adamw(meta_tokens, g_meta, m_meta_tokens, v_meta_tokens)

    as_row = lambda a: a.reshape(1, D_MODEL)
    small_results = _update_small(tot, [
        (norm_w, m_norm_w, v_norm_w), (b_in, m_b_in, v_b_in), (lb_logits, m_lb_logits, v_lb_logits),
        (hg_norm_w, m_hg_norm_w, v_hg_norm_w), (pool_scale, m_pool_scale, v_pool_scale),
        (as_row(final_norm_w), as_row(m_final_norm_w), as_row(v_final_norm_w))])

    def leaves(kind, meta_part, win_part):
        nw, bi, lbl, hg, ps, fw = [r[kind] for r in small_results]
        wdh, wdp, wo, pw = [r[kind] for r in blob_results]
        return [meta_part, nw, win_part[None], bi, lbl, hg, pw.reshape(pool_w.shape), ps,
                wdh[None], wdp[None], wo[None], fw.reshape(D_MODEL)]

    loss = tot[ROW_LOSS, 0]
    return (loss, grad_x,
            *leaves(0, g_meta, g_w_in),
            *leaves(1, d_meta, d_win),
            *leaves(2, nm_meta, nm_win),
            *leaves(3, nv_meta, nv_win))
```

```python
import functools

import numpy as np
import jax
import jax.numpy as jnp
from jax import lax
from jax.experimental import pallas as pl
from jax.experimental.pallas import tpu as pltpu

F32 = jnp.float32
BF16 = jnp.bfloat16

D_MODEL = 1024
N_SEG = 8
N_HEADS = 8
HEAD_DIM = 128
CHUNK = 64
N_META = 16
PAD_ROWS = CHUNK - N_META
FIRST_TOKEN_ROW = CHUNK
LEVELS = (32, 16, 8, 4, 2, 1)
N_EXP = 2 + len(LEVELS)
POOL_WINDOWS = (2, 4, 8, 16)
POOL_GDIM = D_MODEL // len(POOL_WINDOWS)
HALO = 16
FORWARD_HEADS_PER_STEP = 4
BACKWARD_HEADS_PER_STEP = 2
LOCAL_UNROLL = 13
BACKWARD_UNROLL = 13
EPS = 1e-6
N_CHIPS = 4
N_DEV = 8
BLOB_RELATIONS_DIRECT = (0, 1, 2)
BLOB_RELATIONS_DIAGONAL = (3,)
SEGS_REC = (0, 1, 2)
SEGS_MIX = (3, 4, 5, 6, 7)

ADAM_LR = 0.001
ADAM_B1 = 0.9
ADAM_B2 = 0.999
ADAM_EPS = 1e-08
ADAM_WD = 0.01
ADAM_STEP = 10

VMEM_LIMIT_BYTES = 56 * 1024 * 1024

ROW_LOSS = 0
ROW_META = 1
ROW_NORM_W = ROW_META + N_META
ROW_B_IN = ROW_NORM_W + 1
ROW_LB = ROW_B_IN + N_SEG
ROW_HG_W = ROW_LB + 2
ROW_POOL_SCALE = ROW_HG_W + 1
ROW_FINAL_W = ROW_POOL_SCALE + 1
SMALL_ROWS = 32


def _tile(total, cap, mult=16):
    best = None
    for t in range(mult, min(total, cap) + 1, mult):
        if total % t == 0:
            best = t
    assert best is not None, (total, cap, mult)
    return best


def _token_window(tm, tile_of):
    def index(*grid):
        return (pl.multiple_of(jnp.maximum(tile_of(*grid) * tm - FIRST_TOKEN_ROW, 0), HALO), 0)
    return pl.BlockSpec((pl.Element(tm), pl.Element(D_MODEL)), index)


def _padded_tile(window, head, tile):
    first = jnp.concatenate([head, pltpu.roll(window, FIRST_TOKEN_ROW, 0)[FIRST_TOKEN_ROW:]], axis=0)
    return jnp.where(tile == 0, first, window)


def _params(sem=None):
    return pltpu.CompilerParams(dimension_semantics=sem, vmem_limit_bytes=VMEM_LIMIT_BYTES)


def _dot(a, b):
    return jnp.dot(a, b, preferred_element_type=F32)


def _dot_nt(a, b):
    return lax.dot_general(a, b, (((1,), (1,)), ((), ())), preferred_element_type=F32)


def _dot_tn(a, b):
    return lax.dot_general(a, b, (((0,), (0,)), ((), ())), preferred_element_type=F32)


def _sigmoid_pair(x):
    t = jnp.exp(-jnp.abs(x))
    r = 1.0 / (1.0 + t)
    pos = x >= 0
    return jnp.where(pos, r, t * r), jnp.where(pos, t * r, r)


def _exponent_matrix():
    t = np.arange(CHUNK)[:, None]
    j = np.arange(CHUNK)[None, :]
    blocks = [j <= t, j > t]
    for m in LEVELS:
        rho = (t // (2 * m)) * (2 * m) + m
        upper = (t >= rho) & (j > rho) & (j <= t)
        lower = (t < rho) & (j > t) & (j <= rho)
        blocks.append(upper | lower)
    return np.concatenate(blocks, axis=0).astype(np.float32)


def _pair_masks():
    t = np.arange(CHUNK)[:, None]
    s = np.arange(CHUNK)[None, :]
    masks = [t == s]
    for m in LEVELS:
        same = (t // (2 * m)) == (s // (2 * m))
        masks.append(same & ((t % (2 * m)) >= m) & ((s % (2 * m)) < m))
    return np.stack(masks).astype(np.float32)


LEVEL_PAIRS = ((0, 1), (2, 3), (4, 5), (6, None))


def _paired_masks():
    m = _pair_masks()
    zero = np.zeros_like(m[0])
    return np.stack([np.concatenate([m[a], zero if b is None else m[b]], axis=1) for a, b in LEVEL_PAIRS])


def _lower_bound(lbl):
    return 1.0 / (1.0 + jnp.exp(lbl[1:2, :] - lbl[0:1, :]))


def _in_proj(tokens, m4, norm_w, w16, b_in, seg_order, rows):
    tm = _tile(rows, 1040)
    nt = rows // tm
    gather = _ShardGather(w16.shape[0], own_shard_apart=True)

    def body(order_ref, z_ref, nw_ref, b_ref, w_in_ref, m_in_ref, h_ref, p_ref, w4_ref, head_ref, m4_ref,
             h_all, w_buf, w_sem, fill_sem, send_sems, recv_sems, meta_send, meta_recv, meta_sem):
        kk, i = pl.program_id(0), pl.program_id(1)

        @pl.when((kk == 0) & (i == 0))
        def _():
            x, y, c, chips = _place()

            def meta_copy(j, chip, to):
                return pltpu.make_async_remote_copy(
                    src_ref=m4_ref.at[chip], dst_ref=m4_ref.at[chip], send_sem=meta_send.at[j],
                    recv_sem=meta_recv.at[j], device_id=to, device_id_type=MESH)

            sends = [meta_copy(j, 2 * x + y, (cx, cy, c)) for j, (cx, cy) in enumerate(chips)]
            for cp in sends:
                cp.start()
            gather.start(w4_ref, send_sems, recv_sems, which=(0, 1), own_ref=w_in_ref)
            for j, (cx, cy) in enumerate(chips):
                meta_copy(j, 2 * cx + cy, (x, y, c)).wait_recv()
            for cp in sends:
                cp.wait_send()
            head_ref[0:PAD_ROWS, :] = jnp.zeros((PAD_ROWS, D_MODEL), F32)
            q_cols = D_MODEL // N_CHIPS
            for k in range(N_CHIPS):
                cp = pltpu.make_async_copy(
                    m4_ref.at[k], head_ref.at[pl.ds(PAD_ROWS, N_META), pl.ds(k * q_cols, q_cols)], meta_sem)
                cp.start()
                cp.wait()

        @pl.when((kk == 2) & (i == 0))
        def _():
            gather.start_diagonal_after_neighbours(w4_ref, send_sems, recv_sems, own_ref=w_in_ref)

        @pl.when(kk == 0)
        def _():
            zt = _padded_tile(z_ref[...], head_ref[...], i)
            rstd = lax.rsqrt(jnp.mean(zt * zt, axis=-1, keepdims=True) + EPS)
            h = (zt * rstd * nw_ref[...]).astype(BF16)
            h_all[pl.ds(pl.multiple_of(i * tm, 16), tm), :] = h
            h_ref[...] = h

        @pl.when((kk == 2) & (i == 0))
        def _():
            gather.pass_on(0, w4_ref, send_sems, recv_sems)
            gather.pass_on(1, w4_ref, send_sems, recv_sems)
            gather.await_sibling(0, w4_ref, send_sems, recv_sems)

        @pl.when((kk == 4) & (i == 0))
        def _():
            gather.await_sibling(1, w4_ref, send_sems, recv_sems)

        @pl.when((kk == 5) & (i == 0))
        def _():
            gather.pass_on(2, w4_ref, send_sems, recv_sems)

        @pl.when((kk == 6) & (i == 0))
        def _():
            gather.await_sibling(2, w4_ref, send_sems, recv_sems)

        def weights(which, own):
            seg = order_ref[2 * (kk // 2) + which]
            cols = pl.ds(pl.multiple_of((seg % 2) * D_MODEL, D_MODEL), D_MODEL)
            src = w_in_ref.at[:, cols] if own else w4_ref.at[seg // 2, :, cols]
            return pltpu.make_async_copy(src, w_buf.at[which], w_sem.at[which])

        def fill_own_slot(which):
            seg = order_ref[which]
            cols = pl.ds(pl.multiple_of((seg % 2) * D_MODEL, D_MODEL), D_MODEL)
            return pltpu.make_async_copy(w_buf.at[which], w4_ref.at[seg // 2, :, cols], fill_sem.at[which])

        @pl.when((i == 0) & (kk == 0))
        def _():
            weights(0, True).start()
            weights(1, True).start()
            weights(0, True).wait()
            fill_own_slot(0).start()

        @pl.when((i == 0) & (kk == 1))
        def _():
            weights(1, True).wait()
            fill_own_slot(1).start()

        @pl.when((i == 0) & (kk == 2))
        def _():
            fill_own_slot(0).wait()
            fill_own_slot(1).wait()

        @pl.when((i == 0) & (kk % 2 == 0) & (kk > 0))
        def _():
            weights(0, False).start()
            weights(1, False).start()
            weights(0, False).wait()

        @pl.when((i == 0) & (kk % 2 == 1) & (kk > 1))
        def _():
            weights(1, False).wait()

        p_ref[0] = _dot(h_all[pl.ds(pl.multiple_of(i * tm, 16), tm), :], w_buf[kk % 2]) + b_ref[...]

        @pl.when((kk == N_SEG - 1) & (i == nt - 1))
        def _():
            gather.finish(w4_ref, send_sems, recv_sems, which=(2,), own_ref=w_in_ref)

    first_pass = lambda kk, i, order_ref: (jnp.where(kk == 0, i, nt - 1), 0)
    return pl.pallas_call(
        body, name="in_proj",
        grid_spec=pltpu.PrefetchScalarGridSpec(
            num_scalar_prefetch=1, grid=(N_SEG, nt),
            in_specs=[
                _token_window(tm, lambda kk, i, order_ref: jnp.where(kk == 0, i, nt - 1)),
                pl.BlockSpec((1, D_MODEL), lambda kk, i, order_ref: (0, 0)),
                pl.BlockSpec((1, D_MODEL), lambda kk, i, order_ref: (0, order_ref[kk])),
                ANY, ANY,
            ],
            out_specs=[
                pl.BlockSpec((tm, D_MODEL), first_pass),
                pl.BlockSpec((1, tm, D_MODEL), lambda kk, i, order_ref: (order_ref[kk], i, 0)),
                ANY,
                pl.BlockSpec((FIRST_TOKEN_ROW, D_MODEL), lambda kk, i, order_ref: (0, 0)),
                ANY,
            ],
            scratch_shapes=[
                pltpu.VMEM((rows, D_MODEL), BF16),
                pltpu.VMEM((2, D_MODEL, D_MODEL), BF16),
                pltpu.SemaphoreType.DMA((2,)), pltpu.SemaphoreType.DMA((2,)),
            ] + gather.semaphores() + [
                pltpu.SemaphoreType.DMA((N_CHIPS - 1,)), pltpu.SemaphoreType.DMA((N_CHIPS - 1,)),
                pltpu.SemaphoreType.DMA,
            ]),
        out_shape=[
            jax.ShapeDtypeStruct((rows, D_MODEL), BF16),
            jax.ShapeDtypeStruct((N_SEG, rows, D_MODEL), F32),
            jax.ShapeDtypeStruct((N_CHIPS,) + w16.shape, w16.dtype),
            jax.ShapeDtypeStruct((FIRST_TOKEN_ROW, D_MODEL), F32),
            jax.ShapeDtypeStruct(m4.shape, m4.dtype),
        ],
        input_output_aliases={5: 4},
        compiler_params=_params(("arbitrary", "arbitrary")),
    )(seg_order, tokens, norm_w, b_in, w16, m4)


def _hgrn_forward(p3, lb_logits, wexp2, masks2, blob4, rows):
    n_chunks = rows // CHUNK
    cpb = _tile(n_chunks, 13, mult=1)
    rb_rows = cpb * CHUNK
    n_rb = n_chunks // cpb
    lanes = cpb * HEAD_DIM
    hps = FORWARD_HEADS_PER_STEP
    n_hb = N_HEADS // hps
    width = hps * HEAD_DIM
    gather = _ShardGather(blob4.shape[1])

    def body(q_ref, fz_ref, v_ref, lbl_ref, wexp_ref, mask_ref, b_in_ref, o_ref, s_ref, e16_ref, a2_ref, b4_ref,
             st_all, e_all, u_all, q_all, kk_all, v_all, send_sems, recv_sems):
        rb = pl.program_id(1)

        @pl.when((pl.program_id(0) == 0) & (rb == 0))
        def _():
            gather.start(b4_ref, send_sems, recv_sems)

        @pl.when(rb == 0)
        def _():
            st_all[...] = jnp.zeros_like(st_all)

        for j in range(hps):
            cols = pl.ds(j * HEAD_DIM, HEAD_DIM)
            one_head(rb, q_ref.at[0, :, cols], fz_ref.at[0, :, cols], v_ref.at[0, :, cols], lbl_ref.at[:, cols],
                     wexp_ref, mask_ref, o_ref.at[:, cols], s_ref.at[j], e16_ref.at[j, 0], a2_ref.at[:, cols],
                     st_all.at[j], e_all.at[j], u_all.at[j], q_all.at[j], kk_all.at[j], v_all.at[j])

        @pl.when((pl.program_id(0) == n_hb // 2) & (rb == 0))
        def _():
            for j in range(N_CHIPS - 1):
                gather.pass_on(j, b4_ref, send_sems, recv_sems)

        @pl.when((pl.program_id(0) == n_hb - 1) & (rb == n_rb - 1))
        def _():
            for j in range(N_CHIPS - 1):
                gather.await_sibling(j, b4_ref, send_sems, recv_sems)
            gather.finish(b4_ref, send_sems, recv_sems)

    def one_head(rb, q_ref, fz_ref, v_ref, lbl_ref, wexp_ref, mask_ref, o_ref, s_ref, e16_ref, a2_ref,
                 st_ref, e_ref, u_ref, q_s, kk_s, v_s):
        lb = _lower_bound(lbl_ref[...])
        row = rb * rb_rows + lax.broadcasted_iota(jnp.int32, (rb_rows, 1), 0)
        valid = row >= PAD_ROWS
        sg, sn = _sigmoid_pair(fz_ref[...])
        g = jnp.where(valid, jnp.log(lb + (1.0 - lb) * sg), 0.0)
        kk_s[...] = jnp.where(valid, (1.0 - lb) * sn, 0.0)
        q_s[...] = jnp.where(valid, q_ref[...], 0.0)
        v_s[...] = jnp.where(valid, v_ref[...], 0.0).astype(BF16)
        hi = g.astype(BF16)
        mid = (g - hi.astype(F32)).astype(BF16)
        g2 = jnp.concatenate(
            [jnp.concatenate([hi[b * CHUNK:(b + 1) * CHUNK], mid[b * CHUNK:(b + 1) * CHUNK]], axis=0)
             for b in range(cpb)], axis=1)
        e_ref[...] = jnp.exp(_dot(wexp_ref[...], g2))
        e16_ref[...] = e_ref[...].astype(BF16)

        def contribution(b, carry):
            r0 = pl.multiple_of(b * CHUNK, CHUNK)
            l0 = pl.multiple_of(b * HEAD_DIM, HEAD_DIM)
            kc16 = (kk_s[pl.ds(r0, CHUNK), :] * e_ref[CHUNK:2 * CHUNK, pl.ds(l0, HEAD_DIM)]).astype(BF16)
            u_ref[b] = _dot_tn(v_s[pl.ds(r0, CHUNK), :], kc16)
            return carry

        lax.fori_loop(0, cpb, contribution, 0, unroll=LOCAL_UNROLL)

        def recur(b, st):
            l0 = pl.multiple_of(b * HEAD_DIM, HEAD_DIM)
            s_ref[b] = st
            return st * e_ref[CHUNK - 1:CHUNK, pl.ds(l0, HEAD_DIM)] + u_ref[b]

        st_ref[...] = lax.fori_loop(0, cpb, recur, st_ref[...], unroll=LOCAL_UNROLL)

        zeros16 = jnp.zeros((CHUNK, HEAD_DIM), BF16)

        def local(b, carry):
            r0 = pl.multiple_of(b * CHUNK, CHUNK)
            l0 = pl.multiple_of(b * HEAD_DIM, HEAD_DIM)
            q = q_s[pl.ds(r0, CHUNK), :]
            kk = kk_s[pl.ds(r0, CHUNK), :]
            v16 = v_s[pl.ds(r0, CHUNK), :]

            def scaled(entry):
                if entry == 0:
                    return q.astype(BF16), kk.astype(BF16)
                e_m = e_ref[(1 + entry) * CHUNK:(2 + entry) * CHUNK, pl.ds(l0, HEAD_DIM)]
                return (q * e_m).astype(BF16), (kk * e_m).astype(BF16)

            a2 = jnp.zeros((CHUNK, 2 * CHUNK), F32)
            for p, (ea, eb) in enumerate(LEVEL_PAIRS):
                qa, ka = scaled(ea)
                if eb is None:
                    prod = _dot_nt(qa, jnp.concatenate([ka, zeros16], axis=0))
                else:
                    qb_, kb_ = scaled(eb)
                    rhs = jnp.concatenate([jnp.concatenate([ka, zeros16], axis=1),
                                           jnp.concatenate([zeros16, kb_], axis=1)], axis=0)
                    prod = _dot_nt(jnp.concatenate([qa, qb_], axis=1), rhs)
                a2 = a2 + mask_ref[p] * prod
            a2_16 = a2.astype(BF16)
            a2_ref[pl.ds(r0, CHUNK), :] = a2_16
            qb16 = (q * e_ref[0:CHUNK, pl.ds(l0, HEAD_DIM)]).astype(BF16)
            o_ref[pl.ds(r0, CHUNK), :] = (_dot(a2_16, jnp.concatenate([v16, v16], axis=0))
                                          + _dot_nt(qb16, s_ref[b].astype(BF16)))
            return carry

        lax.fori_loop(0, cpb, local, 0, unroll=LOCAL_UNROLL)

    head_block = lambda seg: pl.BlockSpec((1, rb_rows, width), lambda h, r: (seg, r, h))
    return pl.pallas_call(
        body, name="hgrn_forward",
        grid=(n_hb, n_rb),
        in_specs=[
            head_block(0), head_block(1), head_block(2),
            pl.BlockSpec((2, width), lambda h, r: (0, h)),
            pl.BlockSpec((N_EXP * CHUNK, 2 * CHUNK), lambda h, r: (0, 0)),
            pl.BlockSpec((len(LEVEL_PAIRS), CHUNK, 2 * CHUNK), lambda h, r: (0, 0, 0)),
            ANY,
        ],
        out_specs=[
            pl.BlockSpec((rb_rows, width), lambda h, r: (r, h)),
            pl.BlockSpec((hps, cpb, HEAD_DIM, HEAD_DIM), lambda h, r: (h, r, 0, 0)),
            pl.BlockSpec((hps, 1, N_EXP * CHUNK, lanes), lambda h, r: (h, r, 0, 0)),
            pl.BlockSpec((rb_rows, width), lambda h, r: (r, h)),
            ANY,
        ],
        out_shape=[
            jax.ShapeDtypeStruct((rows, D_MODEL), F32),
            jax.ShapeDtypeStruct((N_HEADS, n_chunks, HEAD_DIM, HEAD_DIM), F32),
            jax.ShapeDtypeStruct((N_HEADS, n_rb, N_EXP * CHUNK, lanes), BF16),
            jax.ShapeDtypeStruct((rows, D_MODEL), BF16),
            jax.ShapeDtypeStruct(blob4.shape, blob4.dtype),
        ],
        input_output_aliases={6: 4},
        scratch_shapes=[
            pltpu.VMEM((hps, HEAD_DIM, HEAD_DIM), F32),
            pltpu.VMEM((hps, N_EXP * CHUNK, lanes), F32),
            pltpu.VMEM((hps, cpb, HEAD_DIM, HEAD_DIM), F32),
            pltpu.VMEM((hps, rb_rows, HEAD_DIM), F32),
            pltpu.VMEM((hps, rb_rows, HEAD_DIM), F32),
            pltpu.VMEM((hps, rb_rows, HEAD_DIM), BF16),
        ] + gather.semaphores(),
        compiler_params=_params(("arbitrary", "arbitrary")),
    )(p3, p3, p3, lb_logits, wexp2, masks2, blob4)


def _hgrn_backward(p3, d_o, states, e16, a2, lb_logits, wexp_t, masks2, dw16, blob16, rx_blob, rows):
    n_chunks = rows // CHUNK
    cpb = _tile(n_chunks, 13, mult=1)
    rb_rows = cpb * CHUNK
    n_rb = n_chunks // cpb
    lanes = cpb * HEAD_DIM
    exchange = _GradExchange(SEGS_MIX, True, BLOB_RELATIONS_DIAGONAL)

    hps = BACKWARD_HEADS_PER_STEP
    n_hb = N_HEADS // hps
    width = hps * HEAD_DIM

    def body(q_ref, fz_ref, v_ref, do_ref, s_ref, e_ref, a2_ref, lbl_ref, wexpt_ref, mask_ref, dw_ref, blob_ref,
             rxb_in_ref, dp_ref, dlb_ref, rxw_ref, rxb_ref, *scratch):
        per_head, (send_sems, recv_sems) = scratch[:-2], scratch[-2:]
        step = pl.program_id(1)
        rb = n_rb - 1 - step

        @pl.when((pl.program_id(0) == 0) & (step == 0))
        def _():
            exchange.start(dw_ref, rxw_ref, blob_ref, rxb_ref, send_sems, recv_sems)

        for j in range(hps):
            cols = pl.ds(j * HEAD_DIM, HEAD_DIM)
            one_head(step, rb, q_ref.at[0, :, cols], fz_ref.at[0, :, cols], v_ref.at[0, :, cols], do_ref.at[:, cols],
                     s_ref.at[j], e_ref.at[j, 0], a2_ref.at[:, cols], lbl_ref.at[:, cols], wexpt_ref, mask_ref,
                     dp_ref.at[:, :, cols], dlb_ref.at[:, cols], *[ref.at[j] for ref in per_head])

        @pl.when((pl.program_id(0) == n_hb - 1) & (step == n_rb - 1))
        def _():
            exchange.wait(dw_ref, rxw_ref, blob_ref, rxb_ref, send_sems, recv_sems)

    def one_head(step, rb, q_ref, fz_ref, v_ref, do_ref, s_ref, e_ref, a2_ref, lbl_ref, wexpt_ref, mask_ref,
                 dp_ref, dlb_ref, dst_ref, g_ref, dsn_ref, q_s, kk_s, v_s, do_s, dq_s, dkk_s, dg_s, dx_s, da2_s):
        @pl.when(step == 0)
        def _():
            dst_ref[...] = jnp.zeros_like(dst_ref)
            dlb_ref[...] = jnp.zeros_like(dlb_ref)

        lb = _lower_bound(lbl_ref[...])
        row = rb * rb_rows + lax.broadcasted_iota(jnp.int32, (rb_rows, 1), 0)
        valid = row >= PAD_ROWS
        sg, sn = _sigmoid_pair(fz_ref[...])
        f = lb + (1.0 - lb) * sg
        g = jnp.where(valid, jnp.log(f), 0.0)
        kk_s[...] = jnp.where(valid, (1.0 - lb) * sn, 0.0)
        q_s[...] = jnp.where(valid, q_ref[...], 0.0)
        v_s[...] = jnp.where(valid, v_ref[...], 0.0).astype(BF16)
        do_s[...] = do_ref[...].astype(BF16)
        e_last_all = jnp.exp(jnp.concatenate(
            [jnp.sum(g[b * CHUNK:(b + 1) * CHUNK], axis=0, keepdims=True) for b in range(cpb)], axis=0))
        last_row = lax.broadcasted_iota(jnp.int32, (CHUNK, 1), 0) == CHUNK - 1
        zeros16 = jnp.zeros((CHUNK, HEAD_DIM), BF16)

        def factor(block, l0):
            return e_ref[block * CHUNK:(block + 1) * CHUNK, pl.ds(l0, HEAD_DIM)].astype(F32)

        def contribution(b, carry):
            r0 = pl.multiple_of(b * CHUNK, CHUNK)
            l0 = pl.multiple_of(b * HEAD_DIM, HEAD_DIM)
            qb16 = (q_s[pl.ds(r0, CHUNK), :] * factor(0, l0)).astype(BF16)
            g_ref[b] = _dot_tn(do_s[pl.ds(r0, CHUNK), :], qb16)
            return carry

        lax.fori_loop(0, cpb, contribution, 0, unroll=LOCAL_UNROLL)

        cur = dst_ref[...]
        for b in reversed(range(cpb)):
            dsn_ref[b] = cur
            cur = cur * e_last_all[b:b + 1, :] + g_ref[b]
        dst_ref[...] = cur

        def through_state(b, carry):
            r0 = pl.multiple_of(b * CHUNK, CHUNK)
            l0 = pl.multiple_of(b * HEAD_DIM, HEAD_DIM)
            v16 = v_s[pl.ds(r0, CHUNK), :]
            do16 = do_s[pl.ds(r0, CHUNK), :]
            st = s_ref[b]
            dsn = dsn_ref[b]
            dsn16 = dsn.astype(BF16)
            e_b, e_c = factor(0, l0), factor(1, l0)
            qb, kc = q_s[pl.ds(r0, CHUNK), :] * e_b, kk_s[pl.ds(r0, CHUNK), :] * e_c

            t = _dot_tn(a2_ref[pl.ds(r0, CHUNK), :], do16)
            dv = t[0:CHUNK] + t[CHUNK:2 * CHUNK] + _dot_nt(kc.astype(BF16), dsn16)
            dp_ref[2, pl.ds(r0, CHUNK), :] = dv.astype(BF16)
            da2_s[pl.ds(r0, CHUNK), :] = _dot_nt(do16, jnp.concatenate([v16, v16], axis=0))
            dqb = _dot(do16, st.astype(BF16))
            dkc = _dot(v16, dsn16)
            de = jnp.sum(dsn * st, axis=0, keepdims=True) * e_b[CHUNK - 1:CHUNK, :]
            dq_s[pl.ds(r0, CHUNK), :] = e_b * dqb
            dkk_s[pl.ds(r0, CHUNK), :] = e_c * dkc
            dx_s[0:CHUNK, pl.ds(l0, HEAD_DIM)] = (qb * dqb + jnp.where(last_row, de, 0.0)).astype(BF16)
            dx_s[CHUNK:2 * CHUNK, pl.ds(l0, HEAD_DIM)] = (kc * dkc).astype(BF16)
            return carry

        lax.fori_loop(0, cpb, through_state, 0, unroll=BACKWARD_UNROLL)

        def local(b, carry):
            r0 = pl.multiple_of(b * CHUNK, CHUNK)
            l0 = pl.multiple_of(b * HEAD_DIM, HEAD_DIM)
            q = q_s[pl.ds(r0, CHUNK), :]
            kk = kk_s[pl.ds(r0, CHUNK), :]
            da2 = da2_s[pl.ds(r0, CHUNK), :]
            dq = dq_s[pl.ds(r0, CHUNK), :]
            dkk = dkk_s[pl.ds(r0, CHUNK), :]

            def scaled(entry):
                if entry == 0:
                    return q, kk, None
                e_m = factor(1 + entry, l0)
                return q * e_m, kk * e_m, e_m

            for p, (ea, eb) in enumerate(LEVEL_PAIRS):
                dm = mask_ref[p] * da2
                dm_t = dm.T.astype(BF16)
                qa, ka, e_a = scaled(ea)
                if eb is None:
                    rhs_k = jnp.concatenate([jnp.concatenate([ka.astype(BF16), zeros16], axis=1),
                                             jnp.concatenate([zeros16, zeros16], axis=1)], axis=0)
                else:
                    qb_, kb_, e_bb = scaled(eb)
                    rhs_k = jnp.concatenate([jnp.concatenate([ka.astype(BF16), zeros16], axis=1),
                                             jnp.concatenate([zeros16, kb_.astype(BF16)], axis=1)], axis=0)
                dq2 = _dot(dm.astype(BF16), rhs_k)
                parts = [(ea, qa, ka, e_a, dq2[:, :HEAD_DIM], _dot(dm_t[0:CHUNK], qa.astype(BF16)))]
                if eb is not None:
                    parts.append((eb, qb_, kb_, e_bb, dq2[:, HEAD_DIM:],
                                  _dot(dm_t[CHUNK:2 * CHUNK], qb_.astype(BF16))))
                for entry, q_m, k_m, e_m, dq_m, dk_m in parts:
                    if entry == 0:
                        dq = dq + dq_m
                        dkk = dkk + dk_m
                    else:
                        dq = dq + e_m * dq_m
                        dkk = dkk + e_m * dk_m
                        dx_s[(1 + entry) * CHUNK:(2 + entry) * CHUNK, pl.ds(l0, HEAD_DIM)] = (
                            q_m * dq_m + k_m * dk_m).astype(BF16)
            dq_s[pl.ds(r0, CHUNK), :] = dq
            dkk_s[pl.ds(r0, CHUNK), :] = dkk
            return carry

        lax.fori_loop(0, cpb, local, 0, unroll=BACKWARD_UNROLL)

        dg_all = _dot(wexpt_ref[...], dx_s[...])
        for b in range(cpb):
            dg_s[b * CHUNK:(b + 1) * CHUNK, :] = dg_all[:, b * HEAD_DIM:(b + 1) * HEAD_DIM]
        t = jnp.where(valid, dg_s[...] / f - dkk_s[...], 0.0)
        dlb_ref[...] += jnp.sum(sn * t, axis=0, keepdims=True)
        dp_ref[0] = jnp.where(valid, dq_s[...], 0.0).astype(BF16)
        dp_ref[1] = ((1.0 - lb) * sg * sn * t).astype(BF16)

    head_block = lambda seg: pl.BlockSpec((1, rb_rows, width), lambda h, s: (seg, n_rb - 1 - s, h))
    row_block = pl.BlockSpec((rb_rows, width), lambda h, s: (n_rb - 1 - s, h))
    return pl.pallas_call(
        body, name="hgrn_backward",
        grid=(n_hb, n_rb),
        in_specs=[
            head_block(0), head_block(1), head_block(2),
            row_block,
            pl.BlockSpec((hps, cpb, HEAD_DIM, HEAD_DIM), lambda h, s: (h, n_rb - 1 - s, 0, 0)),
            pl.BlockSpec((hps, 1, N_EXP * CHUNK, lanes), lambda h, s: (h, n_rb - 1 - s, 0, 0)),
            row_block,
            pl.BlockSpec((2, width), lambda h, s: (0, h)),
            pl.BlockSpec((CHUNK, N_EXP * CHUNK), lambda h, s: (0, 0)),
            pl.BlockSpec((len(LEVEL_PAIRS), CHUNK, 2 * CHUNK), lambda h, s: (0, 0, 0)),
            ANY, ANY, ANY,
        ],
        out_specs=[
            pl.BlockSpec((3, rb_rows, width), lambda h, s: (0, n_rb - 1 - s, h)),
            pl.BlockSpec((1, width), lambda h, s: (0, h)),
            ANY, ANY,
        ],
        out_shape=[
            jax.ShapeDtypeStruct((3, rows, D_MODEL), BF16),
            jax.ShapeDtypeStruct((1, D_MODEL), F32),
            exchange.landing_w(), jax.ShapeDtypeStruct(rx_blob.shape, rx_blob.dtype),
        ],
        input_output_aliases={12: 3},
        scratch_shapes=[
            pltpu.VMEM((hps, HEAD_DIM, HEAD_DIM), F32),
            pltpu.VMEM((hps, cpb, HEAD_DIM, HEAD_DIM), F32),
            pltpu.VMEM((hps, cpb, HEAD_DIM, HEAD_DIM), F32),
            pltpu.VMEM((hps, rb_rows, HEAD_DIM), F32),
            pltpu.VMEM((hps, rb_rows, HEAD_DIM), F32),
            pltpu.VMEM((hps, rb_rows, HEAD_DIM), BF16),
            pltpu.VMEM((hps, rb_rows, HEAD_DIM), BF16),
            pltpu.VMEM((hps, rb_rows, HEAD_DIM), F32),
            pltpu.VMEM((hps, rb_rows, HEAD_DIM), F32),
            pltpu.VMEM((hps, rb_rows, HEAD_DIM), F32),
            pltpu.VMEM((hps, N_EXP * CHUNK, lanes), BF16),
            pltpu.VMEM((hps, rb_rows, 2 * CHUNK), F32),
        ] + exchange.semaphores(),
        compiler_params=_params(("arbitrary", "arbitrary")),
    )(p3, p3, p3, d_o, states, e16, a2, lb_logits, wexp_t, masks2, dw16, blob16, rx_blob)


def _sigmoid(x):
    return 1.0 / (1.0 + jnp.exp(-x))


def _silu_and_grad(x):
    s = _sigmoid(x)
    return x * s, s * (1.0 + x * (1.0 - s))


def _window_sum(ext, width, forward_looking):
    n = ext.shape[0]
    s = ext
    step = 1
    while step < width:
        s = s + pltpu.roll(s, (n - step) if forward_looking else step, 0)
        step *= 2
    return s


def _mixers(o, p3, tokens, head, tgt, wdh, wdp, wout, poolw, hg_w, pool_scale, final_w, rows):
    tm = _tile(rows, 208)
    nt = rows // tm
    halo_blocks = tm // HALO
    n_grp = len(POOL_WINDOWS)
    q_rows = D_MODEL // N_CHIPS
    blob_rows = 3 * q_rows + n_grp * POOL_GDIM * POOL_GDIM // (N_CHIPS * D_MODEL)

    def body(o_ref, ghg_ref, u_ref, gpl_ref, mhg_ref, mpl_ref, uh_ref, z_ref, t_ref,
             wdh_ref, wdp_ref, wout_ref, pw_ref, hgw_ref, ps_ref, fw_ref, head_ref,
             do_ref, dz2_ref, dp_ref, blob_ref, dpw_ref, small_ref, carry_ref):
        step = pl.program_id(0)
        tile = nt - 1 - step

        def add_to_blob(piece, dw):
            for k in range(N_CHIPS):
                blob_ref[k, piece * q_rows:(piece + 1) * q_rows, :] += dw[k * q_rows:(k + 1) * q_rows]

        @pl.when(step == 0)
        def _():
            blob_ref[...] = jnp.zeros_like(blob_ref)
            dpw_ref[...] = jnp.zeros_like(dpw_ref)
            small_ref[...] = jnp.zeros_like(small_ref)
            carry_ref[...] = jnp.zeros_like(carry_ref)

        row = tile * tm + lax.broadcasted_iota(jnp.int32, (tm, 1), 0)
        real = row >= PAD_ROWS
        pos1 = jnp.maximum(row - PAD_ROWS + 1, 1).astype(F32)

        u = jnp.where(real, u_ref[0], 0.0)
        halo_row = tile * tm - HALO + lax.broadcasted_iota(jnp.int32, (HALO, 1), 0)
        uh = jnp.where(halo_row >= PAD_ROWS, uh_ref[0], 0.0)
        ext = jnp.concatenate([uh, u], axis=0)
        pooled, inv_cnt, mixed = [], [], []
        for g, w in enumerate(POOL_WINDOWS):
            cols = slice(g * POOL_GDIM, (g + 1) * POOL_GDIM)
            inv = 1.0 / jnp.minimum(pos1, float(w))
            ws = _window_sum(ext[:, cols], w, False)[HALO:]
            pg = (ws * inv - u[:, cols]).astype(BF16)
            pooled.append(pg)
            inv_cnt.append(inv)
            mixed.append(_dot(pg, pw_ref[g]))
        mixed = jnp.concatenate(mixed, axis=1)
        gpl = gpl_ref[0]
        sp, dsp = _silu_and_grad(gpl)
        ps = ps_ref[...]
        a_pool = (mixed * ps * sp).astype(BF16)
        y_pool = _dot(a_pool, wdp_ref[...])

        o = o_ref[...]
        o_hat, rstd_h = [], []
        for h in range(N_HEADS):
            oh = o[:, h * HEAD_DIM:(h + 1) * HEAD_DIM]
            r = lax.rsqrt(jnp.mean(oh * oh, axis=-1, keepdims=True) + EPS)
            rstd_h.append(r)
            o_hat.append(oh * r)
        o_hat = jnp.concatenate(o_hat, axis=1)
        hgw = hgw_ref[...]
        o_n = o_hat * hgw
        ghg = ghg_ref[0]
        sh, dsh = _silu_and_grad(ghg)
        a_hg = (o_n * sh).astype(BF16)
        y_hg = _dot(a_hg, wdh_ref[...])

        s_mh = _sigmoid(mhg_ref[0])
        s_mp = _sigmoid(mpl_ref[0])
        merged = (s_mh * y_hg + s_mp * y_pool).astype(BF16)
        z2 = _padded_tile(z_ref[...], head_ref[...], tile) + _dot(merged, wout_ref[...])
        rstd2 = lax.rsqrt(jnp.mean(z2 * z2, axis=-1, keepdims=True) + EPS)
        zh = z2 * rstd2
        fw = fw_ref[...]
        target = _padded_tile(t_ref[...], jnp.zeros((FIRST_TOKEN_ROW, D_MODEL), F32), tile)
        err = jnp.where(row >= FIRST_TOKEN_ROW, zh * fw - target, 0.0)
        small_ref[ROW_LOSS:ROW_LOSS + 1, :] += jnp.sum(err * err, axis=0, keepdims=True) * (0.5 / D_MODEL)
        dy = err * (1.0 / D_MODEL)

        small_ref[ROW_FINAL_W:ROW_FINAL_W + 1, :] += jnp.sum(dy * zh, axis=0, keepdims=True)
        uu = dy * fw
        dz2 = rstd2 * (uu - zh * jnp.mean(uu * zh, axis=-1, keepdims=True))
        dz2_ref[...] = dz2
        dz2_16 = dz2.astype(BF16)
        dmerged = _dot_nt(dz2_16, wout_ref[...])
        add_to_blob(2, _dot_tn(merged, dz2_16))
        dy_hg = (s_mh * dmerged).astype(BF16)
        dy_pool = (s_mp * dmerged).astype(BF16)
        dp_ref[3] = (dmerged * y_hg * s_mh * (1.0 - s_mh)).astype(BF16)
        dp_ref[4] = (dmerged * y_pool * s_mp * (1.0 - s_mp)).astype(BF16)

        da_hg = _dot_nt(dy_hg, wdh_ref[...])
        add_to_blob(0, _dot_tn(a_hg, dy_hg))
        dp_ref[0] = (da_hg * o_n * dsh).astype(BF16)
        do_n = da_hg * sh
        small_ref[ROW_HG_W:ROW_HG_W + 1, :] += jnp.sum(do_n * o_hat, axis=0, keepdims=True)
        d_hat = do_n * hgw
        for h in range(N_HEADS):
            cols = slice(h * HEAD_DIM, (h + 1) * HEAD_DIM)
            dh_, oh_ = d_hat[:, cols], o_hat[:, cols]
            do_ref[:, cols] = rstd_h[h] * (dh_ - oh_ * jnp.mean(dh_ * oh_, axis=-1, keepdims=True))

        da_pool = _dot_nt(dy_pool, wdp_ref[...])
        add_to_blob(1, _dot_tn(a_pool, dy_pool))
        small_ref[ROW_POOL_SCALE:ROW_POOL_SCALE + 1, :] += jnp.sum(da_pool * mixed * sp, axis=0, keepdims=True)
        dp_ref[2] = (da_pool * mixed * ps * dsp).astype(BF16)
        dmixed = (da_pool * ps * sp).astype(BF16)
        carry = carry_ref[...]
        du, new_carry = [], []
        for g, w in enumerate(POOL_WINDOWS):
            cols = slice(g * POOL_GDIM, (g + 1) * POOL_GDIM)
            dmg = dmixed[:, cols]
            dpooled = _dot_nt(dmg, pw_ref[g])
            dpw_ref[g] += _dot_tn(pooled[g], dmg)
            dps = dpooled * inv_cnt[g]
            ext_b = jnp.concatenate([dps, carry[:, cols]], axis=0)
            du.append(_window_sum(ext_b, w, True)[:tm] - dpooled)
            new_carry.append(dps[:HALO])
        dp_ref[1] = jnp.where(real, jnp.concatenate(du, axis=1), 0.0).astype(BF16)
        carry_ref[...] = jnp.concatenate(new_carry, axis=1)

    row_block = pl.BlockSpec((tm, D_MODEL), lambda s: (nt - 1 - s, 0))
    seg_block = lambda seg: pl.BlockSpec((1, tm, D_MODEL), lambda s: (seg, nt - 1 - s, 0))
    whole = pl.BlockSpec(memory_space=pltpu.VMEM)
    return pl.pallas_call(
        body, name="mixers",
        grid=(nt,),
        in_specs=[
            row_block, seg_block(3), seg_block(4), seg_block(5), seg_block(6), seg_block(7),
            pl.BlockSpec((1, HALO, D_MODEL),
                         lambda s: (4, jnp.maximum((nt - 1 - s) * halo_blocks - 1, 0), 0)),
            _token_window(tm, lambda s: nt - 1 - s), _token_window(tm, lambda s: nt - 1 - s),
            whole, whole, whole, whole, whole, whole, whole, whole,
        ],
        out_specs=[
            row_block, row_block,
            pl.BlockSpec((5, tm, D_MODEL), lambda s: (0, nt - 1 - s, 0)),
            whole, whole, whole,
        ],
        out_shape=[
            jax.ShapeDtypeStruct((rows, D_MODEL), F32),
            jax.ShapeDtypeStruct((rows, D_MODEL), F32),
            jax.ShapeDtypeStruct((5, rows, D_MODEL), BF16),
            jax.ShapeDtypeStruct((N_CHIPS, blob_rows, D_MODEL), F32),
            jax.ShapeDtypeStruct((n_grp, POOL_GDIM, POOL_GDIM), F32),
            jax.ShapeDtypeStruct((SMALL_ROWS, D_MODEL), F32),
        ],
        scratch_shapes=[pltpu.VMEM((HALO, D_MODEL), F32)],
        compiler_params=_params(("arbitrary",)),
    )(o, p3, p3, p3, p3, p3, p3, tokens, tgt, wdh, wdp, wout, poolw, hg_w, pool_scale, final_w, head)


def _seg_specs(tm, row_of, seg_of):
    def spec_a(*g):
        k = seg_of(*g)
        return (jnp.minimum(k, 2), jnp.where(k < 3, row_of(*g), 0), 0)

    def spec_b(*g):
        k = seg_of(*g)
        return (jnp.maximum(k - 3, 0), jnp.where(k >= 3, row_of(*g), 0), 0)

    return pl.BlockSpec((1, tm, D_MODEL), spec_a), pl.BlockSpec((1, tm, D_MODEL), spec_b)


def _in_proj_weight_grad(h, dp, rows, name, blob16=None):
    n_seg = dp.shape[0]
    tm = _tile(rows, 1040)
    nt = rows // tm
    half = D_MODEL // 2
    exchange = _GradExchange((), True, BLOB_RELATIONS_DIRECT) if blob16 is not None else None

    def body(*refs):
        if exchange is None:
            (h_ref, dp_ref, part_ref, part16_ref, db_ref, acc_ref, bacc_ref, stage_ref, land_ref,
             send_sems, recv_sems) = refs
        else:
            (h_ref, dp_ref, blob_ref, part_ref, part16_ref, db_ref, rxb_ref, acc_ref, bacc_ref, stage_ref, land_ref,
             send_sems, recv_sems, blob_send, blob_recv) = refs
        k, i = pl.program_id(0), pl.program_id(1)
        x, y, c = lax.axis_index("x"), lax.axis_index("y"), lax.axis_index("c")

        if exchange is not None:
            @pl.when((k == 0) & (i == 0))
            def _():
                exchange.start(None, None, blob_ref, rxb_ref, blob_send, blob_recv)

        def to_sibling(seg):
            return pltpu.make_async_remote_copy(
                src_ref=stage_ref.at[seg], dst_ref=land_ref.at[seg], send_sem=send_sems.at[seg],
                recv_sem=recv_sems.at[seg], device_id=(x, y, 1 - c), device_id_type=MESH)

        @pl.when(i == 0)
        def _():
            acc_ref[...] = jnp.zeros_like(acc_ref)
            bacc_ref[...] = jnp.zeros_like(bacc_ref)

        dpt = dp_ref[0]
        acc_ref[...] += _dot_tn(h_ref[...], dpt)
        bacc_ref[...] += jnp.sum(dpt.astype(F32), axis=0, keepdims=True)

        @pl.when(i == nt - 1)
        def _():
            db_ref[0] = bacc_ref[...]
            part_ref[k] = acc_ref[pl.ds(pl.multiple_of(c * half, half), half), :]
            stage_ref[k] = acc_ref[pl.ds(pl.multiple_of((1 - c) * half, half), half), :].astype(BF16)
            to_sibling(k).start()

        @pl.when((k == n_seg - 1) & (i == nt - 1))
        def _():
            for seg in range(n_seg):
                to_sibling(seg).wait_recv()
                total = part_ref[seg] + land_ref[seg].astype(F32)
                part_ref[seg] = total
                part16_ref[seg] = total.astype(BF16)
            for seg in range(n_seg):
                to_sibling(seg).wait_send()
            if exchange is not None:
                exchange.wait(None, None, blob_ref, rxb_ref, blob_send, blob_recv)

    whole = pl.BlockSpec(memory_space=pltpu.VMEM)
    with_blob = exchange is not None
    return pl.pallas_call(
        body, name=name,
        grid=(n_seg, nt),
        in_specs=[pl.BlockSpec((tm, D_MODEL), lambda k, i: (i, 0)),
                  pl.BlockSpec((1, tm, D_MODEL), lambda k, i: (k, i, 0))] + [ANY] * with_blob,
        out_specs=[whole, whole, pl.BlockSpec((1, 1, D_MODEL), lambda k, i: (k, 0, 0))] + [ANY] * with_blob,
        out_shape=[
            jax.ShapeDtypeStruct((n_seg, half, D_MODEL), F32),
            jax.ShapeDtypeStruct((n_seg, half, D_MODEL), BF16),
            jax.ShapeDtypeStruct((n_seg, 1, D_MODEL), F32),
        ] + ([exchange.landing_blob(blob16)] if with_blob else []),
        scratch_shapes=[
            pltpu.VMEM((D_MODEL, D_MODEL), F32), pltpu.VMEM((1, D_MODEL), F32),
            pltpu.VMEM((n_seg, half, D_MODEL), BF16),
            pltpu.VMEM((n_seg, half, D_MODEL), BF16),
            pltpu.SemaphoreType.DMA((n_seg,)), pltpu.SemaphoreType.DMA((n_seg,)),
        ] + (exchange.semaphores() if with_blob else []),
        compiler_params=_params(("arbitrary", "arbitrary")),
    )(h, dp, *([blob16] if with_blob else []))


def _input_grad(dpa, dpb, w4, tokens, head, dz2, norm_w, dw16, rows):
    tm = _tile(rows, 1040)
    nt = rows // tm
    assert nt >= 2, rows
    exchange = _GradExchange(SEGS_REC, with_blob=False)

    def body(dpa_ref, dpb_ref, w_ref, z_ref, head_ref, dz2_ref, nw_ref, dw_ref, gx_ref, dmeta_ref, dnw_ref, rxw_ref,
             acc_ref, dz_buf, out_sem, send_sems, recv_sems):
        i, k = pl.program_id(0), pl.program_id(1)

        def first_tile_out():
            return pltpu.make_async_copy(dz_buf.at[pl.ds(FIRST_TOKEN_ROW, tm - FIRST_TOKEN_ROW), :],
                                         gx_ref.at[pl.ds(0, tm - FIRST_TOKEN_ROW), :], out_sem)

        def tile_out(tile):
            start = pl.multiple_of(tile * tm - FIRST_TOKEN_ROW, HALO)
            return pltpu.make_async_copy(dz_buf, gx_ref.at[pl.ds(start, tm), :], out_sem)

        @pl.when((i == 0) & (k == 0))
        def _():
            exchange.start(dw_ref, rxw_ref, None, None, send_sems, recv_sems)
            dnw_ref[...] = jnp.zeros_like(dnw_ref)

        @pl.when((i == nt - 1) & (k == N_SEG - 1))
        def _():
            exchange.wait(dw_ref, rxw_ref, None, None, send_sems, recv_sems)

        @pl.when(k == 0)
        def _():
            acc_ref[...] = jnp.zeros_like(acc_ref)

        @pl.when(k < 3)
        def _():
            acc_ref[...] += _dot_nt(dpa_ref[0], w_ref[0])

        @pl.when(k >= 3)
        def _():
            acc_ref[...] += _dot_nt(dpb_ref[0], w_ref[0])

        @pl.when(k == N_SEG - 1)
        def _():
            zt = _padded_tile(z_ref[...], head_ref[...], i)
            rstd = lax.rsqrt(jnp.mean(zt * zt, axis=-1, keepdims=True) + EPS)
            zh = zt * rstd
            dh = acc_ref[...]
            dnw_ref[...] += jnp.sum(dh * zh, axis=0, keepdims=True)
            uu = dh * nw_ref[...]
            dz = dz2_ref[...] + rstd * (uu - zh * jnp.mean(uu * zh, axis=-1, keepdims=True))

            @pl.when(i == 1)
            def _():
                first_tile_out().wait()

            @pl.when(i >= 2)
            def _():
                tile_out(i - 1).wait()

            dz_buf[...] = dz

            @pl.when(i == 0)
            def _():
                dmeta_ref[...] = dz[PAD_ROWS:FIRST_TOKEN_ROW]
                first_tile_out().start()

            @pl.when(i > 0)
            def _():
                tile_out(i).start()

            @pl.when(i == nt - 1)
            def _():
                tile_out(i).wait()

    spec_a, spec_b = _seg_specs(tm, lambda i, k: i, lambda i, k: k)
    last_only = pl.BlockSpec((tm, D_MODEL), lambda i, k: (jnp.where(k == N_SEG - 1, i, 0), 0))
    return pl.pallas_call(
        body, name="input_grad",
        grid=(nt, N_SEG),
        in_specs=[
            spec_a, spec_b,
            pl.BlockSpec((1, D_MODEL, D_MODEL), lambda i, k: (k // 2, 0, k % 2)),
            _token_window(tm, lambda i, k: jnp.where(k == N_SEG - 1, i, 0)),
            pl.BlockSpec((FIRST_TOKEN_ROW, D_MODEL), lambda i, k: (0, 0)),
            last_only,
            pl.BlockSpec((1, D_MODEL), lambda i, k: (0, 0)),
            ANY,
        ],
        out_specs=[
            ANY,
            pl.BlockSpec((N_META, D_MODEL), lambda i, k: (0, 0)),
            pl.BlockSpec((1, D_MODEL), lambda i, k: (0, 0)),
            ANY,
        ],
        out_shape=[
            jax.ShapeDtypeStruct((rows - FIRST_TOKEN_ROW, D_MODEL), F32),
            jax.ShapeDtypeStruct((N_META, D_MODEL), F32),
            jax.ShapeDtypeStruct((1, D_MODEL), F32),
            exchange.landing_w(),
        ],
        scratch_shapes=[pltpu.VMEM((tm, D_MODEL), F32), pltpu.VMEM((tm, D_MODEL), F32),
                        pltpu.SemaphoreType.DMA] + exchange.semaphores(),
        compiler_params=_params(("arbitrary", "arbitrary")),
    )(dpa, dpb, w4, tokens, head, dz2, norm_w, dw16)


def _local_step(tokens, m4, tgt, w16, blob4, seg_order, norm_w, b_in, lb_logits, hg_w, pool_scale, final_w):
    rows = FIRST_TOKEN_ROW + tokens.shape[0]
    q = D_MODEL // N_CHIPS
    n_grp = len(POOL_WINDOWS)
    pg = POOL_GDIM // N_CHIPS

    wexp2 = jnp.asarray(np.tile(_exponent_matrix(), (1, 2)), BF16)
    wexp_t = jnp.asarray(_exponent_matrix().T, BF16)
    masks2 = jnp.asarray(_paired_masks(), F32)

    h, p3, w4, head, _ = _in_proj(tokens, m4, norm_w, w16, b_in, seg_order, rows)
    o, states, e16, a2, blob4 = _hgrn_forward(p3, lb_logits, wexp2, masks2, blob4, rows)
    wdh = blob4[:, 0:q].reshape(D_MODEL, D_MODEL)
    wdp = blob4[:, q:2 * q].reshape(D_MODEL, D_MODEL)
    wout = blob4[:, 2 * q:3 * q].reshape(D_MODEL, D_MODEL)
    poolw = blob4[:, 3 * q:].reshape(N_CHIPS, n_grp, pg, POOL_GDIM).transpose(1, 0, 2, 3)
    poolw = poolw.reshape(n_grp, POOL_GDIM, POOL_GDIM)
    d_o, dz2, dpb, dblob4, dpw, small = _mixers(
        o, p3, tokens, head, tgt, wdh, wdp, wout, poolw, hg_w, pool_scale, final_w, rows)
    dpw4 = dpw.reshape(n_grp, N_CHIPS, pg, POOL_GDIM).transpose(1, 0, 2, 3)
    dpw4 = dpw4.reshape(N_CHIPS, n_grp * pg * POOL_GDIM // D_MODEL, D_MODEL)
    dblob4 = dblob4.at[:, 3 * q:, :].set(dpw4)

    blob16 = dblob4.astype(BF16)
    dw_mix, dw_mix16, db_mix, rx_blob = _in_proj_weight_grad(h, dpb, rows, "in_proj_weight_grad_mix", blob16)
    dpa, dlb, rxw_mix, rx_blob = _hgrn_backward(
        p3, d_o, states, e16, a2, lb_logits, wexp_t, masks2, dw_mix16, blob16, rx_blob, rows)
    dw_rec, dw_rec16, db_rec = _in_proj_weight_grad(h, dpa, rows, "in_proj_weight_grad_rec")
    d_tokens, d_meta, dnw, rxw_rec = _input_grad(dpa, dpb, w4, tokens, head, dz2, norm_w, dw_rec16, rows)

    small_parts = (small, d_meta, dnw, db_rec, db_mix, dlb)
    return d_tokens, (dw_rec, dw_mix, rxw_rec, rxw_mix), (dblob4, rx_blob), small_parts


ANY = pl.BlockSpec(memory_space=pl.ANY)
MESH = pl.DeviceIdType.MESH


def _place():
    x, y, c = lax.axis_index("x"), lax.axis_index("y"), lax.axis_index("c")
    chips = [(1 - x, y), (x, 1 - y), (1 - x, 1 - y)]
    return x, y, c, chips


class _ShardGather:
    def __init__(self, rows, own_shard_apart=False):
        self.half = rows // 2
        self.own_shard_apart = own_shard_apart

    def semaphores(self):
        return [pltpu.SemaphoreType.DMA((6,)), pltpu.SemaphoreType.DMA((6,))]

    def _copy(self, k, slot, to, send_sems, recv_sems, src=None):
        return pltpu.make_async_remote_copy(src_ref=slot if src is None else src, dst_ref=slot,
                                            send_sem=send_sems.at[k], recv_sem=recv_sems.at[k],
                                            device_id=to, device_id_type=MESH)

    def _half(self, ref4, chip, which):
        return ref4.at[chip, pl.ds(which * self.half, self.half), :]

    def _ici_copy(self, j, ref4, own_ref, send_sems, recv_sems):
        x, y, c, chips = _place()
        cx, cy = chips[j]
        src = own_ref.at[pl.ds(c * self.half, self.half), :] if self.own_shard_apart else None
        return self._copy(j, self._half(ref4, 2 * x + y, c), (cx, cy, c), send_sems, recv_sems, src)

    def start(self, ref4, send_sems, recv_sems, which=(0, 1, 2), own_ref=None):
        for j in which:
            self._ici_copy(j, ref4, own_ref, send_sems, recv_sems).start()

    def start_diagonal_after_neighbours(self, ref4, send_sems, recv_sems, own_ref=None):
        for j in (0, 1):
            self._ici_copy(j, ref4, own_ref, send_sems, recv_sems).wait_send()
        self.start(ref4, send_sems, recv_sems, which=(2,), own_ref=own_ref)

    def pass_on(self, j, ref4, send_sems, recv_sems):
        x, y, c, chips = _place()
        cx, cy = chips[j]
        landed = self._half(ref4, 2 * cx + cy, c)
        self._copy(j, landed, (cx, cy, c), send_sems, recv_sems).wait_recv()
        self._copy(3 + j, landed, (x, y, 1 - c), send_sems, recv_sems).start()

    def await_sibling(self, j, ref4, send_sems, recv_sems):
        x, y, c, chips = _place()
        cx, cy = chips[j]
        self._copy(3 + j, self._half(ref4, 2 * cx + cy, 1 - c), (x, y, 1 - c), send_sems, recv_sems).wait_recv()

    def finish(self, ref4, send_sems, recv_sems, which=(0, 1, 2), own_ref=None):
        x, y, c, chips = _place()
        for j, (cx, cy) in enumerate(chips):
            if j in which:
                self._ici_copy(j, ref4, own_ref, send_sems, recv_sems).wait_send()
            self._copy(3 + j, self._half(ref4, 2 * cx + cy, c), (x, y, 1 - c), send_sems, recv_sems).wait_send()


class _GradExchange:
    def __init__(self, segs, with_blob, blob_chip_relations=tuple(range(N_CHIPS))):
        self.segs = tuple(segs)
        self.with_blob = with_blob
        self.blob_chip_relations = tuple(blob_chip_relations)

    def landing_w(self):
        return jax.ShapeDtypeStruct((N_CHIPS, 2, D_MODEL // 2, D_MODEL), BF16)

    def landing_blob(self, blob16):
        return jax.ShapeDtypeStruct((N_DEV, blob16.shape[1] // 2, D_MODEL), BF16)

    def semaphores(self):
        n_send = len(self.segs) + (2 * N_CHIPS if self.with_blob else 0)
        n_recv = 2 * N_CHIPS + (N_DEV if self.with_blob else 0)
        return [pltpu.SemaphoreType.DMA((n_send,)), pltpu.SemaphoreType.DMA((n_recv,))]

    def _copies(self, dw_ref, rxw_ref, blob_ref, rxb_ref, send_sems, recv_sems):
        x, y, c = lax.axis_index("x"), lax.axis_index("y"), lax.axis_index("c")
        chip = 2 * x + y

        def relation(kx, ky, h):
            return (x ^ kx) * 4 + (y ^ ky) * 2 + (c ^ h)

        def copy(src, dst, send_k, recv_k, to):
            return functools.partial(pltpu.make_async_remote_copy, src_ref=src, dst_ref=dst,
                                     send_sem=send_sems.at[send_k], recv_sem=recv_sems.at[recv_k],
                                     device_id=to, device_id_type=MESH)

        sends, recvs = [], []
        for i, s in enumerate(self.segs):
            kx, ky = (s // 2) >> 1, (s // 2) & 1
            r = (x ^ kx) * 2 + (y ^ ky)
            sends.append((r != 0, copy(dw_ref.at[i], rxw_ref.at[r, s % 2], i, 2 * r + s % 2, (kx, ky, c))))
        for j in range(2):
            mine = [s // 2 for s in self.segs if s % 2 == j]
            if mine:
                cond = functools.reduce(lambda a, b: a | b, [chip == k for k in mine])
                for r in range(1, N_CHIPS):
                    slot = rxw_ref.at[r, j]
                    recvs.append((cond, copy(slot, slot, 0, 2 * r + j, (x, y, c))))
        if self.with_blob:
            hb = blob_ref.shape[1] // 2
            first_send, first_recv = len(self.segs), 2 * N_CHIPS
            for k in range(N_CHIPS):
                for h in range(2):
                    r = relation(k >> 1, k & 1, h)
                    travels = functools.reduce(lambda a, b: a | b, [r // 2 == q for q in self.blob_chip_relations])
                    sends.append(((r != 0) & travels,
                                  copy(blob_ref.at[k, pl.ds(h * hb, hb), :], rxb_ref.at[r],
                                       first_send + 2 * k + h, first_recv + r, (k >> 1, k & 1, h))))
            for r in range(1, N_DEV):
                if r // 2 in self.blob_chip_relations:
                    slot = rxb_ref.at[r]
                    recvs.append((None, copy(slot, slot, 0, first_recv + r, (x, y, c))))
        return sends, recvs

    def start(self, *refs):
        sends, _ = self._copies(*refs)
        for cond, make in sends:
            pl.when(cond)(lambda make=make: make().start())

    def wait(self, *refs):
        sends, recvs = self._copies(*refs)
        for cond, make in sends:
            pl.when(cond)(lambda make=make: make().wait_send())
        for cond, make in recvs:
            if cond is None:
                make().wait_recv()
            else:
                pl.when(cond)(lambda make=make: make().wait_recv())


def _sum_landed(own, rx_ref):
    total = own
    for r in range(1, rx_ref.shape[0]):
        total = total + rx_ref[r, 0].astype(F32)
    return total


def _finish_w(dw_rec, dw_mix, rx_rec, rx_mix, place_arr):
    half = D_MODEL // 2
    tm = _tile(half, 256)
    n_rec = len(SEGS_REC)

    def body(place_ref, own_rec_ref, own_mix_ref, rx_rec_ref, rx_mix_ref, out_ref):
        seg = 2 * place_ref[0] + pl.program_id(0)

        @pl.when(seg < n_rec)
        def _():
            out_ref[0] = _sum_landed(own_rec_ref[0], rx_rec_ref)

        @pl.when(seg >= n_rec)
        def _():
            out_ref[0] = _sum_landed(own_mix_ref[0], rx_mix_ref)

    def own_spec(first, count):
        def index(j, i, place_ref):
            seg = 2 * place_ref[0] + j
            return (jnp.clip(seg - first, 0, count - 1), i, 0)
        return pl.BlockSpec((1, tm, D_MODEL), index)

    rx_spec = pl.BlockSpec((N_CHIPS, 1, tm, D_MODEL), lambda j, i, place_ref: (0, j, i, 0))
    return pl.pallas_call(
        body, name="finish_w",
        grid_spec=pltpu.PrefetchScalarGridSpec(
            num_scalar_prefetch=1, grid=(2, half // tm),
            in_specs=[own_spec(0, n_rec), own_spec(n_rec, len(SEGS_MIX)), rx_spec, rx_spec],
            out_specs=pl.BlockSpec((1, tm, D_MODEL), lambda j, i, place_ref: (place_ref[1], i, j))),
        out_shape=jax.ShapeDtypeStruct((2, half, 2 * D_MODEL), F32),
        compiler_params=_params(("arbitrary", "arbitrary")),
    )(place_arr, dw_rec, dw_mix, rx_rec, rx_mix)


def _finish_blob(dblob4, rx_blob, place_arr):
    n, rows, cols = rx_blob.shape
    tm = _tile(rows, 256)

    def body(place_ref, own_ref, rx_ref, out_ref):
        out_ref[0] = _sum_landed(own_ref[0, 0], rx_ref)

    return pl.pallas_call(
        body, name="finish_blob",
        grid_spec=pltpu.PrefetchScalarGridSpec(
            num_scalar_prefetch=1, grid=(rows // tm,),
            in_specs=[pl.BlockSpec((1, 1, tm, cols), lambda i, place_ref: (place_ref[0], place_ref[1], i, 0)),
                      pl.BlockSpec((n, 1, tm, cols), lambda i, place_ref: (0, 0, i, 0))],
            out_specs=pl.BlockSpec((1, tm, cols), lambda i, place_ref: (place_ref[1], i, 0))),
        out_shape=jax.ShapeDtypeStruct((2, rows, cols), F32),
        compiler_params=_params(("arbitrary",)),
    )(place_arr, dblob4.reshape(N_CHIPS, 2, rows, cols), rx_blob.reshape(n, 1, rows, cols))


def _share_finished(fw2, fb2, small_parts):
    def body(w_in_ref, b_in_ref, mix_ref, dmeta_ref, dnw_ref, dbrec_ref, dbmix_ref, dlb_ref, w_ref, b_ref, s_ref,
             rows_ref, local_sem, send_sems, recv_sems):
        x, y, c, _ = _place()
        sibling = (x, y, 1 - c)

        def copy(k, src, dst, to):
            return pltpu.make_async_remote_copy(src_ref=src, dst_ref=dst, send_sem=send_sems.at[k],
                                                recv_sem=recv_sems.at[k], device_id=to, device_id_type=MESH)

        sends = [copy(0, w_ref.at[c], w_ref.at[c], sibling), copy(1, b_ref.at[c], b_ref.at[c], sibling)]
        for cp in sends:
            cp.start()
        rows_ref[...] = mix_ref[...]
        rows_ref[ROW_META:ROW_META + N_META, :] = dmeta_ref[...]
        rows_ref[ROW_NORM_W:ROW_NORM_W + 1, :] = dnw_ref[...]
        for k, seg in enumerate(SEGS_REC):
            rows_ref[ROW_B_IN + seg:ROW_B_IN + seg + 1, :] = dbrec_ref[k]
        for k, seg in enumerate(SEGS_MIX):
            rows_ref[ROW_B_IN + seg:ROW_B_IN + seg + 1, :] = dbmix_ref[k]
        rows_ref[ROW_LB:ROW_LB + 1, :] = dlb_ref[...]
        for r in range(1, N_DEV):
            peer = (x ^ ((r >> 2) & 1), y ^ ((r >> 1) & 1), c ^ (r & 1))
            sends.append(copy(1 + r, rows_ref, s_ref.at[r], peer))
            sends[-1].start()
        own = pltpu.make_async_copy(rows_ref, s_ref.at[0], local_sem)
        own.start()
        own.wait()
        landed = [w_ref.at[1 - c], b_ref.at[1 - c]] + [s_ref.at[r] for r in range(1, N_DEV)]
        for k, slot in enumerate(landed):
            copy(k, slot, slot, (x, y, c)).wait_recv()
        for cp in sends:
            cp.wait_send()

    same = lambda a: jax.ShapeDtypeStruct(a.shape, a.dtype)
    n_sem = 2 + N_DEV - 1
    whole = pl.BlockSpec(memory_space=pltpu.VMEM)
    return pl.pallas_call(
        body, name="share_finished",
        in_specs=[ANY, ANY] + [whole] * len(small_parts), out_specs=[ANY, ANY, ANY],
        out_shape=[same(fw2), same(fb2), jax.ShapeDtypeStruct((N_DEV, SMALL_ROWS, D_MODEL), F32)],
        input_output_aliases={0: 0, 1: 1},
        scratch_shapes=[pltpu.VMEM((SMALL_ROWS, D_MODEL), F32), pltpu.SemaphoreType.DMA,
                        pltpu.SemaphoreType.DMA((n_sem,)), pltpu.SemaphoreType.DMA((n_sem,))],
    )(fw2, fb2, *small_parts)


def _sum_small(slots, lb_logits, me_arr):
    def body(me_ref, slots_ref, lbl_ref, out_ref):
        me = me_ref[0]
        total = slots_ref[me]
        for d in range(1, N_DEV):
            total = total + slots_ref[d ^ me]
        out_ref[...] = total
        out_ref[ROW_LOSS:ROW_LOSS + 1, :] = jnp.broadcast_to(
            jnp.sum(total[ROW_LOSS:ROW_LOSS + 1, :], axis=-1, keepdims=True), (1, D_MODEL))
        lb = _lower_bound(lbl_ref[...])
        g0 = total[ROW_LB:ROW_LB + 1, :] * lb * (1.0 - lb)
        out_ref[ROW_LB:ROW_LB + 1, :] = g0
        out_ref[ROW_LB + 1:ROW_LB + 2, :] = -g0

    return pl.pallas_call(
        body, name="sum_small",
        grid_spec=pltpu.PrefetchScalarGridSpec(
            num_scalar_prefetch=1, grid=(1,),
            in_specs=[pl.BlockSpec((N_DEV, SMALL_ROWS, D_MODEL), lambda i, me_ref: (0, 0, 0)),
                      pl.BlockSpec((2, D_MODEL), lambda i, me_ref: (0, 0))],
            out_specs=pl.BlockSpec((SMALL_ROWS, D_MODEL), lambda i, me_ref: (0, 0))),
        out_shape=jax.ShapeDtypeStruct((SMALL_ROWS, D_MODEL), F32),
        compiler_params=_params(("arbitrary",)),
    )(me_arr, slots, lb_logits)


def _adamw_step(w, g, m, v):
    c1 = 1.0 / (1.0 - ADAM_B1 ** ADAM_STEP)
    c2 = 1.0 / (1.0 - ADAM_B2 ** ADAM_STEP)
    nm = ADAM_B1 * m + (1.0 - ADAM_B1) * g
    nv = ADAM_B2 * v + (1.0 - ADAM_B2) * (g * g)
    return -ADAM_LR * ((nm * c1) / (jnp.sqrt(nv * c2) + ADAM_EPS) + ADAM_WD * w), nm, nv


SMALL_PARAMS = (("norm_w", ROW_NORM_W, 1), ("b_in", ROW_B_IN, N_SEG), ("lb_logits", ROW_LB, 2),
                ("hg_norm_w", ROW_HG_W, 1), ("pool_scale", ROW_POOL_SCALE, 1), ("final_norm_w", ROW_FINAL_W, 1))


def _update_small(tot, triples):
    n = len(SMALL_PARAMS)

    def body(tot_ref, *refs):
        ins, outs = refs[:3 * n], refs[3 * n:]
        for p, (name, row, n_rows) in enumerate(SMALL_PARAMS):
            w_ref, m_ref, v_ref = ins[3 * p:3 * p + 3]
            g_ref, d_ref, nm_ref, nv_ref = outs[4 * p:4 * p + 4]
            if w_ref.shape[0] == n_rows:
                pieces = [(slice(None), slice(None), tot_ref[row:row + n_rows, :])]
            else:
                pieces = [(slice(None), slice(k * D_MODEL, (k + 1) * D_MODEL), tot_ref[row + k:row + k + 1, :])
                          for k in range(n_rows)]
            for rows_, cols_, g in pieces:
                d, nm, nv = _adamw_step(w_ref[rows_, cols_], g, m_ref[rows_, cols_], v_ref[rows_, cols_])
                g_ref[rows_, cols_] = g
                d_ref[rows_, cols_] = d
                nm_ref[rows_, cols_] = nm
                nv_ref[rows_, cols_] = nv

    whole = pl.BlockSpec(memory_space=pltpu.VMEM)
    flat = [a for t in triples for a in t]
    out_shape = [jax.ShapeDtypeStruct(t[0].shape, F32) for t in triples for _ in range(4)]
    outs = pl.pallas_call(
        body, name="update_small",
        in_specs=[whole] * (1 + len(flat)), out_specs=[whole] * len(out_shape), out_shape=out_shape,
        compiler_params=_params(),
    )(tot, *flat)
    return [tuple(outs[4 * p:4 * p + 4]) for p in range(n)]


def _update_blob(g_blob, triples):
    q_rows = triples[0][0].shape[0]
    pool_rows = triples[3][0].shape[0]
    steps = q_rows // pool_rows

    def body(*refs):
        ins, outs = refs[:16], refs[16:]
        for p in range(4):
            g_ref, (w_ref, m_ref, v_ref) = ins[p], ins[4 + 3 * p:7 + 3 * p]
            go_ref, d_ref, nm_ref, nv_ref = outs[4 * p:4 * p + 4]

            def update():
                g = g_ref[...]
                go_ref[...] = g
                d_ref[...], nm_ref[...], nv_ref[...] = _adamw_step(w_ref[...], g, m_ref[...], v_ref[...])

            if p < 3:
                update()
            else:
                pl.when(pl.program_id(0) == 0)(update)

    blk = pl.BlockSpec((pool_rows, D_MODEL), lambda i: (i, 0))
    once = pl.BlockSpec((pool_rows, D_MODEL), lambda i: (0, 0))
    g_specs = [pl.BlockSpec((pool_rows, D_MODEL), lambda i, p=p: (steps * p + i, 0)) for p in range(3)]
    g_specs.append(pl.BlockSpec((pool_rows, D_MODEL), lambda i: (3 * steps, 0)))
    piece_specs = [blk] * 9 + [once] * 3
    out_specs = [blk] * 12 + [once] * 4
    out_shape = [jax.ShapeDtypeStruct(t[0].shape, F32) for t in triples for _ in range(4)]
    outs = pl.pallas_call(
        body, name="update_blob",
        grid=(steps,), in_specs=g_specs + piece_specs, out_specs=out_specs, out_shape=out_shape,
        compiler_params=_params(("arbitrary",)),
    )(g_blob, g_blob, g_blob, g_blob, *[a for t in triples for a in t])
    return [tuple(outs[4 * p:4 * p + 4]) for p in range(4)]


def _adamw(w, g, m, v):
    rows, cols = w.shape
    tm = _tile(rows, 256, mult=8) if rows % 8 == 0 else rows

    def body(w_ref, g_ref, m_ref, v_ref, d_ref, nm_ref, nv_ref):
        d_ref[...], nm_ref[...], nv_ref[...] = _adamw_step(w_ref[...], g_ref[...], m_ref[...], v_ref[...])

    blk = pl.BlockSpec((tm, cols), lambda i: (i, 0))
    sds = jax.ShapeDtypeStruct((rows, cols), F32)
    return pl.pallas_call(
        body, name="adamw",
        grid=(rows // tm,), in_specs=[blk] * 4, out_specs=[blk] * 3, out_shape=[sds] * 3,
        compiler_params=_params(("arbitrary",)),
    )(w, g, m, v)


def kernel(x, meta_tokens, norm_w, w_in, b_in, lb_logits, hg_norm_w, pool_w, pool_scale, w_down_hg, w_down_pool, w_out, final_norm_w, loss_target, m_meta_tokens, m_norm_w, m_w_in, m_b_in, m_lb_logits, m_hg_norm_w, m_pool_w, m_pool_scale, m_w_down_hg, m_w_down_pool, m_w_out, m_final_norm_w, v_meta_tokens, v_norm_w, v_w_in, v_b_in, v_lb_logits, v_hg_norm_w, v_pool_w, v_pool_scale, v_w_down_hg, v_w_down_pool, v_w_out, v_final_norm_w):
    seq = x.shape[1]
    xi, yi, ci = lax.axis_index("x"), lax.axis_index("y"), lax.axis_index("c")
    chip = 2 * xi + yi
    place_arr = jnp.stack([chip, ci]).astype(jnp.int32)
    me_arr = jnp.reshape(4 * xi + 2 * yi + ci, (1,)).astype(jnp.int32)
    q = D_MODEL // N_CHIPS

    def blob_of(wdh, wdp, wo, pw):
        return jnp.concatenate([wdh[0], wdp[0], wo[0], pw[0].reshape(-1, D_MODEL)], axis=0)

    def in_every_slot(a):
        return jnp.broadcast_to(a[None], (N_CHIPS,) + a.shape)

    m4 = in_every_slot(meta_tokens)
    w16 = w_in[0].astype(BF16)
    blob4 = in_every_slot(blob_of(w_down_hg, w_down_pool, w_out, pool_w).astype(BF16))
    seg_order = jnp.stack([2 * (chip ^ rel) + t for rel in (0, 2, 1, 3) for t in (0, 1)]).astype(jnp.int32)

    fw2 = final_norm_w.reshape(1, D_MODEL)
    d_tokens, w_parts, blob_parts, small = _local_step(
        x[0], m4, loss_target[0], w16, blob4, seg_order, norm_w, b_in, lb_logits, hg_norm_w, pool_scale, fw2)
    grad_x = d_tokens[None]

    fin_w = _finish_w(*w_parts, place_arr)
    fin_b = _finish_blob(*blob_parts, place_arr)
    gw2, gb2, slots = _share_finished(fin_w, fin_b, small)
    tot = _sum_small(slots, lb_logits, me_arr)
    g_w_in = gw2.reshape(D_MODEL, 2 * D_MODEL)
    g_blob = gb2.reshape(-1, D_MODEL)

    d_win, nm_win, nv_win = _adamw(w_in[0], g_w_in, m_w_in[0], v_w_in[0])
    pool_rows = lambda a: a[0].reshape(-1, D_MODEL)
    blob_results = _update_blob(g_blob, [
        (w_down_hg[0], m_w_down_hg[0], v_w_down_hg[0]), (w_down_pool[0], m_w_down_pool[0], v_w_down_pool[0]),
        (w_out[0], m_w_out[0], v_w_out[0]), (pool_rows(pool_w), pool_rows(m_pool_w), pool_rows(v_pool_w))])
    g_meta = lax.dynamic_slice_in_dim(tot[ROW_META:ROW_META + N_META], chip * q, q, axis=1)
    d_meta, nm_meta, nv_meta = _adamw(meta_tokens, g_meta, m_meta_tokens, v_meta_tokens)

    as_row = lambda a: a.reshape(1, D_MODEL)
    small_results = _update_small(tot, [
        (norm_w, m_norm_w, v_norm_w), (b_in, m_b_in, v_b_in), (lb_logits, m_lb_logits, v_lb_logits),
        (hg_norm_w, m_hg_norm_w, v_hg_norm_w), (pool_scale, m_pool_scale, v_pool_scale),
        (as_row(final_norm_w), as_row(m_final_norm_w), as_row(v_final_norm_w))])

    def leaves(kind, meta_part, win_part):
        nw, bi, lbl, hg, ps, fw = [r[kind] for r in small_results]
        wdh, wdp, wo, pw = [r[kind] for r in blob_results]
        return [meta_part, nw, win_part[None], bi, lbl, hg, pw.reshape(pool_w.shape), ps,
                wdh[None], wdp[None], wo[None], fw.reshape(D_MODEL)]

    loss = tot[ROW_LOSS, 0]
    return (loss, grad_x,
            *leaves(0, g_meta, g_w_in),
            *leaves(1, d_meta, d_win),
            *leaves(2, nm_meta, nm_win),
            *leaves(3, nv_meta, nv_win))
```

```python
import functools

import numpy as np
import jax
import jax.numpy as jnp
from jax import lax
from jax.experimental import pallas as pl
from jax.experimental.pallas import tpu as pltpu

F32 = jnp.float32
BF16 = jnp.bfloat16

D_MODEL = 1024
N_SEG = 8
N_HEADS = 8
HEAD_DIM = 128
CHUNK = 64
N_META = 16
PAD_ROWS = CHUNK - N_META
FIRST_TOKEN_ROW = CHUNK
LEVELS = (32, 16, 8, 4, 2, 1)
N_EXP = 2 + len(LEVELS)
POOL_WINDOWS = (2, 4, 8, 16)
POOL_GDIM = D_MODEL // len(POOL_WINDOWS)
HALO = 16
FORWARD_HEADS_PER_STEP = 4
BACKWARD_HEADS_PER_STEP = 2
LOCAL_UNROLL = 13
BACKWARD_UNROLL = 13
EPS = 1e-6
N_CHIPS = 4
N_DEV = 8
BLOB_RELATIONS_DIRECT = (0, 1, 2)
BLOB_RELATIONS_DIAGONAL = (3,)
SEGS_REC = (0, 1, 2)
SEGS_MIX = (3, 4, 5, 6, 7)

ADAM_LR = 0.001
ADAM_B1 = 0.9
ADAM_B2 = 0.999
ADAM_EPS = 1e-08
ADAM_WD = 0.01
ADAM_STEP = 10

VMEM_LIMIT_BYTES = 56 * 1024 * 1024

ROW_LOSS = 0
ROW_META = 1
ROW_NORM_W = ROW_META + N_META
ROW_B_IN = ROW_NORM_W + 1
ROW_LB = ROW_B_IN + N_SEG
ROW_HG_W = ROW_LB + 2
ROW_POOL_SCALE = ROW_HG_W + 1
ROW_FINAL_W = ROW_POOL_SCALE + 1
SMALL_ROWS = 32


def _tile(total, cap, mult=16):
    best = None
    for t in range(mult, min(total, cap) + 1, mult):
        if total % t == 0:
            best = t
    assert best is not None, (total, cap, mult)
    return best


def _token_window(tm, tile_of):
    def index(*grid):
        return (pl.multiple_of(jnp.maximum(tile_of(*grid) * tm - FIRST_TOKEN_ROW, 0), HALO), 0)
    return pl.BlockSpec((pl.Element(tm), pl.Element(D_MODEL)), index)


def _padded_tile(window, head, tile):
    first = jnp.concatenate([head, pltpu.roll(window, FIRST_TOKEN_ROW, 0)[FIRST_TOKEN_ROW:]], axis=0)
    return jnp.where(tile == 0, first, window)


def _params(sem=None):
    return pltpu.CompilerParams(dimension_semantics=sem, vmem_limit_bytes=VMEM_LIMIT_BYTES)


def _dot(a, b):
    return jnp.dot(a, b, preferred_element_type=F32)


def _dot_nt(a, b):
    return lax.dot_general(a, b, (((1,), (1,)), ((), ())), preferred_element_type=F32)


def _dot_tn(a, b):
    return lax.dot_general(a, b, (((0,), (0,)), ((), ())), preferred_element_type=F32)


def _sigmoid_pair(x):
    t = jnp.exp(-jnp.abs(x))
    r = 1.0 / (1.0 + t)
    pos = x >= 0
    return jnp.where(pos, r, t * r), jnp.where(pos, t * r, r)


def _exponent_matrix():
    t = np.arange(CHUNK)[:, None]
    j = np.arange(CHUNK)[None, :]
    blocks = [j <= t, j > t]
    for m in LEVELS:
        rho = (t // (2 * m)) * (2 * m) + m
        upper = (t >= rho) & (j > rho) & (j <= t)
        lower = (t < rho) & (j > t) & (j <= rho)
        blocks.append(upper | lower)
    return np.concatenate(blocks, axis=0).astype(np.float32)


def _pair_masks():
    t = np.arange(CHUNK)[:, None]
    s = np.arange(CHUNK)[None, :]
    masks = [t == s]
    for m in LEVELS:
        same = (t // (2 * m)) == (s // (2 * m))
        masks.append(same & ((t % (2 * m)) >= m) & ((s % (2 * m)) < m))
    return np.stack(masks).astype(np.float32)


LEVEL_PAIRS = ((0, 1), (2, 3), (4, 5), (6, None))


def _paired_masks():
    m = _pair_masks()
    zero = np.zeros_like(m[0])
    return np.stack([np.concatenate([m[a], zero if b is None else m[b]], axis=1) for a, b in LEVEL_PAIRS])


def _lower_bound(lbl):
    return 1.0 / (1.0 + jnp.exp(lbl[1:2, :] - lbl[0:1, :]))


def _in_proj(tokens, m4, norm_w, w16, b_in, seg_order, rows):
    tm = _tile(rows, 1040)
    nt = rows // tm
    gather = _ShardGather(w16.shape[0], own_shard_apart=True)

    def body(order_ref, z_ref, nw_ref, b_ref, w_in_ref, m_in_ref, h_ref, p_ref, w4_ref, head_ref, m4_ref,
             h_all, w_buf, w_sem, fill_sem, send_sems, recv_sems, meta_send, meta_recv, meta_sem):
        kk, i = pl.program_id(0), pl.program_id(1)

        @pl.when((kk == 0) & (i == 0))
        def _():
            x, y, c, chips = _place()

            def meta_copy(j, chip, to):
                return pltpu.make_async_remote_copy(
                    src_ref=m4_ref.at[chip], dst_ref=m4_ref.at[chip], send_sem=meta_send.at[j],
                    recv_sem=meta_recv.at[j], device_id=to, device_id_type=MESH)

            sends = [meta_copy(j, 2 * x + y, (cx, cy, c)) for j, (cx, cy) in enumerate(chips)]
            for cp in sends:
                cp.start()
            gather.start(w4_ref, send_sems, recv_sems, which=(0, 1), own_ref=w_in_ref)
            for j, (cx, cy) in enumerate(chips):
                meta_copy(j, 2 * cx + cy, (x, y, c)).wait_recv()
            for cp in sends:
                cp.wait_send()
            head_ref[0:PAD_ROWS, :] = jnp.zeros((PAD_ROWS, D_MODEL), F32)
            q_cols = D_MODEL // N_CHIPS
            for k in range(N_CHIPS):
                cp = pltpu.make_async_copy(
                    m4_ref.at[k], head_ref.at[pl.ds(PAD_ROWS, N_META), pl.ds(k * q_cols, q_cols)], meta_sem)
                cp.start()
                cp.wait()

        @pl.when((kk == 2) & (i == 0))
        def _():
            gather.start_diagonal_after_neighbours(w4_ref, send_sems, recv_sems, own_ref=w_in_ref)

        @pl.when(kk == 0)
        def _():
            zt = _padded_tile(z_ref[...], head_ref[...], i)
            rstd = lax.rsqrt(jnp.mean(zt * zt, axis=-1, keepdims=True) + EPS)
            h = (zt * rstd * nw_ref[...]).astype(BF16)
            h_all[pl.ds(pl.multiple_of(i * tm, 16), tm), :] = h
            h_ref[...] = h

        @pl.when((kk == 2) & (i == 0))
        def _():
            gather.pass_on(0, w4_ref, send_sems, recv_sems)
            gather.pass_on(1, w4_ref, send_sems, recv_sems)
            gather.await_sibling(0, w4_ref, send_sems, recv_sems)

        @pl.when((kk == 4) & (i == 0))
        def _():
            gather.await_sibling(1, w4_ref, send_sems, recv_sems)

        @pl.when((kk == 5) & (i == 0))
        def _():
            gather.pass_on(2, w4_ref, send_sems, recv_sems)

        @pl.when((kk == 6) & (i == 0))
        def _():
            gather.await_sibling(2, w4_ref, send_sems, recv_sems)

        def weights(which, own):
            seg = order_ref[2 * (kk // 2) + which]
            cols = pl.ds(pl.multiple_of((seg % 2) * D_MODEL, D_MODEL), D_MODEL)
            src = w_in_ref.at[:, cols] if own else w4_ref.at[seg // 2, :, cols]
            return pltpu.make_async_copy(src, w_buf.at[which], w_sem.at[which])

        def fill_own_slot(which):
            seg = order_ref[which]
            cols = pl.ds(pl.multiple_of((seg % 2) * D_MODEL, D_MODEL), D_MODEL)
            return pltpu.make_async_copy(w_buf.at[which], w4_ref.at[seg // 2, :, cols], fill_sem.at[which])

        @pl.when((i == 0) & (kk == 0))
        def _():
            weights(0, True).start()
            weights(1, True).start()
            weights(0, True).wait()
            fill_own_slot(0).start()

        @pl.when((i == 0) & (kk == 1))
        def _():
            weights(1, True).wait()
            fill_own_slot(1).start()

        @pl.when((i == 0) & (kk == 2))
        def _():
            fill_own_slot(0).wait()
            fill_own_slot(1).wait()

        @pl.when((i == 0) & (kk % 2 == 0) & (kk > 0))
        def _():
            weights(0, False).start()
            weights(1, False).start()
            weights(0, False).wait()

        @pl.when((i == 0) & (kk % 2 == 1) & (kk > 1))
        def _():
            weights(1, False).wait()

        p_ref[0] = _dot(h_all[pl.ds(pl.multiple_of(i * tm, 16), tm), :], w_buf[kk % 2]) + b_ref[...]

        @pl.when((kk == N_SEG - 1) & (i == nt - 1))
        def _():
            gather.finish(w4_ref, send_sems, recv_sems, which=(2,), own_ref=w_in_ref)

    first_pass = lambda kk, i, order_ref: (jnp.where(kk == 0, i, nt - 1), 0)
    return pl.pallas_call(
        body, name="in_proj",
        grid_spec=pltpu.PrefetchScalarGridSpec(
            num_scalar_prefetch=1, grid=(N_SEG, nt),
            in_specs=[
                _token_window(tm, lambda kk, i, order_ref: jnp.where(kk == 0, i, nt - 1)),
                pl.BlockSpec((1, D_MODEL), lambda kk, i, order_ref: (0, 0)),
                pl.BlockSpec((1, D_MODEL), lambda kk, i, order_ref: (0, order_ref[kk])),
                ANY, ANY,
            ],
            out_specs=[
                pl.BlockSpec((tm, D_MODEL), first_pass),
                pl.BlockSpec((1, tm, D_MODEL), lambda kk, i, order_ref: (order_ref[kk], i, 0)),
                ANY,
                pl.BlockSpec((FIRST_TOKEN_ROW, D_MODEL), lambda kk, i, order_ref: (0, 0)),
                ANY,
            ],
            scratch_shapes=[
                pltpu.VMEM((rows, D_MODEL), BF16),
                pltpu.VMEM((2, D_MODEL, D_MODEL), BF16),
                pltpu.SemaphoreType.DMA((2,)), pltpu.SemaphoreType.DMA((2,)),
            ] + gather.semaphores() + [
                pltpu.SemaphoreType.DMA((N_CHIPS - 1,)), pltpu.SemaphoreType.DMA((N_CHIPS - 1,)),
                pltpu.SemaphoreType.DMA,
            ]),
        out_shape=[
            jax.ShapeDtypeStruct((rows, D_MODEL), BF16),
            jax.ShapeDtypeStruct((N_SEG, rows, D_MODEL), F32),
            jax.ShapeDtypeStruct((N_CHIPS,) + w16.shape, w16.dtype),
            jax.ShapeDtypeStruct((FIRST_TOKEN_ROW, D_MODEL), F32),
            jax.ShapeDtypeStruct(m4.shape, m4.dtype),
        ],
        input_output_aliases={5: 4},
        compiler_params=_params(("arbitrary", "arbitrary")),
    )(seg_order, tokens, norm_w, b_in, w16, m4)


def _hgrn_forward(p3, lb_logits, wexp2, masks2, blob16, rows):
    n_chunks = rows // CHUNK
    cpb = _tile(n_chunks, 13, mult=1)
    rb_rows = cpb * CHUNK
    n_rb = n_chunks // cpb
    lanes = cpb * HEAD_DIM
    hps = FORWARD_HEADS_PER_STEP
    n_hb = N_HEADS // hps
    width = hps * HEAD_DIM
    gather = _ShardGather(blob16.shape[0], own_shard_apart=True)

    def body(q_ref, fz_ref, v_ref, lbl_ref, wexp_ref, mask_ref, b_in_ref, o_ref, s_ref, e16_ref, a2_ref, b4_ref,
             st_all, e_all, u_all, q_all, kk_all, v_all, send_sems, recv_sems, own_buf, own_sems):
        rb = pl.program_id(1)

        def own_in():
            return pltpu.make_async_copy(b_in_ref, own_buf, own_sems.at[0])

        def own_out():
            x, y, _, _ = _place()
            return pltpu.make_async_copy(own_buf, b4_ref.at[2 * x + y], own_sems.at[1])

        @pl.when((pl.program_id(0) == 0) & (rb == 0))
        def _():
            gather.start(b4_ref, send_sems, recv_sems, own_ref=b_in_ref)
            own_in().start()

        @pl.when(rb == 0)
        def _():
            st_all[...] = jnp.zeros_like(st_all)

        for j in range(hps):
            cols = pl.ds(j * HEAD_DIM, HEAD_DIM)
            one_head(rb, q_ref.at[0, :, cols], fz_ref.at[0, :, cols], v_ref.at[0, :, cols], lbl_ref.at[:, cols],
                     wexp_ref, mask_ref, o_ref.at[:, cols], s_ref.at[j], e16_ref.at[j, 0], a2_ref.at[:, cols],
                     st_all.at[j], e_all.at[j], u_all.at[j], q_all.at[j], kk_all.at[j], v_all.at[j])

        @pl.when((pl.program_id(0) == n_hb // 2) & (rb == 0))
        def _():
            for j in range(N_CHIPS - 1):
                gather.pass_on(j, b4_ref, send_sems, recv_sems)
            own_in().wait()
            own_out().start()

        @pl.when((pl.program_id(0) == n_hb - 1) & (rb == n_rb - 1))
        def _():
            for j in range(N_CHIPS - 1):
                gather.await_sibling(j, b4_ref, send_sems, recv_sems)
            gather.finish(b4_ref, send_sems, recv_sems, own_ref=b_in_ref)
            own_out().wait()

    def one_head(rb, q_ref, fz_ref, v_ref, lbl_ref, wexp_ref, mask_ref, o_ref, s_ref, e16_ref, a2_ref,
                 st_ref, e_ref, u_ref, q_s, kk_s, v_s):
        lb = _lower_bound(lbl_ref[...])
        row = rb * rb_rows + lax.broadcasted_iota(jnp.int32, (rb_rows, 1), 0)
        valid = row >= PAD_ROWS
        sg, sn = _sigmoid_pair(fz_ref[...])
        g = jnp.where(valid, jnp.log(lb + (1.0 - lb) * sg), 0.0)
        kk_s[...] = jnp.where(valid, (1.0 - lb) * sn, 0.0)
        q_s[...] = jnp.where(valid, q_ref[...], 0.0)
        v_s[...] = jnp.where(valid, v_ref[...], 0.0).astype(BF16)
        hi = g.astype(BF16)
        mid = (g - hi.astype(F32)).astype(BF16)
        g2 = jnp.concatenate(
            [jnp.concatenate([hi[b * CHUNK:(b + 1) * CHUNK], mid[b * CHUNK:(b + 1) * CHUNK]], axis=0)
             for b in range(cpb)], axis=1)
        e_ref[...] = jnp.exp(_dot(wexp_ref[...], g2))
        e16_ref[...] = e_ref[...].astype(BF16)

        def contribution(b, carry):
            r0 = pl.multiple_of(b * CHUNK, CHUNK)
            l0 = pl.multiple_of(b * HEAD_DIM, HEAD_DIM)
            kc16 = (kk_s[pl.ds(r0, CHUNK), :] * e_ref[CHUNK:2 * CHUNK, pl.ds(l0, HEAD_DIM)]).astype(BF16)
            u_ref[b] = _dot_tn(v_s[pl.ds(r0, CHUNK), :], kc16)
            return carry

        lax.fori_loop(0, cpb, contribution, 0, unroll=LOCAL_UNROLL)

        def recur(b, st):
            l0 = pl.multiple_of(b * HEAD_DIM, HEAD_DIM)
            s_ref[b] = st
            return st * e_ref[CHUNK - 1:CHUNK, pl.ds(l0, HEAD_DIM)] + u_ref[b]

        st_ref[...] = lax.fori_loop(0, cpb, recur, st_ref[...], unroll=LOCAL_UNROLL)

        zeros16 = jnp.zeros((CHUNK, HEAD_DIM), BF16)

        def local(b, carry):
            r0 = pl.multiple_of(b * CHUNK, CHUNK)
            l0 = pl.multiple_of(b * HEAD_DIM, HEAD_DIM)
            q = q_s[pl.ds(r0, CHUNK), :]
            kk = kk_s[pl.ds(r0, CHUNK), :]
            v16 = v_s[pl.ds(r0, CHUNK), :]

            def scaled(entry):
                if entry == 0:
                    return q.astype(BF16), kk.astype(BF16)
                e_m = e_ref[(1 + entry) * CHUNK:(2 + entry) * CHUNK, pl.ds(l0, HEAD_DIM)]
                return (q * e_m).astype(BF16), (kk * e_m).astype(BF16)

            a2 = jnp.zeros((CHUNK, 2 * CHUNK), F32)
            for p, (ea, eb) in enumerate(LEVEL_PAIRS):
                qa, ka = scaled(ea)
                if eb is None:
                    prod = _dot_nt(qa, jnp.concatenate([ka, zeros16], axis=0))
                else:
                    qb_, kb_ = scaled(eb)
                    rhs = jnp.concatenate([jnp.concatenate([ka, zeros16], axis=1),
                                           jnp.concatenate([zeros16, kb_], axis=1)], axis=0)
                    prod = _dot_nt(jnp.concatenate([qa, qb_], axis=1), rhs)
                a2 = a2 + mask_ref[p] * prod
            a2_16 = a2.astype(BF16)
            a2_ref[pl.ds(r0, CHUNK), :] = a2_16
            qb16 = (q * e_ref[0:CHUNK, pl.ds(l0, HEAD_DIM)]).astype(BF16)
            o_ref[pl.ds(r0, CHUNK), :] = (_dot(a2_16, jnp.concatenate([v16, v16], axis=0))
                                          + _dot_nt(qb16, s_ref[b].astype(BF16)))
            return carry

        lax.fori_loop(0, cpb, local, 0, unroll=LOCAL_UNROLL)

    head_block = lambda seg: pl.BlockSpec((1, rb_rows, width), lambda h, r: (seg, r, h))
    return pl.pallas_call(
        body, name="hgrn_forward",
        grid=(n_hb, n_rb),
        in_specs=[
            head_block(0), head_block(1), head_block(2),
            pl.BlockSpec((2, width), lambda h, r: (0, h)),
            pl.BlockSpec((N_EXP * CHUNK, 2 * CHUNK), lambda h, r: (0, 0)),
            pl.BlockSpec((len(LEVEL_PAIRS), CHUNK, 2 * CHUNK), lambda h, r: (0, 0, 0)),
            ANY,
        ],
        out_specs=[
            pl.BlockSpec((rb_rows, width), lambda h, r: (r, h)),
            pl.BlockSpec((hps, cpb, HEAD_DIM, HEAD_DIM), lambda h, r: (h, r, 0, 0)),
            pl.BlockSpec((hps, 1, N_EXP * CHUNK, lanes), lambda h, r: (h, r, 0, 0)),
            pl.BlockSpec((rb_rows, width), lambda h, r: (r, h)),
            ANY,
        ],
        out_shape=[
            jax.ShapeDtypeStruct((rows, D_MODEL), F32),
            jax.ShapeDtypeStruct((N_HEADS, n_chunks, HEAD_DIM, HEAD_DIM), F32),
            jax.ShapeDtypeStruct((N_HEADS, n_rb, N_EXP * CHUNK, lanes), BF16),
            jax.ShapeDtypeStruct((rows, D_MODEL), BF16),
            jax.ShapeDtypeStruct((N_CHIPS,) + blob16.shape, blob16.dtype),
        ],
        scratch_shapes=[
            pltpu.VMEM((hps, HEAD_DIM, HEAD_DIM), F32),
            pltpu.VMEM((hps, N_EXP * CHUNK, lanes), F32),
            pltpu.VMEM((hps, cpb, HEAD_DIM, HEAD_DIM), F32),
            pltpu.VMEM((hps, rb_rows, HEAD_DIM), F32),
            pltpu.VMEM((hps, rb_rows, HEAD_DIM), F32),
            pltpu.VMEM((hps, rb_rows, HEAD_DIM), BF16),
        ] + gather.semaphores() + [
            pltpu.VMEM(blob16.shape, blob16.dtype),
            pltpu.SemaphoreType.DMA((2,)),
        ],
        compiler_params=_params(("arbitrary", "arbitrary")),
    )(p3, p3, p3, lb_logits, wexp2, masks2, blob16)


def _hgrn_backward(p3, d_o, states, e16, a2, lb_logits, wexp_t, masks2, dw16, blob16, rx_blob, rows):
    n_chunks = rows // CHUNK
    cpb = _tile(n_chunks, 13, mult=1)
    rb_rows = cpb * CHUNK
    n_rb = n_chunks // cpb
    lanes = cpb * HEAD_DIM
    exchange = _GradExchange(SEGS_MIX, True, BLOB_RELATIONS_DIAGONAL)

    hps = BACKWARD_HEADS_PER_STEP
    n_hb = N_HEADS // hps
    width = hps * HEAD_DIM

    def body(q_ref, fz_ref, v_ref, do_ref, s_ref, e_ref, a2_ref, lbl_ref, wexpt_ref, mask_ref, dw_ref, blob_ref,
             rxb_in_ref, dp_ref, dlb_ref, rxw_ref, rxb_ref, *scratch):
        per_head, (send_sems, recv_sems) = scratch[:-2], scratch[-2:]
        step = pl.program_id(1)
        rb = n_rb - 1 - step

        @pl.when((pl.program_id(0) == 0) & (step == 0))
        def _():
            exchange.start(dw_ref, rxw_ref, blob_ref, rxb_ref, send_sems, recv_sems)

        for j in range(hps):
            cols = pl.ds(j * HEAD_DIM, HEAD_DIM)
            one_head(step, rb, q_ref.at[0, :, cols], fz_ref.at[0, :, cols], v_ref.at[0, :, cols], do_ref.at[:, cols],
                     s_ref.at[j], e_ref.at[j, 0], a2_ref.at[:, cols], lbl_ref.at[:, cols], wexpt_ref, mask_ref,
                     dp_ref.at[:, :, cols], dlb_ref.at[:, cols], *[ref.at[j] for ref in per_head])

        @pl.when((pl.program_id(0) == n_hb - 1) & (step == n_rb - 1))
        def _():
            exchange.wait(dw_ref, rxw_ref, blob_ref, rxb_ref, send_sems, recv_sems)

    def one_head(step, rb, q_ref, fz_ref, v_ref, do_ref, s_ref, e_ref, a2_ref, lbl_ref, wexpt_ref, mask_ref,
                 dp_ref, dlb_ref, dst_ref, g_ref, dsn_ref, q_s, kk_s, v_s, do_s, dq_s, dkk_s, dg_s, dx_s, da2_s):
        @pl.when(step == 0)
        def _():
            dst_ref[...] = jnp.zeros_like(dst_ref)
            dlb_ref[...] = jnp.zeros_like(dlb_ref)

        lb = _lower_bound(lbl_ref[...])
        row = rb * rb_rows + lax.broadcasted_iota(jnp.int32, (rb_rows, 1), 0)
        valid = row >= PAD_ROWS
        sg, sn = _sigmoid_pair(fz_ref[...])
        f = lb + (1.0 - lb) * sg
        g = jnp.where(valid, jnp.log(f), 0.0)
        kk_s[...] = jnp.where(valid, (1.0 - lb) * sn, 0.0)
        q_s[...] = jnp.where(valid, q_ref[...], 0.0)
        v_s[...] = jnp.where(valid, v_ref[...], 0.0).astype(BF16)
        do_s[...] = do_ref[...].astype(BF16)
        e_last_all = jnp.exp(jnp.concatenate(
            [jnp.sum(g[b * CHUNK:(b + 1) * CHUNK], axis=0, keepdims=True) for b in range(cpb)], axis=0))
        last_row = lax.broadcasted_iota(jnp.int32, (CHUNK, 1), 0) == CHUNK - 1
        zeros16 = jnp.zeros((CHUNK, HEAD_DIM), BF16)

        def factor(block, l0):
            return e_ref[block * CHUNK:(block + 1) * CHUNK, pl.ds(l0, HEAD_DIM)].astype(F32)

        def contribution(b, carry):
            r0 = pl.multiple_of(b * CHUNK, CHUNK)
            l0 = pl.multiple_of(b * HEAD_DIM, HEAD_DIM)
            qb16 = (q_s[pl.ds(r0, CHUNK), :] * factor(0, l0)).astype(BF16)
            g_ref[b] = _dot_tn(do_s[pl.ds(r0, CHUNK), :], qb16)
            return carry

        lax.fori_loop(0, cpb, contribution, 0, unroll=LOCAL_UNROLL)

        cur = dst_ref[...]
        for b in reversed(range(cpb)):
            dsn_ref[b] = cur
            cur = cur * e_last_all[b:b + 1, :] + g_ref[b]
        dst_ref[...] = cur

        def through_state(b, carry):
            r0 = pl.multiple_of(b * CHUNK, CHUNK)
            l0 = pl.multiple_of(b * HEAD_DIM, HEAD_DIM)
            v16 = v_s[pl.ds(r0, CHUNK), :]
            do16 = do_s[pl.ds(r0, CHUNK), :]
            st = s_ref[b]
            dsn = dsn_ref[b]
            dsn16 = dsn.astype(BF16)
            e_b, e_c = factor(0, l0), factor(1, l0)
            qb, kc = q_s[pl.ds(r0, CHUNK), :] * e_b, kk_s[pl.ds(r0, CHUNK), :] * e_c

            t = _dot_tn(a2_ref[pl.ds(r0, CHUNK), :], do16)
            dv = t[0:CHUNK] + t[CHUNK:2 * CHUNK] + _dot_nt(kc.astype(BF16), dsn16)
            dp_ref[2, pl.ds(r0, CHUNK), :] = dv.astype(BF16)
            da2_s[pl.ds(r0, CHUNK), :] = _dot_nt(do16, jnp.concatenate([v16, v16], axis=0))
            dqb = _dot(do16, st.astype(BF16))
            dkc = _dot(v16, dsn16)
            de = jnp.sum(dsn * st, axis=0, keepdims=True) * e_b[CHUNK - 1:CHUNK, :]
            dq_s[pl.ds(r0, CHUNK), :] = e_b * dqb
            dkk_s[pl.ds(r0, CHUNK), :] = e_c * dkc
            dx_s[0:CHUNK, pl.ds(l0, HEAD_DIM)] = (qb * dqb + jnp.where(last_row, de, 0.0)).astype(BF16)
            dx_s[CHUNK:2 * CHUNK, pl.ds(l0, HEAD_DIM)] = (kc * dkc).astype(BF16)
            return carry

        lax.fori_loop(0, cpb, through_state, 0, unroll=BACKWARD_UNROLL)

        def local(b, carry):
            r0 = pl.multiple_of(b * CHUNK, CHUNK)
            l0 = pl.multiple_of(b * HEAD_DIM, HEAD_DIM)
            q = q_s[pl.ds(r0, CHUNK), :]
            kk = kk_s[pl.ds(r0, CHUNK), :]
            da2 = da2_s[pl.ds(r0, CHUNK), :]
            dq = dq_s[pl.ds(r0, CHUNK), :]
            dkk = dkk_s[pl.ds(r0, CHUNK), :]

            def scaled(entry):
                if entry == 0:
                    return q, kk, None
                e_m = factor(1 + entry, l0)
                return q * e_m, kk * e_m, e_m

            for p, (ea, eb) in enumerate(LEVEL_PAIRS):
                dm = mask_ref[p] * da2
                dm_t = dm.T.astype(BF16)
                qa, ka, e_a = scaled(ea)
                if eb is None:
                    rhs_k = jnp.concatenate([jnp.concatenate([ka.astype(BF16), zeros16], axis=1),
                                             jnp.concatenate([zeros16, zeros16], axis=1)], axis=0)
                else:
                    qb_, kb_, e_bb = scaled(eb)
                    rhs_k = jnp.concatenate([jnp.concatenate([ka.astype(BF16), zeros16], axis=1),
                                             jnp.concatenate([zeros16, kb_.astype(BF16)], axis=1)], axis=0)
                dq2 = _dot(dm.astype(BF16), rhs_k)
                parts = [(ea, qa, ka, e_a, dq2[:, :HEAD_DIM], _dot(dm_t[0:CHUNK], qa.astype(BF16)))]
                if eb is not None:
                    parts.append((eb, qb_, kb_, e_bb, dq2[:, HEAD_DIM:],
                                  _dot(dm_t[CHUNK:2 * CHUNK], qb_.astype(BF16))))
                for entry, q_m, k_m, e_m, dq_m, dk_m in parts:
                    if entry == 0:
                        dq = dq + dq_m
                        dkk = dkk + dk_m
                    else:
                        dq = dq + e_m * dq_m
                        dkk = dkk + e_m * dk_m
                        dx_s[(1 + entry) * CHUNK:(2 + entry) * CHUNK, pl.ds(l0, HEAD_DIM)] = (
                            q_m * dq_m + k_m * dk_m).astype(BF16)
            dq_s[pl.ds(r0, CHUNK), :] = dq
            dkk_s[pl.ds(r0, CHUNK), :] = dkk
            return carry

        lax.fori_loop(0, cpb, local, 0, unroll=BACKWARD_UNROLL)

        dg_all = _dot(wexpt_ref[...], dx_s[...])
        for b in range(cpb):
            dg_s[b * CHUNK:(b + 1) * CHUNK, :] = dg_all[:, b * HEAD_DIM:(b + 1) * HEAD_DIM]
        t = jnp.where(valid, dg_s[...] / f - dkk_s[...], 0.0)
        dlb_ref[...] += jnp.sum(sn * t, axis=0, keepdims=True)
        dp_ref[0] = jnp.where(valid, dq_s[...], 0.0).astype(BF16)
        dp_ref[1] = ((1.0 - lb) * sg * sn * t).astype(BF16)

    head_block = lambda seg: pl.BlockSpec((1, rb_rows, width), lambda h, s: (seg, n_rb - 1 - s, h))
    row_block = pl.BlockSpec((rb_rows, width), lambda h, s: (n_rb - 1 - s, h))
    return pl.pallas_call(
        body, name="hgrn_backward",
        grid=(n_hb, n_rb),
        in_specs=[
            head_block(0), head_block(1), head_block(2),
            row_block,
            pl.BlockSpec((hps, cpb, HEAD_DIM, HEAD_DIM), lambda h, s: (h, n_rb - 1 - s, 0, 0)),
            pl.BlockSpec((hps, 1, N_EXP * CHUNK, lanes), lambda h, s: (h, n_rb - 1 - s, 0, 0)),
            row_block,
            pl.BlockSpec((2, width), lambda h, s: (0, h)),
            pl.BlockSpec((CHUNK, N_EXP * CHUNK), lambda h, s: (0, 0)),
            pl.BlockSpec((len(LEVEL_PAIRS), CHUNK, 2 * CHUNK), lambda h, s: (0, 0, 0)),
            ANY, ANY, ANY,
        ],
        out_specs=[
            pl.BlockSpec((3, rb_rows, width), lambda h, s: (0, n_rb - 1 - s, h)),
            pl.BlockSpec((1, width), lambda h, s: (0, h)),
            ANY, ANY,
        ],
        out_shape=[
            jax.ShapeDtypeStruct((3, rows, D_MODEL), BF16),
            jax.ShapeDtypeStruct((1, D_MODEL), F32),
            exchange.landing_w(), jax.ShapeDtypeStruct(rx_blob.shape, rx_blob.dtype),
        ],
        input_output_aliases={12: 3},
        scratch_shapes=[
            pltpu.VMEM((hps, HEAD_DIM, HEAD_DIM), F32),
            pltpu.VMEM((hps, cpb, HEAD_DIM, HEAD_DIM), F32),
            pltpu.VMEM((hps, cpb, HEAD_DIM, HEAD_DIM), F32),
            pltpu.VMEM((hps, rb_rows, HEAD_DIM), F32),
            pltpu.VMEM((hps, rb_rows, HEAD_DIM), F32),
            pltpu.VMEM((hps, rb_rows, HEAD_DIM), BF16),
            pltpu.VMEM((hps, rb_rows, HEAD_DIM), BF16),
            pltpu.VMEM((hps, rb_rows, HEAD_DIM), F32),
            pltpu.VMEM((hps, rb_rows, HEAD_DIM), F32),
            pltpu.VMEM((hps, rb_rows, HEAD_DIM), F32),
            pltpu.VMEM((hps, N_EXP * CHUNK, lanes), BF16),
            pltpu.VMEM((hps, rb_rows, 2 * CHUNK), F32),
        ] + exchange.semaphores(),
        compiler_params=_params(("arbitrary", "arbitrary")),
    )(p3, p3, p3, d_o, states, e16, a2, lb_logits, wexp_t, masks2, dw16, blob16, rx_blob)


def _sigmoid(x):
    return 1.0 / (1.0 + jnp.exp(-x))


def _silu_and_grad(x):
    s = _sigmoid(x)
    return x * s, s * (1.0 + x * (1.0 - s))


def _window_sum(ext, width, forward_looking):
    n = ext.shape[0]
    s = ext
    step = 1
    while step < width:
        s = s + pltpu.roll(s, (n - step) if forward_looking else step, 0)
        step *= 2
    return s


def _mixers(o, p3, tokens, head, tgt, wdh, wdp, wout, poolw, hg_w, pool_scale, final_w, rows):
    tm = _tile(rows, 208)
    nt = rows // tm
    halo_blocks = tm // HALO
    n_grp = len(POOL_WINDOWS)
    q_rows = D_MODEL // N_CHIPS
    blob_rows = 3 * q_rows + n_grp * POOL_GDIM * POOL_GDIM // (N_CHIPS * D_MODEL)

    def body(o_ref, ghg_ref, u_ref, gpl_ref, mhg_ref, mpl_ref, uh_ref, z_ref, t_ref,
             wdh_ref, wdp_ref, wout_ref, pw_ref, hgw_ref, ps_ref, fw_ref, head_ref,
             do_ref, dz2_ref, dp_ref, blob_ref, dpw_ref, small_ref, carry_ref):
        step = pl.program_id(0)
        tile = nt - 1 - step

        def add_to_blob(piece, dw):
            for k in range(N_CHIPS):
                blob_ref[k, piece * q_rows:(piece + 1) * q_rows, :] += dw[k * q_rows:(k + 1) * q_rows]

        @pl.when(step == 0)
        def _():
            blob_ref[...] = jnp.zeros_like(blob_ref)
            dpw_ref[...] = jnp.zeros_like(dpw_ref)
            small_ref[...] = jnp.zeros_like(small_ref)
            carry_ref[...] = jnp.zeros_like(carry_ref)

        row = tile * tm + lax.broadcasted_iota(jnp.int32, (tm, 1), 0)
        real = row >= PAD_ROWS
        pos1 = jnp.maximum(row - PAD_ROWS + 1, 1).astype(F32)

        u = jnp.where(real, u_ref[0], 0.0)
        halo_row = tile * tm - HALO + lax.broadcasted_iota(jnp.int32, (HALO, 1), 0)
        uh = jnp.where(halo_row >= PAD_ROWS, uh_ref[0], 0.0)
        ext = jnp.concatenate([uh, u], axis=0)
        pooled, inv_cnt, mixed = [], [], []
        for g, w in enumerate(POOL_WINDOWS):
            cols = slice(g * POOL_GDIM, (g + 1) * POOL_GDIM)
            inv = 1.0 / jnp.minimum(pos1, float(w))
            ws = _window_sum(ext[:, cols], w, False)[HALO:]
            pg = (ws * inv - u[:, cols]).astype(BF16)
            pooled.append(pg)
            inv_cnt.append(inv)
            mixed.append(_dot(pg, pw_ref[g]))
        mixed = jnp.concatenate(mixed, axis=1)
        gpl = gpl_ref[0]
        sp, dsp = _silu_and_grad(gpl)
        ps = ps_ref[...]
        a_pool = (mixed * ps * sp).astype(BF16)
        y_pool = _dot(a_pool, wdp_ref[...])

        o = o_ref[...]
        o_hat, rstd_h = [], []
        for h in range(N_HEADS):
            oh = o[:, h * HEAD_DIM:(h + 1) * HEAD_DIM]
            r = lax.rsqrt(jnp.mean(oh * oh, axis=-1, keepdims=True) + EPS)
            rstd_h.append(r)
            o_hat.append(oh * r)
        o_hat = jnp.concatenate(o_hat, axis=1)
        hgw = hgw_ref[...]
        o_n = o_hat * hgw
        ghg = ghg_ref[0]
        sh, dsh = _silu_and_grad(ghg)
        a_hg = (o_n * sh).astype(BF16)
        y_hg = _dot(a_hg, wdh_ref[...])

        s_mh = _sigmoid(mhg_ref[0])
        s_mp = _sigmoid(mpl_ref[0])
        merged = (s_mh * y_hg + s_mp * y_pool).astype(BF16)
        z2 = _padded_tile(z_ref[...], head_ref[...], tile) + _dot(merged, wout_ref[...])
        rstd2 = lax.rsqrt(jnp.mean(z2 * z2, axis=-1, keepdims=True) + EPS)
        zh = z2 * rstd2
        fw = fw_ref[...]
        target = _padded_tile(t_ref[...], jnp.zeros((FIRST_TOKEN_ROW, D_MODEL), F32), tile)
        err = jnp.where(row >= FIRST_TOKEN_ROW, zh * fw - target, 0.0)
        small_ref[ROW_LOSS:ROW_LOSS + 1, :] += jnp.sum(err * err, axis=0, keepdims=True) * (0.5 / D_MODEL)
        dy = err * (1.0 / D_MODEL)

        small_ref[ROW_FINAL_W:ROW_FINAL_W + 1, :] += jnp.sum(dy * zh, axis=0, keepdims=True)
        uu = dy * fw
        dz2 = rstd2 * (uu - zh * jnp.mean(uu * zh, axis=-1, keepdims=True))
        dz2_ref[...] = dz2
        dz2_16 = dz2.astype(BF16)
        dmerged = _dot_nt(dz2_16, wout_ref[...])
        add_to_blob(2, _dot_tn(merged, dz2_16))
        dy_hg = (s_mh * dmerged).astype(BF16)
        dy_pool = (s_mp * dmerged).astype(BF16)
        dp_ref[3] = (dmerged * y_hg * s_mh * (1.0 - s_mh)).astype(BF16)
        dp_ref[4] = (dmerged * y_pool * s_mp * (1.0 - s_mp)).astype(BF16)

        da_hg = _dot_nt(dy_hg, wdh_ref[...])
        add_to_blob(0, _dot_tn(a_hg, dy_hg))
        dp_ref[0] = (da_hg * o_n * dsh).astype(BF16)
        do_n = da_hg * sh
        small_ref[ROW_HG_W:ROW_HG_W + 1, :] += jnp.sum(do_n * o_hat, axis=0, keepdims=True)
        d_hat = do_n * hgw
        for h in range(N_HEADS):
            cols = slice(h * HEAD_DIM, (h + 1) * HEAD_DIM)
            dh_, oh_ = d_hat[:, cols], o_hat[:, cols]
            do_ref[:, cols] = rstd_h[h] * (dh_ - oh_ * jnp.mean(dh_ * oh_, axis=-1, keepdims=True))

        da_pool = _dot_nt(dy_pool, wdp_ref[...])
        add_to_blob(1, _dot_tn(a_pool, dy_pool))
        small_ref[ROW_POOL_SCALE:ROW_POOL_SCALE + 1, :] += jnp.sum(da_pool * mixed * sp, axis=0, keepdims=True)
        dp_ref[2] = (da_pool * mixed * ps * dsp).astype(BF16)
        dmixed = (da_pool * ps * sp).astype(BF16)
        carry = carry_ref[...]
        du, new_carry = [], []
        for g, w in enumerate(POOL_WINDOWS):
            cols = slice(g * POOL_GDIM, (g + 1) * POOL_GDIM)
            dmg = dmixed[:, cols]
            dpooled = _dot_nt(dmg, pw_ref[g])
            dpw_ref[g] += _dot_tn(pooled[g], dmg)
            dps = dpooled * inv_cnt[g]
            ext_b = jnp.concatenate([dps, carry[:, cols]], axis=0)
            du.append(_window_sum(ext_b, w, True)[:tm] - dpooled)
            new_carry.append(dps[:HALO])
        dp_ref[1] = jnp.where(real, jnp.concatenate(du, axis=1), 0.0).astype(BF16)
        carry_ref[...] = jnp.concatenate(new_carry, axis=1)

    row_block = pl.BlockSpec((tm, D_MODEL), lambda s: (nt - 1 - s, 0))
    seg_block = lambda seg: pl.BlockSpec((1, tm, D_MODEL), lambda s: (seg, nt - 1 - s, 0))
    whole = pl.BlockSpec(memory_space=pltpu.VMEM)
    return pl.pallas_call(
        body, name="mixers",
        grid=(nt,),
        in_specs=[
            row_block, seg_block(3), seg_block(4), seg_block(5), seg_block(6), seg_block(7),
            pl.BlockSpec((1, HALO, D_MODEL),
                         lambda s: (4, jnp.maximum((nt - 1 - s) * halo_blocks - 1, 0), 0)),
            _token_window(tm, lambda s: nt - 1 - s), _token_window(tm, lambda s: nt - 1 - s),
            whole, whole, whole, whole, whole, whole, whole, whole,
        ],
        out_specs=[
            row_block, row_block,
            pl.BlockSpec((5, tm, D_MODEL), lambda s: (0, nt - 1 - s, 0)),
            whole, whole, whole,
        ],
        out_shape=[
            jax.ShapeDtypeStruct((rows, D_MODEL), F32),
            jax.ShapeDtypeStruct((rows, D_MODEL), F32),
            jax.ShapeDtypeStruct((5, rows, D_MODEL), BF16),
            jax.ShapeDtypeStruct((N_CHIPS, blob_rows, D_MODEL), F32),
            jax.ShapeDtypeStruct((n_grp, POOL_GDIM, POOL_GDIM), F32),
            jax.ShapeDtypeStruct((SMALL_ROWS, D_MODEL), F32),
        ],
        scratch_shapes=[pltpu.VMEM((HALO, D_MODEL), F32)],
        compiler_params=_params(("arbitrary",)),
    )(o, p3, p3, p3, p3, p3, p3, tokens, tgt, wdh, wdp, wout, poolw, hg_w, pool_scale, final_w, head)


def _seg_specs(tm, row_of, seg_of):
    def spec_a(*g):
        k = seg_of(*g)
        return (jnp.minimum(k, 2), jnp.where(k < 3, row_of(*g), 0), 0)

    def spec_b(*g):
        k = seg_of(*g)
        return (jnp.maximum(k - 3, 0), jnp.where(k >= 3, row_of(*g), 0), 0)

    return pl.BlockSpec((1, tm, D_MODEL), spec_a), pl.BlockSpec((1, tm, D_MODEL), spec_b)


def _in_proj_weight_grad(h, dp, rows, name, blob16=None):
    n_seg = dp.shape[0]
    tm = _tile(rows, 1040)
    nt = rows // tm
    half = D_MODEL // 2
    exchange = _GradExchange((), True, BLOB_RELATIONS_DIRECT) if blob16 is not None else None

    def body(*refs):
        if exchange is None:
            (h_ref, dp_ref, part_ref, part16_ref, db_ref, acc_ref, bacc_ref, stage_ref, land_ref,
             send_sems, recv_sems) = refs
        else:
            (h_ref, dp_ref, blob_ref, part_ref, part16_ref, db_ref, rxb_ref, acc_ref, bacc_ref, stage_ref, land_ref,
             send_sems, recv_sems, blob_send, blob_recv) = refs
        k, i = pl.program_id(0), pl.program_id(1)
        x, y, c = lax.axis_index("x"), lax.axis_index("y"), lax.axis_index("c")

        if exchange is not None:
            @pl.when((k == 0) & (i == 0))
            def _():
                exchange.start(None, None, blob_ref, rxb_ref, blob_send, blob_recv)

        def to_sibling(seg):
            return pltpu.make_async_remote_copy(
                src_ref=stage_ref.at[seg], dst_ref=land_ref.at[seg], send_sem=send_sems.at[seg],
                recv_sem=recv_sems.at[seg], device_id=(x, y, 1 - c), device_id_type=MESH)

        @pl.when(i == 0)
        def _():
            acc_ref[...] = jnp.zeros_like(acc_ref)
            bacc_ref[...] = jnp.zeros_like(bacc_ref)

        dpt = dp_ref[0]
        acc_ref[...] += _dot_tn(h_ref[...], dpt)
        bacc_ref[...] += jnp.sum(dpt.astype(F32), axis=0, keepdims=True)

        @pl.when(i == nt - 1)
        def _():
            db_ref[0] = bacc_ref[...]
            part_ref[k] = acc_ref[pl.ds(pl.multiple_of(c * half, half), half), :]
            stage_ref[k] = acc_ref[pl.ds(pl.multiple_of((1 - c) * half, half), half), :].astype(BF16)
            to_sibling(k).start()

        @pl.when((k == n_seg - 1) & (i == nt - 1))
        def _():
            for seg in range(n_seg):
                to_sibling(seg).wait_recv()
                total = part_ref[seg] + land_ref[seg].astype(F32)
                part_ref[seg] = total
                part16_ref[seg] = total.astype(BF16)
            for seg in range(n_seg):
                to_sibling(seg).wait_send()
            if exchange is not None:
                exchange.wait(None, None, blob_ref, rxb_ref, blob_send, blob_recv)

    whole = pl.BlockSpec(memory_space=pltpu.VMEM)
    with_blob = exchange is not None
    return pl.pallas_call(
        body, name=name,
        grid=(n_seg, nt),
        in_specs=[pl.BlockSpec((tm, D_MODEL), lambda k, i: (i, 0)),
                  pl.BlockSpec((1, tm, D_MODEL), lambda k, i: (k, i, 0))] + [ANY] * with_blob,
        out_specs=[whole, whole, pl.BlockSpec((1, 1, D_MODEL), lambda k, i: (k, 0, 0))] + [ANY] * with_blob,
        out_shape=[
            jax.ShapeDtypeStruct((n_seg, half, D_MODEL), F32),
            jax.ShapeDtypeStruct((n_seg, half, D_MODEL), BF16),
            jax.ShapeDtypeStruct((n_seg, 1, D_MODEL), F32),
        ] + ([exchange.landing_blob(blob16)] if with_blob else []),
        scratch_shapes=[
            pltpu.VMEM((D_MODEL, D_MODEL), F32), pltpu.VMEM((1, D_MODEL), F32),
            pltpu.VMEM((n_seg, half, D_MODEL), BF16),
            pltpu.VMEM((n_seg, half, D_MODEL), BF16),
            pltpu.SemaphoreType.DMA((n_seg,)), pltpu.SemaphoreType.DMA((n_seg,)),
        ] + (exchange.semaphores() if with_blob else []),
        compiler_params=_params(("arbitrary", "arbitrary")),
    )(h, dp, *([blob16] if with_blob else []))


def _input_grad(dpa, dpb, w4, tokens, head, dz2, norm_w, dw16, rows):
    tm = _tile(rows, 1040)
    nt = rows // tm
    assert nt >= 2, rows
    exchange = _GradExchange(SEGS_REC, with_blob=False)

    def body(dpa_ref, dpb_ref, w_ref, z_ref, head_ref, dz2_ref, nw_ref, dw_ref, gx_ref, dmeta_ref, dnw_ref, rxw_ref,
             acc_ref, dz_buf, out_sem, send_sems, recv_sems):
        i, k = pl.program_id(0), pl.program_id(1)

        def first_tile_out():
            return pltpu.make_async_copy(dz_buf.at[pl.ds(FIRST_TOKEN_ROW, tm - FIRST_TOKEN_ROW), :],
                                         gx_ref.at[pl.ds(0, tm - FIRST_TOKEN_ROW), :], out_sem)

        def tile_out(tile):
            start = pl.multiple_of(tile * tm - FIRST_TOKEN_ROW, HALO)
            return pltpu.make_async_copy(dz_buf, gx_ref.at[pl.ds(start, tm), :], out_sem)

        @pl.when((i == 0) & (k == 0))
        def _():
            exchange.start(dw_ref, rxw_ref, None, None, send_sems, recv_sems)
            dnw_ref[...] = jnp.zeros_like(dnw_ref)

        @pl.when((i == nt - 1) & (k == N_SEG - 1))
        def _():
            exchange.wait(dw_ref, rxw_ref, None, None, send_sems, recv_sems)

        @pl.when(k == 0)
        def _():
            acc_ref[...] = jnp.zeros_like(acc_ref)

        @pl.when(k < 3)
        def _():
            acc_ref[...] += _dot_nt(dpa_ref[0], w_ref[0])

        @pl.when(k >= 3)
        def _():
            acc_ref[...] += _dot_nt(dpb_ref[0], w_ref[0])

        @pl.when(k == N_SEG - 1)
        def _():
            zt = _padded_tile(z_ref[...], head_ref[...], i)
            rstd = lax.rsqrt(jnp.mean(zt * zt, axis=-1, keepdims=True) + EPS)
            zh = zt * rstd
            dh = acc_ref[...]
            dnw_ref[...] += jnp.sum(dh * zh, axis=0, keepdims=True)
            uu = dh * nw_ref[...]
            dz = dz2_ref[...] + rstd * (uu - zh * jnp.mean(uu * zh, axis=-1, keepdims=True))

            @pl.when(i == 1)
            def _():
                first_tile_out().wait()

            @pl.when(i >= 2)
            def _():
                tile_out(i - 1).wait()

            dz_buf[...] = dz

            @pl.when(i == 0)
            def _():
                dmeta_ref[...] = dz[PAD_ROWS:FIRST_TOKEN_ROW]
                first_tile_out().start()

            @pl.when(i > 0)
            def _():
                tile_out(i).start()

            @pl.when(i == nt - 1)
            def _():
                tile_out(i).wait()

    spec_a, spec_b = _seg_specs(tm, lambda i, k: i, lambda i, k: k)
    last_only = pl.BlockSpec((tm, D_MODEL), lambda i, k: (jnp.where(k == N_SEG - 1, i, 0), 0))
    return pl.pallas_call(
        body, name="input_grad",
        grid=(nt, N_SEG),
        in_specs=[
            spec_a, spec_b,
            pl.BlockSpec((1, D_MODEL, D_MODEL), lambda i, k: (k // 2, 0, k % 2)),
            _token_window(tm, lambda i, k: jnp.where(k == N_SEG - 1, i, 0)),
            pl.BlockSpec((FIRST_TOKEN_ROW, D_MODEL), lambda i, k: (0, 0)),
            last_only,
            pl.BlockSpec((1, D_MODEL), lambda i, k: (0, 0)),
            ANY,
        ],
        out_specs=[
            ANY,
            pl.BlockSpec((N_META, D_MODEL), lambda i, k: (0, 0)),
            pl.BlockSpec((1, D_MODEL), lambda i, k: (0, 0)),
            ANY,
        ],
        out_shape=[
            jax.ShapeDtypeStruct((rows - FIRST_TOKEN_ROW, D_MODEL), F32),
            jax.ShapeDtypeStruct((N_META, D_MODEL), F32),
            jax.ShapeDtypeStruct((1, D_MODEL), F32),
            exchange.landing_w(),
        ],
        scratch_shapes=[pltpu.VMEM((tm, D_MODEL), F32), pltpu.VMEM((tm, D_MODEL), F32),
                        pltpu.SemaphoreType.DMA] + exchange.semaphores(),
        compiler_params=_params(("arbitrary", "arbitrary")),
    )(dpa, dpb, w4, tokens, head, dz2, norm_w, dw16)


def _local_step(tokens, m4, tgt, w16, wblob16, seg_order, norm_w, b_in, lb_logits, hg_w, pool_scale, final_w):
    rows = FIRST_TOKEN_ROW + tokens.shape[0]
    q = D_MODEL // N_CHIPS
    n_grp = len(POOL_WINDOWS)
    pg = POOL_GDIM // N_CHIPS

    wexp2 = jnp.asarray(np.tile(_exponent_matrix(), (1, 2)), BF16)
    wexp_t = jnp.asarray(_exponent_matrix().T, BF16)
    masks2 = jnp.asarray(_paired_masks(), F32)

    h, p3, w4, head, _ = _in_proj(tokens, m4, norm_w, w16, b_in, seg_order, rows)
    o, states, e16, a2, blob4 = _hgrn_forward(p3, lb_logits, wexp2, masks2, wblob16, rows)
    wdh = blob4[:, 0:q].reshape(D_MODEL, D_MODEL)
    wdp = blob4[:, q:2 * q].reshape(D_MODEL, D_MODEL)
    wout = blob4[:, 2 * q:3 * q].reshape(D_MODEL, D_MODEL)
    poolw = blob4[:, 3 * q:].reshape(N_CHIPS, n_grp, pg, POOL_GDIM).transpose(1, 0, 2, 3)
    poolw = poolw.reshape(n_grp, POOL_GDIM, POOL_GDIM)
    d_o, dz2, dpb, dblob4, dpw, small = _mixers(
        o, p3, tokens, head, tgt, wdh, wdp, wout, poolw, hg_w, pool_scale, final_w, rows)
    dpw4 = dpw.reshape(n_grp, N_CHIPS, pg, POOL_GDIM).transpose(1, 0, 2, 3)
    dpw4 = dpw4.reshape(N_CHIPS, n_grp * pg * POOL_GDIM // D_MODEL, D_MODEL)
    dblob4 = dblob4.at[:, 3 * q:, :].set(dpw4)

    blob16 = dblob4.astype(BF16)
    dw_mix, dw_mix16, db_mix, rx_blob = _in_proj_weight_grad(h, dpb, rows, "in_proj_weight_grad_mix", blob16)
    dpa, dlb, rxw_mix, rx_blob = _hgrn_backward(
        p3, d_o, states, e16, a2, lb_logits, wexp_t, masks2, dw_mix16, blob16, rx_blob, rows)
    dw_rec, dw_rec16, db_rec = _in_proj_weight_grad(h, dpa, rows, "in_proj_weight_grad_rec")
    d_tokens, d_meta, dnw, rxw_rec = _input_grad(dpa, dpb, w4, tokens, head, dz2, norm_w, dw_rec16, rows)

    small_parts = (small, d_meta, dnw, db_rec, db_mix, dlb)
    return d_tokens, (dw_rec, dw_mix, rxw_rec, rxw_mix), (dblob4, rx_blob), small_parts


ANY = pl.BlockSpec(memory_space=pl.ANY)
MESH = pl.DeviceIdType.MESH


def _place():
    x, y, c = lax.axis_index("x"), lax.axis_index("y"), lax.axis_index("c")
    chips = [(1 - x, y), (x, 1 - y), (1 - x, 1 - y)]
    return x, y, c, chips


class _ShardGather:
    def __init__(self, rows, own_shard_apart=False):
        self.half = rows // 2
        self.own_shard_apart = own_shard_apart

    def semaphores(self):
        return [pltpu.SemaphoreType.DMA((6,)), pltpu.SemaphoreType.DMA((6,))]

    def _copy(self, k, slot, to, send_sems, recv_sems, src=None):
        return pltpu.make_async_remote_copy(src_ref=slot if src is None else src, dst_ref=slot,
                                            send_sem=send_sems.at[k], recv_sem=recv_sems.at[k],
                                            device_id=to, device_id_type=MESH)

    def _half(self, ref4, chip, which):
        return ref4.at[chip, pl.ds(which * self.half, self.half), :]

    def _ici_copy(self, j, ref4, own_ref, send_sems, recv_sems):
        x, y, c, chips = _place()
        cx, cy = chips[j]
        src = own_ref.at[pl.ds(c * self.half, self.half), :] if self.own_shard_apart else None
        return self._copy(j, self._half(ref4, 2 * x + y, c), (cx, cy, c), send_sems, recv_sems, src)

    def start(self, ref4, send_sems, recv_sems, which=(0, 1, 2), own_ref=None):
        for j in which:
            self._ici_copy(j, ref4, own_ref, send_sems, recv_sems).start()

    def start_diagonal_after_neighbours(self, ref4, send_sems, recv_sems, own_ref=None):
        for j in (0, 1):
            self._ici_copy(j, ref4, own_ref, send_sems, recv_sems).wait_send()
        self.start(ref4, send_sems, recv_sems, which=(2,), own_ref=own_ref)

    def pass_on(self, j, ref4, send_sems, recv_sems):
        x, y, c, chips = _place()
        cx, cy = chips[j]
        landed = self._half(ref4, 2 * cx + cy, c)
        self._copy(j, landed, (cx, cy, c), send_sems, recv_sems).wait_recv()
        self._copy(3 + j, landed, (x, y, 1 - c), send_sems, recv_sems).start()

    def await_sibling(self, j, ref4, send_sems, recv_sems):
        x, y, c, chips = _place()
        cx, cy = chips[j]
        self._copy(3 + j, self._half(ref4, 2 * cx + cy, 1 - c), (x, y, 1 - c), send_sems, recv_sems).wait_recv()

    def finish(self, ref4, send_sems, recv_sems, which=(0, 1, 2), own_ref=None):
        x, y, c, chips = _place()
        for j, (cx, cy) in enumerate(chips):
            if j in which:
                self._ici_copy(j, ref4, own_ref, send_sems, recv_sems).wait_send()
            self._copy(3 + j, self._half(ref4, 2 * cx + cy, c), (x, y, 1 - c), send_sems, recv_sems).wait_send()


class _GradExchange:
    def __init__(self, segs, with_blob, blob_chip_relations=tuple(range(N_CHIPS))):
        self.segs = tuple(segs)
        self.with_blob = with_blob
        self.blob_chip_relations = tuple(blob_chip_relations)

    def landing_w(self):
        return jax.ShapeDtypeStruct((N_CHIPS, 2, D_MODEL // 2, D_MODEL), BF16)

    def landing_blob(self, blob16):
        return jax.ShapeDtypeStruct((N_DEV, blob16.shape[1] // 2, D_MODEL), BF16)

    def semaphores(self):
        n_send = len(self.segs) + (2 * N_CHIPS if self.with_blob else 0)
        n_recv = 2 * N_CHIPS + (N_DEV if self.with_blob else 0)
        return [pltpu.SemaphoreType.DMA((n_send,)), pltpu.SemaphoreType.DMA((n_recv,))]

    def _copies(self, dw_ref, rxw_ref, blob_ref, rxb_ref, send_sems, recv_sems):
        x, y, c = lax.axis_index("x"), lax.axis_index("y"), lax.axis_index("c")
        chip = 2 * x + y

        def relation(kx, ky, h):
            return (x ^ kx) * 4 + (y ^ ky) * 2 + (c ^ h)

        def copy(src, dst, send_k, recv_k, to):
            return functools.partial(pltpu.make_async_remote_copy, src_ref=src, dst_ref=dst,
                                     send_sem=send_sems.at[send_k], recv_sem=recv_sems.at[recv_k],
                                     device_id=to, device_id_type=MESH)

        sends, recvs = [], []
        for i, s in enumerate(self.segs):
            kx, ky = (s // 2) >> 1, (s // 2) & 1
            r = (x ^ kx) * 2 + (y ^ ky)
            sends.append((r != 0, copy(dw_ref.at[i], rxw_ref.at[r, s % 2], i, 2 * r + s % 2, (kx, ky, c))))
        for j in range(2):
            mine = [s // 2 for s in self.segs if s % 2 == j]
            if mine:
                cond = functools.reduce(lambda a, b: a | b, [chip == k for k in mine])
                for r in range(1, N_CHIPS):
                    slot = rxw_ref.at[r, j]
                    recvs.append((cond, copy(slot, slot, 0, 2 * r + j, (x, y, c))))
        if self.with_blob:
            hb = blob_ref.shape[1] // 2
            first_send, first_recv = len(self.segs), 2 * N_CHIPS
            for k in range(N_CHIPS):
                for h in range(2):
                    r = relation(k >> 1, k & 1, h)
                    travels = functools.reduce(lambda a, b: a | b, [r // 2 == q for q in self.blob_chip_relations])
                    sends.append(((r != 0) & travels,
                                  copy(blob_ref.at[k, pl.ds(h * hb, hb), :], rxb_ref.at[r],
                                       first_send + 2 * k + h, first_recv + r, (k >> 1, k & 1, h))))
            for r in range(1, N_DEV):
                if r // 2 in self.blob_chip_relations:
                    slot = rxb_ref.at[r]
                    recvs.append((None, copy(slot, slot, 0, first_recv + r, (x, y, c))))
        return sends, recvs

    def start(self, *refs):
        sends, _ = self._copies(*refs)
        for cond, make in sends:
            pl.when(cond)(lambda make=make: make().start())

    def wait(self, *refs):
        sends, recvs = self._copies(*refs)
        for cond, make in sends:
            pl.when(cond)(lambda make=make: make().wait_send())
        for cond, make in recvs:
            if cond is None:
                make().wait_recv()
            else:
                pl.when(cond)(lambda make=make: make().wait_recv())


def _sum_landed(own, rx_ref):
    total = own
    for r in range(1, rx_ref.shape[0]):
        total = total + rx_ref[r, 0].astype(F32)
    return total


def _finish_w(dw_rec, dw_mix, rx_rec, rx_mix, place_arr):
    half = D_MODEL // 2
    tm = _tile(half, 256)
    n_rec = len(SEGS_REC)

    def body(place_ref, own_rec_ref, own_mix_ref, rx_rec_ref, rx_mix_ref, out_ref):
        seg = 2 * place_ref[0] + pl.program_id(0)

        @pl.when(seg < n_rec)
        def _():
            out_ref[0] = _sum_landed(own_rec_ref[0], rx_rec_ref)

        @pl.when(seg >= n_rec)
        def _():
            out_ref[0] = _sum_landed(own_mix_ref[0], rx_mix_ref)

    def own_spec(first, count):
        def index(j, i, place_ref):
            seg = 2 * place_ref[0] + j
            return (jnp.clip(seg - first, 0, count - 1), i, 0)
        return pl.BlockSpec((1, tm, D_MODEL), index)

    rx_spec = pl.BlockSpec((N_CHIPS, 1, tm, D_MODEL), lambda j, i, place_ref: (0, j, i, 0))
    return pl.pallas_call(
        body, name="finish_w",
        grid_spec=pltpu.PrefetchScalarGridSpec(
            num_scalar_prefetch=1, grid=(2, half // tm),
            in_specs=[own_spec(0, n_rec), own_spec(n_rec, len(SEGS_MIX)), rx_spec, rx_spec],
            out_specs=pl.BlockSpec((1, tm, D_MODEL), lambda j, i, place_ref: (place_ref[1], i, j))),
        out_shape=jax.ShapeDtypeStruct((2, half, 2 * D_MODEL), F32),
        compiler_params=_params(("arbitrary", "arbitrary")),
    )(place_arr, dw_rec, dw_mix, rx_rec, rx_mix)


def _finish_blob(dblob4, rx_blob, place_arr):
    n, rows, cols = rx_blob.shape
    tm = _tile(rows, 256)

    def body(place_ref, own_ref, rx_ref, out_ref):
        out_ref[0] = _sum_landed(own_ref[0, 0], rx_ref)

    return pl.pallas_call(
        body, name="finish_blob",
        grid_spec=pltpu.PrefetchScalarGridSpec(
            num_scalar_prefetch=1, grid=(rows // tm,),
            in_specs=[pl.BlockSpec((1, 1, tm, cols), lambda i, place_ref: (place_ref[0], place_ref[1], i, 0)),
                      pl.BlockSpec((n, 1, tm, cols), lambda i, place_ref: (0, 0, i, 0))],
            out_specs=pl.BlockSpec((1, tm, cols), lambda i, place_ref: (place_ref[1], i, 0))),
        out_shape=jax.ShapeDtypeStruct((2, rows, cols), F32),
        compiler_params=_params(("arbitrary",)),
    )(place_arr, dblob4.reshape(N_CHIPS, 2, rows, cols), rx_blob.reshape(n, 1, rows, cols))


def _share_finished(fw2, fb2, small_parts):
    def body(w_in_ref, b_in_ref, mix_ref, dmeta_ref, dnw_ref, dbrec_ref, dbmix_ref, dlb_ref, w_ref, b_ref, s_ref,
             rows_ref, local_sem, send_sems, recv_sems):
        x, y, c, _ = _place()
        sibling = (x, y, 1 - c)

        def copy(k, src, dst, to):
            return pltpu.make_async_remote_copy(src_ref=src, dst_ref=dst, send_sem=send_sems.at[k],
                                                recv_sem=recv_sems.at[k], device_id=to, device_id_type=MESH)

        sends = [copy(0, w_ref.at[c], w_ref.at[c], sibling), copy(1, b_ref.at[c], b_ref.at[c], sibling)]
        for cp in sends:
            cp.start()
        rows_ref[...] = mix_ref[...]
        rows_ref[ROW_META:ROW_META + N_META, :] = dmeta_ref[...]
        rows_ref[ROW_NORM_W:ROW_NORM_W + 1, :] = dnw_ref[...]
        for k, seg in enumerate(SEGS_REC):
            rows_ref[ROW_B_IN + seg:ROW_B_IN + seg + 1, :] = dbrec_ref[k]
        for k, seg in enumerate(SEGS_MIX):
            rows_ref[ROW_B_IN + seg:ROW_B_IN + seg + 1, :] = dbmix_ref[k]
        rows_ref[ROW_LB:ROW_LB + 1, :] = dlb_ref[...]
        for r in range(1, N_DEV):
            peer = (x ^ ((r >> 2) & 1), y ^ ((r >> 1) & 1), c ^ (r & 1))
            sends.append(copy(1 + r, rows_ref, s_ref.at[r], peer))
            sends[-1].start()
        own = pltpu.make_async_copy(rows_ref, s_ref.at[0], local_sem)
        own.start()
        own.wait()
        landed = [w_ref.at[1 - c], b_ref.at[1 - c]] + [s_ref.at[r] for r in range(1, N_DEV)]
        for k, slot in enumerate(landed):
            copy(k, slot, slot, (x, y, c)).wait_recv()
        for cp in sends:
            cp.wait_send()

    same = lambda a: jax.ShapeDtypeStruct(a.shape, a.dtype)
    n_sem = 2 + N_DEV - 1
    whole = pl.BlockSpec(memory_space=pltpu.VMEM)
    return pl.pallas_call(
        body, name="share_finished",
        in_specs=[ANY, ANY] + [whole] * len(small_parts), out_specs=[ANY, ANY, ANY],
        out_shape=[same(fw2), same(fb2), jax.ShapeDtypeStruct((N_DEV, SMALL_ROWS, D_MODEL), F32)],
        input_output_aliases={0: 0, 1: 1},
        scratch_shapes=[pltpu.VMEM((SMALL_ROWS, D_MODEL), F32), pltpu.SemaphoreType.DMA,
                        pltpu.SemaphoreType.DMA((n_sem,)), pltpu.SemaphoreType.DMA((n_sem,))],
    )(fw2, fb2, *small_parts)


def _sum_small(slots, lb_logits, me_arr):
    def body(me_ref, slots_ref, lbl_ref, out_ref):
        me = me_ref[0]
        total = slots_ref[me]
        for d in range(1, N_DEV):
            total = total + slots_ref[d ^ me]
        out_ref[...] = total
        out_ref[ROW_LOSS:ROW_LOSS + 1, :] = jnp.broadcast_to(
            jnp.sum(total[ROW_LOSS:ROW_LOSS + 1, :], axis=-1, keepdims=True), (1, D_MODEL))
        lb = _lower_bound(lbl_ref[...])
        g0 = total[ROW_LB:ROW_LB + 1, :] * lb * (1.0 - lb)
        out_ref[ROW_LB:ROW_LB + 1, :] = g0
        out_ref[ROW_LB + 1:ROW_LB + 2, :] = -g0

    return pl.pallas_call(
        body, name="sum_small",
        grid_spec=pltpu.PrefetchScalarGridSpec(
            num_scalar_prefetch=1, grid=(1,),
            in_specs=[pl.BlockSpec((N_DEV, SMALL_ROWS, D_MODEL), lambda i, me_ref: (0, 0, 0)),
                      pl.BlockSpec((2, D_MODEL), lambda i, me_ref: (0, 0))],
            out_specs=pl.BlockSpec((SMALL_ROWS, D_MODEL), lambda i, me_ref: (0, 0))),
        out_shape=jax.ShapeDtypeStruct((SMALL_ROWS, D_MODEL), F32),
        compiler_params=_params(("arbitrary",)),
    )(me_arr, slots, lb_logits)


def _adamw_step(w, g, m, v):
    c1 = 1.0 / (1.0 - ADAM_B1 ** ADAM_STEP)
    c2 = 1.0 / (1.0 - ADAM_B2 ** ADAM_STEP)
    nm = ADAM_B1 * m + (1.0 - ADAM_B1) * g
    nv = ADAM_B2 * v + (1.0 - ADAM_B2) * (g * g)
    return -ADAM_LR * ((nm * c1) / (jnp.sqrt(nv * c2) + ADAM_EPS) + ADAM_WD * w), nm, nv


SMALL_PARAMS = (("norm_w", ROW_NORM_W, 1), ("b_in", ROW_B_IN, N_SEG), ("lb_logits", ROW_LB, 2),
                ("hg_norm_w", ROW_HG_W, 1), ("pool_scale", ROW_POOL_SCALE, 1), ("final_norm_w", ROW_FINAL_W, 1))


def _update_small(tot, triples):
    n = len(SMALL_PARAMS)

    def body(tot_ref, *refs):
        ins, outs = refs[:3 * n], refs[3 * n:]
        for p, (name, row, n_rows) in enumerate(SMALL_PARAMS):
            w_ref, m_ref, v_ref = ins[3 * p:3 * p + 3]
            g_ref, d_ref, nm_ref, nv_ref = outs[4 * p:4 * p + 4]
            if w_ref.shape[0] == n_rows:
                pieces = [(slice(None), slice(None), tot_ref[row:row + n_rows, :])]
            else:
                pieces = [(slice(None), slice(k * D_MODEL, (k + 1) * D_MODEL), tot_ref[row + k:row + k + 1, :])
                          for k in range(n_rows)]
            for rows_, cols_, g in pieces:
                d, nm, nv = _adamw_step(w_ref[rows_, cols_], g, m_ref[rows_, cols_], v_ref[rows_, cols_])
                g_ref[rows_, cols_] = g
                d_ref[rows_, cols_] = d
                nm_ref[rows_, cols_] = nm
                nv_ref[rows_, cols_] = nv

    whole = pl.BlockSpec(memory_space=pltpu.VMEM)
    flat = [a for t in triples for a in t]
    out_shape = [jax.ShapeDtypeStruct(t[0].shape, F32) for t in triples for _ in range(4)]
    outs = pl.pallas_call(
        body, name="update_small",
        in_specs=[whole] * (1 + len(flat)), out_specs=[whole] * len(out_shape), out_shape=out_shape,
        compiler_params=_params(),
    )(tot, *flat)
    return [tuple(outs[4 * p:4 * p + 4]) for p in range(n)]


def _update_blob(g_blob, triples):
    q_rows = triples[0][0].shape[0]
    pool_rows = triples[3][0].shape[0]
    steps = q_rows // pool_rows

    def body(*refs):
        ins, outs = refs[:16], refs[16:]
        for p in range(4):
            g_ref, (w_ref, m_ref, v_ref) = ins[p], ins[4 + 3 * p:7 + 3 * p]
            go_ref, d_ref, nm_ref, nv_ref = outs[4 * p:4 * p + 4]

            def update():
                g = g_ref[...]
                go_ref[...] = g
                d_ref[...], nm_ref[...], nv_ref[...] = _adamw_step(w_ref[...], g, m_ref[...], v_ref[...])

            if p < 3:
                update()
            else:
                pl.when(pl.program_id(0) == 0)(update)

    blk = pl.BlockSpec((pool_rows, D_MODEL), lambda i: (i, 0))
    once = pl.BlockSpec((pool_rows, D_MODEL), lambda i: (0, 0))
    g_specs = [pl.BlockSpec((pool_rows, D_MODEL), lambda i, p=p: (steps * p + i, 0)) for p in range(3)]
    g_specs.append(pl.BlockSpec((pool_rows, D_MODEL), lambda i: (3 * steps, 0)))
    piece_specs = [blk] * 9 + [once] * 3
    out_specs = [blk] * 12 + [once] * 4
    out_shape = [jax.ShapeDtypeStruct(t[0].shape, F32) for t in triples for _ in range(4)]
    outs = pl.pallas_call(
        body, name="update_blob",
        grid=(steps,), in_specs=g_specs + piece_specs, out_specs=out_specs, out_shape=out_shape,
        compiler_params=_params(("arbitrary",)),
    )(g_blob, g_blob, g_blob, g_blob, *[a for t in triples for a in t])
    return [tuple(outs[4 * p:4 * p + 4]) for p in range(4)]


def _adamw(w, g, m, v):
    rows, cols = w.shape
    tm = _tile(rows, 256, mult=8) if rows % 8 == 0 else rows

    def body(w_ref, g_ref, m_ref, v_ref, d_ref, nm_ref, nv_ref):
        d_ref[...], nm_ref[...], nv_ref[...] = _adamw_step(w_ref[...], g_ref[...], m_ref[...], v_ref[...])

    blk = pl.BlockSpec((tm, cols), lambda i: (i, 0))
    sds = jax.ShapeDtypeStruct((rows, cols), F32)
    return pl.pallas_call(
        body, name="adamw",
        grid=(rows // tm,), in_specs=[blk] * 4, out_specs=[blk] * 3, out_shape=[sds] * 3,
        compiler_params=_params(("arbitrary",)),
    )(w, g, m, v)


def kernel(x, meta_tokens, norm_w, w_in, b_in, lb_logits, hg_norm_w, pool_w, pool_scale, w_down_hg, w_down_pool, w_out, final_norm_w, loss_target, m_meta_tokens, m_norm_w, m_w_in, m_b_in, m_lb_logits, m_hg_norm_w, m_pool_w, m_pool_scale, m_w_down_hg, m_w_down_pool, m_w_out, m_final_norm_w, v_meta_tokens, v_norm_w, v_w_in, v_b_in, v_lb_logits, v_hg_norm_w, v_pool_w, v_pool_scale, v_w_down_hg, v_w_down_pool, v_w_out, v_final_norm_w):
    seq = x.shape[1]
    xi, yi, ci = lax.axis_index("x"), lax.axis_index("y"), lax.axis_index("c")
    chip = 2 * xi + yi
    place_arr = jnp.stack([chip, ci]).astype(jnp.int32)
    me_arr = jnp.reshape(4 * xi + 2 * yi + ci, (1,)).astype(jnp.int32)
    q = D_MODEL // N_CHIPS

    def blob_of(wdh, wdp, wo, pw):
        return jnp.concatenate([wdh[0], wdp[0], wo[0], pw[0].reshape(-1, D_MODEL)], axis=0)

    def in_every_slot(a):
        return jnp.broadcast_to(a[None], (N_CHIPS,) + a.shape)

    m4 = in_every_slot(meta_tokens)
    w16 = w_in[0].astype(BF16)
    wblob16 = blob_of(w_down_hg, w_down_pool, w_out, pool_w).astype(BF16)
    seg_order = jnp.stack([2 * (chip ^ rel) + t for rel in (0, 2, 1, 3) for t in (0, 1)]).astype(jnp.int32)

    fw2 = final_norm_w.reshape(1, D_MODEL)
    d_tokens, w_parts, blob_parts, small = _local_step(
        x[0], m4, loss_target[0], w16, wblob16, seg_order, norm_w, b_in, lb_logits, hg_norm_w, pool_scale, fw2)
    grad_x = d_tokens[None]

    fin_w = _finish_w(*w_parts, place_arr)
    fin_b = _finish_blob(*blob_parts, place_arr)
    gw2, gb2, slots = _share_finished(fin_w, fin_b, small)
    tot = _sum_small(slots, lb_logits, me_arr)
    g_w_in = gw2.reshape(D_MODEL, 2 * D_MODEL)
    g_blob = gb2.reshape(-1, D_MODEL)

    d_win, nm_win, nv_win = _adamw(w_in[0], g_w_in, m_w_in[0], v_w_in[0])
    pool_rows = lambda a: a[0].reshape(-1, D_MODEL)
    blob_results = _update_blob(g_blob, [
        (w_down_hg[0], m_w_down_hg[0], v_w_down_hg[0]), (w_down_pool[0], m_w_down_pool[0], v_w_down_pool[0]),
        (w_out[0], m_w_out[0], v_w_out[0]), (pool_rows(pool_w), pool_rows(m_pool_w), pool_rows(v_pool_w))])
    g_meta = lax.dynamic_slice_in_dim(tot[ROW_META:ROW_META + N_META], chip * q, q, axis=1)
    d_meta, nm_meta, nv_meta = _adamw(meta_tokens, g_meta, m_meta_tokens, v_meta_tokens)

    as_row = lambda a: a.reshape(1, D_MODEL)
    small_results = _update_small(tot, [
        (norm_w, m_norm_w, v_norm_w), (b_in, m_b_in, v_b_in), (lb_logits, m_lb_logits, v_lb_logits),
        (hg_norm_w, m_hg_norm_w, v_hg_norm_w), (pool_scale, m_pool_scale, v_pool_scale),
        (as_row(final_norm_w), as_row(m_final_norm_w), as_row(v_final_norm_w))])

    def leaves(kind, meta_part, win_part):
        nw, bi, lbl, hg, ps, fw = [r[kind] for r in small_results]
        wdh, wdp, wo, pw = [r[kind] for r in blob_results]
        return [meta_part, nw, win_part[None], bi, lbl, hg, pw.reshape(pool_w.shape), ps,
                wdh[None], wdp[None], wo[None], fw.reshape(D_MODEL)]

    loss = tot[ROW_LOSS, 0]
    return (loss, grad_x,
            *leaves(0, g_meta, g_w_in),
            *leaves(1, d_meta, d_win),
            *leaves(2, nm_meta, nm_win),
            *leaves(3, nv_meta, nv_win))
```
